```python
import jax, jax.numpy as jnp
from jax import lax
import numpy as np


D_MODEL = 1024
BATCH = 8
SEQ = 8192
DEPTH = 2

HEAD_DIM = 64
CONV_CH = D_MODEL // 4
CONV_WIDTH = 31
ATT_WIDTH = 3 * D_MODEL // 8
N_ATT_HEADS = ATT_WIDTH // HEAD_DIM
DILATION_PAIRS = ((128, 1), (512, 4), (2048, 16))
ATT_BLOCK = 128
ALIBI_MAX_EXP = 8.0
MASK_VALUE = -1e30
REC_WIDTH = 3 * D_MODEL // 8
REC_KEY_DIM = 64
REC_VAL_DIM = 64
N_REC_HEADS = REC_WIDTH // REC_VAL_DIM
REC_KEY_WIDTH = N_REC_HEADS * REC_KEY_DIM
REC_CHUNK = 64
F_TINY = 1e-30
MIX_WIDTH = CONV_CH + ATT_WIDTH + REC_WIDTH
IN_SPLITS = (CONV_CH, CONV_CH, ATT_WIDTH, ATT_WIDTH, ATT_WIDTH,
             REC_KEY_WIDTH, REC_KEY_WIDTH, REC_KEY_WIDTH, REC_WIDTH, REC_WIDTH)
IN_COLS = sum(IN_SPLITS)
D_FF = ((8 * D_MODEL // 3 + 127) // 128) * 128
FFN_CONV_WIDTH = 3
N_MOD = 6
EPS = 1e-6

kernel_name = "hybrid_conv_dilattn_hgrn2_encoder"


def _rmsnorm(x, w):
    xf = x.astype(jnp.float32)
    y = xf * lax.rsqrt(jnp.mean(xf * xf, axis=-1, keepdims=True) + EPS)
    return (y * w.astype(jnp.float32)).astype(x.dtype)


def _layernorm(x, w, b):
    xf = x.astype(jnp.float32)
    mu = jnp.mean(xf, axis=-1, keepdims=True)
    var = jnp.mean(jnp.square(xf - mu), axis=-1, keepdims=True)
    y = (xf - mu) * lax.rsqrt(var + EPS)
    return (y * w.astype(jnp.float32) + b.astype(jnp.float32)).astype(x.dtype)


def _depthwise_conv(x, w, b=None):
    k_w, ch = w.shape
    y = lax.conv_general_dilated(
        x, w[:, None, :].astype(x.dtype), window_strides=(1,),
        padding=[(k_w // 2, k_w // 2)],
        dimension_numbers=('NWC', 'WIO', 'NWC'), feature_group_count=ch)
    if b is not None:
        y = y + b.astype(x.dtype)
    return y


def _dilated_window_attention(q, k, v, slopes, window, dilation):
    B, S, H, Dh = q.shape
    half = window // (2 * dilation)
    L = S // dilation

    def to_sub(t):
        return t.reshape(B, L, dilation, H, Dh).transpose(0, 2, 1, 3, 4).reshape(B * dilation, L, H, Dh)

    qs, ks, vs = to_sub(q), to_sub(k), to_sub(v)
    blk = min(ATT_BLOCK, L)
    nb = -(-L // blk)
    Lp = nb * blk
    span = blk + 2 * half
    qs = jnp.pad(qs, ((0, 0), (0, Lp - L), (0, 0), (0, 0)))
    pad_k = ((0, 0), (half, half + Lp - L), (0, 0), (0, 0))
    ks = jnp.pad(ks, pad_k)
    vs = jnp.pad(vs, pad_k)
    idx = (np.arange(nb) * blk)[:, None] + np.arange(span)[None, :]
    kb = ks[:, idx]
    vb = vs[:, idx]
    qb = qs.reshape(B * dilation, nb, blk, H, Dh)
    rel = np.arange(span)[None, :] - half - np.arange(blk)[:, None]
    key_pos = idx - half
    valid = ((np.abs(rel) <= half)[None]
             & (key_pos[:, None, :] >= 0) & (key_pos[:, None, :] < L))
    dist = jnp.asarray(np.abs(rel) * dilation, jnp.float32)
    scores = jnp.einsum('bnqhd,bnkhd->bnhqk', qb, kb) * (Dh ** -0.5)
    scores = scores - slopes[:, None, None] * dist
    scores = jnp.where(jnp.asarray(valid)[:, None], scores, MASK_VALUE)
    lse = jax.nn.logsumexp(scores, axis=-1)
    p = jnp.exp(scores - lse[..., None])
    o = jnp.einsum('bnhqk,bnkhd->bnqhd', p, vb).reshape(B * dilation, Lp, H, Dh)[:, :L]
    lse = lse.transpose(0, 1, 3, 2).reshape(B * dilation, Lp, H)[:, :L]

    def from_sub(t):
        t5 = t.reshape((B, dilation, L) + t.shape[2:])
        t5 = jnp.moveaxis(t5, 1, 2)
        return t5.reshape((B, S) + t5.shape[3:])

    return from_sub(o), from_sub(lse)


def _mixture_of_dilated_attention(q, k, v):
    slopes = jnp.asarray(2.0 ** (-ALIBI_MAX_EXP * np.arange(1, N_ATT_HEADS + 1) / N_ATT_HEADS), jnp.float32)
    outs, lses = [], []
    for window, dilation in DILATION_PAIRS:
        o_g, l_g = _dilated_window_attention(q, k, v, slopes, window, dilation)
        outs.append(o_g)
        lses.append(l_g)
    wts = jax.nn.softmax(jnp.stack(lses, 0), axis=0)
    return jnp.einsum('gbsh,gbshd->bshd', wts, jnp.stack(outs, 0))


def _hgrn2_scan(q, k, v, logf):
    B, S, H, dk = q.shape
    dv = v.shape[-1]
    n = S // REC_CHUNK

    def chunks(t):
        return t.reshape(B, n, REC_CHUNK, H, t.shape[-1]).transpose(1, 0, 3, 2, 4)

    qc, kc, vc = chunks(q), chunks(k), chunks(v)
    bc = jnp.cumsum(chunks(logf), axis=3)
    lower = jnp.asarray(np.tril(np.ones((REC_CHUNK, REC_CHUNK), bool)))[:, :, None]

    def step(state, inp):
        qt, kt, vt, bt = inp
        diff = bt[:, :, :, None, :] - bt[:, :, None, :, :]
        decay = jnp.where(lower, jnp.exp(jnp.where(lower, diff, 0.0)), 0.0)
        scores = jnp.einsum('bhtk,bhsk,bhtsk->bhts', qt, kt, decay)
        o = (jnp.einsum('bhts,bhsv->bhtv', scores, vt)
             + jnp.einsum('bhtk,bhkv->bhtv', qt * jnp.exp(bt), state))
        b_last = bt[:, :, -1:, :]
        state = (jnp.exp(b_last[:, :, 0, :])[..., None] * state
                 + jnp.einsum('bhsk,bhsv->bhkv', kt * jnp.exp(b_last - bt), vt))
        return state, o

    state0 = jnp.zeros((B, H, dk, dv), q.dtype)
    _, o = lax.scan(step, state0, (qc, kc, vc, bc))
    return o.transpose(1, 0, 3, 2, 4).reshape(B, S, H, dv)


def _hgrn2_gate(z, lb):
    f = lb + (1.0 - lb) * jax.nn.sigmoid(z)
    logf = jnp.log(jnp.maximum(f, F_TINY))
    k = (1.0 - lb) * jax.nn.sigmoid(-z)
    return logf, k


def _token_mixers(h, w_in, conv_w, conv_b, ln_w, ln_b, lb_fwd, lb_bwd, rec_norm_w, w_out):
    B, S, _ = h.shape
    f32 = jnp.float32
    proj = h @ w_in.astype(h.dtype)
    cuts = [int(c) for c in np.cumsum(IN_SPLITS)[:-1]]
    (a_val, a_gate, q_att, k_att, v_att,
     q_rec, z_fwd, z_bwd, i_rec, g_rec) = jnp.split(proj, cuts, axis=-1)

    a = a_val * jax.nn.sigmoid(a_gate)
    a = _depthwise_conv(a, conv_w, conv_b)
    a = jax.nn.silu(_layernorm(a, ln_w, ln_b))

    def heads(t, d):
        return t.astype(f32).reshape(B, S, -1, d)
    att = _mixture_of_dilated_attention(heads(q_att, HEAD_DIM), heads(k_att, HEAD_DIM), heads(v_att, HEAD_DIM))
    att = att.reshape(B, S, ATT_WIDTH).astype(h.dtype)

    qr = heads(jax.nn.silu(q_rec.astype(f32)), REC_KEY_DIM)
    vr = heads(i_rec, REC_VAL_DIM)
    logf_f, k_f = _hgrn2_gate(z_fwd.astype(f32), lb_fwd)
    logf_b, k_b = _hgrn2_gate(z_bwd.astype(f32), lb_bwd)
    o_f = _hgrn2_scan(qr, heads(k_f, REC_KEY_DIM), vr, heads(logf_f, REC_KEY_DIM))
    flip = lambda t: jnp.flip(t, axis=1)
    o_b = flip(_hgrn2_scan(flip(qr), flip(heads(k_b, REC_KEY_DIM)), flip(vr), flip(heads(logf_b, REC_KEY_DIM))))
    o_r = o_f + o_b
    o_r = o_r * lax.rsqrt(jnp.mean(o_r * o_r, axis=-1, keepdims=True) + EPS)
    o_r = o_r.reshape(B, S, REC_WIDTH) * rec_norm_w.astype(f32)
    rec = (o_r * jax.nn.silu(g_rec.astype(f32))).astype(h.dtype)

    mixed = jnp.concatenate([a, att, rec], axis=-1)
    return mixed @ w_out.astype(h.dtype)


def _conv_ffn(h, w_up, conv_w, w_down):
    u = h @ w_up.astype(h.dtype)
    u = _depthwise_conv(u, conv_w)
    gate, val = jnp.split(u, 2, axis=-1)
    return (jax.nn.gelu(gate, approximate=False) * val) @ w_down.astype(h.dtype)


def _fwd_setup_inputs(seed: int = 0) -> dict:
    key = jax.random.key(seed)
    ks = jax.random.split(key, 18)
    f32 = jnp.float32
    D = D_MODEL

    def nrm(k, shape, scale):
        return jax.random.normal(k, shape, f32) * scale

    return {
        "x": nrm(ks[0], (BATCH, SEQ, D), 1.0),
        "c": nrm(ks[1], (BATCH, D), 1.0),
        "w_ada": nrm(ks[2], (DEPTH, D, N_MOD * D), 0.5 * D ** -0.5),
        "b_ada": nrm(ks[3], (DEPTH, N_MOD * D), 0.02),
        "norm1_w": 1.0 + nrm(ks[4], (DEPTH, D), 0.05),
        "w_in": nrm(ks[5], (DEPTH, D, IN_COLS), D ** -0.5),
        "conv_a_w": nrm(ks[6], (DEPTH, CONV_WIDTH, CONV_CH), CONV_WIDTH ** -0.5),
        "conv_a_b": nrm(ks[7], (DEPTH, CONV_CH), 0.02),
        "ln_a_w": 1.0 + nrm(ks[8], (DEPTH, CONV_CH), 0.05),
        "ln_a_b": nrm(ks[9], (DEPTH, CONV_CH), 0.02),
        "lb_gamma": nrm(ks[10], (DEPTH, 2, REC_KEY_WIDTH), 1.0),
        "rec_norm_w": 1.0 + nrm(ks[11], (DEPTH, REC_WIDTH), 0.05),
        "w_out": nrm(ks[12], (DEPTH, MIX_WIDTH, D), MIX_WIDTH ** -0.5),
        "norm2_w": 1.0 + nrm(ks[13], (DEPTH, D), 0.05),
        "w_up": nrm(ks[14], (DEPTH, D, 2 * D_FF), D ** -0.5),
        "conv_f_w": nrm(ks[15], (DEPTH, FFN_CONV_WIDTH, 2 * D_FF), FFN_CONV_WIDTH ** -0.5),
        "w_down": nrm(ks[16], (DEPTH, D_FF, D), D_FF ** -0.5),
        "final_norm_w": 1.0 + nrm(ks[17], (D,), 0.05),
    }


def _fwd_reference(x, c, w_ada, b_ada, norm1_w, w_in, conv_a_w, conv_a_b, ln_a_w, ln_a_b,
              lb_gamma, rec_norm_w, w_out, norm2_w, w_up, conv_f_w, w_down, final_norm_w):
    p = jax.nn.softmax(lb_gamma.astype(jnp.float32), axis=0)
    lower_bounds = jnp.cumsum(p, axis=0) - p[0:1]
    cond = jax.nn.silu(c)
    for l in range(DEPTH):
        mod = (cond @ w_ada[l] + b_ada[l]).astype(x.dtype)[:, None, :]
        sh1, sc1, g1, sh2, sc2, g2 = jnp.split(mod, N_MOD, axis=-1)
        h = _rmsnorm(x, norm1_w[l]) * (1.0 + sc1) + sh1
        x = x + g1 * _token_mixers(h, w_in[l], conv_a_w[l], conv_a_b[l], ln_a_w[l], ln_a_b[l],
                                   lower_bounds[l, 0], lower_bounds[l, 1], rec_norm_w[l], w_out[l])
        h = _rmsnorm(x, norm2_w[l]) * (1.0 + sc2) + sh2
        x = x + g2 * _conv_ffn(h, w_up[l], conv_f_w[l], w_down[l])
    return _rmsnorm(x, final_norm_w)


import jax as _jax
import jax.numpy as _jnp

TWIN_FORMAT = 'train_step'
FWD_PARAMS = ['x', 'c', 'w_ada', 'b_ada', 'norm1_w', 'w_in', 'conv_a_w', 'conv_a_b', 'ln_a_w', 'ln_a_b', 'lb_gamma', 'rec_norm_w', 'w_out', 'norm2_w', 'w_up', 'conv_f_w', 'w_down', 'final_norm_w']
TWIN_WEIGHTS = ['w_ada', 'b_ada', 'norm1_w', 'w_in', 'conv_a_w', 'conv_a_b', 'ln_a_w', 'ln_a_b', 'lb_gamma', 'rec_norm_w', 'w_out', 'norm2_w', 'w_up', 'conv_f_w', 'w_down', 'final_norm_w']
TWIN_DIFF_INPUT = 'x'
TWIN_INPUTS = ['x', 'c', 'w_ada', 'b_ada', 'norm1_w', 'w_in', 'conv_a_w', 'conv_a_b', 'ln_a_w', 'ln_a_b', 'lb_gamma', 'rec_norm_w', 'w_out', 'norm2_w', 'w_up', 'conv_f_w', 'w_down', 'final_norm_w', 'loss_target', 'm_w_ada', 'm_b_ada', 'm_norm1_w', 'm_w_in', 'm_conv_a_w', 'm_conv_a_b', 'm_ln_a_w', 'm_ln_a_b', 'm_lb_gamma', 'm_rec_norm_w', 'm_w_out', 'm_norm2_w', 'm_w_up', 'm_conv_f_w', 'm_w_down', 'm_final_norm_w', 'v_w_ada', 'v_b_ada', 'v_norm1_w', 'v_w_in', 'v_conv_a_w', 'v_conv_a_b', 'v_ln_a_w', 'v_ln_a_b', 'v_lb_gamma', 'v_rec_norm_w', 'v_w_out', 'v_norm2_w', 'v_w_up', 'v_conv_f_w', 'v_w_down', 'v_final_norm_w']
TWIN_OUTPUTS = ['loss', 'grad_x', 'grad_w_ada', 'grad_b_ada', 'grad_norm1_w', 'grad_w_in', 'grad_conv_a_w', 'grad_conv_a_b', 'grad_ln_a_w', 'grad_ln_a_b', 'grad_lb_gamma', 'grad_rec_norm_w', 'grad_w_out', 'grad_norm2_w', 'grad_w_up', 'grad_conv_f_w', 'grad_w_down', 'grad_final_norm_w', 'delta_w_ada', 'delta_b_ada', 'delta_norm1_w', 'delta_w_in', 'delta_conv_a_w', 'delta_conv_a_b', 'delta_ln_a_w', 'delta_ln_a_b', 'delta_lb_gamma', 'delta_rec_norm_w', 'delta_w_out', 'delta_norm2_w', 'delta_w_up', 'delta_conv_f_w', 'delta_w_down', 'delta_final_norm_w', 'new_m_w_ada', 'new_m_b_ada', 'new_m_norm1_w', 'new_m_w_in', 'new_m_conv_a_w', 'new_m_conv_a_b', 'new_m_ln_a_w', 'new_m_ln_a_b', 'new_m_lb_gamma', 'new_m_rec_norm_w', 'new_m_w_out', 'new_m_norm2_w', 'new_m_w_up', 'new_m_conv_f_w', 'new_m_w_down', 'new_m_final_norm_w', 'new_v_w_ada', 'new_v_b_ada', 'new_v_norm1_w', 'new_v_w_in', 'new_v_conv_a_w', 'new_v_conv_a_b', 'new_v_ln_a_w', 'new_v_ln_a_b', 'new_v_lb_gamma', 'new_v_rec_norm_w', 'new_v_w_out', 'new_v_norm2_w', 'new_v_w_up', 'new_v_conv_f_w', 'new_v_w_down', 'new_v_final_norm_w']
TWIN_LEAF_KINDS = {'loss': 'loss', 'grad_x': 'grad_x', 'grad_w_ada': 'grad_w', 'grad_b_ada': 'grad_w', 'grad_norm1_w': 'grad_w', 'grad_w_in': 'grad_w', 'grad_conv_a_w': 'grad_w', 'grad_conv_a_b': 'grad_w', 'grad_ln_a_w': 'grad_w', 'grad_ln_a_b': 'grad_w', 'grad_lb_gamma': 'grad_w', 'grad_rec_norm_w': 'grad_w', 'grad_w_out': 'grad_w', 'grad_norm2_w': 'grad_w', 'grad_w_up': 'grad_w', 'grad_conv_f_w': 'grad_w', 'grad_w_down': 'grad_w', 'grad_final_norm_w': 'grad_w', 'delta_w_ada': 'delta_w', 'delta_b_ada': 'delta_w', 'delta_norm1_w': 'delta_w', 'delta_w_in': 'delta_w', 'delta_conv_a_w': 'delta_w', 'delta_conv_a_b': 'delta_w', 'delta_ln_a_w': 'delta_w', 'delta_ln_a_b': 'delta_w', 'delta_lb_gamma': 'delta_w', 'delta_rec_norm_w': 'delta_w', 'delta_w_out': 'delta_w', 'delta_norm2_w': 'delta_w', 'delta_w_up': 'delta_w', 'delta_conv_f_w': 'delta_w', 'delta_w_down': 'delta_w', 'delta_final_norm_w': 'delta_w', 'new_m_w_ada': 'new_m', 'new_m_b_ada': 'new_m', 'new_m_norm1_w': 'new_m', 'new_m_w_in': 'new_m', 'new_m_conv_a_w': 'new_m', 'new_m_conv_a_b': 'new_m', 'new_m_ln_a_w': 'new_m', 'new_m_ln_a_b': 'new_m', 'new_m_lb_gamma': 'new_m', 'new_m_rec_norm_w': 'new_m', 'new_m_w_out': 'new_m', 'new_m_norm2_w': 'new_m', 'new_m_w_up': 'new_m', 'new_m_conv_f_w': 'new_m', 'new_m_w_down': 'new_m', 'new_m_final_norm_w': 'new_m', 'new_v_w_ada': 'new_v', 'new_v_b_ada': 'new_v', 'new_v_norm1_w': 'new_v', 'new_v_w_in': 'new_v', 'new_v_conv_a_w': 'new_v', 'new_v_conv_a_b': 'new_v', 'new_v_ln_a_w': 'new_v', 'new_v_ln_a_b': 'new_v', 'new_v_lb_gamma': 'new_v', 'new_v_rec_norm_w': 'new_v', 'new_v_w_out': 'new_v', 'new_v_norm2_w': 'new_v', 'new_v_w_up': 'new_v', 'new_v_conv_f_w': 'new_v', 'new_v_w_down': 'new_v', 'new_v_final_norm_w': 'new_v'}


def _forward(args):
    return _fwd_reference(*[args[k] for k in FWD_PARAMS])


def _output_shape():
    def fwd():
        inp = _fwd_setup_inputs(0)
        return _fwd_reference(*[inp[k] for k in FWD_PARAMS])
    out = _jax.eval_shape(fwd)
    return out.shape, out.dtype

N_MICROBATCH = 1
ADAM_LR = 0.001
ADAM_B1 = 0.9
ADAM_B2 = 0.999
ADAM_EPS = 1e-08
ADAM_WD = 0.01
ADAM_STEP = 10
PER_EXAMPLE_BATCH_AXIS = {'x': 0, 'c': 0, 'loss_target': 0}
SHARED_INPUTS = []
_WEIGHT_DTYPES = {'w_ada': _jnp.float32, 'b_ada': _jnp.float32, 'norm1_w': _jnp.float32, 'w_in': _jnp.float32, 'conv_a_w': _jnp.float32, 'conv_a_b': _jnp.float32, 'ln_a_w': _jnp.float32, 'ln_a_b': _jnp.float32, 'lb_gamma': _jnp.float32, 'rec_norm_w': _jnp.float32, 'w_out': _jnp.float32, 'norm2_w': _jnp.float32, 'w_up': _jnp.float32, 'conv_f_w': _jnp.float32, 'w_down': _jnp.float32, 'final_norm_w': _jnp.float32}
MOMENT_SCALE = {'w_ada': 9.091623e-02, 'b_ada': 1.715282e-01, 'norm1_w': 5.362416e-02, 'w_in': 3.183800e-02, 'conv_a_w': 4.880586e-02, 'conv_a_b': 1.008415e-01, 'ln_a_w': 6.813480e-02, 'ln_a_b': 6.213153e-02, 'lb_gamma': 3.009512e-03, 'rec_norm_w': 5.845551e-02, 'w_out': 4.554905e-02, 'norm2_w': 8.636255e-02, 'w_up': 3.570721e-02, 'conv_f_w': 3.781024e-02, 'w_down': 5.863677e-02, 'final_norm_w': 6.418442e+01}


def _to_microbatches(a, axis):
    t = _jnp.moveaxis(a, axis, 0)
    t = t.reshape((N_MICROBATCH, t.shape[0] // N_MICROBATCH) + t.shape[1:])
    return _jnp.moveaxis(t, 1, axis + 1)


def setup_inputs(seed: int = 0) -> dict:
    inp = _fwd_setup_inputs(seed)
    key = _jax.random.fold_in(_jax.random.key(seed), 7919)
    shape, _ = _output_shape()
    out = dict(inp)
    out["loss_target"] = _jax.random.normal(_jax.random.fold_in(key, 0), shape, _jnp.float32)
    for i, name in enumerate(TWIN_WEIGHTS):
        w = inp[name].astype(_jnp.float32)
        if MOMENT_SCALE is None:
            s = _jnp.sqrt(_jnp.mean(_jnp.square(w)) + 1e-30)
        else:
            s = MOMENT_SCALE[name]
        km, kv = _jax.random.split(_jax.random.fold_in(key, i + 1))
        out[name] = w
        out["m_" + name] = s * _jax.random.normal(km, w.shape, _jnp.float32)
        out["v_" + name] = (s * s) * _jax.random.uniform(kv, w.shape, _jnp.float32, 0.5, 1.5)
    if N_MICROBATCH > 1:
        for name, axis in PER_EXAMPLE_BATCH_AXIS.items():
            out[name] = _to_microbatches(out[name], axis)
    return {'x': out['x'], 'c': out['c'], 'w_ada': out['w_ada'], 'b_ada': out['b_ada'], 'norm1_w': out['norm1_w'], 'w_in': out['w_in'], 'conv_a_w': out['conv_a_w'], 'conv_a_b': out['conv_a_b'], 'ln_a_w': out['ln_a_w'], 'ln_a_b': out['ln_a_b'], 'lb_gamma': out['lb_gamma'], 'rec_norm_w': out['rec_norm_w'], 'w_out': out['w_out'], 'norm2_w': out['norm2_w'], 'w_up': out['w_up'], 'conv_f_w': out['conv_f_w'], 'w_down': out['w_down'], 'final_norm_w': out['final_norm_w'], 'loss_target': out['loss_target'], 'm_w_ada': out['m_w_ada'], 'm_b_ada': out['m_b_ada'], 'm_norm1_w': out['m_norm1_w'], 'm_w_in': out['m_w_in'], 'm_conv_a_w': out['m_conv_a_w'], 'm_conv_a_b': out['m_conv_a_b'], 'm_ln_a_w': out['m_ln_a_w'], 'm_ln_a_b': out['m_ln_a_b'], 'm_lb_gamma': out['m_lb_gamma'], 'm_rec_norm_w': out['m_rec_norm_w'], 'm_w_out': out['m_w_out'], 'm_norm2_w': out['m_norm2_w'], 'm_w_up': out['m_w_up'], 'm_conv_f_w': out['m_conv_f_w'], 'm_w_down': out['m_w_down'], 'm_final_norm_w': out['m_final_norm_w'], 'v_w_ada': out['v_w_ada'], 'v_b_ada': out['v_b_ada'], 'v_norm1_w': out['v_norm1_w'], 'v_w_in': out['v_w_in'], 'v_conv_a_w': out['v_conv_a_w'], 'v_conv_a_b': out['v_conv_a_b'], 'v_ln_a_w': out['v_ln_a_w'], 'v_ln_a_b': out['v_ln_a_b'], 'v_lb_gamma': out['v_lb_gamma'], 'v_rec_norm_w': out['v_rec_norm_w'], 'v_w_out': out['v_w_out'], 'v_norm2_w': out['v_norm2_w'], 'v_w_up': out['v_w_up'], 'v_conv_f_w': out['v_conv_f_w'], 'v_w_down': out['v_w_down'], 'v_final_norm_w': out['v_final_norm_w']}


def _loss(weights, diff, rest, loss_target):
    with _jax.named_scope("forward"):
        args = {**rest, TWIN_DIFF_INPUT: diff, **{k: w.astype(_WEIGHT_DTYPES[k]) for k, w in weights.items()}}
        y = _forward(args)
    with _jax.named_scope("loss_head"):
        err = _jnp.square(y.astype(_jnp.float32) - loss_target)
        return 0.5 * _jnp.sum(_jnp.mean(err, axis=-1)) if err.ndim else 0.5 * err


def _adamw(w, g, m, v):
    m = ADAM_B1 * m + (1.0 - ADAM_B1) * g
    v = ADAM_B2 * v + (1.0 - ADAM_B2) * _jnp.square(g)
    m_hat = m / (1.0 - ADAM_B1 ** ADAM_STEP)
    v_hat = v / (1.0 - ADAM_B2 ** ADAM_STEP)
    delta = -ADAM_LR * (m_hat / (_jnp.sqrt(v_hat) + ADAM_EPS) + ADAM_WD * w)
    return delta, m, v


def reference(x, c, w_ada, b_ada, norm1_w, w_in, conv_a_w, conv_a_b, ln_a_w, ln_a_b, lb_gamma, rec_norm_w, w_out, norm2_w, w_up, conv_f_w, w_down, final_norm_w, loss_target, m_w_ada, m_b_ada, m_norm1_w, m_w_in, m_conv_a_w, m_conv_a_b, m_ln_a_w, m_ln_a_b, m_lb_gamma, m_rec_norm_w, m_w_out, m_norm2_w, m_w_up, m_conv_f_w, m_w_down, m_final_norm_w, v_w_ada, v_b_ada, v_norm1_w, v_w_in, v_conv_a_w, v_conv_a_b, v_ln_a_w, v_ln_a_b, v_lb_gamma, v_rec_norm_w, v_w_out, v_norm2_w, v_w_up, v_conv_f_w, v_w_down, v_final_norm_w):
    given = dict(x=x, c=c, w_ada=w_ada, b_ada=b_ada, norm1_w=norm1_w, w_in=w_in, conv_a_w=conv_a_w, conv_a_b=conv_a_b, ln_a_w=ln_a_w, ln_a_b=ln_a_b, lb_gamma=lb_gamma, rec_norm_w=rec_norm_w, w_out=w_out, norm2_w=norm2_w, w_up=w_up, conv_f_w=conv_f_w, w_down=w_down, final_norm_w=final_norm_w, loss_target=loss_target, m_w_ada=m_w_ada, m_b_ada=m_b_ada, m_norm1_w=m_norm1_w, m_w_in=m_w_in, m_conv_a_w=m_conv_a_w, m_conv_a_b=m_conv_a_b, m_ln_a_w=m_ln_a_w, m_ln_a_b=m_ln_a_b, m_lb_gamma=m_lb_gamma, m_rec_norm_w=m_rec_norm_w, m_w_out=m_w_out, m_norm2_w=m_norm2_w, m_w_up=m_w_up, m_conv_f_w=m_conv_f_w, m_w_down=m_w_down, m_final_norm_w=m_final_norm_w, v_w_ada=v_w_ada, v_b_ada=v_b_ada, v_norm1_w=v_norm1_w, v_w_in=v_w_in, v_conv_a_w=v_conv_a_w, v_conv_a_b=v_conv_a_b, v_ln_a_w=v_ln_a_w, v_ln_a_b=v_ln_a_b, v_lb_gamma=v_lb_gamma, v_rec_norm_w=v_rec_norm_w, v_w_out=v_w_out, v_norm2_w=v_norm2_w, v_w_up=v_w_up, v_conv_f_w=v_conv_f_w, v_w_down=v_w_down, v_final_norm_w=v_final_norm_w)
    weights = {n: given[n] for n in TWIN_WEIGHTS}
    shared = {n: given[n] for n in SHARED_INPUTS}
    per_example = {n: given[n] for n in ['x', 'c']}
    grad_fn = _jax.value_and_grad(_loss, argnums=(0, 1))

    def one_microbatch(ex, loss_target):
        ex = dict(ex)
        diff = ex.pop(TWIN_DIFF_INPUT)
        return grad_fn(weights, diff, {**shared, **ex}, loss_target)

    if N_MICROBATCH == 1:
        loss, (grad_w, grad_x) = one_microbatch(per_example, given["loss_target"])
    else:
        def body(carry, xs):
            loss_sum, grad_sum = carry
            l_k, (gw_k, gx_k) = one_microbatch(xs[0], xs[1])
            with _jax.named_scope("update"):
                return (loss_sum + l_k, _jax.tree.map(_jnp.add, grad_sum, gw_k)), gx_k

        init = (_jnp.zeros((), _jnp.float32), _jax.tree.map(_jnp.zeros_like, weights))
        (loss, grad_w), grad_x = _jax.lax.scan(body, init, (per_example, given["loss_target"]))
    with _jax.named_scope("update"):
        delta_w, new_m, new_v = {}, {}, {}
        for n in TWIN_WEIGHTS:
            delta_w[n], new_m[n], new_v[n] = _adamw(weights[n], grad_w[n], given["m_" + n], given["v_" + n])
    return (loss, grad_x, *[grad_w[n] for n in TWIN_WEIGHTS], *[delta_w[n] for n in TWIN_WEIGHTS],
            *[new_m[n] for n in TWIN_WEIGHTS], *[new_v[n] for n in TWIN_WEIGHTS])
```

```python
import numpy as np
import jax
import jax.numpy as jnp
from jax import lax
from jax.experimental import pallas as pl
from jax.experimental.pallas import tpu as pltpu

F32 = jnp.float32
BF16 = jnp.bfloat16

D_MODEL = 1024
DEPTH = 2
HEAD_DIM = 64
CONV_CH = 256
CONV_WIDTH = 31
ATT_WIDTH = 384
N_HEADS = 6
DILATIONS = (1, 4, 16)
ATT_HALF = 64
ATT_BLOCK = 128
ALIBI_MAX_EXP = 8.0
MASK_VALUE = -1e30
REC_WIDTH = 384
REC_CHUNK = 64
F_TINY = 1e-30
D_FF = 2816
N_MOD = 6
EPS = 1e-6
G_CONV = (0, 512)
G_QKV = (512, 1664)
G_REC = (1664, 3584)
IN_COLS = 3584

ADAM_LR = 0.001
ADAM_B1 = 0.9
ADAM_B2 = 0.999
ADAM_EPS = 1e-08
ADAM_WD = 0.01
ADAM_STEP = 10

VMEM_LIMIT_BYTES = 56 * 1024 * 1024
LANES = 128
MESH = pl.DeviceIdType.MESH
ANY = pl.BlockSpec(memory_space=pl.ANY)


def _params(n_axes):
    return pltpu.CompilerParams(dimension_semantics=("arbitrary",) * n_axes,
                                vmem_limit_bytes=VMEM_LIMIT_BYTES)


def _tile(n, target):
    best = None
    for t in range(LANES, min(n, target) + 1, LANES):
        if n % t == 0:
            best = t
    return best or n


def _sigmoid(x):
    return jax.nn.sigmoid(x)


def _silu_grad(x):
    s = _sigmoid(x)
    return s * (1.0 + x * (1.0 - s))


def _matmul(a, b, mode, out_dtype, name, tm=512, tn=512, tk=1024):
    if mode == "nn":
        (m, k), (k2, n) = a.shape, b.shape
    elif mode == "nt":
        (m, k), (n, k2) = a.shape, b.shape
    else:
        (k, m), (k2, n) = a.shape, b.shape
    assert k == k2, (a.shape, b.shape, mode)
    tm, tn, tk = _tile(m, tm), _tile(n, tn), _tile(k, tk)
    nk = k // tk
    a_spec = (pl.BlockSpec((tk, tm), lambda i, j, kk: (kk, i)) if mode == "tn"
              else pl.BlockSpec((tm, tk), lambda i, j, kk: (i, kk)))
    b_spec = (pl.BlockSpec((tn, tk), lambda i, j, kk: (j, kk)) if mode == "nt"
              else pl.BlockSpec((tk, tn), lambda i, j, kk: (kk, j)))
    dims = {"nn": (((1,), (0,)), ((), ())), "nt": (((1,), (1,)), ((), ())),
            "tn": (((0,), (0,)), ((), ()))}[mode]

    def body(a_ref, b_ref, o_ref, acc_ref):
        kk = pl.program_id(2)

        @pl.when(kk == 0)
        def _():
            acc_ref[...] = jnp.zeros_like(acc_ref)

        acc_ref[...] += lax.dot_general(a_ref[...].astype(BF16), b_ref[...].astype(BF16), dims,
                                        preferred_element_type=F32)

        @pl.when(kk == nk - 1)
        def _():
            o_ref[...] = acc_ref[...].astype(out_dtype)

    return pl.pallas_call(
        body, name=name, grid=(m // tm, n // tn, nk),
        in_specs=[a_spec, b_spec],
        out_specs=pl.BlockSpec((tm, tn), lambda i, j, kk: (i, j)),
        out_shape=jax.ShapeDtypeStruct((m, n), out_dtype),
        scratch_shapes=[pltpu.VMEM((tm, tn), F32)],
        compiler_params=pltpu.CompilerParams(dimension_semantics=("parallel", "parallel", "arbitrary"),
                                             vmem_limit_bytes=VMEM_LIMIT_BYTES),
    )(a, b)


NORM_ROWS = 256


def _row_spec(width, rows=NORM_ROWS):
    return pl.BlockSpec((rows, width), lambda i: (i, 0))


def _vec_spec(width):
    return pl.BlockSpec((1, width), lambda i: (0, 0))


def _resid_norm_mod(x, r, g, nw, sc, sh, name):
    s, d = x.shape
    has_r = r is not None

    def body(*refs):
        if has_r:
            x_ref, r_ref, g_ref, nw_ref, sc_ref, sh_ref, xn_ref, h_ref = refs
            xn = x_ref[...] + g_ref[...] * r_ref[...]
            xn_ref[...] = xn
        else:
            x_ref, nw_ref, sc_ref, sh_ref, h_ref = refs
            xn = x_ref[...]
        rstd = lax.rsqrt(jnp.mean(xn * xn, axis=-1, keepdims=True) + EPS)
        y = xn * rstd * nw_ref[...]
        h_ref[...] = (y * (1.0 + sc_ref[...]) + sh_ref[...]).astype(BF16)

    if has_r:
        ins, in_specs = (x, r, g, nw, sc, sh), [_row_spec(d), _row_spec(d)] + [_vec_spec(d)] * 4
        out_shape = (jax.ShapeDtypeStruct((s, d), F32), jax.ShapeDtypeStruct((s, d), BF16))
        out_specs = (_row_spec(d), _row_spec(d))
    else:
        ins, in_specs = (x, nw, sc, sh), [_row_spec(d)] + [_vec_spec(d)] * 3
        out_shape = jax.ShapeDtypeStruct((s, d), BF16)
        out_specs = _row_spec(d)
    return pl.pallas_call(body, name=name, grid=(s // NORM_ROWS,), in_specs=in_specs, out_specs=out_specs,
                          out_shape=out_shape, compiler_params=_params(1))(*ins)


def _final_loss(x, r, g, fw, tgt, name):
    s, d = x.shape

    def body(x_ref, r_ref, g_ref, fw_ref, t_ref, loss_ref, dx_ref, dr_ref, dg_ref, dfw_ref):
        @pl.when(pl.program_id(0) == 0)
        def _():
            loss_ref[...] = jnp.zeros_like(loss_ref)
            dg_ref[...] = jnp.zeros_like(dg_ref)
            dfw_ref[...] = jnp.zeros_like(dfw_ref)

        rr = r_ref[...]
        gg = g_ref[...]
        xn = x_ref[...] + gg * rr
        rstd = lax.rsqrt(jnp.mean(xn * xn, axis=-1, keepdims=True) + EPS)
        xh = xn * rstd
        fwv = fw_ref[...]
        e = xh * fwv - t_ref[...]
        loss_ref[...] += 0.5 * jnp.sum(jnp.mean(e * e, axis=-1, keepdims=True), axis=0, keepdims=True)
        dy = e * (1.0 / d)
        dfw_ref[...] += jnp.sum(dy * xh, axis=0, keepdims=True)
        dxh = dy * fwv
        dx = rstd * (dxh - xh * jnp.mean(dxh * xh, axis=-1, keepdims=True))
        dx_ref[...] = dx
        dr_ref[...] = (gg * dx).astype(BF16)
        dg_ref[...] += jnp.sum(dx * rr, axis=0, keepdims=True)

    return pl.pallas_call(
        body, name=name, grid=(s // NORM_ROWS,),
        in_specs=[_row_spec(d), _row_spec(d), _vec_spec(d), _vec_spec(d), _row_spec(d)],
        out_specs=(_vec_spec(LANES), _row_spec(d), _row_spec(d), _vec_spec(d), _vec_spec(d)),
        out_shape=(jax.ShapeDtypeStruct((1, LANES), F32), jax.ShapeDtypeStruct((s, d), F32),
                   jax.ShapeDtypeStruct((s, d), BF16), jax.ShapeDtypeStruct((1, d), F32),
                   jax.ShapeDtypeStruct((1, d), F32)),
        compiler_params=_params(1))(x, r, g, fw, tgt)


def _norm_bwd(x, dhs, dxres, nw, sc, g, r, name):
    s, d = x.shape
    n_dh = len(dhs)
    has_g = g is not None

    def body(*refs):
        x_ref = refs[0]
        dh_refs = refs[1:1 + n_dh]
        dxres_ref, nw_ref, sc_ref = refs[1 + n_dh:4 + n_dh]
        pos = 4 + n_dh
        if has_g:
            g_ref, r_ref = refs[pos:pos + 2]
            pos += 2
            dx_ref, dr_ref, dsh_ref, dsc_ref, dnw_ref, dg_ref = refs[pos:]
            accs = (dsh_ref, dsc_ref, dnw_ref, dg_ref)
        else:
            dx_ref, dsh_ref, dsc_ref, dnw_ref = refs[pos:]
            accs = (dsh_ref, dsc_ref, dnw_ref)

        @pl.when(pl.program_id(0) == 0)
        def _():
            for acc in accs:
                acc[...] = jnp.zeros_like(acc)

        xv = x_ref[...]
        dh = dh_refs[0][...]
        for extra in dh_refs[1:]:
            dh = dh + extra[...]
        rstd = lax.rsqrt(jnp.mean(xv * xv, axis=-1, keepdims=True) + EPS)
        xh = xv * rstd
        nwv = nw_ref[...]
        dsh_ref[...] += jnp.sum(dh, axis=0, keepdims=True)
        dsc_ref[...] += jnp.sum(dh * (xh * nwv), axis=0, keepdims=True)
        dy = dh * (1.0 + sc_ref[...])
        dnw_ref[...] += jnp.sum(dy * xh, axis=0, keepdims=True)
        dxh = dy * nwv
        dx = dxres_ref[...] + rstd * (dxh - xh * jnp.mean(dxh * xh, axis=-1, keepdims=True))
        dx_ref[...] = dx
        if has_g:
            dr_ref[...] = (g_ref[...] * dx).astype(BF16)
            dg_ref[...] += jnp.sum(dx * r_ref[...], axis=0, keepdims=True)

    ins = [x, *dhs, dxres, nw, sc]
    in_specs = [_row_spec(d)] * (2 + n_dh) + [_vec_spec(d)] * 2
    out_shape = [jax.ShapeDtypeStruct((s, d), F32)]
    out_specs = [_row_spec(d)]
    if has_g:
        ins += [g, r]
        in_specs += [_vec_spec(d), _row_spec(d)]
        out_shape.append(jax.ShapeDtypeStruct((s, d), BF16))
        out_specs.append(_row_spec(d))
    n_vec = 4 if has_g else 3
    out_shape += [jax.ShapeDtypeStruct((1, d), F32)] * n_vec
    out_specs += [_vec_spec(d)] * n_vec
    return pl.pallas_call(body, name=name, grid=(s // NORM_ROWS,), in_specs=in_specs, out_specs=tuple(out_specs),
                          out_shape=tuple(out_shape), compiler_params=_params(1))(*ins)


FFN_ROWS = 256
FFN_COLS = 1408
HALO = 8
INV_SQRT2 = 0.7071067811865476
INV_SQRT_2PI = 0.3989422804014327


def _gelu(x):
    return 0.5 * x * (1.0 + lax.erf(x * INV_SQRT2))


def _gelu_grad(x):
    return 0.5 * (1.0 + lax.erf(x * INV_SQRT2)) + x * (INV_SQRT_2PI * jnp.exp(-0.5 * x * x))


def _halo_specs(rows, cols, halo, n_rows_total, col_of):
    per = rows // halo
    last = n_rows_total // halo - 1
    cur = pl.BlockSpec((rows, cols), lambda j, i: (i, col_of(j)))
    prev = pl.BlockSpec((halo, cols), lambda j, i: (jnp.maximum(i * per - 1, 0), col_of(j)))
    nxt = pl.BlockSpec((halo, cols), lambda j, i: (jnp.minimum((i + 1) * per, last), col_of(j)))
    return [prev, cur, nxt]


def _shift_rows(x, k):
    n = x.shape[0]
    return pltpu.roll(x, k % n, axis=0)


def _conv3(ext, w):
    return w[0:1, :] * _shift_rows(ext, 1) + w[1:2, :] * ext + w[2:3, :] * _shift_rows(ext, -1)


def _ext_block(prev_ref, cur_ref, next_ref, i, n_i):
    prev = jnp.where(i > 0, prev_ref[...], 0.0)
    nxt = jnp.where(i < n_i - 1, next_ref[...], 0.0)
    return jnp.concatenate([prev, cur_ref[...], nxt], axis=0)


def _ffn_act(u, cw, name):
    s = u.shape[0]
    nc, ns = D_FF // FFN_COLS, s // FFN_ROWS

    def body(gp, gc, gn, vp, vc, vn, wg_ref, wv_ref, o_ref):
        i = pl.program_id(1)
        cg = _conv3(_ext_block(gp, gc, gn, i, ns), wg_ref[...])[HALO:HALO + FFN_ROWS]
        cv = _conv3(_ext_block(vp, vc, vn, i, ns), wv_ref[...])[HALO:HALO + FFN_ROWS]
        o_ref[...] = (_gelu(cg) * cv).astype(BF16)

    in_specs = (_halo_specs(FFN_ROWS, FFN_COLS, HALO, s, lambda j: j)
                + _halo_specs(FFN_ROWS, FFN_COLS, HALO, s, lambda j: j + nc)
                + [pl.BlockSpec((3, FFN_COLS), lambda j, i: (0, j)),
                   pl.BlockSpec((3, FFN_COLS), lambda j, i: (0, j + nc))])
    return pl.pallas_call(
        body, name=name, grid=(nc, ns), in_specs=in_specs,
        out_specs=pl.BlockSpec((FFN_ROWS, FFN_COLS), lambda j, i: (i, j)),
        out_shape=jax.ShapeDtypeStruct((s, D_FF), BF16), compiler_params=_params(2),
    )(u, u, u, u, u, u, cw, cw)


def _ffn_act_bwd(u, dact, cw, name):
    s = u.shape[0]
    nc, ns = D_FF // FFN_COLS, s // FFN_ROWS

    def body(gp, gc, gn, vp, vc, vn, dp, dc, dn, wg_ref, wv_ref, dug_ref, duv_ref, dwg_ref, dwv_ref):
        i = pl.program_id(1)

        @pl.when(i == 0)
        def _():
            dwg_ref[...] = jnp.zeros_like(dwg_ref)
            dwv_ref[...] = jnp.zeros_like(dwv_ref)

        ug = _ext_block(gp, gc, gn, i, ns)
        uv = _ext_block(vp, vc, vn, i, ns)
        da = _ext_block(dp, dc, dn, i, ns)
        wg, wv = wg_ref[...], wv_ref[...]
        cg, cv = _conv3(ug, wg), _conv3(uv, wv)
        dcg = da * cv * _gelu_grad(cg)
        dcv = da * _gelu(cg)
        inner = slice(HALO, HALO + FFN_ROWS)
        for d_c, uu, w, du_ref, dw_ref in ((dcg, ug, wg, dug_ref, dwg_ref), (dcv, uv, wv, duv_ref, dwv_ref)):
            du = w[0:1, :] * _shift_rows(d_c, -1) + w[1:2, :] * d_c + w[2:3, :] * _shift_rows(d_c, 1)
            du_ref[...] = du[inner].astype(BF16)
            d_in = d_c[inner]
            for tap in range(3):
                dw_ref[tap:tap + 1, :] += jnp.sum(d_in * _shift_rows(uu, 1 - tap)[inner], axis=0, keepdims=True)

    in_specs = (_halo_specs(FFN_ROWS, FFN_COLS, HALO, s, lambda j: j)
                + _halo_specs(FFN_ROWS, FFN_COLS, HALO, s, lambda j: j + nc)
                + _halo_specs(FFN_ROWS, FFN_COLS, HALO, s, lambda j: j)
                + [pl.BlockSpec((3, FFN_COLS), lambda j, i: (0, j)),
                   pl.BlockSpec((3, FFN_COLS), lambda j, i: (0, j + nc))])
    blk = pl.BlockSpec((FFN_ROWS, FFN_COLS), lambda j, i: (i, j))
    acc = pl.BlockSpec((HALO, FFN_COLS), lambda j, i: (0, j))
    return pl.pallas_call(
        body, name=name, grid=(nc, ns), in_specs=in_specs, out_specs=(blk, blk, acc, acc),
        out_shape=(jax.ShapeDtypeStruct((s, D_FF), BF16), jax.ShapeDtypeStruct((s, D_FF), BF16),
                   jax.ShapeDtypeStruct((HALO, D_FF), F32), jax.ShapeDtypeStruct((HALO, D_FF), F32)),
        compiler_params=_params(2),
    )(u, u, u, u, u, u, dact, dact, dact, cw, cw)


CONV_ROWS = 512
CONV_HALO = 16
CONV_PAD = CONV_WIDTH // 2


def _conv_halo_specs(cols, s):
    per = CONV_ROWS // CONV_HALO
    last = s // CONV_HALO - 1
    return [pl.BlockSpec((CONV_HALO, cols), lambda i: (jnp.maximum(i * per - 1, 0), 0)),
            pl.BlockSpec((CONV_ROWS, cols), lambda i: (i, 0)),
            pl.BlockSpec((CONV_HALO, cols), lambda i: (jnp.minimum((i + 1) * per, last), 0))]


def _glu_ext(pp, pc, pn, i, n_i):
    ext = _ext_block(pp, pc, pn, i, n_i)
    return ext[:, :CONV_CH] * _sigmoid(ext[:, CONV_CH:])


def _conv_mixer(pa, cw, cb, lnw, lnb, name):
    s = pa.shape[0]
    ns = s // CONV_ROWS

    def body(pp, pc, pn, cw_ref, cb_ref, lnw_ref, lnb_ref, o_ref, c_ref):
        i = pl.program_id(0)
        a = _glu_ext(pp, pc, pn, i, ns)
        acc = jnp.zeros((CONV_ROWS, CONV_CH), F32)
        for tap in range(CONV_WIDTH):
            acc = acc + cw_ref[tap:tap + 1, :] * _shift_rows(a, -(tap + 1))[:CONV_ROWS]
        cv = acc + cb_ref[...]
        c_ref[...] = cv
        mu = jnp.mean(cv, axis=-1, keepdims=True)
        xc = cv - mu
        rstd = lax.rsqrt(jnp.mean(xc * xc, axis=-1, keepdims=True) + EPS)
        y = xc * rstd * lnw_ref[...] + lnb_ref[...]
        o_ref[...] = (y * _sigmoid(y)).astype(BF16)

    vec = pl.BlockSpec((1, CONV_CH), lambda i: (0, 0))
    blk = pl.BlockSpec((CONV_ROWS, CONV_CH), lambda i: (i, 0))
    return pl.pallas_call(
        body, name=name, grid=(ns,),
        in_specs=_conv_halo_specs(2 * CONV_CH, s) + [pl.BlockSpec((CONV_WIDTH, CONV_CH), lambda i: (0, 0)), vec, vec, vec],
        out_specs=(blk, blk),
        out_shape=(jax.ShapeDtypeStruct((s, CONV_CH), BF16), jax.ShapeDtypeStruct((s, CONV_CH), F32)),
        compiler_params=_params(1))(pa, pa, pa, cw, cb, lnw, lnb)


def _conv_mixer_bwd_ln(cv, dout, lnw, lnb, name):
    s = cv.shape[0]

    def body(c_ref, do_ref, lnw_ref, lnb_ref, dc_ref, dlnw_ref, dlnb_ref, dcb_ref):
        @pl.when(pl.program_id(0) == 0)
        def _():
            dlnw_ref[...] = jnp.zeros_like(dlnw_ref)
            dlnb_ref[...] = jnp.zeros_like(dlnb_ref)
            dcb_ref[...] = jnp.zeros_like(dcb_ref)

        c = c_ref[...]
        mu = jnp.mean(c, axis=-1, keepdims=True)
        xc = c - mu
        rstd = lax.rsqrt(jnp.mean(xc * xc, axis=-1, keepdims=True) + EPS)
        xh = xc * rstd
        w = lnw_ref[...]
        y = xh * w + lnb_ref[...]
        dy = do_ref[...] * _silu_grad(y)
        dlnw_ref[...] += jnp.sum(dy * xh, axis=0, keepdims=True)
        dlnb_ref[...] += jnp.sum(dy, axis=0, keepdims=True)
        dxh = dy * w
        dc = rstd * (dxh - jnp.mean(dxh, axis=-1, keepdims=True) - xh * jnp.mean(dxh * xh, axis=-1, keepdims=True))
        dc_ref[...] = dc
        dcb_ref[...] += jnp.sum(dc, axis=0, keepdims=True)

    vec = pl.BlockSpec((1, CONV_CH), lambda i: (0, 0))
    blk = pl.BlockSpec((CONV_ROWS, CONV_CH), lambda i: (i, 0))
    return pl.pallas_call(
        body, name=name, grid=(s // CONV_ROWS,), in_specs=[blk, blk, vec, vec], out_specs=(blk, vec, vec, vec),
        out_shape=(jax.ShapeDtypeStruct((s, CONV_CH), F32),) + (jax.ShapeDtypeStruct((1, CONV_CH), F32),) * 3,
        compiler_params=_params(1))(cv, dout, lnw, lnb)


def _conv_mixer_bwd_conv(pa, dc, cw, name):
    s = pa.shape[0]
    ns = s // CONV_ROWS

    def body(pp, pc, pn, dp, dcc, dn, cw_ref, dpa_ref, dcw_ref):
        i = pl.program_id(0)

        @pl.when(i == 0)
        def _():
            dcw_ref[...] = jnp.zeros_like(dcw_ref)

        a = _glu_ext(pp, pc, pn, i, ns)
        dce = _ext_block(dp, dcc, dn, i, ns)
        dcur = dcc[...]
        da = jnp.zeros((CONV_ROWS, CONV_CH), F32)
        for tap in range(CONV_WIDTH):
            da = da + cw_ref[tap:tap + 1, :] * _shift_rows(dce, -(CONV_WIDTH - tap))[:CONV_ROWS]
            dcw_ref[tap:tap + 1, :] += jnp.sum(dcur * _shift_rows(a, -(tap + 1))[:CONV_ROWS], axis=0, keepdims=True)
        cur = pc[...]
        val, sg = cur[:, :CONV_CH], _sigmoid(cur[:, CONV_CH:])
        dpa_ref[:, :CONV_CH] = (da * sg).astype(BF16)
        dpa_ref[:, CONV_CH:] = (da * val * sg * (1.0 - sg)).astype(BF16)

    return pl.pallas_call(
        body, name=name, grid=(ns,),
        in_specs=_conv_halo_specs(2 * CONV_CH, s) + _conv_halo_specs(CONV_CH, s)
        + [pl.BlockSpec((CONV_WIDTH, CONV_CH), lambda i: (0, 0))],
        out_specs=(pl.BlockSpec((CONV_ROWS, 2 * CONV_CH), lambda i: (i, 0)),
                   pl.BlockSpec((32, CONV_CH), lambda i: (0, 0))),
        out_shape=(jax.ShapeDtypeStruct((s, 2 * CONV_CH), BF16), jax.ShapeDtypeStruct((32, CONV_CH), F32)),
        compiler_params=_params(1))(pa, pa, pa, dc, dc, dc, cw)


SLOPES = tuple(float(2.0 ** (-ALIBI_MAX_EXP * (h + 1) / N_HEADS)) for h in range(N_HEADS))
ATT_SCALE = HEAD_DIM ** -0.5


def _band_specs(n_blocks, col_of):
    return [pl.BlockSpec((ATT_BLOCK, ATT_WIDTH), lambda r, i: (jnp.maximum(i - 1, 0), col_of(r))),
            pl.BlockSpec((ATT_BLOCK, ATT_WIDTH), lambda r, i: (i, col_of(r))),
            pl.BlockSpec((ATT_BLOCK, ATT_WIDTH), lambda r, i: (jnp.minimum(i + 1, n_blocks - 1), col_of(r)))]


def _win(refs, sl):
    return jnp.concatenate([ref[:, sl] for ref in refs], axis=0)


def _attn_branch(qkv, dil, name):
    s = qkv.shape[0]
    length = s // dil
    nb = length // ATT_BLOCK
    view = qkv.reshape(length, dil * 3 * ATT_WIDTH)

    def body(q_ref, kp, kc, kn, vp, vc, vn, o_ref, l_ref):
        i = pl.program_id(1)
        row = lax.broadcasted_iota(jnp.int32, (ATT_BLOCK, 3 * ATT_BLOCK), 0)
        col = lax.broadcasted_iota(jnp.int32, (ATT_BLOCK, 3 * ATT_BLOCK), 1)
        kpos = (i - 1) * ATT_BLOCK + col
        dist = jnp.abs(i * ATT_BLOCK + row - kpos)
        valid = (dist <= ATT_HALF) & (kpos >= 0) & (kpos < length)
        distf = dist.astype(F32) * float(dil)
        for h in range(N_HEADS):
            sl = slice(h * HEAD_DIM, (h + 1) * HEAD_DIM)
            sc = lax.dot_general(q_ref[:, sl], _win((kp, kc, kn), sl), (((1,), (1,)), ((), ())),
                                 preferred_element_type=F32) * ATT_SCALE - SLOPES[h] * distf
            sc = jnp.where(valid, sc, MASK_VALUE)
            m = jnp.max(sc, axis=-1, keepdims=True)
            p = jnp.exp(sc - m)
            den = jnp.sum(p, axis=-1, keepdims=True)
            o = jnp.dot(p.astype(BF16), _win((vp, vc, vn), sl), preferred_element_type=F32) / den
            o_ref[:, sl] = o
            l_ref[:, sl] = jnp.broadcast_to(m + jnp.log(den), (ATT_BLOCK, HEAD_DIM))

    out_blk = pl.BlockSpec((ATT_BLOCK, ATT_WIDTH), lambda r, i: (i, r))
    o, lse = pl.pallas_call(
        body, name=name, grid=(dil, nb),
        in_specs=[pl.BlockSpec((ATT_BLOCK, ATT_WIDTH), lambda r, i: (i, 3 * r))]
        + _band_specs(nb, lambda r: 3 * r + 1) + _band_specs(nb, lambda r: 3 * r + 2),
        out_specs=(out_blk, out_blk),
        out_shape=(jax.ShapeDtypeStruct((length, dil * ATT_WIDTH), F32),) * 2,
        compiler_params=_params(2))(view, view, view, view, view, view, view)
    return o.reshape(s, ATT_WIDTH), lse.reshape(s, ATT_WIDTH)


ATT_ROWS = 512


def _attn_combine(outs, lses, name):
    s = outs[0].shape[0]

    def body(o1, o2, o3, l1, l2, l3, att_ref, att32_ref, lse_ref):
        ls = [l1[...], l2[...], l3[...]]
        m = jnp.maximum(jnp.maximum(ls[0], ls[1]), ls[2])
        es = [jnp.exp(l - m) for l in ls]
        den = es[0] + es[1] + es[2]
        att = (es[0] * o1[...] + es[1] * o2[...] + es[2] * o3[...]) / den
        att_ref[...] = att.astype(BF16)
        att32_ref[...] = att
        lse_ref[...] = m + jnp.log(den)

    blk = pl.BlockSpec((ATT_ROWS, ATT_WIDTH), lambda i: (i, 0))
    return pl.pallas_call(
        body, name=name, grid=(s // ATT_ROWS,), in_specs=[blk] * 6, out_specs=(blk, blk, blk),
        out_shape=(jax.ShapeDtypeStruct((s, ATT_WIDTH), BF16), jax.ShapeDtypeStruct((s, ATT_WIDTH), F32),
                   jax.ShapeDtypeStruct((s, ATT_WIDTH), F32)),
        compiler_params=_params(1))(*outs, *lses)


def _attn_delta(datt, att, name):
    s = att.shape[0]

    def body(d_ref, a_ref, delta_ref, dbf_ref):
        prod = d_ref[...] * a_ref[...]
        for h in range(N_HEADS):
            sl = slice(h * HEAD_DIM, (h + 1) * HEAD_DIM)
            delta_ref[:, sl] = jnp.broadcast_to(jnp.sum(prod[:, sl], axis=-1, keepdims=True), (ATT_ROWS, HEAD_DIM))
        dbf_ref[...] = d_ref[...].astype(BF16)

    blk = pl.BlockSpec((ATT_ROWS, ATT_WIDTH), lambda i: (i, 0))
    return pl.pallas_call(
        body, name=name, grid=(s // ATT_ROWS,), in_specs=[blk, blk], out_specs=(blk, blk),
        out_shape=(jax.ShapeDtypeStruct((s, ATT_WIDTH), F32), jax.ShapeDtypeStruct((s, ATT_WIDTH), BF16)),
        compiler_params=_params(1))(datt, att)


def _attn_branch_bwd(qkv, do, lse, delta, prev, dil, name):
    s = qkv.shape[0]
    length = s // dil
    nb = length // ATT_BLOCK
    view = qkv.reshape(length, dil * 3 * ATT_WIDTH)
    sub = lambda t: t.reshape(length, dil * ATT_WIDTH)
    has_prev = prev is not None
    tn = (((0,), (0,)), ((), ()))
    nt = (((1,), (1,)), ((), ()))

    def body(*refs):
        qs, ks, vs, dos, ls, des = (refs[3 * n:3 * n + 3] for n in range(6))
        rest = refs[18:]
        if has_prev:
            pq, pk, pv = rest[:3]
            rest = rest[3:]
        dq_ref, dk_ref, dv_ref = rest
        i = pl.program_id(1)
        row = lax.broadcasted_iota(jnp.int32, (ATT_BLOCK, 3 * ATT_BLOCK), 0)
        col = lax.broadcasted_iota(jnp.int32, (ATT_BLOCK, 3 * ATT_BLOCK), 1)
        kpos = (i - 1) * ATT_BLOCK + col
        dist_q = jnp.abs(i * ATT_BLOCK + row - kpos)
        valid_q = (dist_q <= ATT_HALF) & (kpos >= 0) & (kpos < length)
        distf_q = dist_q.astype(F32) * float(dil)
        rowk = lax.broadcasted_iota(jnp.int32, (3 * ATT_BLOCK, ATT_BLOCK), 0)
        colk = lax.broadcasted_iota(jnp.int32, (3 * ATT_BLOCK, ATT_BLOCK), 1)
        qpos = (i - 1) * ATT_BLOCK + rowk
        dist_k = jnp.abs(qpos - (i * ATT_BLOCK + colk))
        valid_k = (dist_k <= ATT_HALF) & (qpos >= 0) & (qpos < length)
        distf_k = dist_k.astype(F32) * float(dil)
        for h in range(N_HEADS):
            sl = slice(h * HEAD_DIM, (h + 1) * HEAD_DIM)
            one = slice(h * HEAD_DIM, h * HEAD_DIM + 1)
            k_win, v_win = _win(ks, sl), _win(vs, sl)
            q_cur, do_cur = qs[1][:, sl], dos[1][:, sl]
            sc = lax.dot_general(q_cur, k_win, nt, preferred_element_type=F32) * ATT_SCALE - SLOPES[h] * distf_q
            p = jnp.exp(jnp.where(valid_q, sc - ls[1][:, one], MASK_VALUE))
            dp = lax.dot_general(do_cur, v_win, nt, preferred_element_type=F32)
            ds = (p * (dp - des[1][:, one]) * ATT_SCALE).astype(BF16)
            dq = jnp.dot(ds, k_win, preferred_element_type=F32)

            q_win, do_win = _win(qs, sl), _win(dos, sl)
            k_cur, v_cur = ks[1][:, sl], vs[1][:, sl]
            sc2 = lax.dot_general(q_win, k_cur, nt, preferred_element_type=F32) * ATT_SCALE - SLOPES[h] * distf_k
            p2 = jnp.exp(jnp.where(valid_k, sc2 - _win(ls, one), MASK_VALUE))
            dv = lax.dot_general(p2.astype(BF16), do_win, tn, preferred_element_type=F32)
            dp2 = lax.dot_general(do_win, v_cur, nt, preferred_element_type=F32)
            ds2 = (p2 * (dp2 - _win(des, one)) * ATT_SCALE).astype(BF16)
            dk = lax.dot_general(ds2, q_win, tn, preferred_element_type=F32)
            if has_prev:
                dq, dk, dv = dq + pq[:, sl], dk + pk[:, sl], dv + pv[:, sl]
            dq_ref[:, sl], dk_ref[:, sl], dv_ref[:, sl] = dq, dk, dv

    blk = pl.BlockSpec((ATT_BLOCK, ATT_WIDTH), lambda r, i: (i, r))
    in_specs = (_band_specs(nb, lambda r: 3 * r) + _band_specs(nb, lambda r: 3 * r + 1)
                + _band_specs(nb, lambda r: 3 * r + 2) + _band_specs(nb, lambda r: r) * 3)
    ins = [view] * 9 + [sub(do)] * 3 + [sub(lse)] * 3 + [sub(delta)] * 3
    if has_prev:
        in_specs += [blk] * 3
        ins += [sub(t) for t in prev]
    outs = pl.pallas_call(
        body, name=name, grid=(dil, nb), in_specs=in_specs, out_specs=(blk, blk, blk),
        out_shape=(jax.ShapeDtypeStruct((length, dil * ATT_WIDTH), F32),) * 3,
        compiler_params=_params(2))(*ins)
    return tuple(t.reshape(s, ATT_WIDTH) for t in outs)


TB = 2 * REC_CHUNK
REC_ROWS = 5 * REC_WIDTH


def _chunk_scan(x, pos, rev):
    for step in (1, 2, 4, 8, 16, 32):
        if rev:
            x = x + jnp.where(pos < REC_CHUNK - step, pltpu.roll(x, TB - step, axis=1), 0.0)
        else:
            x = x + jnp.where(pos >= step, pltpu.roll(x, step, axis=1), 0.0)
    return x


def _hg_prep(qraw, z, lb, rev):
    lane = lax.broadcasted_iota(jnp.int32, (REC_WIDTH, TB), 1)
    pos = lane & (REC_CHUNK - 1)
    in_a = lane < REC_CHUNK
    sig, sigm = _sigmoid(z), _sigmoid(-z)
    f = lb + (1.0 - lb) * sig
    kk = (1.0 - lb) * sigm
    b = _chunk_scan(jnp.log(jnp.maximum(f, F_TINY)), pos, rev)
    end_a = b[:, 0:1] if rev else b[:, REC_CHUNK - 1:REC_CHUNK]
    end_b = b[:, REC_CHUNK:REC_CHUNK + 1] if rev else b[:, TB - 1:TB]
    bend = jnp.where(in_a, end_a, end_b)
    q = qraw * _sigmoid(qraw)
    return dict(pos=pos, in_a=in_a, sig=sig, sigm=sigm, f=f, kk=kk, b=b, end_a=end_a, end_b=end_b,
                q=q, qh=q * jnp.exp(b), kh=kk * jnp.exp(bend - b), ekb=jnp.exp(bend - b))


def _block_diag_mask():
    r = lax.broadcasted_iota(jnp.int32, (REC_WIDTH, REC_WIDTH), 0) // HEAD_DIM
    c = lax.broadcasted_iota(jnp.int32, (REC_WIDTH, REC_WIDTH), 1) // HEAD_DIM
    return (r == c).astype(F32)


def _heads(x):
    return x.reshape(N_HEADS, HEAD_DIM, TB)


def _hg_shift(delta, rev):
    return jnp.where(delta == 0, 0, TB - delta) if rev else delta


def _hg_unshift(delta, rev):
    return delta if rev else jnp.where(delta == 0, 0, TB - delta)


def _hgrn_scan(projt, lb, rev, name):
    s = projt.shape[1]
    nblk = s // TB
    zrow = 2 if rev else 1
    tmap = (lambda i: nblk - 1 - i) if rev else (lambda i: i)
    tn = (((0,), (0,)), ((), ()))
    nt = (((1,), (1,)), ((), ()))

    def body(q_ref, z_ref, v_ref, lb_ref, o_ref, hs_ref, h_ref):
        @pl.when(pl.program_id(0) == 0)
        def _():
            h_ref[...] = jnp.zeros_like(h_ref)

        v = v_ref[...]
        pr = _hg_prep(q_ref[...], z_ref[...], lb_ref[...], rev)
        q, kk, b, pos = pr["q"], pr["kk"], pr["b"], pr["pos"]

        def pair_step(delta, o):
            sh = _hg_shift(delta, rev)
            kd, bd, vd = pltpu.roll(kk, sh, axis=1), pltpu.roll(b, sh, axis=1), pltpu.roll(v, sh, axis=1)
            valid = (pos <= REC_CHUNK - 1 - delta) if rev else (pos >= delta)
            w = jnp.where(valid, q * kd * jnp.exp(jnp.where(valid, b - bd, 0.0)), 0.0)
            a = jnp.sum(_heads(w), axis=1, keepdims=True)
            return o + (a * _heads(vd)).reshape(REC_WIDTH, TB)

        o = lax.fori_loop(0, REC_CHUNK, pair_step, jnp.zeros((REC_WIDTH, TB), F32))
        bd_mask = _block_diag_mask()
        vb = v.astype(BF16)
        order = ((1, ~pr["in_a"], pr["end_b"]), (0, pr["in_a"], pr["end_a"]))
        if not rev:
            order = order[::-1]
        for slot, msk, bend in order:
            h0 = h_ref[...]
            hs_ref[slot] = h0
            o = o + lax.dot_general(h0.astype(BF16), jnp.where(msk, pr["qh"], 0.0).astype(BF16), tn,
                                    preferred_element_type=F32)
            upd = lax.dot_general(jnp.where(msk, pr["kh"], 0.0).astype(BF16), vb, nt, preferred_element_type=F32)
            h_ref[...] = jnp.exp(bend) * h0 + upd * bd_mask
        o_ref[...] = o

    row_blk = lambda r: pl.BlockSpec((REC_WIDTH, TB), lambda i: (r, tmap(i)))
    return pl.pallas_call(
        body, name=name, grid=(nblk,),
        in_specs=[row_blk(0), row_blk(zrow), row_blk(3), pl.BlockSpec((REC_WIDTH, 1), lambda i: (0, 0))],
        out_specs=(pl.BlockSpec((REC_WIDTH, TB), lambda i: (0, tmap(i))),
                   pl.BlockSpec((2, REC_WIDTH, REC_WIDTH), lambda i: (tmap(i), 0, 0))),
        out_shape=(jax.ShapeDtypeStruct((REC_WIDTH, s), F32),
                   jax.ShapeDtypeStruct((s // REC_CHUNK, REC_WIDTH, REC_WIDTH), F32)),
        scratch_shapes=[pltpu.VMEM((REC_WIDTH, REC_WIDTH), F32)],
        compiler_params=_params(1))(projt, projt, projt, lb)


def _hgrn_scan_bwd(projt, lb, dot, hs, prev, rev, name):
    s = projt.shape[1]
    nblk = s // TB
    zrow = 2 if rev else 1
    tmap = (lambda i: i) if rev else (lambda i: nblk - 1 - i)
    has_prev = prev is not None
    tn = (((0,), (0,)), ((), ()))
    nt = (((1,), (1,)), ((), ()))

    def body(*refs):
        q_ref, z_ref, v_ref, lb_ref, do_ref, hs_ref = refs[:6]
        rest = refs[6:]
        if has_prev:
            pq_ref, pv_ref = rest[:2]
            rest = rest[2:]
        dq_ref, dz_ref, dv_ref, dlb_ref, dh_ref = rest

        @pl.when(pl.program_id(0) == 0)
        def _():
            dh_ref[...] = jnp.zeros_like(dh_ref)
            dlb_ref[...] = jnp.zeros_like(dlb_ref)

        qraw, v, do, lbv = q_ref[...], v_ref[...], do_ref[...], lb_ref[...]
        pr = _hg_prep(qraw, z_ref[...], lbv, rev)
        q, kk, b, pos, in_a = pr["q"], pr["kk"], pr["b"], pr["pos"], pr["in_a"]

        def pair_step(delta, carry):
            dq, dk, dv = carry
            sh, back = _hg_shift(delta, rev), _hg_unshift(delta, rev)
            kd, bd, vd = pltpu.roll(kk, sh, axis=1), pltpu.roll(b, sh, axis=1), pltpu.roll(v, sh, axis=1)
            valid = (pos <= REC_CHUNK - 1 - delta) if rev else (pos >= delta)
            e = jnp.where(valid, jnp.exp(jnp.where(valid, b - bd, 0.0)), 0.0)
            qe, e_k = q * e, kd * e
            a = jnp.sum(_heads(qe * kd), axis=1, keepdims=True)
            da = jnp.sum(_heads(do * vd), axis=1, keepdims=True)
            dq = dq + (da * _heads(e_k)).reshape(REC_WIDTH, TB)
            dk = dk + pltpu.roll((da * _heads(qe)).reshape(REC_WIDTH, TB), back, axis=1)
            dv = dv + pltpu.roll((a * _heads(do)).reshape(REC_WIDTH, TB), back, axis=1)
            return dq, dk, dv

        zero = jnp.zeros((REC_WIDTH, TB), F32)
        dq, dk, dv = lax.fori_loop(0, REC_CHUNK, pair_step, (zero, zero, zero))

        bd_mask = _block_diag_mask()
        dob, vb = do.astype(BF16), v.astype(BF16)
        eb = jnp.exp(b)
        const = zero
        order = ((0, in_a, pr["end_a"]), (1, ~in_a, pr["end_b"]))
        if not rev:
            order = order[::-1]
        for slot, msk, bend in order:
            h0 = hs_ref[slot]
            dh1 = dh_ref[...]
            dh1b = dh1.astype(BF16)
            dq = dq + eb * jnp.dot(h0.astype(BF16), jnp.where(msk, do, 0.0).astype(BF16), preferred_element_type=F32)
            dv = dv + lax.dot_general(dh1b, jnp.where(msk, pr["kh"], 0.0).astype(BF16), tn, preferred_element_type=F32)
            dk_int = pr["ekb"] * jnp.dot(dh1b, jnp.where(msk, v, 0.0).astype(BF16), preferred_element_type=F32)
            dk = dk + dk_int
            ebend = jnp.exp(bend)
            c = (jnp.sum(kk * dk_int, axis=1, keepdims=True)
                 + ebend * jnp.sum(h0 * dh1, axis=1, keepdims=True))
            const = const + jnp.where(msk, c, 0.0)
            upd = lax.dot_general(jnp.where(msk, pr["qh"], 0.0).astype(BF16), dob, nt, preferred_element_type=F32)
            dh_ref[...] = ebend * dh1 + upd * bd_mask

        dg = _chunk_scan(q * dq - kk * dk, pos, not rev) + const
        sig, sigm, f = pr["sig"], pr["sigm"], pr["f"]
        live = f > F_TINY
        inv_f = 1.0 / jnp.maximum(f, F_TINY)
        one_lb = 1.0 - lbv
        dz = sig * sigm * one_lb * (jnp.where(live, dg * inv_f, 0.0) - dk)
        dlb_ref[...] += jnp.sum(sigm * (jnp.where(live, dg * inv_f, 0.0) - dk), axis=1, keepdims=True)
        dqr = dq * _silu_grad(qraw)
        if has_prev:
            dqr = dqr + pq_ref[...]
            dv = dv + pv_ref[...]
        dq_ref[...] = dqr
        dz_ref[...] = dz
        dv_ref[...] = dv

    row_blk = lambda r: pl.BlockSpec((REC_WIDTH, TB), lambda i: (r, tmap(i)))
    blk = pl.BlockSpec((REC_WIDTH, TB), lambda i: (0, tmap(i)))
    col = pl.BlockSpec((REC_WIDTH, 1), lambda i: (0, 0))
    in_specs = [row_blk(0), row_blk(zrow), row_blk(3), col, blk,
                pl.BlockSpec((2, REC_WIDTH, REC_WIDTH), lambda i: (tmap(i), 0, 0))]
    ins = [projt, projt, projt, lb, dot, hs]
    if has_prev:
        in_specs += [blk, blk]
        ins += list(prev)
    t_shape = jax.ShapeDtypeStruct((REC_WIDTH, s), F32)
    return pl.pallas_call(
        body, name=name, grid=(nblk,), in_specs=in_specs, out_specs=(blk, blk, blk, col),
        out_shape=(t_shape, t_shape, t_shape, jax.ShapeDtypeStruct((REC_WIDTH, 1), F32)),
        scratch_shapes=[pltpu.VMEM((REC_WIDTH, REC_WIDTH), F32)],
        compiler_params=_params(1))(*ins)


REC_OUT_COLS = 512


def _head_rms(o):
    o3 = o.reshape(N_HEADS, HEAD_DIM, o.shape[1])
    rstd = lax.rsqrt(jnp.mean(o3 * o3, axis=1, keepdims=True) + EPS)
    return o3 * rstd, rstd


def _hgrn_out(of, ob, projt, wn, name):
    s = of.shape[1]

    def body(of_ref, ob_ref, g_ref, wn_ref, o_ref):
        on, _ = _head_rms(of_ref[...] + ob_ref[...])
        g = g_ref[...]
        y = on.reshape(REC_WIDTH, REC_OUT_COLS) * wn_ref[...] * (g * _sigmoid(g))
        o_ref[...] = y.T.astype(BF16)

    blk = pl.BlockSpec((REC_WIDTH, REC_OUT_COLS), lambda i: (0, i))
    return pl.pallas_call(
        body, name=name, grid=(s // REC_OUT_COLS,),
        in_specs=[blk, blk, pl.BlockSpec((REC_WIDTH, REC_OUT_COLS), lambda i: (4, i)),
                  pl.BlockSpec((REC_WIDTH, 1), lambda i: (0, 0))],
        out_specs=pl.BlockSpec((REC_OUT_COLS, REC_WIDTH), lambda i: (i, 0)),
        out_shape=jax.ShapeDtypeStruct((s, REC_WIDTH), BF16), compiler_params=_params(1))(of, ob, projt, wn)


def _hgrn_out_bwd(drec, of, ob, projt, wn, name):
    s = of.shape[1]

    def body(d_ref, of_ref, ob_ref, g_ref, wn_ref, do_ref, dg_ref, dwn_ref):
        @pl.when(pl.program_id(0) == 0)
        def _():
            dwn_ref[...] = jnp.zeros_like(dwn_ref)

        dy = d_ref[...].T
        on3, rstd = _head_rms(of_ref[...] + ob_ref[...])
        on = on3.reshape(REC_WIDTH, REC_OUT_COLS)
        g, wnv = g_ref[...], wn_ref[...]
        dg_ref[...] = dy * on * wnv * _silu_grad(g)
        d_onw = dy * (g * _sigmoid(g))
        dwn_ref[...] += jnp.sum(d_onw * on, axis=1, keepdims=True)
        d_on3 = (d_onw * wnv).reshape(N_HEADS, HEAD_DIM, REC_OUT_COLS)
        do3 = rstd * (d_on3 - on3 * jnp.mean(d_on3 * on3, axis=1, keepdims=True))
        do_ref[...] = do3.reshape(REC_WIDTH, REC_OUT_COLS)

    blk = pl.BlockSpec((REC_WIDTH, REC_OUT_COLS), lambda i: (0, i))
    col = pl.BlockSpec((REC_WIDTH, 1), lambda i: (0, 0))
    t_shape = jax.ShapeDtypeStruct((REC_WIDTH, s), F32)
    return pl.pallas_call(
        body, name=name, grid=(s // REC_OUT_COLS,),
        in_specs=[pl.BlockSpec((REC_OUT_COLS, REC_WIDTH), lambda i: (i, 0)), blk, blk,
                  pl.BlockSpec((REC_WIDTH, REC_OUT_COLS), lambda i: (4, i)), col],
        out_specs=(blk, blk, col),
        out_shape=(t_shape, t_shape, jax.ShapeDtypeStruct((REC_WIDTH, 1), F32)),
        compiler_params=_params(1))(drec, of, ob, projt, wn)


def _lower_bounds(gamma, name):
    def body(g_ref, lb_ref, p_ref):
        g0, g1 = g_ref[0:1, :], g_ref[1:2, :]
        m = jnp.maximum(g0, g1)
        e0, e1 = jnp.exp(g0 - m), jnp.exp(g1 - m)
        p0, p1 = e0 / (e0 + e1), e1 / (e0 + e1)
        lb_ref[...] = (p0 + p1) - p0
        p_ref[0:1, :] = p0
        p_ref[1:2, :] = p1

    n = gamma.shape[1]
    return pl.pallas_call(body, name=name,
                          out_shape=(jax.ShapeDtypeStruct((1, n), F32), jax.ShapeDtypeStruct((2, n), F32)))(gamma)


def _lower_bounds_bwd(dlb1, p, name):
    def body(d_ref, p_ref, o_ref):
        p0, p1, d = p_ref[0:1, :], p_ref[1:2, :], d_ref[...]
        inner = p1 * d
        o_ref[0:1, :] = p0 * (0.0 - inner)
        o_ref[1:2, :] = p1 * (d - inner)

    return pl.pallas_call(body, name=name, out_shape=jax.ShapeDtypeStruct(p.shape, F32))(dlb1, p)


def _split_layer_weights(w_in, w_out, w_up, w_down):
    return dict(conv=w_in[:, G_CONV[0]:G_CONV[1]], qkv=w_in[:, G_QKV[0]:G_QKV[1]],
                rec_t=w_in[:, G_REC[0]:].T, nat=w_in[:, :G_REC[0]],
                out=w_out, out_a=w_out[:CONV_CH], out_b=w_out[CONV_CH:CONV_CH + ATT_WIDTH],
                out_c=w_out[CONV_CH + ATT_WIDTH:], up=w_up, down=w_down)


def _col(v):
    return v.reshape(-1, 1)


def _sequence_step(x, tgt, mods, lbs, small, big, final_w):
    saved = []
    xin = x
    h1 = _resid_norm_mod(x, None, None, small[0]["norm1_w"], mods[0][1:2], mods[0][0:1], "norm1_l0")
    for l in range(DEPTH):
        sm, w, md = small[l], big[l], mods[l]
        pa = _matmul(h1, w["conv"], "nn", F32, f"proj_conv_l{l}")
        qkv = _matmul(h1, w["qkv"], "nn", BF16, f"proj_qkv_l{l}")
        projt = _matmul(w["rec_t"], h1, "nt", F32, f"proj_rec_l{l}")
        a_out, cv = _conv_mixer(pa, sm["conv_a_w"], sm["conv_a_b"], sm["ln_a_w"], sm["ln_a_b"], f"conv_mixer_l{l}")
        outs, lses = zip(*[_attn_branch(qkv, d, f"attn_d{d}_l{l}") for d in DILATIONS])
        att, att32, lse = _attn_combine(outs, lses, f"attn_combine_l{l}")
        lb_f, lb_b = _col(lbs[l][0]), _col(lbs[l][1])
        of, hsf = _hgrn_scan(projt, lb_f, False, f"hgrn_fwd_l{l}")
        ob, hsb = _hgrn_scan(projt, lb_b, True, f"hgrn_rev_l{l}")
        wn = _col(sm["rec_norm_w"])
        rec = _hgrn_out(of, ob, projt, wn, f"hgrn_out_l{l}")
        mixed = jnp.concatenate([a_out, att, rec], axis=1)
        r1 = _matmul(mixed, w["out"], "nn", F32, f"out_proj_l{l}")
        xmid, h2 = _resid_norm_mod(xin, r1, md[2:3], sm["norm2_w"], md[4:5], md[3:4], f"norm2_l{l}")
        u = _matmul(h2, w["up"], "nn", F32, f"ffn_up_l{l}")
        act = _ffn_act(u, sm["conv_f_w"], f"ffn_act_l{l}")
        r2 = _matmul(act, w["down"], "nn", F32, f"ffn_down_l{l}")
        saved.append(dict(xin=xin, h1=h1, pa=pa, qkv=qkv, projt=projt, cv=cv, att32=att32, lse=lse, of=of, ob=ob,
                          hsf=hsf, hsb=hsb, lb_f=lb_f, lb_b=lb_b, wn=wn, mixed=mixed, r1=r1, xmid=xmid, h2=h2,
                          u=u, act=act, r2=r2))
        if l + 1 < DEPTH:
            nxt = small[l + 1]
            xin, h1 = _resid_norm_mod(xmid, r2, md[5:6], nxt["norm1_w"], mods[l + 1][1:2], mods[l + 1][0:1],
                                      f"norm1_l{l + 1}")
    top = saved[-1]
    loss, dx, dr2, dg2, dfw = _final_loss(top["xmid"], top["r2"], mods[-1][5:6], final_w, tgt, "final_loss")

    grads = [None] * DEPTH
    for l in reversed(range(DEPTH)):
        sm, w, md, sv = small[l], big[l], mods[l], saved[l]
        dact = _matmul(dr2, w["down"], "nt", F32, f"d_act_l{l}")
        g_down = _matmul(sv["act"], dr2, "tn", F32, f"dw_down_l{l}")
        dug, duv, dwg, dwv = _ffn_act_bwd(sv["u"], dact, sm["conv_f_w"], f"ffn_act_bwd_l{l}")
        du = jnp.concatenate([dug, duv], axis=1)
        dh2 = _matmul(du, w["up"], "nt", F32, f"d_h2_l{l}")
        g_up = _matmul(sv["h2"], du, "tn", F32, f"dw_up_l{l}")
        dxmid, dr1, dsh2, dsc2, dnw2, dg1 = _norm_bwd(sv["xmid"], [dh2], dx, sm["norm2_w"], md[4:5], md[2:3], sv["r1"],
                                                     f"norm2_bwd_l{l}")
        dmix_a = _matmul(dr1, w["out_a"], "nt", F32, f"d_mix_a_l{l}")
        dmix_b = _matmul(dr1, w["out_b"], "nt", F32, f"d_mix_b_l{l}")
        dmix_c = _matmul(dr1, w["out_c"], "nt", F32, f"d_mix_c_l{l}")
        g_out = _matmul(sv["mixed"], dr1, "tn", F32, f"dw_out_l{l}")
        dc, dlnw, dlnb, dcb = _conv_mixer_bwd_ln(sv["cv"], dmix_a, sm["ln_a_w"], sm["ln_a_b"], f"conv_mixer_bwd_ln_l{l}")
        dpa, dcw = _conv_mixer_bwd_conv(sv["pa"], dc, sm["conv_a_w"], f"conv_mixer_bwd_conv_l{l}")
        delta, dobf = _attn_delta(dmix_b, sv["att32"], f"attn_delta_l{l}")
        dqkv = None
        for d in DILATIONS:
            dqkv = _attn_branch_bwd(sv["qkv"], dobf, sv["lse"], delta, dqkv, d, f"attn_bwd_d{d}_l{l}")
        dot, dgt, dwn = _hgrn_out_bwd(dmix_c, sv["of"], sv["ob"], sv["projt"], sv["wn"], f"hgrn_out_bwd_l{l}")
        dqf, dzf, dvf, dlbf = _hgrn_scan_bwd(sv["projt"], sv["lb_f"], dot, sv["hsf"], None, False, f"hgrn_fwd_bwd_l{l}")
        dqt, dzb, dvt, dlbb = _hgrn_scan_bwd(sv["projt"], sv["lb_b"], dot, sv["hsb"], (dqf, dvf), True,
                                             f"hgrn_rev_bwd_l{l}")
        dprojt = jnp.concatenate([dqt, dzf, dzb, dvt, dgt], axis=0).astype(BF16)
        dnat = jnp.concatenate([dpa] + [t.astype(BF16) for t in dqkv], axis=1)
        dh1_a = _matmul(dnat, w["nat"], "nt", F32, f"d_h1_nat_l{l}")
        dh1_b = _matmul(dprojt, w["rec_t"], "tn", F32, f"d_h1_rec_l{l}")
        g_in_nat = _matmul(sv["h1"], dnat, "tn", F32, f"dw_in_nat_l{l}")
        g_in_rec_t = _matmul(dprojt, sv["h1"], "nn", F32, f"dw_in_rec_l{l}")
        g_in = jnp.concatenate([g_in_nat, g_in_rec_t.T], axis=1)
        if l > 0:
            below = saved[l - 1]
            dx, dr2, dsh1, dsc1, dnw1, dg2_below = _norm_bwd(sv["xin"], [dh1_a, dh1_b], dxmid, sm["norm1_w"], md[1:2],
                                                            mods[l - 1][5:6], below["r2"], f"norm1_bwd_l{l}")
        else:
            dx, dsh1, dsc1, dnw1 = _norm_bwd(sv["xin"], [dh1_a, dh1_b], dxmid, sm["norm1_w"], md[1:2], None, None,
                                             f"norm1_bwd_l{l}")
        grads[l] = dict(w_in=g_in, w_out=g_out, w_up=g_up, w_down=g_down,
                        mod=[dsh1, dsc1, dg1, dsh2, dsc2, dg2], norm1_w=dnw1, conv_a_w=dcw[:CONV_WIDTH], conv_a_b=dcb,
                        ln_a_w=dlnw, ln_a_b=dlnb, lb=jnp.concatenate([dlbf.reshape(1, -1), dlbb.reshape(1, -1)], axis=0),
                        rec_norm_w=dwn.reshape(1, -1), norm2_w=dnw2,
                        conv_f_w=jnp.concatenate([dwg[:3], dwv[:3]], axis=1))
        if l > 0:
            dg2 = dg2_below
    return loss[0, 0], dx, grads, dfw


def _adamw_math(w, g, m, v):
    m = ADAM_B1 * m + (1.0 - ADAM_B1) * g
    v = ADAM_B2 * v + (1.0 - ADAM_B2) * (g * g)
    m_hat = m / (1.0 - ADAM_B1 ** ADAM_STEP)
    v_hat = v / (1.0 - ADAM_B2 ** ADAM_STEP)
    delta = -ADAM_LR * (m_hat / (jnp.sqrt(v_hat) + ADAM_EPS) + ADAM_WD * w)
    return delta, m, v


def _row_tile(rows, cols, max_elems=384 * 1024):
    best = None
    for t in range(8, rows + 1, 8):
        if rows % t == 0 and t * cols <= max_elems:
            best = t
    return best or rows


def _adamw(w, g, m, v, name):
    nl, r, c = w.shape
    tr = _row_tile(r, c)

    def body(w_ref, g_ref, m_ref, v_ref, d_ref, m2_ref, v2_ref):
        d_ref[...], m2_ref[...], v2_ref[...] = _adamw_math(w_ref[...], g_ref[...], m_ref[...], v_ref[...])

    blk = pl.BlockSpec((None, tr, c), lambda l, i: (l, i, 0))
    shape = jax.ShapeDtypeStruct((nl, r, c), F32)
    return pl.pallas_call(body, name=name, grid=(nl, r // tr), in_specs=[blk] * 4, out_specs=(blk, blk, blk),
                          out_shape=(shape, shape, shape), compiler_params=_params(2))(w, g, m, v)


ADA_SHARD = N_MOD * D_MODEL // 4
ADA_COLS = 512
ADA_ROWS = 256
HIGHEST = lax.Precision.HIGHEST


def _ada_mod(c_all, w_ada, b_sh, name):
    def body(c_ref, w_ref, b_ref, o_ref):
        cv = c_ref[...]
        o_ref[...] = jnp.dot(cv * _sigmoid(cv), w_ref[...], precision=HIGHEST, preferred_element_type=F32) + b_ref[...]

    return pl.pallas_call(
        body, name=name, grid=(DEPTH, ADA_SHARD // ADA_COLS),
        in_specs=[pl.BlockSpec((8, D_MODEL), lambda l, j: (0, 0)),
                  pl.BlockSpec((None, D_MODEL, ADA_COLS), lambda l, j: (l, 0, j)),
                  pl.BlockSpec((None, 1, ADA_COLS), lambda l, j: (l, 0, j))],
        out_specs=pl.BlockSpec((None, 8, ADA_COLS), lambda l, j: (l, 0, j)),
        out_shape=jax.ShapeDtypeStruct((DEPTH, 8, ADA_SHARD), F32), compiler_params=_params(2))(c_all, w_ada, b_sh)


def _ada_update(c_all, dmod_sh, w, m, v, name):
    def body(c_ref, d_ref, w_ref, m_ref, v_ref, g_ref, dl_ref, m2_ref, v2_ref):
        cv = c_ref[...]
        g = lax.dot_general(cv * _sigmoid(cv), d_ref[...], (((0,), (0,)), ((), ())), precision=HIGHEST,
                            preferred_element_type=F32)
        g_ref[...] = g
        dl_ref[...], m2_ref[...], v2_ref[...] = _adamw_math(w_ref[...], g, m_ref[...], v_ref[...])

    blk = pl.BlockSpec((None, ADA_ROWS, ADA_SHARD), lambda l, i: (l, i, 0))
    shape = jax.ShapeDtypeStruct((DEPTH, D_MODEL, ADA_SHARD), F32)
    return pl.pallas_call(
        body, name=name, grid=(DEPTH, D_MODEL // ADA_ROWS),
        in_specs=[pl.BlockSpec((8, ADA_ROWS), lambda l, i: (0, i)),
                  pl.BlockSpec((None, 8, ADA_SHARD), lambda l, i: (l, 0, 0)), blk, blk, blk],
        out_specs=(blk,) * 4, out_shape=(shape,) * 4, compiler_params=_params(2))(c_all, dmod_sh, w, m, v)


def _sum_devices(packs, name):
    def body(p_ref, o_ref):
        acc = p_ref[0]
        for dev in range(1, 8):
            acc = acc + p_ref[dev]
        o_ref[...] = acc

    return pl.pallas_call(body, name=name, out_shape=jax.ShapeDtypeStruct(packs.shape[1:], F32))(packs)


def _add_pieces(a, b, name):
    nl, nj, pr, pc = a.shape

    def body(a_ref, b_ref, o_ref):
        o_ref[...] = a_ref[...] + b_ref[...]

    blk = pl.BlockSpec((None, None, pr, pc), lambda l, j: (l, j, 0, 0))
    return pl.pallas_call(body, name=name, grid=(nl, nj), in_specs=[blk, blk], out_specs=blk,
                          out_shape=jax.ShapeDtypeStruct(a.shape, F32), compiler_params=_params(2))(a, b)


def _sum_slots(b, name):
    nl, nj, pr, pc = b.shape

    def body(s0, s1, s2, s3, o_ref):
        o_ref[...] = ((s0[...] + s1[...]) + s2[...]) + s3[...]

    specs = [pl.BlockSpec((None, None, pr, pc), (lambda l, slot=slot: (l, slot, 0, 0))) for slot in range(nj)]
    return pl.pallas_call(body, name=name, grid=(nl,), in_specs=specs,
                          out_specs=pl.BlockSpec((None, pr, pc), lambda l: (l, 0, 0)),
                          out_shape=jax.ShapeDtypeStruct((nl, pr, pc), F32), compiler_params=_params(1))(b, b, b, b)


def _mesh_pos():
    return lax.axis_index("x"), lax.axis_index("y"), lax.axis_index("c")


def _flip(v, bit):
    return 1 - v if bit else v


def _allgather_devices(x, name):
    m_per, n = x.shape

    def body(x_ref, out_ref, send_sems, recv_sems, local_sem):
        ix, iy, ic = _mesh_pos()
        me, sibling = (ix, iy, ic), (ix, iy, 1 - ic)
        chips = [(1 - ix, iy), (ix, 1 - iy), (1 - ix, 1 - iy)]

        def rows(px, py, pc):
            return out_ref.at[pl.ds((4 * px + 2 * py + pc) * m_per, m_per), :]

        def copy(k, block, to, src=None):
            return pltpu.make_async_remote_copy(
                src_ref=rows(*block) if src is None else src, dst_ref=rows(*block),
                send_sem=send_sems.at[k], recv_sem=recv_sems.at[k], device_id=to, device_id_type=MESH)

        mine = pltpu.make_async_copy(x_ref, rows(*me), local_sem)
        mine.start()
        first = [copy(0, me, sibling, src=x_ref)]
        first += [copy(1 + j, me, (*chip, ic), src=x_ref) for j, chip in enumerate(chips)]
        for cp in first:
            cp.start()
        passed = [copy(4 + j, (*chip, ic), sibling) for j, chip in enumerate(chips)]
        for j, chip in enumerate(chips):
            copy(1 + j, (*chip, ic), me).wait_recv()
            passed[j].start()
        copy(0, sibling, me).wait_recv()
        for j, chip in enumerate(chips):
            copy(4 + j, (*chip, 1 - ic), me).wait_recv()
        for cp in first + passed:
            cp.wait_send()
        mine.wait()

    return pl.pallas_call(
        body, name=name, out_shape=jax.ShapeDtypeStruct((8 * m_per, n), x.dtype),
        in_specs=[pl.BlockSpec(memory_space=pltpu.VMEM)], out_specs=pl.BlockSpec(memory_space=pltpu.VMEM),
        scratch_shapes=[pltpu.SemaphoreType.DMA((7,)), pltpu.SemaphoreType.DMA((7,)), pltpu.SemaphoreType.DMA],
    )(x)


def _gather_chips(shards, name):
    n = len(shards)

    def body(*refs):
        ins, outs = refs[:n], refs[n:2 * n]
        send_sems, recv_sems, local_sems = refs[2 * n:]
        ix, iy, ic = _mesh_pos()
        me = 2 * ix + iy
        local = [pltpu.make_async_copy(ins[a], outs[a].at[me], local_sems.at[a]) for a in range(n)]
        for cp in local:
            cp.start()
        remote = []
        for a in range(n):
            for k in (1, 2, 3):
                px, py = _flip(ix, k & 2), _flip(iy, k & 1)
                sems = dict(send_sem=send_sems.at[3 * a + k - 1], recv_sem=recv_sems.at[3 * a + k - 1],
                            device_id=(px, py, ic), device_id_type=MESH)
                out_cp = pltpu.make_async_remote_copy(src_ref=ins[a], dst_ref=outs[a].at[me], **sems)
                in_cp = pltpu.make_async_remote_copy(src_ref=ins[a], dst_ref=outs[a].at[2 * px + py], **sems)
                out_cp.start()
                remote.append((out_cp, in_cp))
        for out_cp, in_cp in remote:
            out_cp.wait_send()
            in_cp.wait_recv()
        for cp in local:
            cp.wait()

    return pl.pallas_call(
        body, name=name, in_specs=[ANY] * n, out_specs=tuple([ANY] * n),
        out_shape=tuple(jax.ShapeDtypeStruct((4,) + t.shape, t.dtype) for t in shards),
        scratch_shapes=[pltpu.SemaphoreType.DMA((3 * n,)), pltpu.SemaphoreType.DMA((3 * n,)),
                        pltpu.SemaphoreType.DMA((n,))],
    )(*shards)


BIG_KINDS = (("w_in", "col", D_MODEL, IN_COLS), ("w_out", "row", D_MODEL, D_MODEL),
             ("w_up", "col", D_MODEL, 2 * D_FF), ("w_down", "row", D_FF, D_MODEL))


def _piece_shape(how, r, c):
    return (r // 2, c // 4) if how == "col" else (r // 8, c)


def _aligned(start, multiple):
    return start if isinstance(start, int) else pl.multiple_of(start, multiple)


def _piece(ref, how, r, c, chip, half):
    if how == "col":
        return ref.at[pl.ds(_aligned(half * (r // 2), 8), r // 2), pl.ds(_aligned(chip * (c // 4), LANES), c // 4)]
    n = r // 4
    return ref.at[pl.ds(_aligned(chip * n + half * (n // 2), 8), n // 2), :]


def _rs_pair_exchange(grads, name):
    nk = len(BIG_KINDS)
    flat = [grads[ki][l] for ki in range(nk) for l in range(DEPTH)]
    per = DEPTH * 4

    def body(*refs):
        g = refs[:nk * DEPTH]
        mine, land = refs[nk * DEPTH:nk * DEPTH + nk], refs[nk * DEPTH + nk:nk * DEPTH + 2 * nk]
        send_sems, recv_sems, local_sems = refs[nk * DEPTH + 2 * nk:]
        ix, iy, ic = _mesh_pos()
        sibling = (ix, iy, 1 - ic)
        waits = []
        for ki, (_, how, r, c) in enumerate(BIG_KINDS):
            for l in range(DEPTH):
                for j in range(4):
                    sem = ki * per + l * 4 + j
                    src = g[ki * DEPTH + l]
                    loc = pltpu.make_async_copy(_piece(src, how, r, c, j, ic), mine[ki].at[l, j], local_sems.at[sem])
                    rem = pltpu.make_async_remote_copy(
                        src_ref=_piece(src, how, r, c, j, 1 - ic), dst_ref=land[ki].at[l, j],
                        send_sem=send_sems.at[sem], recv_sem=recv_sems.at[sem], device_id=sibling, device_id_type=MESH)
                    loc.start()
                    rem.start()
                    waits.append((loc, rem))
        for loc, rem in waits:
            loc.wait()
            rem.wait_send()
            rem.wait_recv()

    shapes = [jax.ShapeDtypeStruct((DEPTH, 4) + _piece_shape(how, r, c), F32) for _, how, r, c in BIG_KINDS]
    outs = pl.pallas_call(
        body, name=name, in_specs=[ANY] * len(flat), out_specs=tuple([ANY] * (2 * nk)), out_shape=tuple(shapes + shapes),
        scratch_shapes=[pltpu.SemaphoreType.DMA((nk * per,))] * 3,
    )(*flat)
    return [(outs[ki], outs[nk + ki]) for ki in range(nk)]


def _rs_chip_exchange(pair_sums, name):
    nk = len(pair_sums)

    def body(*refs):
        src, dst = refs[:nk], refs[nk:2 * nk]
        send_sems, recv_sems, local_sems = refs[2 * nk:]
        ix, iy, ic = _mesh_pos()
        me = 2 * ix + iy
        waits = []
        for ki in range(nk):
            for l in range(DEPTH):
                loc = pltpu.make_async_copy(src[ki].at[l, me], dst[ki].at[l, 0], local_sems.at[ki * DEPTH + l])
                loc.start()
                waits.append((loc, None))
                for k in (1, 2, 3):
                    px, py = _flip(ix, k & 2), _flip(iy, k & 1)
                    sem = (ki * DEPTH + l) * 3 + k - 1
                    rem = pltpu.make_async_remote_copy(
                        src_ref=src[ki].at[l, 2 * px + py], dst_ref=dst[ki].at[l, k],
                        send_sem=send_sems.at[sem], recv_sem=recv_sems.at[sem], device_id=(px, py, ic), device_id_type=MESH)
                    rem.start()
                    waits.append((None, rem))
        for loc, rem in waits:
            if loc is not None:
                loc.wait()
            else:
                rem.wait_send()
                rem.wait_recv()

    return pl.pallas_call(
        body, name=name, in_specs=[ANY] * nk, out_specs=tuple([ANY] * nk),
        out_shape=tuple(jax.ShapeDtypeStruct(t.shape, F32) for t in pair_sums),
        scratch_shapes=[pltpu.SemaphoreType.DMA((nk * DEPTH * 3,)), pltpu.SemaphoreType.DMA((nk * DEPTH * 3,)),
                        pltpu.SemaphoreType.DMA((nk * DEPTH,))],
    )(*pair_sums)


def _rs_pair_share(halves, name):
    nk = len(halves)

    def body(*refs):
        src, dst = refs[:nk], refs[nk:2 * nk]
        send_sems, recv_sems, local_sems = refs[2 * nk:]
        ix, iy, ic = _mesh_pos()
        sibling = (ix, iy, 1 - ic)
        waits = []
        for ki in range(nk):
            for l in range(DEPTH):
                sem = ki * DEPTH + l
                loc = pltpu.make_async_copy(src[ki].at[l], dst[ki].at[l, ic], local_sems.at[sem])
                out_cp = pltpu.make_async_remote_copy(
                    src_ref=src[ki].at[l], dst_ref=dst[ki].at[l, ic], send_sem=send_sems.at[sem],
                    recv_sem=recv_sems.at[sem], device_id=sibling, device_id_type=MESH)
                in_cp = pltpu.make_async_remote_copy(
                    src_ref=src[ki].at[l], dst_ref=dst[ki].at[l, 1 - ic], send_sem=send_sems.at[sem],
                    recv_sem=recv_sems.at[sem], device_id=sibling, device_id_type=MESH)
                loc.start()
                out_cp.start()
                waits.append((loc, out_cp, in_cp))
        for loc, out_cp, in_cp in waits:
            loc.wait()
            out_cp.wait_send()
            in_cp.wait_recv()

    return pl.pallas_call(
        body, name=name, in_specs=[ANY] * nk, out_specs=tuple([ANY] * nk),
        out_shape=tuple(jax.ShapeDtypeStruct((DEPTH, 2) + t.shape[1:], F32) for t in halves),
        scratch_shapes=[pltpu.SemaphoreType.DMA((nk * DEPTH,))] * 3,
    )(*halves)


def _reduce_scatter_big(grads):
    pairs = _rs_pair_exchange(grads, "rs_pair_exchange")
    pair_sums = [_add_pieces(mine, theirs, f"rs_pair_sum_{BIG_KINDS[ki][0]}") for ki, (mine, theirs) in enumerate(pairs)]
    slots = _rs_chip_exchange(pair_sums, "rs_chip_exchange")
    halves = [_sum_slots(t, f"rs_chip_sum_{BIG_KINDS[ki][0]}") for ki, t in enumerate(slots)]
    both = _rs_pair_share(halves, "rs_pair_share")
    out = []
    for (_, how, r, c), t in zip(BIG_KINDS, both):
        shard = (r, c // 4) if how == "col" else (r // 4, c)
        out.append(t.reshape((DEPTH,) + shard))
    return out


WEIGHT_NAMES = ("w_ada", "b_ada", "norm1_w", "w_in", "conv_a_w", "conv_a_b", "ln_a_w", "ln_a_b", "lb_gamma",
                "rec_norm_w", "w_out", "norm2_w", "w_up", "conv_f_w", "w_down", "final_norm_w")
SMALL_PARAMS = (("b_ada", (DEPTH, N_MOD * D_MODEL), None), ("norm1_w", (DEPTH, D_MODEL), None),
                ("conv_a_w", (DEPTH, CONV_WIDTH, CONV_CH), 2), ("conv_a_b", (DEPTH, CONV_CH), None),
                ("ln_a_w", (DEPTH, CONV_CH), None), ("ln_a_b", (DEPTH, CONV_CH), None),
                ("lb_gamma", (DEPTH, 2, REC_WIDTH), 2), ("rec_norm_w", (DEPTH, REC_WIDTH), None),
                ("norm2_w", (DEPTH, D_MODEL), None), ("conv_f_w", (DEPTH, 3, 2 * D_FF), 2),
                ("final_norm_w", (D_MODEL,), None))


def _pack_rows(parts):
    flat = jnp.concatenate([p.reshape(-1) for p in parts])
    total = flat.shape[0]
    padded = -(-total // (8 * LANES)) * (8 * LANES)
    return jnp.pad(flat, (0, padded - total)).reshape(padded // LANES, LANES)


def _unpack(flat, shapes):
    out, off = [], 0
    for shp in shapes:
        size = int(np.prod(shp))
        out.append(flat[off:off + size].reshape(shp))
        off += size
    return out


def _unstack_chips(t, axis):
    return jnp.concatenate([t[j] for j in range(4)], axis=axis)


def kernel(x, c, w_ada, b_ada, norm1_w, w_in, conv_a_w, conv_a_b, ln_a_w, ln_a_b, lb_gamma, rec_norm_w, w_out, norm2_w, w_up, conv_f_w, w_down, final_norm_w, loss_target, m_w_ada, m_b_ada, m_norm1_w, m_w_in, m_conv_a_w, m_conv_a_b, m_ln_a_w, m_ln_a_b, m_lb_gamma, m_rec_norm_w, m_w_out, m_norm2_w, m_w_up, m_conv_f_w, m_w_down, m_final_norm_w, v_w_ada, v_b_ada, v_norm1_w, v_w_in, v_conv_a_w, v_conv_a_b, v_ln_a_w, v_ln_a_b, v_lb_gamma, v_rec_norm_w, v_w_out, v_norm2_w, v_w_up, v_conv_f_w, v_w_down, v_final_norm_w):
    params = dict(zip(WEIGHT_NAMES, (w_ada, b_ada, norm1_w, w_in, conv_a_w, conv_a_b, ln_a_w, ln_a_b, lb_gamma,
                                     rec_norm_w, w_out, norm2_w, w_up, conv_f_w, w_down, final_norm_w)))
    mom1 = dict(zip(WEIGHT_NAMES, (m_w_ada, m_b_ada, m_norm1_w, m_w_in, m_conv_a_w, m_conv_a_b, m_ln_a_w, m_ln_a_b,
                                   m_lb_gamma, m_rec_norm_w, m_w_out, m_norm2_w, m_w_up, m_conv_f_w, m_w_down,
                                   m_final_norm_w)))
    mom2 = dict(zip(WEIGHT_NAMES, (v_w_ada, v_b_ada, v_norm1_w, v_w_in, v_conv_a_w, v_conv_a_b, v_ln_a_w, v_ln_a_b,
                                   v_lb_gamma, v_rec_norm_w, v_w_out, v_norm2_w, v_w_up, v_conv_f_w, v_w_down,
                                   v_final_norm_w)))
    ix, iy, ic = _mesh_pos()
    chip = 2 * ix + iy
    dev = 2 * chip + ic

    c_all = _allgather_devices(c.reshape(8, LANES), "gather_cond").reshape(8, D_MODEL)
    b_sh = lax.dynamic_slice_in_dim(b_ada, chip * ADA_SHARD, ADA_SHARD, axis=1)
    mod_sh = _ada_mod(c_all, w_ada, b_sh.reshape(DEPTH, 1, ADA_SHARD), "ada_mod")
    gathered = _gather_chips([mod_sh, w_in.astype(BF16), w_out.astype(BF16), w_up.astype(BF16), w_down.astype(BF16),
                              conv_a_w, conv_f_w, lb_gamma], "gather_weights")
    mod_mine = lax.dynamic_index_in_dim(gathered[0], dev, axis=2, keepdims=False)
    mods = [jnp.concatenate([mod_mine[j, l] for j in range(4)]).reshape(N_MOD, D_MODEL) for l in range(DEPTH)]
    w_in_f, w_up_f = _unstack_chips(gathered[1], 2), _unstack_chips(gathered[3], 2)
    w_out_f, w_down_f = _unstack_chips(gathered[2], 1), _unstack_chips(gathered[4], 1)
    conv_a_w_f, conv_f_w_f, gamma_f = (_unstack_chips(gathered[k], 2) for k in (5, 6, 7))

    lb1, p_soft = _lower_bounds(gamma_f.reshape(DEPTH, 2 * REC_WIDTH), "lower_bounds")
    lbs = [jnp.zeros((2, REC_WIDTH), F32), lb1.reshape(2, REC_WIDTH)]
    small, big = [], []
    for l in range(DEPTH):
        small.append(dict(norm1_w=norm1_w[l][None], conv_a_w=conv_a_w_f[l], conv_a_b=conv_a_b[l][None],
                          ln_a_w=ln_a_w[l][None], ln_a_b=ln_a_b[l][None], rec_norm_w=rec_norm_w[l],
                          norm2_w=norm2_w[l][None], conv_f_w=conv_f_w_f[l]))
        big.append(_split_layer_weights(w_in_f[l], w_out_f[l], w_up_f[l], w_down_f[l]))

    loss, dx, grads, dfw = _sequence_step(x[0], loss_target[0], mods, lbs, small, big, final_norm_w[None])
    loss = lax.psum(loss, ("x", "y", "c"))

    dgamma = _lower_bounds_bwd(grads[1]["lb"].reshape(1, 2 * REC_WIDTH), p_soft, "lower_bounds_bwd")
    dmod = [jnp.concatenate(grads[l]["mod"], axis=1) for l in range(DEPTH)]
    stack = lambda key: jnp.stack([grads[l][key] for l in range(DEPTH)])
    local_small = dict(b_ada=jnp.concatenate(dmod, axis=0), norm1_w=stack("norm1_w"), conv_a_w=stack("conv_a_w"),
                       conv_a_b=stack("conv_a_b"), ln_a_w=stack("ln_a_w"), ln_a_b=stack("ln_a_b"), lb_gamma=dgamma,
                       rec_norm_w=stack("rec_norm_w"), norm2_w=stack("norm2_w"), conv_f_w=stack("conv_f_w"),
                       final_norm_w=dfw)
    pack = _pack_rows([local_small[name] for name, _, _ in SMALL_PARAMS])
    rows = pack.shape[0]
    packs = _allgather_devices(pack, "gather_small_grads").reshape(8, rows, LANES)
    summed = _sum_devices(packs, "sum_small_grads").reshape(-1)
    small_grads = dict(zip([n for n, _, _ in SMALL_PARAMS], _unpack(summed, [shp for _, shp, _ in SMALL_PARAMS])))

    dmod_all = packs.reshape(8, rows * LANES)[:, :DEPTH * N_MOD * D_MODEL].reshape(8, DEPTH, N_MOD * D_MODEL)
    dmod_sh = lax.dynamic_slice_in_dim(dmod_all, chip * ADA_SHARD, ADA_SHARD, axis=2).transpose(1, 0, 2)
    g_ada, d_ada, m_ada, v_ada = _ada_update(c_all, dmod_sh, w_ada, m_w_ada, v_w_ada, "ada_update")

    for name, shp, axis in SMALL_PARAMS:
        if axis is not None:
            width = shp[axis] // 4
            small_grads[name] = lax.dynamic_slice_in_dim(small_grads[name], chip * width, width, axis=axis)
    names = [n for n, _, _ in SMALL_PARAMS]
    packed = [_pack_rows([src[n] for n in names])[None] for src in (params, small_grads, mom1, mom2)]
    small_out = _adamw(*packed, "adamw_small")
    shapes = [params[n].shape for n in names]
    small_delta, small_m, small_v = (dict(zip(names, _unpack(t.reshape(-1), shapes))) for t in small_out)

    summed_big = _reduce_scatter_big([[grads[l][name] for l in range(DEPTH)] for name, _, _, _ in BIG_KINDS])
    grad, delta, new_m, new_v = dict(small_grads), small_delta, small_m, small_v
    grad["w_ada"], delta["w_ada"], new_m["w_ada"], new_v["w_ada"] = g_ada, d_ada, m_ada, v_ada
    for (name, _, _, _), g_sh in zip(BIG_KINDS, summed_big):
        grad[name] = g_sh
        delta[name], new_m[name], new_v[name] = _adamw(params[name], g_sh, mom1[name], mom2[name], f"adamw_{name}")

    return (loss, dx[None], *[grad[n] for n in WEIGHT_NAMES], *[delta[n] for n in WEIGHT_NAMES],
            *[new_m[n] for n in WEIGHT_NAMES], *[new_v[n] for n in WEIGHT_NAMES])
```

```python
import numpy as np
import jax
import jax.numpy as jnp
from jax import lax
from jax.experimental import pallas as pl
from jax.experimental.pallas import tpu as pltpu

F32 = jnp.float32
BF16 = jnp.bfloat16

D_MODEL = 1024
DEPTH = 2
HEAD_DIM = 64
CONV_CH = 256
CONV_WIDTH = 31
ATT_WIDTH = 384
N_HEADS = 6
DILATIONS = (1, 4, 16)
ATT_HALF = 64
ATT_BLOCK = 128
ALIBI_MAX_EXP = 8.0
MASK_VALUE = -1e30
REC_WIDTH = 384
REC_CHUNK = 64
F_TINY = 1e-30
D_FF = 2816
N_MOD = 6
EPS = 1e-6
G_CONV = (0, 512)
G_QKV = (512, 1664)
G_REC = (1664, 3584)
IN_COLS = 3584

ADAM_LR = 0.001
ADAM_B1 = 0.9
ADAM_B2 = 0.999
ADAM_EPS = 1e-08
ADAM_WD = 0.01
ADAM_STEP = 10

VMEM_LIMIT_BYTES = 56 * 1024 * 1024
LANES = 128
MESH = pl.DeviceIdType.MESH
ANY = pl.BlockSpec(memory_space=pl.ANY)


def _params(n_axes):
    return pltpu.CompilerParams(dimension_semantics=("arbitrary",) * n_axes,
                                vmem_limit_bytes=VMEM_LIMIT_BYTES)


def _tile(n, target):
    best = None
    for t in range(LANES, min(n, target) + 1, LANES):
        if n % t == 0:
            best = t
    return best or n


def _sigmoid(x):
    return jax.nn.sigmoid(x)


def _silu_grad(x):
    s = _sigmoid(x)
    return s * (1.0 + x * (1.0 - s))


MM_ACC_ELEMS = 1024 * 1024


def _matmul(a, b, mode, out_dtype, name, tm=1024, tn=1792, tk=1792):
    if mode == "nn":
        (m, k), (k2, n) = a.shape, b.shape
    elif mode == "nt":
        (m, k), (n, k2) = a.shape, b.shape
    else:
        (k, m), (k2, n) = a.shape, b.shape
    assert k == k2, (a.shape, b.shape, mode)
    tn, tk = _tile(n, tn), _tile(k, tk)
    tm = _tile(m, min(tm, MM_ACC_ELEMS // tn))
    nk = k // tk
    a_spec = (pl.BlockSpec((tk, tm), lambda i, j, kk: (kk, i)) if mode == "tn"
              else pl.BlockSpec((tm, tk), lambda i, j, kk: (i, kk)))
    b_spec = (pl.BlockSpec((tn, tk), lambda i, j, kk: (j, kk)) if mode == "nt"
              else pl.BlockSpec((tk, tn), lambda i, j, kk: (kk, j)))
    dims = {"nn": (((1,), (0,)), ((), ())), "nt": (((1,), (1,)), ((), ())),
            "tn": (((0,), (0,)), ((), ()))}[mode]

    def body(a_ref, b_ref, o_ref, *scratch):
        part = lax.dot_general(a_ref[...].astype(BF16), b_ref[...].astype(BF16), dims, preferred_element_type=F32)
        if nk == 1:
            o_ref[...] = part.astype(out_dtype)
            return
        acc_ref, = scratch
        kk = pl.program_id(2)

        @pl.when(kk == 0)
        def _():
            acc_ref[...] = part

        @pl.when(kk > 0)
        def _():
            acc_ref[...] += part

        @pl.when(kk == nk - 1)
        def _():
            o_ref[...] = acc_ref[...].astype(out_dtype)

    return pl.pallas_call(
        body, name=name, grid=(m // tm, n // tn, nk),
        in_specs=[a_spec, b_spec],
        out_specs=pl.BlockSpec((tm, tn), lambda i, j, kk: (i, j)),
        out_shape=jax.ShapeDtypeStruct((m, n), out_dtype),
        scratch_shapes=[pltpu.VMEM((tm, tn), F32)] if nk > 1 else [],
        compiler_params=pltpu.CompilerParams(dimension_semantics=("parallel", "parallel", "arbitrary"),
                                             vmem_limit_bytes=VMEM_LIMIT_BYTES),
    )(a, b)


NORM_ROWS = 256


def _row_spec(width, rows=NORM_ROWS):
    return pl.BlockSpec((rows, width), lambda i: (i, 0))


def _vec_spec(width):
    return pl.BlockSpec((1, width), lambda i: (0, 0))


def _resid_norm_mod(x, r, g, nw, sc, sh, name):
    s, d = x.shape
    has_r = r is not None

    def body(*refs):
        if has_r:
            x_ref, r_ref, g_ref, nw_ref, sc_ref, sh_ref, xn_ref, h_ref = refs
            xn = x_ref[...] + g_ref[...] * r_ref[...]
            xn_ref[...] = xn
        else:
            x_ref, nw_ref, sc_ref, sh_ref, h_ref = refs
            xn = x_ref[...]
        rstd = lax.rsqrt(jnp.mean(xn * xn, axis=-1, keepdims=True) + EPS)
        y = xn * rstd * nw_ref[...]
        h_ref[...] = (y * (1.0 + sc_ref[...]) + sh_ref[...]).astype(BF16)

    if has_r:
        ins, in_specs = (x, r, g, nw, sc, sh), [_row_spec(d), _row_spec(d)] + [_vec_spec(d)] * 4
        out_shape = (jax.ShapeDtypeStruct((s, d), F32), jax.ShapeDtypeStruct((s, d), BF16))
        out_specs = (_row_spec(d), _row_spec(d))
    else:
        ins, in_specs = (x, nw, sc, sh), [_row_spec(d)] + [_vec_spec(d)] * 3
        out_shape = jax.ShapeDtypeStruct((s, d), BF16)
        out_specs = _row_spec(d)
    return pl.pallas_call(body, name=name, grid=(s // NORM_ROWS,), in_specs=in_specs, out_specs=out_specs,
                          out_shape=out_shape, compiler_params=_params(1))(*ins)


def _final_loss(x, r, g, fw, tgt, name):
    s, d = x.shape

    def body(x_ref, r_ref, g_ref, fw_ref, t_ref, loss_ref, dx_ref, dr_ref, dg_ref, dfw_ref):
        @pl.when(pl.program_id(0) == 0)
        def _():
            loss_ref[...] = jnp.zeros_like(loss_ref)
            dg_ref[...] = jnp.zeros_like(dg_ref)
            dfw_ref[...] = jnp.zeros_like(dfw_ref)

        rr = r_ref[...]
        gg = g_ref[...]
        xn = x_ref[...] + gg * rr
        rstd = lax.rsqrt(jnp.mean(xn * xn, axis=-1, keepdims=True) + EPS)
        xh = xn * rstd
        fwv = fw_ref[...]
        e = xh * fwv - t_ref[...]
        loss_ref[...] += 0.5 * jnp.sum(jnp.mean(e * e, axis=-1, keepdims=True), axis=0, keepdims=True)
        dy = e * (1.0 / d)
        dfw_ref[...] += jnp.sum(dy * xh, axis=0, keepdims=True)
        dxh = dy * fwv
        dx = rstd * (dxh - xh * jnp.mean(dxh * xh, axis=-1, keepdims=True))
        dx_ref[...] = dx
        dr_ref[...] = (gg * dx).astype(BF16)
        dg_ref[...] += jnp.sum(dx * rr, axis=0, keepdims=True)

    return pl.pallas_call(
        body, name=name, grid=(s // NORM_ROWS,),
        in_specs=[_row_spec(d), _row_spec(d), _vec_spec(d), _vec_spec(d), _row_spec(d)],
        out_specs=(_vec_spec(LANES), _row_spec(d), _row_spec(d), _vec_spec(d), _vec_spec(d)),
        out_shape=(jax.ShapeDtypeStruct((1, LANES), F32), jax.ShapeDtypeStruct((s, d), F32),
                   jax.ShapeDtypeStruct((s, d), BF16), jax.ShapeDtypeStruct((1, d), F32),
                   jax.ShapeDtypeStruct((1, d), F32)),
        compiler_params=_params(1))(x, r, g, fw, tgt)


def _norm_bwd(x, dhs, dxres, nw, sc, g, r, name):
    s, d = x.shape
    n_dh = len(dhs)
    has_g = g is not None

    def body(*refs):
        x_ref = refs[0]
        dh_refs = refs[1:1 + n_dh]
        dxres_ref, nw_ref, sc_ref = refs[1 + n_dh:4 + n_dh]
        pos = 4 + n_dh
        if has_g:
            g_ref, r_ref = refs[pos:pos + 2]
            pos += 2
            dx_ref, dr_ref, dsh_ref, dsc_ref, dnw_ref, dg_ref = refs[pos:]
            accs = (dsh_ref, dsc_ref, dnw_ref, dg_ref)
        else:
            dx_ref, dsh_ref, dsc_ref, dnw_ref = refs[pos:]
            accs = (dsh_ref, dsc_ref, dnw_ref)

        @pl.when(pl.program_id(0) == 0)
        def _():
            for acc in accs:
                acc[...] = jnp.zeros_like(acc)

        xv = x_ref[...]
        dh = dh_refs[0][...]
        for extra in dh_refs[1:]:
            dh = dh + extra[...]
        rstd = lax.rsqrt(jnp.mean(xv * xv, axis=-1, keepdims=True) + EPS)
        xh = xv * rstd
        nwv = nw_ref[...]
        dsh_ref[...] += jnp.sum(dh, axis=0, keepdims=True)
        dsc_ref[...] += jnp.sum(dh * (xh * nwv), axis=0, keepdims=True)
        dy = dh * (1.0 + sc_ref[...])
        dnw_ref[...] += jnp.sum(dy * xh, axis=0, keepdims=True)
        dxh = dy * nwv
        dx = dxres_ref[...] + rstd * (dxh - xh * jnp.mean(dxh * xh, axis=-1, keepdims=True))
        dx_ref[...] = dx
        if has_g:
            dr_ref[...] = (g_ref[...] * dx).astype(BF16)
            dg_ref[...] += jnp.sum(dx * r_ref[...], axis=0, keepdims=True)

    ins = [x, *dhs, dxres, nw, sc]
    in_specs = [_row_spec(d)] * (2 + n_dh) + [_vec_spec(d)] * 2
    out_shape = [jax.ShapeDtypeStruct((s, d), F32)]
    out_specs = [_row_spec(d)]
    if has_g:
        ins += [g, r]
        in_specs += [_vec_spec(d), _row_spec(d)]
        out_shape.append(jax.ShapeDtypeStruct((s, d), BF16))
        out_specs.append(_row_spec(d))
    n_vec = 4 if has_g else 3
    out_shape += [jax.ShapeDtypeStruct((1, d), F32)] * n_vec
    out_specs += [_vec_spec(d)] * n_vec
    return pl.pallas_call(body, name=name, grid=(s // NORM_ROWS,), in_specs=in_specs, out_specs=tuple(out_specs),
                          out_shape=tuple(out_shape), compiler_params=_params(1))(*ins)


FFN_ROWS = 256
FFN_COLS = 1408
HALO = 8
INV_SQRT2 = 0.7071067811865476
INV_SQRT_2PI = 0.3989422804014327


def _gelu(x):
    return 0.5 * x * (1.0 + lax.erf(x * INV_SQRT2))


def _gelu_grad(x):
    return 0.5 * (1.0 + lax.erf(x * INV_SQRT2)) + x * (INV_SQRT_2PI * jnp.exp(-0.5 * x * x))


def _halo_specs(rows, cols, halo, n_rows_total, col_of):
    per = rows // halo
    last = n_rows_total // halo - 1
    cur = pl.BlockSpec((rows, cols), lambda j, i: (i, col_of(j)))
    prev = pl.BlockSpec((halo, cols), lambda j, i: (jnp.maximum(i * per - 1, 0), col_of(j)))
    nxt = pl.BlockSpec((halo, cols), lambda j, i: (jnp.minimum((i + 1) * per, last), col_of(j)))
    return [prev, cur, nxt]


def _shift_rows(x, k):
    n = x.shape[0]
    return pltpu.roll(x, k % n, axis=0)


def _conv3(ext, w):
    return w[0:1, :] * _shift_rows(ext, 1) + w[1:2, :] * ext + w[2:3, :] * _shift_rows(ext, -1)


def _ext_block(prev_ref, cur_ref, next_ref, i, n_i):
    prev = jnp.where(i > 0, prev_ref[...], 0.0)
    nxt = jnp.where(i < n_i - 1, next_ref[...], 0.0)
    return jnp.concatenate([prev, cur_ref[...], nxt], axis=0)


def _ffn_act(u, cw, name):
    s = u.shape[0]
    nc, ns = D_FF // FFN_COLS, s // FFN_ROWS

    def body(gp, gc, gn, vp, vc, vn, wg_ref, wv_ref, o_ref):
        i = pl.program_id(1)
        cg = _conv3(_ext_block(gp, gc, gn, i, ns), wg_ref[...])[HALO:HALO + FFN_ROWS]
        cv = _conv3(_ext_block(vp, vc, vn, i, ns), wv_ref[...])[HALO:HALO + FFN_ROWS]
        o_ref[...] = (_gelu(cg) * cv).astype(BF16)

    in_specs = (_halo_specs(FFN_ROWS, FFN_COLS, HALO, s, lambda j: j)
                + _halo_specs(FFN_ROWS, FFN_COLS, HALO, s, lambda j: j + nc)
                + [pl.BlockSpec((3, FFN_COLS), lambda j, i: (0, j)),
                   pl.BlockSpec((3, FFN_COLS), lambda j, i: (0, j + nc))])
    return pl.pallas_call(
        body, name=name, grid=(nc, ns), in_specs=in_specs,
        out_specs=pl.BlockSpec((FFN_ROWS, FFN_COLS), lambda j, i: (i, j)),
        out_shape=jax.ShapeDtypeStruct((s, D_FF), BF16), compiler_params=_params(2),
    )(u, u, u, u, u, u, cw, cw)


def _ffn_act_bwd(u, dact, cw, name):
    s = u.shape[0]
    nc, ns = D_FF // FFN_COLS, s // FFN_ROWS

    def body(gp, gc, gn, vp, vc, vn, dp, dc, dn, wg_ref, wv_ref, dug_ref, duv_ref, dwg_ref, dwv_ref):
        i = pl.program_id(1)

        @pl.when(i == 0)
        def _():
            dwg_ref[...] = jnp.zeros_like(dwg_ref)
            dwv_ref[...] = jnp.zeros_like(dwv_ref)

        ug = _ext_block(gp, gc, gn, i, ns)
        uv = _ext_block(vp, vc, vn, i, ns)
        da = _ext_block(dp, dc, dn, i, ns)
        wg, wv = wg_ref[...], wv_ref[...]
        cg, cv = _conv3(ug, wg), _conv3(uv, wv)
        dcg = da * cv * _gelu_grad(cg)
        dcv = da * _gelu(cg)
        inner = slice(HALO, HALO + FFN_ROWS)
        for d_c, uu, w, du_ref, dw_ref in ((dcg, ug, wg, dug_ref, dwg_ref), (dcv, uv, wv, duv_ref, dwv_ref)):
            du = w[0:1, :] * _shift_rows(d_c, -1) + w[1:2, :] * d_c + w[2:3, :] * _shift_rows(d_c, 1)
            du_ref[...] = du[inner].astype(BF16)
            d_in = d_c[inner]
            for tap in range(3):
                dw_ref[tap:tap + 1, :] += jnp.sum(d_in * _shift_rows(uu, 1 - tap)[inner], axis=0, keepdims=True)

    in_specs = (_halo_specs(FFN_ROWS, FFN_COLS, HALO, s, lambda j: j)
                + _halo_specs(FFN_ROWS, FFN_COLS, HALO, s, lambda j: j + nc)
                + _halo_specs(FFN_ROWS, FFN_COLS, HALO, s, lambda j: j)
                + [pl.BlockSpec((3, FFN_COLS), lambda j, i: (0, j)),
                   pl.BlockSpec((3, FFN_COLS), lambda j, i: (0, j + nc))])
    blk = pl.BlockSpec((FFN_ROWS, FFN_COLS), lambda j, i: (i, j))
    acc = pl.BlockSpec((HALO, FFN_COLS), lambda j, i: (0, j))
    return pl.pallas_call(
        body, name=name, grid=(nc, ns), in_specs=in_specs, out_specs=(blk, blk, acc, acc),
        out_shape=(jax.ShapeDtypeStruct((s, D_FF), BF16), jax.ShapeDtypeStruct((s, D_FF), BF16),
                   jax.ShapeDtypeStruct((HALO, D_FF), F32), jax.ShapeDtypeStruct((HALO, D_FF), F32)),
        compiler_params=_params(2),
    )(u, u, u, u, u, u, dact, dact, dact, cw, cw)


CONV_ROWS = 512
CONV_HALO = 16
CONV_PAD = CONV_WIDTH // 2


def _conv_halo_specs(cols, s):
    per = CONV_ROWS // CONV_HALO
    last = s // CONV_HALO - 1
    return [pl.BlockSpec((CONV_HALO, cols), lambda i: (jnp.maximum(i * per - 1, 0), 0)),
            pl.BlockSpec((CONV_ROWS, cols), lambda i: (i, 0)),
            pl.BlockSpec((CONV_HALO, cols), lambda i: (jnp.minimum((i + 1) * per, last), 0))]


def _glu_ext(pp, pc, pn, i, n_i):
    ext = _ext_block(pp, pc, pn, i, n_i)
    return ext[:, :CONV_CH] * _sigmoid(ext[:, CONV_CH:])


def _conv_mixer(pa, cw, cb, lnw, lnb, name):
    s = pa.shape[0]
    ns = s // CONV_ROWS

    def body(pp, pc, pn, cw_ref, cb_ref, lnw_ref, lnb_ref, o_ref, c_ref):
        i = pl.program_id(0)
        a = _glu_ext(pp, pc, pn, i, ns)
        acc = jnp.zeros((CONV_ROWS, CONV_CH), F32)
        for tap in range(CONV_WIDTH):
            acc = acc + cw_ref[tap:tap + 1, :] * _shift_rows(a, -(tap + 1))[:CONV_ROWS]
        cv = acc + cb_ref[...]
        c_ref[...] = cv
        mu = jnp.mean(cv, axis=-1, keepdims=True)
        xc = cv - mu
        rstd = lax.rsqrt(jnp.mean(xc * xc, axis=-1, keepdims=True) + EPS)
        y = xc * rstd * lnw_ref[...] + lnb_ref[...]
        o_ref[...] = (y * _sigmoid(y)).astype(BF16)

    vec = pl.BlockSpec((1, CONV_CH), lambda i: (0, 0))
    blk = pl.BlockSpec((CONV_ROWS, CONV_CH), lambda i: (i, 0))
    return pl.pallas_call(
        body, name=name, grid=(ns,),
        in_specs=_conv_halo_specs(2 * CONV_CH, s) + [pl.BlockSpec((CONV_WIDTH, CONV_CH), lambda i: (0, 0)), vec, vec, vec],
        out_specs=(blk, blk),
        out_shape=(jax.ShapeDtypeStruct((s, CONV_CH), BF16), jax.ShapeDtypeStruct((s, CONV_CH), F32)),
        compiler_params=_params(1))(pa, pa, pa, cw, cb, lnw, lnb)


def _conv_mixer_bwd_ln(cv, dout, lnw, lnb, name):
    s = cv.shape[0]

    def body(c_ref, do_ref, lnw_ref, lnb_ref, dc_ref, dlnw_ref, dlnb_ref, dcb_ref):
        @pl.when(pl.program_id(0) == 0)
        def _():
            dlnw_ref[...] = jnp.zeros_like(dlnw_ref)
            dlnb_ref[...] = jnp.zeros_like(dlnb_ref)
            dcb_ref[...] = jnp.zeros_like(dcb_ref)

        c = c_ref[...]
        mu = jnp.mean(c, axis=-1, keepdims=True)
        xc = c - mu
        rstd = lax.rsqrt(jnp.mean(xc * xc, axis=-1, keepdims=True) + EPS)
        xh = xc * rstd
        w = lnw_ref[...]
        y = xh * w + lnb_ref[...]
        dy = do_ref[...] * _silu_grad(y)
        dlnw_ref[...] += jnp.sum(dy * xh, axis=0, keepdims=True)
        dlnb_ref[...] += jnp.sum(dy, axis=0, keepdims=True)
        dxh = dy * w
        dc = rstd * (dxh - jnp.mean(dxh, axis=-1, keepdims=True) - xh * jnp.mean(dxh * xh, axis=-1, keepdims=True))
        dc_ref[...] = dc
        dcb_ref[...] += jnp.sum(dc, axis=0, keepdims=True)

    vec = pl.BlockSpec((1, CONV_CH), lambda i: (0, 0))
    blk = pl.BlockSpec((CONV_ROWS, CONV_CH), lambda i: (i, 0))
    return pl.pallas_call(
        body, name=name, grid=(s // CONV_ROWS,), in_specs=[blk, blk, vec, vec], out_specs=(blk, vec, vec, vec),
        out_shape=(jax.ShapeDtypeStruct((s, CONV_CH), F32),) + (jax.ShapeDtypeStruct((1, CONV_CH), F32),) * 3,
        compiler_params=_params(1))(cv, dout, lnw, lnb)


def _conv_mixer_bwd_conv(pa, dc, cw, name):
    s = pa.shape[0]
    ns = s // CONV_ROWS

    def body(pp, pc, pn, dp, dcc, dn, cw_ref, dpa_ref, dcw_ref):
        i = pl.program_id(0)

        @pl.when(i == 0)
        def _():
            dcw_ref[...] = jnp.zeros_like(dcw_ref)

        a = _glu_ext(pp, pc, pn, i, ns)
        dce = _ext_block(dp, dcc, dn, i, ns)
        dcur = dcc[...]
        da = jnp.zeros((CONV_ROWS, CONV_CH), F32)
        for tap in range(CONV_WIDTH):
            da = da + cw_ref[tap:tap + 1, :] * _shift_rows(dce, -(CONV_WIDTH - tap))[:CONV_ROWS]
            dcw_ref[tap:tap + 1, :] += jnp.sum(dcur * _shift_rows(a, -(tap + 1))[:CONV_ROWS], axis=0, keepdims=True)
        cur = pc[...]
        val, sg = cur[:, :CONV_CH], _sigmoid(cur[:, CONV_CH:])
        dpa_ref[:, :CONV_CH] = (da * sg).astype(BF16)
        dpa_ref[:, CONV_CH:] = (da * val * sg * (1.0 - sg)).astype(BF16)

    return pl.pallas_call(
        body, name=name, grid=(ns,),
        in_specs=_conv_halo_specs(2 * CONV_CH, s) + _conv_halo_specs(CONV_CH, s)
        + [pl.BlockSpec((CONV_WIDTH, CONV_CH), lambda i: (0, 0))],
        out_specs=(pl.BlockSpec((CONV_ROWS, 2 * CONV_CH), lambda i: (i, 0)),
                   pl.BlockSpec((32, CONV_CH), lambda i: (0, 0))),
        out_shape=(jax.ShapeDtypeStruct((s, 2 * CONV_CH), BF16), jax.ShapeDtypeStruct((32, CONV_CH), F32)),
        compiler_params=_params(1))(pa, pa, pa, dc, dc, dc, cw)


SLOPES = tuple(float(2.0 ** (-ALIBI_MAX_EXP * (h + 1) / N_HEADS)) for h in range(N_HEADS))
ATT_SCALE = HEAD_DIM ** -0.5


def _band_specs(n_blocks, col_of):
    return [pl.BlockSpec((ATT_BLOCK, ATT_WIDTH), lambda r, i: (jnp.maximum(i - 1, 0), col_of(r))),
            pl.BlockSpec((ATT_BLOCK, ATT_WIDTH), lambda r, i: (i, col_of(r))),
            pl.BlockSpec((ATT_BLOCK, ATT_WIDTH), lambda r, i: (jnp.minimum(i + 1, n_blocks - 1), col_of(r)))]


def _win(refs, sl):
    return jnp.concatenate([ref[:, sl] for ref in refs], axis=0)


def _attn_branch(qkv, dil, name):
    s = qkv.shape[0]
    length = s // dil
    nb = length // ATT_BLOCK
    view = qkv.reshape(length, dil * 3 * ATT_WIDTH)

    def body(q_ref, kp, kc, kn, vp, vc, vn, o_ref, l_ref):
        i = pl.program_id(1)
        row = lax.broadcasted_iota(jnp.int32, (ATT_BLOCK, 3 * ATT_BLOCK), 0)
        col = lax.broadcasted_iota(jnp.int32, (ATT_BLOCK, 3 * ATT_BLOCK), 1)
        kpos = (i - 1) * ATT_BLOCK + col
        dist = jnp.abs(i * ATT_BLOCK + row - kpos)
        valid = (dist <= ATT_HALF) & (kpos >= 0) & (kpos < length)
        distf = dist.astype(F32) * float(dil)
        for h in range(N_HEADS):
            sl = slice(h * HEAD_DIM, (h + 1) * HEAD_DIM)
            sc = lax.dot_general(q_ref[:, sl], _win((kp, kc, kn), sl), (((1,), (1,)), ((), ())),
                                 preferred_element_type=F32) * ATT_SCALE - SLOPES[h] * distf
            sc = jnp.where(valid, sc, MASK_VALUE)
            m = jnp.max(sc, axis=-1, keepdims=True)
            p = jnp.exp(sc - m)
            den = jnp.sum(p, axis=-1, keepdims=True)
            o = jnp.dot(p.astype(BF16), _win((vp, vc, vn), sl), preferred_element_type=F32) / den
            o_ref[:, sl] = o
            l_ref[:, sl] = jnp.broadcast_to(m + jnp.log(den), (ATT_BLOCK, HEAD_DIM))

    out_blk = pl.BlockSpec((ATT_BLOCK, ATT_WIDTH), lambda r, i: (i, r))
    o, lse = pl.pallas_call(
        body, name=name, grid=(dil, nb),
        in_specs=[pl.BlockSpec((ATT_BLOCK, ATT_WIDTH), lambda r, i: (i, 3 * r))]
        + _band_specs(nb, lambda r: 3 * r + 1) + _band_specs(nb, lambda r: 3 * r + 2),
        out_specs=(out_blk, out_blk),
        out_shape=(jax.ShapeDtypeStruct((length, dil * ATT_WIDTH), F32),) * 2,
        compiler_params=_params(2))(view, view, view, view, view, view, view)
    return o.reshape(s, ATT_WIDTH), lse.reshape(s, ATT_WIDTH)


ATT_ROWS = 512


def _attn_combine(outs, lses, name):
    s = outs[0].shape[0]

    def body(o1, o2, o3, l1, l2, l3, att_ref, att32_ref, lse_ref):
        ls = [l1[...], l2[...], l3[...]]
        m = jnp.maximum(jnp.maximum(ls[0], ls[1]), ls[2])
        es = [jnp.exp(l - m) for l in ls]
        den = es[0] + es[1] + es[2]
        att = (es[0] * o1[...] + es[1] * o2[...] + es[2] * o3[...]) / den
        att_ref[...] = att.astype(BF16)
        att32_ref[...] = att
        lse_ref[...] = m + jnp.log(den)

    blk = pl.BlockSpec((ATT_ROWS, ATT_WIDTH), lambda i: (i, 0))
    return pl.pallas_call(
        body, name=name, grid=(s // ATT_ROWS,), in_specs=[blk] * 6, out_specs=(blk, blk, blk),
        out_shape=(jax.ShapeDtypeStruct((s, ATT_WIDTH), BF16), jax.ShapeDtypeStruct((s, ATT_WIDTH), F32),
                   jax.ShapeDtypeStruct((s, ATT_WIDTH), F32)),
        compiler_params=_params(1))(*outs, *lses)


def _attn_delta(datt, att, name):
    s = att.shape[0]

    def body(d_ref, a_ref, delta_ref, dbf_ref):
        prod = d_ref[...] * a_ref[...]
        for h in range(N_HEADS):
            sl = slice(h * HEAD_DIM, (h + 1) * HEAD_DIM)
            delta_ref[:, sl] = jnp.broadcast_to(jnp.sum(prod[:, sl], axis=-1, keepdims=True), (ATT_ROWS, HEAD_DIM))
        dbf_ref[...] = d_ref[...].astype(BF16)

    blk = pl.BlockSpec((ATT_ROWS, ATT_WIDTH), lambda i: (i, 0))
    return pl.pallas_call(
        body, name=name, grid=(s // ATT_ROWS,), in_specs=[blk, blk], out_specs=(blk, blk),
        out_shape=(jax.ShapeDtypeStruct((s, ATT_WIDTH), F32), jax.ShapeDtypeStruct((s, ATT_WIDTH), BF16)),
        compiler_params=_params(1))(datt, att)


def _attn_branch_bwd(qkv, do, lse, delta, prev, dil, name):
    s = qkv.shape[0]
    length = s // dil
    nb = length // ATT_BLOCK
    view = qkv.reshape(length, dil * 3 * ATT_WIDTH)
    sub = lambda t: t.reshape(length, dil * ATT_WIDTH)
    has_prev = prev is not None
    tn = (((0,), (0,)), ((), ()))
    nt = (((1,), (1,)), ((), ()))

    def body(*refs):
        qs, ks, vs, dos, ls, des = (refs[3 * n:3 * n + 3] for n in range(6))
        rest = refs[18:]
        if has_prev:
            pq, pk, pv = rest[:3]
            rest = rest[3:]
        dq_ref, dk_ref, dv_ref = rest
        i = pl.program_id(1)
        row = lax.broadcasted_iota(jnp.int32, (ATT_BLOCK, 3 * ATT_BLOCK), 0)
        col = lax.broadcasted_iota(jnp.int32, (ATT_BLOCK, 3 * ATT_BLOCK), 1)
        kpos = (i - 1) * ATT_BLOCK + col
        dist_q = jnp.abs(i * ATT_BLOCK + row - kpos)
        valid_q = (dist_q <= ATT_HALF) & (kpos >= 0) & (kpos < length)
        distf_q = dist_q.astype(F32) * float(dil)
        rowk = lax.broadcasted_iota(jnp.int32, (3 * ATT_BLOCK, ATT_BLOCK), 0)
        colk = lax.broadcasted_iota(jnp.int32, (3 * ATT_BLOCK, ATT_BLOCK), 1)
        qpos = (i - 1) * ATT_BLOCK + rowk
        dist_k = jnp.abs(qpos - (i * ATT_BLOCK + colk))
        valid_k = (dist_k <= ATT_HALF) & (qpos >= 0) & (qpos < length)
        distf_k = dist_k.astype(F32) * float(dil)
        for h in range(N_HEADS):
            sl = slice(h * HEAD_DIM, (h + 1) * HEAD_DIM)
            one = slice(h * HEAD_DIM, h * HEAD_DIM + 1)
            k_win, v_win = _win(ks, sl), _win(vs, sl)
            q_cur, do_cur = qs[1][:, sl], dos[1][:, sl]
            sc = lax.dot_general(q_cur, k_win, nt, preferred_element_type=F32) * ATT_SCALE - SLOPES[h] * distf_q
            p = jnp.exp(jnp.where(valid_q, sc - ls[1][:, one], MASK_VALUE))
            dp = lax.dot_general(do_cur, v_win, nt, preferred_element_type=F32)
            ds = (p * (dp - des[1][:, one]) * ATT_SCALE).astype(BF16)
            dq = jnp.dot(ds, k_win, preferred_element_type=F32)

            q_win, do_win = _win(qs, sl), _win(dos, sl)
            k_cur, v_cur = ks[1][:, sl], vs[1][:, sl]
            sc2 = lax.dot_general(q_win, k_cur, nt, preferred_element_type=F32) * ATT_SCALE - SLOPES[h] * distf_k
            p2 = jnp.exp(jnp.where(valid_k, sc2 - _win(ls, one), MASK_VALUE))
            dv = lax.dot_general(p2.astype(BF16), do_win, tn, preferred_element_type=F32)
            dp2 = lax.dot_general(do_win, v_cur, nt, preferred_element_type=F32)
            ds2 = (p2 * (dp2 - _win(des, one)) * ATT_SCALE).astype(BF16)
            dk = lax.dot_general(ds2, q_win, tn, preferred_element_type=F32)
            if has_prev:
                dq, dk, dv = dq + pq[:, sl], dk + pk[:, sl], dv + pv[:, sl]
            dq_ref[:, sl], dk_ref[:, sl], dv_ref[:, sl] = dq, dk, dv

    blk = pl.BlockSpec((ATT_BLOCK, ATT_WIDTH), lambda r, i: (i, r))
    in_specs = (_band_specs(nb, lambda r: 3 * r) + _band_specs(nb, lambda r: 3 * r + 1)
                + _band_specs(nb, lambda r: 3 * r + 2) + _band_specs(nb, lambda r: r) * 3)
    ins = [view] * 9 + [sub(do)] * 3 + [sub(lse)] * 3 + [sub(delta)] * 3
    if has_prev:
        in_specs += [blk] * 3
        ins += [sub(t) for t in prev]
    outs = pl.pallas_call(
        body, name=name, grid=(dil, nb), in_specs=in_specs, out_specs=(blk, blk, blk),
        out_shape=(jax.ShapeDtypeStruct((length, dil * ATT_WIDTH), F32),) * 3,
        compiler_params=_params(2))(*ins)
    return tuple(t.reshape(s, ATT_WIDTH) for t in outs)


TB = 2 * REC_CHUNK
REC_SUB = 16
REC_ROWS = 5 * REC_WIDTH


def _chunk_scan(x, pos, rev):
    for step in (1, 2, 4, 8, 16, 32):
        if rev:
            x = x + jnp.where(pos < REC_CHUNK - step, pltpu.roll(x, TB - step, axis=1), 0.0)
        else:
            x = x + jnp.where(pos >= step, pltpu.roll(x, step, axis=1), 0.0)
    return x


def _hg_prep(qraw, z, lb, rev):
    lane = lax.broadcasted_iota(jnp.int32, (REC_WIDTH, TB), 1)
    pos = lane & (REC_CHUNK - 1)
    in_a = lane < REC_CHUNK
    sig, sigm = _sigmoid(z), _sigmoid(-z)
    f = lb + (1.0 - lb) * sig
    kk = (1.0 - lb) * sigm
    b = _chunk_scan(jnp.log(jnp.maximum(f, F_TINY)), pos, rev)
    end_a = b[:, 0:1] if rev else b[:, REC_CHUNK - 1:REC_CHUNK]
    end_b = b[:, REC_CHUNK:REC_CHUNK + 1] if rev else b[:, TB - 1:TB]
    bend = jnp.where(in_a, end_a, end_b)
    q = qraw * _sigmoid(qraw)
    sub = ((REC_CHUNK - 1 - pos) if rev else pos) // REC_SUB
    eq, ek = [], []
    for i in range(1, REC_CHUNK // REC_SUB):
        la = (REC_CHUNK - REC_SUB * i) if rev else (REC_SUB * i - 1)
        ri = jnp.where(in_a, b[:, la:la + 1], b[:, la + REC_CHUNK:la + REC_CHUNK + 1])
        eq.append(jnp.where(sub == i, jnp.exp(jnp.minimum(b - ri, 0.0)), 0.0))
        ek.append(jnp.where(sub < i, jnp.exp(jnp.minimum(ri - b, 0.0)), 0.0))
    return dict(pos=pos, in_a=in_a, sig=sig, sigm=sigm, f=f, kk=kk, b=b, end_a=end_a, end_b=end_b,
                q=q, qh=q * jnp.exp(b), kh=kk * jnp.exp(bend - b), ekb=jnp.exp(bend - b), eq=eq, ek=ek,
                pos_sub=lane & (REC_SUB - 1))


def _pair_masks(rev):
    row = lax.broadcasted_iota(jnp.int32, (TB, TB), 0)
    col = lax.broadcasted_iota(jnp.int32, (TB, TB), 1)
    same = (row < REC_CHUNK) == (col < REC_CHUNK)
    scan = lambda p: ((REC_CHUNK - 1 - (p & (REC_CHUNK - 1))) if rev else (p & (REC_CHUNK - 1))) // REC_SUB
    causal = same & ((row >= col) if rev else (row <= col))
    earlier_sub = same & (scan(row) < scan(col))
    return same, causal, earlier_sub, col - row


def _head_rows(x, h):
    return x[h * HEAD_DIM:(h + 1) * HEAD_DIM, :]


def _stack_subs(parts, h):
    return jnp.concatenate([_head_rows(p, h) for p in parts], axis=0)


def _block_diag_mask():
    r = lax.broadcasted_iota(jnp.int32, (REC_WIDTH, REC_WIDTH), 0) // HEAD_DIM
    c = lax.broadcasted_iota(jnp.int32, (REC_WIDTH, REC_WIDTH), 1) // HEAD_DIM
    return (r == c).astype(F32)


def _heads(x):
    return x.reshape(N_HEADS, HEAD_DIM, TB)


def _hg_shift(delta, rev):
    return jnp.where(delta == 0, 0, TB - delta) if rev else delta


def _hg_unshift(delta, rev):
    return delta if rev else jnp.where(delta == 0, 0, TB - delta)


def _hgrn_scan(projt, lb, rev, name):
    s = projt.shape[1]
    nblk = s // TB
    zrow = 2 if rev else 1
    tmap = (lambda i: nblk - 1 - i) if rev else (lambda i: i)
    tn = (((0,), (0,)), ((), ()))
    nt = (((1,), (1,)), ((), ()))

    def body(q_ref, z_ref, v_ref, lb_ref, o_ref, hs_ref, at_ref, h_ref, acc_ref):
        @pl.when(pl.program_id(0) == 0)
        def _():
            h_ref[...] = jnp.zeros_like(h_ref)

        v = v_ref[...]
        vb = v.astype(BF16)
        pr = _hg_prep(q_ref[...], z_ref[...], lb_ref[...], rev)
        q, kk, b, pos_sub = pr["q"], pr["kk"], pr["b"], pr["pos_sub"]
        same, _, _, offset = _pair_masks(rev)
        qt = [q * e for e in pr["eq"]]
        kt = [kk * e for e in pr["ek"]]
        for h in range(N_HEADS):
            sc = lax.dot_general(_stack_subs(kt, h).astype(BF16), _stack_subs(qt, h).astype(BF16), tn,
                                 preferred_element_type=F32)
            acc_ref[h] = jnp.where(same, sc, 0.0)

        def pair_step(delta, carry):
            sh = _hg_shift(delta, rev)
            kd, bd = pltpu.roll(kk, sh, axis=1), pltpu.roll(b, sh, axis=1)
            valid = (pos_sub <= REC_SUB - 1 - delta) if rev else (pos_sub >= delta)
            w = jnp.where(valid, q * kd * jnp.exp(jnp.where(valid, b - bd, 0.0)), 0.0)
            a = jnp.sum(_heads(w), axis=1)
            hit = offset == (-delta if rev else delta)
            for h in range(N_HEADS):
                acc_ref[h] += jnp.where(hit, a[h:h + 1, :], 0.0)
            return carry

        lax.fori_loop(0, REC_SUB, pair_step, 0)
        outs = []
        for h in range(N_HEADS):
            a_bf = acc_ref[h].astype(BF16)
            at_ref[h] = a_bf
            outs.append(jnp.dot(_head_rows(vb, h), a_bf, preferred_element_type=F32))
        o = jnp.concatenate(outs, axis=0)
        bd_mask = _block_diag_mask()
        order = ((1, ~pr["in_a"], pr["end_b"]), (0, pr["in_a"], pr["end_a"]))
        if not rev:
            order = order[::-1]
        for slot, msk, bend in order:
            h0 = h_ref[...]
            hs_ref[slot] = h0
            o = o + lax.dot_general(h0.astype(BF16), jnp.where(msk, pr["qh"], 0.0).astype(BF16), tn,
                                    preferred_element_type=F32)
            upd = lax.dot_general(jnp.where(msk, pr["kh"], 0.0).astype(BF16), vb, nt, preferred_element_type=F32)
            h_ref[...] = jnp.exp(bend) * h0 + upd * bd_mask
        o_ref[...] = o

    row_blk = lambda r: pl.BlockSpec((REC_WIDTH, TB), lambda i: (r, tmap(i)))
    return pl.pallas_call(
        body, name=name, grid=(nblk,),
        in_specs=[row_blk(0), row_blk(zrow), row_blk(3), pl.BlockSpec((REC_WIDTH, 1), lambda i: (0, 0))],
        out_specs=(pl.BlockSpec((REC_WIDTH, TB), lambda i: (0, tmap(i))),
                   pl.BlockSpec((2, REC_WIDTH, REC_WIDTH), lambda i: (tmap(i), 0, 0)),
                   pl.BlockSpec((None, N_HEADS, TB, TB), lambda i: (tmap(i), 0, 0, 0))),
        out_shape=(jax.ShapeDtypeStruct((REC_WIDTH, s), F32),
                   jax.ShapeDtypeStruct((s // REC_CHUNK, REC_WIDTH, REC_WIDTH), F32),
                   jax.ShapeDtypeStruct((nblk, N_HEADS, TB, TB), BF16)),
        scratch_shapes=[pltpu.VMEM((REC_WIDTH, REC_WIDTH), F32), pltpu.VMEM((N_HEADS, TB, TB), F32)],
        compiler_params=_params(1))(projt, projt, projt, lb)


def _hgrn_scan_bwd(projt, lb, dot, hs, at, prev, rev, name):
    s = projt.shape[1]
    nblk = s // TB
    zrow = 2 if rev else 1
    tmap = (lambda i: i) if rev else (lambda i: nblk - 1 - i)
    has_prev = prev is not None
    tn = (((0,), (0,)), ((), ()))
    nt = (((1,), (1,)), ((), ()))

    def body(*refs):
        q_ref, z_ref, v_ref, lb_ref, do_ref, hs_ref, at_ref = refs[:7]
        rest = refs[7:]
        if has_prev:
            pq_ref, pv_ref = rest[:2]
            rest = rest[2:]
        dq_ref, dz_ref, dv_ref, dlb_ref, dh_ref, dat_ref = rest

        @pl.when(pl.program_id(0) == 0)
        def _():
            dh_ref[...] = jnp.zeros_like(dh_ref)
            dlb_ref[...] = jnp.zeros_like(dlb_ref)

        qraw, v, do, lbv = q_ref[...], v_ref[...], do_ref[...], lb_ref[...]
        dob, vb = do.astype(BF16), v.astype(BF16)
        pr = _hg_prep(qraw, z_ref[...], lbv, rev)
        q, kk, b, pos_sub, in_a = pr["q"], pr["kk"], pr["b"], pr["pos_sub"], pr["in_a"]
        _, causal, earlier_sub, offset = _pair_masks(rev)
        qt = [q * e for e in pr["eq"]]
        kt = [kk * e for e in pr["ek"]]
        n_sub = len(qt)
        dq_h, dk_h, dv_h = [], [], []
        for h in range(N_HEADS):
            d_at = jnp.where(causal, lax.dot_general(_head_rows(vb, h), _head_rows(dob, h), tn,
                                                     preferred_element_type=F32), 0.0)
            dat_ref[h] = d_at
            dv_h.append(lax.dot_general(_head_rows(dob, h), at_ref[h], nt, preferred_element_type=F32))
            d_off = jnp.where(earlier_sub, d_at, 0.0).astype(BF16)
            dqt = jnp.dot(_stack_subs(kt, h).astype(BF16), d_off, preferred_element_type=F32)
            dkt = lax.dot_general(_stack_subs(qt, h).astype(BF16), d_off, nt, preferred_element_type=F32)
            dq_h.append(sum(_head_rows(pr["eq"][i], h) * dqt[i * HEAD_DIM:(i + 1) * HEAD_DIM] for i in range(n_sub)))
            dk_h.append(sum(_head_rows(pr["ek"][i], h) * dkt[i * HEAD_DIM:(i + 1) * HEAD_DIM] for i in range(n_sub)))
        dq0, dk0, dv = (jnp.concatenate(t, axis=0) for t in (dq_h, dk_h, dv_h))

        def pair_step(delta, carry):
            dq, dk = carry
            sh, back = _hg_shift(delta, rev), _hg_unshift(delta, rev)
            kd, bd = pltpu.roll(kk, sh, axis=1), pltpu.roll(b, sh, axis=1)
            valid = (pos_sub <= REC_SUB - 1 - delta) if rev else (pos_sub >= delta)
            e = jnp.where(valid, jnp.exp(jnp.where(valid, b - bd, 0.0)), 0.0)
            hit = offset == (-delta if rev else delta)
            da = jnp.concatenate(
                [jnp.broadcast_to(jnp.sum(jnp.where(hit, dat_ref[h], 0.0), axis=0, keepdims=True), (HEAD_DIM, TB))
                 for h in range(N_HEADS)], axis=0)
            dq = dq + da * kd * e
            dk = dk + pltpu.roll(da * q * e, back, axis=1)
            return dq, dk

        dq, dk = lax.fori_loop(0, REC_SUB, pair_step, (dq0, dk0))

        zero = jnp.zeros((REC_WIDTH, TB), F32)
        bd_mask = _block_diag_mask()
        eb = jnp.exp(b)
        const = zero
        order = ((0, in_a, pr["end_a"]), (1, ~in_a, pr["end_b"]))
        if not rev:
            order = order[::-1]
        for slot, msk, bend in order:
            h0 = hs_ref[slot]
            dh1 = dh_ref[...]
            dh1b = dh1.astype(BF16)
            dq = dq + eb * jnp.dot(h0.astype(BF16), jnp.where(msk, do, 0.0).astype(BF16), preferred_element_type=F32)
            dv = dv + lax.dot_general(dh1b, jnp.where(msk, pr["kh"], 0.0).astype(BF16), tn, preferred_element_type=F32)
            dk_int = pr["ekb"] * jnp.dot(dh1b, jnp.where(msk, v, 0.0).astype(BF16), preferred_element_type=F32)
            dk = dk + dk_int
            ebend = jnp.exp(bend)
            c = (jnp.sum(kk * dk_int, axis=1, keepdims=True)
                 + ebend * jnp.sum(h0 * dh1, axis=1, keepdims=True))
            const = const + jnp.where(msk, c, 0.0)
            upd = lax.dot_general(jnp.where(msk, pr["qh"], 0.0).astype(BF16), dob, nt, preferred_element_type=F32)
            dh_ref[...] = ebend * dh1 + upd * bd_mask

        dg = _chunk_scan(q * dq - kk * dk, pr["pos"], not rev) + const
        sig, sigm, f = pr["sig"], pr["sigm"], pr["f"]
        live = f > F_TINY
        inv_f = 1.0 / jnp.maximum(f, F_TINY)
        one_lb = 1.0 - lbv
        dz = sig * sigm * one_lb * (jnp.where(live, dg * inv_f, 0.0) - dk)
        dlb_ref[...] += jnp.sum(sigm * (jnp.where(live, dg * inv_f, 0.0) - dk), axis=1, keepdims=True)
        dqr = dq * _silu_grad(qraw)
        if has_prev:
            dqr = dqr + pq_ref[...]
            dv = dv + pv_ref[...]
        dq_ref[...] = dqr
        dz_ref[...] = dz
        dv_ref[...] = dv

    row_blk = lambda r: pl.BlockSpec((REC_WIDTH, TB), lambda i: (r, tmap(i)))
    blk = pl.BlockSpec((REC_WIDTH, TB), lambda i: (0, tmap(i)))
    col = pl.BlockSpec((REC_WIDTH, 1), lambda i: (0, 0))
    in_specs = [row_blk(0), row_blk(zrow), row_blk(3), col, blk,
                pl.BlockSpec((2, REC_WIDTH, REC_WIDTH), lambda i: (tmap(i), 0, 0)),
                pl.BlockSpec((None, N_HEADS, TB, TB), lambda i: (tmap(i), 0, 0, 0))]
    ins = [projt, projt, projt, lb, dot, hs, at]
    if has_prev:
        in_specs += [blk, blk]
        ins += list(prev)
    t_shape = jax.ShapeDtypeStruct((REC_WIDTH, s), F32)
    return pl.pallas_call(
        body, name=name, grid=(nblk,), in_specs=in_specs, out_specs=(blk, blk, blk, col),
        out_shape=(t_shape, t_shape, t_shape, jax.ShapeDtypeStruct((REC_WIDTH, 1), F32)),
        scratch_shapes=[pltpu.VMEM((REC_WIDTH, REC_WIDTH), F32), pltpu.VMEM((N_HEADS, TB, TB), F32)],
        compiler_params=_params(1))(*ins)


REC_OUT_COLS = 512


def _head_rms(o):
    o3 = o.reshape(N_HEADS, HEAD_DIM, o.shape[1])
    rstd = lax.rsqrt(jnp.mean(o3 * o3, axis=1, keepdims=True) + EPS)
    return o3 * rstd, rstd


def _hgrn_out(of, ob, projt, wn, name):
    s = of.shape[1]

    def body(of_ref, ob_ref, g_ref, wn_ref, o_ref):
        on, _ = _head_rms(of_ref[...] + ob_ref[...])
        g = g_ref[...]
        y = on.reshape(REC_WIDTH, REC_OUT_COLS) * wn_ref[...] * (g * _sigmoid(g))
        o_ref[...] = y.T.astype(BF16)

    blk = pl.BlockSpec((REC_WIDTH, REC_OUT_COLS), lambda i: (0, i))
    return pl.pallas_call(
        body, name=name, grid=(s // REC_OUT_COLS,),
        in_specs=[blk, blk, pl.BlockSpec((REC_WIDTH, REC_OUT_COLS), lambda i: (4, i)),
                  pl.BlockSpec((REC_WIDTH, 1), lambda i: (0, 0))],
        out_specs=pl.BlockSpec((REC_OUT_COLS, REC_WIDTH), lambda i: (i, 0)),
        out_shape=jax.ShapeDtypeStruct((s, REC_WIDTH), BF16), compiler_params=_params(1))(of, ob, projt, wn)


def _hgrn_out_bwd(drec, of, ob, projt, wn, name):
    s = of.shape[1]

    def body(d_ref, of_ref, ob_ref, g_ref, wn_ref, do_ref, dg_ref, dwn_ref):
        @pl.when(pl.program_id(0) == 0)
        def _():
            dwn_ref[...] = jnp.zeros_like(dwn_ref)

        dy = d_ref[...].T
        on3, rstd = _head_rms(of_ref[...] + ob_ref[...])
        on = on3.reshape(REC_WIDTH, REC_OUT_COLS)
        g, wnv = g_ref[...], wn_ref[...]
        dg_ref[...] = dy * on * wnv * _silu_grad(g)
        d_onw = dy * (g * _sigmoid(g))
        dwn_ref[...] += jnp.sum(d_onw * on, axis=1, keepdims=True)
        d_on3 = (d_onw * wnv).reshape(N_HEADS, HEAD_DIM, REC_OUT_COLS)
        do3 = rstd * (d_on3 - on3 * jnp.mean(d_on3 * on3, axis=1, keepdims=True))
        do_ref[...] = do3.reshape(REC_WIDTH, REC_OUT_COLS)

    blk = pl.BlockSpec((REC_WIDTH, REC_OUT_COLS), lambda i: (0, i))
    col = pl.BlockSpec((REC_WIDTH, 1), lambda i: (0, 0))
    t_shape = jax.ShapeDtypeStruct((REC_WIDTH, s), F32)
    return pl.pallas_call(
        body, name=name, grid=(s // REC_OUT_COLS,),
        in_specs=[pl.BlockSpec((REC_OUT_COLS, REC_WIDTH), lambda i: (i, 0)), blk, blk,
                  pl.BlockSpec((REC_WIDTH, REC_OUT_COLS), lambda i: (4, i)), col],
        out_specs=(blk, blk, col),
        out_shape=(t_shape, t_shape, jax.ShapeDtypeStruct((REC_WIDTH, 1), F32)),
        compiler_params=_params(1))(drec, of, ob, projt, wn)


def _lower_bounds(gamma, name):
    def body(g_ref, lb_ref, p_ref):
        g0, g1 = g_ref[0:1, :], g_ref[1:2, :]
        m = jnp.maximum(g0, g1)
        e0, e1 = jnp.exp(g0 - m), jnp.exp(g1 - m)
        p0, p1 = e0 / (e0 + e1), e1 / (e0 + e1)
        lb_ref[...] = (p0 + p1) - p0
        p_ref[0:1, :] = p0
        p_ref[1:2, :] = p1

    n = gamma.shape[1]
    return pl.pallas_call(body, name=name,
                          out_shape=(jax.ShapeDtypeStruct((1, n), F32), jax.ShapeDtypeStruct((2, n), F32)))(gamma)


def _lower_bounds_bwd(dlb1, p, name):
    def body(d_ref, p_ref, o_ref):
        p0, p1, d = p_ref[0:1, :], p_ref[1:2, :], d_ref[...]
        inner = p1 * d
        o_ref[0:1, :] = p0 * (0.0 - inner)
        o_ref[1:2, :] = p1 * (d - inner)

    return pl.pallas_call(body, name=name, out_shape=jax.ShapeDtypeStruct(p.shape, F32))(dlb1, p)


def _split_layer_weights(w_in, w_out, w_up, w_down):
    return dict(conv=w_in[:, G_CONV[0]:G_CONV[1]], qkv=w_in[:, G_QKV[0]:G_QKV[1]],
                rec_t=w_in[:, G_REC[0]:].T, nat=w_in[:, :G_REC[0]],
                out=w_out, out_a=w_out[:CONV_CH], out_b=w_out[CONV_CH:CONV_CH + ATT_WIDTH],
                out_c=w_out[CONV_CH + ATT_WIDTH:], up=w_up, down=w_down)


def _col(v):
    return v.reshape(-1, 1)


def _sequence_step(x, tgt, mods, lbs, small, big, final_w):
    saved = []
    xin = x
    h1 = _resid_norm_mod(x, None, None, small[0]["norm1_w"], mods[0][1:2], mods[0][0:1], "norm1_first")
    for l in range(DEPTH):
        sm, w, md = small[l], big[l], mods[l]
        pa = _matmul(h1, w["conv"], "nn", F32, f"proj_conv")
        qkv = _matmul(h1, w["qkv"], "nn", BF16, f"proj_qkv")
        projt = _matmul(w["rec_t"], h1, "nt", F32, f"proj_rec")
        a_out, cv = _conv_mixer(pa, sm["conv_a_w"], sm["conv_a_b"], sm["ln_a_w"], sm["ln_a_b"], f"conv_mixer")
        outs, lses = zip(*[_attn_branch(qkv, d, f"attn_d{d}") for d in DILATIONS])
        att, att32, lse = _attn_combine(outs, lses, f"attn_combine")
        lb_f, lb_b = _col(lbs[l][0]), _col(lbs[l][1])
        of, hsf, atf = _hgrn_scan(projt, lb_f, False, "hgrn_fwd")
        ob, hsb, atb = _hgrn_scan(projt, lb_b, True, "hgrn_rev")
        wn = _col(sm["rec_norm_w"])
        rec = _hgrn_out(of, ob, projt, wn, f"hgrn_out")
        mixed = jnp.concatenate([a_out, att, rec], axis=1)
        r1 = _matmul(mixed, w["out"], "nn", F32, f"out_proj")
        xmid, h2 = _resid_norm_mod(xin, r1, md[2:3], sm["norm2_w"], md[4:5], md[3:4], f"norm2")
        u = _matmul(h2, w["up"], "nn", F32, f"ffn_up")
        act = _ffn_act(u, sm["conv_f_w"], f"ffn_act")
        r2 = _matmul(act, w["down"], "nn", F32, f"ffn_down")
        saved.append(dict(xin=xin, h1=h1, pa=pa, qkv=qkv, projt=projt, cv=cv, att32=att32, lse=lse, of=of, ob=ob,
                          hsf=hsf, hsb=hsb, atf=atf, atb=atb, lb_f=lb_f, lb_b=lb_b, wn=wn, mixed=mixed, r1=r1, xmid=xmid, h2=h2,
                          u=u, act=act, r2=r2))
        if l + 1 < DEPTH:
            nxt = small[l + 1]
            xin, h1 = _resid_norm_mod(xmid, r2, md[5:6], nxt["norm1_w"], mods[l + 1][1:2], mods[l + 1][0:1],
                                      "norm1")
    top = saved[-1]
    loss, dx, dr2, dg2, dfw = _final_loss(top["xmid"], top["r2"], mods[-1][5:6], final_w, tgt, "final_loss")

    grads = [None] * DEPTH
    for l in reversed(range(DEPTH)):
        sm, w, md, sv = small[l], big[l], mods[l], saved[l]
        dact = _matmul(dr2, w["down"], "nt", F32, f"d_act")
        g_down = _matmul(sv["act"], dr2, "tn", F32, f"dw_down")
        dug, duv, dwg, dwv = _ffn_act_bwd(sv["u"], dact, sm["conv_f_w"], f"ffn_act_bwd")
        du = jnp.concatenate([dug, duv], axis=1)
        dh2 = _matmul(du, w["up"], "nt", F32, f"d_h2")
        g_up = _matmul(sv["h2"], du, "tn", F32, f"dw_up")
        dxmid, dr1, dsh2, dsc2, dnw2, dg1 = _norm_bwd(sv["xmid"], [dh2], dx, sm["norm2_w"], md[4:5], md[2:3], sv["r1"],
                                                     f"norm2_bwd")
        dmix_a = _matmul(dr1, w["out_a"], "nt", F32, f"d_mix_a")
        dmix_b = _matmul(dr1, w["out_b"], "nt", F32, f"d_mix_b")
        dmix_c = _matmul(dr1, w["out_c"], "nt", F32, f"d_mix_c")
        g_out = _matmul(sv["mixed"], dr1, "tn", F32, f"dw_out")
        dc, dlnw, dlnb, dcb = _conv_mixer_bwd_ln(sv["cv"], dmix_a, sm["ln_a_w"], sm["ln_a_b"], f"conv_mixer_bwd_ln")
        dpa, dcw = _conv_mixer_bwd_conv(sv["pa"], dc, sm["conv_a_w"], f"conv_mixer_bwd_conv")
        delta, dobf = _attn_delta(dmix_b, sv["att32"], f"attn_delta")
        dqkv = None
        for d in DILATIONS:
            dqkv = _attn_branch_bwd(sv["qkv"], dobf, sv["lse"], delta, dqkv, d, f"attn_bwd_d{d}")
        dot, dgt, dwn = _hgrn_out_bwd(dmix_c, sv["of"], sv["ob"], sv["projt"], sv["wn"], f"hgrn_out_bwd")
        dqf, dzf, dvf, dlbf = _hgrn_scan_bwd(sv["projt"], sv["lb_f"], dot, sv["hsf"], sv["atf"], None, False,
                                             "hgrn_fwd_bwd")
        dqt, dzb, dvt, dlbb = _hgrn_scan_bwd(sv["projt"], sv["lb_b"], dot, sv["hsb"], sv["atb"], (dqf, dvf), True,
                                             "hgrn_rev_bwd")
        dprojt = jnp.concatenate([dqt, dzf, dzb, dvt, dgt], axis=0).astype(BF16)
        dnat = jnp.concatenate([dpa] + [t.astype(BF16) for t in dqkv], axis=1)
        dh1_a = _matmul(dnat, w["nat"], "nt", F32, f"d_h1_nat")
        dh1_b = _matmul(dprojt, w["rec_t"], "tn", F32, f"d_h1_rec")
        g_in_nat = _matmul(sv["h1"], dnat, "tn", F32, f"dw_in_nat")
        g_in_rec_t = _matmul(dprojt, sv["h1"], "nn", F32, f"dw_in_rec")
        g_in = jnp.concatenate([g_in_nat, g_in_rec_t.T], axis=1)
        if l > 0:
            below = saved[l - 1]
            dx, dr2, dsh1, dsc1, dnw1, dg2_below = _norm_bwd(sv["xin"], [dh1_a, dh1_b], dxmid, sm["norm1_w"], md[1:2],
                                                            mods[l - 1][5:6], below["r2"], f"norm1_bwd")
        else:
            dx, dsh1, dsc1, dnw1 = _norm_bwd(sv["xin"], [dh1_a, dh1_b], dxmid, sm["norm1_w"], md[1:2], None, None,
                                             f"norm1_bwd")
        grads[l] = dict(w_in=g_in, w_out=g_out, w_up=g_up, w_down=g_down,
                        mod=[dsh1, dsc1, dg1, dsh2, dsc2, dg2], norm1_w=dnw1, conv_a_w=dcw[:CONV_WIDTH], conv_a_b=dcb,
                        ln_a_w=dlnw, ln_a_b=dlnb, lb=jnp.concatenate([dlbf.reshape(1, -1), dlbb.reshape(1, -1)], axis=0),
                        rec_norm_w=dwn.reshape(1, -1), norm2_w=dnw2,
                        conv_f_w=jnp.concatenate([dwg[:3], dwv[:3]], axis=1))
        if l > 0:
            dg2 = dg2_below
    return loss[0, 0], dx, grads, dfw


def _adamw_math(w, g, m, v):
    m = ADAM_B1 * m + (1.0 - ADAM_B1) * g
    v = ADAM_B2 * v + (1.0 - ADAM_B2) * (g * g)
    m_hat = m / (1.0 - ADAM_B1 ** ADAM_STEP)
    v_hat = v / (1.0 - ADAM_B2 ** ADAM_STEP)
    delta = -ADAM_LR * (m_hat / (jnp.sqrt(v_hat) + ADAM_EPS) + ADAM_WD * w)
    return delta, m, v


def _row_tile(rows, cols, max_elems=384 * 1024):
    best = None
    for t in range(8, rows + 1, 8):
        if rows % t == 0 and t * cols <= max_elems:
            best = t
    return best or rows


def _adamw(w, g, m, v, name):
    nl, r, c = w.shape
    tr = _row_tile(r, c)

    def body(w_ref, g_ref, m_ref, v_ref, d_ref, m2_ref, v2_ref):
        d_ref[...], m2_ref[...], v2_ref[...] = _adamw_math(w_ref[...], g_ref[...], m_ref[...], v_ref[...])

    blk = pl.BlockSpec((None, tr, c), lambda l, i: (l, i, 0))
    shape = jax.ShapeDtypeStruct((nl, r, c), F32)
    return pl.pallas_call(body, name=name, grid=(nl, r // tr), in_specs=[blk] * 4, out_specs=(blk, blk, blk),
                          out_shape=(shape, shape, shape), compiler_params=_params(2))(w, g, m, v)


ADA_SHARD = N_MOD * D_MODEL // 4
ADA_COLS = 512
ADA_ROWS = 256
HIGHEST = lax.Precision.HIGHEST


def _ada_mod(c_all, w_ada, b_sh, name):
    def body(c_ref, w_ref, b_ref, o_ref):
        cv = c_ref[...]
        o_ref[...] = jnp.dot(cv * _sigmoid(cv), w_ref[...], precision=HIGHEST, preferred_element_type=F32) + b_ref[...]

    return pl.pallas_call(
        body, name=name, grid=(DEPTH, ADA_SHARD // ADA_COLS),
        in_specs=[pl.BlockSpec((8, D_MODEL), lambda l, j: (0, 0)),
                  pl.BlockSpec((None, D_MODEL, ADA_COLS), lambda l, j: (l, 0, j)),
                  pl.BlockSpec((None, 1, ADA_COLS), lambda l, j: (l, 0, j))],
        out_specs=pl.BlockSpec((None, 8, ADA_COLS), lambda l, j: (l, 0, j)),
        out_shape=jax.ShapeDtypeStruct((DEPTH, 8, ADA_SHARD), F32), compiler_params=_params(2))(c_all, w_ada, b_sh)


def _ada_update(c_all, dmod_sh, w, m, v, name):
    def body(c_ref, d_ref, w_ref, m_ref, v_ref, g_ref, dl_ref, m2_ref, v2_ref):
        cv = c_ref[...]
        g = lax.dot_general(cv * _sigmoid(cv), d_ref[...], (((0,), (0,)), ((), ())), precision=HIGHEST,
                            preferred_element_type=F32)
        g_ref[...] = g
        dl_ref[...], m2_ref[...], v2_ref[...] = _adamw_math(w_ref[...], g, m_ref[...], v_ref[...])

    blk = pl.BlockSpec((None, ADA_ROWS, ADA_SHARD), lambda l, i: (l, i, 0))
    shape = jax.ShapeDtypeStruct((DEPTH, D_MODEL, ADA_SHARD), F32)
    return pl.pallas_call(
        body, name=name, grid=(DEPTH, D_MODEL // ADA_ROWS),
        in_specs=[pl.BlockSpec((8, ADA_ROWS), lambda l, i: (0, i)),
                  pl.BlockSpec((None, 8, ADA_SHARD), lambda l, i: (l, 0, 0)), blk, blk, blk],
        out_specs=(blk,) * 4, out_shape=(shape,) * 4, compiler_params=_params(2))(c_all, dmod_sh, w, m, v)


def _sum_devices(packs, name):
    def body(p_ref, o_ref):
        acc = p_ref[0]
        for dev in range(1, 8):
            acc = acc + p_ref[dev]
        o_ref[...] = acc

    return pl.pallas_call(body, name=name, out_shape=jax.ShapeDtypeStruct(packs.shape[1:], F32))(packs)


def _mesh_pos():
    return lax.axis_index("x"), lax.axis_index("y"), lax.axis_index("c")


def _flip(v, bit):
    return 1 - v if bit else v


def _allgather_devices(x, name):
    m_per, n = x.shape

    def body(x_ref, out_ref, send_sems, recv_sems, local_sem):
        ix, iy, ic = _mesh_pos()
        me, sibling = (ix, iy, ic), (ix, iy, 1 - ic)
        chips = [(1 - ix, iy), (ix, 1 - iy), (1 - ix, 1 - iy)]

        def rows(px, py, pc):
            return out_ref.at[pl.ds((4 * px + 2 * py + pc) * m_per, m_per), :]

        def copy(k, block, to, src=None):
            return pltpu.make_async_remote_copy(
                src_ref=rows(*block) if src is None else src, dst_ref=rows(*block),
                send_sem=send_sems.at[k], recv_sem=recv_sems.at[k], device_id=to, device_id_type=MESH)

        mine = pltpu.make_async_copy(x_ref, rows(*me), local_sem)
        mine.start()
        first = [copy(0, me, sibling, src=x_ref)]
        first += [copy(1 + j, me, (*chip, ic), src=x_ref) for j, chip in enumerate(chips)]
        for cp in first:
            cp.start()
        passed = [copy(4 + j, (*chip, ic), sibling) for j, chip in enumerate(chips)]
        for j, chip in enumerate(chips):
            copy(1 + j, (*chip, ic), me).wait_recv()
            passed[j].start()
        copy(0, sibling, me).wait_recv()
        for j, chip in enumerate(chips):
            copy(4 + j, (*chip, 1 - ic), me).wait_recv()
        for cp in first + passed:
            cp.wait_send()
        mine.wait()

    return pl.pallas_call(
        body, name=name, out_shape=jax.ShapeDtypeStruct((8 * m_per, n), x.dtype),
        in_specs=[pl.BlockSpec(memory_space=pltpu.VMEM)], out_specs=pl.BlockSpec(memory_space=pltpu.VMEM),
        scratch_shapes=[pltpu.SemaphoreType.DMA((7,)), pltpu.SemaphoreType.DMA((7,)), pltpu.SemaphoreType.DMA],
    )(x)


def _gather_chips(shards, name):
    n = len(shards)

    def body(*refs):
        ins, outs = refs[:n], refs[n:2 * n]
        send_sems, recv_sems, local_sems = refs[2 * n:]
        ix, iy, ic = _mesh_pos()
        me = 2 * ix + iy
        local = [pltpu.make_async_copy(ins[a], outs[a].at[me], local_sems.at[a]) for a in range(n)]
        for cp in local:
            cp.start()
        remote = []
        for a in range(n):
            for k in (1, 2, 3):
                px, py = _flip(ix, k & 2), _flip(iy, k & 1)
                sems = dict(send_sem=send_sems.at[3 * a + k - 1], recv_sem=recv_sems.at[3 * a + k - 1],
                            device_id=(px, py, ic), device_id_type=MESH)
                out_cp = pltpu.make_async_remote_copy(src_ref=ins[a], dst_ref=outs[a].at[me], **sems)
                in_cp = pltpu.make_async_remote_copy(src_ref=ins[a], dst_ref=outs[a].at[2 * px + py], **sems)
                out_cp.start()
                remote.append((out_cp, in_cp))
        for out_cp, in_cp in remote:
            out_cp.wait_send()
            in_cp.wait_recv()
        for cp in local:
            cp.wait()

    return pl.pallas_call(
        body, name=name, in_specs=[ANY] * n, out_specs=tuple([ANY] * n),
        out_shape=tuple(jax.ShapeDtypeStruct((4,) + t.shape, t.dtype) for t in shards),
        scratch_shapes=[pltpu.SemaphoreType.DMA((3 * n,)), pltpu.SemaphoreType.DMA((3 * n,)),
                        pltpu.SemaphoreType.DMA((n,))],
    )(*shards)


BIG_KINDS = (("w_in", "col", D_MODEL, IN_COLS), ("w_out", "row", D_MODEL, D_MODEL),
             ("w_up", "col", D_MODEL, 2 * D_FF), ("w_down", "row", D_FF, D_MODEL))


def _piece_shape(how, r, c):
    return (r // 2, c // 4) if how == "col" else (r // 8, c)


def _aligned(start, multiple):
    return start if isinstance(start, int) else pl.multiple_of(start, multiple)


def _piece(ref, how, r, c, chip, half):
    if how == "col":
        return ref.at[pl.ds(_aligned(half * (r // 2), 8), r // 2), pl.ds(_aligned(chip * (c // 4), LANES), c // 4)]
    n = r // 4
    return ref.at[pl.ds(_aligned(chip * n + half * (n // 2), 8), n // 2), :]


def _rs_pair_exchange(grads, name):
    nk = len(BIG_KINDS)
    flat = [grads[ki][l] for ki in range(nk) for l in range(DEPTH)]
    per = DEPTH * 4

    def body(*refs):
        g, land = refs[:nk * DEPTH], refs[nk * DEPTH:nk * DEPTH + nk]
        send_sems, recv_sems = refs[nk * DEPTH + nk:]
        ix, iy, ic = _mesh_pos()
        sibling = (ix, iy, 1 - ic)
        copies = []
        for ki, (_, how, r, c) in enumerate(BIG_KINDS):
            for l in range(DEPTH):
                for j in range(4):
                    sem = ki * per + l * 4 + j
                    rem = pltpu.make_async_remote_copy(
                        src_ref=_piece(g[ki * DEPTH + l], how, r, c, j, 1 - ic), dst_ref=land[ki].at[l, j],
                        send_sem=send_sems.at[sem], recv_sem=recv_sems.at[sem], device_id=sibling, device_id_type=MESH)
                    rem.start()
                    copies.append(rem)
        for rem in copies:
            rem.wait_send()
            rem.wait_recv()

    shapes = [jax.ShapeDtypeStruct((DEPTH, 4) + _piece_shape(how, r, c), F32) for _, how, r, c in BIG_KINDS]
    return pl.pallas_call(
        body, name=name, in_specs=[ANY] * len(flat), out_specs=tuple([ANY] * nk), out_shape=tuple(shapes),
        scratch_shapes=[pltpu.SemaphoreType.DMA((nk * per,))] * 2,
    )(*flat)


def _pair_sum(g, theirs, layer, how, core, name):
    r, c = g.shape
    pr, pc = _piece_shape(how, r, c)
    if how == "col":
        mine_spec = pl.BlockSpec((pr, pc), lambda j, core_ref: (core_ref[0], j))
    else:
        mine_spec = pl.BlockSpec((pr, pc), lambda j, core_ref: (2 * j + core_ref[0], 0))

    def body(core_ref, g_ref, t_ref, o_ref):
        o_ref[...] = g_ref[...] + t_ref[...]

    return pl.pallas_call(
        body, name=name,
        grid_spec=pltpu.PrefetchScalarGridSpec(
            num_scalar_prefetch=1, grid=(4,),
            in_specs=[mine_spec, pl.BlockSpec((None, None, pr, pc), lambda j, core_ref: (layer, j, 0, 0))],
            out_specs=pl.BlockSpec((None, pr, pc), lambda j, core_ref: (j, 0, 0))),
        out_shape=jax.ShapeDtypeStruct((4, pr, pc), F32), compiler_params=_params(1))(core, g, theirs)


def _rs_chip_exchange(pair_sums, name):
    nk = len(pair_sums)
    flat = [pair_sums[ki][l] for ki in range(nk) for l in range(DEPTH)]

    def body(*refs):
        src, dst = refs[:nk * DEPTH], refs[nk * DEPTH:nk * DEPTH + nk]
        send_sems, recv_sems = refs[nk * DEPTH + nk:]
        ix, iy, ic = _mesh_pos()
        copies = []
        for ki in range(nk):
            for l in range(DEPTH):
                for k in (1, 2, 3):
                    px, py = _flip(ix, k & 2), _flip(iy, k & 1)
                    sem = (ki * DEPTH + l) * 3 + k - 1
                    rem = pltpu.make_async_remote_copy(
                        src_ref=src[ki * DEPTH + l].at[2 * px + py], dst_ref=dst[ki].at[l, k - 1],
                        send_sem=send_sems.at[sem], recv_sem=recv_sems.at[sem], device_id=(px, py, ic), device_id_type=MESH)
                    rem.start()
                    copies.append(rem)
        for rem in copies:
            rem.wait_send()
            rem.wait_recv()

    return pl.pallas_call(
        body, name=name, in_specs=[ANY] * len(flat), out_specs=tuple([ANY] * nk),
        out_shape=tuple(jax.ShapeDtypeStruct((DEPTH, 3) + pair_sums[ki][0].shape[1:], F32) for ki in range(nk)),
        scratch_shapes=[pltpu.SemaphoreType.DMA((nk * DEPTH * 3,))] * 2,
    )(*flat)


def _chip_sum(own, others, layer, chip, name):
    _, pr, pc = own.shape

    def body(chip_ref, own_ref, s1, s2, s3, o_ref):
        o_ref[...] = ((own_ref[...] + s1[...]) + s2[...]) + s3[...]

    slot = lambda k: pl.BlockSpec((None, None, pr, pc), lambda i, chip_ref: (layer, k, 0, 0))
    return pl.pallas_call(
        body, name=name,
        grid_spec=pltpu.PrefetchScalarGridSpec(
            num_scalar_prefetch=1, grid=(1,),
            in_specs=[pl.BlockSpec((None, pr, pc), lambda i, chip_ref: (chip_ref[0], 0, 0)), slot(0), slot(1), slot(2)],
            out_specs=pl.BlockSpec((pr, pc), lambda i, chip_ref: (0, 0))),
        out_shape=jax.ShapeDtypeStruct((pr, pc), F32), compiler_params=_params(1))(chip, own, others, others, others)


def _rs_pair_share(halves, name):
    nk = len(halves)
    flat = [halves[ki][l] for ki in range(nk) for l in range(DEPTH)]

    def body(*refs):
        src, dst = refs[:nk * DEPTH], refs[nk * DEPTH:nk * DEPTH + nk]
        send_sems, recv_sems = refs[nk * DEPTH + nk:]
        ix, iy, ic = _mesh_pos()
        copies = []
        for ki in range(nk):
            for l in range(DEPTH):
                sem = ki * DEPTH + l
                rem = pltpu.make_async_remote_copy(
                    src_ref=src[sem], dst_ref=dst[ki].at[l], send_sem=send_sems.at[sem], recv_sem=recv_sems.at[sem],
                    device_id=(ix, iy, 1 - ic), device_id_type=MESH)
                rem.start()
                copies.append(rem)
        for rem in copies:
            rem.wait_send()
            rem.wait_recv()

    return pl.pallas_call(
        body, name=name, in_specs=[ANY] * len(flat), out_specs=tuple([ANY] * nk),
        out_shape=tuple(jax.ShapeDtypeStruct((DEPTH,) + halves[ki][0].shape, F32) for ki in range(nk)),
        scratch_shapes=[pltpu.SemaphoreType.DMA((nk * DEPTH,))] * 2,
    )(*flat)


def _adamw_halves(w, mine, theirs, m, v, core, name):
    nl, pr, pc = theirs.shape
    shape = w.shape
    view = lambda t: t.reshape(nl, 2, pr, pc)
    tr = _row_tile(pr, pc, 256 * 1024)

    def body(core_ref, w_ref, a0_ref, a1_ref, t_ref, m_ref, v_ref, g_ref, d_ref, m2_ref, v2_ref):
        own = jnp.where(pl.program_id(0) == 0, a0_ref[...], a1_ref[...])
        g = jnp.where(pl.program_id(1) == core_ref[0], own, t_ref[...])
        g_ref[...] = g
        d_ref[...], m2_ref[...], v2_ref[...] = _adamw_math(w_ref[...], g, m_ref[...], v_ref[...])

    blk = pl.BlockSpec((None, None, tr, pc), lambda l, h, i, core_ref: (l, h, i, 0))
    own_blk = pl.BlockSpec((tr, pc), lambda l, h, i, core_ref: (i, 0))
    out = jax.ShapeDtypeStruct((nl, 2, pr, pc), F32)
    outs = pl.pallas_call(
        body, name=name,
        grid_spec=pltpu.PrefetchScalarGridSpec(
            num_scalar_prefetch=1, grid=(nl, 2, pr // tr),
            in_specs=[blk, own_blk, own_blk, pl.BlockSpec((None, tr, pc), lambda l, h, i, core_ref: (l, i, 0)), blk, blk],
            out_specs=(blk,) * 4),
        out_shape=(out,) * 4, compiler_params=_params(3),
    )(core, view(w), mine[0], mine[1], theirs, view(m), view(v))
    return tuple(t.reshape(shape) for t in outs)


def _reduce_scatter_big(grads, core, chip):
    theirs = _rs_pair_exchange(grads, "rs_pair_exchange")
    pair_sums = [[_pair_sum(grads[ki][l], theirs[ki], l, how, core, f"rs_pair_sum_{kind}") for l in range(DEPTH)]
                 for ki, (kind, how, _, _) in enumerate(BIG_KINDS)]
    slots = _rs_chip_exchange(pair_sums, "rs_chip_exchange")
    halves = [[_chip_sum(pair_sums[ki][l], slots[ki], l, chip, f"rs_chip_sum_{kind}") for l in range(DEPTH)]
              for ki, (kind, _, _, _) in enumerate(BIG_KINDS)]
    other = _rs_pair_share(halves, "rs_pair_share")
    return list(zip(halves, other))


WEIGHT_NAMES = ("w_ada", "b_ada", "norm1_w", "w_in", "conv_a_w", "conv_a_b", "ln_a_w", "ln_a_b", "lb_gamma",
                "rec_norm_w", "w_out", "norm2_w", "w_up", "conv_f_w", "w_down", "final_norm_w")
SMALL_PARAMS = (("b_ada", (DEPTH, N_MOD * D_MODEL), None), ("norm1_w", (DEPTH, D_MODEL), None),
                ("conv_a_w", (DEPTH, CONV_WIDTH, CONV_CH), 2), ("conv_a_b", (DEPTH, CONV_CH), None),
                ("ln_a_w", (DEPTH, CONV_CH), None), ("ln_a_b", (DEPTH, CONV_CH), None),
                ("lb_gamma", (DEPTH, 2, REC_WIDTH), 2), ("rec_norm_w", (DEPTH, REC_WIDTH), None),
                ("norm2_w", (DEPTH, D_MODEL), None), ("conv_f_w", (DEPTH, 3, 2 * D_FF), 2),
                ("final_norm_w", (D_MODEL,), None))


def _pack_rows(parts):
    flat = jnp.concatenate([p.reshape(-1) for p in parts])
    total = flat.shape[0]
    padded = -(-total // (8 * LANES)) * (8 * LANES)
    return jnp.pad(flat, (0, padded - total)).reshape(padded // LANES, LANES)


def _unpack(flat, shapes):
    out, off = [], 0
    for shp in shapes:
        size = int(np.prod(shp))
        out.append(flat[off:off + size].reshape(shp))
        off += size
    return out


def _unstack_chips(t, axis):
    return jnp.concatenate([t[j] for j in range(4)], axis=axis)


def kernel(x, c, w_ada, b_ada, norm1_w, w_in, conv_a_w, conv_a_b, ln_a_w, ln_a_b, lb_gamma, rec_norm_w, w_out, norm2_w, w_up, conv_f_w, w_down, final_norm_w, loss_target, m_w_ada, m_b_ada, m_norm1_w, m_w_in, m_conv_a_w, m_conv_a_b, m_ln_a_w, m_ln_a_b, m_lb_gamma, m_rec_norm_w, m_w_out, m_norm2_w, m_w_up, m_conv_f_w, m_w_down, m_final_norm_w, v_w_ada, v_b_ada, v_norm1_w, v_w_in, v_conv_a_w, v_conv_a_b, v_ln_a_w, v_ln_a_b, v_lb_gamma, v_rec_norm_w, v_w_out, v_norm2_w, v_w_up, v_conv_f_w, v_w_down, v_final_norm_w):
    params = dict(zip(WEIGHT_NAMES, (w_ada, b_ada, norm1_w, w_in, conv_a_w, conv_a_b, ln_a_w, ln_a_b, lb_gamma,
                                     rec_norm_w, w_out, norm2_w, w_up, conv_f_w, w_down, final_norm_w)))
    mom1 = dict(zip(WEIGHT_NAMES, (m_w_ada, m_b_ada, m_norm1_w, m_w_in, m_conv_a_w, m_conv_a_b, m_ln_a_w, m_ln_a_b,
                                   m_lb_gamma, m_rec_norm_w, m_w_out, m_norm2_w, m_w_up, m_conv_f_w, m_w_down,
                                   m_final_norm_w)))
    mom2 = dict(zip(WEIGHT_NAMES, (v_w_ada, v_b_ada, v_norm1_w, v_w_in, v_conv_a_w, v_conv_a_b, v_ln_a_w, v_ln_a_b,
                                   v_lb_gamma, v_rec_norm_w, v_w_out, v_norm2_w, v_w_up, v_conv_f_w, v_w_down,
                                   v_final_norm_w)))
    ix, iy, ic = _mesh_pos()
    chip = 2 * ix + iy
    dev = 2 * chip + ic

    c_all = _allgather_devices(c.reshape(8, LANES), "gather_cond").reshape(8, D_MODEL)
    b_sh = lax.dynamic_slice_in_dim(b_ada, chip * ADA_SHARD, ADA_SHARD, axis=1)
    mod_sh = _ada_mod(c_all, w_ada, b_sh.reshape(DEPTH, 1, ADA_SHARD), "ada_mod")
    gathered = _gather_chips([mod_sh, w_in.astype(BF16), w_out.astype(BF16), w_up.astype(BF16), w_down.astype(BF16),
                              conv_a_w, conv_f_w, lb_gamma], "gather_weights")
    mod_mine = lax.dynamic_index_in_dim(gathered[0], dev, axis=2, keepdims=False)
    mods = [jnp.concatenate([mod_mine[j, l] for j in range(4)]).reshape(N_MOD, D_MODEL) for l in range(DEPTH)]
    w_in_f, w_up_f = _unstack_chips(gathered[1], 2), _unstack_chips(gathered[3], 2)
    w_out_f, w_down_f = _unstack_chips(gathered[2], 1), _unstack_chips(gathered[4], 1)
    conv_a_w_f, conv_f_w_f, gamma_f = (_unstack_chips(gathered[k], 2) for k in (5, 6, 7))

    lb1, p_soft = _lower_bounds(gamma_f.reshape(DEPTH, 2 * REC_WIDTH), "lower_bounds")
    lbs = [jnp.zeros((2, REC_WIDTH), F32), lb1.reshape(2, REC_WIDTH)]
    small, big = [], []
    for l in range(DEPTH):
        small.append(dict(norm1_w=norm1_w[l][None], conv_a_w=conv_a_w_f[l], conv_a_b=conv_a_b[l][None],
                          ln_a_w=ln_a_w[l][None], ln_a_b=ln_a_b[l][None], rec_norm_w=rec_norm_w[l],
                          norm2_w=norm2_w[l][None], conv_f_w=conv_f_w_f[l]))
        big.append(_split_layer_weights(w_in_f[l], w_out_f[l], w_up_f[l], w_down_f[l]))

    loss, dx, grads, dfw = _sequence_step(x[0], loss_target[0], mods, lbs, small, big, final_norm_w[None])
    loss = lax.psum(loss, ("x", "y", "c"))

    dgamma = _lower_bounds_bwd(grads[1]["lb"].reshape(1, 2 * REC_WIDTH), p_soft, "lower_bounds_bwd")
    dmod = [jnp.concatenate(grads[l]["mod"], axis=1) for l in range(DEPTH)]
    stack = lambda key: jnp.stack([grads[l][key] for l in range(DEPTH)])
    local_small = dict(b_ada=jnp.concatenate(dmod, axis=0), norm1_w=stack("norm1_w"), conv_a_w=stack("conv_a_w"),
                       conv_a_b=stack("conv_a_b"), ln_a_w=stack("ln_a_w"), ln_a_b=stack("ln_a_b"), lb_gamma=dgamma,
                       rec_norm_w=stack("rec_norm_w"), norm2_w=stack("norm2_w"), conv_f_w=stack("conv_f_w"),
                       final_norm_w=dfw)
    pack = _pack_rows([local_small[name] for name, _, _ in SMALL_PARAMS])
    rows = pack.shape[0]
    packs = _allgather_devices(pack, "gather_small_grads").reshape(8, rows, LANES)
    summed = _sum_devices(packs, "sum_small_grads").reshape(-1)
    small_grads = dict(zip([n for n, _, _ in SMALL_PARAMS], _unpack(summed, [shp for _, shp, _ in SMALL_PARAMS])))

    dmod_all = packs.reshape(8, rows * LANES)[:, :DEPTH * N_MOD * D_MODEL].reshape(8, DEPTH, N_MOD * D_MODEL)
    dmod_sh = lax.dynamic_slice_in_dim(dmod_all, chip * ADA_SHARD, ADA_SHARD, axis=2).transpose(1, 0, 2)
    g_ada, d_ada, m_ada, v_ada = _ada_update(c_all, dmod_sh, w_ada, m_w_ada, v_w_ada, "ada_update")

    for name, shp, axis in SMALL_PARAMS:
        if axis is not None:
            width = shp[axis] // 4
            small_grads[name] = lax.dynamic_slice_in_dim(small_grads[name], chip * width, width, axis=axis)
    names = [n for n, _, _ in SMALL_PARAMS]
    packed = [_pack_rows([src[n] for n in names])[None] for src in (params, small_grads, mom1, mom2)]
    small_out = _adamw(*packed, "adamw_small")
    shapes = [params[n].shape for n in names]
    small_delta, small_m, small_v = (dict(zip(names, _unpack(t.reshape(-1), shapes))) for t in small_out)

    core_id, chip_id = ic.astype(jnp.int32).reshape(1), chip.astype(jnp.int32).reshape(1)
    summed_big = _reduce_scatter_big([[grads[l][name] for l in range(DEPTH)] for name, _, _, _ in BIG_KINDS],
                                     core_id, chip_id)
    grad, delta, new_m, new_v = dict(small_grads), small_delta, small_m, small_v
    grad["w_ada"], delta["w_ada"], new_m["w_ada"], new_v["w_ada"] = g_ada, d_ada, m_ada, v_ada
    for (name, _, _, _), (mine, theirs) in zip(BIG_KINDS, summed_big):
        grad[name], delta[name], new_m[name], new_v[name] = _adamw_halves(
            params[name], mine, theirs, mom1[name], mom2[name], core_id, f"adamw_{name}")

    return (loss, dx[None], *[grad[n] for n in WEIGHT_NAMES], *[delta[n] for n in WEIGHT_NAMES],
            *[new_m[n] for n in WEIGHT_NAMES], *[new_v[n] for n in WEIGHT_NAMES])
```

```python
import numpy as np
import jax
import jax.numpy as jnp
from jax import lax
from jax.experimental import pallas as pl
from jax.experimental.pallas import tpu as pltpu

F32 = jnp.float32
BF16 = jnp.bfloat16

D_MODEL = 1024
DEPTH = 2
HEAD_DIM = 64
CONV_CH = 256
CONV_WIDTH = 31
ATT_WIDTH = 384
N_HEADS = 6
DILATIONS = (1, 4, 16)
ATT_HALF = 64
ATT_BLOCK = 128
ALIBI_MAX_EXP = 8.0
MASK_VALUE = -1e30
REC_WIDTH = 384
REC_CHUNK = 64
F_TINY = 1e-30
D_FF = 2816
N_MOD = 6
EPS = 1e-6
G_CONV = (0, 512)
G_QKV = (512, 1664)
G_REC = (1664, 3584)
IN_COLS = 3584

ADAM_LR = 0.001
ADAM_B1 = 0.9
ADAM_B2 = 0.999
ADAM_EPS = 1e-08
ADAM_WD = 0.01
ADAM_STEP = 10

VMEM_LIMIT_BYTES = 56 * 1024 * 1024
LANES = 128
MESH = pl.DeviceIdType.MESH
ANY = pl.BlockSpec(memory_space=pl.ANY)


def _params(n_axes):
    return pltpu.CompilerParams(dimension_semantics=("arbitrary",) * n_axes,
                                vmem_limit_bytes=VMEM_LIMIT_BYTES)


def _tile(n, target):
    best = None
    for t in range(LANES, min(n, target) + 1, LANES):
        if n % t == 0:
            best = t
    return best or n


def _sigmoid(x):
    return jax.nn.sigmoid(x)


def _silu_grad(x):
    s = _sigmoid(x)
    return s * (1.0 + x * (1.0 - s))


MM_ACC_ELEMS = 1024 * 1024


def _matmul(a, b, mode, out_dtype, name, tm=1024, tn=1792, tk=1792):
    if mode == "nn":
        (m, k), (k2, n) = a.shape, b.shape
    elif mode == "nt":
        (m, k), (n, k2) = a.shape, b.shape
    else:
        (k, m), (k2, n) = a.shape, b.shape
    assert k == k2, (a.shape, b.shape, mode)
    tn, tk = _tile(n, tn), _tile(k, tk)
    tm = _tile(m, min(tm, MM_ACC_ELEMS // tn))
    nk = k // tk
    a_spec = (pl.BlockSpec((tk, tm), lambda i, j, kk: (kk, i)) if mode == "tn"
              else pl.BlockSpec((tm, tk), lambda i, j, kk: (i, kk)))
    b_spec = (pl.BlockSpec((tn, tk), lambda i, j, kk: (j, kk)) if mode == "nt"
              else pl.BlockSpec((tk, tn), lambda i, j, kk: (kk, j)))
    dims = {"nn": (((1,), (0,)), ((), ())), "nt": (((1,), (1,)), ((), ())),
            "tn": (((0,), (0,)), ((), ()))}[mode]

    def body(a_ref, b_ref, o_ref, *scratch):
        part = lax.dot_general(a_ref[...].astype(BF16), b_ref[...].astype(BF16), dims, preferred_element_type=F32)
        if nk == 1:
            o_ref[...] = part.astype(out_dtype)
            return
        acc_ref, = scratch
        kk = pl.program_id(2)

        @pl.when(kk == 0)
        def _():
            acc_ref[...] = part

        @pl.when(kk > 0)
        def _():
            acc_ref[...] += part

        @pl.when(kk == nk - 1)
        def _():
            o_ref[...] = acc_ref[...].astype(out_dtype)

    return pl.pallas_call(
        body, name=name, grid=(m // tm, n // tn, nk),
        in_specs=[a_spec, b_spec],
        out_specs=pl.BlockSpec((tm, tn), lambda i, j, kk: (i, j)),
        out_shape=jax.ShapeDtypeStruct((m, n), out_dtype),
        scratch_shapes=[pltpu.VMEM((tm, tn), F32)] if nk > 1 else [],
        compiler_params=pltpu.CompilerParams(dimension_semantics=("parallel", "parallel", "arbitrary"),
                                             vmem_limit_bytes=VMEM_LIMIT_BYTES),
    )(a, b)


NORM_ROWS = 256


def _row_spec(width, rows=NORM_ROWS):
    return pl.BlockSpec((rows, width), lambda i: (i, 0))


def _vec_spec(width):
    return pl.BlockSpec((1, width), lambda i: (0, 0))


def _resid_norm_mod(x, r, g, nw, sc, sh, name):
    s, d = x.shape
    has_r = r is not None

    def body(*refs):
        if has_r:
            x_ref, r_ref, g_ref, nw_ref, sc_ref, sh_ref, xn_ref, h_ref = refs
            xn = x_ref[...] + g_ref[...] * r_ref[...]
            xn_ref[...] = xn
        else:
            x_ref, nw_ref, sc_ref, sh_ref, h_ref = refs
            xn = x_ref[...]
        rstd = lax.rsqrt(jnp.mean(xn * xn, axis=-1, keepdims=True) + EPS)
        y = xn * rstd * nw_ref[...]
        h_ref[...] = (y * (1.0 + sc_ref[...]) + sh_ref[...]).astype(BF16)

    if has_r:
        ins, in_specs = (x, r, g, nw, sc, sh), [_row_spec(d), _row_spec(d)] + [_vec_spec(d)] * 4
        out_shape = (jax.ShapeDtypeStruct((s, d), F32), jax.ShapeDtypeStruct((s, d), BF16))
        out_specs = (_row_spec(d), _row_spec(d))
    else:
        ins, in_specs = (x, nw, sc, sh), [_row_spec(d)] + [_vec_spec(d)] * 3
        out_shape = jax.ShapeDtypeStruct((s, d), BF16)
        out_specs = _row_spec(d)
    return pl.pallas_call(body, name=name, grid=(s // NORM_ROWS,), in_specs=in_specs, out_specs=out_specs,
                          out_shape=out_shape, compiler_params=_params(1))(*ins)


def _final_loss(x, r, g, fw, tgt, name):
    s, d = x.shape

    def body(x_ref, r_ref, g_ref, fw_ref, t_ref, loss_ref, dx_ref, dr_ref, dg_ref, dfw_ref):
        @pl.when(pl.program_id(0) == 0)
        def _():
            loss_ref[...] = jnp.zeros_like(loss_ref)
            dg_ref[...] = jnp.zeros_like(dg_ref)
            dfw_ref[...] = jnp.zeros_like(dfw_ref)

        rr = r_ref[...]
        gg = g_ref[...]
        xn = x_ref[...] + gg * rr
        rstd = lax.rsqrt(jnp.mean(xn * xn, axis=-1, keepdims=True) + EPS)
        xh = xn * rstd
        fwv = fw_ref[...]
        e = xh * fwv - t_ref[...]
        loss_ref[...] += 0.5 * jnp.sum(jnp.mean(e * e, axis=-1, keepdims=True), axis=0, keepdims=True)
        dy = e * (1.0 / d)
        dfw_ref[...] += jnp.sum(dy * xh, axis=0, keepdims=True)
        dxh = dy * fwv
        dx = rstd * (dxh - xh * jnp.mean(dxh * xh, axis=-1, keepdims=True))
        dx_ref[...] = dx
        dr_ref[...] = (gg * dx).astype(BF16)
        dg_ref[...] += jnp.sum(dx * rr, axis=0, keepdims=True)

    return pl.pallas_call(
        body, name=name, grid=(s // NORM_ROWS,),
        in_specs=[_row_spec(d), _row_spec(d), _vec_spec(d), _vec_spec(d), _row_spec(d)],
        out_specs=(_vec_spec(LANES), _row_spec(d), _row_spec(d), _vec_spec(d), _vec_spec(d)),
        out_shape=(jax.ShapeDtypeStruct((1, LANES), F32), jax.ShapeDtypeStruct((s, d), F32),
                   jax.ShapeDtypeStruct((s, d), BF16), jax.ShapeDtypeStruct((1, d), F32),
                   jax.ShapeDtypeStruct((1, d), F32)),
        compiler_params=_params(1))(x, r, g, fw, tgt)


def _norm_bwd(x, dhs, dxres, nw, sc, g, r, name):
    s, d = x.shape
    n_dh = len(dhs)
    has_g = g is not None

    def body(*refs):
        x_ref = refs[0]
        dh_refs = refs[1:1 + n_dh]
        dxres_ref, nw_ref, sc_ref = refs[1 + n_dh:4 + n_dh]
        pos = 4 + n_dh
        if has_g:
            g_ref, r_ref = refs[pos:pos + 2]
            pos += 2
            dx_ref, dr_ref, dsh_ref, dsc_ref, dnw_ref, dg_ref = refs[pos:]
            accs = (dsh_ref, dsc_ref, dnw_ref, dg_ref)
        else:
            dx_ref, dsh_ref, dsc_ref, dnw_ref = refs[pos:]
            accs = (dsh_ref, dsc_ref, dnw_ref)

        @pl.when(pl.program_id(0) == 0)
        def _():
            for acc in accs:
                acc[...] = jnp.zeros_like(acc)

        xv = x_ref[...]
        dh = dh_refs[0][...]
        for extra in dh_refs[1:]:
            dh = dh + extra[...]
        rstd = lax.rsqrt(jnp.mean(xv * xv, axis=-1, keepdims=True) + EPS)
        xh = xv * rstd
        nwv = nw_ref[...]
        dsh_ref[...] += jnp.sum(dh, axis=0, keepdims=True)
        dsc_ref[...] += jnp.sum(dh * (xh * nwv), axis=0, keepdims=True)
        dy = dh * (1.0 + sc_ref[...])
        dnw_ref[...] += jnp.sum(dy * xh, axis=0, keepdims=True)
        dxh = dy * nwv
        dx = dxres_ref[...] + rstd * (dxh - xh * jnp.mean(dxh * xh, axis=-1, keepdims=True))
        dx_ref[...] = dx
        if has_g:
            dr_ref[...] = (g_ref[...] * dx).astype(BF16)
            dg_ref[...] += jnp.sum(dx * r_ref[...], axis=0, keepdims=True)

    ins = [x, *dhs, dxres, nw, sc]
    in_specs = [_row_spec(d)] * (2 + n_dh) + [_vec_spec(d)] * 2
    out_shape = [jax.ShapeDtypeStruct((s, d), F32)]
    out_specs = [_row_spec(d)]
    if has_g:
        ins += [g, r]
        in_specs += [_vec_spec(d), _row_spec(d)]
        out_shape.append(jax.ShapeDtypeStruct((s, d), BF16))
        out_specs.append(_row_spec(d))
    n_vec = 4 if has_g else 3
    out_shape += [jax.ShapeDtypeStruct((1, d), F32)] * n_vec
    out_specs += [_vec_spec(d)] * n_vec
    return pl.pallas_call(body, name=name, grid=(s // NORM_ROWS,), in_specs=in_specs, out_specs=tuple(out_specs),
                          out_shape=tuple(out_shape), compiler_params=_params(1))(*ins)


FFN_ROWS = 256
FFN_COLS = 1408
HALO = 8
INV_SQRT2 = 0.7071067811865476
INV_SQRT_2PI = 0.3989422804014327


def _gelu(x):
    return 0.5 * x * (1.0 + lax.erf(x * INV_SQRT2))


def _gelu_grad(x):
    return 0.5 * (1.0 + lax.erf(x * INV_SQRT2)) + x * (INV_SQRT_2PI * jnp.exp(-0.5 * x * x))


def _halo_specs(rows, cols, halo, n_rows_total, col_of):
    per = rows // halo
    last = n_rows_total // halo - 1
    cur = pl.BlockSpec((rows, cols), lambda j, i: (i, col_of(j)))
    prev = pl.BlockSpec((halo, cols), lambda j, i: (jnp.maximum(i * per - 1, 0), col_of(j)))
    nxt = pl.BlockSpec((halo, cols), lambda j, i: (jnp.minimum((i + 1) * per, last), col_of(j)))
    return [prev, cur, nxt]


def _shift_rows(x, k):
    n = x.shape[0]
    return pltpu.roll(x, k % n, axis=0)


def _conv3(ext, w):
    return w[0:1, :] * _shift_rows(ext, 1) + w[1:2, :] * ext + w[2:3, :] * _shift_rows(ext, -1)


def _ext_block(prev_ref, cur_ref, next_ref, i, n_i):
    prev = jnp.where(i > 0, prev_ref[...], 0.0)
    nxt = jnp.where(i < n_i - 1, next_ref[...], 0.0)
    return jnp.concatenate([prev, cur_ref[...], nxt], axis=0)


def _ffn_act(u, cw, name):
    s = u.shape[0]
    nc, ns = D_FF // FFN_COLS, s // FFN_ROWS

    def body(gp, gc, gn, vp, vc, vn, wg_ref, wv_ref, o_ref):
        i = pl.program_id(1)
        cg = _conv3(_ext_block(gp, gc, gn, i, ns), wg_ref[...])[HALO:HALO + FFN_ROWS]
        cv = _conv3(_ext_block(vp, vc, vn, i, ns), wv_ref[...])[HALO:HALO + FFN_ROWS]
        o_ref[...] = (_gelu(cg) * cv).astype(BF16)

    in_specs = (_halo_specs(FFN_ROWS, FFN_COLS, HALO, s, lambda j: j)
                + _halo_specs(FFN_ROWS, FFN_COLS, HALO, s, lambda j: j + nc)
                + [pl.BlockSpec((3, FFN_COLS), lambda j, i: (0, j)),
                   pl.BlockSpec((3, FFN_COLS), lambda j, i: (0, j + nc))])
    return pl.pallas_call(
        body, name=name, grid=(nc, ns), in_specs=in_specs,
        out_specs=pl.BlockSpec((FFN_ROWS, FFN_COLS), lambda j, i: (i, j)),
        out_shape=jax.ShapeDtypeStruct((s, D_FF), BF16), compiler_params=_params(2),
    )(u, u, u, u, u, u, cw, cw)


def _ffn_act_bwd(u, dact, cw, name):
    s = u.shape[0]
    nc, ns = D_FF // FFN_COLS, s // FFN_ROWS

    def body(gp, gc, gn, vp, vc, vn, dp, dc, dn, wg_ref, wv_ref, dug_ref, duv_ref, dwg_ref, dwv_ref):
        i = pl.program_id(1)

        @pl.when(i == 0)
        def _():
            dwg_ref[...] = jnp.zeros_like(dwg_ref)
            dwv_ref[...] = jnp.zeros_like(dwv_ref)

        ug = _ext_block(gp, gc, gn, i, ns)
        uv = _ext_block(vp, vc, vn, i, ns)
        da = _ext_block(dp, dc, dn, i, ns)
        wg, wv = wg_ref[...], wv_ref[...]
        cg, cv = _conv3(ug, wg), _conv3(uv, wv)
        dcg = da * cv * _gelu_grad(cg)
        dcv = da * _gelu(cg)
        inner = slice(HALO, HALO + FFN_ROWS)
        for d_c, uu, w, du_ref, dw_ref in ((dcg, ug, wg, dug_ref, dwg_ref), (dcv, uv, wv, duv_ref, dwv_ref)):
            du = w[0:1, :] * _shift_rows(d_c, -1) + w[1:2, :] * d_c + w[2:3, :] * _shift_rows(d_c, 1)
            du_ref[...] = du[inner].astype(BF16)
            d_in = d_c[inner]
            for tap in range(3):
                dw_ref[tap:tap + 1, :] += jnp.sum(d_in * _shift_rows(uu, 1 - tap)[inner], axis=0, keepdims=True)

    in_specs = (_halo_specs(FFN_ROWS, FFN_COLS, HALO, s, lambda j: j)
                + _halo_specs(FFN_ROWS, FFN_COLS, HALO, s, lambda j: j + nc)
                + _halo_specs(FFN_ROWS, FFN_COLS, HALO, s, lambda j: j)
                + [pl.BlockSpec((3, FFN_COLS), lambda j, i: (0, j)),
                   pl.BlockSpec((3, FFN_COLS), lambda j, i: (0, j + nc))])
    blk = pl.BlockSpec((FFN_ROWS, FFN_COLS), lambda j, i: (i, j))
    acc = pl.BlockSpec((HALO, FFN_COLS), lambda j, i: (0, j))
    return pl.pallas_call(
        body, name=name, grid=(nc, ns), in_specs=in_specs, out_specs=(blk, blk, acc, acc),
        out_shape=(jax.ShapeDtypeStruct((s, D_FF), BF16), jax.ShapeDtypeStruct((s, D_FF), BF16),
                   jax.ShapeDtypeStruct((HALO, D_FF), F32), jax.ShapeDtypeStruct((HALO, D_FF), F32)),
        compiler_params=_params(2),
    )(u, u, u, u, u, u, dact, dact, dact, cw, cw)


CONV_ROWS = 512
CONV_HALO = 16
CONV_PAD = CONV_WIDTH // 2


def _conv_halo_specs(cols, s):
    per = CONV_ROWS // CONV_HALO
    last = s // CONV_HALO - 1
    return [pl.BlockSpec((CONV_HALO, cols), lambda i: (jnp.maximum(i * per - 1, 0), 0)),
            pl.BlockSpec((CONV_ROWS, cols), lambda i: (i, 0)),
            pl.BlockSpec((CONV_HALO, cols), lambda i: (jnp.minimum((i + 1) * per, last), 0))]


def _glu_ext(pp, pc, pn, i, n_i):
    ext = _ext_block(pp, pc, pn, i, n_i)
    return ext[:, :CONV_CH] * _sigmoid(ext[:, CONV_CH:])


def _conv_mixer(pa, cw, cb, lnw, lnb, name):
    s = pa.shape[0]
    ns = s // CONV_ROWS

    def body(pp, pc, pn, cw_ref, cb_ref, lnw_ref, lnb_ref, o_ref, c_ref):
        i = pl.program_id(0)
        a = _glu_ext(pp, pc, pn, i, ns)
        acc = jnp.zeros((CONV_ROWS, CONV_CH), F32)
        for tap in range(CONV_WIDTH):
            acc = acc + cw_ref[tap:tap + 1, :] * _shift_rows(a, -(tap + 1))[:CONV_ROWS]
        cv = acc + cb_ref[...]
        c_ref[...] = cv
        mu = jnp.mean(cv, axis=-1, keepdims=True)
        xc = cv - mu
        rstd = lax.rsqrt(jnp.mean(xc * xc, axis=-1, keepdims=True) + EPS)
        y = xc * rstd * lnw_ref[...] + lnb_ref[...]
        o_ref[...] = (y * _sigmoid(y)).astype(BF16)

    vec = pl.BlockSpec((1, CONV_CH), lambda i: (0, 0))
    blk = pl.BlockSpec((CONV_ROWS, CONV_CH), lambda i: (i, 0))
    return pl.pallas_call(
        body, name=name, grid=(ns,),
        in_specs=_conv_halo_specs(2 * CONV_CH, s) + [pl.BlockSpec((CONV_WIDTH, CONV_CH), lambda i: (0, 0)), vec, vec, vec],
        out_specs=(blk, blk),
        out_shape=(jax.ShapeDtypeStruct((s, CONV_CH), BF16), jax.ShapeDtypeStruct((s, CONV_CH), F32)),
        compiler_params=_params(1))(pa, pa, pa, cw, cb, lnw, lnb)


def _conv_mixer_bwd_ln(cv, dout, lnw, lnb, name):
    s = cv.shape[0]

    def body(c_ref, do_ref, lnw_ref, lnb_ref, dc_ref, dlnw_ref, dlnb_ref, dcb_ref):
        @pl.when(pl.program_id(0) == 0)
        def _():
            dlnw_ref[...] = jnp.zeros_like(dlnw_ref)
            dlnb_ref[...] = jnp.zeros_like(dlnb_ref)
            dcb_ref[...] = jnp.zeros_like(dcb_ref)

        c = c_ref[...]
        mu = jnp.mean(c, axis=-1, keepdims=True)
        xc = c - mu
        rstd = lax.rsqrt(jnp.mean(xc * xc, axis=-1, keepdims=True) + EPS)
        xh = xc * rstd
        w = lnw_ref[...]
        y = xh * w + lnb_ref[...]
        dy = do_ref[...] * _silu_grad(y)
        dlnw_ref[...] += jnp.sum(dy * xh, axis=0, keepdims=True)
        dlnb_ref[...] += jnp.sum(dy, axis=0, keepdims=True)
        dxh = dy * w
        dc = rstd * (dxh - jnp.mean(dxh, axis=-1, keepdims=True) - xh * jnp.mean(dxh * xh, axis=-1, keepdims=True))
        dc_ref[...] = dc
        dcb_ref[...] += jnp.sum(dc, axis=0, keepdims=True)

    vec = pl.BlockSpec((1, CONV_CH), lambda i: (0, 0))
    blk = pl.BlockSpec((CONV_ROWS, CONV_CH), lambda i: (i, 0))
    return pl.pallas_call(
        body, name=name, grid=(s // CONV_ROWS,), in_specs=[blk, blk, vec, vec], out_specs=(blk, vec, vec, vec),
        out_shape=(jax.ShapeDtypeStruct((s, CONV_CH), F32),) + (jax.ShapeDtypeStruct((1, CONV_CH), F32),) * 3,
        compiler_params=_params(1))(cv, dout, lnw, lnb)


def _conv_mixer_bwd_conv(pa, dc, cw, name):
    s = pa.shape[0]
    ns = s // CONV_ROWS

    def body(pp, pc, pn, dp, dcc, dn, cw_ref, dpa_ref, dcw_ref):
        i = pl.program_id(0)

        @pl.when(i == 0)
        def _():
            dcw_ref[...] = jnp.zeros_like(dcw_ref)

        a = _glu_ext(pp, pc, pn, i, ns)
        dce = _ext_block(dp, dcc, dn, i, ns)
        dcur = dcc[...]
        da = jnp.zeros((CONV_ROWS, CONV_CH), F32)
        for tap in range(CONV_WIDTH):
            da = da + cw_ref[tap:tap + 1, :] * _shift_rows(dce, -(CONV_WIDTH - tap))[:CONV_ROWS]
            dcw_ref[tap:tap + 1, :] += jnp.sum(dcur * _shift_rows(a, -(tap + 1))[:CONV_ROWS], axis=0, keepdims=True)
        cur = pc[...]
        val, sg = cur[:, :CONV_CH], _sigmoid(cur[:, CONV_CH:])
        dpa_ref[:, :CONV_CH] = (da * sg).astype(BF16)
        dpa_ref[:, CONV_CH:] = (da * val * sg * (1.0 - sg)).astype(BF16)

    return pl.pallas_call(
        body, name=name, grid=(ns,),
        in_specs=_conv_halo_specs(2 * CONV_CH, s) + _conv_halo_specs(CONV_CH, s)
        + [pl.BlockSpec((CONV_WIDTH, CONV_CH), lambda i: (0, 0))],
        out_specs=(pl.BlockSpec((CONV_ROWS, 2 * CONV_CH), lambda i: (i, 0)),
                   pl.BlockSpec((32, CONV_CH), lambda i: (0, 0))),
        out_shape=(jax.ShapeDtypeStruct((s, 2 * CONV_CH), BF16), jax.ShapeDtypeStruct((32, CONV_CH), F32)),
        compiler_params=_params(1))(pa, pa, pa, dc, dc, dc, cw)


SLOPES = tuple(float(2.0 ** (-ALIBI_MAX_EXP * (h + 1) / N_HEADS)) for h in range(N_HEADS))
ATT_SCALE = HEAD_DIM ** -0.5


def _band_specs(n_blocks, col_of):
    return [pl.BlockSpec((ATT_BLOCK, ATT_WIDTH), lambda r, i: (jnp.maximum(i - 1, 0), col_of(r))),
            pl.BlockSpec((ATT_BLOCK, ATT_WIDTH), lambda r, i: (i, col_of(r))),
            pl.BlockSpec((ATT_BLOCK, ATT_WIDTH), lambda r, i: (jnp.minimum(i + 1, n_blocks - 1), col_of(r)))]


def _win(refs, sl):
    return jnp.concatenate([ref[:, sl] for ref in refs], axis=0)


def _attn_branch(qkv, dil, name):
    s = qkv.shape[0]
    length = s // dil
    nb = length // ATT_BLOCK
    view = qkv.reshape(length, dil * 3 * ATT_WIDTH)

    def body(q_ref, kp, kc, kn, vp, vc, vn, o_ref, l_ref):
        i = pl.program_id(1)
        row = lax.broadcasted_iota(jnp.int32, (ATT_BLOCK, 3 * ATT_BLOCK), 0)
        col = lax.broadcasted_iota(jnp.int32, (ATT_BLOCK, 3 * ATT_BLOCK), 1)
        kpos = (i - 1) * ATT_BLOCK + col
        dist = jnp.abs(i * ATT_BLOCK + row - kpos)
        valid = (dist <= ATT_HALF) & (kpos >= 0) & (kpos < length)
        distf = dist.astype(F32) * float(dil)
        for h in range(N_HEADS):
            sl = slice(h * HEAD_DIM, (h + 1) * HEAD_DIM)
            sc = lax.dot_general(q_ref[:, sl], _win((kp, kc, kn), sl), (((1,), (1,)), ((), ())),
                                 preferred_element_type=F32) * ATT_SCALE - SLOPES[h] * distf
            sc = jnp.where(valid, sc, MASK_VALUE)
            m = jnp.max(sc, axis=-1, keepdims=True)
            p = jnp.exp(sc - m)
            den = jnp.sum(p, axis=-1, keepdims=True)
            o = jnp.dot(p.astype(BF16), _win((vp, vc, vn), sl), preferred_element_type=F32) / den
            o_ref[:, sl] = o
            l_ref[:, sl] = jnp.broadcast_to(m + jnp.log(den), (ATT_BLOCK, HEAD_DIM))

    out_blk = pl.BlockSpec((ATT_BLOCK, ATT_WIDTH), lambda r, i: (i, r))
    o, lse = pl.pallas_call(
        body, name=name, grid=(dil, nb),
        in_specs=[pl.BlockSpec((ATT_BLOCK, ATT_WIDTH), lambda r, i: (i, 3 * r))]
        + _band_specs(nb, lambda r: 3 * r + 1) + _band_specs(nb, lambda r: 3 * r + 2),
        out_specs=(out_blk, out_blk),
        out_shape=(jax.ShapeDtypeStruct((length, dil * ATT_WIDTH), F32),) * 2,
        compiler_params=_params(2))(view, view, view, view, view, view, view)
    return o.reshape(s, ATT_WIDTH), lse.reshape(s, ATT_WIDTH)


ATT_ROWS = 512


def _attn_combine(outs, lses, name):
    s = outs[0].shape[0]

    def body(o1, o2, o3, l1, l2, l3, att_ref, att32_ref, lse_ref):
        ls = [l1[...], l2[...], l3[...]]
        m = jnp.maximum(jnp.maximum(ls[0], ls[1]), ls[2])
        es = [jnp.exp(l - m) for l in ls]
        den = es[0] + es[1] + es[2]
        att = (es[0] * o1[...] + es[1] * o2[...] + es[2] * o3[...]) / den
        att_ref[...] = att.astype(BF16)
        att32_ref[...] = att
        lse_ref[...] = m + jnp.log(den)

    blk = pl.BlockSpec((ATT_ROWS, ATT_WIDTH), lambda i: (i, 0))
    return pl.pallas_call(
        body, name=name, grid=(s // ATT_ROWS,), in_specs=[blk] * 6, out_specs=(blk, blk, blk),
        out_shape=(jax.ShapeDtypeStruct((s, ATT_WIDTH), BF16), jax.ShapeDtypeStruct((s, ATT_WIDTH), F32),
                   jax.ShapeDtypeStruct((s, ATT_WIDTH), F32)),
        compiler_params=_params(1))(*outs, *lses)


def _attn_delta(datt, att, name):
    s = att.shape[0]

    def body(d_ref, a_ref, delta_ref, dbf_ref):
        prod = d_ref[...] * a_ref[...]
        for h in range(N_HEADS):
            sl = slice(h * HEAD_DIM, (h + 1) * HEAD_DIM)
            delta_ref[:, sl] = jnp.broadcast_to(jnp.sum(prod[:, sl], axis=-1, keepdims=True), (ATT_ROWS, HEAD_DIM))
        dbf_ref[...] = d_ref[...].astype(BF16)

    blk = pl.BlockSpec((ATT_ROWS, ATT_WIDTH), lambda i: (i, 0))
    return pl.pallas_call(
        body, name=name, grid=(s // ATT_ROWS,), in_specs=[blk, blk], out_specs=(blk, blk),
        out_shape=(jax.ShapeDtypeStruct((s, ATT_WIDTH), F32), jax.ShapeDtypeStruct((s, ATT_WIDTH), BF16)),
        compiler_params=_params(1))(datt, att)


def _attn_branch_bwd(qkv, do, lse, delta, prev, dil, name):
    s = qkv.shape[0]
    length = s // dil
    nb = length // ATT_BLOCK
    view = qkv.reshape(length, dil * 3 * ATT_WIDTH)
    sub = lambda t: t.reshape(length, dil * ATT_WIDTH)
    has_prev = prev is not None
    tn = (((0,), (0,)), ((), ()))
    nt = (((1,), (1,)), ((), ()))

    def body(*refs):
        qs, ks, vs, dos, ls, des = (refs[3 * n:3 * n + 3] for n in range(6))
        rest = refs[18:]
        if has_prev:
            pq, pk, pv = rest[:3]
            rest = rest[3:]
        dq_ref, dk_ref, dv_ref = rest
        i = pl.program_id(1)
        row = lax.broadcasted_iota(jnp.int32, (ATT_BLOCK, 3 * ATT_BLOCK), 0)
        col = lax.broadcasted_iota(jnp.int32, (ATT_BLOCK, 3 * ATT_BLOCK), 1)
        kpos = (i - 1) * ATT_BLOCK + col
        dist_q = jnp.abs(i * ATT_BLOCK + row - kpos)
        valid_q = (dist_q <= ATT_HALF) & (kpos >= 0) & (kpos < length)
        distf_q = dist_q.astype(F32) * float(dil)
        rowk = lax.broadcasted_iota(jnp.int32, (3 * ATT_BLOCK, ATT_BLOCK), 0)
        colk = lax.broadcasted_iota(jnp.int32, (3 * ATT_BLOCK, ATT_BLOCK), 1)
        qpos = (i - 1) * ATT_BLOCK + rowk
        dist_k = jnp.abs(qpos - (i * ATT_BLOCK + colk))
        valid_k = (dist_k <= ATT_HALF) & (qpos >= 0) & (qpos < length)
        distf_k = dist_k.astype(F32) * float(dil)
        for h in range(N_HEADS):
            sl = slice(h * HEAD_DIM, (h + 1) * HEAD_DIM)
            one = slice(h * HEAD_DIM, h * HEAD_DIM + 1)
            k_win, v_win = _win(ks, sl), _win(vs, sl)
            q_cur, do_cur = qs[1][:, sl], dos[1][:, sl]
            sc = lax.dot_general(q_cur, k_win, nt, preferred_element_type=F32) * ATT_SCALE - SLOPES[h] * distf_q
            p = jnp.exp(jnp.where(valid_q, sc - ls[1][:, one], MASK_VALUE))
            dp = lax.dot_general(do_cur, v_win, nt, preferred_element_type=F32)
            ds = (p * (dp - des[1][:, one]) * ATT_SCALE).astype(BF16)
            dq = jnp.dot(ds, k_win, preferred_element_type=F32)

            q_win, do_win = _win(qs, sl), _win(dos, sl)
            k_cur, v_cur = ks[1][:, sl], vs[1][:, sl]
            sc2 = lax.dot_general(q_win, k_cur, nt, preferred_element_type=F32) * ATT_SCALE - SLOPES[h] * distf_k
            p2 = jnp.exp(jnp.where(valid_k, sc2 - _win(ls, one), MASK_VALUE))
            dv = lax.dot_general(p2.astype(BF16), do_win, tn, preferred_element_type=F32)
            dp2 = lax.dot_general(do_win, v_cur, nt, preferred_element_type=F32)
            ds2 = (p2 * (dp2 - _win(des, one)) * ATT_SCALE).astype(BF16)
            dk = lax.dot_general(ds2, q_win, tn, preferred_element_type=F32)
            if has_prev:
                dq, dk, dv = dq + pq[:, sl], dk + pk[:, sl], dv + pv[:, sl]
            dq_ref[:, sl], dk_ref[:, sl], dv_ref[:, sl] = dq, dk, dv

    blk = pl.BlockSpec((ATT_BLOCK, ATT_WIDTH), lambda r, i: (i, r))
    in_specs = (_band_specs(nb, lambda r: 3 * r) + _band_specs(nb, lambda r: 3 * r + 1)
                + _band_specs(nb, lambda r: 3 * r + 2) + _band_specs(nb, lambda r: r) * 3)
    ins = [view] * 9 + [sub(do)] * 3 + [sub(lse)] * 3 + [sub(delta)] * 3
    if has_prev:
        in_specs += [blk] * 3
        ins += [sub(t) for t in prev]
    outs = pl.pallas_call(
        body, name=name, grid=(dil, nb), in_specs=in_specs, out_specs=(blk, blk, blk),
        out_shape=(jax.ShapeDtypeStruct((length, dil * ATT_WIDTH), F32),) * 3,
        compiler_params=_params(2))(*ins)
    return tuple(t.reshape(s, ATT_WIDTH) for t in outs)


TB = 2 * REC_CHUNK
REC_SUB = 16
REC_ROWS = 5 * REC_WIDTH


def _chunk_scan(x, pos, rev):
    del pos
    row = lax.broadcasted_iota(jnp.int32, (TB, TB), 0)
    col = lax.broadcasted_iota(jnp.int32, (TB, TB), 1)
    same = (row < REC_CHUNK) == (col < REC_CHUNK)
    tri = (same & ((row >= col) if rev else (row <= col))).astype(BF16)
    hi = x.astype(BF16)
    rest = x - hi.astype(F32)
    mid = rest.astype(BF16)
    low = (rest - mid.astype(F32)).astype(BF16)
    return (jnp.dot(hi, tri, preferred_element_type=F32) + jnp.dot(mid, tri, preferred_element_type=F32)
            + jnp.dot(low, tri, preferred_element_type=F32))


def _hg_prep(qraw, z, lb, rev):
    lane = lax.broadcasted_iota(jnp.int32, (REC_WIDTH, TB), 1)
    pos = lane & (REC_CHUNK - 1)
    in_a = lane < REC_CHUNK
    sig, sigm = _sigmoid(z), _sigmoid(-z)
    f = lb + (1.0 - lb) * sig
    kk = (1.0 - lb) * sigm
    b = _chunk_scan(jnp.log(jnp.maximum(f, F_TINY)), pos, rev)
    end_a = b[:, 0:1] if rev else b[:, REC_CHUNK - 1:REC_CHUNK]
    end_b = b[:, REC_CHUNK:REC_CHUNK + 1] if rev else b[:, TB - 1:TB]
    bend = jnp.where(in_a, end_a, end_b)
    q = qraw * _sigmoid(qraw)
    sub = ((REC_CHUNK - 1 - pos) if rev else pos) // REC_SUB
    eq, ek = [], []
    for i in range(1, REC_CHUNK // REC_SUB):
        la = (REC_CHUNK - REC_SUB * i) if rev else (REC_SUB * i - 1)
        ri = jnp.where(in_a, b[:, la:la + 1], b[:, la + REC_CHUNK:la + REC_CHUNK + 1])
        eq.append(jnp.where(sub == i, jnp.exp(jnp.minimum(b - ri, 0.0)), 0.0))
        ek.append(jnp.where(sub < i, jnp.exp(jnp.minimum(ri - b, 0.0)), 0.0))
    return dict(pos=pos, in_a=in_a, sig=sig, sigm=sigm, f=f, kk=kk, b=b, end_a=end_a, end_b=end_b,
                q=q, qh=q * jnp.exp(b), kh=kk * jnp.exp(bend - b), ekb=jnp.exp(bend - b), eq=eq, ek=ek,
                pos_sub=lane & (REC_SUB - 1))


def _pair_masks(rev):
    row = lax.broadcasted_iota(jnp.int32, (TB, TB), 0)
    col = lax.broadcasted_iota(jnp.int32, (TB, TB), 1)
    same = (row < REC_CHUNK) == (col < REC_CHUNK)
    scan = lambda p: ((REC_CHUNK - 1 - (p & (REC_CHUNK - 1))) if rev else (p & (REC_CHUNK - 1))) // REC_SUB
    causal = same & ((row >= col) if rev else (row <= col))
    earlier_sub = same & (scan(row) < scan(col))
    return same, causal, earlier_sub, col - row


def _head_rows(x, h):
    return x[h * HEAD_DIM:(h + 1) * HEAD_DIM, :]


def _stack_subs(parts, h):
    return jnp.concatenate([_head_rows(p, h) for p in parts], axis=0)


def _block_diag_mask():
    r = lax.broadcasted_iota(jnp.int32, (REC_WIDTH, REC_WIDTH), 0) // HEAD_DIM
    c = lax.broadcasted_iota(jnp.int32, (REC_WIDTH, REC_WIDTH), 1) // HEAD_DIM
    return (r == c).astype(F32)


def _heads(x):
    return x.reshape(N_HEADS, HEAD_DIM, TB)


def _hg_shift(delta, rev):
    return jnp.where(delta == 0, 0, TB - delta) if rev else delta


def _hg_unshift(delta, rev):
    return delta if rev else jnp.where(delta == 0, 0, TB - delta)


def _hgrn_scan(projt, lb, rev, name):
    s = projt.shape[1]
    nblk = s // TB
    zrow = 2 if rev else 1
    tmap = (lambda i: nblk - 1 - i) if rev else (lambda i: i)
    tn = (((0,), (0,)), ((), ()))
    nt = (((1,), (1,)), ((), ()))

    def body(q_ref, z_ref, v_ref, lb_ref, o_ref, hs_ref, at_ref, h_ref, acc_ref):
        @pl.when(pl.program_id(0) == 0)
        def _():
            h_ref[...] = jnp.zeros_like(h_ref)

        v = v_ref[...]
        vb = v.astype(BF16)
        pr = _hg_prep(q_ref[...], z_ref[...], lb_ref[...], rev)
        q, kk, b, pos_sub = pr["q"], pr["kk"], pr["b"], pr["pos_sub"]
        same, _, _, offset = _pair_masks(rev)
        qt = [q * e for e in pr["eq"]]
        kt = [kk * e for e in pr["ek"]]
        for h in range(N_HEADS):
            sc = lax.dot_general(_stack_subs(kt, h).astype(BF16), _stack_subs(qt, h).astype(BF16), tn,
                                 preferred_element_type=F32)
            acc_ref[h] = jnp.where(same, sc, 0.0)

        def pair_step(delta, carry):
            sh = _hg_shift(delta, rev)
            kd, bd = pltpu.roll(kk, sh, axis=1), pltpu.roll(b, sh, axis=1)
            valid = (pos_sub <= REC_SUB - 1 - delta) if rev else (pos_sub >= delta)
            w = jnp.where(valid, q * kd * jnp.exp(jnp.where(valid, b - bd, 0.0)), 0.0)
            a = jnp.sum(_heads(w), axis=1)
            hit = offset == (-delta if rev else delta)
            for h in range(N_HEADS):
                acc_ref[h] += jnp.where(hit, a[h:h + 1, :], 0.0)
            return carry

        lax.fori_loop(0, REC_SUB, pair_step, 0)
        outs = []
        for h in range(N_HEADS):
            a_bf = acc_ref[h].astype(BF16)
            at_ref[h] = a_bf
            outs.append(jnp.dot(_head_rows(vb, h), a_bf, preferred_element_type=F32))
        o = jnp.concatenate(outs, axis=0)
        bd_mask = _block_diag_mask()
        order = ((1, ~pr["in_a"], pr["end_b"]), (0, pr["in_a"], pr["end_a"]))
        if not rev:
            order = order[::-1]
        for slot, msk, bend in order:
            h0 = h_ref[...]
            hs_ref[slot] = h0
            o = o + lax.dot_general(h0.astype(BF16), jnp.where(msk, pr["qh"], 0.0).astype(BF16), tn,
                                    preferred_element_type=F32)
            upd = lax.dot_general(jnp.where(msk, pr["kh"], 0.0).astype(BF16), vb, nt, preferred_element_type=F32)
            h_ref[...] = jnp.exp(bend) * h0 + upd * bd_mask
        o_ref[...] = o

    row_blk = lambda r: pl.BlockSpec((REC_WIDTH, TB), lambda i: (r, tmap(i)))
    return pl.pallas_call(
        body, name=name, grid=(nblk,),
        in_specs=[row_blk(0), row_blk(zrow), row_blk(3), pl.BlockSpec((REC_WIDTH, 1), lambda i: (0, 0))],
        out_specs=(pl.BlockSpec((REC_WIDTH, TB), lambda i: (0, tmap(i))),
                   pl.BlockSpec((2, REC_WIDTH, REC_WIDTH), lambda i: (tmap(i), 0, 0)),
                   pl.BlockSpec((None, N_HEADS, TB, TB), lambda i: (tmap(i), 0, 0, 0))),
        out_shape=(jax.ShapeDtypeStruct((REC_WIDTH, s), F32),
                   jax.ShapeDtypeStruct((s // REC_CHUNK, REC_WIDTH, REC_WIDTH), F32),
                   jax.ShapeDtypeStruct((nblk, N_HEADS, TB, TB), BF16)),
        scratch_shapes=[pltpu.VMEM((REC_WIDTH, REC_WIDTH), F32), pltpu.VMEM((N_HEADS, TB, TB), F32)],
        compiler_params=_params(1))(projt, projt, projt, lb)


def _hgrn_scan_bwd(projt, lb, dot, hs, at, prev, rev, name):
    s = projt.shape[1]
    nblk = s // TB
    zrow = 2 if rev else 1
    tmap = (lambda i: i) if rev else (lambda i: nblk - 1 - i)
    has_prev = prev is not None
    tn = (((0,), (0,)), ((), ()))
    nt = (((1,), (1,)), ((), ()))

    def body(*refs):
        q_ref, z_ref, v_ref, lb_ref, do_ref, hs_ref, at_ref = refs[:7]
        rest = refs[7:]
        if has_prev:
            pq_ref, pv_ref = rest[:2]
            rest = rest[2:]
        dq_ref, dz_ref, dv_ref, dlb_ref, dh_ref, dat_ref = rest

        @pl.when(pl.program_id(0) == 0)
        def _():
            dh_ref[...] = jnp.zeros_like(dh_ref)
            dlb_ref[...] = jnp.zeros_like(dlb_ref)

        qraw, v, do, lbv = q_ref[...], v_ref[...], do_ref[...], lb_ref[...]
        dob, vb = do.astype(BF16), v.astype(BF16)
        pr = _hg_prep(qraw, z_ref[...], lbv, rev)
        q, kk, b, pos_sub, in_a = pr["q"], pr["kk"], pr["b"], pr["pos_sub"], pr["in_a"]
        _, causal, earlier_sub, offset = _pair_masks(rev)
        qt = [q * e for e in pr["eq"]]
        kt = [kk * e for e in pr["ek"]]
        n_sub = len(qt)
        dq_h, dk_h, dv_h = [], [], []
        for h in range(N_HEADS):
            d_at = jnp.where(causal, lax.dot_general(_head_rows(vb, h), _head_rows(dob, h), tn,
                                                     preferred_element_type=F32), 0.0)
            dat_ref[h] = d_at
            dv_h.append(lax.dot_general(_head_rows(dob, h), at_ref[h], nt, preferred_element_type=F32))
            d_off = jnp.where(earlier_sub, d_at, 0.0).astype(BF16)
            dqt = jnp.dot(_stack_subs(kt, h).astype(BF16), d_off, preferred_element_type=F32)
            dkt = lax.dot_general(_stack_subs(qt, h).astype(BF16), d_off, nt, preferred_element_type=F32)
            dq_h.append(sum(_head_rows(pr["eq"][i], h) * dqt[i * HEAD_DIM:(i + 1) * HEAD_DIM] for i in range(n_sub)))
            dk_h.append(sum(_head_rows(pr["ek"][i], h) * dkt[i * HEAD_DIM:(i + 1) * HEAD_DIM] for i in range(n_sub)))
        dq0, dk0, dv = (jnp.concatenate(t, axis=0) for t in (dq_h, dk_h, dv_h))

        def pair_step(delta, carry):
            dq, dk = carry
            sh, back = _hg_shift(delta, rev), _hg_unshift(delta, rev)
            kd, bd = pltpu.roll(kk, sh, axis=1), pltpu.roll(b, sh, axis=1)
            valid = (pos_sub <= REC_SUB - 1 - delta) if rev else (pos_sub >= delta)
            e = jnp.where(valid, jnp.exp(jnp.where(valid, b - bd, 0.0)), 0.0)
            hit = offset == (-delta if rev else delta)
            da = jnp.concatenate(
                [jnp.broadcast_to(jnp.sum(jnp.where(hit, dat_ref[h], 0.0), axis=0, keepdims=True), (HEAD_DIM, TB))
                 for h in range(N_HEADS)], axis=0)
            dq = dq + da * kd * e
            dk = dk + pltpu.roll(da * q * e, back, axis=1)
            return dq, dk

        dq, dk = lax.fori_loop(0, REC_SUB, pair_step, (dq0, dk0))

        zero = jnp.zeros((REC_WIDTH, TB), F32)
        bd_mask = _block_diag_mask()
        eb = jnp.exp(b)
        const = zero
        order = ((0, in_a, pr["end_a"]), (1, ~in_a, pr["end_b"]))
        if not rev:
            order = order[::-1]
        for slot, msk, bend in order:
            h0 = hs_ref[slot]
            dh1 = dh_ref[...]
            dh1b = dh1.astype(BF16)
            dq = dq + eb * jnp.dot(h0.astype(BF16), jnp.where(msk, do, 0.0).astype(BF16), preferred_element_type=F32)
            dv = dv + lax.dot_general(dh1b, jnp.where(msk, pr["kh"], 0.0).astype(BF16), tn, preferred_element_type=F32)
            dk_int = pr["ekb"] * jnp.dot(dh1b, jnp.where(msk, v, 0.0).astype(BF16), preferred_element_type=F32)
            dk = dk + dk_int
            ebend = jnp.exp(bend)
            c = (jnp.sum(kk * dk_int, axis=1, keepdims=True)
                 + ebend * jnp.sum(h0 * dh1, axis=1, keepdims=True))
            const = const + jnp.where(msk, c, 0.0)
            upd = lax.dot_general(jnp.where(msk, pr["qh"], 0.0).astype(BF16), dob, nt, preferred_element_type=F32)
            dh_ref[...] = ebend * dh1 + upd * bd_mask

        dg = _chunk_scan(q * dq - kk * dk, pr["pos"], not rev) + const
        sig, sigm, f = pr["sig"], pr["sigm"], pr["f"]
        live = f > F_TINY
        inv_f = 1.0 / jnp.maximum(f, F_TINY)
        one_lb = 1.0 - lbv
        dz = sig * sigm * one_lb * (jnp.where(live, dg * inv_f, 0.0) - dk)
        dlb_ref[...] += jnp.sum(sigm * (jnp.where(live, dg * inv_f, 0.0) - dk), axis=1, keepdims=True)
        dqr = dq * _silu_grad(qraw)
        if has_prev:
            dqr = dqr + pq_ref[...]
            dv = dv + pv_ref[...]
        dq_ref[...] = dqr
        dz_ref[...] = dz
        dv_ref[...] = dv

    row_blk = lambda r: pl.BlockSpec((REC_WIDTH, TB), lambda i: (r, tmap(i)))
    blk = pl.BlockSpec((REC_WIDTH, TB), lambda i: (0, tmap(i)))
    col = pl.BlockSpec((REC_WIDTH, 1), lambda i: (0, 0))
    in_specs = [row_blk(0), row_blk(zrow), row_blk(3), col, blk,
                pl.BlockSpec((2, REC_WIDTH, REC_WIDTH), lambda i: (tmap(i), 0, 0)),
                pl.BlockSpec((None, N_HEADS, TB, TB), lambda i: (tmap(i), 0, 0, 0))]
    ins = [projt, projt, projt, lb, dot, hs, at]
    if has_prev:
        in_specs += [blk, blk]
        ins += list(prev)
    t_shape = jax.ShapeDtypeStruct((REC_WIDTH, s), F32)
    return pl.pallas_call(
        body, name=name, grid=(nblk,), in_specs=in_specs, out_specs=(blk, blk, blk, col),
        out_shape=(t_shape, t_shape, t_shape, jax.ShapeDtypeStruct((REC_WIDTH, 1), F32)),
        scratch_shapes=[pltpu.VMEM((REC_WIDTH, REC_WIDTH), F32), pltpu.VMEM((N_HEADS, TB, TB), F32)],
        compiler_params=_params(1))(*ins)


REC_OUT_COLS = 512


def _head_rms(o):
    o3 = o.reshape(N_HEADS, HEAD_DIM, o.shape[1])
    rstd = lax.rsqrt(jnp.mean(o3 * o3, axis=1, keepdims=True) + EPS)
    return o3 * rstd, rstd


def _hgrn_out(of, ob, projt, wn, name):
    s = of.shape[1]

    def body(of_ref, ob_ref, g_ref, wn_ref, o_ref):
        on, _ = _head_rms(of_ref[...] + ob_ref[...])
        g = g_ref[...]
        y = on.reshape(REC_WIDTH, REC_OUT_COLS) * wn_ref[...] * (g * _sigmoid(g))
        o_ref[...] = y.T.astype(BF16)

    blk = pl.BlockSpec((REC_WIDTH, REC_OUT_COLS), lambda i: (0, i))
    return pl.pallas_call(
        body, name=name, grid=(s // REC_OUT_COLS,),
        in_specs=[blk, blk, pl.BlockSpec((REC_WIDTH, REC_OUT_COLS), lambda i: (4, i)),
                  pl.BlockSpec((REC_WIDTH, 1), lambda i: (0, 0))],
        out_specs=pl.BlockSpec((REC_OUT_COLS, REC_WIDTH), lambda i: (i, 0)),
        out_shape=jax.ShapeDtypeStruct((s, REC_WIDTH), BF16), compiler_params=_params(1))(of, ob, projt, wn)


def _hgrn_out_bwd(drec, of, ob, projt, wn, name):
    s = of.shape[1]

    def body(d_ref, of_ref, ob_ref, g_ref, wn_ref, do_ref, dg_ref, dwn_ref):
        @pl.when(pl.program_id(0) == 0)
        def _():
            dwn_ref[...] = jnp.zeros_like(dwn_ref)

        dy = d_ref[...].T
        on3, rstd = _head_rms(of_ref[...] + ob_ref[...])
        on = on3.reshape(REC_WIDTH, REC_OUT_COLS)
        g, wnv = g_ref[...], wn_ref[...]
        dg_ref[...] = dy * on * wnv * _silu_grad(g)
        d_onw = dy * (g * _sigmoid(g))
        dwn_ref[...] += jnp.sum(d_onw * on, axis=1, keepdims=True)
        d_on3 = (d_onw * wnv).reshape(N_HEADS, HEAD_DIM, REC_OUT_COLS)
        do3 = rstd * (d_on3 - on3 * jnp.mean(d_on3 * on3, axis=1, keepdims=True))
        do_ref[...] = do3.reshape(REC_WIDTH, REC_OUT_COLS)

    blk = pl.BlockSpec((REC_WIDTH, REC_OUT_COLS), lambda i: (0, i))
    col = pl.BlockSpec((REC_WIDTH, 1), lambda i: (0, 0))
    t_shape = jax.ShapeDtypeStruct((REC_WIDTH, s), F32)
    return pl.pallas_call(
        body, name=name, grid=(s // REC_OUT_COLS,),
        in_specs=[pl.BlockSpec((REC_OUT_COLS, REC_WIDTH), lambda i: (i, 0)), blk, blk,
                  pl.BlockSpec((REC_WIDTH, REC_OUT_COLS), lambda i: (4, i)), col],
        out_specs=(blk, blk, col),
        out_shape=(t_shape, t_shape, jax.ShapeDtypeStruct((REC_WIDTH, 1), F32)),
        compiler_params=_params(1))(drec, of, ob, projt, wn)


def _lower_bounds(gamma, name):
    def body(g_ref, lb_ref, p_ref):
        g0, g1 = g_ref[0:1, :], g_ref[1:2, :]
        m = jnp.maximum(g0, g1)
        e0, e1 = jnp.exp(g0 - m), jnp.exp(g1 - m)
        p0, p1 = e0 / (e0 + e1), e1 / (e0 + e1)
        lb_ref[...] = (p0 + p1) - p0
        p_ref[0:1, :] = p0
        p_ref[1:2, :] = p1

    n = gamma.shape[1]
    return pl.pallas_call(body, name=name,
                          out_shape=(jax.ShapeDtypeStruct((1, n), F32), jax.ShapeDtypeStruct((2, n), F32)))(gamma)


def _lower_bounds_bwd(dlb1, p, name):
    def body(d_ref, p_ref, o_ref):
        p0, p1, d = p_ref[0:1, :], p_ref[1:2, :], d_ref[...]
        inner = p1 * d
        o_ref[0:1, :] = p0 * (0.0 - inner)
        o_ref[1:2, :] = p1 * (d - inner)

    return pl.pallas_call(body, name=name, out_shape=jax.ShapeDtypeStruct(p.shape, F32))(dlb1, p)


def _split_w_in(w_in):
    return dict(conv=w_in[:, G_CONV[0]:G_CONV[1]], qkv=w_in[:, G_QKV[0]:G_QKV[1]],
                rec_t=w_in[:, G_REC[0]:].T, nat=w_in[:, :G_REC[0]])


def _split_w_rest(w_out, w_up, w_down):
    return dict(out=w_out, out_a=w_out[:CONV_CH], out_b=w_out[CONV_CH:CONV_CH + ATT_WIDTH],
                out_c=w_out[CONV_CH + ATT_WIDTH:], up=w_up, down=w_down)


def _col(v):
    return v.reshape(-1, 1)


def _sequence_step(x, tgt, mods, lbs, small, w_in0, later_weights, final_w):
    saved = []
    xin = x
    big = [_split_w_in(w_in0), None]
    h1 = _resid_norm_mod(x, None, None, small[0]["norm1_w"], mods[0][1:2], mods[0][0:1], "norm1_first")
    for l in range(DEPTH):
        sm, w, md = small[l], big[l], mods[l]
        pa = _matmul(h1, w["conv"], "nn", F32, f"proj_conv")
        qkv = _matmul(h1, w["qkv"], "nn", BF16, f"proj_qkv")
        projt = _matmul(w["rec_t"], h1, "nt", F32, f"proj_rec")
        a_out, cv = _conv_mixer(pa, sm["conv_a_w"], sm["conv_a_b"], sm["ln_a_w"], sm["ln_a_b"], f"conv_mixer")
        outs, lses = zip(*[_attn_branch(qkv, d, f"attn_d{d}") for d in DILATIONS])
        att, att32, lse = _attn_combine(outs, lses, f"attn_combine")
        lb_f, lb_b = _col(lbs[l][0]), _col(lbs[l][1])
        of, hsf, atf = _hgrn_scan(projt, lb_f, False, "hgrn_fwd")
        ob, hsb, atb = _hgrn_scan(projt, lb_b, True, "hgrn_rev")
        wn = _col(sm["rec_norm_w"])
        rec = _hgrn_out(of, ob, projt, wn, f"hgrn_out")
        mixed = jnp.concatenate([a_out, att, rec], axis=1)
        if l == 0:
            w_in1, w_out_all, w_up_all, w_down_all = later_weights(rec)
            big[0].update(_split_w_rest(w_out_all[0], w_up_all[0], w_down_all[0]))
            big[1] = dict(_split_w_in(w_in1), **_split_w_rest(w_out_all[1], w_up_all[1], w_down_all[1]))
        r1 = _matmul(mixed, w["out"], "nn", F32, f"out_proj")
        xmid, h2 = _resid_norm_mod(xin, r1, md[2:3], sm["norm2_w"], md[4:5], md[3:4], f"norm2")
        u = _matmul(h2, w["up"], "nn", F32, f"ffn_up")
        act = _ffn_act(u, sm["conv_f_w"], f"ffn_act")
        r2 = _matmul(act, w["down"], "nn", F32, f"ffn_down")
        saved.append(dict(xin=xin, h1=h1, pa=pa, qkv=qkv, projt=projt, cv=cv, att32=att32, lse=lse, of=of, ob=ob,
                          hsf=hsf, hsb=hsb, atf=atf, atb=atb, lb_f=lb_f, lb_b=lb_b, wn=wn, mixed=mixed, r1=r1, xmid=xmid, h2=h2,
                          u=u, act=act, r2=r2))
        if l + 1 < DEPTH:
            nxt = small[l + 1]
            xin, h1 = _resid_norm_mod(xmid, r2, md[5:6], nxt["norm1_w"], mods[l + 1][1:2], mods[l + 1][0:1],
                                      "norm1")
    top = saved[-1]
    loss, dx, dr2, dg2, dfw = _final_loss(top["xmid"], top["r2"], mods[-1][5:6], final_w, tgt, "final_loss")

    grads = [None] * DEPTH
    for l in reversed(range(DEPTH)):
        sm, w, md, sv = small[l], big[l], mods[l], saved[l]
        dact = _matmul(dr2, w["down"], "nt", F32, f"d_act")
        g_down = _matmul(sv["act"], dr2, "tn", F32, f"dw_down")
        dug, duv, dwg, dwv = _ffn_act_bwd(sv["u"], dact, sm["conv_f_w"], f"ffn_act_bwd")
        du = jnp.concatenate([dug, duv], axis=1)
        dh2 = _matmul(du, w["up"], "nt", F32, f"d_h2")
        g_up = _matmul(sv["h2"], du, "tn", F32, f"dw_up")
        dxmid, dr1, dsh2, dsc2, dnw2, dg1 = _norm_bwd(sv["xmid"], [dh2], dx, sm["norm2_w"], md[4:5], md[2:3], sv["r1"],
                                                     f"norm2_bwd")
        dmix_a = _matmul(dr1, w["out_a"], "nt", F32, f"d_mix_a")
        dmix_b = _matmul(dr1, w["out_b"], "nt", F32, f"d_mix_b")
        dmix_c = _matmul(dr1, w["out_c"], "nt", F32, f"d_mix_c")
        g_out = _matmul(sv["mixed"], dr1, "tn", F32, f"dw_out")
        dc, dlnw, dlnb, dcb = _conv_mixer_bwd_ln(sv["cv"], dmix_a, sm["ln_a_w"], sm["ln_a_b"], f"conv_mixer_bwd_ln")
        dpa, dcw = _conv_mixer_bwd_conv(sv["pa"], dc, sm["conv_a_w"], f"conv_mixer_bwd_conv")
        delta, dobf = _attn_delta(dmix_b, sv["att32"], f"attn_delta")
        dqkv = None
        for d in DILATIONS:
            dqkv = _attn_branch_bwd(sv["qkv"], dobf, sv["lse"], delta, dqkv, d, f"attn_bwd_d{d}")
        dot, dgt, dwn = _hgrn_out_bwd(dmix_c, sv["of"], sv["ob"], sv["projt"], sv["wn"], f"hgrn_out_bwd")
        dqf, dzf, dvf, dlbf = _hgrn_scan_bwd(sv["projt"], sv["lb_f"], dot, sv["hsf"], sv["atf"], None, False,
                                             "hgrn_fwd_bwd")
        dqt, dzb, dvt, dlbb = _hgrn_scan_bwd(sv["projt"], sv["lb_b"], dot, sv["hsb"], sv["atb"], (dqf, dvf), True,
                                             "hgrn_rev_bwd")
        dprojt = jnp.concatenate([dqt, dzf, dzb, dvt, dgt], axis=0).astype(BF16)
        dnat = jnp.concatenate([dpa] + [t.astype(BF16) for t in dqkv], axis=1)
        dh1_a = _matmul(dnat, w["nat"], "nt", F32, f"d_h1_nat")
        dh1_b = _matmul(dprojt, w["rec_t"], "tn", F32, f"d_h1_rec")
        g_in_nat = _matmul(sv["h1"], dnat, "tn", F32, f"dw_in_nat")
        g_in_rec_t = _matmul(dprojt, sv["h1"], "nn", F32, f"dw_in_rec")
        g_in = jnp.concatenate([g_in_nat, g_in_rec_t.T], axis=1)
        if l > 0:
            below = saved[l - 1]
            dx, dr2, dsh1, dsc1, dnw1, dg2_below = _norm_bwd(sv["xin"], [dh1_a, dh1_b], dxmid, sm["norm1_w"], md[1:2],
                                                            mods[l - 1][5:6], below["r2"], f"norm1_bwd")
        else:
            dx, dsh1, dsc1, dnw1 = _norm_bwd(sv["xin"], [dh1_a, dh1_b], dxmid, sm["norm1_w"], md[1:2], None, None,
                                             f"norm1_bwd")
        grads[l] = dict(w_in=g_in, w_out=g_out, w_up=g_up, w_down=g_down,
                        mod=[dsh1, dsc1, dg1, dsh2, dsc2, dg2], norm1_w=dnw1, conv_a_w=dcw[:CONV_WIDTH], conv_a_b=dcb,
                        ln_a_w=dlnw, ln_a_b=dlnb, lb=jnp.concatenate([dlbf.reshape(1, -1), dlbb.reshape(1, -1)], axis=0),
                        rec_norm_w=dwn.reshape(1, -1), norm2_w=dnw2,
                        conv_f_w=jnp.concatenate([dwg[:3], dwv[:3]], axis=1))
        if l > 0:
            dg2 = dg2_below
    return loss[0, 0], dx, grads, dfw


def _adamw_math(w, g, m, v):
    m = ADAM_B1 * m + (1.0 - ADAM_B1) * g
    v = ADAM_B2 * v + (1.0 - ADAM_B2) * (g * g)
    m_hat = m / (1.0 - ADAM_B1 ** ADAM_STEP)
    v_hat = v / (1.0 - ADAM_B2 ** ADAM_STEP)
    delta = -ADAM_LR * (m_hat / (jnp.sqrt(v_hat) + ADAM_EPS) + ADAM_WD * w)
    return delta, m, v


def _row_tile(rows, cols, max_elems=384 * 1024):
    best = None
    for t in range(8, rows + 1, 8):
        if rows % t == 0 and t * cols <= max_elems:
            best = t
    return best or rows


def _adamw(w, g, m, v, name):
    nl, r, c = w.shape
    tr = _row_tile(r, c)

    def body(w_ref, g_ref, m_ref, v_ref, d_ref, m2_ref, v2_ref):
        d_ref[...], m2_ref[...], v2_ref[...] = _adamw_math(w_ref[...], g_ref[...], m_ref[...], v_ref[...])

    blk = pl.BlockSpec((None, tr, c), lambda l, i: (l, i, 0))
    shape = jax.ShapeDtypeStruct((nl, r, c), F32)
    return pl.pallas_call(body, name=name, grid=(nl, r // tr), in_specs=[blk] * 4, out_specs=(blk, blk, blk),
                          out_shape=(shape, shape, shape), compiler_params=_params(2))(w, g, m, v)


ADA_SHARD = N_MOD * D_MODEL // 4
ADA_COLS = 512
ADA_ROWS = 256
HIGHEST = lax.Precision.HIGHEST


def _ada_mod(c_all, w_ada, b_sh, name):
    def body(c_ref, w_ref, b_ref, o_ref):
        cv = c_ref[...]
        o_ref[...] = jnp.dot(cv * _sigmoid(cv), w_ref[...], precision=HIGHEST, preferred_element_type=F32) + b_ref[...]

    return pl.pallas_call(
        body, name=name, grid=(DEPTH, ADA_SHARD // ADA_COLS),
        in_specs=[pl.BlockSpec((8, D_MODEL), lambda l, j: (0, 0)),
                  pl.BlockSpec((None, D_MODEL, ADA_COLS), lambda l, j: (l, 0, j)),
                  pl.BlockSpec((None, 1, ADA_COLS), lambda l, j: (l, 0, j))],
        out_specs=pl.BlockSpec((None, 8, ADA_COLS), lambda l, j: (l, 0, j)),
        out_shape=jax.ShapeDtypeStruct((DEPTH, 8, ADA_SHARD), F32), compiler_params=_params(2))(c_all, w_ada, b_sh)


def _ada_update(c_all, dmod_sh, w, m, v, name):
    def body(c_ref, d_ref, w_ref, m_ref, v_ref, g_ref, dl_ref, m2_ref, v2_ref):
        cv = c_ref[...]
        g = lax.dot_general(cv * _sigmoid(cv), d_ref[...], (((0,), (0,)), ((), ())), precision=HIGHEST,
                            preferred_element_type=F32)
        g_ref[...] = g
        dl_ref[...], m2_ref[...], v2_ref[...] = _adamw_math(w_ref[...], g, m_ref[...], v_ref[...])

    blk = pl.BlockSpec((None, ADA_ROWS, ADA_SHARD), lambda l, i: (l, i, 0))
    shape = jax.ShapeDtypeStruct((DEPTH, D_MODEL, ADA_SHARD), F32)
    return pl.pallas_call(
        body, name=name, grid=(DEPTH, D_MODEL // ADA_ROWS),
        in_specs=[pl.BlockSpec((8, ADA_ROWS), lambda l, i: (0, i)),
                  pl.BlockSpec((None, 8, ADA_SHARD), lambda l, i: (l, 0, 0)), blk, blk, blk],
        out_specs=(blk,) * 4, out_shape=(shape,) * 4, compiler_params=_params(2))(c_all, dmod_sh, w, m, v)


def _sum_devices(packs, name):
    def body(p_ref, o_ref):
        acc = p_ref[0]
        for dev in range(1, 8):
            acc = acc + p_ref[dev]
        o_ref[...] = acc

    return pl.pallas_call(body, name=name, out_shape=jax.ShapeDtypeStruct(packs.shape[1:], F32))(packs)


def _mesh_pos():
    return lax.axis_index("x"), lax.axis_index("y"), lax.axis_index("c")


def _flip(v, bit):
    return 1 - v if bit else v


def _allgather_devices(x, name):
    m_per, n = x.shape

    def body(x_ref, out_ref, send_sems, recv_sems, local_sem):
        ix, iy, ic = _mesh_pos()
        me, sibling = (ix, iy, ic), (ix, iy, 1 - ic)
        chips = [(1 - ix, iy), (ix, 1 - iy), (1 - ix, 1 - iy)]

        def rows(px, py, pc):
            return out_ref.at[pl.ds((4 * px + 2 * py + pc) * m_per, m_per), :]

        def copy(k, block, to, src=None):
            return pltpu.make_async_remote_copy(
                src_ref=rows(*block) if src is None else src, dst_ref=rows(*block),
                send_sem=send_sems.at[k], recv_sem=recv_sems.at[k], device_id=to, device_id_type=MESH)

        mine = pltpu.make_async_copy(x_ref, rows(*me), local_sem)
        mine.start()
        first = [copy(0, me, sibling, src=x_ref)]
        first += [copy(1 + j, me, (*chip, ic), src=x_ref) for j, chip in enumerate(chips)]
        for cp in first:
            cp.start()
        passed = [copy(4 + j, (*chip, ic), sibling) for j, chip in enumerate(chips)]
        for j, chip in enumerate(chips):
            copy(1 + j, (*chip, ic), me).wait_recv()
            passed[j].start()
        copy(0, sibling, me).wait_recv()
        for j, chip in enumerate(chips):
            copy(4 + j, (*chip, 1 - ic), me).wait_recv()
        for cp in first + passed:
            cp.wait_send()
        mine.wait()

    return pl.pallas_call(
        body, name=name, out_shape=jax.ShapeDtypeStruct((8 * m_per, n), x.dtype),
        in_specs=[pl.BlockSpec(memory_space=pltpu.VMEM)], out_specs=pl.BlockSpec(memory_space=pltpu.VMEM),
        scratch_shapes=[pltpu.SemaphoreType.DMA((7,)), pltpu.SemaphoreType.DMA((7,)), pltpu.SemaphoreType.DMA],
    )(x)


def _gather_chips(shards, name):
    n = len(shards)

    def body(*refs):
        ins, outs = refs[:n], refs[n:2 * n]
        send_sems, recv_sems, local_sems = refs[2 * n:]
        ix, iy, ic = _mesh_pos()
        me = 2 * ix + iy
        local = [pltpu.make_async_copy(ins[a], outs[a].at[me], local_sems.at[a]) for a in range(n)]
        for cp in local:
            cp.start()
        remote = []
        for a in range(n):
            for k in (1, 2, 3):
                px, py = _flip(ix, k & 2), _flip(iy, k & 1)
                sems = dict(send_sem=send_sems.at[3 * a + k - 1], recv_sem=recv_sems.at[3 * a + k - 1],
                            device_id=(px, py, ic), device_id_type=MESH)
                out_cp = pltpu.make_async_remote_copy(src_ref=ins[a], dst_ref=outs[a].at[me], **sems)
                in_cp = pltpu.make_async_remote_copy(src_ref=ins[a], dst_ref=outs[a].at[2 * px + py], **sems)
                out_cp.start()
                remote.append((out_cp, in_cp))
        for out_cp, in_cp in remote:
            out_cp.wait_send()
            in_cp.wait_recv()
        for cp in local:
            cp.wait()

    return pl.pallas_call(
        body, name=name, in_specs=[ANY] * n, out_specs=tuple([ANY] * n),
        out_shape=tuple(jax.ShapeDtypeStruct((4,) + t.shape, t.dtype) for t in shards),
        scratch_shapes=[pltpu.SemaphoreType.DMA((3 * n,)), pltpu.SemaphoreType.DMA((3 * n,)),
                        pltpu.SemaphoreType.DMA((n,))],
    )(*shards)


HBM = pl.BlockSpec(memory_space=pltpu.HBM)
SEM = pl.BlockSpec(memory_space=pltpu.SEMAPHORE)
DATAFLOW = pltpu.SideEffectType.DATAFLOW_SIDE_EFFECTING


def _peer_chip(ix, iy, k):
    return _flip(ix, k & 2), _flip(iy, k & 1)


def _gather_chips_start(shards, name):
    n = len(shards)

    def body(*refs):
        src, land = refs[:n], refs[n:2 * n]
        send_sems, recv_sems = refs[2 * n], refs[2 * n + 1]
        token = refs[-1]
        ix, iy, ic = _mesh_pos()
        me = 2 * ix + iy
        for a in range(n):
            for k in (1, 2, 3):
                px, py = _peer_chip(ix, iy, k)
                pltpu.make_async_remote_copy(
                    src_ref=src[a], dst_ref=land[a].at[me], send_sem=send_sems.at[3 * a + k - 1],
                    recv_sem=recv_sems.at[3 * a + k - 1], device_id=(px, py, ic), device_id_type=MESH).start()
        token[...] = jnp.zeros_like(token)

    hbm = lambda shape, dtype: pltpu.HBM(shape, dtype)
    operands = ([pltpu.with_memory_space_constraint(t, pltpu.HBM) for t in shards]
                + [pltpu.with_memory_space_constraint(lax.empty((4,) + t.shape, t.dtype), pltpu.HBM) for t in shards])
    return pl.pallas_call(
        body, name=name,
        out_shape=(pltpu.SemaphoreType.DMA((3 * n,)), pltpu.SemaphoreType.DMA((3 * n,)),
                   *[hbm(t.shape, t.dtype) for t in shards], *[hbm((4,) + t.shape, t.dtype) for t in shards],
                   jax.ShapeDtypeStruct((8, LANES), F32)),
        in_specs=(HBM,) * (2 * n),
        out_specs=(SEM, SEM) + (HBM,) * (2 * n) + (pl.BlockSpec(memory_space=pltpu.VMEM),),
        input_output_aliases={a: 2 + a for a in range(2 * n)},
        compiler_params=pltpu.CompilerParams(has_side_effects=DATAFLOW),
    )(*operands)


def _gather_chips_wait(started, after, name):
    send_sems, recv_sems = started[0], started[1]
    thru = started[2:-1]
    n = len(thru) // 2

    def body(*refs):
        src, land = refs[:n], refs[n:2 * n]
        send_sems, recv_sems = refs[2 * n], refs[2 * n + 1]
        ix, iy, ic = _mesh_pos()
        for a in range(n):
            for k in (1, 2, 3):
                px, py = _peer_chip(ix, iy, k)
                cp = pltpu.make_async_remote_copy(
                    src_ref=src[a], dst_ref=land[a].at[2 * px + py], send_sem=send_sems.at[3 * a + k - 1],
                    recv_sem=recv_sems.at[3 * a + k - 1], device_id=(px, py, ic), device_id_type=MESH)
                cp.wait_send()
                cp.wait_recv()

    outs = pl.pallas_call(
        body, name=name,
        out_shape=tuple(pltpu.HBM(t.shape, t.dtype) for t in thru),
        in_specs=(HBM,) * (2 * n) + (SEM, SEM, ANY), out_specs=(HBM,) * (2 * n),
        input_output_aliases={a: a for a in range(2 * n)},
        compiler_params=pltpu.CompilerParams(has_side_effects=DATAFLOW),
    )(*thru, send_sems, recv_sems, after)
    return outs[:n], outs[n:]


BIG_KINDS = (("w_in", "col", D_MODEL, IN_COLS), ("w_out", "row", D_MODEL, D_MODEL),
             ("w_up", "col", D_MODEL, 2 * D_FF), ("w_down", "row", D_FF, D_MODEL))


def _piece_shape(how, r, c):
    return (r // 2, c // 4) if how == "col" else (r // 8, c)


def _aligned(start, multiple):
    return start if isinstance(start, int) else pl.multiple_of(start, multiple)


def _piece(ref, how, r, c, chip, half):
    if how == "col":
        return ref.at[pl.ds(_aligned(half * (r // 2), 8), r // 2), pl.ds(_aligned(chip * (c // 4), LANES), c // 4)]
    n = r // 4
    return ref.at[pl.ds(_aligned(chip * n + half * (n // 2), 8), n // 2), :]


def _rs_pair_exchange(grads, name):
    nk = len(BIG_KINDS)
    flat = [grads[ki][l] for ki in range(nk) for l in range(DEPTH)]
    per = DEPTH * 4

    def body(*refs):
        g, land = refs[:nk * DEPTH], refs[nk * DEPTH:nk * DEPTH + nk]
        send_sems, recv_sems = refs[nk * DEPTH + nk:]
        ix, iy, ic = _mesh_pos()
        sibling = (ix, iy, 1 - ic)
        copies = []
        for ki, (_, how, r, c) in enumerate(BIG_KINDS):
            for l in range(DEPTH):
                for j in range(4):
                    sem = ki * per + l * 4 + j
                    rem = pltpu.make_async_remote_copy(
                        src_ref=_piece(g[ki * DEPTH + l], how, r, c, j, 1 - ic), dst_ref=land[ki].at[l, j],
                        send_sem=send_sems.at[sem], recv_sem=recv_sems.at[sem], device_id=sibling, device_id_type=MESH)
                    rem.start()
                    copies.append(rem)
        for rem in copies:
            rem.wait_send()
            rem.wait_recv()

    shapes = [jax.ShapeDtypeStruct((DEPTH, 4) + _piece_shape(how, r, c), F32) for _, how, r, c in BIG_KINDS]
    return pl.pallas_call(
        body, name=name, in_specs=[ANY] * len(flat), out_specs=tuple([ANY] * nk), out_shape=tuple(shapes),
        scratch_shapes=[pltpu.SemaphoreType.DMA((nk * per,))] * 2,
    )(*flat)


def _pair_sum(g, theirs, layer, how, core, name):
    r, c = g.shape
    pr, pc = _piece_shape(how, r, c)
    if how == "col":
        mine_spec = pl.BlockSpec((pr, pc), lambda j, core_ref: (core_ref[0], j))
    else:
        mine_spec = pl.BlockSpec((pr, pc), lambda j, core_ref: (2 * j + core_ref[0], 0))

    def body(core_ref, g_ref, t_ref, o_ref, ob_ref):
        total = g_ref[...] + t_ref[...]
        o_ref[...] = total
        ob_ref[...] = total.astype(BF16)

    out_blk = pl.BlockSpec((None, pr, pc), lambda j, core_ref: (j, 0, 0))
    return pl.pallas_call(
        body, name=name,
        grid_spec=pltpu.PrefetchScalarGridSpec(
            num_scalar_prefetch=1, grid=(4,),
            in_specs=[mine_spec, pl.BlockSpec((None, None, pr, pc), lambda j, core_ref: (layer, j, 0, 0))],
            out_specs=(out_blk, out_blk)),
        out_shape=(jax.ShapeDtypeStruct((4, pr, pc), F32), jax.ShapeDtypeStruct((4, pr, pc), BF16)),
        compiler_params=_params(1))(core, g, theirs)


def _rs_chip_exchange(pair_sums, name):
    nk = len(pair_sums)
    flat = [pair_sums[ki][l] for ki in range(nk) for l in range(DEPTH)]

    def body(*refs):
        src, dst = refs[:nk * DEPTH], refs[nk * DEPTH:nk * DEPTH + nk]
        send_sems, recv_sems = refs[nk * DEPTH + nk:]
        ix, iy, ic = _mesh_pos()
        copies = []
        for ki in range(nk):
            for l in range(DEPTH):
                for k in (1, 2, 3):
                    px, py = _flip(ix, k & 2), _flip(iy, k & 1)
                    sem = (ki * DEPTH + l) * 3 + k - 1
                    rem = pltpu.make_async_remote_copy(
                        src_ref=src[ki * DEPTH + l].at[2 * px + py], dst_ref=dst[ki].at[l, k - 1],
                        send_sem=send_sems.at[sem], recv_sem=recv_sems.at[sem], device_id=(px, py, ic), device_id_type=MESH)
                    rem.start()
                    copies.append(rem)
        for rem in copies:
            rem.wait_send()
            rem.wait_recv()

    return pl.pallas_call(
        body, name=name, in_specs=[ANY] * len(flat), out_specs=tuple([ANY] * nk),
        out_shape=tuple(jax.ShapeDtypeStruct((DEPTH, 3) + pair_sums[ki][0].shape[1:], pair_sums[ki][0].dtype)
                        for ki in range(nk)),
        scratch_shapes=[pltpu.SemaphoreType.DMA((nk * DEPTH * 3,))] * 2,
    )(*flat)


def _chip_sum(own, others, layer, chip, name):
    _, pr, pc = own.shape

    def body(chip_ref, own_ref, s1, s2, s3, o_ref):
        o_ref[...] = ((own_ref[...] + s1[...].astype(F32)) + s2[...].astype(F32)) + s3[...].astype(F32)

    slot = lambda k: pl.BlockSpec((None, None, pr, pc), lambda i, chip_ref: (layer, k, 0, 0))
    return pl.pallas_call(
        body, name=name,
        grid_spec=pltpu.PrefetchScalarGridSpec(
            num_scalar_prefetch=1, grid=(1,),
            in_specs=[pl.BlockSpec((None, pr, pc), lambda i, chip_ref: (chip_ref[0], 0, 0)), slot(0), slot(1), slot(2)],
            out_specs=pl.BlockSpec((pr, pc), lambda i, chip_ref: (0, 0))),
        out_shape=jax.ShapeDtypeStruct((pr, pc), F32), compiler_params=_params(1))(chip, own, others, others, others)


def _rs_pair_share(halves, name):
    nk = len(halves)
    flat = [halves[ki][l] for ki in range(nk) for l in range(DEPTH)]

    def body(*refs):
        src, dst = refs[:nk * DEPTH], refs[nk * DEPTH:nk * DEPTH + nk]
        send_sems, recv_sems = refs[nk * DEPTH + nk:]
        ix, iy, ic = _mesh_pos()
        copies = []
        for ki in range(nk):
            for l in range(DEPTH):
                sem = ki * DEPTH + l
                rem = pltpu.make_async_remote_copy(
                    src_ref=src[sem], dst_ref=dst[ki].at[l], send_sem=send_sems.at[sem], recv_sem=recv_sems.at[sem],
                    device_id=(ix, iy, 1 - ic), device_id_type=MESH)
                rem.start()
                copies.append(rem)
        for rem in copies:
            rem.wait_send()
            rem.wait_recv()

    return pl.pallas_call(
        body, name=name, in_specs=[ANY] * len(flat), out_specs=tuple([ANY] * nk),
        out_shape=tuple(jax.ShapeDtypeStruct((DEPTH,) + halves[ki][0].shape, F32) for ki in range(nk)),
        scratch_shapes=[pltpu.SemaphoreType.DMA((nk * DEPTH,))] * 2,
    )(*flat)


def _adamw_halves(w, mine, theirs, m, v, core, name):
    nl, pr, pc = theirs.shape
    shape = w.shape
    view = lambda t: t.reshape(nl, 2, pr, pc)
    tr = _row_tile(pr, pc, 256 * 1024)

    def body(core_ref, w_ref, a0_ref, a1_ref, t_ref, m_ref, v_ref, g_ref, d_ref, m2_ref, v2_ref):
        own = jnp.where(pl.program_id(0) == 0, a0_ref[...], a1_ref[...])
        g = jnp.where(pl.program_id(1) == core_ref[0], own, t_ref[...])
        g_ref[...] = g
        d_ref[...], m2_ref[...], v2_ref[...] = _adamw_math(w_ref[...], g, m_ref[...], v_ref[...])

    blk = pl.BlockSpec((None, None, tr, pc), lambda l, h, i, core_ref: (l, h, i, 0))
    own_blk = pl.BlockSpec((tr, pc), lambda l, h, i, core_ref: (i, 0))
    out = jax.ShapeDtypeStruct((nl, 2, pr, pc), F32)
    outs = pl.pallas_call(
        body, name=name,
        grid_spec=pltpu.PrefetchScalarGridSpec(
            num_scalar_prefetch=1, grid=(nl, 2, pr // tr),
            in_specs=[blk, own_blk, own_blk, pl.BlockSpec((None, tr, pc), lambda l, h, i, core_ref: (l, i, 0)), blk, blk],
            out_specs=(blk,) * 4),
        out_shape=(out,) * 4, compiler_params=_params(3),
    )(core, view(w), mine[0], mine[1], theirs, view(m), view(v))
    return tuple(t.reshape(shape) for t in outs)


def _reduce_scatter_big(grads, core, chip):
    theirs = _rs_pair_exchange(grads, "rs_pair_exchange")
    pair_sums = [[_pair_sum(grads[ki][l], theirs[ki], l, how, core, f"rs_pair_sum_{kind}") for l in range(DEPTH)]
                 for ki, (kind, how, _, _) in enumerate(BIG_KINDS)]
    slots = _rs_chip_exchange([[both[1] for both in row] for row in pair_sums], "rs_chip_exchange")
    halves = [[_chip_sum(pair_sums[ki][l][0], slots[ki], l, chip, f"rs_chip_sum_{kind}") for l in range(DEPTH)]
              for ki, (kind, _, _, _) in enumerate(BIG_KINDS)]
    other = _rs_pair_share(halves, "rs_pair_share")
    return list(zip(halves, other))


WEIGHT_NAMES = ("w_ada", "b_ada", "norm1_w", "w_in", "conv_a_w", "conv_a_b", "ln_a_w", "ln_a_b", "lb_gamma",
                "rec_norm_w", "w_out", "norm2_w", "w_up", "conv_f_w", "w_down", "final_norm_w")
SMALL_PARAMS = (("b_ada", (DEPTH, N_MOD * D_MODEL), None), ("norm1_w", (DEPTH, D_MODEL), None),
                ("conv_a_w", (DEPTH, CONV_WIDTH, CONV_CH), 2), ("conv_a_b", (DEPTH, CONV_CH), None),
                ("ln_a_w", (DEPTH, CONV_CH), None), ("ln_a_b", (DEPTH, CONV_CH), None),
                ("lb_gamma", (DEPTH, 2, REC_WIDTH), 2), ("rec_norm_w", (DEPTH, REC_WIDTH), None),
                ("norm2_w", (DEPTH, D_MODEL), None), ("conv_f_w", (DEPTH, 3, 2 * D_FF), 2),
                ("final_norm_w", (D_MODEL,), None))


def _pack_rows(parts):
    flat = jnp.concatenate([p.reshape(-1) for p in parts])
    total = flat.shape[0]
    padded = -(-total // (8 * LANES)) * (8 * LANES)
    return jnp.pad(flat, (0, padded - total)).reshape(padded // LANES, LANES)


def _unpack(flat, shapes):
    out, off = [], 0
    for shp in shapes:
        size = int(np.prod(shp))
        out.append(flat[off:off + size].reshape(shp))
        off += size
    return out


def _unstack_chips(t, axis):
    return jnp.concatenate([t[j] for j in range(4)], axis=axis)


def kernel(x, c, w_ada, b_ada, norm1_w, w_in, conv_a_w, conv_a_b, ln_a_w, ln_a_b, lb_gamma, rec_norm_w, w_out, norm2_w, w_up, conv_f_w, w_down, final_norm_w, loss_target, m_w_ada, m_b_ada, m_norm1_w, m_w_in, m_conv_a_w, m_conv_a_b, m_ln_a_w, m_ln_a_b, m_lb_gamma, m_rec_norm_w, m_w_out, m_norm2_w, m_w_up, m_conv_f_w, m_w_down, m_final_norm_w, v_w_ada, v_b_ada, v_norm1_w, v_w_in, v_conv_a_w, v_conv_a_b, v_ln_a_w, v_ln_a_b, v_lb_gamma, v_rec_norm_w, v_w_out, v_norm2_w, v_w_up, v_conv_f_w, v_w_down, v_final_norm_w):
    params = dict(zip(WEIGHT_NAMES, (w_ada, b_ada, norm1_w, w_in, conv_a_w, conv_a_b, ln_a_w, ln_a_b, lb_gamma,
                                     rec_norm_w, w_out, norm2_w, w_up, conv_f_w, w_down, final_norm_w)))
    mom1 = dict(zip(WEIGHT_NAMES, (m_w_ada, m_b_ada, m_norm1_w, m_w_in, m_conv_a_w, m_conv_a_b, m_ln_a_w, m_ln_a_b,
                                   m_lb_gamma, m_rec_norm_w, m_w_out, m_norm2_w, m_w_up, m_conv_f_w, m_w_down,
                                   m_final_norm_w)))
    mom2 = dict(zip(WEIGHT_NAMES, (v_w_ada, v_b_ada, v_norm1_w, v_w_in, v_conv_a_w, v_conv_a_b, v_ln_a_w, v_ln_a_b,
                                   v_lb_gamma, v_rec_norm_w, v_w_out, v_norm2_w, v_w_up, v_conv_f_w, v_w_down,
                                   v_final_norm_w)))
    ix, iy, ic = _mesh_pos()
    chip = 2 * ix + iy
    dev = 2 * chip + ic

    c_all = _allgather_devices(c.reshape(8, LANES), "gather_cond").reshape(8, D_MODEL)
    b_sh = lax.dynamic_slice_in_dim(b_ada, chip * ADA_SHARD, ADA_SHARD, axis=1)
    mod_sh = _ada_mod(c_all, w_ada, b_sh.reshape(DEPTH, 1, ADA_SHARD), "ada_mod")
    w_in_b, w_out_b, w_up_b, w_down_b = (t.astype(BF16) for t in (w_in, w_out, w_up, w_down))
    first = _gather_chips([mod_sh, conv_a_w, conv_f_w, lb_gamma, w_in_b[0]], "gather_first")
    later = [w_in_b[1], w_out_b, w_up_b, w_down_b]
    started = _gather_chips_start(later, "gather_rest_start")
    mod_mine = lax.dynamic_index_in_dim(first[0], dev, axis=2, keepdims=False) + started[-1][0, 0]
    mods = [jnp.concatenate([mod_mine[j, l] for j in range(4)]).reshape(N_MOD, D_MODEL) for l in range(DEPTH)]
    conv_a_w_f, conv_f_w_f, gamma_f = (_unstack_chips(first[k], 2) for k in (1, 2, 3))
    w_in0 = _unstack_chips(first[4], 1)

    def later_weights(after):
        own, lands = _gather_chips_wait(started, after, "gather_rest_wait")
        full = [lax.dynamic_update_index_in_dim(land, mine, chip, 0) for land, mine in zip(lands, own)]
        return (_unstack_chips(full[0], 1), _unstack_chips(full[1], 1), _unstack_chips(full[2], 2),
                _unstack_chips(full[3], 1))

    lb1, p_soft = _lower_bounds(gamma_f.reshape(DEPTH, 2 * REC_WIDTH), "lower_bounds")
    lbs = [jnp.zeros((2, REC_WIDTH), F32), lb1.reshape(2, REC_WIDTH)]
    small = []
    for l in range(DEPTH):
        small.append(dict(norm1_w=norm1_w[l][None], conv_a_w=conv_a_w_f[l], conv_a_b=conv_a_b[l][None],
                          ln_a_w=ln_a_w[l][None], ln_a_b=ln_a_b[l][None], rec_norm_w=rec_norm_w[l],
                          norm2_w=norm2_w[l][None], conv_f_w=conv_f_w_f[l]))

    loss, dx, grads, dfw = _sequence_step(x[0], loss_target[0], mods, lbs, small, w_in0, later_weights,
                                          final_norm_w[None])
    loss = lax.psum(loss, ("x", "y", "c"))

    dgamma = _lower_bounds_bwd(grads[1]["lb"].reshape(1, 2 * REC_WIDTH), p_soft, "lower_bounds_bwd")
    dmod = [jnp.concatenate(grads[l]["mod"], axis=1) for l in range(DEPTH)]
    stack = lambda key: jnp.stack([grads[l][key] for l in range(DEPTH)])
    local_small = dict(b_ada=jnp.concatenate(dmod, axis=0), norm1_w=stack("norm1_w"), conv_a_w=stack("conv_a_w"),
                       conv_a_b=stack("conv_a_b"), ln_a_w=stack("ln_a_w"), ln_a_b=stack("ln_a_b"), lb_gamma=dgamma,
                       rec_norm_w=stack("rec_norm_w"), norm2_w=stack("norm2_w"), conv_f_w=stack("conv_f_w"),
                       final_norm_w=dfw)
    pack = _pack_rows([local_small[name] for name, _, _ in SMALL_PARAMS])
    rows = pack.shape[0]
    packs = _allgather_devices(pack, "gather_small_grads").reshape(8, rows, LANES)
    summed = _sum_devices(packs, "sum_small_grads").reshape(-1)
    small_grads = dict(zip([n for n, _, _ in SMALL_PARAMS], _unpack(summed, [shp for _, shp, _ in SMALL_PARAMS])))

    dmod_all = packs.reshape(8, rows * LANES)[:, :DEPTH * N_MOD * D_MODEL].reshape(8, DEPTH, N_MOD * D_MODEL)
    dmod_sh = lax.dynamic_slice_in_dim(dmod_all, chip * ADA_SHARD, ADA_SHARD, axis=2).transpose(1, 0, 2)
    g_ada, d_ada, m_ada, v_ada = _ada_update(c_all, dmod_sh, w_ada, m_w_ada, v_w_ada, "ada_update")

    for name, shp, axis in SMALL_PARAMS:
        if axis is not None:
            width = shp[axis] // 4
            small_grads[name] = lax.dynamic_slice_in_dim(small_grads[name], chip * width, width, axis=axis)
    names = [n for n, _, _ in SMALL_PARAMS]
    packed = [_pack_rows([src[n] for n in names])[None] for src in (params, small_grads, mom1, mom2)]
    small_out = _adamw(*packed, "adamw_small")
    shapes = [params[n].shape for n in names]
    small_delta, small_m, small_v = (dict(zip(names, _unpack(t.reshape(-1), shapes))) for t in small_out)

    core_id, chip_id = ic.astype(jnp.int32).reshape(1), chip.astype(jnp.int32).reshape(1)
    summed_big = _reduce_scatter_big([[grads[l][name] for l in range(DEPTH)] for name, _, _, _ in BIG_KINDS],
                                     core_id, chip_id)
    grad, delta, new_m, new_v = dict(small_grads), small_delta, small_m, small_v
    grad["w_ada"], delta["w_ada"], new_m["w_ada"], new_v["w_ada"] = g_ada, d_ada, m_ada, v_ada
    for (name, _, _, _), (mine, theirs) in zip(BIG_KINDS, summed_big):
        grad[name], delta[name], new_m[name], new_v[name] = _adamw_halves(
            params[name], mine, theirs, mom1[name], mom2[name], core_id, f"adamw_{name}")

    return (loss, dx[None], *[grad[n] for n in WEIGHT_NAMES], *[delta[n] for n in WEIGHT_NAMES],
            *[new_m[n] for n in WEIGHT_NAMES], *[new_v[n] for n in WEIGHT_NAMES])
```

```python
import numpy as np
import jax
import jax.numpy as jnp
from jax import lax
from jax.experimental import pallas as pl
from jax.experimental.pallas import tpu as pltpu

F32 = jnp.float32
BF16 = jnp.bfloat16

D_MODEL = 1024
DEPTH = 2
HEAD_DIM = 64
CONV_CH = 256
CONV_WIDTH = 31
ATT_WIDTH = 384
N_HEADS = 6
DILATIONS = (1, 4, 16)
ATT_HALF = 64
ATT_BLOCK = 128
ALIBI_MAX_EXP = 8.0
MASK_VALUE = -1e30
REC_WIDTH = 384
REC_CHUNK = 64
F_TINY = 1e-30
D_FF = 2816
N_MOD = 6
EPS = 1e-6
G_CONV = (0, 512)
G_QKV = (512, 1664)
G_REC = (1664, 3584)
IN_COLS = 3584

ADAM_LR = 0.001
ADAM_B1 = 0.9
ADAM_B2 = 0.999
ADAM_EPS = 1e-08
ADAM_WD = 0.01
ADAM_STEP = 10

VMEM_LIMIT_BYTES = 56 * 1024 * 1024
LANES = 128
MESH = pl.DeviceIdType.MESH
ANY = pl.BlockSpec(memory_space=pl.ANY)


def _params(n_axes):
    return pltpu.CompilerParams(dimension_semantics=("arbitrary",) * n_axes,
                                vmem_limit_bytes=VMEM_LIMIT_BYTES)


def _tile(n, target):
    best = None
    for t in range(LANES, min(n, target) + 1, LANES):
        if n % t == 0:
            best = t
    return best or n


def _sigmoid(x):
    return jax.nn.sigmoid(x)


def _silu_grad(x):
    s = _sigmoid(x)
    return s * (1.0 + x * (1.0 - s))


MM_ACC_ELEMS = 1024 * 1024


def _matmul(a, b, mode, out_dtype, name, tm=1024, tn=1792, tk=1792):
    if mode == "nn":
        (m, k), (k2, n) = a.shape, b.shape
    elif mode == "nt":
        (m, k), (n, k2) = a.shape, b.shape
    else:
        (k, m), (k2, n) = a.shape, b.shape
    assert k == k2, (a.shape, b.shape, mode)
    tn, tk = _tile(n, tn), _tile(k, tk)
    tm = _tile(m, min(tm, MM_ACC_ELEMS // tn))
    nk = k // tk
    a_spec = (pl.BlockSpec((tk, tm), lambda i, j, kk: (kk, i)) if mode == "tn"
              else pl.BlockSpec((tm, tk), lambda i, j, kk: (i, kk)))
    b_spec = (pl.BlockSpec((tn, tk), lambda i, j, kk: (j, kk)) if mode == "nt"
              else pl.BlockSpec((tk, tn), lambda i, j, kk: (kk, j)))
    dims = {"nn": (((1,), (0,)), ((), ())), "nt": (((1,), (1,)), ((), ())),
            "tn": (((0,), (0,)), ((), ()))}[mode]

    def body(a_ref, b_ref, o_ref, *scratch):
        part = lax.dot_general(a_ref[...].astype(BF16), b_ref[...].astype(BF16), dims, preferred_element_type=F32)
        if nk == 1:
            o_ref[...] = part.astype(out_dtype)
            return
        acc_ref, = scratch
        kk = pl.program_id(2)

        @pl.when(kk == 0)
        def _():
            acc_ref[...] = part

        @pl.when(kk > 0)
        def _():
            acc_ref[...] += part

        @pl.when(kk == nk - 1)
        def _():
            o_ref[...] = acc_ref[...].astype(out_dtype)

    return pl.pallas_call(
        body, name=name, grid=(m // tm, n // tn, nk),
        in_specs=[a_spec, b_spec],
        out_specs=pl.BlockSpec((tm, tn), lambda i, j, kk: (i, j)),
        out_shape=jax.ShapeDtypeStruct((m, n), out_dtype),
        scratch_shapes=[pltpu.VMEM((tm, tn), F32)] if nk > 1 else [],
        compiler_params=pltpu.CompilerParams(dimension_semantics=("parallel", "parallel", "arbitrary"),
                                             vmem_limit_bytes=VMEM_LIMIT_BYTES),
    )(a, b)


NORM_ROWS = 256


def _row_spec(width, rows=NORM_ROWS):
    return pl.BlockSpec((rows, width), lambda i: (i, 0))


def _vec_spec(width):
    return pl.BlockSpec((1, width), lambda i: (0, 0))


def _resid_norm_mod(x, r, g, nw, sc, sh, name):
    s, d = x.shape
    has_r = r is not None

    def body(*refs):
        if has_r:
            x_ref, r_ref, g_ref, nw_ref, sc_ref, sh_ref, xn_ref, h_ref = refs
            xn = x_ref[...] + g_ref[...] * r_ref[...]
            xn_ref[...] = xn
        else:
            x_ref, nw_ref, sc_ref, sh_ref, h_ref = refs
            xn = x_ref[...]
        rstd = lax.rsqrt(jnp.mean(xn * xn, axis=-1, keepdims=True) + EPS)
        y = xn * rstd * nw_ref[...]
        h_ref[...] = (y * (1.0 + sc_ref[...]) + sh_ref[...]).astype(BF16)

    if has_r:
        ins, in_specs = (x, r, g, nw, sc, sh), [_row_spec(d), _row_spec(d)] + [_vec_spec(d)] * 4
        out_shape = (jax.ShapeDtypeStruct((s, d), F32), jax.ShapeDtypeStruct((s, d), BF16))
        out_specs = (_row_spec(d), _row_spec(d))
    else:
        ins, in_specs = (x, nw, sc, sh), [_row_spec(d)] + [_vec_spec(d)] * 3
        out_shape = jax.ShapeDtypeStruct((s, d), BF16)
        out_specs = _row_spec(d)
    return pl.pallas_call(body, name=name, grid=(s // NORM_ROWS,), in_specs=in_specs, out_specs=out_specs,
                          out_shape=out_shape, compiler_params=_params(1))(*ins)


def _final_loss(x, r, g, fw, tgt, name):
    s, d = x.shape

    def body(x_ref, r_ref, g_ref, fw_ref, t_ref, loss_ref, dx_ref, dr_ref, dg_ref, dfw_ref):
        @pl.when(pl.program_id(0) == 0)
        def _():
            loss_ref[...] = jnp.zeros_like(loss_ref)
            dg_ref[...] = jnp.zeros_like(dg_ref)
            dfw_ref[...] = jnp.zeros_like(dfw_ref)

        rr = r_ref[...]
        gg = g_ref[...]
        xn = x_ref[...] + gg * rr
        rstd = lax.rsqrt(jnp.mean(xn * xn, axis=-1, keepdims=True) + EPS)
        xh = xn * rstd
        fwv = fw_ref[...]
        e = xh * fwv - t_ref[...]
        loss_ref[...] += 0.5 * jnp.sum(jnp.mean(e * e, axis=-1, keepdims=True), axis=0, keepdims=True)
        dy = e * (1.0 / d)
        dfw_ref[...] += jnp.sum(dy * xh, axis=0, keepdims=True)
        dxh = dy * fwv
        dx = rstd * (dxh - xh * jnp.mean(dxh * xh, axis=-1, keepdims=True))
        dx_ref[...] = dx
        dr_ref[...] = (gg * dx).astype(BF16)
        dg_ref[...] += jnp.sum(dx * rr, axis=0, keepdims=True)

    return pl.pallas_call(
        body, name=name, grid=(s // NORM_ROWS,),
        in_specs=[_row_spec(d), _row_spec(d), _vec_spec(d), _vec_spec(d), _row_spec(d)],
        out_specs=(_vec_spec(LANES), _row_spec(d), _row_spec(d), _vec_spec(d), _vec_spec(d)),
        out_shape=(jax.ShapeDtypeStruct((1, LANES), F32), jax.ShapeDtypeStruct((s, d), F32),
                   jax.ShapeDtypeStruct((s, d), BF16), jax.ShapeDtypeStruct((1, d), F32),
                   jax.ShapeDtypeStruct((1, d), F32)),
        compiler_params=_params(1))(x, r, g, fw, tgt)


def _norm_bwd(x, dhs, dxres, nw, sc, g, r, name):
    s, d = x.shape
    n_dh = len(dhs)
    has_g = g is not None

    def body(*refs):
        x_ref = refs[0]
        dh_refs = refs[1:1 + n_dh]
        dxres_ref, nw_ref, sc_ref = refs[1 + n_dh:4 + n_dh]
        pos = 4 + n_dh
        if has_g:
            g_ref, r_ref = refs[pos:pos + 2]
            pos += 2
            dx_ref, dr_ref, dsh_ref, dsc_ref, dnw_ref, dg_ref = refs[pos:]
            accs = (dsh_ref, dsc_ref, dnw_ref, dg_ref)
        else:
            dx_ref, dsh_ref, dsc_ref, dnw_ref = refs[pos:]
            accs = (dsh_ref, dsc_ref, dnw_ref)

        @pl.when(pl.program_id(0) == 0)
        def _():
            for acc in accs:
                acc[...] = jnp.zeros_like(acc)

        xv = x_ref[...]
        dh = dh_refs[0][...]
        for extra in dh_refs[1:]:
            dh = dh + extra[...]
        rstd = lax.rsqrt(jnp.mean(xv * xv, axis=-1, keepdims=True) + EPS)
        xh = xv * rstd
        nwv = nw_ref[...]
        dsh_ref[...] += jnp.sum(dh, axis=0, keepdims=True)
        dsc_ref[...] += jnp.sum(dh * (xh * nwv), axis=0, keepdims=True)
        dy = dh * (1.0 + sc_ref[...])
        dnw_ref[...] += jnp.sum(dy * xh, axis=0, keepdims=True)
        dxh = dy * nwv
        dx = dxres_ref[...] + rstd * (dxh - xh * jnp.mean(dxh * xh, axis=-1, keepdims=True))
        dx_ref[...] = dx
        if has_g:
            dr_ref[...] = (g_ref[...] * dx).astype(BF16)
            dg_ref[...] += jnp.sum(dx * r_ref[...], axis=0, keepdims=True)

    ins = [x, *dhs, dxres, nw, sc]
    in_specs = [_row_spec(d)] * (2 + n_dh) + [_vec_spec(d)] * 2
    out_shape = [jax.ShapeDtypeStruct((s, d), F32)]
    out_specs = [_row_spec(d)]
    if has_g:
        ins += [g, r]
        in_specs += [_vec_spec(d), _row_spec(d)]
        out_shape.append(jax.ShapeDtypeStruct((s, d), BF16))
        out_specs.append(_row_spec(d))
    n_vec = 4 if has_g else 3
    out_shape += [jax.ShapeDtypeStruct((1, d), F32)] * n_vec
    out_specs += [_vec_spec(d)] * n_vec
    return pl.pallas_call(body, name=name, grid=(s // NORM_ROWS,), in_specs=in_specs, out_specs=tuple(out_specs),
                          out_shape=tuple(out_shape), compiler_params=_params(1))(*ins)


FFN_ROWS = 256
FFN_COLS = 1408
HALO = 8
INV_SQRT2 = 0.7071067811865476
INV_SQRT_2PI = 0.3989422804014327


def _gelu(x):
    return 0.5 * x * (1.0 + lax.erf(x * INV_SQRT2))


def _gelu_grad(x):
    return 0.5 * (1.0 + lax.erf(x * INV_SQRT2)) + x * (INV_SQRT_2PI * jnp.exp(-0.5 * x * x))


def _halo_specs(rows, cols, halo, n_rows_total, col_of):
    per = rows // halo
    last = n_rows_total // halo - 1
    cur = pl.BlockSpec((rows, cols), lambda j, i: (i, col_of(j)))
    prev = pl.BlockSpec((halo, cols), lambda j, i: (jnp.maximum(i * per - 1, 0), col_of(j)))
    nxt = pl.BlockSpec((halo, cols), lambda j, i: (jnp.minimum((i + 1) * per, last), col_of(j)))
    return [prev, cur, nxt]


def _shift_rows(x, k):
    n = x.shape[0]
    return pltpu.roll(x, k % n, axis=0)


def _conv3(ext, w):
    return w[0:1, :] * _shift_rows(ext, 1) + w[1:2, :] * ext + w[2:3, :] * _shift_rows(ext, -1)


def _ext_block(prev_ref, cur_ref, next_ref, i, n_i):
    prev = jnp.where(i > 0, prev_ref[...], 0.0)
    nxt = jnp.where(i < n_i - 1, next_ref[...], 0.0)
    return jnp.concatenate([prev, cur_ref[...], nxt], axis=0)


def _ffn_act(u, cw, name):
    s = u.shape[0]
    nc, ns = D_FF // FFN_COLS, s // FFN_ROWS

    def body(gp, gc, gn, vp, vc, vn, wg_ref, wv_ref, o_ref):
        i = pl.program_id(1)
        cg = _conv3(_ext_block(gp, gc, gn, i, ns), wg_ref[...])[HALO:HALO + FFN_ROWS]
        cv = _conv3(_ext_block(vp, vc, vn, i, ns), wv_ref[...])[HALO:HALO + FFN_ROWS]
        o_ref[...] = (_gelu(cg) * cv).astype(BF16)

    in_specs = (_halo_specs(FFN_ROWS, FFN_COLS, HALO, s, lambda j: j)
                + _halo_specs(FFN_ROWS, FFN_COLS, HALO, s, lambda j: j + nc)
                + [pl.BlockSpec((3, FFN_COLS), lambda j, i: (0, j)),
                   pl.BlockSpec((3, FFN_COLS), lambda j, i: (0, j + nc))])
    return pl.pallas_call(
        body, name=name, grid=(nc, ns), in_specs=in_specs,
        out_specs=pl.BlockSpec((FFN_ROWS, FFN_COLS), lambda j, i: (i, j)),
        out_shape=jax.ShapeDtypeStruct((s, D_FF), BF16), compiler_params=_params(2),
    )(u, u, u, u, u, u, cw, cw)


def _ffn_act_bwd(u, dact, cw, name):
    s = u.shape[0]
    nc, ns = D_FF // FFN_COLS, s // FFN_ROWS

    def body(gp, gc, gn, vp, vc, vn, dp, dc, dn, wg_ref, wv_ref, dug_ref, duv_ref, dwg_ref, dwv_ref):
        i = pl.program_id(1)

        @pl.when(i == 0)
        def _():
            dwg_ref[...] = jnp.zeros_like(dwg_ref)
            dwv_ref[...] = jnp.zeros_like(dwv_ref)

        ug = _ext_block(gp, gc, gn, i, ns)
        uv = _ext_block(vp, vc, vn, i, ns)
        da = _ext_block(dp, dc, dn, i, ns)
        wg, wv = wg_ref[...], wv_ref[...]
        cg, cv = _conv3(ug, wg), _conv3(uv, wv)
        dcg = da * cv * _gelu_grad(cg)
        dcv = da * _gelu(cg)
        inner = slice(HALO, HALO + FFN_ROWS)
        for d_c, uu, w, du_ref, dw_ref in ((dcg, ug, wg, dug_ref, dwg_ref), (dcv, uv, wv, duv_ref, dwv_ref)):
            du = w[0:1, :] * _shift_rows(d_c, -1) + w[1:2, :] * d_c + w[2:3, :] * _shift_rows(d_c, 1)
            du_ref[...] = du[inner].astype(BF16)
            d_in = d_c[inner]
            for tap in range(3):
                dw_ref[tap:tap + 1, :] += jnp.sum(d_in * _shift_rows(uu, 1 - tap)[inner], axis=0, keepdims=True)

    in_specs = (_halo_specs(FFN_ROWS, FFN_COLS, HALO, s, lambda j: j)
                + _halo_specs(FFN_ROWS, FFN_COLS, HALO, s, lambda j: j + nc)
                + _halo_specs(FFN_ROWS, FFN_COLS, HALO, s, lambda j: j)
                + [pl.BlockSpec((3, FFN_COLS), lambda j, i: (0, j)),
                   pl.BlockSpec((3, FFN_COLS), lambda j, i: (0, j + nc))])
    blk = pl.BlockSpec((FFN_ROWS, FFN_COLS), lambda j, i: (i, j))
    acc = pl.BlockSpec((HALO, FFN_COLS), lambda j, i: (0, j))
    return pl.pallas_call(
        body, name=name, grid=(nc, ns), in_specs=in_specs, out_specs=(blk, blk, acc, acc),
        out_shape=(jax.ShapeDtypeStruct((s, D_FF), BF16), jax.ShapeDtypeStruct((s, D_FF), BF16),
                   jax.ShapeDtypeStruct((HALO, D_FF), F32), jax.ShapeDtypeStruct((HALO, D_FF), F32)),
        compiler_params=_params(2),
    )(u, u, u, u, u, u, dact, dact, dact, cw, cw)


CONV_ROWS = 512
CONV_HALO = 16
CONV_PAD = CONV_WIDTH // 2


def _conv_halo_specs(cols, s):
    per = CONV_ROWS // CONV_HALO
    last = s // CONV_HALO - 1
    return [pl.BlockSpec((CONV_HALO, cols), lambda i: (jnp.maximum(i * per - 1, 0), 0)),
            pl.BlockSpec((CONV_ROWS, cols), lambda i: (i, 0)),
            pl.BlockSpec((CONV_HALO, cols), lambda i: (jnp.minimum((i + 1) * per, last), 0))]


def _glu_ext(pp, pc, pn, i, n_i):
    ext = _ext_block(pp, pc, pn, i, n_i)
    return ext[:, :CONV_CH] * _sigmoid(ext[:, CONV_CH:])


def _conv_mixer(pa, cw, cb, lnw, lnb, name):
    s = pa.shape[0]
    ns = s // CONV_ROWS

    def body(pp, pc, pn, cw_ref, cb_ref, lnw_ref, lnb_ref, o_ref, c_ref):
        i = pl.program_id(0)
        a = _glu_ext(pp, pc, pn, i, ns)
        acc = jnp.zeros((CONV_ROWS, CONV_CH), F32)
        for tap in range(CONV_WIDTH):
            acc = acc + cw_ref[tap:tap + 1, :] * _shift_rows(a, -(tap + 1))[:CONV_ROWS]
        cv = acc + cb_ref[...]
        c_ref[...] = cv
        mu = jnp.mean(cv, axis=-1, keepdims=True)
        xc = cv - mu
        rstd = lax.rsqrt(jnp.mean(xc * xc, axis=-1, keepdims=True) + EPS)
        y = xc * rstd * lnw_ref[...] + lnb_ref[...]
        o_ref[...] = (y * _sigmoid(y)).astype(BF16)

    vec = pl.BlockSpec((1, CONV_CH), lambda i: (0, 0))
    blk = pl.BlockSpec((CONV_ROWS, CONV_CH), lambda i: (i, 0))
    return pl.pallas_call(
        body, name=name, grid=(ns,),
        in_specs=_conv_halo_specs(2 * CONV_CH, s) + [pl.BlockSpec((CONV_WIDTH, CONV_CH), lambda i: (0, 0)), vec, vec, vec],
        out_specs=(blk, blk),
        out_shape=(jax.ShapeDtypeStruct((s, CONV_CH), BF16), jax.ShapeDtypeStruct((s, CONV_CH), F32)),
        compiler_params=_params(1))(pa, pa, pa, cw, cb, lnw, lnb)


def _conv_mixer_bwd_ln(cv, dout, lnw, lnb, name):
    s = cv.shape[0]

    def body(c_ref, do_ref, lnw_ref, lnb_ref, dc_ref, dlnw_ref, dlnb_ref, dcb_ref):
        @pl.when(pl.program_id(0) == 0)
        def _():
            dlnw_ref[...] = jnp.zeros_like(dlnw_ref)
            dlnb_ref[...] = jnp.zeros_like(dlnb_ref)
            dcb_ref[...] = jnp.zeros_like(dcb_ref)

        c = c_ref[...]
        mu = jnp.mean(c, axis=-1, keepdims=True)
        xc = c - mu
        rstd = lax.rsqrt(jnp.mean(xc * xc, axis=-1, keepdims=True) + EPS)
        xh = xc * rstd
        w = lnw_ref[...]
        y = xh * w + lnb_ref[...]
        dy = do_ref[...] * _silu_grad(y)
        dlnw_ref[...] += jnp.sum(dy * xh, axis=0, keepdims=True)
        dlnb_ref[...] += jnp.sum(dy, axis=0, keepdims=True)
        dxh = dy * w
        dc = rstd * (dxh - jnp.mean(dxh, axis=-1, keepdims=True) - xh * jnp.mean(dxh * xh, axis=-1, keepdims=True))
        dc_ref[...] = dc
        dcb_ref[...] += jnp.sum(dc, axis=0, keepdims=True)

    vec = pl.BlockSpec((1, CONV_CH), lambda i: (0, 0))
    blk = pl.BlockSpec((CONV_ROWS, CONV_CH), lambda i: (i, 0))
    return pl.pallas_call(
        body, name=name, grid=(s // CONV_ROWS,), in_specs=[blk, blk, vec, vec], out_specs=(blk, vec, vec, vec),
        out_shape=(jax.ShapeDtypeStruct((s, CONV_CH), F32),) + (jax.ShapeDtypeStruct((1, CONV_CH), F32),) * 3,
        compiler_params=_params(1))(cv, dout, lnw, lnb)


def _conv_mixer_bwd_conv(pa, dc, cw, name):
    s = pa.shape[0]
    ns = s // CONV_ROWS

    def body(pp, pc, pn, dp, dcc, dn, cw_ref, dpa_ref, dcw_ref):
        i = pl.program_id(0)

        @pl.when(i == 0)
        def _():
            dcw_ref[...] = jnp.zeros_like(dcw_ref)

        a = _glu_ext(pp, pc, pn, i, ns)
        dce = _ext_block(dp, dcc, dn, i, ns)
        dcur = dcc[...]
        da = jnp.zeros((CONV_ROWS, CONV_CH), F32)
        for tap in range(CONV_WIDTH):
            da = da + cw_ref[tap:tap + 1, :] * _shift_rows(dce, -(CONV_WIDTH - tap))[:CONV_ROWS]
            dcw_ref[tap:tap + 1, :] += jnp.sum(dcur * _shift_rows(a, -(tap + 1))[:CONV_ROWS], axis=0, keepdims=True)
        cur = pc[...]
        val, sg = cur[:, :CONV_CH], _sigmoid(cur[:, CONV_CH:])
        dpa_ref[:, :CONV_CH] = (da * sg).astype(BF16)
        dpa_ref[:, CONV_CH:] = (da * val * sg * (1.0 - sg)).astype(BF16)

    return pl.pallas_call(
        body, name=name, grid=(ns,),
        in_specs=_conv_halo_specs(2 * CONV_CH, s) + _conv_halo_specs(CONV_CH, s)
        + [pl.BlockSpec((CONV_WIDTH, CONV_CH), lambda i: (0, 0))],
        out_specs=(pl.BlockSpec((CONV_ROWS, 2 * CONV_CH), lambda i: (i, 0)),
                   pl.BlockSpec((32, CONV_CH), lambda i: (0, 0))),
        out_shape=(jax.ShapeDtypeStruct((s, 2 * CONV_CH), BF16), jax.ShapeDtypeStruct((32, CONV_CH), F32)),
        compiler_params=_params(1))(pa, pa, pa, dc, dc, dc, cw)


SLOPES = tuple(float(2.0 ** (-ALIBI_MAX_EXP * (h + 1) / N_HEADS)) for h in range(N_HEADS))
ATT_SCALE = HEAD_DIM ** -0.5


PAIR = 2 * HEAD_DIM
N_PAIRS = N_HEADS // 2
ATT_WIN = ATT_BLOCK + 2 * ATT_HALF


def _window_specs(dil, n_steps, col_of):
    rows, halo = ATT_BLOCK * dil, ATT_HALF * dil
    return [pl.BlockSpec((halo, PAIR), lambda i, p: (jnp.maximum(2 * i - 1, 0), col_of(p))),
            pl.BlockSpec((rows, PAIR), lambda i, p: (i, col_of(p))),
            pl.BlockSpec((halo, PAIR), lambda i, p: (jnp.minimum(2 * i + 2, 2 * n_steps - 1), col_of(p)))]


def _residue(ref, r, n, dil):
    return ref[pl.ds(r, n, stride=dil), :] if dil > 1 else ref[pl.ds(r, n), :]


def _residue_window(refs, r, dil):
    prev, cur, nxt = refs
    return jnp.concatenate([_residue(prev, r, ATT_HALF, dil), _residue(cur, r, ATT_BLOCK, dil),
                            _residue(nxt, r, ATT_HALF, dil)], axis=0)


def _band_masks(i, length, dil, transposed):
    shape = (ATT_WIN, ATT_BLOCK) if transposed else (ATT_BLOCK, ATT_WIN)
    row = lax.broadcasted_iota(jnp.int32, shape, 0)
    col = lax.broadcasted_iota(jnp.int32, shape, 1)
    wide = row if transposed else col
    dist = jnp.abs((row - col - ATT_HALF) if transposed else (row + ATT_HALF - col))
    wpos = i * ATT_BLOCK - ATT_HALF + wide
    valid = (dist <= ATT_HALF) & (wpos >= 0) & (wpos < length)
    return valid, dist.astype(F32) * float(dil)


def _attn_branch(qkv, dil, name):
    s = qkv.shape[0]
    rows = ATT_BLOCK * dil
    n_steps = s // rows
    length = s // dil
    nt = (((1,), (1,)), ((), ()))

    def body(q_ref, kp, kc, kn, vp, vc, vn, o_ref, l_ref):
        i, pair = pl.program_id(0), pl.program_id(1)
        valid, distf = _band_masks(i, length, dil, False)
        for r in range(dil):
            q = _residue(q_ref, r, ATT_BLOCK, dil).astype(BF16)
            k = _residue_window((kp, kc, kn), r, dil).astype(BF16)
            v = _residue_window((vp, vc, vn), r, dil).astype(BF16)
            outs, lses = [], []
            for hh in range(2):
                sl = slice(hh * HEAD_DIM, (hh + 1) * HEAD_DIM)
                slope = jnp.where(pair == 0, SLOPES[hh], jnp.where(pair == 1, SLOPES[2 + hh], SLOPES[4 + hh]))
                sc = lax.dot_general(q[:, sl], k[:, sl], nt, preferred_element_type=F32) * ATT_SCALE - slope * distf
                sc = jnp.where(valid, sc, MASK_VALUE)
                m = jnp.max(sc, axis=-1, keepdims=True)
                p = jnp.exp(sc - m)
                den = jnp.sum(p, axis=-1, keepdims=True)
                outs.append(jnp.dot(p.astype(BF16), v[:, sl], preferred_element_type=F32) / den)
                lses.append(jnp.broadcast_to(m + jnp.log(den), (ATT_BLOCK, HEAD_DIM)))
            if dil > 1:
                o_ref[pl.ds(r, ATT_BLOCK, stride=dil), :] = jnp.concatenate(outs, axis=1)
                l_ref[pl.ds(r, ATT_BLOCK, stride=dil), :] = jnp.concatenate(lses, axis=1)
            else:
                o_ref[...] = jnp.concatenate(outs, axis=1)
                l_ref[...] = jnp.concatenate(lses, axis=1)

    out_blk = pl.BlockSpec((rows, PAIR), lambda i, p: (i, p))
    return pl.pallas_call(
        body, name=name, grid=(n_steps, N_PAIRS),
        in_specs=[pl.BlockSpec((rows, PAIR), lambda i, p: (i, p))]
        + _window_specs(dil, n_steps, lambda p: N_PAIRS + p) + _window_specs(dil, n_steps, lambda p: 2 * N_PAIRS + p),
        out_specs=(out_blk, out_blk),
        out_shape=(jax.ShapeDtypeStruct((s, ATT_WIDTH), F32),) * 2,
        compiler_params=_params(2))(qkv, qkv, qkv, qkv, qkv, qkv, qkv)


ATT_ROWS = 512


def _attn_combine(outs, lses, name):
    s = outs[0].shape[0]

    def body(o1, o2, o3, l1, l2, l3, att_ref, att32_ref, lse_ref):
        ls = [l1[...], l2[...], l3[...]]
        m = jnp.maximum(jnp.maximum(ls[0], ls[1]), ls[2])
        es = [jnp.exp(l - m) for l in ls]
        den = es[0] + es[1] + es[2]
        att = (es[0] * o1[...] + es[1] * o2[...] + es[2] * o3[...]) / den
        att_ref[...] = att.astype(BF16)
        att32_ref[...] = att
        lse_ref[...] = m + jnp.log(den)

    blk = pl.BlockSpec((ATT_ROWS, ATT_WIDTH), lambda i: (i, 0))
    return pl.pallas_call(
        body, name=name, grid=(s // ATT_ROWS,), in_specs=[blk] * 6, out_specs=(blk, blk, blk),
        out_shape=(jax.ShapeDtypeStruct((s, ATT_WIDTH), BF16), jax.ShapeDtypeStruct((s, ATT_WIDTH), F32),
                   jax.ShapeDtypeStruct((s, ATT_WIDTH), F32)),
        compiler_params=_params(1))(*outs, *lses)


def _attn_delta(datt, att, name):
    s = att.shape[0]

    def body(d_ref, a_ref, delta_ref):
        prod = d_ref[...] * a_ref[...]
        for h in range(N_HEADS):
            sl = slice(h * HEAD_DIM, (h + 1) * HEAD_DIM)
            delta_ref[:, sl] = jnp.broadcast_to(jnp.sum(prod[:, sl], axis=-1, keepdims=True), (ATT_ROWS, HEAD_DIM))

    blk = pl.BlockSpec((ATT_ROWS, ATT_WIDTH), lambda i: (i, 0))
    return pl.pallas_call(
        body, name=name, grid=(s // ATT_ROWS,), in_specs=[blk, blk], out_specs=blk,
        out_shape=jax.ShapeDtypeStruct((s, ATT_WIDTH), F32), compiler_params=_params(1))(datt, att)


def _attn_branch_bwd(qkv, do, lse, delta, prev, dil, name):
    s = qkv.shape[0]
    rows = ATT_BLOCK * dil
    n_steps = s // rows
    length = s // dil
    has_prev = prev is not None
    tn = (((0,), (0,)), ((), ()))
    nt = (((1,), (1,)), ((), ()))

    def body(*refs):
        qs, ks, vs, dos, ls, des = (refs[3 * n:3 * n + 3] for n in range(6))
        rest = refs[18:]
        if has_prev:
            pq, pk, pv = rest[:3]
            rest = rest[3:]
        dq_ref, dk_ref, dv_ref = rest
        i, pair = pl.program_id(0), pl.program_id(1)
        valid_q, distf_q = _band_masks(i, length, dil, False)
        valid_k, distf_k = _band_masks(i, length, dil, True)
        for r in range(dil):
            cur = lambda t: _residue(t[1], r, ATT_BLOCK, dil)
            q_cur, k_cur, v_cur, do_cur = (cur(t).astype(BF16) for t in (qs, ks, vs, dos))
            l_cur, de_cur = cur(ls), cur(des)
            q_win, k_win, v_win, do_win = (_residue_window(t, r, dil).astype(BF16) for t in (qs, ks, vs, dos))
            l_win, de_win = _residue_window(ls, r, dil), _residue_window(des, r, dil)
            dqs, dks, dvs = [], [], []
            for hh in range(2):
                sl = slice(hh * HEAD_DIM, (hh + 1) * HEAD_DIM)
                one = slice(hh * HEAD_DIM, hh * HEAD_DIM + 1)
                slope = jnp.where(pair == 0, SLOPES[hh], jnp.where(pair == 1, SLOPES[2 + hh], SLOPES[4 + hh]))
                sc = lax.dot_general(q_cur[:, sl], k_win[:, sl], nt, preferred_element_type=F32) * ATT_SCALE - slope * distf_q
                p = jnp.exp(jnp.where(valid_q, sc - l_cur[:, one], MASK_VALUE))
                dp = lax.dot_general(do_cur[:, sl], v_win[:, sl], nt, preferred_element_type=F32)
                ds = (p * (dp - de_cur[:, one]) * ATT_SCALE).astype(BF16)
                dqs.append(jnp.dot(ds, k_win[:, sl], preferred_element_type=F32))

                sc2 = lax.dot_general(q_win[:, sl], k_cur[:, sl], nt, preferred_element_type=F32) * ATT_SCALE - slope * distf_k
                p2 = jnp.exp(jnp.where(valid_k, sc2 - l_win[:, one], MASK_VALUE))
                dvs.append(lax.dot_general(p2.astype(BF16), do_win[:, sl], tn, preferred_element_type=F32))
                dp2 = lax.dot_general(do_win[:, sl], v_cur[:, sl], nt, preferred_element_type=F32)
                ds2 = (p2 * (dp2 - de_win[:, one]) * ATT_SCALE).astype(BF16)
                dks.append(lax.dot_general(ds2, q_win[:, sl], tn, preferred_element_type=F32))
            for parts, acc, out in ((dqs, pq if has_prev else None, dq_ref), (dks, pk if has_prev else None, dk_ref),
                                    (dvs, pv if has_prev else None, dv_ref)):
                val = jnp.concatenate(parts, axis=1)
                if has_prev:
                    val = val + _residue(acc, r, ATT_BLOCK, dil)
                if dil > 1:
                    out[pl.ds(r, ATT_BLOCK, stride=dil), :] = val
                else:
                    out[...] = val

    blk = pl.BlockSpec((rows, PAIR), lambda i, p: (i, p))
    in_specs = (_window_specs(dil, n_steps, lambda p: p) + _window_specs(dil, n_steps, lambda p: N_PAIRS + p)
                + _window_specs(dil, n_steps, lambda p: 2 * N_PAIRS + p) + _window_specs(dil, n_steps, lambda p: p) * 3)
    ins = [qkv] * 9 + [do] * 3 + [lse] * 3 + [delta] * 3
    if has_prev:
        in_specs += [blk] * 3
        ins += list(prev)
    return pl.pallas_call(
        body, name=name, grid=(n_steps, N_PAIRS), in_specs=in_specs, out_specs=(blk, blk, blk),
        out_shape=(jax.ShapeDtypeStruct((s, ATT_WIDTH), F32),) * 3,
        compiler_params=_params(2))(*ins)


TB = 2 * REC_CHUNK
REC_SUB = 8
REC_ROWS = 5 * REC_WIDTH


def _chunk_scan(x, pos, rev):
    del pos
    row = lax.broadcasted_iota(jnp.int32, (TB, TB), 0)
    col = lax.broadcasted_iota(jnp.int32, (TB, TB), 1)
    same = (row < REC_CHUNK) == (col < REC_CHUNK)
    tri = (same & ((row >= col) if rev else (row <= col))).astype(BF16)
    hi = x.astype(BF16)
    rest = x - hi.astype(F32)
    mid = rest.astype(BF16)
    low = (rest - mid.astype(F32)).astype(BF16)
    return (jnp.dot(hi, tri, preferred_element_type=F32) + jnp.dot(mid, tri, preferred_element_type=F32)
            + jnp.dot(low, tri, preferred_element_type=F32))


def _hg_prep(qraw, z, lb, rev):
    lane = lax.broadcasted_iota(jnp.int32, (REC_WIDTH, TB), 1)
    pos = lane & (REC_CHUNK - 1)
    in_a = lane < REC_CHUNK
    sig, sigm = _sigmoid(z), _sigmoid(-z)
    f = lb + (1.0 - lb) * sig
    kk = (1.0 - lb) * sigm
    b = _chunk_scan(jnp.log(jnp.maximum(f, F_TINY)), pos, rev)
    end_a = b[:, 0:1] if rev else b[:, REC_CHUNK - 1:REC_CHUNK]
    end_b = b[:, REC_CHUNK:REC_CHUNK + 1] if rev else b[:, TB - 1:TB]
    bend = jnp.where(in_a, end_a, end_b)
    q = qraw * _sigmoid(qraw)
    sub = ((REC_CHUNK - 1 - pos) if rev else pos) // REC_SUB
    eq, ek = [], []
    for i in range(1, REC_CHUNK // REC_SUB):
        la = (REC_CHUNK - REC_SUB * i) if rev else (REC_SUB * i - 1)
        ri = jnp.where(in_a, b[:, la:la + 1], b[:, la + REC_CHUNK:la + REC_CHUNK + 1])
        eq.append(jnp.where(sub == i, jnp.exp(jnp.minimum(b - ri, 0.0)), 0.0))
        ek.append(jnp.where(sub < i, jnp.exp(jnp.minimum(ri - b, 0.0)), 0.0))
    return dict(pos=pos, in_a=in_a, sig=sig, sigm=sigm, f=f, kk=kk, b=b, end_a=end_a, end_b=end_b,
                q=q, qh=q * jnp.exp(b), kh=kk * jnp.exp(bend - b), ekb=jnp.exp(bend - b), eq=eq, ek=ek,
                pos_sub=lane & (REC_SUB - 1))


def _pair_masks(rev):
    row = lax.broadcasted_iota(jnp.int32, (TB, TB), 0)
    col = lax.broadcasted_iota(jnp.int32, (TB, TB), 1)
    same = (row < REC_CHUNK) == (col < REC_CHUNK)
    scan = lambda p: ((REC_CHUNK - 1 - (p & (REC_CHUNK - 1))) if rev else (p & (REC_CHUNK - 1))) // REC_SUB
    causal = same & ((row >= col) if rev else (row <= col))
    earlier_sub = same & (scan(row) < scan(col))
    return same, causal, earlier_sub, col - row


def _head_rows(x, h):
    return x[h * HEAD_DIM:(h + 1) * HEAD_DIM, :]


def _stack_subs(parts, h):
    return jnp.concatenate([_head_rows(p, h) for p in parts], axis=0)


def _block_diag_mask():
    r = lax.broadcasted_iota(jnp.int32, (REC_WIDTH, REC_WIDTH), 0) // HEAD_DIM
    c = lax.broadcasted_iota(jnp.int32, (REC_WIDTH, REC_WIDTH), 1) // HEAD_DIM
    return (r == c).astype(F32)


def _heads(x):
    return x.reshape(N_HEADS, HEAD_DIM, TB)


def _hg_shift(delta, rev):
    return jnp.where(delta == 0, 0, TB - delta) if rev else delta


def _hg_unshift(delta, rev):
    return delta if rev else jnp.where(delta == 0, 0, TB - delta)


def _hgrn_scan(projt, lb, rev, name):
    s = projt.shape[1]
    nblk = s // TB
    zrow = 2 if rev else 1
    tmap = (lambda i: nblk - 1 - i) if rev else (lambda i: i)
    tn = (((0,), (0,)), ((), ()))
    nt = (((1,), (1,)), ((), ()))

    def body(q_ref, z_ref, v_ref, lb_ref, o_ref, hs_ref, at_ref, h_ref, acc_ref):
        @pl.when(pl.program_id(0) == 0)
        def _():
            h_ref[...] = jnp.zeros_like(h_ref)

        v = v_ref[...]
        vb = v.astype(BF16)
        pr = _hg_prep(q_ref[...], z_ref[...], lb_ref[...], rev)
        q, kk, b, pos_sub = pr["q"], pr["kk"], pr["b"], pr["pos_sub"]
        same, _, _, offset = _pair_masks(rev)
        qt = [q * e for e in pr["eq"]]
        kt = [kk * e for e in pr["ek"]]
        for h in range(N_HEADS):
            sc = lax.dot_general(_stack_subs(kt, h).astype(BF16), _stack_subs(qt, h).astype(BF16), tn,
                                 preferred_element_type=F32)
            acc_ref[h] = jnp.where(same, sc, 0.0)

        def pair_step(delta, carry):
            sh = _hg_shift(delta, rev)
            kd, bd = pltpu.roll(kk, sh, axis=1), pltpu.roll(b, sh, axis=1)
            valid = (pos_sub <= REC_SUB - 1 - delta) if rev else (pos_sub >= delta)
            w = jnp.where(valid, q * kd * jnp.exp(jnp.where(valid, b - bd, 0.0)), 0.0)
            a = jnp.sum(_heads(w), axis=1)
            hit = offset == (-delta if rev else delta)
            for h in range(N_HEADS):
                acc_ref[h] += jnp.where(hit, a[h:h + 1, :], 0.0)
            return carry

        lax.fori_loop(0, REC_SUB, pair_step, 0)
        outs = []
        for h in range(N_HEADS):
            a_bf = acc_ref[h].astype(BF16)
            at_ref[h] = a_bf
            outs.append(jnp.dot(_head_rows(vb, h), a_bf, preferred_element_type=F32))
        o = jnp.concatenate(outs, axis=0)
        bd_mask = _block_diag_mask()
        order = ((1, ~pr["in_a"], pr["end_b"]), (0, pr["in_a"], pr["end_a"]))
        if not rev:
            order = order[::-1]
        for slot, msk, bend in order:
            h0 = h_ref[...]
            hs_ref[slot] = h0
            o = o + lax.dot_general(h0.astype(BF16), jnp.where(msk, pr["qh"], 0.0).astype(BF16), tn,
                                    preferred_element_type=F32)
            upd = lax.dot_general(jnp.where(msk, pr["kh"], 0.0).astype(BF16), vb, nt, preferred_element_type=F32)
            h_ref[...] = jnp.exp(bend) * h0 + upd * bd_mask
        o_ref[...] = o

    row_blk = lambda r: pl.BlockSpec((REC_WIDTH, TB), lambda i: (r, tmap(i)))
    return pl.pallas_call(
        body, name=name, grid=(nblk,),
        in_specs=[row_blk(0), row_blk(zrow), row_blk(3), pl.BlockSpec((REC_WIDTH, 1), lambda i: (0, 0))],
        out_specs=(pl.BlockSpec((REC_WIDTH, TB), lambda i: (0, tmap(i))),
                   pl.BlockSpec((2, REC_WIDTH, REC_WIDTH), lambda i: (tmap(i), 0, 0)),
                   pl.BlockSpec((None, N_HEADS, TB, TB), lambda i: (tmap(i), 0, 0, 0))),
        out_shape=(jax.ShapeDtypeStruct((REC_WIDTH, s), F32),
                   jax.ShapeDtypeStruct((s // REC_CHUNK, REC_WIDTH, REC_WIDTH), F32),
                   jax.ShapeDtypeStruct((nblk, N_HEADS, TB, TB), BF16)),
        scratch_shapes=[pltpu.VMEM((REC_WIDTH, REC_WIDTH), F32), pltpu.VMEM((N_HEADS, TB, TB), F32)],
        compiler_params=_params(1))(projt, projt, projt, lb)


def _hgrn_scan_bwd(projt, lb, dot, hs, at, prev, rev, name):
    s = projt.shape[1]
    nblk = s // TB
    zrow = 2 if rev else 1
    tmap = (lambda i: i) if rev else (lambda i: nblk - 1 - i)
    has_prev = prev is not None
    tn = (((0,), (0,)), ((), ()))
    nt = (((1,), (1,)), ((), ()))

    def body(*refs):
        q_ref, z_ref, v_ref, lb_ref, do_ref, hs_ref, at_ref = refs[:7]
        rest = refs[7:]
        if has_prev:
            pq_ref, pv_ref = rest[:2]
            rest = rest[2:]
        dq_ref, dz_ref, dv_ref, dlb_ref, dh_ref, dat_ref = rest

        @pl.when(pl.program_id(0) == 0)
        def _():
            dh_ref[...] = jnp.zeros_like(dh_ref)
            dlb_ref[...] = jnp.zeros_like(dlb_ref)

        qraw, v, do, lbv = q_ref[...], v_ref[...], do_ref[...], lb_ref[...]
        dob, vb = do.astype(BF16), v.astype(BF16)
        pr = _hg_prep(qraw, z_ref[...], lbv, rev)
        q, kk, b, pos_sub, in_a = pr["q"], pr["kk"], pr["b"], pr["pos_sub"], pr["in_a"]
        _, causal, earlier_sub, offset = _pair_masks(rev)
        qt = [q * e for e in pr["eq"]]
        kt = [kk * e for e in pr["ek"]]
        n_sub = len(qt)
        dq_h, dk_h, dv_h = [], [], []
        for h in range(N_HEADS):
            d_at = jnp.where(causal, lax.dot_general(_head_rows(vb, h), _head_rows(dob, h), tn,
                                                     preferred_element_type=F32), 0.0)
            dat_ref[h] = d_at
            dv_h.append(lax.dot_general(_head_rows(dob, h), at_ref[h], nt, preferred_element_type=F32))
            d_off = jnp.where(earlier_sub, d_at, 0.0).astype(BF16)
            dqt = jnp.dot(_stack_subs(kt, h).astype(BF16), d_off, preferred_element_type=F32)
            dkt = lax.dot_general(_stack_subs(qt, h).astype(BF16), d_off, nt, preferred_element_type=F32)
            dq_h.append(sum(_head_rows(pr["eq"][i], h) * dqt[i * HEAD_DIM:(i + 1) * HEAD_DIM] for i in range(n_sub)))
            dk_h.append(sum(_head_rows(pr["ek"][i], h) * dkt[i * HEAD_DIM:(i + 1) * HEAD_DIM] for i in range(n_sub)))
        dq0, dk0, dv = (jnp.concatenate(t, axis=0) for t in (dq_h, dk_h, dv_h))

        def pair_step(delta, carry):
            dq, dk = carry
            sh, back = _hg_shift(delta, rev), _hg_unshift(delta, rev)
            kd, bd = pltpu.roll(kk, sh, axis=1), pltpu.roll(b, sh, axis=1)
            valid = (pos_sub <= REC_SUB - 1 - delta) if rev else (pos_sub >= delta)
            e = jnp.where(valid, jnp.exp(jnp.where(valid, b - bd, 0.0)), 0.0)
            hit = offset == (-delta if rev else delta)
            da = jnp.concatenate(
                [jnp.broadcast_to(jnp.sum(jnp.where(hit, dat_ref[h], 0.0), axis=0, keepdims=True), (HEAD_DIM, TB))
                 for h in range(N_HEADS)], axis=0)
            dq = dq + da * kd * e
            dk = dk + pltpu.roll(da * q * e, back, axis=1)
            return dq, dk

        dq, dk = lax.fori_loop(0, REC_SUB, pair_step, (dq0, dk0))

        zero = jnp.zeros((REC_WIDTH, TB), F32)
        bd_mask = _block_diag_mask()
        eb = jnp.exp(b)
        const = zero
        order = ((0, in_a, pr["end_a"]), (1, ~in_a, pr["end_b"]))
        if not rev:
            order = order[::-1]
        for slot, msk, bend in order:
            h0 = hs_ref[slot]
            dh1 = dh_ref[...]
            dh1b = dh1.astype(BF16)
            dq = dq + eb * jnp.dot(h0.astype(BF16), jnp.where(msk, do, 0.0).astype(BF16), preferred_element_type=F32)
            dv = dv + lax.dot_general(dh1b, jnp.where(msk, pr["kh"], 0.0).astype(BF16), tn, preferred_element_type=F32)
            dk_int = pr["ekb"] * jnp.dot(dh1b, jnp.where(msk, v, 0.0).astype(BF16), preferred_element_type=F32)
            dk = dk + dk_int
            ebend = jnp.exp(bend)
            c = (jnp.sum(kk * dk_int, axis=1, keepdims=True)
                 + ebend * jnp.sum(h0 * dh1, axis=1, keepdims=True))
            const = const + jnp.where(msk, c, 0.0)
            upd = lax.dot_general(jnp.where(msk, pr["qh"], 0.0).astype(BF16), dob, nt, preferred_element_type=F32)
            dh_ref[...] = ebend * dh1 + upd * bd_mask

        dg = _chunk_scan(q * dq - kk * dk, pr["pos"], not rev) + const
        sig, sigm, f = pr["sig"], pr["sigm"], pr["f"]
        live = f > F_TINY
        inv_f = 1.0 / jnp.maximum(f, F_TINY)
        one_lb = 1.0 - lbv
        dz = sig * sigm * one_lb * (jnp.where(live, dg * inv_f, 0.0) - dk)
        dlb_ref[...] += jnp.sum(sigm * (jnp.where(live, dg * inv_f, 0.0) - dk), axis=1, keepdims=True)
        dqr = dq * _silu_grad(qraw)
        if has_prev:
            dqr = dqr + pq_ref[...]
            dv = dv + pv_ref[...]
        dq_ref[...] = dqr
        dz_ref[...] = dz
        dv_ref[...] = dv

    row_blk = lambda r: pl.BlockSpec((REC_WIDTH, TB), lambda i: (r, tmap(i)))
    blk = pl.BlockSpec((REC_WIDTH, TB), lambda i: (0, tmap(i)))
    col = pl.BlockSpec((REC_WIDTH, 1), lambda i: (0, 0))
    in_specs = [row_blk(0), row_blk(zrow), row_blk(3), col, blk,
                pl.BlockSpec((2, REC_WIDTH, REC_WIDTH), lambda i: (tmap(i), 0, 0)),
                pl.BlockSpec((None, N_HEADS, TB, TB), lambda i: (tmap(i), 0, 0, 0))]
    ins = [projt, projt, projt, lb, dot, hs, at]
    if has_prev:
        in_specs += [blk, blk]
        ins += list(prev)
    t_shape = jax.ShapeDtypeStruct((REC_WIDTH, s), F32)
    return pl.pallas_call(
        body, name=name, grid=(nblk,), in_specs=in_specs, out_specs=(blk, blk, blk, col),
        out_shape=(t_shape, t_shape, t_shape, jax.ShapeDtypeStruct((REC_WIDTH, 1), F32)),
        scratch_shapes=[pltpu.VMEM((REC_WIDTH, REC_WIDTH), F32), pltpu.VMEM((N_HEADS, TB, TB), F32)],
        compiler_params=_params(1))(*ins)


REC_OUT_COLS = 512


def _head_rms(o):
    o3 = o.reshape(N_HEADS, HEAD_DIM, o.shape[1])
    rstd = lax.rsqrt(jnp.mean(o3 * o3, axis=1, keepdims=True) + EPS)
    return o3 * rstd, rstd


def _hgrn_out(of, ob, projt, wn, name):
    s = of.shape[1]

    def body(of_ref, ob_ref, g_ref, wn_ref, o_ref):
        on, _ = _head_rms(of_ref[...] + ob_ref[...])
        g = g_ref[...]
        y = on.reshape(REC_WIDTH, REC_OUT_COLS) * wn_ref[...] * (g * _sigmoid(g))
        o_ref[...] = y.T.astype(BF16)

    blk = pl.BlockSpec((REC_WIDTH, REC_OUT_COLS), lambda i: (0, i))
    return pl.pallas_call(
        body, name=name, grid=(s // REC_OUT_COLS,),
        in_specs=[blk, blk, pl.BlockSpec((REC_WIDTH, REC_OUT_COLS), lambda i: (4, i)),
                  pl.BlockSpec((REC_WIDTH, 1), lambda i: (0, 0))],
        out_specs=pl.BlockSpec((REC_OUT_COLS, REC_WIDTH), lambda i: (i, 0)),
        out_shape=jax.ShapeDtypeStruct((s, REC_WIDTH), BF16), compiler_params=_params(1))(of, ob, projt, wn)


def _hgrn_out_bwd(drec, of, ob, projt, wn, name):
    s = of.shape[1]

    def body(d_ref, of_ref, ob_ref, g_ref, wn_ref, do_ref, dg_ref, dwn_ref):
        @pl.when(pl.program_id(0) == 0)
        def _():
            dwn_ref[...] = jnp.zeros_like(dwn_ref)

        dy = d_ref[...].T
        on3, rstd = _head_rms(of_ref[...] + ob_ref[...])
        on = on3.reshape(REC_WIDTH, REC_OUT_COLS)
        g, wnv = g_ref[...], wn_ref[...]
        dg_ref[...] = dy * on * wnv * _silu_grad(g)
        d_onw = dy * (g * _sigmoid(g))
        dwn_ref[...] += jnp.sum(d_onw * on, axis=1, keepdims=True)
        d_on3 = (d_onw * wnv).reshape(N_HEADS, HEAD_DIM, REC_OUT_COLS)
        do3 = rstd * (d_on3 - on3 * jnp.mean(d_on3 * on3, axis=1, keepdims=True))
        do_ref[...] = do3.reshape(REC_WIDTH, REC_OUT_COLS)

    blk = pl.BlockSpec((REC_WIDTH, REC_OUT_COLS), lambda i: (0, i))
    col = pl.BlockSpec((REC_WIDTH, 1), lambda i: (0, 0))
    t_shape = jax.ShapeDtypeStruct((REC_WIDTH, s), F32)
    return pl.pallas_call(
        body, name=name, grid=(s // REC_OUT_COLS,),
        in_specs=[pl.BlockSpec((REC_OUT_COLS, REC_WIDTH), lambda i: (i, 0)), blk, blk,
                  pl.BlockSpec((REC_WIDTH, REC_OUT_COLS), lambda i: (4, i)), col],
        out_specs=(blk, blk, col),
        out_shape=(t_shape, t_shape, jax.ShapeDtypeStruct((REC_WIDTH, 1), F32)),
        compiler_params=_params(1))(drec, of, ob, projt, wn)


def _lower_bounds(gamma, name):
    def body(g_ref, lb_ref, p_ref):
        g0, g1 = g_ref[0:1, :], g_ref[1:2, :]
        m = jnp.maximum(g0, g1)
        e0, e1 = jnp.exp(g0 - m), jnp.exp(g1 - m)
        p0, p1 = e0 / (e0 + e1), e1 / (e0 + e1)
        lb_ref[...] = (p0 + p1) - p0
        p_ref[0:1, :] = p0
        p_ref[1:2, :] = p1

    n = gamma.shape[1]
    return pl.pallas_call(body, name=name,
                          out_shape=(jax.ShapeDtypeStruct((1, n), F32), jax.ShapeDtypeStruct((2, n), F32)))(gamma)


def _lower_bounds_bwd(dlb1, p, name):
    def body(d_ref, p_ref, o_ref):
        p0, p1, d = p_ref[0:1, :], p_ref[1:2, :], d_ref[...]
        inner = p1 * d
        o_ref[0:1, :] = p0 * (0.0 - inner)
        o_ref[1:2, :] = p1 * (d - inner)

    return pl.pallas_call(body, name=name, out_shape=jax.ShapeDtypeStruct(p.shape, F32))(dlb1, p)


def _split_w_in(w_in):
    return dict(conv=w_in[:, G_CONV[0]:G_CONV[1]], qkv=w_in[:, G_QKV[0]:G_QKV[1]],
                rec_t=w_in[:, G_REC[0]:].T, nat=w_in[:, :G_REC[0]])


def _split_w_rest(w_out, w_up, w_down):
    return dict(out=w_out, out_a=w_out[:CONV_CH], out_b=w_out[CONV_CH:CONV_CH + ATT_WIDTH],
                out_c=w_out[CONV_CH + ATT_WIDTH:], up=w_up, down=w_down)


def _col(v):
    return v.reshape(-1, 1)


def _sequence_step(x, tgt, mods, lbs, small, w_in0, later_weights, final_w):
    saved = []
    xin = x
    big = [_split_w_in(w_in0), None]
    h1 = _resid_norm_mod(x, None, None, small[0]["norm1_w"], mods[0][1:2], mods[0][0:1], "norm1_first")
    for l in range(DEPTH):
        sm, w, md = small[l], big[l], mods[l]
        pa = _matmul(h1, w["conv"], "nn", F32, f"proj_conv")
        qkv = _matmul(h1, w["qkv"], "nn", F32, f"proj_qkv")
        projt = _matmul(w["rec_t"], h1, "nt", F32, f"proj_rec")
        a_out, cv = _conv_mixer(pa, sm["conv_a_w"], sm["conv_a_b"], sm["ln_a_w"], sm["ln_a_b"], f"conv_mixer")
        outs, lses = zip(*[_attn_branch(qkv, d, f"attn_d{d}") for d in DILATIONS])
        att, att32, lse = _attn_combine(outs, lses, f"attn_combine")
        lb_f, lb_b = _col(lbs[l][0]), _col(lbs[l][1])
        of, hsf, atf = _hgrn_scan(projt, lb_f, False, "hgrn_fwd")
        ob, hsb, atb = _hgrn_scan(projt, lb_b, True, "hgrn_rev")
        wn = _col(sm["rec_norm_w"])
        rec = _hgrn_out(of, ob, projt, wn, f"hgrn_out")
        mixed = jnp.concatenate([a_out, att, rec], axis=1)
        if l == 0:
            w_in1, w_out_all, w_up_all, w_down_all = later_weights(rec)
            big[0].update(_split_w_rest(w_out_all[0], w_up_all[0], w_down_all[0]))
            big[1] = dict(_split_w_in(w_in1), **_split_w_rest(w_out_all[1], w_up_all[1], w_down_all[1]))
        r1 = _matmul(mixed, w["out"], "nn", F32, f"out_proj")
        xmid, h2 = _resid_norm_mod(xin, r1, md[2:3], sm["norm2_w"], md[4:5], md[3:4], f"norm2")
        u = _matmul(h2, w["up"], "nn", F32, f"ffn_up")
        act = _ffn_act(u, sm["conv_f_w"], f"ffn_act")
        r2 = _matmul(act, w["down"], "nn", F32, f"ffn_down")
        saved.append(dict(xin=xin, h1=h1, pa=pa, qkv=qkv, projt=projt, cv=cv, att32=att32, lse=lse, of=of, ob=ob,
                          hsf=hsf, hsb=hsb, atf=atf, atb=atb, lb_f=lb_f, lb_b=lb_b, wn=wn, mixed=mixed, r1=r1, xmid=xmid, h2=h2,
                          u=u, act=act, r2=r2))
        if l + 1 < DEPTH:
            nxt = small[l + 1]
            xin, h1 = _resid_norm_mod(xmid, r2, md[5:6], nxt["norm1_w"], mods[l + 1][1:2], mods[l + 1][0:1],
                                      "norm1")
    top = saved[-1]
    loss, dx, dr2, dg2, dfw = _final_loss(top["xmid"], top["r2"], mods[-1][5:6], final_w, tgt, "final_loss")

    grads = [None] * DEPTH
    for l in reversed(range(DEPTH)):
        sm, w, md, sv = small[l], big[l], mods[l], saved[l]
        dact = _matmul(dr2, w["down"], "nt", F32, f"d_act")
        g_down = _matmul(sv["act"], dr2, "tn", F32, f"dw_down")
        dug, duv, dwg, dwv = _ffn_act_bwd(sv["u"], dact, sm["conv_f_w"], f"ffn_act_bwd")
        du = jnp.concatenate([dug, duv], axis=1)
        dh2 = _matmul(du, w["up"], "nt", F32, f"d_h2")
        g_up = _matmul(sv["h2"], du, "tn", F32, f"dw_up")
        dxmid, dr1, dsh2, dsc2, dnw2, dg1 = _norm_bwd(sv["xmid"], [dh2], dx, sm["norm2_w"], md[4:5], md[2:3], sv["r1"],
                                                     f"norm2_bwd")
        dmix_a = _matmul(dr1, w["out_a"], "nt", F32, f"d_mix_a")
        dmix_b = _matmul(dr1, w["out_b"], "nt", F32, f"d_mix_b")
        dmix_c = _matmul(dr1, w["out_c"], "nt", F32, f"d_mix_c")
        g_out = _matmul(sv["mixed"], dr1, "tn", F32, f"dw_out")
        dc, dlnw, dlnb, dcb = _conv_mixer_bwd_ln(sv["cv"], dmix_a, sm["ln_a_w"], sm["ln_a_b"], f"conv_mixer_bwd_ln")
        dpa, dcw = _conv_mixer_bwd_conv(sv["pa"], dc, sm["conv_a_w"], f"conv_mixer_bwd_conv")
        delta = _attn_delta(dmix_b, sv["att32"], "attn_delta")
        dqkv = None
        for d in DILATIONS:
            dqkv = _attn_branch_bwd(sv["qkv"], dmix_b, sv["lse"], delta, dqkv, d, f"attn_bwd_d{d}")
        dot, dgt, dwn = _hgrn_out_bwd(dmix_c, sv["of"], sv["ob"], sv["projt"], sv["wn"], f"hgrn_out_bwd")
        dqf, dzf, dvf, dlbf = _hgrn_scan_bwd(sv["projt"], sv["lb_f"], dot, sv["hsf"], sv["atf"], None, False,
                                             "hgrn_fwd_bwd")
        dqt, dzb, dvt, dlbb = _hgrn_scan_bwd(sv["projt"], sv["lb_b"], dot, sv["hsb"], sv["atb"], (dqf, dvf), True,
                                             "hgrn_rev_bwd")
        dprojt = jnp.concatenate([dqt, dzf, dzb, dvt, dgt], axis=0).astype(BF16)
        dnat = jnp.concatenate([dpa] + [t.astype(BF16) for t in dqkv], axis=1)
        dh1_a = _matmul(dnat, w["nat"], "nt", F32, f"d_h1_nat")
        dh1_b = _matmul(dprojt, w["rec_t"], "tn", F32, f"d_h1_rec")
        g_in_nat = _matmul(sv["h1"], dnat, "tn", F32, f"dw_in_nat")
        g_in_rec_t = _matmul(dprojt, sv["h1"], "nn", F32, f"dw_in_rec")
        g_in = jnp.concatenate([g_in_nat, g_in_rec_t.T], axis=1)
        if l > 0:
            below = saved[l - 1]
            dx, dr2, dsh1, dsc1, dnw1, dg2_below = _norm_bwd(sv["xin"], [dh1_a, dh1_b], dxmid, sm["norm1_w"], md[1:2],
                                                            mods[l - 1][5:6], below["r2"], f"norm1_bwd")
        else:
            dx, dsh1, dsc1, dnw1 = _norm_bwd(sv["xin"], [dh1_a, dh1_b], dxmid, sm["norm1_w"], md[1:2], None, None,
                                             f"norm1_bwd")
        grads[l] = dict(w_in=g_in, w_out=g_out, w_up=g_up, w_down=g_down,
                        mod=[dsh1, dsc1, dg1, dsh2, dsc2, dg2], norm1_w=dnw1, conv_a_w=dcw[:CONV_WIDTH], conv_a_b=dcb,
                        ln_a_w=dlnw, ln_a_b=dlnb, lb=jnp.concatenate([dlbf.reshape(1, -1), dlbb.reshape(1, -1)], axis=0),
                        rec_norm_w=dwn.reshape(1, -1), norm2_w=dnw2,
                        conv_f_w=jnp.concatenate([dwg[:3], dwv[:3]], axis=1))
        if l > 0:
            dg2 = dg2_below
    return loss[0, 0], dx, grads, dfw


def _adamw_math(w, g, m, v):
    m = ADAM_B1 * m + (1.0 - ADAM_B1) * g
    v = ADAM_B2 * v + (1.0 - ADAM_B2) * (g * g)
    m_hat = m / (1.0 - ADAM_B1 ** ADAM_STEP)
    v_hat = v / (1.0 - ADAM_B2 ** ADAM_STEP)
    delta = -ADAM_LR * (m_hat / (jnp.sqrt(v_hat) + ADAM_EPS) + ADAM_WD * w)
    return delta, m, v


def _row_tile(rows, cols, max_elems=384 * 1024):
    best = None
    for t in range(8, rows + 1, 8):
        if rows % t == 0 and t * cols <= max_elems:
            best = t
    return best or rows


def _adamw(w, g, m, v, name):
    nl, r, c = w.shape
    tr = _row_tile(r, c)

    def body(w_ref, g_ref, m_ref, v_ref, d_ref, m2_ref, v2_ref):
        d_ref[...], m2_ref[...], v2_ref[...] = _adamw_math(w_ref[...], g_ref[...], m_ref[...], v_ref[...])

    blk = pl.BlockSpec((None, tr, c), lambda l, i: (l, i, 0))
    shape = jax.ShapeDtypeStruct((nl, r, c), F32)
    return pl.pallas_call(body, name=name, grid=(nl, r // tr), in_specs=[blk] * 4, out_specs=(blk, blk, blk),
                          out_shape=(shape, shape, shape), compiler_params=_params(2))(w, g, m, v)


ADA_SHARD = N_MOD * D_MODEL // 4
ADA_COLS = 512
ADA_ROWS = 256
HIGHEST = lax.Precision.HIGHEST


def _ada_mod(c_all, w_ada, b_sh, name):
    def body(c_ref, w_ref, b_ref, o_ref):
        cv = c_ref[...]
        o_ref[...] = jnp.dot(cv * _sigmoid(cv), w_ref[...], precision=HIGHEST, preferred_element_type=F32) + b_ref[...]

    return pl.pallas_call(
        body, name=name, grid=(DEPTH, ADA_SHARD // ADA_COLS),
        in_specs=[pl.BlockSpec((8, D_MODEL), lambda l, j: (0, 0)),
                  pl.BlockSpec((None, D_MODEL, ADA_COLS), lambda l, j: (l, 0, j)),
                  pl.BlockSpec((None, 1, ADA_COLS), lambda l, j: (l, 0, j))],
        out_specs=pl.BlockSpec((None, 8, ADA_COLS), lambda l, j: (l, 0, j)),
        out_shape=jax.ShapeDtypeStruct((DEPTH, 8, ADA_SHARD), F32), compiler_params=_params(2))(c_all, w_ada, b_sh)


def _ada_update(c_all, dmod_sh, w, m, v, name):
    def body(c_ref, d_ref, w_ref, m_ref, v_ref, g_ref, dl_ref, m2_ref, v2_ref):
        cv = c_ref[...]
        g = lax.dot_general(cv * _sigmoid(cv), d_ref[...], (((0,), (0,)), ((), ())), precision=HIGHEST,
                            preferred_element_type=F32)
        g_ref[...] = g
        dl_ref[...], m2_ref[...], v2_ref[...] = _adamw_math(w_ref[...], g, m_ref[...], v_ref[...])

    blk = pl.BlockSpec((None, ADA_ROWS, ADA_SHARD), lambda l, i: (l, i, 0))
    shape = jax.ShapeDtypeStruct((DEPTH, D_MODEL, ADA_SHARD), F32)
    return pl.pallas_call(
        body, name=name, grid=(DEPTH, D_MODEL // ADA_ROWS),
        in_specs=[pl.BlockSpec((8, ADA_ROWS), lambda l, i: (0, i)),
                  pl.BlockSpec((None, 8, ADA_SHARD), lambda l, i: (l, 0, 0)), blk, blk, blk],
        out_specs=(blk,) * 4, out_shape=(shape,) * 4, compiler_params=_params(2))(c_all, dmod_sh, w, m, v)


def _sum_devices(packs, name):
    def body(p_ref, o_ref):
        acc = p_ref[0]
        for dev in range(1, 8):
            acc = acc + p_ref[dev]
        o_ref[...] = acc

    return pl.pallas_call(body, name=name, out_shape=jax.ShapeDtypeStruct(packs.shape[1:], F32))(packs)


def _mesh_pos():
    return lax.axis_index("x"), lax.axis_index("y"), lax.axis_index("c")


def _flip(v, bit):
    return 1 - v if bit else v


def _allgather_devices(x, name):
    m_per, n = x.shape

    def body(x_ref, out_ref, send_sems, recv_sems, local_sem):
        ix, iy, ic = _mesh_pos()
        me, sibling = (ix, iy, ic), (ix, iy, 1 - ic)
        chips = [(1 - ix, iy), (ix, 1 - iy), (1 - ix, 1 - iy)]

        def rows(px, py, pc):
            return out_ref.at[pl.ds((4 * px + 2 * py + pc) * m_per, m_per), :]

        def copy(k, block, to, src=None):
            return pltpu.make_async_remote_copy(
                src_ref=rows(*block) if src is None else src, dst_ref=rows(*block),
                send_sem=send_sems.at[k], recv_sem=recv_sems.at[k], device_id=to, device_id_type=MESH)

        mine = pltpu.make_async_copy(x_ref, rows(*me), local_sem)
        mine.start()
        first = [copy(0, me, sibling, src=x_ref)]
        first += [copy(1 + j, me, (*chip, ic), src=x_ref) for j, chip in enumerate(chips)]
        for cp in first:
            cp.start()
        passed = [copy(4 + j, (*chip, ic), sibling) for j, chip in enumerate(chips)]
        for j, chip in enumerate(chips):
            copy(1 + j, (*chip, ic), me).wait_recv()
            passed[j].start()
        copy(0, sibling, me).wait_recv()
        for j, chip in enumerate(chips):
            copy(4 + j, (*chip, 1 - ic), me).wait_recv()
        for cp in first + passed:
            cp.wait_send()
        mine.wait()

    return pl.pallas_call(
        body, name=name, out_shape=jax.ShapeDtypeStruct((8 * m_per, n), x.dtype),
        in_specs=[pl.BlockSpec(memory_space=pltpu.VMEM)], out_specs=pl.BlockSpec(memory_space=pltpu.VMEM),
        scratch_shapes=[pltpu.SemaphoreType.DMA((7,)), pltpu.SemaphoreType.DMA((7,)), pltpu.SemaphoreType.DMA],
    )(x)


def _gather_chips(shards, name):
    n = len(shards)

    def body(*refs):
        ins, outs = refs[:n], refs[n:2 * n]
        send_sems, recv_sems, local_sems = refs[2 * n:]
        ix, iy, ic = _mesh_pos()
        me = 2 * ix + iy
        local = [pltpu.make_async_copy(ins[a], outs[a].at[me], local_sems.at[a]) for a in range(n)]
        for cp in local:
            cp.start()
        remote = []
        for a in range(n):
            for k in (1, 2, 3):
                px, py = _flip(ix, k & 2), _flip(iy, k & 1)
                sems = dict(send_sem=send_sems.at[3 * a + k - 1], recv_sem=recv_sems.at[3 * a + k - 1],
                            device_id=(px, py, ic), device_id_type=MESH)
                out_cp = pltpu.make_async_remote_copy(src_ref=ins[a], dst_ref=outs[a].at[me], **sems)
                in_cp = pltpu.make_async_remote_copy(src_ref=ins[a], dst_ref=outs[a].at[2 * px + py], **sems)
                out_cp.start()
                remote.append((out_cp, in_cp))
        for out_cp, in_cp in remote:
            out_cp.wait_send()
            in_cp.wait_recv()
        for cp in local:
            cp.wait()

    return pl.pallas_call(
        body, name=name, in_specs=[ANY] * n, out_specs=tuple([ANY] * n),
        out_shape=tuple(jax.ShapeDtypeStruct((4,) + t.shape, t.dtype) for t in shards),
        scratch_shapes=[pltpu.SemaphoreType.DMA((3 * n,)), pltpu.SemaphoreType.DMA((3 * n,)),
                        pltpu.SemaphoreType.DMA((n,))],
    )(*shards)


HBM = pl.BlockSpec(memory_space=pltpu.HBM)
SEM = pl.BlockSpec(memory_space=pltpu.SEMAPHORE)
DATAFLOW = pltpu.SideEffectType.DATAFLOW_SIDE_EFFECTING


def _peer_chip(ix, iy, k):
    return _flip(ix, k & 2), _flip(iy, k & 1)


def _gather_chips_start(shards, name):
    n = len(shards)

    def body(*refs):
        src, land = refs[:n], refs[n:2 * n]
        send_sems, recv_sems = refs[2 * n], refs[2 * n + 1]
        token = refs[-1]
        ix, iy, ic = _mesh_pos()
        me = 2 * ix + iy
        for a in range(n):
            for k in (1, 2, 3):
                px, py = _peer_chip(ix, iy, k)
                pltpu.make_async_remote_copy(
                    src_ref=src[a], dst_ref=land[a].at[me], send_sem=send_sems.at[3 * a + k - 1],
                    recv_sem=recv_sems.at[3 * a + k - 1], device_id=(px, py, ic), device_id_type=MESH).start()
        token[...] = jnp.zeros_like(token)

    hbm = lambda shape, dtype: pltpu.HBM(shape, dtype)
    operands = ([pltpu.with_memory_space_constraint(t, pltpu.HBM) for t in shards]
                + [pltpu.with_memory_space_constraint(lax.empty((4,) + t.shape, t.dtype), pltpu.HBM) for t in shards])
    return pl.pallas_call(
        body, name=name,
        out_shape=(pltpu.SemaphoreType.DMA((3 * n,)), pltpu.SemaphoreType.DMA((3 * n,)),
                   *[hbm(t.shape, t.dtype) for t in shards], *[hbm((4,) + t.shape, t.dtype) for t in shards],
                   jax.ShapeDtypeStruct((8, LANES), F32)),
        in_specs=(HBM,) * (2 * n),
        out_specs=(SEM, SEM) + (HBM,) * (2 * n) + (pl.BlockSpec(memory_space=pltpu.VMEM),),
        input_output_aliases={a: 2 + a for a in range(2 * n)},
        compiler_params=pltpu.CompilerParams(has_side_effects=DATAFLOW),
    )(*operands)


def _gather_chips_wait(started, after, name):
    send_sems, recv_sems = started[0], started[1]
    thru = started[2:-1]
    n = len(thru) // 2

    def body(*refs):
        src, land = refs[:n], refs[n:2 * n]
        send_sems, recv_sems = refs[2 * n], refs[2 * n + 1]
        ix, iy, ic = _mesh_pos()
        for a in range(n):
            for k in (1, 2, 3):
                px, py = _peer_chip(ix, iy, k)
                cp = pltpu.make_async_remote_copy(
                    src_ref=src[a], dst_ref=land[a].at[2 * px + py], send_sem=send_sems.at[3 * a + k - 1],
                    recv_sem=recv_sems.at[3 * a + k - 1], device_id=(px, py, ic), device_id_type=MESH)
                cp.wait_send()
                cp.wait_recv()

    outs = pl.pallas_call(
        body, name=name,
        out_shape=tuple(pltpu.HBM(t.shape, t.dtype) for t in thru),
        in_specs=(HBM,) * (2 * n) + (SEM, SEM, ANY), out_specs=(HBM,) * (2 * n),
        input_output_aliases={a: a for a in range(2 * n)},
        compiler_params=pltpu.CompilerParams(has_side_effects=DATAFLOW),
    )(*thru, send_sems, recv_sems, after)
    return outs[:n], outs[n:]


BIG_KINDS = (("w_in", "col", D_MODEL, IN_COLS), ("w_out", "row", D_MODEL, D_MODEL),
             ("w_up", "col", D_MODEL, 2 * D_FF), ("w_down", "row", D_FF, D_MODEL))


def _piece_shape(how, r, c):
    return (r // 2, c // 4) if how == "col" else (r // 8, c)


def _aligned(start, multiple):
    return start if isinstance(start, int) else pl.multiple_of(start, multiple)


def _piece(ref, how, r, c, chip, half):
    if how == "col":
        return ref.at[pl.ds(_aligned(half * (r // 2), 8), r // 2), pl.ds(_aligned(chip * (c // 4), LANES), c // 4)]
    n = r // 4
    return ref.at[pl.ds(_aligned(chip * n + half * (n // 2), 8), n // 2), :]


def _rs_pair_exchange(grads, name):
    nk = len(BIG_KINDS)
    flat = [grads[ki][l] for ki in range(nk) for l in range(DEPTH)]
    per = DEPTH * 4

    def body(*refs):
        g, land = refs[:nk * DEPTH], refs[nk * DEPTH:nk * DEPTH + nk]
        send_sems, recv_sems = refs[nk * DEPTH + nk:]
        ix, iy, ic = _mesh_pos()
        sibling = (ix, iy, 1 - ic)
        copies = []
        for ki, (_, how, r, c) in enumerate(BIG_KINDS):
            for l in range(DEPTH):
                for j in range(4):
                    sem = ki * per + l * 4 + j
                    rem = pltpu.make_async_remote_copy(
                        src_ref=_piece(g[ki * DEPTH + l], how, r, c, j, 1 - ic), dst_ref=land[ki].at[l, j],
                        send_sem=send_sems.at[sem], recv_sem=recv_sems.at[sem], device_id=sibling, device_id_type=MESH)
                    rem.start()
                    copies.append(rem)
        for rem in copies:
            rem.wait_send()
            rem.wait_recv()

    shapes = [jax.ShapeDtypeStruct((DEPTH, 4) + _piece_shape(how, r, c), F32) for _, how, r, c in BIG_KINDS]
    return pl.pallas_call(
        body, name=name, in_specs=[ANY] * len(flat), out_specs=tuple([ANY] * nk), out_shape=tuple(shapes),
        scratch_shapes=[pltpu.SemaphoreType.DMA((nk * per,))] * 2,
    )(*flat)


def _pair_sum(g, theirs, layer, how, core, name):
    r, c = g.shape
    pr, pc = _piece_shape(how, r, c)
    if how == "col":
        mine_spec = pl.BlockSpec((pr, pc), lambda j, core_ref: (core_ref[0], j))
    else:
        mine_spec = pl.BlockSpec((pr, pc), lambda j, core_ref: (2 * j + core_ref[0], 0))

    def body(core_ref, g_ref, t_ref, o_ref, ob_ref):
        total = g_ref[...] + t_ref[...]
        o_ref[...] = total
        ob_ref[...] = total.astype(BF16)

    out_blk = pl.BlockSpec((None, pr, pc), lambda j, core_ref: (j, 0, 0))
    return pl.pallas_call(
        body, name=name,
        grid_spec=pltpu.PrefetchScalarGridSpec(
            num_scalar_prefetch=1, grid=(4,),
            in_specs=[mine_spec, pl.BlockSpec((None, None, pr, pc), lambda j, core_ref: (layer, j, 0, 0))],
            out_specs=(out_blk, out_blk)),
        out_shape=(jax.ShapeDtypeStruct((4, pr, pc), F32), jax.ShapeDtypeStruct((4, pr, pc), BF16)),
        compiler_params=_params(1))(core, g, theirs)


def _rs_chip_exchange(pair_sums, name):
    nk = len(pair_sums)
    flat = [pair_sums[ki][l] for ki in range(nk) for l in range(DEPTH)]

    def body(*refs):
        src, dst = refs[:nk * DEPTH], refs[nk * DEPTH:nk * DEPTH + nk]
        send_sems, recv_sems = refs[nk * DEPTH + nk:]
        ix, iy, ic = _mesh_pos()
        copies = []
        for ki in range(nk):
            for l in range(DEPTH):
                for k in (1, 2, 3):
                    px, py = _flip(ix, k & 2), _flip(iy, k & 1)
                    sem = (ki * DEPTH + l) * 3 + k - 1
                    rem = pltpu.make_async_remote_copy(
                        src_ref=src[ki * DEPTH + l].at[2 * px + py], dst_ref=dst[ki].at[l, k - 1],
                        send_sem=send_sems.at[sem], recv_sem=recv_sems.at[sem], device_id=(px, py, ic), device_id_type=MESH)
                    rem.start()
                    copies.append(rem)
        for rem in copies:
            rem.wait_send()
            rem.wait_recv()

    return pl.pallas_call(
        body, name=name, in_specs=[ANY] * len(flat), out_specs=tuple([ANY] * nk),
        out_shape=tuple(jax.ShapeDtypeStruct((DEPTH, 3) + pair_sums[ki][0].shape[1:], pair_sums[ki][0].dtype)
                        for ki in range(nk)),
        scratch_shapes=[pltpu.SemaphoreType.DMA((nk * DEPTH * 3,))] * 2,
    )(*flat)


def _chip_sum(own, others, layer, chip, name):
    _, pr, pc = own.shape

    def body(chip_ref, own_ref, s1, s2, s3, o_ref):
        o_ref[...] = ((own_ref[...] + s1[...].astype(F32)) + s2[...].astype(F32)) + s3[...].astype(F32)

    slot = lambda k: pl.BlockSpec((None, None, pr, pc), lambda i, chip_ref: (layer, k, 0, 0))
    return pl.pallas_call(
        body, name=name,
        grid_spec=pltpu.PrefetchScalarGridSpec(
            num_scalar_prefetch=1, grid=(1,),
            in_specs=[pl.BlockSpec((None, pr, pc), lambda i, chip_ref: (chip_ref[0], 0, 0)), slot(0), slot(1), slot(2)],
            out_specs=pl.BlockSpec((pr, pc), lambda i, chip_ref: (0, 0))),
        out_shape=jax.ShapeDtypeStruct((pr, pc), F32), compiler_params=_params(1))(chip, own, others, others, others)


def _rs_pair_share(halves, name):
    nk = len(halves)
    flat = [halves[ki][l] for ki in range(nk) for l in range(DEPTH)]

    def body(*refs):
        src, dst = refs[:nk * DEPTH], refs[nk * DEPTH:nk * DEPTH + nk]
        send_sems, recv_sems = refs[nk * DEPTH + nk:]
        ix, iy, ic = _mesh_pos()
        copies = []
        for ki in range(nk):
            for l in range(DEPTH):
                sem = ki * DEPTH + l
                rem = pltpu.make_async_remote_copy(
                    src_ref=src[sem], dst_ref=dst[ki].at[l], send_sem=send_sems.at[sem], recv_sem=recv_sems.at[sem],
                    device_id=(ix, iy, 1 - ic), device_id_type=MESH)
                rem.start()
                copies.append(rem)
        for rem in copies:
            rem.wait_send()
            rem.wait_recv()

    return pl.pallas_call(
        body, name=name, in_specs=[ANY] * len(flat), out_specs=tuple([ANY] * nk),
        out_shape=tuple(jax.ShapeDtypeStruct((DEPTH,) + halves[ki][0].shape, F32) for ki in range(nk)),
        scratch_shapes=[pltpu.SemaphoreType.DMA((nk * DEPTH,))] * 2,
    )(*flat)


def _adamw_halves(w, mine, theirs, m, v, core, name):
    nl, pr, pc = theirs.shape
    shape = w.shape
    view = lambda t: t.reshape(nl, 2, pr, pc)
    tr = _row_tile(pr, pc, 256 * 1024)

    def body(core_ref, w_ref, a0_ref, a1_ref, t_ref, m_ref, v_ref, g_ref, d_ref, m2_ref, v2_ref):
        own = jnp.where(pl.program_id(0) == 0, a0_ref[...], a1_ref[...])
        g = jnp.where(pl.program_id(1) == core_ref[0], own, t_ref[...])
        g_ref[...] = g
        d_ref[...], m2_ref[...], v2_ref[...] = _adamw_math(w_ref[...], g, m_ref[...], v_ref[...])

    blk = pl.BlockSpec((None, None, tr, pc), lambda l, h, i, core_ref: (l, h, i, 0))
    own_blk = pl.BlockSpec((tr, pc), lambda l, h, i, core_ref: (i, 0))
    out = jax.ShapeDtypeStruct((nl, 2, pr, pc), F32)
    outs = pl.pallas_call(
        body, name=name,
        grid_spec=pltpu.PrefetchScalarGridSpec(
            num_scalar_prefetch=1, grid=(nl, 2, pr // tr),
            in_specs=[blk, own_blk, own_blk, pl.BlockSpec((None, tr, pc), lambda l, h, i, core_ref: (l, i, 0)), blk, blk],
            out_specs=(blk,) * 4),
        out_shape=(out,) * 4, compiler_params=_params(3),
    )(core, view(w), mine[0], mine[1], theirs, view(m), view(v))
    return tuple(t.reshape(shape) for t in outs)


def _reduce_scatter_big(grads, core, chip):
    theirs = _rs_pair_exchange(grads, "rs_pair_exchange")
    pair_sums = [[_pair_sum(grads[ki][l], theirs[ki], l, how, core, f"rs_pair_sum_{kind}") for l in range(DEPTH)]
                 for ki, (kind, how, _, _) in enumerate(BIG_KINDS)]
    slots = _rs_chip_exchange([[both[1] for both in row] for row in pair_sums], "rs_chip_exchange")
    halves = [[_chip_sum(pair_sums[ki][l][0], slots[ki], l, chip, f"rs_chip_sum_{kind}") for l in range(DEPTH)]
              for ki, (kind, _, _, _) in enumerate(BIG_KINDS)]
    other = _rs_pair_share(halves, "rs_pair_share")
    return list(zip(halves, other))


WEIGHT_NAMES = ("w_ada", "b_ada", "norm1_w", "w_in", "conv_a_w", "conv_a_b", "ln_a_w", "ln_a_b", "lb_gamma",
                "rec_norm_w", "w_out", "norm2_w", "w_up", "conv_f_w", "w_down", "final_norm_w")
SMALL_PARAMS = (("b_ada", (DEPTH, N_MOD * D_MODEL), None), ("norm1_w", (DEPTH, D_MODEL), None),
                ("conv_a_w", (DEPTH, CONV_WIDTH, CONV_CH), 2), ("conv_a_b", (DEPTH, CONV_CH), None),
                ("ln_a_w", (DEPTH, CONV_CH), None), ("ln_a_b", (DEPTH, CONV_CH), None),
                ("lb_gamma", (DEPTH, 2, REC_WIDTH), 2), ("rec_norm_w", (DEPTH, REC_WIDTH), None),
                ("norm2_w", (DEPTH, D_MODEL), None), ("conv_f_w", (DEPTH, 3, 2 * D_FF), 2),
                ("final_norm_w", (D_MODEL,), None))


def _pack_rows(parts):
    flat = jnp.concatenate([p.reshape(-1) for p in parts])
    total = flat.shape[0]
    padded = -(-total // (8 * LANES)) * (8 * LANES)
    return jnp.pad(flat, (0, padded - total)).reshape(padded // LANES, LANES)


def _unpack(flat, shapes):
    out, off = [], 0
    for shp in shapes:
        size = int(np.prod(shp))
        out.append(flat[off:off + size].reshape(shp))
        off += size
    return out


def _unstack_chips(t, axis):
    return jnp.concatenate([t[j] for j in range(4)], axis=axis)


def kernel(x, c, w_ada, b_ada, norm1_w, w_in, conv_a_w, conv_a_b, ln_a_w, ln_a_b, lb_gamma, rec_norm_w, w_out, norm2_w, w_up, conv_f_w, w_down, final_norm_w, loss_target, m_w_ada, m_b_ada, m_norm1_w, m_w_in, m_conv_a_w, m_conv_a_b, m_ln_a_w, m_ln_a_b, m_lb_gamma, m_rec_norm_w, m_w_out, m_norm2_w, m_w_up, m_conv_f_w, m_w_down, m_final_norm_w, v_w_ada, v_b_ada, v_norm1_w, v_w_in, v_conv_a_w, v_conv_a_b, v_ln_a_w, v_ln_a_b, v_lb_gamma, v_rec_norm_w, v_w_out, v_norm2_w, v_w_up, v_conv_f_w, v_w_down, v_final_norm_w):
    params = dict(zip(WEIGHT_NAMES, (w_ada, b_ada, norm1_w, w_in, conv_a_w, conv_a_b, ln_a_w, ln_a_b, lb_gamma,
                                     rec_norm_w, w_out, norm2_w, w_up, conv_f_w, w_down, final_norm_w)))
    mom1 = dict(zip(WEIGHT_NAMES, (m_w_ada, m_b_ada, m_norm1_w, m_w_in, m_conv_a_w, m_conv_a_b, m_ln_a_w, m_ln_a_b,
                                   m_lb_gamma, m_rec_norm_w, m_w_out, m_norm2_w, m_w_up, m_conv_f_w, m_w_down,
                                   m_final_norm_w)))
    mom2 = dict(zip(WEIGHT_NAMES, (v_w_ada, v_b_ada, v_norm1_w, v_w_in, v_conv_a_w, v_conv_a_b, v_ln_a_w, v_ln_a_b,
                                   v_lb_gamma, v_rec_norm_w, v_w_out, v_norm2_w, v_w_up, v_conv_f_w, v_w_down,
                                   v_final_norm_w)))
    ix, iy, ic = _mesh_pos()
    chip = 2 * ix + iy
    dev = 2 * chip + ic

    c_all = _allgather_devices(c.reshape(8, LANES), "gather_cond").reshape(8, D_MODEL)
    b_sh = lax.dynamic_slice_in_dim(b_ada, chip * ADA_SHARD, ADA_SHARD, axis=1)
    mod_sh = _ada_mod(c_all, w_ada, b_sh.reshape(DEPTH, 1, ADA_SHARD), "ada_mod")
    w_in_b, w_out_b, w_up_b, w_down_b = (t.astype(BF16) for t in (w_in, w_out, w_up, w_down))
    first = _gather_chips([mod_sh, conv_a_w, conv_f_w, lb_gamma, w_in_b[0]], "gather_first")
    later = [w_in_b[1], w_out_b, w_up_b, w_down_b]
    started = _gather_chips_start(later, "gather_rest_start")
    mod_mine = lax.dynamic_index_in_dim(first[0], dev, axis=2, keepdims=False) + started[-1][0, 0]
    mods = [jnp.concatenate([mod_mine[j, l] for j in range(4)]).reshape(N_MOD, D_MODEL) for l in range(DEPTH)]
    conv_a_w_f, conv_f_w_f, gamma_f = (_unstack_chips(first[k], 2) for k in (1, 2, 3))
    w_in0 = _unstack_chips(first[4], 1)

    def later_weights(after):
        own, lands = _gather_chips_wait(started, after, "gather_rest_wait")
        full = [lax.dynamic_update_index_in_dim(land, mine, chip, 0) for land, mine in zip(lands, own)]
        return (_unstack_chips(full[0], 1), _unstack_chips(full[1], 1), _unstack_chips(full[2], 2),
                _unstack_chips(full[3], 1))

    lb1, p_soft = _lower_bounds(gamma_f.reshape(DEPTH, 2 * REC_WIDTH), "lower_bounds")
    lbs = [jnp.zeros((2, REC_WIDTH), F32), lb1.reshape(2, REC_WIDTH)]
    small = []
    for l in range(DEPTH):
        small.append(dict(norm1_w=norm1_w[l][None], conv_a_w=conv_a_w_f[l], conv_a_b=conv_a_b[l][None],
                          ln_a_w=ln_a_w[l][None], ln_a_b=ln_a_b[l][None], rec_norm_w=rec_norm_w[l],
                          norm2_w=norm2_w[l][None], conv_f_w=conv_f_w_f[l]))

    loss, dx, grads, dfw = _sequence_step(x[0], loss_target[0], mods, lbs, small, w_in0, later_weights,
                                          final_norm_w[None])
    loss = lax.psum(loss, ("x", "y", "c"))

    dgamma = _lower_bounds_bwd(grads[1]["lb"].reshape(1, 2 * REC_WIDTH), p_soft, "lower_bounds_bwd")
    dmod = [jnp.concatenate(grads[l]["mod"], axis=1) for l in range(DEPTH)]
    stack = lambda key: jnp.stack([grads[l][key] for l in range(DEPTH)])
    local_small = dict(b_ada=jnp.concatenate(dmod, axis=0), norm1_w=stack("norm1_w"), conv_a_w=stack("conv_a_w"),
                       conv_a_b=stack("conv_a_b"), ln_a_w=stack("ln_a_w"), ln_a_b=stack("ln_a_b"), lb_gamma=dgamma,
                       rec_norm_w=stack("rec_norm_w"), norm2_w=stack("norm2_w"), conv_f_w=stack("conv_f_w"),
                       final_norm_w=dfw)
    pack = _pack_rows([local_small[name] for name, _, _ in SMALL_PARAMS])
    rows = pack.shape[0]
    packs = _allgather_devices(pack, "gather_small_grads").reshape(8, rows, LANES)
    summed = _sum_devices(packs, "sum_small_grads").reshape(-1)
    small_grads = dict(zip([n for n, _, _ in SMALL_PARAMS], _unpack(summed, [shp for _, shp, _ in SMALL_PARAMS])))

    dmod_all = packs.reshape(8, rows * LANES)[:, :DEPTH * N_MOD * D_MODEL].reshape(8, DEPTH, N_MOD * D_MODEL)
    dmod_sh = lax.dynamic_slice_in_dim(dmod_all, chip * ADA_SHARD, ADA_SHARD, axis=2).transpose(1, 0, 2)
    g_ada, d_ada, m_ada, v_ada = _ada_update(c_all, dmod_sh, w_ada, m_w_ada, v_w_ada, "ada_update")

    for name, shp, axis in SMALL_PARAMS:
        if axis is not None:
            width = shp[axis] // 4
            small_grads[name] = lax.dynamic_slice_in_dim(small_grads[name], chip * width, width, axis=axis)
    names = [n for n, _, _ in SMALL_PARAMS]
    packed = [_pack_rows([src[n] for n in names])[None] for src in (params, small_grads, mom1, mom2)]
    small_out = _adamw(*packed, "adamw_small")
    shapes = [params[n].shape for n in names]
    small_delta, small_m, small_v = (dict(zip(names, _unpack(t.reshape(-1), shapes))) for t in small_out)

    core_id, chip_id = ic.astype(jnp.int32).reshape(1), chip.astype(jnp.int32).reshape(1)
    summed_big = _reduce_scatter_big([[grads[l][name] for l in range(DEPTH)] for name, _, _, _ in BIG_KINDS],
                                     core_id, chip_id)
    grad, delta, new_m, new_v = dict(small_grads), small_delta, small_m, small_v
    grad["w_ada"], delta["w_ada"], new_m["w_ada"], new_v["w_ada"] = g_ada, d_ada, m_ada, v_ada
    for (name, _, _, _), (mine, theirs) in zip(BIG_KINDS, summed_big):
        grad[name], delta[name], new_m[name], new_v[name] = _adamw_halves(
            params[name], mine, theirs, mom1[name], mom2[name], core_id, f"adamw_{name}")

    return (loss, dx[None], *[grad[n] for n in WEIGHT_NAMES], *[delta[n] for n in WEIGHT_NAMES],
            *[new_m[n] for n in WEIGHT_NAMES], *[new_v[n] for n in WEIGHT_NAMES])
```

```python
import numpy as np
import jax
import jax.numpy as jnp
from jax import lax
from jax.experimental import pallas as pl
from jax.experimental.pallas import tpu as pltpu

F32 = jnp.float32
BF16 = jnp.bfloat16

D_MODEL = 1024
DEPTH = 2
HEAD_DIM = 64
CONV_CH = 256
CONV_WIDTH = 31
ATT_WIDTH = 384
N_HEADS = 6
DILATIONS = (1, 4, 16)
ATT_HALF = 64
ATT_BLOCK = 128
ALIBI_MAX_EXP = 8.0
MASK_VALUE = -1e30
REC_WIDTH = 384
REC_CHUNK = 64
F_TINY = 1e-30
D_FF = 2816
N_MOD = 6
EPS = 1e-6
G_CONV = (0, 512)
G_QKV = (512, 1664)
G_REC = (1664, 3584)
IN_COLS = 3584

ADAM_LR = 0.001
ADAM_B1 = 0.9
ADAM_B2 = 0.999
ADAM_EPS = 1e-08
ADAM_WD = 0.01
ADAM_STEP = 10

VMEM_LIMIT_BYTES = 56 * 1024 * 1024
LANES = 128
MESH = pl.DeviceIdType.MESH
ANY = pl.BlockSpec(memory_space=pl.ANY)


def _params(n_axes):
    return pltpu.CompilerParams(dimension_semantics=("arbitrary",) * n_axes,
                                vmem_limit_bytes=VMEM_LIMIT_BYTES)


def _tile(n, target):
    best = None
    for t in range(LANES, min(n, target) + 1, LANES):
        if n % t == 0:
            best = t
    return best or n


def _sigmoid(x):
    return jax.nn.sigmoid(x)


def _silu_grad(x):
    s = _sigmoid(x)
    return s * (1.0 + x * (1.0 - s))


MM_ACC_ELEMS = 1536 * 1024


def _matmul(a, b, mode, out_dtype, name, tm=1024, tn=1792, tk=1792):
    if mode == "nn":
        (m, k), (k2, n) = a.shape, b.shape
    elif mode == "nt":
        (m, k), (n, k2) = a.shape, b.shape
    else:
        (k, m), (k2, n) = a.shape, b.shape
    assert k == k2, (a.shape, b.shape, mode)
    tn, tk = _tile(n, tn), _tile(k, tk)
    tm = _tile(m, min(tm, MM_ACC_ELEMS // tn))
    nk = k // tk
    a_spec = (pl.BlockSpec((tk, tm), lambda i, j, kk: (kk, i)) if mode == "tn"
              else pl.BlockSpec((tm, tk), lambda i, j, kk: (i, kk)))
    b_spec = (pl.BlockSpec((tn, tk), lambda i, j, kk: (j, kk)) if mode == "nt"
              else pl.BlockSpec((tk, tn), lambda i, j, kk: (kk, j)))
    dims = {"nn": (((1,), (0,)), ((), ())), "nt": (((1,), (1,)), ((), ())),
            "tn": (((0,), (0,)), ((), ()))}[mode]

    def body(a_ref, b_ref, o_ref, *scratch):
        part = lax.dot_general(a_ref[...].astype(BF16), b_ref[...].astype(BF16), dims, preferred_element_type=F32)
        if nk == 1:
            o_ref[...] = part.astype(out_dtype)
            return
        acc_ref, = scratch
        kk = pl.program_id(2)

        @pl.when(kk == 0)
        def _():
            acc_ref[...] = part

        @pl.when(kk > 0)
        def _():
            acc_ref[...] += part

        @pl.when(kk == nk - 1)
        def _():
            o_ref[...] = acc_ref[...].astype(out_dtype)

    return pl.pallas_call(
        body, name=name, grid=(m // tm, n // tn, nk),
        in_specs=[a_spec, b_spec],
        out_specs=pl.BlockSpec((tm, tn), lambda i, j, kk: (i, j)),
        out_shape=jax.ShapeDtypeStruct((m, n), out_dtype),
        scratch_shapes=[pltpu.VMEM((tm, tn), F32)] if nk > 1 else [],
        compiler_params=pltpu.CompilerParams(dimension_semantics=("parallel", "parallel", "arbitrary"),
                                             vmem_limit_bytes=VMEM_LIMIT_BYTES),
    )(a, b)


NORM_ROWS = 256


def _row_spec(width, rows=NORM_ROWS):
    return pl.BlockSpec((rows, width), lambda i: (i, 0))


def _vec_spec(width):
    return pl.BlockSpec((1, width), lambda i: (0, 0))


def _resid_norm_mod(x, r, g, nw, sc, sh, name):
    s, d = x.shape
    has_r = r is not None

    def body(*refs):
        if has_r:
            x_ref, r_ref, g_ref, nw_ref, sc_ref, sh_ref, xn_ref, h_ref = refs
            xn = x_ref[...] + g_ref[...] * r_ref[...]
            xn_ref[...] = xn
        else:
            x_ref, nw_ref, sc_ref, sh_ref, h_ref = refs
            xn = x_ref[...]
        rstd = lax.rsqrt(jnp.mean(xn * xn, axis=-1, keepdims=True) + EPS)
        y = xn * rstd * nw_ref[...]
        h_ref[...] = (y * (1.0 + sc_ref[...]) + sh_ref[...]).astype(BF16)

    if has_r:
        ins, in_specs = (x, r, g, nw, sc, sh), [_row_spec(d), _row_spec(d)] + [_vec_spec(d)] * 4
        out_shape = (jax.ShapeDtypeStruct((s, d), F32), jax.ShapeDtypeStruct((s, d), BF16))
        out_specs = (_row_spec(d), _row_spec(d))
    else:
        ins, in_specs = (x, nw, sc, sh), [_row_spec(d)] + [_vec_spec(d)] * 3
        out_shape = jax.ShapeDtypeStruct((s, d), BF16)
        out_specs = _row_spec(d)
    return pl.pallas_call(body, name=name, grid=(s // NORM_ROWS,), in_specs=in_specs, out_specs=out_specs,
                          out_shape=out_shape, compiler_params=_params(1))(*ins)


def _final_loss(x, r, g, fw, tgt, name):
    s, d = x.shape

    def body(x_ref, r_ref, g_ref, fw_ref, t_ref, loss_ref, dx_ref, dr_ref, dg_ref, dfw_ref):
        @pl.when(pl.program_id(0) == 0)
        def _():
            loss_ref[...] = jnp.zeros_like(loss_ref)
            dg_ref[...] = jnp.zeros_like(dg_ref)
            dfw_ref[...] = jnp.zeros_like(dfw_ref)

        rr = r_ref[...]
        gg = g_ref[...]
        xn = x_ref[...] + gg * rr
        rstd = lax.rsqrt(jnp.mean(xn * xn, axis=-1, keepdims=True) + EPS)
        xh = xn * rstd
        fwv = fw_ref[...]
        e = xh * fwv - t_ref[...]
        loss_ref[...] += 0.5 * jnp.sum(jnp.mean(e * e, axis=-1, keepdims=True), axis=0, keepdims=True)
        dy = e * (1.0 / d)
        dfw_ref[...] += jnp.sum(dy * xh, axis=0, keepdims=True)
        dxh = dy * fwv
        dx = rstd * (dxh - xh * jnp.mean(dxh * xh, axis=-1, keepdims=True))
        dx_ref[...] = dx
        dr_ref[...] = (gg * dx).astype(BF16)
        dg_ref[...] += jnp.sum(dx * rr, axis=0, keepdims=True)

    return pl.pallas_call(
        body, name=name, grid=(s // NORM_ROWS,),
        in_specs=[_row_spec(d), _row_spec(d), _vec_spec(d), _vec_spec(d), _row_spec(d)],
        out_specs=(_vec_spec(LANES), _row_spec(d), _row_spec(d), _vec_spec(d), _vec_spec(d)),
        out_shape=(jax.ShapeDtypeStruct((1, LANES), F32), jax.ShapeDtypeStruct((s, d), F32),
                   jax.ShapeDtypeStruct((s, d), BF16), jax.ShapeDtypeStruct((1, d), F32),
                   jax.ShapeDtypeStruct((1, d), F32)),
        compiler_params=_params(1))(x, r, g, fw, tgt)


def _norm_bwd(x, dhs, dxres, nw, sc, g, r, name):
    s, d = x.shape
    n_dh = len(dhs)
    has_g = g is not None

    def body(*refs):
        x_ref = refs[0]
        dh_refs = refs[1:1 + n_dh]
        dxres_ref, nw_ref, sc_ref = refs[1 + n_dh:4 + n_dh]
        pos = 4 + n_dh
        if has_g:
            g_ref, r_ref = refs[pos:pos + 2]
            pos += 2
            dx_ref, dr_ref, dsh_ref, dsc_ref, dnw_ref, dg_ref = refs[pos:]
            accs = (dsh_ref, dsc_ref, dnw_ref, dg_ref)
        else:
            dx_ref, dsh_ref, dsc_ref, dnw_ref = refs[pos:]
            accs = (dsh_ref, dsc_ref, dnw_ref)

        @pl.when(pl.program_id(0) == 0)
        def _():
            for acc in accs:
                acc[...] = jnp.zeros_like(acc)

        xv = x_ref[...]
        dh = dh_refs[0][...]
        for extra in dh_refs[1:]:
            dh = dh + extra[...]
        rstd = lax.rsqrt(jnp.mean(xv * xv, axis=-1, keepdims=True) + EPS)
        xh = xv * rstd
        nwv = nw_ref[...]
        dsh_ref[...] += jnp.sum(dh, axis=0, keepdims=True)
        dsc_ref[...] += jnp.sum(dh * (xh * nwv), axis=0, keepdims=True)
        dy = dh * (1.0 + sc_ref[...])
        dnw_ref[...] += jnp.sum(dy * xh, axis=0, keepdims=True)
        dxh = dy * nwv
        dx = dxres_ref[...] + rstd * (dxh - xh * jnp.mean(dxh * xh, axis=-1, keepdims=True))
        dx_ref[...] = dx
        if has_g:
            dr_ref[...] = (g_ref[...] * dx).astype(BF16)
            dg_ref[...] += jnp.sum(dx * r_ref[...], axis=0, keepdims=True)

    ins = [x, *dhs, dxres, nw, sc]
    in_specs = [_row_spec(d)] * (2 + n_dh) + [_vec_spec(d)] * 2
    out_shape = [jax.ShapeDtypeStruct((s, d), F32)]
    out_specs = [_row_spec(d)]
    if has_g:
        ins += [g, r]
        in_specs += [_vec_spec(d), _row_spec(d)]
        out_shape.append(jax.ShapeDtypeStruct((s, d), BF16))
        out_specs.append(_row_spec(d))
    n_vec = 4 if has_g else 3
    out_shape += [jax.ShapeDtypeStruct((1, d), F32)] * n_vec
    out_specs += [_vec_spec(d)] * n_vec
    return pl.pallas_call(body, name=name, grid=(s // NORM_ROWS,), in_specs=in_specs, out_specs=tuple(out_specs),
                          out_shape=tuple(out_shape), compiler_params=_params(1))(*ins)


FFN_ROWS = 256
FFN_COLS = 1408
HALO = 16
INV_SQRT2 = 0.7071067811865476
INV_SQRT_2PI = 0.3989422804014327


def _gelu(x):
    return 0.5 * x * (1.0 + lax.erf(x * INV_SQRT2))


def _gelu_grad(x):
    return 0.5 * (1.0 + lax.erf(x * INV_SQRT2)) + x * (INV_SQRT_2PI * jnp.exp(-0.5 * x * x))


def _halo_specs(rows, cols, halo, n_rows_total, col_of):
    per = rows // halo
    last = n_rows_total // halo - 1
    cur = pl.BlockSpec((rows, cols), lambda j, i: (i, col_of(j)))
    prev = pl.BlockSpec((halo, cols), lambda j, i: (jnp.maximum(i * per - 1, 0), col_of(j)))
    nxt = pl.BlockSpec((halo, cols), lambda j, i: (jnp.minimum((i + 1) * per, last), col_of(j)))
    return [prev, cur, nxt]


def _shift_rows(x, k):
    n = x.shape[0]
    return pltpu.roll(x, k % n, axis=0)


def _conv3(ext, w):
    return w[0:1, :] * _shift_rows(ext, 1) + w[1:2, :] * ext + w[2:3, :] * _shift_rows(ext, -1)


def _ext_block(prev_ref, cur_ref, next_ref, i, n_i):
    prev = jnp.where(i > 0, prev_ref[...].astype(F32), 0.0)
    nxt = jnp.where(i < n_i - 1, next_ref[...].astype(F32), 0.0)
    return jnp.concatenate([prev, cur_ref[...].astype(F32), nxt], axis=0)


def _ffn_act(u, cw, name):
    s = u.shape[0]
    nc, ns = D_FF // FFN_COLS, s // FFN_ROWS

    def body(gp, gc, gn, vp, vc, vn, wg_ref, wv_ref, o_ref):
        i = pl.program_id(1)
        cg = _conv3(_ext_block(gp, gc, gn, i, ns), wg_ref[...])[HALO:HALO + FFN_ROWS]
        cv = _conv3(_ext_block(vp, vc, vn, i, ns), wv_ref[...])[HALO:HALO + FFN_ROWS]
        o_ref[...] = (_gelu(cg) * cv).astype(BF16)

    in_specs = (_halo_specs(FFN_ROWS, FFN_COLS, HALO, s, lambda j: j)
                + _halo_specs(FFN_ROWS, FFN_COLS, HALO, s, lambda j: j + nc)
                + [pl.BlockSpec((3, FFN_COLS), lambda j, i: (0, j)),
                   pl.BlockSpec((3, FFN_COLS), lambda j, i: (0, j + nc))])
    return pl.pallas_call(
        body, name=name, grid=(nc, ns), in_specs=in_specs,
        out_specs=pl.BlockSpec((FFN_ROWS, FFN_COLS), lambda j, i: (i, j)),
        out_shape=jax.ShapeDtypeStruct((s, D_FF), BF16), compiler_params=_params(2),
    )(u, u, u, u, u, u, cw, cw)


def _ffn_act_bwd(u, dact, cw, name):
    s = u.shape[0]
    nc, ns = D_FF // FFN_COLS, s // FFN_ROWS

    def body(gp, gc, gn, vp, vc, vn, dp, dc, dn, wg_ref, wv_ref, dug_ref, duv_ref, dwg_ref, dwv_ref):
        i = pl.program_id(1)

        @pl.when(i == 0)
        def _():
            dwg_ref[...] = jnp.zeros_like(dwg_ref)
            dwv_ref[...] = jnp.zeros_like(dwv_ref)

        ug = _ext_block(gp, gc, gn, i, ns)
        uv = _ext_block(vp, vc, vn, i, ns)
        da = _ext_block(dp, dc, dn, i, ns)
        wg, wv = wg_ref[...], wv_ref[...]
        cg, cv = _conv3(ug, wg), _conv3(uv, wv)
        dcg = da * cv * _gelu_grad(cg)
        dcv = da * _gelu(cg)
        inner = slice(HALO, HALO + FFN_ROWS)
        for d_c, uu, w, du_ref, dw_ref in ((dcg, ug, wg, dug_ref, dwg_ref), (dcv, uv, wv, duv_ref, dwv_ref)):
            du = w[0:1, :] * _shift_rows(d_c, -1) + w[1:2, :] * d_c + w[2:3, :] * _shift_rows(d_c, 1)
            du_ref[...] = du[inner].astype(BF16)
            d_in = d_c[inner]
            for tap in range(3):
                dw_ref[tap:tap + 1, :] += jnp.sum(d_in * _shift_rows(uu, 1 - tap)[inner], axis=0, keepdims=True)

    in_specs = (_halo_specs(FFN_ROWS, FFN_COLS, HALO, s, lambda j: j)
                + _halo_specs(FFN_ROWS, FFN_COLS, HALO, s, lambda j: j + nc)
                + _halo_specs(FFN_ROWS, FFN_COLS, HALO, s, lambda j: j)
                + [pl.BlockSpec((3, FFN_COLS), lambda j, i: (0, j)),
                   pl.BlockSpec((3, FFN_COLS), lambda j, i: (0, j + nc))])
    blk = pl.BlockSpec((FFN_ROWS, FFN_COLS), lambda j, i: (i, j))
    acc = pl.BlockSpec((HALO, FFN_COLS), lambda j, i: (0, j))
    return pl.pallas_call(
        body, name=name, grid=(nc, ns), in_specs=in_specs, out_specs=(blk, blk, acc, acc),
        out_shape=(jax.ShapeDtypeStruct((s, D_FF), BF16), jax.ShapeDtypeStruct((s, D_FF), BF16),
                   jax.ShapeDtypeStruct((HALO, D_FF), F32), jax.ShapeDtypeStruct((HALO, D_FF), F32)),
        compiler_params=_params(2),
    )(u, u, u, u, u, u, dact, dact, dact, cw, cw)


CONV_ROWS = 512
CONV_HALO = 16
CONV_PAD = CONV_WIDTH // 2


def _conv_halo_specs(cols, s):
    per = CONV_ROWS // CONV_HALO
    last = s // CONV_HALO - 1
    return [pl.BlockSpec((CONV_HALO, cols), lambda i: (jnp.maximum(i * per - 1, 0), 0)),
            pl.BlockSpec((CONV_ROWS, cols), lambda i: (i, 0)),
            pl.BlockSpec((CONV_HALO, cols), lambda i: (jnp.minimum((i + 1) * per, last), 0))]


def _glu_ext(pp, pc, pn, i, n_i):
    ext = _ext_block(pp, pc, pn, i, n_i)
    return ext[:, :CONV_CH] * _sigmoid(ext[:, CONV_CH:])


def _conv_mixer(pa, cw, cb, lnw, lnb, name):
    s = pa.shape[0]
    ns = s // CONV_ROWS

    def body(pp, pc, pn, cw_ref, cb_ref, lnw_ref, lnb_ref, o_ref, c_ref):
        i = pl.program_id(0)
        a = _glu_ext(pp, pc, pn, i, ns)
        acc = jnp.zeros((CONV_ROWS, CONV_CH), F32)
        for tap in range(CONV_WIDTH):
            acc = acc + cw_ref[tap:tap + 1, :] * _shift_rows(a, -(tap + 1))[:CONV_ROWS]
        cv = acc + cb_ref[...]
        c_ref[...] = cv
        mu = jnp.mean(cv, axis=-1, keepdims=True)
        xc = cv - mu
        rstd = lax.rsqrt(jnp.mean(xc * xc, axis=-1, keepdims=True) + EPS)
        y = xc * rstd * lnw_ref[...] + lnb_ref[...]
        o_ref[...] = (y * _sigmoid(y)).astype(BF16)

    vec = pl.BlockSpec((1, CONV_CH), lambda i: (0, 0))
    blk = pl.BlockSpec((CONV_ROWS, CONV_CH), lambda i: (i, 0))
    return pl.pallas_call(
        body, name=name, grid=(ns,),
        in_specs=_conv_halo_specs(2 * CONV_CH, s) + [pl.BlockSpec((CONV_WIDTH, CONV_CH), lambda i: (0, 0)), vec, vec, vec],
        out_specs=(blk, blk),
        out_shape=(jax.ShapeDtypeStruct((s, CONV_CH), BF16), jax.ShapeDtypeStruct((s, CONV_CH), F32)),
        compiler_params=_params(1))(pa, pa, pa, cw, cb, lnw, lnb)


def _conv_mixer_bwd_ln(cv, dout, lnw, lnb, name):
    s = cv.shape[0]

    def body(c_ref, do_ref, lnw_ref, lnb_ref, dc_ref, dlnw_ref, dlnb_ref, dcb_ref):
        @pl.when(pl.program_id(0) == 0)
        def _():
            dlnw_ref[...] = jnp.zeros_like(dlnw_ref)
            dlnb_ref[...] = jnp.zeros_like(dlnb_ref)
            dcb_ref[...] = jnp.zeros_like(dcb_ref)

        c = c_ref[...]
        mu = jnp.mean(c, axis=-1, keepdims=True)
        xc = c - mu
        rstd = lax.rsqrt(jnp.mean(xc * xc, axis=-1, keepdims=True) + EPS)
        xh = xc * rstd
        w = lnw_ref[...]
        y = xh * w + lnb_ref[...]
        dy = do_ref[...] * _silu_grad(y)
        dlnw_ref[...] += jnp.sum(dy * xh, axis=0, keepdims=True)
        dlnb_ref[...] += jnp.sum(dy, axis=0, keepdims=True)
        dxh = dy * w
        dc = rstd * (dxh - jnp.mean(dxh, axis=-1, keepdims=True) - xh * jnp.mean(dxh * xh, axis=-1, keepdims=True))
        dc_ref[...] = dc
        dcb_ref[...] += jnp.sum(dc, axis=0, keepdims=True)

    vec = pl.BlockSpec((1, CONV_CH), lambda i: (0, 0))
    blk = pl.BlockSpec((CONV_ROWS, CONV_CH), lambda i: (i, 0))
    return pl.pallas_call(
        body, name=name, grid=(s // CONV_ROWS,), in_specs=[blk, blk, vec, vec], out_specs=(blk, vec, vec, vec),
        out_shape=(jax.ShapeDtypeStruct((s, CONV_CH), F32),) + (jax.ShapeDtypeStruct((1, CONV_CH), F32),) * 3,
        compiler_params=_params(1))(cv, dout, lnw, lnb)


def _conv_mixer_bwd_conv(pa, dc, cw, name):
    s = pa.shape[0]
    ns = s // CONV_ROWS

    def body(pp, pc, pn, dp, dcc, dn, cw_ref, dpa_ref, dcw_ref):
        i = pl.program_id(0)

        @pl.when(i == 0)
        def _():
            dcw_ref[...] = jnp.zeros_like(dcw_ref)

        a = _glu_ext(pp, pc, pn, i, ns)
        dce = _ext_block(dp, dcc, dn, i, ns)
        dcur = dcc[...]
        da = jnp.zeros((CONV_ROWS, CONV_CH), F32)
        for tap in range(CONV_WIDTH):
            da = da + cw_ref[tap:tap + 1, :] * _shift_rows(dce, -(CONV_WIDTH - tap))[:CONV_ROWS]
            dcw_ref[tap:tap + 1, :] += jnp.sum(dcur * _shift_rows(a, -(tap + 1))[:CONV_ROWS], axis=0, keepdims=True)
        cur = pc[...]
        val, sg = cur[:, :CONV_CH], _sigmoid(cur[:, CONV_CH:])
        dpa_ref[:, :CONV_CH] = (da * sg).astype(BF16)
        dpa_ref[:, CONV_CH:] = (da * val * sg * (1.0 - sg)).astype(BF16)

    return pl.pallas_call(
        body, name=name, grid=(ns,),
        in_specs=_conv_halo_specs(2 * CONV_CH, s) + _conv_halo_specs(CONV_CH, s)
        + [pl.BlockSpec((CONV_WIDTH, CONV_CH), lambda i: (0, 0))],
        out_specs=(pl.BlockSpec((CONV_ROWS, 2 * CONV_CH), lambda i: (i, 0)),
                   pl.BlockSpec((32, CONV_CH), lambda i: (0, 0))),
        out_shape=(jax.ShapeDtypeStruct((s, 2 * CONV_CH), BF16), jax.ShapeDtypeStruct((32, CONV_CH), F32)),
        compiler_params=_params(1))(pa, pa, pa, dc, dc, dc, cw)


SLOPES = tuple(float(2.0 ** (-ALIBI_MAX_EXP * (h + 1) / N_HEADS)) for h in range(N_HEADS))
ATT_SCALE = HEAD_DIM ** -0.5


PAIR = 2 * HEAD_DIM
N_PAIRS = N_HEADS // 2
ATT_WIN = ATT_BLOCK + 2 * ATT_HALF


ATT_GROUPS = {1: 4, 4: 1, 16: 1}


def _window_specs(dil, n_steps, col_of):
    per = 2 * ATT_GROUPS[dil]
    rows, halo = ATT_BLOCK * dil * ATT_GROUPS[dil], ATT_HALF * dil
    return [pl.BlockSpec((halo, PAIR), lambda i, p: (jnp.maximum(per * i - 1, 0), col_of(p))),
            pl.BlockSpec((rows, PAIR), lambda i, p: (i, col_of(p))),
            pl.BlockSpec((halo, PAIR), lambda i, p: (jnp.minimum(per * (i + 1), per * n_steps - 1), col_of(p)))]


def _residue(ref, r, n, dil, start=0):
    return ref[pl.ds(start * dil + r, n, stride=dil), :] if dil > 1 else ref[pl.ds(start + r, n), :]


def _store_residue(ref, r, dil, start, val):
    if dil > 1:
        ref[pl.ds(start * dil + r, val.shape[0], stride=dil), :] = val
    else:
        ref[pl.ds(start + r, val.shape[0]), :] = val


def _residue_window(refs, r, dil, g=0):
    prev, cur, nxt = refs
    groups = ATT_GROUPS[dil]
    lo = max(g * ATT_BLOCK - ATT_HALF, 0)
    hi = min((g + 1) * ATT_BLOCK + ATT_HALF, groups * ATT_BLOCK)
    parts = [_residue(prev, r, ATT_HALF, dil)] if g == 0 else []
    parts.append(_residue(cur, r, hi - lo, dil, lo))
    if g == groups - 1:
        parts.append(_residue(nxt, r, ATT_HALF, dil))
    return jnp.concatenate(parts, axis=0)


def _band_masks(i, length, dil, transposed):
    shape = (ATT_WIN, ATT_BLOCK) if transposed else (ATT_BLOCK, ATT_WIN)
    row = lax.broadcasted_iota(jnp.int32, shape, 0)
    col = lax.broadcasted_iota(jnp.int32, shape, 1)
    wide = row if transposed else col
    dist = jnp.abs((row - col - ATT_HALF) if transposed else (row + ATT_HALF - col))
    wpos = i * ATT_BLOCK - ATT_HALF + wide
    valid = (dist <= ATT_HALF) & (wpos >= 0) & (wpos < length)
    return valid, dist.astype(F32) * float(dil)


def _attn_branch(qkv, dil, name):
    s = qkv.shape[0]
    groups = ATT_GROUPS[dil]
    rows = ATT_BLOCK * dil * groups
    n_steps = s // rows
    length = s // dil
    nt = (((1,), (1,)), ((), ()))

    def body(q_ref, kp, kc, kn, vp, vc, vn, o_ref, l_ref):
        i, pair = pl.program_id(0), pl.program_id(1)
        for g, r in [(g, r) for g in range(groups) for r in range(dil)]:
            valid, distf = _band_masks(i * groups + g, length, dil, False)
            q = _residue(q_ref, r, ATT_BLOCK, dil, g * ATT_BLOCK).astype(BF16)
            k = _residue_window((kp, kc, kn), r, dil, g).astype(BF16)
            v = _residue_window((vp, vc, vn), r, dil, g).astype(BF16)
            outs, lses = [], []
            for hh in range(2):
                sl = slice(hh * HEAD_DIM, (hh + 1) * HEAD_DIM)
                slope = jnp.where(pair == 0, SLOPES[hh], jnp.where(pair == 1, SLOPES[2 + hh], SLOPES[4 + hh]))
                sc = lax.dot_general(q[:, sl], k[:, sl], nt, preferred_element_type=F32) * ATT_SCALE - slope * distf
                sc = jnp.where(valid, sc, MASK_VALUE)
                m = jnp.max(sc, axis=-1, keepdims=True)
                p = jnp.exp(sc - m)
                den = jnp.sum(p, axis=-1, keepdims=True)
                outs.append(jnp.dot(p.astype(BF16), v[:, sl], preferred_element_type=F32) / den)
                lses.append(jnp.broadcast_to(m + jnp.log(den), (ATT_BLOCK, HEAD_DIM)))
            _store_residue(o_ref, r, dil, g * ATT_BLOCK, jnp.concatenate(outs, axis=1))
            _store_residue(l_ref, r, dil, g * ATT_BLOCK, jnp.concatenate(lses, axis=1))

    out_blk = pl.BlockSpec((rows, PAIR), lambda i, p: (i, p))
    return pl.pallas_call(
        body, name=name, grid=(n_steps, N_PAIRS),
        in_specs=[pl.BlockSpec((rows, PAIR), lambda i, p: (i, p))]
        + _window_specs(dil, n_steps, lambda p: N_PAIRS + p) + _window_specs(dil, n_steps, lambda p: 2 * N_PAIRS + p),
        out_specs=(out_blk, out_blk),
        out_shape=(jax.ShapeDtypeStruct((s, ATT_WIDTH), F32),) * 2,
        compiler_params=_params(2))(qkv, qkv, qkv, qkv, qkv, qkv, qkv)


ATT_ROWS = 512


def _attn_combine(outs, lses, name):
    s = outs[0].shape[0]

    def body(o1, o2, o3, l1, l2, l3, att_ref, att32_ref, lse_ref):
        ls = [l1[...], l2[...], l3[...]]
        m = jnp.maximum(jnp.maximum(ls[0], ls[1]), ls[2])
        es = [jnp.exp(l - m) for l in ls]
        den = es[0] + es[1] + es[2]
        att = (es[0] * o1[...] + es[1] * o2[...] + es[2] * o3[...]) / den
        att_ref[...] = att.astype(BF16)
        att32_ref[...] = att
        lse_ref[...] = m + jnp.log(den)

    blk = pl.BlockSpec((ATT_ROWS, ATT_WIDTH), lambda i: (i, 0))
    return pl.pallas_call(
        body, name=name, grid=(s // ATT_ROWS,), in_specs=[blk] * 6, out_specs=(blk, blk, blk),
        out_shape=(jax.ShapeDtypeStruct((s, ATT_WIDTH), BF16), jax.ShapeDtypeStruct((s, ATT_WIDTH), F32),
                   jax.ShapeDtypeStruct((s, ATT_WIDTH), F32)),
        compiler_params=_params(1))(*outs, *lses)


def _attn_delta(datt, att, name):
    s = att.shape[0]

    def body(d_ref, a_ref, delta_ref):
        prod = d_ref[...] * a_ref[...]
        for h in range(N_HEADS):
            sl = slice(h * HEAD_DIM, (h + 1) * HEAD_DIM)
            delta_ref[:, sl] = jnp.broadcast_to(jnp.sum(prod[:, sl], axis=-1, keepdims=True), (ATT_ROWS, HEAD_DIM))

    blk = pl.BlockSpec((ATT_ROWS, ATT_WIDTH), lambda i: (i, 0))
    return pl.pallas_call(
        body, name=name, grid=(s // ATT_ROWS,), in_specs=[blk, blk], out_specs=blk,
        out_shape=jax.ShapeDtypeStruct((s, ATT_WIDTH), F32), compiler_params=_params(1))(datt, att)


def _attn_branch_bwd(qkv, do, lse, delta, prev, dil, name):
    s = qkv.shape[0]
    groups = ATT_GROUPS[dil]
    rows = ATT_BLOCK * dil * groups
    n_steps = s // rows
    length = s // dil
    has_prev = prev is not None
    tn = (((0,), (0,)), ((), ()))
    nt = (((1,), (1,)), ((), ()))

    def body(*refs):
        qs, ks, vs, dos, ls, des = (refs[3 * n:3 * n + 3] for n in range(6))
        rest = refs[18:]
        if has_prev:
            pq, pk, pv = rest[:3]
            rest = rest[3:]
        dq_ref, dk_ref, dv_ref = rest
        i, pair = pl.program_id(0), pl.program_id(1)
        for g, r in [(g, r) for g in range(groups) for r in range(dil)]:
            valid_q, distf_q = _band_masks(i * groups + g, length, dil, False)
            valid_k, distf_k = _band_masks(i * groups + g, length, dil, True)
            cur = lambda t: _residue(t[1], r, ATT_BLOCK, dil, g * ATT_BLOCK)
            q_cur, k_cur, v_cur, do_cur = (cur(t).astype(BF16) for t in (qs, ks, vs, dos))
            l_cur, de_cur = cur(ls), cur(des)
            q_win, k_win, v_win, do_win = (_residue_window(t, r, dil, g).astype(BF16) for t in (qs, ks, vs, dos))
            l_win, de_win = _residue_window(ls, r, dil, g), _residue_window(des, r, dil, g)
            dqs, dks, dvs = [], [], []
            for hh in range(2):
                sl = slice(hh * HEAD_DIM, (hh + 1) * HEAD_DIM)
                one = slice(hh * HEAD_DIM, hh * HEAD_DIM + 1)
                slope = jnp.where(pair == 0, SLOPES[hh], jnp.where(pair == 1, SLOPES[2 + hh], SLOPES[4 + hh]))
                sc = lax.dot_general(q_cur[:, sl], k_win[:, sl], nt, preferred_element_type=F32) * ATT_SCALE - slope * distf_q
                p = jnp.exp(jnp.where(valid_q, sc - l_cur[:, one], MASK_VALUE))
                dp = lax.dot_general(do_cur[:, sl], v_win[:, sl], nt, preferred_element_type=F32)
                ds = (p * (dp - de_cur[:, one]) * ATT_SCALE).astype(BF16)
                dqs.append(jnp.dot(ds, k_win[:, sl], preferred_element_type=F32))

                sc2 = lax.dot_general(q_win[:, sl], k_cur[:, sl], nt, preferred_element_type=F32) * ATT_SCALE - slope * distf_k
                p2 = jnp.exp(jnp.where(valid_k, sc2 - l_win[:, one], MASK_VALUE))
                dvs.append(lax.dot_general(p2.astype(BF16), do_win[:, sl], tn, preferred_element_type=F32))
                dp2 = lax.dot_general(do_win[:, sl], v_cur[:, sl], nt, preferred_element_type=F32)
                ds2 = (p2 * (dp2 - de_win[:, one]) * ATT_SCALE).astype(BF16)
                dks.append(lax.dot_general(ds2, q_win[:, sl], tn, preferred_element_type=F32))
            for parts, acc, out in ((dqs, pq if has_prev else None, dq_ref), (dks, pk if has_prev else None, dk_ref),
                                    (dvs, pv if has_prev else None, dv_ref)):
                val = jnp.concatenate(parts, axis=1)
                if has_prev:
                    val = val + _residue(acc, r, ATT_BLOCK, dil, g * ATT_BLOCK)
                _store_residue(out, r, dil, g * ATT_BLOCK, val)

    blk = pl.BlockSpec((rows, PAIR), lambda i, p: (i, p))
    in_specs = (_window_specs(dil, n_steps, lambda p: p) + _window_specs(dil, n_steps, lambda p: N_PAIRS + p)
                + _window_specs(dil, n_steps, lambda p: 2 * N_PAIRS + p) + _window_specs(dil, n_steps, lambda p: p) * 3)
    ins = [qkv] * 9 + [do] * 3 + [lse] * 3 + [delta] * 3
    if has_prev:
        in_specs += [blk] * 3
        ins += list(prev)
    return pl.pallas_call(
        body, name=name, grid=(n_steps, N_PAIRS), in_specs=in_specs, out_specs=(blk, blk, blk),
        out_shape=(jax.ShapeDtypeStruct((s, ATT_WIDTH), F32),) * 3,
        compiler_params=_params(2))(*ins)


TB = 2 * REC_CHUNK
REC_SUB = 8
REC_ROWS = 5 * REC_WIDTH


def _chunk_scan(x, pos, rev):
    del pos
    row = lax.broadcasted_iota(jnp.int32, (TB, TB), 0)
    col = lax.broadcasted_iota(jnp.int32, (TB, TB), 1)
    same = (row < REC_CHUNK) == (col < REC_CHUNK)
    tri = (same & ((row >= col) if rev else (row <= col))).astype(BF16)
    hi = x.astype(BF16)
    rest = x - hi.astype(F32)
    mid = rest.astype(BF16)
    low = (rest - mid.astype(F32)).astype(BF16)
    return (jnp.dot(hi, tri, preferred_element_type=F32) + jnp.dot(mid, tri, preferred_element_type=F32)
            + jnp.dot(low, tri, preferred_element_type=F32))


def _hg_prep(qraw, z, lb, rev):
    lane = lax.broadcasted_iota(jnp.int32, (REC_WIDTH, TB), 1)
    pos = lane & (REC_CHUNK - 1)
    in_a = lane < REC_CHUNK
    sig, sigm = _sigmoid(z), _sigmoid(-z)
    f = lb + (1.0 - lb) * sig
    kk = (1.0 - lb) * sigm
    b = _chunk_scan(jnp.log(jnp.maximum(f, F_TINY)), pos, rev)
    end_a = b[:, 0:1] if rev else b[:, REC_CHUNK - 1:REC_CHUNK]
    end_b = b[:, REC_CHUNK:REC_CHUNK + 1] if rev else b[:, TB - 1:TB]
    bend = jnp.where(in_a, end_a, end_b)
    q = qraw * _sigmoid(qraw)
    sub = ((REC_CHUNK - 1 - pos) if rev else pos) // REC_SUB
    eq, ek = [], []
    for i in range(1, REC_CHUNK // REC_SUB):
        la = (REC_CHUNK - REC_SUB * i) if rev else (REC_SUB * i - 1)
        ri = jnp.where(in_a, b[:, la:la + 1], b[:, la + REC_CHUNK:la + REC_CHUNK + 1])
        eq.append(jnp.where(sub == i, jnp.exp(jnp.minimum(b - ri, 0.0)), 0.0))
        ek.append(jnp.where(sub < i, jnp.exp(jnp.minimum(ri - b, 0.0)), 0.0))
    return dict(pos=pos, in_a=in_a, sig=sig, sigm=sigm, f=f, kk=kk, b=b, end_a=end_a, end_b=end_b,
                q=q, qh=q * jnp.exp(b), kh=kk * jnp.exp(bend - b), ekb=jnp.exp(bend - b), eq=eq, ek=ek,
                pos_sub=lane & (REC_SUB - 1))


def _pair_masks(rev):
    row = lax.broadcasted_iota(jnp.int32, (TB, TB), 0)
    col = lax.broadcasted_iota(jnp.int32, (TB, TB), 1)
    same = (row < REC_CHUNK) == (col < REC_CHUNK)
    scan = lambda p: ((REC_CHUNK - 1 - (p & (REC_CHUNK - 1))) if rev else (p & (REC_CHUNK - 1))) // REC_SUB
    causal = same & ((row >= col) if rev else (row <= col))
    earlier_sub = same & (scan(row) < scan(col))
    return same, causal, earlier_sub, col - row


def _head_rows(x, h):
    return x[h * HEAD_DIM:(h + 1) * HEAD_DIM, :]


def _stack_subs(parts, h):
    return jnp.concatenate([_head_rows(p, h) for p in parts], axis=0)


def _block_diag_mask():
    r = lax.broadcasted_iota(jnp.int32, (REC_WIDTH, REC_WIDTH), 0) // HEAD_DIM
    c = lax.broadcasted_iota(jnp.int32, (REC_WIDTH, REC_WIDTH), 1) // HEAD_DIM
    return (r == c).astype(F32)


def _heads(x):
    return x.reshape(N_HEADS, HEAD_DIM, TB)


def _hg_shift(delta, rev):
    return jnp.where(delta == 0, 0, TB - delta) if rev else delta


def _hg_unshift(delta, rev):
    return delta if rev else jnp.where(delta == 0, 0, TB - delta)


def _hgrn_scan(projt, lb, rev, name):
    s = projt.shape[1]
    nblk = s // TB
    zrow = 2 if rev else 1
    tmap = (lambda i: nblk - 1 - i) if rev else (lambda i: i)
    tn = (((0,), (0,)), ((), ()))
    nt = (((1,), (1,)), ((), ()))

    def body(q_ref, z_ref, v_ref, lb_ref, o_ref, hs_ref, at_ref, h_ref, acc_ref):
        @pl.when(pl.program_id(0) == 0)
        def _():
            h_ref[...] = jnp.zeros_like(h_ref)

        v = v_ref[...]
        vb = v.astype(BF16)
        pr = _hg_prep(q_ref[...], z_ref[...], lb_ref[...], rev)
        q, kk, b, pos_sub = pr["q"], pr["kk"], pr["b"], pr["pos_sub"]
        same, _, _, offset = _pair_masks(rev)
        qt = [q * e for e in pr["eq"]]
        kt = [kk * e for e in pr["ek"]]
        for h in range(N_HEADS):
            sc = lax.dot_general(_stack_subs(kt, h).astype(BF16), _stack_subs(qt, h).astype(BF16), tn,
                                 preferred_element_type=F32)
            acc_ref[h] = jnp.where(same, sc, 0.0)

        def pair_step(delta, carry):
            sh = _hg_shift(delta, rev)
            kd, bd = pltpu.roll(kk, sh, axis=1), pltpu.roll(b, sh, axis=1)
            valid = (pos_sub <= REC_SUB - 1 - delta) if rev else (pos_sub >= delta)
            w = jnp.where(valid, q * kd * jnp.exp(jnp.where(valid, b - bd, 0.0)), 0.0)
            a = jnp.sum(_heads(w), axis=1)
            hit = offset == (-delta if rev else delta)
            for h in range(N_HEADS):
                acc_ref[h] += jnp.where(hit, a[h:h + 1, :], 0.0)
            return carry

        lax.fori_loop(0, REC_SUB, pair_step, 0)
        outs = []
        for h in range(N_HEADS):
            a_bf = acc_ref[h].astype(BF16)
            at_ref[h] = a_bf
            outs.append(jnp.dot(_head_rows(vb, h), a_bf, preferred_element_type=F32))
        o = jnp.concatenate(outs, axis=0)
        bd_mask = _block_diag_mask()
        order = ((1, ~pr["in_a"], pr["end_b"]), (0, pr["in_a"], pr["end_a"]))
        if not rev:
            order = order[::-1]
        for slot, msk, bend in order:
            h0 = h_ref[...]
            hs_ref[slot] = h0
            o = o + lax.dot_general(h0.astype(BF16), jnp.where(msk, pr["qh"], 0.0).astype(BF16), tn,
                                    preferred_element_type=F32)
            upd = lax.dot_general(jnp.where(msk, pr["kh"], 0.0).astype(BF16), vb, nt, preferred_element_type=F32)
            h_ref[...] = jnp.exp(bend) * h0 + upd * bd_mask
        o_ref[...] = o

    row_blk = lambda r: pl.BlockSpec((REC_WIDTH, TB), lambda i: (r, tmap(i)))
    return pl.pallas_call(
        body, name=name, grid=(nblk,),
        in_specs=[row_blk(0), row_blk(zrow), row_blk(3), pl.BlockSpec((REC_WIDTH, 1), lambda i: (0, 0))],
        out_specs=(pl.BlockSpec((REC_WIDTH, TB), lambda i: (0, tmap(i))),
                   pl.BlockSpec((2, REC_WIDTH, REC_WIDTH), lambda i: (tmap(i), 0, 0)),
                   pl.BlockSpec((None, N_HEADS, TB, TB), lambda i: (tmap(i), 0, 0, 0))),
        out_shape=(jax.ShapeDtypeStruct((REC_WIDTH, s), F32),
                   jax.ShapeDtypeStruct((s // REC_CHUNK, REC_WIDTH, REC_WIDTH), F32),
                   jax.ShapeDtypeStruct((nblk, N_HEADS, TB, TB), BF16)),
        scratch_shapes=[pltpu.VMEM((REC_WIDTH, REC_WIDTH), F32), pltpu.VMEM((N_HEADS, TB, TB), F32)],
        compiler_params=_params(1))(projt, projt, projt, lb)


def _hgrn_scan_bwd(projt, lb, dot, hs, at, prev, rev, name):
    s = projt.shape[1]
    nblk = s // TB
    zrow = 2 if rev else 1
    tmap = (lambda i: i) if rev else (lambda i: nblk - 1 - i)
    has_prev = prev is not None
    tn = (((0,), (0,)), ((), ()))
    nt = (((1,), (1,)), ((), ()))

    def body(*refs):
        q_ref, z_ref, v_ref, lb_ref, do_ref, hs_ref, at_ref = refs[:7]
        rest = refs[7:]
        if has_prev:
            pq_ref, pv_ref = rest[:2]
            rest = rest[2:]
        dq_ref, dz_ref, dv_ref, dlb_ref, dh_ref, dat_ref = rest

        @pl.when(pl.program_id(0) == 0)
        def _():
            dh_ref[...] = jnp.zeros_like(dh_ref)
            dlb_ref[...] = jnp.zeros_like(dlb_ref)

        qraw, v, do, lbv = q_ref[...], v_ref[...], do_ref[...], lb_ref[...]
        dob, vb = do.astype(BF16), v.astype(BF16)
        pr = _hg_prep(qraw, z_ref[...], lbv, rev)
        q, kk, b, pos_sub, in_a = pr["q"], pr["kk"], pr["b"], pr["pos_sub"], pr["in_a"]
        _, causal, earlier_sub, offset = _pair_masks(rev)
        qt = [q * e for e in pr["eq"]]
        kt = [kk * e for e in pr["ek"]]
        n_sub = len(qt)
        dq_h, dk_h, dv_h = [], [], []
        for h in range(N_HEADS):
            d_at = jnp.where(causal, lax.dot_general(_head_rows(vb, h), _head_rows(dob, h), tn,
                                                     preferred_element_type=F32), 0.0)
            dat_ref[h] = d_at
            dv_h.append(lax.dot_general(_head_rows(dob, h), at_ref[h], nt, preferred_element_type=F32))
            d_off = jnp.where(earlier_sub, d_at, 0.0).astype(BF16)
            dqt = jnp.dot(_stack_subs(kt, h).astype(BF16), d_off, preferred_element_type=F32)
            dkt = lax.dot_general(_stack_subs(qt, h).astype(BF16), d_off, nt, preferred_element_type=F32)
            dq_h.append(sum(_head_rows(pr["eq"][i], h) * dqt[i * HEAD_DIM:(i + 1) * HEAD_DIM] for i in range(n_sub)))
            dk_h.append(sum(_head_rows(pr["ek"][i], h) * dkt[i * HEAD_DIM:(i + 1) * HEAD_DIM] for i in range(n_sub)))
        dq0, dk0, dv = (jnp.concatenate(t, axis=0) for t in (dq_h, dk_h, dv_h))

        def pair_step(delta, carry):
            dq, dk = carry
            sh, back = _hg_shift(delta, rev), _hg_unshift(delta, rev)
            kd, bd = pltpu.roll(kk, sh, axis=1), pltpu.roll(b, sh, axis=1)
            valid = (pos_sub <= REC_SUB - 1 - delta) if rev else (pos_sub >= delta)
            e = jnp.where(valid, jnp.exp(jnp.where(valid, b - bd, 0.0)), 0.0)
            hit = offset == (-delta if rev else delta)
            da = jnp.concatenate(
                [jnp.broadcast_to(jnp.sum(jnp.where(hit, dat_ref[h], 0.0), axis=0, keepdims=True), (HEAD_DIM, TB))
                 for h in range(N_HEADS)], axis=0)
            dq = dq + da * kd * e
            dk = dk + pltpu.roll(da * q * e, back, axis=1)
            return dq, dk

        dq, dk = lax.fori_loop(0, REC_SUB, pair_step, (dq0, dk0))

        zero = jnp.zeros((REC_WIDTH, TB), F32)
        bd_mask = _block_diag_mask()
        eb = jnp.exp(b)
        const = zero
        order = ((0, in_a, pr["end_a"]), (1, ~in_a, pr["end_b"]))
        if not rev:
            order = order[::-1]
        for slot, msk, bend in order:
            h0 = hs_ref[slot]
            dh1 = dh_ref[...]
            dh1b = dh1.astype(BF16)
            dq = dq + eb * jnp.dot(h0.astype(BF16), jnp.where(msk, do, 0.0).astype(BF16), preferred_element_type=F32)
            dv = dv + lax.dot_general(dh1b, jnp.where(msk, pr["kh"], 0.0).astype(BF16), tn, preferred_element_type=F32)
            dk_int = pr["ekb"] * jnp.dot(dh1b, jnp.where(msk, v, 0.0).astype(BF16), preferred_element_type=F32)
            dk = dk + dk_int
            ebend = jnp.exp(bend)
            c = (jnp.sum(kk * dk_int, axis=1, keepdims=True)
                 + ebend * jnp.sum(h0 * dh1, axis=1, keepdims=True))
            const = const + jnp.where(msk, c, 0.0)
            upd = lax.dot_general(jnp.where(msk, pr["qh"], 0.0).astype(BF16), dob, nt, preferred_element_type=F32)
            dh_ref[...] = ebend * dh1 + upd * bd_mask

        dg = _chunk_scan(q * dq - kk * dk, pr["pos"], not rev) + const
        sig, sigm, f = pr["sig"], pr["sigm"], pr["f"]
        live = f > F_TINY
        inv_f = 1.0 / jnp.maximum(f, F_TINY)
        one_lb = 1.0 - lbv
        dz = sig * sigm * one_lb * (jnp.where(live, dg * inv_f, 0.0) - dk)
        dlb_ref[...] += jnp.sum(sigm * (jnp.where(live, dg * inv_f, 0.0) - dk), axis=1, keepdims=True)
        dqr = dq * _silu_grad(qraw)
        if has_prev:
            dqr = dqr + pq_ref[...]
            dv = dv + pv_ref[...]
        dq_ref[...] = dqr
        dz_ref[...] = dz
        dv_ref[...] = dv

    row_blk = lambda r: pl.BlockSpec((REC_WIDTH, TB), lambda i: (r, tmap(i)))
    blk = pl.BlockSpec((REC_WIDTH, TB), lambda i: (0, tmap(i)))
    col = pl.BlockSpec((REC_WIDTH, 1), lambda i: (0, 0))
    in_specs = [row_blk(0), row_blk(zrow), row_blk(3), col, blk,
                pl.BlockSpec((2, REC_WIDTH, REC_WIDTH), lambda i: (tmap(i), 0, 0)),
                pl.BlockSpec((None, N_HEADS, TB, TB), lambda i: (tmap(i), 0, 0, 0))]
    ins = [projt, projt, projt, lb, dot, hs, at]
    if has_prev:
        in_specs += [blk, blk]
        ins += list(prev)
    t_shape = jax.ShapeDtypeStruct((REC_WIDTH, s), F32)
    return pl.pallas_call(
        body, name=name, grid=(nblk,), in_specs=in_specs, out_specs=(blk, blk, blk, col),
        out_shape=(t_shape, t_shape, t_shape, jax.ShapeDtypeStruct((REC_WIDTH, 1), F32)),
        scratch_shapes=[pltpu.VMEM((REC_WIDTH, REC_WIDTH), F32), pltpu.VMEM((N_HEADS, TB, TB), F32)],
        compiler_params=_params(1))(*ins)


REC_OUT_COLS = 512


def _head_rms(o):
    o3 = o.reshape(N_HEADS, HEAD_DIM, o.shape[1])
    rstd = lax.rsqrt(jnp.mean(o3 * o3, axis=1, keepdims=True) + EPS)
    return o3 * rstd, rstd


def _hgrn_out(of, ob, projt, wn, name):
    s = of.shape[1]

    def body(of_ref, ob_ref, g_ref, wn_ref, o_ref):
        on, _ = _head_rms(of_ref[...] + ob_ref[...])
        g = g_ref[...]
        y = on.reshape(REC_WIDTH, REC_OUT_COLS) * wn_ref[...] * (g * _sigmoid(g))
        o_ref[...] = y.T.astype(BF16)

    blk = pl.BlockSpec((REC_WIDTH, REC_OUT_COLS), lambda i: (0, i))
    return pl.pallas_call(
        body, name=name, grid=(s // REC_OUT_COLS,),
        in_specs=[blk, blk, pl.BlockSpec((REC_WIDTH, REC_OUT_COLS), lambda i: (4, i)),
                  pl.BlockSpec((REC_WIDTH, 1), lambda i: (0, 0))],
        out_specs=pl.BlockSpec((REC_OUT_COLS, REC_WIDTH), lambda i: (i, 0)),
        out_shape=jax.ShapeDtypeStruct((s, REC_WIDTH), BF16), compiler_params=_params(1))(of, ob, projt, wn)


def _hgrn_out_bwd(drec, of, ob, projt, wn, name):
    s = of.shape[1]

    def body(d_ref, of_ref, ob_ref, g_ref, wn_ref, do_ref, dg_ref, dwn_ref):
        @pl.when(pl.program_id(0) == 0)
        def _():
            dwn_ref[...] = jnp.zeros_like(dwn_ref)

        dy = d_ref[...].T
        on3, rstd = _head_rms(of_ref[...] + ob_ref[...])
        on = on3.reshape(REC_WIDTH, REC_OUT_COLS)
        g, wnv = g_ref[...], wn_ref[...]
        dg_ref[...] = dy * on * wnv * _silu_grad(g)
        d_onw = dy * (g * _sigmoid(g))
        dwn_ref[...] += jnp.sum(d_onw * on, axis=1, keepdims=True)
        d_on3 = (d_onw * wnv).reshape(N_HEADS, HEAD_DIM, REC_OUT_COLS)
        do3 = rstd * (d_on3 - on3 * jnp.mean(d_on3 * on3, axis=1, keepdims=True))
        do_ref[...] = do3.reshape(REC_WIDTH, REC_OUT_COLS)

    blk = pl.BlockSpec((REC_WIDTH, REC_OUT_COLS), lambda i: (0, i))
    col = pl.BlockSpec((REC_WIDTH, 1), lambda i: (0, 0))
    t_shape = jax.ShapeDtypeStruct((REC_WIDTH, s), F32)
    return pl.pallas_call(
        body, name=name, grid=(s // REC_OUT_COLS,),
        in_specs=[pl.BlockSpec((REC_OUT_COLS, REC_WIDTH), lambda i: (i, 0)), blk, blk,
                  pl.BlockSpec((REC_WIDTH, REC_OUT_COLS), lambda i: (4, i)), col],
        out_specs=(blk, blk, col),
        out_shape=(t_shape, t_shape, jax.ShapeDtypeStruct((REC_WIDTH, 1), F32)),
        compiler_params=_params(1))(drec, of, ob, projt, wn)


def _lower_bounds(gamma, name):
    def body(g_ref, lb_ref, p_ref):
        g0, g1 = g_ref[0:1, :], g_ref[1:2, :]
        m = jnp.maximum(g0, g1)
        e0, e1 = jnp.exp(g0 - m), jnp.exp(g1 - m)
        p0, p1 = e0 / (e0 + e1), e1 / (e0 + e1)
        lb_ref[...] = (p0 + p1) - p0
        p_ref[0:1, :] = p0
        p_ref[1:2, :] = p1

    n = gamma.shape[1]
    return pl.pallas_call(body, name=name,
                          out_shape=(jax.ShapeDtypeStruct((1, n), F32), jax.ShapeDtypeStruct((2, n), F32)))(gamma)


def _lower_bounds_bwd(dlb1, p, name):
    def body(d_ref, p_ref, o_ref):
        p0, p1, d = p_ref[0:1, :], p_ref[1:2, :], d_ref[...]
        inner = p1 * d
        o_ref[0:1, :] = p0 * (0.0 - inner)
        o_ref[1:2, :] = p1 * (d - inner)

    return pl.pallas_call(body, name=name, out_shape=jax.ShapeDtypeStruct(p.shape, F32))(dlb1, p)


def _split_w_in(w_in):
    return dict(conv=w_in[:, G_CONV[0]:G_CONV[1]], qkv=w_in[:, G_QKV[0]:G_QKV[1]],
                rec_t=w_in[:, G_REC[0]:].T, nat=w_in[:, :G_REC[0]])


def _split_w_rest(w_out, w_up, w_down):
    return dict(out=w_out, out_a=w_out[:CONV_CH], out_b=w_out[CONV_CH:CONV_CH + ATT_WIDTH],
                out_c=w_out[CONV_CH + ATT_WIDTH:], up=w_up, down=w_down)


def _col(v):
    return v.reshape(-1, 1)


def _sequence_step(x, tgt, mods, lbs, small, w_in0, later_weights, final_w):
    saved = []
    xin = x
    big = [_split_w_in(w_in0), None]
    h1 = _resid_norm_mod(x, None, None, small[0]["norm1_w"], mods[0][1:2], mods[0][0:1], "norm1_first")
    for l in range(DEPTH):
        sm, w, md = small[l], big[l], mods[l]
        pa = _matmul(h1, w["conv"], "nn", F32, f"proj_conv")
        qkv = _matmul(h1, w["qkv"], "nn", F32, f"proj_qkv")
        projt = _matmul(w["rec_t"], h1, "nt", F32, f"proj_rec")
        a_out, cv = _conv_mixer(pa, sm["conv_a_w"], sm["conv_a_b"], sm["ln_a_w"], sm["ln_a_b"], f"conv_mixer")
        outs, lses = zip(*[_attn_branch(qkv, d, f"attn_d{d}") for d in DILATIONS])
        att, att32, lse = _attn_combine(outs, lses, f"attn_combine")
        lb_f, lb_b = _col(lbs[l][0]), _col(lbs[l][1])
        of, hsf, atf = _hgrn_scan(projt, lb_f, False, "hgrn_fwd")
        ob, hsb, atb = _hgrn_scan(projt, lb_b, True, "hgrn_rev")
        wn = _col(sm["rec_norm_w"])
        rec = _hgrn_out(of, ob, projt, wn, f"hgrn_out")
        mixed = jnp.concatenate([a_out, att, rec], axis=1)
        if l == 0:
            w_in1, w_out_all, w_up_all, w_down_all = later_weights(rec)
            big[0].update(_split_w_rest(w_out_all[0], w_up_all[0], w_down_all[0]))
            big[1] = dict(_split_w_in(w_in1), **_split_w_rest(w_out_all[1], w_up_all[1], w_down_all[1]))
        r1 = _matmul(mixed, w["out"], "nn", F32, f"out_proj")
        xmid, h2 = _resid_norm_mod(xin, r1, md[2:3], sm["norm2_w"], md[4:5], md[3:4], f"norm2")
        u = _matmul(h2, w["up"], "nn", BF16, f"ffn_up")
        act = _ffn_act(u, sm["conv_f_w"], f"ffn_act")
        r2 = _matmul(act, w["down"], "nn", F32, f"ffn_down")
        saved.append(dict(xin=xin, h1=h1, pa=pa, qkv=qkv, projt=projt, cv=cv, att32=att32, lse=lse, of=of, ob=ob,
                          hsf=hsf, hsb=hsb, atf=atf, atb=atb, lb_f=lb_f, lb_b=lb_b, wn=wn, mixed=mixed, r1=r1, xmid=xmid, h2=h2,
                          u=u, act=act, r2=r2))
        if l + 1 < DEPTH:
            nxt = small[l + 1]
            xin, h1 = _resid_norm_mod(xmid, r2, md[5:6], nxt["norm1_w"], mods[l + 1][1:2], mods[l + 1][0:1],
                                      "norm1")
    top = saved[-1]
    loss, dx, dr2, dg2, dfw = _final_loss(top["xmid"], top["r2"], mods[-1][5:6], final_w, tgt, "final_loss")

    grads = [None] * DEPTH
    for l in reversed(range(DEPTH)):
        sm, w, md, sv = small[l], big[l], mods[l], saved[l]
        dact = _matmul(dr2, w["down"], "nt", BF16, f"d_act")
        g_down = _matmul(sv["act"], dr2, "tn", F32, f"dw_down")
        dug, duv, dwg, dwv = _ffn_act_bwd(sv["u"], dact, sm["conv_f_w"], f"ffn_act_bwd")
        du = jnp.concatenate([dug, duv], axis=1)
        dh2 = _matmul(du, w["up"], "nt", F32, f"d_h2")
        g_up = _matmul(sv["h2"], du, "tn", F32, f"dw_up")
        dxmid, dr1, dsh2, dsc2, dnw2, dg1 = _norm_bwd(sv["xmid"], [dh2], dx, sm["norm2_w"], md[4:5], md[2:3], sv["r1"],
                                                     f"norm2_bwd")
        dmix_a = _matmul(dr1, w["out_a"], "nt", F32, f"d_mix_a")
        dmix_b = _matmul(dr1, w["out_b"], "nt", F32, f"d_mix_b")
        dmix_c = _matmul(dr1, w["out_c"], "nt", F32, f"d_mix_c")
        g_out = _matmul(sv["mixed"], dr1, "tn", F32, f"dw_out")
        dc, dlnw, dlnb, dcb = _conv_mixer_bwd_ln(sv["cv"], dmix_a, sm["ln_a_w"], sm["ln_a_b"], f"conv_mixer_bwd_ln")
        dpa, dcw = _conv_mixer_bwd_conv(sv["pa"], dc, sm["conv_a_w"], f"conv_mixer_bwd_conv")
        delta = _attn_delta(dmix_b, sv["att32"], "attn_delta")
        dqkv = None
        for d in DILATIONS:
            dqkv = _attn_branch_bwd(sv["qkv"], dmix_b, sv["lse"], delta, dqkv, d, f"attn_bwd_d{d}")
        dot, dgt, dwn = _hgrn_out_bwd(dmix_c, sv["of"], sv["ob"], sv["projt"], sv["wn"], f"hgrn_out_bwd")
        dqf, dzf, dvf, dlbf = _hgrn_scan_bwd(sv["projt"], sv["lb_f"], dot, sv["hsf"], sv["atf"], None, False,
                                             "hgrn_fwd_bwd")
        dqt, dzb, dvt, dlbb = _hgrn_scan_bwd(sv["projt"], sv["lb_b"], dot, sv["hsb"], sv["atb"], (dqf, dvf), True,
                                             "hgrn_rev_bwd")
        dprojt = jnp.concatenate([dqt, dzf, dzb, dvt, dgt], axis=0).astype(BF16)
        dnat = jnp.concatenate([dpa] + [t.astype(BF16) for t in dqkv], axis=1)
        dh1_a = _matmul(dnat, w["nat"], "nt", F32, f"d_h1_nat")
        dh1_b = _matmul(dprojt, w["rec_t"], "tn", F32, f"d_h1_rec")
        g_in_nat = _matmul(sv["h1"], dnat, "tn", F32, f"dw_in_nat")
        g_in_rec_t = _matmul(dprojt, sv["h1"], "nn", F32, f"dw_in_rec")
        g_in = jnp.concatenate([g_in_nat, g_in_rec_t.T], axis=1)
        if l > 0:
            below = saved[l - 1]
            dx, dr2, dsh1, dsc1, dnw1, dg2_below = _norm_bwd(sv["xin"], [dh1_a, dh1_b], dxmid, sm["norm1_w"], md[1:2],
                                                            mods[l - 1][5:6], below["r2"], f"norm1_bwd")
        else:
            dx, dsh1, dsc1, dnw1 = _norm_bwd(sv["xin"], [dh1_a, dh1_b], dxmid, sm["norm1_w"], md[1:2], None, None,
                                             f"norm1_bwd")
        grads[l] = dict(w_in=g_in, w_out=g_out, w_up=g_up, w_down=g_down,
                        mod=[dsh1, dsc1, dg1, dsh2, dsc2, dg2], norm1_w=dnw1, conv_a_w=dcw[:CONV_WIDTH], conv_a_b=dcb,
                        ln_a_w=dlnw, ln_a_b=dlnb, lb=jnp.concatenate([dlbf.reshape(1, -1), dlbb.reshape(1, -1)], axis=0),
                        rec_norm_w=dwn.reshape(1, -1), norm2_w=dnw2,
                        conv_f_w=jnp.concatenate([dwg[:3], dwv[:3]], axis=1))
        if l > 0:
            dg2 = dg2_below
    return loss[0, 0], dx, grads, dfw


def _adamw_math(w, g, m, v):
    m = ADAM_B1 * m + (1.0 - ADAM_B1) * g
    v = ADAM_B2 * v + (1.0 - ADAM_B2) * (g * g)
    m_hat = m / (1.0 - ADAM_B1 ** ADAM_STEP)
    v_hat = v / (1.0 - ADAM_B2 ** ADAM_STEP)
    delta = -ADAM_LR * (m_hat / (jnp.sqrt(v_hat) + ADAM_EPS) + ADAM_WD * w)
    return delta, m, v


def _row_tile(rows, cols, max_elems=384 * 1024):
    best = None
    for t in range(8, rows + 1, 8):
        if rows % t == 0 and t * cols <= max_elems:
            best = t
    return best or rows


def _adamw(w, g, m, v, name):
    nl, r, c = w.shape
    tr = _row_tile(r, c)

    def body(w_ref, g_ref, m_ref, v_ref, d_ref, m2_ref, v2_ref):
        d_ref[...], m2_ref[...], v2_ref[...] = _adamw_math(w_ref[...], g_ref[...], m_ref[...], v_ref[...])

    blk = pl.BlockSpec((None, tr, c), lambda l, i: (l, i, 0))
    shape = jax.ShapeDtypeStruct((nl, r, c), F32)
    return pl.pallas_call(body, name=name, grid=(nl, r // tr), in_specs=[blk] * 4, out_specs=(blk, blk, blk),
                          out_shape=(shape, shape, shape), compiler_params=_params(2))(w, g, m, v)


ADA_SHARD = N_MOD * D_MODEL // 4
ADA_COLS = 512
ADA_ROWS = 256
HIGHEST = lax.Precision.HIGHEST


def _ada_mod(c_all, w_ada, b_sh, name):
    def body(c_ref, w_ref, b_ref, o_ref):
        cv = c_ref[...]
        o_ref[...] = jnp.dot(cv * _sigmoid(cv), w_ref[...], precision=HIGHEST, preferred_element_type=F32) + b_ref[...]

    return pl.pallas_call(
        body, name=name, grid=(DEPTH, ADA_SHARD // ADA_COLS),
        in_specs=[pl.BlockSpec((8, D_MODEL), lambda l, j: (0, 0)),
                  pl.BlockSpec((None, D_MODEL, ADA_COLS), lambda l, j: (l, 0, j)),
                  pl.BlockSpec((None, 1, ADA_COLS), lambda l, j: (l, 0, j))],
        out_specs=pl.BlockSpec((None, 8, ADA_COLS), lambda l, j: (l, 0, j)),
        out_shape=jax.ShapeDtypeStruct((DEPTH, 8, ADA_SHARD), F32), compiler_params=_params(2))(c_all, w_ada, b_sh)


def _ada_update(c_all, dmod_sh, w, m, v, name):
    def body(c_ref, d_ref, w_ref, m_ref, v_ref, g_ref, dl_ref, m2_ref, v2_ref):
        cv = c_ref[...]
        g = lax.dot_general(cv * _sigmoid(cv), d_ref[...], (((0,), (0,)), ((), ())), precision=HIGHEST,
                            preferred_element_type=F32)
        g_ref[...] = g
        dl_ref[...], m2_ref[...], v2_ref[...] = _adamw_math(w_ref[...], g, m_ref[...], v_ref[...])

    blk = pl.BlockSpec((None, ADA_ROWS, ADA_SHARD), lambda l, i: (l, i, 0))
    shape = jax.ShapeDtypeStruct((DEPTH, D_MODEL, ADA_SHARD), F32)
    return pl.pallas_call(
        body, name=name, grid=(DEPTH, D_MODEL // ADA_ROWS),
        in_specs=[pl.BlockSpec((8, ADA_ROWS), lambda l, i: (0, i)),
                  pl.BlockSpec((None, 8, ADA_SHARD), lambda l, i: (l, 0, 0)), blk, blk, blk],
        out_specs=(blk,) * 4, out_shape=(shape,) * 4, compiler_params=_params(2))(c_all, dmod_sh, w, m, v)


def _sum_devices(packs, name):
    def body(p_ref, o_ref):
        acc = p_ref[0]
        for dev in range(1, 8):
            acc = acc + p_ref[dev]
        o_ref[...] = acc

    return pl.pallas_call(body, name=name, out_shape=jax.ShapeDtypeStruct(packs.shape[1:], F32))(packs)


def _mesh_pos():
    return lax.axis_index("x"), lax.axis_index("y"), lax.axis_index("c")


def _flip(v, bit):
    return 1 - v if bit else v


def _allgather_devices(x, name):
    m_per, n = x.shape

    def body(x_ref, out_ref, send_sems, recv_sems, local_sem):
        ix, iy, ic = _mesh_pos()
        me, sibling = (ix, iy, ic), (ix, iy, 1 - ic)
        chips = [(1 - ix, iy), (ix, 1 - iy), (1 - ix, 1 - iy)]

        def rows(px, py, pc):
            return out_ref.at[pl.ds((4 * px + 2 * py + pc) * m_per, m_per), :]

        def copy(k, block, to, src=None):
            return pltpu.make_async_remote_copy(
                src_ref=rows(*block) if src is None else src, dst_ref=rows(*block),
                send_sem=send_sems.at[k], recv_sem=recv_sems.at[k], device_id=to, device_id_type=MESH)

        mine = pltpu.make_async_copy(x_ref, rows(*me), local_sem)
        mine.start()
        first = [copy(0, me, sibling, src=x_ref)]
        first += [copy(1 + j, me, (*chip, ic), src=x_ref) for j, chip in enumerate(chips)]
        for cp in first:
            cp.start()
        passed = [copy(4 + j, (*chip, ic), sibling) for j, chip in enumerate(chips)]
        for j, chip in enumerate(chips):
            copy(1 + j, (*chip, ic), me).wait_recv()
            passed[j].start()
        copy(0, sibling, me).wait_recv()
        for j, chip in enumerate(chips):
            copy(4 + j, (*chip, 1 - ic), me).wait_recv()
        for cp in first + passed:
            cp.wait_send()
        mine.wait()

    return pl.pallas_call(
        body, name=name, out_shape=jax.ShapeDtypeStruct((8 * m_per, n), x.dtype),
        in_specs=[pl.BlockSpec(memory_space=pltpu.VMEM)], out_specs=pl.BlockSpec(memory_space=pltpu.VMEM),
        scratch_shapes=[pltpu.SemaphoreType.DMA((7,)), pltpu.SemaphoreType.DMA((7,)), pltpu.SemaphoreType.DMA],
    )(x)


def _gather_chips(shards, name):
    n = len(shards)

    def body(*refs):
        ins, outs = refs[:n], refs[n:2 * n]
        send_sems, recv_sems, local_sems = refs[2 * n:]
        ix, iy, ic = _mesh_pos()
        me = 2 * ix + iy
        local = [pltpu.make_async_copy(ins[a], outs[a].at[me], local_sems.at[a]) for a in range(n)]
        for cp in local:
            cp.start()
        remote = []
        for a in range(n):
            for k in (1, 2, 3):
                px, py = _flip(ix, k & 2), _flip(iy, k & 1)
                sems = dict(send_sem=send_sems.at[3 * a + k - 1], recv_sem=recv_sems.at[3 * a + k - 1],
                            device_id=(px, py, ic), device_id_type=MESH)
                out_cp = pltpu.make_async_remote_copy(src_ref=ins[a], dst_ref=outs[a].at[me], **sems)
                in_cp = pltpu.make_async_remote_copy(src_ref=ins[a], dst_ref=outs[a].at[2 * px + py], **sems)
                out_cp.start()
                remote.append((out_cp, in_cp))
        for out_cp, in_cp in remote:
            out_cp.wait_send()
            in_cp.wait_recv()
        for cp in local:
            cp.wait()

    return pl.pallas_call(
        body, name=name, in_specs=[ANY] * n, out_specs=tuple([ANY] * n),
        out_shape=tuple(jax.ShapeDtypeStruct((4,) + t.shape, t.dtype) for t in shards),
        scratch_shapes=[pltpu.SemaphoreType.DMA((3 * n,)), pltpu.SemaphoreType.DMA((3 * n,)),
                        pltpu.SemaphoreType.DMA((n,))],
    )(*shards)


HBM = pl.BlockSpec(memory_space=pltpu.HBM)
SEM = pl.BlockSpec(memory_space=pltpu.SEMAPHORE)
DATAFLOW = pltpu.SideEffectType.DATAFLOW_SIDE_EFFECTING


def _peer_chip(ix, iy, k):
    return _flip(ix, k & 2), _flip(iy, k & 1)


def _gather_chips_start(shards, name):
    n = len(shards)

    def body(*refs):
        src, land = refs[:n], refs[n:2 * n]
        send_sems, recv_sems = refs[2 * n], refs[2 * n + 1]
        token = refs[-1]
        ix, iy, ic = _mesh_pos()
        me = 2 * ix + iy
        for a in range(n):
            for k in (1, 2, 3):
                px, py = _peer_chip(ix, iy, k)
                pltpu.make_async_remote_copy(
                    src_ref=src[a], dst_ref=land[a].at[me], send_sem=send_sems.at[3 * a + k - 1],
                    recv_sem=recv_sems.at[3 * a + k - 1], device_id=(px, py, ic), device_id_type=MESH).start()
        token[...] = jnp.zeros_like(token)

    hbm = lambda shape, dtype: pltpu.HBM(shape, dtype)
    operands = ([pltpu.with_memory_space_constraint(t, pltpu.HBM) for t in shards]
                + [pltpu.with_memory_space_constraint(lax.empty((4,) + t.shape, t.dtype), pltpu.HBM) for t in shards])
    return pl.pallas_call(
        body, name=name,
        out_shape=(pltpu.SemaphoreType.DMA((3 * n,)), pltpu.SemaphoreType.DMA((3 * n,)),
                   *[hbm(t.shape, t.dtype) for t in shards], *[hbm((4,) + t.shape, t.dtype) for t in shards],
                   jax.ShapeDtypeStruct((8, LANES), F32)),
        in_specs=(HBM,) * (2 * n),
        out_specs=(SEM, SEM) + (HBM,) * (2 * n) + (pl.BlockSpec(memory_space=pltpu.VMEM),),
        input_output_aliases={a: 2 + a for a in range(2 * n)},
        compiler_params=pltpu.CompilerParams(has_side_effects=DATAFLOW),
    )(*operands)


def _gather_chips_wait(started, after, name):
    send_sems, recv_sems = started[0], started[1]
    thru = started[2:-1]
    n = len(thru) // 2

    def body(*refs):
        src, land = refs[:n], refs[n:2 * n]
        send_sems, recv_sems = refs[2 * n], refs[2 * n + 1]
        ix, iy, ic = _mesh_pos()
        for a in range(n):
            for k in (1, 2, 3):
                px, py = _peer_chip(ix, iy, k)
                cp = pltpu.make_async_remote_copy(
                    src_ref=src[a], dst_ref=land[a].at[2 * px + py], send_sem=send_sems.at[3 * a + k - 1],
                    recv_sem=recv_sems.at[3 * a + k - 1], device_id=(px, py, ic), device_id_type=MESH)
                cp.wait_send()
                cp.wait_recv()

    outs = pl.pallas_call(
        body, name=name,
        out_shape=tuple(pltpu.HBM(t.shape, t.dtype) for t in thru),
        in_specs=(HBM,) * (2 * n) + (SEM, SEM, ANY), out_specs=(HBM,) * (2 * n),
        input_output_aliases={a: a for a in range(2 * n)},
        compiler_params=pltpu.CompilerParams(has_side_effects=DATAFLOW),
    )(*thru, send_sems, recv_sems, after)
    return outs[:n], outs[n:]


BIG_KINDS = (("w_in", "col", D_MODEL, IN_COLS), ("w_out", "row", D_MODEL, D_MODEL),
             ("w_up", "col", D_MODEL, 2 * D_FF), ("w_down", "row", D_FF, D_MODEL))


def _piece_shape(how, r, c):
    return (r // 2, c // 4) if how == "col" else (r // 8, c)


def _aligned(start, multiple):
    return start if isinstance(start, int) else pl.multiple_of(start, multiple)


def _piece(ref, how, r, c, chip, half):
    if how == "col":
        return ref.at[pl.ds(_aligned(half * (r // 2), 8), r // 2), pl.ds(_aligned(chip * (c // 4), LANES), c // 4)]
    n = r // 4
    return ref.at[pl.ds(_aligned(chip * n + half * (n // 2), 8), n // 2), :]


def _rs_pair_exchange(grads, name):
    nk = len(BIG_KINDS)
    flat = [grads[ki][l] for ki in range(nk) for l in range(DEPTH)]
    per = DEPTH * 4

    def body(*refs):
        g, land = refs[:nk * DEPTH], refs[nk * DEPTH:nk * DEPTH + nk]
        send_sems, recv_sems = refs[nk * DEPTH + nk:]
        ix, iy, ic = _mesh_pos()
        sibling = (ix, iy, 1 - ic)
        copies = []
        for ki, (_, how, r, c) in enumerate(BIG_KINDS):
            for l in range(DEPTH):
                for j in range(4):
                    sem = ki * per + l * 4 + j
                    rem = pltpu.make_async_remote_copy(
                        src_ref=_piece(g[ki * DEPTH + l], how, r, c, j, 1 - ic), dst_ref=land[ki].at[l, j],
                        send_sem=send_sems.at[sem], recv_sem=recv_sems.at[sem], device_id=sibling, device_id_type=MESH)
                    rem.start()
                    copies.append(rem)
        for rem in copies:
            rem.wait_send()
            rem.wait_recv()

    shapes = [jax.ShapeDtypeStruct((DEPTH, 4) + _piece_shape(how, r, c), F32) for _, how, r, c in BIG_KINDS]
    return pl.pallas_call(
        body, name=name, in_specs=[ANY] * len(flat), out_specs=tuple([ANY] * nk), out_shape=tuple(shapes),
        scratch_shapes=[pltpu.SemaphoreType.DMA((nk * per,))] * 2,
    )(*flat)


def _pair_sum(g, theirs, layer, how, core, name):
    r, c = g.shape
    pr, pc = _piece_shape(how, r, c)
    if how == "col":
        mine_spec = pl.BlockSpec((pr, pc), lambda j, core_ref: (core_ref[0], j))
    else:
        mine_spec = pl.BlockSpec((pr, pc), lambda j, core_ref: (2 * j + core_ref[0], 0))

    def body(core_ref, g_ref, t_ref, o_ref, ob_ref):
        total = g_ref[...] + t_ref[...]
        o_ref[...] = total
        ob_ref[...] = total.astype(BF16)

    out_blk = pl.BlockSpec((None, pr, pc), lambda j, core_ref: (j, 0, 0))
    return pl.pallas_call(
        body, name=name,
        grid_spec=pltpu.PrefetchScalarGridSpec(
            num_scalar_prefetch=1, grid=(4,),
            in_specs=[mine_spec, pl.BlockSpec((None, None, pr, pc), lambda j, core_ref: (layer, j, 0, 0))],
            out_specs=(out_blk, out_blk)),
        out_shape=(jax.ShapeDtypeStruct((4, pr, pc), F32), jax.ShapeDtypeStruct((4, pr, pc), BF16)),
        compiler_params=_params(1))(core, g, theirs)


def _rs_chip_exchange(pair_sums, name):
    nk = len(pair_sums)
    flat = [pair_sums[ki][l] for ki in range(nk) for l in range(DEPTH)]

    def body(*refs):
        src, dst = refs[:nk * DEPTH], refs[nk * DEPTH:nk * DEPTH + nk]
        send_sems, recv_sems = refs[nk * DEPTH + nk:]
        ix, iy, ic = _mesh_pos()
        copies = []
        for ki in range(nk):
            for l in range(DEPTH):
                for k in (1, 2, 3):
                    px, py = _flip(ix, k & 2), _flip(iy, k & 1)
                    sem = (ki * DEPTH + l) * 3 + k - 1
                    rem = pltpu.make_async_remote_copy(
                        src_ref=src[ki * DEPTH + l].at[2 * px + py], dst_ref=dst[ki].at[l, k - 1],
                        send_sem=send_sems.at[sem], recv_sem=recv_sems.at[sem], device_id=(px, py, ic), device_id_type=MESH)
                    rem.start()
                    copies.append(rem)
        for rem in copies:
            rem.wait_send()
            rem.wait_recv()

    return pl.pallas_call(
        body, name=name, in_specs=[ANY] * len(flat), out_specs=tuple([ANY] * nk),
        out_shape=tuple(jax.ShapeDtypeStruct((DEPTH, 3) + pair_sums[ki][0].shape[1:], pair_sums[ki][0].dtype)
                        for ki in range(nk)),
        scratch_shapes=[pltpu.SemaphoreType.DMA((nk * DEPTH * 3,))] * 2,
    )(*flat)


def _chip_sum(own, others, layer, chip, name):
    _, pr, pc = own.shape

    def body(chip_ref, own_ref, s1, s2, s3, o_ref):
        o_ref[...] = ((own_ref[...] + s1[...].astype(F32)) + s2[...].astype(F32)) + s3[...].astype(F32)

    slot = lambda k: pl.BlockSpec((None, None, pr, pc), lambda i, chip_ref: (layer, k, 0, 0))
    return pl.pallas_call(
        body, name=name,
        grid_spec=pltpu.PrefetchScalarGridSpec(
            num_scalar_prefetch=1, grid=(1,),
            in_specs=[pl.BlockSpec((None, pr, pc), lambda i, chip_ref: (chip_ref[0], 0, 0)), slot(0), slot(1), slot(2)],
            out_specs=pl.BlockSpec((pr, pc), lambda i, chip_ref: (0, 0))),
        out_shape=jax.ShapeDtypeStruct((pr, pc), F32), compiler_params=_params(1))(chip, own, others, others, others)


def _rs_pair_share(halves, name):
    nk = len(halves)
    flat = [halves[ki][l] for ki in range(nk) for l in range(DEPTH)]

    def body(*refs):
        src, dst = refs[:nk * DEPTH], refs[nk * DEPTH:nk * DEPTH + nk]
        send_sems, recv_sems = refs[nk * DEPTH + nk:]
        ix, iy, ic = _mesh_pos()
        copies = []
        for ki in range(nk):
            for l in range(DEPTH):
                sem = ki * DEPTH + l
                rem = pltpu.make_async_remote_copy(
                    src_ref=src[sem], dst_ref=dst[ki].at[l], send_sem=send_sems.at[sem], recv_sem=recv_sems.at[sem],
                    device_id=(ix, iy, 1 - ic), device_id_type=MESH)
                rem.start()
                copies.append(rem)
        for rem in copies:
            rem.wait_send()
            rem.wait_recv()

    return pl.pallas_call(
        body, name=name, in_specs=[ANY] * len(flat), out_specs=tuple([ANY] * nk),
        out_shape=tuple(jax.ShapeDtypeStruct((DEPTH,) + halves[ki][0].shape, F32) for ki in range(nk)),
        scratch_shapes=[pltpu.SemaphoreType.DMA((nk * DEPTH,))] * 2,
    )(*flat)


def _adamw_halves(w, mine, theirs, m, v, core, name):
    nl, pr, pc = theirs.shape
    shape = w.shape
    view = lambda t: t.reshape(nl, 2, pr, pc)
    tr = _row_tile(pr, pc, 256 * 1024)

    def body(core_ref, w_ref, a0_ref, a1_ref, t_ref, m_ref, v_ref, g_ref, d_ref, m2_ref, v2_ref):
        own = jnp.where(pl.program_id(0) == 0, a0_ref[...], a1_ref[...])
        g = jnp.where(pl.program_id(1) == core_ref[0], own, t_ref[...])
        g_ref[...] = g
        d_ref[...], m2_ref[...], v2_ref[...] = _adamw_math(w_ref[...], g, m_ref[...], v_ref[...])

    blk = pl.BlockSpec((None, None, tr, pc), lambda l, h, i, core_ref: (l, h, i, 0))
    own_blk = pl.BlockSpec((tr, pc), lambda l, h, i, core_ref: (i, 0))
    out = jax.ShapeDtypeStruct((nl, 2, pr, pc), F32)
    outs = pl.pallas_call(
        body, name=name,
        grid_spec=pltpu.PrefetchScalarGridSpec(
            num_scalar_prefetch=1, grid=(nl, 2, pr // tr),
            in_specs=[blk, own_blk, own_blk, pl.BlockSpec((None, tr, pc), lambda l, h, i, core_ref: (l, i, 0)), blk, blk],
            out_specs=(blk,) * 4),
        out_shape=(out,) * 4, compiler_params=_params(3),
    )(core, view(w), mine[0], mine[1], theirs, view(m), view(v))
    return tuple(t.reshape(shape) for t in outs)


def _reduce_scatter_big(grads, core, chip):
    theirs = _rs_pair_exchange(grads, "rs_pair_exchange")
    pair_sums = [[_pair_sum(grads[ki][l], theirs[ki], l, how, core, f"rs_pair_sum_{kind}") for l in range(DEPTH)]
                 for ki, (kind, how, _, _) in enumerate(BIG_KINDS)]
    slots = _rs_chip_exchange([[both[1] for both in row] for row in pair_sums], "rs_chip_exchange")
    halves = [[_chip_sum(pair_sums[ki][l][0], slots[ki], l, chip, f"rs_chip_sum_{kind}") for l in range(DEPTH)]
              for ki, (kind, _, _, _) in enumerate(BIG_KINDS)]
    other = _rs_pair_share(halves, "rs_pair_share")
    return list(zip(halves, other))


WEIGHT_NAMES = ("w_ada", "b_ada", "norm1_w", "w_in", "conv_a_w", "conv_a_b", "ln_a_w", "ln_a_b", "lb_gamma",
                "rec_norm_w", "w_out", "norm2_w", "w_up", "conv_f_w", "w_down", "final_norm_w")
SMALL_PARAMS = (("b_ada", (DEPTH, N_MOD * D_MODEL), None), ("norm1_w", (DEPTH, D_MODEL), None),
                ("conv_a_w", (DEPTH, CONV_WIDTH, CONV_CH), 2), ("conv_a_b", (DEPTH, CONV_CH), None),
                ("ln_a_w", (DEPTH, CONV_CH), None), ("ln_a_b", (DEPTH, CONV_CH), None),
                ("lb_gamma", (DEPTH, 2, REC_WIDTH), 2), ("rec_norm_w", (DEPTH, REC_WIDTH), None),
                ("norm2_w", (DEPTH, D_MODEL), None), ("conv_f_w", (DEPTH, 3, 2 * D_FF), 2),
                ("final_norm_w", (D_MODEL,), None))


def _pack_rows(parts):
    flat = jnp.concatenate([p.reshape(-1) for p in parts])
    total = flat.shape[0]
    padded = -(-total // (8 * LANES)) * (8 * LANES)
    return jnp.pad(flat, (0, padded - total)).reshape(padded // LANES, LANES)


def _unpack(flat, shapes):
    out, off = [], 0
    for shp in shapes:
        size = int(np.prod(shp))
        out.append(flat[off:off + size].reshape(shp))
        off += size
    return out


def _unstack_chips(t, axis):
    return jnp.concatenate([t[j] for j in range(4)], axis=axis)


def kernel(x, c, w_ada, b_ada, norm1_w, w_in, conv_a_w, conv_a_b, ln_a_w, ln_a_b, lb_gamma, rec_norm_w, w_out, norm2_w, w_up, conv_f_w, w_down, final_norm_w, loss_target, m_w_ada, m_b_ada, m_norm1_w, m_w_in, m_conv_a_w, m_conv_a_b, m_ln_a_w, m_ln_a_b, m_lb_gamma, m_rec_norm_w, m_w_out, m_norm2_w, m_w_up, m_conv_f_w, m_w_down, m_final_norm_w, v_w_ada, v_b_ada, v_norm1_w, v_w_in, v_conv_a_w, v_conv_a_b, v_ln_a_w, v_ln_a_b, v_lb_gamma, v_rec_norm_w, v_w_out, v_norm2_w, v_w_up, v_conv_f_w, v_w_down, v_final_norm_w):
    params = dict(zip(WEIGHT_NAMES, (w_ada, b_ada, norm1_w, w_in, conv_a_w, conv_a_b, ln_a_w, ln_a_b, lb_gamma,
                                     rec_norm_w, w_out, norm2_w, w_up, conv_f_w, w_down, final_norm_w)))
    mom1 = dict(zip(WEIGHT_NAMES, (m_w_ada, m_b_ada, m_norm1_w, m_w_in, m_conv_a_w, m_conv_a_b, m_ln_a_w, m_ln_a_b,
                                   m_lb_gamma, m_rec_norm_w, m_w_out, m_norm2_w, m_w_up, m_conv_f_w, m_w_down,
                                   m_final_norm_w)))
    mom2 = dict(zip(WEIGHT_NAMES, (v_w_ada, v_b_ada, v_norm1_w, v_w_in, v_conv_a_w, v_conv_a_b, v_ln_a_w, v_ln_a_b,
                                   v_lb_gamma, v_rec_norm_w, v_w_out, v_norm2_w, v_w_up, v_conv_f_w, v_w_down,
                                   v_final_norm_w)))
    ix, iy, ic = _mesh_pos()
    chip = 2 * ix + iy
    dev = 2 * chip + ic

    c_all = _allgather_devices(c.reshape(8, LANES), "gather_cond").reshape(8, D_MODEL)
    b_sh = lax.dynamic_slice_in_dim(b_ada, chip * ADA_SHARD, ADA_SHARD, axis=1)
    mod_sh = _ada_mod(c_all, w_ada, b_sh.reshape(DEPTH, 1, ADA_SHARD), "ada_mod")
    w_in_b, w_out_b, w_up_b, w_down_b = (t.astype(BF16) for t in (w_in, w_out, w_up, w_down))
    first = _gather_chips([mod_sh, conv_a_w, conv_f_w, lb_gamma, w_in_b[0]], "gather_first")
    later = [w_in_b[1], w_out_b, w_up_b, w_down_b]
    started = _gather_chips_start(later, "gather_rest_start")
    mod_mine = lax.dynamic_index_in_dim(first[0], dev, axis=2, keepdims=False) + started[-1][0, 0]
    mods = [jnp.concatenate([mod_mine[j, l] for j in range(4)]).reshape(N_MOD, D_MODEL) for l in range(DEPTH)]
    conv_a_w_f, conv_f_w_f, gamma_f = (_unstack_chips(first[k], 2) for k in (1, 2, 3))
    w_in0 = _unstack_chips(first[4], 1)

    def later_weights(after):
        own, lands = _gather_chips_wait(started, after, "gather_rest_wait")
        full = [lax.dynamic_update_index_in_dim(land, mine, chip, 0) for land, mine in zip(lands, own)]
        return (_unstack_chips(full[0], 1), _unstack_chips(full[1], 1), _unstack_chips(full[2], 2),
                _unstack_chips(full[3], 1))

    lb1, p_soft = _lower_bounds(gamma_f.reshape(DEPTH, 2 * REC_WIDTH), "lower_bounds")
    lbs = [jnp.zeros((2, REC_WIDTH), F32), lb1.reshape(2, REC_WIDTH)]
    small = []
    for l in range(DEPTH):
        small.append(dict(norm1_w=norm1_w[l][None], conv_a_w=conv_a_w_f[l], conv_a_b=conv_a_b[l][None],
                          ln_a_w=ln_a_w[l][None], ln_a_b=ln_a_b[l][None], rec_norm_w=rec_norm_w[l],
                          norm2_w=norm2_w[l][None], conv_f_w=conv_f_w_f[l]))

    loss, dx, grads, dfw = _sequence_step(x[0], loss_target[0], mods, lbs, small, w_in0, later_weights,
                                          final_norm_w[None])
    loss = lax.psum(loss, ("x", "y", "c"))

    dgamma = _lower_bounds_bwd(grads[1]["lb"].reshape(1, 2 * REC_WIDTH), p_soft, "lower_bounds_bwd")
    dmod = [jnp.concatenate(grads[l]["mod"], axis=1) for l in range(DEPTH)]
    stack = lambda key: jnp.stack([grads[l][key] for l in range(DEPTH)])
    local_small = dict(b_ada=jnp.concatenate(dmod, axis=0), norm1_w=stack("norm1_w"), conv_a_w=stack("conv_a_w"),
                       conv_a_b=stack("conv_a_b"), ln_a_w=stack("ln_a_w"), ln_a_b=stack("ln_a_b"), lb_gamma=dgamma,
                       rec_norm_w=stack("rec_norm_w"), norm2_w=stack("norm2_w"), conv_f_w=stack("conv_f_w"),
                       final_norm_w=dfw)
    pack = _pack_rows([local_small[name] for name, _, _ in SMALL_PARAMS])
    rows = pack.shape[0]
    packs = _allgather_devices(pack, "gather_small_grads").reshape(8, rows, LANES)
    summed = _sum_devices(packs, "sum_small_grads").reshape(-1)
    small_grads = dict(zip([n for n, _, _ in SMALL_PARAMS], _unpack(summed, [shp for _, shp, _ in SMALL_PARAMS])))

    dmod_all = packs.reshape(8, rows * LANES)[:, :DEPTH * N_MOD * D_MODEL].reshape(8, DEPTH, N_MOD * D_MODEL)
    dmod_sh = lax.dynamic_slice_in_dim(dmod_all, chip * ADA_SHARD, ADA_SHARD, axis=2).transpose(1, 0, 2)
    g_ada, d_ada, m_ada, v_ada = _ada_update(c_all, dmod_sh, w_ada, m_w_ada, v_w_ada, "ada_update")

    for name, shp, axis in SMALL_PARAMS:
        if axis is not None:
            width = shp[axis] // 4
            small_grads[name] = lax.dynamic_slice_in_dim(small_grads[name], chip * width, width, axis=axis)
    names = [n for n, _, _ in SMALL_PARAMS]
    packed = [_pack_rows([src[n] for n in names])[None] for src in (params, small_grads, mom1, mom2)]
    small_out = _adamw(*packed, "adamw_small")
    shapes = [params[n].shape for n in names]
    small_delta, small_m, small_v = (dict(zip(names, _unpack(t.reshape(-1), shapes))) for t in small_out)

    core_id, chip_id = ic.astype(jnp.int32).reshape(1), chip.astype(jnp.int32).reshape(1)
    summed_big = _reduce_scatter_big([[grads[l][name] for l in range(DEPTH)] for name, _, _, _ in BIG_KINDS],
                                     core_id, chip_id)
    grad, delta, new_m, new_v = dict(small_grads), small_delta, small_m, small_v
    grad["w_ada"], delta["w_ada"], new_m["w_ada"], new_v["w_ada"] = g_ada, d_ada, m_ada, v_ada
    for (name, _, _, _), (mine, theirs) in zip(BIG_KINDS, summed_big):
        grad[name], delta[name], new_m[name], new_v[name] = _adamw_halves(
            params[name], mine, theirs, mom1[name], mom2[name], core_id, f"adamw_{name}")

    return (loss, dx[None], *[grad[n] for n in WEIGHT_NAMES], *[delta[n] for n in WEIGHT_NAMES],
            *[new_m[n] for n in WEIGHT_NAMES], *[new_v[n] for n in WEIGHT_NAMES])
```

```python
import numpy as np
import jax
import jax.numpy as jnp
from jax import lax
from jax.experimental import pallas as pl
from jax.experimental.pallas import tpu as pltpu

F32 = jnp.float32
BF16 = jnp.bfloat16

D_MODEL = 1024
DEPTH = 2
HEAD_DIM = 64
CONV_CH = 256
CONV_WIDTH = 31
ATT_WIDTH = 384
N_HEADS = 6
DILATIONS = (1, 4, 16)
ATT_HALF = 64
ATT_BLOCK = 128
ALIBI_MAX_EXP = 8.0
MASK_VALUE = -1e30
REC_WIDTH = 384
REC_CHUNK = 64
F_TINY = 1e-30
D_FF = 2816
N_MOD = 6
EPS = 1e-6
G_CONV = (0, 512)
G_QKV = (512, 1664)
G_REC = (1664, 3584)
IN_COLS = 3584

ADAM_LR = 0.001
ADAM_B1 = 0.9
ADAM_B2 = 0.999
ADAM_EPS = 1e-08
ADAM_WD = 0.01
ADAM_STEP = 10

VMEM_LIMIT_BYTES = 56 * 1024 * 1024
LANES = 128
MESH = pl.DeviceIdType.MESH
ANY = pl.BlockSpec(memory_space=pl.ANY)


def _params(n_axes):
    return pltpu.CompilerParams(dimension_semantics=("arbitrary",) * n_axes,
                                vmem_limit_bytes=VMEM_LIMIT_BYTES)


def _tile(n, target):
    best = None
    for t in range(LANES, min(n, target) + 1, LANES):
        if n % t == 0:
            best = t
    return best or n


def _sigmoid(x):
    return jax.nn.sigmoid(x)


def _silu_grad(x):
    s = _sigmoid(x)
    return s * (1.0 + x * (1.0 - s))


MM_ACC_ELEMS = 1536 * 1024


def _matmul(a, b, mode, out_dtype, name, tm=1024, tn=1792, tk=1792):
    if mode == "nn":
        (m, k), (k2, n) = a.shape, b.shape
    elif mode == "nt":
        (m, k), (n, k2) = a.shape, b.shape
    else:
        (k, m), (k2, n) = a.shape, b.shape
    assert k == k2, (a.shape, b.shape, mode)
    tn, tk = _tile(n, tn), _tile(k, tk)
    tm = _tile(m, min(tm, MM_ACC_ELEMS // tn))
    nk = k // tk
    a_spec = (pl.BlockSpec((tk, tm), lambda i, j, kk: (kk, i)) if mode == "tn"
              else pl.BlockSpec((tm, tk), lambda i, j, kk: (i, kk)))
    b_spec = (pl.BlockSpec((tn, tk), lambda i, j, kk: (j, kk)) if mode == "nt"
              else pl.BlockSpec((tk, tn), lambda i, j, kk: (kk, j)))
    dims = {"nn": (((1,), (0,)), ((), ())), "nt": (((1,), (1,)), ((), ())),
            "tn": (((0,), (0,)), ((), ()))}[mode]

    def body(a_ref, b_ref, o_ref, *scratch):
        part = lax.dot_general(a_ref[...].astype(BF16), b_ref[...].astype(BF16), dims, preferred_element_type=F32)
        if nk == 1:
            o_ref[...] = part.astype(out_dtype)
            return
        acc_ref, = scratch
        kk = pl.program_id(2)

        @pl.when(kk == 0)
        def _():
            acc_ref[...] = part

        @pl.when(kk > 0)
        def _():
            acc_ref[...] += part

        @pl.when(kk == nk - 1)
        def _():
            o_ref[...] = acc_ref[...].astype(out_dtype)

    return pl.pallas_call(
        body, name=name, grid=(m // tm, n // tn, nk),
        in_specs=[a_spec, b_spec],
        out_specs=pl.BlockSpec((tm, tn), lambda i, j, kk: (i, j)),
        out_shape=jax.ShapeDtypeStruct((m, n), out_dtype),
        scratch_shapes=[pltpu.VMEM((tm, tn), F32)] if nk > 1 else [],
        compiler_params=pltpu.CompilerParams(dimension_semantics=("parallel", "parallel", "arbitrary"),
                                             vmem_limit_bytes=VMEM_LIMIT_BYTES),
    )(a, b)


NORM_ROWS = 256


def _row_spec(width, rows=NORM_ROWS):
    return pl.BlockSpec((rows, width), lambda i: (i, 0))


def _vec_spec(width):
    return pl.BlockSpec((1, width), lambda i: (0, 0))


def _resid_norm_mod(x, r, g, nw, sc, sh, name):
    s, d = x.shape
    has_r = r is not None

    def body(*refs):
        if has_r:
            x_ref, r_ref, g_ref, nw_ref, sc_ref, sh_ref, xn_ref, h_ref = refs
            xn = x_ref[...] + g_ref[...] * r_ref[...]
            xn_ref[...] = xn
        else:
            x_ref, nw_ref, sc_ref, sh_ref, h_ref = refs
            xn = x_ref[...]
        rstd = lax.rsqrt(jnp.mean(xn * xn, axis=-1, keepdims=True) + EPS)
        y = xn * rstd * nw_ref[...]
        h_ref[...] = (y * (1.0 + sc_ref[...]) + sh_ref[...]).astype(BF16)

    if has_r:
        ins, in_specs = (x, r, g, nw, sc, sh), [_row_spec(d), _row_spec(d)] + [_vec_spec(d)] * 4
        out_shape = (jax.ShapeDtypeStruct((s, d), F32), jax.ShapeDtypeStruct((s, d), BF16))
        out_specs = (_row_spec(d), _row_spec(d))
    else:
        ins, in_specs = (x, nw, sc, sh), [_row_spec(d)] + [_vec_spec(d)] * 3
        out_shape = jax.ShapeDtypeStruct((s, d), BF16)
        out_specs = _row_spec(d)
    return pl.pallas_call(body, name=name, grid=(s // NORM_ROWS,), in_specs=in_specs, out_specs=out_specs,
                          out_shape=out_shape, compiler_params=_params(1))(*ins)


def _final_loss(x, r, g, fw, tgt, name):
    s, d = x.shape

    def body(x_ref, r_ref, g_ref, fw_ref, t_ref, loss_ref, dx_ref, dr_ref, dg_ref, dfw_ref):
        @pl.when(pl.program_id(0) == 0)
        def _():
            loss_ref[...] = jnp.zeros_like(loss_ref)
            dg_ref[...] = jnp.zeros_like(dg_ref)
            dfw_ref[...] = jnp.zeros_like(dfw_ref)

        rr = r_ref[...]
        gg = g_ref[...]
        xn = x_ref[...] + gg * rr
        rstd = lax.rsqrt(jnp.mean(xn * xn, axis=-1, keepdims=True) + EPS)
        xh = xn * rstd
        fwv = fw_ref[...]
        e = xh * fwv - t_ref[...]
        loss_ref[...] += 0.5 * jnp.sum(jnp.mean(e * e, axis=-1, keepdims=True), axis=0, keepdims=True)
        dy = e * (1.0 / d)
        dfw_ref[...] += jnp.sum(dy * xh, axis=0, keepdims=True)
        dxh = dy * fwv
        dx = rstd * (dxh - xh * jnp.mean(dxh * xh, axis=-1, keepdims=True))
        dx_ref[...] = dx
        dr_ref[...] = (gg * dx).astype(BF16)
        dg_ref[...] += jnp.sum(dx * rr, axis=0, keepdims=True)

    return pl.pallas_call(
        body, name=name, grid=(s // NORM_ROWS,),
        in_specs=[_row_spec(d), _row_spec(d), _vec_spec(d), _vec_spec(d), _row_spec(d)],
        out_specs=(_vec_spec(LANES), _row_spec(d), _row_spec(d), _vec_spec(d), _vec_spec(d)),
        out_shape=(jax.ShapeDtypeStruct((1, LANES), F32), jax.ShapeDtypeStruct((s, d), F32),
                   jax.ShapeDtypeStruct((s, d), BF16), jax.ShapeDtypeStruct((1, d), F32),
                   jax.ShapeDtypeStruct((1, d), F32)),
        compiler_params=_params(1))(x, r, g, fw, tgt)


def _norm_bwd(x, dhs, dxres, nw, sc, g, r, name):
    s, d = x.shape
    n_dh = len(dhs)
    has_g = g is not None

    def body(*refs):
        x_ref = refs[0]
        dh_refs = refs[1:1 + n_dh]
        dxres_ref, nw_ref, sc_ref = refs[1 + n_dh:4 + n_dh]
        pos = 4 + n_dh
        if has_g:
            g_ref, r_ref = refs[pos:pos + 2]
            pos += 2
            dx_ref, dr_ref, dsh_ref, dsc_ref, dnw_ref, dg_ref = refs[pos:]
            accs = (dsh_ref, dsc_ref, dnw_ref, dg_ref)
        else:
            dx_ref, dsh_ref, dsc_ref, dnw_ref = refs[pos:]
            accs = (dsh_ref, dsc_ref, dnw_ref)

        @pl.when(pl.program_id(0) == 0)
        def _():
            for acc in accs:
                acc[...] = jnp.zeros_like(acc)

        xv = x_ref[...]
        dh = dh_refs[0][...]
        for extra in dh_refs[1:]:
            dh = dh + extra[...]
        rstd = lax.rsqrt(jnp.mean(xv * xv, axis=-1, keepdims=True) + EPS)
        xh = xv * rstd
        nwv = nw_ref[...]
        dsh_ref[...] += jnp.sum(dh, axis=0, keepdims=True)
        dsc_ref[...] += jnp.sum(dh * (xh * nwv), axis=0, keepdims=True)
        dy = dh * (1.0 + sc_ref[...])
        dnw_ref[...] += jnp.sum(dy * xh, axis=0, keepdims=True)
        dxh = dy * nwv
        dx = dxres_ref[...] + rstd * (dxh - xh * jnp.mean(dxh * xh, axis=-1, keepdims=True))
        dx_ref[...] = dx
        if has_g:
            dr_ref[...] = (g_ref[...] * dx).astype(BF16)
            dg_ref[...] += jnp.sum(dx * r_ref[...], axis=0, keepdims=True)

    ins = [x, *dhs, dxres, nw, sc]
    in_specs = [_row_spec(d)] * (2 + n_dh) + [_vec_spec(d)] * 2
    out_shape = [jax.ShapeDtypeStruct((s, d), F32)]
    out_specs = [_row_spec(d)]
    if has_g:
        ins += [g, r]
        in_specs += [_vec_spec(d), _row_spec(d)]
        out_shape.append(jax.ShapeDtypeStruct((s, d), BF16))
        out_specs.append(_row_spec(d))
    n_vec = 4 if has_g else 3
    out_shape += [jax.ShapeDtypeStruct((1, d), F32)] * n_vec
    out_specs += [_vec_spec(d)] * n_vec
    return pl.pallas_call(body, name=name, grid=(s // NORM_ROWS,), in_specs=in_specs, out_specs=tuple(out_specs),
                          out_shape=tuple(out_shape), compiler_params=_params(1))(*ins)


FFN_ROWS = 256
FFN_COLS = 1408
HALO = 16
INV_SQRT2 = 0.7071067811865476
INV_SQRT_2PI = 0.3989422804014327


def _gelu(x):
    return 0.5 * x * (1.0 + lax.erf(x * INV_SQRT2))


def _gelu_grad(x):
    return 0.5 * (1.0 + lax.erf(x * INV_SQRT2)) + x * (INV_SQRT_2PI * jnp.exp(-0.5 * x * x))


def _halo_specs(rows, cols, halo, n_rows_total, col_of):
    per = rows // halo
    last = n_rows_total // halo - 1
    cur = pl.BlockSpec((rows, cols), lambda j, i: (i, col_of(j)))
    prev = pl.BlockSpec((halo, cols), lambda j, i: (jnp.maximum(i * per - 1, 0), col_of(j)))
    nxt = pl.BlockSpec((halo, cols), lambda j, i: (jnp.minimum((i + 1) * per, last), col_of(j)))
    return [prev, cur, nxt]


def _shift_rows(x, k):
    n = x.shape[0]
    return pltpu.roll(x, k % n, axis=0)


def _conv3(ext, w):
    return w[0:1, :] * _shift_rows(ext, 1) + w[1:2, :] * ext + w[2:3, :] * _shift_rows(ext, -1)


def _ext_block(prev_ref, cur_ref, next_ref, i, n_i):
    prev = jnp.where(i > 0, prev_ref[...].astype(F32), 0.0)
    nxt = jnp.where(i < n_i - 1, next_ref[...].astype(F32), 0.0)
    return jnp.concatenate([prev, cur_ref[...].astype(F32), nxt], axis=0)


def _ffn_act(u, cw, name):
    s = u.shape[0]
    nc, ns = D_FF // FFN_COLS, s // FFN_ROWS

    def body(gp, gc, gn, vp, vc, vn, wg_ref, wv_ref, o_ref):
        i = pl.program_id(1)
        cg = _conv3(_ext_block(gp, gc, gn, i, ns), wg_ref[...])[HALO:HALO + FFN_ROWS]
        cv = _conv3(_ext_block(vp, vc, vn, i, ns), wv_ref[...])[HALO:HALO + FFN_ROWS]
        o_ref[...] = (_gelu(cg) * cv).astype(BF16)

    in_specs = (_halo_specs(FFN_ROWS, FFN_COLS, HALO, s, lambda j: j)
                + _halo_specs(FFN_ROWS, FFN_COLS, HALO, s, lambda j: j + nc)
                + [pl.BlockSpec((3, FFN_COLS), lambda j, i: (0, j)),
                   pl.BlockSpec((3, FFN_COLS), lambda j, i: (0, j + nc))])
    return pl.pallas_call(
        body, name=name, grid=(nc, ns), in_specs=in_specs,
        out_specs=pl.BlockSpec((FFN_ROWS, FFN_COLS), lambda j, i: (i, j)),
        out_shape=jax.ShapeDtypeStruct((s, D_FF), BF16), compiler_params=_params(2),
    )(u, u, u, u, u, u, cw, cw)


def _ffn_act_bwd(u, dact, cw, name):
    s = u.shape[0]
    nc, ns = D_FF // FFN_COLS, s // FFN_ROWS

    def body(gp, gc, gn, vp, vc, vn, dp, dc, dn, wg_ref, wv_ref, dug_ref, duv_ref, dwg_ref, dwv_ref):
        i = pl.program_id(1)

        @pl.when(i == 0)
        def _():
            dwg_ref[...] = jnp.zeros_like(dwg_ref)
            dwv_ref[...] = jnp.zeros_like(dwv_ref)

        ug = _ext_block(gp, gc, gn, i, ns)
        uv = _ext_block(vp, vc, vn, i, ns)
        da = _ext_block(dp, dc, dn, i, ns)
        wg, wv = wg_ref[...], wv_ref[...]
        cg, cv = _conv3(ug, wg), _conv3(uv, wv)
        dcg = da * cv * _gelu_grad(cg)
        dcv = da * _gelu(cg)
        inner = slice(HALO, HALO + FFN_ROWS)
        for d_c, uu, w, du_ref, dw_ref in ((dcg, ug, wg, dug_ref, dwg_ref), (dcv, uv, wv, duv_ref, dwv_ref)):
            du = w[0:1, :] * _shift_rows(d_c, -1) + w[1:2, :] * d_c + w[2:3, :] * _shift_rows(d_c, 1)
            du_ref[...] = du[inner].astype(BF16)
            d_in = d_c[inner]
            for tap in range(3):
                dw_ref[tap:tap + 1, :] += jnp.sum(d_in * _shift_rows(uu, 1 - tap)[inner], axis=0, keepdims=True)

    in_specs = (_halo_specs(FFN_ROWS, FFN_COLS, HALO, s, lambda j: j)
                + _halo_specs(FFN_ROWS, FFN_COLS, HALO, s, lambda j: j + nc)
                + _halo_specs(FFN_ROWS, FFN_COLS, HALO, s, lambda j: j)
                + [pl.BlockSpec((3, FFN_COLS), lambda j, i: (0, j)),
                   pl.BlockSpec((3, FFN_COLS), lambda j, i: (0, j + nc))])
    blk = pl.BlockSpec((FFN_ROWS, FFN_COLS), lambda j, i: (i, j))
    acc = pl.BlockSpec((HALO, FFN_COLS), lambda j, i: (0, j))
    return pl.pallas_call(
        body, name=name, grid=(nc, ns), in_specs=in_specs, out_specs=(blk, blk, acc, acc),
        out_shape=(jax.ShapeDtypeStruct((s, D_FF), BF16), jax.ShapeDtypeStruct((s, D_FF), BF16),
                   jax.ShapeDtypeStruct((HALO, D_FF), F32), jax.ShapeDtypeStruct((HALO, D_FF), F32)),
        compiler_params=_params(2),
    )(u, u, u, u, u, u, dact, dact, dact, cw, cw)


CONV_ROWS = 512
CONV_HALO = 16
CONV_PAD = CONV_WIDTH // 2


def _conv_halo_specs(cols, s):
    per = CONV_ROWS // CONV_HALO
    last = s // CONV_HALO - 1
    return [pl.BlockSpec((CONV_HALO, cols), lambda i: (jnp.maximum(i * per - 1, 0), 0)),
            pl.BlockSpec((CONV_ROWS, cols), lambda i: (i, 0)),
            pl.BlockSpec((CONV_HALO, cols), lambda i: (jnp.minimum((i + 1) * per, last), 0))]


def _glu_ext(pp, pc, pn, i, n_i):
    ext = _ext_block(pp, pc, pn, i, n_i)
    return ext[:, :CONV_CH] * _sigmoid(ext[:, CONV_CH:])


def _conv_mixer(pa, cw, cb, lnw, lnb, name):
    s = pa.shape[0]
    ns = s // CONV_ROWS

    def body(pp, pc, pn, cw_ref, cb_ref, lnw_ref, lnb_ref, o_ref, c_ref):
        i = pl.program_id(0)
        a = _glu_ext(pp, pc, pn, i, ns)
        acc = jnp.zeros((CONV_ROWS, CONV_CH), F32)
        for tap in range(CONV_WIDTH):
            acc = acc + cw_ref[tap:tap + 1, :] * _shift_rows(a, -(tap + 1))[:CONV_ROWS]
        cv = acc + cb_ref[...]
        c_ref[...] = cv
        mu = jnp.mean(cv, axis=-1, keepdims=True)
        xc = cv - mu
        rstd = lax.rsqrt(jnp.mean(xc * xc, axis=-1, keepdims=True) + EPS)
        y = xc * rstd * lnw_ref[...] + lnb_ref[...]
        o_ref[...] = (y * _sigmoid(y)).astype(BF16)

    vec = pl.BlockSpec((1, CONV_CH), lambda i: (0, 0))
    blk = pl.BlockSpec((CONV_ROWS, CONV_CH), lambda i: (i, 0))
    return pl.pallas_call(
        body, name=name, grid=(ns,),
        in_specs=_conv_halo_specs(2 * CONV_CH, s) + [pl.BlockSpec((CONV_WIDTH, CONV_CH), lambda i: (0, 0)), vec, vec, vec],
        out_specs=(blk, blk),
        out_shape=(jax.ShapeDtypeStruct((s, CONV_CH), BF16), jax.ShapeDtypeStruct((s, CONV_CH), F32)),
        compiler_params=_params(1))(pa, pa, pa, cw, cb, lnw, lnb)


def _conv_mixer_bwd_ln(cv, dout, lnw, lnb, name):
    s = cv.shape[0]

    def body(c_ref, do_ref, lnw_ref, lnb_ref, dc_ref, dlnw_ref, dlnb_ref, dcb_ref):
        @pl.when(pl.program_id(0) == 0)
        def _():
            dlnw_ref[...] = jnp.zeros_like(dlnw_ref)
            dlnb_ref[...] = jnp.zeros_like(dlnb_ref)
            dcb_ref[...] = jnp.zeros_like(dcb_ref)

        c = c_ref[...]
        mu = jnp.mean(c, axis=-1, keepdims=True)
        xc = c - mu
        rstd = lax.rsqrt(jnp.mean(xc * xc, axis=-1, keepdims=True) + EPS)
        xh = xc * rstd
        w = lnw_ref[...]
        y = xh * w + lnb_ref[...]
        dy = do_ref[...] * _silu_grad(y)
        dlnw_ref[...] += jnp.sum(dy * xh, axis=0, keepdims=True)
        dlnb_ref[...] += jnp.sum(dy, axis=0, keepdims=True)
        dxh = dy * w
        dc = rstd * (dxh - jnp.mean(dxh, axis=-1, keepdims=True) - xh * jnp.mean(dxh * xh, axis=-1, keepdims=True))
        dc_ref[...] = dc
        dcb_ref[...] += jnp.sum(dc, axis=0, keepdims=True)

    vec = pl.BlockSpec((1, CONV_CH), lambda i: (0, 0))
    blk = pl.BlockSpec((CONV_ROWS, CONV_CH), lambda i: (i, 0))
    return pl.pallas_call(
        body, name=name, grid=(s // CONV_ROWS,), in_specs=[blk, blk, vec, vec], out_specs=(blk, vec, vec, vec),
        out_shape=(jax.ShapeDtypeStruct((s, CONV_CH), F32),) + (jax.ShapeDtypeStruct((1, CONV_CH), F32),) * 3,
        compiler_params=_params(1))(cv, dout, lnw, lnb)


def _conv_mixer_bwd_conv(pa, dc, cw, name):
    s = pa.shape[0]
    ns = s // CONV_ROWS

    def body(pp, pc, pn, dp, dcc, dn, cw_ref, dpa_ref, dcw_ref):
        i = pl.program_id(0)

        @pl.when(i == 0)
        def _():
            dcw_ref[...] = jnp.zeros_like(dcw_ref)

        a = _glu_ext(pp, pc, pn, i, ns)
        dce = _ext_block(dp, dcc, dn, i, ns)
        dcur = dcc[...]
        da = jnp.zeros((CONV_ROWS, CONV_CH), F32)
        for tap in range(CONV_WIDTH):
            da = da + cw_ref[tap:tap + 1, :] * _shift_rows(dce, -(CONV_WIDTH - tap))[:CONV_ROWS]
            dcw_ref[tap:tap + 1, :] += jnp.sum(dcur * _shift_rows(a, -(tap + 1))[:CONV_ROWS], axis=0, keepdims=True)
        cur = pc[...]
        val, sg = cur[:, :CONV_CH], _sigmoid(cur[:, CONV_CH:])
        dpa_ref[:, :CONV_CH] = (da * sg).astype(BF16)
        dpa_ref[:, CONV_CH:] = (da * val * sg * (1.0 - sg)).astype(BF16)

    return pl.pallas_call(
        body, name=name, grid=(ns,),
        in_specs=_conv_halo_specs(2 * CONV_CH, s) + _conv_halo_specs(CONV_CH, s)
        + [pl.BlockSpec((CONV_WIDTH, CONV_CH), lambda i: (0, 0))],
        out_specs=(pl.BlockSpec((CONV_ROWS, 2 * CONV_CH), lambda i: (i, 0)),
                   pl.BlockSpec((32, CONV_CH), lambda i: (0, 0))),
        out_shape=(jax.ShapeDtypeStruct((s, 2 * CONV_CH), BF16), jax.ShapeDtypeStruct((32, CONV_CH), F32)),
        compiler_params=_params(1))(pa, pa, pa, dc, dc, dc, cw)


SLOPES = tuple(float(2.0 ** (-ALIBI_MAX_EXP * (h + 1) / N_HEADS)) for h in range(N_HEADS))
ATT_SCALE = HEAD_DIM ** -0.5


PAIR = 2 * HEAD_DIM
N_PAIRS = N_HEADS // 2
ATT_WIN = ATT_BLOCK + 2 * ATT_HALF


ATT_GROUPS = {1: 4, 4: 1, 16: 1}


def _window_specs(dil, n_steps, col_of):
    per = 2 * ATT_GROUPS[dil]
    rows, halo = ATT_BLOCK * dil * ATT_GROUPS[dil], ATT_HALF * dil
    return [pl.BlockSpec((halo, PAIR), lambda i, p: (jnp.maximum(per * i - 1, 0), col_of(p))),
            pl.BlockSpec((rows, PAIR), lambda i, p: (i, col_of(p))),
            pl.BlockSpec((halo, PAIR), lambda i, p: (jnp.minimum(per * (i + 1), per * n_steps - 1), col_of(p)))]


def _residue(ref, r, n, dil, start=0):
    return ref[pl.ds(start * dil + r, n, stride=dil), :] if dil > 1 else ref[pl.ds(start + r, n), :]


def _store_residue(ref, r, dil, start, val):
    if dil > 1:
        ref[pl.ds(start * dil + r, val.shape[0], stride=dil), :] = val
    else:
        ref[pl.ds(start + r, val.shape[0]), :] = val


def _residue_window(refs, r, dil, g=0):
    prev, cur, nxt = refs
    groups = ATT_GROUPS[dil]
    lo = max(g * ATT_BLOCK - ATT_HALF, 0)
    hi = min((g + 1) * ATT_BLOCK + ATT_HALF, groups * ATT_BLOCK)
    parts = [_residue(prev, r, ATT_HALF, dil)] if g == 0 else []
    parts.append(_residue(cur, r, hi - lo, dil, lo))
    if g == groups - 1:
        parts.append(_residue(nxt, r, ATT_HALF, dil))
    return jnp.concatenate(parts, axis=0)


def _band_masks(i, length, dil, transposed):
    shape = (ATT_WIN, ATT_BLOCK) if transposed else (ATT_BLOCK, ATT_WIN)
    row = lax.broadcasted_iota(jnp.int32, shape, 0)
    col = lax.broadcasted_iota(jnp.int32, shape, 1)
    wide = row if transposed else col
    dist = jnp.abs((row - col - ATT_HALF) if transposed else (row + ATT_HALF - col))
    wpos = i * ATT_BLOCK - ATT_HALF + wide
    valid = (dist <= ATT_HALF) & (wpos >= 0) & (wpos < length)
    return valid, dist.astype(F32) * float(dil)


def _attn_branch(qkv, dil, name):
    s = qkv.shape[0]
    groups = ATT_GROUPS[dil]
    rows = ATT_BLOCK * dil * groups
    n_steps = s // rows
    length = s // dil
    nt = (((1,), (1,)), ((), ()))

    def body(q_ref, kp, kc, kn, vp, vc, vn, o_ref, l_ref):
        i, pair = pl.program_id(0), pl.program_id(1)
        for g, r in [(g, r) for g in range(groups) for r in range(dil)]:
            valid, distf = _band_masks(i * groups + g, length, dil, False)
            q = _residue(q_ref, r, ATT_BLOCK, dil, g * ATT_BLOCK).astype(BF16)
            k = _residue_window((kp, kc, kn), r, dil, g).astype(BF16)
            v = _residue_window((vp, vc, vn), r, dil, g).astype(BF16)
            outs, lses = [], []
            for hh in range(2):
                sl = slice(hh * HEAD_DIM, (hh + 1) * HEAD_DIM)
                slope = jnp.where(pair == 0, SLOPES[hh], jnp.where(pair == 1, SLOPES[2 + hh], SLOPES[4 + hh]))
                sc = lax.dot_general(q[:, sl], k[:, sl], nt, preferred_element_type=F32) * ATT_SCALE - slope * distf
                sc = jnp.where(valid, sc, MASK_VALUE)
                m = jnp.max(sc, axis=-1, keepdims=True)
                p = jnp.exp(sc - m)
                den = jnp.sum(p, axis=-1, keepdims=True)
                outs.append(jnp.dot(p.astype(BF16), v[:, sl], preferred_element_type=F32) / den)
                lses.append(jnp.broadcast_to(m + jnp.log(den), (ATT_BLOCK, HEAD_DIM)))
            _store_residue(o_ref, r, dil, g * ATT_BLOCK, jnp.concatenate(outs, axis=1))
            _store_residue(l_ref, r, dil, g * ATT_BLOCK, jnp.concatenate(lses, axis=1))

    out_blk = pl.BlockSpec((rows, PAIR), lambda i, p: (i, p))
    return pl.pallas_call(
        body, name=name, grid=(n_steps, N_PAIRS),
        in_specs=[pl.BlockSpec((rows, PAIR), lambda i, p: (i, p))]
        + _window_specs(dil, n_steps, lambda p: N_PAIRS + p) + _window_specs(dil, n_steps, lambda p: 2 * N_PAIRS + p),
        out_specs=(out_blk, out_blk),
        out_shape=(jax.ShapeDtypeStruct((s, ATT_WIDTH), F32),) * 2,
        compiler_params=_params(2))(qkv, qkv, qkv, qkv, qkv, qkv, qkv)


ATT_ROWS = 512


def _attn_combine(outs, lses, name):
    s = outs[0].shape[0]

    def body(o1, o2, o3, l1, l2, l3, att_ref, att32_ref, lse_ref):
        ls = [l1[...], l2[...], l3[...]]
        m = jnp.maximum(jnp.maximum(ls[0], ls[1]), ls[2])
        es = [jnp.exp(l - m) for l in ls]
        den = es[0] + es[1] + es[2]
        att = (es[0] * o1[...] + es[1] * o2[...] + es[2] * o3[...]) / den
        att_ref[...] = att.astype(BF16)
        att32_ref[...] = att
        lse_ref[...] = m + jnp.log(den)

    blk = pl.BlockSpec((ATT_ROWS, ATT_WIDTH), lambda i: (i, 0))
    return pl.pallas_call(
        body, name=name, grid=(s // ATT_ROWS,), in_specs=[blk] * 6, out_specs=(blk, blk, blk),
        out_shape=(jax.ShapeDtypeStruct((s, ATT_WIDTH), BF16), jax.ShapeDtypeStruct((s, ATT_WIDTH), F32),
                   jax.ShapeDtypeStruct((s, ATT_WIDTH), F32)),
        compiler_params=_params(1))(*outs, *lses)


def _attn_delta(datt, att, name):
    s = att.shape[0]

    def body(d_ref, a_ref, delta_ref):
        prod = d_ref[...] * a_ref[...]
        for h in range(N_HEADS):
            sl = slice(h * HEAD_DIM, (h + 1) * HEAD_DIM)
            delta_ref[:, sl] = jnp.broadcast_to(jnp.sum(prod[:, sl], axis=-1, keepdims=True), (ATT_ROWS, HEAD_DIM))

    blk = pl.BlockSpec((ATT_ROWS, ATT_WIDTH), lambda i: (i, 0))
    return pl.pallas_call(
        body, name=name, grid=(s // ATT_ROWS,), in_specs=[blk, blk], out_specs=blk,
        out_shape=jax.ShapeDtypeStruct((s, ATT_WIDTH), F32), compiler_params=_params(1))(datt, att)


def _attn_branch_bwd(qkv, do, lse, delta, prev, dil, name):
    s = qkv.shape[0]
    groups = ATT_GROUPS[dil]
    rows = ATT_BLOCK * dil * groups
    n_steps = s // rows
    length = s // dil
    has_prev = prev is not None
    tn = (((0,), (0,)), ((), ()))
    nt = (((1,), (1,)), ((), ()))

    def body(*refs):
        qs, ks, vs, dos, ls, des = (refs[3 * n:3 * n + 3] for n in range(6))
        rest = refs[18:]
        if has_prev:
            pq, pk, pv = rest[:3]
            rest = rest[3:]
        dq_ref, dk_ref, dv_ref = rest
        i, pair = pl.program_id(0), pl.program_id(1)
        for g, r in [(g, r) for g in range(groups) for r in range(dil)]:
            valid_q, distf_q = _band_masks(i * groups + g, length, dil, False)
            valid_k, distf_k = _band_masks(i * groups + g, length, dil, True)
            cur = lambda t: _residue(t[1], r, ATT_BLOCK, dil, g * ATT_BLOCK)
            q_cur, k_cur, v_cur, do_cur = (cur(t).astype(BF16) for t in (qs, ks, vs, dos))
            l_cur, de_cur = cur(ls), cur(des)
            q_win, k_win, v_win, do_win = (_residue_window(t, r, dil, g).astype(BF16) for t in (qs, ks, vs, dos))
            l_win, de_win = _residue_window(ls, r, dil, g), _residue_window(des, r, dil, g)
            dqs, dks, dvs = [], [], []
            for hh in range(2):
                sl = slice(hh * HEAD_DIM, (hh + 1) * HEAD_DIM)
                one = slice(hh * HEAD_DIM, hh * HEAD_DIM + 1)
                slope = jnp.where(pair == 0, SLOPES[hh], jnp.where(pair == 1, SLOPES[2 + hh], SLOPES[4 + hh]))
                sc = lax.dot_general(q_cur[:, sl], k_win[:, sl], nt, preferred_element_type=F32) * ATT_SCALE - slope * distf_q
                p = jnp.exp(jnp.where(valid_q, sc - l_cur[:, one], MASK_VALUE))
                dp = lax.dot_general(do_cur[:, sl], v_win[:, sl], nt, preferred_element_type=F32)
                ds = (p * (dp - de_cur[:, one]) * ATT_SCALE).astype(BF16)
                dqs.append(jnp.dot(ds, k_win[:, sl], preferred_element_type=F32))

                sc2 = lax.dot_general(q_win[:, sl], k_cur[:, sl], nt, preferred_element_type=F32) * ATT_SCALE - slope * distf_k
                p2 = jnp.exp(jnp.where(valid_k, sc2 - l_win[:, one], MASK_VALUE))
                dvs.append(lax.dot_general(p2.astype(BF16), do_win[:, sl], tn, preferred_element_type=F32))
                dp2 = lax.dot_general(do_win[:, sl], v_cur[:, sl], nt, preferred_element_type=F32)
                ds2 = (p2 * (dp2 - de_win[:, one]) * ATT_SCALE).astype(BF16)
                dks.append(lax.dot_general(ds2, q_win[:, sl], tn, preferred_element_type=F32))
            for parts, acc, out in ((dqs, pq if has_prev else None, dq_ref), (dks, pk if has_prev else None, dk_ref),
                                    (dvs, pv if has_prev else None, dv_ref)):
                val = jnp.concatenate(parts, axis=1)
                if has_prev:
                    val = val + _residue(acc, r, ATT_BLOCK, dil, g * ATT_BLOCK)
                _store_residue(out, r, dil, g * ATT_BLOCK, val)

    blk = pl.BlockSpec((rows, PAIR), lambda i, p: (i, p))
    in_specs = (_window_specs(dil, n_steps, lambda p: p) + _window_specs(dil, n_steps, lambda p: N_PAIRS + p)
                + _window_specs(dil, n_steps, lambda p: 2 * N_PAIRS + p) + _window_specs(dil, n_steps, lambda p: p) * 3)
    ins = [qkv] * 9 + [do] * 3 + [lse] * 3 + [delta] * 3
    if has_prev:
        in_specs += [blk] * 3
        ins += list(prev)
    return pl.pallas_call(
        body, name=name, grid=(n_steps, N_PAIRS), in_specs=in_specs, out_specs=(blk, blk, blk),
        out_shape=(jax.ShapeDtypeStruct((s, ATT_WIDTH), F32),) * 3,
        compiler_params=_params(2))(*ins)


TB = 2 * REC_CHUNK
REC_ROWS = 5 * REC_WIDTH


REC_LEVELS = 6


def _scan_pos(p, rev):
    p = p & (REC_CHUNK - 1)
    return (REC_CHUNK - 1 - p) if rev else p


def _split3(x):
    hi = x.astype(BF16)
    rest = x - hi.astype(F32)
    mid = rest.astype(BF16)
    return hi, mid, (rest - mid.astype(F32)).astype(BF16)


def _chunk_sums(x, rev, with_levels):
    row = lax.broadcasted_iota(jnp.int32, (TB, TB), 0)
    col = lax.broadcasted_iota(jnp.int32, (TB, TB), 1)
    same = (row < REC_CHUNK) == (col < REC_CHUNK)
    s_row, s_col = _scan_pos(row, rev), _scan_pos(col, rev)
    mats = [same & (s_row <= s_col)]
    if with_levels:
        for level in range(1, REC_LEVELS + 1):
            shift = REC_LEVELS + 1 - level
            boundary = ((s_col >> shift) << shift) + (REC_CHUNK >> level) - 1
            mats.append(same & (s_row <= boundary))
        mats.append(same)
    cat = jnp.concatenate([m.astype(BF16) for m in mats], axis=1)
    total = sum(jnp.dot(term, cat, preferred_element_type=F32) for term in _split3(x))
    return [total[:, n * TB:(n + 1) * TB] for n in range(len(mats))]


def _hg_prep(qraw, z, lb, rev):
    lane = lax.broadcasted_iota(jnp.int32, (REC_WIDTH, TB), 1)
    in_a = lane < REC_CHUNK
    scan = _scan_pos(lane, rev)
    sig, sigm = _sigmoid(z), _sigmoid(-z)
    f = lb + (1.0 - lb) * sig
    kk = (1.0 - lb) * sigm
    sums = _chunk_sums(jnp.log(jnp.maximum(f, F_TINY)), rev, True)
    b, bend = sums[0], sums[-1]
    q = qraw * _sigmoid(qraw)
    eq, ek = [], []
    for level in range(1, REC_LEVELS + 1):
        r = sums[level]
        e = jnp.exp(jnp.minimum(b - r, r - b))
        second = ((scan >> (REC_LEVELS - level)) & 1) == 1
        eq.append(jnp.where(second, e, 0.0))
        ek.append(jnp.where(second, 0.0, e))
    lanes_end = (0, REC_CHUNK) if rev else (REC_CHUNK - 1, TB - 1)
    end_a, end_b = (b[:, n:n + 1] for n in lanes_end)
    return dict(in_a=in_a, sig=sig, sigm=sigm, f=f, kk=kk, b=b, end_a=end_a, end_b=end_b,
                q=q, qh=q * jnp.exp(b), kh=kk * jnp.exp(bend - b), ekb=jnp.exp(bend - b), eq=eq, ek=ek)


def _level_masks(rev):
    row = lax.broadcasted_iota(jnp.int32, (TB, TB), 0)
    col = lax.broadcasted_iota(jnp.int32, (TB, TB), 1)
    same = (row < REC_CHUNK) == (col < REC_CHUNK)
    s_row, s_col = _scan_pos(row, rev), _scan_pos(col, rev)
    masks = [same & ((s_row >> (REC_LEVELS + 1 - level)) == (s_col >> (REC_LEVELS + 1 - level)))
             for level in range(1, REC_LEVELS + 1)]
    return masks, row == col


def _head_rows(x, h):
    return x[h * HEAD_DIM:(h + 1) * HEAD_DIM, :]


def _block_diag_mask():
    r = lax.broadcasted_iota(jnp.int32, (REC_WIDTH, REC_WIDTH), 0) // HEAD_DIM
    c = lax.broadcasted_iota(jnp.int32, (REC_WIDTH, REC_WIDTH), 1) // HEAD_DIM
    return (r == c).astype(F32)


def _heads(x):
    return x.reshape(N_HEADS, HEAD_DIM, TB)


def _hgrn_scan(projt, lb, rev, name):
    s = projt.shape[1]
    nblk = s // TB
    zrow = 2 if rev else 1
    tmap = (lambda i: nblk - 1 - i) if rev else (lambda i: i)
    tn = (((0,), (0,)), ((), ()))
    nt = (((1,), (1,)), ((), ()))

    def body(q_ref, z_ref, v_ref, lb_ref, o_ref, hs_ref, at_ref, h_ref):
        @pl.when(pl.program_id(0) == 0)
        def _():
            h_ref[...] = jnp.zeros_like(h_ref)

        v = v_ref[...]
        vb = v.astype(BF16)
        pr = _hg_prep(q_ref[...], z_ref[...], lb_ref[...], rev)
        q, kk = pr["q"], pr["kk"]
        masks, diag = _level_masks(rev)
        qt = [(q * e).astype(BF16) for e in pr["eq"]]
        kt = [(kk * e).astype(BF16) for e in pr["ek"]]
        own = jnp.sum(_heads(q * kk), axis=1)
        outs = []
        for h in range(N_HEADS):
            sc = jnp.where(diag, own[h:h + 1, :], 0.0)
            for level in range(REC_LEVELS):
                sc = sc + jnp.where(masks[level],
                                    lax.dot_general(_head_rows(kt[level], h), _head_rows(qt[level], h), tn,
                                                    preferred_element_type=F32), 0.0)
            a_bf = sc.astype(BF16)
            at_ref[h] = a_bf
            outs.append(jnp.dot(_head_rows(vb, h), a_bf, preferred_element_type=F32))
        o = jnp.concatenate(outs, axis=0)
        bd_mask = _block_diag_mask()
        order = ((1, ~pr["in_a"], pr["end_b"]), (0, pr["in_a"], pr["end_a"]))
        if not rev:
            order = order[::-1]
        for slot, msk, bend in order:
            h0 = h_ref[...]
            hs_ref[slot] = h0
            o = o + lax.dot_general(h0.astype(BF16), jnp.where(msk, pr["qh"], 0.0).astype(BF16), tn,
                                    preferred_element_type=F32)
            upd = lax.dot_general(jnp.where(msk, pr["kh"], 0.0).astype(BF16), vb, nt, preferred_element_type=F32)
            h_ref[...] = jnp.exp(bend) * h0 + upd * bd_mask
        o_ref[...] = o

    row_blk = lambda r: pl.BlockSpec((REC_WIDTH, TB), lambda i: (r, tmap(i)))
    return pl.pallas_call(
        body, name=name, grid=(nblk,),
        in_specs=[row_blk(0), row_blk(zrow), row_blk(3), pl.BlockSpec((REC_WIDTH, 1), lambda i: (0, 0))],
        out_specs=(pl.BlockSpec((REC_WIDTH, TB), lambda i: (0, tmap(i))),
                   pl.BlockSpec((2, REC_WIDTH, REC_WIDTH), lambda i: (tmap(i), 0, 0)),
                   pl.BlockSpec((None, N_HEADS, TB, TB), lambda i: (tmap(i), 0, 0, 0))),
        out_shape=(jax.ShapeDtypeStruct((REC_WIDTH, s), F32),
                   jax.ShapeDtypeStruct((s // REC_CHUNK, REC_WIDTH, REC_WIDTH), F32),
                   jax.ShapeDtypeStruct((nblk, N_HEADS, TB, TB), BF16)),
        scratch_shapes=[pltpu.VMEM((REC_WIDTH, REC_WIDTH), F32)],
        compiler_params=_params(1))(projt, projt, projt, lb)


def _hgrn_scan_bwd(projt, lb, dot, hs, at, prev, rev, name):
    s = projt.shape[1]
    nblk = s // TB
    zrow = 2 if rev else 1
    tmap = (lambda i: i) if rev else (lambda i: nblk - 1 - i)
    has_prev = prev is not None
    tn = (((0,), (0,)), ((), ()))
    nt = (((1,), (1,)), ((), ()))

    def body(*refs):
        q_ref, z_ref, v_ref, lb_ref, do_ref, hs_ref, at_ref = refs[:7]
        rest = refs[7:]
        if has_prev:
            pq_ref, pv_ref = rest[:2]
            rest = rest[2:]
        dq_ref, dz_ref, dv_ref, dlb_ref, dh_ref = rest

        @pl.when(pl.program_id(0) == 0)
        def _():
            dh_ref[...] = jnp.zeros_like(dh_ref)
            dlb_ref[...] = jnp.zeros_like(dlb_ref)

        qraw, v, do, lbv = q_ref[...], v_ref[...], do_ref[...], lb_ref[...]
        dob, vb = do.astype(BF16), v.astype(BF16)
        pr = _hg_prep(qraw, z_ref[...], lbv, rev)
        q, kk, b, in_a = pr["q"], pr["kk"], pr["b"], pr["in_a"]
        masks, diag = _level_masks(rev)
        qt = [(q * e).astype(BF16) for e in pr["eq"]]
        kt = [(kk * e).astype(BF16) for e in pr["ek"]]
        dq_h, dk_h, dv_h, db_h = [], [], [], []
        for h in range(N_HEADS):
            d_at = lax.dot_general(_head_rows(vb, h), _head_rows(dob, h), tn, preferred_element_type=F32)
            dv_h.append(lax.dot_general(_head_rows(dob, h), at_ref[h], nt, preferred_element_type=F32))
            d_own = jnp.sum(jnp.where(diag, d_at, 0.0), axis=0, keepdims=True)
            dq_acc, dk_acc = d_own * _head_rows(kk, h), d_own * _head_rows(q, h)
            db_acc = jnp.zeros((HEAD_DIM, TB), F32)
            for lv in range(REC_LEVELS):
                d_lv = jnp.where(masks[lv], d_at, 0.0).astype(BF16)
                q_lv, k_lv = _head_rows(qt[lv], h), _head_rows(kt[lv], h)
                dqt = jnp.dot(k_lv, d_lv, preferred_element_type=F32)
                dkt = lax.dot_general(q_lv, d_lv, nt, preferred_element_type=F32)
                dq_acc = dq_acc + _head_rows(pr["eq"][lv], h) * dqt
                dk_acc = dk_acc + _head_rows(pr["ek"][lv], h) * dkt
                db_acc = db_acc + q_lv.astype(F32) * dqt - k_lv.astype(F32) * dkt
            dq_h.append(dq_acc)
            dk_h.append(dk_acc)
            db_h.append(db_acc)
        dq_in, dk_in, dv, db_in = (jnp.concatenate(t, axis=0) for t in (dq_h, dk_h, dv_h, db_h))
        dq = dk = jnp.zeros((REC_WIDTH, TB), F32)

        zero = jnp.zeros((REC_WIDTH, TB), F32)
        bd_mask = _block_diag_mask()
        eb = jnp.exp(b)
        const = zero
        order = ((0, in_a, pr["end_a"]), (1, ~in_a, pr["end_b"]))
        if not rev:
            order = order[::-1]
        for slot, msk, bend in order:
            h0 = hs_ref[slot]
            dh1 = dh_ref[...]
            dh1b = dh1.astype(BF16)
            dq = dq + eb * jnp.dot(h0.astype(BF16), jnp.where(msk, do, 0.0).astype(BF16), preferred_element_type=F32)
            dv = dv + lax.dot_general(dh1b, jnp.where(msk, pr["kh"], 0.0).astype(BF16), tn, preferred_element_type=F32)
            dk_int = pr["ekb"] * jnp.dot(dh1b, jnp.where(msk, v, 0.0).astype(BF16), preferred_element_type=F32)
            dk = dk + dk_int
            ebend = jnp.exp(bend)
            c = (jnp.sum(kk * dk_int, axis=1, keepdims=True)
                 + ebend * jnp.sum(h0 * dh1, axis=1, keepdims=True))
            const = const + jnp.where(msk, c, 0.0)
            upd = lax.dot_general(jnp.where(msk, pr["qh"], 0.0).astype(BF16), dob, nt, preferred_element_type=F32)
            dh_ref[...] = ebend * dh1 + upd * bd_mask

        dg = _chunk_sums(db_in + q * dq - kk * dk, not rev, False)[0] + const
        dq, dk = dq + dq_in, dk + dk_in
        sig, sigm, f = pr["sig"], pr["sigm"], pr["f"]
        live = f > F_TINY
        inv_f = 1.0 / jnp.maximum(f, F_TINY)
        one_lb = 1.0 - lbv
        dz = sig * sigm * one_lb * (jnp.where(live, dg * inv_f, 0.0) - dk)
        dlb_ref[...] += jnp.sum(sigm * (jnp.where(live, dg * inv_f, 0.0) - dk), axis=1, keepdims=True)
        dqr = dq * _silu_grad(qraw)
        if has_prev:
            dqr = dqr + pq_ref[...]
            dv = dv + pv_ref[...]
        dq_ref[...] = dqr
        dz_ref[...] = dz
        dv_ref[...] = dv

    row_blk = lambda r: pl.BlockSpec((REC_WIDTH, TB), lambda i: (r, tmap(i)))
    blk = pl.BlockSpec((REC_WIDTH, TB), lambda i: (0, tmap(i)))
    col = pl.BlockSpec((REC_WIDTH, 1), lambda i: (0, 0))
    in_specs = [row_blk(0), row_blk(zrow), row_blk(3), col, blk,
                pl.BlockSpec((2, REC_WIDTH, REC_WIDTH), lambda i: (tmap(i), 0, 0)),
                pl.BlockSpec((None, N_HEADS, TB, TB), lambda i: (tmap(i), 0, 0, 0))]
    ins = [projt, projt, projt, lb, dot, hs, at]
    if has_prev:
        in_specs += [blk, blk]
        ins += list(prev)
    t_shape = jax.ShapeDtypeStruct((REC_WIDTH, s), F32)
    return pl.pallas_call(
        body, name=name, grid=(nblk,), in_specs=in_specs, out_specs=(blk, blk, blk, col),
        out_shape=(t_shape, t_shape, t_shape, jax.ShapeDtypeStruct((REC_WIDTH, 1), F32)),
        scratch_shapes=[pltpu.VMEM((REC_WIDTH, REC_WIDTH), F32)],
        compiler_params=_params(1))(*ins)


REC_OUT_COLS = 512


def _head_rms(o):
    o3 = o.reshape(N_HEADS, HEAD_DIM, o.shape[1])
    rstd = lax.rsqrt(jnp.mean(o3 * o3, axis=1, keepdims=True) + EPS)
    return o3 * rstd, rstd


def _hgrn_out(of, ob, projt, wn, name):
    s = of.shape[1]

    def body(of_ref, ob_ref, g_ref, wn_ref, o_ref):
        on, _ = _head_rms(of_ref[...] + ob_ref[...])
        g = g_ref[...]
        y = on.reshape(REC_WIDTH, REC_OUT_COLS) * wn_ref[...] * (g * _sigmoid(g))
        o_ref[...] = y.T.astype(BF16)

    blk = pl.BlockSpec((REC_WIDTH, REC_OUT_COLS), lambda i: (0, i))
    return pl.pallas_call(
        body, name=name, grid=(s // REC_OUT_COLS,),
        in_specs=[blk, blk, pl.BlockSpec((REC_WIDTH, REC_OUT_COLS), lambda i: (4, i)),
                  pl.BlockSpec((REC_WIDTH, 1), lambda i: (0, 0))],
        out_specs=pl.BlockSpec((REC_OUT_COLS, REC_WIDTH), lambda i: (i, 0)),
        out_shape=jax.ShapeDtypeStruct((s, REC_WIDTH), BF16), compiler_params=_params(1))(of, ob, projt, wn)


def _hgrn_out_bwd(drec, of, ob, projt, wn, name):
    s = of.shape[1]

    def body(d_ref, of_ref, ob_ref, g_ref, wn_ref, do_ref, dg_ref, dwn_ref):
        @pl.when(pl.program_id(0) == 0)
        def _():
            dwn_ref[...] = jnp.zeros_like(dwn_ref)

        dy = d_ref[...].T
        on3, rstd = _head_rms(of_ref[...] + ob_ref[...])
        on = on3.reshape(REC_WIDTH, REC_OUT_COLS)
        g, wnv = g_ref[...], wn_ref[...]
        dg_ref[...] = dy * on * wnv * _silu_grad(g)
        d_onw = dy * (g * _sigmoid(g))
        dwn_ref[...] += jnp.sum(d_onw * on, axis=1, keepdims=True)
        d_on3 = (d_onw * wnv).reshape(N_HEADS, HEAD_DIM, REC_OUT_COLS)
        do3 = rstd * (d_on3 - on3 * jnp.mean(d_on3 * on3, axis=1, keepdims=True))
        do_ref[...] = do3.reshape(REC_WIDTH, REC_OUT_COLS)

    blk = pl.BlockSpec((REC_WIDTH, REC_OUT_COLS), lambda i: (0, i))
    col = pl.BlockSpec((REC_WIDTH, 1), lambda i: (0, 0))
    t_shape = jax.ShapeDtypeStruct((REC_WIDTH, s), F32)
    return pl.pallas_call(
        body, name=name, grid=(s // REC_OUT_COLS,),
        in_specs=[pl.BlockSpec((REC_OUT_COLS, REC_WIDTH), lambda i: (i, 0)), blk, blk,
                  pl.BlockSpec((REC_WIDTH, REC_OUT_COLS), lambda i: (4, i)), col],
        out_specs=(blk, blk, col),
        out_shape=(t_shape, t_shape, jax.ShapeDtypeStruct((REC_WIDTH, 1), F32)),
        compiler_params=_params(1))(drec, of, ob, projt, wn)


def _lower_bounds(gamma, name):
    def body(g_ref, lb_ref, p_ref):
        g0, g1 = g_ref[0:1, :], g_ref[1:2, :]
        m = jnp.maximum(g0, g1)
        e0, e1 = jnp.exp(g0 - m), jnp.exp(g1 - m)
        p0, p1 = e0 / (e0 + e1), e1 / (e0 + e1)
        lb_ref[...] = (p0 + p1) - p0
        p_ref[0:1, :] = p0
        p_ref[1:2, :] = p1

    n = gamma.shape[1]
    return pl.pallas_call(body, name=name,
                          out_shape=(jax.ShapeDtypeStruct((1, n), F32), jax.ShapeDtypeStruct((2, n), F32)))(gamma)


def _lower_bounds_bwd(dlb1, p, name):
    def body(d_ref, p_ref, o_ref):
        p0, p1, d = p_ref[0:1, :], p_ref[1:2, :], d_ref[...]
        inner = p1 * d
        o_ref[0:1, :] = p0 * (0.0 - inner)
        o_ref[1:2, :] = p1 * (d - inner)

    return pl.pallas_call(body, name=name, out_shape=jax.ShapeDtypeStruct(p.shape, F32))(dlb1, p)


def _split_w_in(w_in):
    return dict(conv=w_in[:, G_CONV[0]:G_CONV[1]], qkv=w_in[:, G_QKV[0]:G_QKV[1]],
                rec_t=w_in[:, G_REC[0]:].T, nat=w_in[:, :G_REC[0]])


def _split_w_rest(w_out, w_up, w_down):
    return dict(out=w_out, out_a=w_out[:CONV_CH], out_b=w_out[CONV_CH:CONV_CH + ATT_WIDTH],
                out_c=w_out[CONV_CH + ATT_WIDTH:], up=w_up, down=w_down)


def _col(v):
    return v.reshape(-1, 1)


def _sequence_step(x, tgt, mods, lbs, small, w_in0, later_weights, final_w):
    saved = []
    xin = x
    big = [_split_w_in(w_in0), None]
    h1 = _resid_norm_mod(x, None, None, small[0]["norm1_w"], mods[0][1:2], mods[0][0:1], "norm1_first")
    for l in range(DEPTH):
        sm, w, md = small[l], big[l], mods[l]
        pa = _matmul(h1, w["conv"], "nn", F32, f"proj_conv")
        qkv = _matmul(h1, w["qkv"], "nn", F32, f"proj_qkv")
        projt = _matmul(w["rec_t"], h1, "nt", F32, f"proj_rec")
        a_out, cv = _conv_mixer(pa, sm["conv_a_w"], sm["conv_a_b"], sm["ln_a_w"], sm["ln_a_b"], f"conv_mixer")
        outs, lses = zip(*[_attn_branch(qkv, d, f"attn_d{d}") for d in DILATIONS])
        att, att32, lse = _attn_combine(outs, lses, f"attn_combine")
        lb_f, lb_b = _col(lbs[l][0]), _col(lbs[l][1])
        of, hsf, atf = _hgrn_scan(projt, lb_f, False, "hgrn_fwd")
        ob, hsb, atb = _hgrn_scan(projt, lb_b, True, "hgrn_rev")
        wn = _col(sm["rec_norm_w"])
        rec = _hgrn_out(of, ob, projt, wn, f"hgrn_out")
        mixed = jnp.concatenate([a_out, att, rec], axis=1)
        if l == 0:
            w_in1, w_out_all, w_up_all, w_down_all = later_weights(rec)
            big[0].update(_split_w_rest(w_out_all[0], w_up_all[0], w_down_all[0]))
            big[1] = dict(_split_w_in(w_in1), **_split_w_rest(w_out_all[1], w_up_all[1], w_down_all[1]))
        r1 = _matmul(mixed, w["out"], "nn", F32, f"out_proj")
        xmid, h2 = _resid_norm_mod(xin, r1, md[2:3], sm["norm2_w"], md[4:5], md[3:4], f"norm2")
        u = _matmul(h2, w["up"], "nn", BF16, f"ffn_up")
        act = _ffn_act(u, sm["conv_f_w"], f"ffn_act")
        r2 = _matmul(act, w["down"], "nn", F32, f"ffn_down")
        saved.append(dict(xin=xin, h1=h1, pa=pa, qkv=qkv, projt=projt, cv=cv, att32=att32, lse=lse, of=of, ob=ob,
                          hsf=hsf, hsb=hsb, atf=atf, atb=atb, lb_f=lb_f, lb_b=lb_b, wn=wn, mixed=mixed, r1=r1, xmid=xmid, h2=h2,
                          u=u, act=act, r2=r2))
        if l + 1 < DEPTH:
            nxt = small[l + 1]
            xin, h1 = _resid_norm_mod(xmid, r2, md[5:6], nxt["norm1_w"], mods[l + 1][1:2], mods[l + 1][0:1],
                                      "norm1")
    top = saved[-1]
    loss, dx, dr2, dg2, dfw = _final_loss(top["xmid"], top["r2"], mods[-1][5:6], final_w, tgt, "final_loss")

    grads = [None] * DEPTH
    for l in reversed(range(DEPTH)):
        sm, w, md, sv = small[l], big[l], mods[l], saved[l]
        dact = _matmul(dr2, w["down"], "nt", BF16, f"d_act")
        g_down = _matmul(dr2, sv["act"], "tn", F32, "dw_down").T
        dug, duv, dwg, dwv = _ffn_act_bwd(sv["u"], dact, sm["conv_f_w"], f"ffn_act_bwd")
        du = jnp.concatenate([dug, duv], axis=1)
        dh2 = _matmul(du, w["up"], "nt", F32, f"d_h2")
        g_up = _matmul(sv["h2"], du, "tn", F32, f"dw_up")
        dxmid, dr1, dsh2, dsc2, dnw2, dg1 = _norm_bwd(sv["xmid"], [dh2], dx, sm["norm2_w"], md[4:5], md[2:3], sv["r1"],
                                                     f"norm2_bwd")
        dmix_a = _matmul(dr1, w["out_a"], "nt", F32, f"d_mix_a")
        dmix_b = _matmul(dr1, w["out_b"], "nt", F32, f"d_mix_b")
        dmix_c = _matmul(dr1, w["out_c"], "nt", F32, f"d_mix_c")
        g_out = _matmul(sv["mixed"], dr1, "tn", F32, f"dw_out")
        dc, dlnw, dlnb, dcb = _conv_mixer_bwd_ln(sv["cv"], dmix_a, sm["ln_a_w"], sm["ln_a_b"], f"conv_mixer_bwd_ln")
        dpa, dcw = _conv_mixer_bwd_conv(sv["pa"], dc, sm["conv_a_w"], f"conv_mixer_bwd_conv")
        delta = _attn_delta(dmix_b, sv["att32"], "attn_delta")
        dqkv = None
        for d in DILATIONS:
            dqkv = _attn_branch_bwd(sv["qkv"], dmix_b, sv["lse"], delta, dqkv, d, f"attn_bwd_d{d}")
        dot, dgt, dwn = _hgrn_out_bwd(dmix_c, sv["of"], sv["ob"], sv["projt"], sv["wn"], f"hgrn_out_bwd")
        dqf, dzf, dvf, dlbf = _hgrn_scan_bwd(sv["projt"], sv["lb_f"], dot, sv["hsf"], sv["atf"], None, False,
                                             "hgrn_fwd_bwd")
        dqt, dzb, dvt, dlbb = _hgrn_scan_bwd(sv["projt"], sv["lb_b"], dot, sv["hsb"], sv["atb"], (dqf, dvf), True,
                                             "hgrn_rev_bwd")
        dprojt = jnp.concatenate([dqt, dzf, dzb, dvt, dgt], axis=0).astype(BF16)
        dnat = jnp.concatenate([dpa] + [t.astype(BF16) for t in dqkv], axis=1)
        dh1_a = _matmul(dnat, w["nat"], "nt", F32, f"d_h1_nat")
        dh1_b = _matmul(dprojt, w["rec_t"], "tn", F32, f"d_h1_rec")
        g_in_nat = _matmul(sv["h1"], dnat, "tn", F32, f"dw_in_nat")
        g_in_rec_t = _matmul(dprojt, sv["h1"], "nn", F32, f"dw_in_rec")
        g_in = jnp.concatenate([g_in_nat, g_in_rec_t.T], axis=1)
        if l > 0:
            below = saved[l - 1]
            dx, dr2, dsh1, dsc1, dnw1, dg2_below = _norm_bwd(sv["xin"], [dh1_a, dh1_b], dxmid, sm["norm1_w"], md[1:2],
                                                            mods[l - 1][5:6], below["r2"], f"norm1_bwd")
        else:
            dx, dsh1, dsc1, dnw1 = _norm_bwd(sv["xin"], [dh1_a, dh1_b], dxmid, sm["norm1_w"], md[1:2], None, None,
                                             f"norm1_bwd")
        grads[l] = dict(w_in=g_in, w_out=g_out, w_up=g_up, w_down=g_down,
                        mod=[dsh1, dsc1, dg1, dsh2, dsc2, dg2], norm1_w=dnw1, conv_a_w=dcw[:CONV_WIDTH], conv_a_b=dcb,
                        ln_a_w=dlnw, ln_a_b=dlnb, lb=jnp.concatenate([dlbf.reshape(1, -1), dlbb.reshape(1, -1)], axis=0),
                        rec_norm_w=dwn.reshape(1, -1), norm2_w=dnw2,
                        conv_f_w=jnp.concatenate([dwg[:3], dwv[:3]], axis=1))
        if l > 0:
            dg2 = dg2_below
    return loss[0, 0], dx, grads, dfw


def _adamw_math(w, g, m, v):
    m = ADAM_B1 * m + (1.0 - ADAM_B1) * g
    v = ADAM_B2 * v + (1.0 - ADAM_B2) * (g * g)
    m_hat = m / (1.0 - ADAM_B1 ** ADAM_STEP)
    v_hat = v / (1.0 - ADAM_B2 ** ADAM_STEP)
    delta = -ADAM_LR * (m_hat / (jnp.sqrt(v_hat) + ADAM_EPS) + ADAM_WD * w)
    return delta, m, v


def _row_tile(rows, cols, max_elems=384 * 1024):
    best = None
    for t in range(8, rows + 1, 8):
        if rows % t == 0 and t * cols <= max_elems:
            best = t
    return best or rows


def _adamw(w, g, m, v, name):
    nl, r, c = w.shape
    tr = _row_tile(r, c)

    def body(w_ref, g_ref, m_ref, v_ref, d_ref, m2_ref, v2_ref):
        d_ref[...], m2_ref[...], v2_ref[...] = _adamw_math(w_ref[...], g_ref[...], m_ref[...], v_ref[...])

    blk = pl.BlockSpec((None, tr, c), lambda l, i: (l, i, 0))
    shape = jax.ShapeDtypeStruct((nl, r, c), F32)
    return pl.pallas_call(body, name=name, grid=(nl, r // tr), in_specs=[blk] * 4, out_specs=(blk, blk, blk),
                          out_shape=(shape, shape, shape), compiler_params=_params(2))(w, g, m, v)


ADA_SHARD = N_MOD * D_MODEL // 4
ADA_COLS = 512
ADA_ROWS = 256
HIGHEST = lax.Precision.HIGHEST


def _ada_mod(c_all, w_ada, b_sh, name):
    def body(c_ref, w_ref, b_ref, o_ref):
        cv = c_ref[...]
        o_ref[...] = jnp.dot(cv * _sigmoid(cv), w_ref[...], precision=HIGHEST, preferred_element_type=F32) + b_ref[...]

    return pl.pallas_call(
        body, name=name, grid=(DEPTH, ADA_SHARD // ADA_COLS),
        in_specs=[pl.BlockSpec((8, D_MODEL), lambda l, j: (0, 0)),
                  pl.BlockSpec((None, D_MODEL, ADA_COLS), lambda l, j: (l, 0, j)),
                  pl.BlockSpec((None, 1, ADA_COLS), lambda l, j: (l, 0, j))],
        out_specs=pl.BlockSpec((None, 8, ADA_COLS), lambda l, j: (l, 0, j)),
        out_shape=jax.ShapeDtypeStruct((DEPTH, 8, ADA_SHARD), F32), compiler_params=_params(2))(c_all, w_ada, b_sh)


def _ada_update(c_all, dmod_sh, w, m, v, name):
    def body(c_ref, d_ref, w_ref, m_ref, v_ref, g_ref, dl_ref, m2_ref, v2_ref):
        cv = c_ref[...]
        g = lax.dot_general(cv * _sigmoid(cv), d_ref[...], (((0,), (0,)), ((), ())), precision=HIGHEST,
                            preferred_element_type=F32)
        g_ref[...] = g
        dl_ref[...], m2_ref[...], v2_ref[...] = _adamw_math(w_ref[...], g, m_ref[...], v_ref[...])

    blk = pl.BlockSpec((None, ADA_ROWS, ADA_SHARD), lambda l, i: (l, i, 0))
    shape = jax.ShapeDtypeStruct((DEPTH, D_MODEL, ADA_SHARD), F32)
    return pl.pallas_call(
        body, name=name, grid=(DEPTH, D_MODEL // ADA_ROWS),
        in_specs=[pl.BlockSpec((8, ADA_ROWS), lambda l, i: (0, i)),
                  pl.BlockSpec((None, 8, ADA_SHARD), lambda l, i: (l, 0, 0)), blk, blk, blk],
        out_specs=(blk,) * 4, out_shape=(shape,) * 4, compiler_params=_params(2))(c_all, dmod_sh, w, m, v)


def _sum_devices(packs, name):
    def body(p_ref, o_ref):
        acc = p_ref[0]
        for dev in range(1, 8):
            acc = acc + p_ref[dev]
        o_ref[...] = acc

    return pl.pallas_call(body, name=name, out_shape=jax.ShapeDtypeStruct(packs.shape[1:], F32))(packs)


def _mesh_pos():
    return lax.axis_index("x"), lax.axis_index("y"), lax.axis_index("c")


def _flip(v, bit):
    return 1 - v if bit else v


def _allgather_devices(x, name):
    m_per, n = x.shape

    def body(x_ref, out_ref, send_sems, recv_sems, local_sem):
        ix, iy, ic = _mesh_pos()
        me, sibling = (ix, iy, ic), (ix, iy, 1 - ic)
        chips = [(1 - ix, iy), (ix, 1 - iy), (1 - ix, 1 - iy)]

        def rows(px, py, pc):
            return out_ref.at[pl.ds((4 * px + 2 * py + pc) * m_per, m_per), :]

        def copy(k, block, to, src=None):
            return pltpu.make_async_remote_copy(
                src_ref=rows(*block) if src is None else src, dst_ref=rows(*block),
                send_sem=send_sems.at[k], recv_sem=recv_sems.at[k], device_id=to, device_id_type=MESH)

        mine = pltpu.make_async_copy(x_ref, rows(*me), local_sem)
        mine.start()
        first = [copy(0, me, sibling, src=x_ref)]
        first += [copy(1 + j, me, (*chip, ic), src=x_ref) for j, chip in enumerate(chips)]
        for cp in first:
            cp.start()
        passed = [copy(4 + j, (*chip, ic), sibling) for j, chip in enumerate(chips)]
        for j, chip in enumerate(chips):
            copy(1 + j, (*chip, ic), me).wait_recv()
            passed[j].start()
        copy(0, sibling, me).wait_recv()
        for j, chip in enumerate(chips):
            copy(4 + j, (*chip, 1 - ic), me).wait_recv()
        for cp in first + passed:
            cp.wait_send()
        mine.wait()

    return pl.pallas_call(
        body, name=name, out_shape=jax.ShapeDtypeStruct((8 * m_per, n), x.dtype),
        in_specs=[pl.BlockSpec(memory_space=pltpu.VMEM)], out_specs=pl.BlockSpec(memory_space=pltpu.VMEM),
        scratch_shapes=[pltpu.SemaphoreType.DMA((7,)), pltpu.SemaphoreType.DMA((7,)), pltpu.SemaphoreType.DMA],
    )(x)


def _gather_chips(shards, name):
    n = len(shards)

    def body(*refs):
        ins, outs = refs[:n], refs[n:2 * n]
        send_sems, recv_sems, local_sems = refs[2 * n:]
        ix, iy, ic = _mesh_pos()
        me = 2 * ix + iy
        local = [pltpu.make_async_copy(ins[a], outs[a].at[me], local_sems.at[a]) for a in range(n)]
        for cp in local:
            cp.start()
        remote = []
        for a in range(n):
            for k in (1, 2, 3):
                px, py = _flip(ix, k & 2), _flip(iy, k & 1)
                sems = dict(send_sem=send_sems.at[3 * a + k - 1], recv_sem=recv_sems.at[3 * a + k - 1],
                            device_id=(px, py, ic), device_id_type=MESH)
                out_cp = pltpu.make_async_remote_copy(src_ref=ins[a], dst_ref=outs[a].at[me], **sems)
                in_cp = pltpu.make_async_remote_copy(src_ref=ins[a], dst_ref=outs[a].at[2 * px + py], **sems)
                out_cp.start()
                remote.append((out_cp, in_cp))
        for out_cp, in_cp in remote:
            out_cp.wait_send()
            in_cp.wait_recv()
        for cp in local:
            cp.wait()

    return pl.pallas_call(
        body, name=name, in_specs=[ANY] * n, out_specs=tuple([ANY] * n),
        out_shape=tuple(jax.ShapeDtypeStruct((4,) + t.shape, t.dtype) for t in shards),
        scratch_shapes=[pltpu.SemaphoreType.DMA((3 * n,)), pltpu.SemaphoreType.DMA((3 * n,)),
                        pltpu.SemaphoreType.DMA((n,))],
    )(*shards)


HBM = pl.BlockSpec(memory_space=pltpu.HBM)
SEM = pl.BlockSpec(memory_space=pltpu.SEMAPHORE)
DATAFLOW = pltpu.SideEffectType.DATAFLOW_SIDE_EFFECTING


def _peer_chip(ix, iy, k):
    return _flip(ix, k & 2), _flip(iy, k & 1)


def _gather_chips_start(shards, name):
    n = len(shards)

    def body(*refs):
        src, land = refs[:n], refs[n:2 * n]
        send_sems, recv_sems = refs[2 * n], refs[2 * n + 1]
        token = refs[-1]
        ix, iy, ic = _mesh_pos()
        me = 2 * ix + iy
        for a in range(n):
            for k in (1, 2, 3):
                px, py = _peer_chip(ix, iy, k)
                pltpu.make_async_remote_copy(
                    src_ref=src[a], dst_ref=land[a].at[me], send_sem=send_sems.at[3 * a + k - 1],
                    recv_sem=recv_sems.at[3 * a + k - 1], device_id=(px, py, ic), device_id_type=MESH).start()
        token[...] = jnp.zeros_like(token)

    hbm = lambda shape, dtype: pltpu.HBM(shape, dtype)
    operands = ([pltpu.with_memory_space_constraint(t, pltpu.HBM) for t in shards]
                + [pltpu.with_memory_space_constraint(lax.empty((4,) + t.shape, t.dtype), pltpu.HBM) for t in shards])
    return pl.pallas_call(
        body, name=name,
        out_shape=(pltpu.SemaphoreType.DMA((3 * n,)), pltpu.SemaphoreType.DMA((3 * n,)),
                   *[hbm(t.shape, t.dtype) for t in shards], *[hbm((4,) + t.shape, t.dtype) for t in shards],
                   jax.ShapeDtypeStruct((8, LANES), F32)),
        in_specs=(HBM,) * (2 * n),
        out_specs=(SEM, SEM) + (HBM,) * (2 * n) + (pl.BlockSpec(memory_space=pltpu.VMEM),),
        input_output_aliases={a: 2 + a for a in range(2 * n)},
        compiler_params=pltpu.CompilerParams(has_side_effects=DATAFLOW),
    )(*operands)


def _gather_chips_wait(started, after, name):
    send_sems, recv_sems = started[0], started[1]
    thru = started[2:-1]
    n = len(thru) // 2

    def body(*refs):
        src, land = refs[:n], refs[n:2 * n]
        send_sems, recv_sems = refs[2 * n], refs[2 * n + 1]
        ix, iy, ic = _mesh_pos()
        for a in range(n):
            for k in (1, 2, 3):
                px, py = _peer_chip(ix, iy, k)
                cp = pltpu.make_async_remote_copy(
                    src_ref=src[a], dst_ref=land[a].at[2 * px + py], send_sem=send_sems.at[3 * a + k - 1],
                    recv_sem=recv_sems.at[3 * a + k - 1], device_id=(px, py, ic), device_id_type=MESH)
                cp.wait_send()
                cp.wait_recv()

    outs = pl.pallas_call(
        body, name=name,
        out_shape=tuple(pltpu.HBM(t.shape, t.dtype) for t in thru),
        in_specs=(HBM,) * (2 * n) + (SEM, SEM, ANY), out_specs=(HBM,) * (2 * n),
        input_output_aliases={a: a for a in range(2 * n)},
        compiler_params=pltpu.CompilerParams(has_side_effects=DATAFLOW),
    )(*thru, send_sems, recv_sems, after)
    return outs[:n], outs[n:]


BIG_KINDS = (("w_in", "col", D_MODEL, IN_COLS), ("w_out", "row", D_MODEL, D_MODEL),
             ("w_up", "col", D_MODEL, 2 * D_FF), ("w_down", "row", D_FF, D_MODEL))


def _piece_shape(how, r, c):
    return (r // 2, c // 4) if how == "col" else (r // 8, c)


def _aligned(start, multiple):
    return start if isinstance(start, int) else pl.multiple_of(start, multiple)


def _piece(ref, how, r, c, chip, half):
    if how == "col":
        return ref.at[pl.ds(_aligned(half * (r // 2), 8), r // 2), pl.ds(_aligned(chip * (c // 4), LANES), c // 4)]
    n = r // 4
    return ref.at[pl.ds(_aligned(chip * n + half * (n // 2), 8), n // 2), :]


def _rs_pair_exchange(grads, name):
    nk = len(BIG_KINDS)
    flat = [grads[ki][l] for ki in range(nk) for l in range(DEPTH)]
    per = DEPTH * 4

    def body(*refs):
        g, land = refs[:nk * DEPTH], refs[nk * DEPTH:nk * DEPTH + nk]
        send_sems, recv_sems = refs[nk * DEPTH + nk:]
        ix, iy, ic = _mesh_pos()
        sibling = (ix, iy, 1 - ic)
        copies = []
        for ki, (_, how, r, c) in enumerate(BIG_KINDS):
            for l in range(DEPTH):
                for j in range(4):
                    sem = ki * per + l * 4 + j
                    rem = pltpu.make_async_remote_copy(
                        src_ref=_piece(g[ki * DEPTH + l], how, r, c, j, 1 - ic), dst_ref=land[ki].at[l, j],
                        send_sem=send_sems.at[sem], recv_sem=recv_sems.at[sem], device_id=sibling, device_id_type=MESH)
                    rem.start()
                    copies.append(rem)
        for rem in copies:
            rem.wait_send()
            rem.wait_recv()

    shapes = [jax.ShapeDtypeStruct((DEPTH, 4) + _piece_shape(how, r, c), F32) for _, how, r, c in BIG_KINDS]
    return pl.pallas_call(
        body, name=name, in_specs=[ANY] * len(flat), out_specs=tuple([ANY] * nk), out_shape=tuple(shapes),
        scratch_shapes=[pltpu.SemaphoreType.DMA((nk * per,))] * 2,
    )(*flat)


def _pair_sum(g, theirs, layer, how, core, name):
    r, c = g.shape
    pr, pc = _piece_shape(how, r, c)
    if how == "col":
        mine_spec = pl.BlockSpec((pr, pc), lambda j, core_ref: (core_ref[0], j))
    else:
        mine_spec = pl.BlockSpec((pr, pc), lambda j, core_ref: (2 * j + core_ref[0], 0))

    def body(core_ref, g_ref, t_ref, o_ref, ob_ref):
        total = g_ref[...] + t_ref[...]
        o_ref[...] = total
        ob_ref[...] = total.astype(BF16)

    out_blk = pl.BlockSpec((None, pr, pc), lambda j, core_ref: (j, 0, 0))
    return pl.pallas_call(
        body, name=name,
        grid_spec=pltpu.PrefetchScalarGridSpec(
            num_scalar_prefetch=1, grid=(4,),
            in_specs=[mine_spec, pl.BlockSpec((None, None, pr, pc), lambda j, core_ref: (layer, j, 0, 0))],
            out_specs=(out_blk, out_blk)),
        out_shape=(jax.ShapeDtypeStruct((4, pr, pc), F32), jax.ShapeDtypeStruct((4, pr, pc), BF16)),
        compiler_params=_params(1))(core, g, theirs)


def _rs_chip_exchange(pair_sums, name):
    nk = len(pair_sums)
    flat = [pair_sums[ki][l] for ki in range(nk) for l in range(DEPTH)]

    def body(*refs):
        src, dst = refs[:nk * DEPTH], refs[nk * DEPTH:nk * DEPTH + nk]
        send_sems, recv_sems = refs[nk * DEPTH + nk:]
        ix, iy, ic = _mesh_pos()
        copies = []
        for ki in range(nk):
            for l in range(DEPTH):
                for k in (1, 2, 3):
                    px, py = _flip(ix, k & 2), _flip(iy, k & 1)
                    sem = (ki * DEPTH + l) * 3 + k - 1
                    rem = pltpu.make_async_remote_copy(
                        src_ref=src[ki * DEPTH + l].at[2 * px + py], dst_ref=dst[ki].at[l, k - 1],
                        send_sem=send_sems.at[sem], recv_sem=recv_sems.at[sem], device_id=(px, py, ic), device_id_type=MESH)
                    rem.start()
                    copies.append(rem)
        for rem in copies:
            rem.wait_send()
            rem.wait_recv()

    return pl.pallas_call(
        body, name=name, in_specs=[ANY] * len(flat), out_specs=tuple([ANY] * nk),
        out_shape=tuple(jax.ShapeDtypeStruct((DEPTH, 3) + pair_sums[ki][0].shape[1:], pair_sums[ki][0].dtype)
                        for ki in range(nk)),
        scratch_shapes=[pltpu.SemaphoreType.DMA((nk * DEPTH * 3,))] * 2,
    )(*flat)


def _chip_sum(own, others, layer, chip, name):
    _, pr, pc = own.shape

    def body(chip_ref, own_ref, s1, s2, s3, o_ref):
        o_ref[...] = ((own_ref[...] + s1[...].astype(F32)) + s2[...].astype(F32)) + s3[...].astype(F32)

    slot = lambda k: pl.BlockSpec((None, None, pr, pc), lambda i, chip_ref: (layer, k, 0, 0))
    return pl.pallas_call(
        body, name=name,
        grid_spec=pltpu.PrefetchScalarGridSpec(
            num_scalar_prefetch=1, grid=(1,),
            in_specs=[pl.BlockSpec((None, pr, pc), lambda i, chip_ref: (chip_ref[0], 0, 0)), slot(0), slot(1), slot(2)],
            out_specs=pl.BlockSpec((pr, pc), lambda i, chip_ref: (0, 0))),
        out_shape=jax.ShapeDtypeStruct((pr, pc), F32), compiler_params=_params(1))(chip, own, others, others, others)


def _rs_pair_share(halves, name):
    nk = len(halves)
    flat = [halves[ki][l] for ki in range(nk) for l in range(DEPTH)]

    def body(*refs):
        src, dst = refs[:nk * DEPTH], refs[nk * DEPTH:nk * DEPTH + nk]
        send_sems, recv_sems = refs[nk * DEPTH + nk:]
        ix, iy, ic = _mesh_pos()
        copies = []
        for ki in range(nk):
            for l in range(DEPTH):
                sem = ki * DEPTH + l
                rem = pltpu.make_async_remote_copy(
                    src_ref=src[sem], dst_ref=dst[ki].at[l], send_sem=send_sems.at[sem], recv_sem=recv_sems.at[sem],
                    device_id=(ix, iy, 1 - ic), device_id_type=MESH)
                rem.start()
                copies.append(rem)
        for rem in copies:
            rem.wait_send()
            rem.wait_recv()

    return pl.pallas_call(
        body, name=name, in_specs=[ANY] * len(flat), out_specs=tuple([ANY] * nk),
        out_shape=tuple(jax.ShapeDtypeStruct((DEPTH,) + halves[ki][0].shape, F32) for ki in range(nk)),
        scratch_shapes=[pltpu.SemaphoreType.DMA((nk * DEPTH,))] * 2,
    )(*flat)


def _adamw_halves(w, mine, theirs, m, v, core, name):
    nl, pr, pc = theirs.shape
    shape = w.shape
    view = lambda t: t.reshape(nl, 2, pr, pc)
    tr = _row_tile(pr, pc, 256 * 1024)

    def body(core_ref, w_ref, a0_ref, a1_ref, t_ref, m_ref, v_ref, g_ref, d_ref, m2_ref, v2_ref):
        own = jnp.where(pl.program_id(0) == 0, a0_ref[...], a1_ref[...])
        g = jnp.where(pl.program_id(1) == core_ref[0], own, t_ref[...])
        g_ref[...] = g
        d_ref[...], m2_ref[...], v2_ref[...] = _adamw_math(w_ref[...], g, m_ref[...], v_ref[...])

    blk = pl.BlockSpec((None, None, tr, pc), lambda l, h, i, core_ref: (l, h, i, 0))
    own_blk = pl.BlockSpec((tr, pc), lambda l, h, i, core_ref: (i, 0))
    out = jax.ShapeDtypeStruct((nl, 2, pr, pc), F32)
    outs = pl.pallas_call(
        body, name=name,
        grid_spec=pltpu.PrefetchScalarGridSpec(
            num_scalar_prefetch=1, grid=(nl, 2, pr // tr),
            in_specs=[blk, own_blk, own_blk, pl.BlockSpec((None, tr, pc), lambda l, h, i, core_ref: (l, i, 0)), blk, blk],
            out_specs=(blk,) * 4),
        out_shape=(out,) * 4, compiler_params=_params(3),
    )(core, view(w), mine[0], mine[1], theirs, view(m), view(v))
    return tuple(t.reshape(shape) for t in outs)


def _reduce_scatter_big(grads, core, chip):
    theirs = _rs_pair_exchange(grads, "rs_pair_exchange")
    pair_sums = [[_pair_sum(grads[ki][l], theirs[ki], l, how, core, f"rs_pair_sum_{kind}") for l in range(DEPTH)]
                 for ki, (kind, how, _, _) in enumerate(BIG_KINDS)]
    slots = _rs_chip_exchange([[both[1] for both in row] for row in pair_sums], "rs_chip_exchange")
    halves = [[_chip_sum(pair_sums[ki][l][0], slots[ki], l, chip, f"rs_chip_sum_{kind}") for l in range(DEPTH)]
              for ki, (kind, _, _, _) in enumerate(BIG_KINDS)]
    other = _rs_pair_share(halves, "rs_pair_share")
    return list(zip(halves, other))


WEIGHT_NAMES = ("w_ada", "b_ada", "norm1_w", "w_in", "conv_a_w", "conv_a_b", "ln_a_w", "ln_a_b", "lb_gamma",
                "rec_norm_w", "w_out", "norm2_w", "w_up", "conv_f_w", "w_down", "final_norm_w")
SMALL_PARAMS = (("b_ada", (DEPTH, N_MOD * D_MODEL), None), ("norm1_w", (DEPTH, D_MODEL), None),
                ("conv_a_w", (DEPTH, CONV_WIDTH, CONV_CH), 2), ("conv_a_b", (DEPTH, CONV_CH), None),
                ("ln_a_w", (DEPTH, CONV_CH), None), ("ln_a_b", (DEPTH, CONV_CH), None),
                ("lb_gamma", (DEPTH, 2, REC_WIDTH), 2), ("rec_norm_w", (DEPTH, REC_WIDTH), None),
                ("norm2_w", (DEPTH, D_MODEL), None), ("conv_f_w", (DEPTH, 3, 2 * D_FF), 2),
                ("final_norm_w", (D_MODEL,), None))


def _pack_rows(parts):
    flat = jnp.concatenate([p.reshape(-1) for p in parts])
    total = flat.shape[0]
    padded = -(-total // (8 * LANES)) * (8 * LANES)
    return jnp.pad(flat, (0, padded - total)).reshape(padded // LANES, LANES)


def _unpack(flat, shapes):
    out, off = [], 0
    for shp in shapes:
        size = int(np.prod(shp))
        out.append(flat[off:off + size].reshape(shp))
        off += size
    return out


def _unstack_chips(t, axis):
    return jnp.concatenate([t[j] for j in range(4)], axis=axis)


def kernel(x, c, w_ada, b_ada, norm1_w, w_in, conv_a_w, conv_a_b, ln_a_w, ln_a_b, lb_gamma, rec_norm_w, w_out, norm2_w, w_up, conv_f_w, w_down, final_norm_w, loss_target, m_w_ada, m_b_ada, m_norm1_w, m_w_in, m_conv_a_w, m_conv_a_b, m_ln_a_w, m_ln_a_b, m_lb_gamma, m_rec_norm_w, m_w_out, m_norm2_w, m_w_up, m_conv_f_w, m_w_down, m_final_norm_w, v_w_ada, v_b_ada, v_norm1_w, v_w_in, v_conv_a_w, v_conv_a_b, v_ln_a_w, v_ln_a_b, v_lb_gamma, v_rec_norm_w, v_w_out, v_norm2_w, v_w_up, v_conv_f_w, v_w_down, v_final_norm_w):
    params = dict(zip(WEIGHT_NAMES, (w_ada, b_ada, norm1_w, w_in, conv_a_w, conv_a_b, ln_a_w, ln_a_b, lb_gamma,
                                     rec_norm_w, w_out, norm2_w, w_up, conv_f_w, w_down, final_norm_w)))
    mom1 = dict(zip(WEIGHT_NAMES, (m_w_ada, m_b_ada, m_norm1_w, m_w_in, m_conv_a_w, m_conv_a_b, m_ln_a_w, m_ln_a_b,
                                   m_lb_gamma, m_rec_norm_w, m_w_out, m_norm2_w, m_w_up, m_conv_f_w, m_w_down,
                                   m_final_norm_w)))
    mom2 = dict(zip(WEIGHT_NAMES, (v_w_ada, v_b_ada, v_norm1_w, v_w_in, v_conv_a_w, v_conv_a_b, v_ln_a_w, v_ln_a_b,
                                   v_lb_gamma, v_rec_norm_w, v_w_out, v_norm2_w, v_w_up, v_conv_f_w, v_w_down,
                                   v_final_norm_w)))
    ix, iy, ic = _mesh_pos()
    chip = 2 * ix + iy
    dev = 2 * chip + ic

    c_all = _allgather_devices(c.reshape(8, LANES), "gather_cond").reshape(8, D_MODEL)
    b_sh = lax.dynamic_slice_in_dim(b_ada, chip * ADA_SHARD, ADA_SHARD, axis=1)
    mod_sh = _ada_mod(c_all, w_ada, b_sh.reshape(DEPTH, 1, ADA_SHARD), "ada_mod")
    w_in_b, w_out_b, w_up_b, w_down_b = (t.astype(BF16) for t in (w_in, w_out, w_up, w_down))
    first = _gather_chips([mod_sh, conv_a_w, conv_f_w, lb_gamma, w_in_b[0]], "gather_first")
    later = [w_in_b[1], w_out_b, w_up_b, w_down_b]
    started = _gather_chips_start(later, "gather_rest_start")
    mod_mine = lax.dynamic_index_in_dim(first[0], dev, axis=2, keepdims=False) + started[-1][0, 0]
    mods = [jnp.concatenate([mod_mine[j, l] for j in range(4)]).reshape(N_MOD, D_MODEL) for l in range(DEPTH)]
    conv_a_w_f, conv_f_w_f, gamma_f = (_unstack_chips(first[k], 2) for k in (1, 2, 3))
    w_in0 = _unstack_chips(first[4], 1)

    def later_weights(after):
        own, lands = _gather_chips_wait(started, after, "gather_rest_wait")
        full = [lax.dynamic_update_index_in_dim(land, mine, chip, 0) for land, mine in zip(lands, own)]
        return (_unstack_chips(full[0], 1), _unstack_chips(full[1], 1), _unstack_chips(full[2], 2),
                _unstack_chips(full[3], 1))

    lb1, p_soft = _lower_bounds(gamma_f.reshape(DEPTH, 2 * REC_WIDTH), "lower_bounds")
    lbs = [jnp.zeros((2, REC_WIDTH), F32), lb1.reshape(2, REC_WIDTH)]
    small = []
    for l in range(DEPTH):
        small.append(dict(norm1_w=norm1_w[l][None], conv_a_w=conv_a_w_f[l], conv_a_b=conv_a_b[l][None],
                          ln_a_w=ln_a_w[l][None], ln_a_b=ln_a_b[l][None], rec_norm_w=rec_norm_w[l],
                          norm2_w=norm2_w[l][None], conv_f_w=conv_f_w_f[l]))

    loss, dx, grads, dfw = _sequence_step(x[0], loss_target[0], mods, lbs, small, w_in0, later_weights,
                                          final_norm_w[None])
    loss = lax.psum(loss, ("x", "y", "c"))

    dgamma = _lower_bounds_bwd(grads[1]["lb"].reshape(1, 2 * REC_WIDTH), p_soft, "lower_bounds_bwd")
    dmod = [jnp.concatenate(grads[l]["mod"], axis=1) for l in range(DEPTH)]
    stack = lambda key: jnp.stack([grads[l][key] for l in range(DEPTH)])
    local_small = dict(b_ada=jnp.concatenate(dmod, axis=0), norm1_w=stack("norm1_w"), conv_a_w=stack("conv_a_w"),
                       conv_a_b=stack("conv_a_b"), ln_a_w=stack("ln_a_w"), ln_a_b=stack("ln_a_b"), lb_gamma=dgamma,
                       rec_norm_w=stack("rec_norm_w"), norm2_w=stack("norm2_w"), conv_f_w=stack("conv_f_w"),
                       final_norm_w=dfw)
    pack = _pack_rows([local_small[name] for name, _, _ in SMALL_PARAMS])
    rows = pack.shape[0]
    packs = _allgather_devices(pack, "gather_small_grads").reshape(8, rows, LANES)
    summed = _sum_devices(packs, "sum_small_grads").reshape(-1)
    small_grads = dict(zip([n for n, _, _ in SMALL_PARAMS], _unpack(summed, [shp for _, shp, _ in SMALL_PARAMS])))

    dmod_all = packs.reshape(8, rows * LANES)[:, :DEPTH * N_MOD * D_MODEL].reshape(8, DEPTH, N_MOD * D_MODEL)
    dmod_sh = lax.dynamic_slice_in_dim(dmod_all, chip * ADA_SHARD, ADA_SHARD, axis=2).transpose(1, 0, 2)
    g_ada, d_ada, m_ada, v_ada = _ada_update(c_all, dmod_sh, w_ada, m_w_ada, v_w_ada, "ada_update")

    for name, shp, axis in SMALL_PARAMS:
        if axis is not None:
            width = shp[axis] // 4
            small_grads[name] = lax.dynamic_slice_in_dim(small_grads[name], chip * width, width, axis=axis)
    names = [n for n, _, _ in SMALL_PARAMS]
    packed = [_pack_rows([src[n] for n in names])[None] for src in (params, small_grads, mom1, mom2)]
    small_out = _adamw(*packed, "adamw_small")
    shapes = [params[n].shape for n in names]
    small_delta, small_m, small_v = (dict(zip(names, _unpack(t.reshape(-1), shapes))) for t in small_out)

    core_id, chip_id = ic.astype(jnp.int32).reshape(1), chip.astype(jnp.int32).reshape(1)
    summed_big = _reduce_scatter_big([[grads[l][name] for l in range(DEPTH)] for name, _, _, _ in BIG_KINDS],
                                     core_id, chip_id)
    grad, delta, new_m, new_v = dict(small_grads), small_delta, small_m, small_v
    grad["w_ada"], delta["w_ada"], new_m["w_ada"], new_v["w_ada"] = g_ada, d_ada, m_ada, v_ada
    for (name, _, _, _), (mine, theirs) in zip(BIG_KINDS, summed_big):
        grad[name], delta[name], new_m[name], new_v[name] = _adamw_halves(
            params[name], mine, theirs, mom1[name], mom2[name], core_id, f"adamw_{name}")

    return (loss, dx[None], *[grad[n] for n in WEIGHT_NAMES], *[delta[n] for n in WEIGHT_NAMES],
            *[new_m[n] for n in WEIGHT_NAMES], *[new_v[n] for n in WEIGHT_NAMES])
```

```python
import numpy as np
import jax
import jax.numpy as jnp
from jax import lax
from jax.experimental import pallas as pl
from jax.experimental.pallas import tpu as pltpu

F32 = jnp.float32
BF16 = jnp.bfloat16

D_MODEL = 1024
DEPTH = 2
HEAD_DIM = 64
CONV_CH = 256
CONV_WIDTH = 31
ATT_WIDTH = 384
N_HEADS = 6
DILATIONS = (1, 4, 16)
ATT_HALF = 64
ATT_BLOCK = 128
ALIBI_MAX_EXP = 8.0
MASK_VALUE = -1e30
REC_WIDTH = 384
REC_CHUNK = 64
F_TINY = 1e-30
D_FF = 2816
N_MOD = 6
EPS = 1e-6
G_CONV = (0, 512)
G_QKV = (512, 1664)
G_REC = (1664, 3584)
IN_COLS = 3584

ADAM_LR = 0.001
ADAM_B1 = 0.9
ADAM_B2 = 0.999
ADAM_EPS = 1e-08
ADAM_WD = 0.01
ADAM_STEP = 10

VMEM_LIMIT_BYTES = 56 * 1024 * 1024
LANES = 128
MESH = pl.DeviceIdType.MESH
ANY = pl.BlockSpec(memory_space=pl.ANY)


def _params(n_axes):
    return pltpu.CompilerParams(dimension_semantics=("arbitrary",) * n_axes,
                                vmem_limit_bytes=VMEM_LIMIT_BYTES)


def _tile(n, target):
    best = None
    for t in range(LANES, min(n, target) + 1, LANES):
        if n % t == 0:
            best = t
    return best or n


def _sigmoid(x):
    return jax.nn.sigmoid(x)


def _silu_grad(x):
    s = _sigmoid(x)
    return s * (1.0 + x * (1.0 - s))


MM_ACC_ELEMS = 1536 * 1024


def _matmul(a, b, mode, out_dtype, name, tm=1024, tn=1792, tk=1792):
    if mode == "nn":
        (m, k), (k2, n) = a.shape, b.shape
    elif mode == "nt":
        (m, k), (n, k2) = a.shape, b.shape
    else:
        (k, m), (k2, n) = a.shape, b.shape
    assert k == k2, (a.shape, b.shape, mode)
    tn, tk = _tile(n, tn), _tile(k, tk)
    tm = _tile(m, min(tm, MM_ACC_ELEMS // tn))
    nk = k // tk
    a_spec = (pl.BlockSpec((tk, tm), lambda i, j, kk: (kk, i)) if mode == "tn"
              else pl.BlockSpec((tm, tk), lambda i, j, kk: (i, kk)))
    b_spec = (pl.BlockSpec((tn, tk), lambda i, j, kk: (j, kk)) if mode == "nt"
              else pl.BlockSpec((tk, tn), lambda i, j, kk: (kk, j)))
    dims = {"nn": (((1,), (0,)), ((), ())), "nt": (((1,), (1,)), ((), ())),
            "tn": (((0,), (0,)), ((), ()))}[mode]

    def body(a_ref, b_ref, o_ref, *scratch):
        part = lax.dot_general(a_ref[...].astype(BF16), b_ref[...].astype(BF16), dims, preferred_element_type=F32)
        if nk == 1:
            o_ref[...] = part.astype(out_dtype)
            return
        acc_ref, = scratch
        kk = pl.program_id(2)

        @pl.when(kk == 0)
        def _():
            acc_ref[...] = part

        @pl.when(kk > 0)
        def _():
            acc_ref[...] += part

        @pl.when(kk == nk - 1)
        def _():
            o_ref[...] = acc_ref[...].astype(out_dtype)

    return pl.pallas_call(
        body, name=name, grid=(m // tm, n // tn, nk),
        in_specs=[a_spec, b_spec],
        out_specs=pl.BlockSpec((tm, tn), lambda i, j, kk: (i, j)),
        out_shape=jax.ShapeDtypeStruct((m, n), out_dtype),
        scratch_shapes=[pltpu.VMEM((tm, tn), F32)] if nk > 1 else [],
        compiler_params=pltpu.CompilerParams(dimension_semantics=("parallel", "parallel", "arbitrary"),
                                             vmem_limit_bytes=VMEM_LIMIT_BYTES),
    )(a, b)


NORM_ROWS = 256


def _row_spec(width, rows=NORM_ROWS):
    return pl.BlockSpec((rows, width), lambda i: (i, 0))


def _vec_spec(width):
    return pl.BlockSpec((1, width), lambda i: (0, 0))


def _resid_norm_mod(x, r, g, nw, sc, sh, name):
    s, d = x.shape
    has_r = r is not None

    def body(*refs):
        if has_r:
            x_ref, r_ref, g_ref, nw_ref, sc_ref, sh_ref, xn_ref, h_ref = refs
            xn = x_ref[...] + g_ref[...] * r_ref[...]
            xn_ref[...] = xn
        else:
            x_ref, nw_ref, sc_ref, sh_ref, h_ref = refs
            xn = x_ref[...]
        rstd = lax.rsqrt(jnp.mean(xn * xn, axis=-1, keepdims=True) + EPS)
        y = xn * rstd * nw_ref[...]
        h_ref[...] = (y * (1.0 + sc_ref[...]) + sh_ref[...]).astype(BF16)

    if has_r:
        ins, in_specs = (x, r, g, nw, sc, sh), [_row_spec(d), _row_spec(d)] + [_vec_spec(d)] * 4
        out_shape = (jax.ShapeDtypeStruct((s, d), F32), jax.ShapeDtypeStruct((s, d), BF16))
        out_specs = (_row_spec(d), _row_spec(d))
    else:
        ins, in_specs = (x, nw, sc, sh), [_row_spec(d)] + [_vec_spec(d)] * 3
        out_shape = jax.ShapeDtypeStruct((s, d), BF16)
        out_specs = _row_spec(d)
    return pl.pallas_call(body, name=name, grid=(s // NORM_ROWS,), in_specs=in_specs, out_specs=out_specs,
                          out_shape=out_shape, compiler_params=_params(1))(*ins)


def _final_loss(x, r, g, fw, tgt, name):
    s, d = x.shape

    def body(x_ref, r_ref, g_ref, fw_ref, t_ref, loss_ref, dx_ref, dr_ref, dg_ref, dfw_ref):
        @pl.when(pl.program_id(0) == 0)
        def _():
            loss_ref[...] = jnp.zeros_like(loss_ref)
            dg_ref[...] = jnp.zeros_like(dg_ref)
            dfw_ref[...] = jnp.zeros_like(dfw_ref)

        rr = r_ref[...]
        gg = g_ref[...]
        xn = x_ref[...] + gg * rr
        rstd = lax.rsqrt(jnp.mean(xn * xn, axis=-1, keepdims=True) + EPS)
        xh = xn * rstd
        fwv = fw_ref[...]
        e = xh * fwv - t_ref[...]
        loss_ref[...] += 0.5 * jnp.sum(jnp.mean(e * e, axis=-1, keepdims=True), axis=0, keepdims=True)
        dy = e * (1.0 / d)
        dfw_ref[...] += jnp.sum(dy * xh, axis=0, keepdims=True)
        dxh = dy * fwv
        dx = rstd * (dxh - xh * jnp.mean(dxh * xh, axis=-1, keepdims=True))
        dx_ref[...] = dx
        dr_ref[...] = (gg * dx).astype(BF16)
        dg_ref[...] += jnp.sum(dx * rr, axis=0, keepdims=True)

    return pl.pallas_call(
        body, name=name, grid=(s // NORM_ROWS,),
        in_specs=[_row_spec(d), _row_spec(d), _vec_spec(d), _vec_spec(d), _row_spec(d)],
        out_specs=(_vec_spec(LANES), _row_spec(d), _row_spec(d), _vec_spec(d), _vec_spec(d)),
        out_shape=(jax.ShapeDtypeStruct((1, LANES), F32), jax.ShapeDtypeStruct((s, d), F32),
                   jax.ShapeDtypeStruct((s, d), BF16), jax.ShapeDtypeStruct((1, d), F32),
                   jax.ShapeDtypeStruct((1, d), F32)),
        compiler_params=_params(1))(x, r, g, fw, tgt)


def _norm_bwd(x, dhs, dxres, nw, sc, g, r, name):
    s, d = x.shape
    n_dh = len(dhs)
    has_g = g is not None

    def body(*refs):
        x_ref = refs[0]
        dh_refs = refs[1:1 + n_dh]
        dxres_ref, nw_ref, sc_ref = refs[1 + n_dh:4 + n_dh]
        pos = 4 + n_dh
        if has_g:
            g_ref, r_ref = refs[pos:pos + 2]
            pos += 2
            dx_ref, dr_ref, dsh_ref, dsc_ref, dnw_ref, dg_ref = refs[pos:]
            accs = (dsh_ref, dsc_ref, dnw_ref, dg_ref)
        else:
            dx_ref, dsh_ref, dsc_ref, dnw_ref = refs[pos:]
            accs = (dsh_ref, dsc_ref, dnw_ref)

        @pl.when(pl.program_id(0) == 0)
        def _():
            for acc in accs:
                acc[...] = jnp.zeros_like(acc)

        xv = x_ref[...]
        dh = dh_refs[0][...]
        for extra in dh_refs[1:]:
            dh = dh + extra[...]
        rstd = lax.rsqrt(jnp.mean(xv * xv, axis=-1, keepdims=True) + EPS)
        xh = xv * rstd
        nwv = nw_ref[...]
        dsh_ref[...] += jnp.sum(dh, axis=0, keepdims=True)
        dsc_ref[...] += jnp.sum(dh * (xh * nwv), axis=0, keepdims=True)
        dy = dh * (1.0 + sc_ref[...])
        dnw_ref[...] += jnp.sum(dy * xh, axis=0, keepdims=True)
        dxh = dy * nwv
        dx = dxres_ref[...] + rstd * (dxh - xh * jnp.mean(dxh * xh, axis=-1, keepdims=True))
        dx_ref[...] = dx
        if has_g:
            dr_ref[...] = (g_ref[...] * dx).astype(BF16)
            dg_ref[...] += jnp.sum(dx * r_ref[...], axis=0, keepdims=True)

    ins = [x, *dhs, dxres, nw, sc]
    in_specs = [_row_spec(d)] * (2 + n_dh) + [_vec_spec(d)] * 2
    out_shape = [jax.ShapeDtypeStruct((s, d), F32)]
    out_specs = [_row_spec(d)]
    if has_g:
        ins += [g, r]
        in_specs += [_vec_spec(d), _row_spec(d)]
        out_shape.append(jax.ShapeDtypeStruct((s, d), BF16))
        out_specs.append(_row_spec(d))
    n_vec = 4 if has_g else 3
    out_shape += [jax.ShapeDtypeStruct((1, d), F32)] * n_vec
    out_specs += [_vec_spec(d)] * n_vec
    return pl.pallas_call(body, name=name, grid=(s // NORM_ROWS,), in_specs=in_specs, out_specs=tuple(out_specs),
                          out_shape=tuple(out_shape), compiler_params=_params(1))(*ins)


FFN_ROWS = 256
FFN_COLS = 1408
HALO = 16
INV_SQRT2 = 0.7071067811865476
INV_SQRT_2PI = 0.3989422804014327


def _gelu(x):
    return 0.5 * x * (1.0 + lax.erf(x * INV_SQRT2))


def _gelu_grad(x):
    return 0.5 * (1.0 + lax.erf(x * INV_SQRT2)) + x * (INV_SQRT_2PI * jnp.exp(-0.5 * x * x))


def _halo_specs(rows, cols, halo, n_rows_total, col_of):
    per = rows // halo
    last = n_rows_total // halo - 1
    cur = pl.BlockSpec((rows, cols), lambda j, i: (i, col_of(j)))
    prev = pl.BlockSpec((halo, cols), lambda j, i: (jnp.maximum(i * per - 1, 0), col_of(j)))
    nxt = pl.BlockSpec((halo, cols), lambda j, i: (jnp.minimum((i + 1) * per, last), col_of(j)))
    return [prev, cur, nxt]


def _shift_rows(x, k):
    n = x.shape[0]
    return pltpu.roll(x, k % n, axis=0)


def _conv3(ext, w):
    return w[0:1, :] * _shift_rows(ext, 1) + w[1:2, :] * ext + w[2:3, :] * _shift_rows(ext, -1)


def _ext_block(prev_ref, cur_ref, next_ref, i, n_i):
    prev = jnp.where(i > 0, prev_ref[...].astype(F32), 0.0)
    nxt = jnp.where(i < n_i - 1, next_ref[...].astype(F32), 0.0)
    return jnp.concatenate([prev, cur_ref[...].astype(F32), nxt], axis=0)


def _ffn_act(u, cw, name):
    s = u.shape[0]
    nc, ns = D_FF // FFN_COLS, s // FFN_ROWS

    def body(gp, gc, gn, vp, vc, vn, wg_ref, wv_ref, o_ref):
        i = pl.program_id(1)
        cg = _conv3(_ext_block(gp, gc, gn, i, ns), wg_ref[...])[HALO:HALO + FFN_ROWS]
        cv = _conv3(_ext_block(vp, vc, vn, i, ns), wv_ref[...])[HALO:HALO + FFN_ROWS]
        o_ref[...] = (_gelu(cg) * cv).astype(BF16)

    in_specs = (_halo_specs(FFN_ROWS, FFN_COLS, HALO, s, lambda j: j)
                + _halo_specs(FFN_ROWS, FFN_COLS, HALO, s, lambda j: j + nc)
                + [pl.BlockSpec((3, FFN_COLS), lambda j, i: (0, j)),
                   pl.BlockSpec((3, FFN_COLS), lambda j, i: (0, j + nc))])
    return pl.pallas_call(
        body, name=name, grid=(nc, ns), in_specs=in_specs,
        out_specs=pl.BlockSpec((FFN_ROWS, FFN_COLS), lambda j, i: (i, j)),
        out_shape=jax.ShapeDtypeStruct((s, D_FF), BF16), compiler_params=_params(2),
    )(u, u, u, u, u, u, cw, cw)


def _ffn_act_bwd(u, dact, cw, name):
    s = u.shape[0]
    nc, ns = D_FF // FFN_COLS, s // FFN_ROWS

    def body(gp, gc, gn, vp, vc, vn, dp, dc, dn, wg_ref, wv_ref, dug_ref, duv_ref, dwg_ref, dwv_ref):
        i = pl.program_id(1)

        @pl.when(i == 0)
        def _():
            dwg_ref[...] = jnp.zeros_like(dwg_ref)
            dwv_ref[...] = jnp.zeros_like(dwv_ref)

        ug = _ext_block(gp, gc, gn, i, ns)
        uv = _ext_block(vp, vc, vn, i, ns)
        da = _ext_block(dp, dc, dn, i, ns)
        wg, wv = wg_ref[...], wv_ref[...]
        cg, cv = _conv3(ug, wg), _conv3(uv, wv)
        dcg = da * cv * _gelu_grad(cg)
        dcv = da * _gelu(cg)
        inner = slice(HALO, HALO + FFN_ROWS)
        for d_c, uu, w, du_ref, dw_ref in ((dcg, ug, wg, dug_ref, dwg_ref), (dcv, uv, wv, duv_ref, dwv_ref)):
            du = w[0:1, :] * _shift_rows(d_c, -1) + w[1:2, :] * d_c + w[2:3, :] * _shift_rows(d_c, 1)
            du_ref[...] = du[inner].astype(BF16)
            d_in = d_c[inner]
            for tap in range(3):
                dw_ref[tap:tap + 1, :] += jnp.sum(d_in * _shift_rows(uu, 1 - tap)[inner], axis=0, keepdims=True)

    in_specs = (_halo_specs(FFN_ROWS, FFN_COLS, HALO, s, lambda j: j)
                + _halo_specs(FFN_ROWS, FFN_COLS, HALO, s, lambda j: j + nc)
                + _halo_specs(FFN_ROWS, FFN_COLS, HALO, s, lambda j: j)
                + [pl.BlockSpec((3, FFN_COLS), lambda j, i: (0, j)),
                   pl.BlockSpec((3, FFN_COLS), lambda j, i: (0, j + nc))])
    blk = pl.BlockSpec((FFN_ROWS, FFN_COLS), lambda j, i: (i, j))
    acc = pl.BlockSpec((HALO, FFN_COLS), lambda j, i: (0, j))
    return pl.pallas_call(
        body, name=name, grid=(nc, ns), in_specs=in_specs, out_specs=(blk, blk, acc, acc),
        out_shape=(jax.ShapeDtypeStruct((s, D_FF), BF16), jax.ShapeDtypeStruct((s, D_FF), BF16),
                   jax.ShapeDtypeStruct((HALO, D_FF), F32), jax.ShapeDtypeStruct((HALO, D_FF), F32)),
        compiler_params=_params(2),
    )(u, u, u, u, u, u, dact, dact, dact, cw, cw)


CONV_ROWS = 512
CONV_HALO = 16
CONV_PAD = CONV_WIDTH // 2


def _conv_halo_specs(cols, s):
    per = CONV_ROWS // CONV_HALO
    last = s // CONV_HALO - 1
    return [pl.BlockSpec((CONV_HALO, cols), lambda i: (jnp.maximum(i * per - 1, 0), 0)),
            pl.BlockSpec((CONV_ROWS, cols), lambda i: (i, 0)),
            pl.BlockSpec((CONV_HALO, cols), lambda i: (jnp.minimum((i + 1) * per, last), 0))]


def _glu_ext(pp, pc, pn, i, n_i):
    ext = _ext_block(pp, pc, pn, i, n_i)
    return ext[:, :CONV_CH] * _sigmoid(ext[:, CONV_CH:])


def _conv_mixer(pa, cw, cb, lnw, lnb, name):
    s = pa.shape[0]
    ns = s // CONV_ROWS

    def body(pp, pc, pn, cw_ref, cb_ref, lnw_ref, lnb_ref, o_ref, c_ref):
        i = pl.program_id(0)
        a = _glu_ext(pp, pc, pn, i, ns)
        acc = jnp.zeros((CONV_ROWS, CONV_CH), F32)
        for tap in range(CONV_WIDTH):
            acc = acc + cw_ref[tap:tap + 1, :] * _shift_rows(a, -(tap + 1))[:CONV_ROWS]
        cv = acc + cb_ref[...]
        c_ref[...] = cv
        mu = jnp.mean(cv, axis=-1, keepdims=True)
        xc = cv - mu
        rstd = lax.rsqrt(jnp.mean(xc * xc, axis=-1, keepdims=True) + EPS)
        y = xc * rstd * lnw_ref[...] + lnb_ref[...]
        o_ref[...] = (y * _sigmoid(y)).astype(BF16)

    vec = pl.BlockSpec((1, CONV_CH), lambda i: (0, 0))
    blk = pl.BlockSpec((CONV_ROWS, CONV_CH), lambda i: (i, 0))
    return pl.pallas_call(
        body, name=name, grid=(ns,),
        in_specs=_conv_halo_specs(2 * CONV_CH, s) + [pl.BlockSpec((CONV_WIDTH, CONV_CH), lambda i: (0, 0)), vec, vec, vec],
        out_specs=(blk, blk),
        out_shape=(jax.ShapeDtypeStruct((s, CONV_CH), BF16), jax.ShapeDtypeStruct((s, CONV_CH), F32)),
        compiler_params=_params(1))(pa, pa, pa, cw, cb, lnw, lnb)


def _conv_mixer_bwd_ln(cv, dout, lnw, lnb, name):
    s = cv.shape[0]

    def body(c_ref, do_ref, lnw_ref, lnb_ref, dc_ref, dlnw_ref, dlnb_ref, dcb_ref):
        @pl.when(pl.program_id(0) == 0)
        def _():
            dlnw_ref[...] = jnp.zeros_like(dlnw_ref)
            dlnb_ref[...] = jnp.zeros_like(dlnb_ref)
            dcb_ref[...] = jnp.zeros_like(dcb_ref)

        c = c_ref[...]
        mu = jnp.mean(c, axis=-1, keepdims=True)
        xc = c - mu
        rstd = lax.rsqrt(jnp.mean(xc * xc, axis=-1, keepdims=True) + EPS)
        xh = xc * rstd
        w = lnw_ref[...]
        y = xh * w + lnb_ref[...]
        dy = do_ref[...] * _silu_grad(y)
        dlnw_ref[...] += jnp.sum(dy * xh, axis=0, keepdims=True)
        dlnb_ref[...] += jnp.sum(dy, axis=0, keepdims=True)
        dxh = dy * w
        dc = rstd * (dxh - jnp.mean(dxh, axis=-1, keepdims=True) - xh * jnp.mean(dxh * xh, axis=-1, keepdims=True))
        dc_ref[...] = dc
        dcb_ref[...] += jnp.sum(dc, axis=0, keepdims=True)

    vec = pl.BlockSpec((1, CONV_CH), lambda i: (0, 0))
    blk = pl.BlockSpec((CONV_ROWS, CONV_CH), lambda i: (i, 0))
    return pl.pallas_call(
        body, name=name, grid=(s // CONV_ROWS,), in_specs=[blk, blk, vec, vec], out_specs=(blk, vec, vec, vec),
        out_shape=(jax.ShapeDtypeStruct((s, CONV_CH), F32),) + (jax.ShapeDtypeStruct((1, CONV_CH), F32),) * 3,
        compiler_params=_params(1))(cv, dout, lnw, lnb)


def _conv_mixer_bwd_conv(pa, dc, cw, name):
    s = pa.shape[0]
    ns = s // CONV_ROWS

    def body(pp, pc, pn, dp, dcc, dn, cw_ref, dpa_ref, dcw_ref):
        i = pl.program_id(0)

        @pl.when(i == 0)
        def _():
            dcw_ref[...] = jnp.zeros_like(dcw_ref)

        a = _glu_ext(pp, pc, pn, i, ns)
        dce = _ext_block(dp, dcc, dn, i, ns)
        dcur = dcc[...]
        da = jnp.zeros((CONV_ROWS, CONV_CH), F32)
        for tap in range(CONV_WIDTH):
            da = da + cw_ref[tap:tap + 1, :] * _shift_rows(dce, -(CONV_WIDTH - tap))[:CONV_ROWS]
            dcw_ref[tap:tap + 1, :] += jnp.sum(dcur * _shift_rows(a, -(tap + 1))[:CONV_ROWS], axis=0, keepdims=True)
        cur = pc[...]
        val, sg = cur[:, :CONV_CH], _sigmoid(cur[:, CONV_CH:])
        dpa_ref[:, :CONV_CH] = (da * sg).astype(BF16)
        dpa_ref[:, CONV_CH:] = (da * val * sg * (1.0 - sg)).astype(BF16)

    return pl.pallas_call(
        body, name=name, grid=(ns,),
        in_specs=_conv_halo_specs(2 * CONV_CH, s) + _conv_halo_specs(CONV_CH, s)
        + [pl.BlockSpec((CONV_WIDTH, CONV_CH), lambda i: (0, 0))],
        out_specs=(pl.BlockSpec((CONV_ROWS, 2 * CONV_CH), lambda i: (i, 0)),
                   pl.BlockSpec((32, CONV_CH), lambda i: (0, 0))),
        out_shape=(jax.ShapeDtypeStruct((s, 2 * CONV_CH), BF16), jax.ShapeDtypeStruct((32, CONV_CH), F32)),
        compiler_params=_params(1))(pa, pa, pa, dc, dc, dc, cw)


SLOPES = tuple(float(2.0 ** (-ALIBI_MAX_EXP * (h + 1) / N_HEADS)) for h in range(N_HEADS))
ATT_SCALE = HEAD_DIM ** -0.5


PAIR = 2 * HEAD_DIM
N_PAIRS = N_HEADS // 2
ATT_WIN = ATT_BLOCK + 2 * ATT_HALF


ATT_GROUPS = {1: 4, 4: 1, 16: 1}


def _window_specs(dil, n_steps, col_of):
    per = 2 * ATT_GROUPS[dil]
    rows, halo = ATT_BLOCK * dil * ATT_GROUPS[dil], ATT_HALF * dil
    return [pl.BlockSpec((halo, PAIR), lambda i, p: (jnp.maximum(per * i - 1, 0), col_of(p))),
            pl.BlockSpec((rows, PAIR), lambda i, p: (i, col_of(p))),
            pl.BlockSpec((halo, PAIR), lambda i, p: (jnp.minimum(per * (i + 1), per * n_steps - 1), col_of(p)))]


def _residue(ref, r, n, dil, start=0):
    return ref[pl.ds(start * dil + r, n, stride=dil), :] if dil > 1 else ref[pl.ds(start + r, n), :]


def _store_residue(ref, r, dil, start, val):
    if dil > 1:
        ref[pl.ds(start * dil + r, val.shape[0], stride=dil), :] = val
    else:
        ref[pl.ds(start + r, val.shape[0]), :] = val


def _residue_window(refs, r, dil, g=0):
    prev, cur, nxt = refs
    groups = ATT_GROUPS[dil]
    lo = max(g * ATT_BLOCK - ATT_HALF, 0)
    hi = min((g + 1) * ATT_BLOCK + ATT_HALF, groups * ATT_BLOCK)
    parts = [_residue(prev, r, ATT_HALF, dil)] if g == 0 else []
    parts.append(_residue(cur, r, hi - lo, dil, lo))
    if g == groups - 1:
        parts.append(_residue(nxt, r, ATT_HALF, dil))
    return jnp.concatenate(parts, axis=0)


def _band_masks(i, length, dil, transposed):
    shape = (ATT_WIN, ATT_BLOCK) if transposed else (ATT_BLOCK, ATT_WIN)
    row = lax.broadcasted_iota(jnp.int32, shape, 0)
    col = lax.broadcasted_iota(jnp.int32, shape, 1)
    wide = row if transposed else col
    dist = jnp.abs((row - col - ATT_HALF) if transposed else (row + ATT_HALF - col))
    wpos = i * ATT_BLOCK - ATT_HALF + wide
    valid = (dist <= ATT_HALF) & (wpos >= 0) & (wpos < length)
    return valid, dist.astype(F32) * float(dil)


def _attn_branch(qkv, dil, name):
    s = qkv.shape[0]
    groups = ATT_GROUPS[dil]
    rows = ATT_BLOCK * dil * groups
    n_steps = s // rows
    length = s // dil
    nt = (((1,), (1,)), ((), ()))

    def body(q_ref, kp, kc, kn, vp, vc, vn, o_ref, l_ref):
        i, pair = pl.program_id(0), pl.program_id(1)
        items = [(g, r) for g in range(groups) for r in range(dil)]
        q = jnp.stack([_residue(q_ref, r, ATT_BLOCK, dil, g * ATT_BLOCK) for g, r in items]).astype(BF16)
        k = jnp.stack([_residue_window((kp, kc, kn), r, dil, g) for g, r in items]).astype(BF16)
        v = jnp.stack([_residue_window((vp, vc, vn), r, dil, g) for g, r in items]).astype(BF16)
        per_group = [_band_masks(i * groups + g, length, dil, False) for g in range(groups)]
        valid = jnp.stack([per_group[g][0] for g, _ in items]) if groups > 1 else per_group[0][0][None]
        distf = jnp.stack([per_group[g][1] for g, _ in items]) if groups > 1 else per_group[0][1][None]
        outs, lses = [], []
        for hh in range(2):
            sl = slice(hh * HEAD_DIM, (hh + 1) * HEAD_DIM)
            slope = jnp.where(pair == 0, SLOPES[hh], jnp.where(pair == 1, SLOPES[2 + hh], SLOPES[4 + hh]))
            sc = jnp.einsum("bqd,bkd->bqk", q[:, :, sl], k[:, :, sl], preferred_element_type=F32) * ATT_SCALE
            sc = jnp.where(valid, sc - slope * distf, MASK_VALUE)
            m = jnp.max(sc, axis=-1, keepdims=True)
            p = jnp.exp(sc - m)
            den = jnp.sum(p, axis=-1, keepdims=True)
            outs.append(jnp.einsum("bqk,bkd->bqd", p.astype(BF16), v[:, :, sl], preferred_element_type=F32) / den)
            lses.append(jnp.broadcast_to(m + jnp.log(den), (len(items), ATT_BLOCK, HEAD_DIM)))
        o_all, l_all = jnp.concatenate(outs, axis=2), jnp.concatenate(lses, axis=2)
        for n, (g, r) in enumerate(items):
            _store_residue(o_ref, r, dil, g * ATT_BLOCK, o_all[n])
            _store_residue(l_ref, r, dil, g * ATT_BLOCK, l_all[n])

    out_blk = pl.BlockSpec((rows, PAIR), lambda i, p: (i, p))
    return pl.pallas_call(
        body, name=name, grid=(n_steps, N_PAIRS),
        in_specs=[pl.BlockSpec((rows, PAIR), lambda i, p: (i, p))]
        + _window_specs(dil, n_steps, lambda p: N_PAIRS + p) + _window_specs(dil, n_steps, lambda p: 2 * N_PAIRS + p),
        out_specs=(out_blk, out_blk),
        out_shape=(jax.ShapeDtypeStruct((s, ATT_WIDTH), F32),) * 2,
        compiler_params=_params(2))(qkv, qkv, qkv, qkv, qkv, qkv, qkv)


ATT_ROWS = 512


def _attn_combine(outs, lses, name):
    s = outs[0].shape[0]

    def body(o1, o2, o3, l1, l2, l3, att_ref, att32_ref, lse_ref):
        ls = [l1[...], l2[...], l3[...]]
        m = jnp.maximum(jnp.maximum(ls[0], ls[1]), ls[2])
        es = [jnp.exp(l - m) for l in ls]
        den = es[0] + es[1] + es[2]
        att = (es[0] * o1[...] + es[1] * o2[...] + es[2] * o3[...]) / den
        att_ref[...] = att.astype(BF16)
        att32_ref[...] = att
        lse_ref[...] = m + jnp.log(den)

    blk = pl.BlockSpec((ATT_ROWS, ATT_WIDTH), lambda i: (i, 0))
    return pl.pallas_call(
        body, name=name, grid=(s // ATT_ROWS,), in_specs=[blk] * 6, out_specs=(blk, blk, blk),
        out_shape=(jax.ShapeDtypeStruct((s, ATT_WIDTH), BF16), jax.ShapeDtypeStruct((s, ATT_WIDTH), F32),
                   jax.ShapeDtypeStruct((s, ATT_WIDTH), F32)),
        compiler_params=_params(1))(*outs, *lses)


def _attn_delta(datt, att, name):
    s = att.shape[0]

    def body(d_ref, a_ref, delta_ref):
        prod = d_ref[...] * a_ref[...]
        for h in range(N_HEADS):
            sl = slice(h * HEAD_DIM, (h + 1) * HEAD_DIM)
            delta_ref[:, sl] = jnp.broadcast_to(jnp.sum(prod[:, sl], axis=-1, keepdims=True), (ATT_ROWS, HEAD_DIM))

    blk = pl.BlockSpec((ATT_ROWS, ATT_WIDTH), lambda i: (i, 0))
    return pl.pallas_call(
        body, name=name, grid=(s // ATT_ROWS,), in_specs=[blk, blk], out_specs=blk,
        out_shape=jax.ShapeDtypeStruct((s, ATT_WIDTH), F32), compiler_params=_params(1))(datt, att)


def _attn_branch_bwd(qkv, do, lse, delta, prev, dil, name):
    s = qkv.shape[0]
    groups = ATT_GROUPS[dil]
    rows = ATT_BLOCK * dil * groups
    n_steps = s // rows
    length = s // dil
    has_prev = prev is not None
    tn = (((0,), (0,)), ((), ()))
    nt = (((1,), (1,)), ((), ()))

    def body(*refs):
        qs, ks, vs, dos, ls, des = (refs[3 * n:3 * n + 3] for n in range(6))
        rest = refs[18:]
        if has_prev:
            pq, pk, pv = rest[:3]
            rest = rest[3:]
        dq_ref, dk_ref, dv_ref = rest
        i, pair = pl.program_id(0), pl.program_id(1)
        items = [(g, r) for g in range(groups) for r in range(dil)]
        cur = lambda t: jnp.stack([_residue(t[1], r, ATT_BLOCK, dil, g * ATT_BLOCK) for g, r in items])
        win = lambda t: jnp.stack([_residue_window(t, r, dil, g) for g, r in items])
        q_cur, k_cur, v_cur, do_cur = (cur(t).astype(BF16) for t in (qs, ks, vs, dos))
        q_win, k_win, v_win, do_win = (win(t).astype(BF16) for t in (qs, ks, vs, dos))
        l_cur, de_cur, l_win, de_win = cur(ls), cur(des), win(ls), win(des)

        def masks(transposed):
            per_group = [_band_masks(i * groups + g, length, dil, transposed) for g in range(groups)]
            if groups == 1:
                return per_group[0][0][None], per_group[0][1][None]
            return jnp.stack([per_group[g][0] for g, _ in items]), jnp.stack([per_group[g][1] for g, _ in items])

        valid_q, distf_q = masks(False)
        valid_k, distf_k = masks(True)
        dot = lambda eq, a, b: jnp.einsum(eq, a, b, preferred_element_type=F32)
        dqs, dks, dvs = [], [], []
        for hh in range(2):
            sl = slice(hh * HEAD_DIM, (hh + 1) * HEAD_DIM)
            one = slice(hh * HEAD_DIM, hh * HEAD_DIM + 1)
            slope = jnp.where(pair == 0, SLOPES[hh], jnp.where(pair == 1, SLOPES[2 + hh], SLOPES[4 + hh]))
            sc = dot("bqd,bkd->bqk", q_cur[:, :, sl], k_win[:, :, sl]) * ATT_SCALE - slope * distf_q
            p = jnp.exp(jnp.where(valid_q, sc - l_cur[:, :, one], MASK_VALUE))
            dp = dot("bqd,bkd->bqk", do_cur[:, :, sl], v_win[:, :, sl])
            ds = (p * (dp - de_cur[:, :, one]) * ATT_SCALE).astype(BF16)
            dqs.append(dot("bqk,bkd->bqd", ds, k_win[:, :, sl]))

            sc2 = dot("bqd,bkd->bqk", q_win[:, :, sl], k_cur[:, :, sl]) * ATT_SCALE - slope * distf_k
            p2 = jnp.exp(jnp.where(valid_k, sc2 - l_win[:, :, one], MASK_VALUE))
            dvs.append(dot("bqk,bqd->bkd", p2.astype(BF16), do_win[:, :, sl]))
            dp2 = dot("bqd,bkd->bqk", do_win[:, :, sl], v_cur[:, :, sl])
            ds2 = (p2 * (dp2 - de_win[:, :, one]) * ATT_SCALE).astype(BF16)
            dks.append(dot("bqk,bqd->bkd", ds2, q_win[:, :, sl]))
        for parts, acc, out in ((dqs, pq if has_prev else None, dq_ref), (dks, pk if has_prev else None, dk_ref),
                                (dvs, pv if has_prev else None, dv_ref)):
            val = jnp.concatenate(parts, axis=2)
            for n, (g, r) in enumerate(items):
                piece = val[n]
                if has_prev:
                    piece = piece + _residue(acc, r, ATT_BLOCK, dil, g * ATT_BLOCK)
                _store_residue(out, r, dil, g * ATT_BLOCK, piece)

    blk = pl.BlockSpec((rows, PAIR), lambda i, p: (i, p))
    in_specs = (_window_specs(dil, n_steps, lambda p: p) + _window_specs(dil, n_steps, lambda p: N_PAIRS + p)
                + _window_specs(dil, n_steps, lambda p: 2 * N_PAIRS + p) + _window_specs(dil, n_steps, lambda p: p) * 3)
    ins = [qkv] * 9 + [do] * 3 + [lse] * 3 + [delta] * 3
    if has_prev:
        in_specs += [blk] * 3
        ins += list(prev)
    return pl.pallas_call(
        body, name=name, grid=(n_steps, N_PAIRS), in_specs=in_specs, out_specs=(blk, blk, blk),
        out_shape=(jax.ShapeDtypeStruct((s, ATT_WIDTH), F32),) * 3,
        compiler_params=_params(2))(*ins)


TB = 2 * REC_CHUNK
REC_ROWS = 5 * REC_WIDTH


REC_LEVELS = 6


def _scan_pos(p, rev):
    p = p & (REC_CHUNK - 1)
    return (REC_CHUNK - 1 - p) if rev else p


def _split3(x):
    hi = x.astype(BF16)
    rest = x - hi.astype(F32)
    mid = rest.astype(BF16)
    return hi, mid, (rest - mid.astype(F32)).astype(BF16)


def _chunk_sums(x, rev, with_levels):
    row = lax.broadcasted_iota(jnp.int32, (TB, TB), 0)
    col = lax.broadcasted_iota(jnp.int32, (TB, TB), 1)
    same = (row < REC_CHUNK) == (col < REC_CHUNK)
    s_row, s_col = _scan_pos(row, rev), _scan_pos(col, rev)
    mats = [same & (s_row <= s_col)]
    if with_levels:
        for level in range(1, REC_LEVELS + 1):
            shift = REC_LEVELS + 1 - level
            boundary = ((s_col >> shift) << shift) + (REC_CHUNK >> level) - 1
            mats.append(same & (s_row <= boundary))
        mats.append(same)
    cat = jnp.concatenate([m.astype(BF16) for m in mats], axis=1)
    total = sum(jnp.dot(term, cat, preferred_element_type=F32) for term in _split3(x))
    return [total[:, n * TB:(n + 1) * TB] for n in range(len(mats))]


def _hg_prep(qraw, z, lb, rev):
    lane = lax.broadcasted_iota(jnp.int32, (REC_WIDTH, TB), 1)
    in_a = lane < REC_CHUNK
    scan = _scan_pos(lane, rev)
    sig, sigm = _sigmoid(z), _sigmoid(-z)
    f = lb + (1.0 - lb) * sig
    kk = (1.0 - lb) * sigm
    sums = _chunk_sums(jnp.log(jnp.maximum(f, F_TINY)), rev, True)
    b, bend = sums[0], sums[-1]
    q = qraw * _sigmoid(qraw)
    eq, ek = [], []
    for level in range(1, REC_LEVELS + 1):
        r = sums[level]
        e = jnp.exp(jnp.minimum(b - r, r - b))
        second = ((scan >> (REC_LEVELS - level)) & 1) == 1
        eq.append(jnp.where(second, e, 0.0))
        ek.append(jnp.where(second, 0.0, e))
    lanes_end = (0, REC_CHUNK) if rev else (REC_CHUNK - 1, TB - 1)
    end_a, end_b = (b[:, n:n + 1] for n in lanes_end)
    return dict(in_a=in_a, sig=sig, sigm=sigm, f=f, kk=kk, b=b, end_a=end_a, end_b=end_b,
                q=q, qh=q * jnp.exp(b), kh=kk * jnp.exp(bend - b), ekb=jnp.exp(bend - b), eq=eq, ek=ek)


def _level_masks(rev):
    row = lax.broadcasted_iota(jnp.int32, (TB, TB), 0)
    col = lax.broadcasted_iota(jnp.int32, (TB, TB), 1)
    same = (row < REC_CHUNK) == (col < REC_CHUNK)
    s_row, s_col = _scan_pos(row, rev), _scan_pos(col, rev)
    masks = [same & ((s_row >> (REC_LEVELS + 1 - level)) == (s_col >> (REC_LEVELS + 1 - level)))
             for level in range(1, REC_LEVELS + 1)]
    return masks, row == col


def _head_rows(x, h):
    return x[h * HEAD_DIM:(h + 1) * HEAD_DIM, :]


def _block_diag_mask():
    r = lax.broadcasted_iota(jnp.int32, (REC_WIDTH, REC_WIDTH), 0) // HEAD_DIM
    c = lax.broadcasted_iota(jnp.int32, (REC_WIDTH, REC_WIDTH), 1) // HEAD_DIM
    return (r == c).astype(F32)


def _heads(x):
    return x.reshape(N_HEADS, HEAD_DIM, TB)


def _hgrn_scan(projt, lb, rev, name):
    s = projt.shape[1]
    nblk = s // TB
    zrow = 2 if rev else 1
    tmap = (lambda i: nblk - 1 - i) if rev else (lambda i: i)
    tn = (((0,), (0,)), ((), ()))
    nt = (((1,), (1,)), ((), ()))

    def body(q_ref, z_ref, v_ref, lb_ref, o_ref, hs_ref, at_ref, h_ref):
        @pl.when(pl.program_id(0) == 0)
        def _():
            h_ref[...] = jnp.zeros_like(h_ref)

        v = v_ref[...]
        vb = v.astype(BF16)
        pr = _hg_prep(q_ref[...], z_ref[...], lb_ref[...], rev)
        q, kk = pr["q"], pr["kk"]
        masks, diag = _level_masks(rev)
        qt = [(q * e).astype(BF16) for e in pr["eq"]]
        kt = [(kk * e).astype(BF16) for e in pr["ek"]]
        own = jnp.sum(_heads(q * kk), axis=1)
        outs = []
        for h in range(N_HEADS):
            sc = jnp.where(diag, own[h:h + 1, :], 0.0)
            for level in range(REC_LEVELS):
                sc = sc + jnp.where(masks[level],
                                    lax.dot_general(_head_rows(kt[level], h), _head_rows(qt[level], h), tn,
                                                    preferred_element_type=F32), 0.0)
            a_bf = sc.astype(BF16)
            at_ref[h] = a_bf
            outs.append(jnp.dot(_head_rows(vb, h), a_bf, preferred_element_type=F32))
        o = jnp.concatenate(outs, axis=0)
        bd_mask = _block_diag_mask()
        order = ((1, ~pr["in_a"], pr["end_b"]), (0, pr["in_a"], pr["end_a"]))
        if not rev:
            order = order[::-1]
        for slot, msk, bend in order:
            h0 = h_ref[...]
            hs_ref[slot] = h0
            o = o + lax.dot_general(h0.astype(BF16), jnp.where(msk, pr["qh"], 0.0).astype(BF16), tn,
                                    preferred_element_type=F32)
            upd = lax.dot_general(jnp.where(msk, pr["kh"], 0.0).astype(BF16), vb, nt, preferred_element_type=F32)
            h_ref[...] = jnp.exp(bend) * h0 + upd * bd_mask
        o_ref[...] = o

    row_blk = lambda r: pl.BlockSpec((REC_WIDTH, TB), lambda i: (r, tmap(i)))
    return pl.pallas_call(
        body, name=name, grid=(nblk,),
        in_specs=[row_blk(0), row_blk(zrow), row_blk(3), pl.BlockSpec((REC_WIDTH, 1), lambda i: (0, 0))],
        out_specs=(pl.BlockSpec((REC_WIDTH, TB), lambda i: (0, tmap(i))),
                   pl.BlockSpec((2, REC_WIDTH, REC_WIDTH), lambda i: (tmap(i), 0, 0)),
                   pl.BlockSpec((None, N_HEADS, TB, TB), lambda i: (tmap(i), 0, 0, 0))),
        out_shape=(jax.ShapeDtypeStruct((REC_WIDTH, s), F32),
                   jax.ShapeDtypeStruct((s // REC_CHUNK, REC_WIDTH, REC_WIDTH), F32),
                   jax.ShapeDtypeStruct((nblk, N_HEADS, TB, TB), BF16)),
        scratch_shapes=[pltpu.VMEM((REC_WIDTH, REC_WIDTH), F32)],
        compiler_params=_params(1))(projt, projt, projt, lb)


def _hgrn_scan_bwd(projt, lb, dot, hs, at, prev, rev, name):
    s = projt.shape[1]
    nblk = s // TB
    zrow = 2 if rev else 1
    tmap = (lambda i: i) if rev else (lambda i: nblk - 1 - i)
    has_prev = prev is not None
    tn = (((0,), (0,)), ((), ()))
    nt = (((1,), (1,)), ((), ()))

    def body(*refs):
        q_ref, z_ref, v_ref, lb_ref, do_ref, hs_ref, at_ref = refs[:7]
        rest = refs[7:]
        if has_prev:
            pq_ref, pv_ref = rest[:2]
            rest = rest[2:]
        dq_ref, dz_ref, dv_ref, dlb_ref, dh_ref = rest

        @pl.when(pl.program_id(0) == 0)
        def _():
            dh_ref[...] = jnp.zeros_like(dh_ref)
            dlb_ref[...] = jnp.zeros_like(dlb_ref)

        qraw, v, do, lbv = q_ref[...], v_ref[...], do_ref[...], lb_ref[...]
        dob, vb = do.astype(BF16), v.astype(BF16)
        pr = _hg_prep(qraw, z_ref[...], lbv, rev)
        q, kk, b, in_a = pr["q"], pr["kk"], pr["b"], pr["in_a"]
        masks, diag = _level_masks(rev)
        qt = [(q * e).astype(BF16) for e in pr["eq"]]
        kt = [(kk * e).astype(BF16) for e in pr["ek"]]
        dq_h, dk_h, dv_h, db_h = [], [], [], []
        for h in range(N_HEADS):
            d_at = lax.dot_general(_head_rows(vb, h), _head_rows(dob, h), tn, preferred_element_type=F32)
            dv_h.append(lax.dot_general(_head_rows(dob, h), at_ref[h], nt, preferred_element_type=F32))
            d_own = jnp.sum(jnp.where(diag, d_at, 0.0), axis=0, keepdims=True)
            dq_acc, dk_acc = d_own * _head_rows(kk, h), d_own * _head_rows(q, h)
            db_acc = jnp.zeros((HEAD_DIM, TB), F32)
            for lv in range(REC_LEVELS):
                d_lv = jnp.where(masks[lv], d_at, 0.0).astype(BF16)
                q_lv, k_lv = _head_rows(qt[lv], h), _head_rows(kt[lv], h)
                dqt = jnp.dot(k_lv, d_lv, preferred_element_type=F32)
                dkt = lax.dot_general(q_lv, d_lv, nt, preferred_element_type=F32)
                dq_acc = dq_acc + _head_rows(pr["eq"][lv], h) * dqt
                dk_acc = dk_acc + _head_rows(pr["ek"][lv], h) * dkt
                db_acc = db_acc + q_lv.astype(F32) * dqt - k_lv.astype(F32) * dkt
            dq_h.append(dq_acc)
            dk_h.append(dk_acc)
            db_h.append(db_acc)
        dq_in, dk_in, dv, db_in = (jnp.concatenate(t, axis=0) for t in (dq_h, dk_h, dv_h, db_h))
        dq = dk = jnp.zeros((REC_WIDTH, TB), F32)

        zero = jnp.zeros((REC_WIDTH, TB), F32)
        bd_mask = _block_diag_mask()
        eb = jnp.exp(b)
        const = zero
        order = ((0, in_a, pr["end_a"]), (1, ~in_a, pr["end_b"]))
        if not rev:
            order = order[::-1]
        for slot, msk, bend in order:
            h0 = hs_ref[slot]
            dh1 = dh_ref[...]
            dh1b = dh1.astype(BF16)
            dq = dq + eb * jnp.dot(h0.astype(BF16), jnp.where(msk, do, 0.0).astype(BF16), preferred_element_type=F32)
            dv = dv + lax.dot_general(dh1b, jnp.where(msk, pr["kh"], 0.0).astype(BF16), tn, preferred_element_type=F32)
            dk_int = pr["ekb"] * jnp.dot(dh1b, jnp.where(msk, v, 0.0).astype(BF16), preferred_element_type=F32)
            dk = dk + dk_int
            ebend = jnp.exp(bend)
            c = (jnp.sum(kk * dk_int, axis=1, keepdims=True)
                 + ebend * jnp.sum(h0 * dh1, axis=1, keepdims=True))
            const = const + jnp.where(msk, c, 0.0)
            upd = lax.dot_general(jnp.where(msk, pr["qh"], 0.0).astype(BF16), dob, nt, preferred_element_type=F32)
            dh_ref[...] = ebend * dh1 + upd * bd_mask

        dg = _chunk_sums(db_in + q * dq - kk * dk, not rev, False)[0] + const
        dq, dk = dq + dq_in, dk + dk_in
        sig, sigm, f = pr["sig"], pr["sigm"], pr["f"]
        live = f > F_TINY
        inv_f = 1.0 / jnp.maximum(f, F_TINY)
        one_lb = 1.0 - lbv
        dz = sig * sigm * one_lb * (jnp.where(live, dg * inv_f, 0.0) - dk)
        dlb_ref[...] += jnp.sum(sigm * (jnp.where(live, dg * inv_f, 0.0) - dk), axis=1, keepdims=True)
        dqr = dq * _silu_grad(qraw)
        if has_prev:
            dqr = dqr + pq_ref[...]
            dv = dv + pv_ref[...]
        dq_ref[...] = dqr
        dz_ref[...] = dz
        dv_ref[...] = dv

    row_blk = lambda r: pl.BlockSpec((REC_WIDTH, TB), lambda i: (r, tmap(i)))
    blk = pl.BlockSpec((REC_WIDTH, TB), lambda i: (0, tmap(i)))
    col = pl.BlockSpec((REC_WIDTH, 1), lambda i: (0, 0))
    in_specs = [row_blk(0), row_blk(zrow), row_blk(3), col, blk,
                pl.BlockSpec((2, REC_WIDTH, REC_WIDTH), lambda i: (tmap(i), 0, 0)),
                pl.BlockSpec((None, N_HEADS, TB, TB), lambda i: (tmap(i), 0, 0, 0))]
    ins = [projt, projt, projt, lb, dot, hs, at]
    if has_prev:
        in_specs += [blk, blk]
        ins += list(prev)
    t_shape = jax.ShapeDtypeStruct((REC_WIDTH, s), F32)
    return pl.pallas_call(
        body, name=name, grid=(nblk,), in_specs=in_specs, out_specs=(blk, blk, blk, col),
        out_shape=(t_shape, t_shape, t_shape, jax.ShapeDtypeStruct((REC_WIDTH, 1), F32)),
        scratch_shapes=[pltpu.VMEM((REC_WIDTH, REC_WIDTH), F32)],
        compiler_params=_params(1))(*ins)


REC_OUT_COLS = 512


def _head_rms(o):
    o3 = o.reshape(N_HEADS, HEAD_DIM, o.shape[1])
    rstd = lax.rsqrt(jnp.mean(o3 * o3, axis=1, keepdims=True) + EPS)
    return o3 * rstd, rstd


def _hgrn_out(of, ob, projt, wn, name):
    s = of.shape[1]

    def body(of_ref, ob_ref, g_ref, wn_ref, o_ref):
        on, _ = _head_rms(of_ref[...] + ob_ref[...])
        g = g_ref[...]
        y = on.reshape(REC_WIDTH, REC_OUT_COLS) * wn_ref[...] * (g * _sigmoid(g))
        o_ref[...] = y.T.astype(BF16)

    blk = pl.BlockSpec((REC_WIDTH, REC_OUT_COLS), lambda i: (0, i))
    return pl.pallas_call(
        body, name=name, grid=(s // REC_OUT_COLS,),
        in_specs=[blk, blk, pl.BlockSpec((REC_WIDTH, REC_OUT_COLS), lambda i: (4, i)),
                  pl.BlockSpec((REC_WIDTH, 1), lambda i: (0, 0))],
        out_specs=pl.BlockSpec((REC_OUT_COLS, REC_WIDTH), lambda i: (i, 0)),
        out_shape=jax.ShapeDtypeStruct((s, REC_WIDTH), BF16), compiler_params=_params(1))(of, ob, projt, wn)


def _hgrn_out_bwd(drec, of, ob, projt, wn, name):
    s = of.shape[1]

    def body(d_ref, of_ref, ob_ref, g_ref, wn_ref, do_ref, dg_ref, dwn_ref):
        @pl.when(pl.program_id(0) == 0)
        def _():
            dwn_ref[...] = jnp.zeros_like(dwn_ref)

        dy = d_ref[...].T
        on3, rstd = _head_rms(of_ref[...] + ob_ref[...])
        on = on3.reshape(REC_WIDTH, REC_OUT_COLS)
        g, wnv = g_ref[...], wn_ref[...]
        dg_ref[...] = dy * on * wnv * _silu_grad(g)
        d_onw = dy * (g * _sigmoid(g))
        dwn_ref[...] += jnp.sum(d_onw * on, axis=1, keepdims=True)
        d_on3 = (d_onw * wnv).reshape(N_HEADS, HEAD_DIM, REC_OUT_COLS)
        do3 = rstd * (d_on3 - on3 * jnp.mean(d_on3 * on3, axis=1, keepdims=True))
        do_ref[...] = do3.reshape(REC_WIDTH, REC_OUT_COLS)

    blk = pl.BlockSpec((REC_WIDTH, REC_OUT_COLS), lambda i: (0, i))
    col = pl.BlockSpec((REC_WIDTH, 1), lambda i: (0, 0))
    t_shape = jax.ShapeDtypeStruct((REC_WIDTH, s), F32)
    return pl.pallas_call(
        body, name=name, grid=(s // REC_OUT_COLS,),
        in_specs=[pl.BlockSpec((REC_OUT_COLS, REC_WIDTH), lambda i: (i, 0)), blk, blk,
                  pl.BlockSpec((REC_WIDTH, REC_OUT_COLS), lambda i: (4, i)), col],
        out_specs=(blk, blk, col),
        out_shape=(t_shape, t_shape, jax.ShapeDtypeStruct((REC_WIDTH, 1), F32)),
        compiler_params=_params(1))(drec, of, ob, projt, wn)


def _lower_bounds(gamma, name):
    def body(g_ref, lb_ref, p_ref):
        g0, g1 = g_ref[0:1, :], g_ref[1:2, :]
        m = jnp.maximum(g0, g1)
        e0, e1 = jnp.exp(g0 - m), jnp.exp(g1 - m)
        p0, p1 = e0 / (e0 + e1), e1 / (e0 + e1)
        lb_ref[...] = (p0 + p1) - p0
        p_ref[0:1, :] = p0
        p_ref[1:2, :] = p1

    n = gamma.shape[1]
    return pl.pallas_call(body, name=name,
                          out_shape=(jax.ShapeDtypeStruct((1, n), F32), jax.ShapeDtypeStruct((2, n), F32)))(gamma)


def _lower_bounds_bwd(dlb1, p, name):
    def body(d_ref, p_ref, o_ref):
        p0, p1, d = p_ref[0:1, :], p_ref[1:2, :], d_ref[...]
        inner = p1 * d
        o_ref[0:1, :] = p0 * (0.0 - inner)
        o_ref[1:2, :] = p1 * (d - inner)

    return pl.pallas_call(body, name=name, out_shape=jax.ShapeDtypeStruct(p.shape, F32))(dlb1, p)


def _split_w_in(w_in):
    return dict(conv=w_in[:, G_CONV[0]:G_CONV[1]], qkv=w_in[:, G_QKV[0]:G_QKV[1]],
                rec_t=w_in[:, G_REC[0]:].T, nat=w_in[:, :G_REC[0]])


def _split_w_rest(w_out, w_up, w_down):
    return dict(out=w_out, out_a=w_out[:CONV_CH], out_b=w_out[CONV_CH:CONV_CH + ATT_WIDTH],
                out_c=w_out[CONV_CH + ATT_WIDTH:], up=w_up, down=w_down)


def _col(v):
    return v.reshape(-1, 1)


def _sequence_step(x, tgt, mods, lbs, small, w_in0, later_weights, final_w):
    saved = []
    xin = x
    big = [_split_w_in(w_in0), None]
    h1 = _resid_norm_mod(x, None, None, small[0]["norm1_w"], mods[0][1:2], mods[0][0:1], "norm1_first")
    for l in range(DEPTH):
        sm, w, md = small[l], big[l], mods[l]
        pa = _matmul(h1, w["conv"], "nn", F32, f"proj_conv")
        qkv = _matmul(h1, w["qkv"], "nn", F32, f"proj_qkv")
        projt = _matmul(w["rec_t"], h1, "nt", F32, f"proj_rec")
        a_out, cv = _conv_mixer(pa, sm["conv_a_w"], sm["conv_a_b"], sm["ln_a_w"], sm["ln_a_b"], f"conv_mixer")
        outs, lses = zip(*[_attn_branch(qkv, d, f"attn_d{d}") for d in DILATIONS])
        att, att32, lse = _attn_combine(outs, lses, f"attn_combine")
        lb_f, lb_b = _col(lbs[l][0]), _col(lbs[l][1])
        of, hsf, atf = _hgrn_scan(projt, lb_f, False, "hgrn_fwd")
        ob, hsb, atb = _hgrn_scan(projt, lb_b, True, "hgrn_rev")
        wn = _col(sm["rec_norm_w"])
        rec = _hgrn_out(of, ob, projt, wn, f"hgrn_out")
        mixed = jnp.concatenate([a_out, att, rec], axis=1)
        if l == 0:
            w_in1, w_out_all, w_up_all, w_down_all = later_weights(rec)
            big[0].update(_split_w_rest(w_out_all[0], w_up_all[0], w_down_all[0]))
            big[1] = dict(_split_w_in(w_in1), **_split_w_rest(w_out_all[1], w_up_all[1], w_down_all[1]))
        r1 = _matmul(mixed, w["out"], "nn", F32, f"out_proj")
        xmid, h2 = _resid_norm_mod(xin, r1, md[2:3], sm["norm2_w"], md[4:5], md[3:4], f"norm2")
        u = _matmul(h2, w["up"], "nn", BF16, f"ffn_up")
        act = _ffn_act(u, sm["conv_f_w"], f"ffn_act")
        r2 = _matmul(act, w["down"], "nn", F32, f"ffn_down")
        saved.append(dict(xin=xin, h1=h1, pa=pa, qkv=qkv, projt=projt, cv=cv, att32=att32, lse=lse, of=of, ob=ob,
                          hsf=hsf, hsb=hsb, atf=atf, atb=atb, lb_f=lb_f, lb_b=lb_b, wn=wn, mixed=mixed, r1=r1, xmid=xmid, h2=h2,
                          u=u, act=act, r2=r2))
        if l + 1 < DEPTH:
            nxt = small[l + 1]
            xin, h1 = _resid_norm_mod(xmid, r2, md[5:6], nxt["norm1_w"], mods[l + 1][1:2], mods[l + 1][0:1],
                                      "norm1")
    top = saved[-1]
    loss, dx, dr2, dg2, dfw = _final_loss(top["xmid"], top["r2"], mods[-1][5:6], final_w, tgt, "final_loss")

    grads = [None] * DEPTH
    for l in reversed(range(DEPTH)):
        sm, w, md, sv = small[l], big[l], mods[l], saved[l]
        dact = _matmul(dr2, w["down"], "nt", BF16, f"d_act")
        g_down = _matmul(dr2, sv["act"], "tn", F32, "dw_down").T
        dug, duv, dwg, dwv = _ffn_act_bwd(sv["u"], dact, sm["conv_f_w"], f"ffn_act_bwd")
        du = jnp.concatenate([dug, duv], axis=1)
        dh2 = _matmul(du, w["up"], "nt", F32, f"d_h2")
        g_up = _matmul(sv["h2"], du, "tn", F32, f"dw_up")
        dxmid, dr1, dsh2, dsc2, dnw2, dg1 = _norm_bwd(sv["xmid"], [dh2], dx, sm["norm2_w"], md[4:5], md[2:3], sv["r1"],
                                                     f"norm2_bwd")
        dmix_a = _matmul(dr1, w["out_a"], "nt", F32, f"d_mix_a")
        dmix_b = _matmul(dr1, w["out_b"], "nt", F32, f"d_mix_b")
        dmix_c = _matmul(dr1, w["out_c"], "nt", F32, f"d_mix_c")
        g_out = _matmul(sv["mixed"], dr1, "tn", F32, f"dw_out")
        dc, dlnw, dlnb, dcb = _conv_mixer_bwd_ln(sv["cv"], dmix_a, sm["ln_a_w"], sm["ln_a_b"], f"conv_mixer_bwd_ln")
        dpa, dcw = _conv_mixer_bwd_conv(sv["pa"], dc, sm["conv_a_w"], f"conv_mixer_bwd_conv")
        delta = _attn_delta(dmix_b, sv["att32"], "attn_delta")
        dqkv = None
        for d in DILATIONS:
            dqkv = _attn_branch_bwd(sv["qkv"], dmix_b, sv["lse"], delta, dqkv, d, f"attn_bwd_d{d}")
        dot, dgt, dwn = _hgrn_out_bwd(dmix_c, sv["of"], sv["ob"], sv["projt"], sv["wn"], f"hgrn_out_bwd")
        dqf, dzf, dvf, dlbf = _hgrn_scan_bwd(sv["projt"], sv["lb_f"], dot, sv["hsf"], sv["atf"], None, False,
                                             "hgrn_fwd_bwd")
        dqt, dzb, dvt, dlbb = _hgrn_scan_bwd(sv["projt"], sv["lb_b"], dot, sv["hsb"], sv["atb"], (dqf, dvf), True,
                                             "hgrn_rev_bwd")
        dprojt = jnp.concatenate([dqt, dzf, dzb, dvt, dgt], axis=0).astype(BF16)
        dnat = jnp.concatenate([dpa] + [t.astype(BF16) for t in dqkv], axis=1)
        dh1_a = _matmul(dnat, w["nat"], "nt", F32, f"d_h1_nat")
        dh1_b = _matmul(dprojt, w["rec_t"], "tn", F32, f"d_h1_rec")
        g_in_nat = _matmul(sv["h1"], dnat, "tn", F32, f"dw_in_nat")
        g_in_rec_t = _matmul(dprojt, sv["h1"], "nn", F32, f"dw_in_rec")
        g_in = jnp.concatenate([g_in_nat, g_in_rec_t.T], axis=1)
        if l > 0:
            below = saved[l - 1]
            dx, dr2, dsh1, dsc1, dnw1, dg2_below = _norm_bwd(sv["xin"], [dh1_a, dh1_b], dxmid, sm["norm1_w"], md[1:2],
                                                            mods[l - 1][5:6], below["r2"], f"norm1_bwd")
        else:
            dx, dsh1, dsc1, dnw1 = _norm_bwd(sv["xin"], [dh1_a, dh1_b], dxmid, sm["norm1_w"], md[1:2], None, None,
                                             f"norm1_bwd")
        grads[l] = dict(w_in=g_in, w_out=g_out, w_up=g_up, w_down=g_down,
                        mod=[dsh1, dsc1, dg1, dsh2, dsc2, dg2], norm1_w=dnw1, conv_a_w=dcw[:CONV_WIDTH], conv_a_b=dcb,
                        ln_a_w=dlnw, ln_a_b=dlnb, lb=jnp.concatenate([dlbf.reshape(1, -1), dlbb.reshape(1, -1)], axis=0),
                        rec_norm_w=dwn.reshape(1, -1), norm2_w=dnw2,
                        conv_f_w=jnp.concatenate([dwg[:3], dwv[:3]], axis=1))
        if l > 0:
            dg2 = dg2_below
    return loss[0, 0], dx, grads, dfw


def _adamw_math(w, g, m, v):
    m = ADAM_B1 * m + (1.0 - ADAM_B1) * g
    v = ADAM_B2 * v + (1.0 - ADAM_B2) * (g * g)
    m_hat = m / (1.0 - ADAM_B1 ** ADAM_STEP)
    v_hat = v / (1.0 - ADAM_B2 ** ADAM_STEP)
    delta = -ADAM_LR * (m_hat / (jnp.sqrt(v_hat) + ADAM_EPS) + ADAM_WD * w)
    return delta, m, v


def _row_tile(rows, cols, max_elems=384 * 1024):
    best = None
    for t in range(8, rows + 1, 8):
        if rows % t == 0 and t * cols <= max_elems:
            best = t
    return best or rows


def _adamw(w, g, m, v, name):
    nl, r, c = w.shape
    tr = _row_tile(r, c)

    def body(w_ref, g_ref, m_ref, v_ref, d_ref, m2_ref, v2_ref):
        d_ref[...], m2_ref[...], v2_ref[...] = _adamw_math(w_ref[...], g_ref[...], m_ref[...], v_ref[...])

    blk = pl.BlockSpec((None, tr, c), lambda l, i: (l, i, 0))
    shape = jax.ShapeDtypeStruct((nl, r, c), F32)
    return pl.pallas_call(body, name=name, grid=(nl, r // tr), in_specs=[blk] * 4, out_specs=(blk, blk, blk),
                          out_shape=(shape, shape, shape), compiler_params=_params(2))(w, g, m, v)


ADA_SHARD = N_MOD * D_MODEL // 4
ADA_COLS = 512
ADA_ROWS = 256
HIGHEST = lax.Precision.HIGHEST


def _ada_mod(c_all, w_ada, b_sh, name):
    def body(c_ref, w_ref, b_ref, o_ref):
        cv = c_ref[...]
        o_ref[...] = jnp.dot(cv * _sigmoid(cv), w_ref[...], precision=HIGHEST, preferred_element_type=F32) + b_ref[...]

    return pl.pallas_call(
        body, name=name, grid=(DEPTH, ADA_SHARD // ADA_COLS),
        in_specs=[pl.BlockSpec((8, D_MODEL), lambda l, j: (0, 0)),
                  pl.BlockSpec((None, D_MODEL, ADA_COLS), lambda l, j: (l, 0, j)),
                  pl.BlockSpec((None, 1, ADA_COLS), lambda l, j: (l, 0, j))],
        out_specs=pl.BlockSpec((None, 8, ADA_COLS), lambda l, j: (l, 0, j)),
        out_shape=jax.ShapeDtypeStruct((DEPTH, 8, ADA_SHARD), F32), compiler_params=_params(2))(c_all, w_ada, b_sh)


def _ada_update(c_all, dmod_sh, w, m, v, name):
    def body(c_ref, d_ref, w_ref, m_ref, v_ref, g_ref, dl_ref, m2_ref, v2_ref):
        cv = c_ref[...]
        g = lax.dot_general(cv * _sigmoid(cv), d_ref[...], (((0,), (0,)), ((), ())), precision=HIGHEST,
                            preferred_element_type=F32)
        g_ref[...] = g
        dl_ref[...], m2_ref[...], v2_ref[...] = _adamw_math(w_ref[...], g, m_ref[...], v_ref[...])

    blk = pl.BlockSpec((None, ADA_ROWS, ADA_SHARD), lambda l, i: (l, i, 0))
    shape = jax.ShapeDtypeStruct((DEPTH, D_MODEL, ADA_SHARD), F32)
    return pl.pallas_call(
        body, name=name, grid=(DEPTH, D_MODEL // ADA_ROWS),
        in_specs=[pl.BlockSpec((8, ADA_ROWS), lambda l, i: (0, i)),
                  pl.BlockSpec((None, 8, ADA_SHARD), lambda l, i: (l, 0, 0)), blk, blk, blk],
        out_specs=(blk,) * 4, out_shape=(shape,) * 4, compiler_params=_params(2))(c_all, dmod_sh, w, m, v)


def _sum_devices(packs, name):
    def body(p_ref, o_ref):
        acc = p_ref[0]
        for dev in range(1, 8):
            acc = acc + p_ref[dev]
        o_ref[...] = acc

    return pl.pallas_call(body, name=name, out_shape=jax.ShapeDtypeStruct(packs.shape[1:], F32))(packs)


def _mesh_pos():
    return lax.axis_index("x"), lax.axis_index("y"), lax.axis_index("c")


def _flip(v, bit):
    return 1 - v if bit else v


def _allgather_devices(x, name):
    m_per, n = x.shape

    def body(x_ref, out_ref, send_sems, recv_sems, local_sem):
        ix, iy, ic = _mesh_pos()
        me, sibling = (ix, iy, ic), (ix, iy, 1 - ic)
        chips = [(1 - ix, iy), (ix, 1 - iy), (1 - ix, 1 - iy)]

        def rows(px, py, pc):
            return out_ref.at[pl.ds((4 * px + 2 * py + pc) * m_per, m_per), :]

        def copy(k, block, to, src=None):
            return pltpu.make_async_remote_copy(
                src_ref=rows(*block) if src is None else src, dst_ref=rows(*block),
                send_sem=send_sems.at[k], recv_sem=recv_sems.at[k], device_id=to, device_id_type=MESH)

        mine = pltpu.make_async_copy(x_ref, rows(*me), local_sem)
        mine.start()
        first = [copy(0, me, sibling, src=x_ref)]
        first += [copy(1 + j, me, (*chip, ic), src=x_ref) for j, chip in enumerate(chips)]
        for cp in first:
            cp.start()
        passed = [copy(4 + j, (*chip, ic), sibling) for j, chip in enumerate(chips)]
        for j, chip in enumerate(chips):
            copy(1 + j, (*chip, ic), me).wait_recv()
            passed[j].start()
        copy(0, sibling, me).wait_recv()
        for j, chip in enumerate(chips):
            copy(4 + j, (*chip, 1 - ic), me).wait_recv()
        for cp in first + passed:
            cp.wait_send()
        mine.wait()

    return pl.pallas_call(
        body, name=name, out_shape=jax.ShapeDtypeStruct((8 * m_per, n), x.dtype),
        in_specs=[pl.BlockSpec(memory_space=pltpu.VMEM)], out_specs=pl.BlockSpec(memory_space=pltpu.VMEM),
        scratch_shapes=[pltpu.SemaphoreType.DMA((7,)), pltpu.SemaphoreType.DMA((7,)), pltpu.SemaphoreType.DMA],
    )(x)


def _gather_chips(shards, name):
    n = len(shards)

    def body(*refs):
        ins, outs = refs[:n], refs[n:2 * n]
        send_sems, recv_sems, local_sems = refs[2 * n:]
        ix, iy, ic = _mesh_pos()
        me = 2 * ix + iy
        local = [pltpu.make_async_copy(ins[a], outs[a].at[me], local_sems.at[a]) for a in range(n)]
        for cp in local:
            cp.start()
        remote = []
        for a in range(n):
            for k in (1, 2, 3):
                px, py = _flip(ix, k & 2), _flip(iy, k & 1)
                sems = dict(send_sem=send_sems.at[3 * a + k - 1], recv_sem=recv_sems.at[3 * a + k - 1],
                            device_id=(px, py, ic), device_id_type=MESH)
                out_cp = pltpu.make_async_remote_copy(src_ref=ins[a], dst_ref=outs[a].at[me], **sems)
                in_cp = pltpu.make_async_remote_copy(src_ref=ins[a], dst_ref=outs[a].at[2 * px + py], **sems)
                out_cp.start()
                remote.append((out_cp, in_cp))
        for out_cp, in_cp in remote:
            out_cp.wait_send()
            in_cp.wait_recv()
        for cp in local:
            cp.wait()

    return pl.pallas_call(
        body, name=name, in_specs=[ANY] * n, out_specs=tuple([ANY] * n),
        out_shape=tuple(jax.ShapeDtypeStruct((4,) + t.shape, t.dtype) for t in shards),
        scratch_shapes=[pltpu.SemaphoreType.DMA((3 * n,)), pltpu.SemaphoreType.DMA((3 * n,)),
                        pltpu.SemaphoreType.DMA((n,))],
    )(*shards)


HBM = pl.BlockSpec(memory_space=pltpu.HBM)
SEM = pl.BlockSpec(memory_space=pltpu.SEMAPHORE)
DATAFLOW = pltpu.SideEffectType.DATAFLOW_SIDE_EFFECTING


def _peer_chip(ix, iy, k):
    return _flip(ix, k & 2), _flip(iy, k & 1)


def _gather_chips_start(shards, name):
    n = len(shards)

    def body(*refs):
        src, land = refs[:n], refs[n:2 * n]
        send_sems, recv_sems = refs[2 * n], refs[2 * n + 1]
        token = refs[-1]
        ix, iy, ic = _mesh_pos()
        me = 2 * ix + iy
        for a in range(n):
            for k in (1, 2, 3):
                px, py = _peer_chip(ix, iy, k)
                pltpu.make_async_remote_copy(
                    src_ref=src[a], dst_ref=land[a].at[me], send_sem=send_sems.at[3 * a + k - 1],
                    recv_sem=recv_sems.at[3 * a + k - 1], device_id=(px, py, ic), device_id_type=MESH).start()
        token[...] = jnp.zeros_like(token)

    hbm = lambda shape, dtype: pltpu.HBM(shape, dtype)
    operands = ([pltpu.with_memory_space_constraint(t, pltpu.HBM) for t in shards]
                + [pltpu.with_memory_space_constraint(lax.empty((4,) + t.shape, t.dtype), pltpu.HBM) for t in shards])
    return pl.pallas_call(
        body, name=name,
        out_shape=(pltpu.SemaphoreType.DMA((3 * n,)), pltpu.SemaphoreType.DMA((3 * n,)),
                   *[hbm(t.shape, t.dtype) for t in shards], *[hbm((4,) + t.shape, t.dtype) for t in shards],
                   jax.ShapeDtypeStruct((8, LANES), F32)),
        in_specs=(HBM,) * (2 * n),
        out_specs=(SEM, SEM) + (HBM,) * (2 * n) + (pl.BlockSpec(memory_space=pltpu.VMEM),),
        input_output_aliases={a: 2 + a for a in range(2 * n)},
        compiler_params=pltpu.CompilerParams(has_side_effects=DATAFLOW),
    )(*operands)


def _gather_chips_wait(started, after, name):
    send_sems, recv_sems = started[0], started[1]
    thru = started[2:-1]
    n = len(thru) // 2

    def body(*refs):
        src, land = refs[:n], refs[n:2 * n]
        send_sems, recv_sems = refs[2 * n], refs[2 * n + 1]
        ix, iy, ic = _mesh_pos()
        for a in range(n):
            for k in (1, 2, 3):
                px, py = _peer_chip(ix, iy, k)
                cp = pltpu.make_async_remote_copy(
                    src_ref=src[a], dst_ref=land[a].at[2 * px + py], send_sem=send_sems.at[3 * a + k - 1],
                    recv_sem=recv_sems.at[3 * a + k - 1], device_id=(px, py, ic), device_id_type=MESH)
                cp.wait_send()
                cp.wait_recv()

    outs = pl.pallas_call(
        body, name=name,
        out_shape=tuple(pltpu.HBM(t.shape, t.dtype) for t in thru),
        in_specs=(HBM,) * (2 * n) + (SEM, SEM, ANY), out_specs=(HBM,) * (2 * n),
        input_output_aliases={a: a for a in range(2 * n)},
        compiler_params=pltpu.CompilerParams(has_side_effects=DATAFLOW),
    )(*thru, send_sems, recv_sems, after)
    return outs[:n], outs[n:]


BIG_KINDS = (("w_in", "col", D_MODEL, IN_COLS), ("w_out", "row", D_MODEL, D_MODEL),
             ("w_up", "col", D_MODEL, 2 * D_FF), ("w_down", "row", D_FF, D_MODEL))


def _piece_shape(how, r, c):
    return (r // 2, c // 4) if how == "col" else (r // 8, c)


def _aligned(start, multiple):
    return start if isinstance(start, int) else pl.multiple_of(start, multiple)


def _piece(ref, how, r, c, chip, half):
    if how == "col":
        return ref.at[pl.ds(_aligned(half * (r // 2), 8), r // 2), pl.ds(_aligned(chip * (c // 4), LANES), c // 4)]
    n = r // 4
    return ref.at[pl.ds(_aligned(chip * n + half * (n // 2), 8), n // 2), :]


def _rs_pair_exchange(grads, name):
    nk = len(BIG_KINDS)
    flat = [grads[ki][l] for ki in range(nk) for l in range(DEPTH)]
    per = DEPTH * 4

    def body(*refs):
        g, land = refs[:nk * DEPTH], refs[nk * DEPTH:nk * DEPTH + nk]
        send_sems, recv_sems = refs[nk * DEPTH + nk:]
        ix, iy, ic = _mesh_pos()
        sibling = (ix, iy, 1 - ic)
        copies = []
        for ki, (_, how, r, c) in enumerate(BIG_KINDS):
            for l in range(DEPTH):
                for j in range(4):
                    sem = ki * per + l * 4 + j
                    rem = pltpu.make_async_remote_copy(
                        src_ref=_piece(g[ki * DEPTH + l], how, r, c, j, 1 - ic), dst_ref=land[ki].at[l, j],
                        send_sem=send_sems.at[sem], recv_sem=recv_sems.at[sem], device_id=sibling, device_id_type=MESH)
                    rem.start()
                    copies.append(rem)
        for rem in copies:
            rem.wait_send()
            rem.wait_recv()

    shapes = [jax.ShapeDtypeStruct((DEPTH, 4) + _piece_shape(how, r, c), F32) for _, how, r, c in BIG_KINDS]
    return pl.pallas_call(
        body, name=name, in_specs=[ANY] * len(flat), out_specs=tuple([ANY] * nk), out_shape=tuple(shapes),
        scratch_shapes=[pltpu.SemaphoreType.DMA((nk * per,))] * 2,
    )(*flat)


def _pair_sum(g, theirs, layer, how, core, name):
    r, c = g.shape
    pr, pc = _piece_shape(how, r, c)
    if how == "col":
        mine_spec = pl.BlockSpec((pr, pc), lambda j, core_ref: (core_ref[0], j))
    else:
        mine_spec = pl.BlockSpec((pr, pc), lambda j, core_ref: (2 * j + core_ref[0], 0))

    def body(core_ref, g_ref, t_ref, o_ref, ob_ref):
        total = g_ref[...] + t_ref[...]
        o_ref[...] = total
        ob_ref[...] = total.astype(BF16)

    out_blk = pl.BlockSpec((None, pr, pc), lambda j, core_ref: (j, 0, 0))
    return pl.pallas_call(
        body, name=name,
        grid_spec=pltpu.PrefetchScalarGridSpec(
            num_scalar_prefetch=1, grid=(4,),
            in_specs=[mine_spec, pl.BlockSpec((None, None, pr, pc), lambda j, core_ref: (layer, j, 0, 0))],
            out_specs=(out_blk, out_blk)),
        out_shape=(jax.ShapeDtypeStruct((4, pr, pc), F32), jax.ShapeDtypeStruct((4, pr, pc), BF16)),
        compiler_params=_params(1))(core, g, theirs)


def _rs_chip_exchange(pair_sums, name):
    nk = len(pair_sums)
    flat = [pair_sums[ki][l] for ki in range(nk) for l in range(DEPTH)]

    def body(*refs):
        src, dst = refs[:nk * DEPTH], refs[nk * DEPTH:nk * DEPTH + nk]
        send_sems, recv_sems = refs[nk * DEPTH + nk:]
        ix, iy, ic = _mesh_pos()
        copies = []
        for ki in range(nk):
            for l in range(DEPTH):
                for k in (1, 2, 3):
                    px, py = _flip(ix, k & 2), _flip(iy, k & 1)
                    sem = (ki * DEPTH + l) * 3 + k - 1
                    rem = pltpu.make_async_remote_copy(
                        src_ref=src[ki * DEPTH + l].at[2 * px + py], dst_ref=dst[ki].at[l, k - 1],
                        send_sem=send_sems.at[sem], recv_sem=recv_sems.at[sem], device_id=(px, py, ic), device_id_type=MESH)
                    rem.start()
                    copies.append(rem)
        for rem in copies:
            rem.wait_send()
            rem.wait_recv()

    return pl.pallas_call(
        body, name=name, in_specs=[ANY] * len(flat), out_specs=tuple([ANY] * nk),
        out_shape=tuple(jax.ShapeDtypeStruct((DEPTH, 3) + pair_sums[ki][0].shape[1:], pair_sums[ki][0].dtype)
                        for ki in range(nk)),
        scratch_shapes=[pltpu.SemaphoreType.DMA((nk * DEPTH * 3,))] * 2,
    )(*flat)


def _chip_sum(own, others, layer, chip, name):
    _, pr, pc = own.shape

    def body(chip_ref, own_ref, s1, s2, s3, o_ref):
        o_ref[...] = ((own_ref[...] + s1[...].astype(F32)) + s2[...].astype(F32)) + s3[...].astype(F32)

    slot = lambda k: pl.BlockSpec((None, None, pr, pc), lambda i, chip_ref: (layer, k, 0, 0))
    return pl.pallas_call(
        body, name=name,
        grid_spec=pltpu.PrefetchScalarGridSpec(
            num_scalar_prefetch=1, grid=(1,),
            in_specs=[pl.BlockSpec((None, pr, pc), lambda i, chip_ref: (chip_ref[0], 0, 0)), slot(0), slot(1), slot(2)],
            out_specs=pl.BlockSpec((pr, pc), lambda i, chip_ref: (0, 0))),
        out_shape=jax.ShapeDtypeStruct((pr, pc), F32), compiler_params=_params(1))(chip, own, others, others, others)


def _rs_pair_share(halves, name):
    nk = len(halves)
    flat = [halves[ki][l] for ki in range(nk) for l in range(DEPTH)]

    def body(*refs):
        src, dst = refs[:nk * DEPTH], refs[nk * DEPTH:nk * DEPTH + nk]
        send_sems, recv_sems = refs[nk * DEPTH + nk:]
        ix, iy, ic = _mesh_pos()
        copies = []
        for ki in range(nk):
            for l in range(DEPTH):
                sem = ki * DEPTH + l
                rem = pltpu.make_async_remote_copy(
                    src_ref=src[sem], dst_ref=dst[ki].at[l], send_sem=send_sems.at[sem], recv_sem=recv_sems.at[sem],
                    device_id=(ix, iy, 1 - ic), device_id_type=MESH)
                rem.start()
                copies.append(rem)
        for rem in copies:
            rem.wait_send()
            rem.wait_recv()

    return pl.pallas_call(
        body, name=name, in_specs=[ANY] * len(flat), out_specs=tuple([ANY] * nk),
        out_shape=tuple(jax.ShapeDtypeStruct((DEPTH,) + halves[ki][0].shape, F32) for ki in range(nk)),
        scratch_shapes=[pltpu.SemaphoreType.DMA((nk * DEPTH,))] * 2,
    )(*flat)


def _adamw_halves(w, mine, theirs, m, v, core, name):
    nl, pr, pc = theirs.shape
    shape = w.shape
    view = lambda t: t.reshape(nl, 2, pr, pc)
    tr = _row_tile(pr, pc, 256 * 1024)

    def body(core_ref, w_ref, a0_ref, a1_ref, t_ref, m_ref, v_ref, g_ref, d_ref, m2_ref, v2_ref):
        own = jnp.where(pl.program_id(0) == 0, a0_ref[...], a1_ref[...])
        g = jnp.where(pl.program_id(1) == core_ref[0], own, t_ref[...])
        g_ref[...] = g
        d_ref[...], m2_ref[...], v2_ref[...] = _adamw_math(w_ref[...], g, m_ref[...], v_ref[...])

    blk = pl.BlockSpec((None, None, tr, pc), lambda l, h, i, core_ref: (l, h, i, 0))
    own_blk = pl.BlockSpec((tr, pc), lambda l, h, i, core_ref: (i, 0))
    out = jax.ShapeDtypeStruct((nl, 2, pr, pc), F32)
    outs = pl.pallas_call(
        body, name=name,
        grid_spec=pltpu.PrefetchScalarGridSpec(
            num_scalar_prefetch=1, grid=(nl, 2, pr // tr),
            in_specs=[blk, own_blk, own_blk, pl.BlockSpec((None, tr, pc), lambda l, h, i, core_ref: (l, i, 0)), blk, blk],
            out_specs=(blk,) * 4),
        out_shape=(out,) * 4, compiler_params=_params(3),
    )(core, view(w), mine[0], mine[1], theirs, view(m), view(v))
    return tuple(t.reshape(shape) for t in outs)


def _reduce_scatter_big(grads, core, chip):
    theirs = _rs_pair_exchange(grads, "rs_pair_exchange")
    pair_sums = [[_pair_sum(grads[ki][l], theirs[ki], l, how, core, f"rs_pair_sum_{kind}") for l in range(DEPTH)]
                 for ki, (kind, how, _, _) in enumerate(BIG_KINDS)]
    slots = _rs_chip_exchange([[both[1] for both in row] for row in pair_sums], "rs_chip_exchange")
    halves = [[_chip_sum(pair_sums[ki][l][0], slots[ki], l, chip, f"rs_chip_sum_{kind}") for l in range(DEPTH)]
              for ki, (kind, _, _, _) in enumerate(BIG_KINDS)]
    other = _rs_pair_share(halves, "rs_pair_share")
    return list(zip(halves, other))


WEIGHT_NAMES = ("w_ada", "b_ada", "norm1_w", "w_in", "conv_a_w", "conv_a_b", "ln_a_w", "ln_a_b", "lb_gamma",
                "rec_norm_w", "w_out", "norm2_w", "w_up", "conv_f_w", "w_down", "final_norm_w")
SMALL_PARAMS = (("b_ada", (DEPTH, N_MOD * D_MODEL), None), ("norm1_w", (DEPTH, D_MODEL), None),
                ("conv_a_w", (DEPTH, CONV_WIDTH, CONV_CH), 2), ("conv_a_b", (DEPTH, CONV_CH), None),
                ("ln_a_w", (DEPTH, CONV_CH), None), ("ln_a_b", (DEPTH, CONV_CH), None),
                ("lb_gamma", (DEPTH, 2, REC_WIDTH), 2), ("rec_norm_w", (DEPTH, REC_WIDTH), None),
                ("norm2_w", (DEPTH, D_MODEL), None), ("conv_f_w", (DEPTH, 3, 2 * D_FF), 2),
                ("final_norm_w", (D_MODEL,), None))


def _pack_rows(parts):
    flat = jnp.concatenate([p.reshape(-1) for p in parts])
    total = flat.shape[0]
    padded = -(-total // (8 * LANES)) * (8 * LANES)
    return jnp.pad(flat, (0, padded - total)).reshape(padded // LANES, LANES)


def _unpack(flat, shapes):
    out, off = [], 0
    for shp in shapes:
        size = int(np.prod(shp))
        out.append(flat[off:off + size].reshape(shp))
        off += size
    return out


def _unstack_chips(t, axis):
    return jnp.concatenate([t[j] for j in range(4)], axis=axis)


def kernel(x, c, w_ada, b_ada, norm1_w, w_in, conv_a_w, conv_a_b, ln_a_w, ln_a_b, lb_gamma, rec_norm_w, w_out, norm2_w, w_up, conv_f_w, w_down, final_norm_w, loss_target, m_w_ada, m_b_ada, m_norm1_w, m_w_in, m_conv_a_w, m_conv_a_b, m_ln_a_w, m_ln_a_b, m_lb_gamma, m_rec_norm_w, m_w_out, m_norm2_w, m_w_up, m_conv_f_w, m_w_down, m_final_norm_w, v_w_ada, v_b_ada, v_norm1_w, v_w_in, v_conv_a_w, v_conv_a_b, v_ln_a_w, v_ln_a_b, v_lb_gamma, v_rec_norm_w, v_w_out, v_norm2_w, v_w_up, v_conv_f_w, v_w_down, v_final_norm_w):
    params = dict(zip(WEIGHT_NAMES, (w_ada, b_ada, norm1_w, w_in, conv_a_w, conv_a_b, ln_a_w, ln_a_b, lb_gamma,
                                     rec_norm_w, w_out, norm2_w, w_up, conv_f_w, w_down, final_norm_w)))
    mom1 = dict(zip(WEIGHT_NAMES, (m_w_ada, m_b_ada, m_norm1_w, m_w_in, m_conv_a_w, m_conv_a_b, m_ln_a_w, m_ln_a_b,
                                   m_lb_gamma, m_rec_norm_w, m_w_out, m_norm2_w, m_w_up, m_conv_f_w, m_w_down,
                                   m_final_norm_w)))
    mom2 = dict(zip(WEIGHT_NAMES, (v_w_ada, v_b_ada, v_norm1_w, v_w_in, v_conv_a_w, v_conv_a_b, v_ln_a_w, v_ln_a_b,
                                   v_lb_gamma, v_rec_norm_w, v_w_out, v_norm2_w, v_w_up, v_conv_f_w, v_w_down,
                                   v_final_norm_w)))
    ix, iy, ic = _mesh_pos()
    chip = 2 * ix + iy
    dev = 2 * chip + ic

    c_all = _allgather_devices(c.reshape(8, LANES), "gather_cond").reshape(8, D_MODEL)
    b_sh = lax.dynamic_slice_in_dim(b_ada, chip * ADA_SHARD, ADA_SHARD, axis=1)
    mod_sh = _ada_mod(c_all, w_ada, b_sh.reshape(DEPTH, 1, ADA_SHARD), "ada_mod")
    w_in_b, w_out_b, w_up_b, w_down_b = (t.astype(BF16) for t in (w_in, w_out, w_up, w_down))
    first = _gather_chips([mod_sh, conv_a_w, conv_f_w, lb_gamma, w_in_b[0]], "gather_first")
    later = [w_in_b[1], w_out_b, w_up_b, w_down_b]
    started = _gather_chips_start(later, "gather_rest_start")
    mod_mine = lax.dynamic_index_in_dim(first[0], dev, axis=2, keepdims=False) + started[-1][0, 0]
    mods = [jnp.concatenate([mod_mine[j, l] for j in range(4)]).reshape(N_MOD, D_MODEL) for l in range(DEPTH)]
    conv_a_w_f, conv_f_w_f, gamma_f = (_unstack_chips(first[k], 2) for k in (1, 2, 3))
    w_in0 = _unstack_chips(first[4], 1)

    def later_weights(after):
        own, lands = _gather_chips_wait(started, after, "gather_rest_wait")
        full = [lax.dynamic_update_index_in_dim(land, mine, chip, 0) for land, mine in zip(lands, own)]
        return (_unstack_chips(full[0], 1), _unstack_chips(full[1], 1), _unstack_chips(full[2], 2),
                _unstack_chips(full[3], 1))

    lb1, p_soft = _lower_bounds(gamma_f.reshape(DEPTH, 2 * REC_WIDTH), "lower_bounds")
    lbs = [jnp.zeros((2, REC_WIDTH), F32), lb1.reshape(2, REC_WIDTH)]
    small = []
    for l in range(DEPTH):
        small.append(dict(norm1_w=norm1_w[l][None], conv_a_w=conv_a_w_f[l], conv_a_b=conv_a_b[l][None],
                          ln_a_w=ln_a_w[l][None], ln_a_b=ln_a_b[l][None], rec_norm_w=rec_norm_w[l],
                          norm2_w=norm2_w[l][None], conv_f_w=conv_f_w_f[l]))

    loss, dx, grads, dfw = _sequence_step(x[0], loss_target[0], mods, lbs, small, w_in0, later_weights,
                                          final_norm_w[None])
    loss = lax.psum(loss, ("x", "y", "c"))

    dgamma = _lower_bounds_bwd(grads[1]["lb"].reshape(1, 2 * REC_WIDTH), p_soft, "lower_bounds_bwd")
    dmod = [jnp.concatenate(grads[l]["mod"], axis=1) for l in range(DEPTH)]
    stack = lambda key: jnp.stack([grads[l][key] for l in range(DEPTH)])
    local_small = dict(b_ada=jnp.concatenate(dmod, axis=0), norm1_w=stack("norm1_w"), conv_a_w=stack("conv_a_w"),
                       conv_a_b=stack("conv_a_b"), ln_a_w=stack("ln_a_w"), ln_a_b=stack("ln_a_b"), lb_gamma=dgamma,
                       rec_norm_w=stack("rec_norm_w"), norm2_w=stack("norm2_w"), conv_f_w=stack("conv_f_w"),
                       final_norm_w=dfw)
    pack = _pack_rows([local_small[name] for name, _, _ in SMALL_PARAMS])
    rows = pack.shape[0]
    packs = _allgather_devices(pack, "gather_small_grads").reshape(8, rows, LANES)
    summed = _sum_devices(packs, "sum_small_grads").reshape(-1)
    small_grads = dict(zip([n for n, _, _ in SMALL_PARAMS], _unpack(summed, [shp for _, shp, _ in SMALL_PARAMS])))

    dmod_all = packs.reshape(8, rows * LANES)[:, :DEPTH * N_MOD * D_MODEL].reshape(8, DEPTH, N_MOD * D_MODEL)
    dmod_sh = lax.dynamic_slice_in_dim(dmod_all, chip * ADA_SHARD, ADA_SHARD, axis=2).transpose(1, 0, 2)
    g_ada, d_ada, m_ada, v_ada = _ada_update(c_all, dmod_sh, w_ada, m_w_ada, v_w_ada, "ada_update")

    for name, shp, axis in SMALL_PARAMS:
        if axis is not None:
            width = shp[axis] // 4
            small_grads[name] = lax.dynamic_slice_in_dim(small_grads[name], chip * width, width, axis=axis)
    names = [n for n, _, _ in SMALL_PARAMS]
    packed = [_pack_rows([src[n] for n in names])[None] for src in (params, small_grads, mom1, mom2)]
    small_out = _adamw(*packed, "adamw_small")
    shapes = [params[n].shape for n in names]
    small_delta, small_m, small_v = (dict(zip(names, _unpack(t.reshape(-1), shapes))) for t in small_out)

    core_id, chip_id = ic.astype(jnp.int32).reshape(1), chip.astype(jnp.int32).reshape(1)
    summed_big = _reduce_scatter_big([[grads[l][name] for l in range(DEPTH)] for name, _, _, _ in BIG_KINDS],
                                     core_id, chip_id)
    grad, delta, new_m, new_v = dict(small_grads), small_delta, small_m, small_v
    grad["w_ada"], delta["w_ada"], new_m["w_ada"], new_v["w_ada"] = g_ada, d_ada, m_ada, v_ada
    for (name, _, _, _), (mine, theirs) in zip(BIG_KINDS, summed_big):
        grad[name], delta[name], new_m[name], new_v[name] = _adamw_halves(
            params[name], mine, theirs, mom1[name], mom2[name], core_id, f"adamw_{name}")

    return (loss, dx[None], *[grad[n] for n in WEIGHT_NAMES], *[delta[n] for n in WEIGHT_NAMES],
            *[new_m[n] for n in WEIGHT_NAMES], *[new_v[n] for n in WEIGHT_NAMES])
```

```python
import numpy as np
import jax
import jax.numpy as jnp
from jax import lax
from jax.experimental import pallas as pl
from jax.experimental.pallas import tpu as pltpu

F32 = jnp.float32
BF16 = jnp.bfloat16

D_MODEL = 1024
DEPTH = 2
HEAD_DIM = 64
CONV_CH = 256
CONV_WIDTH = 31
ATT_WIDTH = 384
N_HEADS = 6
DILATIONS = (1, 4, 16)
ATT_HALF = 64
ATT_BLOCK = 128
ALIBI_MAX_EXP = 8.0
MASK_VALUE = -1e30
REC_WIDTH = 384
REC_CHUNK = 64
F_TINY = 1e-30
D_FF = 2816
N_MOD = 6
EPS = 1e-6
G_CONV = (0, 512)
G_QKV = (512, 1664)
G_REC = (1664, 3584)
IN_COLS = 3584

ADAM_LR = 0.001
ADAM_B1 = 0.9
ADAM_B2 = 0.999
ADAM_EPS = 1e-08
ADAM_WD = 0.01
ADAM_STEP = 10

VMEM_LIMIT_BYTES = 56 * 1024 * 1024
LANES = 128
MESH = pl.DeviceIdType.MESH
ANY = pl.BlockSpec(memory_space=pl.ANY)


def _params(n_axes):
    return pltpu.CompilerParams(dimension_semantics=("arbitrary",) * n_axes,
                                vmem_limit_bytes=VMEM_LIMIT_BYTES)


def _tile(n, target):
    best = None
    for t in range(LANES, min(n, target) + 1, LANES):
        if n % t == 0:
            best = t
    return best or n


def _sigmoid(x):
    return jax.nn.sigmoid(x)


def _silu_grad(x):
    s = _sigmoid(x)
    return s * (1.0 + x * (1.0 - s))


MM_ACC_ELEMS = 1536 * 1024


def _matmul(a, b, mode, out_dtype, name, tm=1024, tn=1792, tk=1792):
    if mode == "nn":
        (m, k), (k2, n) = a.shape, b.shape
    elif mode == "nt":
        (m, k), (n, k2) = a.shape, b.shape
    else:
        (k, m), (k2, n) = a.shape, b.shape
    assert k == k2, (a.shape, b.shape, mode)
    tn, tk = _tile(n, tn), _tile(k, tk)
    tm = _tile(m, min(tm, MM_ACC_ELEMS // tn))
    nk = k // tk
    a_spec = (pl.BlockSpec((tk, tm), lambda i, j, kk: (kk, i)) if mode == "tn"
              else pl.BlockSpec((tm, tk), lambda i, j, kk: (i, kk)))
    b_spec = (pl.BlockSpec((tn, tk), lambda i, j, kk: (j, kk)) if mode == "nt"
              else pl.BlockSpec((tk, tn), lambda i, j, kk: (kk, j)))
    dims = {"nn": (((1,), (0,)), ((), ())), "nt": (((1,), (1,)), ((), ())),
            "tn": (((0,), (0,)), ((), ()))}[mode]

    def body(a_ref, b_ref, o_ref, *scratch):
        part = lax.dot_general(a_ref[...].astype(BF16), b_ref[...].astype(BF16), dims, preferred_element_type=F32)
        if nk == 1:
            o_ref[...] = part.astype(out_dtype)
            return
        acc_ref, = scratch
        kk = pl.program_id(2)

        @pl.when(kk == 0)
        def _():
            acc_ref[...] = part

        @pl.when(kk > 0)
        def _():
            acc_ref[...] += part

        @pl.when(kk == nk - 1)
        def _():
            o_ref[...] = acc_ref[...].astype(out_dtype)

    return pl.pallas_call(
        body, name=name, grid=(m // tm, n // tn, nk),
        in_specs=[a_spec, b_spec],
        out_specs=pl.BlockSpec((tm, tn), lambda i, j, kk: (i, j)),
        out_shape=jax.ShapeDtypeStruct((m, n), out_dtype),
        scratch_shapes=[pltpu.VMEM((tm, tn), F32)] if nk > 1 else [],
        compiler_params=pltpu.CompilerParams(dimension_semantics=("parallel", "parallel", "arbitrary"),
                                             vmem_limit_bytes=VMEM_LIMIT_BYTES),
    )(a, b)


NORM_ROWS = 256


def _row_spec(width, rows=NORM_ROWS):
    return pl.BlockSpec((rows, width), lambda i: (i, 0))


def _vec_spec(width):
    return pl.BlockSpec((1, width), lambda i: (0, 0))


def _resid_norm_mod(x, r, g, nw, sc, sh, name):
    s, d = x.shape
    has_r = r is not None

    def body(*refs):
        if has_r:
            x_ref, r_ref, g_ref, nw_ref, sc_ref, sh_ref, xn_ref, h_ref = refs
            xn = x_ref[...] + g_ref[...] * r_ref[...].astype(F32)
            xn_ref[...] = xn
        else:
            x_ref, nw_ref, sc_ref, sh_ref, h_ref = refs
            xn = x_ref[...]
        rstd = lax.rsqrt(jnp.mean(xn * xn, axis=-1, keepdims=True) + EPS)
        y = xn * rstd * nw_ref[...]
        h_ref[...] = (y * (1.0 + sc_ref[...]) + sh_ref[...]).astype(BF16)

    if has_r:
        ins, in_specs = (x, r, g, nw, sc, sh), [_row_spec(d), _row_spec(d)] + [_vec_spec(d)] * 4
        out_shape = (jax.ShapeDtypeStruct((s, d), F32), jax.ShapeDtypeStruct((s, d), BF16))
        out_specs = (_row_spec(d), _row_spec(d))
    else:
        ins, in_specs = (x, nw, sc, sh), [_row_spec(d)] + [_vec_spec(d)] * 3
        out_shape = jax.ShapeDtypeStruct((s, d), BF16)
        out_specs = _row_spec(d)
    return pl.pallas_call(body, name=name, grid=(s // NORM_ROWS,), in_specs=in_specs, out_specs=out_specs,
                          out_shape=out_shape, compiler_params=_params(1))(*ins)


def _final_loss(x, r, g, fw, tgt, name):
    s, d = x.shape

    def body(x_ref, r_ref, g_ref, fw_ref, t_ref, loss_ref, dx_ref, dr_ref, dg_ref, dfw_ref):
        @pl.when(pl.program_id(0) == 0)
        def _():
            loss_ref[...] = jnp.zeros_like(loss_ref)
            dg_ref[...] = jnp.zeros_like(dg_ref)
            dfw_ref[...] = jnp.zeros_like(dfw_ref)

        rr = r_ref[...].astype(F32)
        gg = g_ref[...]
        xn = x_ref[...] + gg * rr
        rstd = lax.rsqrt(jnp.mean(xn * xn, axis=-1, keepdims=True) + EPS)
        xh = xn * rstd
        fwv = fw_ref[...]
        e = xh * fwv - t_ref[...]
        loss_ref[...] += 0.5 * jnp.sum(jnp.mean(e * e, axis=-1, keepdims=True), axis=0, keepdims=True)
        dy = e * (1.0 / d)
        dfw_ref[...] += jnp.sum(dy * xh, axis=0, keepdims=True)
        dxh = dy * fwv
        dx = rstd * (dxh - xh * jnp.mean(dxh * xh, axis=-1, keepdims=True))
        dx_ref[...] = dx
        dr_ref[...] = (gg * dx).astype(BF16)
        dg_ref[...] += jnp.sum(dx * rr, axis=0, keepdims=True)

    return pl.pallas_call(
        body, name=name, grid=(s // NORM_ROWS,),
        in_specs=[_row_spec(d), _row_spec(d), _vec_spec(d), _vec_spec(d), _row_spec(d)],
        out_specs=(_vec_spec(LANES), _row_spec(d), _row_spec(d), _vec_spec(d), _vec_spec(d)),
        out_shape=(jax.ShapeDtypeStruct((1, LANES), F32), jax.ShapeDtypeStruct((s, d), F32),
                   jax.ShapeDtypeStruct((s, d), BF16), jax.ShapeDtypeStruct((1, d), F32),
                   jax.ShapeDtypeStruct((1, d), F32)),
        compiler_params=_params(1))(x, r, g, fw, tgt)


def _norm_bwd(x, dhs, dxres, nw, sc, g, r, name):
    s, d = x.shape
    n_dh = len(dhs)
    has_g = g is not None

    def body(*refs):
        x_ref = refs[0]
        dh_refs = refs[1:1 + n_dh]
        dxres_ref, nw_ref, sc_ref = refs[1 + n_dh:4 + n_dh]
        pos = 4 + n_dh
        if has_g:
            g_ref, r_ref = refs[pos:pos + 2]
            pos += 2
            dx_ref, dr_ref, dsh_ref, dsc_ref, dnw_ref, dg_ref = refs[pos:]
            accs = (dsh_ref, dsc_ref, dnw_ref, dg_ref)
        else:
            dx_ref, dsh_ref, dsc_ref, dnw_ref = refs[pos:]
            accs = (dsh_ref, dsc_ref, dnw_ref)

        @pl.when(pl.program_id(0) == 0)
        def _():
            for acc in accs:
                acc[...] = jnp.zeros_like(acc)

        xv = x_ref[...]
        dh = dh_refs[0][...].astype(F32)
        for extra in dh_refs[1:]:
            dh = dh + extra[...].astype(F32)
        rstd = lax.rsqrt(jnp.mean(xv * xv, axis=-1, keepdims=True) + EPS)
        xh = xv * rstd
        nwv = nw_ref[...]
        dsh_ref[...] += jnp.sum(dh, axis=0, keepdims=True)
        dsc_ref[...] += jnp.sum(dh * (xh * nwv), axis=0, keepdims=True)
        dy = dh * (1.0 + sc_ref[...])
        dnw_ref[...] += jnp.sum(dy * xh, axis=0, keepdims=True)
        dxh = dy * nwv
        dx = dxres_ref[...] + rstd * (dxh - xh * jnp.mean(dxh * xh, axis=-1, keepdims=True))
        dx_ref[...] = dx
        if has_g:
            dr_ref[...] = (g_ref[...] * dx).astype(BF16)
            dg_ref[...] += jnp.sum(dx * r_ref[...].astype(F32), axis=0, keepdims=True)

    ins = [x, *dhs, dxres, nw, sc]
    in_specs = [_row_spec(d)] * (2 + n_dh) + [_vec_spec(d)] * 2
    out_shape = [jax.ShapeDtypeStruct((s, d), F32)]
    out_specs = [_row_spec(d)]
    if has_g:
        ins += [g, r]
        in_specs += [_vec_spec(d), _row_spec(d)]
        out_shape.append(jax.ShapeDtypeStruct((s, d), BF16))
        out_specs.append(_row_spec(d))
    n_vec = 4 if has_g else 3
    out_shape += [jax.ShapeDtypeStruct((1, d), F32)] * n_vec
    out_specs += [_vec_spec(d)] * n_vec
    return pl.pallas_call(body, name=name, grid=(s // NORM_ROWS,), in_specs=in_specs, out_specs=tuple(out_specs),
                          out_shape=tuple(out_shape), compiler_params=_params(1))(*ins)


FFN_ROWS = 256
FFN_COLS = 1408
HALO = 16
INV_SQRT2 = 0.7071067811865476
INV_SQRT_2PI = 0.3989422804014327


def _gelu(x):
    return 0.5 * x * (1.0 + lax.erf(x * INV_SQRT2))


def _gelu_grad(x):
    return 0.5 * (1.0 + lax.erf(x * INV_SQRT2)) + x * (INV_SQRT_2PI * jnp.exp(-0.5 * x * x))


def _halo_specs(rows, cols, halo, n_rows_total, col_of):
    per = rows // halo
    last = n_rows_total // halo - 1
    cur = pl.BlockSpec((rows, cols), lambda j, i: (i, col_of(j)))
    prev = pl.BlockSpec((halo, cols), lambda j, i: (jnp.maximum(i * per - 1, 0), col_of(j)))
    nxt = pl.BlockSpec((halo, cols), lambda j, i: (jnp.minimum((i + 1) * per, last), col_of(j)))
    return [prev, cur, nxt]


def _shift_rows(x, k):
    n = x.shape[0]
    return pltpu.roll(x, k % n, axis=0)


def _conv3(ext, w):
    return w[0:1, :] * _shift_rows(ext, 1) + w[1:2, :] * ext + w[2:3, :] * _shift_rows(ext, -1)


def _ext_block(prev_ref, cur_ref, next_ref, i, n_i):
    prev = jnp.where(i > 0, prev_ref[...].astype(F32), 0.0)
    nxt = jnp.where(i < n_i - 1, next_ref[...].astype(F32), 0.0)
    return jnp.concatenate([prev, cur_ref[...].astype(F32), nxt], axis=0)


def _ffn_act(u, cw, name):
    s = u.shape[0]
    nc, ns = D_FF // FFN_COLS, s // FFN_ROWS

    def body(gp, gc, gn, vp, vc, vn, wg_ref, wv_ref, o_ref):
        i = pl.program_id(1)
        cg = _conv3(_ext_block(gp, gc, gn, i, ns), wg_ref[...])[HALO:HALO + FFN_ROWS]
        cv = _conv3(_ext_block(vp, vc, vn, i, ns), wv_ref[...])[HALO:HALO + FFN_ROWS]
        o_ref[...] = (_gelu(cg) * cv).astype(BF16)

    in_specs = (_halo_specs(FFN_ROWS, FFN_COLS, HALO, s, lambda j: j)
                + _halo_specs(FFN_ROWS, FFN_COLS, HALO, s, lambda j: j + nc)
                + [pl.BlockSpec((3, FFN_COLS), lambda j, i: (0, j)),
                   pl.BlockSpec((3, FFN_COLS), lambda j, i: (0, j + nc))])
    return pl.pallas_call(
        body, name=name, grid=(nc, ns), in_specs=in_specs,
        out_specs=pl.BlockSpec((FFN_ROWS, FFN_COLS), lambda j, i: (i, j)),
        out_shape=jax.ShapeDtypeStruct((s, D_FF), BF16), compiler_params=_params(2),
    )(u, u, u, u, u, u, cw, cw)


def _ffn_act_bwd(u, dact, cw, name):
    s = u.shape[0]
    nc, ns = D_FF // FFN_COLS, s // FFN_ROWS

    def body(gp, gc, gn, vp, vc, vn, dp, dc, dn, wg_ref, wv_ref, dug_ref, duv_ref, dwg_ref, dwv_ref):
        i = pl.program_id(1)

        @pl.when(i == 0)
        def _():
            dwg_ref[...] = jnp.zeros_like(dwg_ref)
            dwv_ref[...] = jnp.zeros_like(dwv_ref)

        ug = _ext_block(gp, gc, gn, i, ns)
        uv = _ext_block(vp, vc, vn, i, ns)
        da = _ext_block(dp, dc, dn, i, ns)
        wg, wv = wg_ref[...], wv_ref[...]
        cg, cv = _conv3(ug, wg), _conv3(uv, wv)
        dcg = da * cv * _gelu_grad(cg)
        dcv = da * _gelu(cg)
        inner = slice(HALO, HALO + FFN_ROWS)
        for d_c, uu, w, du_ref, dw_ref in ((dcg, ug, wg, dug_ref, dwg_ref), (dcv, uv, wv, duv_ref, dwv_ref)):
            d_next, d_prev = _shift_rows(d_c, -1), _shift_rows(d_c, 1)
            du = w[0:1, :] * d_next + w[1:2, :] * d_c + w[2:3, :] * d_prev
            du_ref[...] = du[inner].astype(BF16)
            u_in = uu[inner]
            for tap, d_tap in enumerate((d_next, d_c, d_prev)):
                dw_ref[tap:tap + 1, :] += jnp.sum(d_tap[inner] * u_in, axis=0, keepdims=True)

    in_specs = (_halo_specs(FFN_ROWS, FFN_COLS, HALO, s, lambda j: j)
                + _halo_specs(FFN_ROWS, FFN_COLS, HALO, s, lambda j: j + nc)
                + _halo_specs(FFN_ROWS, FFN_COLS, HALO, s, lambda j: j)
                + [pl.BlockSpec((3, FFN_COLS), lambda j, i: (0, j)),
                   pl.BlockSpec((3, FFN_COLS), lambda j, i: (0, j + nc))])
    blk = pl.BlockSpec((FFN_ROWS, FFN_COLS), lambda j, i: (i, j))
    acc = pl.BlockSpec((HALO, FFN_COLS), lambda j, i: (0, j))
    return pl.pallas_call(
        body, name=name, grid=(nc, ns), in_specs=in_specs, out_specs=(blk, blk, acc, acc),
        out_shape=(jax.ShapeDtypeStruct((s, D_FF), BF16), jax.ShapeDtypeStruct((s, D_FF), BF16),
                   jax.ShapeDtypeStruct((HALO, D_FF), F32), jax.ShapeDtypeStruct((HALO, D_FF), F32)),
        compiler_params=_params(2),
    )(u, u, u, u, u, u, dact, dact, dact, cw, cw)


CONV_ROWS = 512
CONV_HALO = 16
CONV_PAD = CONV_WIDTH // 2


def _conv_halo_specs(cols, s):
    per = CONV_ROWS // CONV_HALO
    last = s // CONV_HALO - 1
    return [pl.BlockSpec((CONV_HALO, cols), lambda i: (jnp.maximum(i * per - 1, 0), 0)),
            pl.BlockSpec((CONV_ROWS, cols), lambda i: (i, 0)),
            pl.BlockSpec((CONV_HALO, cols), lambda i: (jnp.minimum((i + 1) * per, last), 0))]


def _glu_ext(pp, pc, pn, i, n_i):
    ext = _ext_block(pp, pc, pn, i, n_i)
    return ext[:, :CONV_CH] * _sigmoid(ext[:, CONV_CH:])


def _conv_mixer(pa, cw, cb, lnw, lnb, name):
    s = pa.shape[0]
    ns = s // CONV_ROWS

    def body(pp, pc, pn, cw_ref, cb_ref, lnw_ref, lnb_ref, o_ref, c_ref):
        i = pl.program_id(0)
        a = _glu_ext(pp, pc, pn, i, ns)
        acc = jnp.zeros((CONV_ROWS, CONV_CH), F32)
        for tap in range(CONV_WIDTH):
            acc = acc + cw_ref[tap:tap + 1, :] * _shift_rows(a, -(tap + 1))[:CONV_ROWS]
        cv = acc + cb_ref[...]
        c_ref[...] = cv
        mu = jnp.mean(cv, axis=-1, keepdims=True)
        xc = cv - mu
        rstd = lax.rsqrt(jnp.mean(xc * xc, axis=-1, keepdims=True) + EPS)
        y = xc * rstd * lnw_ref[...] + lnb_ref[...]
        o_ref[...] = (y * _sigmoid(y)).astype(BF16)

    vec = pl.BlockSpec((1, CONV_CH), lambda i: (0, 0))
    blk = pl.BlockSpec((CONV_ROWS, CONV_CH), lambda i: (i, 0))
    return pl.pallas_call(
        body, name=name, grid=(ns,),
        in_specs=_conv_halo_specs(2 * CONV_CH, s) + [pl.BlockSpec((CONV_WIDTH, CONV_CH), lambda i: (0, 0)), vec, vec, vec],
        out_specs=(blk, blk),
        out_shape=(jax.ShapeDtypeStruct((s, CONV_CH), BF16), jax.ShapeDtypeStruct((s, CONV_CH), F32)),
        compiler_params=_params(1))(pa, pa, pa, cw, cb, lnw, lnb)


def _conv_mixer_bwd_ln(cv, dout, lnw, lnb, name):
    s = cv.shape[0]

    def body(c_ref, do_ref, lnw_ref, lnb_ref, dc_ref, dlnw_ref, dlnb_ref, dcb_ref):
        @pl.when(pl.program_id(0) == 0)
        def _():
            dlnw_ref[...] = jnp.zeros_like(dlnw_ref)
            dlnb_ref[...] = jnp.zeros_like(dlnb_ref)
            dcb_ref[...] = jnp.zeros_like(dcb_ref)

        c = c_ref[...]
        mu = jnp.mean(c, axis=-1, keepdims=True)
        xc = c - mu
        rstd = lax.rsqrt(jnp.mean(xc * xc, axis=-1, keepdims=True) + EPS)
        xh = xc * rstd
        w = lnw_ref[...]
        y = xh * w + lnb_ref[...]
        dy = do_ref[...] * _silu_grad(y)
        dlnw_ref[...] += jnp.sum(dy * xh, axis=0, keepdims=True)
        dlnb_ref[...] += jnp.sum(dy, axis=0, keepdims=True)
        dxh = dy * w
        dc = rstd * (dxh - jnp.mean(dxh, axis=-1, keepdims=True) - xh * jnp.mean(dxh * xh, axis=-1, keepdims=True))
        dc_ref[...] = dc
        dcb_ref[...] += jnp.sum(dc, axis=0, keepdims=True)

    vec = pl.BlockSpec((1, CONV_CH), lambda i: (0, 0))
    blk = pl.BlockSpec((CONV_ROWS, CONV_CH), lambda i: (i, 0))
    return pl.pallas_call(
        body, name=name, grid=(s // CONV_ROWS,), in_specs=[blk, blk, vec, vec], out_specs=(blk, vec, vec, vec),
        out_shape=(jax.ShapeDtypeStruct((s, CONV_CH), F32),) + (jax.ShapeDtypeStruct((1, CONV_CH), F32),) * 3,
        compiler_params=_params(1))(cv, dout, lnw, lnb)


def _conv_mixer_bwd_conv(pa, dc, cw, name):
    s = pa.shape[0]
    ns = s // CONV_ROWS

    def body(pc, dp, dcc, dn, cw_ref, dpa_ref, dcw_ref):
        i = pl.program_id(0)

        @pl.when(i == 0)
        def _():
            dcw_ref[...] = jnp.zeros_like(dcw_ref)

        cur = pc[...]
        val, sg = cur[:, :CONV_CH], _sigmoid(cur[:, CONV_CH:])
        a_cur = val * sg
        dce = _ext_block(dp, dcc, dn, i, ns)
        da = jnp.zeros((CONV_ROWS, CONV_CH), F32)
        for tap in range(CONV_WIDTH):
            shifted = _shift_rows(dce, -(CONV_WIDTH - tap))[:CONV_ROWS]
            da = da + cw_ref[tap:tap + 1, :] * shifted
            dcw_ref[tap:tap + 1, :] += jnp.sum(shifted * a_cur, axis=0, keepdims=True)
        dpa_ref[:, :CONV_CH] = (da * sg).astype(BF16)
        dpa_ref[:, CONV_CH:] = (da * val * sg * (1.0 - sg)).astype(BF16)

    return pl.pallas_call(
        body, name=name, grid=(ns,),
        in_specs=[pl.BlockSpec((CONV_ROWS, 2 * CONV_CH), lambda i: (i, 0))] + _conv_halo_specs(CONV_CH, s)
        + [pl.BlockSpec((CONV_WIDTH, CONV_CH), lambda i: (0, 0))],
        out_specs=(pl.BlockSpec((CONV_ROWS, 2 * CONV_CH), lambda i: (i, 0)),
                   pl.BlockSpec((32, CONV_CH), lambda i: (0, 0))),
        out_shape=(jax.ShapeDtypeStruct((s, 2 * CONV_CH), BF16), jax.ShapeDtypeStruct((32, CONV_CH), F32)),
        compiler_params=_params(1))(pa, dc, dc, dc, cw)


SLOPES = tuple(float(2.0 ** (-ALIBI_MAX_EXP * (h + 1) / N_HEADS)) for h in range(N_HEADS))
ATT_SCALE = HEAD_DIM ** -0.5


PAIR = 2 * HEAD_DIM
N_PAIRS = N_HEADS // 2
ATT_WIN = ATT_BLOCK + 2 * ATT_HALF


ATT_GROUPS = {1: 4, 4: 1, 16: 1}


def _window_specs(dil, n_steps, col_of):
    per = 2 * ATT_GROUPS[dil]
    rows, halo = ATT_BLOCK * dil * ATT_GROUPS[dil], ATT_HALF * dil
    return [pl.BlockSpec((halo, PAIR), lambda i, p: (jnp.maximum(per * i - 1, 0), col_of(p))),
            pl.BlockSpec((rows, PAIR), lambda i, p: (i, col_of(p))),
            pl.BlockSpec((halo, PAIR), lambda i, p: (jnp.minimum(per * (i + 1), per * n_steps - 1), col_of(p)))]


def _residue(ref, r, n, dil, start=0):
    return ref[pl.ds(start * dil + r, n, stride=dil), :] if dil > 1 else ref[pl.ds(start + r, n), :]


def _store_residue(ref, r, dil, start, val):
    if dil > 1:
        ref[pl.ds(start * dil + r, val.shape[0], stride=dil), :] = val
    else:
        ref[pl.ds(start + r, val.shape[0]), :] = val


def _residue_window(refs, r, dil, g=0):
    prev, cur, nxt = refs
    groups = ATT_GROUPS[dil]
    lo = max(g * ATT_BLOCK - ATT_HALF, 0)
    hi = min((g + 1) * ATT_BLOCK + ATT_HALF, groups * ATT_BLOCK)
    parts = [_residue(prev, r, ATT_HALF, dil)] if g == 0 else []
    parts.append(_residue(cur, r, hi - lo, dil, lo))
    if g == groups - 1:
        parts.append(_residue(nxt, r, ATT_HALF, dil))
    return jnp.concatenate(parts, axis=0)


def _band_masks(i, length, dil, transposed):
    shape = (ATT_WIN, ATT_BLOCK) if transposed else (ATT_BLOCK, ATT_WIN)
    row = lax.broadcasted_iota(jnp.int32, shape, 0)
    col = lax.broadcasted_iota(jnp.int32, shape, 1)
    wide = row if transposed else col
    dist = jnp.abs((row - col - ATT_HALF) if transposed else (row + ATT_HALF - col))
    wpos = i * ATT_BLOCK - ATT_HALF + wide
    valid = (dist <= ATT_HALF) & (wpos >= 0) & (wpos < length)
    return valid, dist.astype(F32) * float(dil)


def _attn_branch(qkv, dil, name):
    s = qkv.shape[0]
    groups = ATT_GROUPS[dil]
    rows = ATT_BLOCK * dil * groups
    n_steps = s // rows
    length = s // dil
    nt = (((1,), (1,)), ((), ()))

    def body(q_ref, kp, kc, kn, vp, vc, vn, o_ref, l_ref):
        i, pair = pl.program_id(0), pl.program_id(1)
        items = [(g, r) for g in range(groups) for r in range(dil)]
        q = jnp.stack([_residue(q_ref, r, ATT_BLOCK, dil, g * ATT_BLOCK) for g, r in items]).astype(BF16)
        k = jnp.stack([_residue_window((kp, kc, kn), r, dil, g) for g, r in items]).astype(BF16)
        v = jnp.stack([_residue_window((vp, vc, vn), r, dil, g) for g, r in items]).astype(BF16)
        per_group = [_band_masks(i * groups + g, length, dil, False) for g in range(groups)]
        valid = jnp.stack([per_group[g][0] for g, _ in items]) if groups > 1 else per_group[0][0][None]
        distf = jnp.stack([per_group[g][1] for g, _ in items]) if groups > 1 else per_group[0][1][None]
        outs, lses = [], []
        for hh in range(2):
            sl = slice(hh * HEAD_DIM, (hh + 1) * HEAD_DIM)
            slope = jnp.where(pair == 0, SLOPES[hh], jnp.where(pair == 1, SLOPES[2 + hh], SLOPES[4 + hh]))
            sc = jnp.einsum("bqd,bkd->bqk", q[:, :, sl], k[:, :, sl], preferred_element_type=F32) * ATT_SCALE
            sc = jnp.where(valid, sc - slope * distf, MASK_VALUE)
            m = jnp.max(sc, axis=-1, keepdims=True)
            p = jnp.exp(sc - m)
            den = jnp.sum(p, axis=-1, keepdims=True)
            outs.append(jnp.einsum("bqk,bkd->bqd", p.astype(BF16), v[:, :, sl], preferred_element_type=F32) / den)
            lses.append(jnp.broadcast_to(m + jnp.log(den), (len(items), ATT_BLOCK, HEAD_DIM)))
        o_all, l_all = jnp.concatenate(outs, axis=2), jnp.concatenate(lses, axis=2)
        for n, (g, r) in enumerate(items):
            _store_residue(o_ref, r, dil, g * ATT_BLOCK, o_all[n])
            _store_residue(l_ref, r, dil, g * ATT_BLOCK, l_all[n])

    out_blk = pl.BlockSpec((rows, PAIR), lambda i, p: (i, p))
    return pl.pallas_call(
        body, name=name, grid=(n_steps, N_PAIRS),
        in_specs=[pl.BlockSpec((rows, PAIR), lambda i, p: (i, p))]
        + _window_specs(dil, n_steps, lambda p: N_PAIRS + p) + _window_specs(dil, n_steps, lambda p: 2 * N_PAIRS + p),
        out_specs=(out_blk, out_blk),
        out_shape=(jax.ShapeDtypeStruct((s, ATT_WIDTH), F32),) * 2,
        compiler_params=_params(2))(qkv, qkv, qkv, qkv, qkv, qkv, qkv)


ATT_ROWS = 512


def _attn_combine(outs, lses, name):
    s = outs[0].shape[0]

    def body(o1, o2, o3, l1, l2, l3, att_ref, att32_ref, lse_ref):
        ls = [l1[...], l2[...], l3[...]]
        m = jnp.maximum(jnp.maximum(ls[0], ls[1]), ls[2])
        es = [jnp.exp(l - m) for l in ls]
        den = es[0] + es[1] + es[2]
        att = (es[0] * o1[...] + es[1] * o2[...] + es[2] * o3[...]) / den
        att_ref[...] = att.astype(BF16)
        att32_ref[...] = att
        lse_ref[...] = m + jnp.log(den)

    blk = pl.BlockSpec((ATT_ROWS, ATT_WIDTH), lambda i: (i, 0))
    return pl.pallas_call(
        body, name=name, grid=(s // ATT_ROWS,), in_specs=[blk] * 6, out_specs=(blk, blk, blk),
        out_shape=(jax.ShapeDtypeStruct((s, ATT_WIDTH), BF16), jax.ShapeDtypeStruct((s, ATT_WIDTH), F32),
                   jax.ShapeDtypeStruct((s, ATT_WIDTH), F32)),
        compiler_params=_params(1))(*outs, *lses)


def _attn_delta(datt, att, name):
    s = att.shape[0]

    def body(d_ref, a_ref, delta_ref):
        prod = d_ref[...] * a_ref[...]
        for h in range(N_HEADS):
            sl = slice(h * HEAD_DIM, (h + 1) * HEAD_DIM)
            delta_ref[:, sl] = jnp.broadcast_to(jnp.sum(prod[:, sl], axis=-1, keepdims=True), (ATT_ROWS, HEAD_DIM))

    blk = pl.BlockSpec((ATT_ROWS, ATT_WIDTH), lambda i: (i, 0))
    return pl.pallas_call(
        body, name=name, grid=(s // ATT_ROWS,), in_specs=[blk, blk], out_specs=blk,
        out_shape=jax.ShapeDtypeStruct((s, ATT_WIDTH), F32), compiler_params=_params(1))(datt, att)


def _attn_branch_bwd(qkv, do, lse, delta, prev, dil, name):
    s = qkv.shape[0]
    groups = ATT_GROUPS[dil]
    rows = ATT_BLOCK * dil * groups
    n_steps = s // rows
    length = s // dil
    has_prev = prev is not None
    tn = (((0,), (0,)), ((), ()))
    nt = (((1,), (1,)), ((), ()))

    def body(*refs):
        qs, ks, vs, dos, ls, des = (refs[3 * n:3 * n + 3] for n in range(6))
        rest = refs[18:]
        if has_prev:
            pq, pk, pv = rest[:3]
            rest = rest[3:]
        dq_ref, dk_ref, dv_ref = rest
        i, pair = pl.program_id(0), pl.program_id(1)
        items = [(g, r) for g in range(groups) for r in range(dil)]
        cur = lambda t: jnp.stack([_residue(t[1], r, ATT_BLOCK, dil, g * ATT_BLOCK) for g, r in items])
        win = lambda t: jnp.stack([_residue_window(t, r, dil, g) for g, r in items])
        q_cur, k_cur, v_cur, do_cur = (cur(t).astype(BF16) for t in (qs, ks, vs, dos))
        q_win, k_win, v_win, do_win = (win(t).astype(BF16) for t in (qs, ks, vs, dos))
        l_cur, de_cur, l_win, de_win = cur(ls), cur(des), win(ls), win(des)

        def masks(transposed):
            per_group = [_band_masks(i * groups + g, length, dil, transposed) for g in range(groups)]
            if groups == 1:
                return per_group[0][0][None], per_group[0][1][None]
            return jnp.stack([per_group[g][0] for g, _ in items]), jnp.stack([per_group[g][1] for g, _ in items])

        valid_q, distf_q = masks(False)
        valid_k, distf_k = masks(True)
        dot = lambda eq, a, b: jnp.einsum(eq, a, b, preferred_element_type=F32)
        dqs, dks, dvs = [], [], []
        for hh in range(2):
            sl = slice(hh * HEAD_DIM, (hh + 1) * HEAD_DIM)
            one = slice(hh * HEAD_DIM, hh * HEAD_DIM + 1)
            slope = jnp.where(pair == 0, SLOPES[hh], jnp.where(pair == 1, SLOPES[2 + hh], SLOPES[4 + hh]))
            sc = dot("bqd,bkd->bqk", q_cur[:, :, sl], k_win[:, :, sl]) * ATT_SCALE - slope * distf_q
            p = jnp.exp(jnp.where(valid_q, sc - l_cur[:, :, one], MASK_VALUE))
            dp = dot("bqd,bkd->bqk", do_cur[:, :, sl], v_win[:, :, sl])
            ds = (p * (dp - de_cur[:, :, one]) * ATT_SCALE).astype(BF16)
            dqs.append(dot("bqk,bkd->bqd", ds, k_win[:, :, sl]))

            sc2 = dot("bqd,bkd->bqk", q_win[:, :, sl], k_cur[:, :, sl]) * ATT_SCALE - slope * distf_k
            p2 = jnp.exp(jnp.where(valid_k, sc2 - l_win[:, :, one], MASK_VALUE))
            dvs.append(dot("bqk,bqd->bkd", p2.astype(BF16), do_win[:, :, sl]))
            dp2 = dot("bqd,bkd->bqk", do_win[:, :, sl], v_cur[:, :, sl])
            ds2 = (p2 * (dp2 - de_win[:, :, one]) * ATT_SCALE).astype(BF16)
            dks.append(dot("bqk,bqd->bkd", ds2, q_win[:, :, sl]))
        for parts, acc, out in ((dqs, pq if has_prev else None, dq_ref), (dks, pk if has_prev else None, dk_ref),
                                (dvs, pv if has_prev else None, dv_ref)):
            val = jnp.concatenate(parts, axis=2)
            for n, (g, r) in enumerate(items):
                piece = val[n]
                if has_prev:
                    piece = piece + _residue(acc, r, ATT_BLOCK, dil, g * ATT_BLOCK)
                _store_residue(out, r, dil, g * ATT_BLOCK, piece)

    blk = pl.BlockSpec((rows, PAIR), lambda i, p: (i, p))
    in_specs = (_window_specs(dil, n_steps, lambda p: p) + _window_specs(dil, n_steps, lambda p: N_PAIRS + p)
                + _window_specs(dil, n_steps, lambda p: 2 * N_PAIRS + p) + _window_specs(dil, n_steps, lambda p: p) * 3)
    ins = [qkv] * 9 + [do] * 3 + [lse] * 3 + [delta] * 3
    if has_prev:
        in_specs += [blk] * 3
        ins += list(prev)
    return pl.pallas_call(
        body, name=name, grid=(n_steps, N_PAIRS), in_specs=in_specs, out_specs=(blk, blk, blk),
        out_shape=(jax.ShapeDtypeStruct((s, ATT_WIDTH), F32),) * 3,
        compiler_params=_params(2))(*ins)


TB = 2 * REC_CHUNK
REC_ROWS = 5 * REC_WIDTH


REC_LEVELS = 6


def _scan_pos(p, rev):
    p = p & (REC_CHUNK - 1)
    return (REC_CHUNK - 1 - p) if rev else p


def _split3(x):
    hi = x.astype(BF16)
    rest = x - hi.astype(F32)
    mid = rest.astype(BF16)
    return hi, mid, (rest - mid.astype(F32)).astype(BF16)


def _chunk_sums(x, rev, with_levels):
    row = lax.broadcasted_iota(jnp.int32, (TB, TB), 0)
    col = lax.broadcasted_iota(jnp.int32, (TB, TB), 1)
    same = (row < REC_CHUNK) == (col < REC_CHUNK)
    s_row, s_col = _scan_pos(row, rev), _scan_pos(col, rev)
    mats = [same & (s_row <= s_col)]
    if with_levels:
        for level in range(1, REC_LEVELS + 1):
            shift = REC_LEVELS + 1 - level
            boundary = ((s_col >> shift) << shift) + (REC_CHUNK >> level) - 1
            mats.append(same & (s_row <= boundary))
        mats.append(same)
    cat = jnp.concatenate([m.astype(BF16) for m in mats], axis=1)
    total = sum(jnp.dot(term, cat, preferred_element_type=F32) for term in _split3(x))
    return [total[:, n * TB:(n + 1) * TB] for n in range(len(mats))]


def _hg_prep(qraw, z, lb, rev):
    lane = lax.broadcasted_iota(jnp.int32, (REC_WIDTH, TB), 1)
    in_a = lane < REC_CHUNK
    scan = _scan_pos(lane, rev)
    sig, sigm = _sigmoid(z), _sigmoid(-z)
    f = lb + (1.0 - lb) * sig
    kk = (1.0 - lb) * sigm
    sums = _chunk_sums(jnp.log(jnp.maximum(f, F_TINY)), rev, True)
    b, bend = sums[0], sums[-1]
    q = qraw * _sigmoid(qraw)
    eq, ek = [], []
    for level in range(1, REC_LEVELS + 1):
        r = sums[level]
        e = jnp.exp(jnp.minimum(b - r, r - b))
        second = ((scan >> (REC_LEVELS - level)) & 1) == 1
        eq.append(jnp.where(second, e, 0.0))
        ek.append(jnp.where(second, 0.0, e))
    lanes_end = (0, REC_CHUNK) if rev else (REC_CHUNK - 1, TB - 1)
    end_a, end_b = (b[:, n:n + 1] for n in lanes_end)
    return dict(in_a=in_a, sig=sig, sigm=sigm, f=f, kk=kk, b=b, end_a=end_a, end_b=end_b,
                q=q, qh=q * jnp.exp(b), kh=kk * jnp.exp(bend - b), ekb=jnp.exp(bend - b), eq=eq, ek=ek)


def _level_masks(rev):
    row = lax.broadcasted_iota(jnp.int32, (TB, TB), 0)
    col = lax.broadcasted_iota(jnp.int32, (TB, TB), 1)
    same = (row < REC_CHUNK) == (col < REC_CHUNK)
    s_row, s_col = _scan_pos(row, rev), _scan_pos(col, rev)
    masks = [same & ((s_row >> (REC_LEVELS + 1 - level)) == (s_col >> (REC_LEVELS + 1 - level)))
             for level in range(1, REC_LEVELS + 1)]
    return masks, row == col


def _head_rows(x, h):
    return x[h * HEAD_DIM:(h + 1) * HEAD_DIM, :]


def _block_diag_mask():
    r = lax.broadcasted_iota(jnp.int32, (REC_WIDTH, REC_WIDTH), 0) // HEAD_DIM
    c = lax.broadcasted_iota(jnp.int32, (REC_WIDTH, REC_WIDTH), 1) // HEAD_DIM
    return (r == c).astype(F32)


def _heads(x):
    return x.reshape(N_HEADS, HEAD_DIM, TB)


def _hgrn_scan(projt, lb, rev, name):
    s = projt.shape[1]
    nblk = s // TB
    zrow = 2 if rev else 1
    tmap = (lambda i: nblk - 1 - i) if rev else (lambda i: i)
    tn = (((0,), (0,)), ((), ()))
    nt = (((1,), (1,)), ((), ()))

    def body(q_ref, z_ref, v_ref, lb_ref, o_ref, hs_ref, at_ref, h_ref):
        @pl.when(pl.program_id(0) == 0)
        def _():
            h_ref[...] = jnp.zeros_like(h_ref)

        v = v_ref[...]
        vb = v.astype(BF16)
        pr = _hg_prep(q_ref[...], z_ref[...], lb_ref[...], rev)
        q, kk = pr["q"], pr["kk"]
        masks, diag = _level_masks(rev)
        qt = [(q * e).astype(BF16) for e in pr["eq"]]
        kt = [(kk * e).astype(BF16) for e in pr["ek"]]
        own = jnp.sum(_heads(q * kk), axis=1)
        outs = []
        for h in range(N_HEADS):
            sc = jnp.where(diag, own[h:h + 1, :], 0.0)
            for level in range(REC_LEVELS):
                sc = sc + jnp.where(masks[level],
                                    lax.dot_general(_head_rows(kt[level], h), _head_rows(qt[level], h), tn,
                                                    preferred_element_type=F32), 0.0)
            a_bf = sc.astype(BF16)
            at_ref[h] = a_bf
            outs.append(jnp.dot(_head_rows(vb, h), a_bf, preferred_element_type=F32))
        o = jnp.concatenate(outs, axis=0)
        bd_mask = _block_diag_mask()
        order = ((1, ~pr["in_a"], pr["end_b"]), (0, pr["in_a"], pr["end_a"]))
        if not rev:
            order = order[::-1]
        for slot, msk, bend in order:
            h0 = h_ref[...]
            hs_ref[slot] = h0
            o = o + lax.dot_general(h0.astype(BF16), jnp.where(msk, pr["qh"], 0.0).astype(BF16), tn,
                                    preferred_element_type=F32)
            upd = lax.dot_general(jnp.where(msk, pr["kh"], 0.0).astype(BF16), vb, nt, preferred_element_type=F32)
            h_ref[...] = jnp.exp(bend) * h0 + upd * bd_mask
        o_ref[...] = o

    row_blk = lambda r: pl.BlockSpec((REC_WIDTH, TB), lambda i: (r, tmap(i)))
    return pl.pallas_call(
        body, name=name, grid=(nblk,),
        in_specs=[row_blk(0), row_blk(zrow), row_blk(3), pl.BlockSpec((REC_WIDTH, 1), lambda i: (0, 0))],
        out_specs=(pl.BlockSpec((REC_WIDTH, TB), lambda i: (0, tmap(i))),
                   pl.BlockSpec((2, REC_WIDTH, REC_WIDTH), lambda i: (tmap(i), 0, 0)),
                   pl.BlockSpec((None, N_HEADS, TB, TB), lambda i: (tmap(i), 0, 0, 0))),
        out_shape=(jax.ShapeDtypeStruct((REC_WIDTH, s), F32),
                   jax.ShapeDtypeStruct((s // REC_CHUNK, REC_WIDTH, REC_WIDTH), F32),
                   jax.ShapeDtypeStruct((nblk, N_HEADS, TB, TB), BF16)),
        scratch_shapes=[pltpu.VMEM((REC_WIDTH, REC_WIDTH), F32)],
        compiler_params=_params(1))(projt, projt, projt, lb)


def _hgrn_scan_bwd(projt, lb, dot, hs, at, prev, rev, name):
    s = projt.shape[1]
    nblk = s // TB
    zrow = 2 if rev else 1
    tmap = (lambda i: i) if rev else (lambda i: nblk - 1 - i)
    has_prev = prev is not None
    tn = (((0,), (0,)), ((), ()))
    nt = (((1,), (1,)), ((), ()))

    def body(*refs):
        q_ref, z_ref, v_ref, lb_ref, do_ref, hs_ref, at_ref = refs[:7]
        rest = refs[7:]
        if has_prev:
            pq_ref, pv_ref = rest[:2]
            rest = rest[2:]
        dq_ref, dz_ref, dv_ref, dlb_ref, dh_ref = rest

        @pl.when(pl.program_id(0) == 0)
        def _():
            dh_ref[...] = jnp.zeros_like(dh_ref)
            dlb_ref[...] = jnp.zeros_like(dlb_ref)

        qraw, v, do, lbv = q_ref[...], v_ref[...], do_ref[...], lb_ref[...]
        dob, vb = do.astype(BF16), v.astype(BF16)
        pr = _hg_prep(qraw, z_ref[...], lbv, rev)
        q, kk, b, in_a = pr["q"], pr["kk"], pr["b"], pr["in_a"]
        masks, diag = _level_masks(rev)
        qt = [(q * e).astype(BF16) for e in pr["eq"]]
        kt = [(kk * e).astype(BF16) for e in pr["ek"]]
        dq_h, dk_h, dv_h, db_h = [], [], [], []
        for h in range(N_HEADS):
            d_at = lax.dot_general(_head_rows(vb, h), _head_rows(dob, h), tn, preferred_element_type=F32)
            dv_h.append(lax.dot_general(_head_rows(dob, h), at_ref[h], nt, preferred_element_type=F32))
            d_own = jnp.sum(jnp.where(diag, d_at, 0.0), axis=0, keepdims=True)
            dq_acc, dk_acc = d_own * _head_rows(kk, h), d_own * _head_rows(q, h)
            db_acc = jnp.zeros((HEAD_DIM, TB), F32)
            for lv in range(REC_LEVELS):
                d_lv = jnp.where(masks[lv], d_at, 0.0).astype(BF16)
                q_lv, k_lv = _head_rows(qt[lv], h), _head_rows(kt[lv], h)
                dqt = jnp.dot(k_lv, d_lv, preferred_element_type=F32)
                dkt = lax.dot_general(q_lv, d_lv, nt, preferred_element_type=F32)
                dq_acc = dq_acc + _head_rows(pr["eq"][lv], h) * dqt
                dk_acc = dk_acc + _head_rows(pr["ek"][lv], h) * dkt
                db_acc = db_acc + q_lv.astype(F32) * dqt - k_lv.astype(F32) * dkt
            dq_h.append(dq_acc)
            dk_h.append(dk_acc)
            db_h.append(db_acc)
        dq_in, dk_in, dv, db_in = (jnp.concatenate(t, axis=0) for t in (dq_h, dk_h, dv_h, db_h))
        dq = dk = jnp.zeros((REC_WIDTH, TB), F32)

        zero = jnp.zeros((REC_WIDTH, TB), F32)
        bd_mask = _block_diag_mask()
        eb = jnp.exp(b)
        const = zero
        order = ((0, in_a, pr["end_a"]), (1, ~in_a, pr["end_b"]))
        if not rev:
            order = order[::-1]
        for slot, msk, bend in order:
            h0 = hs_ref[slot]
            dh1 = dh_ref[...]
            dh1b = dh1.astype(BF16)
            dq = dq + eb * jnp.dot(h0.astype(BF16), jnp.where(msk, do, 0.0).astype(BF16), preferred_element_type=F32)
            dv = dv + lax.dot_general(dh1b, jnp.where(msk, pr["kh"], 0.0).astype(BF16), tn, preferred_element_type=F32)
            dk_int = pr["ekb"] * jnp.dot(dh1b, jnp.where(msk, v, 0.0).astype(BF16), preferred_element_type=F32)
            dk = dk + dk_int
            ebend = jnp.exp(bend)
            c = (jnp.sum(kk * dk_int, axis=1, keepdims=True)
                 + ebend * jnp.sum(h0 * dh1, axis=1, keepdims=True))
            const = const + jnp.where(msk, c, 0.0)
            upd = lax.dot_general(jnp.where(msk, pr["qh"], 0.0).astype(BF16), dob, nt, preferred_element_type=F32)
            dh_ref[...] = ebend * dh1 + upd * bd_mask

        dg = _chunk_sums(db_in + q * dq - kk * dk, not rev, False)[0] + const
        dq, dk = dq + dq_in, dk + dk_in
        sig, sigm, f = pr["sig"], pr["sigm"], pr["f"]
        live = f > F_TINY
        inv_f = 1.0 / jnp.maximum(f, F_TINY)
        one_lb = 1.0 - lbv
        dz = sig * sigm * one_lb * (jnp.where(live, dg * inv_f, 0.0) - dk)
        dlb_ref[...] += jnp.sum(sigm * (jnp.where(live, dg * inv_f, 0.0) - dk), axis=1, keepdims=True)
        dqr = dq * _silu_grad(qraw)
        if has_prev:
            dqr = dqr + pq_ref[...]
            dv = dv + pv_ref[...]
        dq_ref[...] = dqr
        dz_ref[...] = dz
        dv_ref[...] = dv

    row_blk = lambda r: pl.BlockSpec((REC_WIDTH, TB), lambda i: (r, tmap(i)))
    blk = pl.BlockSpec((REC_WIDTH, TB), lambda i: (0, tmap(i)))
    col = pl.BlockSpec((REC_WIDTH, 1), lambda i: (0, 0))
    in_specs = [row_blk(0), row_blk(zrow), row_blk(3), col, blk,
                pl.BlockSpec((2, REC_WIDTH, REC_WIDTH), lambda i: (tmap(i), 0, 0)),
                pl.BlockSpec((None, N_HEADS, TB, TB), lambda i: (tmap(i), 0, 0, 0))]
    ins = [projt, projt, projt, lb, dot, hs, at]
    if has_prev:
        in_specs += [blk, blk]
        ins += list(prev)
    t_shape = jax.ShapeDtypeStruct((REC_WIDTH, s), F32)
    return pl.pallas_call(
        body, name=name, grid=(nblk,), in_specs=in_specs, out_specs=(blk, blk, blk, col),
        out_shape=(t_shape, t_shape, t_shape, jax.ShapeDtypeStruct((REC_WIDTH, 1), F32)),
        scratch_shapes=[pltpu.VMEM((REC_WIDTH, REC_WIDTH), F32)],
        compiler_params=_params(1))(*ins)


REC_OUT_COLS = 512


def _head_rms(o):
    o3 = o.reshape(N_HEADS, HEAD_DIM, o.shape[1])
    rstd = lax.rsqrt(jnp.mean(o3 * o3, axis=1, keepdims=True) + EPS)
    return o3 * rstd, rstd


def _hgrn_out(of, ob, projt, wn, name):
    s = of.shape[1]

    def body(of_ref, ob_ref, g_ref, wn_ref, o_ref):
        on, _ = _head_rms(of_ref[...] + ob_ref[...])
        g = g_ref[...]
        y = on.reshape(REC_WIDTH, REC_OUT_COLS) * wn_ref[...] * (g * _sigmoid(g))
        o_ref[...] = y.T.astype(BF16)

    blk = pl.BlockSpec((REC_WIDTH, REC_OUT_COLS), lambda i: (0, i))
    return pl.pallas_call(
        body, name=name, grid=(s // REC_OUT_COLS,),
        in_specs=[blk, blk, pl.BlockSpec((REC_WIDTH, REC_OUT_COLS), lambda i: (4, i)),
                  pl.BlockSpec((REC_WIDTH, 1), lambda i: (0, 0))],
        out_specs=pl.BlockSpec((REC_OUT_COLS, REC_WIDTH), lambda i: (i, 0)),
        out_shape=jax.ShapeDtypeStruct((s, REC_WIDTH), BF16), compiler_params=_params(1))(of, ob, projt, wn)


def _hgrn_out_bwd(drec, of, ob, projt, wn, name):
    s = of.shape[1]

    def body(d_ref, of_ref, ob_ref, g_ref, wn_ref, do_ref, dg_ref, dwn_ref):
        @pl.when(pl.program_id(0) == 0)
        def _():
            dwn_ref[...] = jnp.zeros_like(dwn_ref)

        dy = d_ref[...].T
        on3, rstd = _head_rms(of_ref[...] + ob_ref[...])
        on = on3.reshape(REC_WIDTH, REC_OUT_COLS)
        g, wnv = g_ref[...], wn_ref[...]
        dg_ref[...] = dy * on * wnv * _silu_grad(g)
        d_onw = dy * (g * _sigmoid(g))
        dwn_ref[...] += jnp.sum(d_onw * on, axis=1, keepdims=True)
        d_on3 = (d_onw * wnv).reshape(N_HEADS, HEAD_DIM, REC_OUT_COLS)
        do3 = rstd * (d_on3 - on3 * jnp.mean(d_on3 * on3, axis=1, keepdims=True))
        do_ref[...] = do3.reshape(REC_WIDTH, REC_OUT_COLS)

    blk = pl.BlockSpec((REC_WIDTH, REC_OUT_COLS), lambda i: (0, i))
    col = pl.BlockSpec((REC_WIDTH, 1), lambda i: (0, 0))
    t_shape = jax.ShapeDtypeStruct((REC_WIDTH, s), F32)
    return pl.pallas_call(
        body, name=name, grid=(s // REC_OUT_COLS,),
        in_specs=[pl.BlockSpec((REC_OUT_COLS, REC_WIDTH), lambda i: (i, 0)), blk, blk,
                  pl.BlockSpec((REC_WIDTH, REC_OUT_COLS), lambda i: (4, i)), col],
        out_specs=(blk, blk, col),
        out_shape=(t_shape, t_shape, jax.ShapeDtypeStruct((REC_WIDTH, 1), F32)),
        compiler_params=_params(1))(drec, of, ob, projt, wn)


def _lower_bounds(gamma, name):
    def body(g_ref, lb_ref, p_ref):
        g0, g1 = g_ref[0:1, :], g_ref[1:2, :]
        m = jnp.maximum(g0, g1)
        e0, e1 = jnp.exp(g0 - m), jnp.exp(g1 - m)
        p0, p1 = e0 / (e0 + e1), e1 / (e0 + e1)
        lb_ref[...] = (p0 + p1) - p0
        p_ref[0:1, :] = p0
        p_ref[1:2, :] = p1

    n = gamma.shape[1]
    return pl.pallas_call(body, name=name,
                          out_shape=(jax.ShapeDtypeStruct((1, n), F32), jax.ShapeDtypeStruct((2, n), F32)))(gamma)


def _lower_bounds_bwd(dlb1, p, name):
    def body(d_ref, p_ref, o_ref):
        p0, p1, d = p_ref[0:1, :], p_ref[1:2, :], d_ref[...]
        inner = p1 * d
        o_ref[0:1, :] = p0 * (0.0 - inner)
        o_ref[1:2, :] = p1 * (d - inner)

    return pl.pallas_call(body, name=name, out_shape=jax.ShapeDtypeStruct(p.shape, F32))(dlb1, p)


def _split_w_in(w_in):
    return dict(conv=w_in[:, G_CONV[0]:G_CONV[1]], qkv=w_in[:, G_QKV[0]:G_QKV[1]],
                rec_t=w_in[:, G_REC[0]:].T, nat=w_in[:, :G_REC[0]])


def _split_w_rest(w_out, w_up, w_down):
    return dict(out=w_out, out_a=w_out[:CONV_CH], out_b=w_out[CONV_CH:CONV_CH + ATT_WIDTH],
                out_c=w_out[CONV_CH + ATT_WIDTH:], up=w_up, down=w_down)


def _col(v):
    return v.reshape(-1, 1)


def _sequence_step(x, tgt, mods, lbs, small, w_in0, later_weights, final_w):
    saved = []
    xin = x
    big = [_split_w_in(w_in0), None]
    h1 = _resid_norm_mod(x, None, None, small[0]["norm1_w"], mods[0][1:2], mods[0][0:1], "norm1_first")
    for l in range(DEPTH):
        sm, w, md = small[l], big[l], mods[l]
        pa = _matmul(h1, w["conv"], "nn", F32, f"proj_conv")
        qkv = _matmul(h1, w["qkv"], "nn", F32, f"proj_qkv")
        projt = _matmul(w["rec_t"], h1, "nt", F32, f"proj_rec")
        a_out, cv = _conv_mixer(pa, sm["conv_a_w"], sm["conv_a_b"], sm["ln_a_w"], sm["ln_a_b"], f"conv_mixer")
        outs, lses = zip(*[_attn_branch(qkv, d, f"attn_d{d}") for d in DILATIONS])
        att, att32, lse = _attn_combine(outs, lses, f"attn_combine")
        lb_f, lb_b = _col(lbs[l][0]), _col(lbs[l][1])
        of, hsf, atf = _hgrn_scan(projt, lb_f, False, "hgrn_fwd")
        ob, hsb, atb = _hgrn_scan(projt, lb_b, True, "hgrn_rev")
        wn = _col(sm["rec_norm_w"])
        rec = _hgrn_out(of, ob, projt, wn, f"hgrn_out")
        mixed = jnp.concatenate([a_out, att, rec], axis=1)
        if l == 0:
            w_in1, w_out_all, w_up_all, w_down_all = later_weights(rec)
            big[0].update(_split_w_rest(w_out_all[0], w_up_all[0], w_down_all[0]))
            big[1] = dict(_split_w_in(w_in1), **_split_w_rest(w_out_all[1], w_up_all[1], w_down_all[1]))
        r1 = _matmul(mixed, w["out"], "nn", BF16, "out_proj")
        xmid, h2 = _resid_norm_mod(xin, r1, md[2:3], sm["norm2_w"], md[4:5], md[3:4], f"norm2")
        u = _matmul(h2, w["up"], "nn", BF16, f"ffn_up")
        act = _ffn_act(u, sm["conv_f_w"], f"ffn_act")
        r2 = _matmul(act, w["down"], "nn", BF16, "ffn_down")
        saved.append(dict(xin=xin, h1=h1, pa=pa, qkv=qkv, projt=projt, cv=cv, att32=att32, lse=lse, of=of, ob=ob,
                          hsf=hsf, hsb=hsb, atf=atf, atb=atb, lb_f=lb_f, lb_b=lb_b, wn=wn, mixed=mixed, r1=r1, xmid=xmid, h2=h2,
                          u=u, act=act, r2=r2))
        if l + 1 < DEPTH:
            nxt = small[l + 1]
            xin, h1 = _resid_norm_mod(xmid, r2, md[5:6], nxt["norm1_w"], mods[l + 1][1:2], mods[l + 1][0:1],
                                      "norm1")
    top = saved[-1]
    loss, dx, dr2, dg2, dfw = _final_loss(top["xmid"], top["r2"], mods[-1][5:6], final_w, tgt, "final_loss")

    grads = [None] * DEPTH
    for l in reversed(range(DEPTH)):
        sm, w, md, sv = small[l], big[l], mods[l], saved[l]
        dact = _matmul(dr2, w["down"], "nt", BF16, f"d_act")
        g_down = _matmul(dr2, sv["act"], "tn", F32, "dw_down").T
        dug, duv, dwg, dwv = _ffn_act_bwd(sv["u"], dact, sm["conv_f_w"], f"ffn_act_bwd")
        du = jnp.concatenate([dug, duv], axis=1)
        dh2 = _matmul(du, w["up"], "nt", BF16, "d_h2")
        g_up = _matmul(sv["h2"], du, "tn", F32, f"dw_up")
        dxmid, dr1, dsh2, dsc2, dnw2, dg1 = _norm_bwd(sv["xmid"], [dh2], dx, sm["norm2_w"], md[4:5], md[2:3], sv["r1"],
                                                     f"norm2_bwd")
        dmix_a = _matmul(dr1, w["out_a"], "nt", F32, f"d_mix_a")
        dmix_b = _matmul(dr1, w["out_b"], "nt", F32, f"d_mix_b")
        dmix_c = _matmul(dr1, w["out_c"], "nt", F32, f"d_mix_c")
        g_out = _matmul(sv["mixed"], dr1, "tn", F32, f"dw_out")
        dc, dlnw, dlnb, dcb = _conv_mixer_bwd_ln(sv["cv"], dmix_a, sm["ln_a_w"], sm["ln_a_b"], f"conv_mixer_bwd_ln")
        dpa, dcw = _conv_mixer_bwd_conv(sv["pa"], dc, sm["conv_a_w"], f"conv_mixer_bwd_conv")
        delta = _attn_delta(dmix_b, sv["att32"], "attn_delta")
        dqkv = None
        for d in DILATIONS:
            dqkv = _attn_branch_bwd(sv["qkv"], dmix_b, sv["lse"], delta, dqkv, d, f"attn_bwd_d{d}")
        dot, dgt, dwn = _hgrn_out_bwd(dmix_c, sv["of"], sv["ob"], sv["projt"], sv["wn"], f"hgrn_out_bwd")
        dqf, dzf, dvf, dlbf = _hgrn_scan_bwd(sv["projt"], sv["lb_f"], dot, sv["hsf"], sv["atf"], None, False,
                                             "hgrn_fwd_bwd")
        dqt, dzb, dvt, dlbb = _hgrn_scan_bwd(sv["projt"], sv["lb_b"], dot, sv["hsb"], sv["atb"], (dqf, dvf), True,
                                             "hgrn_rev_bwd")
        dprojt = jnp.concatenate([dqt, dzf, dzb, dvt, dgt], axis=0).astype(BF16)
        dnat = jnp.concatenate([dpa] + [t.astype(BF16) for t in dqkv], axis=1)
        dh1_a = _matmul(dnat, w["nat"], "nt", BF16, "d_h1_nat")
        dh1_b = _matmul(dprojt, w["rec_t"], "tn", BF16, "d_h1_rec")
        g_in_nat = _matmul(sv["h1"], dnat, "tn", F32, f"dw_in_nat")
        g_in_rec_t = _matmul(dprojt, sv["h1"], "nn", F32, f"dw_in_rec")
        g_in = jnp.concatenate([g_in_nat, g_in_rec_t.T], axis=1)
        if l > 0:
            below = saved[l - 1]
            dx, dr2, dsh1, dsc1, dnw1, dg2_below = _norm_bwd(sv["xin"], [dh1_a, dh1_b], dxmid, sm["norm1_w"], md[1:2],
                                                            mods[l - 1][5:6], below["r2"], f"norm1_bwd")
        else:
            dx, dsh1, dsc1, dnw1 = _norm_bwd(sv["xin"], [dh1_a, dh1_b], dxmid, sm["norm1_w"], md[1:2], None, None,
                                             f"norm1_bwd")
        grads[l] = dict(w_in=g_in, w_out=g_out, w_up=g_up, w_down=g_down,
                        mod=[dsh1, dsc1, dg1, dsh2, dsc2, dg2], norm1_w=dnw1, conv_a_w=dcw[:CONV_WIDTH], conv_a_b=dcb,
                        ln_a_w=dlnw, ln_a_b=dlnb, lb=jnp.concatenate([dlbf.reshape(1, -1), dlbb.reshape(1, -1)], axis=0),
                        rec_norm_w=dwn.reshape(1, -1), norm2_w=dnw2,
                        conv_f_w=jnp.concatenate([dwg[:3], dwv[:3]], axis=1))
        if l > 0:
            dg2 = dg2_below
    return loss[0, 0], dx, grads, dfw


def _adamw_math(w, g, m, v):
    m = ADAM_B1 * m + (1.0 - ADAM_B1) * g
    v = ADAM_B2 * v + (1.0 - ADAM_B2) * (g * g)
    m_hat = m / (1.0 - ADAM_B1 ** ADAM_STEP)
    v_hat = v / (1.0 - ADAM_B2 ** ADAM_STEP)
    delta = -ADAM_LR * (m_hat / (jnp.sqrt(v_hat) + ADAM_EPS) + ADAM_WD * w)
    return delta, m, v


def _row_tile(rows, cols, max_elems=384 * 1024):
    best = None
    for t in range(8, rows + 1, 8):
        if rows % t == 0 and t * cols <= max_elems:
            best = t
    return best or rows


def _adamw(w, g, m, v, name):
    nl, r, c = w.shape
    tr = _row_tile(r, c)

    def body(w_ref, g_ref, m_ref, v_ref, d_ref, m2_ref, v2_ref):
        d_ref[...], m2_ref[...], v2_ref[...] = _adamw_math(w_ref[...], g_ref[...], m_ref[...], v_ref[...])

    blk = pl.BlockSpec((None, tr, c), lambda l, i: (l, i, 0))
    shape = jax.ShapeDtypeStruct((nl, r, c), F32)
    return pl.pallas_call(body, name=name, grid=(nl, r // tr), in_specs=[blk] * 4, out_specs=(blk, blk, blk),
                          out_shape=(shape, shape, shape), compiler_params=_params(2))(w, g, m, v)


ADA_SHARD = N_MOD * D_MODEL // 4
ADA_COLS = 512
ADA_ROWS = 256
HIGHEST = lax.Precision.HIGHEST


def _ada_mod(c_all, w_ada, b_sh, name):
    def body(c_ref, w_ref, b_ref, o_ref):
        cv = c_ref[...]
        o_ref[...] = jnp.dot(cv * _sigmoid(cv), w_ref[...], precision=HIGHEST, preferred_element_type=F32) + b_ref[...]

    return pl.pallas_call(
        body, name=name, grid=(DEPTH, ADA_SHARD // ADA_COLS),
        in_specs=[pl.BlockSpec((8, D_MODEL), lambda l, j: (0, 0)),
                  pl.BlockSpec((None, D_MODEL, ADA_COLS), lambda l, j: (l, 0, j)),
                  pl.BlockSpec((None, 1, ADA_COLS), lambda l, j: (l, 0, j))],
        out_specs=pl.BlockSpec((None, 8, ADA_COLS), lambda l, j: (l, 0, j)),
        out_shape=jax.ShapeDtypeStruct((DEPTH, 8, ADA_SHARD), F32), compiler_params=_params(2))(c_all, w_ada, b_sh)


def _ada_update(c_all, dmod_sh, w, m, v, name):
    def body(c_ref, d_ref, w_ref, m_ref, v_ref, g_ref, dl_ref, m2_ref, v2_ref):
        cv = c_ref[...]
        g = lax.dot_general(cv * _sigmoid(cv), d_ref[...], (((0,), (0,)), ((), ())), precision=HIGHEST,
                            preferred_element_type=F32)
        g_ref[...] = g
        dl_ref[...], m2_ref[...], v2_ref[...] = _adamw_math(w_ref[...], g, m_ref[...], v_ref[...])

    blk = pl.BlockSpec((None, ADA_ROWS, ADA_SHARD), lambda l, i: (l, i, 0))
    shape = jax.ShapeDtypeStruct((DEPTH, D_MODEL, ADA_SHARD), F32)
    return pl.pallas_call(
        body, name=name, grid=(DEPTH, D_MODEL // ADA_ROWS),
        in_specs=[pl.BlockSpec((8, ADA_ROWS), lambda l, i: (0, i)),
                  pl.BlockSpec((None, 8, ADA_SHARD), lambda l, i: (l, 0, 0)), blk, blk, blk],
        out_specs=(blk,) * 4, out_shape=(shape,) * 4, compiler_params=_params(2))(c_all, dmod_sh, w, m, v)


def _sum_devices(packs, name):
    def body(p_ref, o_ref):
        acc = p_ref[0]
        for dev in range(1, 8):
            acc = acc + p_ref[dev]
        o_ref[...] = acc

    return pl.pallas_call(body, name=name, out_shape=jax.ShapeDtypeStruct(packs.shape[1:], F32))(packs)


def _mesh_pos():
    return lax.axis_index("x"), lax.axis_index("y"), lax.axis_index("c")


def _flip(v, bit):
    return 1 - v if bit else v


def _allgather_devices(x, name):
    m_per, n = x.shape

    def body(x_ref, out_ref, send_sems, recv_sems, local_sem):
        ix, iy, ic = _mesh_pos()
        me, sibling = (ix, iy, ic), (ix, iy, 1 - ic)
        chips = [(1 - ix, iy), (ix, 1 - iy), (1 - ix, 1 - iy)]

        def rows(px, py, pc):
            return out_ref.at[pl.ds((4 * px + 2 * py + pc) * m_per, m_per), :]

        def copy(k, block, to, src=None):
            return pltpu.make_async_remote_copy(
                src_ref=rows(*block) if src is None else src, dst_ref=rows(*block),
                send_sem=send_sems.at[k], recv_sem=recv_sems.at[k], device_id=to, device_id_type=MESH)

        mine = pltpu.make_async_copy(x_ref, rows(*me), local_sem)
        mine.start()
        first = [copy(0, me, sibling, src=x_ref)]
        first += [copy(1 + j, me, (*chip, ic), src=x_ref) for j, chip in enumerate(chips)]
        for cp in first:
            cp.start()
        passed = [copy(4 + j, (*chip, ic), sibling) for j, chip in enumerate(chips)]
        for j, chip in enumerate(chips):
            copy(1 + j, (*chip, ic), me).wait_recv()
            passed[j].start()
        copy(0, sibling, me).wait_recv()
        for j, chip in enumerate(chips):
            copy(4 + j, (*chip, 1 - ic), me).wait_recv()
        for cp in first + passed:
            cp.wait_send()
        mine.wait()

    return pl.pallas_call(
        body, name=name, out_shape=jax.ShapeDtypeStruct((8 * m_per, n), x.dtype),
        in_specs=[pl.BlockSpec(memory_space=pltpu.VMEM)], out_specs=pl.BlockSpec(memory_space=pltpu.VMEM),
        scratch_shapes=[pltpu.SemaphoreType.DMA((7,)), pltpu.SemaphoreType.DMA((7,)), pltpu.SemaphoreType.DMA],
    )(x)


def _gather_chips(shards, name):
    n = len(shards)

    def body(*refs):
        ins, outs = refs[:n], refs[n:2 * n]
        send_sems, recv_sems, local_sems = refs[2 * n:]
        ix, iy, ic = _mesh_pos()
        me = 2 * ix + iy
        local = [pltpu.make_async_copy(ins[a], outs[a].at[me], local_sems.at[a]) for a in range(n)]
        for cp in local:
            cp.start()
        remote = []
        for a in range(n):
            for k in (1, 2, 3):
                px, py = _flip(ix, k & 2), _flip(iy, k & 1)
                sems = dict(send_sem=send_sems.at[3 * a + k - 1], recv_sem=recv_sems.at[3 * a + k - 1],
                            device_id=(px, py, ic), device_id_type=MESH)
                out_cp = pltpu.make_async_remote_copy(src_ref=ins[a], dst_ref=outs[a].at[me], **sems)
                in_cp = pltpu.make_async_remote_copy(src_ref=ins[a], dst_ref=outs[a].at[2 * px + py], **sems)
                out_cp.start()
                remote.append((out_cp, in_cp))
        for out_cp, in_cp in remote:
            out_cp.wait_send()
            in_cp.wait_recv()
        for cp in local:
            cp.wait()

    return pl.pallas_call(
        body, name=name, in_specs=[ANY] * n, out_specs=tuple([ANY] * n),
        out_shape=tuple(jax.ShapeDtypeStruct((4,) + t.shape, t.dtype) for t in shards),
        scratch_shapes=[pltpu.SemaphoreType.DMA((3 * n,)), pltpu.SemaphoreType.DMA((3 * n,)),
                        pltpu.SemaphoreType.DMA((n,))],
    )(*shards)


HBM = pl.BlockSpec(memory_space=pltpu.HBM)
SEM = pl.BlockSpec(memory_space=pltpu.SEMAPHORE)
DATAFLOW = pltpu.SideEffectType.DATAFLOW_SIDE_EFFECTING


def _peer_chip(ix, iy, k):
    return _flip(ix, k & 2), _flip(iy, k & 1)


def _gather_chips_start(shards, name):
    n = len(shards)

    def body(*refs):
        src, land = refs[:n], refs[n:2 * n]
        send_sems, recv_sems = refs[2 * n], refs[2 * n + 1]
        token = refs[-1]
        ix, iy, ic = _mesh_pos()
        me = 2 * ix + iy
        for a in range(n):
            for k in (1, 2, 3):
                px, py = _peer_chip(ix, iy, k)
                pltpu.make_async_remote_copy(
                    src_ref=src[a], dst_ref=land[a].at[me], send_sem=send_sems.at[3 * a + k - 1],
                    recv_sem=recv_sems.at[3 * a + k - 1], device_id=(px, py, ic), device_id_type=MESH).start()
        token[...] = jnp.zeros_like(token)

    hbm = lambda shape, dtype: pltpu.HBM(shape, dtype)
    operands = ([pltpu.with_memory_space_constraint(t, pltpu.HBM) for t in shards]
                + [pltpu.with_memory_space_constraint(lax.empty((4,) + t.shape, t.dtype), pltpu.HBM) for t in shards])
    return pl.pallas_call(
        body, name=name,
        out_shape=(pltpu.SemaphoreType.DMA((3 * n,)), pltpu.SemaphoreType.DMA((3 * n,)),
                   *[hbm(t.shape, t.dtype) for t in shards], *[hbm((4,) + t.shape, t.dtype) for t in shards],
                   jax.ShapeDtypeStruct((8, LANES), F32)),
        in_specs=(HBM,) * (2 * n),
        out_specs=(SEM, SEM) + (HBM,) * (2 * n) + (pl.BlockSpec(memory_space=pltpu.VMEM),),
        input_output_aliases={a: 2 + a for a in range(2 * n)},
        compiler_params=pltpu.CompilerParams(has_side_effects=DATAFLOW),
    )(*operands)


def _gather_chips_wait(started, after, name):
    send_sems, recv_sems = started[0], started[1]
    thru = started[2:-1]
    n = len(thru) // 2

    def body(*refs):
        src, land = refs[:n], refs[n:2 * n]
        send_sems, recv_sems = refs[2 * n], refs[2 * n + 1]
        ix, iy, ic = _mesh_pos()
        for a in range(n):
            for k in (1, 2, 3):
                px, py = _peer_chip(ix, iy, k)
                cp = pltpu.make_async_remote_copy(
                    src_ref=src[a], dst_ref=land[a].at[2 * px + py], send_sem=send_sems.at[3 * a + k - 1],
                    recv_sem=recv_sems.at[3 * a + k - 1], device_id=(px, py, ic), device_id_type=MESH)
                cp.wait_send()
                cp.wait_recv()

    outs = pl.pallas_call(
        body, name=name,
        out_shape=tuple(pltpu.HBM(t.shape, t.dtype) for t in thru),
        in_specs=(HBM,) * (2 * n) + (SEM, SEM, ANY), out_specs=(HBM,) * (2 * n),
        input_output_aliases={a: a for a in range(2 * n)},
        compiler_params=pltpu.CompilerParams(has_side_effects=DATAFLOW),
    )(*thru, send_sems, recv_sems, after)
    return outs[:n], outs[n:]


BIG_KINDS = (("w_in", "col", D_MODEL, IN_COLS), ("w_out", "row", D_MODEL, D_MODEL),
             ("w_up", "col", D_MODEL, 2 * D_FF), ("w_down", "row", D_FF, D_MODEL))


def _piece_shape(how, r, c):
    return (r // 2, c // 4) if how == "col" else (r // 8, c)


def _aligned(start, multiple):
    return start if isinstance(start, int) else pl.multiple_of(start, multiple)


def _piece(ref, how, r, c, chip, half):
    if how == "col":
        return ref.at[pl.ds(_aligned(half * (r // 2), 8), r // 2), pl.ds(_aligned(chip * (c // 4), LANES), c // 4)]
    n = r // 4
    return ref.at[pl.ds(_aligned(chip * n + half * (n // 2), 8), n // 2), :]


def _rs_pair_exchange(grads, name):
    nk = len(BIG_KINDS)
    flat = [grads[ki][l] for ki in range(nk) for l in range(DEPTH)]
    per = DEPTH * 4

    def body(*refs):
        g, land = refs[:nk * DEPTH], refs[nk * DEPTH:nk * DEPTH + nk]
        send_sems, recv_sems = refs[nk * DEPTH + nk:]
        ix, iy, ic = _mesh_pos()
        sibling = (ix, iy, 1 - ic)
        copies = []
        for ki, (_, how, r, c) in enumerate(BIG_KINDS):
            for l in range(DEPTH):
                for j in range(4):
                    sem = ki * per + l * 4 + j
                    rem = pltpu.make_async_remote_copy(
                        src_ref=_piece(g[ki * DEPTH + l], how, r, c, j, 1 - ic), dst_ref=land[ki].at[l, j],
                        send_sem=send_sems.at[sem], recv_sem=recv_sems.at[sem], device_id=sibling, device_id_type=MESH)
                    rem.start()
                    copies.append(rem)
        for rem in copies:
            rem.wait_send()
            rem.wait_recv()

    shapes = [jax.ShapeDtypeStruct((DEPTH, 4) + _piece_shape(how, r, c), F32) for _, how, r, c in BIG_KINDS]
    return pl.pallas_call(
        body, name=name, in_specs=[ANY] * len(flat), out_specs=tuple([ANY] * nk), out_shape=tuple(shapes),
        scratch_shapes=[pltpu.SemaphoreType.DMA((nk * per,))] * 2,
    )(*flat)


def _pair_sum(g, theirs, layer, how, core, name):
    r, c = g.shape
    pr, pc = _piece_shape(how, r, c)
    if how == "col":
        mine_spec = pl.BlockSpec((pr, pc), lambda j, core_ref: (core_ref[0], j))
    else:
        mine_spec = pl.BlockSpec((pr, pc), lambda j, core_ref: (2 * j + core_ref[0], 0))

    def body(core_ref, g_ref, t_ref, o_ref, ob_ref):
        total = g_ref[...] + t_ref[...]
        o_ref[...] = total
        ob_ref[...] = total.astype(BF16)

    out_blk = pl.BlockSpec((None, pr, pc), lambda j, core_ref: (j, 0, 0))
    return pl.pallas_call(
        body, name=name,
        grid_spec=pltpu.PrefetchScalarGridSpec(
            num_scalar_prefetch=1, grid=(4,),
            in_specs=[mine_spec, pl.BlockSpec((None, None, pr, pc), lambda j, core_ref: (layer, j, 0, 0))],
            out_specs=(out_blk, out_blk)),
        out_shape=(jax.ShapeDtypeStruct((4, pr, pc), F32), jax.ShapeDtypeStruct((4, pr, pc), BF16)),
        compiler_params=_params(1))(core, g, theirs)


def _rs_chip_exchange(pair_sums, name):
    nk = len(pair_sums)
    flat = [pair_sums[ki][l] for ki in range(nk) for l in range(DEPTH)]

    def body(*refs):
        src, dst = refs[:nk * DEPTH], refs[nk * DEPTH:nk * DEPTH + nk]
        send_sems, recv_sems = refs[nk * DEPTH + nk:]
        ix, iy, ic = _mesh_pos()
        copies = []
        for ki in range(nk):
            for l in range(DEPTH):
                for k in (1, 2, 3):
                    px, py = _flip(ix, k & 2), _flip(iy, k & 1)
                    sem = (ki * DEPTH + l) * 3 + k - 1
                    rem = pltpu.make_async_remote_copy(
                        src_ref=src[ki * DEPTH + l].at[2 * px + py], dst_ref=dst[ki].at[l, k - 1],
                        send_sem=send_sems.at[sem], recv_sem=recv_sems.at[sem], device_id=(px, py, ic), device_id_type=MESH)
                    rem.start()
                    copies.append(rem)
        for rem in copies:
            rem.wait_send()
            rem.wait_recv()

    return pl.pallas_call(
        body, name=name, in_specs=[ANY] * len(flat), out_specs=tuple([ANY] * nk),
        out_shape=tuple(jax.ShapeDtypeStruct((DEPTH, 3) + pair_sums[ki][0].shape[1:], pair_sums[ki][0].dtype)
                        for ki in range(nk)),
        scratch_shapes=[pltpu.SemaphoreType.DMA((nk * DEPTH * 3,))] * 2,
    )(*flat)


def _chip_sum(own, others, layer, chip, name):
    _, pr, pc = own.shape

    def body(chip_ref, own_ref, s1, s2, s3, o_ref):
        o_ref[...] = ((own_ref[...] + s1[...].astype(F32)) + s2[...].astype(F32)) + s3[...].astype(F32)

    slot = lambda k: pl.BlockSpec((None, None, pr, pc), lambda i, chip_ref: (layer, k, 0, 0))
    return pl.pallas_call(
        body, name=name,
        grid_spec=pltpu.PrefetchScalarGridSpec(
            num_scalar_prefetch=1, grid=(1,),
            in_specs=[pl.BlockSpec((None, pr, pc), lambda i, chip_ref: (chip_ref[0], 0, 0)), slot(0), slot(1), slot(2)],
            out_specs=pl.BlockSpec((pr, pc), lambda i, chip_ref: (0, 0))),
        out_shape=jax.ShapeDtypeStruct((pr, pc), F32), compiler_params=_params(1))(chip, own, others, others, others)


def _rs_pair_share(halves, name):
    nk = len(halves)
    flat = [halves[ki][l] for ki in range(nk) for l in range(DEPTH)]

    def body(*refs):
        src, dst = refs[:nk * DEPTH], refs[nk * DEPTH:nk * DEPTH + nk]
        send_sems, recv_sems = refs[nk * DEPTH + nk:]
        ix, iy, ic = _mesh_pos()
        copies = []
        for ki in range(nk):
            for l in range(DEPTH):
                sem = ki * DEPTH + l
                rem = pltpu.make_async_remote_copy(
                    src_ref=src[sem], dst_ref=dst[ki].at[l], send_sem=send_sems.at[sem], recv_sem=recv_sems.at[sem],
                    device_id=(ix, iy, 1 - ic), device_id_type=MESH)
                rem.start()
                copies.append(rem)
        for rem in copies:
            rem.wait_send()
            rem.wait_recv()

    return pl.pallas_call(
        body, name=name, in_specs=[ANY] * len(flat), out_specs=tuple([ANY] * nk),
        out_shape=tuple(jax.ShapeDtypeStruct((DEPTH,) + halves[ki][0].shape, F32) for ki in range(nk)),
        scratch_shapes=[pltpu.SemaphoreType.DMA((nk * DEPTH,))] * 2,
    )(*flat)


def _adamw_halves(w, mine, theirs, m, v, core, name):
    nl, pr, pc = theirs.shape
    shape = w.shape
    view = lambda t: t.reshape(nl, 2, pr, pc)
    tr = _row_tile(pr, pc, 256 * 1024)

    def body(core_ref, w_ref, a0_ref, a1_ref, t_ref, m_ref, v_ref, g_ref, d_ref, m2_ref, v2_ref):
        own = jnp.where(pl.program_id(0) == 0, a0_ref[...], a1_ref[...])
        g = jnp.where(pl.program_id(1) == core_ref[0], own, t_ref[...])
        g_ref[...] = g
        d_ref[...], m2_ref[...], v2_ref[...] = _adamw_math(w_ref[...], g, m_ref[...], v_ref[...])

    blk = pl.BlockSpec((None, None, tr, pc), lambda l, h, i, core_ref: (l, h, i, 0))
    own_blk = pl.BlockSpec((tr, pc), lambda l, h, i, core_ref: (i, 0))
    out = jax.ShapeDtypeStruct((nl, 2, pr, pc), F32)
    outs = pl.pallas_call(
        body, name=name,
        grid_spec=pltpu.PrefetchScalarGridSpec(
            num_scalar_prefetch=1, grid=(nl, 2, pr // tr),
            in_specs=[blk, own_blk, own_blk, pl.BlockSpec((None, tr, pc), lambda l, h, i, core_ref: (l, i, 0)), blk, blk],
            out_specs=(blk,) * 4),
        out_shape=(out,) * 4, compiler_params=_params(3),
    )(core, view(w), mine[0], mine[1], theirs, view(m), view(v))
    return tuple(t.reshape(shape) for t in outs)


def _reduce_scatter_big(grads, core, chip):
    theirs = _rs_pair_exchange(grads, "rs_pair_exchange")
    pair_sums = [[_pair_sum(grads[ki][l], theirs[ki], l, how, core, f"rs_pair_sum_{kind}") for l in range(DEPTH)]
                 for ki, (kind, how, _, _) in enumerate(BIG_KINDS)]
    slots = _rs_chip_exchange([[both[1] for both in row] for row in pair_sums], "rs_chip_exchange")
    halves = [[_chip_sum(pair_sums[ki][l][0], slots[ki], l, chip, f"rs_chip_sum_{kind}") for l in range(DEPTH)]
              for ki, (kind, _, _, _) in enumerate(BIG_KINDS)]
    other = _rs_pair_share(halves, "rs_pair_share")
    return list(zip(halves, other))


WEIGHT_NAMES = ("w_ada", "b_ada", "norm1_w", "w_in", "conv_a_w", "conv_a_b", "ln_a_w", "ln_a_b", "lb_gamma",
                "rec_norm_w", "w_out", "norm2_w", "w_up", "conv_f_w", "w_down", "final_norm_w")
SMALL_PARAMS = (("b_ada", (DEPTH, N_MOD * D_MODEL), None), ("norm1_w", (DEPTH, D_MODEL), None),
                ("conv_a_w", (DEPTH, CONV_WIDTH, CONV_CH), 2), ("conv_a_b", (DEPTH, CONV_CH), None),
                ("ln_a_w", (DEPTH, CONV_CH), None), ("ln_a_b", (DEPTH, CONV_CH), None),
                ("lb_gamma", (DEPTH, 2, REC_WIDTH), 2), ("rec_norm_w", (DEPTH, REC_WIDTH), None),
                ("norm2_w", (DEPTH, D_MODEL), None), ("conv_f_w", (DEPTH, 3, 2 * D_FF), 2),
                ("final_norm_w", (D_MODEL,), None))


def _pack_rows(parts):
    flat = jnp.concatenate([p.reshape(-1) for p in parts])
    total = flat.shape[0]
    padded = -(-total // (8 * LANES)) * (8 * LANES)
    return jnp.pad(flat, (0, padded - total)).reshape(padded // LANES, LANES)


def _unpack(flat, shapes):
    out, off = [], 0
    for shp in shapes:
        size = int(np.prod(shp))
        out.append(flat[off:off + size].reshape(shp))
        off += size
    return out


def _unstack_chips(t, axis):
    return jnp.concatenate([t[j] for j in range(4)], axis=axis)


def kernel(x, c, w_ada, b_ada, norm1_w, w_in, conv_a_w, conv_a_b, ln_a_w, ln_a_b, lb_gamma, rec_norm_w, w_out, norm2_w, w_up, conv_f_w, w_down, final_norm_w, loss_target, m_w_ada, m_b_ada, m_norm1_w, m_w_in, m_conv_a_w, m_conv_a_b, m_ln_a_w, m_ln_a_b, m_lb_gamma, m_rec_norm_w, m_w_out, m_norm2_w, m_w_up, m_conv_f_w, m_w_down, m_final_norm_w, v_w_ada, v_b_ada, v_norm1_w, v_w_in, v_conv_a_w, v_conv_a_b, v_ln_a_w, v_ln_a_b, v_lb_gamma, v_rec_norm_w, v_w_out, v_norm2_w, v_w_up, v_conv_f_w, v_w_down, v_final_norm_w):
    params = dict(zip(WEIGHT_NAMES, (w_ada, b_ada, norm1_w, w_in, conv_a_w, conv_a_b, ln_a_w, ln_a_b, lb_gamma,
                                     rec_norm_w, w_out, norm2_w, w_up, conv_f_w, w_down, final_norm_w)))
    mom1 = dict(zip(WEIGHT_NAMES, (m_w_ada, m_b_ada, m_norm1_w, m_w_in, m_conv_a_w, m_conv_a_b, m_ln_a_w, m_ln_a_b,
                                   m_lb_gamma, m_rec_norm_w, m_w_out, m_norm2_w, m_w_up, m_conv_f_w, m_w_down,
                                   m_final_norm_w)))
    mom2 = dict(zip(WEIGHT_NAMES, (v_w_ada, v_b_ada, v_norm1_w, v_w_in, v_conv_a_w, v_conv_a_b, v_ln_a_w, v_ln_a_b,
                                   v_lb_gamma, v_rec_norm_w, v_w_out, v_norm2_w, v_w_up, v_conv_f_w, v_w_down,
                                   v_final_norm_w)))
    ix, iy, ic = _mesh_pos()
    chip = 2 * ix + iy
    dev = 2 * chip + ic

    c_all = _allgather_devices(c.reshape(8, LANES), "gather_cond").reshape(8, D_MODEL)
    b_sh = lax.dynamic_slice_in_dim(b_ada, chip * ADA_SHARD, ADA_SHARD, axis=1)
    mod_sh = _ada_mod(c_all, w_ada, b_sh.reshape(DEPTH, 1, ADA_SHARD), "ada_mod")
    w_in_b, w_out_b, w_up_b, w_down_b = (t.astype(BF16) for t in (w_in, w_out, w_up, w_down))
    first = _gather_chips([mod_sh, conv_a_w, conv_f_w, lb_gamma, w_in_b[0]], "gather_first")
    later = [w_in_b[1], w_out_b, w_up_b, w_down_b]
    started = _gather_chips_start(later, "gather_rest_start")
    mod_mine = lax.dynamic_index_in_dim(first[0], dev, axis=2, keepdims=False) + started[-1][0, 0]
    mods = [jnp.concatenate([mod_mine[j, l] for j in range(4)]).reshape(N_MOD, D_MODEL) for l in range(DEPTH)]
    conv_a_w_f, conv_f_w_f, gamma_f = (_unstack_chips(first[k], 2) for k in (1, 2, 3))
    w_in0 = _unstack_chips(first[4], 1)

    def later_weights(after):
        own, lands = _gather_chips_wait(started, after, "gather_rest_wait")
        full = [lax.dynamic_update_index_in_dim(land, mine, chip, 0) for land, mine in zip(lands, own)]
        return (_unstack_chips(full[0], 1), _unstack_chips(full[1], 1), _unstack_chips(full[2], 2),
                _unstack_chips(full[3], 1))

    lb1, p_soft = _lower_bounds(gamma_f.reshape(DEPTH, 2 * REC_WIDTH), "lower_bounds")
    lbs = [jnp.zeros((2, REC_WIDTH), F32), lb1.reshape(2, REC_WIDTH)]
    small = []
    for l in range(DEPTH):
        small.append(dict(norm1_w=norm1_w[l][None], conv_a_w=conv_a_w_f[l], conv_a_b=conv_a_b[l][None],
                          ln_a_w=ln_a_w[l][None], ln_a_b=ln_a_b[l][None], rec_norm_w=rec_norm_w[l],
                          norm2_w=norm2_w[l][None], conv_f_w=conv_f_w_f[l]))

    loss, dx, grads, dfw = _sequence_step(x[0], loss_target[0], mods, lbs, small, w_in0, later_weights,
                                          final_norm_w[None])
    loss = lax.psum(loss, ("x", "y", "c"))

    dgamma = _lower_bounds_bwd(grads[1]["lb"].reshape(1, 2 * REC_WIDTH), p_soft, "lower_bounds_bwd")
    dmod = [jnp.concatenate(grads[l]["mod"], axis=1) for l in range(DEPTH)]
    stack = lambda key: jnp.stack([grads[l][key] for l in range(DEPTH)])
    local_small = dict(b_ada=jnp.concatenate(dmod, axis=0), norm1_w=stack("norm1_w"), conv_a_w=stack("conv_a_w"),
                       conv_a_b=stack("conv_a_b"), ln_a_w=stack("ln_a_w"), ln_a_b=stack("ln_a_b"), lb_gamma=dgamma,
                       rec_norm_w=stack("rec_norm_w"), norm2_w=stack("norm2_w"), conv_f_w=stack("conv_f_w"),
                       final_norm_w=dfw)
    pack = _pack_rows([local_small[name] for name, _, _ in SMALL_PARAMS])
    rows = pack.shape[0]
    packs = _allgather_devices(pack, "gather_small_grads").reshape(8, rows, LANES)
    summed = _sum_devices(packs, "sum_small_grads").reshape(-1)
    small_grads = dict(zip([n for n, _, _ in SMALL_PARAMS], _unpack(summed, [shp for _, shp, _ in SMALL_PARAMS])))

    dmod_all = packs.reshape(8, rows * LANES)[:, :DEPTH * N_MOD * D_MODEL].reshape(8, DEPTH, N_MOD * D_MODEL)
    dmod_sh = lax.dynamic_slice_in_dim(dmod_all, chip * ADA_SHARD, ADA_SHARD, axis=2).transpose(1, 0, 2)
    g_ada, d_ada, m_ada, v_ada = _ada_update(c_all, dmod_sh, w_ada, m_w_ada, v_w_ada, "ada_update")

    for name, shp, axis in SMALL_PARAMS:
        if axis is not None:
            width = shp[axis] // 4
            small_grads[name] = lax.dynamic_slice_in_dim(small_grads[name], chip * width, width, axis=axis)
    names = [n for n, _, _ in SMALL_PARAMS]
    packed = [_pack_rows([src[n] for n in names])[None] for src in (params, small_grads, mom1, mom2)]
    small_out = _adamw(*packed, "adamw_small")
    shapes = [params[n].shape for n in names]
    small_delta, small_m, small_v = (dict(zip(names, _unpack(t.reshape(-1), shapes))) for t in small_out)

    core_id, chip_id = ic.astype(jnp.int32).reshape(1), chip.astype(jnp.int32).reshape(1)
    summed_big = _reduce_scatter_big([[grads[l][name] for l in range(DEPTH)] for name, _, _, _ in BIG_KINDS],
                                     core_id, chip_id)
    grad, delta, new_m, new_v = dict(small_grads), small_delta, small_m, small_v
    grad["w_ada"], delta["w_ada"], new_m["w_ada"], new_v["w_ada"] = g_ada, d_ada, m_ada, v_ada
    for (name, _, _, _), (mine, theirs) in zip(BIG_KINDS, summed_big):
        grad[name], delta[name], new_m[name], new_v[name] = _adamw_halves(
            params[name], mine, theirs, mom1[name], mom2[name], core_id, f"adamw_{name}")

    return (loss, dx[None], *[grad[n] for n in WEIGHT_NAMES], *[delta[n] for n in WEIGHT_NAMES],
            *[new_m[n] for n in WEIGHT_NAMES], *[new_v[n] for n in WEIGHT_NAMES])
```

```python
import numpy as np
import jax
import jax.numpy as jnp
from jax import lax
from jax.experimental import pallas as pl
from jax.experimental.pallas import tpu as pltpu

F32 = jnp.float32
BF16 = jnp.bfloat16

D_MODEL = 1024
DEPTH = 2
HEAD_DIM = 64
CONV_CH = 256
CONV_WIDTH = 31
ATT_WIDTH = 384
N_HEADS = 6
DILATIONS = (1, 4, 16)
ATT_HALF = 64
ATT_BLOCK = 128
ALIBI_MAX_EXP = 8.0
MASK_VALUE = -1e30
REC_WIDTH = 384
REC_CHUNK = 64
F_TINY = 1e-30
D_FF = 2816
N_MOD = 6
EPS = 1e-6
G_CONV = (0, 512)
G_QKV = (512, 1664)
G_REC = (1664, 3584)
IN_COLS = 3584

ADAM_LR = 0.001
ADAM_B1 = 0.9
ADAM_B2 = 0.999
ADAM_EPS = 1e-08
ADAM_WD = 0.01
ADAM_STEP = 10

VMEM_LIMIT_BYTES = 56 * 1024 * 1024
LANES = 128
MESH = pl.DeviceIdType.MESH
ANY = pl.BlockSpec(memory_space=pl.ANY)


def _params(n_axes):
    return pltpu.CompilerParams(dimension_semantics=("arbitrary",) * n_axes,
                                vmem_limit_bytes=VMEM_LIMIT_BYTES)


def _tile(n, target):
    best = None
    for t in range(LANES, min(n, target) + 1, LANES):
        if n % t == 0:
            best = t
    return best or n


def _sigmoid(x):
    return jax.nn.sigmoid(x)


def _silu_grad(x):
    s = _sigmoid(x)
    return s * (1.0 + x * (1.0 - s))


MM_ACC_ELEMS = 1536 * 1024


def _matmul(a, b, mode, out_dtype, name, tm=1024, tn=1792, tk=1792):
    if mode == "nn":
        (m, k), (k2, n) = a.shape, b.shape
    elif mode == "nt":
        (m, k), (n, k2) = a.shape, b.shape
    else:
        (k, m), (k2, n) = a.shape, b.shape
    assert k == k2, (a.shape, b.shape, mode)
    tn, tk = _tile(n, tn), _tile(k, tk)
    tm = _tile(m, min(tm, MM_ACC_ELEMS // tn))
    nk = k // tk
    a_spec = (pl.BlockSpec((tk, tm), lambda i, j, kk: (kk, i)) if mode == "tn"
              else pl.BlockSpec((tm, tk), lambda i, j, kk: (i, kk)))
    b_spec = (pl.BlockSpec((tn, tk), lambda i, j, kk: (j, kk)) if mode == "nt"
              else pl.BlockSpec((tk, tn), lambda i, j, kk: (kk, j)))
    dims = {"nn": (((1,), (0,)), ((), ())), "nt": (((1,), (1,)), ((), ())),
            "tn": (((0,), (0,)), ((), ()))}[mode]

    def body(a_ref, b_ref, o_ref, *scratch):
        part = lax.dot_general(a_ref[...].astype(BF16), b_ref[...].astype(BF16), dims, preferred_element_type=F32)
        if nk == 1:
            o_ref[...] = part.astype(out_dtype)
            return
        acc_ref, = scratch
        kk = pl.program_id(2)

        @pl.when(kk == 0)
        def _():
            acc_ref[...] = part

        @pl.when(kk > 0)
        def _():
            acc_ref[...] += part

        @pl.when(kk == nk - 1)
        def _():
            o_ref[...] = acc_ref[...].astype(out_dtype)

    return pl.pallas_call(
        body, name=name, grid=(m // tm, n // tn, nk),
        in_specs=[a_spec, b_spec],
        out_specs=pl.BlockSpec((tm, tn), lambda i, j, kk: (i, j)),
        out_shape=jax.ShapeDtypeStruct((m, n), out_dtype),
        scratch_shapes=[pltpu.VMEM((tm, tn), F32)] if nk > 1 else [],
        compiler_params=pltpu.CompilerParams(dimension_semantics=("parallel", "parallel", "arbitrary"),
                                             vmem_limit_bytes=VMEM_LIMIT_BYTES),
    )(a, b)


NORM_ROWS = 256


def _row_spec(width, rows=NORM_ROWS):
    return pl.BlockSpec((rows, width), lambda i: (i, 0))


def _vec_spec(width):
    return pl.BlockSpec((1, width), lambda i: (0, 0))


def _resid_norm_mod(x, r, g, nw, sc, sh, name):
    s, d = x.shape
    has_r = r is not None

    def body(*refs):
        if has_r:
            x_ref, r_ref, g_ref, nw_ref, sc_ref, sh_ref, xn_ref, h_ref = refs
            xn = x_ref[...] + g_ref[...] * r_ref[...].astype(F32)
            xn_ref[...] = xn
        else:
            x_ref, nw_ref, sc_ref, sh_ref, h_ref = refs
            xn = x_ref[...]
        rstd = lax.rsqrt(jnp.mean(xn * xn, axis=-1, keepdims=True) + EPS)
        y = xn * rstd * nw_ref[...]
        h_ref[...] = (y * (1.0 + sc_ref[...]) + sh_ref[...]).astype(BF16)

    if has_r:
        ins, in_specs = (x, r, g, nw, sc, sh), [_row_spec(d), _row_spec(d)] + [_vec_spec(d)] * 4
        out_shape = (jax.ShapeDtypeStruct((s, d), F32), jax.ShapeDtypeStruct((s, d), BF16))
        out_specs = (_row_spec(d), _row_spec(d))
    else:
        ins, in_specs = (x, nw, sc, sh), [_row_spec(d)] + [_vec_spec(d)] * 3
        out_shape = jax.ShapeDtypeStruct((s, d), BF16)
        out_specs = _row_spec(d)
    return pl.pallas_call(body, name=name, grid=(s // NORM_ROWS,), in_specs=in_specs, out_specs=out_specs,
                          out_shape=out_shape, compiler_params=_params(1))(*ins)


def _final_loss(x, r, g, fw, tgt, name):
    s, d = x.shape

    def body(x_ref, r_ref, g_ref, fw_ref, t_ref, loss_ref, dx_ref, dr_ref, dg_ref, dfw_ref):
        @pl.when(pl.program_id(0) == 0)
        def _():
            loss_ref[...] = jnp.zeros_like(loss_ref)
            dg_ref[...] = jnp.zeros_like(dg_ref)
            dfw_ref[...] = jnp.zeros_like(dfw_ref)

        rr = r_ref[...].astype(F32)
        gg = g_ref[...]
        xn = x_ref[...] + gg * rr
        rstd = lax.rsqrt(jnp.mean(xn * xn, axis=-1, keepdims=True) + EPS)
        xh = xn * rstd
        fwv = fw_ref[...]
        e = xh * fwv - t_ref[...]
        loss_ref[...] += 0.5 * jnp.sum(jnp.mean(e * e, axis=-1, keepdims=True), axis=0, keepdims=True)
        dy = e * (1.0 / d)
        dfw_ref[...] += jnp.sum(dy * xh, axis=0, keepdims=True)
        dxh = dy * fwv
        dx = rstd * (dxh - xh * jnp.mean(dxh * xh, axis=-1, keepdims=True))
        dx_ref[...] = dx
        dr_ref[...] = (gg * dx).astype(BF16)
        dg_ref[...] += jnp.sum(dx * rr, axis=0, keepdims=True)

    return pl.pallas_call(
        body, name=name, grid=(s // NORM_ROWS,),
        in_specs=[_row_spec(d), _row_spec(d), _vec_spec(d), _vec_spec(d), _row_spec(d)],
        out_specs=(_vec_spec(LANES), _row_spec(d), _row_spec(d), _vec_spec(d), _vec_spec(d)),
        out_shape=(jax.ShapeDtypeStruct((1, LANES), F32), jax.ShapeDtypeStruct((s, d), F32),
                   jax.ShapeDtypeStruct((s, d), BF16), jax.ShapeDtypeStruct((1, d), F32),
                   jax.ShapeDtypeStruct((1, d), F32)),
        compiler_params=_params(1))(x, r, g, fw, tgt)


def _norm_bwd(x, dhs, dxres, nw, sc, g, r, name):
    s, d = x.shape
    n_dh = len(dhs)
    has_g = g is not None

    def body(*refs):
        x_ref = refs[0]
        dh_refs = refs[1:1 + n_dh]
        dxres_ref, nw_ref, sc_ref = refs[1 + n_dh:4 + n_dh]
        pos = 4 + n_dh
        if has_g:
            g_ref, r_ref = refs[pos:pos + 2]
            pos += 2
            dx_ref, dr_ref, dsh_ref, dsc_ref, dnw_ref, dg_ref = refs[pos:]
            accs = (dsh_ref, dsc_ref, dnw_ref, dg_ref)
        else:
            dx_ref, dsh_ref, dsc_ref, dnw_ref = refs[pos:]
            accs = (dsh_ref, dsc_ref, dnw_ref)

        @pl.when(pl.program_id(0) == 0)
        def _():
            for acc in accs:
                acc[...] = jnp.zeros_like(acc)

        xv = x_ref[...]
        dh = dh_refs[0][...].astype(F32)
        for extra in dh_refs[1:]:
            dh = dh + extra[...].astype(F32)
        rstd = lax.rsqrt(jnp.mean(xv * xv, axis=-1, keepdims=True) + EPS)
        xh = xv * rstd
        nwv = nw_ref[...]
        dsh_ref[...] += jnp.sum(dh, axis=0, keepdims=True)
        dsc_ref[...] += jnp.sum(dh * (xh * nwv), axis=0, keepdims=True)
        dy = dh * (1.0 + sc_ref[...])
        dnw_ref[...] += jnp.sum(dy * xh, axis=0, keepdims=True)
        dxh = dy * nwv
        dx = dxres_ref[...] + rstd * (dxh - xh * jnp.mean(dxh * xh, axis=-1, keepdims=True))
        dx_ref[...] = dx
        if has_g:
            dr_ref[...] = (g_ref[...] * dx).astype(BF16)
            dg_ref[...] += jnp.sum(dx * r_ref[...].astype(F32), axis=0, keepdims=True)

    ins = [x, *dhs, dxres, nw, sc]
    in_specs = [_row_spec(d)] * (2 + n_dh) + [_vec_spec(d)] * 2
    out_shape = [jax.ShapeDtypeStruct((s, d), F32)]
    out_specs = [_row_spec(d)]
    if has_g:
        ins += [g, r]
        in_specs += [_vec_spec(d), _row_spec(d)]
        out_shape.append(jax.ShapeDtypeStruct((s, d), BF16))
        out_specs.append(_row_spec(d))
    n_vec = 4 if has_g else 3
    out_shape += [jax.ShapeDtypeStruct((1, d), F32)] * n_vec
    out_specs += [_vec_spec(d)] * n_vec
    return pl.pallas_call(body, name=name, grid=(s // NORM_ROWS,), in_specs=in_specs, out_specs=tuple(out_specs),
                          out_shape=tuple(out_shape), compiler_params=_params(1))(*ins)


FFN_ROWS = 256
FFN_COLS = 1408
HALO = 16
INV_SQRT2 = 0.7071067811865476
INV_SQRT_2PI = 0.3989422804014327


def _gelu(x):
    return 0.5 * x * (1.0 + lax.erf(x * INV_SQRT2))


def _gelu_grad(x):
    return 0.5 * (1.0 + lax.erf(x * INV_SQRT2)) + x * (INV_SQRT_2PI * jnp.exp(-0.5 * x * x))


def _halo_specs(rows, cols, halo, n_rows_total, col_of):
    per = rows // halo
    last = n_rows_total // halo - 1
    cur = pl.BlockSpec((rows, cols), lambda j, i: (i, col_of(j)))
    prev = pl.BlockSpec((halo, cols), lambda j, i: (jnp.maximum(i * per - 1, 0), col_of(j)))
    nxt = pl.BlockSpec((halo, cols), lambda j, i: (jnp.minimum((i + 1) * per, last), col_of(j)))
    return [prev, cur, nxt]


def _shift_rows(x, k):
    n = x.shape[0]
    return pltpu.roll(x, k % n, axis=0)


def _conv3(ext, w):
    return w[0:1, :] * _shift_rows(ext, 1) + w[1:2, :] * ext + w[2:3, :] * _shift_rows(ext, -1)


def _ext_block(prev_ref, cur_ref, next_ref, i, n_i):
    prev = jnp.where(i > 0, prev_ref[...].astype(F32), 0.0)
    nxt = jnp.where(i < n_i - 1, next_ref[...].astype(F32), 0.0)
    return jnp.concatenate([prev, cur_ref[...].astype(F32), nxt], axis=0)


def _ffn_act(u, cw, name):
    s = u.shape[0]
    nc, ns = D_FF // FFN_COLS, s // FFN_ROWS

    def body(gp, gc, gn, vp, vc, vn, wg_ref, wv_ref, o_ref):
        i = pl.program_id(1)
        cg = _conv3(_ext_block(gp, gc, gn, i, ns), wg_ref[...])[HALO:HALO + FFN_ROWS]
        cv = _conv3(_ext_block(vp, vc, vn, i, ns), wv_ref[...])[HALO:HALO + FFN_ROWS]
        o_ref[...] = (_gelu(cg) * cv).astype(BF16)

    in_specs = (_halo_specs(FFN_ROWS, FFN_COLS, HALO, s, lambda j: j)
                + _halo_specs(FFN_ROWS, FFN_COLS, HALO, s, lambda j: j + nc)
                + [pl.BlockSpec((3, FFN_COLS), lambda j, i: (0, j)),
                   pl.BlockSpec((3, FFN_COLS), lambda j, i: (0, j + nc))])
    return pl.pallas_call(
        body, name=name, grid=(nc, ns), in_specs=in_specs,
        out_specs=pl.BlockSpec((FFN_ROWS, FFN_COLS), lambda j, i: (i, j)),
        out_shape=jax.ShapeDtypeStruct((s, D_FF), BF16), compiler_params=_params(2),
    )(u, u, u, u, u, u, cw, cw)


def _ffn_act_bwd(u, dact, cw, name):
    s = u.shape[0]
    nc, ns = D_FF // FFN_COLS, s // FFN_ROWS

    def body(gp, gc, gn, vp, vc, vn, dp, dc, dn, wg_ref, wv_ref, dug_ref, duv_ref, dwg_ref, dwv_ref):
        i = pl.program_id(1)

        @pl.when(i == 0)
        def _():
            dwg_ref[...] = jnp.zeros_like(dwg_ref)
            dwv_ref[...] = jnp.zeros_like(dwv_ref)

        ug = _ext_block(gp, gc, gn, i, ns)
        uv = _ext_block(vp, vc, vn, i, ns)
        da = _ext_block(dp, dc, dn, i, ns)
        wg, wv = wg_ref[...], wv_ref[...]
        cg, cv = _conv3(ug, wg), _conv3(uv, wv)
        dcg = da * cv * _gelu_grad(cg)
        dcv = da * _gelu(cg)
        inner = slice(HALO, HALO + FFN_ROWS)
        for d_c, uu, w, du_ref, dw_ref in ((dcg, ug, wg, dug_ref, dwg_ref), (dcv, uv, wv, duv_ref, dwv_ref)):
            d_next, d_prev = _shift_rows(d_c, -1), _shift_rows(d_c, 1)
            du = w[0:1, :] * d_next + w[1:2, :] * d_c + w[2:3, :] * d_prev
            du_ref[...] = du[inner].astype(BF16)
            u_in = uu[inner]
            for tap, d_tap in enumerate((d_next, d_c, d_prev)):
                dw_ref[tap:tap + 1, :] += jnp.sum(d_tap[inner] * u_in, axis=0, keepdims=True)

    in_specs = (_halo_specs(FFN_ROWS, FFN_COLS, HALO, s, lambda j: j)
                + _halo_specs(FFN_ROWS, FFN_COLS, HALO, s, lambda j: j + nc)
                + _halo_specs(FFN_ROWS, FFN_COLS, HALO, s, lambda j: j)
                + [pl.BlockSpec((3, FFN_COLS), lambda j, i: (0, j)),
                   pl.BlockSpec((3, FFN_COLS), lambda j, i: (0, j + nc))])
    blk = pl.BlockSpec((FFN_ROWS, FFN_COLS), lambda j, i: (i, j))
    acc = pl.BlockSpec((HALO, FFN_COLS), lambda j, i: (0, j))
    return pl.pallas_call(
        body, name=name, grid=(nc, ns), in_specs=in_specs, out_specs=(blk, blk, acc, acc),
        out_shape=(jax.ShapeDtypeStruct((s, D_FF), BF16), jax.ShapeDtypeStruct((s, D_FF), BF16),
                   jax.ShapeDtypeStruct((HALO, D_FF), F32), jax.ShapeDtypeStruct((HALO, D_FF), F32)),
        compiler_params=_params(2),
    )(u, u, u, u, u, u, dact, dact, dact, cw, cw)


CONV_ROWS = 512
CONV_HALO = 16
CONV_PAD = CONV_WIDTH // 2


def _conv_halo_specs(cols, s):
    per = CONV_ROWS // CONV_HALO
    last = s // CONV_HALO - 1
    return [pl.BlockSpec((CONV_HALO, cols), lambda i: (jnp.maximum(i * per - 1, 0), 0)),
            pl.BlockSpec((CONV_ROWS, cols), lambda i: (i, 0)),
            pl.BlockSpec((CONV_HALO, cols), lambda i: (jnp.minimum((i + 1) * per, last), 0))]


def _glu_ext(pp, pc, pn, i, n_i):
    ext = _ext_block(pp, pc, pn, i, n_i)
    return ext[:, :CONV_CH] * _sigmoid(ext[:, CONV_CH:])


def _conv_mixer(pa, cw, cb, lnw, lnb, name):
    s = pa.shape[0]
    ns = s // CONV_ROWS

    def body(pp, pc, pn, cw_ref, cb_ref, lnw_ref, lnb_ref, o_ref, c_ref):
        i = pl.program_id(0)
        a = _glu_ext(pp, pc, pn, i, ns)
        acc = jnp.zeros((CONV_ROWS, CONV_CH), F32)
        for tap in range(CONV_WIDTH):
            acc = acc + cw_ref[tap:tap + 1, :] * _shift_rows(a, -(tap + 1))[:CONV_ROWS]
        cv = acc + cb_ref[...]
        c_ref[...] = cv
        mu = jnp.mean(cv, axis=-1, keepdims=True)
        xc = cv - mu
        rstd = lax.rsqrt(jnp.mean(xc * xc, axis=-1, keepdims=True) + EPS)
        y = xc * rstd * lnw_ref[...] + lnb_ref[...]
        o_ref[...] = (y * _sigmoid(y)).astype(BF16)

    vec = pl.BlockSpec((1, CONV_CH), lambda i: (0, 0))
    blk = pl.BlockSpec((CONV_ROWS, CONV_CH), lambda i: (i, 0))
    return pl.pallas_call(
        body, name=name, grid=(ns,),
        in_specs=_conv_halo_specs(2 * CONV_CH, s) + [pl.BlockSpec((CONV_WIDTH, CONV_CH), lambda i: (0, 0)), vec, vec, vec],
        out_specs=(blk, blk),
        out_shape=(jax.ShapeDtypeStruct((s, CONV_CH), BF16), jax.ShapeDtypeStruct((s, CONV_CH), F32)),
        compiler_params=_params(1))(pa, pa, pa, cw, cb, lnw, lnb)


def _conv_mixer_bwd_ln(cv, dout, lnw, lnb, name):
    s = cv.shape[0]

    def body(c_ref, do_ref, lnw_ref, lnb_ref, dc_ref, dlnw_ref, dlnb_ref, dcb_ref):
        @pl.when(pl.program_id(0) == 0)
        def _():
            dlnw_ref[...] = jnp.zeros_like(dlnw_ref)
            dlnb_ref[...] = jnp.zeros_like(dlnb_ref)
            dcb_ref[...] = jnp.zeros_like(dcb_ref)

        c = c_ref[...]
        mu = jnp.mean(c, axis=-1, keepdims=True)
        xc = c - mu
        rstd = lax.rsqrt(jnp.mean(xc * xc, axis=-1, keepdims=True) + EPS)
        xh = xc * rstd
        w = lnw_ref[...]
        y = xh * w + lnb_ref[...]
        dy = do_ref[...] * _silu_grad(y)
        dlnw_ref[...] += jnp.sum(dy * xh, axis=0, keepdims=True)
        dlnb_ref[...] += jnp.sum(dy, axis=0, keepdims=True)
        dxh = dy * w
        dc = rstd * (dxh - jnp.mean(dxh, axis=-1, keepdims=True) - xh * jnp.mean(dxh * xh, axis=-1, keepdims=True))
        dc_ref[...] = dc
        dcb_ref[...] += jnp.sum(dc, axis=0, keepdims=True)

    vec = pl.BlockSpec((1, CONV_CH), lambda i: (0, 0))
    blk = pl.BlockSpec((CONV_ROWS, CONV_CH), lambda i: (i, 0))
    return pl.pallas_call(
        body, name=name, grid=(s // CONV_ROWS,), in_specs=[blk, blk, vec, vec], out_specs=(blk, vec, vec, vec),
        out_shape=(jax.ShapeDtypeStruct((s, CONV_CH), F32),) + (jax.ShapeDtypeStruct((1, CONV_CH), F32),) * 3,
        compiler_params=_params(1))(cv, dout, lnw, lnb)


def _conv_mixer_bwd_conv(pa, dc, cw, name):
    s = pa.shape[0]
    ns = s // CONV_ROWS

    def body(pc, dp, dcc, dn, cw_ref, dpa_ref, dcw_ref):
        i = pl.program_id(0)

        @pl.when(i == 0)
        def _():
            dcw_ref[...] = jnp.zeros_like(dcw_ref)

        cur = pc[...]
        val, sg = cur[:, :CONV_CH], _sigmoid(cur[:, CONV_CH:])
        a_cur = val * sg
        dce = _ext_block(dp, dcc, dn, i, ns)
        da = jnp.zeros((CONV_ROWS, CONV_CH), F32)
        for tap in range(CONV_WIDTH):
            shifted = _shift_rows(dce, -(CONV_WIDTH - tap))[:CONV_ROWS]
            da = da + cw_ref[tap:tap + 1, :] * shifted
            dcw_ref[tap:tap + 1, :] += jnp.sum(shifted * a_cur, axis=0, keepdims=True)
        dpa_ref[:, :CONV_CH] = (da * sg).astype(BF16)
        dpa_ref[:, CONV_CH:] = (da * val * sg * (1.0 - sg)).astype(BF16)

    return pl.pallas_call(
        body, name=name, grid=(ns,),
        in_specs=[pl.BlockSpec((CONV_ROWS, 2 * CONV_CH), lambda i: (i, 0))] + _conv_halo_specs(CONV_CH, s)
        + [pl.BlockSpec((CONV_WIDTH, CONV_CH), lambda i: (0, 0))],
        out_specs=(pl.BlockSpec((CONV_ROWS, 2 * CONV_CH), lambda i: (i, 0)),
                   pl.BlockSpec((32, CONV_CH), lambda i: (0, 0))),
        out_shape=(jax.ShapeDtypeStruct((s, 2 * CONV_CH), BF16), jax.ShapeDtypeStruct((32, CONV_CH), F32)),
        compiler_params=_params(1))(pa, dc, dc, dc, cw)


SLOPES = tuple(float(2.0 ** (-ALIBI_MAX_EXP * (h + 1) / N_HEADS)) for h in range(N_HEADS))
ATT_SCALE = HEAD_DIM ** -0.5


PAIR = 2 * HEAD_DIM
N_PAIRS = N_HEADS // 2
ATT_WIN = ATT_BLOCK + 2 * ATT_HALF


ATT_GROUPS = {1: 4, 4: 1, 16: 1}


def _window_specs(dil, n_steps, col_of):
    per = 2 * ATT_GROUPS[dil]
    rows, halo = ATT_BLOCK * dil * ATT_GROUPS[dil], ATT_HALF * dil
    return [pl.BlockSpec((halo, PAIR), lambda i, p: (jnp.maximum(per * i - 1, 0), col_of(p))),
            pl.BlockSpec((rows, PAIR), lambda i, p: (i, col_of(p))),
            pl.BlockSpec((halo, PAIR), lambda i, p: (jnp.minimum(per * (i + 1), per * n_steps - 1), col_of(p)))]


def _residue(ref, r, n, dil, start=0):
    return ref[pl.ds(start * dil + r, n, stride=dil), :] if dil > 1 else ref[pl.ds(start + r, n), :]


def _store_residue(ref, r, dil, start, val):
    if dil > 1:
        ref[pl.ds(start * dil + r, val.shape[0], stride=dil), :] = val
    else:
        ref[pl.ds(start + r, val.shape[0]), :] = val


def _residue_window(refs, r, dil, g=0):
    prev, cur, nxt = refs
    groups = ATT_GROUPS[dil]
    lo = max(g * ATT_BLOCK - ATT_HALF, 0)
    hi = min((g + 1) * ATT_BLOCK + ATT_HALF, groups * ATT_BLOCK)
    parts = [_residue(prev, r, ATT_HALF, dil)] if g == 0 else []
    parts.append(_residue(cur, r, hi - lo, dil, lo))
    if g == groups - 1:
        parts.append(_residue(nxt, r, ATT_HALF, dil))
    return jnp.concatenate(parts, axis=0)


def _band_masks(i, length, dil, transposed):
    shape = (ATT_WIN, ATT_BLOCK) if transposed else (ATT_BLOCK, ATT_WIN)
    row = lax.broadcasted_iota(jnp.int32, shape, 0)
    col = lax.broadcasted_iota(jnp.int32, shape, 1)
    wide = row if transposed else col
    dist = jnp.abs((row - col - ATT_HALF) if transposed else (row + ATT_HALF - col))
    wpos = i * ATT_BLOCK - ATT_HALF + wide
    valid = (dist <= ATT_HALF) & (wpos >= 0) & (wpos < length)
    return valid, dist.astype(F32) * float(dil)


def _attn_branch(qkv, dil, name):
    s = qkv.shape[0]
    groups = ATT_GROUPS[dil]
    rows = ATT_BLOCK * dil * groups
    n_steps = s // rows
    length = s // dil
    nt = (((1,), (1,)), ((), ()))

    def body(q_ref, kp, kc, kn, vp, vc, vn, o_ref, l_ref):
        i, pair = pl.program_id(0), pl.program_id(1)
        items = [(g, r) for g in range(groups) for r in range(dil)]
        q = jnp.stack([_residue(q_ref, r, ATT_BLOCK, dil, g * ATT_BLOCK) for g, r in items]).astype(BF16)
        k = jnp.stack([_residue_window((kp, kc, kn), r, dil, g) for g, r in items]).astype(BF16)
        v = jnp.stack([_residue_window((vp, vc, vn), r, dil, g) for g, r in items]).astype(BF16)
        per_group = [_band_masks(i * groups + g, length, dil, False) for g in range(groups)]
        valid = jnp.stack([per_group[g][0] for g, _ in items]) if groups > 1 else per_group[0][0][None]
        distf = jnp.stack([per_group[g][1] for g, _ in items]) if groups > 1 else per_group[0][1][None]
        outs, lses = [], []
        for hh in range(2):
            sl = slice(hh * HEAD_DIM, (hh + 1) * HEAD_DIM)
            slope = jnp.where(pair == 0, SLOPES[hh], jnp.where(pair == 1, SLOPES[2 + hh], SLOPES[4 + hh]))
            sc = jnp.einsum("bqd,bkd->bqk", q[:, :, sl], k[:, :, sl], preferred_element_type=F32) * ATT_SCALE
            sc = jnp.where(valid, sc - slope * distf, MASK_VALUE)
            m = jnp.max(sc, axis=-1, keepdims=True)
            p = jnp.exp(sc - m)
            den = jnp.sum(p, axis=-1, keepdims=True)
            outs.append(jnp.einsum("bqk,bkd->bqd", p.astype(BF16), v[:, :, sl], preferred_element_type=F32) / den)
            lses.append(jnp.broadcast_to(m + jnp.log(den), (len(items), ATT_BLOCK, HEAD_DIM)))
        o_all, l_all = jnp.concatenate(outs, axis=2), jnp.concatenate(lses, axis=2)
        for n, (g, r) in enumerate(items):
            _store_residue(o_ref, r, dil, g * ATT_BLOCK, o_all[n])
            _store_residue(l_ref, r, dil, g * ATT_BLOCK, l_all[n])

    out_blk = pl.BlockSpec((rows, PAIR), lambda i, p: (i, p))
    return pl.pallas_call(
        body, name=name, grid=(n_steps, N_PAIRS),
        in_specs=[pl.BlockSpec((rows, PAIR), lambda i, p: (i, p))]
        + _window_specs(dil, n_steps, lambda p: N_PAIRS + p) + _window_specs(dil, n_steps, lambda p: 2 * N_PAIRS + p),
        out_specs=(out_blk, out_blk),
        out_shape=(jax.ShapeDtypeStruct((s, ATT_WIDTH), F32),) * 2,
        compiler_params=_params(2))(qkv, qkv, qkv, qkv, qkv, qkv, qkv)


ATT_ROWS = 512


def _attn_combine(outs, lses, name):
    s = outs[0].shape[0]

    def body(o1, o2, o3, l1, l2, l3, att_ref, att32_ref, lse_ref):
        ls = [l1[...], l2[...], l3[...]]
        m = jnp.maximum(jnp.maximum(ls[0], ls[1]), ls[2])
        es = [jnp.exp(l - m) for l in ls]
        den = es[0] + es[1] + es[2]
        att = (es[0] * o1[...] + es[1] * o2[...] + es[2] * o3[...]) / den
        att_ref[...] = att.astype(BF16)
        att32_ref[...] = att
        lse_ref[...] = m + jnp.log(den)

    blk = pl.BlockSpec((ATT_ROWS, ATT_WIDTH), lambda i: (i, 0))
    return pl.pallas_call(
        body, name=name, grid=(s // ATT_ROWS,), in_specs=[blk] * 6, out_specs=(blk, blk, blk),
        out_shape=(jax.ShapeDtypeStruct((s, ATT_WIDTH), BF16), jax.ShapeDtypeStruct((s, ATT_WIDTH), F32),
                   jax.ShapeDtypeStruct((s, ATT_WIDTH), F32)),
        compiler_params=_params(1))(*outs, *lses)


def _attn_delta(datt, att, name):
    s = att.shape[0]

    def body(d_ref, a_ref, delta_ref):
        prod = d_ref[...] * a_ref[...]
        for h in range(N_HEADS):
            sl = slice(h * HEAD_DIM, (h + 1) * HEAD_DIM)
            delta_ref[:, sl] = jnp.broadcast_to(jnp.sum(prod[:, sl], axis=-1, keepdims=True), (ATT_ROWS, HEAD_DIM))

    blk = pl.BlockSpec((ATT_ROWS, ATT_WIDTH), lambda i: (i, 0))
    return pl.pallas_call(
        body, name=name, grid=(s // ATT_ROWS,), in_specs=[blk, blk], out_specs=blk,
        out_shape=jax.ShapeDtypeStruct((s, ATT_WIDTH), F32), compiler_params=_params(1))(datt, att)


def _attn_branch_bwd(qkv, do, lse, delta, prev, dil, name):
    s = qkv.shape[0]
    groups = ATT_GROUPS[dil]
    rows = ATT_BLOCK * dil * groups
    n_steps = s // rows
    length = s // dil
    has_prev = prev is not None
    tn = (((0,), (0,)), ((), ()))
    nt = (((1,), (1,)), ((), ()))

    def body(*refs):
        qs, ks, vs, dos, ls, des = (refs[3 * n:3 * n + 3] for n in range(6))
        rest = refs[18:]
        if has_prev:
            pq, pk, pv = rest[:3]
            rest = rest[3:]
        dq_ref, dk_ref, dv_ref = rest
        i, pair = pl.program_id(0), pl.program_id(1)
        items = [(g, r) for g in range(groups) for r in range(dil)]
        cur = lambda t: jnp.stack([_residue(t[1], r, ATT_BLOCK, dil, g * ATT_BLOCK) for g, r in items])
        win = lambda t: jnp.stack([_residue_window(t, r, dil, g) for g, r in items])
        q_cur, k_cur, v_cur, do_cur = (cur(t).astype(BF16) for t in (qs, ks, vs, dos))
        q_win, k_win, v_win, do_win = (win(t).astype(BF16) for t in (qs, ks, vs, dos))
        l_cur, de_cur, l_win, de_win = cur(ls), cur(des), win(ls), win(des)

        def masks(transposed):
            per_group = [_band_masks(i * groups + g, length, dil, transposed) for g in range(groups)]
            if groups == 1:
                return per_group[0][0][None], per_group[0][1][None]
            return jnp.stack([per_group[g][0] for g, _ in items]), jnp.stack([per_group[g][1] for g, _ in items])

        valid_q, distf_q = masks(False)
        valid_k, distf_k = masks(True)
        dot = lambda eq, a, b: jnp.einsum(eq, a, b, preferred_element_type=F32)
        dqs, dks, dvs = [], [], []
        for hh in range(2):
            sl = slice(hh * HEAD_DIM, (hh + 1) * HEAD_DIM)
            one = slice(hh * HEAD_DIM, hh * HEAD_DIM + 1)
            slope = jnp.where(pair == 0, SLOPES[hh], jnp.where(pair == 1, SLOPES[2 + hh], SLOPES[4 + hh]))
            sc = dot("bqd,bkd->bqk", q_cur[:, :, sl], k_win[:, :, sl]) * ATT_SCALE - slope * distf_q
            p = jnp.exp(jnp.where(valid_q, sc - l_cur[:, :, one], MASK_VALUE))
            dp = dot("bqd,bkd->bqk", do_cur[:, :, sl], v_win[:, :, sl])
            ds = (p * (dp - de_cur[:, :, one]) * ATT_SCALE).astype(BF16)
            dqs.append(dot("bqk,bkd->bqd", ds, k_win[:, :, sl]))

            sc2 = dot("bqd,bkd->bqk", q_win[:, :, sl], k_cur[:, :, sl]) * ATT_SCALE - slope * distf_k
            p2 = jnp.exp(jnp.where(valid_k, sc2 - l_win[:, :, one], MASK_VALUE))
            dvs.append(dot("bqk,bqd->bkd", p2.astype(BF16), do_win[:, :, sl]))
            dp2 = dot("bqd,bkd->bqk", do_win[:, :, sl], v_cur[:, :, sl])
            ds2 = (p2 * (dp2 - de_win[:, :, one]) * ATT_SCALE).astype(BF16)
            dks.append(dot("bqk,bqd->bkd", ds2, q_win[:, :, sl]))
        for parts, acc, out in ((dqs, pq if has_prev else None, dq_ref), (dks, pk if has_prev else None, dk_ref),
                                (dvs, pv if has_prev else None, dv_ref)):
            val = jnp.concatenate(parts, axis=2)
            for n, (g, r) in enumerate(items):
                piece = val[n]
                if has_prev:
                    piece = piece + _residue(acc, r, ATT_BLOCK, dil, g * ATT_BLOCK)
                _store_residue(out, r, dil, g * ATT_BLOCK, piece)

    blk = pl.BlockSpec((rows, PAIR), lambda i, p: (i, p))
    in_specs = (_window_specs(dil, n_steps, lambda p: p) + _window_specs(dil, n_steps, lambda p: N_PAIRS + p)
                + _window_specs(dil, n_steps, lambda p: 2 * N_PAIRS + p) + _window_specs(dil, n_steps, lambda p: p) * 3)
    ins = [qkv] * 9 + [do] * 3 + [lse] * 3 + [delta] * 3
    if has_prev:
        in_specs += [blk] * 3
        ins += list(prev)
    return pl.pallas_call(
        body, name=name, grid=(n_steps, N_PAIRS), in_specs=in_specs, out_specs=(blk, blk, blk),
        out_shape=(jax.ShapeDtypeStruct((s, ATT_WIDTH), F32),) * 3,
        compiler_params=_params(2))(*ins)


TB = 2 * REC_CHUNK
REC_ROWS = 5 * REC_WIDTH


REC_LEVELS = 6


def _scan_pos(p, rev):
    p = p & (REC_CHUNK - 1)
    return (REC_CHUNK - 1 - p) if rev else p


def _split3(x):
    hi = x.astype(BF16)
    rest = x - hi.astype(F32)
    mid = rest.astype(BF16)
    return hi, mid, (rest - mid.astype(F32)).astype(BF16)


def _chunk_sums(x, rev, with_levels):
    row = lax.broadcasted_iota(jnp.int32, (TB, TB), 0)
    col = lax.broadcasted_iota(jnp.int32, (TB, TB), 1)
    same = (row < REC_CHUNK) == (col < REC_CHUNK)
    s_row, s_col = _scan_pos(row, rev), _scan_pos(col, rev)
    mats = [same & (s_row <= s_col)]
    if with_levels:
        for level in range(1, REC_LEVELS + 1):
            shift = REC_LEVELS + 1 - level
            boundary = ((s_col >> shift) << shift) + (REC_CHUNK >> level) - 1
            mats.append(same & (s_row <= boundary))
        mats.append(same)
    cat = jnp.concatenate([m.astype(BF16) for m in mats], axis=1)
    total = sum(jnp.dot(term, cat, preferred_element_type=F32) for term in _split3(x))
    return [total[:, n * TB:(n + 1) * TB] for n in range(len(mats))]


def _hg_prep(qraw, z, lb, rev):
    lane = lax.broadcasted_iota(jnp.int32, (REC_WIDTH, TB), 1)
    in_a = lane < REC_CHUNK
    scan = _scan_pos(lane, rev)
    sig, sigm = _sigmoid(z), _sigmoid(-z)
    f = lb + (1.0 - lb) * sig
    kk = (1.0 - lb) * sigm
    sums = _chunk_sums(jnp.log(jnp.maximum(f, F_TINY)), rev, True)
    b, bend = sums[0], sums[-1]
    q = qraw * _sigmoid(qraw)
    eq, ek = [], []
    for level in range(1, REC_LEVELS + 1):
        r = sums[level]
        e = jnp.exp(jnp.minimum(b - r, r - b))
        second = ((scan >> (REC_LEVELS - level)) & 1) == 1
        eq.append(jnp.where(second, e, 0.0))
        ek.append(jnp.where(second, 0.0, e))
    lanes_end = (0, REC_CHUNK) if rev else (REC_CHUNK - 1, TB - 1)
    end_a, end_b = (b[:, n:n + 1] for n in lanes_end)
    return dict(in_a=in_a, sig=sig, sigm=sigm, f=f, kk=kk, b=b, end_a=end_a, end_b=end_b,
                q=q, qh=q * jnp.exp(b), kh=kk * jnp.exp(bend - b), ekb=jnp.exp(bend - b), eq=eq, ek=ek)


def _level_masks(rev):
    row = lax.broadcasted_iota(jnp.int32, (TB, TB), 0)
    col = lax.broadcasted_iota(jnp.int32, (TB, TB), 1)
    same = (row < REC_CHUNK) == (col < REC_CHUNK)
    s_row, s_col = _scan_pos(row, rev), _scan_pos(col, rev)
    masks = [same & ((s_row >> (REC_LEVELS + 1 - level)) == (s_col >> (REC_LEVELS + 1 - level)))
             for level in range(1, REC_LEVELS + 1)]
    return masks, row == col


def _head_rows(x, h):
    return x[h * HEAD_DIM:(h + 1) * HEAD_DIM, :]


def _block_diag_mask():
    r = lax.broadcasted_iota(jnp.int32, (REC_WIDTH, REC_WIDTH), 0) // HEAD_DIM
    c = lax.broadcasted_iota(jnp.int32, (REC_WIDTH, REC_WIDTH), 1) // HEAD_DIM
    return (r == c).astype(F32)


def _heads(x):
    return x.reshape(N_HEADS, HEAD_DIM, TB)


def _hgrn_scan(projt, lb, rev, name):
    s = projt.shape[1]
    nblk = s // TB
    zrow = 2 if rev else 1
    tmap = (lambda i: nblk - 1 - i) if rev else (lambda i: i)
    tn = (((0,), (0,)), ((), ()))
    nt = (((1,), (1,)), ((), ()))

    def body(q_ref, z_ref, v_ref, lb_ref, o_ref, hs_ref, at_ref, h_ref):
        @pl.when(pl.program_id(0) == 0)
        def _():
            h_ref[...] = jnp.zeros_like(h_ref)

        v = v_ref[...]
        vb = v.astype(BF16)
        pr = _hg_prep(q_ref[...], z_ref[...], lb_ref[...], rev)
        q, kk = pr["q"], pr["kk"]
        masks, diag = _level_masks(rev)
        own = jnp.sum(_heads(q * kk), axis=1, keepdims=True)
        sc = jnp.where(diag[None], own, 0.0)
        for level in range(REC_LEVELS):
            qt = _heads((q * pr["eq"][level]).astype(BF16))
            kt = _heads((kk * pr["ek"][level]).astype(BF16))
            sc = sc + jnp.where(masks[level][None],
                                jnp.einsum("hks,hkt->hst", kt, qt, preferred_element_type=F32), 0.0)
        a_bf = sc.astype(BF16)
        at_ref[...] = a_bf
        o = jnp.einsum("hvs,hst->hvt", _heads(vb), a_bf, preferred_element_type=F32).reshape(REC_WIDTH, TB)
        bd_mask = _block_diag_mask()
        order = ((1, ~pr["in_a"], pr["end_b"]), (0, pr["in_a"], pr["end_a"]))
        if not rev:
            order = order[::-1]
        for slot, msk, bend in order:
            h0 = h_ref[...]
            hs_ref[slot] = h0
            o = o + lax.dot_general(h0.astype(BF16), jnp.where(msk, pr["qh"], 0.0).astype(BF16), tn,
                                    preferred_element_type=F32)
            upd = lax.dot_general(jnp.where(msk, pr["kh"], 0.0).astype(BF16), vb, nt, preferred_element_type=F32)
            h_ref[...] = jnp.exp(bend) * h0 + upd * bd_mask
        o_ref[...] = o

    row_blk = lambda r: pl.BlockSpec((REC_WIDTH, TB), lambda i: (r, tmap(i)))
    return pl.pallas_call(
        body, name=name, grid=(nblk,),
        in_specs=[row_blk(0), row_blk(zrow), row_blk(3), pl.BlockSpec((REC_WIDTH, 1), lambda i: (0, 0))],
        out_specs=(pl.BlockSpec((REC_WIDTH, TB), lambda i: (0, tmap(i))),
                   pl.BlockSpec((2, REC_WIDTH, REC_WIDTH), lambda i: (tmap(i), 0, 0)),
                   pl.BlockSpec((None, N_HEADS, TB, TB), lambda i: (tmap(i), 0, 0, 0))),
        out_shape=(jax.ShapeDtypeStruct((REC_WIDTH, s), F32),
                   jax.ShapeDtypeStruct((s // REC_CHUNK, REC_WIDTH, REC_WIDTH), F32),
                   jax.ShapeDtypeStruct((nblk, N_HEADS, TB, TB), BF16)),
        scratch_shapes=[pltpu.VMEM((REC_WIDTH, REC_WIDTH), F32)],
        compiler_params=_params(1))(projt, projt, projt, lb)


def _hgrn_scan_bwd(projt, lb, dot, hs, at, prev, rev, name):
    s = projt.shape[1]
    nblk = s // TB
    zrow = 2 if rev else 1
    tmap = (lambda i: i) if rev else (lambda i: nblk - 1 - i)
    has_prev = prev is not None
    tn = (((0,), (0,)), ((), ()))
    nt = (((1,), (1,)), ((), ()))

    def body(*refs):
        q_ref, z_ref, v_ref, lb_ref, do_ref, hs_ref, at_ref = refs[:7]
        rest = refs[7:]
        if has_prev:
            pq_ref, pv_ref = rest[:2]
            rest = rest[2:]
        dq_ref, dz_ref, dv_ref, dlb_ref, dh_ref = rest

        @pl.when(pl.program_id(0) == 0)
        def _():
            dh_ref[...] = jnp.zeros_like(dh_ref)
            dlb_ref[...] = jnp.zeros_like(dlb_ref)

        qraw, v, do, lbv = q_ref[...], v_ref[...], do_ref[...], lb_ref[...]
        dob, vb = do.astype(BF16), v.astype(BF16)
        pr = _hg_prep(qraw, z_ref[...], lbv, rev)
        q, kk, b, in_a = pr["q"], pr["kk"], pr["b"], pr["in_a"]
        masks, diag = _level_masks(rev)
        dot = lambda eq, x, y: jnp.einsum(eq, x, y, preferred_element_type=F32)
        d_at = dot("hvs,hvt->hst", _heads(vb), _heads(dob))
        dv = dot("hvt,hst->hvs", _heads(dob), at_ref[...]).reshape(REC_WIDTH, TB)
        d_own = jnp.sum(jnp.where(diag[None], d_at, 0.0), axis=1, keepdims=True)
        dq_in = (d_own * _heads(kk)).reshape(REC_WIDTH, TB)
        dk_in = (d_own * _heads(q)).reshape(REC_WIDTH, TB)
        db_in = jnp.zeros((REC_WIDTH, TB), F32)
        for lv in range(REC_LEVELS):
            d_lv = jnp.where(masks[lv][None], d_at, 0.0).astype(BF16)
            q_lv, k_lv = (q * pr["eq"][lv]).astype(BF16), (kk * pr["ek"][lv]).astype(BF16)
            dqt = dot("hks,hst->hkt", _heads(k_lv), d_lv).reshape(REC_WIDTH, TB)
            dkt = dot("hkt,hst->hks", _heads(q_lv), d_lv).reshape(REC_WIDTH, TB)
            dq_in = dq_in + pr["eq"][lv] * dqt
            dk_in = dk_in + pr["ek"][lv] * dkt
            db_in = db_in + q_lv.astype(F32) * dqt - k_lv.astype(F32) * dkt
        dq = dk = jnp.zeros((REC_WIDTH, TB), F32)

        zero = jnp.zeros((REC_WIDTH, TB), F32)
        bd_mask = _block_diag_mask()
        eb = jnp.exp(b)
        const = zero
        order = ((0, in_a, pr["end_a"]), (1, ~in_a, pr["end_b"]))
        if not rev:
            order = order[::-1]
        for slot, msk, bend in order:
            h0 = hs_ref[slot]
            dh1 = dh_ref[...]
            dh1b = dh1.astype(BF16)
            dq = dq + eb * jnp.dot(h0.astype(BF16), jnp.where(msk, do, 0.0).astype(BF16), preferred_element_type=F32)
            dv = dv + lax.dot_general(dh1b, jnp.where(msk, pr["kh"], 0.0).astype(BF16), tn, preferred_element_type=F32)
            dk_int = pr["ekb"] * jnp.dot(dh1b, jnp.where(msk, v, 0.0).astype(BF16), preferred_element_type=F32)
            dk = dk + dk_int
            ebend = jnp.exp(bend)
            c = (jnp.sum(kk * dk_int, axis=1, keepdims=True)
                 + ebend * jnp.sum(h0 * dh1, axis=1, keepdims=True))
            const = const + jnp.where(msk, c, 0.0)
            upd = lax.dot_general(jnp.where(msk, pr["qh"], 0.0).astype(BF16), dob, nt, preferred_element_type=F32)
            dh_ref[...] = ebend * dh1 + upd * bd_mask

        dg = _chunk_sums(db_in + q * dq - kk * dk, not rev, False)[0] + const
        dq, dk = dq + dq_in, dk + dk_in
        sig, sigm, f = pr["sig"], pr["sigm"], pr["f"]
        live = f > F_TINY
        inv_f = 1.0 / jnp.maximum(f, F_TINY)
        one_lb = 1.0 - lbv
        dz = sig * sigm * one_lb * (jnp.where(live, dg * inv_f, 0.0) - dk)
        dlb_ref[...] += jnp.sum(sigm * (jnp.where(live, dg * inv_f, 0.0) - dk), axis=1, keepdims=True)
        dqr = dq * _silu_grad(qraw)
        if has_prev:
            dqr = dqr + pq_ref[...]
            dv = dv + pv_ref[...]
        dq_ref[...] = dqr
        dz_ref[...] = dz
        dv_ref[...] = dv

    row_blk = lambda r: pl.BlockSpec((REC_WIDTH, TB), lambda i: (r, tmap(i)))
    blk = pl.BlockSpec((REC_WIDTH, TB), lambda i: (0, tmap(i)))
    col = pl.BlockSpec((REC_WIDTH, 1), lambda i: (0, 0))
    in_specs = [row_blk(0), row_blk(zrow), row_blk(3), col, blk,
                pl.BlockSpec((2, REC_WIDTH, REC_WIDTH), lambda i: (tmap(i), 0, 0)),
                pl.BlockSpec((None, N_HEADS, TB, TB), lambda i: (tmap(i), 0, 0, 0))]
    ins = [projt, projt, projt, lb, dot, hs, at]
    if has_prev:
        in_specs += [blk, blk]
        ins += list(prev)
    t_shape = jax.ShapeDtypeStruct((REC_WIDTH, s), F32)
    return pl.pallas_call(
        body, name=name, grid=(nblk,), in_specs=in_specs, out_specs=(blk, blk, blk, col),
        out_shape=(t_shape, t_shape, t_shape, jax.ShapeDtypeStruct((REC_WIDTH, 1), F32)),
        scratch_shapes=[pltpu.VMEM((REC_WIDTH, REC_WIDTH), F32)],
        compiler_params=_params(1))(*ins)


REC_OUT_COLS = 512


def _head_rms(o):
    o3 = o.reshape(N_HEADS, HEAD_DIM, o.shape[1])
    rstd = lax.rsqrt(jnp.mean(o3 * o3, axis=1, keepdims=True) + EPS)
    return o3 * rstd, rstd


def _hgrn_out(of, ob, projt, wn, name):
    s = of.shape[1]

    def body(of_ref, ob_ref, g_ref, wn_ref, o_ref):
        on, _ = _head_rms(of_ref[...] + ob_ref[...])
        g = g_ref[...]
        y = on.reshape(REC_WIDTH, REC_OUT_COLS) * wn_ref[...] * (g * _sigmoid(g))
        o_ref[...] = y.T.astype(BF16)

    blk = pl.BlockSpec((REC_WIDTH, REC_OUT_COLS), lambda i: (0, i))
    return pl.pallas_call(
        body, name=name, grid=(s // REC_OUT_COLS,),
        in_specs=[blk, blk, pl.BlockSpec((REC_WIDTH, REC_OUT_COLS), lambda i: (4, i)),
                  pl.BlockSpec((REC_WIDTH, 1), lambda i: (0, 0))],
        out_specs=pl.BlockSpec((REC_OUT_COLS, REC_WIDTH), lambda i: (i, 0)),
        out_shape=jax.ShapeDtypeStruct((s, REC_WIDTH), BF16), compiler_params=_params(1))(of, ob, projt, wn)


def _hgrn_out_bwd(drec, of, ob, projt, wn, name):
    s = of.shape[1]

    def body(d_ref, of_ref, ob_ref, g_ref, wn_ref, do_ref, dg_ref, dwn_ref):
        @pl.when(pl.program_id(0) == 0)
        def _():
            dwn_ref[...] = jnp.zeros_like(dwn_ref)

        dy = d_ref[...].T
        on3, rstd = _head_rms(of_ref[...] + ob_ref[...])
        on = on3.reshape(REC_WIDTH, REC_OUT_COLS)
        g, wnv = g_ref[...], wn_ref[...]
        dg_ref[...] = dy * on * wnv * _silu_grad(g)
        d_onw = dy * (g * _sigmoid(g))
        dwn_ref[...] += jnp.sum(d_onw * on, axis=1, keepdims=True)
        d_on3 = (d_onw * wnv).reshape(N_HEADS, HEAD_DIM, REC_OUT_COLS)
        do3 = rstd * (d_on3 - on3 * jnp.mean(d_on3 * on3, axis=1, keepdims=True))
        do_ref[...] = do3.reshape(REC_WIDTH, REC_OUT_COLS)

    blk = pl.BlockSpec((REC_WIDTH, REC_OUT_COLS), lambda i: (0, i))
    col = pl.BlockSpec((REC_WIDTH, 1), lambda i: (0, 0))
    t_shape = jax.ShapeDtypeStruct((REC_WIDTH, s), F32)
    return pl.pallas_call(
        body, name=name, grid=(s // REC_OUT_COLS,),
        in_specs=[pl.BlockSpec((REC_OUT_COLS, REC_WIDTH), lambda i: (i, 0)), blk, blk,
                  pl.BlockSpec((REC_WIDTH, REC_OUT_COLS), lambda i: (4, i)), col],
        out_specs=(blk, blk, col),
        out_shape=(t_shape, t_shape, jax.ShapeDtypeStruct((REC_WIDTH, 1), F32)),
        compiler_params=_params(1))(drec, of, ob, projt, wn)


def _lower_bounds(gamma, name):
    def body(g_ref, lb_ref, p_ref):
        g0, g1 = g_ref[0:1, :], g_ref[1:2, :]
        m = jnp.maximum(g0, g1)
        e0, e1 = jnp.exp(g0 - m), jnp.exp(g1 - m)
        p0, p1 = e0 / (e0 + e1), e1 / (e0 + e1)
        lb_ref[...] = (p0 + p1) - p0
        p_ref[0:1, :] = p0
        p_ref[1:2, :] = p1

    n = gamma.shape[1]
    return pl.pallas_call(body, name=name,
                          out_shape=(jax.ShapeDtypeStruct((1, n), F32), jax.ShapeDtypeStruct((2, n), F32)))(gamma)


def _lower_bounds_bwd(dlb1, p, name):
    def body(d_ref, p_ref, o_ref):
        p0, p1, d = p_ref[0:1, :], p_ref[1:2, :], d_ref[...]
        inner = p1 * d
        o_ref[0:1, :] = p0 * (0.0 - inner)
        o_ref[1:2, :] = p1 * (d - inner)

    return pl.pallas_call(body, name=name, out_shape=jax.ShapeDtypeStruct(p.shape, F32))(dlb1, p)


def _split_w_in(w_in):
    return dict(conv=w_in[:, G_CONV[0]:G_CONV[1]], qkv=w_in[:, G_QKV[0]:G_QKV[1]],
                rec_t=w_in[:, G_REC[0]:].T, nat=w_in[:, :G_REC[0]])


def _split_w_rest(w_out, w_up, w_down):
    return dict(out=w_out, out_a=w_out[:CONV_CH], out_b=w_out[CONV_CH:CONV_CH + ATT_WIDTH],
                out_c=w_out[CONV_CH + ATT_WIDTH:], up=w_up, down=w_down)


def _col(v):
    return v.reshape(-1, 1)


def _sequence_step(x, tgt, mods, lbs, small, w_in0, later_weights, final_w):
    saved = []
    xin = x
    big = [_split_w_in(w_in0), None]
    h1 = _resid_norm_mod(x, None, None, small[0]["norm1_w"], mods[0][1:2], mods[0][0:1], "norm1_first")
    for l in range(DEPTH):
        sm, w, md = small[l], big[l], mods[l]
        pa = _matmul(h1, w["conv"], "nn", F32, f"proj_conv")
        qkv = _matmul(h1, w["qkv"], "nn", F32, f"proj_qkv")
        projt = _matmul(w["rec_t"], h1, "nt", F32, f"proj_rec")
        a_out, cv = _conv_mixer(pa, sm["conv_a_w"], sm["conv_a_b"], sm["ln_a_w"], sm["ln_a_b"], f"conv_mixer")
        outs, lses = zip(*[_attn_branch(qkv, d, f"attn_d{d}") for d in DILATIONS])
        att, att32, lse = _attn_combine(outs, lses, f"attn_combine")
        lb_f, lb_b = _col(lbs[l][0]), _col(lbs[l][1])
        of, hsf, atf = _hgrn_scan(projt, lb_f, False, "hgrn_fwd")
        ob, hsb, atb = _hgrn_scan(projt, lb_b, True, "hgrn_rev")
        wn = _col(sm["rec_norm_w"])
        rec = _hgrn_out(of, ob, projt, wn, f"hgrn_out")
        mixed = jnp.concatenate([a_out, att, rec], axis=1)
        if l == 0:
            w_in1, w_out_all, w_up_all, w_down_all = later_weights(rec)
            big[0].update(_split_w_rest(w_out_all[0], w_up_all[0], w_down_all[0]))
            big[1] = dict(_split_w_in(w_in1), **_split_w_rest(w_out_all[1], w_up_all[1], w_down_all[1]))
        r1 = _matmul(mixed, w["out"], "nn", BF16, "out_proj")
        xmid, h2 = _resid_norm_mod(xin, r1, md[2:3], sm["norm2_w"], md[4:5], md[3:4], f"norm2")
        u = _matmul(h2, w["up"], "nn", BF16, f"ffn_up")
        act = _ffn_act(u, sm["conv_f_w"], f"ffn_act")
        r2 = _matmul(act, w["down"], "nn", BF16, "ffn_down")
        saved.append(dict(xin=xin, h1=h1, pa=pa, qkv=qkv, projt=projt, cv=cv, att32=att32, lse=lse, of=of, ob=ob,
                          hsf=hsf, hsb=hsb, atf=atf, atb=atb, lb_f=lb_f, lb_b=lb_b, wn=wn, mixed=mixed, r1=r1, xmid=xmid, h2=h2,
                          u=u, act=act, r2=r2))
        if l + 1 < DEPTH:
            nxt = small[l + 1]
            xin, h1 = _resid_norm_mod(xmid, r2, md[5:6], nxt["norm1_w"], mods[l + 1][1:2], mods[l + 1][0:1],
                                      "norm1")
    top = saved[-1]
    loss, dx, dr2, dg2, dfw = _final_loss(top["xmid"], top["r2"], mods[-1][5:6], final_w, tgt, "final_loss")

    grads = [None] * DEPTH
    for l in reversed(range(DEPTH)):
        sm, w, md, sv = small[l], big[l], mods[l], saved[l]
        dact = _matmul(dr2, w["down"], "nt", BF16, f"d_act")
        g_down = _matmul(dr2, sv["act"], "tn", F32, "dw_down").T
        dug, duv, dwg, dwv = _ffn_act_bwd(sv["u"], dact, sm["conv_f_w"], f"ffn_act_bwd")
        du = jnp.concatenate([dug, duv], axis=1)
        dh2 = _matmul(du, w["up"], "nt", BF16, "d_h2")
        g_up = _matmul(sv["h2"], du, "tn", F32, f"dw_up")
        dxmid, dr1, dsh2, dsc2, dnw2, dg1 = _norm_bwd(sv["xmid"], [dh2], dx, sm["norm2_w"], md[4:5], md[2:3], sv["r1"],
                                                     f"norm2_bwd")
        dmix_a = _matmul(dr1, w["out_a"], "nt", F32, f"d_mix_a")
        dmix_b = _matmul(dr1, w["out_b"], "nt", F32, f"d_mix_b")
        dmix_c = _matmul(dr1, w["out_c"], "nt", F32, f"d_mix_c")
        g_out = _matmul(sv["mixed"], dr1, "tn", F32, f"dw_out")
        dc, dlnw, dlnb, dcb = _conv_mixer_bwd_ln(sv["cv"], dmix_a, sm["ln_a_w"], sm["ln_a_b"], f"conv_mixer_bwd_ln")
        dpa, dcw = _conv_mixer_bwd_conv(sv["pa"], dc, sm["conv_a_w"], f"conv_mixer_bwd_conv")
        delta = _attn_delta(dmix_b, sv["att32"], "attn_delta")
        dqkv = None
        for d in DILATIONS:
            dqkv = _attn_branch_bwd(sv["qkv"], dmix_b, sv["lse"], delta, dqkv, d, f"attn_bwd_d{d}")
        dot, dgt, dwn = _hgrn_out_bwd(dmix_c, sv["of"], sv["ob"], sv["projt"], sv["wn"], f"hgrn_out_bwd")
        dqf, dzf, dvf, dlbf = _hgrn_scan_bwd(sv["projt"], sv["lb_f"], dot, sv["hsf"], sv["atf"], None, False,
                                             "hgrn_fwd_bwd")
        dqt, dzb, dvt, dlbb = _hgrn_scan_bwd(sv["projt"], sv["lb_b"], dot, sv["hsb"], sv["atb"], (dqf, dvf), True,
                                             "hgrn_rev_bwd")
        dprojt = jnp.concatenate([dqt, dzf, dzb, dvt, dgt], axis=0).astype(BF16)
        dnat = jnp.concatenate([dpa] + [t.astype(BF16) for t in dqkv], axis=1)
        dh1_a = _matmul(dnat, w["nat"], "nt", BF16, "d_h1_nat")
        dh1_b = _matmul(dprojt, w["rec_t"], "tn", BF16, "d_h1_rec")
        g_in_nat = _matmul(sv["h1"], dnat, "tn", F32, f"dw_in_nat")
        g_in_rec_t = _matmul(dprojt, sv["h1"], "nn", F32, f"dw_in_rec")
        g_in = jnp.concatenate([g_in_nat, g_in_rec_t.T], axis=1)
        if l > 0:
            below = saved[l - 1]
            dx, dr2, dsh1, dsc1, dnw1, dg2_below = _norm_bwd(sv["xin"], [dh1_a, dh1_b], dxmid, sm["norm1_w"], md[1:2],
                                                            mods[l - 1][5:6], below["r2"], f"norm1_bwd")
        else:
            dx, dsh1, dsc1, dnw1 = _norm_bwd(sv["xin"], [dh1_a, dh1_b], dxmid, sm["norm1_w"], md[1:2], None, None,
                                             f"norm1_bwd")
        grads[l] = dict(w_in=g_in, w_out=g_out, w_up=g_up, w_down=g_down,
                        mod=[dsh1, dsc1, dg1, dsh2, dsc2, dg2], norm1_w=dnw1, conv_a_w=dcw[:CONV_WIDTH], conv_a_b=dcb,
                        ln_a_w=dlnw, ln_a_b=dlnb, lb=jnp.concatenate([dlbf.reshape(1, -1), dlbb.reshape(1, -1)], axis=0),
                        rec_norm_w=dwn.reshape(1, -1), norm2_w=dnw2,
                        conv_f_w=jnp.concatenate([dwg[:3], dwv[:3]], axis=1))
        if l > 0:
            dg2 = dg2_below
    return loss[0, 0], dx, grads, dfw


def _adamw_math(w, g, m, v):
    m = ADAM_B1 * m + (1.0 - ADAM_B1) * g
    v = ADAM_B2 * v + (1.0 - ADAM_B2) * (g * g)
    m_hat = m / (1.0 - ADAM_B1 ** ADAM_STEP)
    v_hat = v / (1.0 - ADAM_B2 ** ADAM_STEP)
    delta = -ADAM_LR * (m_hat / (jnp.sqrt(v_hat) + ADAM_EPS) + ADAM_WD * w)
    return delta, m, v


def _row_tile(rows, cols, max_elems=384 * 1024):
    best = None
    for t in range(8, rows + 1, 8):
        if rows % t == 0 and t * cols <= max_elems:
            best = t
    return best or rows


def _adamw(w, g, m, v, name):
    nl, r, c = w.shape
    tr = _row_tile(r, c)

    def body(w_ref, g_ref, m_ref, v_ref, d_ref, m2_ref, v2_ref):
        d_ref[...], m2_ref[...], v2_ref[...] = _adamw_math(w_ref[...], g_ref[...], m_ref[...], v_ref[...])

    blk = pl.BlockSpec((None, tr, c), lambda l, i: (l, i, 0))
    shape = jax.ShapeDtypeStruct((nl, r, c), F32)
    return pl.pallas_call(body, name=name, grid=(nl, r // tr), in_specs=[blk] * 4, out_specs=(blk, blk, blk),
                          out_shape=(shape, shape, shape), compiler_params=_params(2))(w, g, m, v)


ADA_SHARD = N_MOD * D_MODEL // 4
ADA_COLS = 512
ADA_ROWS = 256
HIGHEST = lax.Precision.HIGHEST


def _ada_mod(c_all, w_ada, b_sh, name):
    def body(c_ref, w_ref, b_ref, o_ref):
        cv = c_ref[...]
        o_ref[...] = jnp.dot(cv * _sigmoid(cv), w_ref[...], precision=HIGHEST, preferred_element_type=F32) + b_ref[...]

    return pl.pallas_call(
        body, name=name, grid=(DEPTH, ADA_SHARD // ADA_COLS),
        in_specs=[pl.BlockSpec((8, D_MODEL), lambda l, j: (0, 0)),
                  pl.BlockSpec((None, D_MODEL, ADA_COLS), lambda l, j: (l, 0, j)),
                  pl.BlockSpec((None, 1, ADA_COLS), lambda l, j: (l, 0, j))],
        out_specs=pl.BlockSpec((None, 8, ADA_COLS), lambda l, j: (l, 0, j)),
        out_shape=jax.ShapeDtypeStruct((DEPTH, 8, ADA_SHARD), F32), compiler_params=_params(2))(c_all, w_ada, b_sh)


def _ada_update(c_all, dmod_sh, w, m, v, name):
    def body(c_ref, d_ref, w_ref, m_ref, v_ref, g_ref, dl_ref, m2_ref, v2_ref):
        cv = c_ref[...]
        g = lax.dot_general(cv * _sigmoid(cv), d_ref[...], (((0,), (0,)), ((), ())), precision=HIGHEST,
                            preferred_element_type=F32)
        g_ref[...] = g
        dl_ref[...], m2_ref[...], v2_ref[...] = _adamw_math(w_ref[...], g, m_ref[...], v_ref[...])

    blk = pl.BlockSpec((None, ADA_ROWS, ADA_SHARD), lambda l, i: (l, i, 0))
    shape = jax.ShapeDtypeStruct((DEPTH, D_MODEL, ADA_SHARD), F32)
    return pl.pallas_call(
        body, name=name, grid=(DEPTH, D_MODEL // ADA_ROWS),
        in_specs=[pl.BlockSpec((8, ADA_ROWS), lambda l, i: (0, i)),
                  pl.BlockSpec((None, 8, ADA_SHARD), lambda l, i: (l, 0, 0)), blk, blk, blk],
        out_specs=(blk,) * 4, out_shape=(shape,) * 4, compiler_params=_params(2))(c_all, dmod_sh, w, m, v)


def _sum_devices(packs, name):
    def body(p_ref, o_ref):
        acc = p_ref[0]
        for dev in range(1, 8):
            acc = acc + p_ref[dev]
        o_ref[...] = acc

    return pl.pallas_call(body, name=name, out_shape=jax.ShapeDtypeStruct(packs.shape[1:], F32))(packs)


def _mesh_pos():
    return lax.axis_index("x"), lax.axis_index("y"), lax.axis_index("c")


def _flip(v, bit):
    return 1 - v if bit else v


def _allgather_devices(x, name):
    m_per, n = x.shape

    def body(x_ref, out_ref, send_sems, recv_sems, local_sem):
        ix, iy, ic = _mesh_pos()
        me, sibling = (ix, iy, ic), (ix, iy, 1 - ic)
        chips = [(1 - ix, iy), (ix, 1 - iy), (1 - ix, 1 - iy)]

        def rows(px, py, pc):
            return out_ref.at[pl.ds((4 * px + 2 * py + pc) * m_per, m_per), :]

        def copy(k, block, to, src=None):
            return pltpu.make_async_remote_copy(
                src_ref=rows(*block) if src is None else src, dst_ref=rows(*block),
                send_sem=send_sems.at[k], recv_sem=recv_sems.at[k], device_id=to, device_id_type=MESH)

        mine = pltpu.make_async_copy(x_ref, rows(*me), local_sem)
        mine.start()
        first = [copy(0, me, sibling, src=x_ref)]
        first += [copy(1 + j, me, (*chip, ic), src=x_ref) for j, chip in enumerate(chips)]
        for cp in first:
            cp.start()
        passed = [copy(4 + j, (*chip, ic), sibling) for j, chip in enumerate(chips)]
        for j, chip in enumerate(chips):
            copy(1 + j, (*chip, ic), me).wait_recv()
            passed[j].start()
        copy(0, sibling, me).wait_recv()
        for j, chip in enumerate(chips):
            copy(4 + j, (*chip, 1 - ic), me).wait_recv()
        for cp in first + passed:
            cp.wait_send()
        mine.wait()

    return pl.pallas_call(
        body, name=name, out_shape=jax.ShapeDtypeStruct((8 * m_per, n), x.dtype),
        in_specs=[pl.BlockSpec(memory_space=pltpu.VMEM)], out_specs=pl.BlockSpec(memory_space=pltpu.VMEM),
        scratch_shapes=[pltpu.SemaphoreType.DMA((7,)), pltpu.SemaphoreType.DMA((7,)), pltpu.SemaphoreType.DMA],
    )(x)


def _gather_chips(shards, name):
    n = len(shards)

    def body(*refs):
        ins, outs = refs[:n], refs[n:2 * n]
        send_sems, recv_sems, local_sems = refs[2 * n:]
        ix, iy, ic = _mesh_pos()
        me = 2 * ix + iy
        local = [pltpu.make_async_copy(ins[a], outs[a].at[me], local_sems.at[a]) for a in range(n)]
        for cp in local:
            cp.start()
        remote = []
        for a in range(n):
            for k in (1, 2, 3):
                px, py = _flip(ix, k & 2), _flip(iy, k & 1)
                sems = dict(send_sem=send_sems.at[3 * a + k - 1], recv_sem=recv_sems.at[3 * a + k - 1],
                            device_id=(px, py, ic), device_id_type=MESH)
                out_cp = pltpu.make_async_remote_copy(src_ref=ins[a], dst_ref=outs[a].at[me], **sems)
                in_cp = pltpu.make_async_remote_copy(src_ref=ins[a], dst_ref=outs[a].at[2 * px + py], **sems)
                out_cp.start()
                remote.append((out_cp, in_cp))
        for out_cp, in_cp in remote:
            out_cp.wait_send()
            in_cp.wait_recv()
        for cp in local:
            cp.wait()

    return pl.pallas_call(
        body, name=name, in_specs=[ANY] * n, out_specs=tuple([ANY] * n),
        out_shape=tuple(jax.ShapeDtypeStruct((4,) + t.shape, t.dtype) for t in shards),
        scratch_shapes=[pltpu.SemaphoreType.DMA((3 * n,)), pltpu.SemaphoreType.DMA((3 * n,)),
                        pltpu.SemaphoreType.DMA((n,))],
    )(*shards)


HBM = pl.BlockSpec(memory_space=pltpu.HBM)
SEM = pl.BlockSpec(memory_space=pltpu.SEMAPHORE)
DATAFLOW = pltpu.SideEffectType.DATAFLOW_SIDE_EFFECTING


def _peer_chip(ix, iy, k):
    return _flip(ix, k & 2), _flip(iy, k & 1)


def _gather_chips_start(shards, name):
    n = len(shards)

    def body(*refs):
        src, land = refs[:n], refs[n:2 * n]
        send_sems, recv_sems = refs[2 * n], refs[2 * n + 1]
        token = refs[-1]
        ix, iy, ic = _mesh_pos()
        me = 2 * ix + iy
        for a in range(n):
            for k in (1, 2, 3):
                px, py = _peer_chip(ix, iy, k)
                pltpu.make_async_remote_copy(
                    src_ref=src[a], dst_ref=land[a].at[me], send_sem=send_sems.at[3 * a + k - 1],
                    recv_sem=recv_sems.at[3 * a + k - 1], device_id=(px, py, ic), device_id_type=MESH).start()
        token[...] = jnp.zeros_like(token)

    hbm = lambda shape, dtype: pltpu.HBM(shape, dtype)
    operands = ([pltpu.with_memory_space_constraint(t, pltpu.HBM) for t in shards]
                + [pltpu.with_memory_space_constraint(lax.empty((4,) + t.shape, t.dtype), pltpu.HBM) for t in shards])
    return pl.pallas_call(
        body, name=name,
        out_shape=(pltpu.SemaphoreType.DMA((3 * n,)), pltpu.SemaphoreType.DMA((3 * n,)),
                   *[hbm(t.shape, t.dtype) for t in shards], *[hbm((4,) + t.shape, t.dtype) for t in shards],
                   jax.ShapeDtypeStruct((8, LANES), F32)),
        in_specs=(HBM,) * (2 * n),
        out_specs=(SEM, SEM) + (HBM,) * (2 * n) + (pl.BlockSpec(memory_space=pltpu.VMEM),),
        input_output_aliases={a: 2 + a for a in range(2 * n)},
        compiler_params=pltpu.CompilerParams(has_side_effects=DATAFLOW),
    )(*operands)


def _gather_chips_wait(started, after, name):
    send_sems, recv_sems = started[0], started[1]
    thru = started[2:-1]
    n = len(thru) // 2

    def body(*refs):
        src, land = refs[:n], refs[n:2 * n]
        send_sems, recv_sems = refs[2 * n], refs[2 * n + 1]
        ix, iy, ic = _mesh_pos()
        for a in range(n):
            for k in (1, 2, 3):
                px, py = _peer_chip(ix, iy, k)
                cp = pltpu.make_async_remote_copy(
                    src_ref=src[a], dst_ref=land[a].at[2 * px + py], send_sem=send_sems.at[3 * a + k - 1],
                    recv_sem=recv_sems.at[3 * a + k - 1], device_id=(px, py, ic), device_id_type=MESH)
                cp.wait_send()
                cp.wait_recv()

    outs = pl.pallas_call(
        body, name=name,
        out_shape=tuple(pltpu.HBM(t.shape, t.dtype) for t in thru),
        in_specs=(HBM,) * (2 * n) + (SEM, SEM, ANY), out_specs=(HBM,) * (2 * n),
        input_output_aliases={a: a for a in range(2 * n)},
        compiler_params=pltpu.CompilerParams(has_side_effects=DATAFLOW),
    )(*thru, send_sems, recv_sems, after)
    return outs[:n], outs[n:]


BIG_KINDS = (("w_in", "col", D_MODEL, IN_COLS), ("w_out", "row", D_MODEL, D_MODEL),
             ("w_up", "col", D_MODEL, 2 * D_FF), ("w_down", "row", D_FF, D_MODEL))


def _piece_shape(how, r, c):
    return (r // 2, c // 4) if how == "col" else (r // 8, c)


def _aligned(start, multiple):
    return start if isinstance(start, int) else pl.multiple_of(start, multiple)


def _piece(ref, how, r, c, chip, half):
    if how == "col":
        return ref.at[pl.ds(_aligned(half * (r // 2), 8), r // 2), pl.ds(_aligned(chip * (c // 4), LANES), c // 4)]
    n = r // 4
    return ref.at[pl.ds(_aligned(chip * n + half * (n // 2), 8), n // 2), :]


def _rs_pair_exchange(grads, name):
    nk = len(BIG_KINDS)
    flat = [grads[ki][l] for ki in range(nk) for l in range(DEPTH)]
    per = DEPTH * 4

    def body(*refs):
        g, land = refs[:nk * DEPTH], refs[nk * DEPTH:nk * DEPTH + nk]
        send_sems, recv_sems = refs[nk * DEPTH + nk:]
        ix, iy, ic = _mesh_pos()
        sibling = (ix, iy, 1 - ic)
        copies = []
        for ki, (_, how, r, c) in enumerate(BIG_KINDS):
            for l in range(DEPTH):
                for j in range(4):
                    sem = ki * per + l * 4 + j
                    rem = pltpu.make_async_remote_copy(
                        src_ref=_piece(g[ki * DEPTH + l], how, r, c, j, 1 - ic), dst_ref=land[ki].at[l, j],
                        send_sem=send_sems.at[sem], recv_sem=recv_sems.at[sem], device_id=sibling, device_id_type=MESH)
                    rem.start()
                    copies.append(rem)
        for rem in copies:
            rem.wait_send()
            rem.wait_recv()

    shapes = [jax.ShapeDtypeStruct((DEPTH, 4) + _piece_shape(how, r, c), F32) for _, how, r, c in BIG_KINDS]
    return pl.pallas_call(
        body, name=name, in_specs=[ANY] * len(flat), out_specs=tuple([ANY] * nk), out_shape=tuple(shapes),
        scratch_shapes=[pltpu.SemaphoreType.DMA((nk * per,))] * 2,
    )(*flat)


def _pair_sum(g, theirs, layer, how, core, name):
    r, c = g.shape
    pr, pc = _piece_shape(how, r, c)
    if how == "col":
        mine_spec = pl.BlockSpec((pr, pc), lambda j, core_ref: (core_ref[0], j))
    else:
        mine_spec = pl.BlockSpec((pr, pc), lambda j, core_ref: (2 * j + core_ref[0], 0))

    def body(core_ref, g_ref, t_ref, o_ref, ob_ref):
        total = g_ref[...] + t_ref[...]
        o_ref[...] = total
        ob_ref[...] = total.astype(BF16)

    out_blk = pl.BlockSpec((None, pr, pc), lambda j, core_ref: (j, 0, 0))
    return pl.pallas_call(
        body, name=name,
        grid_spec=pltpu.PrefetchScalarGridSpec(
            num_scalar_prefetch=1, grid=(4,),
            in_specs=[mine_spec, pl.BlockSpec((None, None, pr, pc), lambda j, core_ref: (layer, j, 0, 0))],
            out_specs=(out_blk, out_blk)),
        out_shape=(jax.ShapeDtypeStruct((4, pr, pc), F32), jax.ShapeDtypeStruct((4, pr, pc), BF16)),
        compiler_params=_params(1))(core, g, theirs)


def _rs_chip_exchange(pair_sums, name):
    nk = len(pair_sums)
    flat = [pair_sums[ki][l] for ki in range(nk) for l in range(DEPTH)]

    def body(*refs):
        src, dst = refs[:nk * DEPTH], refs[nk * DEPTH:nk * DEPTH + nk]
        send_sems, recv_sems = refs[nk * DEPTH + nk:]
        ix, iy, ic = _mesh_pos()
        copies = []
        for ki in range(nk):
            for l in range(DEPTH):
                for k in (1, 2, 3):
                    px, py = _flip(ix, k & 2), _flip(iy, k & 1)
                    sem = (ki * DEPTH + l) * 3 + k - 1
                    rem = pltpu.make_async_remote_copy(
                        src_ref=src[ki * DEPTH + l].at[2 * px + py], dst_ref=dst[ki].at[l, k - 1],
                        send_sem=send_sems.at[sem], recv_sem=recv_sems.at[sem], device_id=(px, py, ic), device_id_type=MESH)
                    rem.start()
                    copies.append(rem)
        for rem in copies:
            rem.wait_send()
            rem.wait_recv()

    return pl.pallas_call(
        body, name=name, in_specs=[ANY] * len(flat), out_specs=tuple([ANY] * nk),
        out_shape=tuple(jax.ShapeDtypeStruct((DEPTH, 3) + pair_sums[ki][0].shape[1:], pair_sums[ki][0].dtype)
                        for ki in range(nk)),
        scratch_shapes=[pltpu.SemaphoreType.DMA((nk * DEPTH * 3,))] * 2,
    )(*flat)


def _chip_sum(own, others, layer, chip, name):
    _, pr, pc = own.shape

    def body(chip_ref, own_ref, s1, s2, s3, o_ref):
        o_ref[...] = ((own_ref[...] + s1[...].astype(F32)) + s2[...].astype(F32)) + s3[...].astype(F32)

    slot = lambda k: pl.BlockSpec((None, None, pr, pc), lambda i, chip_ref: (layer, k, 0, 0))
    return pl.pallas_call(
        body, name=name,
        grid_spec=pltpu.PrefetchScalarGridSpec(
            num_scalar_prefetch=1, grid=(1,),
            in_specs=[pl.BlockSpec((None, pr, pc), lambda i, chip_ref: (chip_ref[0], 0, 0)), slot(0), slot(1), slot(2)],
            out_specs=pl.BlockSpec((pr, pc), lambda i, chip_ref: (0, 0))),
        out_shape=jax.ShapeDtypeStruct((pr, pc), F32), compiler_params=_params(1))(chip, own, others, others, others)


def _rs_pair_share(halves, name):
    nk = len(halves)
    flat = [halves[ki][l] for ki in range(nk) for l in range(DEPTH)]

    def body(*refs):
        src, dst = refs[:nk * DEPTH], refs[nk * DEPTH:nk * DEPTH + nk]
        send_sems, recv_sems = refs[nk * DEPTH + nk:]
        ix, iy, ic = _mesh_pos()
        copies = []
        for ki in range(nk):
            for l in range(DEPTH):
                sem = ki * DEPTH + l
                rem = pltpu.make_async_remote_copy(
                    src_ref=src[sem], dst_ref=dst[ki].at[l], send_sem=send_sems.at[sem], recv_sem=recv_sems.at[sem],
                    device_id=(ix, iy, 1 - ic), device_id_type=MESH)
                rem.start()
                copies.append(rem)
        for rem in copies:
            rem.wait_send()
            rem.wait_recv()

    return pl.pallas_call(
        body, name=name, in_specs=[ANY] * len(flat), out_specs=tuple([ANY] * nk),
        out_shape=tuple(jax.ShapeDtypeStruct((DEPTH,) + halves[ki][0].shape, F32) for ki in range(nk)),
        scratch_shapes=[pltpu.SemaphoreType.DMA((nk * DEPTH,))] * 2,
    )(*flat)


def _adamw_halves(w, mine, theirs, m, v, core, name):
    nl, pr, pc = theirs.shape
    shape = w.shape
    view = lambda t: t.reshape(nl, 2, pr, pc)
    tr = _row_tile(pr, pc, 256 * 1024)

    def body(core_ref, w_ref, a0_ref, a1_ref, t_ref, m_ref, v_ref, g_ref, d_ref, m2_ref, v2_ref):
        own = jnp.where(pl.program_id(0) == 0, a0_ref[...], a1_ref[...])
        g = jnp.where(pl.program_id(1) == core_ref[0], own, t_ref[...])
        g_ref[...] = g
        d_ref[...], m2_ref[...], v2_ref[...] = _adamw_math(w_ref[...], g, m_ref[...], v_ref[...])

    blk = pl.BlockSpec((None, None, tr, pc), lambda l, h, i, core_ref: (l, h, i, 0))
    own_blk = pl.BlockSpec((tr, pc), lambda l, h, i, core_ref: (i, 0))
    out = jax.ShapeDtypeStruct((nl, 2, pr, pc), F32)
    outs = pl.pallas_call(
        body, name=name,
        grid_spec=pltpu.PrefetchScalarGridSpec(
            num_scalar_prefetch=1, grid=(nl, 2, pr // tr),
            in_specs=[blk, own_blk, own_blk, pl.BlockSpec((None, tr, pc), lambda l, h, i, core_ref: (l, i, 0)), blk, blk],
            out_specs=(blk,) * 4),
        out_shape=(out,) * 4, compiler_params=_params(3),
    )(core, view(w), mine[0], mine[1], theirs, view(m), view(v))
    return tuple(t.reshape(shape) for t in outs)


def _reduce_scatter_big(grads, core, chip):
    theirs = _rs_pair_exchange(grads, "rs_pair_exchange")
    pair_sums = [[_pair_sum(grads[ki][l], theirs[ki], l, how, core, f"rs_pair_sum_{kind}") for l in range(DEPTH)]
                 for ki, (kind, how, _, _) in enumerate(BIG_KINDS)]
    slots = _rs_chip_exchange([[both[1] for both in row] for row in pair_sums], "rs_chip_exchange")
    halves = [[_chip_sum(pair_sums[ki][l][0], slots[ki], l, chip, f"rs_chip_sum_{kind}") for l in range(DEPTH)]
              for ki, (kind, _, _, _) in enumerate(BIG_KINDS)]
    other = _rs_pair_share(halves, "rs_pair_share")
    return list(zip(halves, other))


WEIGHT_NAMES = ("w_ada", "b_ada", "norm1_w", "w_in", "conv_a_w", "conv_a_b", "ln_a_w", "ln_a_b", "lb_gamma",
                "rec_norm_w", "w_out", "norm2_w", "w_up", "conv_f_w", "w_down", "final_norm_w")
SMALL_PARAMS = (("b_ada", (DEPTH, N_MOD * D_MODEL), None), ("norm1_w", (DEPTH, D_MODEL), None),
                ("conv_a_w", (DEPTH, CONV_WIDTH, CONV_CH), 2), ("conv_a_b", (DEPTH, CONV_CH), None),
                ("ln_a_w", (DEPTH, CONV_CH), None), ("ln_a_b", (DEPTH, CONV_CH), None),
                ("lb_gamma", (DEPTH, 2, REC_WIDTH), 2), ("rec_norm_w", (DEPTH, REC_WIDTH), None),
                ("norm2_w", (DEPTH, D_MODEL), None), ("conv_f_w", (DEPTH, 3, 2 * D_FF), 2),
                ("final_norm_w", (D_MODEL,), None))


def _pack_rows(parts):
    flat = jnp.concatenate([p.reshape(-1) for p in parts])
    total = flat.shape[0]
    padded = -(-total // (8 * LANES)) * (8 * LANES)
    return jnp.pad(flat, (0, padded - total)).reshape(padded // LANES, LANES)


def _unpack(flat, shapes):
    out, off = [], 0
    for shp in shapes:
        size = int(np.prod(shp))
        out.append(flat[off:off + size].reshape(shp))
        off += size
    return out


def _unstack_chips(t, axis):
    return jnp.concatenate([t[j] for j in range(4)], axis=axis)


def kernel(x, c, w_ada, b_ada, norm1_w, w_in, conv_a_w, conv_a_b, ln_a_w, ln_a_b, lb_gamma, rec_norm_w, w_out, norm2_w, w_up, conv_f_w, w_down, final_norm_w, loss_target, m_w_ada, m_b_ada, m_norm1_w, m_w_in, m_conv_a_w, m_conv_a_b, m_ln_a_w, m_ln_a_b, m_lb_gamma, m_rec_norm_w, m_w_out, m_norm2_w, m_w_up, m_conv_f_w, m_w_down, m_final_norm_w, v_w_ada, v_b_ada, v_norm1_w, v_w_in, v_conv_a_w, v_conv_a_b, v_ln_a_w, v_ln_a_b, v_lb_gamma, v_rec_norm_w, v_w_out, v_norm2_w, v_w_up, v_conv_f_w, v_w_down, v_final_norm_w):
    params = dict(zip(WEIGHT_NAMES, (w_ada, b_ada, norm1_w, w_in, conv_a_w, conv_a_b, ln_a_w, ln_a_b, lb_gamma,
                                     rec_norm_w, w_out, norm2_w, w_up, conv_f_w, w_down, final_norm_w)))
    mom1 = dict(zip(WEIGHT_NAMES, (m_w_ada, m_b_ada, m_norm1_w, m_w_in, m_conv_a_w, m_conv_a_b, m_ln_a_w, m_ln_a_b,
                                   m_lb_gamma, m_rec_norm_w, m_w_out, m_norm2_w, m_w_up, m_conv_f_w, m_w_down,
                                   m_final_norm_w)))
    mom2 = dict(zip(WEIGHT_NAMES, (v_w_ada, v_b_ada, v_norm1_w, v_w_in, v_conv_a_w, v_conv_a_b, v_ln_a_w, v_ln_a_b,
                                   v_lb_gamma, v_rec_norm_w, v_w_out, v_norm2_w, v_w_up, v_conv_f_w, v_w_down,
                                   v_final_norm_w)))
    ix, iy, ic = _mesh_pos()
    chip = 2 * ix + iy
    dev = 2 * chip + ic

    c_all = _allgather_devices(c.reshape(8, LANES), "gather_cond").reshape(8, D_MODEL)
    b_sh = lax.dynamic_slice_in_dim(b_ada, chip * ADA_SHARD, ADA_SHARD, axis=1)
    mod_sh = _ada_mod(c_all, w_ada, b_sh.reshape(DEPTH, 1, ADA_SHARD), "ada_mod")
    w_in_b, w_out_b, w_up_b, w_down_b = (t.astype(BF16) for t in (w_in, w_out, w_up, w_down))
    first = _gather_chips([mod_sh, conv_a_w, conv_f_w, lb_gamma, w_in_b[0]], "gather_first")
    later = [w_in_b[1], w_out_b, w_up_b, w_down_b]
    started = _gather_chips_start(later, "gather_rest_start")
    mod_mine = lax.dynamic_index_in_dim(first[0], dev, axis=2, keepdims=False) + started[-1][0, 0]
    mods = [jnp.concatenate([mod_mine[j, l] for j in range(4)]).reshape(N_MOD, D_MODEL) for l in range(DEPTH)]
    conv_a_w_f, conv_f_w_f, gamma_f = (_unstack_chips(first[k], 2) for k in (1, 2, 3))
    w_in0 = _unstack_chips(first[4], 1)

    def later_weights(after):
        own, lands = _gather_chips_wait(started, after, "gather_rest_wait")
        full = [lax.dynamic_update_index_in_dim(land, mine, chip, 0) for land, mine in zip(lands, own)]
        return (_unstack_chips(full[0], 1), _unstack_chips(full[1], 1), _unstack_chips(full[2], 2),
                _unstack_chips(full[3], 1))

    lb1, p_soft = _lower_bounds(gamma_f.reshape(DEPTH, 2 * REC_WIDTH), "lower_bounds")
    lbs = [jnp.zeros((2, REC_WIDTH), F32), lb1.reshape(2, REC_WIDTH)]
    small = []
    for l in range(DEPTH):
        small.append(dict(norm1_w=norm1_w[l][None], conv_a_w=conv_a_w_f[l], conv_a_b=conv_a_b[l][None],
                          ln_a_w=ln_a_w[l][None], ln_a_b=ln_a_b[l][None], rec_norm_w=rec_norm_w[l],
                          norm2_w=norm2_w[l][None], conv_f_w=conv_f_w_f[l]))

    loss, dx, grads, dfw = _sequence_step(x[0], loss_target[0], mods, lbs, small, w_in0, later_weights,
                                          final_norm_w[None])
    loss = lax.psum(loss, ("x", "y", "c"))

    dgamma = _lower_bounds_bwd(grads[1]["lb"].reshape(1, 2 * REC_WIDTH), p_soft, "lower_bounds_bwd")
    dmod = [jnp.concatenate(grads[l]["mod"], axis=1) for l in range(DEPTH)]
    stack = lambda key: jnp.stack([grads[l][key] for l in range(DEPTH)])
    local_small = dict(b_ada=jnp.concatenate(dmod, axis=0), norm1_w=stack("norm1_w"), conv_a_w=stack("conv_a_w"),
                       conv_a_b=stack("conv_a_b"), ln_a_w=stack("ln_a_w"), ln_a_b=stack("ln_a_b"), lb_gamma=dgamma,
                       rec_norm_w=stack("rec_norm_w"), norm2_w=stack("norm2_w"), conv_f_w=stack("conv_f_w"),
                       final_norm_w=dfw)
    pack = _pack_rows([local_small[name] for name, _, _ in SMALL_PARAMS])
    rows = pack.shape[0]
    packs = _allgather_devices(pack, "gather_small_grads").reshape(8, rows, LANES)
    summed = _sum_devices(packs, "sum_small_grads").reshape(-1)
    small_grads = dict(zip([n for n, _, _ in SMALL_PARAMS], _unpack(summed, [shp for _, shp, _ in SMALL_PARAMS])))

    dmod_all = packs.reshape(8, rows * LANES)[:, :DEPTH * N_MOD * D_MODEL].reshape(8, DEPTH, N_MOD * D_MODEL)
    dmod_sh = lax.dynamic_slice_in_dim(dmod_all, chip * ADA_SHARD, ADA_SHARD, axis=2).transpose(1, 0, 2)
    g_ada, d_ada, m_ada, v_ada = _ada_update(c_all, dmod_sh, w_ada, m_w_ada, v_w_ada, "ada_update")

    for name, shp, axis in SMALL_PARAMS:
        if axis is not None:
            width = shp[axis] // 4
            small_grads[name] = lax.dynamic_slice_in_dim(small_grads[name], chip * width, width, axis=axis)
    names = [n for n, _, _ in SMALL_PARAMS]
    packed = [_pack_rows([src[n] for n in names])[None] for src in (params, small_grads, mom1, mom2)]
    small_out = _adamw(*packed, "adamw_small")
    shapes = [params[n].shape for n in names]
    small_delta, small_m, small_v = (dict(zip(names, _unpack(t.reshape(-1), shapes))) for t in small_out)

    core_id, chip_id = ic.astype(jnp.int32).reshape(1), chip.astype(jnp.int32).reshape(1)
    summed_big = _reduce_scatter_big([[grads[l][name] for l in range(DEPTH)] for name, _, _, _ in BIG_KINDS],
                                     core_id, chip_id)
    grad, delta, new_m, new_v = dict(small_grads), small_delta, small_m, small_v
    grad["w_ada"], delta["w_ada"], new_m["w_ada"], new_v["w_ada"] = g_ada, d_ada, m_ada, v_ada
    for (name, _, _, _), (mine, theirs) in zip(BIG_KINDS, summed_big):
        grad[name], delta[name], new_m[name], new_v[name] = _adamw_halves(
            params[name], mine, theirs, mom1[name], mom2[name], core_id, f"adamw_{name}")

    return (loss, dx[None], *[grad[n] for n in WEIGHT_NAMES], *[delta[n] for n in WEIGHT_NAMES],
            *[new_m[n] for n in WEIGHT_NAMES], *[new_v[n] for n in WEIGHT_NAMES])
```

```python
import numpy as np
import jax
import jax.numpy as jnp
from jax import lax
from jax.experimental import pallas as pl
from jax.experimental.pallas import tpu as pltpu

F32 = jnp.float32
BF16 = jnp.bfloat16

D_MODEL = 1024
DEPTH = 2
HEAD_DIM = 64
CONV_CH = 256
CONV_WIDTH = 31
ATT_WIDTH = 384
N_HEADS = 6
DILATIONS = (1, 4, 16)
ATT_HALF = 64
ATT_BLOCK = 128
ALIBI_MAX_EXP = 8.0
MASK_VALUE = -1e30
REC_WIDTH = 384
REC_CHUNK = 64
F_TINY = 1e-30
D_FF = 2816
N_MOD = 6
EPS = 1e-6
G_CONV = (0, 512)
G_QKV = (512, 1664)
G_REC = (1664, 3584)
IN_COLS = 3584

ADAM_LR = 0.001
ADAM_B1 = 0.9
ADAM_B2 = 0.999
ADAM_EPS = 1e-08
ADAM_WD = 0.01
ADAM_STEP = 10

VMEM_LIMIT_BYTES = 56 * 1024 * 1024
LANES = 128
MESH = pl.DeviceIdType.MESH
ANY = pl.BlockSpec(memory_space=pl.ANY)


def _params(n_axes):
    return pltpu.CompilerParams(dimension_semantics=("arbitrary",) * n_axes,
                                vmem_limit_bytes=VMEM_LIMIT_BYTES)


def _tile(n, target):
    best = None
    for t in range(LANES, min(n, target) + 1, LANES):
        if n % t == 0:
            best = t
    return best or n


def _sigmoid(x):
    return jax.nn.sigmoid(x)


def _silu_grad(x):
    s = _sigmoid(x)
    return s * (1.0 + x * (1.0 - s))


MM_ACC_ELEMS = 1536 * 1024


def _matmul(a, b, mode, out_dtype, name, tm=1024, tn=1792, tk=1792):
    if mode == "nn":
        (m, k), (k2, n) = a.shape, b.shape
    elif mode == "nt":
        (m, k), (n, k2) = a.shape, b.shape
    else:
        (k, m), (k2, n) = a.shape, b.shape
    assert k == k2, (a.shape, b.shape, mode)
    tn, tk = _tile(n, tn), _tile(k, tk)
    tm = _tile(m, min(tm, MM_ACC_ELEMS // tn))
    nk = k // tk
    a_spec = (pl.BlockSpec((tk, tm), lambda i, j, kk: (kk, i)) if mode == "tn"
              else pl.BlockSpec((tm, tk), lambda i, j, kk: (i, kk)))
    b_spec = (pl.BlockSpec((tn, tk), lambda i, j, kk: (j, kk)) if mode == "nt"
              else pl.BlockSpec((tk, tn), lambda i, j, kk: (kk, j)))
    dims = {"nn": (((1,), (0,)), ((), ())), "nt": (((1,), (1,)), ((), ())),
            "tn": (((0,), (0,)), ((), ()))}[mode]

    def body(a_ref, b_ref, o_ref, *scratch):
        part = lax.dot_general(a_ref[...].astype(BF16), b_ref[...].astype(BF16), dims, preferred_element_type=F32)
        if nk == 1:
            o_ref[...] = part.astype(out_dtype)
            return
        acc_ref, = scratch
        kk = pl.program_id(2)

        @pl.when(kk == 0)
        def _():
            acc_ref[...] = part

        @pl.when(kk > 0)
        def _():
            acc_ref[...] += part

        @pl.when(kk == nk - 1)
        def _():
            o_ref[...] = acc_ref[...].astype(out_dtype)

    return pl.pallas_call(
        body, name=name, grid=(m // tm, n // tn, nk),
        in_specs=[a_spec, b_spec],
        out_specs=pl.BlockSpec((tm, tn), lambda i, j, kk: (i, j)),
        out_shape=jax.ShapeDtypeStruct((m, n), out_dtype),
        scratch_shapes=[pltpu.VMEM((tm, tn), F32)] if nk > 1 else [],
        compiler_params=pltpu.CompilerParams(dimension_semantics=("parallel", "parallel", "arbitrary"),
                                             vmem_limit_bytes=VMEM_LIMIT_BYTES),
    )(a, b)


NORM_ROWS = 256


def _row_spec(width, rows=NORM_ROWS):
    return pl.BlockSpec((rows, width), lambda i: (i, 0))


def _vec_spec(width):
    return pl.BlockSpec((1, width), lambda i: (0, 0))


def _resid_norm_mod(x, r, g, nw, sc, sh, name):
    s, d = x.shape
    has_r = r is not None

    def body(*refs):
        if has_r:
            x_ref, r_ref, g_ref, nw_ref, sc_ref, sh_ref, xn_ref, h_ref = refs
            xn = x_ref[...] + g_ref[...] * r_ref[...].astype(F32)
            xn_ref[...] = xn
        else:
            x_ref, nw_ref, sc_ref, sh_ref, h_ref = refs
            xn = x_ref[...]
        rstd = lax.rsqrt(jnp.mean(xn * xn, axis=-1, keepdims=True) + EPS)
        y = xn * rstd * nw_ref[...]
        h_ref[...] = (y * (1.0 + sc_ref[...]) + sh_ref[...]).astype(BF16)

    if has_r:
        ins, in_specs = (x, r, g, nw, sc, sh), [_row_spec(d), _row_spec(d)] + [_vec_spec(d)] * 4
        out_shape = (jax.ShapeDtypeStruct((s, d), F32), jax.ShapeDtypeStruct((s, d), BF16))
        out_specs = (_row_spec(d), _row_spec(d))
    else:
        ins, in_specs = (x, nw, sc, sh), [_row_spec(d)] + [_vec_spec(d)] * 3
        out_shape = jax.ShapeDtypeStruct((s, d), BF16)
        out_specs = _row_spec(d)
    return pl.pallas_call(body, name=name, grid=(s // NORM_ROWS,), in_specs=in_specs, out_specs=out_specs,
                          out_shape=out_shape, compiler_params=_params(1))(*ins)


def _final_loss(x, r, g, fw, tgt, name):
    s, d = x.shape

    def body(x_ref, r_ref, g_ref, fw_ref, t_ref, loss_ref, dx_ref, dr_ref, dg_ref, dfw_ref):
        @pl.when(pl.program_id(0) == 0)
        def _():
            loss_ref[...] = jnp.zeros_like(loss_ref)
            dg_ref[...] = jnp.zeros_like(dg_ref)
            dfw_ref[...] = jnp.zeros_like(dfw_ref)

        rr = r_ref[...].astype(F32)
        gg = g_ref[...]
        xn = x_ref[...] + gg * rr
        rstd = lax.rsqrt(jnp.mean(xn * xn, axis=-1, keepdims=True) + EPS)
        xh = xn * rstd
        fwv = fw_ref[...]
        e = xh * fwv - t_ref[...]
        loss_ref[...] += 0.5 * jnp.sum(jnp.mean(e * e, axis=-1, keepdims=True), axis=0, keepdims=True)
        dy = e * (1.0 / d)
        dfw_ref[...] += jnp.sum(dy * xh, axis=0, keepdims=True)
        dxh = dy * fwv
        dx = rstd * (dxh - xh * jnp.mean(dxh * xh, axis=-1, keepdims=True))
        dx_ref[...] = dx
        dr_ref[...] = (gg * dx).astype(BF16)
        dg_ref[...] += jnp.sum(dx * rr, axis=0, keepdims=True)

    return pl.pallas_call(
        body, name=name, grid=(s // NORM_ROWS,),
        in_specs=[_row_spec(d), _row_spec(d), _vec_spec(d), _vec_spec(d), _row_spec(d)],
        out_specs=(_vec_spec(LANES), _row_spec(d), _row_spec(d), _vec_spec(d), _vec_spec(d)),
        out_shape=(jax.ShapeDtypeStruct((1, LANES), F32), jax.ShapeDtypeStruct((s, d), F32),
                   jax.ShapeDtypeStruct((s, d), BF16), jax.ShapeDtypeStruct((1, d), F32),
                   jax.ShapeDtypeStruct((1, d), F32)),
        compiler_params=_params(1))(x, r, g, fw, tgt)


def _norm_bwd(x, dhs, dxres, nw, sc, g, r, name):
    s, d = x.shape
    n_dh = len(dhs)
    has_g = g is not None

    def body(*refs):
        x_ref = refs[0]
        dh_refs = refs[1:1 + n_dh]
        dxres_ref, nw_ref, sc_ref = refs[1 + n_dh:4 + n_dh]
        pos = 4 + n_dh
        if has_g:
            g_ref, r_ref = refs[pos:pos + 2]
            pos += 2
            dx_ref, dr_ref, dsh_ref, dsc_ref, dnw_ref, dg_ref = refs[pos:]
            accs = (dsh_ref, dsc_ref, dnw_ref, dg_ref)
        else:
            dx_ref, dsh_ref, dsc_ref, dnw_ref = refs[pos:]
            accs = (dsh_ref, dsc_ref, dnw_ref)

        @pl.when(pl.program_id(0) == 0)
        def _():
            for acc in accs:
                acc[...] = jnp.zeros_like(acc)

        xv = x_ref[...]
        dh = dh_refs[0][...].astype(F32)
        for extra in dh_refs[1:]:
            dh = dh + extra[...].astype(F32)
        rstd = lax.rsqrt(jnp.mean(xv * xv, axis=-1, keepdims=True) + EPS)
        xh = xv * rstd
        nwv = nw_ref[...]
        dsh_ref[...] += jnp.sum(dh, axis=0, keepdims=True)
        dsc_ref[...] += jnp.sum(dh * (xh * nwv), axis=0, keepdims=True)
        dy = dh * (1.0 + sc_ref[...])
        dnw_ref[...] += jnp.sum(dy * xh, axis=0, keepdims=True)
        dxh = dy * nwv
        dx = dxres_ref[...] + rstd * (dxh - xh * jnp.mean(dxh * xh, axis=-1, keepdims=True))
        dx_ref[...] = dx
        if has_g:
            dr_ref[...] = (g_ref[...] * dx).astype(BF16)
            dg_ref[...] += jnp.sum(dx * r_ref[...].astype(F32), axis=0, keepdims=True)

    ins = [x, *dhs, dxres, nw, sc]
    in_specs = [_row_spec(d)] * (2 + n_dh) + [_vec_spec(d)] * 2
    out_shape = [jax.ShapeDtypeStruct((s, d), F32)]
    out_specs = [_row_spec(d)]
    if has_g:
        ins += [g, r]
        in_specs += [_vec_spec(d), _row_spec(d)]
        out_shape.append(jax.ShapeDtypeStruct((s, d), BF16))
        out_specs.append(_row_spec(d))
    n_vec = 4 if has_g else 3
    out_shape += [jax.ShapeDtypeStruct((1, d), F32)] * n_vec
    out_specs += [_vec_spec(d)] * n_vec
    return pl.pallas_call(body, name=name, grid=(s // NORM_ROWS,), in_specs=in_specs, out_specs=tuple(out_specs),
                          out_shape=tuple(out_shape), compiler_params=_params(1))(*ins)


FFN_ROWS = 256
FFN_COLS = 1408
HALO = 16
INV_SQRT2 = 0.7071067811865476
INV_SQRT_2PI = 0.3989422804014327


def _gelu(x):
    return 0.5 * x * (1.0 + lax.erf(x * INV_SQRT2))


def _gelu_grad(x):
    return 0.5 * (1.0 + lax.erf(x * INV_SQRT2)) + x * (INV_SQRT_2PI * jnp.exp(-0.5 * x * x))


def _halo_specs(rows, cols, halo, n_rows_total, col_of):
    per = rows // halo
    last = n_rows_total // halo - 1
    cur = pl.BlockSpec((rows, cols), lambda j, i: (i, col_of(j)))
    prev = pl.BlockSpec((halo, cols), lambda j, i: (jnp.maximum(i * per - 1, 0), col_of(j)))
    nxt = pl.BlockSpec((halo, cols), lambda j, i: (jnp.minimum((i + 1) * per, last), col_of(j)))
    return [prev, cur, nxt]


def _shift_rows(x, k):
    n = x.shape[0]
    return pltpu.roll(x, k % n, axis=0)


def _conv3(ext, w):
    return w[0:1, :] * _shift_rows(ext, 1) + w[1:2, :] * ext + w[2:3, :] * _shift_rows(ext, -1)


def _ext_block(prev_ref, cur_ref, next_ref, i, n_i):
    prev = jnp.where(i > 0, prev_ref[...].astype(F32), 0.0)
    nxt = jnp.where(i < n_i - 1, next_ref[...].astype(F32), 0.0)
    return jnp.concatenate([prev, cur_ref[...].astype(F32), nxt], axis=0)


def _ffn_act(u, cw, name):
    s = u.shape[0]
    nc, ns = D_FF // FFN_COLS, s // FFN_ROWS

    def body(gp, gc, gn, vp, vc, vn, wg_ref, wv_ref, o_ref):
        i = pl.program_id(1)
        cg = _conv3(_ext_block(gp, gc, gn, i, ns), wg_ref[...])[HALO:HALO + FFN_ROWS]
        cv = _conv3(_ext_block(vp, vc, vn, i, ns), wv_ref[...])[HALO:HALO + FFN_ROWS]
        o_ref[...] = (_gelu(cg) * cv).astype(BF16)

    in_specs = (_halo_specs(FFN_ROWS, FFN_COLS, HALO, s, lambda j: j)
                + _halo_specs(FFN_ROWS, FFN_COLS, HALO, s, lambda j: j + nc)
                + [pl.BlockSpec((3, FFN_COLS), lambda j, i: (0, j)),
                   pl.BlockSpec((3, FFN_COLS), lambda j, i: (0, j + nc))])
    return pl.pallas_call(
        body, name=name, grid=(nc, ns), in_specs=in_specs,
        out_specs=pl.BlockSpec((FFN_ROWS, FFN_COLS), lambda j, i: (i, j)),
        out_shape=jax.ShapeDtypeStruct((s, D_FF), BF16), compiler_params=_params(2),
    )(u, u, u, u, u, u, cw, cw)


def _ffn_act_bwd(u, dact, cw, name):
    s = u.shape[0]
    nc, ns = D_FF // FFN_COLS, s // FFN_ROWS

    def body(gp, gc, gn, vp, vc, vn, dp, dc, dn, wg_ref, wv_ref, dug_ref, duv_ref, dwg_ref, dwv_ref):
        i = pl.program_id(1)

        @pl.when(i == 0)
        def _():
            dwg_ref[...] = jnp.zeros_like(dwg_ref)
            dwv_ref[...] = jnp.zeros_like(dwv_ref)

        ug = _ext_block(gp, gc, gn, i, ns)
        uv = _ext_block(vp, vc, vn, i, ns)
        da = _ext_block(dp, dc, dn, i, ns)
        wg, wv = wg_ref[...], wv_ref[...]
        cg, cv = _conv3(ug, wg), _conv3(uv, wv)
        dcg = da * cv * _gelu_grad(cg)
        dcv = da * _gelu(cg)
        inner = slice(HALO, HALO + FFN_ROWS)
        for d_c, uu, w, du_ref, dw_ref in ((dcg, ug, wg, dug_ref, dwg_ref), (dcv, uv, wv, duv_ref, dwv_ref)):
            d_next, d_prev = _shift_rows(d_c, -1), _shift_rows(d_c, 1)
            du = w[0:1, :] * d_next + w[1:2, :] * d_c + w[2:3, :] * d_prev
            du_ref[...] = du[inner].astype(BF16)
            u_in = uu[inner]
            for tap, d_tap in enumerate((d_next, d_c, d_prev)):
                dw_ref[tap:tap + 1, :] += jnp.sum(d_tap[inner] * u_in, axis=0, keepdims=True)

    in_specs = (_halo_specs(FFN_ROWS, FFN_COLS, HALO, s, lambda j: j)
                + _halo_specs(FFN_ROWS, FFN_COLS, HALO, s, lambda j: j + nc)
                + _halo_specs(FFN_ROWS, FFN_COLS, HALO, s, lambda j: j)
                + [pl.BlockSpec((3, FFN_COLS), lambda j, i: (0, j)),
                   pl.BlockSpec((3, FFN_COLS), lambda j, i: (0, j + nc))])
    blk = pl.BlockSpec((FFN_ROWS, FFN_COLS), lambda j, i: (i, j))
    acc = pl.BlockSpec((HALO, FFN_COLS), lambda j, i: (0, j))
    return pl.pallas_call(
        body, name=name, grid=(nc, ns), in_specs=in_specs, out_specs=(blk, blk, acc, acc),
        out_shape=(jax.ShapeDtypeStruct((s, D_FF), BF16), jax.ShapeDtypeStruct((s, D_FF), BF16),
                   jax.ShapeDtypeStruct((HALO, D_FF), F32), jax.ShapeDtypeStruct((HALO, D_FF), F32)),
        compiler_params=_params(2),
    )(u, u, u, u, u, u, dact, dact, dact, cw, cw)


CONV_ROWS = 512
CONV_HALO = 16
CONV_PAD = CONV_WIDTH // 2


def _conv_halo_specs(cols, s):
    per = CONV_ROWS // CONV_HALO
    last = s // CONV_HALO - 1
    return [pl.BlockSpec((CONV_HALO, cols), lambda i: (jnp.maximum(i * per - 1, 0), 0)),
            pl.BlockSpec((CONV_ROWS, cols), lambda i: (i, 0)),
            pl.BlockSpec((CONV_HALO, cols), lambda i: (jnp.minimum((i + 1) * per, last), 0))]


def _glu_ext(pp, pc, pn, i, n_i):
    ext = _ext_block(pp, pc, pn, i, n_i)
    return ext[:, :CONV_CH] * _sigmoid(ext[:, CONV_CH:])


def _conv_mixer(pa, cw, cb, lnw, lnb, name):
    s = pa.shape[0]
    ns = s // CONV_ROWS

    def body(pp, pc, pn, cw_ref, cb_ref, lnw_ref, lnb_ref, o_ref, c_ref):
        i = pl.program_id(0)
        a = _glu_ext(pp, pc, pn, i, ns)
        acc = jnp.zeros((CONV_ROWS, CONV_CH), F32)
        for tap in range(CONV_WIDTH):
            acc = acc + cw_ref[tap:tap + 1, :] * _shift_rows(a, -(tap + 1))[:CONV_ROWS]
        cv = acc + cb_ref[...]
        c_ref[...] = cv
        mu = jnp.mean(cv, axis=-1, keepdims=True)
        xc = cv - mu
        rstd = lax.rsqrt(jnp.mean(xc * xc, axis=-1, keepdims=True) + EPS)
        y = xc * rstd * lnw_ref[...] + lnb_ref[...]
        o_ref[...] = (y * _sigmoid(y)).astype(BF16)

    vec = pl.BlockSpec((1, CONV_CH), lambda i: (0, 0))
    blk = pl.BlockSpec((CONV_ROWS, CONV_CH), lambda i: (i, 0))
    return pl.pallas_call(
        body, name=name, grid=(ns,),
        in_specs=_conv_halo_specs(2 * CONV_CH, s) + [pl.BlockSpec((CONV_WIDTH, CONV_CH), lambda i: (0, 0)), vec, vec, vec],
        out_specs=(blk, blk),
        out_shape=(jax.ShapeDtypeStruct((s, CONV_CH), BF16), jax.ShapeDtypeStruct((s, CONV_CH), F32)),
        compiler_params=_params(1))(pa, pa, pa, cw, cb, lnw, lnb)


def _conv_mixer_bwd_ln(cv, dout, lnw, lnb, name):
    s = cv.shape[0]

    def body(c_ref, do_ref, lnw_ref, lnb_ref, dc_ref, dlnw_ref, dlnb_ref, dcb_ref):
        @pl.when(pl.program_id(0) == 0)
        def _():
            dlnw_ref[...] = jnp.zeros_like(dlnw_ref)
            dlnb_ref[...] = jnp.zeros_like(dlnb_ref)
            dcb_ref[...] = jnp.zeros_like(dcb_ref)

        c = c_ref[...]
        mu = jnp.mean(c, axis=-1, keepdims=True)
        xc = c - mu
        rstd = lax.rsqrt(jnp.mean(xc * xc, axis=-1, keepdims=True) + EPS)
        xh = xc * rstd
        w = lnw_ref[...]
        y = xh * w + lnb_ref[...]
        dy = do_ref[...] * _silu_grad(y)
        dlnw_ref[...] += jnp.sum(dy * xh, axis=0, keepdims=True)
        dlnb_ref[...] += jnp.sum(dy, axis=0, keepdims=True)
        dxh = dy * w
        dc = rstd * (dxh - jnp.mean(dxh, axis=-1, keepdims=True) - xh * jnp.mean(dxh * xh, axis=-1, keepdims=True))
        dc_ref[...] = dc
        dcb_ref[...] += jnp.sum(dc, axis=0, keepdims=True)

    vec = pl.BlockSpec((1, CONV_CH), lambda i: (0, 0))
    blk = pl.BlockSpec((CONV_ROWS, CONV_CH), lambda i: (i, 0))
    return pl.pallas_call(
        body, name=name, grid=(s // CONV_ROWS,), in_specs=[blk, blk, vec, vec], out_specs=(blk, vec, vec, vec),
        out_shape=(jax.ShapeDtypeStruct((s, CONV_CH), F32),) + (jax.ShapeDtypeStruct((1, CONV_CH), F32),) * 3,
        compiler_params=_params(1))(cv, dout, lnw, lnb)


def _conv_mixer_bwd_conv(pa, dc, cw, name):
    s = pa.shape[0]
    ns = s // CONV_ROWS

    def body(pc, dp, dcc, dn, cw_ref, dpa_ref, dcw_ref):
        i = pl.program_id(0)

        @pl.when(i == 0)
        def _():
            dcw_ref[...] = jnp.zeros_like(dcw_ref)

        cur = pc[...]
        val, sg = cur[:, :CONV_CH], _sigmoid(cur[:, CONV_CH:])
        a_cur = val * sg
        dce = _ext_block(dp, dcc, dn, i, ns)
        da = jnp.zeros((CONV_ROWS, CONV_CH), F32)
        for tap in range(CONV_WIDTH):
            shifted = _shift_rows(dce, -(CONV_WIDTH - tap))[:CONV_ROWS]
            da = da + cw_ref[tap:tap + 1, :] * shifted
            dcw_ref[tap:tap + 1, :] += jnp.sum(shifted * a_cur, axis=0, keepdims=True)
        dpa_ref[:, :CONV_CH] = (da * sg).astype(BF16)
        dpa_ref[:, CONV_CH:] = (da * val * sg * (1.0 - sg)).astype(BF16)

    return pl.pallas_call(
        body, name=name, grid=(ns,),
        in_specs=[pl.BlockSpec((CONV_ROWS, 2 * CONV_CH), lambda i: (i, 0))] + _conv_halo_specs(CONV_CH, s)
        + [pl.BlockSpec((CONV_WIDTH, CONV_CH), lambda i: (0, 0))],
        out_specs=(pl.BlockSpec((CONV_ROWS, 2 * CONV_CH), lambda i: (i, 0)),
                   pl.BlockSpec((32, CONV_CH), lambda i: (0, 0))),
        out_shape=(jax.ShapeDtypeStruct((s, 2 * CONV_CH), BF16), jax.ShapeDtypeStruct((32, CONV_CH), F32)),
        compiler_params=_params(1))(pa, dc, dc, dc, cw)


SLOPES = tuple(float(2.0 ** (-ALIBI_MAX_EXP * (h + 1) / N_HEADS)) for h in range(N_HEADS))
ATT_SCALE = HEAD_DIM ** -0.5


PAIR = 2 * HEAD_DIM
N_PAIRS = N_HEADS // 2
ATT_WIN = ATT_BLOCK + 2 * ATT_HALF


ATT_GROUPS = {1: 4, 4: 1, 16: 1}


def _window_specs(dil, n_steps, col_of):
    per = 2 * ATT_GROUPS[dil]
    rows, halo = ATT_BLOCK * dil * ATT_GROUPS[dil], ATT_HALF * dil
    return [pl.BlockSpec((halo, PAIR), lambda i, p: (jnp.maximum(per * i - 1, 0), col_of(p))),
            pl.BlockSpec((rows, PAIR), lambda i, p: (i, col_of(p))),
            pl.BlockSpec((halo, PAIR), lambda i, p: (jnp.minimum(per * (i + 1), per * n_steps - 1), col_of(p)))]


def _residue(ref, r, n, dil, start=0):
    return ref[pl.ds(start * dil + r, n, stride=dil), :] if dil > 1 else ref[pl.ds(start + r, n), :]


def _store_residue(ref, r, dil, start, val):
    if dil > 1:
        ref[pl.ds(start * dil + r, val.shape[0], stride=dil), :] = val
    else:
        ref[pl.ds(start + r, val.shape[0]), :] = val


def _residue_window(refs, r, dil, g=0):
    prev, cur, nxt = refs
    groups = ATT_GROUPS[dil]
    lo = max(g * ATT_BLOCK - ATT_HALF, 0)
    hi = min((g + 1) * ATT_BLOCK + ATT_HALF, groups * ATT_BLOCK)
    parts = [_residue(prev, r, ATT_HALF, dil)] if g == 0 else []
    parts.append(_residue(cur, r, hi - lo, dil, lo))
    if g == groups - 1:
        parts.append(_residue(nxt, r, ATT_HALF, dil))
    return jnp.concatenate(parts, axis=0)


def _band_masks(i, length, dil, transposed):
    shape = (ATT_WIN, ATT_BLOCK) if transposed else (ATT_BLOCK, ATT_WIN)
    row = lax.broadcasted_iota(jnp.int32, shape, 0)
    col = lax.broadcasted_iota(jnp.int32, shape, 1)
    wide = row if transposed else col
    dist = jnp.abs((row - col - ATT_HALF) if transposed else (row + ATT_HALF - col))
    wpos = i * ATT_BLOCK - ATT_HALF + wide
    valid = (dist <= ATT_HALF) & (wpos >= 0) & (wpos < length)
    return valid, dist.astype(F32) * float(dil)


def _attn_branch(qkv, dil, name):
    s = qkv.shape[0]
    groups = ATT_GROUPS[dil]
    rows = ATT_BLOCK * dil * groups
    n_steps = s // rows
    length = s // dil
    nt = (((1,), (1,)), ((), ()))

    def body(q_ref, kp, kc, kn, vp, vc, vn, o_ref, l_ref):
        i, pair = pl.program_id(0), pl.program_id(1)
        items = [(g, r) for g in range(groups) for r in range(dil)]
        q = jnp.stack([_residue(q_ref, r, ATT_BLOCK, dil, g * ATT_BLOCK) for g, r in items]).astype(BF16)
        k = jnp.stack([_residue_window((kp, kc, kn), r, dil, g) for g, r in items]).astype(BF16)
        v = jnp.stack([_residue_window((vp, vc, vn), r, dil, g) for g, r in items]).astype(BF16)
        per_group = [_band_masks(i * groups + g, length, dil, False) for g in range(groups)]
        valid = jnp.stack([per_group[g][0] for g, _ in items]) if groups > 1 else per_group[0][0][None]
        distf = jnp.stack([per_group[g][1] for g, _ in items]) if groups > 1 else per_group[0][1][None]
        outs, lses = [], []
        for hh in range(2):
            sl = slice(hh * HEAD_DIM, (hh + 1) * HEAD_DIM)
            slope = jnp.where(pair == 0, SLOPES[hh], jnp.where(pair == 1, SLOPES[2 + hh], SLOPES[4 + hh]))
            sc = jnp.einsum("bqd,bkd->bqk", q[:, :, sl], k[:, :, sl], preferred_element_type=F32) * ATT_SCALE
            sc = jnp.where(valid, sc - slope * distf, MASK_VALUE)
            m = jnp.max(sc, axis=-1, keepdims=True)
            p = jnp.exp(sc - m)
            den = jnp.sum(p, axis=-1, keepdims=True)
            outs.append(jnp.einsum("bqk,bkd->bqd", p.astype(BF16), v[:, :, sl], preferred_element_type=F32) / den)
            lses.append(jnp.broadcast_to(m + jnp.log(den), (len(items), ATT_BLOCK, HEAD_DIM)))
        o_all, l_all = jnp.concatenate(outs, axis=2), jnp.concatenate(lses, axis=2)
        for n, (g, r) in enumerate(items):
            _store_residue(o_ref, r, dil, g * ATT_BLOCK, o_all[n])
            _store_residue(l_ref, r, dil, g * ATT_BLOCK, l_all[n])

    out_blk = pl.BlockSpec((rows, PAIR), lambda i, p: (i, p))
    return pl.pallas_call(
        body, name=name, grid=(n_steps, N_PAIRS),
        in_specs=[pl.BlockSpec((rows, PAIR), lambda i, p: (i, p))]
        + _window_specs(dil, n_steps, lambda p: N_PAIRS + p) + _window_specs(dil, n_steps, lambda p: 2 * N_PAIRS + p),
        out_specs=(out_blk, out_blk),
        out_shape=(jax.ShapeDtypeStruct((s, ATT_WIDTH), F32),) * 2,
        compiler_params=_params(2))(qkv, qkv, qkv, qkv, qkv, qkv, qkv)


ATT_ROWS = 512


def _attn_combine(outs, lses, name):
    s = outs[0].shape[0]

    def body(o1, o2, o3, l1, l2, l3, att_ref, att32_ref, lse_ref):
        ls = [l1[...], l2[...], l3[...]]
        m = jnp.maximum(jnp.maximum(ls[0], ls[1]), ls[2])
        es = [jnp.exp(l - m) for l in ls]
        den = es[0] + es[1] + es[2]
        att = (es[0] * o1[...] + es[1] * o2[...] + es[2] * o3[...]) / den
        att_ref[...] = att.astype(BF16)
        att32_ref[...] = att
        lse_ref[...] = m + jnp.log(den)

    blk = pl.BlockSpec((ATT_ROWS, ATT_WIDTH), lambda i: (i, 0))
    return pl.pallas_call(
        body, name=name, grid=(s // ATT_ROWS,), in_specs=[blk] * 6, out_specs=(blk, blk, blk),
        out_shape=(jax.ShapeDtypeStruct((s, ATT_WIDTH), BF16), jax.ShapeDtypeStruct((s, ATT_WIDTH), F32),
                   jax.ShapeDtypeStruct((s, ATT_WIDTH), F32)),
        compiler_params=_params(1))(*outs, *lses)


def _attn_delta(datt, att, name):
    s = att.shape[0]

    def body(d_ref, a_ref, delta_ref):
        prod = d_ref[...] * a_ref[...]
        for h in range(N_HEADS):
            sl = slice(h * HEAD_DIM, (h + 1) * HEAD_DIM)
            delta_ref[:, sl] = jnp.broadcast_to(jnp.sum(prod[:, sl], axis=-1, keepdims=True), (ATT_ROWS, HEAD_DIM))

    blk = pl.BlockSpec((ATT_ROWS, ATT_WIDTH), lambda i: (i, 0))
    return pl.pallas_call(
        body, name=name, grid=(s // ATT_ROWS,), in_specs=[blk, blk], out_specs=blk,
        out_shape=jax.ShapeDtypeStruct((s, ATT_WIDTH), F32), compiler_params=_params(1))(datt, att)


def _attn_branch_bwd(qkv, do, lse, delta, prev, dil, name):
    s = qkv.shape[0]
    groups = ATT_GROUPS[dil]
    rows = ATT_BLOCK * dil * groups
    n_steps = s // rows
    length = s // dil
    has_prev = prev is not None
    tn = (((0,), (0,)), ((), ()))
    nt = (((1,), (1,)), ((), ()))

    def body(*refs):
        qs, ks, vs, dos, ls, des = (refs[3 * n:3 * n + 3] for n in range(6))
        rest = refs[18:]
        if has_prev:
            pq, pk, pv = rest[:3]
            rest = rest[3:]
        dq_ref, dk_ref, dv_ref = rest
        i, pair = pl.program_id(0), pl.program_id(1)
        items = [(g, r) for g in range(groups) for r in range(dil)]
        cur = lambda t: jnp.stack([_residue(t[1], r, ATT_BLOCK, dil, g * ATT_BLOCK) for g, r in items])
        win = lambda t: jnp.stack([_residue_window(t, r, dil, g) for g, r in items])
        q_cur, k_cur, v_cur, do_cur = (cur(t).astype(BF16) for t in (qs, ks, vs, dos))
        q_win, k_win, v_win, do_win = (win(t).astype(BF16) for t in (qs, ks, vs, dos))
        l_cur, de_cur, l_win, de_win = cur(ls), cur(des), win(ls), win(des)

        def masks(transposed):
            per_group = [_band_masks(i * groups + g, length, dil, transposed) for g in range(groups)]
            if groups == 1:
                return per_group[0][0][None], per_group[0][1][None]
            return jnp.stack([per_group[g][0] for g, _ in items]), jnp.stack([per_group[g][1] for g, _ in items])

        valid_q, distf_q = masks(False)
        valid_k, distf_k = masks(True)
        dot = lambda eq, a, b: jnp.einsum(eq, a, b, preferred_element_type=F32)
        dqs, dks, dvs = [], [], []
        for hh in range(2):
            sl = slice(hh * HEAD_DIM, (hh + 1) * HEAD_DIM)
            one = slice(hh * HEAD_DIM, hh * HEAD_DIM + 1)
            slope = jnp.where(pair == 0, SLOPES[hh], jnp.where(pair == 1, SLOPES[2 + hh], SLOPES[4 + hh]))
            sc = dot("bqd,bkd->bqk", q_cur[:, :, sl], k_win[:, :, sl]) * ATT_SCALE - slope * distf_q
            p = jnp.exp(jnp.where(valid_q, sc - l_cur[:, :, one], MASK_VALUE))
            dp = dot("bqd,bkd->bqk", do_cur[:, :, sl], v_win[:, :, sl])
            ds = (p * (dp - de_cur[:, :, one]) * ATT_SCALE).astype(BF16)
            dqs.append(dot("bqk,bkd->bqd", ds, k_win[:, :, sl]))

            sc2 = dot("bqd,bkd->bqk", q_win[:, :, sl], k_cur[:, :, sl]) * ATT_SCALE - slope * distf_k
            p2 = jnp.exp(jnp.where(valid_k, sc2 - l_win[:, :, one], MASK_VALUE))
            dvs.append(dot("bqk,bqd->bkd", p2.astype(BF16), do_win[:, :, sl]))
            dp2 = dot("bqd,bkd->bqk", do_win[:, :, sl], v_cur[:, :, sl])
            ds2 = (p2 * (dp2 - de_win[:, :, one]) * ATT_SCALE).astype(BF16)
            dks.append(dot("bqk,bqd->bkd", ds2, q_win[:, :, sl]))
        for parts, acc, out in ((dqs, pq if has_prev else None, dq_ref), (dks, pk if has_prev else None, dk_ref),
                                (dvs, pv if has_prev else None, dv_ref)):
            val = jnp.concatenate(parts, axis=2)
            for n, (g, r) in enumerate(items):
                piece = val[n]
                if has_prev:
                    piece = piece + _residue(acc, r, ATT_BLOCK, dil, g * ATT_BLOCK)
                _store_residue(out, r, dil, g * ATT_BLOCK, piece)

    blk = pl.BlockSpec((rows, PAIR), lambda i, p: (i, p))
    in_specs = (_window_specs(dil, n_steps, lambda p: p) + _window_specs(dil, n_steps, lambda p: N_PAIRS + p)
                + _window_specs(dil, n_steps, lambda p: 2 * N_PAIRS + p) + _window_specs(dil, n_steps, lambda p: p) * 3)
    ins = [qkv] * 9 + [do] * 3 + [lse] * 3 + [delta] * 3
    if has_prev:
        in_specs += [blk] * 3
        ins += list(prev)
    return pl.pallas_call(
        body, name=name, grid=(n_steps, N_PAIRS), in_specs=in_specs, out_specs=(blk, blk, blk),
        out_shape=(jax.ShapeDtypeStruct((s, ATT_WIDTH), F32),) * 3,
        compiler_params=_params(2))(*ins)


TB = 2 * REC_CHUNK
REC_ROWS = 5 * REC_WIDTH


REC_LEVELS = 6


def _scan_pos(p, rev):
    p = p & (REC_CHUNK - 1)
    return (REC_CHUNK - 1 - p) if rev else p


def _split3(x):
    hi = x.astype(BF16)
    rest = x - hi.astype(F32)
    mid = rest.astype(BF16)
    return hi, mid, (rest - mid.astype(F32)).astype(BF16)


def _chunk_sums(x, rev, with_levels):
    row = lax.broadcasted_iota(jnp.int32, (TB, TB), 0)
    col = lax.broadcasted_iota(jnp.int32, (TB, TB), 1)
    same = (row < REC_CHUNK) == (col < REC_CHUNK)
    s_row, s_col = _scan_pos(row, rev), _scan_pos(col, rev)
    mats = [same & (s_row <= s_col)]
    if with_levels:
        for level in range(1, REC_LEVELS + 1):
            shift = REC_LEVELS + 1 - level
            boundary = ((s_col >> shift) << shift) + (REC_CHUNK >> level) - 1
            mats.append(same & (s_row <= boundary))
        mats.append(same)
    cat = jnp.concatenate([m.astype(BF16) for m in mats], axis=1)
    total = sum(jnp.dot(term, cat, preferred_element_type=F32) for term in _split3(x))
    return [total[:, n * TB:(n + 1) * TB] for n in range(len(mats))]


def _hg_prep(qraw, z, lb, rev):
    lane = lax.broadcasted_iota(jnp.int32, (REC_WIDTH, TB), 1)
    in_a = lane < REC_CHUNK
    scan = _scan_pos(lane, rev)
    sig, sigm = _sigmoid(z), _sigmoid(-z)
    f = lb + (1.0 - lb) * sig
    kk = (1.0 - lb) * sigm
    sums = _chunk_sums(jnp.log(jnp.maximum(f, F_TINY)), rev, True)
    b, bend = sums[0], sums[-1]
    q = qraw * _sigmoid(qraw)
    eq, ek = [], []
    for level in range(1, REC_LEVELS + 1):
        r = sums[level]
        e = jnp.exp(jnp.minimum(b - r, r - b))
        second = ((scan >> (REC_LEVELS - level)) & 1) == 1
        eq.append(jnp.where(second, e, 0.0))
        ek.append(jnp.where(second, 0.0, e))
    lanes_end = (0, REC_CHUNK) if rev else (REC_CHUNK - 1, TB - 1)
    end_a, end_b = (b[:, n:n + 1] for n in lanes_end)
    return dict(in_a=in_a, sig=sig, sigm=sigm, f=f, kk=kk, b=b, end_a=end_a, end_b=end_b,
                q=q, qh=q * jnp.exp(b), kh=kk * jnp.exp(bend - b), ekb=jnp.exp(bend - b), eq=eq, ek=ek)


def _level_masks(rev):
    row = lax.broadcasted_iota(jnp.int32, (TB, TB), 0)
    col = lax.broadcasted_iota(jnp.int32, (TB, TB), 1)
    same = (row < REC_CHUNK) == (col < REC_CHUNK)
    s_row, s_col = _scan_pos(row, rev), _scan_pos(col, rev)
    masks = [same & ((s_row >> (REC_LEVELS + 1 - level)) == (s_col >> (REC_LEVELS + 1 - level)))
             for level in range(1, REC_LEVELS + 1)]
    return masks, row == col


def _head_rows(x, h):
    return x[h * HEAD_DIM:(h + 1) * HEAD_DIM, :]


def _block_diag_mask():
    r = lax.broadcasted_iota(jnp.int32, (REC_WIDTH, REC_WIDTH), 0) // HEAD_DIM
    c = lax.broadcasted_iota(jnp.int32, (REC_WIDTH, REC_WIDTH), 1) // HEAD_DIM
    return (r == c).astype(F32)


def _heads(x):
    return x.reshape(N_HEADS, HEAD_DIM, TB)


def _hgrn_scan(projt, lb, rev, name):
    s = projt.shape[1]
    nblk = s // TB
    zrow = 2 if rev else 1
    tmap = (lambda i: nblk - 1 - i) if rev else (lambda i: i)
    tn = (((0,), (0,)), ((), ()))
    nt = (((1,), (1,)), ((), ()))

    def body(q_ref, z_ref, v_ref, lb_ref, o_ref, hs_ref, at_ref, h_ref):
        @pl.when(pl.program_id(0) == 0)
        def _():
            h_ref[...] = jnp.zeros_like(h_ref)

        v = v_ref[...]
        vb = v.astype(BF16)
        pr = _hg_prep(q_ref[...], z_ref[...], lb_ref[...], rev)
        q, kk = pr["q"], pr["kk"]
        masks, diag = _level_masks(rev)
        own = jnp.sum(_heads(q * kk), axis=1, keepdims=True)
        sc = jnp.where(diag[None], own, 0.0)
        for level in range(REC_LEVELS):
            qt = _heads((q * pr["eq"][level]).astype(BF16))
            kt = _heads((kk * pr["ek"][level]).astype(BF16))
            sc = sc + jnp.where(masks[level][None],
                                jnp.einsum("hks,hkt->hst", kt, qt, preferred_element_type=F32), 0.0)
        a_bf = sc.astype(BF16)
        at_ref[...] = a_bf
        o = jnp.einsum("hvs,hst->hvt", _heads(vb), a_bf, preferred_element_type=F32).reshape(REC_WIDTH, TB)
        bd_mask = _block_diag_mask()
        order = ((1, ~pr["in_a"], pr["end_b"]), (0, pr["in_a"], pr["end_a"]))
        if not rev:
            order = order[::-1]
        for slot, msk, bend in order:
            h0 = h_ref[...]
            hs_ref[slot] = h0
            o = o + lax.dot_general(h0.astype(BF16), jnp.where(msk, pr["qh"], 0.0).astype(BF16), tn,
                                    preferred_element_type=F32)
            upd = lax.dot_general(jnp.where(msk, pr["kh"], 0.0).astype(BF16), vb, nt, preferred_element_type=F32)
            h_ref[...] = jnp.exp(bend) * h0 + upd * bd_mask
        o_ref[...] = o

    row_blk = lambda r: pl.BlockSpec((REC_WIDTH, TB), lambda i: (r, tmap(i)))
    return pl.pallas_call(
        body, name=name, grid=(nblk,),
        in_specs=[row_blk(0), row_blk(zrow), row_blk(3), pl.BlockSpec((REC_WIDTH, 1), lambda i: (0, 0))],
        out_specs=(pl.BlockSpec((REC_WIDTH, TB), lambda i: (0, tmap(i))),
                   pl.BlockSpec((2, REC_WIDTH, REC_WIDTH), lambda i: (tmap(i), 0, 0)),
                   pl.BlockSpec((None, N_HEADS, TB, TB), lambda i: (tmap(i), 0, 0, 0))),
        out_shape=(jax.ShapeDtypeStruct((REC_WIDTH, s), F32),
                   jax.ShapeDtypeStruct((s // REC_CHUNK, REC_WIDTH, REC_WIDTH), F32),
                   jax.ShapeDtypeStruct((nblk, N_HEADS, TB, TB), BF16)),
        scratch_shapes=[pltpu.VMEM((REC_WIDTH, REC_WIDTH), F32)],
        compiler_params=_params(1))(projt, projt, projt, lb)


def _hgrn_scan_bwd(projt, lb, dot, hs, at, prev, rev, name):
    s = projt.shape[1]
    nblk = s // TB
    zrow = 2 if rev else 1
    tmap = (lambda i: i) if rev else (lambda i: nblk - 1 - i)
    has_prev = prev is not None
    tn = (((0,), (0,)), ((), ()))
    nt = (((1,), (1,)), ((), ()))

    def body(*refs):
        q_ref, z_ref, v_ref, lb_ref, do_ref, hs_ref, at_ref = refs[:7]
        rest = refs[7:]
        if has_prev:
            pq_ref, pv_ref = rest[:2]
            rest = rest[2:]
        dq_ref, dz_ref, dv_ref, dlb_ref, dh_ref = rest

        @pl.when(pl.program_id(0) == 0)
        def _():
            dh_ref[...] = jnp.zeros_like(dh_ref)
            dlb_ref[...] = jnp.zeros_like(dlb_ref)

        qraw, v, do, lbv = q_ref[...], v_ref[...], do_ref[...], lb_ref[...]
        dob, vb = do.astype(BF16), v.astype(BF16)
        pr = _hg_prep(qraw, z_ref[...], lbv, rev)
        q, kk, b, in_a = pr["q"], pr["kk"], pr["b"], pr["in_a"]
        masks, diag = _level_masks(rev)
        dot = lambda eq, x, y: jnp.einsum(eq, x, y, preferred_element_type=F32)
        d_at = dot("hvs,hvt->hst", _heads(vb), _heads(dob))
        dv = dot("hvt,hst->hvs", _heads(dob), at_ref[...]).reshape(REC_WIDTH, TB)
        d_own = jnp.sum(jnp.where(diag[None], d_at, 0.0), axis=1, keepdims=True)
        dq_in = (d_own * _heads(kk)).reshape(REC_WIDTH, TB)
        dk_in = (d_own * _heads(q)).reshape(REC_WIDTH, TB)
        db_in = jnp.zeros((REC_WIDTH, TB), F32)
        for lv in range(REC_LEVELS):
            d_lv = jnp.where(masks[lv][None], d_at, 0.0).astype(BF16)
            q_lv, k_lv = (q * pr["eq"][lv]).astype(BF16), (kk * pr["ek"][lv]).astype(BF16)
            dqt = dot("hks,hst->hkt", _heads(k_lv), d_lv).reshape(REC_WIDTH, TB)
            dkt = dot("hkt,hst->hks", _heads(q_lv), d_lv).reshape(REC_WIDTH, TB)
            dq_in = dq_in + pr["eq"][lv] * dqt
            dk_in = dk_in + pr["ek"][lv] * dkt
            db_in = db_in + q_lv.astype(F32) * dqt - k_lv.astype(F32) * dkt
        dq = dk = jnp.zeros((REC_WIDTH, TB), F32)

        zero = jnp.zeros((REC_WIDTH, TB), F32)
        bd_mask = _block_diag_mask()
        eb = jnp.exp(b)
        const = zero
        order = ((0, in_a, pr["end_a"]), (1, ~in_a, pr["end_b"]))
        if not rev:
            order = order[::-1]
        for slot, msk, bend in order:
            h0 = hs_ref[slot]
            dh1 = dh_ref[...]
            dh1b = dh1.astype(BF16)
            dq = dq + eb * jnp.dot(h0.astype(BF16), jnp.where(msk, do, 0.0).astype(BF16), preferred_element_type=F32)
            dv = dv + lax.dot_general(dh1b, jnp.where(msk, pr["kh"], 0.0).astype(BF16), tn, preferred_element_type=F32)
            dk_int = pr["ekb"] * jnp.dot(dh1b, jnp.where(msk, v, 0.0).astype(BF16), preferred_element_type=F32)
            dk = dk + dk_int
            ebend = jnp.exp(bend)
            c = (jnp.sum(kk * dk_int, axis=1, keepdims=True)
                 + ebend * jnp.sum(h0 * dh1, axis=1, keepdims=True))
            const = const + jnp.where(msk, c, 0.0)
            upd = lax.dot_general(jnp.where(msk, pr["qh"], 0.0).astype(BF16), dob, nt, preferred_element_type=F32)
            dh_ref[...] = ebend * dh1 + upd * bd_mask

        dg = _chunk_sums(db_in + q * dq - kk * dk, not rev, False)[0] + const
        dq, dk = dq + dq_in, dk + dk_in
        sig, sigm, f = pr["sig"], pr["sigm"], pr["f"]
        live = f > F_TINY
        inv_f = 1.0 / jnp.maximum(f, F_TINY)
        one_lb = 1.0 - lbv
        dz = sig * sigm * one_lb * (jnp.where(live, dg * inv_f, 0.0) - dk)
        dlb_ref[...] += jnp.sum(sigm * (jnp.where(live, dg * inv_f, 0.0) - dk), axis=1, keepdims=True)
        dqr = dq * _silu_grad(qraw)
        if has_prev:
            dqr = dqr + pq_ref[...]
            dv = dv + pv_ref[...]
        dq_ref[...] = dqr
        dz_ref[...] = dz
        dv_ref[...] = dv

    row_blk = lambda r: pl.BlockSpec((REC_WIDTH, TB), lambda i: (r, tmap(i)))
    blk = pl.BlockSpec((REC_WIDTH, TB), lambda i: (0, tmap(i)))
    col = pl.BlockSpec((REC_WIDTH, 1), lambda i: (0, 0))
    in_specs = [row_blk(0), row_blk(zrow), row_blk(3), col, blk,
                pl.BlockSpec((2, REC_WIDTH, REC_WIDTH), lambda i: (tmap(i), 0, 0)),
                pl.BlockSpec((None, N_HEADS, TB, TB), lambda i: (tmap(i), 0, 0, 0))]
    ins = [projt, projt, projt, lb, dot, hs, at]
    if has_prev:
        in_specs += [blk, blk]
        ins += list(prev)
    t_shape = jax.ShapeDtypeStruct((REC_WIDTH, s), F32)
    return pl.pallas_call(
        body, name=name, grid=(nblk,), in_specs=in_specs, out_specs=(blk, blk, blk, col),
        out_shape=(t_shape, t_shape, t_shape, jax.ShapeDtypeStruct((REC_WIDTH, 1), F32)),
        scratch_shapes=[pltpu.VMEM((REC_WIDTH, REC_WIDTH), F32)],
        compiler_params=_params(1))(*ins)


REC_OUT_COLS = 512


def _head_rms(o):
    o3 = o.reshape(N_HEADS, HEAD_DIM, o.shape[1])
    rstd = lax.rsqrt(jnp.mean(o3 * o3, axis=1, keepdims=True) + EPS)
    return o3 * rstd, rstd


def _hgrn_out(of, ob, projt, wn, name):
    s = of.shape[1]

    def body(of_ref, ob_ref, g_ref, wn_ref, o_ref):
        on, _ = _head_rms(of_ref[...] + ob_ref[...])
        g = g_ref[...]
        y = on.reshape(REC_WIDTH, REC_OUT_COLS) * wn_ref[...] * (g * _sigmoid(g))
        o_ref[...] = y.T.astype(BF16)

    blk = pl.BlockSpec((REC_WIDTH, REC_OUT_COLS), lambda i: (0, i))
    return pl.pallas_call(
        body, name=name, grid=(s // REC_OUT_COLS,),
        in_specs=[blk, blk, pl.BlockSpec((REC_WIDTH, REC_OUT_COLS), lambda i: (4, i)),
                  pl.BlockSpec((REC_WIDTH, 1), lambda i: (0, 0))],
        out_specs=pl.BlockSpec((REC_OUT_COLS, REC_WIDTH), lambda i: (i, 0)),
        out_shape=jax.ShapeDtypeStruct((s, REC_WIDTH), BF16), compiler_params=_params(1))(of, ob, projt, wn)


def _hgrn_out_bwd(drec, of, ob, projt, wn, name):
    s = of.shape[1]

    def body(d_ref, of_ref, ob_ref, g_ref, wn_ref, do_ref, dg_ref, dwn_ref):
        @pl.when(pl.program_id(0) == 0)
        def _():
            dwn_ref[...] = jnp.zeros_like(dwn_ref)

        dy = d_ref[...].T
        on3, rstd = _head_rms(of_ref[...] + ob_ref[...])
        on = on3.reshape(REC_WIDTH, REC_OUT_COLS)
        g, wnv = g_ref[...], wn_ref[...]
        dg_ref[...] = dy * on * wnv * _silu_grad(g)
        d_onw = dy * (g * _sigmoid(g))
        dwn_ref[...] += jnp.sum(d_onw * on, axis=1, keepdims=True)
        d_on3 = (d_onw * wnv).reshape(N_HEADS, HEAD_DIM, REC_OUT_COLS)
        do3 = rstd * (d_on3 - on3 * jnp.mean(d_on3 * on3, axis=1, keepdims=True))
        do_ref[...] = do3.reshape(REC_WIDTH, REC_OUT_COLS)

    blk = pl.BlockSpec((REC_WIDTH, REC_OUT_COLS), lambda i: (0, i))
    col = pl.BlockSpec((REC_WIDTH, 1), lambda i: (0, 0))
    t_shape = jax.ShapeDtypeStruct((REC_WIDTH, s), F32)
    return pl.pallas_call(
        body, name=name, grid=(s // REC_OUT_COLS,),
        in_specs=[pl.BlockSpec((REC_OUT_COLS, REC_WIDTH), lambda i: (i, 0)), blk, blk,
                  pl.BlockSpec((REC_WIDTH, REC_OUT_COLS), lambda i: (4, i)), col],
        out_specs=(blk, blk, col),
        out_shape=(t_shape, t_shape, jax.ShapeDtypeStruct((REC_WIDTH, 1), F32)),
        compiler_params=_params(1))(drec, of, ob, projt, wn)


def _lower_bounds(gamma, name):
    def body(g_ref, lb_ref, p_ref):
        g0, g1 = g_ref[0:1, :], g_ref[1:2, :]
        m = jnp.maximum(g0, g1)
        e0, e1 = jnp.exp(g0 - m), jnp.exp(g1 - m)
        p0, p1 = e0 / (e0 + e1), e1 / (e0 + e1)
        lb_ref[...] = (p0 + p1) - p0
        p_ref[0:1, :] = p0
        p_ref[1:2, :] = p1

    n = gamma.shape[1]
    return pl.pallas_call(body, name=name,
                          out_shape=(jax.ShapeDtypeStruct((1, n), F32), jax.ShapeDtypeStruct((2, n), F32)))(gamma)


def _lower_bounds_bwd(dlb1, p, name):
    def body(d_ref, p_ref, o_ref):
        p0, p1, d = p_ref[0:1, :], p_ref[1:2, :], d_ref[...]
        inner = p1 * d
        o_ref[0:1, :] = p0 * (0.0 - inner)
        o_ref[1:2, :] = p1 * (d - inner)

    return pl.pallas_call(body, name=name, out_shape=jax.ShapeDtypeStruct(p.shape, F32))(dlb1, p)


def _split_w_in(w_in):
    return dict(conv=w_in[:, G_CONV[0]:G_CONV[1]], qkv=w_in[:, G_QKV[0]:G_QKV[1]],
                rec_t=w_in[:, G_REC[0]:].T, nat=w_in[:, :G_REC[0]])


def _split_w_rest(w_out, w_up, w_down):
    return dict(out=w_out, out_a=w_out[:CONV_CH], out_b=w_out[CONV_CH:CONV_CH + ATT_WIDTH],
                out_c=w_out[CONV_CH + ATT_WIDTH:], up=w_up, down=w_down)


def _col(v):
    return v.reshape(-1, 1)


def _sequence_step(x, tgt, mods, lbs, small, w_in0, later_weights, final_w, on_layer_grads):
    saved = []
    xin = x
    big = [_split_w_in(w_in0), None]
    h1 = _resid_norm_mod(x, None, None, small[0]["norm1_w"], mods[0][1:2], mods[0][0:1], "norm1_first")
    for l in range(DEPTH):
        sm, w, md = small[l], big[l], mods[l]
        pa = _matmul(h1, w["conv"], "nn", F32, f"proj_conv")
        qkv = _matmul(h1, w["qkv"], "nn", F32, f"proj_qkv")
        projt = _matmul(w["rec_t"], h1, "nt", F32, f"proj_rec")
        a_out, cv = _conv_mixer(pa, sm["conv_a_w"], sm["conv_a_b"], sm["ln_a_w"], sm["ln_a_b"], f"conv_mixer")
        outs, lses = zip(*[_attn_branch(qkv, d, f"attn_d{d}") for d in DILATIONS])
        att, att32, lse = _attn_combine(outs, lses, f"attn_combine")
        lb_f, lb_b = _col(lbs[l][0]), _col(lbs[l][1])
        of, hsf, atf = _hgrn_scan(projt, lb_f, False, "hgrn_fwd")
        ob, hsb, atb = _hgrn_scan(projt, lb_b, True, "hgrn_rev")
        wn = _col(sm["rec_norm_w"])
        rec = _hgrn_out(of, ob, projt, wn, f"hgrn_out")
        mixed = jnp.concatenate([a_out, att, rec], axis=1)
        if l == 0:
            w_in1, w_out_all, w_up_all, w_down_all = later_weights(rec)
            big[0].update(_split_w_rest(w_out_all[0], w_up_all[0], w_down_all[0]))
            big[1] = dict(_split_w_in(w_in1), **_split_w_rest(w_out_all[1], w_up_all[1], w_down_all[1]))
        r1 = _matmul(mixed, w["out"], "nn", BF16, "out_proj")
        xmid, h2 = _resid_norm_mod(xin, r1, md[2:3], sm["norm2_w"], md[4:5], md[3:4], f"norm2")
        u = _matmul(h2, w["up"], "nn", BF16, f"ffn_up")
        act = _ffn_act(u, sm["conv_f_w"], f"ffn_act")
        r2 = _matmul(act, w["down"], "nn", BF16, "ffn_down")
        saved.append(dict(xin=xin, h1=h1, pa=pa, qkv=qkv, projt=projt, cv=cv, att32=att32, lse=lse, of=of, ob=ob,
                          hsf=hsf, hsb=hsb, atf=atf, atb=atb, lb_f=lb_f, lb_b=lb_b, wn=wn, mixed=mixed, r1=r1, xmid=xmid, h2=h2,
                          u=u, act=act, r2=r2))
        if l + 1 < DEPTH:
            nxt = small[l + 1]
            xin, h1 = _resid_norm_mod(xmid, r2, md[5:6], nxt["norm1_w"], mods[l + 1][1:2], mods[l + 1][0:1],
                                      "norm1")
    top = saved[-1]
    loss, dx, dr2, dg2, dfw = _final_loss(top["xmid"], top["r2"], mods[-1][5:6], final_w, tgt, "final_loss")

    grads = [None] * DEPTH
    order_after = None
    for l in reversed(range(DEPTH)):
        sm, w, md, sv = small[l], big[l], mods[l], saved[l]
        dact = _matmul(dr2, w["down"], "nt", BF16, f"d_act")
        g_down = _matmul(dr2, sv["act"], "tn", F32, "dw_down").T
        conv_f_w = sm["conv_f_w"] if order_after is None else sm["conv_f_w"] + order_after
        dug, duv, dwg, dwv = _ffn_act_bwd(sv["u"], dact, conv_f_w, f"ffn_act_bwd")
        du = jnp.concatenate([dug, duv], axis=1)
        dh2 = _matmul(du, w["up"], "nt", BF16, "d_h2")
        g_up = _matmul(sv["h2"], du, "tn", F32, f"dw_up")
        dxmid, dr1, dsh2, dsc2, dnw2, dg1 = _norm_bwd(sv["xmid"], [dh2], dx, sm["norm2_w"], md[4:5], md[2:3], sv["r1"],
                                                     f"norm2_bwd")
        dmix_a = _matmul(dr1, w["out_a"], "nt", F32, f"d_mix_a")
        dmix_b = _matmul(dr1, w["out_b"], "nt", F32, f"d_mix_b")
        dmix_c = _matmul(dr1, w["out_c"], "nt", F32, f"d_mix_c")
        g_out = _matmul(sv["mixed"], dr1, "tn", F32, f"dw_out")
        dc, dlnw, dlnb, dcb = _conv_mixer_bwd_ln(sv["cv"], dmix_a, sm["ln_a_w"], sm["ln_a_b"], f"conv_mixer_bwd_ln")
        dpa, dcw = _conv_mixer_bwd_conv(sv["pa"], dc, sm["conv_a_w"], f"conv_mixer_bwd_conv")
        delta = _attn_delta(dmix_b, sv["att32"], "attn_delta")
        dqkv = None
        for d in DILATIONS:
            dqkv = _attn_branch_bwd(sv["qkv"], dmix_b, sv["lse"], delta, dqkv, d, f"attn_bwd_d{d}")
        dot, dgt, dwn = _hgrn_out_bwd(dmix_c, sv["of"], sv["ob"], sv["projt"], sv["wn"], f"hgrn_out_bwd")
        dqf, dzf, dvf, dlbf = _hgrn_scan_bwd(sv["projt"], sv["lb_f"], dot, sv["hsf"], sv["atf"], None, False,
                                             "hgrn_fwd_bwd")
        dqt, dzb, dvt, dlbb = _hgrn_scan_bwd(sv["projt"], sv["lb_b"], dot, sv["hsb"], sv["atb"], (dqf, dvf), True,
                                             "hgrn_rev_bwd")
        dprojt = jnp.concatenate([dqt, dzf, dzb, dvt, dgt], axis=0).astype(BF16)
        dnat = jnp.concatenate([dpa] + [t.astype(BF16) for t in dqkv], axis=1)
        dh1_a = _matmul(dnat, w["nat"], "nt", BF16, "d_h1_nat")
        dh1_b = _matmul(dprojt, w["rec_t"], "tn", BF16, "d_h1_rec")
        g_in_nat = _matmul(sv["h1"], dnat, "tn", F32, f"dw_in_nat")
        g_in_rec_t = _matmul(dprojt, sv["h1"], "nn", F32, f"dw_in_rec")
        g_in = jnp.concatenate([g_in_nat, g_in_rec_t.T], axis=1)
        if l > 0:
            below = saved[l - 1]
            dx, dr2, dsh1, dsc1, dnw1, dg2_below = _norm_bwd(sv["xin"], [dh1_a, dh1_b], dxmid, sm["norm1_w"], md[1:2],
                                                            mods[l - 1][5:6], below["r2"], f"norm1_bwd")
        else:
            dx, dsh1, dsc1, dnw1 = _norm_bwd(sv["xin"], [dh1_a, dh1_b], dxmid, sm["norm1_w"], md[1:2], None, None,
                                             f"norm1_bwd")
        grads[l] = dict(w_in=g_in, w_out=g_out, w_up=g_up, w_down=g_down,
                        mod=[dsh1, dsc1, dg1, dsh2, dsc2, dg2], norm1_w=dnw1, conv_a_w=dcw[:CONV_WIDTH], conv_a_b=dcb,
                        ln_a_w=dlnw, ln_a_b=dlnb, lb=jnp.concatenate([dlbf.reshape(1, -1), dlbb.reshape(1, -1)], axis=0),
                        rec_norm_w=dwn.reshape(1, -1), norm2_w=dnw2,
                        conv_f_w=jnp.concatenate([dwg[:3], dwv[:3]], axis=1))
        order_after = on_layer_grads(l, [g_in, g_out, g_up, g_down])
        if l > 0:
            dg2 = dg2_below
    return loss[0, 0], dx, grads, dfw


def _adamw_math(w, g, m, v):
    m = ADAM_B1 * m + (1.0 - ADAM_B1) * g
    v = ADAM_B2 * v + (1.0 - ADAM_B2) * (g * g)
    m_hat = m / (1.0 - ADAM_B1 ** ADAM_STEP)
    v_hat = v / (1.0 - ADAM_B2 ** ADAM_STEP)
    delta = -ADAM_LR * (m_hat / (jnp.sqrt(v_hat) + ADAM_EPS) + ADAM_WD * w)
    return delta, m, v


def _row_tile(rows, cols, max_elems=384 * 1024):
    best = None
    for t in range(8, rows + 1, 8):
        if rows % t == 0 and t * cols <= max_elems:
            best = t
    return best or rows


def _adamw(w, g, m, v, name):
    nl, r, c = w.shape
    tr = _row_tile(r, c)

    def body(w_ref, g_ref, m_ref, v_ref, d_ref, m2_ref, v2_ref):
        d_ref[...], m2_ref[...], v2_ref[...] = _adamw_math(w_ref[...], g_ref[...], m_ref[...], v_ref[...])

    blk = pl.BlockSpec((None, tr, c), lambda l, i: (l, i, 0))
    shape = jax.ShapeDtypeStruct((nl, r, c), F32)
    return pl.pallas_call(body, name=name, grid=(nl, r // tr), in_specs=[blk] * 4, out_specs=(blk, blk, blk),
                          out_shape=(shape, shape, shape), compiler_params=_params(2))(w, g, m, v)


ADA_SHARD = N_MOD * D_MODEL // 4
ADA_COLS = 512
ADA_ROWS = 256
HIGHEST = lax.Precision.HIGHEST


def _ada_mod(c_all, w_ada, b_sh, name):
    def body(c_ref, w_ref, b_ref, o_ref):
        cv = c_ref[...]
        o_ref[...] = jnp.dot(cv * _sigmoid(cv), w_ref[...], precision=HIGHEST, preferred_element_type=F32) + b_ref[...]

    return pl.pallas_call(
        body, name=name, grid=(DEPTH, ADA_SHARD // ADA_COLS),
        in_specs=[pl.BlockSpec((8, D_MODEL), lambda l, j: (0, 0)),
                  pl.BlockSpec((None, D_MODEL, ADA_COLS), lambda l, j: (l, 0, j)),
                  pl.BlockSpec((None, 1, ADA_COLS), lambda l, j: (l, 0, j))],
        out_specs=pl.BlockSpec((None, 8, ADA_COLS), lambda l, j: (l, 0, j)),
        out_shape=jax.ShapeDtypeStruct((DEPTH, 8, ADA_SHARD), F32), compiler_params=_params(2))(c_all, w_ada, b_sh)


def _ada_update(c_all, dmod_sh, w, m, v, name):
    def body(c_ref, d_ref, w_ref, m_ref, v_ref, g_ref, dl_ref, m2_ref, v2_ref):
        cv = c_ref[...]
        g = lax.dot_general(cv * _sigmoid(cv), d_ref[...], (((0,), (0,)), ((), ())), precision=HIGHEST,
                            preferred_element_type=F32)
        g_ref[...] = g
        dl_ref[...], m2_ref[...], v2_ref[...] = _adamw_math(w_ref[...], g, m_ref[...], v_ref[...])

    blk = pl.BlockSpec((None, ADA_ROWS, ADA_SHARD), lambda l, i: (l, i, 0))
    shape = jax.ShapeDtypeStruct((DEPTH, D_MODEL, ADA_SHARD), F32)
    return pl.pallas_call(
        body, name=name, grid=(DEPTH, D_MODEL // ADA_ROWS),
        in_specs=[pl.BlockSpec((8, ADA_ROWS), lambda l, i: (0, i)),
                  pl.BlockSpec((None, 8, ADA_SHARD), lambda l, i: (l, 0, 0)), blk, blk, blk],
        out_specs=(blk,) * 4, out_shape=(shape,) * 4, compiler_params=_params(2))(c_all, dmod_sh, w, m, v)


def _sum_devices(packs, name):
    def body(p_ref, o_ref):
        acc = p_ref[0]
        for dev in range(1, 8):
            acc = acc + p_ref[dev]
        o_ref[...] = acc

    return pl.pallas_call(body, name=name, out_shape=jax.ShapeDtypeStruct(packs.shape[1:], F32))(packs)


def _mesh_pos():
    return lax.axis_index("x"), lax.axis_index("y"), lax.axis_index("c")


def _flip(v, bit):
    return 1 - v if bit else v


def _allgather_devices(x, name):
    m_per, n = x.shape

    def body(x_ref, out_ref, send_sems, recv_sems, local_sem):
        ix, iy, ic = _mesh_pos()
        me, sibling = (ix, iy, ic), (ix, iy, 1 - ic)
        chips = [(1 - ix, iy), (ix, 1 - iy), (1 - ix, 1 - iy)]

        def rows(px, py, pc):
            return out_ref.at[pl.ds((4 * px + 2 * py + pc) * m_per, m_per), :]

        def copy(k, block, to, src=None):
            return pltpu.make_async_remote_copy(
                src_ref=rows(*block) if src is None else src, dst_ref=rows(*block),
                send_sem=send_sems.at[k], recv_sem=recv_sems.at[k], device_id=to, device_id_type=MESH)

        mine = pltpu.make_async_copy(x_ref, rows(*me), local_sem)
        mine.start()
        first = [copy(0, me, sibling, src=x_ref)]
        first += [copy(1 + j, me, (*chip, ic), src=x_ref) for j, chip in enumerate(chips)]
        for cp in first:
            cp.start()
        passed = [copy(4 + j, (*chip, ic), sibling) for j, chip in enumerate(chips)]
        for j, chip in enumerate(chips):
            copy(1 + j, (*chip, ic), me).wait_recv()
            passed[j].start()
        copy(0, sibling, me).wait_recv()
        for j, chip in enumerate(chips):
            copy(4 + j, (*chip, 1 - ic), me).wait_recv()
        for cp in first + passed:
            cp.wait_send()
        mine.wait()

    return pl.pallas_call(
        body, name=name, out_shape=jax.ShapeDtypeStruct((8 * m_per, n), x.dtype),
        in_specs=[pl.BlockSpec(memory_space=pltpu.VMEM)], out_specs=pl.BlockSpec(memory_space=pltpu.VMEM),
        scratch_shapes=[pltpu.SemaphoreType.DMA((7,)), pltpu.SemaphoreType.DMA((7,)), pltpu.SemaphoreType.DMA],
    )(x)


def _gather_chips(shards, name):
    n = len(shards)

    def body(*refs):
        ins, outs = refs[:n], refs[n:2 * n]
        send_sems, recv_sems, local_sems = refs[2 * n:]
        ix, iy, ic = _mesh_pos()
        me = 2 * ix + iy
        local = [pltpu.make_async_copy(ins[a], outs[a].at[me], local_sems.at[a]) for a in range(n)]
        for cp in local:
            cp.start()
        remote = []
        for a in range(n):
            for k in (1, 2, 3):
                px, py = _flip(ix, k & 2), _flip(iy, k & 1)
                sems = dict(send_sem=send_sems.at[3 * a + k - 1], recv_sem=recv_sems.at[3 * a + k - 1],
                            device_id=(px, py, ic), device_id_type=MESH)
                out_cp = pltpu.make_async_remote_copy(src_ref=ins[a], dst_ref=outs[a].at[me], **sems)
                in_cp = pltpu.make_async_remote_copy(src_ref=ins[a], dst_ref=outs[a].at[2 * px + py], **sems)
                out_cp.start()
                remote.append((out_cp, in_cp))
        for out_cp, in_cp in remote:
            out_cp.wait_send()
            in_cp.wait_recv()
        for cp in local:
            cp.wait()

    return pl.pallas_call(
        body, name=name, in_specs=[ANY] * n, out_specs=tuple([ANY] * n),
        out_shape=tuple(jax.ShapeDtypeStruct((4,) + t.shape, t.dtype) for t in shards),
        scratch_shapes=[pltpu.SemaphoreType.DMA((3 * n,)), pltpu.SemaphoreType.DMA((3 * n,)),
                        pltpu.SemaphoreType.DMA((n,))],
    )(*shards)


HBM = pl.BlockSpec(memory_space=pltpu.HBM)
SEM = pl.BlockSpec(memory_space=pltpu.SEMAPHORE)
DATAFLOW = pltpu.SideEffectType.DATAFLOW_SIDE_EFFECTING


def _peer_chip(ix, iy, k):
    return _flip(ix, k & 2), _flip(iy, k & 1)


def _gather_chips_start(shards, name):
    n = len(shards)

    def body(*refs):
        src, land = refs[:n], refs[n:2 * n]
        send_sems, recv_sems = refs[2 * n], refs[2 * n + 1]
        token = refs[-1]
        ix, iy, ic = _mesh_pos()
        me = 2 * ix + iy
        for a in range(n):
            for k in (1, 2, 3):
                px, py = _peer_chip(ix, iy, k)
                pltpu.make_async_remote_copy(
                    src_ref=src[a], dst_ref=land[a].at[me], send_sem=send_sems.at[3 * a + k - 1],
                    recv_sem=recv_sems.at[3 * a + k - 1], device_id=(px, py, ic), device_id_type=MESH).start()
        token[...] = jnp.zeros_like(token)

    hbm = lambda shape, dtype: pltpu.HBM(shape, dtype)
    operands = ([pltpu.with_memory_space_constraint(t, pltpu.HBM) for t in shards]
                + [pltpu.with_memory_space_constraint(lax.empty((4,) + t.shape, t.dtype), pltpu.HBM) for t in shards])
    return pl.pallas_call(
        body, name=name,
        out_shape=(pltpu.SemaphoreType.DMA((3 * n,)), pltpu.SemaphoreType.DMA((3 * n,)),
                   *[hbm(t.shape, t.dtype) for t in shards], *[hbm((4,) + t.shape, t.dtype) for t in shards],
                   jax.ShapeDtypeStruct((8, LANES), F32)),
        in_specs=(HBM,) * (2 * n),
        out_specs=(SEM, SEM) + (HBM,) * (2 * n) + (pl.BlockSpec(memory_space=pltpu.VMEM),),
        input_output_aliases={a: 2 + a for a in range(2 * n)},
        compiler_params=pltpu.CompilerParams(has_side_effects=DATAFLOW),
    )(*operands)


def _gather_chips_wait(started, after, name):
    send_sems, recv_sems = started[0], started[1]
    thru = started[2:-1]
    n = len(thru) // 2

    def body(*refs):
        src, land = refs[:n], refs[n:2 * n]
        send_sems, recv_sems = refs[2 * n], refs[2 * n + 1]
        ix, iy, ic = _mesh_pos()
        for a in range(n):
            for k in (1, 2, 3):
                px, py = _peer_chip(ix, iy, k)
                cp = pltpu.make_async_remote_copy(
                    src_ref=src[a], dst_ref=land[a].at[2 * px + py], send_sem=send_sems.at[3 * a + k - 1],
                    recv_sem=recv_sems.at[3 * a + k - 1], device_id=(px, py, ic), device_id_type=MESH)
                cp.wait_send()
                cp.wait_recv()

    outs = pl.pallas_call(
        body, name=name,
        out_shape=tuple(pltpu.HBM(t.shape, t.dtype) for t in thru),
        in_specs=(HBM,) * (2 * n) + (SEM, SEM, ANY), out_specs=(HBM,) * (2 * n),
        input_output_aliases={a: a for a in range(2 * n)},
        compiler_params=pltpu.CompilerParams(has_side_effects=DATAFLOW),
    )(*thru, send_sems, recv_sems, after)
    return outs[:n], outs[n:]


BIG_KINDS = (("w_in", "col", D_MODEL, IN_COLS), ("w_out", "row", D_MODEL, D_MODEL),
             ("w_up", "col", D_MODEL, 2 * D_FF), ("w_down", "row", D_FF, D_MODEL))


def _piece_shape(how, r, c):
    return (r // 2, c // 4) if how == "col" else (r // 8, c)


def _aligned(start, multiple):
    return start if isinstance(start, int) else pl.multiple_of(start, multiple)


def _piece(ref, how, r, c, chip, half):
    if how == "col":
        return ref.at[pl.ds(_aligned(half * (r // 2), 8), r // 2), pl.ds(_aligned(chip * (c // 4), LANES), c // 4)]
    n = r // 4
    return ref.at[pl.ds(_aligned(chip * n + half * (n // 2), 8), n // 2), :]


def _rs_pair_exchange(grads, name):
    nk = len(BIG_KINDS)
    nl = len(grads[0])
    flat = [grads[ki][l] for ki in range(nk) for l in range(nl)]
    per = nl * 4

    def body(*refs):
        g, land = refs[:nk * nl], refs[nk * nl:nk * nl + nk]
        send_sems, recv_sems = refs[nk * nl + nk:]
        ix, iy, ic = _mesh_pos()
        sibling = (ix, iy, 1 - ic)
        copies = []
        for ki, (_, how, r, c) in enumerate(BIG_KINDS):
            for l in range(nl):
                for j in range(4):
                    sem = ki * per + l * 4 + j
                    rem = pltpu.make_async_remote_copy(
                        src_ref=_piece(g[ki * nl + l], how, r, c, j, 1 - ic), dst_ref=land[ki].at[l, j],
                        send_sem=send_sems.at[sem], recv_sem=recv_sems.at[sem], device_id=sibling, device_id_type=MESH)
                    rem.start()
                    copies.append(rem)
        for rem in copies:
            rem.wait_send()
            rem.wait_recv()

    shapes = [jax.ShapeDtypeStruct((nl, 4) + _piece_shape(how, r, c), F32) for _, how, r, c in BIG_KINDS]
    return pl.pallas_call(
        body, name=name, in_specs=[ANY] * len(flat), out_specs=tuple([ANY] * nk), out_shape=tuple(shapes),
        scratch_shapes=[pltpu.SemaphoreType.DMA((nk * per,))] * 2,
    )(*flat)


def _pair_sum(g, theirs, layer, how, core, name):
    r, c = g.shape
    pr, pc = _piece_shape(how, r, c)
    if how == "col":
        mine_spec = pl.BlockSpec((pr, pc), lambda j, core_ref: (core_ref[0], j))
    else:
        mine_spec = pl.BlockSpec((pr, pc), lambda j, core_ref: (2 * j + core_ref[0], 0))

    def body(core_ref, g_ref, t_ref, o_ref, ob_ref):
        total = g_ref[...] + t_ref[...]
        o_ref[...] = total
        ob_ref[...] = total.astype(BF16)

    out_blk = pl.BlockSpec((None, pr, pc), lambda j, core_ref: (j, 0, 0))
    return pl.pallas_call(
        body, name=name,
        grid_spec=pltpu.PrefetchScalarGridSpec(
            num_scalar_prefetch=1, grid=(4,),
            in_specs=[mine_spec, pl.BlockSpec((None, None, pr, pc), lambda j, core_ref: (layer, j, 0, 0))],
            out_specs=(out_blk, out_blk)),
        out_shape=(jax.ShapeDtypeStruct((4, pr, pc), F32), jax.ShapeDtypeStruct((4, pr, pc), BF16)),
        compiler_params=_params(1))(core, g, theirs)


def _rs_chip_exchange(pair_sums, name):
    nk = len(pair_sums)
    nl = len(pair_sums[0])
    flat = [pair_sums[ki][l] for ki in range(nk) for l in range(nl)]

    def body(*refs):
        src, dst = refs[:nk * nl], refs[nk * nl:nk * nl + nk]
        send_sems, recv_sems = refs[nk * nl + nk:]
        ix, iy, ic = _mesh_pos()
        copies = []
        for ki in range(nk):
            for l in range(nl):
                for k in (1, 2, 3):
                    px, py = _peer_chip(ix, iy, k)
                    sem = (ki * nl + l) * 3 + k - 1
                    rem = pltpu.make_async_remote_copy(
                        src_ref=src[ki * nl + l].at[2 * px + py], dst_ref=dst[ki].at[l, k - 1],
                        send_sem=send_sems.at[sem], recv_sem=recv_sems.at[sem], device_id=(px, py, ic), device_id_type=MESH)
                    rem.start()
                    copies.append(rem)
        for rem in copies:
            rem.wait_send()
            rem.wait_recv()

    return pl.pallas_call(
        body, name=name, in_specs=[ANY] * len(flat), out_specs=tuple([ANY] * nk),
        out_shape=tuple(jax.ShapeDtypeStruct((nl, 3) + pair_sums[ki][0].shape[1:], pair_sums[ki][0].dtype)
                        for ki in range(nk)),
        scratch_shapes=[pltpu.SemaphoreType.DMA((nk * nl * 3,))] * 2,
    )(*flat)


def _rs_chip_start(pieces, name):
    n = len(pieces)

    def body(*refs):
        src, land = refs[:n], refs[n:2 * n]
        send_sems, recv_sems = refs[2 * n], refs[2 * n + 1]
        token = refs[-1]
        ix, iy, ic = _mesh_pos()
        for a in range(n):
            for k in (1, 2, 3):
                px, py = _peer_chip(ix, iy, k)
                pltpu.make_async_remote_copy(
                    src_ref=src[a].at[2 * px + py], dst_ref=land[a].at[k - 1], send_sem=send_sems.at[3 * a + k - 1],
                    recv_sem=recv_sems.at[3 * a + k - 1], device_id=(px, py, ic), device_id_type=MESH).start()
        token[...] = jnp.zeros_like(token)

    land_shape = lambda t: (3,) + t.shape[1:]
    operands = ([pltpu.with_memory_space_constraint(t, pltpu.HBM) for t in pieces]
                + [pltpu.with_memory_space_constraint(lax.empty(land_shape(t), t.dtype), pltpu.HBM) for t in pieces])
    return pl.pallas_call(
        body, name=name,
        out_shape=(pltpu.SemaphoreType.DMA((3 * n,)), pltpu.SemaphoreType.DMA((3 * n,)),
                   *[pltpu.HBM(t.shape, t.dtype) for t in pieces], *[pltpu.HBM(land_shape(t), t.dtype) for t in pieces],
                   jax.ShapeDtypeStruct((8, LANES), F32)),
        in_specs=(HBM,) * (2 * n),
        out_specs=(SEM, SEM) + (HBM,) * (2 * n) + (pl.BlockSpec(memory_space=pltpu.VMEM),),
        input_output_aliases={a: 2 + a for a in range(2 * n)},
        compiler_params=pltpu.CompilerParams(has_side_effects=DATAFLOW),
    )(*operands)


def _rs_chip_wait(started, after, name):
    send_sems, recv_sems = started[0], started[1]
    thru = started[2:-1]
    n = len(thru) // 2

    def body(*refs):
        src, land = refs[:n], refs[n:2 * n]
        send_sems, recv_sems = refs[2 * n], refs[2 * n + 1]
        ix, iy, ic = _mesh_pos()
        for a in range(n):
            for k in (1, 2, 3):
                px, py = _peer_chip(ix, iy, k)
                cp = pltpu.make_async_remote_copy(
                    src_ref=src[a].at[2 * px + py], dst_ref=land[a].at[k - 1], send_sem=send_sems.at[3 * a + k - 1],
                    recv_sem=recv_sems.at[3 * a + k - 1], device_id=(px, py, ic), device_id_type=MESH)
                cp.wait_send()
                cp.wait_recv()

    outs = pl.pallas_call(
        body, name=name,
        out_shape=tuple(pltpu.HBM(t.shape, t.dtype) for t in thru),
        in_specs=(HBM,) * (2 * n) + (SEM, SEM, ANY), out_specs=(HBM,) * (2 * n),
        input_output_aliases={a: a for a in range(2 * n)},
        compiler_params=pltpu.CompilerParams(has_side_effects=DATAFLOW),
    )(*thru, send_sems, recv_sems, after)
    return outs[n:]


def _chip_sum(own, others, chip, name):
    _, pr, pc = own.shape

    def body(chip_ref, own_ref, s1, s2, s3, o_ref):
        o_ref[...] = ((own_ref[...] + s1[...].astype(F32)) + s2[...].astype(F32)) + s3[...].astype(F32)

    slot = lambda k: pl.BlockSpec((None, pr, pc), lambda i, chip_ref: (k, 0, 0))
    return pl.pallas_call(
        body, name=name,
        grid_spec=pltpu.PrefetchScalarGridSpec(
            num_scalar_prefetch=1, grid=(1,),
            in_specs=[pl.BlockSpec((None, pr, pc), lambda i, chip_ref: (chip_ref[0], 0, 0)), slot(0), slot(1), slot(2)],
            out_specs=pl.BlockSpec((pr, pc), lambda i, chip_ref: (0, 0))),
        out_shape=jax.ShapeDtypeStruct((pr, pc), F32), compiler_params=_params(1))(chip, own, others, others, others)


def _rs_pair_share(halves, name):
    nk = len(halves)
    flat = [halves[ki][l] for ki in range(nk) for l in range(DEPTH)]

    def body(*refs):
        src, dst = refs[:nk * DEPTH], refs[nk * DEPTH:nk * DEPTH + nk]
        send_sems, recv_sems = refs[nk * DEPTH + nk:]
        ix, iy, ic = _mesh_pos()
        copies = []
        for ki in range(nk):
            for l in range(DEPTH):
                sem = ki * DEPTH + l
                rem = pltpu.make_async_remote_copy(
                    src_ref=src[sem], dst_ref=dst[ki].at[l], send_sem=send_sems.at[sem], recv_sem=recv_sems.at[sem],
                    device_id=(ix, iy, 1 - ic), device_id_type=MESH)
                rem.start()
                copies.append(rem)
        for rem in copies:
            rem.wait_send()
            rem.wait_recv()

    return pl.pallas_call(
        body, name=name, in_specs=[ANY] * len(flat), out_specs=tuple([ANY] * nk),
        out_shape=tuple(jax.ShapeDtypeStruct((DEPTH,) + halves[ki][0].shape, F32) for ki in range(nk)),
        scratch_shapes=[pltpu.SemaphoreType.DMA((nk * DEPTH,))] * 2,
    )(*flat)


def _adamw_halves(w, mine, theirs, m, v, core, name):
    nl, pr, pc = theirs.shape
    shape = w.shape
    view = lambda t: t.reshape(nl, 2, pr, pc)
    tr = _row_tile(pr, pc, 256 * 1024)

    def body(core_ref, w_ref, a0_ref, a1_ref, t_ref, m_ref, v_ref, g_ref, d_ref, m2_ref, v2_ref):
        own = jnp.where(pl.program_id(0) == 0, a0_ref[...], a1_ref[...])
        g = jnp.where(pl.program_id(1) == core_ref[0], own, t_ref[...])
        g_ref[...] = g
        d_ref[...], m2_ref[...], v2_ref[...] = _adamw_math(w_ref[...], g, m_ref[...], v_ref[...])

    blk = pl.BlockSpec((None, None, tr, pc), lambda l, h, i, core_ref: (l, h, i, 0))
    own_blk = pl.BlockSpec((tr, pc), lambda l, h, i, core_ref: (i, 0))
    out = jax.ShapeDtypeStruct((nl, 2, pr, pc), F32)
    outs = pl.pallas_call(
        body, name=name,
        grid_spec=pltpu.PrefetchScalarGridSpec(
            num_scalar_prefetch=1, grid=(nl, 2, pr // tr),
            in_specs=[blk, own_blk, own_blk, pl.BlockSpec((None, tr, pc), lambda l, h, i, core_ref: (l, i, 0)), blk, blk],
            out_specs=(blk,) * 4),
        out_shape=(out,) * 4, compiler_params=_params(3),
    )(core, view(w), mine[0], mine[1], theirs, view(m), view(v))
    return tuple(t.reshape(shape) for t in outs)


def _rs_first_stages(layer_grads, core, tag, in_flight):
    theirs = _rs_pair_exchange([[g] for g in layer_grads], f"rs_pair_exchange_{tag}")
    sums = [_pair_sum(g, theirs[ki], 0, how, core, f"rs_pair_sum_{kind}")
            for ki, ((kind, how, _, _), g) in enumerate(zip(BIG_KINDS, layer_grads))]
    to_send = [both[1] for both in sums]
    if in_flight:
        return [both[0] for both in sums], _rs_chip_start(to_send, f"rs_chip_start_{tag}")
    slots = _rs_chip_exchange([[t] for t in to_send], f"rs_chip_exchange_{tag}")
    return [both[0] for both in sums], [t[0] for t in slots]


def _rs_last_stages(per_layer, chip):
    halves = [[_chip_sum(per_layer[l][0][ki], per_layer[l][1][ki], chip, f"rs_chip_sum_{kind}") for l in range(DEPTH)]
              for ki, (kind, _, _, _) in enumerate(BIG_KINDS)]
    other = _rs_pair_share(halves, "rs_pair_share")
    return list(zip(halves, other))


WEIGHT_NAMES = ("w_ada", "b_ada", "norm1_w", "w_in", "conv_a_w", "conv_a_b", "ln_a_w", "ln_a_b", "lb_gamma",
                "rec_norm_w", "w_out", "norm2_w", "w_up", "conv_f_w", "w_down", "final_norm_w")
SMALL_PARAMS = (("b_ada", (DEPTH, N_MOD * D_MODEL), None), ("norm1_w", (DEPTH, D_MODEL), None),
                ("conv_a_w", (DEPTH, CONV_WIDTH, CONV_CH), 2), ("conv_a_b", (DEPTH, CONV_CH), None),
                ("ln_a_w", (DEPTH, CONV_CH), None), ("ln_a_b", (DEPTH, CONV_CH), None),
                ("lb_gamma", (DEPTH, 2, REC_WIDTH), 2), ("rec_norm_w", (DEPTH, REC_WIDTH), None),
                ("norm2_w", (DEPTH, D_MODEL), None), ("conv_f_w", (DEPTH, 3, 2 * D_FF), 2),
                ("final_norm_w", (D_MODEL,), None))


def _pack_rows(parts):
    flat = jnp.concatenate([p.reshape(-1) for p in parts])
    total = flat.shape[0]
    padded = -(-total // (8 * LANES)) * (8 * LANES)
    return jnp.pad(flat, (0, padded - total)).reshape(padded // LANES, LANES)


def _unpack(flat, shapes):
    out, off = [], 0
    for shp in shapes:
        size = int(np.prod(shp))
        out.append(flat[off:off + size].reshape(shp))
        off += size
    return out


def _unstack_chips(t, axis):
    return jnp.concatenate([t[j] for j in range(4)], axis=axis)


def kernel(x, c, w_ada, b_ada, norm1_w, w_in, conv_a_w, conv_a_b, ln_a_w, ln_a_b, lb_gamma, rec_norm_w, w_out, norm2_w, w_up, conv_f_w, w_down, final_norm_w, loss_target, m_w_ada, m_b_ada, m_norm1_w, m_w_in, m_conv_a_w, m_conv_a_b, m_ln_a_w, m_ln_a_b, m_lb_gamma, m_rec_norm_w, m_w_out, m_norm2_w, m_w_up, m_conv_f_w, m_w_down, m_final_norm_w, v_w_ada, v_b_ada, v_norm1_w, v_w_in, v_conv_a_w, v_conv_a_b, v_ln_a_w, v_ln_a_b, v_lb_gamma, v_rec_norm_w, v_w_out, v_norm2_w, v_w_up, v_conv_f_w, v_w_down, v_final_norm_w):
    params = dict(zip(WEIGHT_NAMES, (w_ada, b_ada, norm1_w, w_in, conv_a_w, conv_a_b, ln_a_w, ln_a_b, lb_gamma,
                                     rec_norm_w, w_out, norm2_w, w_up, conv_f_w, w_down, final_norm_w)))
    mom1 = dict(zip(WEIGHT_NAMES, (m_w_ada, m_b_ada, m_norm1_w, m_w_in, m_conv_a_w, m_conv_a_b, m_ln_a_w, m_ln_a_b,
                                   m_lb_gamma, m_rec_norm_w, m_w_out, m_norm2_w, m_w_up, m_conv_f_w, m_w_down,
                                   m_final_norm_w)))
    mom2 = dict(zip(WEIGHT_NAMES, (v_w_ada, v_b_ada, v_norm1_w, v_w_in, v_conv_a_w, v_conv_a_b, v_ln_a_w, v_ln_a_b,
                                   v_lb_gamma, v_rec_norm_w, v_w_out, v_norm2_w, v_w_up, v_conv_f_w, v_w_down,
                                   v_final_norm_w)))
    ix, iy, ic = _mesh_pos()
    chip = 2 * ix + iy
    dev = 2 * chip + ic

    c_all = _allgather_devices(c.reshape(8, LANES), "gather_cond").reshape(8, D_MODEL)
    b_sh = lax.dynamic_slice_in_dim(b_ada, chip * ADA_SHARD, ADA_SHARD, axis=1)
    mod_sh = _ada_mod(c_all, w_ada, b_sh.reshape(DEPTH, 1, ADA_SHARD), "ada_mod")
    w_in_b, w_out_b, w_up_b, w_down_b = (t.astype(BF16) for t in (w_in, w_out, w_up, w_down))
    first = _gather_chips([mod_sh, conv_a_w, conv_f_w, lb_gamma, w_in_b[0]], "gather_first")
    later = [w_in_b[1], w_out_b, w_up_b, w_down_b]
    started = _gather_chips_start(later, "gather_rest_start")
    mod_mine = lax.dynamic_index_in_dim(first[0], dev, axis=2, keepdims=False) + started[-1][0, 0]
    mods = [jnp.concatenate([mod_mine[j, l] for j in range(4)]).reshape(N_MOD, D_MODEL) for l in range(DEPTH)]
    conv_a_w_f, conv_f_w_f, gamma_f = (_unstack_chips(first[k], 2) for k in (1, 2, 3))
    w_in0 = _unstack_chips(first[4], 1)

    def later_weights(after):
        own, lands = _gather_chips_wait(started, after, "gather_rest_wait")
        full = [lax.dynamic_update_index_in_dim(land, mine, chip, 0) for land, mine in zip(lands, own)]
        return (_unstack_chips(full[0], 1), _unstack_chips(full[1], 1), _unstack_chips(full[2], 2),
                _unstack_chips(full[3], 1))

    lb1, p_soft = _lower_bounds(gamma_f.reshape(DEPTH, 2 * REC_WIDTH), "lower_bounds")
    lbs = [jnp.zeros((2, REC_WIDTH), F32), lb1.reshape(2, REC_WIDTH)]
    small = []
    for l in range(DEPTH):
        small.append(dict(norm1_w=norm1_w[l][None], conv_a_w=conv_a_w_f[l], conv_a_b=conv_a_b[l][None],
                          ln_a_w=ln_a_w[l][None], ln_a_b=ln_a_b[l][None], rec_norm_w=rec_norm_w[l],
                          norm2_w=norm2_w[l][None], conv_f_w=conv_f_w_f[l]))

    core_id, chip_id = ic.astype(jnp.int32).reshape(1), chip.astype(jnp.int32).reshape(1)
    reduce_state = [None] * DEPTH

    def on_layer_grads(l, layer_grads):
        in_flight = l > 0
        reduce_state[l] = _rs_first_stages(layer_grads, core_id, f"l{l}", in_flight)
        return reduce_state[l][1][-1][0:1, 0:1] if in_flight else None

    loss, dx, grads, dfw = _sequence_step(x[0], loss_target[0], mods, lbs, small, w_in0, later_weights,
                                          final_norm_w[None], on_layer_grads)
    loss = lax.psum(loss, ("x", "y", "c"))
    for l in range(1, DEPTH):
        sums, started = reduce_state[l]
        reduce_state[l] = (sums, _rs_chip_wait(started, dx, f"rs_chip_wait_l{l}"))

    dgamma = _lower_bounds_bwd(grads[1]["lb"].reshape(1, 2 * REC_WIDTH), p_soft, "lower_bounds_bwd")
    dmod = [jnp.concatenate(grads[l]["mod"], axis=1) for l in range(DEPTH)]
    stack = lambda key: jnp.stack([grads[l][key] for l in range(DEPTH)])
    local_small = dict(b_ada=jnp.concatenate(dmod, axis=0), norm1_w=stack("norm1_w"), conv_a_w=stack("conv_a_w"),
                       conv_a_b=stack("conv_a_b"), ln_a_w=stack("ln_a_w"), ln_a_b=stack("ln_a_b"), lb_gamma=dgamma,
                       rec_norm_w=stack("rec_norm_w"), norm2_w=stack("norm2_w"), conv_f_w=stack("conv_f_w"),
                       final_norm_w=dfw)
    pack = _pack_rows([local_small[name] for name, _, _ in SMALL_PARAMS])
    rows = pack.shape[0]
    packs = _allgather_devices(pack, "gather_small_grads").reshape(8, rows, LANES)
    summed = _sum_devices(packs, "sum_small_grads").reshape(-1)
    small_grads = dict(zip([n for n, _, _ in SMALL_PARAMS], _unpack(summed, [shp for _, shp, _ in SMALL_PARAMS])))

    dmod_all = packs.reshape(8, rows * LANES)[:, :DEPTH * N_MOD * D_MODEL].reshape(8, DEPTH, N_MOD * D_MODEL)
    dmod_sh = lax.dynamic_slice_in_dim(dmod_all, chip * ADA_SHARD, ADA_SHARD, axis=2).transpose(1, 0, 2)
    g_ada, d_ada, m_ada, v_ada = _ada_update(c_all, dmod_sh, w_ada, m_w_ada, v_w_ada, "ada_update")

    for name, shp, axis in SMALL_PARAMS:
        if axis is not None:
            width = shp[axis] // 4
            small_grads[name] = lax.dynamic_slice_in_dim(small_grads[name], chip * width, width, axis=axis)
    names = [n for n, _, _ in SMALL_PARAMS]
    packed = [_pack_rows([src[n] for n in names])[None] for src in (params, small_grads, mom1, mom2)]
    small_out = _adamw(*packed, "adamw_small")
    shapes = [params[n].shape for n in names]
    small_delta, small_m, small_v = (dict(zip(names, _unpack(t.reshape(-1), shapes))) for t in small_out)

    summed_big = _rs_last_stages(reduce_state, chip_id)
    grad, delta, new_m, new_v = dict(small_grads), small_delta, small_m, small_v
    grad["w_ada"], delta["w_ada"], new_m["w_ada"], new_v["w_ada"] = g_ada, d_ada, m_ada, v_ada
    for (name, _, _, _), (mine, theirs) in zip(BIG_KINDS, summed_big):
        grad[name], delta[name], new_m[name], new_v[name] = _adamw_halves(
            params[name], mine, theirs, mom1[name], mom2[name], core_id, f"adamw_{name}")

    return (loss, dx[None], *[grad[n] for n in WEIGHT_NAMES], *[delta[n] for n in WEIGHT_NAMES],
            *[new_m[n] for n in WEIGHT_NAMES], *[new_v[n] for n in WEIGHT_NAMES])
```

```python
import numpy as np
import jax
import jax.numpy as jnp
from jax import lax
from jax.experimental import pallas as pl
from jax.experimental.pallas import tpu as pltpu

F32 = jnp.float32
BF16 = jnp.bfloat16

D_MODEL = 1024
DEPTH = 2
HEAD_DIM = 64
CONV_CH = 256
CONV_WIDTH = 31
ATT_WIDTH = 384
N_HEADS = 6
DILATIONS = (1, 4, 16)
ATT_HALF = 64
ATT_BLOCK = 128
ALIBI_MAX_EXP = 8.0
MASK_VALUE = -1e30
REC_WIDTH = 384
REC_CHUNK = 64
F_TINY = 1e-30
D_FF = 2816
N_MOD = 6
EPS = 1e-6
G_CONV = (0, 512)
G_QKV = (512, 1664)
G_REC = (1664, 3584)
IN_COLS = 3584

ADAM_LR = 0.001
ADAM_B1 = 0.9
ADAM_B2 = 0.999
ADAM_EPS = 1e-08
ADAM_WD = 0.01
ADAM_STEP = 10

VMEM_LIMIT_BYTES = 56 * 1024 * 1024
LANES = 128
MESH = pl.DeviceIdType.MESH
ANY = pl.BlockSpec(memory_space=pl.ANY)


def _params(n_axes):
    return pltpu.CompilerParams(dimension_semantics=("arbitrary",) * n_axes,
                                vmem_limit_bytes=VMEM_LIMIT_BYTES)


def _tile(n, target):
    best = None
    for t in range(LANES, min(n, target) + 1, LANES):
        if n % t == 0:
            best = t
    return best or n


def _sigmoid(x):
    return jax.nn.sigmoid(x)


def _silu_grad(x):
    s = _sigmoid(x)
    return s * (1.0 + x * (1.0 - s))


MM_ACC_ELEMS = 1536 * 1024


def _matmul(a, b, mode, out_dtype, name, tm=1024, tn=1792, tk=1792):
    if mode == "nn":
        (m, k), (k2, n) = a.shape, b.shape
    elif mode == "nt":
        (m, k), (n, k2) = a.shape, b.shape
    else:
        (k, m), (k2, n) = a.shape, b.shape
    assert k == k2, (a.shape, b.shape, mode)
    tn, tk = _tile(n, tn), _tile(k, tk)
    tm = _tile(m, min(tm, MM_ACC_ELEMS // tn))
    nk = k // tk
    a_spec = (pl.BlockSpec((tk, tm), lambda i, j, kk: (kk, i)) if mode == "tn"
              else pl.BlockSpec((tm, tk), lambda i, j, kk: (i, kk)))
    b_spec = (pl.BlockSpec((tn, tk), lambda i, j, kk: (j, kk)) if mode == "nt"
              else pl.BlockSpec((tk, tn), lambda i, j, kk: (kk, j)))
    dims = {"nn": (((1,), (0,)), ((), ())), "nt": (((1,), (1,)), ((), ())),
            "tn": (((0,), (0,)), ((), ()))}[mode]

    def body(a_ref, b_ref, o_ref, *scratch):
        part = lax.dot_general(a_ref[...].astype(BF16), b_ref[...].astype(BF16), dims, preferred_element_type=F32)
        if nk == 1:
            o_ref[...] = part.astype(out_dtype)
            return
        acc_ref, = scratch
        kk = pl.program_id(2)

        @pl.when(kk == 0)
        def _():
            acc_ref[...] = part

        @pl.when(kk > 0)
        def _():
            acc_ref[...] += part

        @pl.when(kk == nk - 1)
        def _():
            o_ref[...] = acc_ref[...].astype(out_dtype)

    return pl.pallas_call(
        body, name=name, grid=(m // tm, n // tn, nk),
        in_specs=[a_spec, b_spec],
        out_specs=pl.BlockSpec((tm, tn), lambda i, j, kk: (i, j)),
        out_shape=jax.ShapeDtypeStruct((m, n), out_dtype),
        scratch_shapes=[pltpu.VMEM((tm, tn), F32)] if nk > 1 else [],
        compiler_params=pltpu.CompilerParams(dimension_semantics=("parallel", "parallel", "arbitrary"),
                                             vmem_limit_bytes=VMEM_LIMIT_BYTES),
    )(a, b)


NORM_ROWS = 256


def _row_spec(width, rows=NORM_ROWS):
    return pl.BlockSpec((rows, width), lambda i: (i, 0))


def _vec_spec(width):
    return pl.BlockSpec((1, width), lambda i: (0, 0))


def _resid_norm_mod(x, r, g, nw, sc, sh, name):
    s, d = x.shape
    has_r = r is not None

    def body(*refs):
        if has_r:
            x_ref, r_ref, g_ref, nw_ref, sc_ref, sh_ref, xn_ref, h_ref = refs
            xn = x_ref[...] + g_ref[...] * r_ref[...].astype(F32)
            xn_ref[...] = xn
        else:
            x_ref, nw_ref, sc_ref, sh_ref, h_ref = refs
            xn = x_ref[...]
        rstd = lax.rsqrt(jnp.mean(xn * xn, axis=-1, keepdims=True) + EPS)
        y = xn * rstd * nw_ref[...]
        h_ref[...] = (y * (1.0 + sc_ref[...]) + sh_ref[...]).astype(BF16)

    if has_r:
        ins, in_specs = (x, r, g, nw, sc, sh), [_row_spec(d), _row_spec(d)] + [_vec_spec(d)] * 4
        out_shape = (jax.ShapeDtypeStruct((s, d), F32), jax.ShapeDtypeStruct((s, d), BF16))
        out_specs = (_row_spec(d), _row_spec(d))
    else:
        ins, in_specs = (x, nw, sc, sh), [_row_spec(d)] + [_vec_spec(d)] * 3
        out_shape = jax.ShapeDtypeStruct((s, d), BF16)
        out_specs = _row_spec(d)
    return pl.pallas_call(body, name=name, grid=(s // NORM_ROWS,), in_specs=in_specs, out_specs=out_specs,
                          out_shape=out_shape, compiler_params=_params(1))(*ins)


def _final_loss(x, r, g, fw, tgt, name):
    s, d = x.shape

    def body(x_ref, r_ref, g_ref, fw_ref, t_ref, loss_ref, dx_ref, dr_ref, dg_ref, dfw_ref):
        @pl.when(pl.program_id(0) == 0)
        def _():
            loss_ref[...] = jnp.zeros_like(loss_ref)
            dg_ref[...] = jnp.zeros_like(dg_ref)
            dfw_ref[...] = jnp.zeros_like(dfw_ref)

        rr = r_ref[...].astype(F32)
        gg = g_ref[...]
        xn = x_ref[...] + gg * rr
        rstd = lax.rsqrt(jnp.mean(xn * xn, axis=-1, keepdims=True) + EPS)
        xh = xn * rstd
        fwv = fw_ref[...]
        e = xh * fwv - t_ref[...]
        loss_ref[...] += 0.5 * jnp.sum(jnp.mean(e * e, axis=-1, keepdims=True), axis=0, keepdims=True)
        dy = e * (1.0 / d)
        dfw_ref[...] += jnp.sum(dy * xh, axis=0, keepdims=True)
        dxh = dy * fwv
        dx = rstd * (dxh - xh * jnp.mean(dxh * xh, axis=-1, keepdims=True))
        dx_ref[...] = dx
        dr_ref[...] = (gg * dx).astype(BF16)
        dg_ref[...] += jnp.sum(dx * rr, axis=0, keepdims=True)

    return pl.pallas_call(
        body, name=name, grid=(s // NORM_ROWS,),
        in_specs=[_row_spec(d), _row_spec(d), _vec_spec(d), _vec_spec(d), _row_spec(d)],
        out_specs=(_vec_spec(LANES), _row_spec(d), _row_spec(d), _vec_spec(d), _vec_spec(d)),
        out_shape=(jax.ShapeDtypeStruct((1, LANES), F32), jax.ShapeDtypeStruct((s, d), F32),
                   jax.ShapeDtypeStruct((s, d), BF16), jax.ShapeDtypeStruct((1, d), F32),
                   jax.ShapeDtypeStruct((1, d), F32)),
        compiler_params=_params(1))(x, r, g, fw, tgt)


def _norm_bwd(x, dhs, dxres, nw, sc, g, r, name):
    s, d = x.shape
    n_dh = len(dhs)
    has_g = g is not None

    def body(*refs):
        x_ref = refs[0]
        dh_refs = refs[1:1 + n_dh]
        dxres_ref, nw_ref, sc_ref = refs[1 + n_dh:4 + n_dh]
        pos = 4 + n_dh
        if has_g:
            g_ref, r_ref = refs[pos:pos + 2]
            pos += 2
            dx_ref, dr_ref, dsh_ref, dsc_ref, dnw_ref, dg_ref = refs[pos:]
            accs = (dsh_ref, dsc_ref, dnw_ref, dg_ref)
        else:
            dx_ref, dsh_ref, dsc_ref, dnw_ref = refs[pos:]
            accs = (dsh_ref, dsc_ref, dnw_ref)

        @pl.when(pl.program_id(0) == 0)
        def _():
            for acc in accs:
                acc[...] = jnp.zeros_like(acc)

        xv = x_ref[...]
        dh = dh_refs[0][...].astype(F32)
        for extra in dh_refs[1:]:
            dh = dh + extra[...].astype(F32)
        rstd = lax.rsqrt(jnp.mean(xv * xv, axis=-1, keepdims=True) + EPS)
        xh = xv * rstd
        nwv = nw_ref[...]
        dsh_ref[...] += jnp.sum(dh, axis=0, keepdims=True)
        dsc_ref[...] += jnp.sum(dh * (xh * nwv), axis=0, keepdims=True)
        dy = dh * (1.0 + sc_ref[...])
        dnw_ref[...] += jnp.sum(dy * xh, axis=0, keepdims=True)
        dxh = dy * nwv
        dx = dxres_ref[...] + rstd * (dxh - xh * jnp.mean(dxh * xh, axis=-1, keepdims=True))
        dx_ref[...] = dx
        if has_g:
            dr_ref[...] = (g_ref[...] * dx).astype(BF16)
            dg_ref[...] += jnp.sum(dx * r_ref[...].astype(F32), axis=0, keepdims=True)

    ins = [x, *dhs, dxres, nw, sc]
    in_specs = [_row_spec(d)] * (2 + n_dh) + [_vec_spec(d)] * 2
    out_shape = [jax.ShapeDtypeStruct((s, d), F32)]
    out_specs = [_row_spec(d)]
    if has_g:
        ins += [g, r]
        in_specs += [_vec_spec(d), _row_spec(d)]
        out_shape.append(jax.ShapeDtypeStruct((s, d), BF16))
        out_specs.append(_row_spec(d))
    n_vec = 4 if has_g else 3
    out_shape += [jax.ShapeDtypeStruct((1, d), F32)] * n_vec
    out_specs += [_vec_spec(d)] * n_vec
    return pl.pallas_call(body, name=name, grid=(s // NORM_ROWS,), in_specs=in_specs, out_specs=tuple(out_specs),
                          out_shape=tuple(out_shape), compiler_params=_params(1))(*ins)


FFN_ROWS = 256
FFN_COLS = 1408
HALO = 16
INV_SQRT2 = 0.7071067811865476
INV_SQRT_2PI = 0.3989422804014327


def _gelu(x):
    return 0.5 * x * (1.0 + lax.erf(x * INV_SQRT2))


def _gelu_grad(x):
    return 0.5 * (1.0 + lax.erf(x * INV_SQRT2)) + x * (INV_SQRT_2PI * jnp.exp(-0.5 * x * x))


def _halo_specs(rows, cols, halo, n_rows_total, col_of):
    per = rows // halo
    last = n_rows_total // halo - 1
    cur = pl.BlockSpec((rows, cols), lambda j, i: (i, col_of(j)))
    prev = pl.BlockSpec((halo, cols), lambda j, i: (jnp.maximum(i * per - 1, 0), col_of(j)))
    nxt = pl.BlockSpec((halo, cols), lambda j, i: (jnp.minimum((i + 1) * per, last), col_of(j)))
    return [prev, cur, nxt]


def _shift_rows(x, k):
    n = x.shape[0]
    return pltpu.roll(x, k % n, axis=0)


def _conv3(ext, w):
    return w[0:1, :] * _shift_rows(ext, 1) + w[1:2, :] * ext + w[2:3, :] * _shift_rows(ext, -1)


def _ext_block(prev_ref, cur_ref, next_ref, i, n_i):
    prev = jnp.where(i > 0, prev_ref[...].astype(F32), 0.0)
    nxt = jnp.where(i < n_i - 1, next_ref[...].astype(F32), 0.0)
    return jnp.concatenate([prev, cur_ref[...].astype(F32), nxt], axis=0)


def _ffn_act(u, cw, name):
    s = u.shape[0]
    nc, ns = D_FF // FFN_COLS, s // FFN_ROWS

    def body(gp, gc, gn, vp, vc, vn, wg_ref, wv_ref, o_ref, cg_ref, cv_ref):
        i = pl.program_id(1)
        cg = _conv3(_ext_block(gp, gc, gn, i, ns), wg_ref[...])[HALO:HALO + FFN_ROWS]
        cv = _conv3(_ext_block(vp, vc, vn, i, ns), wv_ref[...])[HALO:HALO + FFN_ROWS]
        o_ref[...] = (_gelu(cg) * cv).astype(BF16)
        cg_ref[...] = cg.astype(BF16)
        cv_ref[...] = cv.astype(BF16)

    in_specs = (_halo_specs(FFN_ROWS, FFN_COLS, HALO, s, lambda j: j)
                + _halo_specs(FFN_ROWS, FFN_COLS, HALO, s, lambda j: j + nc)
                + [pl.BlockSpec((3, FFN_COLS), lambda j, i: (0, j)),
                   pl.BlockSpec((3, FFN_COLS), lambda j, i: (0, j + nc))])
    blk = pl.BlockSpec((FFN_ROWS, FFN_COLS), lambda j, i: (i, j))
    return pl.pallas_call(
        body, name=name, grid=(nc, ns), in_specs=in_specs, out_specs=(blk, blk, blk),
        out_shape=(jax.ShapeDtypeStruct((s, D_FF), BF16),) * 3, compiler_params=_params(2),
    )(u, u, u, u, u, u, cw, cw)


def _ffn_act_bwd(u, cg, cv, dact, cw, name):
    s = u.shape[0]
    nc, ns = D_FF // FFN_COLS, s // FFN_ROWS

    def body(ug_ref, uv_ref, gp, gc, gn, vp, vc, vn, dp, dc, dn, wg_ref, wv_ref, dug_ref, duv_ref, dwg_ref, dwv_ref):
        i = pl.program_id(1)

        @pl.when(i == 0)
        def _():
            dwg_ref[...] = jnp.zeros_like(dwg_ref)
            dwv_ref[...] = jnp.zeros_like(dwv_ref)

        cge = _ext_block(gp, gc, gn, i, ns)
        cve = _ext_block(vp, vc, vn, i, ns)
        da = _ext_block(dp, dc, dn, i, ns)
        dcg = da * cve * _gelu_grad(cge)
        dcv = da * _gelu(cge)
        inner = slice(HALO, HALO + FFN_ROWS)
        for d_c, u_ref, w_ref, du_ref, dw_ref in ((dcg, ug_ref, wg_ref, dug_ref, dwg_ref),
                                                  (dcv, uv_ref, wv_ref, duv_ref, dwv_ref)):
            w = w_ref[...]
            d_next, d_prev = _shift_rows(d_c, -1), _shift_rows(d_c, 1)
            du = w[0:1, :] * d_next + w[1:2, :] * d_c + w[2:3, :] * d_prev
            du_ref[...] = du[inner].astype(BF16)
            u_in = u_ref[...].astype(F32)
            for tap, d_tap in enumerate((d_next, d_c, d_prev)):
                dw_ref[tap:tap + 1, :] += jnp.sum(d_tap[inner] * u_in, axis=0, keepdims=True)

    blk = pl.BlockSpec((FFN_ROWS, FFN_COLS), lambda j, i: (i, j))
    in_specs = ([blk, pl.BlockSpec((FFN_ROWS, FFN_COLS), lambda j, i: (i, j + nc))]
                + _halo_specs(FFN_ROWS, FFN_COLS, HALO, s, lambda j: j) * 3
                + [pl.BlockSpec((3, FFN_COLS), lambda j, i: (0, j)),
                   pl.BlockSpec((3, FFN_COLS), lambda j, i: (0, j + nc))])
    acc = pl.BlockSpec((HALO, FFN_COLS), lambda j, i: (0, j))
    return pl.pallas_call(
        body, name=name, grid=(nc, ns), in_specs=in_specs, out_specs=(blk, blk, acc, acc),
        out_shape=(jax.ShapeDtypeStruct((s, D_FF), BF16), jax.ShapeDtypeStruct((s, D_FF), BF16),
                   jax.ShapeDtypeStruct((HALO, D_FF), F32), jax.ShapeDtypeStruct((HALO, D_FF), F32)),
        compiler_params=_params(2),
    )(u, u, cg, cg, cg, cv, cv, cv, dact, dact, dact, cw, cw)


CONV_ROWS = 512
CONV_HALO = 16
CONV_PAD = CONV_WIDTH // 2


def _conv_halo_specs(cols, s):
    per = CONV_ROWS // CONV_HALO
    last = s // CONV_HALO - 1
    return [pl.BlockSpec((CONV_HALO, cols), lambda i: (jnp.maximum(i * per - 1, 0), 0)),
            pl.BlockSpec((CONV_ROWS, cols), lambda i: (i, 0)),
            pl.BlockSpec((CONV_HALO, cols), lambda i: (jnp.minimum((i + 1) * per, last), 0))]


def _glu_ext(pp, pc, pn, i, n_i):
    ext = _ext_block(pp, pc, pn, i, n_i)
    return ext[:, :CONV_CH] * _sigmoid(ext[:, CONV_CH:])


def _conv_mixer(pa, cw, cb, lnw, lnb, name):
    s = pa.shape[0]
    ns = s // CONV_ROWS

    def body(pp, pc, pn, cw_ref, cb_ref, lnw_ref, lnb_ref, o_ref, c_ref):
        i = pl.program_id(0)
        a = _glu_ext(pp, pc, pn, i, ns)
        acc = jnp.zeros((CONV_ROWS, CONV_CH), F32)
        for tap in range(CONV_WIDTH):
            acc = acc + cw_ref[tap:tap + 1, :] * _shift_rows(a, -(tap + 1))[:CONV_ROWS]
        cv = acc + cb_ref[...]
        c_ref[...] = cv
        mu = jnp.mean(cv, axis=-1, keepdims=True)
        xc = cv - mu
        rstd = lax.rsqrt(jnp.mean(xc * xc, axis=-1, keepdims=True) + EPS)
        y = xc * rstd * lnw_ref[...] + lnb_ref[...]
        o_ref[...] = (y * _sigmoid(y)).astype(BF16)

    vec = pl.BlockSpec((1, CONV_CH), lambda i: (0, 0))
    blk = pl.BlockSpec((CONV_ROWS, CONV_CH), lambda i: (i, 0))
    return pl.pallas_call(
        body, name=name, grid=(ns,),
        in_specs=_conv_halo_specs(2 * CONV_CH, s) + [pl.BlockSpec((CONV_WIDTH, CONV_CH), lambda i: (0, 0)), vec, vec, vec],
        out_specs=(blk, blk),
        out_shape=(jax.ShapeDtypeStruct((s, CONV_CH), BF16), jax.ShapeDtypeStruct((s, CONV_CH), F32)),
        compiler_params=_params(1))(pa, pa, pa, cw, cb, lnw, lnb)


def _conv_mixer_bwd_ln(cv, dout, lnw, lnb, name):
    s = cv.shape[0]

    def body(c_ref, do_ref, lnw_ref, lnb_ref, dc_ref, dlnw_ref, dlnb_ref, dcb_ref):
        @pl.when(pl.program_id(0) == 0)
        def _():
            dlnw_ref[...] = jnp.zeros_like(dlnw_ref)
            dlnb_ref[...] = jnp.zeros_like(dlnb_ref)
            dcb_ref[...] = jnp.zeros_like(dcb_ref)

        c = c_ref[...]
        mu = jnp.mean(c, axis=-1, keepdims=True)
        xc = c - mu
        rstd = lax.rsqrt(jnp.mean(xc * xc, axis=-1, keepdims=True) + EPS)
        xh = xc * rstd
        w = lnw_ref[...]
        y = xh * w + lnb_ref[...]
        dy = do_ref[...] * _silu_grad(y)
        dlnw_ref[...] += jnp.sum(dy * xh, axis=0, keepdims=True)
        dlnb_ref[...] += jnp.sum(dy, axis=0, keepdims=True)
        dxh = dy * w
        dc = rstd * (dxh - jnp.mean(dxh, axis=-1, keepdims=True) - xh * jnp.mean(dxh * xh, axis=-1, keepdims=True))
        dc_ref[...] = dc
        dcb_ref[...] += jnp.sum(dc, axis=0, keepdims=True)

    vec = pl.BlockSpec((1, CONV_CH), lambda i: (0, 0))
    blk = pl.BlockSpec((CONV_ROWS, CONV_CH), lambda i: (i, 0))
    return pl.pallas_call(
        body, name=name, grid=(s // CONV_ROWS,), in_specs=[blk, blk, vec, vec], out_specs=(blk, vec, vec, vec),
        out_shape=(jax.ShapeDtypeStruct((s, CONV_CH), F32),) + (jax.ShapeDtypeStruct((1, CONV_CH), F32),) * 3,
        compiler_params=_params(1))(cv, dout, lnw, lnb)


def _conv_mixer_bwd_conv(pa, dc, cw, name):
    s = pa.shape[0]
    ns = s // CONV_ROWS

    def body(pc, dp, dcc, dn, cw_ref, dpa_ref, dcw_ref):
        i = pl.program_id(0)

        @pl.when(i == 0)
        def _():
            dcw_ref[...] = jnp.zeros_like(dcw_ref)

        cur = pc[...]
        val, sg = cur[:, :CONV_CH], _sigmoid(cur[:, CONV_CH:])
        a_cur = val * sg
        dce = _ext_block(dp, dcc, dn, i, ns)
        da = jnp.zeros((CONV_ROWS, CONV_CH), F32)
        for tap in range(CONV_WIDTH):
            shifted = _shift_rows(dce, -(CONV_WIDTH - tap))[:CONV_ROWS]
            da = da + cw_ref[tap:tap + 1, :] * shifted
            dcw_ref[tap:tap + 1, :] += jnp.sum(shifted * a_cur, axis=0, keepdims=True)
        dpa_ref[:, :CONV_CH] = (da * sg).astype(BF16)
        dpa_ref[:, CONV_CH:] = (da * val * sg * (1.0 - sg)).astype(BF16)

    return pl.pallas_call(
        body, name=name, grid=(ns,),
        in_specs=[pl.BlockSpec((CONV_ROWS, 2 * CONV_CH), lambda i: (i, 0))] + _conv_halo_specs(CONV_CH, s)
        + [pl.BlockSpec((CONV_WIDTH, CONV_CH), lambda i: (0, 0))],
        out_specs=(pl.BlockSpec((CONV_ROWS, 2 * CONV_CH), lambda i: (i, 0)),
                   pl.BlockSpec((32, CONV_CH), lambda i: (0, 0))),
        out_shape=(jax.ShapeDtypeStruct((s, 2 * CONV_CH), BF16), jax.ShapeDtypeStruct((32, CONV_CH), F32)),
        compiler_params=_params(1))(pa, dc, dc, dc, cw)


SLOPES = tuple(float(2.0 ** (-ALIBI_MAX_EXP * (h + 1) / N_HEADS)) for h in range(N_HEADS))
ATT_SCALE = HEAD_DIM ** -0.5


PAIR = 2 * HEAD_DIM
N_PAIRS = N_HEADS // 2
ATT_WIN = ATT_BLOCK + 2 * ATT_HALF


ATT_GROUPS = {1: 4, 4: 1, 16: 1}


def _window_specs(dil, n_steps, col_of):
    per = 2 * ATT_GROUPS[dil]
    rows, halo = ATT_BLOCK * dil * ATT_GROUPS[dil], ATT_HALF * dil
    return [pl.BlockSpec((halo, PAIR), lambda i, p: (jnp.maximum(per * i - 1, 0), col_of(p))),
            pl.BlockSpec((rows, PAIR), lambda i, p: (i, col_of(p))),
            pl.BlockSpec((halo, PAIR), lambda i, p: (jnp.minimum(per * (i + 1), per * n_steps - 1), col_of(p)))]


def _residue(ref, r, n, dil, start=0):
    return ref[pl.ds(start * dil + r, n, stride=dil), :] if dil > 1 else ref[pl.ds(start + r, n), :]


def _store_residue(ref, r, dil, start, val):
    if dil > 1:
        ref[pl.ds(start * dil + r, val.shape[0], stride=dil), :] = val
    else:
        ref[pl.ds(start + r, val.shape[0]), :] = val


def _residue_window(refs, r, dil, g=0):
    prev, cur, nxt = refs
    groups = ATT_GROUPS[dil]
    lo = max(g * ATT_BLOCK - ATT_HALF, 0)
    hi = min((g + 1) * ATT_BLOCK + ATT_HALF, groups * ATT_BLOCK)
    parts = [_residue(prev, r, ATT_HALF, dil)] if g == 0 else []
    parts.append(_residue(cur, r, hi - lo, dil, lo))
    if g == groups - 1:
        parts.append(_residue(nxt, r, ATT_HALF, dil))
    return jnp.concatenate(parts, axis=0)


def _band_masks(i, length, dil, transposed):
    shape = (ATT_WIN, ATT_BLOCK) if transposed else (ATT_BLOCK, ATT_WIN)
    row = lax.broadcasted_iota(jnp.int32, shape, 0)
    col = lax.broadcasted_iota(jnp.int32, shape, 1)
    wide = row if transposed else col
    dist = jnp.abs((row - col - ATT_HALF) if transposed else (row + ATT_HALF - col))
    wpos = i * ATT_BLOCK - ATT_HALF + wide
    valid = (dist <= ATT_HALF) & (wpos >= 0) & (wpos < length)
    return valid, dist.astype(F32) * float(dil)


def _attn_branch(qkv, dil, name):
    s = qkv.shape[0]
    groups = ATT_GROUPS[dil]
    rows = ATT_BLOCK * dil * groups
    n_steps = s // rows
    length = s // dil
    nt = (((1,), (1,)), ((), ()))

    def body(q_ref, kp, kc, kn, vp, vc, vn, o_ref, l_ref):
        i, pair = pl.program_id(0), pl.program_id(1)
        items = [(g, r) for g in range(groups) for r in range(dil)]
        q = jnp.stack([_residue(q_ref, r, ATT_BLOCK, dil, g * ATT_BLOCK) for g, r in items]).astype(BF16)
        k = jnp.stack([_residue_window((kp, kc, kn), r, dil, g) for g, r in items]).astype(BF16)
        v = jnp.stack([_residue_window((vp, vc, vn), r, dil, g) for g, r in items]).astype(BF16)
        per_group = [_band_masks(i * groups + g, length, dil, False) for g in range(groups)]
        valid = jnp.stack([per_group[g][0] for g, _ in items]) if groups > 1 else per_group[0][0][None]
        distf = jnp.stack([per_group[g][1] for g, _ in items]) if groups > 1 else per_group[0][1][None]
        outs, lses = [], []
        for hh in range(2):
            sl = slice(hh * HEAD_DIM, (hh + 1) * HEAD_DIM)
            slope = jnp.where(pair == 0, SLOPES[hh], jnp.where(pair == 1, SLOPES[2 + hh], SLOPES[4 + hh]))
            sc = jnp.einsum("bqd,bkd->bqk", q[:, :, sl], k[:, :, sl], preferred_element_type=F32) * ATT_SCALE
            sc = jnp.where(valid, sc - slope * distf, MASK_VALUE)
            m = jnp.max(sc, axis=-1, keepdims=True)
            p = jnp.exp(sc - m)
            den = jnp.sum(p, axis=-1, keepdims=True)
            outs.append(jnp.einsum("bqk,bkd->bqd", p.astype(BF16), v[:, :, sl], preferred_element_type=F32) / den)
            lses.append(jnp.broadcast_to(m + jnp.log(den), (len(items), ATT_BLOCK, HEAD_DIM)))
        o_all, l_all = jnp.concatenate(outs, axis=2), jnp.concatenate(lses, axis=2)
        for n, (g, r) in enumerate(items):
            _store_residue(o_ref, r, dil, g * ATT_BLOCK, o_all[n])
            _store_residue(l_ref, r, dil, g * ATT_BLOCK, l_all[n])

    out_blk = pl.BlockSpec((rows, PAIR), lambda i, p: (i, p))
    return pl.pallas_call(
        body, name=name, grid=(n_steps, N_PAIRS),
        in_specs=[pl.BlockSpec((rows, PAIR), lambda i, p: (i, p))]
        + _window_specs(dil, n_steps, lambda p: N_PAIRS + p) + _window_specs(dil, n_steps, lambda p: 2 * N_PAIRS + p),
        out_specs=(out_blk, out_blk),
        out_shape=(jax.ShapeDtypeStruct((s, ATT_WIDTH), F32),) * 2,
        compiler_params=_params(2))(qkv, qkv, qkv, qkv, qkv, qkv, qkv)


ATT_ROWS = 512


def _attn_combine(outs, lses, name):
    s = outs[0].shape[0]

    def body(o1, o2, o3, l1, l2, l3, att_ref, att32_ref, lse_ref):
        ls = [l1[...], l2[...], l3[...]]
        m = jnp.maximum(jnp.maximum(ls[0], ls[1]), ls[2])
        es = [jnp.exp(l - m) for l in ls]
        den = es[0] + es[1] + es[2]
        att = (es[0] * o1[...] + es[1] * o2[...] + es[2] * o3[...]) / den
        att_ref[...] = att.astype(BF16)
        att32_ref[...] = att
        lse_ref[...] = m + jnp.log(den)

    blk = pl.BlockSpec((ATT_ROWS, ATT_WIDTH), lambda i: (i, 0))
    return pl.pallas_call(
        body, name=name, grid=(s // ATT_ROWS,), in_specs=[blk] * 6, out_specs=(blk, blk, blk),
        out_shape=(jax.ShapeDtypeStruct((s, ATT_WIDTH), BF16), jax.ShapeDtypeStruct((s, ATT_WIDTH), F32),
                   jax.ShapeDtypeStruct((s, ATT_WIDTH), F32)),
        compiler_params=_params(1))(*outs, *lses)


def _attn_delta(datt, att, name):
    s = att.shape[0]

    def body(d_ref, a_ref, delta_ref):
        prod = d_ref[...] * a_ref[...]
        for h in range(N_HEADS):
            sl = slice(h * HEAD_DIM, (h + 1) * HEAD_DIM)
            delta_ref[:, sl] = jnp.broadcast_to(jnp.sum(prod[:, sl], axis=-1, keepdims=True), (ATT_ROWS, HEAD_DIM))

    blk = pl.BlockSpec((ATT_ROWS, ATT_WIDTH), lambda i: (i, 0))
    return pl.pallas_call(
        body, name=name, grid=(s // ATT_ROWS,), in_specs=[blk, blk], out_specs=blk,
        out_shape=jax.ShapeDtypeStruct((s, ATT_WIDTH), F32), compiler_params=_params(1))(datt, att)


def _attn_branch_bwd(qkv, do, lse, delta, prev, dil, name):
    s = qkv.shape[0]
    groups = ATT_GROUPS[dil]
    rows = ATT_BLOCK * dil * groups
    n_steps = s // rows
    length = s // dil
    has_prev = prev is not None
    tn = (((0,), (0,)), ((), ()))
    nt = (((1,), (1,)), ((), ()))

    def body(*refs):
        qs, ks, vs, dos, ls, des = (refs[3 * n:3 * n + 3] for n in range(6))
        rest = refs[18:]
        if has_prev:
            pq, pk, pv = rest[:3]
            rest = rest[3:]
        dq_ref, dk_ref, dv_ref = rest
        i, pair = pl.program_id(0), pl.program_id(1)
        items = [(g, r) for g in range(groups) for r in range(dil)]
        cur = lambda t: jnp.stack([_residue(t[1], r, ATT_BLOCK, dil, g * ATT_BLOCK) for g, r in items])
        win = lambda t: jnp.stack([_residue_window(t, r, dil, g) for g, r in items])
        q_cur, k_cur, v_cur, do_cur = (cur(t).astype(BF16) for t in (qs, ks, vs, dos))
        q_win, k_win, v_win, do_win = (win(t).astype(BF16) for t in (qs, ks, vs, dos))
        l_cur, de_cur, l_win, de_win = cur(ls), cur(des), win(ls), win(des)

        def masks(transposed):
            per_group = [_band_masks(i * groups + g, length, dil, transposed) for g in range(groups)]
            if groups == 1:
                return per_group[0][0][None], per_group[0][1][None]
            return jnp.stack([per_group[g][0] for g, _ in items]), jnp.stack([per_group[g][1] for g, _ in items])

        valid_q, distf_q = masks(False)
        valid_k, distf_k = masks(True)
        dot = lambda eq, a, b: jnp.einsum(eq, a, b, preferred_element_type=F32)
        dqs, dks, dvs = [], [], []
        for hh in range(2):
            sl = slice(hh * HEAD_DIM, (hh + 1) * HEAD_DIM)
            one = slice(hh * HEAD_DIM, hh * HEAD_DIM + 1)
            slope = jnp.where(pair == 0, SLOPES[hh], jnp.where(pair == 1, SLOPES[2 + hh], SLOPES[4 + hh]))
            sc = dot("bqd,bkd->bqk", q_cur[:, :, sl], k_win[:, :, sl]) * ATT_SCALE - slope * distf_q
            p = jnp.exp(jnp.where(valid_q, sc - l_cur[:, :, one], MASK_VALUE))
            dp = dot("bqd,bkd->bqk", do_cur[:, :, sl], v_win[:, :, sl])
            ds = (p * (dp - de_cur[:, :, one]) * ATT_SCALE).astype(BF16)
            dqs.append(dot("bqk,bkd->bqd", ds, k_win[:, :, sl]))

            sc2 = dot("bqd,bkd->bqk", q_win[:, :, sl], k_cur[:, :, sl]) * ATT_SCALE - slope * distf_k
            p2 = jnp.exp(jnp.where(valid_k, sc2 - l_win[:, :, one], MASK_VALUE))
            dvs.append(dot("bqk,bqd->bkd", p2.astype(BF16), do_win[:, :, sl]))
            dp2 = dot("bqd,bkd->bqk", do_win[:, :, sl], v_cur[:, :, sl])
            ds2 = (p2 * (dp2 - de_win[:, :, one]) * ATT_SCALE).astype(BF16)
            dks.append(dot("bqk,bqd->bkd", ds2, q_win[:, :, sl]))
        for parts, acc, out in ((dqs, pq if has_prev else None, dq_ref), (dks, pk if has_prev else None, dk_ref),
                                (dvs, pv if has_prev else None, dv_ref)):
            val = jnp.concatenate(parts, axis=2)
            for n, (g, r) in enumerate(items):
                piece = val[n]
                if has_prev:
                    piece = piece + _residue(acc, r, ATT_BLOCK, dil, g * ATT_BLOCK)
                _store_residue(out, r, dil, g * ATT_BLOCK, piece)

    blk = pl.BlockSpec((rows, PAIR), lambda i, p: (i, p))
    in_specs = (_window_specs(dil, n_steps, lambda p: p) + _window_specs(dil, n_steps, lambda p: N_PAIRS + p)
                + _window_specs(dil, n_steps, lambda p: 2 * N_PAIRS + p) + _window_specs(dil, n_steps, lambda p: p) * 3)
    ins = [qkv] * 9 + [do] * 3 + [lse] * 3 + [delta] * 3
    if has_prev:
        in_specs += [blk] * 3
        ins += list(prev)
    return pl.pallas_call(
        body, name=name, grid=(n_steps, N_PAIRS), in_specs=in_specs, out_specs=(blk, blk, blk),
        out_shape=(jax.ShapeDtypeStruct((s, ATT_WIDTH), F32),) * 3,
        compiler_params=_params(2))(*ins)


TB = 2 * REC_CHUNK
REC_ROWS = 5 * REC_WIDTH


REC_LEVELS = 6


def _scan_pos(p, rev):
    p = p & (REC_CHUNK - 1)
    return (REC_CHUNK - 1 - p) if rev else p


def _split3(x):
    hi = x.astype(BF16)
    rest = x - hi.astype(F32)
    mid = rest.astype(BF16)
    return hi, mid, (rest - mid.astype(F32)).astype(BF16)


def _chunk_sums(x, rev, with_levels):
    row = lax.broadcasted_iota(jnp.int32, (TB, TB), 0)
    col = lax.broadcasted_iota(jnp.int32, (TB, TB), 1)
    same = (row < REC_CHUNK) == (col < REC_CHUNK)
    s_row, s_col = _scan_pos(row, rev), _scan_pos(col, rev)
    mats = [same & (s_row <= s_col)]
    if with_levels:
        for level in range(1, REC_LEVELS + 1):
            shift = REC_LEVELS + 1 - level
            boundary = ((s_col >> shift) << shift) + (REC_CHUNK >> level) - 1
            mats.append(same & (s_row <= boundary))
        mats.append(same)
    cat = jnp.concatenate([m.astype(BF16) for m in mats], axis=1)
    total = sum(jnp.dot(term, cat, preferred_element_type=F32) for term in _split3(x))
    return [total[:, n * TB:(n + 1) * TB] for n in range(len(mats))]


def _hg_prep(qraw, z, lb, rev):
    lane = lax.broadcasted_iota(jnp.int32, (REC_WIDTH, TB), 1)
    in_a = lane < REC_CHUNK
    scan = _scan_pos(lane, rev)
    sig, sigm = _sigmoid(z), _sigmoid(-z)
    f = lb + (1.0 - lb) * sig
    kk = (1.0 - lb) * sigm
    sums = _chunk_sums(jnp.log(jnp.maximum(f, F_TINY)), rev, True)
    b, bend = sums[0], sums[-1]
    q = qraw * _sigmoid(qraw)
    eq, ek = [], []
    for level in range(1, REC_LEVELS + 1):
        r = sums[level]
        e = jnp.exp(jnp.minimum(b - r, r - b))
        second = ((scan >> (REC_LEVELS - level)) & 1) == 1
        eq.append(jnp.where(second, e, 0.0))
        ek.append(jnp.where(second, 0.0, e))
    lanes_end = (0, REC_CHUNK) if rev else (REC_CHUNK - 1, TB - 1)
    end_a, end_b = (b[:, n:n + 1] for n in lanes_end)
    return dict(in_a=in_a, sig=sig, sigm=sigm, f=f, kk=kk, b=b, end_a=end_a, end_b=end_b,
                q=q, qh=q * jnp.exp(b), kh=kk * jnp.exp(bend - b), ekb=jnp.exp(bend - b), eq=eq, ek=ek)


def _level_masks(rev):
    row = lax.broadcasted_iota(jnp.int32, (TB, TB), 0)
    col = lax.broadcasted_iota(jnp.int32, (TB, TB), 1)
    same = (row < REC_CHUNK) == (col < REC_CHUNK)
    s_row, s_col = _scan_pos(row, rev), _scan_pos(col, rev)
    masks = [same & ((s_row >> (REC_LEVELS + 1 - level)) == (s_col >> (REC_LEVELS + 1 - level)))
             for level in range(1, REC_LEVELS + 1)]
    return masks, row == col


def _head_rows(x, h):
    return x[h * HEAD_DIM:(h + 1) * HEAD_DIM, :]


def _block_diag_mask():
    r = lax.broadcasted_iota(jnp.int32, (REC_WIDTH, REC_WIDTH), 0) // HEAD_DIM
    c = lax.broadcasted_iota(jnp.int32, (REC_WIDTH, REC_WIDTH), 1) // HEAD_DIM
    return (r == c).astype(F32)


def _heads(x):
    return x.reshape(N_HEADS, HEAD_DIM, TB)


def _hgrn_scan(projt, lb, rev, name):
    s = projt.shape[1]
    nblk = s // TB
    zrow = 2 if rev else 1
    tmap = (lambda i: nblk - 1 - i) if rev else (lambda i: i)
    tn = (((0,), (0,)), ((), ()))
    nt = (((1,), (1,)), ((), ()))

    def body(q_ref, z_ref, v_ref, lb_ref, o_ref, hs_ref, at_ref, h_ref):
        @pl.when(pl.program_id(0) == 0)
        def _():
            h_ref[...] = jnp.zeros_like(h_ref)

        v = v_ref[...]
        vb = v.astype(BF16)
        pr = _hg_prep(q_ref[...], z_ref[...], lb_ref[...], rev)
        q, kk = pr["q"], pr["kk"]
        masks, diag = _level_masks(rev)
        own = jnp.sum(_heads(q * kk), axis=1, keepdims=True)
        sc = jnp.where(diag[None], own, 0.0)
        for level in range(REC_LEVELS):
            qt = _heads((q * pr["eq"][level]).astype(BF16))
            kt = _heads((kk * pr["ek"][level]).astype(BF16))
            sc = sc + jnp.where(masks[level][None],
                                jnp.einsum("hks,hkt->hst", kt, qt, preferred_element_type=F32), 0.0)
        a_bf = sc.astype(BF16)
        at_ref[...] = a_bf
        o = jnp.einsum("hvs,hst->hvt", _heads(vb), a_bf, preferred_element_type=F32).reshape(REC_WIDTH, TB)
        bd_mask = _block_diag_mask()
        order = ((1, ~pr["in_a"], pr["end_b"]), (0, pr["in_a"], pr["end_a"]))
        if not rev:
            order = order[::-1]
        for slot, msk, bend in order:
            h0 = h_ref[...]
            hs_ref[slot] = h0
            o = o + lax.dot_general(h0.astype(BF16), jnp.where(msk, pr["qh"], 0.0).astype(BF16), tn,
                                    preferred_element_type=F32)
            upd = lax.dot_general(jnp.where(msk, pr["kh"], 0.0).astype(BF16), vb, nt, preferred_element_type=F32)
            h_ref[...] = jnp.exp(bend) * h0 + upd * bd_mask
        o_ref[...] = o

    row_blk = lambda r: pl.BlockSpec((REC_WIDTH, TB), lambda i: (r, tmap(i)))
    return pl.pallas_call(
        body, name=name, grid=(nblk,),
        in_specs=[row_blk(0), row_blk(zrow), row_blk(3), pl.BlockSpec((REC_WIDTH, 1), lambda i: (0, 0))],
        out_specs=(pl.BlockSpec((REC_WIDTH, TB), lambda i: (0, tmap(i))),
                   pl.BlockSpec((2, REC_WIDTH, REC_WIDTH), lambda i: (tmap(i), 0, 0)),
                   pl.BlockSpec((None, N_HEADS, TB, TB), lambda i: (tmap(i), 0, 0, 0))),
        out_shape=(jax.ShapeDtypeStruct((REC_WIDTH, s), F32),
                   jax.ShapeDtypeStruct((s // REC_CHUNK, REC_WIDTH, REC_WIDTH), F32),
                   jax.ShapeDtypeStruct((nblk, N_HEADS, TB, TB), BF16)),
        scratch_shapes=[pltpu.VMEM((REC_WIDTH, REC_WIDTH), F32)],
        compiler_params=_params(1))(projt, projt, projt, lb)


def _hgrn_scan_bwd(projt, lb, dot, hs, at, prev, rev, name):
    s = projt.shape[1]
    nblk = s // TB
    zrow = 2 if rev else 1
    tmap = (lambda i: i) if rev else (lambda i: nblk - 1 - i)
    has_prev = prev is not None
    tn = (((0,), (0,)), ((), ()))
    nt = (((1,), (1,)), ((), ()))

    def body(*refs):
        q_ref, z_ref, v_ref, lb_ref, do_ref, hs_ref, at_ref = refs[:7]
        rest = refs[7:]
        if has_prev:
            pq_ref, pv_ref = rest[:2]
            rest = rest[2:]
        dq_ref, dz_ref, dv_ref, dlb_ref, dh_ref = rest

        @pl.when(pl.program_id(0) == 0)
        def _():
            dh_ref[...] = jnp.zeros_like(dh_ref)
            dlb_ref[...] = jnp.zeros_like(dlb_ref)

        qraw, v, do, lbv = q_ref[...], v_ref[...], do_ref[...], lb_ref[...]
        dob, vb = do.astype(BF16), v.astype(BF16)
        pr = _hg_prep(qraw, z_ref[...], lbv, rev)
        q, kk, b, in_a = pr["q"], pr["kk"], pr["b"], pr["in_a"]
        masks, diag = _level_masks(rev)
        dot = lambda eq, x, y: jnp.einsum(eq, x, y, preferred_element_type=F32)
        d_at = dot("hvs,hvt->hst", _heads(vb), _heads(dob))
        dv = dot("hvt,hst->hvs", _heads(dob), at_ref[...]).reshape(REC_WIDTH, TB)
        d_own = jnp.sum(jnp.where(diag[None], d_at, 0.0), axis=1, keepdims=True)
        dq_in = (d_own * _heads(kk)).reshape(REC_WIDTH, TB)
        dk_in = (d_own * _heads(q)).reshape(REC_WIDTH, TB)
        db_in = jnp.zeros((REC_WIDTH, TB), F32)
        for lv in range(REC_LEVELS):
            d_lv = jnp.where(masks[lv][None], d_at, 0.0).astype(BF16)
            q_lv, k_lv = (q * pr["eq"][lv]).astype(BF16), (kk * pr["ek"][lv]).astype(BF16)
            dqt = dot("hks,hst->hkt", _heads(k_lv), d_lv).reshape(REC_WIDTH, TB)
            dkt = dot("hkt,hst->hks", _heads(q_lv), d_lv).reshape(REC_WIDTH, TB)
            dq_in = dq_in + pr["eq"][lv] * dqt
            dk_in = dk_in + pr["ek"][lv] * dkt
            db_in = db_in + q_lv.astype(F32) * dqt - k_lv.astype(F32) * dkt
        dq = dk = jnp.zeros((REC_WIDTH, TB), F32)

        zero = jnp.zeros((REC_WIDTH, TB), F32)
        bd_mask = _block_diag_mask()
        eb = jnp.exp(b)
        const = zero
        order = ((0, in_a, pr["end_a"]), (1, ~in_a, pr["end_b"]))
        if not rev:
            order = order[::-1]
        for slot, msk, bend in order:
            h0 = hs_ref[slot]
            dh1 = dh_ref[...]
            dh1b = dh1.astype(BF16)
            dq = dq + eb * jnp.dot(h0.astype(BF16), jnp.where(msk, do, 0.0).astype(BF16), preferred_element_type=F32)
            dv = dv + lax.dot_general(dh1b, jnp.where(msk, pr["kh"], 0.0).astype(BF16), tn, preferred_element_type=F32)
            dk_int = pr["ekb"] * jnp.dot(dh1b, jnp.where(msk, v, 0.0).astype(BF16), preferred_element_type=F32)
            dk = dk + dk_int
            ebend = jnp.exp(bend)
            c = (jnp.sum(kk * dk_int, axis=1, keepdims=True)
                 + ebend * jnp.sum(h0 * dh1, axis=1, keepdims=True))
            const = const + jnp.where(msk, c, 0.0)
            upd = lax.dot_general(jnp.where(msk, pr["qh"], 0.0).astype(BF16), dob, nt, preferred_element_type=F32)
            dh_ref[...] = ebend * dh1 + upd * bd_mask

        dg = _chunk_sums(db_in + q * dq - kk * dk, not rev, False)[0] + const
        dq, dk = dq + dq_in, dk + dk_in
        sig, sigm, f = pr["sig"], pr["sigm"], pr["f"]
        live = f > F_TINY
        inv_f = 1.0 / jnp.maximum(f, F_TINY)
        one_lb = 1.0 - lbv
        dz = sig * sigm * one_lb * (jnp.where(live, dg * inv_f, 0.0) - dk)
        dlb_ref[...] += jnp.sum(sigm * (jnp.where(live, dg * inv_f, 0.0) - dk), axis=1, keepdims=True)
        dqr = dq * _silu_grad(qraw)
        if has_prev:
            dqr = dqr + pq_ref[...]
            dv = dv + pv_ref[...]
        dq_ref[...] = dqr
        dz_ref[...] = dz
        dv_ref[...] = dv

    row_blk = lambda r: pl.BlockSpec((REC_WIDTH, TB), lambda i: (r, tmap(i)))
    blk = pl.BlockSpec((REC_WIDTH, TB), lambda i: (0, tmap(i)))
    col = pl.BlockSpec((REC_WIDTH, 1), lambda i: (0, 0))
    in_specs = [row_blk(0), row_blk(zrow), row_blk(3), col, blk,
                pl.BlockSpec((2, REC_WIDTH, REC_WIDTH), lambda i: (tmap(i), 0, 0)),
                pl.BlockSpec((None, N_HEADS, TB, TB), lambda i: (tmap(i), 0, 0, 0))]
    ins = [projt, projt, projt, lb, dot, hs, at]
    if has_prev:
        in_specs += [blk, blk]
        ins += list(prev)
    t_shape = jax.ShapeDtypeStruct((REC_WIDTH, s), F32)
    return pl.pallas_call(
        body, name=name, grid=(nblk,), in_specs=in_specs, out_specs=(blk, blk, blk, col),
        out_shape=(t_shape, t_shape, t_shape, jax.ShapeDtypeStruct((REC_WIDTH, 1), F32)),
        scratch_shapes=[pltpu.VMEM((REC_WIDTH, REC_WIDTH), F32)],
        compiler_params=_params(1))(*ins)


REC_OUT_COLS = 512


def _head_rms(o):
    o3 = o.reshape(N_HEADS, HEAD_DIM, o.shape[1])
    rstd = lax.rsqrt(jnp.mean(o3 * o3, axis=1, keepdims=True) + EPS)
    return o3 * rstd, rstd


def _hgrn_out(of, ob, projt, wn, name):
    s = of.shape[1]

    def body(of_ref, ob_ref, g_ref, wn_ref, o_ref):
        on, _ = _head_rms(of_ref[...] + ob_ref[...])
        g = g_ref[...]
        y = on.reshape(REC_WIDTH, REC_OUT_COLS) * wn_ref[...] * (g * _sigmoid(g))
        o_ref[...] = y.T.astype(BF16)

    blk = pl.BlockSpec((REC_WIDTH, REC_OUT_COLS), lambda i: (0, i))
    return pl.pallas_call(
        body, name=name, grid=(s // REC_OUT_COLS,),
        in_specs=[blk, blk, pl.BlockSpec((REC_WIDTH, REC_OUT_COLS), lambda i: (4, i)),
                  pl.BlockSpec((REC_WIDTH, 1), lambda i: (0, 0))],
        out_specs=pl.BlockSpec((REC_OUT_COLS, REC_WIDTH), lambda i: (i, 0)),
        out_shape=jax.ShapeDtypeStruct((s, REC_WIDTH), BF16), compiler_params=_params(1))(of, ob, projt, wn)


def _hgrn_out_bwd(drec, of, ob, projt, wn, name):
    s = of.shape[1]

    def body(d_ref, of_ref, ob_ref, g_ref, wn_ref, do_ref, dg_ref, dwn_ref):
        @pl.when(pl.program_id(0) == 0)
        def _():
            dwn_ref[...] = jnp.zeros_like(dwn_ref)

        dy = d_ref[...].T
        on3, rstd = _head_rms(of_ref[...] + ob_ref[...])
        on = on3.reshape(REC_WIDTH, REC_OUT_COLS)
        g, wnv = g_ref[...], wn_ref[...]
        dg_ref[...] = dy * on * wnv * _silu_grad(g)
        d_onw = dy * (g * _sigmoid(g))
        dwn_ref[...] += jnp.sum(d_onw * on, axis=1, keepdims=True)
        d_on3 = (d_onw * wnv).reshape(N_HEADS, HEAD_DIM, REC_OUT_COLS)
        do3 = rstd * (d_on3 - on3 * jnp.mean(d_on3 * on3, axis=1, keepdims=True))
        do_ref[...] = do3.reshape(REC_WIDTH, REC_OUT_COLS)

    blk = pl.BlockSpec((REC_WIDTH, REC_OUT_COLS), lambda i: (0, i))
    col = pl.BlockSpec((REC_WIDTH, 1), lambda i: (0, 0))
    t_shape = jax.ShapeDtypeStruct((REC_WIDTH, s), F32)
    return pl.pallas_call(
        body, name=name, grid=(s // REC_OUT_COLS,),
        in_specs=[pl.BlockSpec((REC_OUT_COLS, REC_WIDTH), lambda i: (i, 0)), blk, blk,
                  pl.BlockSpec((REC_WIDTH, REC_OUT_COLS), lambda i: (4, i)), col],
        out_specs=(blk, blk, col),
        out_shape=(t_shape, t_shape, jax.ShapeDtypeStruct((REC_WIDTH, 1), F32)),
        compiler_params=_params(1))(drec, of, ob, projt, wn)


def _lower_bounds(gamma, name):
    def body(g_ref, lb_ref, p_ref):
        g0, g1 = g_ref[0:1, :], g_ref[1:2, :]
        m = jnp.maximum(g0, g1)
        e0, e1 = jnp.exp(g0 - m), jnp.exp(g1 - m)
        p0, p1 = e0 / (e0 + e1), e1 / (e0 + e1)
        lb_ref[...] = (p0 + p1) - p0
        p_ref[0:1, :] = p0
        p_ref[1:2, :] = p1

    n = gamma.shape[1]
    return pl.pallas_call(body, name=name,
                          out_shape=(jax.ShapeDtypeStruct((1, n), F32), jax.ShapeDtypeStruct((2, n), F32)))(gamma)


def _lower_bounds_bwd(dlb1, p, name):
    def body(d_ref, p_ref, o_ref):
        p0, p1, d = p_ref[0:1, :], p_ref[1:2, :], d_ref[...]
        inner = p1 * d
        o_ref[0:1, :] = p0 * (0.0 - inner)
        o_ref[1:2, :] = p1 * (d - inner)

    return pl.pallas_call(body, name=name, out_shape=jax.ShapeDtypeStruct(p.shape, F32))(dlb1, p)


def _split_w_in(w_in):
    return dict(conv=w_in[:, G_CONV[0]:G_CONV[1]], qkv=w_in[:, G_QKV[0]:G_QKV[1]],
                rec_t=w_in[:, G_REC[0]:].T, nat=w_in[:, :G_REC[0]])


def _split_w_rest(w_out, w_up, w_down):
    return dict(out=w_out, out_a=w_out[:CONV_CH], out_b=w_out[CONV_CH:CONV_CH + ATT_WIDTH],
                out_c=w_out[CONV_CH + ATT_WIDTH:], up=w_up, down=w_down)


def _col(v):
    return v.reshape(-1, 1)


def _sequence_step(x, tgt, mods, lbs, small, w_in0, later_weights, final_w, on_layer_grads):
    saved = []
    xin = x
    big = [_split_w_in(w_in0), None]
    h1 = _resid_norm_mod(x, None, None, small[0]["norm1_w"], mods[0][1:2], mods[0][0:1], "norm1_first")
    for l in range(DEPTH):
        sm, w, md = small[l], big[l], mods[l]
        pa = _matmul(h1, w["conv"], "nn", F32, f"proj_conv")
        qkv = _matmul(h1, w["qkv"], "nn", F32, f"proj_qkv")
        projt = _matmul(w["rec_t"], h1, "nt", F32, f"proj_rec")
        a_out, cv = _conv_mixer(pa, sm["conv_a_w"], sm["conv_a_b"], sm["ln_a_w"], sm["ln_a_b"], f"conv_mixer")
        outs, lses = zip(*[_attn_branch(qkv, d, f"attn_d{d}") for d in DILATIONS])
        att, att32, lse = _attn_combine(outs, lses, f"attn_combine")
        lb_f, lb_b = _col(lbs[l][0]), _col(lbs[l][1])
        of, hsf, atf = _hgrn_scan(projt, lb_f, False, "hgrn_fwd")
        ob, hsb, atb = _hgrn_scan(projt, lb_b, True, "hgrn_rev")
        wn = _col(sm["rec_norm_w"])
        rec = _hgrn_out(of, ob, projt, wn, f"hgrn_out")
        mixed = jnp.concatenate([a_out, att, rec], axis=1)
        if l == 0:
            w_in1, w_out_all, w_up_all, w_down_all = later_weights(rec)
            big[0].update(_split_w_rest(w_out_all[0], w_up_all[0], w_down_all[0]))
            big[1] = dict(_split_w_in(w_in1), **_split_w_rest(w_out_all[1], w_up_all[1], w_down_all[1]))
        r1 = _matmul(mixed, w["out"], "nn", BF16, "out_proj")
        xmid, h2 = _resid_norm_mod(xin, r1, md[2:3], sm["norm2_w"], md[4:5], md[3:4], f"norm2")
        u = _matmul(h2, w["up"], "nn", BF16, f"ffn_up")
        act, conv_g, conv_v = _ffn_act(u, sm["conv_f_w"], "ffn_act")
        r2 = _matmul(act, w["down"], "nn", BF16, "ffn_down")
        saved.append(dict(xin=xin, h1=h1, pa=pa, qkv=qkv, projt=projt, cv=cv, att32=att32, lse=lse, of=of, ob=ob,
                          hsf=hsf, hsb=hsb, atf=atf, atb=atb, lb_f=lb_f, lb_b=lb_b, wn=wn, mixed=mixed, r1=r1, xmid=xmid, h2=h2,
                          u=u, conv_g=conv_g, conv_v=conv_v, act=act, r2=r2))
        if l + 1 < DEPTH:
            nxt = small[l + 1]
            xin, h1 = _resid_norm_mod(xmid, r2, md[5:6], nxt["norm1_w"], mods[l + 1][1:2], mods[l + 1][0:1],
                                      "norm1")
    top = saved[-1]
    loss, dx, dr2, dg2, dfw = _final_loss(top["xmid"], top["r2"], mods[-1][5:6], final_w, tgt, "final_loss")

    grads = [None] * DEPTH
    order_after = None
    for l in reversed(range(DEPTH)):
        sm, w, md, sv = small[l], big[l], mods[l], saved[l]
        dact = _matmul(dr2, w["down"], "nt", BF16, f"d_act")
        g_down = _matmul(dr2, sv["act"], "tn", F32, "dw_down").T
        conv_f_w = sm["conv_f_w"] if order_after is None else sm["conv_f_w"] + order_after
        dug, duv, dwg, dwv = _ffn_act_bwd(sv["u"], sv["conv_g"], sv["conv_v"], dact, conv_f_w, "ffn_act_bwd")
        du = jnp.concatenate([dug, duv], axis=1)
        dh2 = _matmul(du, w["up"], "nt", BF16, "d_h2")
        g_up = _matmul(sv["h2"], du, "tn", F32, f"dw_up")
        dxmid, dr1, dsh2, dsc2, dnw2, dg1 = _norm_bwd(sv["xmid"], [dh2], dx, sm["norm2_w"], md[4:5], md[2:3], sv["r1"],
                                                     f"norm2_bwd")
        dmix_a = _matmul(dr1, w["out_a"], "nt", F32, f"d_mix_a")
        dmix_b = _matmul(dr1, w["out_b"], "nt", F32, f"d_mix_b")
        dmix_c = _matmul(dr1, w["out_c"], "nt", F32, f"d_mix_c")
        g_out = _matmul(sv["mixed"], dr1, "tn", F32, f"dw_out")
        dc, dlnw, dlnb, dcb = _conv_mixer_bwd_ln(sv["cv"], dmix_a, sm["ln_a_w"], sm["ln_a_b"], f"conv_mixer_bwd_ln")
        dpa, dcw = _conv_mixer_bwd_conv(sv["pa"], dc, sm["conv_a_w"], f"conv_mixer_bwd_conv")
        delta = _attn_delta(dmix_b, sv["att32"], "attn_delta")
        dqkv = None
        for d in DILATIONS:
            dqkv = _attn_branch_bwd(sv["qkv"], dmix_b, sv["lse"], delta, dqkv, d, f"attn_bwd_d{d}")
        dot, dgt, dwn = _hgrn_out_bwd(dmix_c, sv["of"], sv["ob"], sv["projt"], sv["wn"], f"hgrn_out_bwd")
        dqf, dzf, dvf, dlbf = _hgrn_scan_bwd(sv["projt"], sv["lb_f"], dot, sv["hsf"], sv["atf"], None, False,
                                             "hgrn_fwd_bwd")
        dqt, dzb, dvt, dlbb = _hgrn_scan_bwd(sv["projt"], sv["lb_b"], dot, sv["hsb"], sv["atb"], (dqf, dvf), True,
                                             "hgrn_rev_bwd")
        dprojt = jnp.concatenate([dqt, dzf, dzb, dvt, dgt], axis=0).astype(BF16)
        dnat = jnp.concatenate([dpa] + [t.astype(BF16) for t in dqkv], axis=1)
        dh1_a = _matmul(dnat, w["nat"], "nt", BF16, "d_h1_nat")
        dh1_b = _matmul(dprojt, w["rec_t"], "tn", BF16, "d_h1_rec")
        g_in_nat = _matmul(sv["h1"], dnat, "tn", F32, f"dw_in_nat")
        g_in_rec_t = _matmul(dprojt, sv["h1"], "nn", F32, f"dw_in_rec")
        g_in = jnp.concatenate([g_in_nat, g_in_rec_t.T], axis=1)
        if l > 0:
            below = saved[l - 1]
            dx, dr2, dsh1, dsc1, dnw1, dg2_below = _norm_bwd(sv["xin"], [dh1_a, dh1_b], dxmid, sm["norm1_w"], md[1:2],
                                                            mods[l - 1][5:6], below["r2"], f"norm1_bwd")
        else:
            dx, dsh1, dsc1, dnw1 = _norm_bwd(sv["xin"], [dh1_a, dh1_b], dxmid, sm["norm1_w"], md[1:2], None, None,
                                             f"norm1_bwd")
        grads[l] = dict(w_in=g_in, w_out=g_out, w_up=g_up, w_down=g_down,
                        mod=[dsh1, dsc1, dg1, dsh2, dsc2, dg2], norm1_w=dnw1, conv_a_w=dcw[:CONV_WIDTH], conv_a_b=dcb,
                        ln_a_w=dlnw, ln_a_b=dlnb, lb=jnp.concatenate([dlbf.reshape(1, -1), dlbb.reshape(1, -1)], axis=0),
                        rec_norm_w=dwn.reshape(1, -1), norm2_w=dnw2,
                        conv_f_w=jnp.concatenate([dwg[:3], dwv[:3]], axis=1))
        order_after = on_layer_grads(l, [g_in, g_out, g_up, g_down])
        if l > 0:
            dg2 = dg2_below
    return loss[0, 0], dx, grads, dfw


def _adamw_math(w, g, m, v):
    m = ADAM_B1 * m + (1.0 - ADAM_B1) * g
    v = ADAM_B2 * v + (1.0 - ADAM_B2) * (g * g)
    m_hat = m / (1.0 - ADAM_B1 ** ADAM_STEP)
    v_hat = v / (1.0 - ADAM_B2 ** ADAM_STEP)
    delta = -ADAM_LR * (m_hat / (jnp.sqrt(v_hat) + ADAM_EPS) + ADAM_WD * w)
    return delta, m, v


def _row_tile(rows, cols, max_elems=384 * 1024):
    best = None
    for t in range(8, rows + 1, 8):
        if rows % t == 0 and t * cols <= max_elems:
            best = t
    return best or rows


def _adamw(w, g, m, v, name):
    nl, r, c = w.shape
    tr = _row_tile(r, c)

    def body(w_ref, g_ref, m_ref, v_ref, d_ref, m2_ref, v2_ref):
        d_ref[...], m2_ref[...], v2_ref[...] = _adamw_math(w_ref[...], g_ref[...], m_ref[...], v_ref[...])

    blk = pl.BlockSpec((None, tr, c), lambda l, i: (l, i, 0))
    shape = jax.ShapeDtypeStruct((nl, r, c), F32)
    return pl.pallas_call(body, name=name, grid=(nl, r // tr), in_specs=[blk] * 4, out_specs=(blk, blk, blk),
                          out_shape=(shape, shape, shape), compiler_params=_params(2))(w, g, m, v)


ADA_SHARD = N_MOD * D_MODEL // 4
ADA_COLS = 512
ADA_ROWS = 256
HIGHEST = lax.Precision.HIGHEST


def _ada_mod(c_all, w_ada, b_sh, name):
    def body(c_ref, w_ref, b_ref, o_ref):
        cv = c_ref[...]
        o_ref[...] = jnp.dot(cv * _sigmoid(cv), w_ref[...], precision=HIGHEST, preferred_element_type=F32) + b_ref[...]

    return pl.pallas_call(
        body, name=name, grid=(DEPTH, ADA_SHARD // ADA_COLS),
        in_specs=[pl.BlockSpec((8, D_MODEL), lambda l, j: (0, 0)),
                  pl.BlockSpec((None, D_MODEL, ADA_COLS), lambda l, j: (l, 0, j)),
                  pl.BlockSpec((None, 1, ADA_COLS), lambda l, j: (l, 0, j))],
        out_specs=pl.BlockSpec((None, 8, ADA_COLS), lambda l, j: (l, 0, j)),
        out_shape=jax.ShapeDtypeStruct((DEPTH, 8, ADA_SHARD), F32), compiler_params=_params(2))(c_all, w_ada, b_sh)


def _ada_update(c_all, dmod_sh, w, m, v, name):
    def body(c_ref, d_ref, w_ref, m_ref, v_ref, g_ref, dl_ref, m2_ref, v2_ref):
        cv = c_ref[...]
        g = lax.dot_general(cv * _sigmoid(cv), d_ref[...], (((0,), (0,)), ((), ())), precision=HIGHEST,
                            preferred_element_type=F32)
        g_ref[...] = g
        dl_ref[...], m2_ref[...], v2_ref[...] = _adamw_math(w_ref[...], g, m_ref[...], v_ref[...])

    blk = pl.BlockSpec((None, ADA_ROWS, ADA_SHARD), lambda l, i: (l, i, 0))
    shape = jax.ShapeDtypeStruct((DEPTH, D_MODEL, ADA_SHARD), F32)
    return pl.pallas_call(
        body, name=name, grid=(DEPTH, D_MODEL // ADA_ROWS),
        in_specs=[pl.BlockSpec((8, ADA_ROWS), lambda l, i: (0, i)),
                  pl.BlockSpec((None, 8, ADA_SHARD), lambda l, i: (l, 0, 0)), blk, blk, blk],
        out_specs=(blk,) * 4, out_shape=(shape,) * 4, compiler_params=_params(2))(c_all, dmod_sh, w, m, v)


def _sum_devices(packs, name):
    def body(p_ref, o_ref):
        acc = p_ref[0]
        for dev in range(1, 8):
            acc = acc + p_ref[dev]
        o_ref[...] = acc

    return pl.pallas_call(body, name=name, out_shape=jax.ShapeDtypeStruct(packs.shape[1:], F32))(packs)


def _mesh_pos():
    return lax.axis_index("x"), lax.axis_index("y"), lax.axis_index("c")


def _flip(v, bit):
    return 1 - v if bit else v


def _allgather_devices(x, name):
    m_per, n = x.shape

    def body(x_ref, out_ref, send_sems, recv_sems, local_sem):
        ix, iy, ic = _mesh_pos()
        me, sibling = (ix, iy, ic), (ix, iy, 1 - ic)
        chips = [(1 - ix, iy), (ix, 1 - iy), (1 - ix, 1 - iy)]

        def rows(px, py, pc):
            return out_ref.at[pl.ds((4 * px + 2 * py + pc) * m_per, m_per), :]

        def copy(k, block, to, src=None):
            return pltpu.make_async_remote_copy(
                src_ref=rows(*block) if src is None else src, dst_ref=rows(*block),
                send_sem=send_sems.at[k], recv_sem=recv_sems.at[k], device_id=to, device_id_type=MESH)

        mine = pltpu.make_async_copy(x_ref, rows(*me), local_sem)
        mine.start()
        first = [copy(0, me, sibling, src=x_ref)]
        first += [copy(1 + j, me, (*chip, ic), src=x_ref) for j, chip in enumerate(chips)]
        for cp in first:
            cp.start()
        passed = [copy(4 + j, (*chip, ic), sibling) for j, chip in enumerate(chips)]
        for j, chip in enumerate(chips):
            copy(1 + j, (*chip, ic), me).wait_recv()
            passed[j].start()
        copy(0, sibling, me).wait_recv()
        for j, chip in enumerate(chips):
            copy(4 + j, (*chip, 1 - ic), me).wait_recv()
        for cp in first + passed:
            cp.wait_send()
        mine.wait()

    return pl.pallas_call(
        body, name=name, out_shape=jax.ShapeDtypeStruct((8 * m_per, n), x.dtype),
        in_specs=[pl.BlockSpec(memory_space=pltpu.VMEM)], out_specs=pl.BlockSpec(memory_space=pltpu.VMEM),
        scratch_shapes=[pltpu.SemaphoreType.DMA((7,)), pltpu.SemaphoreType.DMA((7,)), pltpu.SemaphoreType.DMA],
    )(x)


def _gather_chips(shards, name):
    n = len(shards)

    def body(*refs):
        ins, outs = refs[:n], refs[n:2 * n]
        send_sems, recv_sems, local_sems = refs[2 * n:]
        ix, iy, ic = _mesh_pos()
        me = 2 * ix + iy
        local = [pltpu.make_async_copy(ins[a], outs[a].at[me], local_sems.at[a]) for a in range(n)]
        for cp in local:
            cp.start()
        remote = []
        for a in range(n):
            for k in (1, 2, 3):
                px, py = _flip(ix, k & 2), _flip(iy, k & 1)
                sems = dict(send_sem=send_sems.at[3 * a + k - 1], recv_sem=recv_sems.at[3 * a + k - 1],
                            device_id=(px, py, ic), device_id_type=MESH)
                out_cp = pltpu.make_async_remote_copy(src_ref=ins[a], dst_ref=outs[a].at[me], **sems)
                in_cp = pltpu.make_async_remote_copy(src_ref=ins[a], dst_ref=outs[a].at[2 * px + py], **sems)
                out_cp.start()
                remote.append((out_cp, in_cp))
        for out_cp, in_cp in remote:
            out_cp.wait_send()
            in_cp.wait_recv()
        for cp in local:
            cp.wait()

    return pl.pallas_call(
        body, name=name, in_specs=[ANY] * n, out_specs=tuple([ANY] * n),
        out_shape=tuple(jax.ShapeDtypeStruct((4,) + t.shape, t.dtype) for t in shards),
        scratch_shapes=[pltpu.SemaphoreType.DMA((3 * n,)), pltpu.SemaphoreType.DMA((3 * n,)),
                        pltpu.SemaphoreType.DMA((n,))],
    )(*shards)


HBM = pl.BlockSpec(memory_space=pltpu.HBM)
SEM = pl.BlockSpec(memory_space=pltpu.SEMAPHORE)
DATAFLOW = pltpu.SideEffectType.DATAFLOW_SIDE_EFFECTING


def _peer_chip(ix, iy, k):
    return _flip(ix, k & 2), _flip(iy, k & 1)


def _gather_chips_start(shards, name):
    n = len(shards)

    def body(*refs):
        src, land = refs[:n], refs[n:2 * n]
        send_sems, recv_sems = refs[2 * n], refs[2 * n + 1]
        token = refs[-1]
        ix, iy, ic = _mesh_pos()
        me = 2 * ix + iy
        for a in range(n):
            for k in (1, 2, 3):
                px, py = _peer_chip(ix, iy, k)
                pltpu.make_async_remote_copy(
                    src_ref=src[a], dst_ref=land[a].at[me], send_sem=send_sems.at[3 * a + k - 1],
                    recv_sem=recv_sems.at[3 * a + k - 1], device_id=(px, py, ic), device_id_type=MESH).start()
        token[...] = jnp.zeros_like(token)

    hbm = lambda shape, dtype: pltpu.HBM(shape, dtype)
    operands = ([pltpu.with_memory_space_constraint(t, pltpu.HBM) for t in shards]
                + [pltpu.with_memory_space_constraint(lax.empty((4,) + t.shape, t.dtype), pltpu.HBM) for t in shards])
    return pl.pallas_call(
        body, name=name,
        out_shape=(pltpu.SemaphoreType.DMA((3 * n,)), pltpu.SemaphoreType.DMA((3 * n,)),
                   *[hbm(t.shape, t.dtype) for t in shards], *[hbm((4,) + t.shape, t.dtype) for t in shards],
                   jax.ShapeDtypeStruct((8, LANES), F32)),
        in_specs=(HBM,) * (2 * n),
        out_specs=(SEM, SEM) + (HBM,) * (2 * n) + (pl.BlockSpec(memory_space=pltpu.VMEM),),
        input_output_aliases={a: 2 + a for a in range(2 * n)},
        compiler_params=pltpu.CompilerParams(has_side_effects=DATAFLOW),
    )(*operands)


def _gather_chips_wait(started, after, name):
    send_sems, recv_sems = started[0], started[1]
    thru = started[2:-1]
    n = len(thru) // 2

    def body(*refs):
        src, land = refs[:n], refs[n:2 * n]
        send_sems, recv_sems = refs[2 * n], refs[2 * n + 1]
        ix, iy, ic = _mesh_pos()
        for a in range(n):
            for k in (1, 2, 3):
                px, py = _peer_chip(ix, iy, k)
                cp = pltpu.make_async_remote_copy(
                    src_ref=src[a], dst_ref=land[a].at[2 * px + py], send_sem=send_sems.at[3 * a + k - 1],
                    recv_sem=recv_sems.at[3 * a + k - 1], device_id=(px, py, ic), device_id_type=MESH)
                cp.wait_send()
                cp.wait_recv()

    outs = pl.pallas_call(
        body, name=name,
        out_shape=tuple(pltpu.HBM(t.shape, t.dtype) for t in thru),
        in_specs=(HBM,) * (2 * n) + (SEM, SEM, ANY), out_specs=(HBM,) * (2 * n),
        input_output_aliases={a: a for a in range(2 * n)},
        compiler_params=pltpu.CompilerParams(has_side_effects=DATAFLOW),
    )(*thru, send_sems, recv_sems, after)
    return outs[:n], outs[n:]


BIG_KINDS = (("w_in", "col", D_MODEL, IN_COLS), ("w_out", "row", D_MODEL, D_MODEL),
             ("w_up", "col", D_MODEL, 2 * D_FF), ("w_down", "row", D_FF, D_MODEL))


def _piece_shape(how, r, c):
    return (r // 2, c // 4) if how == "col" else (r // 8, c)


def _aligned(start, multiple):
    return start if isinstance(start, int) else pl.multiple_of(start, multiple)


def _piece(ref, how, r, c, chip, half):
    if how == "col":
        return ref.at[pl.ds(_aligned(half * (r // 2), 8), r // 2), pl.ds(_aligned(chip * (c // 4), LANES), c // 4)]
    n = r // 4
    return ref.at[pl.ds(_aligned(chip * n + half * (n // 2), 8), n // 2), :]


def _rs_pair_exchange(grads, name):
    nk = len(BIG_KINDS)
    nl = len(grads[0])
    flat = [grads[ki][l] for ki in range(nk) for l in range(nl)]
    per = nl * 4

    def body(*refs):
        g, land = refs[:nk * nl], refs[nk * nl:nk * nl + nk]
        send_sems, recv_sems = refs[nk * nl + nk:]
        ix, iy, ic = _mesh_pos()
        sibling = (ix, iy, 1 - ic)
        copies = []
        for ki, (_, how, r, c) in enumerate(BIG_KINDS):
            for l in range(nl):
                for j in range(4):
                    sem = ki * per + l * 4 + j
                    rem = pltpu.make_async_remote_copy(
                        src_ref=_piece(g[ki * nl + l], how, r, c, j, 1 - ic), dst_ref=land[ki].at[l, j],
                        send_sem=send_sems.at[sem], recv_sem=recv_sems.at[sem], device_id=sibling, device_id_type=MESH)
                    rem.start()
                    copies.append(rem)
        for rem in copies:
            rem.wait_send()
            rem.wait_recv()

    shapes = [jax.ShapeDtypeStruct((nl, 4) + _piece_shape(how, r, c), F32) for _, how, r, c in BIG_KINDS]
    return pl.pallas_call(
        body, name=name, in_specs=[ANY] * len(flat), out_specs=tuple([ANY] * nk), out_shape=tuple(shapes),
        scratch_shapes=[pltpu.SemaphoreType.DMA((nk * per,))] * 2,
    )(*flat)


def _pair_sum(g, theirs, layer, how, core, name):
    r, c = g.shape
    pr, pc = _piece_shape(how, r, c)
    if how == "col":
        mine_spec = pl.BlockSpec((pr, pc), lambda j, core_ref: (core_ref[0], j))
    else:
        mine_spec = pl.BlockSpec((pr, pc), lambda j, core_ref: (2 * j + core_ref[0], 0))

    def body(core_ref, g_ref, t_ref, o_ref, ob_ref):
        total = g_ref[...] + t_ref[...]
        o_ref[...] = total
        ob_ref[...] = total.astype(BF16)

    out_blk = pl.BlockSpec((None, pr, pc), lambda j, core_ref: (j, 0, 0))
    return pl.pallas_call(
        body, name=name,
        grid_spec=pltpu.PrefetchScalarGridSpec(
            num_scalar_prefetch=1, grid=(4,),
            in_specs=[mine_spec, pl.BlockSpec((None, None, pr, pc), lambda j, core_ref: (layer, j, 0, 0))],
            out_specs=(out_blk, out_blk)),
        out_shape=(jax.ShapeDtypeStruct((4, pr, pc), F32), jax.ShapeDtypeStruct((4, pr, pc), BF16)),
        compiler_params=_params(1))(core, g, theirs)


def _rs_chip_exchange(pair_sums, name):
    nk = len(pair_sums)
    nl = len(pair_sums[0])
    flat = [pair_sums[ki][l] for ki in range(nk) for l in range(nl)]

    def body(*refs):
        src, dst = refs[:nk * nl], refs[nk * nl:nk * nl + nk]
        send_sems, recv_sems = refs[nk * nl + nk:]
        ix, iy, ic = _mesh_pos()
        copies = []
        for ki in range(nk):
            for l in range(nl):
                for k in (1, 2, 3):
                    px, py = _peer_chip(ix, iy, k)
                    sem = (ki * nl + l) * 3 + k - 1
                    rem = pltpu.make_async_remote_copy(
                        src_ref=src[ki * nl + l].at[2 * px + py], dst_ref=dst[ki].at[l, k - 1],
                        send_sem=send_sems.at[sem], recv_sem=recv_sems.at[sem], device_id=(px, py, ic), device_id_type=MESH)
                    rem.start()
                    copies.append(rem)
        for rem in copies:
            rem.wait_send()
            rem.wait_recv()

    return pl.pallas_call(
        body, name=name, in_specs=[ANY] * len(flat), out_specs=tuple([ANY] * nk),
        out_shape=tuple(jax.ShapeDtypeStruct((nl, 3) + pair_sums[ki][0].shape[1:], pair_sums[ki][0].dtype)
                        for ki in range(nk)),
        scratch_shapes=[pltpu.SemaphoreType.DMA((nk * nl * 3,))] * 2,
    )(*flat)


def _rs_chip_start(pieces, name):
    n = len(pieces)

    def body(*refs):
        src, land = refs[:n], refs[n:2 * n]
        send_sems, recv_sems = refs[2 * n], refs[2 * n + 1]
        token = refs[-1]
        ix, iy, ic = _mesh_pos()
        for a in range(n):
            for k in (1, 2, 3):
                px, py = _peer_chip(ix, iy, k)
                pltpu.make_async_remote_copy(
                    src_ref=src[a].at[2 * px + py], dst_ref=land[a].at[k - 1], send_sem=send_sems.at[3 * a + k - 1],
                    recv_sem=recv_sems.at[3 * a + k - 1], device_id=(px, py, ic), device_id_type=MESH).start()
        token[...] = jnp.zeros_like(token)

    land_shape = lambda t: (3,) + t.shape[1:]
    operands = ([pltpu.with_memory_space_constraint(t, pltpu.HBM) for t in pieces]
                + [pltpu.with_memory_space_constraint(lax.empty(land_shape(t), t.dtype), pltpu.HBM) for t in pieces])
    return pl.pallas_call(
        body, name=name,
        out_shape=(pltpu.SemaphoreType.DMA((3 * n,)), pltpu.SemaphoreType.DMA((3 * n,)),
                   *[pltpu.HBM(t.shape, t.dtype) for t in pieces], *[pltpu.HBM(land_shape(t), t.dtype) for t in pieces],
                   jax.ShapeDtypeStruct((8, LANES), F32)),
        in_specs=(HBM,) * (2 * n),
        out_specs=(SEM, SEM) + (HBM,) * (2 * n) + (pl.BlockSpec(memory_space=pltpu.VMEM),),
        input_output_aliases={a: 2 + a for a in range(2 * n)},
        compiler_params=pltpu.CompilerParams(has_side_effects=DATAFLOW),
    )(*operands)


def _rs_chip_wait(started, after, name):
    send_sems, recv_sems = started[0], started[1]
    thru = started[2:-1]
    n = len(thru) // 2

    def body(*refs):
        src, land = refs[:n], refs[n:2 * n]
        send_sems, recv_sems = refs[2 * n], refs[2 * n + 1]
        ix, iy, ic = _mesh_pos()
        for a in range(n):
            for k in (1, 2, 3):
                px, py = _peer_chip(ix, iy, k)
                cp = pltpu.make_async_remote_copy(
                    src_ref=src[a].at[2 * px + py], dst_ref=land[a].at[k - 1], send_sem=send_sems.at[3 * a + k - 1],
                    recv_sem=recv_sems.at[3 * a + k - 1], device_id=(px, py, ic), device_id_type=MESH)
                cp.wait_send()
                cp.wait_recv()

    outs = pl.pallas_call(
        body, name=name,
        out_shape=tuple(pltpu.HBM(t.shape, t.dtype) for t in thru),
        in_specs=(HBM,) * (2 * n) + (SEM, SEM, ANY), out_specs=(HBM,) * (2 * n),
        input_output_aliases={a: a for a in range(2 * n)},
        compiler_params=pltpu.CompilerParams(has_side_effects=DATAFLOW),
    )(*thru, send_sems, recv_sems, after)
    return outs[n:]


def _chip_sum(own, others, chip, name):
    _, pr, pc = own.shape

    def body(chip_ref, own_ref, s1, s2, s3, o_ref):
        o_ref[...] = ((own_ref[...] + s1[...].astype(F32)) + s2[...].astype(F32)) + s3[...].astype(F32)

    slot = lambda k: pl.BlockSpec((None, pr, pc), lambda i, chip_ref: (k, 0, 0))
    return pl.pallas_call(
        body, name=name,
        grid_spec=pltpu.PrefetchScalarGridSpec(
            num_scalar_prefetch=1, grid=(1,),
            in_specs=[pl.BlockSpec((None, pr, pc), lambda i, chip_ref: (chip_ref[0], 0, 0)), slot(0), slot(1), slot(2)],
            out_specs=pl.BlockSpec((pr, pc), lambda i, chip_ref: (0, 0))),
        out_shape=jax.ShapeDtypeStruct((pr, pc), F32), compiler_params=_params(1))(chip, own, others, others, others)


def _rs_pair_share(halves, name):
    nk = len(halves)
    flat = [halves[ki][l] for ki in range(nk) for l in range(DEPTH)]

    def body(*refs):
        src, dst = refs[:nk * DEPTH], refs[nk * DEPTH:nk * DEPTH + nk]
        send_sems, recv_sems = refs[nk * DEPTH + nk:]
        ix, iy, ic = _mesh_pos()
        copies = []
        for ki in range(nk):
            for l in range(DEPTH):
                sem = ki * DEPTH + l
                rem = pltpu.make_async_remote_copy(
                    src_ref=src[sem], dst_ref=dst[ki].at[l], send_sem=send_sems.at[sem], recv_sem=recv_sems.at[sem],
                    device_id=(ix, iy, 1 - ic), device_id_type=MESH)
                rem.start()
                copies.append(rem)
        for rem in copies:
            rem.wait_send()
            rem.wait_recv()

    return pl.pallas_call(
        body, name=name, in_specs=[ANY] * len(flat), out_specs=tuple([ANY] * nk),
        out_shape=tuple(jax.ShapeDtypeStruct((DEPTH,) + halves[ki][0].shape, F32) for ki in range(nk)),
        scratch_shapes=[pltpu.SemaphoreType.DMA((nk * DEPTH,))] * 2,
    )(*flat)


def _adamw_halves(w, mine, theirs, m, v, core, name):
    nl, pr, pc = theirs.shape
    shape = w.shape
    view = lambda t: t.reshape(nl, 2, pr, pc)
    tr = _row_tile(pr, pc, 256 * 1024)

    def body(core_ref, w_ref, a0_ref, a1_ref, t_ref, m_ref, v_ref, g_ref, d_ref, m2_ref, v2_ref):
        own = jnp.where(pl.program_id(0) == 0, a0_ref[...], a1_ref[...])
        g = jnp.where(pl.program_id(1) == core_ref[0], own, t_ref[...])
        g_ref[...] = g
        d_ref[...], m2_ref[...], v2_ref[...] = _adamw_math(w_ref[...], g, m_ref[...], v_ref[...])

    blk = pl.BlockSpec((None, None, tr, pc), lambda l, h, i, core_ref: (l, h, i, 0))
    own_blk = pl.BlockSpec((tr, pc), lambda l, h, i, core_ref: (i, 0))
    out = jax.ShapeDtypeStruct((nl, 2, pr, pc), F32)
    outs = pl.pallas_call(
        body, name=name,
        grid_spec=pltpu.PrefetchScalarGridSpec(
            num_scalar_prefetch=1, grid=(nl, 2, pr // tr),
            in_specs=[blk, own_blk, own_blk, pl.BlockSpec((None, tr, pc), lambda l, h, i, core_ref: (l, i, 0)), blk, blk],
            out_specs=(blk,) * 4),
        out_shape=(out,) * 4, compiler_params=_params(3),
    )(core, view(w), mine[0], mine[1], theirs, view(m), view(v))
    return tuple(t.reshape(shape) for t in outs)


def _rs_first_stages(layer_grads, core, tag, in_flight):
    theirs = _rs_pair_exchange([[g] for g in layer_grads], f"rs_pair_exchange_{tag}")
    sums = [_pair_sum(g, theirs[ki], 0, how, core, f"rs_pair_sum_{kind}")
            for ki, ((kind, how, _, _), g) in enumerate(zip(BIG_KINDS, layer_grads))]
    to_send = [both[1] for both in sums]
    if in_flight:
        return [both[0] for both in sums], _rs_chip_start(to_send, f"rs_chip_start_{tag}")
    slots = _rs_chip_exchange([[t] for t in to_send], f"rs_chip_exchange_{tag}")
    return [both[0] for both in sums], [t[0] for t in slots]


def _rs_last_stages(per_layer, chip):
    halves = [[_chip_sum(per_layer[l][0][ki], per_layer[l][1][ki], chip, f"rs_chip_sum_{kind}") for l in range(DEPTH)]
              for ki, (kind, _, _, _) in enumerate(BIG_KINDS)]
    other = _rs_pair_share(halves, "rs_pair_share")
    return list(zip(halves, other))


WEIGHT_NAMES = ("w_ada", "b_ada", "norm1_w", "w_in", "conv_a_w", "conv_a_b", "ln_a_w", "ln_a_b", "lb_gamma",
                "rec_norm_w", "w_out", "norm2_w", "w_up", "conv_f_w", "w_down", "final_norm_w")
SMALL_PARAMS = (("b_ada", (DEPTH, N_MOD * D_MODEL), None), ("norm1_w", (DEPTH, D_MODEL), None),
                ("conv_a_w", (DEPTH, CONV_WIDTH, CONV_CH), 2), ("conv_a_b", (DEPTH, CONV_CH), None),
                ("ln_a_w", (DEPTH, CONV_CH), None), ("ln_a_b", (DEPTH, CONV_CH), None),
                ("lb_gamma", (DEPTH, 2, REC_WIDTH), 2), ("rec_norm_w", (DEPTH, REC_WIDTH), None),
                ("norm2_w", (DEPTH, D_MODEL), None), ("conv_f_w", (DEPTH, 3, 2 * D_FF), 2),
                ("final_norm_w", (D_MODEL,), None))


def _pack_rows(parts):
    flat = jnp.concatenate([p.reshape(-1) for p in parts])
    total = flat.shape[0]
    padded = -(-total // (8 * LANES)) * (8 * LANES)
    return jnp.pad(flat, (0, padded - total)).reshape(padded // LANES, LANES)


def _unpack(flat, shapes):
    out, off = [], 0
    for shp in shapes:
        size = int(np.prod(shp))
        out.append(flat[off:off + size].reshape(shp))
        off += size
    return out


def _unstack_chips(t, axis):
    return jnp.concatenate([t[j] for j in range(4)], axis=axis)


def kernel(x, c, w_ada, b_ada, norm1_w, w_in, conv_a_w, conv_a_b, ln_a_w, ln_a_b, lb_gamma, rec_norm_w, w_out, norm2_w, w_up, conv_f_w, w_down, final_norm_w, loss_target, m_w_ada, m_b_ada, m_norm1_w, m_w_in, m_conv_a_w, m_conv_a_b, m_ln_a_w, m_ln_a_b, m_lb_gamma, m_rec_norm_w, m_w_out, m_norm2_w, m_w_up, m_conv_f_w, m_w_down, m_final_norm_w, v_w_ada, v_b_ada, v_norm1_w, v_w_in, v_conv_a_w, v_conv_a_b, v_ln_a_w, v_ln_a_b, v_lb_gamma, v_rec_norm_w, v_w_out, v_norm2_w, v_w_up, v_conv_f_w, v_w_down, v_final_norm_w):
    params = dict(zip(WEIGHT_NAMES, (w_ada, b_ada, norm1_w, w_in, conv_a_w, conv_a_b, ln_a_w, ln_a_b, lb_gamma,
                                     rec_norm_w, w_out, norm2_w, w_up, conv_f_w, w_down, final_norm_w)))
    mom1 = dict(zip(WEIGHT_NAMES, (m_w_ada, m_b_ada, m_norm1_w, m_w_in, m_conv_a_w, m_conv_a_b, m_ln_a_w, m_ln_a_b,
                                   m_lb_gamma, m_rec_norm_w, m_w_out, m_norm2_w, m_w_up, m_conv_f_w, m_w_down,
                                   m_final_norm_w)))
    mom2 = dict(zip(WEIGHT_NAMES, (v_w_ada, v_b_ada, v_norm1_w, v_w_in, v_conv_a_w, v_conv_a_b, v_ln_a_w, v_ln_a_b,
                                   v_lb_gamma, v_rec_norm_w, v_w_out, v_norm2_w, v_w_up, v_conv_f_w, v_w_down,
                                   v_final_norm_w)))
    ix, iy, ic = _mesh_pos()
    chip = 2 * ix + iy
    dev = 2 * chip + ic

    c_all = _allgather_devices(c.reshape(8, LANES), "gather_cond").reshape(8, D_MODEL)
    b_sh = lax.dynamic_slice_in_dim(b_ada, chip * ADA_SHARD, ADA_SHARD, axis=1)
    mod_sh = _ada_mod(c_all, w_ada, b_sh.reshape(DEPTH, 1, ADA_SHARD), "ada_mod")
    w_in_b, w_out_b, w_up_b, w_down_b = (t.astype(BF16) for t in (w_in, w_out, w_up, w_down))
    first = _gather_chips([mod_sh, conv_a_w, conv_f_w, lb_gamma, w_in_b[0]], "gather_first")
    later = [w_in_b[1], w_out_b, w_up_b, w_down_b]
    started = _gather_chips_start(later, "gather_rest_start")
    mod_mine = lax.dynamic_index_in_dim(first[0], dev, axis=2, keepdims=False) + started[-1][0, 0]
    mods = [jnp.concatenate([mod_mine[j, l] for j in range(4)]).reshape(N_MOD, D_MODEL) for l in range(DEPTH)]
    conv_a_w_f, conv_f_w_f, gamma_f = (_unstack_chips(first[k], 2) for k in (1, 2, 3))
    w_in0 = _unstack_chips(first[4], 1)

    def later_weights(after):
        own, lands = _gather_chips_wait(started, after, "gather_rest_wait")
        whole = lambda n, axis: jnp.concatenate([jnp.where(chip == j, own[n], lands[n][j]) for j in range(4)], axis=axis)
        return whole(0, 1), whole(1, 1), whole(2, 2), whole(3, 1)

    lb1, p_soft = _lower_bounds(gamma_f.reshape(DEPTH, 2 * REC_WIDTH), "lower_bounds")
    lbs = [jnp.zeros((2, REC_WIDTH), F32), lb1.reshape(2, REC_WIDTH)]
    small = []
    for l in range(DEPTH):
        small.append(dict(norm1_w=norm1_w[l][None], conv_a_w=conv_a_w_f[l], conv_a_b=conv_a_b[l][None],
                          ln_a_w=ln_a_w[l][None], ln_a_b=ln_a_b[l][None], rec_norm_w=rec_norm_w[l],
                          norm2_w=norm2_w[l][None], conv_f_w=conv_f_w_f[l]))

    core_id, chip_id = ic.astype(jnp.int32).reshape(1), chip.astype(jnp.int32).reshape(1)
    reduce_state = [None] * DEPTH

    def on_layer_grads(l, layer_grads):
        in_flight = l > 0
        reduce_state[l] = _rs_first_stages(layer_grads, core_id, f"l{l}", in_flight)
        return reduce_state[l][1][-1][0:1, 0:1] if in_flight else None

    loss, dx, grads, dfw = _sequence_step(x[0], loss_target[0], mods, lbs, small, w_in0, later_weights,
                                          final_norm_w[None], on_layer_grads)
    loss = lax.psum(loss, ("x", "y", "c"))
    for l in range(1, DEPTH):
        sums, started = reduce_state[l]
        reduce_state[l] = (sums, _rs_chip_wait(started, dx, f"rs_chip_wait_l{l}"))

    dgamma = _lower_bounds_bwd(grads[1]["lb"].reshape(1, 2 * REC_WIDTH), p_soft, "lower_bounds_bwd")
    dmod = [jnp.concatenate(grads[l]["mod"], axis=1) for l in range(DEPTH)]
    stack = lambda key: jnp.stack([grads[l][key] for l in range(DEPTH)])
    local_small = dict(b_ada=jnp.concatenate(dmod, axis=0), norm1_w=stack("norm1_w"), conv_a_w=stack("conv_a_w"),
                       conv_a_b=stack("conv_a_b"), ln_a_w=stack("ln_a_w"), ln_a_b=stack("ln_a_b"), lb_gamma=dgamma,
                       rec_norm_w=stack("rec_norm_w"), norm2_w=stack("norm2_w"), conv_f_w=stack("conv_f_w"),
                       final_norm_w=dfw)
    pack = _pack_rows([local_small[name] for name, _, _ in SMALL_PARAMS])
    rows = pack.shape[0]
    packs = _allgather_devices(pack, "gather_small_grads").reshape(8, rows, LANES)
    summed = _sum_devices(packs, "sum_small_grads").reshape(-1)
    small_grads = dict(zip([n for n, _, _ in SMALL_PARAMS], _unpack(summed, [shp for _, shp, _ in SMALL_PARAMS])))

    dmod_all = packs.reshape(8, rows * LANES)[:, :DEPTH * N_MOD * D_MODEL].reshape(8, DEPTH, N_MOD * D_MODEL)
    dmod_sh = lax.dynamic_slice_in_dim(dmod_all, chip * ADA_SHARD, ADA_SHARD, axis=2).transpose(1, 0, 2)
    g_ada, d_ada, m_ada, v_ada = _ada_update(c_all, dmod_sh, w_ada, m_w_ada, v_w_ada, "ada_update")

    for name, shp, axis in SMALL_PARAMS:
        if axis is not None:
            width = shp[axis] // 4
            small_grads[name] = lax.dynamic_slice_in_dim(small_grads[name], chip * width, width, axis=axis)
    names = [n for n, _, _ in SMALL_PARAMS]
    packed = [_pack_rows([src[n] for n in names])[None] for src in (params, small_grads, mom1, mom2)]
    small_out = _adamw(*packed, "adamw_small")
    shapes = [params[n].shape for n in names]
    small_delta, small_m, small_v = (dict(zip(names, _unpack(t.reshape(-1), shapes))) for t in small_out)

    summed_big = _rs_last_stages(reduce_state, chip_id)
    grad, delta, new_m, new_v = dict(small_grads), small_delta, small_m, small_v
    grad["w_ada"], delta["w_ada"], new_m["w_ada"], new_v["w_ada"] = g_ada, d_ada, m_ada, v_ada
    for (name, _, _, _), (mine, theirs) in zip(BIG_KINDS, summed_big):
        grad[name], delta[name], new_m[name], new_v[name] = _adamw_halves(
            params[name], mine, theirs, mom1[name], mom2[name], core_id, f"adamw_{name}")

    return (loss, dx[None], *[grad[n] for n in WEIGHT_NAMES], *[delta[n] for n in WEIGHT_NAMES],
            *[new_m[n] for n in WEIGHT_NAMES], *[new_v[n] for n in WEIGHT_NAMES])
```

```python
import numpy as np
import jax
import jax.numpy as jnp
from jax import lax
from jax.experimental import pallas as pl
from jax.experimental.pallas import tpu as pltpu

F32 = jnp.float32
BF16 = jnp.bfloat16

D_MODEL = 1024
DEPTH = 2
HEAD_DIM = 64
CONV_CH = 256
CONV_WIDTH = 31
ATT_WIDTH = 384
N_HEADS = 6
DILATIONS = (1, 4, 16)
ATT_HALF = 64
ATT_BLOCK = 128
ALIBI_MAX_EXP = 8.0
MASK_VALUE = -1e30
REC_WIDTH = 384
REC_CHUNK = 64
F_TINY = 1e-30
D_FF = 2816
N_MOD = 6
EPS = 1e-6
G_CONV = (0, 512)
G_QKV = (512, 1664)
G_REC = (1664, 3584)
IN_COLS = 3584

ADAM_LR = 0.001
ADAM_B1 = 0.9
ADAM_B2 = 0.999
ADAM_EPS = 1e-08
ADAM_WD = 0.01
ADAM_STEP = 10

VMEM_LIMIT_BYTES = 56 * 1024 * 1024
LANES = 128
MESH = pl.DeviceIdType.MESH
ANY = pl.BlockSpec(memory_space=pl.ANY)


def _params(n_axes):
    return pltpu.CompilerParams(dimension_semantics=("arbitrary",) * n_axes,
                                vmem_limit_bytes=VMEM_LIMIT_BYTES)


def _tile(n, target):
    best = None
    for t in range(LANES, min(n, target) + 1, LANES):
        if n % t == 0:
            best = t
    return best or n


def _sigmoid(x):
    return jax.nn.sigmoid(x)


def _silu_grad(x):
    s = _sigmoid(x)
    return s * (1.0 + x * (1.0 - s))


MM_ACC_ELEMS = 1536 * 1024


def _matmul(a, b, mode, out_dtype, name, tm=1024, tn=1792, tk=1792):
    if mode == "nn":
        (m, k), (k2, n) = a.shape, b.shape
    elif mode == "nt":
        (m, k), (n, k2) = a.shape, b.shape
    else:
        (k, m), (k2, n) = a.shape, b.shape
    assert k == k2, (a.shape, b.shape, mode)
    tn, tk = _tile(n, tn), _tile(k, tk)
    tm = _tile(m, min(tm, MM_ACC_ELEMS // tn))
    nk = k // tk
    a_spec = (pl.BlockSpec((tk, tm), lambda i, j, kk: (kk, i)) if mode == "tn"
              else pl.BlockSpec((tm, tk), lambda i, j, kk: (i, kk)))
    b_spec = (pl.BlockSpec((tn, tk), lambda i, j, kk: (j, kk)) if mode == "nt"
              else pl.BlockSpec((tk, tn), lambda i, j, kk: (kk, j)))
    dims = {"nn": (((1,), (0,)), ((), ())), "nt": (((1,), (1,)), ((), ())),
            "tn": (((0,), (0,)), ((), ()))}[mode]

    def body(a_ref, b_ref, o_ref, *scratch):
        part = lax.dot_general(a_ref[...].astype(BF16), b_ref[...].astype(BF16), dims, preferred_element_type=F32)
        if nk == 1:
            o_ref[...] = part.astype(out_dtype)
            return
        acc_ref, = scratch
        kk = pl.program_id(2)

        @pl.when(kk == 0)
        def _():
            acc_ref[...] = part

        @pl.when(kk > 0)
        def _():
            acc_ref[...] += part

        @pl.when(kk == nk - 1)
        def _():
            o_ref[...] = acc_ref[...].astype(out_dtype)

    return pl.pallas_call(
        body, name=name, grid=(m // tm, n // tn, nk),
        in_specs=[a_spec, b_spec],
        out_specs=pl.BlockSpec((tm, tn), lambda i, j, kk: (i, j)),
        out_shape=jax.ShapeDtypeStruct((m, n), out_dtype),
        scratch_shapes=[pltpu.VMEM((tm, tn), F32)] if nk > 1 else [],
        compiler_params=pltpu.CompilerParams(dimension_semantics=("parallel", "parallel", "arbitrary"),
                                             vmem_limit_bytes=VMEM_LIMIT_BYTES),
    )(a, b)


def _matmul_two_lhs(a1, a2, b, out_dtype, name):
    (m, k1), n = a1.shape, b.shape[0]
    tn, tk = _tile(n, 1792), _tile(k1, 1792)
    tm = _tile(m, min(1024, MM_ACC_ELEMS // tn))
    nk1 = k1 // tk
    nk = 2 * nk1

    def body(a1_ref, a2_ref, b_ref, o_ref, acc_ref):
        kk = pl.program_id(2)
        lhs = jnp.where(kk < nk1, a1_ref[...], a2_ref[...])
        part = lax.dot_general(lhs, b_ref[...], (((1,), (1,)), ((), ())), preferred_element_type=F32)

        @pl.when(kk == 0)
        def _():
            acc_ref[...] = part

        @pl.when(kk > 0)
        def _():
            acc_ref[...] += part

        @pl.when(kk == nk - 1)
        def _():
            o_ref[...] = acc_ref[...].astype(out_dtype)

    return pl.pallas_call(
        body, name=name, grid=(m // tm, n // tn, nk),
        in_specs=[pl.BlockSpec((tm, tk), lambda i, j, kk: (i, jnp.minimum(kk, nk1 - 1))),
                  pl.BlockSpec((tm, tk), lambda i, j, kk: (i, jnp.maximum(kk - nk1, 0))),
                  pl.BlockSpec((tn, tk), lambda i, j, kk: (j, kk))],
        out_specs=pl.BlockSpec((tm, tn), lambda i, j, kk: (i, j)),
        out_shape=jax.ShapeDtypeStruct((m, n), out_dtype),
        scratch_shapes=[pltpu.VMEM((tm, tn), F32)],
        compiler_params=pltpu.CompilerParams(dimension_semantics=("parallel", "parallel", "arbitrary"),
                                             vmem_limit_bytes=VMEM_LIMIT_BYTES),
    )(a1, a2, b)


def _matmul_two_rhs(a, b1, b2, out_dtype, name):
    (k, m), n1 = a.shape, b1.shape[1]
    tn, tk = _tile(n1, 1792), _tile(k, 1792)
    tm = _tile(m, min(1024, MM_ACC_ELEMS // tn))
    nj1, nk = n1 // tn, k // tk

    def body(a_ref, b1_ref, b2_ref, o_ref, acc_ref):
        j, kk = pl.program_id(1), pl.program_id(2)
        rhs = jnp.where(j < nj1, b1_ref[...], b2_ref[...])
        part = lax.dot_general(a_ref[...], rhs, (((0,), (0,)), ((), ())), preferred_element_type=F32)

        @pl.when(kk == 0)
        def _():
            acc_ref[...] = part

        @pl.when(kk > 0)
        def _():
            acc_ref[...] += part

        @pl.when(kk == nk - 1)
        def _():
            o_ref[...] = acc_ref[...].astype(out_dtype)

    return pl.pallas_call(
        body, name=name, grid=(m // tm, 2 * nj1, nk),
        in_specs=[pl.BlockSpec((tk, tm), lambda i, j, kk: (kk, i)),
                  pl.BlockSpec((tk, tn), lambda i, j, kk: (jnp.where(j < nj1, kk, 0), jnp.minimum(j, nj1 - 1))),
                  pl.BlockSpec((tk, tn), lambda i, j, kk: (jnp.where(j < nj1, 0, kk), jnp.maximum(j - nj1, 0)))],
        out_specs=pl.BlockSpec((tm, tn), lambda i, j, kk: (i, j)),
        out_shape=jax.ShapeDtypeStruct((m, 2 * n1), out_dtype),
        scratch_shapes=[pltpu.VMEM((tm, tn), F32)],
        compiler_params=pltpu.CompilerParams(dimension_semantics=("parallel", "parallel", "arbitrary"),
                                             vmem_limit_bytes=VMEM_LIMIT_BYTES),
    )(a, b1, b2)


NORM_ROWS = 256


def _row_spec(width, rows=NORM_ROWS):
    return pl.BlockSpec((rows, width), lambda i: (i, 0))


def _vec_spec(width):
    return pl.BlockSpec((1, width), lambda i: (0, 0))


def _resid_norm_mod(x, r, g, nw, sc, sh, name):
    s, d = x.shape
    has_r = r is not None

    def body(*refs):
        if has_r:
            x_ref, r_ref, g_ref, nw_ref, sc_ref, sh_ref, xn_ref, h_ref = refs
            xn = x_ref[...] + g_ref[...] * r_ref[...].astype(F32)
            xn_ref[...] = xn
        else:
            x_ref, nw_ref, sc_ref, sh_ref, h_ref = refs
            xn = x_ref[...]
        rstd = lax.rsqrt(jnp.mean(xn * xn, axis=-1, keepdims=True) + EPS)
        y = xn * rstd * nw_ref[...]
        h_ref[...] = (y * (1.0 + sc_ref[...]) + sh_ref[...]).astype(BF16)

    if has_r:
        ins, in_specs = (x, r, g, nw, sc, sh), [_row_spec(d), _row_spec(d)] + [_vec_spec(d)] * 4
        out_shape = (jax.ShapeDtypeStruct((s, d), F32), jax.ShapeDtypeStruct((s, d), BF16))
        out_specs = (_row_spec(d), _row_spec(d))
    else:
        ins, in_specs = (x, nw, sc, sh), [_row_spec(d)] + [_vec_spec(d)] * 3
        out_shape = jax.ShapeDtypeStruct((s, d), BF16)
        out_specs = _row_spec(d)
    return pl.pallas_call(body, name=name, grid=(s // NORM_ROWS,), in_specs=in_specs, out_specs=out_specs,
                          out_shape=out_shape, compiler_params=_params(1))(*ins)


def _final_loss(x, r, g, fw, tgt, name):
    s, d = x.shape

    def body(x_ref, r_ref, g_ref, fw_ref, t_ref, loss_ref, dx_ref, dr_ref, dg_ref, dfw_ref):
        @pl.when(pl.program_id(0) == 0)
        def _():
            loss_ref[...] = jnp.zeros_like(loss_ref)
            dg_ref[...] = jnp.zeros_like(dg_ref)
            dfw_ref[...] = jnp.zeros_like(dfw_ref)

        rr = r_ref[...].astype(F32)
        gg = g_ref[...]
        xn = x_ref[...] + gg * rr
        rstd = lax.rsqrt(jnp.mean(xn * xn, axis=-1, keepdims=True) + EPS)
        xh = xn * rstd
        fwv = fw_ref[...]
        e = xh * fwv - t_ref[...]
        loss_ref[...] += 0.5 * jnp.sum(jnp.mean(e * e, axis=-1, keepdims=True), axis=0, keepdims=True)
        dy = e * (1.0 / d)
        dfw_ref[...] += jnp.sum(dy * xh, axis=0, keepdims=True)
        dxh = dy * fwv
        dx = rstd * (dxh - xh * jnp.mean(dxh * xh, axis=-1, keepdims=True))
        dx_ref[...] = dx
        dr_ref[...] = (gg * dx).astype(BF16)
        dg_ref[...] += jnp.sum(dx * rr, axis=0, keepdims=True)

    return pl.pallas_call(
        body, name=name, grid=(s // NORM_ROWS,),
        in_specs=[_row_spec(d), _row_spec(d), _vec_spec(d), _vec_spec(d), _row_spec(d)],
        out_specs=(_vec_spec(LANES), _row_spec(d), _row_spec(d), _vec_spec(d), _vec_spec(d)),
        out_shape=(jax.ShapeDtypeStruct((1, LANES), F32), jax.ShapeDtypeStruct((s, d), F32),
                   jax.ShapeDtypeStruct((s, d), BF16), jax.ShapeDtypeStruct((1, d), F32),
                   jax.ShapeDtypeStruct((1, d), F32)),
        compiler_params=_params(1))(x, r, g, fw, tgt)


def _norm_bwd(x, dhs, dxres, nw, sc, g, r, name):
    s, d = x.shape
    n_dh = len(dhs)
    has_g = g is not None

    def body(*refs):
        x_ref = refs[0]
        dh_refs = refs[1:1 + n_dh]
        dxres_ref, nw_ref, sc_ref = refs[1 + n_dh:4 + n_dh]
        pos = 4 + n_dh
        if has_g:
            g_ref, r_ref = refs[pos:pos + 2]
            pos += 2
            dx_ref, dr_ref, dsh_ref, dsc_ref, dnw_ref, dg_ref = refs[pos:]
            accs = (dsh_ref, dsc_ref, dnw_ref, dg_ref)
        else:
            dx_ref, dsh_ref, dsc_ref, dnw_ref = refs[pos:]
            accs = (dsh_ref, dsc_ref, dnw_ref)

        @pl.when(pl.program_id(0) == 0)
        def _():
            for acc in accs:
                acc[...] = jnp.zeros_like(acc)

        xv = x_ref[...]
        dh = dh_refs[0][...].astype(F32)
        for extra in dh_refs[1:]:
            dh = dh + extra[...].astype(F32)
        rstd = lax.rsqrt(jnp.mean(xv * xv, axis=-1, keepdims=True) + EPS)
        xh = xv * rstd
        nwv = nw_ref[...]
        dsh_ref[...] += jnp.sum(dh, axis=0, keepdims=True)
        dsc_ref[...] += jnp.sum(dh * (xh * nwv), axis=0, keepdims=True)
        dy = dh * (1.0 + sc_ref[...])
        dnw_ref[...] += jnp.sum(dy * xh, axis=0, keepdims=True)
        dxh = dy * nwv
        dx = dxres_ref[...] + rstd * (dxh - xh * jnp.mean(dxh * xh, axis=-1, keepdims=True))
        dx_ref[...] = dx
        if has_g:
            dr_ref[...] = (g_ref[...] * dx).astype(BF16)
            dg_ref[...] += jnp.sum(dx * r_ref[...].astype(F32), axis=0, keepdims=True)

    ins = [x, *dhs, dxres, nw, sc]
    in_specs = [_row_spec(d)] * (2 + n_dh) + [_vec_spec(d)] * 2
    out_shape = [jax.ShapeDtypeStruct((s, d), F32)]
    out_specs = [_row_spec(d)]
    if has_g:
        ins += [g, r]
        in_specs += [_vec_spec(d), _row_spec(d)]
        out_shape.append(jax.ShapeDtypeStruct((s, d), BF16))
        out_specs.append(_row_spec(d))
    n_vec = 4 if has_g else 3
    out_shape += [jax.ShapeDtypeStruct((1, d), F32)] * n_vec
    out_specs += [_vec_spec(d)] * n_vec
    return pl.pallas_call(body, name=name, grid=(s // NORM_ROWS,), in_specs=in_specs, out_specs=tuple(out_specs),
                          out_shape=tuple(out_shape), compiler_params=_params(1))(*ins)


FFN_ROWS = 256
FFN_COLS = 1408
HALO = 16
INV_SQRT2 = 0.7071067811865476
INV_SQRT_2PI = 0.3989422804014327


def _gelu(x):
    return 0.5 * x * (1.0 + lax.erf(x * INV_SQRT2))


def _gelu_grad(x):
    return 0.5 * (1.0 + lax.erf(x * INV_SQRT2)) + x * (INV_SQRT_2PI * jnp.exp(-0.5 * x * x))


def _halo_specs(rows, cols, halo, n_rows_total, col_of):
    per = rows // halo
    last = n_rows_total // halo - 1
    cur = pl.BlockSpec((rows, cols), lambda j, i: (i, col_of(j)))
    prev = pl.BlockSpec((halo, cols), lambda j, i: (jnp.maximum(i * per - 1, 0), col_of(j)))
    nxt = pl.BlockSpec((halo, cols), lambda j, i: (jnp.minimum((i + 1) * per, last), col_of(j)))
    return [prev, cur, nxt]


def _shift_rows(x, k):
    n = x.shape[0]
    return pltpu.roll(x, k % n, axis=0)


def _conv3(ext, w):
    return w[0:1, :] * _shift_rows(ext, 1) + w[1:2, :] * ext + w[2:3, :] * _shift_rows(ext, -1)


def _ext_block(prev_ref, cur_ref, next_ref, i, n_i):
    prev = jnp.where(i > 0, prev_ref[...].astype(F32), 0.0)
    nxt = jnp.where(i < n_i - 1, next_ref[...].astype(F32), 0.0)
    return jnp.concatenate([prev, cur_ref[...].astype(F32), nxt], axis=0)


def _ffn_act(u, cw, name):
    s = u.shape[0]
    nc, ns = D_FF // FFN_COLS, s // FFN_ROWS

    def body(gp, gc, gn, vp, vc, vn, wg_ref, wv_ref, o_ref, cg_ref, cv_ref):
        i = pl.program_id(1)
        cg = _conv3(_ext_block(gp, gc, gn, i, ns), wg_ref[...])[HALO:HALO + FFN_ROWS]
        cv = _conv3(_ext_block(vp, vc, vn, i, ns), wv_ref[...])[HALO:HALO + FFN_ROWS]
        o_ref[...] = (_gelu(cg) * cv).astype(BF16)
        cg_ref[...] = cg.astype(BF16)
        cv_ref[...] = cv.astype(BF16)

    in_specs = (_halo_specs(FFN_ROWS, FFN_COLS, HALO, s, lambda j: j)
                + _halo_specs(FFN_ROWS, FFN_COLS, HALO, s, lambda j: j + nc)
                + [pl.BlockSpec((3, FFN_COLS), lambda j, i: (0, j)),
                   pl.BlockSpec((3, FFN_COLS), lambda j, i: (0, j + nc))])
    blk = pl.BlockSpec((FFN_ROWS, FFN_COLS), lambda j, i: (i, j))
    return pl.pallas_call(
        body, name=name, grid=(nc, ns), in_specs=in_specs, out_specs=(blk, blk, blk),
        out_shape=(jax.ShapeDtypeStruct((s, D_FF), BF16),) * 3, compiler_params=_params(2),
    )(u, u, u, u, u, u, cw, cw)


def _ffn_act_bwd(u, cg, cv, dact, cw, name):
    s = u.shape[0]
    nc, ns = D_FF // FFN_COLS, s // FFN_ROWS

    def body(ug_ref, uv_ref, gp, gc, gn, vp, vc, vn, dp, dc, dn, wg_ref, wv_ref, dug_ref, duv_ref, dwg_ref, dwv_ref):
        i = pl.program_id(1)

        @pl.when(i == 0)
        def _():
            dwg_ref[...] = jnp.zeros_like(dwg_ref)
            dwv_ref[...] = jnp.zeros_like(dwv_ref)

        cge = _ext_block(gp, gc, gn, i, ns)
        cve = _ext_block(vp, vc, vn, i, ns)
        da = _ext_block(dp, dc, dn, i, ns)
        dcg = da * cve * _gelu_grad(cge)
        dcv = da * _gelu(cge)
        inner = slice(HALO, HALO + FFN_ROWS)
        for d_c, u_ref, w_ref, du_ref, dw_ref in ((dcg, ug_ref, wg_ref, dug_ref, dwg_ref),
                                                  (dcv, uv_ref, wv_ref, duv_ref, dwv_ref)):
            w = w_ref[...]
            d_next, d_prev = _shift_rows(d_c, -1), _shift_rows(d_c, 1)
            du = w[0:1, :] * d_next + w[1:2, :] * d_c + w[2:3, :] * d_prev
            du_ref[...] = du[inner].astype(BF16)
            u_in = u_ref[...].astype(F32)
            for tap, d_tap in enumerate((d_next, d_c, d_prev)):
                dw_ref[tap:tap + 1, :] += jnp.sum(d_tap[inner] * u_in, axis=0, keepdims=True)

    blk = pl.BlockSpec((FFN_ROWS, FFN_COLS), lambda j, i: (i, j))
    in_specs = ([blk, pl.BlockSpec((FFN_ROWS, FFN_COLS), lambda j, i: (i, j + nc))]
                + _halo_specs(FFN_ROWS, FFN_COLS, HALO, s, lambda j: j) * 3
                + [pl.BlockSpec((3, FFN_COLS), lambda j, i: (0, j)),
                   pl.BlockSpec((3, FFN_COLS), lambda j, i: (0, j + nc))])
    acc = pl.BlockSpec((HALO, FFN_COLS), lambda j, i: (0, j))
    return pl.pallas_call(
        body, name=name, grid=(nc, ns), in_specs=in_specs, out_specs=(blk, blk, acc, acc),
        out_shape=(jax.ShapeDtypeStruct((s, D_FF), BF16), jax.ShapeDtypeStruct((s, D_FF), BF16),
                   jax.ShapeDtypeStruct((HALO, D_FF), F32), jax.ShapeDtypeStruct((HALO, D_FF), F32)),
        compiler_params=_params(2),
    )(u, u, cg, cg, cg, cv, cv, cv, dact, dact, dact, cw, cw)


CONV_ROWS = 512
CONV_HALO = 16
CONV_PAD = CONV_WIDTH // 2


def _conv_halo_specs(cols, s):
    per = CONV_ROWS // CONV_HALO
    last = s // CONV_HALO - 1
    return [pl.BlockSpec((CONV_HALO, cols), lambda i: (jnp.maximum(i * per - 1, 0), 0)),
            pl.BlockSpec((CONV_ROWS, cols), lambda i: (i, 0)),
            pl.BlockSpec((CONV_HALO, cols), lambda i: (jnp.minimum((i + 1) * per, last), 0))]


def _glu_ext(pp, pc, pn, i, n_i):
    ext = _ext_block(pp, pc, pn, i, n_i)
    return ext[:, :CONV_CH] * _sigmoid(ext[:, CONV_CH:])


def _conv_mixer(pa, cw, cb, lnw, lnb, name):
    s = pa.shape[0]
    ns = s // CONV_ROWS

    def body(pp, pc, pn, cw_ref, cb_ref, lnw_ref, lnb_ref, o_ref, c_ref):
        i = pl.program_id(0)
        a = _glu_ext(pp, pc, pn, i, ns)
        acc = jnp.zeros((CONV_ROWS, CONV_CH), F32)
        for tap in range(CONV_WIDTH):
            acc = acc + cw_ref[tap:tap + 1, :] * _shift_rows(a, -(tap + 1))[:CONV_ROWS]
        cv = acc + cb_ref[...]
        c_ref[...] = cv
        mu = jnp.mean(cv, axis=-1, keepdims=True)
        xc = cv - mu
        rstd = lax.rsqrt(jnp.mean(xc * xc, axis=-1, keepdims=True) + EPS)
        y = xc * rstd * lnw_ref[...] + lnb_ref[...]
        o_ref[...] = (y * _sigmoid(y)).astype(BF16)

    vec = pl.BlockSpec((1, CONV_CH), lambda i: (0, 0))
    blk = pl.BlockSpec((CONV_ROWS, CONV_CH), lambda i: (i, 0))
    return pl.pallas_call(
        body, name=name, grid=(ns,),
        in_specs=_conv_halo_specs(2 * CONV_CH, s) + [pl.BlockSpec((CONV_WIDTH, CONV_CH), lambda i: (0, 0)), vec, vec, vec],
        out_specs=(blk, blk),
        out_shape=(jax.ShapeDtypeStruct((s, CONV_CH), BF16), jax.ShapeDtypeStruct((s, CONV_CH), F32)),
        compiler_params=_params(1))(pa, pa, pa, cw, cb, lnw, lnb)


def _conv_mixer_bwd_ln(cv, dout, lnw, lnb, name):
    s = cv.shape[0]

    def body(c_ref, do_ref, lnw_ref, lnb_ref, dc_ref, dlnw_ref, dlnb_ref, dcb_ref):
        @pl.when(pl.program_id(0) == 0)
        def _():
            dlnw_ref[...] = jnp.zeros_like(dlnw_ref)
            dlnb_ref[...] = jnp.zeros_like(dlnb_ref)
            dcb_ref[...] = jnp.zeros_like(dcb_ref)

        c = c_ref[...]
        mu = jnp.mean(c, axis=-1, keepdims=True)
        xc = c - mu
        rstd = lax.rsqrt(jnp.mean(xc * xc, axis=-1, keepdims=True) + EPS)
        xh = xc * rstd
        w = lnw_ref[...]
        y = xh * w + lnb_ref[...]
        dy = do_ref[...] * _silu_grad(y)
        dlnw_ref[...] += jnp.sum(dy * xh, axis=0, keepdims=True)
        dlnb_ref[...] += jnp.sum(dy, axis=0, keepdims=True)
        dxh = dy * w
        dc = rstd * (dxh - jnp.mean(dxh, axis=-1, keepdims=True) - xh * jnp.mean(dxh * xh, axis=-1, keepdims=True))
        dc_ref[...] = dc
        dcb_ref[...] += jnp.sum(dc, axis=0, keepdims=True)

    vec = pl.BlockSpec((1, CONV_CH), lambda i: (0, 0))
    blk = pl.BlockSpec((CONV_ROWS, CONV_CH), lambda i: (i, 0))
    return pl.pallas_call(
        body, name=name, grid=(s // CONV_ROWS,), in_specs=[blk, blk, vec, vec], out_specs=(blk, vec, vec, vec),
        out_shape=(jax.ShapeDtypeStruct((s, CONV_CH), F32),) + (jax.ShapeDtypeStruct((1, CONV_CH), F32),) * 3,
        compiler_params=_params(1))(cv, dout, lnw, lnb)


def _conv_mixer_bwd_conv(pa, dc, cw, name):
    s = pa.shape[0]
    ns = s // CONV_ROWS

    def body(pc, dp, dcc, dn, cw_ref, dpa_ref, dcw_ref):
        i = pl.program_id(0)

        @pl.when(i == 0)
        def _():
            dcw_ref[...] = jnp.zeros_like(dcw_ref)

        cur = pc[...]
        val, sg = cur[:, :CONV_CH], _sigmoid(cur[:, CONV_CH:])
        a_cur = val * sg
        dce = _ext_block(dp, dcc, dn, i, ns)
        da = jnp.zeros((CONV_ROWS, CONV_CH), F32)
        for tap in range(CONV_WIDTH):
            shifted = _shift_rows(dce, -(CONV_WIDTH - tap))[:CONV_ROWS]
            da = da + cw_ref[tap:tap + 1, :] * shifted
            dcw_ref[tap:tap + 1, :] += jnp.sum(shifted * a_cur, axis=0, keepdims=True)
        dpa_ref[:, :CONV_CH] = (da * sg).astype(BF16)
        dpa_ref[:, CONV_CH:] = (da * val * sg * (1.0 - sg)).astype(BF16)

    return pl.pallas_call(
        body, name=name, grid=(ns,),
        in_specs=[pl.BlockSpec((CONV_ROWS, 2 * CONV_CH), lambda i: (i, 0))] + _conv_halo_specs(CONV_CH, s)
        + [pl.BlockSpec((CONV_WIDTH, CONV_CH), lambda i: (0, 0))],
        out_specs=(pl.BlockSpec((CONV_ROWS, 2 * CONV_CH), lambda i: (i, 0)),
                   pl.BlockSpec((32, CONV_CH), lambda i: (0, 0))),
        out_shape=(jax.ShapeDtypeStruct((s, 2 * CONV_CH), BF16), jax.ShapeDtypeStruct((32, CONV_CH), F32)),
        compiler_params=_params(1))(pa, dc, dc, dc, cw)


SLOPES = tuple(float(2.0 ** (-ALIBI_MAX_EXP * (h + 1) / N_HEADS)) for h in range(N_HEADS))
ATT_SCALE = HEAD_DIM ** -0.5


PAIR = 2 * HEAD_DIM
N_PAIRS = N_HEADS // 2
ATT_WIN = ATT_BLOCK + 2 * ATT_HALF


ATT_GROUPS = {1: 4, 4: 1, 16: 1}


def _window_specs(dil, n_steps, col_of):
    per = 2 * ATT_GROUPS[dil]
    rows, halo = ATT_BLOCK * dil * ATT_GROUPS[dil], ATT_HALF * dil
    return [pl.BlockSpec((halo, PAIR), lambda i, p: (jnp.maximum(per * i - 1, 0), col_of(p))),
            pl.BlockSpec((rows, PAIR), lambda i, p: (i, col_of(p))),
            pl.BlockSpec((halo, PAIR), lambda i, p: (jnp.minimum(per * (i + 1), per * n_steps - 1), col_of(p)))]


def _residue(ref, r, n, dil, start=0):
    return ref[pl.ds(start * dil + r, n, stride=dil), :] if dil > 1 else ref[pl.ds(start + r, n), :]


def _store_residue(ref, r, dil, start, val):
    if dil > 1:
        ref[pl.ds(start * dil + r, val.shape[0], stride=dil), :] = val
    else:
        ref[pl.ds(start + r, val.shape[0]), :] = val


def _residue_window(refs, r, dil, g=0):
    prev, cur, nxt = refs
    groups = ATT_GROUPS[dil]
    lo = max(g * ATT_BLOCK - ATT_HALF, 0)
    hi = min((g + 1) * ATT_BLOCK + ATT_HALF, groups * ATT_BLOCK)
    parts = [_residue(prev, r, ATT_HALF, dil)] if g == 0 else []
    parts.append(_residue(cur, r, hi - lo, dil, lo))
    if g == groups - 1:
        parts.append(_residue(nxt, r, ATT_HALF, dil))
    return jnp.concatenate(parts, axis=0)


def _band_masks(i, length, dil, transposed):
    shape = (ATT_WIN, ATT_BLOCK) if transposed else (ATT_BLOCK, ATT_WIN)
    row = lax.broadcasted_iota(jnp.int32, shape, 0)
    col = lax.broadcasted_iota(jnp.int32, shape, 1)
    wide = row if transposed else col
    dist = jnp.abs((row - col - ATT_HALF) if transposed else (row + ATT_HALF - col))
    wpos = i * ATT_BLOCK - ATT_HALF + wide
    valid = (dist <= ATT_HALF) & (wpos >= 0) & (wpos < length)
    return valid, dist.astype(F32) * float(dil)


def _attn_branch(qkv, dil, name):
    s = qkv.shape[0]
    groups = ATT_GROUPS[dil]
    rows = ATT_BLOCK * dil * groups
    n_steps = s // rows
    length = s // dil
    nt = (((1,), (1,)), ((), ()))

    def body(q_ref, kp, kc, kn, vp, vc, vn, o_ref, l_ref):
        i, pair = pl.program_id(0), pl.program_id(1)
        items = [(g, r) for g in range(groups) for r in range(dil)]
        q = jnp.stack([_residue(q_ref, r, ATT_BLOCK, dil, g * ATT_BLOCK) for g, r in items]).astype(BF16)
        k = jnp.stack([_residue_window((kp, kc, kn), r, dil, g) for g, r in items]).astype(BF16)
        v = jnp.stack([_residue_window((vp, vc, vn), r, dil, g) for g, r in items]).astype(BF16)
        per_group = [_band_masks(i * groups + g, length, dil, False) for g in range(groups)]
        valid = jnp.stack([per_group[g][0] for g, _ in items]) if groups > 1 else per_group[0][0][None]
        distf = jnp.stack([per_group[g][1] for g, _ in items]) if groups > 1 else per_group[0][1][None]
        outs, lses = [], []
        for hh in range(2):
            sl = slice(hh * HEAD_DIM, (hh + 1) * HEAD_DIM)
            slope = jnp.where(pair == 0, SLOPES[hh], jnp.where(pair == 1, SLOPES[2 + hh], SLOPES[4 + hh]))
            sc = jnp.einsum("bqd,bkd->bqk", q[:, :, sl], k[:, :, sl], preferred_element_type=F32) * ATT_SCALE
            sc = jnp.where(valid, sc - slope * distf, MASK_VALUE)
            m = jnp.max(sc, axis=-1, keepdims=True)
            p = jnp.exp(sc - m)
            den = jnp.sum(p, axis=-1, keepdims=True)
            outs.append(jnp.einsum("bqk,bkd->bqd", p.astype(BF16), v[:, :, sl], preferred_element_type=F32) / den)
            lses.append(jnp.broadcast_to(m + jnp.log(den), (len(items), ATT_BLOCK, HEAD_DIM)))
        o_all, l_all = jnp.concatenate(outs, axis=2), jnp.concatenate(lses, axis=2)
        for n, (g, r) in enumerate(items):
            _store_residue(o_ref, r, dil, g * ATT_BLOCK, o_all[n])
            _store_residue(l_ref, r, dil, g * ATT_BLOCK, l_all[n])

    out_blk = pl.BlockSpec((rows, PAIR), lambda i, p: (i, p))
    return pl.pallas_call(
        body, name=name, grid=(n_steps, N_PAIRS),
        in_specs=[pl.BlockSpec((rows, PAIR), lambda i, p: (i, p))]
        + _window_specs(dil, n_steps, lambda p: N_PAIRS + p) + _window_specs(dil, n_steps, lambda p: 2 * N_PAIRS + p),
        out_specs=(out_blk, out_blk),
        out_shape=(jax.ShapeDtypeStruct((s, ATT_WIDTH), F32),) * 2,
        compiler_params=_params(2))(qkv, qkv, qkv, qkv, qkv, qkv, qkv)


ATT_ROWS = 512


def _attn_combine(outs, lses, name):
    s = outs[0].shape[0]

    def body(o1, o2, o3, l1, l2, l3, att_ref, att32_ref, lse_ref):
        ls = [l1[...], l2[...], l3[...]]
        m = jnp.maximum(jnp.maximum(ls[0], ls[1]), ls[2])
        es = [jnp.exp(l - m) for l in ls]
        den = es[0] + es[1] + es[2]
        att = (es[0] * o1[...] + es[1] * o2[...] + es[2] * o3[...]) / den
        att_ref[...] = att.astype(BF16)
        att32_ref[...] = att
        lse_ref[...] = m + jnp.log(den)

    blk = pl.BlockSpec((ATT_ROWS, ATT_WIDTH), lambda i: (i, 0))
    return pl.pallas_call(
        body, name=name, grid=(s // ATT_ROWS,), in_specs=[blk] * 6, out_specs=(blk, blk, blk),
        out_shape=(jax.ShapeDtypeStruct((s, ATT_WIDTH), BF16), jax.ShapeDtypeStruct((s, ATT_WIDTH), F32),
                   jax.ShapeDtypeStruct((s, ATT_WIDTH), F32)),
        compiler_params=_params(1))(*outs, *lses)


def _attn_delta(datt, att, name):
    s = att.shape[0]

    def body(d_ref, a_ref, delta_ref):
        prod = d_ref[...] * a_ref[...]
        for h in range(N_HEADS):
            sl = slice(h * HEAD_DIM, (h + 1) * HEAD_DIM)
            delta_ref[:, sl] = jnp.broadcast_to(jnp.sum(prod[:, sl], axis=-1, keepdims=True), (ATT_ROWS, HEAD_DIM))

    blk = pl.BlockSpec((ATT_ROWS, ATT_WIDTH), lambda i: (i, 0))
    return pl.pallas_call(
        body, name=name, grid=(s // ATT_ROWS,), in_specs=[blk, blk], out_specs=blk,
        out_shape=jax.ShapeDtypeStruct((s, ATT_WIDTH), F32), compiler_params=_params(1))(datt, att)


def _attn_branch_bwd(qkv, do, lse, delta, prev, dil, name):
    s = qkv.shape[0]
    groups = ATT_GROUPS[dil]
    rows = ATT_BLOCK * dil * groups
    n_steps = s // rows
    length = s // dil
    has_prev = prev is not None
    tn = (((0,), (0,)), ((), ()))
    nt = (((1,), (1,)), ((), ()))

    def body(*refs):
        qs, ks, vs, dos, ls, des = (refs[3 * n:3 * n + 3] for n in range(6))
        rest = refs[18:]
        if has_prev:
            pq, pk, pv = rest[:3]
            rest = rest[3:]
        dq_ref, dk_ref, dv_ref = rest
        i, pair = pl.program_id(0), pl.program_id(1)
        items = [(g, r) for g in range(groups) for r in range(dil)]
        cur = lambda t: jnp.stack([_residue(t[1], r, ATT_BLOCK, dil, g * ATT_BLOCK) for g, r in items])
        win = lambda t: jnp.stack([_residue_window(t, r, dil, g) for g, r in items])
        q_cur, k_cur, v_cur, do_cur = (cur(t).astype(BF16) for t in (qs, ks, vs, dos))
        q_win, k_win, v_win, do_win = (win(t).astype(BF16) for t in (qs, ks, vs, dos))
        l_cur, de_cur, l_win, de_win = cur(ls), cur(des), win(ls), win(des)

        def masks(transposed):
            per_group = [_band_masks(i * groups + g, length, dil, transposed) for g in range(groups)]
            if groups == 1:
                return per_group[0][0][None], per_group[0][1][None]
            return jnp.stack([per_group[g][0] for g, _ in items]), jnp.stack([per_group[g][1] for g, _ in items])

        valid_q, distf_q = masks(False)
        valid_k, distf_k = masks(True)
        dot = lambda eq, a, b: jnp.einsum(eq, a, b, preferred_element_type=F32)
        dqs, dks, dvs = [], [], []
        for hh in range(2):
            sl = slice(hh * HEAD_DIM, (hh + 1) * HEAD_DIM)
            one = slice(hh * HEAD_DIM, hh * HEAD_DIM + 1)
            slope = jnp.where(pair == 0, SLOPES[hh], jnp.where(pair == 1, SLOPES[2 + hh], SLOPES[4 + hh]))
            sc = dot("bqd,bkd->bqk", q_cur[:, :, sl], k_win[:, :, sl]) * ATT_SCALE - slope * distf_q
            p = jnp.exp(jnp.where(valid_q, sc - l_cur[:, :, one], MASK_VALUE))
            dp = dot("bqd,bkd->bqk", do_cur[:, :, sl], v_win[:, :, sl])
            ds = (p * (dp - de_cur[:, :, one]) * ATT_SCALE).astype(BF16)
            dqs.append(dot("bqk,bkd->bqd", ds, k_win[:, :, sl]))

            sc2 = dot("bqd,bkd->bqk", q_win[:, :, sl], k_cur[:, :, sl]) * ATT_SCALE - slope * distf_k
            p2 = jnp.exp(jnp.where(valid_k, sc2 - l_win[:, :, one], MASK_VALUE))
            dvs.append(dot("bqk,bqd->bkd", p2.astype(BF16), do_win[:, :, sl]))
            dp2 = dot("bqd,bkd->bqk", do_win[:, :, sl], v_cur[:, :, sl])
            ds2 = (p2 * (dp2 - de_win[:, :, one]) * ATT_SCALE).astype(BF16)
            dks.append(dot("bqk,bqd->bkd", ds2, q_win[:, :, sl]))
        for parts, acc, out in ((dqs, pq if has_prev else None, dq_ref), (dks, pk if has_prev else None, dk_ref),
                                (dvs, pv if has_prev else None, dv_ref)):
            val = jnp.concatenate(parts, axis=2)
            for n, (g, r) in enumerate(items):
                piece = val[n]
                if has_prev:
                    piece = piece + _residue(acc, r, ATT_BLOCK, dil, g * ATT_BLOCK)
                _store_residue(out, r, dil, g * ATT_BLOCK, piece)

    blk = pl.BlockSpec((rows, PAIR), lambda i, p: (i, p))
    in_specs = (_window_specs(dil, n_steps, lambda p: p) + _window_specs(dil, n_steps, lambda p: N_PAIRS + p)
                + _window_specs(dil, n_steps, lambda p: 2 * N_PAIRS + p) + _window_specs(dil, n_steps, lambda p: p) * 3)
    ins = [qkv] * 9 + [do] * 3 + [lse] * 3 + [delta] * 3
    if has_prev:
        in_specs += [blk] * 3
        ins += list(prev)
    return pl.pallas_call(
        body, name=name, grid=(n_steps, N_PAIRS), in_specs=in_specs, out_specs=(blk, blk, blk),
        out_shape=(jax.ShapeDtypeStruct((s, ATT_WIDTH), F32),) * 3,
        compiler_params=_params(2))(*ins)


TB = 2 * REC_CHUNK
REC_ROWS = 5 * REC_WIDTH


REC_LEVELS = 6


def _scan_pos(p, rev):
    p = p & (REC_CHUNK - 1)
    return (REC_CHUNK - 1 - p) if rev else p


def _split3(x):
    hi = x.astype(BF16)
    rest = x - hi.astype(F32)
    mid = rest.astype(BF16)
    return hi, mid, (rest - mid.astype(F32)).astype(BF16)


def _chunk_sums(x, rev, with_levels):
    row = lax.broadcasted_iota(jnp.int32, (TB, TB), 0)
    col = lax.broadcasted_iota(jnp.int32, (TB, TB), 1)
    same = (row < REC_CHUNK) == (col < REC_CHUNK)
    s_row, s_col = _scan_pos(row, rev), _scan_pos(col, rev)
    mats = [same & (s_row <= s_col)]
    if with_levels:
        for level in range(1, REC_LEVELS + 1):
            shift = REC_LEVELS + 1 - level
            boundary = ((s_col >> shift) << shift) + (REC_CHUNK >> level) - 1
            mats.append(same & (s_row <= boundary))
        mats.append(same)
    cat = jnp.concatenate([m.astype(BF16) for m in mats], axis=1)
    total = sum(jnp.dot(term, cat, preferred_element_type=F32) for term in _split3(x))
    return [total[:, n * TB:(n + 1) * TB] for n in range(len(mats))]


def _hg_prep(qraw, z, lb, rev):
    lane = lax.broadcasted_iota(jnp.int32, (REC_WIDTH, TB), 1)
    in_a = lane < REC_CHUNK
    scan = _scan_pos(lane, rev)
    sig, sigm = _sigmoid(z), _sigmoid(-z)
    f = lb + (1.0 - lb) * sig
    kk = (1.0 - lb) * sigm
    sums = _chunk_sums(jnp.log(jnp.maximum(f, F_TINY)), rev, True)
    b, bend = sums[0], sums[-1]
    q = qraw * _sigmoid(qraw)
    eq, ek = [], []
    for level in range(1, REC_LEVELS + 1):
        r = sums[level]
        e = jnp.exp(jnp.minimum(b - r, r - b))
        second = ((scan >> (REC_LEVELS - level)) & 1) == 1
        eq.append(jnp.where(second, e, 0.0))
        ek.append(jnp.where(second, 0.0, e))
    lanes_end = (0, REC_CHUNK) if rev else (REC_CHUNK - 1, TB - 1)
    end_a, end_b = (b[:, n:n + 1] for n in lanes_end)
    return dict(in_a=in_a, sig=sig, sigm=sigm, f=f, kk=kk, b=b, end_a=end_a, end_b=end_b,
                q=q, qh=q * jnp.exp(b), kh=kk * jnp.exp(bend - b), ekb=jnp.exp(bend - b), eq=eq, ek=ek)


def _level_masks(rev):
    row = lax.broadcasted_iota(jnp.int32, (TB, TB), 0)
    col = lax.broadcasted_iota(jnp.int32, (TB, TB), 1)
    same = (row < REC_CHUNK) == (col < REC_CHUNK)
    s_row, s_col = _scan_pos(row, rev), _scan_pos(col, rev)
    masks = [same & ((s_row >> (REC_LEVELS + 1 - level)) == (s_col >> (REC_LEVELS + 1 - level)))
             for level in range(1, REC_LEVELS + 1)]
    return masks, row == col


def _head_rows(x, h):
    return x[h * HEAD_DIM:(h + 1) * HEAD_DIM, :]


def _block_diag_mask():
    r = lax.broadcasted_iota(jnp.int32, (REC_WIDTH, REC_WIDTH), 0) // HEAD_DIM
    c = lax.broadcasted_iota(jnp.int32, (REC_WIDTH, REC_WIDTH), 1) // HEAD_DIM
    return (r == c).astype(F32)


def _heads(x):
    return x.reshape(N_HEADS, HEAD_DIM, TB)


def _hgrn_scan(projt, lb, rev, name):
    s = projt.shape[1]
    nblk = s // TB
    zrow = 2 if rev else 1
    tmap = (lambda i: nblk - 1 - i) if rev else (lambda i: i)
    tn = (((0,), (0,)), ((), ()))
    nt = (((1,), (1,)), ((), ()))

    def body(q_ref, z_ref, v_ref, lb_ref, o_ref, hs_ref, at_ref, h_ref):
        @pl.when(pl.program_id(0) == 0)
        def _():
            h_ref[...] = jnp.zeros_like(h_ref)

        v = v_ref[...]
        vb = v.astype(BF16)
        pr = _hg_prep(q_ref[...], z_ref[...], lb_ref[...], rev)
        q, kk = pr["q"], pr["kk"]
        masks, diag = _level_masks(rev)
        own = jnp.sum(_heads(q * kk), axis=1, keepdims=True)
        sc = jnp.where(diag[None], own, 0.0)
        for level in range(REC_LEVELS):
            qt = _heads((q * pr["eq"][level]).astype(BF16))
            kt = _heads((kk * pr["ek"][level]).astype(BF16))
            sc = sc + jnp.where(masks[level][None],
                                jnp.einsum("hks,hkt->hst", kt, qt, preferred_element_type=F32), 0.0)
        a_bf = sc.astype(BF16)
        at_ref[...] = a_bf
        o = jnp.einsum("hvs,hst->hvt", _heads(vb), a_bf, preferred_element_type=F32).reshape(REC_WIDTH, TB)
        bd_mask = _block_diag_mask()
        order = ((1, ~pr["in_a"], pr["end_b"]), (0, pr["in_a"], pr["end_a"]))
        if not rev:
            order = order[::-1]
        for slot, msk, bend in order:
            h0 = h_ref[...]
            hs_ref[slot] = h0
            o = o + lax.dot_general(h0.astype(BF16), jnp.where(msk, pr["qh"], 0.0).astype(BF16), tn,
                                    preferred_element_type=F32)
            upd = lax.dot_general(jnp.where(msk, pr["kh"], 0.0).astype(BF16), vb, nt, preferred_element_type=F32)
            h_ref[...] = jnp.exp(bend) * h0 + upd * bd_mask
        o_ref[...] = o

    row_blk = lambda r: pl.BlockSpec((REC_WIDTH, TB), lambda i: (r, tmap(i)))
    return pl.pallas_call(
        body, name=name, grid=(nblk,),
        in_specs=[row_blk(0), row_blk(zrow), row_blk(3), pl.BlockSpec((REC_WIDTH, 1), lambda i: (0, 0))],
        out_specs=(pl.BlockSpec((REC_WIDTH, TB), lambda i: (0, tmap(i))),
                   pl.BlockSpec((2, REC_WIDTH, REC_WIDTH), lambda i: (tmap(i), 0, 0)),
                   pl.BlockSpec((None, N_HEADS, TB, TB), lambda i: (tmap(i), 0, 0, 0))),
        out_shape=(jax.ShapeDtypeStruct((REC_WIDTH, s), F32),
                   jax.ShapeDtypeStruct((s // REC_CHUNK, REC_WIDTH, REC_WIDTH), F32),
                   jax.ShapeDtypeStruct((nblk, N_HEADS, TB, TB), BF16)),
        scratch_shapes=[pltpu.VMEM((REC_WIDTH, REC_WIDTH), F32)],
        compiler_params=_params(1))(projt, projt, projt, lb)


def _hgrn_scan_bwd(projt, lb, dot, hs, at, prev, rev, name):
    s = projt.shape[1]
    nblk = s // TB
    zrow = 2 if rev else 1
    tmap = (lambda i: i) if rev else (lambda i: nblk - 1 - i)
    has_prev = prev is not None
    tn = (((0,), (0,)), ((), ()))
    nt = (((1,), (1,)), ((), ()))

    def body(*refs):
        q_ref, z_ref, v_ref, lb_ref, do_ref, hs_ref, at_ref = refs[:7]
        rest = refs[7:]
        if has_prev:
            pq_ref, pv_ref = rest[:2]
            rest = rest[2:]
        dq_ref, dz_ref, dv_ref, dlb_ref, dh_ref = rest

        @pl.when(pl.program_id(0) == 0)
        def _():
            dh_ref[...] = jnp.zeros_like(dh_ref)
            dlb_ref[...] = jnp.zeros_like(dlb_ref)

        qraw, v, do, lbv = q_ref[...], v_ref[...], do_ref[...], lb_ref[...]
        dob, vb = do.astype(BF16), v.astype(BF16)
        pr = _hg_prep(qraw, z_ref[...], lbv, rev)
        q, kk, b, in_a = pr["q"], pr["kk"], pr["b"], pr["in_a"]
        masks, diag = _level_masks(rev)
        dot = lambda eq, x, y: jnp.einsum(eq, x, y, preferred_element_type=F32)
        d_at = dot("hvs,hvt->hst", _heads(vb), _heads(dob))
        dv = dot("hvt,hst->hvs", _heads(dob), at_ref[...]).reshape(REC_WIDTH, TB)
        d_own = jnp.sum(jnp.where(diag[None], d_at, 0.0), axis=1, keepdims=True)
        dq_in = (d_own * _heads(kk)).reshape(REC_WIDTH, TB)
        dk_in = (d_own * _heads(q)).reshape(REC_WIDTH, TB)
        db_in = jnp.zeros((REC_WIDTH, TB), F32)
        for lv in range(REC_LEVELS):
            d_lv = jnp.where(masks[lv][None], d_at, 0.0).astype(BF16)
            q_lv, k_lv = (q * pr["eq"][lv]).astype(BF16), (kk * pr["ek"][lv]).astype(BF16)
            dqt = dot("hks,hst->hkt", _heads(k_lv), d_lv).reshape(REC_WIDTH, TB)
            dkt = dot("hkt,hst->hks", _heads(q_lv), d_lv).reshape(REC_WIDTH, TB)
            dq_in = dq_in + pr["eq"][lv] * dqt
            dk_in = dk_in + pr["ek"][lv] * dkt
            db_in = db_in + q_lv.astype(F32) * dqt - k_lv.astype(F32) * dkt
        dq = dk = jnp.zeros((REC_WIDTH, TB), F32)

        zero = jnp.zeros((REC_WIDTH, TB), F32)
        bd_mask = _block_diag_mask()
        eb = jnp.exp(b)
        const = zero
        order = ((0, in_a, pr["end_a"]), (1, ~in_a, pr["end_b"]))
        if not rev:
            order = order[::-1]
        for slot, msk, bend in order:
            h0 = hs_ref[slot]
            dh1 = dh_ref[...]
            dh1b = dh1.astype(BF16)
            dq = dq + eb * jnp.dot(h0.astype(BF16), jnp.where(msk, do, 0.0).astype(BF16), preferred_element_type=F32)
            dv = dv + lax.dot_general(dh1b, jnp.where(msk, pr["kh"], 0.0).astype(BF16), tn, preferred_element_type=F32)
            dk_int = pr["ekb"] * jnp.dot(dh1b, jnp.where(msk, v, 0.0).astype(BF16), preferred_element_type=F32)
            dk = dk + dk_int
            ebend = jnp.exp(bend)
            c = (jnp.sum(kk * dk_int, axis=1, keepdims=True)
                 + ebend * jnp.sum(h0 * dh1, axis=1, keepdims=True))
            const = const + jnp.where(msk, c, 0.0)
            upd = lax.dot_general(jnp.where(msk, pr["qh"], 0.0).astype(BF16), dob, nt, preferred_element_type=F32)
            dh_ref[...] = ebend * dh1 + upd * bd_mask

        dg = _chunk_sums(db_in + q * dq - kk * dk, not rev, False)[0] + const
        dq, dk = dq + dq_in, dk + dk_in
        sig, sigm, f = pr["sig"], pr["sigm"], pr["f"]
        live = f > F_TINY
        inv_f = 1.0 / jnp.maximum(f, F_TINY)
        one_lb = 1.0 - lbv
        dz = sig * sigm * one_lb * (jnp.where(live, dg * inv_f, 0.0) - dk)
        dlb_ref[...] += jnp.sum(sigm * (jnp.where(live, dg * inv_f, 0.0) - dk), axis=1, keepdims=True)
        dqr = dq * _silu_grad(qraw)
        if has_prev:
            dqr = dqr + pq_ref[...]
            dv = dv + pv_ref[...]
        dq_ref[...] = dqr
        dz_ref[...] = dz
        dv_ref[...] = dv

    row_blk = lambda r: pl.BlockSpec((REC_WIDTH, TB), lambda i: (r, tmap(i)))
    blk = pl.BlockSpec((REC_WIDTH, TB), lambda i: (0, tmap(i)))
    col = pl.BlockSpec((REC_WIDTH, 1), lambda i: (0, 0))
    in_specs = [row_blk(0), row_blk(zrow), row_blk(3), col, blk,
                pl.BlockSpec((2, REC_WIDTH, REC_WIDTH), lambda i: (tmap(i), 0, 0)),
                pl.BlockSpec((None, N_HEADS, TB, TB), lambda i: (tmap(i), 0, 0, 0))]
    ins = [projt, projt, projt, lb, dot, hs, at]
    if has_prev:
        in_specs += [blk, blk]
        ins += list(prev)
    t_shape = jax.ShapeDtypeStruct((REC_WIDTH, s), F32)
    return pl.pallas_call(
        body, name=name, grid=(nblk,), in_specs=in_specs, out_specs=(blk, blk, blk, col),
        out_shape=(t_shape, t_shape, t_shape, jax.ShapeDtypeStruct((REC_WIDTH, 1), F32)),
        scratch_shapes=[pltpu.VMEM((REC_WIDTH, REC_WIDTH), F32)],
        compiler_params=_params(1))(*ins)


REC_OUT_COLS = 512


def _head_rms(o):
    o3 = o.reshape(N_HEADS, HEAD_DIM, o.shape[1])
    rstd = lax.rsqrt(jnp.mean(o3 * o3, axis=1, keepdims=True) + EPS)
    return o3 * rstd, rstd


def _hgrn_out(of, ob, projt, wn, name):
    s = of.shape[1]

    def body(of_ref, ob_ref, g_ref, wn_ref, o_ref):
        on, _ = _head_rms(of_ref[...] + ob_ref[...])
        g = g_ref[...]
        y = on.reshape(REC_WIDTH, REC_OUT_COLS) * wn_ref[...] * (g * _sigmoid(g))
        o_ref[...] = y.T.astype(BF16)

    blk = pl.BlockSpec((REC_WIDTH, REC_OUT_COLS), lambda i: (0, i))
    return pl.pallas_call(
        body, name=name, grid=(s // REC_OUT_COLS,),
        in_specs=[blk, blk, pl.BlockSpec((REC_WIDTH, REC_OUT_COLS), lambda i: (4, i)),
                  pl.BlockSpec((REC_WIDTH, 1), lambda i: (0, 0))],
        out_specs=pl.BlockSpec((REC_OUT_COLS, REC_WIDTH), lambda i: (i, 0)),
        out_shape=jax.ShapeDtypeStruct((s, REC_WIDTH), BF16), compiler_params=_params(1))(of, ob, projt, wn)


def _hgrn_out_bwd(drec, of, ob, projt, wn, name):
    s = of.shape[1]

    def body(d_ref, of_ref, ob_ref, g_ref, wn_ref, do_ref, dg_ref, dwn_ref):
        @pl.when(pl.program_id(0) == 0)
        def _():
            dwn_ref[...] = jnp.zeros_like(dwn_ref)

        dy = d_ref[...].T
        on3, rstd = _head_rms(of_ref[...] + ob_ref[...])
        on = on3.reshape(REC_WIDTH, REC_OUT_COLS)
        g, wnv = g_ref[...], wn_ref[...]
        dg_ref[...] = dy * on * wnv * _silu_grad(g)
        d_onw = dy * (g * _sigmoid(g))
        dwn_ref[...] += jnp.sum(d_onw * on, axis=1, keepdims=True)
        d_on3 = (d_onw * wnv).reshape(N_HEADS, HEAD_DIM, REC_OUT_COLS)
        do3 = rstd * (d_on3 - on3 * jnp.mean(d_on3 * on3, axis=1, keepdims=True))
        do_ref[...] = do3.reshape(REC_WIDTH, REC_OUT_COLS)

    blk = pl.BlockSpec((REC_WIDTH, REC_OUT_COLS), lambda i: (0, i))
    col = pl.BlockSpec((REC_WIDTH, 1), lambda i: (0, 0))
    t_shape = jax.ShapeDtypeStruct((REC_WIDTH, s), F32)
    return pl.pallas_call(
        body, name=name, grid=(s // REC_OUT_COLS,),
        in_specs=[pl.BlockSpec((REC_OUT_COLS, REC_WIDTH), lambda i: (i, 0)), blk, blk,
                  pl.BlockSpec((REC_WIDTH, REC_OUT_COLS), lambda i: (4, i)), col],
        out_specs=(blk, blk, col),
        out_shape=(t_shape, t_shape, jax.ShapeDtypeStruct((REC_WIDTH, 1), F32)),
        compiler_params=_params(1))(drec, of, ob, projt, wn)


def _lower_bounds(gamma, name):
    def body(g_ref, lb_ref, p_ref):
        g0, g1 = g_ref[0:1, :], g_ref[1:2, :]
        m = jnp.maximum(g0, g1)
        e0, e1 = jnp.exp(g0 - m), jnp.exp(g1 - m)
        p0, p1 = e0 / (e0 + e1), e1 / (e0 + e1)
        lb_ref[...] = (p0 + p1) - p0
        p_ref[0:1, :] = p0
        p_ref[1:2, :] = p1

    n = gamma.shape[1]
    return pl.pallas_call(body, name=name,
                          out_shape=(jax.ShapeDtypeStruct((1, n), F32), jax.ShapeDtypeStruct((2, n), F32)))(gamma)


def _lower_bounds_bwd(dlb1, p, name):
    def body(d_ref, p_ref, o_ref):
        p0, p1, d = p_ref[0:1, :], p_ref[1:2, :], d_ref[...]
        inner = p1 * d
        o_ref[0:1, :] = p0 * (0.0 - inner)
        o_ref[1:2, :] = p1 * (d - inner)

    return pl.pallas_call(body, name=name, out_shape=jax.ShapeDtypeStruct(p.shape, F32))(dlb1, p)


def _split_w_in(w_in):
    return dict(conv=w_in[:, G_CONV[0]:G_CONV[1]], qkv=w_in[:, G_QKV[0]:G_QKV[1]],
                rec_t=w_in[:, G_REC[0]:].T, nat=w_in[:, :G_REC[0]])


def _split_w_rest(w_out, w_up, w_down):
    return dict(out=w_out, out_a=w_out[:CONV_CH], out_b=w_out[CONV_CH:CONV_CH + ATT_WIDTH],
                out_c=w_out[CONV_CH + ATT_WIDTH:], up=w_up, down=w_down)


def _col(v):
    return v.reshape(-1, 1)


def _sequence_step(x, tgt, mods, lbs, small, w_in0, later_weights, final_w, on_layer_grads):
    saved = []
    xin = x
    big = [_split_w_in(w_in0), None]
    h1 = _resid_norm_mod(x, None, None, small[0]["norm1_w"], mods[0][1:2], mods[0][0:1], "norm1_first")
    for l in range(DEPTH):
        sm, w, md = small[l], big[l], mods[l]
        pa = _matmul(h1, w["conv"], "nn", F32, f"proj_conv")
        qkv = _matmul(h1, w["qkv"], "nn", F32, f"proj_qkv")
        projt = _matmul(w["rec_t"], h1, "nt", F32, f"proj_rec")
        a_out, cv = _conv_mixer(pa, sm["conv_a_w"], sm["conv_a_b"], sm["ln_a_w"], sm["ln_a_b"], f"conv_mixer")
        outs, lses = zip(*[_attn_branch(qkv, d, f"attn_d{d}") for d in DILATIONS])
        att, att32, lse = _attn_combine(outs, lses, f"attn_combine")
        lb_f, lb_b = _col(lbs[l][0]), _col(lbs[l][1])
        of, hsf, atf = _hgrn_scan(projt, lb_f, False, "hgrn_fwd")
        ob, hsb, atb = _hgrn_scan(projt, lb_b, True, "hgrn_rev")
        wn = _col(sm["rec_norm_w"])
        rec = _hgrn_out(of, ob, projt, wn, f"hgrn_out")
        mixed = jnp.concatenate([a_out, att, rec], axis=1)
        if l == 0:
            w_in1, w_out_all, w_up_all, w_down_all = later_weights(rec)
            big[0].update(_split_w_rest(w_out_all[0], w_up_all[0], w_down_all[0]))
            big[1] = dict(_split_w_in(w_in1), **_split_w_rest(w_out_all[1], w_up_all[1], w_down_all[1]))
        r1 = _matmul(mixed, w["out"], "nn", BF16, "out_proj")
        xmid, h2 = _resid_norm_mod(xin, r1, md[2:3], sm["norm2_w"], md[4:5], md[3:4], f"norm2")
        u = _matmul(h2, w["up"], "nn", BF16, f"ffn_up")
        act, conv_g, conv_v = _ffn_act(u, sm["conv_f_w"], "ffn_act")
        r2 = _matmul(act, w["down"], "nn", BF16, "ffn_down")
        saved.append(dict(xin=xin, h1=h1, pa=pa, qkv=qkv, projt=projt, cv=cv, att32=att32, lse=lse, of=of, ob=ob,
                          hsf=hsf, hsb=hsb, atf=atf, atb=atb, lb_f=lb_f, lb_b=lb_b, wn=wn, mixed=mixed, r1=r1, xmid=xmid, h2=h2,
                          u=u, conv_g=conv_g, conv_v=conv_v, act=act, r2=r2))
        if l + 1 < DEPTH:
            nxt = small[l + 1]
            xin, h1 = _resid_norm_mod(xmid, r2, md[5:6], nxt["norm1_w"], mods[l + 1][1:2], mods[l + 1][0:1],
                                      "norm1")
    top = saved[-1]
    loss, dx, dr2, dg2, dfw = _final_loss(top["xmid"], top["r2"], mods[-1][5:6], final_w, tgt, "final_loss")

    grads = [None] * DEPTH
    order_after = None
    for l in reversed(range(DEPTH)):
        sm, w, md, sv = small[l], big[l], mods[l], saved[l]
        dact = _matmul(dr2, w["down"], "nt", BF16, f"d_act")
        g_down = _matmul(dr2, sv["act"], "tn", F32, "dw_down").T
        conv_f_w = sm["conv_f_w"] if order_after is None else sm["conv_f_w"] + order_after
        dug, duv, dwg, dwv = _ffn_act_bwd(sv["u"], sv["conv_g"], sv["conv_v"], dact, conv_f_w, "ffn_act_bwd")
        dh2 = _matmul_two_lhs(dug, duv, w["up"], BF16, "d_h2")
        g_up = _matmul_two_rhs(sv["h2"], dug, duv, F32, "dw_up")
        dxmid, dr1, dsh2, dsc2, dnw2, dg1 = _norm_bwd(sv["xmid"], [dh2], dx, sm["norm2_w"], md[4:5], md[2:3], sv["r1"],
                                                     f"norm2_bwd")
        dmix_a = _matmul(dr1, w["out_a"], "nt", F32, f"d_mix_a")
        dmix_b = _matmul(dr1, w["out_b"], "nt", F32, f"d_mix_b")
        dmix_c = _matmul(dr1, w["out_c"], "nt", F32, f"d_mix_c")
        g_out = _matmul(sv["mixed"], dr1, "tn", F32, f"dw_out")
        dc, dlnw, dlnb, dcb = _conv_mixer_bwd_ln(sv["cv"], dmix_a, sm["ln_a_w"], sm["ln_a_b"], f"conv_mixer_bwd_ln")
        dpa, dcw = _conv_mixer_bwd_conv(sv["pa"], dc, sm["conv_a_w"], f"conv_mixer_bwd_conv")
        delta = _attn_delta(dmix_b, sv["att32"], "attn_delta")
        dqkv = None
        for d in DILATIONS:
            dqkv = _attn_branch_bwd(sv["qkv"], dmix_b, sv["lse"], delta, dqkv, d, f"attn_bwd_d{d}")
        dot, dgt, dwn = _hgrn_out_bwd(dmix_c, sv["of"], sv["ob"], sv["projt"], sv["wn"], f"hgrn_out_bwd")
        dqf, dzf, dvf, dlbf = _hgrn_scan_bwd(sv["projt"], sv["lb_f"], dot, sv["hsf"], sv["atf"], None, False,
                                             "hgrn_fwd_bwd")
        dqt, dzb, dvt, dlbb = _hgrn_scan_bwd(sv["projt"], sv["lb_b"], dot, sv["hsb"], sv["atb"], (dqf, dvf), True,
                                             "hgrn_rev_bwd")
        dprojt = jnp.concatenate([dqt, dzf, dzb, dvt, dgt], axis=0).astype(BF16)
        dnat = jnp.concatenate([dpa] + [t.astype(BF16) for t in dqkv], axis=1)
        dh1_a = _matmul(dnat, w["nat"], "nt", BF16, "d_h1_nat")
        dh1_b = _matmul(dprojt, w["rec_t"], "tn", BF16, "d_h1_rec")
        g_in_nat = _matmul(sv["h1"], dnat, "tn", F32, f"dw_in_nat")
        g_in_rec_t = _matmul(dprojt, sv["h1"], "nn", F32, f"dw_in_rec")
        g_in = jnp.concatenate([g_in_nat, g_in_rec_t.T], axis=1)
        if l > 0:
            below = saved[l - 1]
            dx, dr2, dsh1, dsc1, dnw1, dg2_below = _norm_bwd(sv["xin"], [dh1_a, dh1_b], dxmid, sm["norm1_w"], md[1:2],
                                                            mods[l - 1][5:6], below["r2"], f"norm1_bwd")
        else:
            dx, dsh1, dsc1, dnw1 = _norm_bwd(sv["xin"], [dh1_a, dh1_b], dxmid, sm["norm1_w"], md[1:2], None, None,
                                             f"norm1_bwd")
        grads[l] = dict(w_in=g_in, w_out=g_out, w_up=g_up, w_down=g_down,
                        mod=[dsh1, dsc1, dg1, dsh2, dsc2, dg2], norm1_w=dnw1, conv_a_w=dcw[:CONV_WIDTH], conv_a_b=dcb,
                        ln_a_w=dlnw, ln_a_b=dlnb, lb=jnp.concatenate([dlbf.reshape(1, -1), dlbb.reshape(1, -1)], axis=0),
                        rec_norm_w=dwn.reshape(1, -1), norm2_w=dnw2,
                        conv_f_w=jnp.concatenate([dwg[:3], dwv[:3]], axis=1))
        order_after = on_layer_grads(l, [g_in, g_out, g_up, g_down])
        if l > 0:
            dg2 = dg2_below
    return loss[0, 0], dx, grads, dfw


def _adamw_math(w, g, m, v):
    m = ADAM_B1 * m + (1.0 - ADAM_B1) * g
    v = ADAM_B2 * v + (1.0 - ADAM_B2) * (g * g)
    m_hat = m / (1.0 - ADAM_B1 ** ADAM_STEP)
    v_hat = v / (1.0 - ADAM_B2 ** ADAM_STEP)
    delta = -ADAM_LR * (m_hat / (jnp.sqrt(v_hat) + ADAM_EPS) + ADAM_WD * w)
    return delta, m, v


def _row_tile(rows, cols, max_elems=384 * 1024):
    best = None
    for t in range(8, rows + 1, 8):
        if rows % t == 0 and t * cols <= max_elems:
            best = t
    return best or rows


def _adamw(w, g, m, v, name):
    nl, r, c = w.shape
    tr = _row_tile(r, c)

    def body(w_ref, g_ref, m_ref, v_ref, d_ref, m2_ref, v2_ref):
        d_ref[...], m2_ref[...], v2_ref[...] = _adamw_math(w_ref[...], g_ref[...], m_ref[...], v_ref[...])

    blk = pl.BlockSpec((None, tr, c), lambda l, i: (l, i, 0))
    shape = jax.ShapeDtypeStruct((nl, r, c), F32)
    return pl.pallas_call(body, name=name, grid=(nl, r // tr), in_specs=[blk] * 4, out_specs=(blk, blk, blk),
                          out_shape=(shape, shape, shape), compiler_params=_params(2))(w, g, m, v)


ADA_SHARD = N_MOD * D_MODEL // 4
ADA_COLS = 512
ADA_ROWS = 256
HIGHEST = lax.Precision.HIGHEST


def _ada_mod(c_all, w_ada, b_sh, name):
    def body(c_ref, w_ref, b_ref, o_ref):
        cv = c_ref[...]
        o_ref[...] = jnp.dot(cv * _sigmoid(cv), w_ref[...], precision=HIGHEST, preferred_element_type=F32) + b_ref[...]

    return pl.pallas_call(
        body, name=name, grid=(DEPTH, ADA_SHARD // ADA_COLS),
        in_specs=[pl.BlockSpec((8, D_MODEL), lambda l, j: (0, 0)),
                  pl.BlockSpec((None, D_MODEL, ADA_COLS), lambda l, j: (l, 0, j)),
                  pl.BlockSpec((None, 1, ADA_COLS), lambda l, j: (l, 0, j))],
        out_specs=pl.BlockSpec((None, 8, ADA_COLS), lambda l, j: (l, 0, j)),
        out_shape=jax.ShapeDtypeStruct((DEPTH, 8, ADA_SHARD), F32), compiler_params=_params(2))(c_all, w_ada, b_sh)


def _ada_update(c_all, dmod_sh, w, m, v, name):
    def body(c_ref, d_ref, w_ref, m_ref, v_ref, g_ref, dl_ref, m2_ref, v2_ref):
        cv = c_ref[...]
        g = lax.dot_general(cv * _sigmoid(cv), d_ref[...], (((0,), (0,)), ((), ())), precision=HIGHEST,
                            preferred_element_type=F32)
        g_ref[...] = g
        dl_ref[...], m2_ref[...], v2_ref[...] = _adamw_math(w_ref[...], g, m_ref[...], v_ref[...])

    blk = pl.BlockSpec((None, ADA_ROWS, ADA_SHARD), lambda l, i: (l, i, 0))
    shape = jax.ShapeDtypeStruct((DEPTH, D_MODEL, ADA_SHARD), F32)
    return pl.pallas_call(
        body, name=name, grid=(DEPTH, D_MODEL // ADA_ROWS),
        in_specs=[pl.BlockSpec((8, ADA_ROWS), lambda l, i: (0, i)),
                  pl.BlockSpec((None, 8, ADA_SHARD), lambda l, i: (l, 0, 0)), blk, blk, blk],
        out_specs=(blk,) * 4, out_shape=(shape,) * 4, compiler_params=_params(2))(c_all, dmod_sh, w, m, v)


def _sum_devices(packs, name):
    def body(p_ref, o_ref):
        acc = p_ref[0]
        for dev in range(1, 8):
            acc = acc + p_ref[dev]
        o_ref[...] = acc

    return pl.pallas_call(body, name=name, out_shape=jax.ShapeDtypeStruct(packs.shape[1:], F32))(packs)


def _mesh_pos():
    return lax.axis_index("x"), lax.axis_index("y"), lax.axis_index("c")


def _flip(v, bit):
    return 1 - v if bit else v


def _allgather_devices(x, name):
    m_per, n = x.shape

    def body(x_ref, out_ref, send_sems, recv_sems, local_sem):
        ix, iy, ic = _mesh_pos()
        me, sibling = (ix, iy, ic), (ix, iy, 1 - ic)
        chips = [(1 - ix, iy), (ix, 1 - iy), (1 - ix, 1 - iy)]

        def rows(px, py, pc):
            return out_ref.at[pl.ds((4 * px + 2 * py + pc) * m_per, m_per), :]

        def copy(k, block, to, src=None):
            return pltpu.make_async_remote_copy(
                src_ref=rows(*block) if src is None else src, dst_ref=rows(*block),
                send_sem=send_sems.at[k], recv_sem=recv_sems.at[k], device_id=to, device_id_type=MESH)

        mine = pltpu.make_async_copy(x_ref, rows(*me), local_sem)
        mine.start()
        first = [copy(0, me, sibling, src=x_ref)]
        first += [copy(1 + j, me, (*chip, ic), src=x_ref) for j, chip in enumerate(chips)]
        for cp in first:
            cp.start()
        passed = [copy(4 + j, (*chip, ic), sibling) for j, chip in enumerate(chips)]
        for j, chip in enumerate(chips):
            copy(1 + j, (*chip, ic), me).wait_recv()
            passed[j].start()
        copy(0, sibling, me).wait_recv()
        for j, chip in enumerate(chips):
            copy(4 + j, (*chip, 1 - ic), me).wait_recv()
        for cp in first + passed:
            cp.wait_send()
        mine.wait()

    return pl.pallas_call(
        body, name=name, out_shape=jax.ShapeDtypeStruct((8 * m_per, n), x.dtype),
        in_specs=[pl.BlockSpec(memory_space=pltpu.VMEM)], out_specs=pl.BlockSpec(memory_space=pltpu.VMEM),
        scratch_shapes=[pltpu.SemaphoreType.DMA((7,)), pltpu.SemaphoreType.DMA((7,)), pltpu.SemaphoreType.DMA],
    )(x)


def _gather_chips(shards, name):
    n = len(shards)

    def body(*refs):
        ins, outs = refs[:n], refs[n:2 * n]
        send_sems, recv_sems, local_sems = refs[2 * n:]
        ix, iy, ic = _mesh_pos()
        me = 2 * ix + iy
        local = [pltpu.make_async_copy(ins[a], outs[a].at[me], local_sems.at[a]) for a in range(n)]
        for cp in local:
            cp.start()
        remote = []
        for a in range(n):
            for k in (1, 2, 3):
                px, py = _flip(ix, k & 2), _flip(iy, k & 1)
                sems = dict(send_sem=send_sems.at[3 * a + k - 1], recv_sem=recv_sems.at[3 * a + k - 1],
                            device_id=(px, py, ic), device_id_type=MESH)
                out_cp = pltpu.make_async_remote_copy(src_ref=ins[a], dst_ref=outs[a].at[me], **sems)
                in_cp = pltpu.make_async_remote_copy(src_ref=ins[a], dst_ref=outs[a].at[2 * px + py], **sems)
                out_cp.start()
                remote.append((out_cp, in_cp))
        for out_cp, in_cp in remote:
            out_cp.wait_send()
            in_cp.wait_recv()
        for cp in local:
            cp.wait()

    return pl.pallas_call(
        body, name=name, in_specs=[ANY] * n, out_specs=tuple([ANY] * n),
        out_shape=tuple(jax.ShapeDtypeStruct((4,) + t.shape, t.dtype) for t in shards),
        scratch_shapes=[pltpu.SemaphoreType.DMA((3 * n,)), pltpu.SemaphoreType.DMA((3 * n,)),
                        pltpu.SemaphoreType.DMA((n,))],
    )(*shards)


HBM = pl.BlockSpec(memory_space=pltpu.HBM)
SEM = pl.BlockSpec(memory_space=pltpu.SEMAPHORE)
DATAFLOW = pltpu.SideEffectType.DATAFLOW_SIDE_EFFECTING


def _peer_chip(ix, iy, k):
    return _flip(ix, k & 2), _flip(iy, k & 1)


def _gather_chips_start(shards, name):
    n = len(shards)

    def body(*refs):
        src, land = refs[:n], refs[n:2 * n]
        send_sems, recv_sems = refs[2 * n], refs[2 * n + 1]
        token = refs[-1]
        ix, iy, ic = _mesh_pos()
        me = 2 * ix + iy
        for a in range(n):
            for k in (1, 2, 3):
                px, py = _peer_chip(ix, iy, k)
                pltpu.make_async_remote_copy(
                    src_ref=src[a], dst_ref=land[a].at[me], send_sem=send_sems.at[3 * a + k - 1],
                    recv_sem=recv_sems.at[3 * a + k - 1], device_id=(px, py, ic), device_id_type=MESH).start()
        token[...] = jnp.zeros_like(token)

    hbm = lambda shape, dtype: pltpu.HBM(shape, dtype)
    operands = ([pltpu.with_memory_space_constraint(t, pltpu.HBM) for t in shards]
                + [pltpu.with_memory_space_constraint(lax.empty((4,) + t.shape, t.dtype), pltpu.HBM) for t in shards])
    return pl.pallas_call(
        body, name=name,
        out_shape=(pltpu.SemaphoreType.DMA((3 * n,)), pltpu.SemaphoreType.DMA((3 * n,)),
                   *[hbm(t.shape, t.dtype) for t in shards], *[hbm((4,) + t.shape, t.dtype) for t in shards],
                   jax.ShapeDtypeStruct((8, LANES), F32)),
        in_specs=(HBM,) * (2 * n),
        out_specs=(SEM, SEM) + (HBM,) * (2 * n) + (pl.BlockSpec(memory_space=pltpu.VMEM),),
        input_output_aliases={a: 2 + a for a in range(2 * n)},
        compiler_params=pltpu.CompilerParams(has_side_effects=DATAFLOW),
    )(*operands)


def _gather_chips_wait(started, after, name):
    send_sems, recv_sems = started[0], started[1]
    thru = started[2:-1]
    n = len(thru) // 2

    def body(*refs):
        src, land = refs[:n], refs[n:2 * n]
        send_sems, recv_sems = refs[2 * n], refs[2 * n + 1]
        ix, iy, ic = _mesh_pos()
        for a in range(n):
            for k in (1, 2, 3):
                px, py = _peer_chip(ix, iy, k)
                cp = pltpu.make_async_remote_copy(
                    src_ref=src[a], dst_ref=land[a].at[2 * px + py], send_sem=send_sems.at[3 * a + k - 1],
                    recv_sem=recv_sems.at[3 * a + k - 1], device_id=(px, py, ic), device_id_type=MESH)
                cp.wait_send()
                cp.wait_recv()

    outs = pl.pallas_call(
        body, name=name,
        out_shape=tuple(pltpu.HBM(t.shape, t.dtype) for t in thru),
        in_specs=(HBM,) * (2 * n) + (SEM, SEM, ANY), out_specs=(HBM,) * (2 * n),
        input_output_aliases={a: a for a in range(2 * n)},
        compiler_params=pltpu.CompilerParams(has_side_effects=DATAFLOW),
    )(*thru, send_sems, recv_sems, after)
    return outs[:n], outs[n:]


BIG_KINDS = (("w_in", "col", D_MODEL, IN_COLS), ("w_out", "row", D_MODEL, D_MODEL),
             ("w_up", "col", D_MODEL, 2 * D_FF), ("w_down", "row", D_FF, D_MODEL))


def _piece_shape(how, r, c):
    return (r // 2, c // 4) if how == "col" else (r // 8, c)


def _aligned(start, multiple):
    return start if isinstance(start, int) else pl.multiple_of(start, multiple)


def _piece(ref, how, r, c, chip, half):
    if how == "col":
        return ref.at[pl.ds(_aligned(half * (r // 2), 8), r // 2), pl.ds(_aligned(chip * (c // 4), LANES), c // 4)]
    n = r // 4
    return ref.at[pl.ds(_aligned(chip * n + half * (n // 2), 8), n // 2), :]


def _rs_pair_exchange(grads, name):
    nk = len(BIG_KINDS)
    nl = len(grads[0])
    flat = [grads[ki][l] for ki in range(nk) for l in range(nl)]
    per = nl * 4

    def body(*refs):
        g, land = refs[:nk * nl], refs[nk * nl:nk * nl + nk]
        send_sems, recv_sems = refs[nk * nl + nk:]
        ix, iy, ic = _mesh_pos()
        sibling = (ix, iy, 1 - ic)
        copies = []
        for ki, (_, how, r, c) in enumerate(BIG_KINDS):
            for l in range(nl):
                for j in range(4):
                    sem = ki * per + l * 4 + j
                    rem = pltpu.make_async_remote_copy(
                        src_ref=_piece(g[ki * nl + l], how, r, c, j, 1 - ic), dst_ref=land[ki].at[l, j],
                        send_sem=send_sems.at[sem], recv_sem=recv_sems.at[sem], device_id=sibling, device_id_type=MESH)
                    rem.start()
                    copies.append(rem)
        for rem in copies:
            rem.wait_send()
            rem.wait_recv()

    shapes = [jax.ShapeDtypeStruct((nl, 4) + _piece_shape(how, r, c), F32) for _, how, r, c in BIG_KINDS]
    return pl.pallas_call(
        body, name=name, in_specs=[ANY] * len(flat), out_specs=tuple([ANY] * nk), out_shape=tuple(shapes),
        scratch_shapes=[pltpu.SemaphoreType.DMA((nk * per,))] * 2,
    )(*flat)


def _pair_sum(g, theirs, layer, how, core, name):
    r, c = g.shape
    pr, pc = _piece_shape(how, r, c)
    if how == "col":
        mine_spec = pl.BlockSpec((pr, pc), lambda j, core_ref: (core_ref[0], j))
    else:
        mine_spec = pl.BlockSpec((pr, pc), lambda j, core_ref: (2 * j + core_ref[0], 0))

    def body(core_ref, g_ref, t_ref, o_ref, ob_ref):
        total = g_ref[...] + t_ref[...]
        o_ref[...] = total
        ob_ref[...] = total.astype(BF16)

    out_blk = pl.BlockSpec((None, pr, pc), lambda j, core_ref: (j, 0, 0))
    return pl.pallas_call(
        body, name=name,
        grid_spec=pltpu.PrefetchScalarGridSpec(
            num_scalar_prefetch=1, grid=(4,),
            in_specs=[mine_spec, pl.BlockSpec((None, None, pr, pc), lambda j, core_ref: (layer, j, 0, 0))],
            out_specs=(out_blk, out_blk)),
        out_shape=(jax.ShapeDtypeStruct((4, pr, pc), F32), jax.ShapeDtypeStruct((4, pr, pc), BF16)),
        compiler_params=_params(1))(core, g, theirs)


def _rs_chip_exchange(pair_sums, name):
    nk = len(pair_sums)
    nl = len(pair_sums[0])
    flat = [pair_sums[ki][l] for ki in range(nk) for l in range(nl)]

    def body(*refs):
        src, dst = refs[:nk * nl], refs[nk * nl:nk * nl + nk]
        send_sems, recv_sems = refs[nk * nl + nk:]
        ix, iy, ic = _mesh_pos()
        copies = []
        for ki in range(nk):
            for l in range(nl):
                for k in (1, 2, 3):
                    px, py = _peer_chip(ix, iy, k)
                    sem = (ki * nl + l) * 3 + k - 1
                    rem = pltpu.make_async_remote_copy(
                        src_ref=src[ki * nl + l].at[2 * px + py], dst_ref=dst[ki].at[l, k - 1],
                        send_sem=send_sems.at[sem], recv_sem=recv_sems.at[sem], device_id=(px, py, ic), device_id_type=MESH)
                    rem.start()
                    copies.append(rem)
        for rem in copies:
            rem.wait_send()
            rem.wait_recv()

    return pl.pallas_call(
        body, name=name, in_specs=[ANY] * len(flat), out_specs=tuple([ANY] * nk),
        out_shape=tuple(jax.ShapeDtypeStruct((nl, 3) + pair_sums[ki][0].shape[1:], pair_sums[ki][0].dtype)
                        for ki in range(nk)),
        scratch_shapes=[pltpu.SemaphoreType.DMA((nk * nl * 3,))] * 2,
    )(*flat)


def _rs_chip_start(pieces, name):
    n = len(pieces)

    def body(*refs):
        src, land = refs[:n], refs[n:2 * n]
        send_sems, recv_sems = refs[2 * n], refs[2 * n + 1]
        token = refs[-1]
        ix, iy, ic = _mesh_pos()
        for a in range(n):
            for k in (1, 2, 3):
                px, py = _peer_chip(ix, iy, k)
                pltpu.make_async_remote_copy(
                    src_ref=src[a].at[2 * px + py], dst_ref=land[a].at[k - 1], send_sem=send_sems.at[3 * a + k - 1],
                    recv_sem=recv_sems.at[3 * a + k - 1], device_id=(px, py, ic), device_id_type=MESH).start()
        token[...] = jnp.zeros_like(token)

    land_shape = lambda t: (3,) + t.shape[1:]
    operands = ([pltpu.with_memory_space_constraint(t, pltpu.HBM) for t in pieces]
                + [pltpu.with_memory_space_constraint(lax.empty(land_shape(t), t.dtype), pltpu.HBM) for t in pieces])
    return pl.pallas_call(
        body, name=name,
        out_shape=(pltpu.SemaphoreType.DMA((3 * n,)), pltpu.SemaphoreType.DMA((3 * n,)),
                   *[pltpu.HBM(t.shape, t.dtype) for t in pieces], *[pltpu.HBM(land_shape(t), t.dtype) for t in pieces],
                   jax.ShapeDtypeStruct((8, LANES), F32)),
        in_specs=(HBM,) * (2 * n),
        out_specs=(SEM, SEM) + (HBM,) * (2 * n) + (pl.BlockSpec(memory_space=pltpu.VMEM),),
        input_output_aliases={a: 2 + a for a in range(2 * n)},
        compiler_params=pltpu.CompilerParams(has_side_effects=DATAFLOW),
    )(*operands)


def _rs_chip_wait(started, after, name):
    send_sems, recv_sems = started[0], started[1]
    thru = started[2:-1]
    n = len(thru) // 2

    def body(*refs):
        src, land = refs[:n], refs[n:2 * n]
        send_sems, recv_sems = refs[2 * n], refs[2 * n + 1]
        ix, iy, ic = _mesh_pos()
        for a in range(n):
            for k in (1, 2, 3):
                px, py = _peer_chip(ix, iy, k)
                cp = pltpu.make_async_remote_copy(
                    src_ref=src[a].at[2 * px + py], dst_ref=land[a].at[k - 1], send_sem=send_sems.at[3 * a + k - 1],
                    recv_sem=recv_sems.at[3 * a + k - 1], device_id=(px, py, ic), device_id_type=MESH)
                cp.wait_send()
                cp.wait_recv()

    outs = pl.pallas_call(
        body, name=name,
        out_shape=tuple(pltpu.HBM(t.shape, t.dtype) for t in thru),
        in_specs=(HBM,) * (2 * n) + (SEM, SEM, ANY), out_specs=(HBM,) * (2 * n),
        input_output_aliases={a: a for a in range(2 * n)},
        compiler_params=pltpu.CompilerParams(has_side_effects=DATAFLOW),
    )(*thru, send_sems, recv_sems, after)
    return outs[n:]


def _chip_sum(own, others, chip, name):
    _, pr, pc = own.shape

    def body(chip_ref, own_ref, s1, s2, s3, o_ref):
        o_ref[...] = ((own_ref[...] + s1[...].astype(F32)) + s2[...].astype(F32)) + s3[...].astype(F32)

    slot = lambda k: pl.BlockSpec((None, pr, pc), lambda i, chip_ref: (k, 0, 0))
    return pl.pallas_call(
        body, name=name,
        grid_spec=pltpu.PrefetchScalarGridSpec(
            num_scalar_prefetch=1, grid=(1,),
            in_specs=[pl.BlockSpec((None, pr, pc), lambda i, chip_ref: (chip_ref[0], 0, 0)), slot(0), slot(1), slot(2)],
            out_specs=pl.BlockSpec((pr, pc), lambda i, chip_ref: (0, 0))),
        out_shape=jax.ShapeDtypeStruct((pr, pc), F32), compiler_params=_params(1))(chip, own, others, others, others)


def _rs_pair_share(halves, name):
    nk = len(halves)
    flat = [halves[ki][l] for ki in range(nk) for l in range(DEPTH)]

    def body(*refs):
        src, dst = refs[:nk * DEPTH], refs[nk * DEPTH:nk * DEPTH + nk]
        send_sems, recv_sems = refs[nk * DEPTH + nk:]
        ix, iy, ic = _mesh_pos()
        copies = []
        for ki in range(nk):
            for l in range(DEPTH):
                sem = ki * DEPTH + l
                rem = pltpu.make_async_remote_copy(
                    src_ref=src[sem], dst_ref=dst[ki].at[l], send_sem=send_sems.at[sem], recv_sem=recv_sems.at[sem],
                    device_id=(ix, iy, 1 - ic), device_id_type=MESH)
                rem.start()
                copies.append(rem)
        for rem in copies:
            rem.wait_send()
            rem.wait_recv()

    return pl.pallas_call(
        body, name=name, in_specs=[ANY] * len(flat), out_specs=tuple([ANY] * nk),
        out_shape=tuple(jax.ShapeDtypeStruct((DEPTH,) + halves[ki][0].shape, F32) for ki in range(nk)),
        scratch_shapes=[pltpu.SemaphoreType.DMA((nk * DEPTH,))] * 2,
    )(*flat)


def _adamw_halves(w, mine, theirs, m, v, core, name):
    nl, pr, pc = theirs.shape
    shape = w.shape
    view = lambda t: t.reshape(nl, 2, pr, pc)
    tr = _row_tile(pr, pc, 256 * 1024)

    def body(core_ref, w_ref, a0_ref, a1_ref, t_ref, m_ref, v_ref, g_ref, d_ref, m2_ref, v2_ref):
        own = jnp.where(pl.program_id(0) == 0, a0_ref[...], a1_ref[...])
        g = jnp.where(pl.program_id(1) == core_ref[0], own, t_ref[...])
        g_ref[...] = g
        d_ref[...], m2_ref[...], v2_ref[...] = _adamw_math(w_ref[...], g, m_ref[...], v_ref[...])

    blk = pl.BlockSpec((None, None, tr, pc), lambda l, h, i, core_ref: (l, h, i, 0))
    own_blk = pl.BlockSpec((tr, pc), lambda l, h, i, core_ref: (i, 0))
    out = jax.ShapeDtypeStruct((nl, 2, pr, pc), F32)
    outs = pl.pallas_call(
        body, name=name,
        grid_spec=pltpu.PrefetchScalarGridSpec(
            num_scalar_prefetch=1, grid=(nl, 2, pr // tr),
            in_specs=[blk, own_blk, own_blk, pl.BlockSpec((None, tr, pc), lambda l, h, i, core_ref: (l, i, 0)), blk, blk],
            out_specs=(blk,) * 4),
        out_shape=(out,) * 4, compiler_params=_params(3),
    )(core, view(w), mine[0], mine[1], theirs, view(m), view(v))
    return tuple(t.reshape(shape) for t in outs)


def _rs_first_stages(layer_grads, core, tag, in_flight):
    theirs = _rs_pair_exchange([[g] for g in layer_grads], f"rs_pair_exchange_{tag}")
    sums = [_pair_sum(g, theirs[ki], 0, how, core, f"rs_pair_sum_{kind}")
            for ki, ((kind, how, _, _), g) in enumerate(zip(BIG_KINDS, layer_grads))]
    to_send = [both[1] for both in sums]
    if in_flight:
        return [both[0] for both in sums], _rs_chip_start(to_send, f"rs_chip_start_{tag}")
    slots = _rs_chip_exchange([[t] for t in to_send], f"rs_chip_exchange_{tag}")
    return [both[0] for both in sums], [t[0] for t in slots]


def _rs_last_stages(per_layer, chip):
    halves = [[_chip_sum(per_layer[l][0][ki], per_layer[l][1][ki], chip, f"rs_chip_sum_{kind}") for l in range(DEPTH)]
              for ki, (kind, _, _, _) in enumerate(BIG_KINDS)]
    other = _rs_pair_share(halves, "rs_pair_share")
    return list(zip(halves, other))


WEIGHT_NAMES = ("w_ada", "b_ada", "norm1_w", "w_in", "conv_a_w", "conv_a_b", "ln_a_w", "ln_a_b", "lb_gamma",
                "rec_norm_w", "w_out", "norm2_w", "w_up", "conv_f_w", "w_down", "final_norm_w")
SMALL_PARAMS = (("b_ada", (DEPTH, N_MOD * D_MODEL), None), ("norm1_w", (DEPTH, D_MODEL), None),
                ("conv_a_w", (DEPTH, CONV_WIDTH, CONV_CH), 2), ("conv_a_b", (DEPTH, CONV_CH), None),
                ("ln_a_w", (DEPTH, CONV_CH), None), ("ln_a_b", (DEPTH, CONV_CH), None),
                ("lb_gamma", (DEPTH, 2, REC_WIDTH), 2), ("rec_norm_w", (DEPTH, REC_WIDTH), None),
                ("norm2_w", (DEPTH, D_MODEL), None), ("conv_f_w", (DEPTH, 3, 2 * D_FF), 2),
                ("final_norm_w", (D_MODEL,), None))


def _pack_rows(parts):
    flat = jnp.concatenate([p.reshape(-1) for p in parts])
    total = flat.shape[0]
    padded = -(-total // (8 * LANES)) * (8 * LANES)
    return jnp.pad(flat, (0, padded - total)).reshape(padded // LANES, LANES)


def _unpack(flat, shapes):
    out, off = [], 0
    for shp in shapes:
        size = int(np.prod(shp))
        out.append(flat[off:off + size].reshape(shp))
        off += size
    return out


def _unstack_chips(t, axis):
    return jnp.concatenate([t[j] for j in range(4)], axis=axis)


def kernel(x, c, w_ada, b_ada, norm1_w, w_in, conv_a_w, conv_a_b, ln_a_w, ln_a_b, lb_gamma, rec_norm_w, w_out, norm2_w, w_up, conv_f_w, w_down, final_norm_w, loss_target, m_w_ada, m_b_ada, m_norm1_w, m_w_in, m_conv_a_w, m_conv_a_b, m_ln_a_w, m_ln_a_b, m_lb_gamma, m_rec_norm_w, m_w_out, m_norm2_w, m_w_up, m_conv_f_w, m_w_down, m_final_norm_w, v_w_ada, v_b_ada, v_norm1_w, v_w_in, v_conv_a_w, v_conv_a_b, v_ln_a_w, v_ln_a_b, v_lb_gamma, v_rec_norm_w, v_w_out, v_norm2_w, v_w_up, v_conv_f_w, v_w_down, v_final_norm_w):
    params = dict(zip(WEIGHT_NAMES, (w_ada, b_ada, norm1_w, w_in, conv_a_w, conv_a_b, ln_a_w, ln_a_b, lb_gamma,
                                     rec_norm_w, w_out, norm2_w, w_up, conv_f_w, w_down, final_norm_w)))
    mom1 = dict(zip(WEIGHT_NAMES, (m_w_ada, m_b_ada, m_norm1_w, m_w_in, m_conv_a_w, m_conv_a_b, m_ln_a_w, m_ln_a_b,
                                   m_lb_gamma, m_rec_norm_w, m_w_out, m_norm2_w, m_w_up, m_conv_f_w, m_w_down,
                                   m_final_norm_w)))
    mom2 = dict(zip(WEIGHT_NAMES, (v_w_ada, v_b_ada, v_norm1_w, v_w_in, v_conv_a_w, v_conv_a_b, v_ln_a_w, v_ln_a_b,
                                   v_lb_gamma, v_rec_norm_w, v_w_out, v_norm2_w, v_w_up, v_conv_f_w, v_w_down,
                                   v_final_norm_w)))
    ix, iy, ic = _mesh_pos()
    chip = 2 * ix + iy
    dev = 2 * chip + ic

    c_all = _allgather_devices(c.reshape(8, LANES), "gather_cond").reshape(8, D_MODEL)
    b_sh = lax.dynamic_slice_in_dim(b_ada, chip * ADA_SHARD, ADA_SHARD, axis=1)
    mod_sh = _ada_mod(c_all, w_ada, b_sh.reshape(DEPTH, 1, ADA_SHARD), "ada_mod")
    w_in_b, w_out_b, w_up_b, w_down_b = (t.astype(BF16) for t in (w_in, w_out, w_up, w_down))
    first = _gather_chips([mod_sh, conv_a_w, conv_f_w, lb_gamma, w_in_b[0]], "gather_first")
    later = [w_in_b[1], w_out_b, w_up_b, w_down_b]
    started = _gather_chips_start(later, "gather_rest_start")
    mod_mine = lax.dynamic_index_in_dim(first[0], dev, axis=2, keepdims=False) + started[-1][0, 0]
    mods = [jnp.concatenate([mod_mine[j, l] for j in range(4)]).reshape(N_MOD, D_MODEL) for l in range(DEPTH)]
    conv_a_w_f, conv_f_w_f, gamma_f = (_unstack_chips(first[k], 2) for k in (1, 2, 3))
    w_in0 = _unstack_chips(first[4], 1)

    def later_weights(after):
        own, lands = _gather_chips_wait(started, after, "gather_rest_wait")
        whole = lambda n, axis: jnp.concatenate([jnp.where(chip == j, own[n], lands[n][j]) for j in range(4)], axis=axis)
        return whole(0, 1), whole(1, 1), whole(2, 2), whole(3, 1)

    lb1, p_soft = _lower_bounds(gamma_f.reshape(DEPTH, 2 * REC_WIDTH), "lower_bounds")
    lbs = [jnp.zeros((2, REC_WIDTH), F32), lb1.reshape(2, REC_WIDTH)]
    small = []
    for l in range(DEPTH):
        small.append(dict(norm1_w=norm1_w[l][None], conv_a_w=conv_a_w_f[l], conv_a_b=conv_a_b[l][None],
                          ln_a_w=ln_a_w[l][None], ln_a_b=ln_a_b[l][None], rec_norm_w=rec_norm_w[l],
                          norm2_w=norm2_w[l][None], conv_f_w=conv_f_w_f[l]))

    core_id, chip_id = ic.astype(jnp.int32).reshape(1), chip.astype(jnp.int32).reshape(1)
    reduce_state = [None] * DEPTH

    def on_layer_grads(l, layer_grads):
        in_flight = l > 0
        reduce_state[l] = _rs_first_stages(layer_grads, core_id, f"l{l}", in_flight)
        return reduce_state[l][1][-1][0:1, 0:1] if in_flight else None

    loss, dx, grads, dfw = _sequence_step(x[0], loss_target[0], mods, lbs, small, w_in0, later_weights,
                                          final_norm_w[None], on_layer_grads)
    loss = lax.psum(loss, ("x", "y", "c"))
    for l in range(1, DEPTH):
        sums, started = reduce_state[l]
        reduce_state[l] = (sums, _rs_chip_wait(started, dx, f"rs_chip_wait_l{l}"))

    dgamma = _lower_bounds_bwd(grads[1]["lb"].reshape(1, 2 * REC_WIDTH), p_soft, "lower_bounds_bwd")
    dmod = [jnp.concatenate(grads[l]["mod"], axis=1) for l in range(DEPTH)]
    stack = lambda key: jnp.stack([grads[l][key] for l in range(DEPTH)])
    local_small = dict(b_ada=jnp.concatenate(dmod, axis=0), norm1_w=stack("norm1_w"), conv_a_w=stack("conv_a_w"),
                       conv_a_b=stack("conv_a_b"), ln_a_w=stack("ln_a_w"), ln_a_b=stack("ln_a_b"), lb_gamma=dgamma,
                       rec_norm_w=stack("rec_norm_w"), norm2_w=stack("norm2_w"), conv_f_w=stack("conv_f_w"),
                       final_norm_w=dfw)
    pack = _pack_rows([local_small[name] for name, _, _ in SMALL_PARAMS])
    rows = pack.shape[0]
    packs = _allgather_devices(pack, "gather_small_grads").reshape(8, rows, LANES)
    summed = _sum_devices(packs, "sum_small_grads").reshape(-1)
    small_grads = dict(zip([n for n, _, _ in SMALL_PARAMS], _unpack(summed, [shp for _, shp, _ in SMALL_PARAMS])))

    dmod_all = packs.reshape(8, rows * LANES)[:, :DEPTH * N_MOD * D_MODEL].reshape(8, DEPTH, N_MOD * D_MODEL)
    dmod_sh = lax.dynamic_slice_in_dim(dmod_all, chip * ADA_SHARD, ADA_SHARD, axis=2).transpose(1, 0, 2)
    g_ada, d_ada, m_ada, v_ada = _ada_update(c_all, dmod_sh, w_ada, m_w_ada, v_w_ada, "ada_update")

    for name, shp, axis in SMALL_PARAMS:
        if axis is not None:
            width = shp[axis] // 4
            small_grads[name] = lax.dynamic_slice_in_dim(small_grads[name], chip * width, width, axis=axis)
    names = [n for n, _, _ in SMALL_PARAMS]
    packed = [_pack_rows([src[n] for n in names])[None] for src in (params, small_grads, mom1, mom2)]
    small_out = _adamw(*packed, "adamw_small")
    shapes = [params[n].shape for n in names]
    small_delta, small_m, small_v = (dict(zip(names, _unpack(t.reshape(-1), shapes))) for t in small_out)

    summed_big = _rs_last_stages(reduce_state, chip_id)
    grad, delta, new_m, new_v = dict(small_grads), small_delta, small_m, small_v
    grad["w_ada"], delta["w_ada"], new_m["w_ada"], new_v["w_ada"] = g_ada, d_ada, m_ada, v_ada
    for (name, _, _, _), (mine, theirs) in zip(BIG_KINDS, summed_big):
        grad[name], delta[name], new_m[name], new_v[name] = _adamw_halves(
            params[name], mine, theirs, mom1[name], mom2[name], core_id, f"adamw_{name}")

    return (loss, dx[None], *[grad[n] for n in WEIGHT_NAMES], *[delta[n] for n in WEIGHT_NAMES],
            *[new_m[n] for n in WEIGHT_NAMES], *[new_v[n] for n in WEIGHT_NAMES])
```

```python
import numpy as np
import jax
import jax.numpy as jnp
from jax import lax
from jax.experimental import pallas as pl
from jax.experimental.pallas import tpu as pltpu

F32 = jnp.float32
BF16 = jnp.bfloat16

D_MODEL = 1024
DEPTH = 2
HEAD_DIM = 64
CONV_CH = 256
CONV_WIDTH = 31
ATT_WIDTH = 384
N_HEADS = 6
DILATIONS = (1, 4, 16)
ATT_HALF = 64
ATT_BLOCK = 128
ALIBI_MAX_EXP = 8.0
MASK_VALUE = -1e30
REC_WIDTH = 384
REC_CHUNK = 64
F_TINY = 1e-30
D_FF = 2816
N_MOD = 6
EPS = 1e-6
G_CONV = (0, 512)
G_QKV = (512, 1664)
G_REC = (1664, 3584)
IN_COLS = 3584

ADAM_LR = 0.001
ADAM_B1 = 0.9
ADAM_B2 = 0.999
ADAM_EPS = 1e-08
ADAM_WD = 0.01
ADAM_STEP = 10

VMEM_LIMIT_BYTES = 56 * 1024 * 1024
LANES = 128
MESH = pl.DeviceIdType.MESH
ANY = pl.BlockSpec(memory_space=pl.ANY)


def _params(n_axes):
    return pltpu.CompilerParams(dimension_semantics=("arbitrary",) * n_axes,
                                vmem_limit_bytes=VMEM_LIMIT_BYTES)


def _tile(n, target):
    best = None
    for t in range(LANES, min(n, target) + 1, LANES):
        if n % t == 0:
            best = t
    return best or n


def _sigmoid(x):
    return jax.nn.sigmoid(x)


def _silu_grad(x):
    s = _sigmoid(x)
    return s * (1.0 + x * (1.0 - s))


MM_ACC_ELEMS = 1536 * 1024


def _matmul(a, b, mode, out_dtype, name, tm=1024, tn=1792, tk=1792):
    if mode == "nn":
        (m, k), (k2, n) = a.shape, b.shape
    elif mode == "nt":
        (m, k), (n, k2) = a.shape, b.shape
    else:
        (k, m), (k2, n) = a.shape, b.shape
    assert k == k2, (a.shape, b.shape, mode)
    tn, tk = _tile(n, tn), _tile(k, tk)
    tm = _tile(m, min(tm, MM_ACC_ELEMS // tn))
    nk = k // tk
    a_spec = (pl.BlockSpec((tk, tm), lambda i, j, kk: (kk, i)) if mode == "tn"
              else pl.BlockSpec((tm, tk), lambda i, j, kk: (i, kk)))
    b_spec = (pl.BlockSpec((tn, tk), lambda i, j, kk: (j, kk)) if mode == "nt"
              else pl.BlockSpec((tk, tn), lambda i, j, kk: (kk, j)))
    dims = {"nn": (((1,), (0,)), ((), ())), "nt": (((1,), (1,)), ((), ())),
            "tn": (((0,), (0,)), ((), ()))}[mode]

    def body(a_ref, b_ref, o_ref, *scratch):
        part = lax.dot_general(a_ref[...].astype(BF16), b_ref[...].astype(BF16), dims, preferred_element_type=F32)
        if nk == 1:
            o_ref[...] = part.astype(out_dtype)
            return
        acc_ref, = scratch
        kk = pl.program_id(2)

        @pl.when(kk == 0)
        def _():
            acc_ref[...] = part

        @pl.when(kk > 0)
        def _():
            acc_ref[...] += part

        @pl.when(kk == nk - 1)
        def _():
            o_ref[...] = acc_ref[...].astype(out_dtype)

    return pl.pallas_call(
        body, name=name, grid=(m // tm, n // tn, nk),
        in_specs=[a_spec, b_spec],
        out_specs=pl.BlockSpec((tm, tn), lambda i, j, kk: (i, j)),
        out_shape=jax.ShapeDtypeStruct((m, n), out_dtype),
        scratch_shapes=[pltpu.VMEM((tm, tn), F32)] if nk > 1 else [],
        compiler_params=pltpu.CompilerParams(dimension_semantics=("parallel", "parallel", "arbitrary"),
                                             vmem_limit_bytes=VMEM_LIMIT_BYTES),
    )(a, b)


def _matmul_two_lhs(a1, a2, b, out_dtype, name):
    (m, k1), n = a1.shape, b.shape[0]
    tn, tk = _tile(n, 1792), _tile(k1, 1792)
    tm = _tile(m, min(1024, MM_ACC_ELEMS // tn))
    nk1 = k1 // tk
    nk = 2 * nk1

    def body(a1_ref, a2_ref, b_ref, o_ref, acc_ref):
        kk = pl.program_id(2)
        lhs = jnp.where(kk < nk1, a1_ref[...], a2_ref[...])
        part = lax.dot_general(lhs, b_ref[...], (((1,), (1,)), ((), ())), preferred_element_type=F32)

        @pl.when(kk == 0)
        def _():
            acc_ref[...] = part

        @pl.when(kk > 0)
        def _():
            acc_ref[...] += part

        @pl.when(kk == nk - 1)
        def _():
            o_ref[...] = acc_ref[...].astype(out_dtype)

    return pl.pallas_call(
        body, name=name, grid=(m // tm, n // tn, nk),
        in_specs=[pl.BlockSpec((tm, tk), lambda i, j, kk: (i, jnp.minimum(kk, nk1 - 1))),
                  pl.BlockSpec((tm, tk), lambda i, j, kk: (i, jnp.maximum(kk - nk1, 0))),
                  pl.BlockSpec((tn, tk), lambda i, j, kk: (j, kk))],
        out_specs=pl.BlockSpec((tm, tn), lambda i, j, kk: (i, j)),
        out_shape=jax.ShapeDtypeStruct((m, n), out_dtype),
        scratch_shapes=[pltpu.VMEM((tm, tn), F32)],
        compiler_params=pltpu.CompilerParams(dimension_semantics=("parallel", "parallel", "arbitrary"),
                                             vmem_limit_bytes=VMEM_LIMIT_BYTES),
    )(a1, a2, b)


def _matmul_two_rhs(a, b1, b2, out_dtype, name):
    (k, m), n1 = a.shape, b1.shape[1]
    tn, tk = _tile(n1, 1792), _tile(k, 1792)
    tm = _tile(m, min(1024, MM_ACC_ELEMS // tn))
    nj1, nk = n1 // tn, k // tk

    def body(a_ref, b1_ref, b2_ref, o_ref, acc_ref):
        j, kk = pl.program_id(1), pl.program_id(2)
        rhs = jnp.where(j < nj1, b1_ref[...], b2_ref[...])
        part = lax.dot_general(a_ref[...], rhs, (((0,), (0,)), ((), ())), preferred_element_type=F32)

        @pl.when(kk == 0)
        def _():
            acc_ref[...] = part

        @pl.when(kk > 0)
        def _():
            acc_ref[...] += part

        @pl.when(kk == nk - 1)
        def _():
            o_ref[...] = acc_ref[...].astype(out_dtype)

    return pl.pallas_call(
        body, name=name, grid=(m // tm, 2 * nj1, nk),
        in_specs=[pl.BlockSpec((tk, tm), lambda i, j, kk: (kk, i)),
                  pl.BlockSpec((tk, tn), lambda i, j, kk: (jnp.where(j < nj1, kk, 0), jnp.minimum(j, nj1 - 1))),
                  pl.BlockSpec((tk, tn), lambda i, j, kk: (jnp.where(j < nj1, 0, kk), jnp.maximum(j - nj1, 0)))],
        out_specs=pl.BlockSpec((tm, tn), lambda i, j, kk: (i, j)),
        out_shape=jax.ShapeDtypeStruct((m, 2 * n1), out_dtype),
        scratch_shapes=[pltpu.VMEM((tm, tn), F32)],
        compiler_params=pltpu.CompilerParams(dimension_semantics=("parallel", "parallel", "arbitrary"),
                                             vmem_limit_bytes=VMEM_LIMIT_BYTES),
    )(a, b1, b2)


NORM_ROWS = 256


def _row_spec(width, rows=NORM_ROWS):
    return pl.BlockSpec((rows, width), lambda i: (i, 0))


def _vec_spec(width):
    return pl.BlockSpec((1, width), lambda i: (0, 0))


def _resid_norm_mod(x, r, g, nw, sc, sh, name):
    s, d = x.shape
    has_r = r is not None

    def body(*refs):
        if has_r:
            x_ref, r_ref, g_ref, nw_ref, sc_ref, sh_ref, xn_ref, h_ref = refs
            xn = x_ref[...] + g_ref[...] * r_ref[...].astype(F32)
            xn_ref[...] = xn
        else:
            x_ref, nw_ref, sc_ref, sh_ref, h_ref = refs
            xn = x_ref[...]
        rstd = lax.rsqrt(jnp.mean(xn * xn, axis=-1, keepdims=True) + EPS)
        y = xn * rstd * nw_ref[...]
        h_ref[...] = (y * (1.0 + sc_ref[...]) + sh_ref[...]).astype(BF16)

    if has_r:
        ins, in_specs = (x, r, g, nw, sc, sh), [_row_spec(d), _row_spec(d)] + [_vec_spec(d)] * 4
        out_shape = (jax.ShapeDtypeStruct((s, d), F32), jax.ShapeDtypeStruct((s, d), BF16))
        out_specs = (_row_spec(d), _row_spec(d))
    else:
        ins, in_specs = (x, nw, sc, sh), [_row_spec(d)] + [_vec_spec(d)] * 3
        out_shape = jax.ShapeDtypeStruct((s, d), BF16)
        out_specs = _row_spec(d)
    return pl.pallas_call(body, name=name, grid=(s // NORM_ROWS,), in_specs=in_specs, out_specs=out_specs,
                          out_shape=out_shape, compiler_params=_params(1))(*ins)


def _final_loss(x, r, g, fw, tgt, name):
    s, d = x.shape

    def body(x_ref, r_ref, g_ref, fw_ref, t_ref, loss_ref, dx_ref, dr_ref, dg_ref, dfw_ref):
        @pl.when(pl.program_id(0) == 0)
        def _():
            loss_ref[...] = jnp.zeros_like(loss_ref)
            dg_ref[...] = jnp.zeros_like(dg_ref)
            dfw_ref[...] = jnp.zeros_like(dfw_ref)

        rr = r_ref[...].astype(F32)
        gg = g_ref[...]
        xn = x_ref[...] + gg * rr
        rstd = lax.rsqrt(jnp.mean(xn * xn, axis=-1, keepdims=True) + EPS)
        xh = xn * rstd
        fwv = fw_ref[...]
        e = xh * fwv - t_ref[...]
        loss_ref[...] += 0.5 * jnp.sum(jnp.mean(e * e, axis=-1, keepdims=True), axis=0, keepdims=True)
        dy = e * (1.0 / d)
        dfw_ref[...] += jnp.sum(dy * xh, axis=0, keepdims=True)
        dxh = dy * fwv
        dx = rstd * (dxh - xh * jnp.mean(dxh * xh, axis=-1, keepdims=True))
        dx_ref[...] = dx
        dr_ref[...] = (gg * dx).astype(BF16)
        dg_ref[...] += jnp.sum(dx * rr, axis=0, keepdims=True)

    return pl.pallas_call(
        body, name=name, grid=(s // NORM_ROWS,),
        in_specs=[_row_spec(d), _row_spec(d), _vec_spec(d), _vec_spec(d), _row_spec(d)],
        out_specs=(_vec_spec(LANES), _row_spec(d), _row_spec(d), _vec_spec(d), _vec_spec(d)),
        out_shape=(jax.ShapeDtypeStruct((1, LANES), F32), jax.ShapeDtypeStruct((s, d), F32),
                   jax.ShapeDtypeStruct((s, d), BF16), jax.ShapeDtypeStruct((1, d), F32),
                   jax.ShapeDtypeStruct((1, d), F32)),
        compiler_params=_params(1))(x, r, g, fw, tgt)


def _norm_bwd(x, dhs, dxres, nw, sc, g, r, name):
    s, d = x.shape
    n_dh = len(dhs)
    has_g = g is not None

    def body(*refs):
        x_ref = refs[0]
        dh_refs = refs[1:1 + n_dh]
        dxres_ref, nw_ref, sc_ref = refs[1 + n_dh:4 + n_dh]
        pos = 4 + n_dh
        if has_g:
            g_ref, r_ref = refs[pos:pos + 2]
            pos += 2
            dx_ref, dr_ref, dsh_ref, dsc_ref, dnw_ref, dg_ref = refs[pos:]
            accs = (dsh_ref, dsc_ref, dnw_ref, dg_ref)
        else:
            dx_ref, dsh_ref, dsc_ref, dnw_ref = refs[pos:]
            accs = (dsh_ref, dsc_ref, dnw_ref)

        @pl.when(pl.program_id(0) == 0)
        def _():
            for acc in accs:
                acc[...] = jnp.zeros_like(acc)

        xv = x_ref[...]
        dh = dh_refs[0][...].astype(F32)
        for extra in dh_refs[1:]:
            dh = dh + extra[...].astype(F32)
        rstd = lax.rsqrt(jnp.mean(xv * xv, axis=-1, keepdims=True) + EPS)
        xh = xv * rstd
        nwv = nw_ref[...]
        dsh_ref[...] += jnp.sum(dh, axis=0, keepdims=True)
        dsc_ref[...] += jnp.sum(dh * (xh * nwv), axis=0, keepdims=True)
        dy = dh * (1.0 + sc_ref[...])
        dnw_ref[...] += jnp.sum(dy * xh, axis=0, keepdims=True)
        dxh = dy * nwv
        dx = dxres_ref[...] + rstd * (dxh - xh * jnp.mean(dxh * xh, axis=-1, keepdims=True))
        dx_ref[...] = dx
        if has_g:
            dr_ref[...] = (g_ref[...] * dx).astype(BF16)
            dg_ref[...] += jnp.sum(dx * r_ref[...].astype(F32), axis=0, keepdims=True)

    ins = [x, *dhs, dxres, nw, sc]
    in_specs = [_row_spec(d)] * (2 + n_dh) + [_vec_spec(d)] * 2
    out_shape = [jax.ShapeDtypeStruct((s, d), F32)]
    out_specs = [_row_spec(d)]
    if has_g:
        ins += [g, r]
        in_specs += [_vec_spec(d), _row_spec(d)]
        out_shape.append(jax.ShapeDtypeStruct((s, d), BF16))
        out_specs.append(_row_spec(d))
    n_vec = 4 if has_g else 3
    out_shape += [jax.ShapeDtypeStruct((1, d), F32)] * n_vec
    out_specs += [_vec_spec(d)] * n_vec
    return pl.pallas_call(body, name=name, grid=(s // NORM_ROWS,), in_specs=in_specs, out_specs=tuple(out_specs),
                          out_shape=tuple(out_shape), compiler_params=_params(1))(*ins)


FFN_ROWS = 256
FFN_COLS = 1408
HALO = 16
INV_SQRT2 = 0.7071067811865476
INV_SQRT_2PI = 0.3989422804014327


def _gelu(x):
    return 0.5 * x * (1.0 + lax.erf(x * INV_SQRT2))


def _gelu_grad(x):
    return 0.5 * (1.0 + lax.erf(x * INV_SQRT2)) + x * (INV_SQRT_2PI * jnp.exp(-0.5 * x * x))


def _halo_specs(rows, cols, halo, n_rows_total, col_of):
    per = rows // halo
    last = n_rows_total // halo - 1
    cur = pl.BlockSpec((rows, cols), lambda j, i: (i, col_of(j)))
    prev = pl.BlockSpec((halo, cols), lambda j, i: (jnp.maximum(i * per - 1, 0), col_of(j)))
    nxt = pl.BlockSpec((halo, cols), lambda j, i: (jnp.minimum((i + 1) * per, last), col_of(j)))
    return [prev, cur, nxt]


def _shift_rows(x, k):
    n = x.shape[0]
    return pltpu.roll(x, k % n, axis=0)


def _conv3(ext, w):
    return w[0:1, :] * _shift_rows(ext, 1) + w[1:2, :] * ext + w[2:3, :] * _shift_rows(ext, -1)


def _ext_block(prev_ref, cur_ref, next_ref, i, n_i):
    prev = jnp.where(i > 0, prev_ref[...].astype(F32), 0.0)
    nxt = jnp.where(i < n_i - 1, next_ref[...].astype(F32), 0.0)
    return jnp.concatenate([prev, cur_ref[...].astype(F32), nxt], axis=0)


def _ffn_act(u, cw, name):
    s = u.shape[0]
    nc, ns = D_FF // FFN_COLS, s // FFN_ROWS

    def body(gp, gc, gn, vp, vc, vn, wg_ref, wv_ref, o_ref, cg_ref, cv_ref):
        i = pl.program_id(1)
        cg = _conv3(_ext_block(gp, gc, gn, i, ns), wg_ref[...])[HALO:HALO + FFN_ROWS]
        cv = _conv3(_ext_block(vp, vc, vn, i, ns), wv_ref[...])[HALO:HALO + FFN_ROWS]
        o_ref[...] = (_gelu(cg) * cv).astype(BF16)
        cg_ref[...] = cg.astype(BF16)
        cv_ref[...] = cv.astype(BF16)

    in_specs = (_halo_specs(FFN_ROWS, FFN_COLS, HALO, s, lambda j: j)
                + _halo_specs(FFN_ROWS, FFN_COLS, HALO, s, lambda j: j + nc)
                + [pl.BlockSpec((3, FFN_COLS), lambda j, i: (0, j)),
                   pl.BlockSpec((3, FFN_COLS), lambda j, i: (0, j + nc))])
    blk = pl.BlockSpec((FFN_ROWS, FFN_COLS), lambda j, i: (i, j))
    return pl.pallas_call(
        body, name=name, grid=(nc, ns), in_specs=in_specs, out_specs=(blk, blk, blk),
        out_shape=(jax.ShapeDtypeStruct((s, D_FF), BF16),) * 3, compiler_params=_params(2),
    )(u, u, u, u, u, u, cw, cw)


def _ffn_act_bwd(u, cg, cv, dact, cw, name):
    s = u.shape[0]
    nc, ns = D_FF // FFN_COLS, s // FFN_ROWS

    def body(ug_ref, uv_ref, gp, gc, gn, vp, vc, vn, dp, dc, dn, wg_ref, wv_ref, dug_ref, duv_ref, dwg_ref, dwv_ref):
        i = pl.program_id(1)

        @pl.when(i == 0)
        def _():
            dwg_ref[...] = jnp.zeros_like(dwg_ref)
            dwv_ref[...] = jnp.zeros_like(dwv_ref)

        cge = _ext_block(gp, gc, gn, i, ns)
        cve = _ext_block(vp, vc, vn, i, ns)
        da = _ext_block(dp, dc, dn, i, ns)
        dcg = da * cve * _gelu_grad(cge)
        dcv = da * _gelu(cge)
        inner = slice(HALO, HALO + FFN_ROWS)
        for d_c, u_ref, w_ref, du_ref, dw_ref in ((dcg, ug_ref, wg_ref, dug_ref, dwg_ref),
                                                  (dcv, uv_ref, wv_ref, duv_ref, dwv_ref)):
            w = w_ref[...]
            d_next, d_prev = _shift_rows(d_c, -1), _shift_rows(d_c, 1)
            du = w[0:1, :] * d_next + w[1:2, :] * d_c + w[2:3, :] * d_prev
            du_ref[...] = du[inner].astype(BF16)
            u_in = u_ref[...].astype(F32)
            for tap, d_tap in enumerate((d_next, d_c, d_prev)):
                dw_ref[tap:tap + 1, :] += jnp.sum(d_tap[inner] * u_in, axis=0, keepdims=True)

    blk = pl.BlockSpec((FFN_ROWS, FFN_COLS), lambda j, i: (i, j))
    in_specs = ([blk, pl.BlockSpec((FFN_ROWS, FFN_COLS), lambda j, i: (i, j + nc))]
                + _halo_specs(FFN_ROWS, FFN_COLS, HALO, s, lambda j: j) * 3
                + [pl.BlockSpec((3, FFN_COLS), lambda j, i: (0, j)),
                   pl.BlockSpec((3, FFN_COLS), lambda j, i: (0, j + nc))])
    acc = pl.BlockSpec((HALO, FFN_COLS), lambda j, i: (0, j))
    return pl.pallas_call(
        body, name=name, grid=(nc, ns), in_specs=in_specs, out_specs=(blk, blk, acc, acc),
        out_shape=(jax.ShapeDtypeStruct((s, D_FF), BF16), jax.ShapeDtypeStruct((s, D_FF), BF16),
                   jax.ShapeDtypeStruct((HALO, D_FF), F32), jax.ShapeDtypeStruct((HALO, D_FF), F32)),
        compiler_params=_params(2),
    )(u, u, cg, cg, cg, cv, cv, cv, dact, dact, dact, cw, cw)


CONV_ROWS = 512
CONV_HALO = 16
CONV_PAD = CONV_WIDTH // 2


def _conv_halo_specs(cols, s):
    per = CONV_ROWS // CONV_HALO
    last = s // CONV_HALO - 1
    return [pl.BlockSpec((CONV_HALO, cols), lambda i: (jnp.maximum(i * per - 1, 0), 0)),
            pl.BlockSpec((CONV_ROWS, cols), lambda i: (i, 0)),
            pl.BlockSpec((CONV_HALO, cols), lambda i: (jnp.minimum((i + 1) * per, last), 0))]


def _glu_ext(pp, pc, pn, i, n_i):
    ext = _ext_block(pp, pc, pn, i, n_i)
    return ext[:, :CONV_CH] * _sigmoid(ext[:, CONV_CH:])


def _conv_mixer(pa, cw, cb, lnw, lnb, name):
    s = pa.shape[0]
    ns = s // CONV_ROWS

    def body(pp, pc, pn, cw_ref, cb_ref, lnw_ref, lnb_ref, o_ref, c_ref):
        i = pl.program_id(0)
        a = _glu_ext(pp, pc, pn, i, ns)
        acc = jnp.zeros((CONV_ROWS, CONV_CH), F32)
        for tap in range(CONV_WIDTH):
            acc = acc + cw_ref[tap:tap + 1, :] * _shift_rows(a, -(tap + 1))[:CONV_ROWS]
        cv = acc + cb_ref[...]
        c_ref[...] = cv
        mu = jnp.mean(cv, axis=-1, keepdims=True)
        xc = cv - mu
        rstd = lax.rsqrt(jnp.mean(xc * xc, axis=-1, keepdims=True) + EPS)
        y = xc * rstd * lnw_ref[...] + lnb_ref[...]
        o_ref[...] = (y * _sigmoid(y)).astype(BF16)

    vec = pl.BlockSpec((1, CONV_CH), lambda i: (0, 0))
    blk = pl.BlockSpec((CONV_ROWS, CONV_CH), lambda i: (i, 0))
    return pl.pallas_call(
        body, name=name, grid=(ns,),
        in_specs=_conv_halo_specs(2 * CONV_CH, s) + [pl.BlockSpec((CONV_WIDTH, CONV_CH), lambda i: (0, 0)), vec, vec, vec],
        out_specs=(blk, blk),
        out_shape=(jax.ShapeDtypeStruct((s, CONV_CH), BF16), jax.ShapeDtypeStruct((s, CONV_CH), F32)),
        compiler_params=_params(1))(pa, pa, pa, cw, cb, lnw, lnb)


def _conv_mixer_bwd_ln(cv, dout, lnw, lnb, name):
    s = cv.shape[0]

    def body(c_ref, do_ref, lnw_ref, lnb_ref, dc_ref, dlnw_ref, dlnb_ref, dcb_ref):
        @pl.when(pl.program_id(0) == 0)
        def _():
            dlnw_ref[...] = jnp.zeros_like(dlnw_ref)
            dlnb_ref[...] = jnp.zeros_like(dlnb_ref)
            dcb_ref[...] = jnp.zeros_like(dcb_ref)

        c = c_ref[...]
        mu = jnp.mean(c, axis=-1, keepdims=True)
        xc = c - mu
        rstd = lax.rsqrt(jnp.mean(xc * xc, axis=-1, keepdims=True) + EPS)
        xh = xc * rstd
        w = lnw_ref[...]
        y = xh * w + lnb_ref[...]
        dy = do_ref[...] * _silu_grad(y)
        dlnw_ref[...] += jnp.sum(dy * xh, axis=0, keepdims=True)
        dlnb_ref[...] += jnp.sum(dy, axis=0, keepdims=True)
        dxh = dy * w
        dc = rstd * (dxh - jnp.mean(dxh, axis=-1, keepdims=True) - xh * jnp.mean(dxh * xh, axis=-1, keepdims=True))
        dc_ref[...] = dc
        dcb_ref[...] += jnp.sum(dc, axis=0, keepdims=True)

    vec = pl.BlockSpec((1, CONV_CH), lambda i: (0, 0))
    blk = pl.BlockSpec((CONV_ROWS, CONV_CH), lambda i: (i, 0))
    return pl.pallas_call(
        body, name=name, grid=(s // CONV_ROWS,), in_specs=[blk, blk, vec, vec], out_specs=(blk, vec, vec, vec),
        out_shape=(jax.ShapeDtypeStruct((s, CONV_CH), F32),) + (jax.ShapeDtypeStruct((1, CONV_CH), F32),) * 3,
        compiler_params=_params(1))(cv, dout, lnw, lnb)


def _conv_mixer_bwd_conv(pa, dc, cw, name):
    s = pa.shape[0]
    ns = s // CONV_ROWS

    def body(pc, dp, dcc, dn, cw_ref, dpa_ref, dcw_ref):
        i = pl.program_id(0)

        @pl.when(i == 0)
        def _():
            dcw_ref[...] = jnp.zeros_like(dcw_ref)

        cur = pc[...]
        val, sg = cur[:, :CONV_CH], _sigmoid(cur[:, CONV_CH:])
        a_cur = val * sg
        dce = _ext_block(dp, dcc, dn, i, ns)
        da = jnp.zeros((CONV_ROWS, CONV_CH), F32)
        for tap in range(CONV_WIDTH):
            shifted = _shift_rows(dce, -(CONV_WIDTH - tap))[:CONV_ROWS]
            da = da + cw_ref[tap:tap + 1, :] * shifted
            dcw_ref[tap:tap + 1, :] += jnp.sum(shifted * a_cur, axis=0, keepdims=True)
        dpa_ref[:, :CONV_CH] = (da * sg).astype(BF16)
        dpa_ref[:, CONV_CH:] = (da * val * sg * (1.0 - sg)).astype(BF16)

    return pl.pallas_call(
        body, name=name, grid=(ns,),
        in_specs=[pl.BlockSpec((CONV_ROWS, 2 * CONV_CH), lambda i: (i, 0))] + _conv_halo_specs(CONV_CH, s)
        + [pl.BlockSpec((CONV_WIDTH, CONV_CH), lambda i: (0, 0))],
        out_specs=(pl.BlockSpec((CONV_ROWS, 2 * CONV_CH), lambda i: (i, 0)),
                   pl.BlockSpec((32, CONV_CH), lambda i: (0, 0))),
        out_shape=(jax.ShapeDtypeStruct((s, 2 * CONV_CH), BF16), jax.ShapeDtypeStruct((32, CONV_CH), F32)),
        compiler_params=_params(1))(pa, dc, dc, dc, cw)


SLOPES = tuple(float(2.0 ** (-ALIBI_MAX_EXP * (h + 1) / N_HEADS)) for h in range(N_HEADS))
ATT_SCALE = HEAD_DIM ** -0.5


PAIR = 2 * HEAD_DIM
N_PAIRS = N_HEADS // 2
ATT_WIN = ATT_BLOCK + 2 * ATT_HALF


ATT_GROUPS = {1: 4, 4: 1, 16: 1}


def _window_specs(dil, n_steps, col_of):
    per = 2 * ATT_GROUPS[dil]
    rows, halo = ATT_BLOCK * dil * ATT_GROUPS[dil], ATT_HALF * dil
    return [pl.BlockSpec((halo, PAIR), lambda i, p: (jnp.maximum(per * i - 1, 0), col_of(p))),
            pl.BlockSpec((rows, PAIR), lambda i, p: (i, col_of(p))),
            pl.BlockSpec((halo, PAIR), lambda i, p: (jnp.minimum(per * (i + 1), per * n_steps - 1), col_of(p)))]


def _residue(ref, r, n, dil, start=0):
    return ref[pl.ds(start * dil + r, n, stride=dil), :] if dil > 1 else ref[pl.ds(start + r, n), :]


def _store_residue(ref, r, dil, start, val):
    if dil > 1:
        ref[pl.ds(start * dil + r, val.shape[0], stride=dil), :] = val
    else:
        ref[pl.ds(start + r, val.shape[0]), :] = val


def _residue_window(refs, r, dil, g=0):
    prev, cur, nxt = refs
    groups = ATT_GROUPS[dil]
    lo = max(g * ATT_BLOCK - ATT_HALF, 0)
    hi = min((g + 1) * ATT_BLOCK + ATT_HALF, groups * ATT_BLOCK)
    parts = [_residue(prev, r, ATT_HALF, dil)] if g == 0 else []
    parts.append(_residue(cur, r, hi - lo, dil, lo))
    if g == groups - 1:
        parts.append(_residue(nxt, r, ATT_HALF, dil))
    return jnp.concatenate(parts, axis=0)


def _band_masks(i, length, dil, transposed):
    shape = (ATT_WIN, ATT_BLOCK) if transposed else (ATT_BLOCK, ATT_WIN)
    row = lax.broadcasted_iota(jnp.int32, shape, 0)
    col = lax.broadcasted_iota(jnp.int32, shape, 1)
    wide = row if transposed else col
    dist = jnp.abs((row - col - ATT_HALF) if transposed else (row + ATT_HALF - col))
    wpos = i * ATT_BLOCK - ATT_HALF + wide
    valid = (dist <= ATT_HALF) & (wpos >= 0) & (wpos < length)
    return valid, dist.astype(F32) * float(dil)


def _attn_branch(qkv, dil, name):
    s = qkv.shape[0]
    groups = ATT_GROUPS[dil]
    rows = ATT_BLOCK * dil * groups
    n_steps = s // rows
    length = s // dil
    nt = (((1,), (1,)), ((), ()))

    def body(q_ref, kp, kc, kn, vp, vc, vn, o_ref, l_ref):
        i, pair = pl.program_id(0), pl.program_id(1)
        items = [(g, r) for g in range(groups) for r in range(dil)]
        q = jnp.stack([_residue(q_ref, r, ATT_BLOCK, dil, g * ATT_BLOCK) for g, r in items]).astype(BF16)
        k = jnp.stack([_residue_window((kp, kc, kn), r, dil, g) for g, r in items]).astype(BF16)
        v = jnp.stack([_residue_window((vp, vc, vn), r, dil, g) for g, r in items]).astype(BF16)
        per_group = [_band_masks(i * groups + g, length, dil, False) for g in range(groups)]
        valid = jnp.stack([per_group[g][0] for g, _ in items]) if groups > 1 else per_group[0][0][None]
        distf = jnp.stack([per_group[g][1] for g, _ in items]) if groups > 1 else per_group[0][1][None]
        outs, lses = [], []
        for hh in range(2):
            sl = slice(hh * HEAD_DIM, (hh + 1) * HEAD_DIM)
            slope = jnp.where(pair == 0, SLOPES[hh], jnp.where(pair == 1, SLOPES[2 + hh], SLOPES[4 + hh]))
            sc = jnp.einsum("bqd,bkd->bqk", q[:, :, sl], k[:, :, sl], preferred_element_type=F32) * ATT_SCALE
            sc = jnp.where(valid, sc - slope * distf, MASK_VALUE)
            m = jnp.max(sc, axis=-1, keepdims=True)
            p = jnp.exp(sc - m)
            den = jnp.sum(p, axis=-1, keepdims=True)
            outs.append(jnp.einsum("bqk,bkd->bqd", p.astype(BF16), v[:, :, sl], preferred_element_type=F32) / den)
            lses.append(jnp.broadcast_to(m + jnp.log(den), (len(items), ATT_BLOCK, HEAD_DIM)))
        o_all, l_all = jnp.concatenate(outs, axis=2), jnp.concatenate(lses, axis=2)
        for n, (g, r) in enumerate(items):
            _store_residue(o_ref, r, dil, g * ATT_BLOCK, o_all[n])
            _store_residue(l_ref, r, dil, g * ATT_BLOCK, l_all[n])

    out_blk = pl.BlockSpec((rows, PAIR), lambda i, p: (i, p))
    return pl.pallas_call(
        body, name=name, grid=(n_steps, N_PAIRS),
        in_specs=[pl.BlockSpec((rows, PAIR), lambda i, p: (i, p))]
        + _window_specs(dil, n_steps, lambda p: N_PAIRS + p) + _window_specs(dil, n_steps, lambda p: 2 * N_PAIRS + p),
        out_specs=(out_blk, out_blk),
        out_shape=(jax.ShapeDtypeStruct((s, ATT_WIDTH), F32),) * 2,
        compiler_params=_params(2))(qkv, qkv, qkv, qkv, qkv, qkv, qkv)


ATT_ROWS = 512


def _attn_combine(outs, lses, name):
    s = outs[0].shape[0]

    def body(o1, o2, o3, l1, l2, l3, att_ref, att32_ref, lse_ref):
        ls = [l1[...], l2[...], l3[...]]
        m = jnp.maximum(jnp.maximum(ls[0], ls[1]), ls[2])
        es = [jnp.exp(l - m) for l in ls]
        den = es[0] + es[1] + es[2]
        att = (es[0] * o1[...] + es[1] * o2[...] + es[2] * o3[...]) / den
        att_ref[...] = att.astype(BF16)
        att32_ref[...] = att
        lse_ref[...] = m + jnp.log(den)

    blk = pl.BlockSpec((ATT_ROWS, ATT_WIDTH), lambda i: (i, 0))
    return pl.pallas_call(
        body, name=name, grid=(s // ATT_ROWS,), in_specs=[blk] * 6, out_specs=(blk, blk, blk),
        out_shape=(jax.ShapeDtypeStruct((s, ATT_WIDTH), BF16), jax.ShapeDtypeStruct((s, ATT_WIDTH), F32),
                   jax.ShapeDtypeStruct((s, ATT_WIDTH), F32)),
        compiler_params=_params(1))(*outs, *lses)


def _attn_delta(datt, att, name):
    s = att.shape[0]

    def body(d_ref, a_ref, delta_ref):
        prod = d_ref[...] * a_ref[...]
        for h in range(N_HEADS):
            sl = slice(h * HEAD_DIM, (h + 1) * HEAD_DIM)
            delta_ref[:, sl] = jnp.broadcast_to(jnp.sum(prod[:, sl], axis=-1, keepdims=True), (ATT_ROWS, HEAD_DIM))

    blk = pl.BlockSpec((ATT_ROWS, ATT_WIDTH), lambda i: (i, 0))
    return pl.pallas_call(
        body, name=name, grid=(s // ATT_ROWS,), in_specs=[blk, blk], out_specs=blk,
        out_shape=jax.ShapeDtypeStruct((s, ATT_WIDTH), F32), compiler_params=_params(1))(datt, att)


def _attn_branch_bwd(qkv, do, lse, delta, prev, dil, out_dtype, name):
    s = qkv.shape[0]
    groups = ATT_GROUPS[dil]
    rows = ATT_BLOCK * dil * groups
    n_steps = s // rows
    length = s // dil
    has_prev = prev is not None
    tn = (((0,), (0,)), ((), ()))
    nt = (((1,), (1,)), ((), ()))

    def body(*refs):
        qs, ks, vs, dos, ls, des = (refs[3 * n:3 * n + 3] for n in range(6))
        rest = refs[18:]
        if has_prev:
            pq, pk, pv = rest[:3]
            rest = rest[3:]
        dq_ref, dk_ref, dv_ref = rest
        i, pair = pl.program_id(0), pl.program_id(1)
        items = [(g, r) for g in range(groups) for r in range(dil)]
        cur = lambda t: jnp.stack([_residue(t[1], r, ATT_BLOCK, dil, g * ATT_BLOCK) for g, r in items])
        win = lambda t: jnp.stack([_residue_window(t, r, dil, g) for g, r in items])
        q_cur, k_cur, v_cur, do_cur = (cur(t).astype(BF16) for t in (qs, ks, vs, dos))
        q_win, k_win, v_win, do_win = (win(t).astype(BF16) for t in (qs, ks, vs, dos))
        l_cur, de_cur, l_win, de_win = cur(ls), cur(des), win(ls), win(des)

        def masks(transposed):
            per_group = [_band_masks(i * groups + g, length, dil, transposed) for g in range(groups)]
            if groups == 1:
                return per_group[0][0][None], per_group[0][1][None]
            return jnp.stack([per_group[g][0] for g, _ in items]), jnp.stack([per_group[g][1] for g, _ in items])

        valid_q, distf_q = masks(False)
        valid_k, distf_k = masks(True)
        dot = lambda eq, a, b: jnp.einsum(eq, a, b, preferred_element_type=F32)
        dqs, dks, dvs = [], [], []
        for hh in range(2):
            sl = slice(hh * HEAD_DIM, (hh + 1) * HEAD_DIM)
            one = slice(hh * HEAD_DIM, hh * HEAD_DIM + 1)
            slope = jnp.where(pair == 0, SLOPES[hh], jnp.where(pair == 1, SLOPES[2 + hh], SLOPES[4 + hh]))
            sc = dot("bqd,bkd->bqk", q_cur[:, :, sl], k_win[:, :, sl]) * ATT_SCALE - slope * distf_q
            p = jnp.exp(jnp.where(valid_q, sc - l_cur[:, :, one], MASK_VALUE))
            dp = dot("bqd,bkd->bqk", do_cur[:, :, sl], v_win[:, :, sl])
            ds = (p * (dp - de_cur[:, :, one]) * ATT_SCALE).astype(BF16)
            dqs.append(dot("bqk,bkd->bqd", ds, k_win[:, :, sl]))

            sc2 = dot("bqd,bkd->bqk", q_win[:, :, sl], k_cur[:, :, sl]) * ATT_SCALE - slope * distf_k
            p2 = jnp.exp(jnp.where(valid_k, sc2 - l_win[:, :, one], MASK_VALUE))
            dvs.append(dot("bqk,bqd->bkd", p2.astype(BF16), do_win[:, :, sl]))
            dp2 = dot("bqd,bkd->bqk", do_win[:, :, sl], v_cur[:, :, sl])
            ds2 = (p2 * (dp2 - de_win[:, :, one]) * ATT_SCALE).astype(BF16)
            dks.append(dot("bqk,bqd->bkd", ds2, q_win[:, :, sl]))
        for parts, acc, out in ((dqs, pq if has_prev else None, dq_ref), (dks, pk if has_prev else None, dk_ref),
                                (dvs, pv if has_prev else None, dv_ref)):
            val = jnp.concatenate(parts, axis=2)
            for n, (g, r) in enumerate(items):
                piece = val[n]
                if has_prev:
                    piece = piece + _residue(acc, r, ATT_BLOCK, dil, g * ATT_BLOCK)
                _store_residue(out, r, dil, g * ATT_BLOCK, piece.astype(out_dtype))

    blk = pl.BlockSpec((rows, PAIR), lambda i, p: (i, p))
    in_specs = (_window_specs(dil, n_steps, lambda p: p) + _window_specs(dil, n_steps, lambda p: N_PAIRS + p)
                + _window_specs(dil, n_steps, lambda p: 2 * N_PAIRS + p) + _window_specs(dil, n_steps, lambda p: p) * 3)
    ins = [qkv] * 9 + [do] * 3 + [lse] * 3 + [delta] * 3
    if has_prev:
        in_specs += [blk] * 3
        ins += list(prev)
    return pl.pallas_call(
        body, name=name, grid=(n_steps, N_PAIRS), in_specs=in_specs, out_specs=(blk, blk, blk),
        out_shape=(jax.ShapeDtypeStruct((s, ATT_WIDTH), out_dtype),) * 3,
        compiler_params=_params(2))(*ins)


TB = 2 * REC_CHUNK
REC_ROWS = 5 * REC_WIDTH


REC_LEVELS = 6


def _scan_pos(p, rev):
    p = p & (REC_CHUNK - 1)
    return (REC_CHUNK - 1 - p) if rev else p


def _split3(x):
    hi = x.astype(BF16)
    rest = x - hi.astype(F32)
    mid = rest.astype(BF16)
    return hi, mid, (rest - mid.astype(F32)).astype(BF16)


def _chunk_sums(x, rev, with_levels):
    row = lax.broadcasted_iota(jnp.int32, (TB, TB), 0)
    col = lax.broadcasted_iota(jnp.int32, (TB, TB), 1)
    same = (row < REC_CHUNK) == (col < REC_CHUNK)
    s_row, s_col = _scan_pos(row, rev), _scan_pos(col, rev)
    mats = [same & (s_row <= s_col)]
    if with_levels:
        for level in range(1, REC_LEVELS + 1):
            shift = REC_LEVELS + 1 - level
            boundary = ((s_col >> shift) << shift) + (REC_CHUNK >> level) - 1
            mats.append(same & (s_row <= boundary))
        mats.append(same)
    cat = jnp.concatenate([m.astype(BF16) for m in mats], axis=1)
    total = sum(jnp.dot(term, cat, preferred_element_type=F32) for term in _split3(x))
    return [total[:, n * TB:(n + 1) * TB] for n in range(len(mats))]


def _hg_prep(qraw, z, lb, rev):
    lane = lax.broadcasted_iota(jnp.int32, (REC_WIDTH, TB), 1)
    in_a = lane < REC_CHUNK
    scan = _scan_pos(lane, rev)
    sig, sigm = _sigmoid(z), _sigmoid(-z)
    f = lb + (1.0 - lb) * sig
    kk = (1.0 - lb) * sigm
    sums = _chunk_sums(jnp.log(jnp.maximum(f, F_TINY)), rev, True)
    b, bend = sums[0], sums[-1]
    q = qraw * _sigmoid(qraw)
    eq, ek = [], []
    for level in range(1, REC_LEVELS + 1):
        r = sums[level]
        e = jnp.exp(jnp.minimum(b - r, r - b))
        second = ((scan >> (REC_LEVELS - level)) & 1) == 1
        eq.append(jnp.where(second, e, 0.0))
        ek.append(jnp.where(second, 0.0, e))
    lanes_end = (0, REC_CHUNK) if rev else (REC_CHUNK - 1, TB - 1)
    end_a, end_b = (b[:, n:n + 1] for n in lanes_end)
    return dict(in_a=in_a, sig=sig, sigm=sigm, f=f, kk=kk, b=b, end_a=end_a, end_b=end_b,
                q=q, qh=q * jnp.exp(b), kh=kk * jnp.exp(bend - b), ekb=jnp.exp(bend - b), eq=eq, ek=ek)


def _level_masks(rev):
    row = lax.broadcasted_iota(jnp.int32, (TB, TB), 0)
    col = lax.broadcasted_iota(jnp.int32, (TB, TB), 1)
    same = (row < REC_CHUNK) == (col < REC_CHUNK)
    s_row, s_col = _scan_pos(row, rev), _scan_pos(col, rev)
    masks = [same & ((s_row >> (REC_LEVELS + 1 - level)) == (s_col >> (REC_LEVELS + 1 - level)))
             for level in range(1, REC_LEVELS + 1)]
    return masks, row == col


def _head_rows(x, h):
    return x[h * HEAD_DIM:(h + 1) * HEAD_DIM, :]


def _block_diag_mask():
    r = lax.broadcasted_iota(jnp.int32, (REC_WIDTH, REC_WIDTH), 0) // HEAD_DIM
    c = lax.broadcasted_iota(jnp.int32, (REC_WIDTH, REC_WIDTH), 1) // HEAD_DIM
    return (r == c).astype(F32)


def _heads(x):
    return x.reshape(N_HEADS, HEAD_DIM, TB)


def _hgrn_scan(projt, lb, rev, name):
    s = projt.shape[1]
    nblk = s // TB
    zrow = 2 if rev else 1
    tmap = (lambda i: nblk - 1 - i) if rev else (lambda i: i)
    tn = (((0,), (0,)), ((), ()))
    nt = (((1,), (1,)), ((), ()))

    def body(q_ref, z_ref, v_ref, lb_ref, o_ref, hs_ref, at_ref, h_ref):
        @pl.when(pl.program_id(0) == 0)
        def _():
            h_ref[...] = jnp.zeros_like(h_ref)

        v = v_ref[...]
        vb = v.astype(BF16)
        pr = _hg_prep(q_ref[...], z_ref[...], lb_ref[...], rev)
        q, kk = pr["q"], pr["kk"]
        masks, diag = _level_masks(rev)
        own = jnp.sum(_heads(q * kk), axis=1, keepdims=True)
        sc = jnp.where(diag[None], own, 0.0)
        for level in range(REC_LEVELS):
            qt = _heads((q * pr["eq"][level]).astype(BF16))
            kt = _heads((kk * pr["ek"][level]).astype(BF16))
            sc = sc + jnp.where(masks[level][None],
                                jnp.einsum("hks,hkt->hst", kt, qt, preferred_element_type=F32), 0.0)
        a_bf = sc.astype(BF16)
        at_ref[...] = a_bf
        o = jnp.einsum("hvs,hst->hvt", _heads(vb), a_bf, preferred_element_type=F32).reshape(REC_WIDTH, TB)
        bd_mask = _block_diag_mask()
        order = ((1, ~pr["in_a"], pr["end_b"]), (0, pr["in_a"], pr["end_a"]))
        if not rev:
            order = order[::-1]
        for slot, msk, bend in order:
            h0 = h_ref[...]
            hs_ref[slot] = h0
            o = o + lax.dot_general(h0.astype(BF16), jnp.where(msk, pr["qh"], 0.0).astype(BF16), tn,
                                    preferred_element_type=F32)
            upd = lax.dot_general(jnp.where(msk, pr["kh"], 0.0).astype(BF16), vb, nt, preferred_element_type=F32)
            h_ref[...] = jnp.exp(bend) * h0 + upd * bd_mask
        o_ref[...] = o

    row_blk = lambda r: pl.BlockSpec((REC_WIDTH, TB), lambda i: (r, tmap(i)))
    return pl.pallas_call(
        body, name=name, grid=(nblk,),
        in_specs=[row_blk(0), row_blk(zrow), row_blk(3), pl.BlockSpec((REC_WIDTH, 1), lambda i: (0, 0))],
        out_specs=(pl.BlockSpec((REC_WIDTH, TB), lambda i: (0, tmap(i))),
                   pl.BlockSpec((2, REC_WIDTH, REC_WIDTH), lambda i: (tmap(i), 0, 0)),
                   pl.BlockSpec((None, N_HEADS, TB, TB), lambda i: (tmap(i), 0, 0, 0))),
        out_shape=(jax.ShapeDtypeStruct((REC_WIDTH, s), F32),
                   jax.ShapeDtypeStruct((s // REC_CHUNK, REC_WIDTH, REC_WIDTH), F32),
                   jax.ShapeDtypeStruct((nblk, N_HEADS, TB, TB), BF16)),
        scratch_shapes=[pltpu.VMEM((REC_WIDTH, REC_WIDTH), F32)],
        compiler_params=_params(1))(projt, projt, projt, lb)


def _hgrn_scan_bwd(projt, lb, dot, hs, at, prev, rev, name):
    s = projt.shape[1]
    nblk = s // TB
    zrow = 2 if rev else 1
    tmap = (lambda i: i) if rev else (lambda i: nblk - 1 - i)
    has_prev = prev is not None
    tn = (((0,), (0,)), ((), ()))
    nt = (((1,), (1,)), ((), ()))

    def body(*refs):
        q_ref, z_ref, v_ref, lb_ref, do_ref, hs_ref, at_ref = refs[:7]
        rest = refs[7:]
        if has_prev:
            pq_ref, pv_ref = rest[:2]
            rest = rest[2:]
        dq_ref, dz_ref, dv_ref, dlb_ref, dh_ref = rest

        @pl.when(pl.program_id(0) == 0)
        def _():
            dh_ref[...] = jnp.zeros_like(dh_ref)
            dlb_ref[...] = jnp.zeros_like(dlb_ref)

        qraw, v, do, lbv = q_ref[...], v_ref[...], do_ref[...], lb_ref[...]
        dob, vb = do.astype(BF16), v.astype(BF16)
        pr = _hg_prep(qraw, z_ref[...], lbv, rev)
        q, kk, b, in_a = pr["q"], pr["kk"], pr["b"], pr["in_a"]
        masks, diag = _level_masks(rev)
        dot = lambda eq, x, y: jnp.einsum(eq, x, y, preferred_element_type=F32)
        d_at = dot("hvs,hvt->hst", _heads(vb), _heads(dob))
        dv = dot("hvt,hst->hvs", _heads(dob), at_ref[...]).reshape(REC_WIDTH, TB)
        d_own = jnp.sum(jnp.where(diag[None], d_at, 0.0), axis=1, keepdims=True)
        dq_in = (d_own * _heads(kk)).reshape(REC_WIDTH, TB)
        dk_in = (d_own * _heads(q)).reshape(REC_WIDTH, TB)
        db_in = jnp.zeros((REC_WIDTH, TB), F32)
        for lv in range(REC_LEVELS):
            d_lv = jnp.where(masks[lv][None], d_at, 0.0).astype(BF16)
            q_lv, k_lv = (q * pr["eq"][lv]).astype(BF16), (kk * pr["ek"][lv]).astype(BF16)
            dqt = dot("hks,hst->hkt", _heads(k_lv), d_lv).reshape(REC_WIDTH, TB)
            dkt = dot("hkt,hst->hks", _heads(q_lv), d_lv).reshape(REC_WIDTH, TB)
            dq_in = dq_in + pr["eq"][lv] * dqt
            dk_in = dk_in + pr["ek"][lv] * dkt
            db_in = db_in + q_lv.astype(F32) * dqt - k_lv.astype(F32) * dkt
        dq = dk = jnp.zeros((REC_WIDTH, TB), F32)

        zero = jnp.zeros((REC_WIDTH, TB), F32)
        bd_mask = _block_diag_mask()
        eb = jnp.exp(b)
        const = zero
        order = ((0, in_a, pr["end_a"]), (1, ~in_a, pr["end_b"]))
        if not rev:
            order = order[::-1]
        for slot, msk, bend in order:
            h0 = hs_ref[slot]
            dh1 = dh_ref[...]
            dh1b = dh1.astype(BF16)
            dq = dq + eb * jnp.dot(h0.astype(BF16), jnp.where(msk, do, 0.0).astype(BF16), preferred_element_type=F32)
            dv = dv + lax.dot_general(dh1b, jnp.where(msk, pr["kh"], 0.0).astype(BF16), tn, preferred_element_type=F32)
            dk_int = pr["ekb"] * jnp.dot(dh1b, jnp.where(msk, v, 0.0).astype(BF16), preferred_element_type=F32)
            dk = dk + dk_int
            ebend = jnp.exp(bend)
            c = (jnp.sum(kk * dk_int, axis=1, keepdims=True)
                 + ebend * jnp.sum(h0 * dh1, axis=1, keepdims=True))
            const = const + jnp.where(msk, c, 0.0)
            upd = lax.dot_general(jnp.where(msk, pr["qh"], 0.0).astype(BF16), dob, nt, preferred_element_type=F32)
            dh_ref[...] = ebend * dh1 + upd * bd_mask

        dg = _chunk_sums(db_in + q * dq - kk * dk, not rev, False)[0] + const
        dq, dk = dq + dq_in, dk + dk_in
        sig, sigm, f = pr["sig"], pr["sigm"], pr["f"]
        live = f > F_TINY
        inv_f = 1.0 / jnp.maximum(f, F_TINY)
        one_lb = 1.0 - lbv
        dz = sig * sigm * one_lb * (jnp.where(live, dg * inv_f, 0.0) - dk)
        dlb_ref[...] += jnp.sum(sigm * (jnp.where(live, dg * inv_f, 0.0) - dk), axis=1, keepdims=True)
        dqr = dq * _silu_grad(qraw)
        if has_prev:
            dqr = dqr + pq_ref[...]
            dv = dv + pv_ref[...]
        dq_ref[...] = dqr
        dz_ref[...] = dz
        dv_ref[...] = dv

    row_blk = lambda r: pl.BlockSpec((REC_WIDTH, TB), lambda i: (r, tmap(i)))
    blk = pl.BlockSpec((REC_WIDTH, TB), lambda i: (0, tmap(i)))
    col = pl.BlockSpec((REC_WIDTH, 1), lambda i: (0, 0))
    in_specs = [row_blk(0), row_blk(zrow), row_blk(3), col, blk,
                pl.BlockSpec((2, REC_WIDTH, REC_WIDTH), lambda i: (tmap(i), 0, 0)),
                pl.BlockSpec((None, N_HEADS, TB, TB), lambda i: (tmap(i), 0, 0, 0))]
    ins = [projt, projt, projt, lb, dot, hs, at]
    if has_prev:
        in_specs += [blk, blk]
        ins += list(prev)
    t_shape = jax.ShapeDtypeStruct((REC_WIDTH, s), F32)
    return pl.pallas_call(
        body, name=name, grid=(nblk,), in_specs=in_specs, out_specs=(blk, blk, blk, col),
        out_shape=(t_shape, t_shape, t_shape, jax.ShapeDtypeStruct((REC_WIDTH, 1), F32)),
        scratch_shapes=[pltpu.VMEM((REC_WIDTH, REC_WIDTH), F32)],
        compiler_params=_params(1))(*ins)


REC_OUT_COLS = 512


def _head_rms(o):
    o3 = o.reshape(N_HEADS, HEAD_DIM, o.shape[1])
    rstd = lax.rsqrt(jnp.mean(o3 * o3, axis=1, keepdims=True) + EPS)
    return o3 * rstd, rstd


def _hgrn_out(of, ob, projt, wn, name):
    s = of.shape[1]

    def body(of_ref, ob_ref, g_ref, wn_ref, o_ref):
        on, _ = _head_rms(of_ref[...] + ob_ref[...])
        g = g_ref[...]
        y = on.reshape(REC_WIDTH, REC_OUT_COLS) * wn_ref[...] * (g * _sigmoid(g))
        o_ref[...] = y.T.astype(BF16)

    blk = pl.BlockSpec((REC_WIDTH, REC_OUT_COLS), lambda i: (0, i))
    return pl.pallas_call(
        body, name=name, grid=(s // REC_OUT_COLS,),
        in_specs=[blk, blk, pl.BlockSpec((REC_WIDTH, REC_OUT_COLS), lambda i: (4, i)),
                  pl.BlockSpec((REC_WIDTH, 1), lambda i: (0, 0))],
        out_specs=pl.BlockSpec((REC_OUT_COLS, REC_WIDTH), lambda i: (i, 0)),
        out_shape=jax.ShapeDtypeStruct((s, REC_WIDTH), BF16), compiler_params=_params(1))(of, ob, projt, wn)


def _hgrn_out_bwd(drec, of, ob, projt, wn, name):
    s = of.shape[1]

    def body(d_ref, of_ref, ob_ref, g_ref, wn_ref, do_ref, dg_ref, dwn_ref):
        @pl.when(pl.program_id(0) == 0)
        def _():
            dwn_ref[...] = jnp.zeros_like(dwn_ref)

        dy = d_ref[...].T
        on3, rstd = _head_rms(of_ref[...] + ob_ref[...])
        on = on3.reshape(REC_WIDTH, REC_OUT_COLS)
        g, wnv = g_ref[...], wn_ref[...]
        dg_ref[...] = dy * on * wnv * _silu_grad(g)
        d_onw = dy * (g * _sigmoid(g))
        dwn_ref[...] += jnp.sum(d_onw * on, axis=1, keepdims=True)
        d_on3 = (d_onw * wnv).reshape(N_HEADS, HEAD_DIM, REC_OUT_COLS)
        do3 = rstd * (d_on3 - on3 * jnp.mean(d_on3 * on3, axis=1, keepdims=True))
        do_ref[...] = do3.reshape(REC_WIDTH, REC_OUT_COLS)

    blk = pl.BlockSpec((REC_WIDTH, REC_OUT_COLS), lambda i: (0, i))
    col = pl.BlockSpec((REC_WIDTH, 1), lambda i: (0, 0))
    t_shape = jax.ShapeDtypeStruct((REC_WIDTH, s), F32)
    return pl.pallas_call(
        body, name=name, grid=(s // REC_OUT_COLS,),
        in_specs=[pl.BlockSpec((REC_OUT_COLS, REC_WIDTH), lambda i: (i, 0)), blk, blk,
                  pl.BlockSpec((REC_WIDTH, REC_OUT_COLS), lambda i: (4, i)), col],
        out_specs=(blk, blk, col),
        out_shape=(t_shape, t_shape, jax.ShapeDtypeStruct((REC_WIDTH, 1), F32)),
        compiler_params=_params(1))(drec, of, ob, projt, wn)


def _lower_bounds(gamma, name):
    def body(g_ref, lb_ref, p_ref):
        g0, g1 = g_ref[0:1, :], g_ref[1:2, :]
        m = jnp.maximum(g0, g1)
        e0, e1 = jnp.exp(g0 - m), jnp.exp(g1 - m)
        p0, p1 = e0 / (e0 + e1), e1 / (e0 + e1)
        lb_ref[...] = (p0 + p1) - p0
        p_ref[0:1, :] = p0
        p_ref[1:2, :] = p1

    n = gamma.shape[1]
    return pl.pallas_call(body, name=name,
                          out_shape=(jax.ShapeDtypeStruct((1, n), F32), jax.ShapeDtypeStruct((2, n), F32)))(gamma)


def _lower_bounds_bwd(dlb1, p, name):
    def body(d_ref, p_ref, o_ref):
        p0, p1, d = p_ref[0:1, :], p_ref[1:2, :], d_ref[...]
        inner = p1 * d
        o_ref[0:1, :] = p0 * (0.0 - inner)
        o_ref[1:2, :] = p1 * (d - inner)

    return pl.pallas_call(body, name=name, out_shape=jax.ShapeDtypeStruct(p.shape, F32))(dlb1, p)


def _split_w_in(w_in):
    return dict(conv=w_in[:, G_CONV[0]:G_CONV[1]], qkv=w_in[:, G_QKV[0]:G_QKV[1]],
                rec_t=w_in[:, G_REC[0]:].T, nat=w_in[:, :G_REC[0]])


def _split_w_rest(w_out, w_up, w_down):
    return dict(out=w_out, out_a=w_out[:CONV_CH], out_b=w_out[CONV_CH:CONV_CH + ATT_WIDTH],
                out_c=w_out[CONV_CH + ATT_WIDTH:], up=w_up, down=w_down)


def _col(v):
    return v.reshape(-1, 1)


def _sequence_step(x, tgt, mods, lbs, small, w_in0, later_weights, final_w, on_layer_grads):
    saved = []
    xin = x
    big = [_split_w_in(w_in0), None]
    h1 = _resid_norm_mod(x, None, None, small[0]["norm1_w"], mods[0][1:2], mods[0][0:1], "norm1_first")
    for l in range(DEPTH):
        sm, w, md = small[l], big[l], mods[l]
        pa = _matmul(h1, w["conv"], "nn", F32, f"proj_conv")
        qkv = _matmul(h1, w["qkv"], "nn", F32, f"proj_qkv")
        projt = _matmul(w["rec_t"], h1, "nt", F32, f"proj_rec")
        a_out, cv = _conv_mixer(pa, sm["conv_a_w"], sm["conv_a_b"], sm["ln_a_w"], sm["ln_a_b"], f"conv_mixer")
        outs, lses = zip(*[_attn_branch(qkv, d, f"attn_d{d}") for d in DILATIONS])
        att, att32, lse = _attn_combine(outs, lses, f"attn_combine")
        lb_f, lb_b = _col(lbs[l][0]), _col(lbs[l][1])
        of, hsf, atf = _hgrn_scan(projt, lb_f, False, "hgrn_fwd")
        ob, hsb, atb = _hgrn_scan(projt, lb_b, True, "hgrn_rev")
        wn = _col(sm["rec_norm_w"])
        rec = _hgrn_out(of, ob, projt, wn, f"hgrn_out")
        mixed = jnp.concatenate([a_out, att, rec], axis=1)
        if l == 0:
            w_in1, w_out_all, w_up_all, w_down_all = later_weights(rec)
            big[0].update(_split_w_rest(w_out_all[0], w_up_all[0], w_down_all[0]))
            big[1] = dict(_split_w_in(w_in1), **_split_w_rest(w_out_all[1], w_up_all[1], w_down_all[1]))
        r1 = _matmul(mixed, w["out"], "nn", BF16, "out_proj")
        xmid, h2 = _resid_norm_mod(xin, r1, md[2:3], sm["norm2_w"], md[4:5], md[3:4], f"norm2")
        u = _matmul(h2, w["up"], "nn", BF16, f"ffn_up")
        act, conv_g, conv_v = _ffn_act(u, sm["conv_f_w"], "ffn_act")
        r2 = _matmul(act, w["down"], "nn", BF16, "ffn_down")
        saved.append(dict(xin=xin, h1=h1, pa=pa, qkv=qkv, projt=projt, cv=cv, att32=att32, lse=lse, of=of, ob=ob,
                          hsf=hsf, hsb=hsb, atf=atf, atb=atb, lb_f=lb_f, lb_b=lb_b, wn=wn, mixed=mixed, r1=r1, xmid=xmid, h2=h2,
                          u=u, conv_g=conv_g, conv_v=conv_v, act=act, r2=r2))
        if l + 1 < DEPTH:
            nxt = small[l + 1]
            xin, h1 = _resid_norm_mod(xmid, r2, md[5:6], nxt["norm1_w"], mods[l + 1][1:2], mods[l + 1][0:1],
                                      "norm1")
    top = saved[-1]
    loss, dx, dr2, dg2, dfw = _final_loss(top["xmid"], top["r2"], mods[-1][5:6], final_w, tgt, "final_loss")

    grads = [None] * DEPTH
    order_after = None
    for l in reversed(range(DEPTH)):
        sm, w, md, sv = small[l], big[l], mods[l], saved[l]
        dact = _matmul(dr2, w["down"], "nt", BF16, f"d_act")
        g_down = _matmul(dr2, sv["act"], "tn", F32, "dw_down").T
        conv_f_w = sm["conv_f_w"] if order_after is None else sm["conv_f_w"] + order_after
        dug, duv, dwg, dwv = _ffn_act_bwd(sv["u"], sv["conv_g"], sv["conv_v"], dact, conv_f_w, "ffn_act_bwd")
        dh2 = _matmul_two_lhs(dug, duv, w["up"], BF16, "d_h2")
        g_up = _matmul_two_rhs(sv["h2"], dug, duv, F32, "dw_up")
        after_ffn = on_layer_grads(l, {2: g_up, 3: g_down}, False)
        norm2_w = sm["norm2_w"] if after_ffn is None else sm["norm2_w"] + after_ffn
        dxmid, dr1, dsh2, dsc2, dnw2, dg1 = _norm_bwd(sv["xmid"], [dh2], dx, norm2_w, md[4:5], md[2:3], sv["r1"],
                                                     f"norm2_bwd")
        dmix_a = _matmul(dr1, w["out_a"], "nt", F32, f"d_mix_a")
        dmix_b = _matmul(dr1, w["out_b"], "nt", F32, f"d_mix_b")
        dmix_c = _matmul(dr1, w["out_c"], "nt", F32, f"d_mix_c")
        g_out = _matmul(sv["mixed"], dr1, "tn", F32, f"dw_out")
        dc, dlnw, dlnb, dcb = _conv_mixer_bwd_ln(sv["cv"], dmix_a, sm["ln_a_w"], sm["ln_a_b"], f"conv_mixer_bwd_ln")
        dpa, dcw = _conv_mixer_bwd_conv(sv["pa"], dc, sm["conv_a_w"], f"conv_mixer_bwd_conv")
        delta = _attn_delta(dmix_b, sv["att32"], "attn_delta")
        dqkv = None
        for d in reversed(DILATIONS):
            dqkv = _attn_branch_bwd(sv["qkv"], dmix_b, sv["lse"], delta, dqkv, d, BF16 if d == 1 else F32,
                                    f"attn_bwd_d{d}")
        dot, dgt, dwn = _hgrn_out_bwd(dmix_c, sv["of"], sv["ob"], sv["projt"], sv["wn"], f"hgrn_out_bwd")
        dqf, dzf, dvf, dlbf = _hgrn_scan_bwd(sv["projt"], sv["lb_f"], dot, sv["hsf"], sv["atf"], None, False,
                                             "hgrn_fwd_bwd")
        dqt, dzb, dvt, dlbb = _hgrn_scan_bwd(sv["projt"], sv["lb_b"], dot, sv["hsb"], sv["atb"], (dqf, dvf), True,
                                             "hgrn_rev_bwd")
        dprojt = jnp.concatenate([dqt, dzf, dzb, dvt, dgt], axis=0).astype(BF16)
        dnat = jnp.concatenate([dpa, *dqkv], axis=1)
        dh1_a = _matmul(dnat, w["nat"], "nt", BF16, "d_h1_nat")
        dh1_b = _matmul(dprojt, w["rec_t"], "tn", BF16, "d_h1_rec")
        g_in_nat = _matmul(sv["h1"], dnat, "tn", F32, f"dw_in_nat")
        g_in_rec_t = _matmul(dprojt, sv["h1"], "nn", F32, f"dw_in_rec")
        g_in = jnp.concatenate([g_in_nat, g_in_rec_t.T], axis=1)
        if l > 0:
            below = saved[l - 1]
            dx, dr2, dsh1, dsc1, dnw1, dg2_below = _norm_bwd(sv["xin"], [dh1_a, dh1_b], dxmid, sm["norm1_w"], md[1:2],
                                                            mods[l - 1][5:6], below["r2"], f"norm1_bwd")
        else:
            dx, dsh1, dsc1, dnw1 = _norm_bwd(sv["xin"], [dh1_a, dh1_b], dxmid, sm["norm1_w"], md[1:2], None, None,
                                             f"norm1_bwd")
        grads[l] = dict(w_in=g_in, w_out=g_out, w_up=g_up, w_down=g_down,
                        mod=[dsh1, dsc1, dg1, dsh2, dsc2, dg2], norm1_w=dnw1, conv_a_w=dcw[:CONV_WIDTH], conv_a_b=dcb,
                        ln_a_w=dlnw, ln_a_b=dlnb, lb=jnp.concatenate([dlbf.reshape(1, -1), dlbb.reshape(1, -1)], axis=0),
                        rec_norm_w=dwn.reshape(1, -1), norm2_w=dnw2,
                        conv_f_w=jnp.concatenate([dwg[:3], dwv[:3]], axis=1))
        order_after = on_layer_grads(l, {0: g_in, 1: g_out}, True)
        if l > 0:
            dg2 = dg2_below
    return loss[0, 0], dx, grads, dfw


def _adamw_math(w, g, m, v):
    m = ADAM_B1 * m + (1.0 - ADAM_B1) * g
    v = ADAM_B2 * v + (1.0 - ADAM_B2) * (g * g)
    m_hat = m / (1.0 - ADAM_B1 ** ADAM_STEP)
    v_hat = v / (1.0 - ADAM_B2 ** ADAM_STEP)
    delta = -ADAM_LR * (m_hat / (jnp.sqrt(v_hat) + ADAM_EPS) + ADAM_WD * w)
    return delta, m, v


def _row_tile(rows, cols, max_elems=384 * 1024):
    best = None
    for t in range(8, rows + 1, 8):
        if rows % t == 0 and t * cols <= max_elems:
            best = t
    return best or rows


def _adamw(w, g, m, v, name):
    nl, r, c = w.shape
    tr = _row_tile(r, c)

    def body(w_ref, g_ref, m_ref, v_ref, d_ref, m2_ref, v2_ref):
        d_ref[...], m2_ref[...], v2_ref[...] = _adamw_math(w_ref[...], g_ref[...], m_ref[...], v_ref[...])

    blk = pl.BlockSpec((None, tr, c), lambda l, i: (l, i, 0))
    shape = jax.ShapeDtypeStruct((nl, r, c), F32)
    return pl.pallas_call(body, name=name, grid=(nl, r // tr), in_specs=[blk] * 4, out_specs=(blk, blk, blk),
                          out_shape=(shape, shape, shape), compiler_params=_params(2))(w, g, m, v)


ADA_SHARD = N_MOD * D_MODEL // 4
ADA_COLS = 512
ADA_ROWS = 256
HIGHEST = lax.Precision.HIGHEST


def _ada_mod(c_all, w_ada, b_sh, name):
    def body(c_ref, w_ref, b_ref, o_ref):
        cv = c_ref[...]
        o_ref[...] = jnp.dot(cv * _sigmoid(cv), w_ref[...], precision=HIGHEST, preferred_element_type=F32) + b_ref[...]

    return pl.pallas_call(
        body, name=name, grid=(DEPTH, ADA_SHARD // ADA_COLS),
        in_specs=[pl.BlockSpec((8, D_MODEL), lambda l, j: (0, 0)),
                  pl.BlockSpec((None, D_MODEL, ADA_COLS), lambda l, j: (l, 0, j)),
                  pl.BlockSpec((None, 1, ADA_COLS), lambda l, j: (l, 0, j))],
        out_specs=pl.BlockSpec((None, 8, ADA_COLS), lambda l, j: (l, 0, j)),
        out_shape=jax.ShapeDtypeStruct((DEPTH, 8, ADA_SHARD), F32), compiler_params=_params(2))(c_all, w_ada, b_sh)


def _ada_update(c_all, dmod_sh, w, m, v, name):
    def body(c_ref, d_ref, w_ref, m_ref, v_ref, g_ref, dl_ref, m2_ref, v2_ref):
        cv = c_ref[...]
        g = lax.dot_general(cv * _sigmoid(cv), d_ref[...], (((0,), (0,)), ((), ())), precision=HIGHEST,
                            preferred_element_type=F32)
        g_ref[...] = g
        dl_ref[...], m2_ref[...], v2_ref[...] = _adamw_math(w_ref[...], g, m_ref[...], v_ref[...])

    blk = pl.BlockSpec((None, ADA_ROWS, ADA_SHARD), lambda l, i: (l, i, 0))
    shape = jax.ShapeDtypeStruct((DEPTH, D_MODEL, ADA_SHARD), F32)
    return pl.pallas_call(
        body, name=name, grid=(DEPTH, D_MODEL // ADA_ROWS),
        in_specs=[pl.BlockSpec((8, ADA_ROWS), lambda l, i: (0, i)),
                  pl.BlockSpec((None, 8, ADA_SHARD), lambda l, i: (l, 0, 0)), blk, blk, blk],
        out_specs=(blk,) * 4, out_shape=(shape,) * 4, compiler_params=_params(2))(c_all, dmod_sh, w, m, v)


def _sum_devices(packs, name):
    def body(p_ref, o_ref):
        acc = p_ref[0]
        for dev in range(1, 8):
            acc = acc + p_ref[dev]
        o_ref[...] = acc

    return pl.pallas_call(body, name=name, out_shape=jax.ShapeDtypeStruct(packs.shape[1:], F32))(packs)


def _mesh_pos():
    return lax.axis_index("x"), lax.axis_index("y"), lax.axis_index("c")


def _flip(v, bit):
    return 1 - v if bit else v


def _allgather_devices(x, name):
    m_per, n = x.shape

    def body(x_ref, out_ref, send_sems, recv_sems, local_sem):
        ix, iy, ic = _mesh_pos()
        me, sibling = (ix, iy, ic), (ix, iy, 1 - ic)
        chips = [(1 - ix, iy), (ix, 1 - iy), (1 - ix, 1 - iy)]

        def rows(px, py, pc):
            return out_ref.at[pl.ds((4 * px + 2 * py + pc) * m_per, m_per), :]

        def copy(k, block, to, src=None):
            return pltpu.make_async_remote_copy(
                src_ref=rows(*block) if src is None else src, dst_ref=rows(*block),
                send_sem=send_sems.at[k], recv_sem=recv_sems.at[k], device_id=to, device_id_type=MESH)

        mine = pltpu.make_async_copy(x_ref, rows(*me), local_sem)
        mine.start()
        first = [copy(0, me, sibling, src=x_ref)]
        first += [copy(1 + j, me, (*chip, ic), src=x_ref) for j, chip in enumerate(chips)]
        for cp in first:
            cp.start()
        passed = [copy(4 + j, (*chip, ic), sibling) for j, chip in enumerate(chips)]
        for j, chip in enumerate(chips):
            copy(1 + j, (*chip, ic), me).wait_recv()
            passed[j].start()
        copy(0, sibling, me).wait_recv()
        for j, chip in enumerate(chips):
            copy(4 + j, (*chip, 1 - ic), me).wait_recv()
        for cp in first + passed:
            cp.wait_send()
        mine.wait()

    return pl.pallas_call(
        body, name=name, out_shape=jax.ShapeDtypeStruct((8 * m_per, n), x.dtype),
        in_specs=[pl.BlockSpec(memory_space=pltpu.VMEM)], out_specs=pl.BlockSpec(memory_space=pltpu.VMEM),
        scratch_shapes=[pltpu.SemaphoreType.DMA((7,)), pltpu.SemaphoreType.DMA((7,)), pltpu.SemaphoreType.DMA],
    )(x)


def _gather_chips(shards, name):
    n = len(shards)

    def body(*refs):
        ins, outs = refs[:n], refs[n:2 * n]
        send_sems, recv_sems, local_sems = refs[2 * n:]
        ix, iy, ic = _mesh_pos()
        me = 2 * ix + iy
        local = [pltpu.make_async_copy(ins[a], outs[a].at[me], local_sems.at[a]) for a in range(n)]
        for cp in local:
            cp.start()
        remote = []
        for a in range(n):
            for k in (1, 2, 3):
                px, py = _flip(ix, k & 2), _flip(iy, k & 1)
                sems = dict(send_sem=send_sems.at[3 * a + k - 1], recv_sem=recv_sems.at[3 * a + k - 1],
                            device_id=(px, py, ic), device_id_type=MESH)
                out_cp = pltpu.make_async_remote_copy(src_ref=ins[a], dst_ref=outs[a].at[me], **sems)
                in_cp = pltpu.make_async_remote_copy(src_ref=ins[a], dst_ref=outs[a].at[2 * px + py], **sems)
                out_cp.start()
                remote.append((out_cp, in_cp))
        for out_cp, in_cp in remote:
            out_cp.wait_send()
            in_cp.wait_recv()
        for cp in local:
            cp.wait()

    return pl.pallas_call(
        body, name=name, in_specs=[ANY] * n, out_specs=tuple([ANY] * n),
        out_shape=tuple(jax.ShapeDtypeStruct((4,) + t.shape, t.dtype) for t in shards),
        scratch_shapes=[pltpu.SemaphoreType.DMA((3 * n,)), pltpu.SemaphoreType.DMA((3 * n,)),
                        pltpu.SemaphoreType.DMA((n,))],
    )(*shards)


HBM = pl.BlockSpec(memory_space=pltpu.HBM)
SEM = pl.BlockSpec(memory_space=pltpu.SEMAPHORE)
DATAFLOW = pltpu.SideEffectType.DATAFLOW_SIDE_EFFECTING


def _peer_chip(ix, iy, k):
    return _flip(ix, k & 2), _flip(iy, k & 1)


def _gather_chips_start(shards, name):
    n = len(shards)

    def body(*refs):
        src, land = refs[:n], refs[n:2 * n]
        send_sems, recv_sems = refs[2 * n], refs[2 * n + 1]
        token = refs[-1]
        ix, iy, ic = _mesh_pos()
        me = 2 * ix + iy
        for a in range(n):
            for k in (1, 2, 3):
                px, py = _peer_chip(ix, iy, k)
                pltpu.make_async_remote_copy(
                    src_ref=src[a], dst_ref=land[a].at[me], send_sem=send_sems.at[3 * a + k - 1],
                    recv_sem=recv_sems.at[3 * a + k - 1], device_id=(px, py, ic), device_id_type=MESH).start()
        token[...] = jnp.zeros_like(token)

    hbm = lambda shape, dtype: pltpu.HBM(shape, dtype)
    operands = ([pltpu.with_memory_space_constraint(t, pltpu.HBM) for t in shards]
                + [pltpu.with_memory_space_constraint(lax.empty((4,) + t.shape, t.dtype), pltpu.HBM) for t in shards])
    return pl.pallas_call(
        body, name=name,
        out_shape=(pltpu.SemaphoreType.DMA((3 * n,)), pltpu.SemaphoreType.DMA((3 * n,)),
                   *[hbm(t.shape, t.dtype) for t in shards], *[hbm((4,) + t.shape, t.dtype) for t in shards],
                   jax.ShapeDtypeStruct((8, LANES), F32)),
        in_specs=(HBM,) * (2 * n),
        out_specs=(SEM, SEM) + (HBM,) * (2 * n) + (pl.BlockSpec(memory_space=pltpu.VMEM),),
        input_output_aliases={a: 2 + a for a in range(2 * n)},
        compiler_params=pltpu.CompilerParams(has_side_effects=DATAFLOW),
    )(*operands)


def _gather_chips_wait(started, after, name):
    send_sems, recv_sems = started[0], started[1]
    thru = started[2:-1]
    n = len(thru) // 2

    def body(*refs):
        src, land = refs[:n], refs[n:2 * n]
        send_sems, recv_sems = refs[2 * n], refs[2 * n + 1]
        ix, iy, ic = _mesh_pos()
        for a in range(n):
            for k in (1, 2, 3):
                px, py = _peer_chip(ix, iy, k)
                cp = pltpu.make_async_remote_copy(
                    src_ref=src[a], dst_ref=land[a].at[2 * px + py], send_sem=send_sems.at[3 * a + k - 1],
                    recv_sem=recv_sems.at[3 * a + k - 1], device_id=(px, py, ic), device_id_type=MESH)
                cp.wait_send()
                cp.wait_recv()

    outs = pl.pallas_call(
        body, name=name,
        out_shape=tuple(pltpu.HBM(t.shape, t.dtype) for t in thru),
        in_specs=(HBM,) * (2 * n) + (SEM, SEM, ANY), out_specs=(HBM,) * (2 * n),
        input_output_aliases={a: a for a in range(2 * n)},
        compiler_params=pltpu.CompilerParams(has_side_effects=DATAFLOW),
    )(*thru, send_sems, recv_sems, after)
    return outs[:n], outs[n:]


BIG_KINDS = (("w_in", "col", D_MODEL, IN_COLS), ("w_out", "row", D_MODEL, D_MODEL),
             ("w_up", "col", D_MODEL, 2 * D_FF), ("w_down", "row", D_FF, D_MODEL))


def _piece_shape(how, r, c):
    return (r // 2, c // 4) if how == "col" else (r // 8, c)


def _aligned(start, multiple):
    return start if isinstance(start, int) else pl.multiple_of(start, multiple)


def _piece(ref, how, r, c, chip, half):
    if how == "col":
        return ref.at[pl.ds(_aligned(half * (r // 2), 8), r // 2), pl.ds(_aligned(chip * (c // 4), LANES), c // 4)]
    n = r // 4
    return ref.at[pl.ds(_aligned(chip * n + half * (n // 2), 8), n // 2), :]


def _rs_pair_exchange(grads, kinds, name):
    nk = len(kinds)
    specs = [BIG_KINDS[ki] for ki in kinds]
    nl = len(grads[0])
    flat = [grads[ki][l] for ki in range(nk) for l in range(nl)]
    per = nl * 4

    def body(*refs):
        g, land = refs[:nk * nl], refs[nk * nl:nk * nl + nk]
        send_sems, recv_sems = refs[nk * nl + nk:]
        ix, iy, ic = _mesh_pos()
        sibling = (ix, iy, 1 - ic)
        copies = []
        for ki, (_, how, r, c) in enumerate(specs):
            for l in range(nl):
                for j in range(4):
                    sem = ki * per + l * 4 + j
                    rem = pltpu.make_async_remote_copy(
                        src_ref=_piece(g[ki * nl + l], how, r, c, j, 1 - ic), dst_ref=land[ki].at[l, j],
                        send_sem=send_sems.at[sem], recv_sem=recv_sems.at[sem], device_id=sibling, device_id_type=MESH)
                    rem.start()
                    copies.append(rem)
        for rem in copies:
            rem.wait_send()
            rem.wait_recv()

    shapes = [jax.ShapeDtypeStruct((nl, 4) + _piece_shape(how, r, c), F32) for _, how, r, c in specs]
    return pl.pallas_call(
        body, name=name, in_specs=[ANY] * len(flat), out_specs=tuple([ANY] * nk), out_shape=tuple(shapes),
        scratch_shapes=[pltpu.SemaphoreType.DMA((nk * per,))] * 2,
    )(*flat)


def _pair_sum(g, theirs, layer, how, core, name):
    r, c = g.shape
    pr, pc = _piece_shape(how, r, c)
    if how == "col":
        mine_spec = pl.BlockSpec((pr, pc), lambda j, core_ref: (core_ref[0], j))
    else:
        mine_spec = pl.BlockSpec((pr, pc), lambda j, core_ref: (2 * j + core_ref[0], 0))

    def body(core_ref, g_ref, t_ref, o_ref, ob_ref):
        total = g_ref[...] + t_ref[...]
        o_ref[...] = total
        ob_ref[...] = total.astype(BF16)

    out_blk = pl.BlockSpec((None, pr, pc), lambda j, core_ref: (j, 0, 0))
    return pl.pallas_call(
        body, name=name,
        grid_spec=pltpu.PrefetchScalarGridSpec(
            num_scalar_prefetch=1, grid=(4,),
            in_specs=[mine_spec, pl.BlockSpec((None, None, pr, pc), lambda j, core_ref: (layer, j, 0, 0))],
            out_specs=(out_blk, out_blk)),
        out_shape=(jax.ShapeDtypeStruct((4, pr, pc), F32), jax.ShapeDtypeStruct((4, pr, pc), BF16)),
        compiler_params=_params(1))(core, g, theirs)


def _rs_chip_exchange(pair_sums, name):
    nk = len(pair_sums)
    nl = len(pair_sums[0])
    flat = [pair_sums[ki][l] for ki in range(nk) for l in range(nl)]

    def body(*refs):
        src, dst = refs[:nk * nl], refs[nk * nl:nk * nl + nk]
        send_sems, recv_sems = refs[nk * nl + nk:]
        ix, iy, ic = _mesh_pos()
        copies = []
        for ki in range(nk):
            for l in range(nl):
                for k in (1, 2, 3):
                    px, py = _peer_chip(ix, iy, k)
                    sem = (ki * nl + l) * 3 + k - 1
                    rem = pltpu.make_async_remote_copy(
                        src_ref=src[ki * nl + l].at[2 * px + py], dst_ref=dst[ki].at[l, k - 1],
                        send_sem=send_sems.at[sem], recv_sem=recv_sems.at[sem], device_id=(px, py, ic), device_id_type=MESH)
                    rem.start()
                    copies.append(rem)
        for rem in copies:
            rem.wait_send()
            rem.wait_recv()

    return pl.pallas_call(
        body, name=name, in_specs=[ANY] * len(flat), out_specs=tuple([ANY] * nk),
        out_shape=tuple(jax.ShapeDtypeStruct((nl, 3) + pair_sums[ki][0].shape[1:], pair_sums[ki][0].dtype)
                        for ki in range(nk)),
        scratch_shapes=[pltpu.SemaphoreType.DMA((nk * nl * 3,))] * 2,
    )(*flat)


def _rs_chip_start(pieces, name):
    n = len(pieces)

    def body(*refs):
        src, land = refs[:n], refs[n:2 * n]
        send_sems, recv_sems = refs[2 * n], refs[2 * n + 1]
        token = refs[-1]
        ix, iy, ic = _mesh_pos()
        for a in range(n):
            for k in (1, 2, 3):
                px, py = _peer_chip(ix, iy, k)
                pltpu.make_async_remote_copy(
                    src_ref=src[a].at[2 * px + py], dst_ref=land[a].at[k - 1], send_sem=send_sems.at[3 * a + k - 1],
                    recv_sem=recv_sems.at[3 * a + k - 1], device_id=(px, py, ic), device_id_type=MESH).start()
        token[...] = jnp.zeros_like(token)

    land_shape = lambda t: (3,) + t.shape[1:]
    operands = ([pltpu.with_memory_space_constraint(t, pltpu.HBM) for t in pieces]
                + [pltpu.with_memory_space_constraint(lax.empty(land_shape(t), t.dtype), pltpu.HBM) for t in pieces])
    return pl.pallas_call(
        body, name=name,
        out_shape=(pltpu.SemaphoreType.DMA((3 * n,)), pltpu.SemaphoreType.DMA((3 * n,)),
                   *[pltpu.HBM(t.shape, t.dtype) for t in pieces], *[pltpu.HBM(land_shape(t), t.dtype) for t in pieces],
                   jax.ShapeDtypeStruct((8, LANES), F32)),
        in_specs=(HBM,) * (2 * n),
        out_specs=(SEM, SEM) + (HBM,) * (2 * n) + (pl.BlockSpec(memory_space=pltpu.VMEM),),
        input_output_aliases={a: 2 + a for a in range(2 * n)},
        compiler_params=pltpu.CompilerParams(has_side_effects=DATAFLOW),
    )(*operands)


def _rs_chip_wait(started, after, name):
    send_sems, recv_sems = started[0], started[1]
    thru = started[2:-1]
    n = len(thru) // 2

    def body(*refs):
        src, land = refs[:n], refs[n:2 * n]
        send_sems, recv_sems = refs[2 * n], refs[2 * n + 1]
        ix, iy, ic = _mesh_pos()
        for a in range(n):
            for k in (1, 2, 3):
                px, py = _peer_chip(ix, iy, k)
                cp = pltpu.make_async_remote_copy(
                    src_ref=src[a].at[2 * px + py], dst_ref=land[a].at[k - 1], send_sem=send_sems.at[3 * a + k - 1],
                    recv_sem=recv_sems.at[3 * a + k - 1], device_id=(px, py, ic), device_id_type=MESH)
                cp.wait_send()
                cp.wait_recv()

    outs = pl.pallas_call(
        body, name=name,
        out_shape=tuple(pltpu.HBM(t.shape, t.dtype) for t in thru),
        in_specs=(HBM,) * (2 * n) + (SEM, SEM, ANY), out_specs=(HBM,) * (2 * n),
        input_output_aliases={a: a for a in range(2 * n)},
        compiler_params=pltpu.CompilerParams(has_side_effects=DATAFLOW),
    )(*thru, send_sems, recv_sems, after)
    return outs[n:]


def _chip_sum(own, others, chip, name):
    _, pr, pc = own.shape

    def body(chip_ref, own_ref, s1, s2, s3, o_ref):
        o_ref[...] = ((own_ref[...] + s1[...].astype(F32)) + s2[...].astype(F32)) + s3[...].astype(F32)

    slot = lambda k: pl.BlockSpec((None, pr, pc), lambda i, chip_ref: (k, 0, 0))
    return pl.pallas_call(
        body, name=name,
        grid_spec=pltpu.PrefetchScalarGridSpec(
            num_scalar_prefetch=1, grid=(1,),
            in_specs=[pl.BlockSpec((None, pr, pc), lambda i, chip_ref: (chip_ref[0], 0, 0)), slot(0), slot(1), slot(2)],
            out_specs=pl.BlockSpec((pr, pc), lambda i, chip_ref: (0, 0))),
        out_shape=jax.ShapeDtypeStruct((pr, pc), F32), compiler_params=_params(1))(chip, own, others, others, others)


def _rs_pair_share(halves, name):
    nk = len(halves)
    flat = [halves[ki][l] for ki in range(nk) for l in range(DEPTH)]

    def body(*refs):
        src, dst = refs[:nk * DEPTH], refs[nk * DEPTH:nk * DEPTH + nk]
        send_sems, recv_sems = refs[nk * DEPTH + nk:]
        ix, iy, ic = _mesh_pos()
        copies = []
        for ki in range(nk):
            for l in range(DEPTH):
                sem = ki * DEPTH + l
                rem = pltpu.make_async_remote_copy(
                    src_ref=src[sem], dst_ref=dst[ki].at[l], send_sem=send_sems.at[sem], recv_sem=recv_sems.at[sem],
                    device_id=(ix, iy, 1 - ic), device_id_type=MESH)
                rem.start()
                copies.append(rem)
        for rem in copies:
            rem.wait_send()
            rem.wait_recv()

    return pl.pallas_call(
        body, name=name, in_specs=[ANY] * len(flat), out_specs=tuple([ANY] * nk),
        out_shape=tuple(jax.ShapeDtypeStruct((DEPTH,) + halves[ki][0].shape, F32) for ki in range(nk)),
        scratch_shapes=[pltpu.SemaphoreType.DMA((nk * DEPTH,))] * 2,
    )(*flat)


def _adamw_halves(w, mine, theirs, m, v, core, name):
    nl, pr, pc = theirs.shape
    shape = w.shape
    view = lambda t: t.reshape(nl, 2, pr, pc)
    tr = _row_tile(pr, pc, 256 * 1024)

    def body(core_ref, w_ref, a0_ref, a1_ref, t_ref, m_ref, v_ref, g_ref, d_ref, m2_ref, v2_ref):
        own = jnp.where(pl.program_id(0) == 0, a0_ref[...], a1_ref[...])
        g = jnp.where(pl.program_id(1) == core_ref[0], own, t_ref[...])
        g_ref[...] = g
        d_ref[...], m2_ref[...], v2_ref[...] = _adamw_math(w_ref[...], g, m_ref[...], v_ref[...])

    blk = pl.BlockSpec((None, None, tr, pc), lambda l, h, i, core_ref: (l, h, i, 0))
    own_blk = pl.BlockSpec((tr, pc), lambda l, h, i, core_ref: (i, 0))
    out = jax.ShapeDtypeStruct((nl, 2, pr, pc), F32)
    outs = pl.pallas_call(
        body, name=name,
        grid_spec=pltpu.PrefetchScalarGridSpec(
            num_scalar_prefetch=1, grid=(nl, 2, pr // tr),
            in_specs=[blk, own_blk, own_blk, pl.BlockSpec((None, tr, pc), lambda l, h, i, core_ref: (l, i, 0)), blk, blk],
            out_specs=(blk,) * 4),
        out_shape=(out,) * 4, compiler_params=_params(3),
    )(core, view(w), mine[0], mine[1], theirs, view(m), view(v))
    return tuple(t.reshape(shape) for t in outs)


def _rs_first_stages(layer_grads, kinds, core, tag, in_flight):
    theirs = _rs_pair_exchange([[g] for g in layer_grads], kinds, f"rs_pair_exchange_{tag}")
    sums = [_pair_sum(g, theirs[n], 0, BIG_KINDS[ki][1], core, f"rs_pair_sum_{BIG_KINDS[ki][0]}")
            for n, (ki, g) in enumerate(zip(kinds, layer_grads))]
    to_send = [both[1] for both in sums]
    if in_flight:
        return [both[0] for both in sums], _rs_chip_start(to_send, f"rs_chip_start_{tag}")
    slots = _rs_chip_exchange([[t] for t in to_send], f"rs_chip_exchange_{tag}")
    return [both[0] for both in sums], [t[0] for t in slots]


def _rs_last_stages(pair_sums, slots, chip):
    halves = [[_chip_sum(pair_sums[ki][l], slots[ki][l], chip, f"rs_chip_sum_{kind}") for l in range(DEPTH)]
              for ki, (kind, _, _, _) in enumerate(BIG_KINDS)]
    other = _rs_pair_share(halves, "rs_pair_share")
    return list(zip(halves, other))


WEIGHT_NAMES = ("w_ada", "b_ada", "norm1_w", "w_in", "conv_a_w", "conv_a_b", "ln_a_w", "ln_a_b", "lb_gamma",
                "rec_norm_w", "w_out", "norm2_w", "w_up", "conv_f_w", "w_down", "final_norm_w")
SMALL_PARAMS = (("b_ada", (DEPTH, N_MOD * D_MODEL), None), ("norm1_w", (DEPTH, D_MODEL), None),
                ("conv_a_w", (DEPTH, CONV_WIDTH, CONV_CH), 2), ("conv_a_b", (DEPTH, CONV_CH), None),
                ("ln_a_w", (DEPTH, CONV_CH), None), ("ln_a_b", (DEPTH, CONV_CH), None),
                ("lb_gamma", (DEPTH, 2, REC_WIDTH), 2), ("rec_norm_w", (DEPTH, REC_WIDTH), None),
                ("norm2_w", (DEPTH, D_MODEL), None), ("conv_f_w", (DEPTH, 3, 2 * D_FF), 2),
                ("final_norm_w", (D_MODEL,), None))


def _pack_rows(parts):
    flat = jnp.concatenate([p.reshape(-1) for p in parts])
    total = flat.shape[0]
    padded = -(-total // (8 * LANES)) * (8 * LANES)
    return jnp.pad(flat, (0, padded - total)).reshape(padded // LANES, LANES)


def _unpack(flat, shapes):
    out, off = [], 0
    for shp in shapes:
        size = int(np.prod(shp))
        out.append(flat[off:off + size].reshape(shp))
        off += size
    return out


def _unstack_chips(t, axis):
    return jnp.concatenate([t[j] for j in range(4)], axis=axis)


def kernel(x, c, w_ada, b_ada, norm1_w, w_in, conv_a_w, conv_a_b, ln_a_w, ln_a_b, lb_gamma, rec_norm_w, w_out, norm2_w, w_up, conv_f_w, w_down, final_norm_w, loss_target, m_w_ada, m_b_ada, m_norm1_w, m_w_in, m_conv_a_w, m_conv_a_b, m_ln_a_w, m_ln_a_b, m_lb_gamma, m_rec_norm_w, m_w_out, m_norm2_w, m_w_up, m_conv_f_w, m_w_down, m_final_norm_w, v_w_ada, v_b_ada, v_norm1_w, v_w_in, v_conv_a_w, v_conv_a_b, v_ln_a_w, v_ln_a_b, v_lb_gamma, v_rec_norm_w, v_w_out, v_norm2_w, v_w_up, v_conv_f_w, v_w_down, v_final_norm_w):
    params = dict(zip(WEIGHT_NAMES, (w_ada, b_ada, norm1_w, w_in, conv_a_w, conv_a_b, ln_a_w, ln_a_b, lb_gamma,
                                     rec_norm_w, w_out, norm2_w, w_up, conv_f_w, w_down, final_norm_w)))
    mom1 = dict(zip(WEIGHT_NAMES, (m_w_ada, m_b_ada, m_norm1_w, m_w_in, m_conv_a_w, m_conv_a_b, m_ln_a_w, m_ln_a_b,
                                   m_lb_gamma, m_rec_norm_w, m_w_out, m_norm2_w, m_w_up, m_conv_f_w, m_w_down,
                                   m_final_norm_w)))
    mom2 = dict(zip(WEIGHT_NAMES, (v_w_ada, v_b_ada, v_norm1_w, v_w_in, v_conv_a_w, v_conv_a_b, v_ln_a_w, v_ln_a_b,
                                   v_lb_gamma, v_rec_norm_w, v_w_out, v_norm2_w, v_w_up, v_conv_f_w, v_w_down,
                                   v_final_norm_w)))
    ix, iy, ic = _mesh_pos()
    chip = 2 * ix + iy
    dev = 2 * chip + ic

    c_all = _allgather_devices(c.reshape(8, LANES), "gather_cond").reshape(8, D_MODEL)
    b_sh = lax.dynamic_slice_in_dim(b_ada, chip * ADA_SHARD, ADA_SHARD, axis=1)
    mod_sh = _ada_mod(c_all, w_ada, b_sh.reshape(DEPTH, 1, ADA_SHARD), "ada_mod")
    w_in_b, w_out_b, w_up_b, w_down_b = (t.astype(BF16) for t in (w_in, w_out, w_up, w_down))
    first = _gather_chips([mod_sh, conv_a_w, conv_f_w, lb_gamma, w_in_b[0]], "gather_first")
    later = [w_in_b[1], w_out_b, w_up_b, w_down_b]
    started = _gather_chips_start(later, "gather_rest_start")
    mod_mine = lax.dynamic_index_in_dim(first[0], dev, axis=2, keepdims=False) + started[-1][0, 0]
    mods = [jnp.concatenate([mod_mine[j, l] for j in range(4)]).reshape(N_MOD, D_MODEL) for l in range(DEPTH)]
    conv_a_w_f, conv_f_w_f, gamma_f = (_unstack_chips(first[k], 2) for k in (1, 2, 3))
    w_in0 = _unstack_chips(first[4], 1)

    def later_weights(after):
        own, lands = _gather_chips_wait(started, after, "gather_rest_wait")
        whole = lambda n, axis: jnp.concatenate([jnp.where(chip == j, own[n], lands[n][j]) for j in range(4)], axis=axis)
        return whole(0, 1), whole(1, 1), whole(2, 2), whole(3, 1)

    lb1, p_soft = _lower_bounds(gamma_f.reshape(DEPTH, 2 * REC_WIDTH), "lower_bounds")
    lbs = [jnp.zeros((2, REC_WIDTH), F32), lb1.reshape(2, REC_WIDTH)]
    small = []
    for l in range(DEPTH):
        small.append(dict(norm1_w=norm1_w[l][None], conv_a_w=conv_a_w_f[l], conv_a_b=conv_a_b[l][None],
                          ln_a_w=ln_a_w[l][None], ln_a_b=ln_a_b[l][None], rec_norm_w=rec_norm_w[l],
                          norm2_w=norm2_w[l][None], conv_f_w=conv_f_w_f[l]))

    core_id, chip_id = ic.astype(jnp.int32).reshape(1), chip.astype(jnp.int32).reshape(1)
    pending, groups = {}, []

    def on_layer_grads(l, by_kind, last):
        if l > 0:
            pending.update(by_kind)
            if not last:
                return None
            by_kind = dict(pending)
        kinds = sorted(by_kind)
        in_flight = not (l == 0 and last)
        tag = f"l{l}" if l > 0 else f"l{l}_{'mix' if last else 'ffn'}"
        sums, exchange = _rs_first_stages([by_kind[k] for k in kinds], kinds, core_id, tag, in_flight)
        groups.append((l, kinds, sums, exchange, in_flight, tag))
        return exchange[-1][0:1, 0:1] if in_flight else None

    loss, dx, grads, dfw = _sequence_step(x[0], loss_target[0], mods, lbs, small, w_in0, later_weights,
                                          final_norm_w[None], on_layer_grads)
    loss = lax.psum(loss, ("x", "y", "c"))
    pair_sums = [[None] * DEPTH for _ in BIG_KINDS]
    slots = [[None] * DEPTH for _ in BIG_KINDS]
    for l, kinds, sums, exchange, in_flight, tag in groups:
        received = _rs_chip_wait(exchange, dx, f"rs_chip_wait_{tag}") if in_flight else exchange
        for n, ki in enumerate(kinds):
            pair_sums[ki][l], slots[ki][l] = sums[n], received[n]

    dgamma = _lower_bounds_bwd(grads[1]["lb"].reshape(1, 2 * REC_WIDTH), p_soft, "lower_bounds_bwd")
    dmod = [jnp.concatenate(grads[l]["mod"], axis=1) for l in range(DEPTH)]
    stack = lambda key: jnp.stack([grads[l][key] for l in range(DEPTH)])
    local_small = dict(b_ada=jnp.concatenate(dmod, axis=0), norm1_w=stack("norm1_w"), conv_a_w=stack("conv_a_w"),
                       conv_a_b=stack("conv_a_b"), ln_a_w=stack("ln_a_w"), ln_a_b=stack("ln_a_b"), lb_gamma=dgamma,
                       rec_norm_w=stack("rec_norm_w"), norm2_w=stack("norm2_w"), conv_f_w=stack("conv_f_w"),
                       final_norm_w=dfw)
    pack = _pack_rows([local_small[name] for name, _, _ in SMALL_PARAMS])
    rows = pack.shape[0]
    packs = _allgather_devices(pack, "gather_small_grads").reshape(8, rows, LANES)
    summed = _sum_devices(packs, "sum_small_grads").reshape(-1)
    small_grads = dict(zip([n for n, _, _ in SMALL_PARAMS], _unpack(summed, [shp for _, shp, _ in SMALL_PARAMS])))

    dmod_all = packs.reshape(8, rows * LANES)[:, :DEPTH * N_MOD * D_MODEL].reshape(8, DEPTH, N_MOD * D_MODEL)
    dmod_sh = lax.dynamic_slice_in_dim(dmod_all, chip * ADA_SHARD, ADA_SHARD, axis=2).transpose(1, 0, 2)
    g_ada, d_ada, m_ada, v_ada = _ada_update(c_all, dmod_sh, w_ada, m_w_ada, v_w_ada, "ada_update")

    for name, shp, axis in SMALL_PARAMS:
        if axis is not None:
            width = shp[axis] // 4
            small_grads[name] = lax.dynamic_slice_in_dim(small_grads[name], chip * width, width, axis=axis)
    names = [n for n, _, _ in SMALL_PARAMS]
    packed = [_pack_rows([src[n] for n in names])[None] for src in (params, small_grads, mom1, mom2)]
    small_out = _adamw(*packed, "adamw_small")
    shapes = [params[n].shape for n in names]
    small_delta, small_m, small_v = (dict(zip(names, _unpack(t.reshape(-1), shapes))) for t in small_out)

    summed_big = _rs_last_stages(pair_sums, slots, chip_id)
    grad, delta, new_m, new_v = dict(small_grads), small_delta, small_m, small_v
    grad["w_ada"], delta["w_ada"], new_m["w_ada"], new_v["w_ada"] = g_ada, d_ada, m_ada, v_ada
    for (name, _, _, _), (mine, theirs) in zip(BIG_KINDS, summed_big):
        grad[name], delta[name], new_m[name], new_v[name] = _adamw_halves(
            params[name], mine, theirs, mom1[name], mom2[name], core_id, f"adamw_{name}")

    return (loss, dx[None], *[grad[n] for n in WEIGHT_NAMES], *[delta[n] for n in WEIGHT_NAMES],
            *[new_m[n] for n in WEIGHT_NAMES], *[new_v[n] for n in WEIGHT_NAMES])
```

```python
import numpy as np
import jax
import jax.numpy as jnp
from jax import lax
from jax.experimental import pallas as pl
from jax.experimental.pallas import tpu as pltpu

F32 = jnp.float32
BF16 = jnp.bfloat16

D_MODEL = 1024
DEPTH = 2
HEAD_DIM = 64
CONV_CH = 256
CONV_WIDTH = 31
ATT_WIDTH = 384
N_HEADS = 6
DILATIONS = (1, 4, 16)
ATT_HALF = 64
ATT_BLOCK = 128
ALIBI_MAX_EXP = 8.0
MASK_VALUE = -1e30
REC_WIDTH = 384
REC_CHUNK = 64
F_TINY = 1e-30
D_FF = 2816
N_MOD = 6
EPS = 1e-6
G_CONV = (0, 512)
G_QKV = (512, 1664)
G_REC = (1664, 3584)
IN_COLS = 3584

ADAM_LR = 0.001
ADAM_B1 = 0.9
ADAM_B2 = 0.999
ADAM_EPS = 1e-08
ADAM_WD = 0.01
ADAM_STEP = 10

VMEM_LIMIT_BYTES = 56 * 1024 * 1024
LANES = 128
MESH = pl.DeviceIdType.MESH
ANY = pl.BlockSpec(memory_space=pl.ANY)


def _params(n_axes):
    return pltpu.CompilerParams(dimension_semantics=("arbitrary",) * n_axes,
                                vmem_limit_bytes=VMEM_LIMIT_BYTES)


def _tile(n, target):
    best = None
    for t in range(LANES, min(n, target) + 1, LANES):
        if n % t == 0:
            best = t
    return best or n


def _sigmoid(x):
    return jax.nn.sigmoid(x)


def _silu_grad(x):
    s = _sigmoid(x)
    return s * (1.0 + x * (1.0 - s))


MM_ACC_ELEMS = 1536 * 1024


def _matmul(a, b, mode, out_dtype, name, tm=1024, tn=1792, tk=1792):
    if mode == "nn":
        (m, k), (k2, n) = a.shape, b.shape
    elif mode == "nt":
        (m, k), (n, k2) = a.shape, b.shape
    else:
        (k, m), (k2, n) = a.shape, b.shape
    assert k == k2, (a.shape, b.shape, mode)
    tn, tk = _tile(n, tn), _tile(k, tk)
    tm = _tile(m, min(tm, MM_ACC_ELEMS // tn))
    nk = k // tk
    a_spec = (pl.BlockSpec((tk, tm), lambda i, j, kk: (kk, i)) if mode == "tn"
              else pl.BlockSpec((tm, tk), lambda i, j, kk: (i, kk)))
    b_spec = (pl.BlockSpec((tn, tk), lambda i, j, kk: (j, kk)) if mode == "nt"
              else pl.BlockSpec((tk, tn), lambda i, j, kk: (kk, j)))
    dims = {"nn": (((1,), (0,)), ((), ())), "nt": (((1,), (1,)), ((), ())),
            "tn": (((0,), (0,)), ((), ()))}[mode]

    def body(a_ref, b_ref, o_ref, *scratch):
        part = lax.dot_general(a_ref[...].astype(BF16), b_ref[...].astype(BF16), dims, preferred_element_type=F32)
        if nk == 1:
            o_ref[...] = part.astype(out_dtype)
            return
        acc_ref, = scratch
        kk = pl.program_id(2)

        @pl.when(kk == 0)
        def _():
            acc_ref[...] = part

        @pl.when(kk > 0)
        def _():
            acc_ref[...] += part

        @pl.when(kk == nk - 1)
        def _():
            o_ref[...] = acc_ref[...].astype(out_dtype)

    return pl.pallas_call(
        body, name=name, grid=(m // tm, n // tn, nk),
        in_specs=[a_spec, b_spec],
        out_specs=pl.BlockSpec((tm, tn), lambda i, j, kk: (i, j)),
        out_shape=jax.ShapeDtypeStruct((m, n), out_dtype),
        scratch_shapes=[pltpu.VMEM((tm, tn), F32)] if nk > 1 else [],
        compiler_params=pltpu.CompilerParams(dimension_semantics=("parallel", "parallel", "arbitrary"),
                                             vmem_limit_bytes=VMEM_LIMIT_BYTES),
    )(a, b)


def _matmul_two_lhs(a1, a2, b, out_dtype, name):
    (m, k1), n = a1.shape, b.shape[0]
    tn, tk = _tile(n, 1792), _tile(k1, 1792)
    tm = _tile(m, min(1024, MM_ACC_ELEMS // tn))
    nk1 = k1 // tk
    nk = 2 * nk1

    def body(a1_ref, a2_ref, b_ref, o_ref, acc_ref):
        kk = pl.program_id(2)
        lhs = jnp.where(kk < nk1, a1_ref[...], a2_ref[...])
        part = lax.dot_general(lhs, b_ref[...], (((1,), (1,)), ((), ())), preferred_element_type=F32)

        @pl.when(kk == 0)
        def _():
            acc_ref[...] = part

        @pl.when(kk > 0)
        def _():
            acc_ref[...] += part

        @pl.when(kk == nk - 1)
        def _():
            o_ref[...] = acc_ref[...].astype(out_dtype)

    return pl.pallas_call(
        body, name=name, grid=(m // tm, n // tn, nk),
        in_specs=[pl.BlockSpec((tm, tk), lambda i, j, kk: (i, jnp.minimum(kk, nk1 - 1))),
                  pl.BlockSpec((tm, tk), lambda i, j, kk: (i, jnp.maximum(kk - nk1, 0))),
                  pl.BlockSpec((tn, tk), lambda i, j, kk: (j, kk))],
        out_specs=pl.BlockSpec((tm, tn), lambda i, j, kk: (i, j)),
        out_shape=jax.ShapeDtypeStruct((m, n), out_dtype),
        scratch_shapes=[pltpu.VMEM((tm, tn), F32)],
        compiler_params=pltpu.CompilerParams(dimension_semantics=("parallel", "parallel", "arbitrary"),
                                             vmem_limit_bytes=VMEM_LIMIT_BYTES),
    )(a1, a2, b)


def _matmul_two_rhs(a, b1, b2, out_dtype, name):
    (k, m), n1 = a.shape, b1.shape[1]
    tn, tk = _tile(n1, 1792), _tile(k, 1792)
    tm = _tile(m, min(1024, MM_ACC_ELEMS // tn))
    nj1, nk = n1 // tn, k // tk

    def body(a_ref, b1_ref, b2_ref, o_ref, acc_ref):
        j, kk = pl.program_id(1), pl.program_id(2)
        rhs = jnp.where(j < nj1, b1_ref[...], b2_ref[...])
        part = lax.dot_general(a_ref[...], rhs, (((0,), (0,)), ((), ())), preferred_element_type=F32)

        @pl.when(kk == 0)
        def _():
            acc_ref[...] = part

        @pl.when(kk > 0)
        def _():
            acc_ref[...] += part

        @pl.when(kk == nk - 1)
        def _():
            o_ref[...] = acc_ref[...].astype(out_dtype)

    return pl.pallas_call(
        body, name=name, grid=(m // tm, 2 * nj1, nk),
        in_specs=[pl.BlockSpec((tk, tm), lambda i, j, kk: (kk, i)),
                  pl.BlockSpec((tk, tn), lambda i, j, kk: (jnp.where(j < nj1, kk, 0), jnp.minimum(j, nj1 - 1))),
                  pl.BlockSpec((tk, tn), lambda i, j, kk: (jnp.where(j < nj1, 0, kk), jnp.maximum(j - nj1, 0)))],
        out_specs=pl.BlockSpec((tm, tn), lambda i, j, kk: (i, j)),
        out_shape=jax.ShapeDtypeStruct((m, 2 * n1), out_dtype),
        scratch_shapes=[pltpu.VMEM((tm, tn), F32)],
        compiler_params=pltpu.CompilerParams(dimension_semantics=("parallel", "parallel", "arbitrary"),
                                             vmem_limit_bytes=VMEM_LIMIT_BYTES),
    )(a, b1, b2)


NORM_ROWS = 256


def _row_spec(width, rows=NORM_ROWS):
    return pl.BlockSpec((rows, width), lambda i: (i, 0))


def _vec_spec(width):
    return pl.BlockSpec((1, width), lambda i: (0, 0))


def _resid_norm_mod(x, r, g, nw, sc, sh, name):
    s, d = x.shape
    has_r = r is not None

    def body(*refs):
        if has_r:
            x_ref, r_ref, g_ref, nw_ref, sc_ref, sh_ref, xn_ref, h_ref = refs
            xn = x_ref[...] + g_ref[...] * r_ref[...].astype(F32)
            xn_ref[...] = xn
        else:
            x_ref, nw_ref, sc_ref, sh_ref, h_ref = refs
            xn = x_ref[...]
        rstd = lax.rsqrt(jnp.mean(xn * xn, axis=-1, keepdims=True) + EPS)
        y = xn * rstd * nw_ref[...]
        h_ref[...] = (y * (1.0 + sc_ref[...]) + sh_ref[...]).astype(BF16)

    if has_r:
        ins, in_specs = (x, r, g, nw, sc, sh), [_row_spec(d), _row_spec(d)] + [_vec_spec(d)] * 4
        out_shape = (jax.ShapeDtypeStruct((s, d), F32), jax.ShapeDtypeStruct((s, d), BF16))
        out_specs = (_row_spec(d), _row_spec(d))
    else:
        ins, in_specs = (x, nw, sc, sh), [_row_spec(d)] + [_vec_spec(d)] * 3
        out_shape = jax.ShapeDtypeStruct((s, d), BF16)
        out_specs = _row_spec(d)
    return pl.pallas_call(body, name=name, grid=(s // NORM_ROWS,), in_specs=in_specs, out_specs=out_specs,
                          out_shape=out_shape, compiler_params=_params(1))(*ins)


def _final_loss(x, r, g, fw, tgt, name):
    s, d = x.shape

    def body(x_ref, r_ref, g_ref, fw_ref, t_ref, loss_ref, dx_ref, dr_ref, dg_ref, dfw_ref):
        @pl.when(pl.program_id(0) == 0)
        def _():
            loss_ref[...] = jnp.zeros_like(loss_ref)
            dg_ref[...] = jnp.zeros_like(dg_ref)
            dfw_ref[...] = jnp.zeros_like(dfw_ref)

        rr = r_ref[...].astype(F32)
        gg = g_ref[...]
        xn = x_ref[...] + gg * rr
        rstd = lax.rsqrt(jnp.mean(xn * xn, axis=-1, keepdims=True) + EPS)
        xh = xn * rstd
        fwv = fw_ref[...]
        e = xh * fwv - t_ref[...]
        loss_ref[...] += 0.5 * jnp.sum(jnp.mean(e * e, axis=-1, keepdims=True), axis=0, keepdims=True)
        dy = e * (1.0 / d)
        dfw_ref[...] += jnp.sum(dy * xh, axis=0, keepdims=True)
        dxh = dy * fwv
        dx = rstd * (dxh - xh * jnp.mean(dxh * xh, axis=-1, keepdims=True))
        dx_ref[...] = dx
        dr_ref[...] = (gg * dx).astype(BF16)
        dg_ref[...] += jnp.sum(dx * rr, axis=0, keepdims=True)

    return pl.pallas_call(
        body, name=name, grid=(s // NORM_ROWS,),
        in_specs=[_row_spec(d), _row_spec(d), _vec_spec(d), _vec_spec(d), _row_spec(d)],
        out_specs=(_vec_spec(LANES), _row_spec(d), _row_spec(d), _vec_spec(d), _vec_spec(d)),
        out_shape=(jax.ShapeDtypeStruct((1, LANES), F32), jax.ShapeDtypeStruct((s, d), F32),
                   jax.ShapeDtypeStruct((s, d), BF16), jax.ShapeDtypeStruct((1, d), F32),
                   jax.ShapeDtypeStruct((1, d), F32)),
        compiler_params=_params(1))(x, r, g, fw, tgt)


def _norm_bwd(x, dhs, dxres, nw, sc, g, r, name):
    s, d = x.shape
    n_dh = len(dhs)
    has_g = g is not None

    def body(*refs):
        x_ref = refs[0]
        dh_refs = refs[1:1 + n_dh]
        dxres_ref, nw_ref, sc_ref = refs[1 + n_dh:4 + n_dh]
        pos = 4 + n_dh
        if has_g:
            g_ref, r_ref = refs[pos:pos + 2]
            pos += 2
            dx_ref, dr_ref, dsh_ref, dsc_ref, dnw_ref, dg_ref = refs[pos:]
            accs = (dsh_ref, dsc_ref, dnw_ref, dg_ref)
        else:
            dx_ref, dsh_ref, dsc_ref, dnw_ref = refs[pos:]
            accs = (dsh_ref, dsc_ref, dnw_ref)

        @pl.when(pl.program_id(0) == 0)
        def _():
            for acc in accs:
                acc[...] = jnp.zeros_like(acc)

        xv = x_ref[...]
        dh = dh_refs[0][...].astype(F32)
        for extra in dh_refs[1:]:
            dh = dh + extra[...].astype(F32)
        rstd = lax.rsqrt(jnp.mean(xv * xv, axis=-1, keepdims=True) + EPS)
        xh = xv * rstd
        nwv = nw_ref[...]
        dsh_ref[...] += jnp.sum(dh, axis=0, keepdims=True)
        dsc_ref[...] += jnp.sum(dh * (xh * nwv), axis=0, keepdims=True)
        dy = dh * (1.0 + sc_ref[...])
        dnw_ref[...] += jnp.sum(dy * xh, axis=0, keepdims=True)
        dxh = dy * nwv
        dx = dxres_ref[...] + rstd * (dxh - xh * jnp.mean(dxh * xh, axis=-1, keepdims=True))
        dx_ref[...] = dx
        if has_g:
            dr_ref[...] = (g_ref[...] * dx).astype(BF16)
            dg_ref[...] += jnp.sum(dx * r_ref[...].astype(F32), axis=0, keepdims=True)

    ins = [x, *dhs, dxres, nw, sc]
    in_specs = [_row_spec(d)] * (2 + n_dh) + [_vec_spec(d)] * 2
    out_shape = [jax.ShapeDtypeStruct((s, d), F32)]
    out_specs = [_row_spec(d)]
    if has_g:
        ins += [g, r]
        in_specs += [_vec_spec(d), _row_spec(d)]
        out_shape.append(jax.ShapeDtypeStruct((s, d), BF16))
        out_specs.append(_row_spec(d))
    n_vec = 4 if has_g else 3
    out_shape += [jax.ShapeDtypeStruct((1, d), F32)] * n_vec
    out_specs += [_vec_spec(d)] * n_vec
    return pl.pallas_call(body, name=name, grid=(s // NORM_ROWS,), in_specs=in_specs, out_specs=tuple(out_specs),
                          out_shape=tuple(out_shape), compiler_params=_params(1))(*ins)


FFN_ROWS = 256
FFN_COLS = 1408
HALO = 16
INV_SQRT2 = 0.7071067811865476
INV_SQRT_2PI = 0.3989422804014327


def _gelu(x):
    return 0.5 * x * (1.0 + lax.erf(x * INV_SQRT2))


def _gelu_grad(x):
    return 0.5 * (1.0 + lax.erf(x * INV_SQRT2)) + x * (INV_SQRT_2PI * jnp.exp(-0.5 * x * x))


def _halo_specs(rows, cols, halo, n_rows_total, col_of):
    per = rows // halo
    last = n_rows_total // halo - 1
    cur = pl.BlockSpec((rows, cols), lambda j, i: (i, col_of(j)))
    prev = pl.BlockSpec((halo, cols), lambda j, i: (jnp.maximum(i * per - 1, 0), col_of(j)))
    nxt = pl.BlockSpec((halo, cols), lambda j, i: (jnp.minimum((i + 1) * per, last), col_of(j)))
    return [prev, cur, nxt]


def _shift_rows(x, k):
    n = x.shape[0]
    return pltpu.roll(x, k % n, axis=0)


def _conv3(ext, w):
    return w[0:1, :] * _shift_rows(ext, 1) + w[1:2, :] * ext + w[2:3, :] * _shift_rows(ext, -1)


def _ext_block(prev_ref, cur_ref, next_ref, i, n_i):
    prev = jnp.where(i > 0, prev_ref[...].astype(F32), 0.0)
    nxt = jnp.where(i < n_i - 1, next_ref[...].astype(F32), 0.0)
    return jnp.concatenate([prev, cur_ref[...].astype(F32), nxt], axis=0)


def _ffn_act(u, cw, name):
    s = u.shape[0]
    nc, ns = D_FF // FFN_COLS, s // FFN_ROWS

    def body(gp, gc, gn, vp, vc, vn, wg_ref, wv_ref, o_ref, cg_ref, cv_ref):
        i = pl.program_id(1)
        cg = _conv3(_ext_block(gp, gc, gn, i, ns), wg_ref[...])[HALO:HALO + FFN_ROWS]
        cv = _conv3(_ext_block(vp, vc, vn, i, ns), wv_ref[...])[HALO:HALO + FFN_ROWS]
        o_ref[...] = (_gelu(cg) * cv).astype(BF16)
        cg_ref[...] = cg.astype(BF16)
        cv_ref[...] = cv.astype(BF16)

    in_specs = (_halo_specs(FFN_ROWS, FFN_COLS, HALO, s, lambda j: j)
                + _halo_specs(FFN_ROWS, FFN_COLS, HALO, s, lambda j: j + nc)
                + [pl.BlockSpec((3, FFN_COLS), lambda j, i: (0, j)),
                   pl.BlockSpec((3, FFN_COLS), lambda j, i: (0, j + nc))])
    blk = pl.BlockSpec((FFN_ROWS, FFN_COLS), lambda j, i: (i, j))
    return pl.pallas_call(
        body, name=name, grid=(nc, ns), in_specs=in_specs, out_specs=(blk, blk, blk),
        out_shape=(jax.ShapeDtypeStruct((s, D_FF), BF16),) * 3, compiler_params=_params(2),
    )(u, u, u, u, u, u, cw, cw)


def _ffn_act_bwd(u, cg, cv, dact, cw, name):
    s = u.shape[0]
    nc, ns = D_FF // FFN_COLS, s // FFN_ROWS

    def body(ug_ref, uv_ref, gp, gc, gn, vp, vc, vn, dp, dc, dn, wg_ref, wv_ref, dug_ref, duv_ref, dwg_ref, dwv_ref):
        i = pl.program_id(1)

        @pl.when(i == 0)
        def _():
            dwg_ref[...] = jnp.zeros_like(dwg_ref)
            dwv_ref[...] = jnp.zeros_like(dwv_ref)

        cge = _ext_block(gp, gc, gn, i, ns)
        cve = _ext_block(vp, vc, vn, i, ns)
        da = _ext_block(dp, dc, dn, i, ns)
        dcg = da * cve * _gelu_grad(cge)
        dcv = da * _gelu(cge)
        inner = slice(HALO, HALO + FFN_ROWS)
        for d_c, u_ref, w_ref, du_ref, dw_ref in ((dcg, ug_ref, wg_ref, dug_ref, dwg_ref),
                                                  (dcv, uv_ref, wv_ref, duv_ref, dwv_ref)):
            w = w_ref[...]
            d_next, d_prev = _shift_rows(d_c, -1), _shift_rows(d_c, 1)
            du = w[0:1, :] * d_next + w[1:2, :] * d_c + w[2:3, :] * d_prev
            du_ref[...] = du[inner].astype(BF16)
            u_in = u_ref[...].astype(F32)
            for tap, d_tap in enumerate((d_next, d_c, d_prev)):
                dw_ref[tap:tap + 1, :] += jnp.sum(d_tap[inner] * u_in, axis=0, keepdims=True)

    blk = pl.BlockSpec((FFN_ROWS, FFN_COLS), lambda j, i: (i, j))
    in_specs = ([blk, pl.BlockSpec((FFN_ROWS, FFN_COLS), lambda j, i: (i, j + nc))]
                + _halo_specs(FFN_ROWS, FFN_COLS, HALO, s, lambda j: j) * 3
                + [pl.BlockSpec((3, FFN_COLS), lambda j, i: (0, j)),
                   pl.BlockSpec((3, FFN_COLS), lambda j, i: (0, j + nc))])
    acc = pl.BlockSpec((HALO, FFN_COLS), lambda j, i: (0, j))
    return pl.pallas_call(
        body, name=name, grid=(nc, ns), in_specs=in_specs, out_specs=(blk, blk, acc, acc),
        out_shape=(jax.ShapeDtypeStruct((s, D_FF), BF16), jax.ShapeDtypeStruct((s, D_FF), BF16),
                   jax.ShapeDtypeStruct((HALO, D_FF), F32), jax.ShapeDtypeStruct((HALO, D_FF), F32)),
        compiler_params=_params(2),
    )(u, u, cg, cg, cg, cv, cv, cv, dact, dact, dact, cw, cw)


CONV_ROWS = 512
CONV_HALO = 16
CONV_PAD = CONV_WIDTH // 2


def _conv_halo_specs(cols, s):
    per = CONV_ROWS // CONV_HALO
    last = s // CONV_HALO - 1
    return [pl.BlockSpec((CONV_HALO, cols), lambda i: (jnp.maximum(i * per - 1, 0), 0)),
            pl.BlockSpec((CONV_ROWS, cols), lambda i: (i, 0)),
            pl.BlockSpec((CONV_HALO, cols), lambda i: (jnp.minimum((i + 1) * per, last), 0))]


def _glu_ext(pp, pc, pn, i, n_i):
    ext = _ext_block(pp, pc, pn, i, n_i)
    return ext[:, :CONV_CH] * _sigmoid(ext[:, CONV_CH:])


def _conv_mixer(pa, cw, cb, lnw, lnb, name):
    s = pa.shape[0]
    ns = s // CONV_ROWS

    def body(pp, pc, pn, cw_ref, cb_ref, lnw_ref, lnb_ref, o_ref, c_ref):
        i = pl.program_id(0)
        a = _glu_ext(pp, pc, pn, i, ns)
        acc = jnp.zeros((CONV_ROWS, CONV_CH), F32)
        for tap in range(CONV_WIDTH):
            acc = acc + cw_ref[tap:tap + 1, :] * _shift_rows(a, -(tap + 1))[:CONV_ROWS]
        cv = acc + cb_ref[...]
        c_ref[...] = cv
        mu = jnp.mean(cv, axis=-1, keepdims=True)
        xc = cv - mu
        rstd = lax.rsqrt(jnp.mean(xc * xc, axis=-1, keepdims=True) + EPS)
        y = xc * rstd * lnw_ref[...] + lnb_ref[...]
        o_ref[...] = (y * _sigmoid(y)).astype(BF16)

    vec = pl.BlockSpec((1, CONV_CH), lambda i: (0, 0))
    blk = pl.BlockSpec((CONV_ROWS, CONV_CH), lambda i: (i, 0))
    return pl.pallas_call(
        body, name=name, grid=(ns,),
        in_specs=_conv_halo_specs(2 * CONV_CH, s) + [pl.BlockSpec((CONV_WIDTH, CONV_CH), lambda i: (0, 0)), vec, vec, vec],
        out_specs=(blk, blk),
        out_shape=(jax.ShapeDtypeStruct((s, CONV_CH), BF16), jax.ShapeDtypeStruct((s, CONV_CH), F32)),
        compiler_params=_params(1))(pa, pa, pa, cw, cb, lnw, lnb)


def _conv_mixer_bwd_ln(cv, dout, lnw, lnb, name):
    s = cv.shape[0]

    def body(c_ref, do_ref, lnw_ref, lnb_ref, dc_ref, dlnw_ref, dlnb_ref, dcb_ref):
        @pl.when(pl.program_id(0) == 0)
        def _():
            dlnw_ref[...] = jnp.zeros_like(dlnw_ref)
            dlnb_ref[...] = jnp.zeros_like(dlnb_ref)
            dcb_ref[...] = jnp.zeros_like(dcb_ref)

        c = c_ref[...]
        mu = jnp.mean(c, axis=-1, keepdims=True)
        xc = c - mu
        rstd = lax.rsqrt(jnp.mean(xc * xc, axis=-1, keepdims=True) + EPS)
        xh = xc * rstd
        w = lnw_ref[...]
        y = xh * w + lnb_ref[...]
        dy = do_ref[...] * _silu_grad(y)
        dlnw_ref[...] += jnp.sum(dy * xh, axis=0, keepdims=True)
        dlnb_ref[...] += jnp.sum(dy, axis=0, keepdims=True)
        dxh = dy * w
        dc = rstd * (dxh - jnp.mean(dxh, axis=-1, keepdims=True) - xh * jnp.mean(dxh * xh, axis=-1, keepdims=True))
        dc_ref[...] = dc
        dcb_ref[...] += jnp.sum(dc, axis=0, keepdims=True)

    vec = pl.BlockSpec((1, CONV_CH), lambda i: (0, 0))
    blk = pl.BlockSpec((CONV_ROWS, CONV_CH), lambda i: (i, 0))
    return pl.pallas_call(
        body, name=name, grid=(s // CONV_ROWS,), in_specs=[blk, blk, vec, vec], out_specs=(blk, vec, vec, vec),
        out_shape=(jax.ShapeDtypeStruct((s, CONV_CH), F32),) + (jax.ShapeDtypeStruct((1, CONV_CH), F32),) * 3,
        compiler_params=_params(1))(cv, dout, lnw, lnb)


def _conv_mixer_bwd_conv(pa, dc, cw, name):
    s = pa.shape[0]
    ns = s // CONV_ROWS

    def body(pc, dp, dcc, dn, cw_ref, dpa_ref, dcw_ref):
        i = pl.program_id(0)

        @pl.when(i == 0)
        def _():
            dcw_ref[...] = jnp.zeros_like(dcw_ref)

        cur = pc[...]
        val, sg = cur[:, :CONV_CH], _sigmoid(cur[:, CONV_CH:])
        a_cur = val * sg
        dce = _ext_block(dp, dcc, dn, i, ns)
        da = jnp.zeros((CONV_ROWS, CONV_CH), F32)
        for tap in range(CONV_WIDTH):
            shifted = _shift_rows(dce, -(CONV_WIDTH - tap))[:CONV_ROWS]
            da = da + cw_ref[tap:tap + 1, :] * shifted
            dcw_ref[tap:tap + 1, :] += jnp.sum(shifted * a_cur, axis=0, keepdims=True)
        dpa_ref[:, :CONV_CH] = (da * sg).astype(BF16)
        dpa_ref[:, CONV_CH:] = (da * val * sg * (1.0 - sg)).astype(BF16)

    return pl.pallas_call(
        body, name=name, grid=(ns,),
        in_specs=[pl.BlockSpec((CONV_ROWS, 2 * CONV_CH), lambda i: (i, 0))] + _conv_halo_specs(CONV_CH, s)
        + [pl.BlockSpec((CONV_WIDTH, CONV_CH), lambda i: (0, 0))],
        out_specs=(pl.BlockSpec((CONV_ROWS, 2 * CONV_CH), lambda i: (i, 0)),
                   pl.BlockSpec((32, CONV_CH), lambda i: (0, 0))),
        out_shape=(jax.ShapeDtypeStruct((s, 2 * CONV_CH), BF16), jax.ShapeDtypeStruct((32, CONV_CH), F32)),
        compiler_params=_params(1))(pa, dc, dc, dc, cw)


SLOPES = tuple(float(2.0 ** (-ALIBI_MAX_EXP * (h + 1) / N_HEADS)) for h in range(N_HEADS))
ATT_SCALE = HEAD_DIM ** -0.5


PAIR = 2 * HEAD_DIM
N_PAIRS = N_HEADS // 2
ATT_WIN = ATT_BLOCK + 2 * ATT_HALF


ATT_GROUPS = {1: 4, 4: 1, 16: 1}


def _window_specs(dil, n_steps, col_of):
    per = 2 * ATT_GROUPS[dil]
    rows, halo = ATT_BLOCK * dil * ATT_GROUPS[dil], ATT_HALF * dil
    return [pl.BlockSpec((halo, PAIR), lambda i, p: (jnp.maximum(per * i - 1, 0), col_of(p))),
            pl.BlockSpec((rows, PAIR), lambda i, p: (i, col_of(p))),
            pl.BlockSpec((halo, PAIR), lambda i, p: (jnp.minimum(per * (i + 1), per * n_steps - 1), col_of(p)))]


def _residue(ref, r, n, dil, start=0):
    return ref[pl.ds(start * dil + r, n, stride=dil), :] if dil > 1 else ref[pl.ds(start + r, n), :]


def _store_residue(ref, r, dil, start, val):
    if dil > 1:
        ref[pl.ds(start * dil + r, val.shape[0], stride=dil), :] = val
    else:
        ref[pl.ds(start + r, val.shape[0]), :] = val


def _residue_window(refs, r, dil, g=0):
    prev, cur, nxt = refs
    groups = ATT_GROUPS[dil]
    lo = max(g * ATT_BLOCK - ATT_HALF, 0)
    hi = min((g + 1) * ATT_BLOCK + ATT_HALF, groups * ATT_BLOCK)
    parts = [_residue(prev, r, ATT_HALF, dil)] if g == 0 else []
    parts.append(_residue(cur, r, hi - lo, dil, lo))
    if g == groups - 1:
        parts.append(_residue(nxt, r, ATT_HALF, dil))
    return jnp.concatenate(parts, axis=0)


def _band_masks(i, length, dil, transposed):
    shape = (ATT_WIN, ATT_BLOCK) if transposed else (ATT_BLOCK, ATT_WIN)
    row = lax.broadcasted_iota(jnp.int32, shape, 0)
    col = lax.broadcasted_iota(jnp.int32, shape, 1)
    wide = row if transposed else col
    dist = jnp.abs((row - col - ATT_HALF) if transposed else (row + ATT_HALF - col))
    wpos = i * ATT_BLOCK - ATT_HALF + wide
    valid = (dist <= ATT_HALF) & (wpos >= 0) & (wpos < length)
    return valid, dist.astype(F32) * float(dil)


def _attn_branch(qkv, dil, name):
    s = qkv.shape[0]
    groups = ATT_GROUPS[dil]
    rows = ATT_BLOCK * dil * groups
    n_steps = s // rows
    length = s // dil
    nt = (((1,), (1,)), ((), ()))

    def body(q_ref, kp, kc, kn, vp, vc, vn, o_ref, l_ref):
        i, pair = pl.program_id(0), pl.program_id(1)
        items = [(g, r) for g in range(groups) for r in range(dil)]
        q = jnp.stack([_residue(q_ref, r, ATT_BLOCK, dil, g * ATT_BLOCK) for g, r in items]).astype(BF16)
        k = jnp.stack([_residue_window((kp, kc, kn), r, dil, g) for g, r in items]).astype(BF16)
        v = jnp.stack([_residue_window((vp, vc, vn), r, dil, g) for g, r in items]).astype(BF16)
        per_group = [_band_masks(i * groups + g, length, dil, False) for g in range(groups)]
        valid = jnp.stack([per_group[g][0] for g, _ in items]) if groups > 1 else per_group[0][0][None]
        distf = jnp.stack([per_group[g][1] for g, _ in items]) if groups > 1 else per_group[0][1][None]
        outs, lses = [], []
        for hh in range(2):
            sl = slice(hh * HEAD_DIM, (hh + 1) * HEAD_DIM)
            slope = jnp.where(pair == 0, SLOPES[hh], jnp.where(pair == 1, SLOPES[2 + hh], SLOPES[4 + hh]))
            sc = jnp.einsum("bqd,bkd->bqk", q[:, :, sl], k[:, :, sl], preferred_element_type=F32) * ATT_SCALE
            sc = jnp.where(valid, sc - slope * distf, MASK_VALUE)
            m = jnp.max(sc, axis=-1, keepdims=True)
            p = jnp.exp(sc - m)
            den = jnp.sum(p, axis=-1, keepdims=True)
            outs.append(jnp.einsum("bqk,bkd->bqd", p.astype(BF16), v[:, :, sl], preferred_element_type=F32) / den)
            lses.append(jnp.broadcast_to(m + jnp.log(den), (len(items), ATT_BLOCK, HEAD_DIM)))
        o_all, l_all = jnp.concatenate(outs, axis=2), jnp.concatenate(lses, axis=2)
        for n, (g, r) in enumerate(items):
            _store_residue(o_ref, r, dil, g * ATT_BLOCK, o_all[n])
            _store_residue(l_ref, r, dil, g * ATT_BLOCK, l_all[n])

    out_blk = pl.BlockSpec((rows, PAIR), lambda i, p: (i, p))
    return pl.pallas_call(
        body, name=name, grid=(n_steps, N_PAIRS),
        in_specs=[pl.BlockSpec((rows, PAIR), lambda i, p: (i, p))]
        + _window_specs(dil, n_steps, lambda p: N_PAIRS + p) + _window_specs(dil, n_steps, lambda p: 2 * N_PAIRS + p),
        out_specs=(out_blk, out_blk),
        out_shape=(jax.ShapeDtypeStruct((s, ATT_WIDTH), F32),) * 2,
        compiler_params=_params(2))(qkv, qkv, qkv, qkv, qkv, qkv, qkv)


ATT_ROWS = 512


def _attn_combine(outs, lses, name):
    s = outs[0].shape[0]

    def body(o1, o2, o3, l1, l2, l3, att_ref, att32_ref, lse_ref):
        ls = [l1[...], l2[...], l3[...]]
        m = jnp.maximum(jnp.maximum(ls[0], ls[1]), ls[2])
        es = [jnp.exp(l - m) for l in ls]
        den = es[0] + es[1] + es[2]
        att = (es[0] * o1[...] + es[1] * o2[...] + es[2] * o3[...]) / den
        att_ref[...] = att.astype(BF16)
        att32_ref[...] = att
        lse_ref[...] = m + jnp.log(den)

    blk = pl.BlockSpec((ATT_ROWS, ATT_WIDTH), lambda i: (i, 0))
    return pl.pallas_call(
        body, name=name, grid=(s // ATT_ROWS,), in_specs=[blk] * 6, out_specs=(blk, blk, blk),
        out_shape=(jax.ShapeDtypeStruct((s, ATT_WIDTH), BF16), jax.ShapeDtypeStruct((s, ATT_WIDTH), F32),
                   jax.ShapeDtypeStruct((s, ATT_WIDTH), F32)),
        compiler_params=_params(1))(*outs, *lses)


def _attn_delta(datt, att, name):
    s = att.shape[0]

    def body(d_ref, a_ref, delta_ref):
        prod = d_ref[...] * a_ref[...]
        for h in range(N_HEADS):
            sl = slice(h * HEAD_DIM, (h + 1) * HEAD_DIM)
            delta_ref[:, sl] = jnp.broadcast_to(jnp.sum(prod[:, sl], axis=-1, keepdims=True), (ATT_ROWS, HEAD_DIM))

    blk = pl.BlockSpec((ATT_ROWS, ATT_WIDTH), lambda i: (i, 0))
    return pl.pallas_call(
        body, name=name, grid=(s // ATT_ROWS,), in_specs=[blk, blk], out_specs=blk,
        out_shape=jax.ShapeDtypeStruct((s, ATT_WIDTH), F32), compiler_params=_params(1))(datt, att)


def _attn_branch_bwd(qkv, do, lse, delta, prev, dil, out_dtype, name):
    s = qkv.shape[0]
    groups = ATT_GROUPS[dil]
    rows = ATT_BLOCK * dil * groups
    n_steps = s // rows
    length = s // dil
    has_prev = prev is not None
    tn = (((0,), (0,)), ((), ()))
    nt = (((1,), (1,)), ((), ()))

    def body(*refs):
        qs, ks, vs, dos, ls, des = (refs[3 * n:3 * n + 3] for n in range(6))
        rest = refs[18:]
        if has_prev:
            pq, pk, pv = rest[:3]
            rest = rest[3:]
        dq_ref, dk_ref, dv_ref = rest
        i, pair = pl.program_id(0), pl.program_id(1)
        items = [(g, r) for g in range(groups) for r in range(dil)]
        cur = lambda t: jnp.stack([_residue(t[1], r, ATT_BLOCK, dil, g * ATT_BLOCK) for g, r in items])
        win = lambda t: jnp.stack([_residue_window(t, r, dil, g) for g, r in items])
        q_cur, k_cur, v_cur, do_cur = (cur(t).astype(BF16) for t in (qs, ks, vs, dos))
        q_win, k_win, v_win, do_win = (win(t).astype(BF16) for t in (qs, ks, vs, dos))
        l_cur, de_cur, l_win, de_win = cur(ls), cur(des), win(ls), win(des)

        def masks(transposed):
            per_group = [_band_masks(i * groups + g, length, dil, transposed) for g in range(groups)]
            if groups == 1:
                return per_group[0][0][None], per_group[0][1][None]
            return jnp.stack([per_group[g][0] for g, _ in items]), jnp.stack([per_group[g][1] for g, _ in items])

        valid_q, distf_q = masks(False)
        valid_k, distf_k = masks(True)
        dot = lambda eq, a, b: jnp.einsum(eq, a, b, preferred_element_type=F32)
        dqs, dks, dvs = [], [], []
        for hh in range(2):
            sl = slice(hh * HEAD_DIM, (hh + 1) * HEAD_DIM)
            one = slice(hh * HEAD_DIM, hh * HEAD_DIM + 1)
            slope = jnp.where(pair == 0, SLOPES[hh], jnp.where(pair == 1, SLOPES[2 + hh], SLOPES[4 + hh]))
            sc = dot("bqd,bkd->bqk", q_cur[:, :, sl], k_win[:, :, sl]) * ATT_SCALE - slope * distf_q
            p = jnp.exp(jnp.where(valid_q, sc - l_cur[:, :, one], MASK_VALUE))
            dp = dot("bqd,bkd->bqk", do_cur[:, :, sl], v_win[:, :, sl])
            ds = (p * (dp - de_cur[:, :, one]) * ATT_SCALE).astype(BF16)
            dqs.append(dot("bqk,bkd->bqd", ds, k_win[:, :, sl]))

            sc2 = dot("bqd,bkd->bqk", q_win[:, :, sl], k_cur[:, :, sl]) * ATT_SCALE - slope * distf_k
            p2 = jnp.exp(jnp.where(valid_k, sc2 - l_win[:, :, one], MASK_VALUE))
            dvs.append(dot("bqk,bqd->bkd", p2.astype(BF16), do_win[:, :, sl]))
            dp2 = dot("bqd,bkd->bqk", do_win[:, :, sl], v_cur[:, :, sl])
            ds2 = (p2 * (dp2 - de_win[:, :, one]) * ATT_SCALE).astype(BF16)
            dks.append(dot("bqk,bqd->bkd", ds2, q_win[:, :, sl]))
        for parts, acc, out in ((dqs, pq if has_prev else None, dq_ref), (dks, pk if has_prev else None, dk_ref),
                                (dvs, pv if has_prev else None, dv_ref)):
            val = jnp.concatenate(parts, axis=2)
            for n, (g, r) in enumerate(items):
                piece = val[n]
                if has_prev:
                    piece = piece + _residue(acc, r, ATT_BLOCK, dil, g * ATT_BLOCK)
                _store_residue(out, r, dil, g * ATT_BLOCK, piece.astype(out_dtype))

    blk = pl.BlockSpec((rows, PAIR), lambda i, p: (i, p))
    in_specs = (_window_specs(dil, n_steps, lambda p: p) + _window_specs(dil, n_steps, lambda p: N_PAIRS + p)
                + _window_specs(dil, n_steps, lambda p: 2 * N_PAIRS + p) + _window_specs(dil, n_steps, lambda p: p) * 3)
    ins = [qkv] * 9 + [do] * 3 + [lse] * 3 + [delta] * 3
    if has_prev:
        in_specs += [blk] * 3
        ins += list(prev)
    return pl.pallas_call(
        body, name=name, grid=(n_steps, N_PAIRS), in_specs=in_specs, out_specs=(blk, blk, blk),
        out_shape=(jax.ShapeDtypeStruct((s, ATT_WIDTH), out_dtype),) * 3,
        compiler_params=_params(2))(*ins)


TB = 2 * REC_CHUNK
REC_ROWS = 5 * REC_WIDTH


REC_LEVELS = 6


def _scan_pos(p, rev):
    p = p & (REC_CHUNK - 1)
    return (REC_CHUNK - 1 - p) if rev else p


def _split3(x):
    hi = x.astype(BF16)
    rest = x - hi.astype(F32)
    mid = rest.astype(BF16)
    return hi, mid, (rest - mid.astype(F32)).astype(BF16)


def _chunk_sums(x, rev, with_levels):
    row = lax.broadcasted_iota(jnp.int32, (TB, TB), 0)
    col = lax.broadcasted_iota(jnp.int32, (TB, TB), 1)
    same = (row < REC_CHUNK) == (col < REC_CHUNK)
    s_row, s_col = _scan_pos(row, rev), _scan_pos(col, rev)
    mats = [same & (s_row <= s_col)]
    if with_levels:
        for level in range(1, REC_LEVELS + 1):
            shift = REC_LEVELS + 1 - level
            boundary = ((s_col >> shift) << shift) + (REC_CHUNK >> level) - 1
            mats.append(same & (s_row <= boundary))
        mats.append(same)
    cat = jnp.concatenate([m.astype(BF16) for m in mats], axis=1)
    total = sum(jnp.dot(term, cat, preferred_element_type=F32) for term in _split3(x))
    return [total[:, n * TB:(n + 1) * TB] for n in range(len(mats))]


def _hg_prep(qraw, z, lb, rev):
    lane = lax.broadcasted_iota(jnp.int32, (REC_WIDTH, TB), 1)
    in_a = lane < REC_CHUNK
    scan = _scan_pos(lane, rev)
    sig, sigm = _sigmoid(z), _sigmoid(-z)
    f = lb + (1.0 - lb) * sig
    kk = (1.0 - lb) * sigm
    sums = _chunk_sums(jnp.log(jnp.maximum(f, F_TINY)), rev, True)
    b, bend = sums[0], sums[-1]
    q = qraw * _sigmoid(qraw)
    eq, ek = [], []
    for level in range(1, REC_LEVELS + 1):
        r = sums[level]
        e = jnp.exp(jnp.minimum(b - r, r - b))
        second = ((scan >> (REC_LEVELS - level)) & 1) == 1
        eq.append(jnp.where(second, e, 0.0))
        ek.append(jnp.where(second, 0.0, e))
    lanes_end = (0, REC_CHUNK) if rev else (REC_CHUNK - 1, TB - 1)
    end_a, end_b = (b[:, n:n + 1] for n in lanes_end)
    return dict(in_a=in_a, sig=sig, sigm=sigm, f=f, kk=kk, b=b, end_a=end_a, end_b=end_b,
                q=q, qh=q * jnp.exp(b), kh=kk * jnp.exp(bend - b), ekb=jnp.exp(bend - b), eq=eq, ek=ek)


def _level_masks(rev):
    row = lax.broadcasted_iota(jnp.int32, (TB, TB), 0)
    col = lax.broadcasted_iota(jnp.int32, (TB, TB), 1)
    same = (row < REC_CHUNK) == (col < REC_CHUNK)
    s_row, s_col = _scan_pos(row, rev), _scan_pos(col, rev)
    masks = [same & ((s_row >> (REC_LEVELS + 1 - level)) == (s_col >> (REC_LEVELS + 1 - level)))
             for level in range(1, REC_LEVELS + 1)]
    return masks, row == col


def _head_rows(x, h):
    return x[h * HEAD_DIM:(h + 1) * HEAD_DIM, :]


def _block_diag_mask():
    r = lax.broadcasted_iota(jnp.int32, (REC_WIDTH, REC_WIDTH), 0) // HEAD_DIM
    c = lax.broadcasted_iota(jnp.int32, (REC_WIDTH, REC_WIDTH), 1) // HEAD_DIM
    return (r == c).astype(F32)


def _heads(x):
    return x.reshape(N_HEADS, HEAD_DIM, TB)


def _hgrn_scan(projt, lb, rev, name):
    s = projt.shape[1]
    nblk = s // TB
    zrow = 2 if rev else 1
    tmap = (lambda i: nblk - 1 - i) if rev else (lambda i: i)
    tn = (((0,), (0,)), ((), ()))
    nt = (((1,), (1,)), ((), ()))

    def body(q_ref, z_ref, v_ref, lb_ref, o_ref, hs_ref, at_ref, h_ref):
        @pl.when(pl.program_id(0) == 0)
        def _():
            h_ref[...] = jnp.zeros_like(h_ref)

        v = v_ref[...]
        vb = v.astype(BF16)
        pr = _hg_prep(q_ref[...], z_ref[...], lb_ref[...], rev)
        q, kk = pr["q"], pr["kk"]
        masks, diag = _level_masks(rev)
        own = jnp.sum(_heads(q * kk), axis=1, keepdims=True)
        sc = jnp.where(diag[None], own, 0.0)
        for level in range(REC_LEVELS):
            qt = _heads((q * pr["eq"][level]).astype(BF16))
            kt = _heads((kk * pr["ek"][level]).astype(BF16))
            sc = sc + jnp.where(masks[level][None],
                                jnp.einsum("hks,hkt->hst", kt, qt, preferred_element_type=F32), 0.0)
        a_bf = sc.astype(BF16)
        at_ref[...] = a_bf
        o = jnp.einsum("hvs,hst->hvt", _heads(vb), a_bf, preferred_element_type=F32).reshape(REC_WIDTH, TB)
        bd_mask = _block_diag_mask()
        order = ((1, ~pr["in_a"], pr["end_b"]), (0, pr["in_a"], pr["end_a"]))
        if not rev:
            order = order[::-1]
        for slot, msk, bend in order:
            h0 = h_ref[...]
            hs_ref[slot] = h0
            o = o + lax.dot_general(h0.astype(BF16), jnp.where(msk, pr["qh"], 0.0).astype(BF16), tn,
                                    preferred_element_type=F32)
            upd = lax.dot_general(jnp.where(msk, pr["kh"], 0.0).astype(BF16), vb, nt, preferred_element_type=F32)
            h_ref[...] = jnp.exp(bend) * h0 + upd * bd_mask
        o_ref[...] = o

    row_blk = lambda r: pl.BlockSpec((REC_WIDTH, TB), lambda i: (r, tmap(i)))
    return pl.pallas_call(
        body, name=name, grid=(nblk,),
        in_specs=[row_blk(0), row_blk(zrow), row_blk(3), pl.BlockSpec((REC_WIDTH, 1), lambda i: (0, 0))],
        out_specs=(pl.BlockSpec((REC_WIDTH, TB), lambda i: (0, tmap(i))),
                   pl.BlockSpec((2, REC_WIDTH, REC_WIDTH), lambda i: (tmap(i), 0, 0)),
                   pl.BlockSpec((None, N_HEADS, TB, TB), lambda i: (tmap(i), 0, 0, 0))),
        out_shape=(jax.ShapeDtypeStruct((REC_WIDTH, s), F32),
                   jax.ShapeDtypeStruct((s // REC_CHUNK, REC_WIDTH, REC_WIDTH), F32),
                   jax.ShapeDtypeStruct((nblk, N_HEADS, TB, TB), BF16)),
        scratch_shapes=[pltpu.VMEM((REC_WIDTH, REC_WIDTH), F32)],
        compiler_params=_params(1))(projt, projt, projt, lb)


def _hgrn_scan_bwd(projt, lb, dot, hs, at, prev, rev, name):
    s = projt.shape[1]
    nblk = s // TB
    zrow = 2 if rev else 1
    tmap = (lambda i: i) if rev else (lambda i: nblk - 1 - i)
    has_prev = prev is not None
    tn = (((0,), (0,)), ((), ()))
    nt = (((1,), (1,)), ((), ()))

    def body(*refs):
        q_ref, z_ref, v_ref, lb_ref, do_ref, hs_ref, at_ref = refs[:7]
        rest = refs[7:]
        if has_prev:
            pq_ref, pv_ref = rest[:2]
            rest = rest[2:]
        dq_ref, dz_ref, dv_ref, dlb_ref, dh_ref = rest

        @pl.when(pl.program_id(0) == 0)
        def _():
            dh_ref[...] = jnp.zeros_like(dh_ref)
            dlb_ref[...] = jnp.zeros_like(dlb_ref)

        qraw, v, do, lbv = q_ref[...], v_ref[...], do_ref[...], lb_ref[...]
        dob, vb = do.astype(BF16), v.astype(BF16)
        pr = _hg_prep(qraw, z_ref[...], lbv, rev)
        q, kk, b, in_a = pr["q"], pr["kk"], pr["b"], pr["in_a"]
        masks, diag = _level_masks(rev)
        dot = lambda eq, x, y: jnp.einsum(eq, x, y, preferred_element_type=F32)
        d_at = dot("hvs,hvt->hst", _heads(vb), _heads(dob))
        dv = dot("hvt,hst->hvs", _heads(dob), at_ref[...]).reshape(REC_WIDTH, TB)
        d_own = jnp.sum(jnp.where(diag[None], d_at, 0.0), axis=1, keepdims=True)
        dq_in = (d_own * _heads(kk)).reshape(REC_WIDTH, TB)
        dk_in = (d_own * _heads(q)).reshape(REC_WIDTH, TB)
        db_in = jnp.zeros((REC_WIDTH, TB), F32)
        for lv in range(REC_LEVELS):
            d_lv = jnp.where(masks[lv][None], d_at, 0.0).astype(BF16)
            q_lv, k_lv = (q * pr["eq"][lv]).astype(BF16), (kk * pr["ek"][lv]).astype(BF16)
            dqt = dot("hks,hst->hkt", _heads(k_lv), d_lv).reshape(REC_WIDTH, TB)
            dkt = dot("hkt,hst->hks", _heads(q_lv), d_lv).reshape(REC_WIDTH, TB)
            dq_in = dq_in + pr["eq"][lv] * dqt
            dk_in = dk_in + pr["ek"][lv] * dkt
            db_in = db_in + q_lv.astype(F32) * dqt - k_lv.astype(F32) * dkt
        dq = dk = jnp.zeros((REC_WIDTH, TB), F32)

        zero = jnp.zeros((REC_WIDTH, TB), F32)
        bd_mask = _block_diag_mask()
        eb = jnp.exp(b)
        const = zero
        order = ((0, in_a, pr["end_a"]), (1, ~in_a, pr["end_b"]))
        if not rev:
            order = order[::-1]
        for slot, msk, bend in order:
            h0 = hs_ref[slot]
            dh1 = dh_ref[...]
            dh1b = dh1.astype(BF16)
            dq = dq + eb * jnp.dot(h0.astype(BF16), jnp.where(msk, do, 0.0).astype(BF16), preferred_element_type=F32)
            dv = dv + lax.dot_general(dh1b, jnp.where(msk, pr["kh"], 0.0).astype(BF16), tn, preferred_element_type=F32)
            dk_int = pr["ekb"] * jnp.dot(dh1b, jnp.where(msk, v, 0.0).astype(BF16), preferred_element_type=F32)
            dk = dk + dk_int
            ebend = jnp.exp(bend)
            c = (jnp.sum(kk * dk_int, axis=1, keepdims=True)
                 + ebend * jnp.sum(h0 * dh1, axis=1, keepdims=True))
            const = const + jnp.where(msk, c, 0.0)
            upd = lax.dot_general(jnp.where(msk, pr["qh"], 0.0).astype(BF16), dob, nt, preferred_element_type=F32)
            dh_ref[...] = ebend * dh1 + upd * bd_mask

        dg = _chunk_sums(db_in + q * dq - kk * dk, not rev, False)[0] + const
        dq, dk = dq + dq_in, dk + dk_in
        sig, sigm, f = pr["sig"], pr["sigm"], pr["f"]
        live = f > F_TINY
        inv_f = 1.0 / jnp.maximum(f, F_TINY)
        one_lb = 1.0 - lbv
        dz = sig * sigm * one_lb * (jnp.where(live, dg * inv_f, 0.0) - dk)
        dlb_ref[...] += jnp.sum(sigm * (jnp.where(live, dg * inv_f, 0.0) - dk), axis=1, keepdims=True)
        dqr = dq * _silu_grad(qraw)
        if has_prev:
            dqr = dqr + pq_ref[...]
            dv = dv + pv_ref[...]
        dq_ref[...] = dqr
        dz_ref[...] = dz
        dv_ref[...] = dv

    row_blk = lambda r: pl.BlockSpec((REC_WIDTH, TB), lambda i: (r, tmap(i)))
    blk = pl.BlockSpec((REC_WIDTH, TB), lambda i: (0, tmap(i)))
    col = pl.BlockSpec((REC_WIDTH, 1), lambda i: (0, 0))
    in_specs = [row_blk(0), row_blk(zrow), row_blk(3), col, blk,
                pl.BlockSpec((2, REC_WIDTH, REC_WIDTH), lambda i: (tmap(i), 0, 0)),
                pl.BlockSpec((None, N_HEADS, TB, TB), lambda i: (tmap(i), 0, 0, 0))]
    ins = [projt, projt, projt, lb, dot, hs, at]
    if has_prev:
        in_specs += [blk, blk]
        ins += list(prev)
    t_shape = jax.ShapeDtypeStruct((REC_WIDTH, s), F32)
    return pl.pallas_call(
        body, name=name, grid=(nblk,), in_specs=in_specs, out_specs=(blk, blk, blk, col),
        out_shape=(t_shape, t_shape, t_shape, jax.ShapeDtypeStruct((REC_WIDTH, 1), F32)),
        scratch_shapes=[pltpu.VMEM((REC_WIDTH, REC_WIDTH), F32)],
        compiler_params=_params(1))(*ins)


REC_OUT_COLS = 512


def _head_rms(o):
    o3 = o.reshape(N_HEADS, HEAD_DIM, o.shape[1])
    rstd = lax.rsqrt(jnp.mean(o3 * o3, axis=1, keepdims=True) + EPS)
    return o3 * rstd, rstd


def _hgrn_out(of, ob, projt, wn, name):
    s = of.shape[1]

    def body(of_ref, ob_ref, g_ref, wn_ref, o_ref):
        on, _ = _head_rms(of_ref[...] + ob_ref[...])
        g = g_ref[...]
        y = on.reshape(REC_WIDTH, REC_OUT_COLS) * wn_ref[...] * (g * _sigmoid(g))
        o_ref[...] = y.T.astype(BF16)

    blk = pl.BlockSpec((REC_WIDTH, REC_OUT_COLS), lambda i: (0, i))
    return pl.pallas_call(
        body, name=name, grid=(s // REC_OUT_COLS,),
        in_specs=[blk, blk, pl.BlockSpec((REC_WIDTH, REC_OUT_COLS), lambda i: (4, i)),
                  pl.BlockSpec((REC_WIDTH, 1), lambda i: (0, 0))],
        out_specs=pl.BlockSpec((REC_OUT_COLS, REC_WIDTH), lambda i: (i, 0)),
        out_shape=jax.ShapeDtypeStruct((s, REC_WIDTH), BF16), compiler_params=_params(1))(of, ob, projt, wn)


def _hgrn_out_bwd(drec, of, ob, projt, wn, name):
    s = of.shape[1]

    def body(d_ref, of_ref, ob_ref, g_ref, wn_ref, do_ref, dg_ref, dwn_ref):
        @pl.when(pl.program_id(0) == 0)
        def _():
            dwn_ref[...] = jnp.zeros_like(dwn_ref)

        dy = d_ref[...].T
        on3, rstd = _head_rms(of_ref[...] + ob_ref[...])
        on = on3.reshape(REC_WIDTH, REC_OUT_COLS)
        g, wnv = g_ref[...], wn_ref[...]
        dg_ref[...] = dy * on * wnv * _silu_grad(g)
        d_onw = dy * (g * _sigmoid(g))
        dwn_ref[...] += jnp.sum(d_onw * on, axis=1, keepdims=True)
        d_on3 = (d_onw * wnv).reshape(N_HEADS, HEAD_DIM, REC_OUT_COLS)
        do3 = rstd * (d_on3 - on3 * jnp.mean(d_on3 * on3, axis=1, keepdims=True))
        do_ref[...] = do3.reshape(REC_WIDTH, REC_OUT_COLS)

    blk = pl.BlockSpec((REC_WIDTH, REC_OUT_COLS), lambda i: (0, i))
    col = pl.BlockSpec((REC_WIDTH, 1), lambda i: (0, 0))
    t_shape = jax.ShapeDtypeStruct((REC_WIDTH, s), F32)
    return pl.pallas_call(
        body, name=name, grid=(s // REC_OUT_COLS,),
        in_specs=[pl.BlockSpec((REC_OUT_COLS, REC_WIDTH), lambda i: (i, 0)), blk, blk,
                  pl.BlockSpec((REC_WIDTH, REC_OUT_COLS), lambda i: (4, i)), col],
        out_specs=(blk, blk, col),
        out_shape=(t_shape, t_shape, jax.ShapeDtypeStruct((REC_WIDTH, 1), F32)),
        compiler_params=_params(1))(drec, of, ob, projt, wn)


def _lower_bounds(gamma, name):
    def body(g_ref, lb_ref, p_ref):
        g0, g1 = g_ref[0:1, :], g_ref[1:2, :]
        m = jnp.maximum(g0, g1)
        e0, e1 = jnp.exp(g0 - m), jnp.exp(g1 - m)
        p0, p1 = e0 / (e0 + e1), e1 / (e0 + e1)
        lb_ref[...] = (p0 + p1) - p0
        p_ref[0:1, :] = p0
        p_ref[1:2, :] = p1

    n = gamma.shape[1]
    return pl.pallas_call(body, name=name,
                          out_shape=(jax.ShapeDtypeStruct((1, n), F32), jax.ShapeDtypeStruct((2, n), F32)))(gamma)


def _lower_bounds_bwd(dlb1, p, name):
    def body(d_ref, p_ref, o_ref):
        p0, p1, d = p_ref[0:1, :], p_ref[1:2, :], d_ref[...]
        inner = p1 * d
        o_ref[0:1, :] = p0 * (0.0 - inner)
        o_ref[1:2, :] = p1 * (d - inner)

    return pl.pallas_call(body, name=name, out_shape=jax.ShapeDtypeStruct(p.shape, F32))(dlb1, p)


def _split_w_in(w_in):
    return dict(conv=w_in[:, G_CONV[0]:G_CONV[1]], qkv=w_in[:, G_QKV[0]:G_QKV[1]],
                rec_t=w_in[:, G_REC[0]:].T, nat=w_in[:, :G_REC[0]])


def _split_w_rest(w_out, w_up, w_down):
    return dict(out=w_out, out_a=w_out[:CONV_CH], out_b=w_out[CONV_CH:CONV_CH + ATT_WIDTH],
                out_c=w_out[CONV_CH + ATT_WIDTH:], up=w_up, down=w_down)


def _col(v):
    return v.reshape(-1, 1)


def _sequence_step(x, tgt, mods, lbs, small, w_in0, later_weights, final_w, on_layer_grads):
    saved = []
    xin = x
    big = [_split_w_in(w_in0), None]
    h1 = _resid_norm_mod(x, None, None, small[0]["norm1_w"], mods[0][1:2], mods[0][0:1], "norm1_first")
    for l in range(DEPTH):
        sm, w, md = small[l], big[l], mods[l]
        pa = _matmul(h1, w["conv"], "nn", F32, f"proj_conv")
        qkv = _matmul(h1, w["qkv"], "nn", F32, f"proj_qkv")
        projt = _matmul(w["rec_t"], h1, "nt", F32, f"proj_rec")
        a_out, cv = _conv_mixer(pa, sm["conv_a_w"], sm["conv_a_b"], sm["ln_a_w"], sm["ln_a_b"], f"conv_mixer")
        outs, lses = zip(*[_attn_branch(qkv, d, f"attn_d{d}") for d in DILATIONS])
        att, att32, lse = _attn_combine(outs, lses, f"attn_combine")
        lb_f, lb_b = _col(lbs[l][0]), _col(lbs[l][1])
        of, hsf, atf = _hgrn_scan(projt, lb_f, False, "hgrn_fwd")
        ob, hsb, atb = _hgrn_scan(projt, lb_b, True, "hgrn_rev")
        wn = _col(sm["rec_norm_w"])
        rec = _hgrn_out(of, ob, projt, wn, f"hgrn_out")
        mixed = jnp.concatenate([a_out, att, rec], axis=1)
        if l == 0:
            w_in1, w_out_all, w_up_all, w_down_all = later_weights(rec)
            big[0].update(_split_w_rest(w_out_all[0], w_up_all[0], w_down_all[0]))
            big[1] = dict(_split_w_in(w_in1), **_split_w_rest(w_out_all[1], w_up_all[1], w_down_all[1]))
        r1 = _matmul(mixed, w["out"], "nn", BF16, "out_proj")
        xmid, h2 = _resid_norm_mod(xin, r1, md[2:3], sm["norm2_w"], md[4:5], md[3:4], f"norm2")
        u = _matmul(h2, w["up"], "nn", BF16, f"ffn_up")
        act, conv_g, conv_v = _ffn_act(u, sm["conv_f_w"], "ffn_act")
        r2 = _matmul(act, w["down"], "nn", BF16, "ffn_down")
        saved.append(dict(xin=xin, h1=h1, pa=pa, qkv=qkv, projt=projt, cv=cv, att32=att32, lse=lse, of=of, ob=ob,
                          hsf=hsf, hsb=hsb, atf=atf, atb=atb, lb_f=lb_f, lb_b=lb_b, wn=wn, mixed=mixed, r1=r1, xmid=xmid, h2=h2,
                          u=u, conv_g=conv_g, conv_v=conv_v, act=act, r2=r2))
        if l + 1 < DEPTH:
            nxt = small[l + 1]
            xin, h1 = _resid_norm_mod(xmid, r2, md[5:6], nxt["norm1_w"], mods[l + 1][1:2], mods[l + 1][0:1],
                                      "norm1")
    top = saved[-1]
    loss, dx, dr2, dg2, dfw = _final_loss(top["xmid"], top["r2"], mods[-1][5:6], final_w, tgt, "final_loss")

    grads = [None] * DEPTH
    order_after = None
    for l in reversed(range(DEPTH)):
        sm, w, md, sv = small[l], big[l], mods[l], saved[l]
        dact = _matmul(dr2, w["down"], "nt", BF16, f"d_act")
        g_down = _matmul(dr2, sv["act"], "tn", F32, "dw_down").T
        conv_f_w = sm["conv_f_w"] if order_after is None else sm["conv_f_w"] + order_after
        dug, duv, dwg, dwv = _ffn_act_bwd(sv["u"], sv["conv_g"], sv["conv_v"], dact, conv_f_w, "ffn_act_bwd")
        dh2 = _matmul_two_lhs(dug, duv, w["up"], BF16, "d_h2")
        g_up = _matmul_two_rhs(sv["h2"], dug, duv, F32, "dw_up")
        after_ffn = on_layer_grads(l, {2: g_up, 3: g_down}, False)
        norm2_w = sm["norm2_w"] if after_ffn is None else sm["norm2_w"] + after_ffn
        dxmid, dr1, dsh2, dsc2, dnw2, dg1 = _norm_bwd(sv["xmid"], [dh2], dx, norm2_w, md[4:5], md[2:3], sv["r1"],
                                                     f"norm2_bwd")
        dmix_a = _matmul(dr1, w["out_a"], "nt", F32, f"d_mix_a")
        dmix_b = _matmul(dr1, w["out_b"], "nt", F32, f"d_mix_b")
        dmix_c = _matmul(dr1, w["out_c"], "nt", F32, f"d_mix_c")
        g_out = _matmul(sv["mixed"], dr1, "tn", F32, f"dw_out")
        dc, dlnw, dlnb, dcb = _conv_mixer_bwd_ln(sv["cv"], dmix_a, sm["ln_a_w"], sm["ln_a_b"], f"conv_mixer_bwd_ln")
        dpa, dcw = _conv_mixer_bwd_conv(sv["pa"], dc, sm["conv_a_w"], f"conv_mixer_bwd_conv")
        delta = _attn_delta(dmix_b, sv["att32"], "attn_delta")
        dqkv = None
        for d in reversed(DILATIONS):
            dqkv = _attn_branch_bwd(sv["qkv"], dmix_b, sv["lse"], delta, dqkv, d, BF16 if d == 1 else F32,
                                    f"attn_bwd_d{d}")
        dot, dgt, dwn = _hgrn_out_bwd(dmix_c, sv["of"], sv["ob"], sv["projt"], sv["wn"], f"hgrn_out_bwd")
        dqf, dzf, dvf, dlbf = _hgrn_scan_bwd(sv["projt"], sv["lb_f"], dot, sv["hsf"], sv["atf"], None, False,
                                             "hgrn_fwd_bwd")
        dqt, dzb, dvt, dlbb = _hgrn_scan_bwd(sv["projt"], sv["lb_b"], dot, sv["hsb"], sv["atb"], (dqf, dvf), True,
                                             "hgrn_rev_bwd")
        dprojt = jnp.concatenate([dqt, dzf, dzb, dvt, dgt], axis=0).astype(BF16)
        dnat = jnp.concatenate([dpa, *dqkv], axis=1)
        dh1_a = _matmul(dnat, w["nat"], "nt", BF16, "d_h1_nat")
        dh1_b = _matmul(dprojt, w["rec_t"], "tn", BF16, "d_h1_rec")
        g_in_nat = _matmul(sv["h1"], dnat, "tn", F32, f"dw_in_nat")
        g_in_rec_t = _matmul(dprojt, sv["h1"], "nn", F32, f"dw_in_rec")
        g_in = jnp.concatenate([g_in_nat, g_in_rec_t.T], axis=1)
        if l > 0:
            below = saved[l - 1]
            dx, dr2, dsh1, dsc1, dnw1, dg2_below = _norm_bwd(sv["xin"], [dh1_a, dh1_b], dxmid, sm["norm1_w"], md[1:2],
                                                            mods[l - 1][5:6], below["r2"], f"norm1_bwd")
        else:
            dx, dsh1, dsc1, dnw1 = _norm_bwd(sv["xin"], [dh1_a, dh1_b], dxmid, sm["norm1_w"], md[1:2], None, None,
                                             f"norm1_bwd")
        grads[l] = dict(w_in=g_in, w_out=g_out, w_up=g_up, w_down=g_down,
                        mod=[dsh1, dsc1, dg1, dsh2, dsc2, dg2], norm1_w=dnw1, conv_a_w=dcw[:CONV_WIDTH], conv_a_b=dcb,
                        ln_a_w=dlnw, ln_a_b=dlnb, lb=jnp.concatenate([dlbf.reshape(1, -1), dlbb.reshape(1, -1)], axis=0),
                        rec_norm_w=dwn.reshape(1, -1), norm2_w=dnw2,
                        conv_f_w=jnp.concatenate([dwg[:3], dwv[:3]], axis=1))
        order_after = on_layer_grads(l, {0: g_in, 1: g_out}, True)
        if l > 0:
            dg2 = dg2_below
    return loss[0, 0], dx, grads, dfw


def _adamw_math(w, g, m, v):
    m = ADAM_B1 * m + (1.0 - ADAM_B1) * g
    v = ADAM_B2 * v + (1.0 - ADAM_B2) * (g * g)
    m_hat = m / (1.0 - ADAM_B1 ** ADAM_STEP)
    v_hat = v / (1.0 - ADAM_B2 ** ADAM_STEP)
    delta = -ADAM_LR * (m_hat / (jnp.sqrt(v_hat) + ADAM_EPS) + ADAM_WD * w)
    return delta, m, v


def _row_tile(rows, cols, max_elems=384 * 1024):
    best = None
    for t in range(8, rows + 1, 8):
        if rows % t == 0 and t * cols <= max_elems:
            best = t
    return best or rows


def _adamw(w, g, m, v, name):
    nl, r, c = w.shape
    tr = _row_tile(r, c)

    def body(w_ref, g_ref, m_ref, v_ref, d_ref, m2_ref, v2_ref):
        d_ref[...], m2_ref[...], v2_ref[...] = _adamw_math(w_ref[...], g_ref[...], m_ref[...], v_ref[...])

    blk = pl.BlockSpec((None, tr, c), lambda l, i: (l, i, 0))
    shape = jax.ShapeDtypeStruct((nl, r, c), F32)
    return pl.pallas_call(body, name=name, grid=(nl, r // tr), in_specs=[blk] * 4, out_specs=(blk, blk, blk),
                          out_shape=(shape, shape, shape), compiler_params=_params(2))(w, g, m, v)


ADA_SHARD = N_MOD * D_MODEL // 4
ADA_COLS = 512
ADA_ROWS = 256
HIGHEST = lax.Precision.HIGHEST


def _ada_mod(c_all, w_ada, b_sh, name):
    def body(c_ref, w_ref, b_ref, o_ref):
        cv = c_ref[...]
        o_ref[...] = jnp.dot(cv * _sigmoid(cv), w_ref[...], precision=HIGHEST, preferred_element_type=F32) + b_ref[...]

    return pl.pallas_call(
        body, name=name, grid=(DEPTH, ADA_SHARD // ADA_COLS),
        in_specs=[pl.BlockSpec((8, D_MODEL), lambda l, j: (0, 0)),
                  pl.BlockSpec((None, D_MODEL, ADA_COLS), lambda l, j: (l, 0, j)),
                  pl.BlockSpec((None, 1, ADA_COLS), lambda l, j: (l, 0, j))],
        out_specs=pl.BlockSpec((None, 8, ADA_COLS), lambda l, j: (l, 0, j)),
        out_shape=jax.ShapeDtypeStruct((DEPTH, 8, ADA_SHARD), F32), compiler_params=_params(2))(c_all, w_ada, b_sh)


def _ada_update(c_all, dmod_sh, w, m, v, name):
    def body(c_ref, d_ref, w_ref, m_ref, v_ref, g_ref, dl_ref, m2_ref, v2_ref):
        cv = c_ref[...]
        g = lax.dot_general(cv * _sigmoid(cv), d_ref[...], (((0,), (0,)), ((), ())), precision=HIGHEST,
                            preferred_element_type=F32)
        g_ref[...] = g
        dl_ref[...], m2_ref[...], v2_ref[...] = _adamw_math(w_ref[...], g, m_ref[...], v_ref[...])

    blk = pl.BlockSpec((None, ADA_ROWS, ADA_SHARD), lambda l, i: (l, i, 0))
    shape = jax.ShapeDtypeStruct((DEPTH, D_MODEL, ADA_SHARD), F32)
    return pl.pallas_call(
        body, name=name, grid=(DEPTH, D_MODEL // ADA_ROWS),
        in_specs=[pl.BlockSpec((8, ADA_ROWS), lambda l, i: (0, i)),
                  pl.BlockSpec((None, 8, ADA_SHARD), lambda l, i: (l, 0, 0)), blk, blk, blk],
        out_specs=(blk,) * 4, out_shape=(shape,) * 4, compiler_params=_params(2))(c_all, dmod_sh, w, m, v)


def _sum_devices(packs, name):
    def body(p_ref, o_ref):
        acc = p_ref[0]
        for dev in range(1, 8):
            acc = acc + p_ref[dev]
        o_ref[...] = acc

    return pl.pallas_call(body, name=name, out_shape=jax.ShapeDtypeStruct(packs.shape[1:], F32))(packs)


def _mesh_pos():
    return lax.axis_index("x"), lax.axis_index("y"), lax.axis_index("c")


def _flip(v, bit):
    return 1 - v if bit else v


def _allgather_devices(x, name):
    m_per, n = x.shape

    def body(x_ref, out_ref, send_sems, recv_sems, local_sem):
        ix, iy, ic = _mesh_pos()
        me, sibling = (ix, iy, ic), (ix, iy, 1 - ic)
        chips = [(1 - ix, iy), (ix, 1 - iy), (1 - ix, 1 - iy)]

        def rows(px, py, pc):
            return out_ref.at[pl.ds((4 * px + 2 * py + pc) * m_per, m_per), :]

        def copy(k, block, to, src=None):
            return pltpu.make_async_remote_copy(
                src_ref=rows(*block) if src is None else src, dst_ref=rows(*block),
                send_sem=send_sems.at[k], recv_sem=recv_sems.at[k], device_id=to, device_id_type=MESH)

        mine = pltpu.make_async_copy(x_ref, rows(*me), local_sem)
        mine.start()
        first = [copy(0, me, sibling, src=x_ref)]
        first += [copy(1 + j, me, (*chip, ic), src=x_ref) for j, chip in enumerate(chips)]
        for cp in first:
            cp.start()
        passed = [copy(4 + j, (*chip, ic), sibling) for j, chip in enumerate(chips)]
        for j, chip in enumerate(chips):
            copy(1 + j, (*chip, ic), me).wait_recv()
            passed[j].start()
        copy(0, sibling, me).wait_recv()
        for j, chip in enumerate(chips):
            copy(4 + j, (*chip, 1 - ic), me).wait_recv()
        for cp in first + passed:
            cp.wait_send()
        mine.wait()

    return pl.pallas_call(
        body, name=name, out_shape=jax.ShapeDtypeStruct((8 * m_per, n), x.dtype),
        in_specs=[pl.BlockSpec(memory_space=pltpu.VMEM)], out_specs=pl.BlockSpec(memory_space=pltpu.VMEM),
        scratch_shapes=[pltpu.SemaphoreType.DMA((7,)), pltpu.SemaphoreType.DMA((7,)), pltpu.SemaphoreType.DMA],
    )(x)


def _gather_chips(shards, name):
    n = len(shards)

    def body(*refs):
        ins, outs = refs[:n], refs[n:2 * n]
        send_sems, recv_sems, local_sems = refs[2 * n:]
        ix, iy, ic = _mesh_pos()
        me = 2 * ix + iy
        local = [pltpu.make_async_copy(ins[a], outs[a].at[me], local_sems.at[a]) for a in range(n)]
        for cp in local:
            cp.start()
        remote = []
        for a in range(n):
            for k in (1, 2, 3):
                px, py = _flip(ix, k & 2), _flip(iy, k & 1)
                sems = dict(send_sem=send_sems.at[3 * a + k - 1], recv_sem=recv_sems.at[3 * a + k - 1],
                            device_id=(px, py, ic), device_id_type=MESH)
                out_cp = pltpu.make_async_remote_copy(src_ref=ins[a], dst_ref=outs[a].at[me], **sems)
                in_cp = pltpu.make_async_remote_copy(src_ref=ins[a], dst_ref=outs[a].at[2 * px + py], **sems)
                out_cp.start()
                remote.append((out_cp, in_cp))
        for out_cp, in_cp in remote:
            out_cp.wait_send()
            in_cp.wait_recv()
        for cp in local:
            cp.wait()

    return pl.pallas_call(
        body, name=name, in_specs=[ANY] * n, out_specs=tuple([ANY] * n),
        out_shape=tuple(jax.ShapeDtypeStruct((4,) + t.shape, t.dtype) for t in shards),
        scratch_shapes=[pltpu.SemaphoreType.DMA((3 * n,)), pltpu.SemaphoreType.DMA((3 * n,)),
                        pltpu.SemaphoreType.DMA((n,))],
    )(*shards)


HBM = pl.BlockSpec(memory_space=pltpu.HBM)
SEM = pl.BlockSpec(memory_space=pltpu.SEMAPHORE)
DATAFLOW = pltpu.SideEffectType.DATAFLOW_SIDE_EFFECTING


def _peer_chip(ix, iy, k):
    return _flip(ix, k & 2), _flip(iy, k & 1)


def _gather_chips_start(shards, name):
    n = len(shards)

    def body(*refs):
        src, land = refs[:n], refs[n:2 * n]
        send_sems, recv_sems = refs[2 * n], refs[2 * n + 1]
        token = refs[-1]
        ix, iy, ic = _mesh_pos()
        me = 2 * ix + iy
        for a in range(n):
            for k in (1, 2, 3):
                px, py = _peer_chip(ix, iy, k)
                pltpu.make_async_remote_copy(
                    src_ref=src[a], dst_ref=land[a].at[me], send_sem=send_sems.at[3 * a + k - 1],
                    recv_sem=recv_sems.at[3 * a + k - 1], device_id=(px, py, ic), device_id_type=MESH).start()
        token[...] = jnp.zeros_like(token)

    hbm = lambda shape, dtype: pltpu.HBM(shape, dtype)
    operands = ([pltpu.with_memory_space_constraint(t, pltpu.HBM) for t in shards]
                + [pltpu.with_memory_space_constraint(lax.empty((4,) + t.shape, t.dtype), pltpu.HBM) for t in shards])
    return pl.pallas_call(
        body, name=name,
        out_shape=(pltpu.SemaphoreType.DMA((3 * n,)), pltpu.SemaphoreType.DMA((3 * n,)),
                   *[hbm(t.shape, t.dtype) for t in shards], *[hbm((4,) + t.shape, t.dtype) for t in shards],
                   jax.ShapeDtypeStruct((8, LANES), F32)),
        in_specs=(HBM,) * (2 * n),
        out_specs=(SEM, SEM) + (HBM,) * (2 * n) + (pl.BlockSpec(memory_space=pltpu.VMEM),),
        input_output_aliases={a: 2 + a for a in range(2 * n)},
        compiler_params=pltpu.CompilerParams(has_side_effects=DATAFLOW),
    )(*operands)


def _gather_chips_wait(started, which, after, name):
    send_sems, recv_sems = started[0], started[1]
    thru = started[2:-1]
    n = len(thru) // 2

    def body(*refs):
        src, land = refs[:n], refs[n:2 * n]
        send_sems, recv_sems = refs[2 * n], refs[2 * n + 1]
        ix, iy, ic = _mesh_pos()
        for a in which:
            for k in (1, 2, 3):
                px, py = _peer_chip(ix, iy, k)
                cp = pltpu.make_async_remote_copy(
                    src_ref=src[a], dst_ref=land[a].at[2 * px + py], send_sem=send_sems.at[3 * a + k - 1],
                    recv_sem=recv_sems.at[3 * a + k - 1], device_id=(px, py, ic), device_id_type=MESH)
                cp.wait_send()
                cp.wait_recv()

    outs = pl.pallas_call(
        body, name=name,
        out_shape=tuple(pltpu.HBM(t.shape, t.dtype) for t in thru),
        in_specs=(HBM,) * (2 * n) + (SEM, SEM, ANY), out_specs=(HBM,) * (2 * n),
        input_output_aliases={a: a for a in range(2 * n)},
        compiler_params=pltpu.CompilerParams(has_side_effects=DATAFLOW),
    )(*thru, send_sems, recv_sems, after)
    return outs[:n], outs[n:]


BIG_KINDS = (("w_in", "col", D_MODEL, IN_COLS), ("w_out", "row", D_MODEL, D_MODEL),
             ("w_up", "col", D_MODEL, 2 * D_FF), ("w_down", "row", D_FF, D_MODEL))


def _piece_shape(how, r, c):
    return (r // 2, c // 4) if how == "col" else (r // 8, c)


def _aligned(start, multiple):
    return start if isinstance(start, int) else pl.multiple_of(start, multiple)


def _piece(ref, how, r, c, chip, half):
    if how == "col":
        return ref.at[pl.ds(_aligned(half * (r // 2), 8), r // 2), pl.ds(_aligned(chip * (c // 4), LANES), c // 4)]
    n = r // 4
    return ref.at[pl.ds(_aligned(chip * n + half * (n // 2), 8), n // 2), :]


def _rs_pair_exchange(grads, kinds, name):
    nk = len(kinds)
    specs = [BIG_KINDS[ki] for ki in kinds]
    nl = len(grads[0])
    flat = [grads[ki][l] for ki in range(nk) for l in range(nl)]
    per = nl * 4

    def body(*refs):
        g, land = refs[:nk * nl], refs[nk * nl:nk * nl + nk]
        send_sems, recv_sems = refs[nk * nl + nk:]
        ix, iy, ic = _mesh_pos()
        sibling = (ix, iy, 1 - ic)
        copies = []
        for ki, (_, how, r, c) in enumerate(specs):
            for l in range(nl):
                for j in range(4):
                    sem = ki * per + l * 4 + j
                    rem = pltpu.make_async_remote_copy(
                        src_ref=_piece(g[ki * nl + l], how, r, c, j, 1 - ic), dst_ref=land[ki].at[l, j],
                        send_sem=send_sems.at[sem], recv_sem=recv_sems.at[sem], device_id=sibling, device_id_type=MESH)
                    rem.start()
                    copies.append(rem)
        for rem in copies:
            rem.wait_send()
            rem.wait_recv()

    shapes = [jax.ShapeDtypeStruct((nl, 4) + _piece_shape(how, r, c), F32) for _, how, r, c in specs]
    return pl.pallas_call(
        body, name=name, in_specs=[ANY] * len(flat), out_specs=tuple([ANY] * nk), out_shape=tuple(shapes),
        scratch_shapes=[pltpu.SemaphoreType.DMA((nk * per,))] * 2,
    )(*flat)


def _pair_sum(g, theirs, layer, how, core, name):
    r, c = g.shape
    pr, pc = _piece_shape(how, r, c)
    if how == "col":
        mine_spec = pl.BlockSpec((pr, pc), lambda j, core_ref: (core_ref[0], j))
    else:
        mine_spec = pl.BlockSpec((pr, pc), lambda j, core_ref: (2 * j + core_ref[0], 0))

    def body(core_ref, g_ref, t_ref, o_ref, ob_ref):
        total = g_ref[...] + t_ref[...]
        o_ref[...] = total
        ob_ref[...] = total.astype(BF16)

    out_blk = pl.BlockSpec((None, pr, pc), lambda j, core_ref: (j, 0, 0))
    return pl.pallas_call(
        body, name=name,
        grid_spec=pltpu.PrefetchScalarGridSpec(
            num_scalar_prefetch=1, grid=(4,),
            in_specs=[mine_spec, pl.BlockSpec((None, None, pr, pc), lambda j, core_ref: (layer, j, 0, 0))],
            out_specs=(out_blk, out_blk)),
        out_shape=(jax.ShapeDtypeStruct((4, pr, pc), F32), jax.ShapeDtypeStruct((4, pr, pc), BF16)),
        compiler_params=_params(1))(core, g, theirs)


def _rs_chip_exchange(pair_sums, name):
    nk = len(pair_sums)
    nl = len(pair_sums[0])
    flat = [pair_sums[ki][l] for ki in range(nk) for l in range(nl)]

    def body(*refs):
        src, dst = refs[:nk * nl], refs[nk * nl:nk * nl + nk]
        send_sems, recv_sems = refs[nk * nl + nk:]
        ix, iy, ic = _mesh_pos()
        copies = []
        for ki in range(nk):
            for l in range(nl):
                for k in (1, 2, 3):
                    px, py = _peer_chip(ix, iy, k)
                    sem = (ki * nl + l) * 3 + k - 1
                    rem = pltpu.make_async_remote_copy(
                        src_ref=src[ki * nl + l].at[2 * px + py], dst_ref=dst[ki].at[l, k - 1],
                        send_sem=send_sems.at[sem], recv_sem=recv_sems.at[sem], device_id=(px, py, ic), device_id_type=MESH)
                    rem.start()
                    copies.append(rem)
        for rem in copies:
            rem.wait_send()
            rem.wait_recv()

    return pl.pallas_call(
        body, name=name, in_specs=[ANY] * len(flat), out_specs=tuple([ANY] * nk),
        out_shape=tuple(jax.ShapeDtypeStruct((nl, 3) + pair_sums[ki][0].shape[1:], pair_sums[ki][0].dtype)
                        for ki in range(nk)),
        scratch_shapes=[pltpu.SemaphoreType.DMA((nk * nl * 3,))] * 2,
    )(*flat)


def _rs_chip_start(pieces, name):
    n = len(pieces)

    def body(*refs):
        src, land = refs[:n], refs[n:2 * n]
        send_sems, recv_sems = refs[2 * n], refs[2 * n + 1]
        token = refs[-1]
        ix, iy, ic = _mesh_pos()
        for a in range(n):
            for k in (1, 2, 3):
                px, py = _peer_chip(ix, iy, k)
                pltpu.make_async_remote_copy(
                    src_ref=src[a].at[2 * px + py], dst_ref=land[a].at[k - 1], send_sem=send_sems.at[3 * a + k - 1],
                    recv_sem=recv_sems.at[3 * a + k - 1], device_id=(px, py, ic), device_id_type=MESH).start()
        token[...] = jnp.zeros_like(token)

    land_shape = lambda t: (3,) + t.shape[1:]
    operands = ([pltpu.with_memory_space_constraint(t, pltpu.HBM) for t in pieces]
                + [pltpu.with_memory_space_constraint(lax.empty(land_shape(t), t.dtype), pltpu.HBM) for t in pieces])
    return pl.pallas_call(
        body, name=name,
        out_shape=(pltpu.SemaphoreType.DMA((3 * n,)), pltpu.SemaphoreType.DMA((3 * n,)),
                   *[pltpu.HBM(t.shape, t.dtype) for t in pieces], *[pltpu.HBM(land_shape(t), t.dtype) for t in pieces],
                   jax.ShapeDtypeStruct((8, LANES), F32)),
        in_specs=(HBM,) * (2 * n),
        out_specs=(SEM, SEM) + (HBM,) * (2 * n) + (pl.BlockSpec(memory_space=pltpu.VMEM),),
        input_output_aliases={a: 2 + a for a in range(2 * n)},
        compiler_params=pltpu.CompilerParams(has_side_effects=DATAFLOW),
    )(*operands)


def _rs_chip_wait(started, after, name):
    send_sems, recv_sems = started[0], started[1]
    thru = started[2:-1]
    n = len(thru) // 2

    def body(*refs):
        src, land = refs[:n], refs[n:2 * n]
        send_sems, recv_sems = refs[2 * n], refs[2 * n + 1]
        ix, iy, ic = _mesh_pos()
        for a in range(n):
            for k in (1, 2, 3):
                px, py = _peer_chip(ix, iy, k)
                cp = pltpu.make_async_remote_copy(
                    src_ref=src[a].at[2 * px + py], dst_ref=land[a].at[k - 1], send_sem=send_sems.at[3 * a + k - 1],
                    recv_sem=recv_sems.at[3 * a + k - 1], device_id=(px, py, ic), device_id_type=MESH)
                cp.wait_send()
                cp.wait_recv()

    outs = pl.pallas_call(
        body, name=name,
        out_shape=tuple(pltpu.HBM(t.shape, t.dtype) for t in thru),
        in_specs=(HBM,) * (2 * n) + (SEM, SEM, ANY), out_specs=(HBM,) * (2 * n),
        input_output_aliases={a: a for a in range(2 * n)},
        compiler_params=pltpu.CompilerParams(has_side_effects=DATAFLOW),
    )(*thru, send_sems, recv_sems, after)
    return outs[n:]


def _chip_sum(own, others, chip, name):
    _, pr, pc = own.shape

    def body(chip_ref, own_ref, s1, s2, s3, o_ref):
        o_ref[...] = ((own_ref[...] + s1[...].astype(F32)) + s2[...].astype(F32)) + s3[...].astype(F32)

    slot = lambda k: pl.BlockSpec((None, pr, pc), lambda i, chip_ref: (k, 0, 0))
    return pl.pallas_call(
        body, name=name,
        grid_spec=pltpu.PrefetchScalarGridSpec(
            num_scalar_prefetch=1, grid=(1,),
            in_specs=[pl.BlockSpec((None, pr, pc), lambda i, chip_ref: (chip_ref[0], 0, 0)), slot(0), slot(1), slot(2)],
            out_specs=pl.BlockSpec((pr, pc), lambda i, chip_ref: (0, 0))),
        out_shape=jax.ShapeDtypeStruct((pr, pc), F32), compiler_params=_params(1))(chip, own, others, others, others)


def _rs_pair_share(halves, name):
    nk = len(halves)
    flat = [halves[ki][l] for ki in range(nk) for l in range(DEPTH)]

    def body(*refs):
        src, dst = refs[:nk * DEPTH], refs[nk * DEPTH:nk * DEPTH + nk]
        send_sems, recv_sems = refs[nk * DEPTH + nk:]
        ix, iy, ic = _mesh_pos()
        copies = []
        for ki in range(nk):
            for l in range(DEPTH):
                sem = ki * DEPTH + l
                rem = pltpu.make_async_remote_copy(
                    src_ref=src[sem], dst_ref=dst[ki].at[l], send_sem=send_sems.at[sem], recv_sem=recv_sems.at[sem],
                    device_id=(ix, iy, 1 - ic), device_id_type=MESH)
                rem.start()
                copies.append(rem)
        for rem in copies:
            rem.wait_send()
            rem.wait_recv()

    return pl.pallas_call(
        body, name=name, in_specs=[ANY] * len(flat), out_specs=tuple([ANY] * nk),
        out_shape=tuple(jax.ShapeDtypeStruct((DEPTH,) + halves[ki][0].shape, F32) for ki in range(nk)),
        scratch_shapes=[pltpu.SemaphoreType.DMA((nk * DEPTH,))] * 2,
    )(*flat)


def _adamw_halves(w, mine, theirs, m, v, core, name):
    nl, pr, pc = theirs.shape
    shape = w.shape
    view = lambda t: t.reshape(nl, 2, pr, pc)
    tr = _row_tile(pr, pc, 256 * 1024)

    def body(core_ref, w_ref, a0_ref, a1_ref, t_ref, m_ref, v_ref, g_ref, d_ref, m2_ref, v2_ref):
        own = jnp.where(pl.program_id(0) == 0, a0_ref[...], a1_ref[...])
        g = jnp.where(pl.program_id(1) == core_ref[0], own, t_ref[...])
        g_ref[...] = g
        d_ref[...], m2_ref[...], v2_ref[...] = _adamw_math(w_ref[...], g, m_ref[...], v_ref[...])

    blk = pl.BlockSpec((None, None, tr, pc), lambda l, h, i, core_ref: (l, h, i, 0))
    own_blk = pl.BlockSpec((tr, pc), lambda l, h, i, core_ref: (i, 0))
    out = jax.ShapeDtypeStruct((nl, 2, pr, pc), F32)
    outs = pl.pallas_call(
        body, name=name,
        grid_spec=pltpu.PrefetchScalarGridSpec(
            num_scalar_prefetch=1, grid=(nl, 2, pr // tr),
            in_specs=[blk, own_blk, own_blk, pl.BlockSpec((None, tr, pc), lambda l, h, i, core_ref: (l, i, 0)), blk, blk],
            out_specs=(blk,) * 4),
        out_shape=(out,) * 4, compiler_params=_params(3),
    )(core, view(w), mine[0], mine[1], theirs, view(m), view(v))
    return tuple(t.reshape(shape) for t in outs)


def _rs_first_stages(layer_grads, kinds, core, tag, in_flight):
    theirs = _rs_pair_exchange([[g] for g in layer_grads], kinds, f"rs_pair_exchange_{tag}")
    sums = [_pair_sum(g, theirs[n], 0, BIG_KINDS[ki][1], core, f"rs_pair_sum_{BIG_KINDS[ki][0]}")
            for n, (ki, g) in enumerate(zip(kinds, layer_grads))]
    to_send = [both[1] for both in sums]
    if in_flight:
        return [both[0] for both in sums], _rs_chip_start(to_send, f"rs_chip_start_{tag}")
    slots = _rs_chip_exchange([[t] for t in to_send], f"rs_chip_exchange_{tag}")
    return [both[0] for both in sums], [t[0] for t in slots]


def _rs_last_stages(pair_sums, slots, chip):
    halves = [[_chip_sum(pair_sums[ki][l], slots[ki][l], chip, f"rs_chip_sum_{kind}") for l in range(DEPTH)]
              for ki, (kind, _, _, _) in enumerate(BIG_KINDS)]
    other = _rs_pair_share(halves, "rs_pair_share")
    return list(zip(halves, other))


WEIGHT_NAMES = ("w_ada", "b_ada", "norm1_w", "w_in", "conv_a_w", "conv_a_b", "ln_a_w", "ln_a_b", "lb_gamma",
                "rec_norm_w", "w_out", "norm2_w", "w_up", "conv_f_w", "w_down", "final_norm_w")
SMALL_PARAMS = (("b_ada", (DEPTH, N_MOD * D_MODEL), None), ("norm1_w", (DEPTH, D_MODEL), None),
                ("conv_a_w", (DEPTH, CONV_WIDTH, CONV_CH), 2), ("conv_a_b", (DEPTH, CONV_CH), None),
                ("ln_a_w", (DEPTH, CONV_CH), None), ("ln_a_b", (DEPTH, CONV_CH), None),
                ("lb_gamma", (DEPTH, 2, REC_WIDTH), 2), ("rec_norm_w", (DEPTH, REC_WIDTH), None),
                ("norm2_w", (DEPTH, D_MODEL), None), ("conv_f_w", (DEPTH, 3, 2 * D_FF), 2),
                ("final_norm_w", (D_MODEL,), None))


def _pack_rows(parts):
    flat = jnp.concatenate([p.reshape(-1) for p in parts])
    total = flat.shape[0]
    padded = -(-total // (8 * LANES)) * (8 * LANES)
    return jnp.pad(flat, (0, padded - total)).reshape(padded // LANES, LANES)


def _unpack(flat, shapes):
    out, off = [], 0
    for shp in shapes:
        size = int(np.prod(shp))
        out.append(flat[off:off + size].reshape(shp))
        off += size
    return out


def _unstack_chips(t, axis):
    return jnp.concatenate([t[j] for j in range(4)], axis=axis)


def kernel(x, c, w_ada, b_ada, norm1_w, w_in, conv_a_w, conv_a_b, ln_a_w, ln_a_b, lb_gamma, rec_norm_w, w_out, norm2_w, w_up, conv_f_w, w_down, final_norm_w, loss_target, m_w_ada, m_b_ada, m_norm1_w, m_w_in, m_conv_a_w, m_conv_a_b, m_ln_a_w, m_ln_a_b, m_lb_gamma, m_rec_norm_w, m_w_out, m_norm2_w, m_w_up, m_conv_f_w, m_w_down, m_final_norm_w, v_w_ada, v_b_ada, v_norm1_w, v_w_in, v_conv_a_w, v_conv_a_b, v_ln_a_w, v_ln_a_b, v_lb_gamma, v_rec_norm_w, v_w_out, v_norm2_w, v_w_up, v_conv_f_w, v_w_down, v_final_norm_w):
    params = dict(zip(WEIGHT_NAMES, (w_ada, b_ada, norm1_w, w_in, conv_a_w, conv_a_b, ln_a_w, ln_a_b, lb_gamma,
                                     rec_norm_w, w_out, norm2_w, w_up, conv_f_w, w_down, final_norm_w)))
    mom1 = dict(zip(WEIGHT_NAMES, (m_w_ada, m_b_ada, m_norm1_w, m_w_in, m_conv_a_w, m_conv_a_b, m_ln_a_w, m_ln_a_b,
                                   m_lb_gamma, m_rec_norm_w, m_w_out, m_norm2_w, m_w_up, m_conv_f_w, m_w_down,
                                   m_final_norm_w)))
    mom2 = dict(zip(WEIGHT_NAMES, (v_w_ada, v_b_ada, v_norm1_w, v_w_in, v_conv_a_w, v_conv_a_b, v_ln_a_w, v_ln_a_b,
                                   v_lb_gamma, v_rec_norm_w, v_w_out, v_norm2_w, v_w_up, v_conv_f_w, v_w_down,
                                   v_final_norm_w)))
    ix, iy, ic = _mesh_pos()
    chip = 2 * ix + iy
    dev = 2 * chip + ic

    w_in_b, w_out_b, w_up_b, w_down_b = (t.astype(BF16) for t in (w_in, w_out, w_up, w_down))
    weights_flying = _gather_chips_start([w_in_b[0], w_out_b, w_up_b, w_down_b, w_in_b[1]], "gather_weights_start")

    c = c + weights_flying[-1][0:1, 0:1]
    c_all = _allgather_devices(c.reshape(8, LANES), "gather_cond").reshape(8, D_MODEL)
    b_sh = lax.dynamic_slice_in_dim(b_ada, chip * ADA_SHARD, ADA_SHARD, axis=1)
    mod_sh = _ada_mod(c_all, w_ada, b_sh.reshape(DEPTH, 1, ADA_SHARD), "ada_mod")
    first = _gather_chips([mod_sh, conv_a_w, conv_f_w, lb_gamma], "gather_first")
    mod_mine = lax.dynamic_index_in_dim(first[0], dev, axis=2, keepdims=False)
    mods = [jnp.concatenate([mod_mine[j, l] for j in range(4)]).reshape(N_MOD, D_MODEL) for l in range(DEPTH)]
    conv_a_w_f, conv_f_w_f, gamma_f = (_unstack_chips(first[k], 2) for k in (1, 2, 3))

    def whole(own, lands, n, axis):
        return jnp.concatenate([jnp.where(chip == j, own[n], lands[n][j]) for j in range(4)], axis=axis)

    own, lands = _gather_chips_wait(weights_flying, [0], mods[0], "gather_w_in0_wait")
    w_in0 = whole(own, lands, 0, 1)
    still_flying = (weights_flying[0], weights_flying[1], *own, *lands, None)

    def later_weights(after):
        own, lands = _gather_chips_wait(still_flying, [1, 2, 3, 4], after, "gather_rest_wait")
        return whole(own, lands, 4, 1), whole(own, lands, 1, 1), whole(own, lands, 2, 2), whole(own, lands, 3, 1)

    lb1, p_soft = _lower_bounds(gamma_f.reshape(DEPTH, 2 * REC_WIDTH), "lower_bounds")
    lbs = [jnp.zeros((2, REC_WIDTH), F32), lb1.reshape(2, REC_WIDTH)]
    small = []
    for l in range(DEPTH):
        small.append(dict(norm1_w=norm1_w[l][None], conv_a_w=conv_a_w_f[l], conv_a_b=conv_a_b[l][None],
                          ln_a_w=ln_a_w[l][None], ln_a_b=ln_a_b[l][None], rec_norm_w=rec_norm_w[l],
                          norm2_w=norm2_w[l][None], conv_f_w=conv_f_w_f[l]))

    core_id, chip_id = ic.astype(jnp.int32).reshape(1), chip.astype(jnp.int32).reshape(1)
    pending, groups = {}, []

    def on_layer_grads(l, by_kind, last):
        if l > 0:
            pending.update(by_kind)
            if not last:
                return None
            by_kind = dict(pending)
        kinds = sorted(by_kind)
        in_flight = not (l == 0 and last)
        tag = f"l{l}" if l > 0 else f"l{l}_{'mix' if last else 'ffn'}"
        sums, exchange = _rs_first_stages([by_kind[k] for k in kinds], kinds, core_id, tag, in_flight)
        groups.append((l, kinds, sums, exchange, in_flight, tag))
        return exchange[-1][0:1, 0:1] if in_flight else None

    loss, dx, grads, dfw = _sequence_step(x[0], loss_target[0], mods, lbs, small, w_in0, later_weights,
                                          final_norm_w[None], on_layer_grads)
    loss = lax.psum(loss, ("x", "y", "c"))
    pair_sums = [[None] * DEPTH for _ in BIG_KINDS]
    slots = [[None] * DEPTH for _ in BIG_KINDS]
    for l, kinds, sums, exchange, in_flight, tag in groups:
        received = _rs_chip_wait(exchange, dx, f"rs_chip_wait_{tag}") if in_flight else exchange
        for n, ki in enumerate(kinds):
            pair_sums[ki][l], slots[ki][l] = sums[n], received[n]

    dgamma = _lower_bounds_bwd(grads[1]["lb"].reshape(1, 2 * REC_WIDTH), p_soft, "lower_bounds_bwd")
    dmod = [jnp.concatenate(grads[l]["mod"], axis=1) for l in range(DEPTH)]
    stack = lambda key: jnp.stack([grads[l][key] for l in range(DEPTH)])
    local_small = dict(b_ada=jnp.concatenate(dmod, axis=0), norm1_w=stack("norm1_w"), conv_a_w=stack("conv_a_w"),
                       conv_a_b=stack("conv_a_b"), ln_a_w=stack("ln_a_w"), ln_a_b=stack("ln_a_b"), lb_gamma=dgamma,
                       rec_norm_w=stack("rec_norm_w"), norm2_w=stack("norm2_w"), conv_f_w=stack("conv_f_w"),
                       final_norm_w=dfw)
    pack = _pack_rows([local_small[name] for name, _, _ in SMALL_PARAMS])
    rows = pack.shape[0]
    packs = _allgather_devices(pack, "gather_small_grads").reshape(8, rows, LANES)
    summed = _sum_devices(packs, "sum_small_grads").reshape(-1)
    small_grads = dict(zip([n for n, _, _ in SMALL_PARAMS], _unpack(summed, [shp for _, shp, _ in SMALL_PARAMS])))

    dmod_all = packs.reshape(8, rows * LANES)[:, :DEPTH * N_MOD * D_MODEL].reshape(8, DEPTH, N_MOD * D_MODEL)
    dmod_sh = lax.dynamic_slice_in_dim(dmod_all, chip * ADA_SHARD, ADA_SHARD, axis=2).transpose(1, 0, 2)
    g_ada, d_ada, m_ada, v_ada = _ada_update(c_all, dmod_sh, w_ada, m_w_ada, v_w_ada, "ada_update")

    for name, shp, axis in SMALL_PARAMS:
        if axis is not None:
            width = shp[axis] // 4
            small_grads[name] = lax.dynamic_slice_in_dim(small_grads[name], chip * width, width, axis=axis)
    names = [n for n, _, _ in SMALL_PARAMS]
    packed = [_pack_rows([src[n] for n in names])[None] for src in (params, small_grads, mom1, mom2)]
    small_out = _adamw(*packed, "adamw_small")
    shapes = [params[n].shape for n in names]
    small_delta, small_m, small_v = (dict(zip(names, _unpack(t.reshape(-1), shapes))) for t in small_out)

    summed_big = _rs_last_stages(pair_sums, slots, chip_id)
    grad, delta, new_m, new_v = dict(small_grads), small_delta, small_m, small_v
    grad["w_ada"], delta["w_ada"], new_m["w_ada"], new_v["w_ada"] = g_ada, d_ada, m_ada, v_ada
    for (name, _, _, _), (mine, theirs) in zip(BIG_KINDS, summed_big):
        grad[name], delta[name], new_m[name], new_v[name] = _adamw_halves(
            params[name], mine, theirs, mom1[name], mom2[name], core_id, f"adamw_{name}")

    return (loss, dx[None], *[grad[n] for n in WEIGHT_NAMES], *[delta[n] for n in WEIGHT_NAMES],
            *[new_m[n] for n in WEIGHT_NAMES], *[new_v[n] for n in WEIGHT_NAMES])
```

```python
import numpy as np
import jax
import jax.numpy as jnp
from jax import lax
from jax.experimental import pallas as pl
from jax.experimental.pallas import tpu as pltpu

F32 = jnp.float32
BF16 = jnp.bfloat16

D_MODEL = 1024
DEPTH = 2
HEAD_DIM = 64
CONV_CH = 256
CONV_WIDTH = 31
ATT_WIDTH = 384
N_HEADS = 6
DILATIONS = (1, 4, 16)
ATT_HALF = 64
ATT_BLOCK = 128
ALIBI_MAX_EXP = 8.0
MASK_VALUE = -1e30
REC_WIDTH = 384
REC_CHUNK = 64
F_TINY = 1e-30
D_FF = 2816
N_MOD = 6
EPS = 1e-6
G_CONV = (0, 512)
G_QKV = (512, 1664)
G_REC = (1664, 3584)
IN_COLS = 3584

ADAM_LR = 0.001
ADAM_B1 = 0.9
ADAM_B2 = 0.999
ADAM_EPS = 1e-08
ADAM_WD = 0.01
ADAM_STEP = 10

VMEM_LIMIT_BYTES = 56 * 1024 * 1024
LANES = 128
MESH = pl.DeviceIdType.MESH
ANY = pl.BlockSpec(memory_space=pl.ANY)


def _params(n_axes):
    return pltpu.CompilerParams(dimension_semantics=("arbitrary",) * n_axes,
                                vmem_limit_bytes=VMEM_LIMIT_BYTES)


def _tile(n, target):
    best = None
    for t in range(LANES, min(n, target) + 1, LANES):
        if n % t == 0:
            best = t
    return best or n


def _sigmoid(x):
    return jax.nn.sigmoid(x)


def _silu_grad(x):
    s = _sigmoid(x)
    return s * (1.0 + x * (1.0 - s))


MM_ACC_ELEMS = 1536 * 1024


def _matmul(a, b, mode, out_dtype, name, tm=1024, tn=1792, tk=1792):
    if mode == "nn":
        (m, k), (k2, n) = a.shape, b.shape
    elif mode == "nt":
        (m, k), (n, k2) = a.shape, b.shape
    else:
        (k, m), (k2, n) = a.shape, b.shape
    assert k == k2, (a.shape, b.shape, mode)
    tn, tk = _tile(n, tn), _tile(k, tk)
    tm = _tile(m, min(tm, MM_ACC_ELEMS // tn))
    nk = k // tk
    a_spec = (pl.BlockSpec((tk, tm), lambda i, j, kk: (kk, i)) if mode == "tn"
              else pl.BlockSpec((tm, tk), lambda i, j, kk: (i, kk)))
    b_spec = (pl.BlockSpec((tn, tk), lambda i, j, kk: (j, kk)) if mode == "nt"
              else pl.BlockSpec((tk, tn), lambda i, j, kk: (kk, j)))
    dims = {"nn": (((1,), (0,)), ((), ())), "nt": (((1,), (1,)), ((), ())),
            "tn": (((0,), (0,)), ((), ()))}[mode]

    def body(a_ref, b_ref, o_ref, *scratch):
        part = lax.dot_general(a_ref[...].astype(BF16), b_ref[...].astype(BF16), dims, preferred_element_type=F32)
        if nk == 1:
            o_ref[...] = part.astype(out_dtype)
            return
        acc_ref, = scratch
        kk = pl.program_id(2)

        @pl.when(kk == 0)
        def _():
            acc_ref[...] = part

        @pl.when(kk > 0)
        def _():
            acc_ref[...] += part

        @pl.when(kk == nk - 1)
        def _():
            o_ref[...] = acc_ref[...].astype(out_dtype)

    return pl.pallas_call(
        body, name=name, grid=(m // tm, n // tn, nk),
        in_specs=[a_spec, b_spec],
        out_specs=pl.BlockSpec((tm, tn), lambda i, j, kk: (i, j)),
        out_shape=jax.ShapeDtypeStruct((m, n), out_dtype),
        scratch_shapes=[pltpu.VMEM((tm, tn), F32)] if nk > 1 else [],
        compiler_params=pltpu.CompilerParams(dimension_semantics=("parallel", "parallel", "arbitrary"),
                                             vmem_limit_bytes=VMEM_LIMIT_BYTES),
    )(a, b)


def _matmul_two_lhs(a1, a2, b, out_dtype, name):
    (m, k1), n = a1.shape, b.shape[0]
    tn, tk = _tile(n, 1792), _tile(k1, 1792)
    tm = _tile(m, min(1024, MM_ACC_ELEMS // tn))
    nk1 = k1 // tk
    nk = 2 * nk1

    def body(a1_ref, a2_ref, b_ref, o_ref, acc_ref):
        kk = pl.program_id(2)
        lhs = jnp.where(kk < nk1, a1_ref[...], a2_ref[...])
        part = lax.dot_general(lhs, b_ref[...], (((1,), (1,)), ((), ())), preferred_element_type=F32)

        @pl.when(kk == 0)
        def _():
            acc_ref[...] = part

        @pl.when(kk > 0)
        def _():
            acc_ref[...] += part

        @pl.when(kk == nk - 1)
        def _():
            o_ref[...] = acc_ref[...].astype(out_dtype)

    return pl.pallas_call(
        body, name=name, grid=(m // tm, n // tn, nk),
        in_specs=[pl.BlockSpec((tm, tk), lambda i, j, kk: (i, jnp.minimum(kk, nk1 - 1))),
                  pl.BlockSpec((tm, tk), lambda i, j, kk: (i, jnp.maximum(kk - nk1, 0))),
                  pl.BlockSpec((tn, tk), lambda i, j, kk: (j, kk))],
        out_specs=pl.BlockSpec((tm, tn), lambda i, j, kk: (i, j)),
        out_shape=jax.ShapeDtypeStruct((m, n), out_dtype),
        scratch_shapes=[pltpu.VMEM((tm, tn), F32)],
        compiler_params=pltpu.CompilerParams(dimension_semantics=("parallel", "parallel", "arbitrary"),
                                             vmem_limit_bytes=VMEM_LIMIT_BYTES),
    )(a1, a2, b)


def _matmul_two_rhs(a, b1, b2, out_dtype, name):
    (k, m), n1 = a.shape, b1.shape[1]
    tn, tk = _tile(n1, 1792), _tile(k, 1792)
    tm = _tile(m, min(1024, MM_ACC_ELEMS // tn))
    nj1, nk = n1 // tn, k // tk

    def body(a_ref, b1_ref, b2_ref, o_ref, acc_ref):
        j, kk = pl.program_id(1), pl.program_id(2)
        rhs = jnp.where(j < nj1, b1_ref[...], b2_ref[...])
        part = lax.dot_general(a_ref[...], rhs, (((0,), (0,)), ((), ())), preferred_element_type=F32)

        @pl.when(kk == 0)
        def _():
            acc_ref[...] = part

        @pl.when(kk > 0)
        def _():
            acc_ref[...] += part

        @pl.when(kk == nk - 1)
        def _():
            o_ref[...] = acc_ref[...].astype(out_dtype)

    return pl.pallas_call(
        body, name=name, grid=(m // tm, 2 * nj1, nk),
        in_specs=[pl.BlockSpec((tk, tm), lambda i, j, kk: (kk, i)),
                  pl.BlockSpec((tk, tn), lambda i, j, kk: (jnp.where(j < nj1, kk, 0), jnp.minimum(j, nj1 - 1))),
                  pl.BlockSpec((tk, tn), lambda i, j, kk: (jnp.where(j < nj1, 0, kk), jnp.maximum(j - nj1, 0)))],
        out_specs=pl.BlockSpec((tm, tn), lambda i, j, kk: (i, j)),
        out_shape=jax.ShapeDtypeStruct((m, 2 * n1), out_dtype),
        scratch_shapes=[pltpu.VMEM((tm, tn), F32)],
        compiler_params=pltpu.CompilerParams(dimension_semantics=("parallel", "parallel", "arbitrary"),
                                             vmem_limit_bytes=VMEM_LIMIT_BYTES),
    )(a, b1, b2)


NORM_ROWS = 256


def _row_spec(width, rows=NORM_ROWS):
    return pl.BlockSpec((rows, width), lambda i: (i, 0))


def _vec_spec(width):
    return pl.BlockSpec((1, width), lambda i: (0, 0))


def _resid_norm_mod(x, r, g, nw, sc, sh, name):
    s, d = x.shape
    has_r = r is not None

    def body(*refs):
        if has_r:
            x_ref, r_ref, g_ref, nw_ref, sc_ref, sh_ref, xn_ref, h_ref = refs
            xn = x_ref[...] + g_ref[...] * r_ref[...].astype(F32)
            xn_ref[...] = xn
        else:
            x_ref, nw_ref, sc_ref, sh_ref, h_ref = refs
            xn = x_ref[...]
        rstd = lax.rsqrt(jnp.mean(xn * xn, axis=-1, keepdims=True) + EPS)
        y = xn * rstd * nw_ref[...]
        h_ref[...] = (y * (1.0 + sc_ref[...]) + sh_ref[...]).astype(BF16)

    if has_r:
        ins, in_specs = (x, r, g, nw, sc, sh), [_row_spec(d), _row_spec(d)] + [_vec_spec(d)] * 4
        out_shape = (jax.ShapeDtypeStruct((s, d), F32), jax.ShapeDtypeStruct((s, d), BF16))
        out_specs = (_row_spec(d), _row_spec(d))
    else:
        ins, in_specs = (x, nw, sc, sh), [_row_spec(d)] + [_vec_spec(d)] * 3
        out_shape = jax.ShapeDtypeStruct((s, d), BF16)
        out_specs = _row_spec(d)
    return pl.pallas_call(body, name=name, grid=(s // NORM_ROWS,), in_specs=in_specs, out_specs=out_specs,
                          out_shape=out_shape, compiler_params=_params(1))(*ins)


def _final_loss(x, r, g, fw, tgt, name):
    s, d = x.shape

    def body(x_ref, r_ref, g_ref, fw_ref, t_ref, loss_ref, dx_ref, dr_ref, dg_ref, dfw_ref):
        @pl.when(pl.program_id(0) == 0)
        def _():
            loss_ref[...] = jnp.zeros_like(loss_ref)
            dg_ref[...] = jnp.zeros_like(dg_ref)
            dfw_ref[...] = jnp.zeros_like(dfw_ref)

        rr = r_ref[...].astype(F32)
        gg = g_ref[...]
        xn = x_ref[...] + gg * rr
        rstd = lax.rsqrt(jnp.mean(xn * xn, axis=-1, keepdims=True) + EPS)
        xh = xn * rstd
        fwv = fw_ref[...]
        e = xh * fwv - t_ref[...]
        loss_ref[...] += 0.5 * jnp.sum(jnp.mean(e * e, axis=-1, keepdims=True), axis=0, keepdims=True)
        dy = e * (1.0 / d)
        dfw_ref[...] += jnp.sum(dy * xh, axis=0, keepdims=True)
        dxh = dy * fwv
        dx = rstd * (dxh - xh * jnp.mean(dxh * xh, axis=-1, keepdims=True))
        dx_ref[...] = dx
        dr_ref[...] = (gg * dx).astype(BF16)
        dg_ref[...] += jnp.sum(dx * rr, axis=0, keepdims=True)

    return pl.pallas_call(
        body, name=name, grid=(s // NORM_ROWS,),
        in_specs=[_row_spec(d), _row_spec(d), _vec_spec(d), _vec_spec(d), _row_spec(d)],
        out_specs=(_vec_spec(LANES), _row_spec(d), _row_spec(d), _vec_spec(d), _vec_spec(d)),
        out_shape=(jax.ShapeDtypeStruct((1, LANES), F32), jax.ShapeDtypeStruct((s, d), F32),
                   jax.ShapeDtypeStruct((s, d), BF16), jax.ShapeDtypeStruct((1, d), F32),
                   jax.ShapeDtypeStruct((1, d), F32)),
        compiler_params=_params(1))(x, r, g, fw, tgt)


def _norm_bwd(x, dhs, dxres, nw, sc, g, r, name):
    s, d = x.shape
    n_dh = len(dhs)
    has_g = g is not None

    def body(*refs):
        x_ref = refs[0]
        dh_refs = refs[1:1 + n_dh]
        dxres_ref, nw_ref, sc_ref = refs[1 + n_dh:4 + n_dh]
        pos = 4 + n_dh
        if has_g:
            g_ref, r_ref = refs[pos:pos + 2]
            pos += 2
            dx_ref, dr_ref, dsh_ref, dsc_ref, dnw_ref, dg_ref = refs[pos:]
            accs = (dsh_ref, dsc_ref, dnw_ref, dg_ref)
        else:
            dx_ref, dsh_ref, dsc_ref, dnw_ref = refs[pos:]
            accs = (dsh_ref, dsc_ref, dnw_ref)

        @pl.when(pl.program_id(0) == 0)
        def _():
            for acc in accs:
                acc[...] = jnp.zeros_like(acc)

        xv = x_ref[...]
        dh = dh_refs[0][...].astype(F32)
        for extra in dh_refs[1:]:
            dh = dh + extra[...].astype(F32)
        rstd = lax.rsqrt(jnp.mean(xv * xv, axis=-1, keepdims=True) + EPS)
        xh = xv * rstd
        nwv = nw_ref[...]
        dsh_ref[...] += jnp.sum(dh, axis=0, keepdims=True)
        dsc_ref[...] += jnp.sum(dh * (xh * nwv), axis=0, keepdims=True)
        dy = dh * (1.0 + sc_ref[...])
        dnw_ref[...] += jnp.sum(dy * xh, axis=0, keepdims=True)
        dxh = dy * nwv
        dx = dxres_ref[...] + rstd * (dxh - xh * jnp.mean(dxh * xh, axis=-1, keepdims=True))
        dx_ref[...] = dx
        if has_g:
            dr_ref[...] = (g_ref[...] * dx).astype(BF16)
            dg_ref[...] += jnp.sum(dx * r_ref[...].astype(F32), axis=0, keepdims=True)

    ins = [x, *dhs, dxres, nw, sc]
    in_specs = [_row_spec(d)] * (2 + n_dh) + [_vec_spec(d)] * 2
    out_shape = [jax.ShapeDtypeStruct((s, d), F32)]
    out_specs = [_row_spec(d)]
    if has_g:
        ins += [g, r]
        in_specs += [_vec_spec(d), _row_spec(d)]
        out_shape.append(jax.ShapeDtypeStruct((s, d), BF16))
        out_specs.append(_row_spec(d))
    n_vec = 4 if has_g else 3
    out_shape += [jax.ShapeDtypeStruct((1, d), F32)] * n_vec
    out_specs += [_vec_spec(d)] * n_vec
    return pl.pallas_call(body, name=name, grid=(s // NORM_ROWS,), in_specs=in_specs, out_specs=tuple(out_specs),
                          out_shape=tuple(out_shape), compiler_params=_params(1))(*ins)


FFN_ROWS = 256
FFN_COLS = 1408
HALO = 16
INV_SQRT2 = 0.7071067811865476
INV_SQRT_2PI = 0.3989422804014327


def _gelu(x):
    return 0.5 * x * (1.0 + lax.erf(x * INV_SQRT2))


def _gelu_grad(x):
    return 0.5 * (1.0 + lax.erf(x * INV_SQRT2)) + x * (INV_SQRT_2PI * jnp.exp(-0.5 * x * x))


def _halo_specs(rows, cols, halo, n_rows_total, col_of):
    per = rows // halo
    last = n_rows_total // halo - 1
    cur = pl.BlockSpec((rows, cols), lambda j, i: (i, col_of(j)))
    prev = pl.BlockSpec((halo, cols), lambda j, i: (jnp.maximum(i * per - 1, 0), col_of(j)))
    nxt = pl.BlockSpec((halo, cols), lambda j, i: (jnp.minimum((i + 1) * per, last), col_of(j)))
    return [prev, cur, nxt]


def _shift_rows(x, k):
    n = x.shape[0]
    return pltpu.roll(x, k % n, axis=0)


def _conv3(ext, w):
    return w[0:1, :] * _shift_rows(ext, 1) + w[1:2, :] * ext + w[2:3, :] * _shift_rows(ext, -1)


def _ext_block(prev_ref, cur_ref, next_ref, i, n_i):
    prev = jnp.where(i > 0, prev_ref[...].astype(F32), 0.0)
    nxt = jnp.where(i < n_i - 1, next_ref[...].astype(F32), 0.0)
    return jnp.concatenate([prev, cur_ref[...].astype(F32), nxt], axis=0)


def _ffn_act(u, cw, name):
    s = u.shape[0]
    nc, ns = D_FF // FFN_COLS, s // FFN_ROWS

    def body(gp, gc, gn, vp, vc, vn, wg_ref, wv_ref, o_ref, cg_ref, cv_ref):
        i = pl.program_id(1)
        cg = _conv3(_ext_block(gp, gc, gn, i, ns), wg_ref[...])[HALO:HALO + FFN_ROWS]
        cv = _conv3(_ext_block(vp, vc, vn, i, ns), wv_ref[...])[HALO:HALO + FFN_ROWS]
        o_ref[...] = (_gelu(cg) * cv).astype(BF16)
        cg_ref[...] = cg.astype(BF16)
        cv_ref[...] = cv.astype(BF16)

    in_specs = (_halo_specs(FFN_ROWS, FFN_COLS, HALO, s, lambda j: j)
                + _halo_specs(FFN_ROWS, FFN_COLS, HALO, s, lambda j: j + nc)
                + [pl.BlockSpec((3, FFN_COLS), lambda j, i: (0, j)),
                   pl.BlockSpec((3, FFN_COLS), lambda j, i: (0, j + nc))])
    blk = pl.BlockSpec((FFN_ROWS, FFN_COLS), lambda j, i: (i, j))
    return pl.pallas_call(
        body, name=name, grid=(nc, ns), in_specs=in_specs, out_specs=(blk, blk, blk),
        out_shape=(jax.ShapeDtypeStruct((s, D_FF), BF16),) * 3, compiler_params=_params(2),
    )(u, u, u, u, u, u, cw, cw)


def _ffn_act_bwd(u, cg, cv, dact, cw, name):
    s = u.shape[0]
    nc, ns = D_FF // FFN_COLS, s // FFN_ROWS

    def body(ug_ref, uv_ref, gp, gc, gn, vp, vc, vn, dp, dc, dn, wg_ref, wv_ref, dug_ref, duv_ref, dwg_ref, dwv_ref):
        i = pl.program_id(1)

        @pl.when(i == 0)
        def _():
            dwg_ref[...] = jnp.zeros_like(dwg_ref)
            dwv_ref[...] = jnp.zeros_like(dwv_ref)

        cge = _ext_block(gp, gc, gn, i, ns)
        cve = _ext_block(vp, vc, vn, i, ns)
        da = _ext_block(dp, dc, dn, i, ns)
        dcg = da * cve * _gelu_grad(cge)
        dcv = da * _gelu(cge)
        inner = slice(HALO, HALO + FFN_ROWS)
        for d_c, u_ref, w_ref, du_ref, dw_ref in ((dcg, ug_ref, wg_ref, dug_ref, dwg_ref),
                                                  (dcv, uv_ref, wv_ref, duv_ref, dwv_ref)):
            w = w_ref[...]
            d_next, d_prev = _shift_rows(d_c, -1), _shift_rows(d_c, 1)
            du = w[0:1, :] * d_next + w[1:2, :] * d_c + w[2:3, :] * d_prev
            du_ref[...] = du[inner].astype(BF16)
            u_in = u_ref[...].astype(F32)
            for tap, d_tap in enumerate((d_next, d_c, d_prev)):
                dw_ref[tap:tap + 1, :] += jnp.sum(d_tap[inner] * u_in, axis=0, keepdims=True)

    blk = pl.BlockSpec((FFN_ROWS, FFN_COLS), lambda j, i: (i, j))
    in_specs = ([blk, pl.BlockSpec((FFN_ROWS, FFN_COLS), lambda j, i: (i, j + nc))]
                + _halo_specs(FFN_ROWS, FFN_COLS, HALO, s, lambda j: j) * 3
                + [pl.BlockSpec((3, FFN_COLS), lambda j, i: (0, j)),
                   pl.BlockSpec((3, FFN_COLS), lambda j, i: (0, j + nc))])
    acc = pl.BlockSpec((HALO, FFN_COLS), lambda j, i: (0, j))
    return pl.pallas_call(
        body, name=name, grid=(nc, ns), in_specs=in_specs, out_specs=(blk, blk, acc, acc),
        out_shape=(jax.ShapeDtypeStruct((s, D_FF), BF16), jax.ShapeDtypeStruct((s, D_FF), BF16),
                   jax.ShapeDtypeStruct((HALO, D_FF), F32), jax.ShapeDtypeStruct((HALO, D_FF), F32)),
        compiler_params=_params(2),
    )(u, u, cg, cg, cg, cv, cv, cv, dact, dact, dact, cw, cw)


CONV_ROWS = 512
CONV_HALO = 16
CONV_PAD = CONV_WIDTH // 2


def _conv_halo_specs(cols, s):
    per = CONV_ROWS // CONV_HALO
    last = s // CONV_HALO - 1
    return [pl.BlockSpec((CONV_HALO, cols), lambda i: (jnp.maximum(i * per - 1, 0), 0)),
            pl.BlockSpec((CONV_ROWS, cols), lambda i: (i, 0)),
            pl.BlockSpec((CONV_HALO, cols), lambda i: (jnp.minimum((i + 1) * per, last), 0))]


def _glu_ext(pp, pc, pn, i, n_i):
    ext = _ext_block(pp, pc, pn, i, n_i)
    return ext[:, :CONV_CH] * _sigmoid(ext[:, CONV_CH:])


def _conv_mixer(pa, cw, cb, lnw, lnb, name):
    s = pa.shape[0]
    ns = s // CONV_ROWS

    def body(pp, pc, pn, cw_ref, cb_ref, lnw_ref, lnb_ref, o_ref, c_ref):
        i = pl.program_id(0)
        a = _glu_ext(pp, pc, pn, i, ns)
        acc = jnp.zeros((CONV_ROWS, CONV_CH), F32)
        for tap in range(CONV_WIDTH):
            acc = acc + cw_ref[tap:tap + 1, :] * _shift_rows(a, -(tap + 1))[:CONV_ROWS]
        cv = acc + cb_ref[...]
        c_ref[...] = cv
        mu = jnp.mean(cv, axis=-1, keepdims=True)
        xc = cv - mu
        rstd = lax.rsqrt(jnp.mean(xc * xc, axis=-1, keepdims=True) + EPS)
        y = xc * rstd * lnw_ref[...] + lnb_ref[...]
        o_ref[...] = (y * _sigmoid(y)).astype(BF16)

    vec = pl.BlockSpec((1, CONV_CH), lambda i: (0, 0))
    blk = pl.BlockSpec((CONV_ROWS, CONV_CH), lambda i: (i, 0))
    return pl.pallas_call(
        body, name=name, grid=(ns,),
        in_specs=_conv_halo_specs(2 * CONV_CH, s) + [pl.BlockSpec((CONV_WIDTH, CONV_CH), lambda i: (0, 0)), vec, vec, vec],
        out_specs=(blk, blk),
        out_shape=(jax.ShapeDtypeStruct((s, CONV_CH), BF16), jax.ShapeDtypeStruct((s, CONV_CH), F32)),
        compiler_params=_params(1))(pa, pa, pa, cw, cb, lnw, lnb)


def _conv_mixer_bwd_ln(cv, dout, lnw, lnb, name):
    s = cv.shape[0]

    def body(c_ref, do_ref, lnw_ref, lnb_ref, dc_ref, dlnw_ref, dlnb_ref, dcb_ref):
        @pl.when(pl.program_id(0) == 0)
        def _():
            dlnw_ref[...] = jnp.zeros_like(dlnw_ref)
            dlnb_ref[...] = jnp.zeros_like(dlnb_ref)
            dcb_ref[...] = jnp.zeros_like(dcb_ref)

        c = c_ref[...]
        mu = jnp.mean(c, axis=-1, keepdims=True)
        xc = c - mu
        rstd = lax.rsqrt(jnp.mean(xc * xc, axis=-1, keepdims=True) + EPS)
        xh = xc * rstd
        w = lnw_ref[...]
        y = xh * w + lnb_ref[...]
        dy = do_ref[...] * _silu_grad(y)
        dlnw_ref[...] += jnp.sum(dy * xh, axis=0, keepdims=True)
        dlnb_ref[...] += jnp.sum(dy, axis=0, keepdims=True)
        dxh = dy * w
        dc = rstd * (dxh - jnp.mean(dxh, axis=-1, keepdims=True) - xh * jnp.mean(dxh * xh, axis=-1, keepdims=True))
        dc_ref[...] = dc
        dcb_ref[...] += jnp.sum(dc, axis=0, keepdims=True)

    vec = pl.BlockSpec((1, CONV_CH), lambda i: (0, 0))
    blk = pl.BlockSpec((CONV_ROWS, CONV_CH), lambda i: (i, 0))
    return pl.pallas_call(
        body, name=name, grid=(s // CONV_ROWS,), in_specs=[blk, blk, vec, vec], out_specs=(blk, vec, vec, vec),
        out_shape=(jax.ShapeDtypeStruct((s, CONV_CH), F32),) + (jax.ShapeDtypeStruct((1, CONV_CH), F32),) * 3,
        compiler_params=_params(1))(cv, dout, lnw, lnb)


def _conv_mixer_bwd_conv(pa, dc, cw, name):
    s = pa.shape[0]
    ns = s // CONV_ROWS

    def body(pc, dp, dcc, dn, cw_ref, dpa_ref, dcw_ref):
        i = pl.program_id(0)

        @pl.when(i == 0)
        def _():
            dcw_ref[...] = jnp.zeros_like(dcw_ref)

        cur = pc[...]
        val, sg = cur[:, :CONV_CH], _sigmoid(cur[:, CONV_CH:])
        a_cur = val * sg
        dce = _ext_block(dp, dcc, dn, i, ns)
        da = jnp.zeros((CONV_ROWS, CONV_CH), F32)
        for tap in range(CONV_WIDTH):
            shifted = _shift_rows(dce, -(CONV_WIDTH - tap))[:CONV_ROWS]
            da = da + cw_ref[tap:tap + 1, :] * shifted
            dcw_ref[tap:tap + 1, :] += jnp.sum(shifted * a_cur, axis=0, keepdims=True)
        dpa_ref[:, :CONV_CH] = (da * sg).astype(BF16)
        dpa_ref[:, CONV_CH:] = (da * val * sg * (1.0 - sg)).astype(BF16)

    return pl.pallas_call(
        body, name=name, grid=(ns,),
        in_specs=[pl.BlockSpec((CONV_ROWS, 2 * CONV_CH), lambda i: (i, 0))] + _conv_halo_specs(CONV_CH, s)
        + [pl.BlockSpec((CONV_WIDTH, CONV_CH), lambda i: (0, 0))],
        out_specs=(pl.BlockSpec((CONV_ROWS, 2 * CONV_CH), lambda i: (i, 0)),
                   pl.BlockSpec((32, CONV_CH), lambda i: (0, 0))),
        out_shape=(jax.ShapeDtypeStruct((s, 2 * CONV_CH), BF16), jax.ShapeDtypeStruct((32, CONV_CH), F32)),
        compiler_params=_params(1))(pa, dc, dc, dc, cw)


SLOPES = tuple(float(2.0 ** (-ALIBI_MAX_EXP * (h + 1) / N_HEADS)) for h in range(N_HEADS))
ATT_SCALE = HEAD_DIM ** -0.5


PAIR = 2 * HEAD_DIM
N_PAIRS = N_HEADS // 2
ATT_WIN = ATT_BLOCK + 2 * ATT_HALF


ATT_GROUPS = {1: 4, 4: 1, 16: 1}


def _window_specs(dil, n_steps, col_of):
    per = 2 * ATT_GROUPS[dil]
    rows, halo = ATT_BLOCK * dil * ATT_GROUPS[dil], ATT_HALF * dil
    return [pl.BlockSpec((halo, PAIR), lambda i, p: (jnp.maximum(per * i - 1, 0), col_of(p))),
            pl.BlockSpec((rows, PAIR), lambda i, p: (i, col_of(p))),
            pl.BlockSpec((halo, PAIR), lambda i, p: (jnp.minimum(per * (i + 1), per * n_steps - 1), col_of(p)))]


def _residue(ref, r, n, dil, start=0):
    return ref[pl.ds(start * dil + r, n, stride=dil), :] if dil > 1 else ref[pl.ds(start + r, n), :]


def _store_residue(ref, r, dil, start, val):
    if dil > 1:
        ref[pl.ds(start * dil + r, val.shape[0], stride=dil), :] = val
    else:
        ref[pl.ds(start + r, val.shape[0]), :] = val


def _residue_window(refs, r, dil, g=0):
    prev, cur, nxt = refs
    groups = ATT_GROUPS[dil]
    lo = max(g * ATT_BLOCK - ATT_HALF, 0)
    hi = min((g + 1) * ATT_BLOCK + ATT_HALF, groups * ATT_BLOCK)
    parts = [_residue(prev, r, ATT_HALF, dil)] if g == 0 else []
    parts.append(_residue(cur, r, hi - lo, dil, lo))
    if g == groups - 1:
        parts.append(_residue(nxt, r, ATT_HALF, dil))
    return jnp.concatenate(parts, axis=0)


def _band_masks(i, length, dil, transposed):
    shape = (ATT_WIN, ATT_BLOCK) if transposed else (ATT_BLOCK, ATT_WIN)
    row = lax.broadcasted_iota(jnp.int32, shape, 0)
    col = lax.broadcasted_iota(jnp.int32, shape, 1)
    wide = row if transposed else col
    dist = jnp.abs((row - col - ATT_HALF) if transposed else (row + ATT_HALF - col))
    wpos = i * ATT_BLOCK - ATT_HALF + wide
    valid = (dist <= ATT_HALF) & (wpos >= 0) & (wpos < length)
    return valid, dist.astype(F32) * float(dil)


def _attn_branch(qkv, dil, name):
    s = qkv.shape[0]
    groups = ATT_GROUPS[dil]
    rows = ATT_BLOCK * dil * groups
    n_steps = s // rows
    length = s // dil
    nt = (((1,), (1,)), ((), ()))

    def body(q_ref, kp, kc, kn, vp, vc, vn, o_ref, l_ref):
        i, pair = pl.program_id(0), pl.program_id(1)
        items = [(g, r) for g in range(groups) for r in range(dil)]
        q = jnp.stack([_residue(q_ref, r, ATT_BLOCK, dil, g * ATT_BLOCK) for g, r in items]).astype(BF16)
        k = jnp.stack([_residue_window((kp, kc, kn), r, dil, g) for g, r in items]).astype(BF16)
        v = jnp.stack([_residue_window((vp, vc, vn), r, dil, g) for g, r in items]).astype(BF16)
        per_group = [_band_masks(i * groups + g, length, dil, False) for g in range(groups)]
        valid = jnp.stack([per_group[g][0] for g, _ in items]) if groups > 1 else per_group[0][0][None]
        distf = jnp.stack([per_group[g][1] for g, _ in items]) if groups > 1 else per_group[0][1][None]
        outs, lses = [], []
        for hh in range(2):
            sl = slice(hh * HEAD_DIM, (hh + 1) * HEAD_DIM)
            slope = jnp.where(pair == 0, SLOPES[hh], jnp.where(pair == 1, SLOPES[2 + hh], SLOPES[4 + hh]))
            sc = jnp.einsum("bqd,bkd->bqk", q[:, :, sl], k[:, :, sl], preferred_element_type=F32) * ATT_SCALE
            sc = jnp.where(valid, sc - slope * distf, MASK_VALUE)
            m = jnp.max(sc, axis=-1, keepdims=True)
            p = jnp.exp(sc - m)
            den = jnp.sum(p, axis=-1, keepdims=True)
            outs.append(jnp.einsum("bqk,bkd->bqd", p.astype(BF16), v[:, :, sl], preferred_element_type=F32) / den)
            lses.append(jnp.broadcast_to(m + jnp.log(den), (len(items), ATT_BLOCK, HEAD_DIM)))
        o_all, l_all = jnp.concatenate(outs, axis=2), jnp.concatenate(lses, axis=2)
        for n, (g, r) in enumerate(items):
            _store_residue(o_ref, r, dil, g * ATT_BLOCK, o_all[n])
            _store_residue(l_ref, r, dil, g * ATT_BLOCK, l_all[n])

    out_blk = pl.BlockSpec((rows, PAIR), lambda i, p: (i, p))
    return pl.pallas_call(
        body, name=name, grid=(n_steps, N_PAIRS),
        in_specs=[pl.BlockSpec((rows, PAIR), lambda i, p: (i, p))]
        + _window_specs(dil, n_steps, lambda p: N_PAIRS + p) + _window_specs(dil, n_steps, lambda p: 2 * N_PAIRS + p),
        out_specs=(out_blk, out_blk),
        out_shape=(jax.ShapeDtypeStruct((s, ATT_WIDTH), F32),) * 2,
        compiler_params=_params(2))(qkv, qkv, qkv, qkv, qkv, qkv, qkv)


ATT_ROWS = 512


def _attn_combine(outs, lses, name):
    s = outs[0].shape[0]

    def body(o1, o2, o3, l1, l2, l3, att_ref, att32_ref, lse_ref):
        ls = [l1[...], l2[...], l3[...]]
        m = jnp.maximum(jnp.maximum(ls[0], ls[1]), ls[2])
        es = [jnp.exp(l - m) for l in ls]
        den = es[0] + es[1] + es[2]
        att = (es[0] * o1[...] + es[1] * o2[...] + es[2] * o3[...]) / den
        att_ref[...] = att.astype(BF16)
        att32_ref[...] = att
        lse_ref[...] = m + jnp.log(den)

    blk = pl.BlockSpec((ATT_ROWS, ATT_WIDTH), lambda i: (i, 0))
    return pl.pallas_call(
        body, name=name, grid=(s // ATT_ROWS,), in_specs=[blk] * 6, out_specs=(blk, blk, blk),
        out_shape=(jax.ShapeDtypeStruct((s, ATT_WIDTH), BF16), jax.ShapeDtypeStruct((s, ATT_WIDTH), F32),
                   jax.ShapeDtypeStruct((s, ATT_WIDTH), F32)),
        compiler_params=_params(1))(*outs, *lses)


def _attn_delta(datt, att, name):
    s = att.shape[0]

    def body(d_ref, a_ref, delta_ref):
        prod = d_ref[...] * a_ref[...]
        for h in range(N_HEADS):
            sl = slice(h * HEAD_DIM, (h + 1) * HEAD_DIM)
            delta_ref[:, sl] = jnp.broadcast_to(jnp.sum(prod[:, sl], axis=-1, keepdims=True), (ATT_ROWS, HEAD_DIM))

    blk = pl.BlockSpec((ATT_ROWS, ATT_WIDTH), lambda i: (i, 0))
    return pl.pallas_call(
        body, name=name, grid=(s // ATT_ROWS,), in_specs=[blk, blk], out_specs=blk,
        out_shape=jax.ShapeDtypeStruct((s, ATT_WIDTH), F32), compiler_params=_params(1))(datt, att)


def _attn_branch_bwd(qkv, do, lse, delta, prev, dil, out_dtype, name):
    s = qkv.shape[0]
    groups = ATT_GROUPS[dil]
    rows = ATT_BLOCK * dil * groups
    n_steps = s // rows
    length = s // dil
    has_prev = prev is not None
    tn = (((0,), (0,)), ((), ()))
    nt = (((1,), (1,)), ((), ()))

    def body(*refs):
        qs, ks, vs, dos, ls, des = (refs[3 * n:3 * n + 3] for n in range(6))
        rest = refs[18:]
        if has_prev:
            pq, pk, pv = rest[:3]
            rest = rest[3:]
        dq_ref, dk_ref, dv_ref = rest
        i, pair = pl.program_id(0), pl.program_id(1)
        items = [(g, r) for g in range(groups) for r in range(dil)]
        cur = lambda t: jnp.stack([_residue(t[1], r, ATT_BLOCK, dil, g * ATT_BLOCK) for g, r in items])
        win = lambda t: jnp.stack([_residue_window(t, r, dil, g) for g, r in items])
        q_cur, k_cur, v_cur, do_cur = (cur(t).astype(BF16) for t in (qs, ks, vs, dos))
        q_win, k_win, v_win, do_win = (win(t).astype(BF16) for t in (qs, ks, vs, dos))
        l_cur, de_cur, l_win, de_win = cur(ls), cur(des), win(ls), win(des)

        def masks(transposed):
            per_group = [_band_masks(i * groups + g, length, dil, transposed) for g in range(groups)]
            if groups == 1:
                return per_group[0][0][None], per_group[0][1][None]
            return jnp.stack([per_group[g][0] for g, _ in items]), jnp.stack([per_group[g][1] for g, _ in items])

        valid_q, distf_q = masks(False)
        valid_k, distf_k = masks(True)
        dot = lambda eq, a, b: jnp.einsum(eq, a, b, preferred_element_type=F32)
        dqs, dks, dvs = [], [], []
        for hh in range(2):
            sl = slice(hh * HEAD_DIM, (hh + 1) * HEAD_DIM)
            one = slice(hh * HEAD_DIM, hh * HEAD_DIM + 1)
            slope = jnp.where(pair == 0, SLOPES[hh], jnp.where(pair == 1, SLOPES[2 + hh], SLOPES[4 + hh]))
            sc = dot("bqd,bkd->bqk", q_cur[:, :, sl], k_win[:, :, sl]) * ATT_SCALE - slope * distf_q
            p = jnp.exp(jnp.where(valid_q, sc - l_cur[:, :, one], MASK_VALUE))
            dp = dot("bqd,bkd->bqk", do_cur[:, :, sl], v_win[:, :, sl])
            ds = (p * (dp - de_cur[:, :, one]) * ATT_SCALE).astype(BF16)
            dqs.append(dot("bqk,bkd->bqd", ds, k_win[:, :, sl]))

            sc2 = dot("bqd,bkd->bqk", q_win[:, :, sl], k_cur[:, :, sl]) * ATT_SCALE - slope * distf_k
            p2 = jnp.exp(jnp.where(valid_k, sc2 - l_win[:, :, one], MASK_VALUE))
            dvs.append(dot("bqk,bqd->bkd", p2.astype(BF16), do_win[:, :, sl]))
            dp2 = dot("bqd,bkd->bqk", do_win[:, :, sl], v_cur[:, :, sl])
            ds2 = (p2 * (dp2 - de_win[:, :, one]) * ATT_SCALE).astype(BF16)
            dks.append(dot("bqk,bqd->bkd", ds2, q_win[:, :, sl]))
        for parts, acc, out in ((dqs, pq if has_prev else None, dq_ref), (dks, pk if has_prev else None, dk_ref),
                                (dvs, pv if has_prev else None, dv_ref)):
            val = jnp.concatenate(parts, axis=2)
            for n, (g, r) in enumerate(items):
                piece = val[n]
                if has_prev:
                    piece = piece + _residue(acc, r, ATT_BLOCK, dil, g * ATT_BLOCK)
                _store_residue(out, r, dil, g * ATT_BLOCK, piece.astype(out_dtype))

    blk = pl.BlockSpec((rows, PAIR), lambda i, p: (i, p))
    in_specs = (_window_specs(dil, n_steps, lambda p: p) + _window_specs(dil, n_steps, lambda p: N_PAIRS + p)
                + _window_specs(dil, n_steps, lambda p: 2 * N_PAIRS + p) + _window_specs(dil, n_steps, lambda p: p) * 3)
    ins = [qkv] * 9 + [do] * 3 + [lse] * 3 + [delta] * 3
    if has_prev:
        in_specs += [blk] * 3
        ins += list(prev)
    return pl.pallas_call(
        body, name=name, grid=(n_steps, N_PAIRS), in_specs=in_specs, out_specs=(blk, blk, blk),
        out_shape=(jax.ShapeDtypeStruct((s, ATT_WIDTH), out_dtype),) * 3,
        compiler_params=_params(2))(*ins)


TB = 2 * REC_CHUNK
REC_ROWS = 5 * REC_WIDTH


REC_LEVELS = 6


def _scan_pos(p, rev):
    p = p & (REC_CHUNK - 1)
    return (REC_CHUNK - 1 - p) if rev else p


def _split3(x):
    hi = x.astype(BF16)
    rest = x - hi.astype(F32)
    mid = rest.astype(BF16)
    return hi, mid, (rest - mid.astype(F32)).astype(BF16)


def _chunk_sums(x, rev, with_levels):
    row = lax.broadcasted_iota(jnp.int32, (TB, TB), 0)
    col = lax.broadcasted_iota(jnp.int32, (TB, TB), 1)
    same = (row < REC_CHUNK) == (col < REC_CHUNK)
    s_row, s_col = _scan_pos(row, rev), _scan_pos(col, rev)
    mats = [same & (s_row <= s_col)]
    if with_levels:
        for level in range(1, REC_LEVELS + 1):
            shift = REC_LEVELS + 1 - level
            boundary = ((s_col >> shift) << shift) + (REC_CHUNK >> level) - 1
            mats.append(same & (s_row <= boundary))
        mats.append(same)
    cat = jnp.concatenate([m.astype(BF16) for m in mats], axis=1)
    total = sum(jnp.dot(term, cat, preferred_element_type=F32) for term in _split3(x))
    return [total[:, n * TB:(n + 1) * TB] for n in range(len(mats))]


def _hg_prep(qraw, z, lb, rev):
    lane = lax.broadcasted_iota(jnp.int32, (REC_WIDTH, TB), 1)
    in_a = lane < REC_CHUNK
    scan = _scan_pos(lane, rev)
    sig, sigm = _sigmoid(z), _sigmoid(-z)
    f = lb + (1.0 - lb) * sig
    kk = (1.0 - lb) * sigm
    sums = _chunk_sums(jnp.log(jnp.maximum(f, F_TINY)), rev, True)
    b, bend = sums[0], sums[-1]
    q = qraw * _sigmoid(qraw)
    eq, ek = [], []
    for level in range(1, REC_LEVELS + 1):
        r = sums[level]
        e = jnp.exp(jnp.minimum(b - r, r - b))
        second = ((scan >> (REC_LEVELS - level)) & 1) == 1
        eq.append(jnp.where(second, e, 0.0))
        ek.append(jnp.where(second, 0.0, e))
    lanes_end = (0, REC_CHUNK) if rev else (REC_CHUNK - 1, TB - 1)
    end_a, end_b = (b[:, n:n + 1] for n in lanes_end)
    return dict(in_a=in_a, sig=sig, sigm=sigm, f=f, kk=kk, b=b, end_a=end_a, end_b=end_b,
                q=q, qh=q * jnp.exp(b), kh=kk * jnp.exp(bend - b), ekb=jnp.exp(bend - b), eq=eq, ek=ek)


def _level_masks(rev):
    row = lax.broadcasted_iota(jnp.int32, (TB, TB), 0)
    col = lax.broadcasted_iota(jnp.int32, (TB, TB), 1)
    same = (row < REC_CHUNK) == (col < REC_CHUNK)
    s_row, s_col = _scan_pos(row, rev), _scan_pos(col, rev)
    masks = [same & ((s_row >> (REC_LEVELS + 1 - level)) == (s_col >> (REC_LEVELS + 1 - level)))
             for level in range(1, REC_LEVELS + 1)]
    return masks, row == col


def _head_rows(x, h):
    return x[h * HEAD_DIM:(h + 1) * HEAD_DIM, :]


def _block_diag_mask():
    r = lax.broadcasted_iota(jnp.int32, (REC_WIDTH, REC_WIDTH), 0) // HEAD_DIM
    c = lax.broadcasted_iota(jnp.int32, (REC_WIDTH, REC_WIDTH), 1) // HEAD_DIM
    return (r == c).astype(F32)


def _heads(x):
    return x.reshape(N_HEADS, HEAD_DIM, TB)


def _hgrn_scan(projt, lb, rev, name):
    s = projt.shape[1]
    nblk = s // TB
    zrow = 2 if rev else 1
    tmap = (lambda i: nblk - 1 - i) if rev else (lambda i: i)
    tn = (((0,), (0,)), ((), ()))
    nt = (((1,), (1,)), ((), ()))

    def body(q_ref, z_ref, v_ref, lb_ref, o_ref, hs_ref, at_ref, h_ref):
        @pl.when(pl.program_id(0) == 0)
        def _():
            h_ref[...] = jnp.zeros_like(h_ref)

        v = v_ref[...]
        vb = v.astype(BF16)
        pr = _hg_prep(q_ref[...], z_ref[...], lb_ref[...], rev)
        q, kk = pr["q"], pr["kk"]
        masks, diag = _level_masks(rev)
        own = jnp.sum(_heads(q * kk), axis=1, keepdims=True)
        sc = jnp.where(diag[None], own, 0.0)
        for level in range(REC_LEVELS):
            qt = _heads((q * pr["eq"][level]).astype(BF16))
            kt = _heads((kk * pr["ek"][level]).astype(BF16))
            sc = sc + jnp.where(masks[level][None],
                                jnp.einsum("hks,hkt->hst", kt, qt, preferred_element_type=F32), 0.0)
        a_bf = sc.astype(BF16)
        at_ref[...] = a_bf
        o = jnp.einsum("hvs,hst->hvt", _heads(vb), a_bf, preferred_element_type=F32).reshape(REC_WIDTH, TB)
        bd_mask = _block_diag_mask()
        order = ((1, ~pr["in_a"], pr["end_b"]), (0, pr["in_a"], pr["end_a"]))
        if not rev:
            order = order[::-1]
        for slot, msk, bend in order:
            h0 = h_ref[...]
            hs_ref[slot] = h0
            o = o + lax.dot_general(h0.astype(BF16), jnp.where(msk, pr["qh"], 0.0).astype(BF16), tn,
                                    preferred_element_type=F32)
            upd = lax.dot_general(jnp.where(msk, pr["kh"], 0.0).astype(BF16), vb, nt, preferred_element_type=F32)
            h_ref[...] = jnp.exp(bend) * h0 + upd * bd_mask
        o_ref[...] = o

    row_blk = lambda r: pl.BlockSpec((REC_WIDTH, TB), lambda i: (r, tmap(i)))
    return pl.pallas_call(
        body, name=name, grid=(nblk,),
        in_specs=[row_blk(0), row_blk(zrow), row_blk(3), pl.BlockSpec((REC_WIDTH, 1), lambda i: (0, 0))],
        out_specs=(pl.BlockSpec((REC_WIDTH, TB), lambda i: (0, tmap(i))),
                   pl.BlockSpec((2, REC_WIDTH, REC_WIDTH), lambda i: (tmap(i), 0, 0)),
                   pl.BlockSpec((None, N_HEADS, TB, TB), lambda i: (tmap(i), 0, 0, 0))),
        out_shape=(jax.ShapeDtypeStruct((REC_WIDTH, s), F32),
                   jax.ShapeDtypeStruct((s // REC_CHUNK, REC_WIDTH, REC_WIDTH), F32),
                   jax.ShapeDtypeStruct((nblk, N_HEADS, TB, TB), BF16)),
        scratch_shapes=[pltpu.VMEM((REC_WIDTH, REC_WIDTH), F32)],
        compiler_params=_params(1))(projt, projt, projt, lb)


def _hgrn_scan_bwd(projt, lb, dot, hs, at, prev, rev, name):
    s = projt.shape[1]
    nblk = s // TB
    zrow = 2 if rev else 1
    tmap = (lambda i: i) if rev else (lambda i: nblk - 1 - i)
    has_prev = prev is not None
    tn = (((0,), (0,)), ((), ()))
    nt = (((1,), (1,)), ((), ()))

    def body(*refs):
        q_ref, z_ref, v_ref, lb_ref, do_ref, hs_ref, at_ref = refs[:7]
        rest = refs[7:]
        if has_prev:
            pq_ref, pv_ref = rest[:2]
            rest = rest[2:]
        dq_ref, dz_ref, dv_ref, dlb_ref, dh_ref = rest

        @pl.when(pl.program_id(0) == 0)
        def _():
            dh_ref[...] = jnp.zeros_like(dh_ref)
            dlb_ref[...] = jnp.zeros_like(dlb_ref)

        qraw, v, do, lbv = q_ref[...], v_ref[...], do_ref[...], lb_ref[...]
        dob, vb = do.astype(BF16), v.astype(BF16)
        pr = _hg_prep(qraw, z_ref[...], lbv, rev)
        q, kk, b, in_a = pr["q"], pr["kk"], pr["b"], pr["in_a"]
        masks, diag = _level_masks(rev)
        dot = lambda eq, x, y: jnp.einsum(eq, x, y, preferred_element_type=F32)
        d_at = dot("hvs,hvt->hst", _heads(vb), _heads(dob))
        dv = dot("hvt,hst->hvs", _heads(dob), at_ref[...]).reshape(REC_WIDTH, TB)
        d_own = jnp.sum(jnp.where(diag[None], d_at, 0.0), axis=1, keepdims=True)
        dq_in = (d_own * _heads(kk)).reshape(REC_WIDTH, TB)
        dk_in = (d_own * _heads(q)).reshape(REC_WIDTH, TB)
        db_in = jnp.zeros((REC_WIDTH, TB), F32)
        for lv in range(REC_LEVELS):
            d_lv = jnp.where(masks[lv][None], d_at, 0.0).astype(BF16)
            q_lv, k_lv = (q * pr["eq"][lv]).astype(BF16), (kk * pr["ek"][lv]).astype(BF16)
            dqt = dot("hks,hst->hkt", _heads(k_lv), d_lv).reshape(REC_WIDTH, TB)
            dkt = dot("hkt,hst->hks", _heads(q_lv), d_lv).reshape(REC_WIDTH, TB)
            dq_in = dq_in + pr["eq"][lv] * dqt
            dk_in = dk_in + pr["ek"][lv] * dkt
            db_in = db_in + q_lv.astype(F32) * dqt - k_lv.astype(F32) * dkt
        dq = dk = jnp.zeros((REC_WIDTH, TB), F32)

        zero = jnp.zeros((REC_WIDTH, TB), F32)
        bd_mask = _block_diag_mask()
        eb = jnp.exp(b)
        const = zero
        order = ((0, in_a, pr["end_a"]), (1, ~in_a, pr["end_b"]))
        if not rev:
            order = order[::-1]
        for slot, msk, bend in order:
            h0 = hs_ref[slot]
            dh1 = dh_ref[...]
            dh1b = dh1.astype(BF16)
            dq = dq + eb * jnp.dot(h0.astype(BF16), jnp.where(msk, do, 0.0).astype(BF16), preferred_element_type=F32)
            dv = dv + lax.dot_general(dh1b, jnp.where(msk, pr["kh"], 0.0).astype(BF16), tn, preferred_element_type=F32)
            dk_int = pr["ekb"] * jnp.dot(dh1b, jnp.where(msk, v, 0.0).astype(BF16), preferred_element_type=F32)
            dk = dk + dk_int
            ebend = jnp.exp(bend)
            c = (jnp.sum(kk * dk_int, axis=1, keepdims=True)
                 + ebend * jnp.sum(h0 * dh1, axis=1, keepdims=True))
            const = const + jnp.where(msk, c, 0.0)
            upd = lax.dot_general(jnp.where(msk, pr["qh"], 0.0).astype(BF16), dob, nt, preferred_element_type=F32)
            dh_ref[...] = ebend * dh1 + upd * bd_mask

        dg = _chunk_sums(db_in + q * dq - kk * dk, not rev, False)[0] + const
        dq, dk = dq + dq_in, dk + dk_in
        sig, sigm, f = pr["sig"], pr["sigm"], pr["f"]
        live = f > F_TINY
        inv_f = 1.0 / jnp.maximum(f, F_TINY)
        one_lb = 1.0 - lbv
        dz = sig * sigm * one_lb * (jnp.where(live, dg * inv_f, 0.0) - dk)
        dlb_ref[...] += jnp.sum(sigm * (jnp.where(live, dg * inv_f, 0.0) - dk), axis=1, keepdims=True)
        dqr = dq * _silu_grad(qraw)
        if has_prev:
            dqr = dqr + pq_ref[...]
            dv = dv + pv_ref[...]
        dq_ref[...] = dqr
        dz_ref[...] = dz
        dv_ref[...] = dv

    row_blk = lambda r: pl.BlockSpec((REC_WIDTH, TB), lambda i: (r, tmap(i)))
    blk = pl.BlockSpec((REC_WIDTH, TB), lambda i: (0, tmap(i)))
    col = pl.BlockSpec((REC_WIDTH, 1), lambda i: (0, 0))
    in_specs = [row_blk(0), row_blk(zrow), row_blk(3), col, blk,
                pl.BlockSpec((2, REC_WIDTH, REC_WIDTH), lambda i: (tmap(i), 0, 0)),
                pl.BlockSpec((None, N_HEADS, TB, TB), lambda i: (tmap(i), 0, 0, 0))]
    ins = [projt, projt, projt, lb, dot, hs, at]
    if has_prev:
        in_specs += [blk, blk]
        ins += list(prev)
    t_shape = jax.ShapeDtypeStruct((REC_WIDTH, s), F32)
    return pl.pallas_call(
        body, name=name, grid=(nblk,), in_specs=in_specs, out_specs=(blk, blk, blk, col),
        out_shape=(t_shape, t_shape, t_shape, jax.ShapeDtypeStruct((REC_WIDTH, 1), F32)),
        scratch_shapes=[pltpu.VMEM((REC_WIDTH, REC_WIDTH), F32)],
        compiler_params=_params(1))(*ins)


REC_OUT_COLS = 512


def _head_rms(o):
    o3 = o.reshape(N_HEADS, HEAD_DIM, o.shape[1])
    rstd = lax.rsqrt(jnp.mean(o3 * o3, axis=1, keepdims=True) + EPS)
    return o3 * rstd, rstd


def _hgrn_out(of, ob, projt, wn, name):
    s = of.shape[1]

    def body(of_ref, ob_ref, g_ref, wn_ref, o_ref):
        on, _ = _head_rms(of_ref[...] + ob_ref[...])
        g = g_ref[...]
        y = on.reshape(REC_WIDTH, REC_OUT_COLS) * wn_ref[...] * (g * _sigmoid(g))
        o_ref[...] = y.T.astype(BF16)

    blk = pl.BlockSpec((REC_WIDTH, REC_OUT_COLS), lambda i: (0, i))
    return pl.pallas_call(
        body, name=name, grid=(s // REC_OUT_COLS,),
        in_specs=[blk, blk, pl.BlockSpec((REC_WIDTH, REC_OUT_COLS), lambda i: (4, i)),
                  pl.BlockSpec((REC_WIDTH, 1), lambda i: (0, 0))],
        out_specs=pl.BlockSpec((REC_OUT_COLS, REC_WIDTH), lambda i: (i, 0)),
        out_shape=jax.ShapeDtypeStruct((s, REC_WIDTH), BF16), compiler_params=_params(1))(of, ob, projt, wn)


def _hgrn_out_bwd(drec, of, ob, projt, wn, name):
    s = of.shape[1]

    def body(d_ref, of_ref, ob_ref, g_ref, wn_ref, do_ref, dg_ref, dwn_ref):
        @pl.when(pl.program_id(0) == 0)
        def _():
            dwn_ref[...] = jnp.zeros_like(dwn_ref)

        dy = d_ref[...].T
        on3, rstd = _head_rms(of_ref[...] + ob_ref[...])
        on = on3.reshape(REC_WIDTH, REC_OUT_COLS)
        g, wnv = g_ref[...], wn_ref[...]
        dg_ref[...] = dy * on * wnv * _silu_grad(g)
        d_onw = dy * (g * _sigmoid(g))
        dwn_ref[...] += jnp.sum(d_onw * on, axis=1, keepdims=True)
        d_on3 = (d_onw * wnv).reshape(N_HEADS, HEAD_DIM, REC_OUT_COLS)
        do3 = rstd * (d_on3 - on3 * jnp.mean(d_on3 * on3, axis=1, keepdims=True))
        do_ref[...] = do3.reshape(REC_WIDTH, REC_OUT_COLS)

    blk = pl.BlockSpec((REC_WIDTH, REC_OUT_COLS), lambda i: (0, i))
    col = pl.BlockSpec((REC_WIDTH, 1), lambda i: (0, 0))
    t_shape = jax.ShapeDtypeStruct((REC_WIDTH, s), F32)
    return pl.pallas_call(
        body, name=name, grid=(s // REC_OUT_COLS,),
        in_specs=[pl.BlockSpec((REC_OUT_COLS, REC_WIDTH), lambda i: (i, 0)), blk, blk,
                  pl.BlockSpec((REC_WIDTH, REC_OUT_COLS), lambda i: (4, i)), col],
        out_specs=(blk, blk, col),
        out_shape=(t_shape, t_shape, jax.ShapeDtypeStruct((REC_WIDTH, 1), F32)),
        compiler_params=_params(1))(drec, of, ob, projt, wn)


def _lower_bounds(gamma, name):
    def body(g_ref, lb_ref, p_ref):
        g0, g1 = g_ref[0:1, :], g_ref[1:2, :]
        m = jnp.maximum(g0, g1)
        e0, e1 = jnp.exp(g0 - m), jnp.exp(g1 - m)
        p0, p1 = e0 / (e0 + e1), e1 / (e0 + e1)
        lb_ref[...] = (p0 + p1) - p0
        p_ref[0:1, :] = p0
        p_ref[1:2, :] = p1

    n = gamma.shape[1]
    return pl.pallas_call(body, name=name,
                          out_shape=(jax.ShapeDtypeStruct((1, n), F32), jax.ShapeDtypeStruct((2, n), F32)))(gamma)


def _lower_bounds_bwd(dlb1, p, name):
    def body(d_ref, p_ref, o_ref):
        p0, p1, d = p_ref[0:1, :], p_ref[1:2, :], d_ref[...]
        inner = p1 * d
        o_ref[0:1, :] = p0 * (0.0 - inner)
        o_ref[1:2, :] = p1 * (d - inner)

    return pl.pallas_call(body, name=name, out_shape=jax.ShapeDtypeStruct(p.shape, F32))(dlb1, p)


def _split_w_in(w_in):
    return dict(conv=w_in[:, G_CONV[0]:G_CONV[1]], qkv=w_in[:, G_QKV[0]:G_QKV[1]],
                rec_t=w_in[:, G_REC[0]:].T, nat=w_in[:, :G_REC[0]])


def _split_w_rest(w_out, w_up, w_down):
    return dict(out=w_out, out_a=w_out[:CONV_CH], out_b=w_out[CONV_CH:CONV_CH + ATT_WIDTH],
                out_c=w_out[CONV_CH + ATT_WIDTH:], up=w_up, down=w_down)


def _col(v):
    return v.reshape(-1, 1)


def _sequence_step(x, tgt, mods, lbs, small, w_in0, later_weights, final_w, on_layer_grads):
    saved = []
    xin = x
    big = [_split_w_in(w_in0), None]
    h1 = _resid_norm_mod(x, None, None, small[0]["norm1_w"], mods[0][1:2], mods[0][0:1], "norm1_first")
    for l in range(DEPTH):
        sm, w, md = small[l], big[l], mods[l]
        pa = _matmul(h1, w["conv"], "nn", F32, f"proj_conv")
        qkv = _matmul(h1, w["qkv"], "nn", F32, f"proj_qkv")
        projt = _matmul(w["rec_t"], h1, "nt", F32, f"proj_rec")
        a_out, cv = _conv_mixer(pa, sm["conv_a_w"], sm["conv_a_b"], sm["ln_a_w"], sm["ln_a_b"], f"conv_mixer")
        outs, lses = zip(*[_attn_branch(qkv, d, f"attn_d{d}") for d in DILATIONS])
        att, att32, lse = _attn_combine(outs, lses, f"attn_combine")
        lb_f, lb_b = _col(lbs[l][0]), _col(lbs[l][1])
        of, hsf, atf = _hgrn_scan(projt, lb_f, False, "hgrn_fwd")
        ob, hsb, atb = _hgrn_scan(projt, lb_b, True, "hgrn_rev")
        wn = _col(sm["rec_norm_w"])
        rec = _hgrn_out(of, ob, projt, wn, f"hgrn_out")
        mixed = jnp.concatenate([a_out, att, rec], axis=1)
        if l == 0:
            w_out_all = later_weights(0, rec)
            big[0].update(_split_w_rest(w_out_all[0], None, None))
        r1 = _matmul(mixed, w["out"], "nn", BF16, "out_proj")
        xmid, h2 = _resid_norm_mod(xin, r1, md[2:3], sm["norm2_w"], md[4:5], md[3:4], f"norm2")
        if l == 0:
            w_in1, w_up_all, w_down_all = later_weights(1, h2)
            big[0].update(up=w_up_all[0], down=w_down_all[0])
            big[1] = dict(_split_w_in(w_in1), **_split_w_rest(w_out_all[1], w_up_all[1], w_down_all[1]))
        u = _matmul(h2, w["up"], "nn", BF16, f"ffn_up")
        act, conv_g, conv_v = _ffn_act(u, sm["conv_f_w"], "ffn_act")
        r2 = _matmul(act, w["down"], "nn", BF16, "ffn_down")
        saved.append(dict(xin=xin, h1=h1, pa=pa, qkv=qkv, projt=projt, cv=cv, att32=att32, lse=lse, of=of, ob=ob,
                          hsf=hsf, hsb=hsb, atf=atf, atb=atb, lb_f=lb_f, lb_b=lb_b, wn=wn, mixed=mixed, r1=r1, xmid=xmid, h2=h2,
                          u=u, conv_g=conv_g, conv_v=conv_v, act=act, r2=r2))
        if l + 1 < DEPTH:
            nxt = small[l + 1]
            xin, h1 = _resid_norm_mod(xmid, r2, md[5:6], nxt["norm1_w"], mods[l + 1][1:2], mods[l + 1][0:1],
                                      "norm1")
    top = saved[-1]
    loss, dx, dr2, dg2, dfw = _final_loss(top["xmid"], top["r2"], mods[-1][5:6], final_w, tgt, "final_loss")

    grads = [None] * DEPTH
    order_after = None
    for l in reversed(range(DEPTH)):
        sm, w, md, sv = small[l], big[l], mods[l], saved[l]
        dact = _matmul(dr2, w["down"], "nt", BF16, f"d_act")
        g_down = _matmul(dr2, sv["act"], "tn", F32, "dw_down").T
        conv_f_w = sm["conv_f_w"] if order_after is None else sm["conv_f_w"] + order_after
        dug, duv, dwg, dwv = _ffn_act_bwd(sv["u"], sv["conv_g"], sv["conv_v"], dact, conv_f_w, "ffn_act_bwd")
        dh2 = _matmul_two_lhs(dug, duv, w["up"], BF16, "d_h2")
        g_up = _matmul_two_rhs(sv["h2"], dug, duv, F32, "dw_up")
        after_ffn = on_layer_grads(l, {2: g_up, 3: g_down}, False)
        norm2_w = sm["norm2_w"] if after_ffn is None else sm["norm2_w"] + after_ffn
        dxmid, dr1, dsh2, dsc2, dnw2, dg1 = _norm_bwd(sv["xmid"], [dh2], dx, norm2_w, md[4:5], md[2:3], sv["r1"],
                                                     f"norm2_bwd")
        dmix_a = _matmul(dr1, w["out_a"], "nt", F32, f"d_mix_a")
        dmix_b = _matmul(dr1, w["out_b"], "nt", F32, f"d_mix_b")
        dmix_c = _matmul(dr1, w["out_c"], "nt", F32, f"d_mix_c")
        g_out = _matmul(sv["mixed"], dr1, "tn", F32, f"dw_out")
        dc, dlnw, dlnb, dcb = _conv_mixer_bwd_ln(sv["cv"], dmix_a, sm["ln_a_w"], sm["ln_a_b"], f"conv_mixer_bwd_ln")
        dpa, dcw = _conv_mixer_bwd_conv(sv["pa"], dc, sm["conv_a_w"], f"conv_mixer_bwd_conv")
        delta = _attn_delta(dmix_b, sv["att32"], "attn_delta")
        dqkv = None
        for d in reversed(DILATIONS):
            dqkv = _attn_branch_bwd(sv["qkv"], dmix_b, sv["lse"], delta, dqkv, d, BF16 if d == 1 else F32,
                                    f"attn_bwd_d{d}")
        dot, dgt, dwn = _hgrn_out_bwd(dmix_c, sv["of"], sv["ob"], sv["projt"], sv["wn"], f"hgrn_out_bwd")
        dqf, dzf, dvf, dlbf = _hgrn_scan_bwd(sv["projt"], sv["lb_f"], dot, sv["hsf"], sv["atf"], None, False,
                                             "hgrn_fwd_bwd")
        dqt, dzb, dvt, dlbb = _hgrn_scan_bwd(sv["projt"], sv["lb_b"], dot, sv["hsb"], sv["atb"], (dqf, dvf), True,
                                             "hgrn_rev_bwd")
        dprojt = jnp.concatenate([dqt, dzf, dzb, dvt, dgt], axis=0).astype(BF16)
        dnat = jnp.concatenate([dpa, *dqkv], axis=1)
        dh1_a = _matmul(dnat, w["nat"], "nt", BF16, "d_h1_nat")
        dh1_b = _matmul(dprojt, w["rec_t"], "tn", BF16, "d_h1_rec")
        g_in_nat = _matmul(sv["h1"], dnat, "tn", F32, f"dw_in_nat")
        g_in_rec_t = _matmul(dprojt, sv["h1"], "nn", F32, f"dw_in_rec")
        g_in = jnp.concatenate([g_in_nat, g_in_rec_t.T], axis=1)
        if l > 0:
            below = saved[l - 1]
            dx, dr2, dsh1, dsc1, dnw1, dg2_below = _norm_bwd(sv["xin"], [dh1_a, dh1_b], dxmid, sm["norm1_w"], md[1:2],
                                                            mods[l - 1][5:6], below["r2"], f"norm1_bwd")
        else:
            dx, dsh1, dsc1, dnw1 = _norm_bwd(sv["xin"], [dh1_a, dh1_b], dxmid, sm["norm1_w"], md[1:2], None, None,
                                             f"norm1_bwd")
        grads[l] = dict(w_in=g_in, w_out=g_out, w_up=g_up, w_down=g_down,
                        mod=[dsh1, dsc1, dg1, dsh2, dsc2, dg2], norm1_w=dnw1, conv_a_w=dcw[:CONV_WIDTH], conv_a_b=dcb,
                        ln_a_w=dlnw, ln_a_b=dlnb, lb=jnp.concatenate([dlbf.reshape(1, -1), dlbb.reshape(1, -1)], axis=0),
                        rec_norm_w=dwn.reshape(1, -1), norm2_w=dnw2,
                        conv_f_w=jnp.concatenate([dwg[:3], dwv[:3]], axis=1))
        order_after = on_layer_grads(l, {0: g_in, 1: g_out}, True)
        if l > 0:
            dg2 = dg2_below
    return loss[0, 0], dx, grads, dfw


def _adamw_math(w, g, m, v):
    m = ADAM_B1 * m + (1.0 - ADAM_B1) * g
    v = ADAM_B2 * v + (1.0 - ADAM_B2) * (g * g)
    m_hat = m / (1.0 - ADAM_B1 ** ADAM_STEP)
    v_hat = v / (1.0 - ADAM_B2 ** ADAM_STEP)
    delta = -ADAM_LR * (m_hat / (jnp.sqrt(v_hat) + ADAM_EPS) + ADAM_WD * w)
    return delta, m, v


def _row_tile(rows, cols, max_elems=384 * 1024):
    best = None
    for t in range(8, rows + 1, 8):
        if rows % t == 0 and t * cols <= max_elems:
            best = t
    return best or rows


def _adamw(w, g, m, v, name):
    nl, r, c = w.shape
    tr = _row_tile(r, c)

    def body(w_ref, g_ref, m_ref, v_ref, d_ref, m2_ref, v2_ref):
        d_ref[...], m2_ref[...], v2_ref[...] = _adamw_math(w_ref[...], g_ref[...], m_ref[...], v_ref[...])

    blk = pl.BlockSpec((None, tr, c), lambda l, i: (l, i, 0))
    shape = jax.ShapeDtypeStruct((nl, r, c), F32)
    return pl.pallas_call(body, name=name, grid=(nl, r // tr), in_specs=[blk] * 4, out_specs=(blk, blk, blk),
                          out_shape=(shape, shape, shape), compiler_params=_params(2))(w, g, m, v)


ADA_SHARD = N_MOD * D_MODEL // 4
ADA_COLS = 512
ADA_ROWS = 256
HIGHEST = lax.Precision.HIGHEST


def _ada_mod(c_all, w_ada, b_sh, name):
    def body(c_ref, w_ref, b_ref, o_ref):
        cv = c_ref[...]
        o_ref[...] = jnp.dot(cv * _sigmoid(cv), w_ref[...], precision=HIGHEST, preferred_element_type=F32) + b_ref[...]

    return pl.pallas_call(
        body, name=name, grid=(DEPTH, ADA_SHARD // ADA_COLS),
        in_specs=[pl.BlockSpec((8, D_MODEL), lambda l, j: (0, 0)),
                  pl.BlockSpec((None, D_MODEL, ADA_COLS), lambda l, j: (l, 0, j)),
                  pl.BlockSpec((None, 1, ADA_COLS), lambda l, j: (l, 0, j))],
        out_specs=pl.BlockSpec((None, 8, ADA_COLS), lambda l, j: (l, 0, j)),
        out_shape=jax.ShapeDtypeStruct((DEPTH, 8, ADA_SHARD), F32), compiler_params=_params(2))(c_all, w_ada, b_sh)


def _ada_update(c_all, dmod_sh, w, m, v, name):
    def body(c_ref, d_ref, w_ref, m_ref, v_ref, g_ref, dl_ref, m2_ref, v2_ref):
        cv = c_ref[...]
        g = lax.dot_general(cv * _sigmoid(cv), d_ref[...], (((0,), (0,)), ((), ())), precision=HIGHEST,
                            preferred_element_type=F32)
        g_ref[...] = g
        dl_ref[...], m2_ref[...], v2_ref[...] = _adamw_math(w_ref[...], g, m_ref[...], v_ref[...])

    blk = pl.BlockSpec((None, ADA_ROWS, ADA_SHARD), lambda l, i: (l, i, 0))
    shape = jax.ShapeDtypeStruct((DEPTH, D_MODEL, ADA_SHARD), F32)
    return pl.pallas_call(
        body, name=name, grid=(DEPTH, D_MODEL // ADA_ROWS),
        in_specs=[pl.BlockSpec((8, ADA_ROWS), lambda l, i: (0, i)),
                  pl.BlockSpec((None, 8, ADA_SHARD), lambda l, i: (l, 0, 0)), blk, blk, blk],
        out_specs=(blk,) * 4, out_shape=(shape,) * 4, compiler_params=_params(2))(c_all, dmod_sh, w, m, v)


def _sum_devices(packs, name):
    def body(p_ref, o_ref):
        acc = p_ref[0]
        for dev in range(1, 8):
            acc = acc + p_ref[dev]
        o_ref[...] = acc

    return pl.pallas_call(body, name=name, out_shape=jax.ShapeDtypeStruct(packs.shape[1:], F32))(packs)


def _mesh_pos():
    return lax.axis_index("x"), lax.axis_index("y"), lax.axis_index("c")


def _flip(v, bit):
    return 1 - v if bit else v


def _allgather_devices(x, name):
    m_per, n = x.shape

    def body(x_ref, out_ref, send_sems, recv_sems, local_sem):
        ix, iy, ic = _mesh_pos()
        me, sibling = (ix, iy, ic), (ix, iy, 1 - ic)
        chips = [(1 - ix, iy), (ix, 1 - iy), (1 - ix, 1 - iy)]

        def rows(px, py, pc):
            return out_ref.at[pl.ds((4 * px + 2 * py + pc) * m_per, m_per), :]

        def copy(k, block, to, src=None):
            return pltpu.make_async_remote_copy(
                src_ref=rows(*block) if src is None else src, dst_ref=rows(*block),
                send_sem=send_sems.at[k], recv_sem=recv_sems.at[k], device_id=to, device_id_type=MESH)

        mine = pltpu.make_async_copy(x_ref, rows(*me), local_sem)
        mine.start()
        first = [copy(0, me, sibling, src=x_ref)]
        first += [copy(1 + j, me, (*chip, ic), src=x_ref) for j, chip in enumerate(chips)]
        for cp in first:
            cp.start()
        passed = [copy(4 + j, (*chip, ic), sibling) for j, chip in enumerate(chips)]
        for j, chip in enumerate(chips):
            copy(1 + j, (*chip, ic), me).wait_recv()
            passed[j].start()
        copy(0, sibling, me).wait_recv()
        for j, chip in enumerate(chips):
            copy(4 + j, (*chip, 1 - ic), me).wait_recv()
        for cp in first + passed:
            cp.wait_send()
        mine.wait()

    return pl.pallas_call(
        body, name=name, out_shape=jax.ShapeDtypeStruct((8 * m_per, n), x.dtype),
        in_specs=[pl.BlockSpec(memory_space=pltpu.VMEM)], out_specs=pl.BlockSpec(memory_space=pltpu.VMEM),
        scratch_shapes=[pltpu.SemaphoreType.DMA((7,)), pltpu.SemaphoreType.DMA((7,)), pltpu.SemaphoreType.DMA],
    )(x)


def _gather_chips(shards, name):
    n = len(shards)

    def body(*refs):
        ins, outs = refs[:n], refs[n:2 * n]
        send_sems, recv_sems, local_sems = refs[2 * n:]
        ix, iy, ic = _mesh_pos()
        me = 2 * ix + iy
        local = [pltpu.make_async_copy(ins[a], outs[a].at[me], local_sems.at[a]) for a in range(n)]
        for cp in local:
            cp.start()
        remote = []
        for a in range(n):
            for k in (1, 2, 3):
                px, py = _flip(ix, k & 2), _flip(iy, k & 1)
                sems = dict(send_sem=send_sems.at[3 * a + k - 1], recv_sem=recv_sems.at[3 * a + k - 1],
                            device_id=(px, py, ic), device_id_type=MESH)
                out_cp = pltpu.make_async_remote_copy(src_ref=ins[a], dst_ref=outs[a].at[me], **sems)
                in_cp = pltpu.make_async_remote_copy(src_ref=ins[a], dst_ref=outs[a].at[2 * px + py], **sems)
                out_cp.start()
                remote.append((out_cp, in_cp))
        for out_cp, in_cp in remote:
            out_cp.wait_send()
            in_cp.wait_recv()
        for cp in local:
            cp.wait()

    return pl.pallas_call(
        body, name=name, in_specs=[ANY] * n, out_specs=tuple([ANY] * n),
        out_shape=tuple(jax.ShapeDtypeStruct((4,) + t.shape, t.dtype) for t in shards),
        scratch_shapes=[pltpu.SemaphoreType.DMA((3 * n,)), pltpu.SemaphoreType.DMA((3 * n,)),
                        pltpu.SemaphoreType.DMA((n,))],
    )(*shards)


HBM = pl.BlockSpec(memory_space=pltpu.HBM)
SEM = pl.BlockSpec(memory_space=pltpu.SEMAPHORE)
DATAFLOW = pltpu.SideEffectType.DATAFLOW_SIDE_EFFECTING


def _peer_chip(ix, iy, k):
    return _flip(ix, k & 2), _flip(iy, k & 1)


def _gather_chips_start(shards, name):
    n = len(shards)

    def body(*refs):
        src, land = refs[:n], refs[n:2 * n]
        send_sems, recv_sems = refs[2 * n], refs[2 * n + 1]
        token = refs[-1]
        ix, iy, ic = _mesh_pos()
        me = 2 * ix + iy
        for a in range(n):
            for k in (1, 2, 3):
                px, py = _peer_chip(ix, iy, k)
                pltpu.make_async_remote_copy(
                    src_ref=src[a], dst_ref=land[a].at[me], send_sem=send_sems.at[3 * a + k - 1],
                    recv_sem=recv_sems.at[3 * a + k - 1], device_id=(px, py, ic), device_id_type=MESH).start()
        token[...] = jnp.zeros_like(token)

    hbm = lambda shape, dtype: pltpu.HBM(shape, dtype)
    operands = ([pltpu.with_memory_space_constraint(t, pltpu.HBM) for t in shards]
                + [pltpu.with_memory_space_constraint(lax.empty((4,) + t.shape, t.dtype), pltpu.HBM) for t in shards])
    return pl.pallas_call(
        body, name=name,
        out_shape=(pltpu.SemaphoreType.DMA((3 * n,)), pltpu.SemaphoreType.DMA((3 * n,)),
                   *[hbm(t.shape, t.dtype) for t in shards], *[hbm((4,) + t.shape, t.dtype) for t in shards],
                   jax.ShapeDtypeStruct((8, LANES), F32)),
        in_specs=(HBM,) * (2 * n),
        out_specs=(SEM, SEM) + (HBM,) * (2 * n) + (pl.BlockSpec(memory_space=pltpu.VMEM),),
        input_output_aliases={a: 2 + a for a in range(2 * n)},
        compiler_params=pltpu.CompilerParams(has_side_effects=DATAFLOW),
    )(*operands)


def _gather_chips_wait(started, which, after, name):
    send_sems, recv_sems = started[0], started[1]
    thru = started[2:-1]
    n = len(thru) // 2

    def body(*refs):
        src, land = refs[:n], refs[n:2 * n]
        send_sems, recv_sems = refs[2 * n], refs[2 * n + 1]
        ix, iy, ic = _mesh_pos()
        for a in which:
            for k in (1, 2, 3):
                px, py = _peer_chip(ix, iy, k)
                cp = pltpu.make_async_remote_copy(
                    src_ref=src[a], dst_ref=land[a].at[2 * px + py], send_sem=send_sems.at[3 * a + k - 1],
                    recv_sem=recv_sems.at[3 * a + k - 1], device_id=(px, py, ic), device_id_type=MESH)
                cp.wait_send()
                cp.wait_recv()

    outs = pl.pallas_call(
        body, name=name,
        out_shape=tuple(pltpu.HBM(t.shape, t.dtype) for t in thru),
        in_specs=(HBM,) * (2 * n) + (SEM, SEM, ANY), out_specs=(HBM,) * (2 * n),
        input_output_aliases={a: a for a in range(2 * n)},
        compiler_params=pltpu.CompilerParams(has_side_effects=DATAFLOW),
    )(*thru, send_sems, recv_sems, after)
    return outs[:n], outs[n:]


BIG_KINDS = (("w_in", "col", D_MODEL, IN_COLS), ("w_out", "row", D_MODEL, D_MODEL),
             ("w_up", "col", D_MODEL, 2 * D_FF), ("w_down", "row", D_FF, D_MODEL))


def _piece_shape(how, r, c):
    return (r // 2, c // 4) if how == "col" else (r // 8, c)


def _aligned(start, multiple):
    return start if isinstance(start, int) else pl.multiple_of(start, multiple)


def _piece(ref, how, r, c, chip, half):
    if how == "col":
        return ref.at[pl.ds(_aligned(half * (r // 2), 8), r // 2), pl.ds(_aligned(chip * (c // 4), LANES), c // 4)]
    n = r // 4
    return ref.at[pl.ds(_aligned(chip * n + half * (n // 2), 8), n // 2), :]


def _rs_pair_exchange(grads, kinds, name):
    nk = len(kinds)
    specs = [BIG_KINDS[ki] for ki in kinds]
    nl = len(grads[0])
    flat = [grads[ki][l] for ki in range(nk) for l in range(nl)]
    per = nl * 4

    def body(*refs):
        g, land = refs[:nk * nl], refs[nk * nl:nk * nl + nk]
        send_sems, recv_sems = refs[nk * nl + nk:]
        ix, iy, ic = _mesh_pos()
        sibling = (ix, iy, 1 - ic)
        copies = []
        for ki, (_, how, r, c) in enumerate(specs):
            for l in range(nl):
                for j in range(4):
                    sem = ki * per + l * 4 + j
                    rem = pltpu.make_async_remote_copy(
                        src_ref=_piece(g[ki * nl + l], how, r, c, j, 1 - ic), dst_ref=land[ki].at[l, j],
                        send_sem=send_sems.at[sem], recv_sem=recv_sems.at[sem], device_id=sibling, device_id_type=MESH)
                    rem.start()
                    copies.append(rem)
        for rem in copies:
            rem.wait_send()
            rem.wait_recv()

    shapes = [jax.ShapeDtypeStruct((nl, 4) + _piece_shape(how, r, c), F32) for _, how, r, c in specs]
    return pl.pallas_call(
        body, name=name, in_specs=[ANY] * len(flat), out_specs=tuple([ANY] * nk), out_shape=tuple(shapes),
        scratch_shapes=[pltpu.SemaphoreType.DMA((nk * per,))] * 2,
    )(*flat)


def _pair_sum(g, theirs, layer, how, core, name):
    r, c = g.shape
    pr, pc = _piece_shape(how, r, c)
    if how == "col":
        mine_spec = pl.BlockSpec((pr, pc), lambda j, core_ref: (core_ref[0], j))
    else:
        mine_spec = pl.BlockSpec((pr, pc), lambda j, core_ref: (2 * j + core_ref[0], 0))

    def body(core_ref, g_ref, t_ref, o_ref, ob_ref):
        total = g_ref[...] + t_ref[...]
        o_ref[...] = total
        ob_ref[...] = total.astype(BF16)

    out_blk = pl.BlockSpec((None, pr, pc), lambda j, core_ref: (j, 0, 0))
    return pl.pallas_call(
        body, name=name,
        grid_spec=pltpu.PrefetchScalarGridSpec(
            num_scalar_prefetch=1, grid=(4,),
            in_specs=[mine_spec, pl.BlockSpec((None, None, pr, pc), lambda j, core_ref: (layer, j, 0, 0))],
            out_specs=(out_blk, out_blk)),
        out_shape=(jax.ShapeDtypeStruct((4, pr, pc), F32), jax.ShapeDtypeStruct((4, pr, pc), BF16)),
        compiler_params=_params(1))(core, g, theirs)


def _rs_chip_exchange(pair_sums, name):
    nk = len(pair_sums)
    nl = len(pair_sums[0])
    flat = [pair_sums[ki][l] for ki in range(nk) for l in range(nl)]

    def body(*refs):
        src, dst = refs[:nk * nl], refs[nk * nl:nk * nl + nk]
        send_sems, recv_sems = refs[nk * nl + nk:]
        ix, iy, ic = _mesh_pos()
        copies = []
        for ki in range(nk):
            for l in range(nl):
                for k in (1, 2, 3):
                    px, py = _peer_chip(ix, iy, k)
                    sem = (ki * nl + l) * 3 + k - 1
                    rem = pltpu.make_async_remote_copy(
                        src_ref=src[ki * nl + l].at[2 * px + py], dst_ref=dst[ki].at[l, k - 1],
                        send_sem=send_sems.at[sem], recv_sem=recv_sems.at[sem], device_id=(px, py, ic), device_id_type=MESH)
                    rem.start()
                    copies.append(rem)
        for rem in copies:
            rem.wait_send()
            rem.wait_recv()

    return pl.pallas_call(
        body, name=name, in_specs=[ANY] * len(flat), out_specs=tuple([ANY] * nk),
        out_shape=tuple(jax.ShapeDtypeStruct((nl, 3) + pair_sums[ki][0].shape[1:], pair_sums[ki][0].dtype)
                        for ki in range(nk)),
        scratch_shapes=[pltpu.SemaphoreType.DMA((nk * nl * 3,))] * 2,
    )(*flat)


def _rs_chip_start(pieces, name):
    n = len(pieces)

    def body(*refs):
        src, land = refs[:n], refs[n:2 * n]
        send_sems, recv_sems = refs[2 * n], refs[2 * n + 1]
        token = refs[-1]
        ix, iy, ic = _mesh_pos()
        for a in range(n):
            for k in (1, 2, 3):
                px, py = _peer_chip(ix, iy, k)
                pltpu.make_async_remote_copy(
                    src_ref=src[a].at[2 * px + py], dst_ref=land[a].at[k - 1], send_sem=send_sems.at[3 * a + k - 1],
                    recv_sem=recv_sems.at[3 * a + k - 1], device_id=(px, py, ic), device_id_type=MESH).start()
        token[...] = jnp.zeros_like(token)

    land_shape = lambda t: (3,) + t.shape[1:]
    operands = ([pltpu.with_memory_space_constraint(t, pltpu.HBM) for t in pieces]
                + [pltpu.with_memory_space_constraint(lax.empty(land_shape(t), t.dtype), pltpu.HBM) for t in pieces])
    return pl.pallas_call(
        body, name=name,
        out_shape=(pltpu.SemaphoreType.DMA((3 * n,)), pltpu.SemaphoreType.DMA((3 * n,)),
                   *[pltpu.HBM(t.shape, t.dtype) for t in pieces], *[pltpu.HBM(land_shape(t), t.dtype) for t in pieces],
                   jax.ShapeDtypeStruct((8, LANES), F32)),
        in_specs=(HBM,) * (2 * n),
        out_specs=(SEM, SEM) + (HBM,) * (2 * n) + (pl.BlockSpec(memory_space=pltpu.VMEM),),
        input_output_aliases={a: 2 + a for a in range(2 * n)},
        compiler_params=pltpu.CompilerParams(has_side_effects=DATAFLOW),
    )(*operands)


def _rs_chip_wait(started, after, name):
    send_sems, recv_sems = started[0], started[1]
    thru = started[2:-1]
    n = len(thru) // 2

    def body(*refs):
        src, land = refs[:n], refs[n:2 * n]
        send_sems, recv_sems = refs[2 * n], refs[2 * n + 1]
        ix, iy, ic = _mesh_pos()
        for a in range(n):
            for k in (1, 2, 3):
                px, py = _peer_chip(ix, iy, k)
                cp = pltpu.make_async_remote_copy(
                    src_ref=src[a].at[2 * px + py], dst_ref=land[a].at[k - 1], send_sem=send_sems.at[3 * a + k - 1],
                    recv_sem=recv_sems.at[3 * a + k - 1], device_id=(px, py, ic), device_id_type=MESH)
                cp.wait_send()
                cp.wait_recv()

    outs = pl.pallas_call(
        body, name=name,
        out_shape=tuple(pltpu.HBM(t.shape, t.dtype) for t in thru),
        in_specs=(HBM,) * (2 * n) + (SEM, SEM, ANY), out_specs=(HBM,) * (2 * n),
        input_output_aliases={a: a for a in range(2 * n)},
        compiler_params=pltpu.CompilerParams(has_side_effects=DATAFLOW),
    )(*thru, send_sems, recv_sems, after)
    return outs[n:]


def _chip_sum(own, others, chip, name):
    _, pr, pc = own.shape

    def body(chip_ref, own_ref, s1, s2, s3, o_ref):
        o_ref[...] = ((own_ref[...] + s1[...].astype(F32)) + s2[...].astype(F32)) + s3[...].astype(F32)

    slot = lambda k: pl.BlockSpec((None, pr, pc), lambda i, chip_ref: (k, 0, 0))
    return pl.pallas_call(
        body, name=name,
        grid_spec=pltpu.PrefetchScalarGridSpec(
            num_scalar_prefetch=1, grid=(1,),
            in_specs=[pl.BlockSpec((None, pr, pc), lambda i, chip_ref: (chip_ref[0], 0, 0)), slot(0), slot(1), slot(2)],
            out_specs=pl.BlockSpec((pr, pc), lambda i, chip_ref: (0, 0))),
        out_shape=jax.ShapeDtypeStruct((pr, pc), F32), compiler_params=_params(1))(chip, own, others, others, others)


def _rs_pair_share(halves, name):
    nk = len(halves)
    flat = [halves[ki][l] for ki in range(nk) for l in range(DEPTH)]

    def body(*refs):
        src, dst = refs[:nk * DEPTH], refs[nk * DEPTH:nk * DEPTH + nk]
        send_sems, recv_sems = refs[nk * DEPTH + nk:]
        ix, iy, ic = _mesh_pos()
        copies = []
        for ki in range(nk):
            for l in range(DEPTH):
                sem = ki * DEPTH + l
                rem = pltpu.make_async_remote_copy(
                    src_ref=src[sem], dst_ref=dst[ki].at[l], send_sem=send_sems.at[sem], recv_sem=recv_sems.at[sem],
                    device_id=(ix, iy, 1 - ic), device_id_type=MESH)
                rem.start()
                copies.append(rem)
        for rem in copies:
            rem.wait_send()
            rem.wait_recv()

    return pl.pallas_call(
        body, name=name, in_specs=[ANY] * len(flat), out_specs=tuple([ANY] * nk),
        out_shape=tuple(jax.ShapeDtypeStruct((DEPTH,) + halves[ki][0].shape, F32) for ki in range(nk)),
        scratch_shapes=[pltpu.SemaphoreType.DMA((nk * DEPTH,))] * 2,
    )(*flat)


def _adamw_halves(w, mine, theirs, m, v, core, name):
    nl, pr, pc = theirs.shape
    shape = w.shape
    view = lambda t: t.reshape(nl, 2, pr, pc)
    tr = _row_tile(pr, pc, 256 * 1024)

    def body(core_ref, w_ref, a0_ref, a1_ref, t_ref, m_ref, v_ref, g_ref, d_ref, m2_ref, v2_ref):
        own = jnp.where(pl.program_id(0) == 0, a0_ref[...], a1_ref[...])
        g = jnp.where(pl.program_id(1) == core_ref[0], own, t_ref[...])
        g_ref[...] = g
        d_ref[...], m2_ref[...], v2_ref[...] = _adamw_math(w_ref[...], g, m_ref[...], v_ref[...])

    blk = pl.BlockSpec((None, None, tr, pc), lambda l, h, i, core_ref: (l, h, i, 0))
    own_blk = pl.BlockSpec((tr, pc), lambda l, h, i, core_ref: (i, 0))
    out = jax.ShapeDtypeStruct((nl, 2, pr, pc), F32)
    outs = pl.pallas_call(
        body, name=name,
        grid_spec=pltpu.PrefetchScalarGridSpec(
            num_scalar_prefetch=1, grid=(nl, 2, pr // tr),
            in_specs=[blk, own_blk, own_blk, pl.BlockSpec((None, tr, pc), lambda l, h, i, core_ref: (l, i, 0)), blk, blk],
            out_specs=(blk,) * 4),
        out_shape=(out,) * 4, compiler_params=_params(3),
    )(core, view(w), mine[0], mine[1], theirs, view(m), view(v))
    return tuple(t.reshape(shape) for t in outs)


def _rs_first_stages(layer_grads, kinds, core, tag, in_flight):
    theirs = _rs_pair_exchange([[g] for g in layer_grads], kinds, f"rs_pair_exchange_{tag}")
    sums = [_pair_sum(g, theirs[n], 0, BIG_KINDS[ki][1], core, f"rs_pair_sum_{BIG_KINDS[ki][0]}")
            for n, (ki, g) in enumerate(zip(kinds, layer_grads))]
    to_send = [both[1] for both in sums]
    if in_flight:
        return [both[0] for both in sums], _rs_chip_start(to_send, f"rs_chip_start_{tag}")
    slots = _rs_chip_exchange([[t] for t in to_send], f"rs_chip_exchange_{tag}")
    return [both[0] for both in sums], [t[0] for t in slots]


def _rs_last_stages(pair_sums, slots, chip):
    halves = [[_chip_sum(pair_sums[ki][l], slots[ki][l], chip, f"rs_chip_sum_{kind}") for l in range(DEPTH)]
              for ki, (kind, _, _, _) in enumerate(BIG_KINDS)]
    other = _rs_pair_share(halves, "rs_pair_share")
    return list(zip(halves, other))


WEIGHT_NAMES = ("w_ada", "b_ada", "norm1_w", "w_in", "conv_a_w", "conv_a_b", "ln_a_w", "ln_a_b", "lb_gamma",
                "rec_norm_w", "w_out", "norm2_w", "w_up", "conv_f_w", "w_down", "final_norm_w")
SMALL_PARAMS = (("b_ada", (DEPTH, N_MOD * D_MODEL), None), ("norm1_w", (DEPTH, D_MODEL), None),
                ("conv_a_w", (DEPTH, CONV_WIDTH, CONV_CH), 2), ("conv_a_b", (DEPTH, CONV_CH), None),
                ("ln_a_w", (DEPTH, CONV_CH), None), ("ln_a_b", (DEPTH, CONV_CH), None),
                ("lb_gamma", (DEPTH, 2, REC_WIDTH), 2), ("rec_norm_w", (DEPTH, REC_WIDTH), None),
                ("norm2_w", (DEPTH, D_MODEL), None), ("conv_f_w", (DEPTH, 3, 2 * D_FF), 2),
                ("final_norm_w", (D_MODEL,), None))


def _pack_rows(parts):
    flat = jnp.concatenate([p.reshape(-1) for p in parts])
    total = flat.shape[0]
    padded = -(-total // (8 * LANES)) * (8 * LANES)
    return jnp.pad(flat, (0, padded - total)).reshape(padded // LANES, LANES)


def _unpack(flat, shapes):
    out, off = [], 0
    for shp in shapes:
        size = int(np.prod(shp))
        out.append(flat[off:off + size].reshape(shp))
        off += size
    return out


def _unstack_chips(t, axis):
    return jnp.concatenate([t[j] for j in range(4)], axis=axis)


def kernel(x, c, w_ada, b_ada, norm1_w, w_in, conv_a_w, conv_a_b, ln_a_w, ln_a_b, lb_gamma, rec_norm_w, w_out, norm2_w, w_up, conv_f_w, w_down, final_norm_w, loss_target, m_w_ada, m_b_ada, m_norm1_w, m_w_in, m_conv_a_w, m_conv_a_b, m_ln_a_w, m_ln_a_b, m_lb_gamma, m_rec_norm_w, m_w_out, m_norm2_w, m_w_up, m_conv_f_w, m_w_down, m_final_norm_w, v_w_ada, v_b_ada, v_norm1_w, v_w_in, v_conv_a_w, v_conv_a_b, v_ln_a_w, v_ln_a_b, v_lb_gamma, v_rec_norm_w, v_w_out, v_norm2_w, v_w_up, v_conv_f_w, v_w_down, v_final_norm_w):
    params = dict(zip(WEIGHT_NAMES, (w_ada, b_ada, norm1_w, w_in, conv_a_w, conv_a_b, ln_a_w, ln_a_b, lb_gamma,
                                     rec_norm_w, w_out, norm2_w, w_up, conv_f_w, w_down, final_norm_w)))
    mom1 = dict(zip(WEIGHT_NAMES, (m_w_ada, m_b_ada, m_norm1_w, m_w_in, m_conv_a_w, m_conv_a_b, m_ln_a_w, m_ln_a_b,
                                   m_lb_gamma, m_rec_norm_w, m_w_out, m_norm2_w, m_w_up, m_conv_f_w, m_w_down,
                                   m_final_norm_w)))
    mom2 = dict(zip(WEIGHT_NAMES, (v_w_ada, v_b_ada, v_norm1_w, v_w_in, v_conv_a_w, v_conv_a_b, v_ln_a_w, v_ln_a_b,
                                   v_lb_gamma, v_rec_norm_w, v_w_out, v_norm2_w, v_w_up, v_conv_f_w, v_w_down,
                                   v_final_norm_w)))
    ix, iy, ic = _mesh_pos()
    chip = 2 * ix + iy
    dev = 2 * chip + ic

    c_all = _allgather_devices(c.reshape(8, LANES), "gather_cond").reshape(8, D_MODEL)
    b_sh = lax.dynamic_slice_in_dim(b_ada, chip * ADA_SHARD, ADA_SHARD, axis=1)
    mod_sh = _ada_mod(c_all, w_ada, b_sh.reshape(DEPTH, 1, ADA_SHARD), "ada_mod")
    w_in_b, w_out_b, w_up_b, w_down_b = (t.astype(BF16) for t in (w_in, w_out, w_up, w_down))
    first = _gather_chips([mod_sh, conv_a_w, conv_f_w, lb_gamma, w_in_b[0]], "gather_first")
    later = [w_out_b, w_up_b, w_down_b, w_in_b[1]]
    started = _gather_chips_start(later, "gather_rest_start")
    mod_mine = lax.dynamic_index_in_dim(first[0], dev, axis=2, keepdims=False) + started[-1][0, 0]
    mods = [jnp.concatenate([mod_mine[j, l] for j in range(4)]).reshape(N_MOD, D_MODEL) for l in range(DEPTH)]
    conv_a_w_f, conv_f_w_f, gamma_f = (_unstack_chips(first[k], 2) for k in (1, 2, 3))
    w_in0 = _unstack_chips(first[4], 1)

    flying = [started]

    def later_weights(stage, after):
        which = [0] if stage == 0 else [1, 2, 3]
        own, lands = _gather_chips_wait(flying[0], which, after, f"gather_rest_wait_{stage}")
        flying[0] = (started[0], started[1], *own, *lands, None)
        whole = lambda n, axis: jnp.concatenate([jnp.where(chip == j, own[n], lands[n][j]) for j in range(4)], axis=axis)
        return whole(0, 1) if stage == 0 else (whole(3, 1), whole(1, 2), whole(2, 1))

    lb1, p_soft = _lower_bounds(gamma_f.reshape(DEPTH, 2 * REC_WIDTH), "lower_bounds")
    lbs = [jnp.zeros((2, REC_WIDTH), F32), lb1.reshape(2, REC_WIDTH)]
    small = []
    for l in range(DEPTH):
        small.append(dict(norm1_w=norm1_w[l][None], conv_a_w=conv_a_w_f[l], conv_a_b=conv_a_b[l][None],
                          ln_a_w=ln_a_w[l][None], ln_a_b=ln_a_b[l][None], rec_norm_w=rec_norm_w[l],
                          norm2_w=norm2_w[l][None], conv_f_w=conv_f_w_f[l]))

    core_id, chip_id = ic.astype(jnp.int32).reshape(1), chip.astype(jnp.int32).reshape(1)
    pending, groups = {}, []

    def on_layer_grads(l, by_kind, last):
        if l > 0:
            pending.update(by_kind)
            if not last:
                return None
            by_kind = dict(pending)
        kinds = sorted(by_kind)
        in_flight = not (l == 0 and last)
        tag = f"l{l}" if l > 0 else f"l{l}_{'mix' if last else 'ffn'}"
        sums, exchange = _rs_first_stages([by_kind[k] for k in kinds], kinds, core_id, tag, in_flight)
        groups.append((l, kinds, sums, exchange, in_flight, tag))
        return exchange[-1][0:1, 0:1] if in_flight else None

    loss, dx, grads, dfw = _sequence_step(x[0], loss_target[0], mods, lbs, small, w_in0, later_weights,
                                          final_norm_w[None], on_layer_grads)
    loss = lax.psum(loss, ("x", "y", "c"))
    pair_sums = [[None] * DEPTH for _ in BIG_KINDS]
    slots = [[None] * DEPTH for _ in BIG_KINDS]
    for l, kinds, sums, exchange, in_flight, tag in groups:
        received = _rs_chip_wait(exchange, dx, f"rs_chip_wait_{tag}") if in_flight else exchange
        for n, ki in enumerate(kinds):
            pair_sums[ki][l], slots[ki][l] = sums[n], received[n]

    dgamma = _lower_bounds_bwd(grads[1]["lb"].reshape(1, 2 * REC_WIDTH), p_soft, "lower_bounds_bwd")
    dmod = [jnp.concatenate(grads[l]["mod"], axis=1) for l in range(DEPTH)]
    stack = lambda key: jnp.stack([grads[l][key] for l in range(DEPTH)])
    local_small = dict(b_ada=jnp.concatenate(dmod, axis=0), norm1_w=stack("norm1_w"), conv_a_w=stack("conv_a_w"),
                       conv_a_b=stack("conv_a_b"), ln_a_w=stack("ln_a_w"), ln_a_b=stack("ln_a_b"), lb_gamma=dgamma,
                       rec_norm_w=stack("rec_norm_w"), norm2_w=stack("norm2_w"), conv_f_w=stack("conv_f_w"),
                       final_norm_w=dfw)
    pack = _pack_rows([local_small[name] for name, _, _ in SMALL_PARAMS])
    rows = pack.shape[0]
    packs = _allgather_devices(pack, "gather_small_grads").reshape(8, rows, LANES)
    summed = _sum_devices(packs, "sum_small_grads").reshape(-1)
    small_grads = dict(zip([n for n, _, _ in SMALL_PARAMS], _unpack(summed, [shp for _, shp, _ in SMALL_PARAMS])))

    dmod_all = packs.reshape(8, rows * LANES)[:, :DEPTH * N_MOD * D_MODEL].reshape(8, DEPTH, N_MOD * D_MODEL)
    dmod_sh = lax.dynamic_slice_in_dim(dmod_all, chip * ADA_SHARD, ADA_SHARD, axis=2).transpose(1, 0, 2)
    g_ada, d_ada, m_ada, v_ada = _ada_update(c_all, dmod_sh, w_ada, m_w_ada, v_w_ada, "ada_update")

    for name, shp, axis in SMALL_PARAMS:
        if axis is not None:
            width = shp[axis] // 4
            small_grads[name] = lax.dynamic_slice_in_dim(small_grads[name], chip * width, width, axis=axis)
    names = [n for n, _, _ in SMALL_PARAMS]
    packed = [_pack_rows([src[n] for n in names])[None] for src in (params, small_grads, mom1, mom2)]
    small_out = _adamw(*packed, "adamw_small")
    shapes = [params[n].shape for n in names]
    small_delta, small_m, small_v = (dict(zip(names, _unpack(t.reshape(-1), shapes))) for t in small_out)

    summed_big = _rs_last_stages(pair_sums, slots, chip_id)
    grad, delta, new_m, new_v = dict(small_grads), small_delta, small_m, small_v
    grad["w_ada"], delta["w_ada"], new_m["w_ada"], new_v["w_ada"] = g_ada, d_ada, m_ada, v_ada
    for (name, _, _, _), (mine, theirs) in zip(BIG_KINDS, summed_big):
        grad[name], delta[name], new_m[name], new_v[name] = _adamw_halves(
            params[name], mine, theirs, mom1[name], mom2[name], core_id, f"adamw_{name}")

    return (loss, dx[None], *[grad[n] for n in WEIGHT_NAMES], *[delta[n] for n in WEIGHT_NAMES],
            *[new_m[n] for n in WEIGHT_NAMES], *[new_v[n] for n in WEIGHT_NAMES])
```

```python
import numpy as np
import jax
import jax.numpy as jnp
from jax import lax
from jax.experimental import pallas as pl
from jax.experimental.pallas import tpu as pltpu

F32 = jnp.float32
BF16 = jnp.bfloat16

D_MODEL = 1024
DEPTH = 2
HEAD_DIM = 64
CONV_CH = 256
CONV_WIDTH = 31
ATT_WIDTH = 384
N_HEADS = 6
DILATIONS = (1, 4, 16)
ATT_HALF = 64
ATT_BLOCK = 128
ALIBI_MAX_EXP = 8.0
MASK_VALUE = -1e30
REC_WIDTH = 384
REC_CHUNK = 64
F_TINY = 1e-30
D_FF = 2816
N_MOD = 6
EPS = 1e-6
G_CONV = (0, 512)
G_QKV = (512, 1664)
G_REC = (1664, 3584)
IN_COLS = 3584

ADAM_LR = 0.001
ADAM_B1 = 0.9
ADAM_B2 = 0.999
ADAM_EPS = 1e-08
ADAM_WD = 0.01
ADAM_STEP = 10

VMEM_LIMIT_BYTES = 56 * 1024 * 1024
LANES = 128
MESH = pl.DeviceIdType.MESH
ANY = pl.BlockSpec(memory_space=pl.ANY)


def _params(n_axes):
    return pltpu.CompilerParams(dimension_semantics=("arbitrary",) * n_axes,
                                vmem_limit_bytes=VMEM_LIMIT_BYTES)


def _tile(n, target):
    best = None
    for t in range(LANES, min(n, target) + 1, LANES):
        if n % t == 0:
            best = t
    return best or n


def _sigmoid(x):
    return jax.nn.sigmoid(x)


def _silu_grad(x):
    s = _sigmoid(x)
    return s * (1.0 + x * (1.0 - s))


MM_ACC_ELEMS = 1536 * 1024


def _matmul(a, b, mode, out_dtype, name, tm=1024, tn=1792, tk=1792):
    if mode == "nn":
        (m, k), (k2, n) = a.shape, b.shape
    elif mode == "nt":
        (m, k), (n, k2) = a.shape, b.shape
    else:
        (k, m), (k2, n) = a.shape, b.shape
    assert k == k2, (a.shape, b.shape, mode)
    tn, tk = _tile(n, tn), _tile(k, tk)
    tm = _tile(m, min(tm, MM_ACC_ELEMS // tn))
    nk = k // tk
    a_spec = (pl.BlockSpec((tk, tm), lambda i, j, kk: (kk, i)) if mode == "tn"
              else pl.BlockSpec((tm, tk), lambda i, j, kk: (i, kk)))
    b_spec = (pl.BlockSpec((tn, tk), lambda i, j, kk: (j, kk)) if mode == "nt"
              else pl.BlockSpec((tk, tn), lambda i, j, kk: (kk, j)))
    dims = {"nn": (((1,), (0,)), ((), ())), "nt": (((1,), (1,)), ((), ())),
            "tn": (((0,), (0,)), ((), ()))}[mode]

    def body(a_ref, b_ref, o_ref, *scratch):
        part = lax.dot_general(a_ref[...].astype(BF16), b_ref[...].astype(BF16), dims, preferred_element_type=F32)
        if nk == 1:
            o_ref[...] = part.astype(out_dtype)
            return
        acc_ref, = scratch
        kk = pl.program_id(2)

        @pl.when(kk == 0)
        def _():
            acc_ref[...] = part

        @pl.when(kk > 0)
        def _():
            acc_ref[...] += part

        @pl.when(kk == nk - 1)
        def _():
            o_ref[...] = acc_ref[...].astype(out_dtype)

    return pl.pallas_call(
        body, name=name, grid=(m // tm, n // tn, nk),
        in_specs=[a_spec, b_spec],
        out_specs=pl.BlockSpec((tm, tn), lambda i, j, kk: (i, j)),
        out_shape=jax.ShapeDtypeStruct((m, n), out_dtype),
        scratch_shapes=[pltpu.VMEM((tm, tn), F32)] if nk > 1 else [],
        compiler_params=pltpu.CompilerParams(dimension_semantics=("parallel", "parallel", "arbitrary"),
                                             vmem_limit_bytes=VMEM_LIMIT_BYTES),
    )(a, b)


def _matmul_two_lhs(a1, a2, b, out_dtype, name):
    (m, k1), n = a1.shape, b.shape[0]
    tn, tk = _tile(n, 1792), _tile(k1, 1792)
    tm = _tile(m, min(1024, MM_ACC_ELEMS // tn))
    nk1 = k1 // tk
    nk = 2 * nk1

    def body(a1_ref, a2_ref, b_ref, o_ref, acc_ref):
        kk = pl.program_id(2)
        lhs = jnp.where(kk < nk1, a1_ref[...], a2_ref[...])
        part = lax.dot_general(lhs, b_ref[...], (((1,), (1,)), ((), ())), preferred_element_type=F32)

        @pl.when(kk == 0)
        def _():
            acc_ref[...] = part

        @pl.when(kk > 0)
        def _():
            acc_ref[...] += part

        @pl.when(kk == nk - 1)
        def _():
            o_ref[...] = acc_ref[...].astype(out_dtype)

    return pl.pallas_call(
        body, name=name, grid=(m // tm, n // tn, nk),
        in_specs=[pl.BlockSpec((tm, tk), lambda i, j, kk: (i, jnp.minimum(kk, nk1 - 1))),
                  pl.BlockSpec((tm, tk), lambda i, j, kk: (i, jnp.maximum(kk - nk1, 0))),
                  pl.BlockSpec((tn, tk), lambda i, j, kk: (j, kk))],
        out_specs=pl.BlockSpec((tm, tn), lambda i, j, kk: (i, j)),
        out_shape=jax.ShapeDtypeStruct((m, n), out_dtype),
        scratch_shapes=[pltpu.VMEM((tm, tn), F32)],
        compiler_params=pltpu.CompilerParams(dimension_semantics=("parallel", "parallel", "arbitrary"),
                                             vmem_limit_bytes=VMEM_LIMIT_BYTES),
    )(a1, a2, b)


def _matmul_two_rhs(a, b1, b2, out_dtype, name):
    (k, m), n1 = a.shape, b1.shape[1]
    tn, tk = _tile(n1, 1792), _tile(k, 1792)
    tm = _tile(m, min(1024, MM_ACC_ELEMS // tn))
    nj1, nk = n1 // tn, k // tk

    def body(a_ref, b1_ref, b2_ref, o_ref, acc_ref):
        j, kk = pl.program_id(1), pl.program_id(2)
        rhs = jnp.where(j < nj1, b1_ref[...], b2_ref[...])
        part = lax.dot_general(a_ref[...], rhs, (((0,), (0,)), ((), ())), preferred_element_type=F32)

        @pl.when(kk == 0)
        def _():
            acc_ref[...] = part

        @pl.when(kk > 0)
        def _():
            acc_ref[...] += part

        @pl.when(kk == nk - 1)
        def _():
            o_ref[...] = acc_ref[...].astype(out_dtype)

    return pl.pallas_call(
        body, name=name, grid=(m // tm, 2 * nj1, nk),
        in_specs=[pl.BlockSpec((tk, tm), lambda i, j, kk: (kk, i)),
                  pl.BlockSpec((tk, tn), lambda i, j, kk: (jnp.where(j < nj1, kk, 0), jnp.minimum(j, nj1 - 1))),
                  pl.BlockSpec((tk, tn), lambda i, j, kk: (jnp.where(j < nj1, 0, kk), jnp.maximum(j - nj1, 0)))],
        out_specs=pl.BlockSpec((tm, tn), lambda i, j, kk: (i, j)),
        out_shape=jax.ShapeDtypeStruct((m, 2 * n1), out_dtype),
        scratch_shapes=[pltpu.VMEM((tm, tn), F32)],
        compiler_params=pltpu.CompilerParams(dimension_semantics=("parallel", "parallel", "arbitrary"),
                                             vmem_limit_bytes=VMEM_LIMIT_BYTES),
    )(a, b1, b2)


NORM_ROWS = 256


def _row_spec(width, rows=NORM_ROWS):
    return pl.BlockSpec((rows, width), lambda i: (i, 0))


def _vec_spec(width):
    return pl.BlockSpec((1, width), lambda i: (0, 0))


def _resid_norm_mod(x, r, g, nw, sc, sh, name):
    s, d = x.shape
    has_r = r is not None

    def body(*refs):
        if has_r:
            x_ref, r_ref, g_ref, nw_ref, sc_ref, sh_ref, xn_ref, h_ref = refs
            xn = x_ref[...] + g_ref[...] * r_ref[...].astype(F32)
            xn_ref[...] = xn
        else:
            x_ref, nw_ref, sc_ref, sh_ref, h_ref = refs
            xn = x_ref[...]
        rstd = lax.rsqrt(jnp.mean(xn * xn, axis=-1, keepdims=True) + EPS)
        y = xn * rstd * nw_ref[...]
        h_ref[...] = (y * (1.0 + sc_ref[...]) + sh_ref[...]).astype(BF16)

    if has_r:
        ins, in_specs = (x, r, g, nw, sc, sh), [_row_spec(d), _row_spec(d)] + [_vec_spec(d)] * 4
        out_shape = (jax.ShapeDtypeStruct((s, d), F32), jax.ShapeDtypeStruct((s, d), BF16))
        out_specs = (_row_spec(d), _row_spec(d))
    else:
        ins, in_specs = (x, nw, sc, sh), [_row_spec(d)] + [_vec_spec(d)] * 3
        out_shape = jax.ShapeDtypeStruct((s, d), BF16)
        out_specs = _row_spec(d)
    return pl.pallas_call(body, name=name, grid=(s // NORM_ROWS,), in_specs=in_specs, out_specs=out_specs,
                          out_shape=out_shape, compiler_params=_params(1))(*ins)


def _final_loss(x, r, g, fw, tgt, name):
    s, d = x.shape

    def body(x_ref, r_ref, g_ref, fw_ref, t_ref, loss_ref, dx_ref, dr_ref, dg_ref, dfw_ref):
        @pl.when(pl.program_id(0) == 0)
        def _():
            loss_ref[...] = jnp.zeros_like(loss_ref)
            dg_ref[...] = jnp.zeros_like(dg_ref)
            dfw_ref[...] = jnp.zeros_like(dfw_ref)

        rr = r_ref[...].astype(F32)
        gg = g_ref[...]
        xn = x_ref[...] + gg * rr
        rstd = lax.rsqrt(jnp.mean(xn * xn, axis=-1, keepdims=True) + EPS)
        xh = xn * rstd
        fwv = fw_ref[...]
        e = xh * fwv - t_ref[...]
        loss_ref[...] += 0.5 * jnp.sum(jnp.mean(e * e, axis=-1, keepdims=True), axis=0, keepdims=True)
        dy = e * (1.0 / d)
        dfw_ref[...] += jnp.sum(dy * xh, axis=0, keepdims=True)
        dxh = dy * fwv
        dx = rstd * (dxh - xh * jnp.mean(dxh * xh, axis=-1, keepdims=True))
        dx_ref[...] = dx
        dr_ref[...] = (gg * dx).astype(BF16)
        dg_ref[...] += jnp.sum(dx * rr, axis=0, keepdims=True)

    return pl.pallas_call(
        body, name=name, grid=(s // NORM_ROWS,),
        in_specs=[_row_spec(d), _row_spec(d), _vec_spec(d), _vec_spec(d), _row_spec(d)],
        out_specs=(_vec_spec(LANES), _row_spec(d), _row_spec(d), _vec_spec(d), _vec_spec(d)),
        out_shape=(jax.ShapeDtypeStruct((1, LANES), F32), jax.ShapeDtypeStruct((s, d), F32),
                   jax.ShapeDtypeStruct((s, d), BF16), jax.ShapeDtypeStruct((1, d), F32),
                   jax.ShapeDtypeStruct((1, d), F32)),
        compiler_params=_params(1))(x, r, g, fw, tgt)


def _norm_bwd(x, dhs, dxres, nw, sc, g, r, name):
    s, d = x.shape
    n_dh = len(dhs)
    has_g = g is not None

    def body(*refs):
        x_ref = refs[0]
        dh_refs = refs[1:1 + n_dh]
        dxres_ref, nw_ref, sc_ref = refs[1 + n_dh:4 + n_dh]
        pos = 4 + n_dh
        if has_g:
            g_ref, r_ref = refs[pos:pos + 2]
            pos += 2
            dx_ref, dr_ref, dsh_ref, dsc_ref, dnw_ref, dg_ref = refs[pos:]
            accs = (dsh_ref, dsc_ref, dnw_ref, dg_ref)
        else:
            dx_ref, dsh_ref, dsc_ref, dnw_ref = refs[pos:]
            accs = (dsh_ref, dsc_ref, dnw_ref)

        @pl.when(pl.program_id(0) == 0)
        def _():
            for acc in accs:
                acc[...] = jnp.zeros_like(acc)

        xv = x_ref[...]
        dh = dh_refs[0][...].astype(F32)
        for extra in dh_refs[1:]:
            dh = dh + extra[...].astype(F32)
        rstd = lax.rsqrt(jnp.mean(xv * xv, axis=-1, keepdims=True) + EPS)
        xh = xv * rstd
        nwv = nw_ref[...]
        dsh_ref[...] += jnp.sum(dh, axis=0, keepdims=True)
        dsc_ref[...] += jnp.sum(dh * (xh * nwv), axis=0, keepdims=True)
        dy = dh * (1.0 + sc_ref[...])
        dnw_ref[...] += jnp.sum(dy * xh, axis=0, keepdims=True)
        dxh = dy * nwv
        dx = dxres_ref[...] + rstd * (dxh - xh * jnp.mean(dxh * xh, axis=-1, keepdims=True))
        dx_ref[...] = dx
        if has_g:
            dr_ref[...] = (g_ref[...] * dx).astype(BF16)
            dg_ref[...] += jnp.sum(dx * r_ref[...].astype(F32), axis=0, keepdims=True)

    ins = [x, *dhs, dxres, nw, sc]
    in_specs = [_row_spec(d)] * (2 + n_dh) + [_vec_spec(d)] * 2
    out_shape = [jax.ShapeDtypeStruct((s, d), F32)]
    out_specs = [_row_spec(d)]
    if has_g:
        ins += [g, r]
        in_specs += [_vec_spec(d), _row_spec(d)]
        out_shape.append(jax.ShapeDtypeStruct((s, d), BF16))
        out_specs.append(_row_spec(d))
    n_vec = 4 if has_g else 3
    out_shape += [jax.ShapeDtypeStruct((1, d), F32)] * n_vec
    out_specs += [_vec_spec(d)] * n_vec
    return pl.pallas_call(body, name=name, grid=(s // NORM_ROWS,), in_specs=in_specs, out_specs=tuple(out_specs),
                          out_shape=tuple(out_shape), compiler_params=_params(1))(*ins)


FFN_ROWS = 256
FFN_COLS = 1408
HALO = 16
INV_SQRT2 = 0.7071067811865476
INV_SQRT_2PI = 0.3989422804014327


def _gelu(x):
    return 0.5 * x * (1.0 + lax.erf(x * INV_SQRT2))


def _gelu_grad(x):
    return 0.5 * (1.0 + lax.erf(x * INV_SQRT2)) + x * (INV_SQRT_2PI * jnp.exp(-0.5 * x * x))


def _halo_specs(rows, cols, halo, n_rows_total, col_of):
    per = rows // halo
    last = n_rows_total // halo - 1
    cur = pl.BlockSpec((rows, cols), lambda j, i: (i, col_of(j)))
    prev = pl.BlockSpec((halo, cols), lambda j, i: (jnp.maximum(i * per - 1, 0), col_of(j)))
    nxt = pl.BlockSpec((halo, cols), lambda j, i: (jnp.minimum((i + 1) * per, last), col_of(j)))
    return [prev, cur, nxt]


def _shift_rows(x, k):
    n = x.shape[0]
    return pltpu.roll(x, k % n, axis=0)


def _conv3(ext, w):
    return w[0:1, :] * _shift_rows(ext, 1) + w[1:2, :] * ext + w[2:3, :] * _shift_rows(ext, -1)


def _ext_block(prev_ref, cur_ref, next_ref, i, n_i):
    prev = jnp.where(i > 0, prev_ref[...].astype(F32), 0.0)
    nxt = jnp.where(i < n_i - 1, next_ref[...].astype(F32), 0.0)
    return jnp.concatenate([prev, cur_ref[...].astype(F32), nxt], axis=0)


def _ffn_act(u, cw, name):
    s = u.shape[0]
    nc, ns = D_FF // FFN_COLS, s // FFN_ROWS

    def body(gp, gc, gn, vp, vc, vn, wg_ref, wv_ref, o_ref, cg_ref, cv_ref):
        i = pl.program_id(1)
        cg = _conv3(_ext_block(gp, gc, gn, i, ns), wg_ref[...])[HALO:HALO + FFN_ROWS]
        cv = _conv3(_ext_block(vp, vc, vn, i, ns), wv_ref[...])[HALO:HALO + FFN_ROWS]
        o_ref[...] = (_gelu(cg) * cv).astype(BF16)
        cg_ref[...] = cg.astype(BF16)
        cv_ref[...] = cv.astype(BF16)

    in_specs = (_halo_specs(FFN_ROWS, FFN_COLS, HALO, s, lambda j: j)
                + _halo_specs(FFN_ROWS, FFN_COLS, HALO, s, lambda j: j + nc)
                + [pl.BlockSpec((3, FFN_COLS), lambda j, i: (0, j)),
                   pl.BlockSpec((3, FFN_COLS), lambda j, i: (0, j + nc))])
    blk = pl.BlockSpec((FFN_ROWS, FFN_COLS), lambda j, i: (i, j))
    return pl.pallas_call(
        body, name=name, grid=(nc, ns), in_specs=in_specs, out_specs=(blk, blk, blk),
        out_shape=(jax.ShapeDtypeStruct((s, D_FF), BF16),) * 3, compiler_params=_params(2),
    )(u, u, u, u, u, u, cw, cw)


def _ffn_act_bwd(u, cg, cv, dact, cw, name):
    s = u.shape[0]
    nc, ns = D_FF // FFN_COLS, s // FFN_ROWS

    def body(ug_ref, uv_ref, gp, gc, gn, vp, vc, vn, dp, dc, dn, wg_ref, wv_ref, dug_ref, duv_ref, dwg_ref, dwv_ref):
        i = pl.program_id(1)

        @pl.when(i == 0)
        def _():
            dwg_ref[...] = jnp.zeros_like(dwg_ref)
            dwv_ref[...] = jnp.zeros_like(dwv_ref)

        cge = _ext_block(gp, gc, gn, i, ns)
        cve = _ext_block(vp, vc, vn, i, ns)
        da = _ext_block(dp, dc, dn, i, ns)
        dcg = da * cve * _gelu_grad(cge)
        dcv = da * _gelu(cge)
        inner = slice(HALO, HALO + FFN_ROWS)
        for d_c, u_ref, w_ref, du_ref, dw_ref in ((dcg, ug_ref, wg_ref, dug_ref, dwg_ref),
                                                  (dcv, uv_ref, wv_ref, duv_ref, dwv_ref)):
            w = w_ref[...]
            d_next, d_prev = _shift_rows(d_c, -1), _shift_rows(d_c, 1)
            du = w[0:1, :] * d_next + w[1:2, :] * d_c + w[2:3, :] * d_prev
            du_ref[...] = du[inner].astype(BF16)
            u_in = u_ref[...].astype(F32)
            for tap, d_tap in enumerate((d_next, d_c, d_prev)):
                dw_ref[tap:tap + 1, :] += jnp.sum(d_tap[inner] * u_in, axis=0, keepdims=True)

    blk = pl.BlockSpec((FFN_ROWS, FFN_COLS), lambda j, i: (i, j))
    in_specs = ([blk, pl.BlockSpec((FFN_ROWS, FFN_COLS), lambda j, i: (i, j + nc))]
                + _halo_specs(FFN_ROWS, FFN_COLS, HALO, s, lambda j: j) * 3
                + [pl.BlockSpec((3, FFN_COLS), lambda j, i: (0, j)),
                   pl.BlockSpec((3, FFN_COLS), lambda j, i: (0, j + nc))])
    acc = pl.BlockSpec((HALO, FFN_COLS), lambda j, i: (0, j))
    return pl.pallas_call(
        body, name=name, grid=(nc, ns), in_specs=in_specs, out_specs=(blk, blk, acc, acc),
        out_shape=(jax.ShapeDtypeStruct((s, D_FF), BF16), jax.ShapeDtypeStruct((s, D_FF), BF16),
                   jax.ShapeDtypeStruct((HALO, D_FF), F32), jax.ShapeDtypeStruct((HALO, D_FF), F32)),
        compiler_params=_params(2),
    )(u, u, cg, cg, cg, cv, cv, cv, dact, dact, dact, cw, cw)


CONV_ROWS = 512
CONV_HALO = 16
CONV_PAD = CONV_WIDTH // 2


def _conv_halo_specs(cols, s):
    per = CONV_ROWS // CONV_HALO
    last = s // CONV_HALO - 1
    return [pl.BlockSpec((CONV_HALO, cols), lambda i: (jnp.maximum(i * per - 1, 0), 0)),
            pl.BlockSpec((CONV_ROWS, cols), lambda i: (i, 0)),
            pl.BlockSpec((CONV_HALO, cols), lambda i: (jnp.minimum((i + 1) * per, last), 0))]


def _glu_ext(pp, pc, pn, i, n_i):
    ext = _ext_block(pp, pc, pn, i, n_i)
    return ext[:, :CONV_CH] * _sigmoid(ext[:, CONV_CH:])


def _conv_mixer(pa, cw, cb, lnw, lnb, name):
    s = pa.shape[0]
    ns = s // CONV_ROWS

    def body(pp, pc, pn, cw_ref, cb_ref, lnw_ref, lnb_ref, o_ref, c_ref):
        i = pl.program_id(0)
        a = _glu_ext(pp, pc, pn, i, ns)
        acc = jnp.zeros((CONV_ROWS, CONV_CH), F32)
        for tap in range(CONV_WIDTH):
            acc = acc + cw_ref[tap:tap + 1, :] * _shift_rows(a, -(tap + 1))[:CONV_ROWS]
        cv = acc + cb_ref[...]
        c_ref[...] = cv
        mu = jnp.mean(cv, axis=-1, keepdims=True)
        xc = cv - mu
        rstd = lax.rsqrt(jnp.mean(xc * xc, axis=-1, keepdims=True) + EPS)
        y = xc * rstd * lnw_ref[...] + lnb_ref[...]
        o_ref[...] = (y * _sigmoid(y)).astype(BF16)

    vec = pl.BlockSpec((1, CONV_CH), lambda i: (0, 0))
    blk = pl.BlockSpec((CONV_ROWS, CONV_CH), lambda i: (i, 0))
    return pl.pallas_call(
        body, name=name, grid=(ns,),
        in_specs=_conv_halo_specs(2 * CONV_CH, s) + [pl.BlockSpec((CONV_WIDTH, CONV_CH), lambda i: (0, 0)), vec, vec, vec],
        out_specs=(blk, blk),
        out_shape=(jax.ShapeDtypeStruct((s, CONV_CH), BF16), jax.ShapeDtypeStruct((s, CONV_CH), F32)),
        compiler_params=_params(1))(pa, pa, pa, cw, cb, lnw, lnb)


def _conv_mixer_bwd_ln(cv, dout, lnw, lnb, name):
    s = cv.shape[0]

    def body(c_ref, do_ref, lnw_ref, lnb_ref, dc_ref, dlnw_ref, dlnb_ref, dcb_ref):
        @pl.when(pl.program_id(0) == 0)
        def _():
            dlnw_ref[...] = jnp.zeros_like(dlnw_ref)
            dlnb_ref[...] = jnp.zeros_like(dlnb_ref)
            dcb_ref[...] = jnp.zeros_like(dcb_ref)

        c = c_ref[...]
        mu = jnp.mean(c, axis=-1, keepdims=True)
        xc = c - mu
        rstd = lax.rsqrt(jnp.mean(xc * xc, axis=-1, keepdims=True) + EPS)
        xh = xc * rstd
        w = lnw_ref[...]
        y = xh * w + lnb_ref[...]
        dy = do_ref[...] * _silu_grad(y)
        dlnw_ref[...] += jnp.sum(dy * xh, axis=0, keepdims=True)
        dlnb_ref[...] += jnp.sum(dy, axis=0, keepdims=True)
        dxh = dy * w
        dc = rstd * (dxh - jnp.mean(dxh, axis=-1, keepdims=True) - xh * jnp.mean(dxh * xh, axis=-1, keepdims=True))
        dc_ref[...] = dc
        dcb_ref[...] += jnp.sum(dc, axis=0, keepdims=True)

    vec = pl.BlockSpec((1, CONV_CH), lambda i: (0, 0))
    blk = pl.BlockSpec((CONV_ROWS, CONV_CH), lambda i: (i, 0))
    return pl.pallas_call(
        body, name=name, grid=(s // CONV_ROWS,), in_specs=[blk, blk, vec, vec], out_specs=(blk, vec, vec, vec),
        out_shape=(jax.ShapeDtypeStruct((s, CONV_CH), F32),) + (jax.ShapeDtypeStruct((1, CONV_CH), F32),) * 3,
        compiler_params=_params(1))(cv, dout, lnw, lnb)


def _conv_mixer_bwd_conv(pa, dc, cw, name):
    s = pa.shape[0]
    ns = s // CONV_ROWS

    def body(pc, dp, dcc, dn, cw_ref, dpa_ref, dcw_ref):
        i = pl.program_id(0)

        @pl.when(i == 0)
        def _():
            dcw_ref[...] = jnp.zeros_like(dcw_ref)

        cur = pc[...]
        val, sg = cur[:, :CONV_CH], _sigmoid(cur[:, CONV_CH:])
        a_cur = val * sg
        dce = _ext_block(dp, dcc, dn, i, ns)
        da = jnp.zeros((CONV_ROWS, CONV_CH), F32)
        for tap in range(CONV_WIDTH):
            shifted = _shift_rows(dce, -(CONV_WIDTH - tap))[:CONV_ROWS]
            da = da + cw_ref[tap:tap + 1, :] * shifted
            dcw_ref[tap:tap + 1, :] += jnp.sum(shifted * a_cur, axis=0, keepdims=True)
        dpa_ref[:, :CONV_CH] = (da * sg).astype(BF16)
        dpa_ref[:, CONV_CH:] = (da * val * sg * (1.0 - sg)).astype(BF16)

    return pl.pallas_call(
        body, name=name, grid=(ns,),
        in_specs=[pl.BlockSpec((CONV_ROWS, 2 * CONV_CH), lambda i: (i, 0))] + _conv_halo_specs(CONV_CH, s)
        + [pl.BlockSpec((CONV_WIDTH, CONV_CH), lambda i: (0, 0))],
        out_specs=(pl.BlockSpec((CONV_ROWS, 2 * CONV_CH), lambda i: (i, 0)),
                   pl.BlockSpec((32, CONV_CH), lambda i: (0, 0))),
        out_shape=(jax.ShapeDtypeStruct((s, 2 * CONV_CH), BF16), jax.ShapeDtypeStruct((32, CONV_CH), F32)),
        compiler_params=_params(1))(pa, dc, dc, dc, cw)


SLOPES = tuple(float(2.0 ** (-ALIBI_MAX_EXP * (h + 1) / N_HEADS)) for h in range(N_HEADS))
ATT_SCALE = HEAD_DIM ** -0.5


PAIR = 2 * HEAD_DIM
N_PAIRS = N_HEADS // 2
ATT_WIN = ATT_BLOCK + 2 * ATT_HALF


ATT_GROUPS = {1: 4, 4: 1, 16: 1}


def _window_specs(dil, n_steps, col_of):
    per = 2 * ATT_GROUPS[dil]
    rows, halo = ATT_BLOCK * dil * ATT_GROUPS[dil], ATT_HALF * dil
    return [pl.BlockSpec((halo, PAIR), lambda i, p: (jnp.maximum(per * i - 1, 0), col_of(p))),
            pl.BlockSpec((rows, PAIR), lambda i, p: (i, col_of(p))),
            pl.BlockSpec((halo, PAIR), lambda i, p: (jnp.minimum(per * (i + 1), per * n_steps - 1), col_of(p)))]


def _residue(ref, r, n, dil, start=0):
    return ref[pl.ds(start * dil + r, n, stride=dil), :] if dil > 1 else ref[pl.ds(start + r, n), :]


def _store_residue(ref, r, dil, start, val):
    if dil > 1:
        ref[pl.ds(start * dil + r, val.shape[0], stride=dil), :] = val
    else:
        ref[pl.ds(start + r, val.shape[0]), :] = val


def _residue_window(refs, r, dil, g=0):
    prev, cur, nxt = refs
    groups = ATT_GROUPS[dil]
    lo = max(g * ATT_BLOCK - ATT_HALF, 0)
    hi = min((g + 1) * ATT_BLOCK + ATT_HALF, groups * ATT_BLOCK)
    parts = [_residue(prev, r, ATT_HALF, dil)] if g == 0 else []
    parts.append(_residue(cur, r, hi - lo, dil, lo))
    if g == groups - 1:
        parts.append(_residue(nxt, r, ATT_HALF, dil))
    return jnp.concatenate(parts, axis=0)


def _band_masks(i, length, dil, transposed):
    shape = (ATT_WIN, ATT_BLOCK) if transposed else (ATT_BLOCK, ATT_WIN)
    row = lax.broadcasted_iota(jnp.int32, shape, 0)
    col = lax.broadcasted_iota(jnp.int32, shape, 1)
    wide = row if transposed else col
    dist = jnp.abs((row - col - ATT_HALF) if transposed else (row + ATT_HALF - col))
    wpos = i * ATT_BLOCK - ATT_HALF + wide
    valid = (dist <= ATT_HALF) & (wpos >= 0) & (wpos < length)
    return valid, dist.astype(F32) * float(dil)


def _attn_branch(qkv, dil, name):
    s = qkv.shape[0]
    groups = ATT_GROUPS[dil]
    rows = ATT_BLOCK * dil * groups
    n_steps = s // rows
    length = s // dil
    nt = (((1,), (1,)), ((), ()))

    def body(q_ref, kp, kc, kn, vp, vc, vn, o_ref, l_ref):
        i, pair = pl.program_id(0), pl.program_id(1)
        items = [(g, r) for g in range(groups) for r in range(dil)]
        q = jnp.stack([_residue(q_ref, r, ATT_BLOCK, dil, g * ATT_BLOCK) for g, r in items]).astype(BF16)
        k = jnp.stack([_residue_window((kp, kc, kn), r, dil, g) for g, r in items]).astype(BF16)
        v = jnp.stack([_residue_window((vp, vc, vn), r, dil, g) for g, r in items]).astype(BF16)
        per_group = [_band_masks(i * groups + g, length, dil, False) for g in range(groups)]
        valid = jnp.stack([per_group[g][0] for g, _ in items]) if groups > 1 else per_group[0][0][None]
        distf = jnp.stack([per_group[g][1] for g, _ in items]) if groups > 1 else per_group[0][1][None]
        outs, lses = [], []
        for hh in range(2):
            sl = slice(hh * HEAD_DIM, (hh + 1) * HEAD_DIM)
            slope = jnp.where(pair == 0, SLOPES[hh], jnp.where(pair == 1, SLOPES[2 + hh], SLOPES[4 + hh]))
            sc = jnp.einsum("bqd,bkd->bqk", q[:, :, sl], k[:, :, sl], preferred_element_type=F32) * ATT_SCALE
            sc = jnp.where(valid, sc - slope * distf, MASK_VALUE)
            m = jnp.max(sc, axis=-1, keepdims=True)
            p = jnp.exp(sc - m)
            den = jnp.sum(p, axis=-1, keepdims=True)
            outs.append(jnp.einsum("bqk,bkd->bqd", p.astype(BF16), v[:, :, sl], preferred_element_type=F32) / den)
            lses.append(jnp.broadcast_to(m + jnp.log(den), (len(items), ATT_BLOCK, HEAD_DIM)))
        o_all, l_all = jnp.concatenate(outs, axis=2), jnp.concatenate(lses, axis=2)
        for n, (g, r) in enumerate(items):
            _store_residue(o_ref, r, dil, g * ATT_BLOCK, o_all[n])
            _store_residue(l_ref, r, dil, g * ATT_BLOCK, l_all[n])

    out_blk = pl.BlockSpec((rows, PAIR), lambda i, p: (i, p))
    return pl.pallas_call(
        body, name=name, grid=(n_steps, N_PAIRS),
        in_specs=[pl.BlockSpec((rows, PAIR), lambda i, p: (i, p))]
        + _window_specs(dil, n_steps, lambda p: N_PAIRS + p) + _window_specs(dil, n_steps, lambda p: 2 * N_PAIRS + p),
        out_specs=(out_blk, out_blk),
        out_shape=(jax.ShapeDtypeStruct((s, ATT_WIDTH), F32),) * 2,
        compiler_params=_params(2))(qkv, qkv, qkv, qkv, qkv, qkv, qkv)


ATT_ROWS = 512


def _attn_combine(outs, lses, name):
    s = outs[0].shape[0]

    def body(o1, o2, o3, l1, l2, l3, att_ref, att32_ref, lse_ref):
        ls = [l1[...], l2[...], l3[...]]
        m = jnp.maximum(jnp.maximum(ls[0], ls[1]), ls[2])
        es = [jnp.exp(l - m) for l in ls]
        den = es[0] + es[1] + es[2]
        att = (es[0] * o1[...] + es[1] * o2[...] + es[2] * o3[...]) / den
        att_ref[...] = att.astype(BF16)
        att32_ref[...] = att
        lse_ref[...] = m + jnp.log(den)

    blk = pl.BlockSpec((ATT_ROWS, ATT_WIDTH), lambda i: (i, 0))
    return pl.pallas_call(
        body, name=name, grid=(s // ATT_ROWS,), in_specs=[blk] * 6, out_specs=(blk, blk, blk),
        out_shape=(jax.ShapeDtypeStruct((s, ATT_WIDTH), BF16), jax.ShapeDtypeStruct((s, ATT_WIDTH), F32),
                   jax.ShapeDtypeStruct((s, ATT_WIDTH), F32)),
        compiler_params=_params(1))(*outs, *lses)


def _attn_delta(datt, att, name):
    s = att.shape[0]

    def body(d_ref, a_ref, delta_ref):
        prod = d_ref[...] * a_ref[...]
        for h in range(N_HEADS):
            sl = slice(h * HEAD_DIM, (h + 1) * HEAD_DIM)
            delta_ref[:, sl] = jnp.broadcast_to(jnp.sum(prod[:, sl], axis=-1, keepdims=True), (ATT_ROWS, HEAD_DIM))

    blk = pl.BlockSpec((ATT_ROWS, ATT_WIDTH), lambda i: (i, 0))
    return pl.pallas_call(
        body, name=name, grid=(s // ATT_ROWS,), in_specs=[blk, blk], out_specs=blk,
        out_shape=jax.ShapeDtypeStruct((s, ATT_WIDTH), F32), compiler_params=_params(1))(datt, att)


def _attn_branch_bwd(qkv, do, lse, delta, prev, dil, out_dtype, name):
    s = qkv.shape[0]
    groups = ATT_GROUPS[dil]
    rows = ATT_BLOCK * dil * groups
    n_steps = s // rows
    length = s // dil
    has_prev = prev is not None
    tn = (((0,), (0,)), ((), ()))
    nt = (((1,), (1,)), ((), ()))

    def body(*refs):
        qs, ks, vs, dos, ls, des = (refs[3 * n:3 * n + 3] for n in range(6))
        rest = refs[18:]
        if has_prev:
            pq, pk, pv = rest[:3]
            rest = rest[3:]
        dq_ref, dk_ref, dv_ref = rest
        i, pair = pl.program_id(0), pl.program_id(1)
        items = [(g, r) for g in range(groups) for r in range(dil)]
        cur = lambda t: jnp.stack([_residue(t[1], r, ATT_BLOCK, dil, g * ATT_BLOCK) for g, r in items])
        win = lambda t: jnp.stack([_residue_window(t, r, dil, g) for g, r in items])
        q_cur, k_cur, v_cur, do_cur = (cur(t).astype(BF16) for t in (qs, ks, vs, dos))
        q_win, k_win, v_win, do_win = (win(t).astype(BF16) for t in (qs, ks, vs, dos))
        l_cur, de_cur, l_win, de_win = cur(ls), cur(des), win(ls), win(des)

        def masks(transposed):
            per_group = [_band_masks(i * groups + g, length, dil, transposed) for g in range(groups)]
            if groups == 1:
                return per_group[0][0][None], per_group[0][1][None]
            return jnp.stack([per_group[g][0] for g, _ in items]), jnp.stack([per_group[g][1] for g, _ in items])

        valid_q, distf_q = masks(False)
        valid_k, distf_k = masks(True)
        dot = lambda eq, a, b: jnp.einsum(eq, a, b, preferred_element_type=F32)
        dqs, dks, dvs = [], [], []
        for hh in range(2):
            sl = slice(hh * HEAD_DIM, (hh + 1) * HEAD_DIM)
            one = slice(hh * HEAD_DIM, hh * HEAD_DIM + 1)
            slope = jnp.where(pair == 0, SLOPES[hh], jnp.where(pair == 1, SLOPES[2 + hh], SLOPES[4 + hh]))
            sc = dot("bqd,bkd->bqk", q_cur[:, :, sl], k_win[:, :, sl]) * ATT_SCALE - slope * distf_q
            p = jnp.exp(jnp.where(valid_q, sc - l_cur[:, :, one], MASK_VALUE))
            dp = dot("bqd,bkd->bqk", do_cur[:, :, sl], v_win[:, :, sl])
            ds = (p * (dp - de_cur[:, :, one]) * ATT_SCALE).astype(BF16)
            dqs.append(dot("bqk,bkd->bqd", ds, k_win[:, :, sl]))

            sc2 = dot("bqd,bkd->bqk", q_win[:, :, sl], k_cur[:, :, sl]) * ATT_SCALE - slope * distf_k
            p2 = jnp.exp(jnp.where(valid_k, sc2 - l_win[:, :, one], MASK_VALUE))
            dvs.append(dot("bqk,bqd->bkd", p2.astype(BF16), do_win[:, :, sl]))
            dp2 = dot("bqd,bkd->bqk", do_win[:, :, sl], v_cur[:, :, sl])
            ds2 = (p2 * (dp2 - de_win[:, :, one]) * ATT_SCALE).astype(BF16)
            dks.append(dot("bqk,bqd->bkd", ds2, q_win[:, :, sl]))
        for parts, acc, out in ((dqs, pq if has_prev else None, dq_ref), (dks, pk if has_prev else None, dk_ref),
                                (dvs, pv if has_prev else None, dv_ref)):
            val = jnp.concatenate(parts, axis=2)
            for n, (g, r) in enumerate(items):
                piece = val[n]
                if has_prev:
                    piece = piece + _residue(acc, r, ATT_BLOCK, dil, g * ATT_BLOCK)
                _store_residue(out, r, dil, g * ATT_BLOCK, piece.astype(out_dtype))

    blk = pl.BlockSpec((rows, PAIR), lambda i, p: (i, p))
    in_specs = (_window_specs(dil, n_steps, lambda p: p) + _window_specs(dil, n_steps, lambda p: N_PAIRS + p)
                + _window_specs(dil, n_steps, lambda p: 2 * N_PAIRS + p) + _window_specs(dil, n_steps, lambda p: p) * 3)
    ins = [qkv] * 9 + [do] * 3 + [lse] * 3 + [delta] * 3
    if has_prev:
        in_specs += [blk] * 3
        ins += list(prev)
    return pl.pallas_call(
        body, name=name, grid=(n_steps, N_PAIRS), in_specs=in_specs, out_specs=(blk, blk, blk),
        out_shape=(jax.ShapeDtypeStruct((s, ATT_WIDTH), out_dtype),) * 3,
        compiler_params=_params(2))(*ins)


TB = 2 * REC_CHUNK
REC_ROWS = 5 * REC_WIDTH


REC_LEVELS = 6


def _scan_pos(p, rev):
    p = p & (REC_CHUNK - 1)
    return (REC_CHUNK - 1 - p) if rev else p


def _split3(x):
    hi = x.astype(BF16)
    rest = x - hi.astype(F32)
    mid = rest.astype(BF16)
    return hi, mid, (rest - mid.astype(F32)).astype(BF16)


def _chunk_sums(x, rev, with_levels):
    row = lax.broadcasted_iota(jnp.int32, (TB, TB), 0)
    col = lax.broadcasted_iota(jnp.int32, (TB, TB), 1)
    same = (row < REC_CHUNK) == (col < REC_CHUNK)
    s_row, s_col = _scan_pos(row, rev), _scan_pos(col, rev)
    mats = [same & (s_row <= s_col)]
    if with_levels:
        for level in range(1, REC_LEVELS + 1):
            shift = REC_LEVELS + 1 - level
            boundary = ((s_col >> shift) << shift) + (REC_CHUNK >> level) - 1
            mats.append(same & (s_row <= boundary))
        mats.append(same)
    cat = jnp.concatenate([m.astype(BF16) for m in mats], axis=1)
    total = sum(jnp.dot(term, cat, preferred_element_type=F32) for term in _split3(x))
    return [total[:, n * TB:(n + 1) * TB] for n in range(len(mats))]


def _hg_prep(qraw, z, lb, rev):
    lane = lax.broadcasted_iota(jnp.int32, (REC_WIDTH, TB), 1)
    in_a = lane < REC_CHUNK
    scan = _scan_pos(lane, rev)
    sig, sigm = _sigmoid(z), _sigmoid(-z)
    f = lb + (1.0 - lb) * sig
    kk = (1.0 - lb) * sigm
    sums = _chunk_sums(jnp.log(jnp.maximum(f, F_TINY)), rev, True)
    b, bend = sums[0], sums[-1]
    q = qraw * _sigmoid(qraw)
    eq, ek = [], []
    for level in range(1, REC_LEVELS + 1):
        r = sums[level]
        e = jnp.exp(jnp.minimum(b - r, r - b))
        second = ((scan >> (REC_LEVELS - level)) & 1) == 1
        eq.append(jnp.where(second, e, 0.0))
        ek.append(jnp.where(second, 0.0, e))
    lanes_end = (0, REC_CHUNK) if rev else (REC_CHUNK - 1, TB - 1)
    end_a, end_b = (b[:, n:n + 1] for n in lanes_end)
    return dict(in_a=in_a, sig=sig, sigm=sigm, f=f, kk=kk, b=b, end_a=end_a, end_b=end_b,
                q=q, qh=q * jnp.exp(b), kh=kk * jnp.exp(bend - b), ekb=jnp.exp(bend - b), eq=eq, ek=ek)


def _level_masks(rev):
    row = lax.broadcasted_iota(jnp.int32, (TB, TB), 0)
    col = lax.broadcasted_iota(jnp.int32, (TB, TB), 1)
    same = (row < REC_CHUNK) == (col < REC_CHUNK)
    s_row, s_col = _scan_pos(row, rev), _scan_pos(col, rev)
    masks = [same & ((s_row >> (REC_LEVELS + 1 - level)) == (s_col >> (REC_LEVELS + 1 - level)))
             for level in range(1, REC_LEVELS + 1)]
    return masks, row == col


def _head_rows(x, h):
    return x[h * HEAD_DIM:(h + 1) * HEAD_DIM, :]


def _block_diag_mask():
    r = lax.broadcasted_iota(jnp.int32, (REC_WIDTH, REC_WIDTH), 0) // HEAD_DIM
    c = lax.broadcasted_iota(jnp.int32, (REC_WIDTH, REC_WIDTH), 1) // HEAD_DIM
    return (r == c).astype(F32)


def _heads(x):
    return x.reshape(N_HEADS, HEAD_DIM, TB)


def _hgrn_scan(projt, lb, rev, name):
    s = projt.shape[1]
    nblk = s // TB
    zrow = 2 if rev else 1
    tmap = (lambda i: nblk - 1 - i) if rev else (lambda i: i)
    tn = (((0,), (0,)), ((), ()))
    nt = (((1,), (1,)), ((), ()))

    def body(q_ref, z_ref, v_ref, lb_ref, o_ref, hs_ref, at_ref, h_ref):
        @pl.when(pl.program_id(0) == 0)
        def _():
            h_ref[...] = jnp.zeros_like(h_ref)

        v = v_ref[...]
        vb = v.astype(BF16)
        pr = _hg_prep(q_ref[...], z_ref[...], lb_ref[...], rev)
        q, kk = pr["q"], pr["kk"]
        masks, diag = _level_masks(rev)
        own = jnp.sum(_heads(q * kk), axis=1, keepdims=True)
        sc = jnp.where(diag[None], own, 0.0)
        for level in range(REC_LEVELS):
            qt = _heads((q * pr["eq"][level]).astype(BF16))
            kt = _heads((kk * pr["ek"][level]).astype(BF16))
            sc = sc + jnp.where(masks[level][None],
                                jnp.einsum("hks,hkt->hst", kt, qt, preferred_element_type=F32), 0.0)
        a_bf = sc.astype(BF16)
        at_ref[...] = a_bf
        o = jnp.einsum("hvs,hst->hvt", _heads(vb), a_bf, preferred_element_type=F32).reshape(REC_WIDTH, TB)
        bd_mask = _block_diag_mask()
        order = ((1, ~pr["in_a"], pr["end_b"]), (0, pr["in_a"], pr["end_a"]))
        if not rev:
            order = order[::-1]
        for slot, msk, bend in order:
            h0 = h_ref[...]
            hs_ref[slot] = h0
            o = o + lax.dot_general(h0.astype(BF16), jnp.where(msk, pr["qh"], 0.0).astype(BF16), tn,
                                    preferred_element_type=F32)
            upd = lax.dot_general(jnp.where(msk, pr["kh"], 0.0).astype(BF16), vb, nt, preferred_element_type=F32)
            h_ref[...] = jnp.exp(bend) * h0 + upd * bd_mask
        o_ref[...] = o

    row_blk = lambda r: pl.BlockSpec((REC_WIDTH, TB), lambda i: (r, tmap(i)))
    return pl.pallas_call(
        body, name=name, grid=(nblk,),
        in_specs=[row_blk(0), row_blk(zrow), row_blk(3), pl.BlockSpec((REC_WIDTH, 1), lambda i: (0, 0))],
        out_specs=(pl.BlockSpec((REC_WIDTH, TB), lambda i: (0, tmap(i))),
                   pl.BlockSpec((2, REC_WIDTH, REC_WIDTH), lambda i: (tmap(i), 0, 0)),
                   pl.BlockSpec((None, N_HEADS, TB, TB), lambda i: (tmap(i), 0, 0, 0))),
        out_shape=(jax.ShapeDtypeStruct((REC_WIDTH, s), F32),
                   jax.ShapeDtypeStruct((s // REC_CHUNK, REC_WIDTH, REC_WIDTH), F32),
                   jax.ShapeDtypeStruct((nblk, N_HEADS, TB, TB), BF16)),
        scratch_shapes=[pltpu.VMEM((REC_WIDTH, REC_WIDTH), F32)],
        compiler_params=_params(1))(projt, projt, projt, lb)


def _hgrn_scan_bwd(projt, lb, dot, hs, at, prev, rev, name):
    s = projt.shape[1]
    nblk = s // TB
    zrow = 2 if rev else 1
    tmap = (lambda i: i) if rev else (lambda i: nblk - 1 - i)
    has_prev = prev is not None
    tn = (((0,), (0,)), ((), ()))
    nt = (((1,), (1,)), ((), ()))

    def body(*refs):
        q_ref, z_ref, v_ref, lb_ref, do_ref, hs_ref, at_ref = refs[:7]
        rest = refs[7:]
        if has_prev:
            pq_ref, pv_ref = rest[:2]
            rest = rest[2:]
        dq_ref, dz_ref, dv_ref, dlb_ref, dh_ref = rest

        @pl.when(pl.program_id(0) == 0)
        def _():
            dh_ref[...] = jnp.zeros_like(dh_ref)
            dlb_ref[...] = jnp.zeros_like(dlb_ref)

        qraw, v, do, lbv = q_ref[...], v_ref[...], do_ref[...], lb_ref[...]
        dob, vb = do.astype(BF16), v.astype(BF16)
        pr = _hg_prep(qraw, z_ref[...], lbv, rev)
        q, kk, b, in_a = pr["q"], pr["kk"], pr["b"], pr["in_a"]
        masks, diag = _level_masks(rev)
        dot = lambda eq, x, y: jnp.einsum(eq, x, y, preferred_element_type=F32)
        d_at = dot("hvs,hvt->hst", _heads(vb), _heads(dob))
        dv = dot("hvt,hst->hvs", _heads(dob), at_ref[...]).reshape(REC_WIDTH, TB)
        d_own = jnp.sum(jnp.where(diag[None], d_at, 0.0), axis=1, keepdims=True)
        dq_in = (d_own * _heads(kk)).reshape(REC_WIDTH, TB)
        dk_in = (d_own * _heads(q)).reshape(REC_WIDTH, TB)
        db_in = jnp.zeros((REC_WIDTH, TB), F32)
        for lv in range(REC_LEVELS):
            d_lv = jnp.where(masks[lv][None], d_at, 0.0).astype(BF16)
            q_lv, k_lv = (q * pr["eq"][lv]).astype(BF16), (kk * pr["ek"][lv]).astype(BF16)
            dqt = dot("hks,hst->hkt", _heads(k_lv), d_lv).reshape(REC_WIDTH, TB)
            dkt = dot("hkt,hst->hks", _heads(q_lv), d_lv).reshape(REC_WIDTH, TB)
            dq_in = dq_in + pr["eq"][lv] * dqt
            dk_in = dk_in + pr["ek"][lv] * dkt
            db_in = db_in + q_lv.astype(F32) * dqt - k_lv.astype(F32) * dkt
        dq = dk = jnp.zeros((REC_WIDTH, TB), F32)

        zero = jnp.zeros((REC_WIDTH, TB), F32)
        bd_mask = _block_diag_mask()
        eb = jnp.exp(b)
        const = zero
        order = ((0, in_a, pr["end_a"]), (1, ~in_a, pr["end_b"]))
        if not rev:
            order = order[::-1]
        for slot, msk, bend in order:
            h0 = hs_ref[slot]
            dh1 = dh_ref[...]
            dh1b = dh1.astype(BF16)
            dq = dq + eb * jnp.dot(h0.astype(BF16), jnp.where(msk, do, 0.0).astype(BF16), preferred_element_type=F32)
            dv = dv + lax.dot_general(dh1b, jnp.where(msk, pr["kh"], 0.0).astype(BF16), tn, preferred_element_type=F32)
            dk_int = pr["ekb"] * jnp.dot(dh1b, jnp.where(msk, v, 0.0).astype(BF16), preferred_element_type=F32)
            dk = dk + dk_int
            ebend = jnp.exp(bend)
            c = (jnp.sum(kk * dk_int, axis=1, keepdims=True)
                 + ebend * jnp.sum(h0 * dh1, axis=1, keepdims=True))
            const = const + jnp.where(msk, c, 0.0)
            upd = lax.dot_general(jnp.where(msk, pr["qh"], 0.0).astype(BF16), dob, nt, preferred_element_type=F32)
            dh_ref[...] = ebend * dh1 + upd * bd_mask

        dg = _chunk_sums(db_in + q * dq - kk * dk, not rev, False)[0] + const
        dq, dk = dq + dq_in, dk + dk_in
        sig, sigm, f = pr["sig"], pr["sigm"], pr["f"]
        live = f > F_TINY
        inv_f = 1.0 / jnp.maximum(f, F_TINY)
        one_lb = 1.0 - lbv
        dz = sig * sigm * one_lb * (jnp.where(live, dg * inv_f, 0.0) - dk)
        dlb_ref[...] += jnp.sum(sigm * (jnp.where(live, dg * inv_f, 0.0) - dk), axis=1, keepdims=True)
        dqr = dq * _silu_grad(qraw)
        if has_prev:
            dqr = dqr + pq_ref[...]
            dv = dv + pv_ref[...]
        dq_ref[...] = dqr
        dz_ref[...] = dz
        dv_ref[...] = dv

    row_blk = lambda r: pl.BlockSpec((REC_WIDTH, TB), lambda i: (r, tmap(i)))
    blk = pl.BlockSpec((REC_WIDTH, TB), lambda i: (0, tmap(i)))
    col = pl.BlockSpec((REC_WIDTH, 1), lambda i: (0, 0))
    in_specs = [row_blk(0), row_blk(zrow), row_blk(3), col, blk,
                pl.BlockSpec((2, REC_WIDTH, REC_WIDTH), lambda i: (tmap(i), 0, 0)),
                pl.BlockSpec((None, N_HEADS, TB, TB), lambda i: (tmap(i), 0, 0, 0))]
    ins = [projt, projt, projt, lb, dot, hs, at]
    if has_prev:
        in_specs += [blk, blk]
        ins += list(prev)
    t_shape = jax.ShapeDtypeStruct((REC_WIDTH, s), F32)
    return pl.pallas_call(
        body, name=name, grid=(nblk,), in_specs=in_specs, out_specs=(blk, blk, blk, col),
        out_shape=(t_shape, t_shape, t_shape, jax.ShapeDtypeStruct((REC_WIDTH, 1), F32)),
        scratch_shapes=[pltpu.VMEM((REC_WIDTH, REC_WIDTH), F32)],
        compiler_params=_params(1))(*ins)


REC_OUT_COLS = 512


def _head_rms(o):
    o3 = o.reshape(N_HEADS, HEAD_DIM, o.shape[1])
    rstd = lax.rsqrt(jnp.mean(o3 * o3, axis=1, keepdims=True) + EPS)
    return o3 * rstd, rstd


def _hgrn_out(of, ob, projt, wn, name):
    s = of.shape[1]

    def body(of_ref, ob_ref, g_ref, wn_ref, o_ref):
        on, _ = _head_rms(of_ref[...] + ob_ref[...])
        g = g_ref[...]
        y = on.reshape(REC_WIDTH, REC_OUT_COLS) * wn_ref[...] * (g * _sigmoid(g))
        o_ref[...] = y.T.astype(BF16)

    blk = pl.BlockSpec((REC_WIDTH, REC_OUT_COLS), lambda i: (0, i))
    return pl.pallas_call(
        body, name=name, grid=(s // REC_OUT_COLS,),
        in_specs=[blk, blk, pl.BlockSpec((REC_WIDTH, REC_OUT_COLS), lambda i: (4, i)),
                  pl.BlockSpec((REC_WIDTH, 1), lambda i: (0, 0))],
        out_specs=pl.BlockSpec((REC_OUT_COLS, REC_WIDTH), lambda i: (i, 0)),
        out_shape=jax.ShapeDtypeStruct((s, REC_WIDTH), BF16), compiler_params=_params(1))(of, ob, projt, wn)


def _hgrn_out_bwd(drec, of, ob, projt, wn, name):
    s = of.shape[1]

    def body(d_ref, of_ref, ob_ref, g_ref, wn_ref, do_ref, dg_ref, dwn_ref):
        @pl.when(pl.program_id(0) == 0)
        def _():
            dwn_ref[...] = jnp.zeros_like(dwn_ref)

        dy = d_ref[...].T
        on3, rstd = _head_rms(of_ref[...] + ob_ref[...])
        on = on3.reshape(REC_WIDTH, REC_OUT_COLS)
        g, wnv = g_ref[...], wn_ref[...]
        dg_ref[...] = dy * on * wnv * _silu_grad(g)
        d_onw = dy * (g * _sigmoid(g))
        dwn_ref[...] += jnp.sum(d_onw * on, axis=1, keepdims=True)
        d_on3 = (d_onw * wnv).reshape(N_HEADS, HEAD_DIM, REC_OUT_COLS)
        do3 = rstd * (d_on3 - on3 * jnp.mean(d_on3 * on3, axis=1, keepdims=True))
        do_ref[...] = do3.reshape(REC_WIDTH, REC_OUT_COLS)

    blk = pl.BlockSpec((REC_WIDTH, REC_OUT_COLS), lambda i: (0, i))
    col = pl.BlockSpec((REC_WIDTH, 1), lambda i: (0, 0))
    t_shape = jax.ShapeDtypeStruct((REC_WIDTH, s), F32)
    return pl.pallas_call(
        body, name=name, grid=(s // REC_OUT_COLS,),
        in_specs=[pl.BlockSpec((REC_OUT_COLS, REC_WIDTH), lambda i: (i, 0)), blk, blk,
                  pl.BlockSpec((REC_WIDTH, REC_OUT_COLS), lambda i: (4, i)), col],
        out_specs=(blk, blk, col),
        out_shape=(t_shape, t_shape, jax.ShapeDtypeStruct((REC_WIDTH, 1), F32)),
        compiler_params=_params(1))(drec, of, ob, projt, wn)


def _lower_bounds(gamma, name):
    def body(g_ref, lb_ref, p_ref):
        g0, g1 = g_ref[0:1, :], g_ref[1:2, :]
        m = jnp.maximum(g0, g1)
        e0, e1 = jnp.exp(g0 - m), jnp.exp(g1 - m)
        p0, p1 = e0 / (e0 + e1), e1 / (e0 + e1)
        lb_ref[...] = (p0 + p1) - p0
        p_ref[0:1, :] = p0
        p_ref[1:2, :] = p1

    n = gamma.shape[1]
    return pl.pallas_call(body, name=name,
                          out_shape=(jax.ShapeDtypeStruct((1, n), F32), jax.ShapeDtypeStruct((2, n), F32)))(gamma)


def _lower_bounds_bwd(dlb1, p, name):
    def body(d_ref, p_ref, o_ref):
        p0, p1, d = p_ref[0:1, :], p_ref[1:2, :], d_ref[...]
        inner = p1 * d
        o_ref[0:1, :] = p0 * (0.0 - inner)
        o_ref[1:2, :] = p1 * (d - inner)

    return pl.pallas_call(body, name=name, out_shape=jax.ShapeDtypeStruct(p.shape, F32))(dlb1, p)


def _split_w_in(w_in):
    return dict(conv=w_in[:, G_CONV[0]:G_CONV[1]], qkv=w_in[:, G_QKV[0]:G_QKV[1]],
                rec_t=w_in[:, G_REC[0]:].T, nat=w_in[:, :G_REC[0]])


def _split_w_rest(w_out, w_up, w_down):
    return dict(out=w_out, out_a=w_out[:CONV_CH], out_b=w_out[CONV_CH:CONV_CH + ATT_WIDTH],
                out_c=w_out[CONV_CH + ATT_WIDTH:], up=w_up, down=w_down)


def _col(v):
    return v.reshape(-1, 1)


def _sequence_step(x, tgt, mods, lbs, small, later_weights, final_w, on_layer_grads):
    saved = []
    xin = x
    h1 = _resid_norm_mod(x, None, None, small[0]["norm1_w"], mods[0][1:2], mods[0][0:1], "norm1_first")
    big = [_split_w_in(later_weights(0, h1)), None]
    for l in range(DEPTH):
        sm, w, md = small[l], big[l], mods[l]
        pa = _matmul(h1, w["conv"], "nn", F32, f"proj_conv")
        qkv = _matmul(h1, w["qkv"], "nn", F32, f"proj_qkv")
        projt = _matmul(w["rec_t"], h1, "nt", F32, f"proj_rec")
        a_out, cv = _conv_mixer(pa, sm["conv_a_w"], sm["conv_a_b"], sm["ln_a_w"], sm["ln_a_b"], f"conv_mixer")
        outs, lses = zip(*[_attn_branch(qkv, d, f"attn_d{d}") for d in DILATIONS])
        att, att32, lse = _attn_combine(outs, lses, f"attn_combine")
        lb_f, lb_b = _col(lbs[l][0]), _col(lbs[l][1])
        of, hsf, atf = _hgrn_scan(projt, lb_f, False, "hgrn_fwd")
        ob, hsb, atb = _hgrn_scan(projt, lb_b, True, "hgrn_rev")
        wn = _col(sm["rec_norm_w"])
        rec = _hgrn_out(of, ob, projt, wn, f"hgrn_out")
        mixed = jnp.concatenate([a_out, att, rec], axis=1)
        if l == 0:
            w_out_all = later_weights(1, rec)
            big[0].update(_split_w_rest(w_out_all[0], None, None))
        r1 = _matmul(mixed, w["out"], "nn", BF16, "out_proj")
        xmid, h2 = _resid_norm_mod(xin, r1, md[2:3], sm["norm2_w"], md[4:5], md[3:4], f"norm2")
        if l == 0:
            w_in1, w_up_all, w_down_all = later_weights(2, h2)
            big[0].update(up=w_up_all[0], down=w_down_all[0])
            big[1] = dict(_split_w_in(w_in1), **_split_w_rest(w_out_all[1], w_up_all[1], w_down_all[1]))
        u = _matmul(h2, w["up"], "nn", BF16, f"ffn_up")
        act, conv_g, conv_v = _ffn_act(u, sm["conv_f_w"], "ffn_act")
        r2 = _matmul(act, w["down"], "nn", BF16, "ffn_down")
        saved.append(dict(xin=xin, h1=h1, pa=pa, qkv=qkv, projt=projt, cv=cv, att32=att32, lse=lse, of=of, ob=ob,
                          hsf=hsf, hsb=hsb, atf=atf, atb=atb, lb_f=lb_f, lb_b=lb_b, wn=wn, mixed=mixed, r1=r1, xmid=xmid, h2=h2,
                          u=u, conv_g=conv_g, conv_v=conv_v, act=act, r2=r2))
        if l + 1 < DEPTH:
            nxt = small[l + 1]
            xin, h1 = _resid_norm_mod(xmid, r2, md[5:6], nxt["norm1_w"], mods[l + 1][1:2], mods[l + 1][0:1],
                                      "norm1")
    top = saved[-1]
    loss, dx, dr2, dg2, dfw = _final_loss(top["xmid"], top["r2"], mods[-1][5:6], final_w, tgt, "final_loss")

    grads = [None] * DEPTH
    order_after = None
    for l in reversed(range(DEPTH)):
        sm, w, md, sv = small[l], big[l], mods[l], saved[l]
        dact = _matmul(dr2, w["down"], "nt", BF16, f"d_act")
        g_down = _matmul(dr2, sv["act"], "tn", F32, "dw_down").T
        conv_f_w = sm["conv_f_w"] if order_after is None else sm["conv_f_w"] + order_after
        dug, duv, dwg, dwv = _ffn_act_bwd(sv["u"], sv["conv_g"], sv["conv_v"], dact, conv_f_w, "ffn_act_bwd")
        dh2 = _matmul_two_lhs(dug, duv, w["up"], BF16, "d_h2")
        g_up = _matmul_two_rhs(sv["h2"], dug, duv, F32, "dw_up")
        after_ffn = on_layer_grads(l, {2: g_up, 3: g_down}, False)
        norm2_w = sm["norm2_w"] if after_ffn is None else sm["norm2_w"] + after_ffn
        dxmid, dr1, dsh2, dsc2, dnw2, dg1 = _norm_bwd(sv["xmid"], [dh2], dx, norm2_w, md[4:5], md[2:3], sv["r1"],
                                                     f"norm2_bwd")
        dmix_a = _matmul(dr1, w["out_a"], "nt", F32, f"d_mix_a")
        dmix_b = _matmul(dr1, w["out_b"], "nt", F32, f"d_mix_b")
        dmix_c = _matmul(dr1, w["out_c"], "nt", F32, f"d_mix_c")
        g_out = _matmul(sv["mixed"], dr1, "tn", F32, f"dw_out")
        dc, dlnw, dlnb, dcb = _conv_mixer_bwd_ln(sv["cv"], dmix_a, sm["ln_a_w"], sm["ln_a_b"], f"conv_mixer_bwd_ln")
        dpa, dcw = _conv_mixer_bwd_conv(sv["pa"], dc, sm["conv_a_w"], f"conv_mixer_bwd_conv")
        delta = _attn_delta(dmix_b, sv["att32"], "attn_delta")
        dqkv = None
        for d in reversed(DILATIONS):
            dqkv = _attn_branch_bwd(sv["qkv"], dmix_b, sv["lse"], delta, dqkv, d, BF16 if d == 1 else F32,
                                    f"attn_bwd_d{d}")
        dot, dgt, dwn = _hgrn_out_bwd(dmix_c, sv["of"], sv["ob"], sv["projt"], sv["wn"], f"hgrn_out_bwd")
        dqf, dzf, dvf, dlbf = _hgrn_scan_bwd(sv["projt"], sv["lb_f"], dot, sv["hsf"], sv["atf"], None, False,
                                             "hgrn_fwd_bwd")
        dqt, dzb, dvt, dlbb = _hgrn_scan_bwd(sv["projt"], sv["lb_b"], dot, sv["hsb"], sv["atb"], (dqf, dvf), True,
                                             "hgrn_rev_bwd")
        dprojt = jnp.concatenate([dqt, dzf, dzb, dvt, dgt], axis=0).astype(BF16)
        dnat = jnp.concatenate([dpa, *dqkv], axis=1)
        dh1_a = _matmul(dnat, w["nat"], "nt", BF16, "d_h1_nat")
        dh1_b = _matmul(dprojt, w["rec_t"], "tn", BF16, "d_h1_rec")
        g_in_nat = _matmul(sv["h1"], dnat, "tn", F32, f"dw_in_nat")
        g_in_rec_t = _matmul(dprojt, sv["h1"], "nn", F32, f"dw_in_rec")
        g_in = jnp.concatenate([g_in_nat, g_in_rec_t.T], axis=1)
        if l > 0:
            below = saved[l - 1]
            dx, dr2, dsh1, dsc1, dnw1, dg2_below = _norm_bwd(sv["xin"], [dh1_a, dh1_b], dxmid, sm["norm1_w"], md[1:2],
                                                            mods[l - 1][5:6], below["r2"], f"norm1_bwd")
        else:
            dx, dsh1, dsc1, dnw1 = _norm_bwd(sv["xin"], [dh1_a, dh1_b], dxmid, sm["norm1_w"], md[1:2], None, None,
                                             f"norm1_bwd")
        grads[l] = dict(w_in=g_in, w_out=g_out, w_up=g_up, w_down=g_down,
                        mod=[dsh1, dsc1, dg1, dsh2, dsc2, dg2], norm1_w=dnw1, conv_a_w=dcw[:CONV_WIDTH], conv_a_b=dcb,
                        ln_a_w=dlnw, ln_a_b=dlnb, lb=jnp.concatenate([dlbf.reshape(1, -1), dlbb.reshape(1, -1)], axis=0),
                        rec_norm_w=dwn.reshape(1, -1), norm2_w=dnw2,
                        conv_f_w=jnp.concatenate([dwg[:3], dwv[:3]], axis=1))
        order_after = on_layer_grads(l, {0: g_in, 1: g_out}, True)
        if l > 0:
            dg2 = dg2_below
    return loss[0, 0], dx, grads, dfw


def _adamw_math(w, g, m, v):
    m = ADAM_B1 * m + (1.0 - ADAM_B1) * g
    v = ADAM_B2 * v + (1.0 - ADAM_B2) * (g * g)
    m_hat = m / (1.0 - ADAM_B1 ** ADAM_STEP)
    v_hat = v / (1.0 - ADAM_B2 ** ADAM_STEP)
    delta = -ADAM_LR * (m_hat / (jnp.sqrt(v_hat) + ADAM_EPS) + ADAM_WD * w)
    return delta, m, v


def _row_tile(rows, cols, max_elems=384 * 1024):
    best = None
    for t in range(8, rows + 1, 8):
        if rows % t == 0 and t * cols <= max_elems:
            best = t
    return best or rows


def _adamw(w, g, m, v, name):
    nl, r, c = w.shape
    tr = _row_tile(r, c)

    def body(w_ref, g_ref, m_ref, v_ref, d_ref, m2_ref, v2_ref):
        d_ref[...], m2_ref[...], v2_ref[...] = _adamw_math(w_ref[...], g_ref[...], m_ref[...], v_ref[...])

    blk = pl.BlockSpec((None, tr, c), lambda l, i: (l, i, 0))
    shape = jax.ShapeDtypeStruct((nl, r, c), F32)
    return pl.pallas_call(body, name=name, grid=(nl, r // tr), in_specs=[blk] * 4, out_specs=(blk, blk, blk),
                          out_shape=(shape, shape, shape), compiler_params=_params(2))(w, g, m, v)


ADA_SHARD = N_MOD * D_MODEL // 4
ADA_COLS = 512
ADA_ROWS = 256
HIGHEST = lax.Precision.HIGHEST


def _ada_mod(c_all, w_ada, b_sh, name):
    def body(c_ref, w_ref, b_ref, o_ref):
        cv = c_ref[...]
        o_ref[...] = jnp.dot(cv * _sigmoid(cv), w_ref[...], precision=HIGHEST, preferred_element_type=F32) + b_ref[...]

    return pl.pallas_call(
        body, name=name, grid=(DEPTH, ADA_SHARD // ADA_COLS),
        in_specs=[pl.BlockSpec((8, D_MODEL), lambda l, j: (0, 0)),
                  pl.BlockSpec((None, D_MODEL, ADA_COLS), lambda l, j: (l, 0, j)),
                  pl.BlockSpec((None, 1, ADA_COLS), lambda l, j: (l, 0, j))],
        out_specs=pl.BlockSpec((None, 8, ADA_COLS), lambda l, j: (l, 0, j)),
        out_shape=jax.ShapeDtypeStruct((DEPTH, 8, ADA_SHARD), F32), compiler_params=_params(2))(c_all, w_ada, b_sh)


def _ada_update(c_all, dmod_sh, w, m, v, name):
    def body(c_ref, d_ref, w_ref, m_ref, v_ref, g_ref, dl_ref, m2_ref, v2_ref):
        cv = c_ref[...]
        g = lax.dot_general(cv * _sigmoid(cv), d_ref[...], (((0,), (0,)), ((), ())), precision=HIGHEST,
                            preferred_element_type=F32)
        g_ref[...] = g
        dl_ref[...], m2_ref[...], v2_ref[...] = _adamw_math(w_ref[...], g, m_ref[...], v_ref[...])

    blk = pl.BlockSpec((None, ADA_ROWS, ADA_SHARD), lambda l, i: (l, i, 0))
    shape = jax.ShapeDtypeStruct((DEPTH, D_MODEL, ADA_SHARD), F32)
    return pl.pallas_call(
        body, name=name, grid=(DEPTH, D_MODEL // ADA_ROWS),
        in_specs=[pl.BlockSpec((8, ADA_ROWS), lambda l, i: (0, i)),
                  pl.BlockSpec((None, 8, ADA_SHARD), lambda l, i: (l, 0, 0)), blk, blk, blk],
        out_specs=(blk,) * 4, out_shape=(shape,) * 4, compiler_params=_params(2))(c_all, dmod_sh, w, m, v)


def _sum_devices(packs, name):
    def body(p_ref, o_ref):
        acc = p_ref[0]
        for dev in range(1, 8):
            acc = acc + p_ref[dev]
        o_ref[...] = acc

    return pl.pallas_call(body, name=name, out_shape=jax.ShapeDtypeStruct(packs.shape[1:], F32))(packs)


def _mesh_pos():
    return lax.axis_index("x"), lax.axis_index("y"), lax.axis_index("c")


def _flip(v, bit):
    return 1 - v if bit else v


def _allgather_devices(x, name):
    m_per, n = x.shape

    def body(x_ref, out_ref, send_sems, recv_sems, local_sem):
        ix, iy, ic = _mesh_pos()
        me, sibling = (ix, iy, ic), (ix, iy, 1 - ic)
        chips = [(1 - ix, iy), (ix, 1 - iy), (1 - ix, 1 - iy)]

        def rows(px, py, pc):
            return out_ref.at[pl.ds((4 * px + 2 * py + pc) * m_per, m_per), :]

        def copy(k, block, to, src=None):
            return pltpu.make_async_remote_copy(
                src_ref=rows(*block) if src is None else src, dst_ref=rows(*block),
                send_sem=send_sems.at[k], recv_sem=recv_sems.at[k], device_id=to, device_id_type=MESH)

        mine = pltpu.make_async_copy(x_ref, rows(*me), local_sem)
        mine.start()
        first = [copy(0, me, sibling, src=x_ref)]
        first += [copy(1 + j, me, (*chip, ic), src=x_ref) for j, chip in enumerate(chips)]
        for cp in first:
            cp.start()
        passed = [copy(4 + j, (*chip, ic), sibling) for j, chip in enumerate(chips)]
        for j, chip in enumerate(chips):
            copy(1 + j, (*chip, ic), me).wait_recv()
            passed[j].start()
        copy(0, sibling, me).wait_recv()
        for j, chip in enumerate(chips):
            copy(4 + j, (*chip, 1 - ic), me).wait_recv()
        for cp in first + passed:
            cp.wait_send()
        mine.wait()

    return pl.pallas_call(
        body, name=name, out_shape=jax.ShapeDtypeStruct((8 * m_per, n), x.dtype),
        in_specs=[pl.BlockSpec(memory_space=pltpu.VMEM)], out_specs=pl.BlockSpec(memory_space=pltpu.VMEM),
        scratch_shapes=[pltpu.SemaphoreType.DMA((7,)), pltpu.SemaphoreType.DMA((7,)), pltpu.SemaphoreType.DMA],
    )(x)


def _gather_chips(shards, name):
    n = len(shards)

    def body(*refs):
        ins, outs = refs[:n], refs[n:2 * n]
        send_sems, recv_sems, local_sems = refs[2 * n:]
        ix, iy, ic = _mesh_pos()
        me = 2 * ix + iy
        local = [pltpu.make_async_copy(ins[a], outs[a].at[me], local_sems.at[a]) for a in range(n)]
        for cp in local:
            cp.start()
        remote = []
        for a in range(n):
            for k in (1, 2, 3):
                px, py = _flip(ix, k & 2), _flip(iy, k & 1)
                sems = dict(send_sem=send_sems.at[3 * a + k - 1], recv_sem=recv_sems.at[3 * a + k - 1],
                            device_id=(px, py, ic), device_id_type=MESH)
                out_cp = pltpu.make_async_remote_copy(src_ref=ins[a], dst_ref=outs[a].at[me], **sems)
                in_cp = pltpu.make_async_remote_copy(src_ref=ins[a], dst_ref=outs[a].at[2 * px + py], **sems)
                out_cp.start()
                remote.append((out_cp, in_cp))
        for out_cp, in_cp in remote:
            out_cp.wait_send()
            in_cp.wait_recv()
        for cp in local:
            cp.wait()

    return pl.pallas_call(
        body, name=name, in_specs=[ANY] * n, out_specs=tuple([ANY] * n),
        out_shape=tuple(jax.ShapeDtypeStruct((4,) + t.shape, t.dtype) for t in shards),
        scratch_shapes=[pltpu.SemaphoreType.DMA((3 * n,)), pltpu.SemaphoreType.DMA((3 * n,)),
                        pltpu.SemaphoreType.DMA((n,))],
    )(*shards)


HBM = pl.BlockSpec(memory_space=pltpu.HBM)
SEM = pl.BlockSpec(memory_space=pltpu.SEMAPHORE)
DATAFLOW = pltpu.SideEffectType.DATAFLOW_SIDE_EFFECTING


def _peer_chip(ix, iy, k):
    return _flip(ix, k & 2), _flip(iy, k & 1)


def _gather_chips_start(shards, name):
    n = len(shards)

    def body(*refs):
        src, land = refs[:n], refs[n:2 * n]
        send_sems, recv_sems = refs[2 * n], refs[2 * n + 1]
        token = refs[-1]
        ix, iy, ic = _mesh_pos()
        me = 2 * ix + iy
        for a in range(n):
            for k in (1, 2, 3):
                px, py = _peer_chip(ix, iy, k)
                pltpu.make_async_remote_copy(
                    src_ref=src[a], dst_ref=land[a].at[me], send_sem=send_sems.at[3 * a + k - 1],
                    recv_sem=recv_sems.at[3 * a + k - 1], device_id=(px, py, ic), device_id_type=MESH).start()
        token[...] = jnp.zeros_like(token)

    hbm = lambda shape, dtype: pltpu.HBM(shape, dtype)
    operands = ([pltpu.with_memory_space_constraint(t, pltpu.HBM) for t in shards]
                + [pltpu.with_memory_space_constraint(lax.empty((4,) + t.shape, t.dtype), pltpu.HBM) for t in shards])
    return pl.pallas_call(
        body, name=name,
        out_shape=(pltpu.SemaphoreType.DMA((3 * n,)), pltpu.SemaphoreType.DMA((3 * n,)),
                   *[hbm(t.shape, t.dtype) for t in shards], *[hbm((4,) + t.shape, t.dtype) for t in shards],
                   jax.ShapeDtypeStruct((8, LANES), F32)),
        in_specs=(HBM,) * (2 * n),
        out_specs=(SEM, SEM) + (HBM,) * (2 * n) + (pl.BlockSpec(memory_space=pltpu.VMEM),),
        input_output_aliases={a: 2 + a for a in range(2 * n)},
        compiler_params=pltpu.CompilerParams(has_side_effects=DATAFLOW),
    )(*operands)


def _gather_chips_wait(started, which, after, name):
    send_sems, recv_sems = started[0], started[1]
    thru = started[2:-1]
    n = len(thru) // 2

    def body(*refs):
        src, land = refs[:n], refs[n:2 * n]
        send_sems, recv_sems = refs[2 * n], refs[2 * n + 1]
        ix, iy, ic = _mesh_pos()
        for a in which:
            for k in (1, 2, 3):
                px, py = _peer_chip(ix, iy, k)
                cp = pltpu.make_async_remote_copy(
                    src_ref=src[a], dst_ref=land[a].at[2 * px + py], send_sem=send_sems.at[3 * a + k - 1],
                    recv_sem=recv_sems.at[3 * a + k - 1], device_id=(px, py, ic), device_id_type=MESH)
                cp.wait_send()
                cp.wait_recv()

    outs = pl.pallas_call(
        body, name=name,
        out_shape=tuple(pltpu.HBM(t.shape, t.dtype) for t in thru),
        in_specs=(HBM,) * (2 * n) + (SEM, SEM, ANY), out_specs=(HBM,) * (2 * n),
        input_output_aliases={a: a for a in range(2 * n)},
        compiler_params=pltpu.CompilerParams(has_side_effects=DATAFLOW),
    )(*thru, send_sems, recv_sems, after)
    return outs[:n], outs[n:]


BIG_KINDS = (("w_in", "col", D_MODEL, IN_COLS), ("w_out", "row", D_MODEL, D_MODEL),
             ("w_up", "col", D_MODEL, 2 * D_FF), ("w_down", "row", D_FF, D_MODEL))


def _piece_shape(how, r, c):
    return (r // 2, c // 4) if how == "col" else (r // 8, c)


def _aligned(start, multiple):
    return start if isinstance(start, int) else pl.multiple_of(start, multiple)


def _piece(ref, how, r, c, chip, half):
    if how == "col":
        return ref.at[pl.ds(_aligned(half * (r // 2), 8), r // 2), pl.ds(_aligned(chip * (c // 4), LANES), c // 4)]
    n = r // 4
    return ref.at[pl.ds(_aligned(chip * n + half * (n // 2), 8), n // 2), :]


def _rs_pair_exchange(grads, kinds, name):
    nk = len(kinds)
    specs = [BIG_KINDS[ki] for ki in kinds]
    nl = len(grads[0])
    flat = [grads[ki][l] for ki in range(nk) for l in range(nl)]
    per = nl * 4

    def body(*refs):
        g, land = refs[:nk * nl], refs[nk * nl:nk * nl + nk]
        send_sems, recv_sems = refs[nk * nl + nk:]
        ix, iy, ic = _mesh_pos()
        sibling = (ix, iy, 1 - ic)
        copies = []
        for ki, (_, how, r, c) in enumerate(specs):
            for l in range(nl):
                for j in range(4):
                    sem = ki * per + l * 4 + j
                    rem = pltpu.make_async_remote_copy(
                        src_ref=_piece(g[ki * nl + l], how, r, c, j, 1 - ic), dst_ref=land[ki].at[l, j],
                        send_sem=send_sems.at[sem], recv_sem=recv_sems.at[sem], device_id=sibling, device_id_type=MESH)
                    rem.start()
                    copies.append(rem)
        for rem in copies:
            rem.wait_send()
            rem.wait_recv()

    shapes = [jax.ShapeDtypeStruct((nl, 4) + _piece_shape(how, r, c), F32) for _, how, r, c in specs]
    return pl.pallas_call(
        body, name=name, in_specs=[ANY] * len(flat), out_specs=tuple([ANY] * nk), out_shape=tuple(shapes),
        scratch_shapes=[pltpu.SemaphoreType.DMA((nk * per,))] * 2,
    )(*flat)


def _pair_sum(g, theirs, layer, how, core, name):
    r, c = g.shape
    pr, pc = _piece_shape(how, r, c)
    if how == "col":
        mine_spec = pl.BlockSpec((pr, pc), lambda j, core_ref: (core_ref[0], j))
    else:
        mine_spec = pl.BlockSpec((pr, pc), lambda j, core_ref: (2 * j + core_ref[0], 0))

    def body(core_ref, g_ref, t_ref, o_ref, ob_ref):
        total = g_ref[...] + t_ref[...]
        o_ref[...] = total
        ob_ref[...] = total.astype(BF16)

    out_blk = pl.BlockSpec((None, pr, pc), lambda j, core_ref: (j, 0, 0))
    return pl.pallas_call(
        body, name=name,
        grid_spec=pltpu.PrefetchScalarGridSpec(
            num_scalar_prefetch=1, grid=(4,),
            in_specs=[mine_spec, pl.BlockSpec((None, None, pr, pc), lambda j, core_ref: (layer, j, 0, 0))],
            out_specs=(out_blk, out_blk)),
        out_shape=(jax.ShapeDtypeStruct((4, pr, pc), F32), jax.ShapeDtypeStruct((4, pr, pc), BF16)),
        compiler_params=_params(1))(core, g, theirs)


def _rs_chip_exchange(pair_sums, name):
    nk = len(pair_sums)
    nl = len(pair_sums[0])
    flat = [pair_sums[ki][l] for ki in range(nk) for l in range(nl)]

    def body(*refs):
        src, dst = refs[:nk * nl], refs[nk * nl:nk * nl + nk]
        send_sems, recv_sems = refs[nk * nl + nk:]
        ix, iy, ic = _mesh_pos()
        copies = []
        for ki in range(nk):
            for l in range(nl):
                for k in (1, 2, 3):
                    px, py = _peer_chip(ix, iy, k)
                    sem = (ki * nl + l) * 3 + k - 1
                    rem = pltpu.make_async_remote_copy(
                        src_ref=src[ki * nl + l].at[2 * px + py], dst_ref=dst[ki].at[l, k - 1],
                        send_sem=send_sems.at[sem], recv_sem=recv_sems.at[sem], device_id=(px, py, ic), device_id_type=MESH)
                    rem.start()
                    copies.append(rem)
        for rem in copies:
            rem.wait_send()
            rem.wait_recv()

    return pl.pallas_call(
        body, name=name, in_specs=[ANY] * len(flat), out_specs=tuple([ANY] * nk),
        out_shape=tuple(jax.ShapeDtypeStruct((nl, 3) + pair_sums[ki][0].shape[1:], pair_sums[ki][0].dtype)
                        for ki in range(nk)),
        scratch_shapes=[pltpu.SemaphoreType.DMA((nk * nl * 3,))] * 2,
    )(*flat)


def _rs_chip_start(pieces, name):
    n = len(pieces)

    def body(*refs):
        src, land = refs[:n], refs[n:2 * n]
        send_sems, recv_sems = refs[2 * n], refs[2 * n + 1]
        token = refs[-1]
        ix, iy, ic = _mesh_pos()
        for a in range(n):
            for k in (1, 2, 3):
                px, py = _peer_chip(ix, iy, k)
                pltpu.make_async_remote_copy(
                    src_ref=src[a].at[2 * px + py], dst_ref=land[a].at[k - 1], send_sem=send_sems.at[3 * a + k - 1],
                    recv_sem=recv_sems.at[3 * a + k - 1], device_id=(px, py, ic), device_id_type=MESH).start()
        token[...] = jnp.zeros_like(token)

    land_shape = lambda t: (3,) + t.shape[1:]
    operands = ([pltpu.with_memory_space_constraint(t, pltpu.HBM) for t in pieces]
                + [pltpu.with_memory_space_constraint(lax.empty(land_shape(t), t.dtype), pltpu.HBM) for t in pieces])
    return pl.pallas_call(
        body, name=name,
        out_shape=(pltpu.SemaphoreType.DMA((3 * n,)), pltpu.SemaphoreType.DMA((3 * n,)),
                   *[pltpu.HBM(t.shape, t.dtype) for t in pieces], *[pltpu.HBM(land_shape(t), t.dtype) for t in pieces],
                   jax.ShapeDtypeStruct((8, LANES), F32)),
        in_specs=(HBM,) * (2 * n),
        out_specs=(SEM, SEM) + (HBM,) * (2 * n) + (pl.BlockSpec(memory_space=pltpu.VMEM),),
        input_output_aliases={a: 2 + a for a in range(2 * n)},
        compiler_params=pltpu.CompilerParams(has_side_effects=DATAFLOW),
    )(*operands)


def _rs_chip_wait(started, after, name):
    send_sems, recv_sems = started[0], started[1]
    thru = started[2:-1]
    n = len(thru) // 2

    def body(*refs):
        src, land = refs[:n], refs[n:2 * n]
        send_sems, recv_sems = refs[2 * n], refs[2 * n + 1]
        ix, iy, ic = _mesh_pos()
        for a in range(n):
            for k in (1, 2, 3):
                px, py = _peer_chip(ix, iy, k)
                cp = pltpu.make_async_remote_copy(
                    src_ref=src[a].at[2 * px + py], dst_ref=land[a].at[k - 1], send_sem=send_sems.at[3 * a + k - 1],
                    recv_sem=recv_sems.at[3 * a + k - 1], device_id=(px, py, ic), device_id_type=MESH)
                cp.wait_send()
                cp.wait_recv()

    outs = pl.pallas_call(
        body, name=name,
        out_shape=tuple(pltpu.HBM(t.shape, t.dtype) for t in thru),
        in_specs=(HBM,) * (2 * n) + (SEM, SEM, ANY), out_specs=(HBM,) * (2 * n),
        input_output_aliases={a: a for a in range(2 * n)},
        compiler_params=pltpu.CompilerParams(has_side_effects=DATAFLOW),
    )(*thru, send_sems, recv_sems, after)
    return outs[n:]


def _chip_sum(own, others, chip, name):
    _, pr, pc = own.shape

    def body(chip_ref, own_ref, s1, s2, s3, o_ref):
        o_ref[...] = ((own_ref[...] + s1[...].astype(F32)) + s2[...].astype(F32)) + s3[...].astype(F32)

    slot = lambda k: pl.BlockSpec((None, pr, pc), lambda i, chip_ref: (k, 0, 0))
    return pl.pallas_call(
        body, name=name,
        grid_spec=pltpu.PrefetchScalarGridSpec(
            num_scalar_prefetch=1, grid=(1,),
            in_specs=[pl.BlockSpec((None, pr, pc), lambda i, chip_ref: (chip_ref[0], 0, 0)), slot(0), slot(1), slot(2)],
            out_specs=pl.BlockSpec((pr, pc), lambda i, chip_ref: (0, 0))),
        out_shape=jax.ShapeDtypeStruct((pr, pc), F32), compiler_params=_params(1))(chip, own, others, others, others)


def _rs_pair_share(halves, name):
    nk = len(halves)
    flat = [halves[ki][l] for ki in range(nk) for l in range(DEPTH)]

    def body(*refs):
        src, dst = refs[:nk * DEPTH], refs[nk * DEPTH:nk * DEPTH + nk]
        send_sems, recv_sems = refs[nk * DEPTH + nk:]
        ix, iy, ic = _mesh_pos()
        copies = []
        for ki in range(nk):
            for l in range(DEPTH):
                sem = ki * DEPTH + l
                rem = pltpu.make_async_remote_copy(
                    src_ref=src[sem], dst_ref=dst[ki].at[l], send_sem=send_sems.at[sem], recv_sem=recv_sems.at[sem],
                    device_id=(ix, iy, 1 - ic), device_id_type=MESH)
                rem.start()
                copies.append(rem)
        for rem in copies:
            rem.wait_send()
            rem.wait_recv()

    return pl.pallas_call(
        body, name=name, in_specs=[ANY] * len(flat), out_specs=tuple([ANY] * nk),
        out_shape=tuple(jax.ShapeDtypeStruct((DEPTH,) + halves[ki][0].shape, F32) for ki in range(nk)),
        scratch_shapes=[pltpu.SemaphoreType.DMA((nk * DEPTH,))] * 2,
    )(*flat)


def _adamw_halves(w, mine, theirs, m, v, core, name):
    nl, pr, pc = theirs.shape
    shape = w.shape
    view = lambda t: t.reshape(nl, 2, pr, pc)
    tr = _row_tile(pr, pc, 256 * 1024)

    def body(core_ref, w_ref, a0_ref, a1_ref, t_ref, m_ref, v_ref, g_ref, d_ref, m2_ref, v2_ref):
        own = jnp.where(pl.program_id(0) == 0, a0_ref[...], a1_ref[...])
        g = jnp.where(pl.program_id(1) == core_ref[0], own, t_ref[...])
        g_ref[...] = g
        d_ref[...], m2_ref[...], v2_ref[...] = _adamw_math(w_ref[...], g, m_ref[...], v_ref[...])

    blk = pl.BlockSpec((None, None, tr, pc), lambda l, h, i, core_ref: (l, h, i, 0))
    own_blk = pl.BlockSpec((tr, pc), lambda l, h, i, core_ref: (i, 0))
    out = jax.ShapeDtypeStruct((nl, 2, pr, pc), F32)
    outs = pl.pallas_call(
        body, name=name,
        grid_spec=pltpu.PrefetchScalarGridSpec(
            num_scalar_prefetch=1, grid=(nl, 2, pr // tr),
            in_specs=[blk, own_blk, own_blk, pl.BlockSpec((None, tr, pc), lambda l, h, i, core_ref: (l, i, 0)), blk, blk],
            out_specs=(blk,) * 4),
        out_shape=(out,) * 4, compiler_params=_params(3),
    )(core, view(w), mine[0], mine[1], theirs, view(m), view(v))
    return tuple(t.reshape(shape) for t in outs)


def _rs_first_stages(layer_grads, kinds, core, tag, in_flight):
    theirs = _rs_pair_exchange([[g] for g in layer_grads], kinds, f"rs_pair_exchange_{tag}")
    sums = [_pair_sum(g, theirs[n], 0, BIG_KINDS[ki][1], core, f"rs_pair_sum_{BIG_KINDS[ki][0]}")
            for n, (ki, g) in enumerate(zip(kinds, layer_grads))]
    to_send = [both[1] for both in sums]
    if in_flight:
        return [both[0] for both in sums], _rs_chip_start(to_send, f"rs_chip_start_{tag}")
    slots = _rs_chip_exchange([[t] for t in to_send], f"rs_chip_exchange_{tag}")
    return [both[0] for both in sums], [t[0] for t in slots]


def _rs_last_stages(pair_sums, slots, chip):
    halves = [[_chip_sum(pair_sums[ki][l], slots[ki][l], chip, f"rs_chip_sum_{kind}") for l in range(DEPTH)]
              for ki, (kind, _, _, _) in enumerate(BIG_KINDS)]
    other = _rs_pair_share(halves, "rs_pair_share")
    return list(zip(halves, other))


WEIGHT_NAMES = ("w_ada", "b_ada", "norm1_w", "w_in", "conv_a_w", "conv_a_b", "ln_a_w", "ln_a_b", "lb_gamma",
                "rec_norm_w", "w_out", "norm2_w", "w_up", "conv_f_w", "w_down", "final_norm_w")
SMALL_PARAMS = (("b_ada", (DEPTH, N_MOD * D_MODEL), None), ("norm1_w", (DEPTH, D_MODEL), None),
                ("conv_a_w", (DEPTH, CONV_WIDTH, CONV_CH), 2), ("conv_a_b", (DEPTH, CONV_CH), None),
                ("ln_a_w", (DEPTH, CONV_CH), None), ("ln_a_b", (DEPTH, CONV_CH), None),
                ("lb_gamma", (DEPTH, 2, REC_WIDTH), 2), ("rec_norm_w", (DEPTH, REC_WIDTH), None),
                ("norm2_w", (DEPTH, D_MODEL), None), ("conv_f_w", (DEPTH, 3, 2 * D_FF), 2),
                ("final_norm_w", (D_MODEL,), None))


def _pack_rows(parts):
    flat = jnp.concatenate([p.reshape(-1) for p in parts])
    total = flat.shape[0]
    padded = -(-total // (8 * LANES)) * (8 * LANES)
    return jnp.pad(flat, (0, padded - total)).reshape(padded // LANES, LANES)


def _unpack(flat, shapes):
    out, off = [], 0
    for shp in shapes:
        size = int(np.prod(shp))
        out.append(flat[off:off + size].reshape(shp))
        off += size
    return out


def _unstack_chips(t, axis):
    return jnp.concatenate([t[j] for j in range(4)], axis=axis)


def kernel(x, c, w_ada, b_ada, norm1_w, w_in, conv_a_w, conv_a_b, ln_a_w, ln_a_b, lb_gamma, rec_norm_w, w_out, norm2_w, w_up, conv_f_w, w_down, final_norm_w, loss_target, m_w_ada, m_b_ada, m_norm1_w, m_w_in, m_conv_a_w, m_conv_a_b, m_ln_a_w, m_ln_a_b, m_lb_gamma, m_rec_norm_w, m_w_out, m_norm2_w, m_w_up, m_conv_f_w, m_w_down, m_final_norm_w, v_w_ada, v_b_ada, v_norm1_w, v_w_in, v_conv_a_w, v_conv_a_b, v_ln_a_w, v_ln_a_b, v_lb_gamma, v_rec_norm_w, v_w_out, v_norm2_w, v_w_up, v_conv_f_w, v_w_down, v_final_norm_w):
    params = dict(zip(WEIGHT_NAMES, (w_ada, b_ada, norm1_w, w_in, conv_a_w, conv_a_b, ln_a_w, ln_a_b, lb_gamma,
                                     rec_norm_w, w_out, norm2_w, w_up, conv_f_w, w_down, final_norm_w)))
    mom1 = dict(zip(WEIGHT_NAMES, (m_w_ada, m_b_ada, m_norm1_w, m_w_in, m_conv_a_w, m_conv_a_b, m_ln_a_w, m_ln_a_b,
                                   m_lb_gamma, m_rec_norm_w, m_w_out, m_norm2_w, m_w_up, m_conv_f_w, m_w_down,
                                   m_final_norm_w)))
    mom2 = dict(zip(WEIGHT_NAMES, (v_w_ada, v_b_ada, v_norm1_w, v_w_in, v_conv_a_w, v_conv_a_b, v_ln_a_w, v_ln_a_b,
                                   v_lb_gamma, v_rec_norm_w, v_w_out, v_norm2_w, v_w_up, v_conv_f_w, v_w_down,
                                   v_final_norm_w)))
    ix, iy, ic = _mesh_pos()
    chip = 2 * ix + iy
    dev = 2 * chip + ic

    c_all = _allgather_devices(c.reshape(8, LANES), "gather_cond").reshape(8, D_MODEL)
    b_sh = lax.dynamic_slice_in_dim(b_ada, chip * ADA_SHARD, ADA_SHARD, axis=1)
    mod_sh = _ada_mod(c_all, w_ada, b_sh.reshape(DEPTH, 1, ADA_SHARD), "ada_mod")
    w_in_b, w_out_b, w_up_b, w_down_b = (t.astype(BF16) for t in (w_in, w_out, w_up, w_down))
    first = _gather_chips([mod_sh, conv_a_w, conv_f_w, lb_gamma], "gather_first")
    later = [w_in_b[0], w_out_b, w_up_b, w_down_b, w_in_b[1]]
    started = _gather_chips_start(later, "gather_weights_start")
    mod_mine = lax.dynamic_index_in_dim(first[0], dev, axis=2, keepdims=False) + started[-1][0, 0]
    mods = [jnp.concatenate([mod_mine[j, l] for j in range(4)]).reshape(N_MOD, D_MODEL) for l in range(DEPTH)]
    conv_a_w_f, conv_f_w_f, gamma_f = (_unstack_chips(first[k], 2) for k in (1, 2, 3))

    flying = [started]

    def later_weights(stage, after):
        which = ([0], [1], [2, 3, 4])[stage]
        own, lands = _gather_chips_wait(flying[0], which, after, f"gather_weights_wait_{stage}")
        flying[0] = (started[0], started[1], *own, *lands, None)
        whole = lambda n, axis: jnp.concatenate([jnp.where(chip == j, own[n], lands[n][j]) for j in range(4)], axis=axis)
        if stage == 0:
            return whole(0, 1)
        if stage == 1:
            return whole(1, 1)
        return whole(4, 1), whole(2, 2), whole(3, 1)

    lb1, p_soft = _lower_bounds(gamma_f.reshape(DEPTH, 2 * REC_WIDTH), "lower_bounds")
    lbs = [jnp.zeros((2, REC_WIDTH), F32), lb1.reshape(2, REC_WIDTH)]
    small = []
    for l in range(DEPTH):
        small.append(dict(norm1_w=norm1_w[l][None], conv_a_w=conv_a_w_f[l], conv_a_b=conv_a_b[l][None],
                          ln_a_w=ln_a_w[l][None], ln_a_b=ln_a_b[l][None], rec_norm_w=rec_norm_w[l],
                          norm2_w=norm2_w[l][None], conv_f_w=conv_f_w_f[l]))

    core_id, chip_id = ic.astype(jnp.int32).reshape(1), chip.astype(jnp.int32).reshape(1)
    pending, groups = {}, []

    def on_layer_grads(l, by_kind, last):
        if l > 0:
            pending.update(by_kind)
            if not last:
                return None
            by_kind = dict(pending)
        kinds = sorted(by_kind)
        in_flight = not (l == 0 and last)
        tag = f"l{l}" if l > 0 else f"l{l}_{'mix' if last else 'ffn'}"
        sums, exchange = _rs_first_stages([by_kind[k] for k in kinds], kinds, core_id, tag, in_flight)
        groups.append((l, kinds, sums, exchange, in_flight, tag))
        return exchange[-1][0:1, 0:1] if in_flight else None

    loss, dx, grads, dfw = _sequence_step(x[0], loss_target[0], mods, lbs, small, later_weights,
                                          final_norm_w[None], on_layer_grads)
    loss = lax.psum(loss, ("x", "y", "c"))
    pair_sums = [[None] * DEPTH for _ in BIG_KINDS]
    slots = [[None] * DEPTH for _ in BIG_KINDS]
    for l, kinds, sums, exchange, in_flight, tag in groups:
        received = _rs_chip_wait(exchange, dx, f"rs_chip_wait_{tag}") if in_flight else exchange
        for n, ki in enumerate(kinds):
            pair_sums[ki][l], slots[ki][l] = sums[n], received[n]

    dgamma = _lower_bounds_bwd(grads[1]["lb"].reshape(1, 2 * REC_WIDTH), p_soft, "lower_bounds_bwd")
    dmod = [jnp.concatenate(grads[l]["mod"], axis=1) for l in range(DEPTH)]
    stack = lambda key: jnp.stack([grads[l][key] for l in range(DEPTH)])
    local_small = dict(b_ada=jnp.concatenate(dmod, axis=0), norm1_w=stack("norm1_w"), conv_a_w=stack("conv_a_w"),
                       conv_a_b=stack("conv_a_b"), ln_a_w=stack("ln_a_w"), ln_a_b=stack("ln_a_b"), lb_gamma=dgamma,
                       rec_norm_w=stack("rec_norm_w"), norm2_w=stack("norm2_w"), conv_f_w=stack("conv_f_w"),
                       final_norm_w=dfw)
    pack = _pack_rows([local_small[name] for name, _, _ in SMALL_PARAMS])
    rows = pack.shape[0]
    packs = _allgather_devices(pack, "gather_small_grads").reshape(8, rows, LANES)
    summed = _sum_devices(packs, "sum_small_grads").reshape(-1)
    small_grads = dict(zip([n for n, _, _ in SMALL_PARAMS], _unpack(summed, [shp for _, shp, _ in SMALL_PARAMS])))

    dmod_all = packs.reshape(8, rows * LANES)[:, :DEPTH * N_MOD * D_MODEL].reshape(8, DEPTH, N_MOD * D_MODEL)
    dmod_sh = lax.dynamic_slice_in_dim(dmod_all, chip * ADA_SHARD, ADA_SHARD, axis=2).transpose(1, 0, 2)
    g_ada, d_ada, m_ada, v_ada = _ada_update(c_all, dmod_sh, w_ada, m_w_ada, v_w_ada, "ada_update")

    for name, shp, axis in SMALL_PARAMS:
        if axis is not None:
            width = shp[axis] // 4
            small_grads[name] = lax.dynamic_slice_in_dim(small_grads[name], chip * width, width, axis=axis)
    names = [n for n, _, _ in SMALL_PARAMS]
    packed = [_pack_rows([src[n] for n in names])[None] for src in (params, small_grads, mom1, mom2)]
    small_out = _adamw(*packed, "adamw_small")
    shapes = [params[n].shape for n in names]
    small_delta, small_m, small_v = (dict(zip(names, _unpack(t.reshape(-1), shapes))) for t in small_out)

    summed_big = _rs_last_stages(pair_sums, slots, chip_id)
    grad, delta, new_m, new_v = dict(small_grads), small_delta, small_m, small_v
    grad["w_ada"], delta["w_ada"], new_m["w_ada"], new_v["w_ada"] = g_ada, d_ada, m_ada, v_ada
    for (name, _, _, _), (mine, theirs) in zip(BIG_KINDS, summed_big):
        grad[name], delta[name], new_m[name], new_v[name] = _adamw_halves(
            params[name], mine, theirs, mom1[name], mom2[name], core_id, f"adamw_{name}")

    return (loss, dx[None], *[grad[n] for n in WEIGHT_NAMES], *[delta[n] for n in WEIGHT_NAMES],
            *[new_m[n] for n in WEIGHT_NAMES], *[new_v[n] for n in WEIGHT_NAMES])
```

```python
import numpy as np
import jax
import jax.numpy as jnp
from jax import lax
from jax.experimental import pallas as pl
from jax.experimental.pallas import tpu as pltpu

F32 = jnp.float32
BF16 = jnp.bfloat16

D_MODEL = 1024
DEPTH = 2
HEAD_DIM = 64
CONV_CH = 256
CONV_WIDTH = 31
ATT_WIDTH = 384
N_HEADS = 6
DILATIONS = (1, 4, 16)
ATT_HALF = 64
ATT_BLOCK = 128
ALIBI_MAX_EXP = 8.0
MASK_VALUE = -1e30
REC_WIDTH = 384
REC_CHUNK = 64
F_TINY = 1e-30
D_FF = 2816
N_MOD = 6
EPS = 1e-6
G_CONV = (0, 512)
G_QKV = (512, 1664)
G_REC = (1664, 3584)
IN_COLS = 3584

ADAM_LR = 0.001
ADAM_B1 = 0.9
ADAM_B2 = 0.999
ADAM_EPS = 1e-08
ADAM_WD = 0.01
ADAM_STEP = 10

VMEM_LIMIT_BYTES = 56 * 1024 * 1024
LANES = 128
MESH = pl.DeviceIdType.MESH
ANY = pl.BlockSpec(memory_space=pl.ANY)


def _params(n_axes):
    return pltpu.CompilerParams(dimension_semantics=("arbitrary",) * n_axes,
                                vmem_limit_bytes=VMEM_LIMIT_BYTES)


def _tile(n, target):
    best = None
    for t in range(LANES, min(n, target) + 1, LANES):
        if n % t == 0:
            best = t
    return best or n


def _sigmoid(x):
    return jax.nn.sigmoid(x)


def _silu_grad(x):
    s = _sigmoid(x)
    return s * (1.0 + x * (1.0 - s))


MM_ACC_ELEMS = 1536 * 1024


def _matmul(a, b, mode, out_dtype, name, tm=1024, tn=1792, tk=1792):
    if mode == "nn":
        (m, k), (k2, n) = a.shape, b.shape
    elif mode == "nt":
        (m, k), (n, k2) = a.shape, b.shape
    else:
        (k, m), (k2, n) = a.shape, b.shape
    assert k == k2, (a.shape, b.shape, mode)
    tn, tk = _tile(n, tn), _tile(k, tk)
    tm = _tile(m, min(tm, MM_ACC_ELEMS // tn))
    nk = k // tk
    a_spec = (pl.BlockSpec((tk, tm), lambda i, j, kk: (kk, i)) if mode == "tn"
              else pl.BlockSpec((tm, tk), lambda i, j, kk: (i, kk)))
    b_spec = (pl.BlockSpec((tn, tk), lambda i, j, kk: (j, kk)) if mode == "nt"
              else pl.BlockSpec((tk, tn), lambda i, j, kk: (kk, j)))
    dims = {"nn": (((1,), (0,)), ((), ())), "nt": (((1,), (1,)), ((), ())),
            "tn": (((0,), (0,)), ((), ()))}[mode]

    def body(a_ref, b_ref, o_ref, *scratch):
        part = lax.dot_general(a_ref[...].astype(BF16), b_ref[...].astype(BF16), dims, preferred_element_type=F32)
        if nk == 1:
            o_ref[...] = part.astype(out_dtype)
            return
        acc_ref, = scratch
        kk = pl.program_id(2)

        @pl.when(kk == 0)
        def _():
            acc_ref[...] = part

        @pl.when(kk > 0)
        def _():
            acc_ref[...] += part

        @pl.when(kk == nk - 1)
        def _():
            o_ref[...] = acc_ref[...].astype(out_dtype)

    return pl.pallas_call(
        body, name=name, grid=(m // tm, n // tn, nk),
        in_specs=[a_spec, b_spec],
        out_specs=pl.BlockSpec((tm, tn), lambda i, j, kk: (i, j)),
        out_shape=jax.ShapeDtypeStruct((m, n), out_dtype),
        scratch_shapes=[pltpu.VMEM((tm, tn), F32)] if nk > 1 else [],
        compiler_params=pltpu.CompilerParams(dimension_semantics=("parallel", "parallel", "arbitrary"),
                                             vmem_limit_bytes=VMEM_LIMIT_BYTES),
    )(a, b)


def _matmul_two_lhs(a1, a2, b, out_dtype, name):
    (m, k1), n = a1.shape, b.shape[0]
    tn, tk = _tile(n, 1792), _tile(k1, 1792)
    tm = _tile(m, min(1024, MM_ACC_ELEMS // tn))
    nk1 = k1 // tk
    nk = 2 * nk1

    def body(a1_ref, a2_ref, b_ref, o_ref, acc_ref):
        kk = pl.program_id(2)
        lhs = jnp.where(kk < nk1, a1_ref[...], a2_ref[...])
        part = lax.dot_general(lhs, b_ref[...], (((1,), (1,)), ((), ())), preferred_element_type=F32)

        @pl.when(kk == 0)
        def _():
            acc_ref[...] = part

        @pl.when(kk > 0)
        def _():
            acc_ref[...] += part

        @pl.when(kk == nk - 1)
        def _():
            o_ref[...] = acc_ref[...].astype(out_dtype)

    return pl.pallas_call(
        body, name=name, grid=(m // tm, n // tn, nk),
        in_specs=[pl.BlockSpec((tm, tk), lambda i, j, kk: (i, jnp.minimum(kk, nk1 - 1))),
                  pl.BlockSpec((tm, tk), lambda i, j, kk: (i, jnp.maximum(kk - nk1, 0))),
                  pl.BlockSpec((tn, tk), lambda i, j, kk: (j, kk))],
        out_specs=pl.BlockSpec((tm, tn), lambda i, j, kk: (i, j)),
        out_shape=jax.ShapeDtypeStruct((m, n), out_dtype),
        scratch_shapes=[pltpu.VMEM((tm, tn), F32)],
        compiler_params=pltpu.CompilerParams(dimension_semantics=("parallel", "parallel", "arbitrary"),
                                             vmem_limit_bytes=VMEM_LIMIT_BYTES),
    )(a1, a2, b)


def _matmul_two_rhs(a, b1, b2, out_dtype, name):
    (k, m), n1 = a.shape, b1.shape[1]
    tn, tk = _tile(n1, 1792), _tile(k, 1792)
    tm = _tile(m, min(1024, MM_ACC_ELEMS // tn))
    nj1, nk = n1 // tn, k // tk

    def body(a_ref, b1_ref, b2_ref, o_ref, acc_ref):
        j, kk = pl.program_id(1), pl.program_id(2)
        rhs = jnp.where(j < nj1, b1_ref[...], b2_ref[...])
        part = lax.dot_general(a_ref[...], rhs, (((0,), (0,)), ((), ())), preferred_element_type=F32)

        @pl.when(kk == 0)
        def _():
            acc_ref[...] = part

        @pl.when(kk > 0)
        def _():
            acc_ref[...] += part

        @pl.when(kk == nk - 1)
        def _():
            o_ref[...] = acc_ref[...].astype(out_dtype)

    return pl.pallas_call(
        body, name=name, grid=(m // tm, 2 * nj1, nk),
        in_specs=[pl.BlockSpec((tk, tm), lambda i, j, kk: (kk, i)),
                  pl.BlockSpec((tk, tn), lambda i, j, kk: (jnp.where(j < nj1, kk, 0), jnp.minimum(j, nj1 - 1))),
                  pl.BlockSpec((tk, tn), lambda i, j, kk: (jnp.where(j < nj1, 0, kk), jnp.maximum(j - nj1, 0)))],
        out_specs=pl.BlockSpec((tm, tn), lambda i, j, kk: (i, j)),
        out_shape=jax.ShapeDtypeStruct((m, 2 * n1), out_dtype),
        scratch_shapes=[pltpu.VMEM((tm, tn), F32)],
        compiler_params=pltpu.CompilerParams(dimension_semantics=("parallel", "parallel", "arbitrary"),
                                             vmem_limit_bytes=VMEM_LIMIT_BYTES),
    )(a, b1, b2)


NORM_ROWS = 256


def _row_spec(width, rows=NORM_ROWS):
    return pl.BlockSpec((rows, width), lambda i: (i, 0))


def _vec_spec(width):
    return pl.BlockSpec((1, width), lambda i: (0, 0))


def _resid_norm_mod(x, r, g, nw, sc, sh, name):
    s, d = x.shape
    has_r = r is not None

    def body(*refs):
        if has_r:
            x_ref, r_ref, g_ref, nw_ref, sc_ref, sh_ref, xn_ref, h_ref = refs
            xn = x_ref[...] + g_ref[...] * r_ref[...].astype(F32)
            xn_ref[...] = xn
        else:
            x_ref, nw_ref, sc_ref, sh_ref, h_ref = refs
            xn = x_ref[...]
        rstd = lax.rsqrt(jnp.mean(xn * xn, axis=-1, keepdims=True) + EPS)
        y = xn * rstd * nw_ref[...]
        h_ref[...] = (y * (1.0 + sc_ref[...]) + sh_ref[...]).astype(BF16)

    if has_r:
        ins, in_specs = (x, r, g, nw, sc, sh), [_row_spec(d), _row_spec(d)] + [_vec_spec(d)] * 4
        out_shape = (jax.ShapeDtypeStruct((s, d), F32), jax.ShapeDtypeStruct((s, d), BF16))
        out_specs = (_row_spec(d), _row_spec(d))
    else:
        ins, in_specs = (x, nw, sc, sh), [_row_spec(d)] + [_vec_spec(d)] * 3
        out_shape = jax.ShapeDtypeStruct((s, d), BF16)
        out_specs = _row_spec(d)
    return pl.pallas_call(body, name=name, grid=(s // NORM_ROWS,), in_specs=in_specs, out_specs=out_specs,
                          out_shape=out_shape, compiler_params=_params(1))(*ins)


def _final_loss(x, r, g, fw, tgt, name):
    s, d = x.shape

    def body(x_ref, r_ref, g_ref, fw_ref, t_ref, loss_ref, dx_ref, dr_ref, dg_ref, dfw_ref):
        @pl.when(pl.program_id(0) == 0)
        def _():
            loss_ref[...] = jnp.zeros_like(loss_ref)
            dg_ref[...] = jnp.zeros_like(dg_ref)
            dfw_ref[...] = jnp.zeros_like(dfw_ref)

        rr = r_ref[...].astype(F32)
        gg = g_ref[...]
        xn = x_ref[...] + gg * rr
        rstd = lax.rsqrt(jnp.mean(xn * xn, axis=-1, keepdims=True) + EPS)
        xh = xn * rstd
        fwv = fw_ref[...]
        e = xh * fwv - t_ref[...]
        loss_ref[...] += 0.5 * jnp.sum(jnp.mean(e * e, axis=-1, keepdims=True), axis=0, keepdims=True)
        dy = e * (1.0 / d)
        dfw_ref[...] += jnp.sum(dy * xh, axis=0, keepdims=True)
        dxh = dy * fwv
        dx = rstd * (dxh - xh * jnp.mean(dxh * xh, axis=-1, keepdims=True))
        dx_ref[...] = dx
        dr_ref[...] = (gg * dx).astype(BF16)
        dg_ref[...] += jnp.sum(dx * rr, axis=0, keepdims=True)

    return pl.pallas_call(
        body, name=name, grid=(s // NORM_ROWS,),
        in_specs=[_row_spec(d), _row_spec(d), _vec_spec(d), _vec_spec(d), _row_spec(d)],
        out_specs=(_vec_spec(LANES), _row_spec(d), _row_spec(d), _vec_spec(d), _vec_spec(d)),
        out_shape=(jax.ShapeDtypeStruct((1, LANES), F32), jax.ShapeDtypeStruct((s, d), F32),
                   jax.ShapeDtypeStruct((s, d), BF16), jax.ShapeDtypeStruct((1, d), F32),
                   jax.ShapeDtypeStruct((1, d), F32)),
        compiler_params=_params(1))(x, r, g, fw, tgt)


def _norm_bwd(x, dhs, dxres, nw, sc, g, r, name):
    s, d = x.shape
    n_dh = len(dhs)
    has_g = g is not None

    def body(*refs):
        x_ref = refs[0]
        dh_refs = refs[1:1 + n_dh]
        dxres_ref, nw_ref, sc_ref = refs[1 + n_dh:4 + n_dh]
        pos = 4 + n_dh
        if has_g:
            g_ref, r_ref = refs[pos:pos + 2]
            pos += 2
            dx_ref, dr_ref, dsh_ref, dsc_ref, dnw_ref, dg_ref = refs[pos:]
            accs = (dsh_ref, dsc_ref, dnw_ref, dg_ref)
        else:
            dx_ref, dsh_ref, dsc_ref, dnw_ref = refs[pos:]
            accs = (dsh_ref, dsc_ref, dnw_ref)

        @pl.when(pl.program_id(0) == 0)
        def _():
            for acc in accs:
                acc[...] = jnp.zeros_like(acc)

        xv = x_ref[...]
        dh = dh_refs[0][...].astype(F32)
        for extra in dh_refs[1:]:
            dh = dh + extra[...].astype(F32)
        rstd = lax.rsqrt(jnp.mean(xv * xv, axis=-1, keepdims=True) + EPS)
        xh = xv * rstd
        nwv = nw_ref[...]
        dsh_ref[...] += jnp.sum(dh, axis=0, keepdims=True)
        dsc_ref[...] += jnp.sum(dh * (xh * nwv), axis=0, keepdims=True)
        dy = dh * (1.0 + sc_ref[...])
        dnw_ref[...] += jnp.sum(dy * xh, axis=0, keepdims=True)
        dxh = dy * nwv
        dx = dxres_ref[...] + rstd * (dxh - xh * jnp.mean(dxh * xh, axis=-1, keepdims=True))
        dx_ref[...] = dx
        if has_g:
            dr_ref[...] = (g_ref[...] * dx).astype(BF16)
            dg_ref[...] += jnp.sum(dx * r_ref[...].astype(F32), axis=0, keepdims=True)

    ins = [x, *dhs, dxres, nw, sc]
    in_specs = [_row_spec(d)] * (2 + n_dh) + [_vec_spec(d)] * 2
    out_shape = [jax.ShapeDtypeStruct((s, d), F32)]
    out_specs = [_row_spec(d)]
    if has_g:
        ins += [g, r]
        in_specs += [_vec_spec(d), _row_spec(d)]
        out_shape.append(jax.ShapeDtypeStruct((s, d), BF16))
        out_specs.append(_row_spec(d))
    n_vec = 4 if has_g else 3
    out_shape += [jax.ShapeDtypeStruct((1, d), F32)] * n_vec
    out_specs += [_vec_spec(d)] * n_vec
    return pl.pallas_call(body, name=name, grid=(s // NORM_ROWS,), in_specs=in_specs, out_specs=tuple(out_specs),
                          out_shape=tuple(out_shape), compiler_params=_params(1))(*ins)


FFN_ROWS = 256
FFN_COLS = 1408
HALO = 16
INV_SQRT2 = 0.7071067811865476
INV_SQRT_2PI = 0.3989422804014327


def _gelu(x):
    return 0.5 * x * (1.0 + lax.erf(x * INV_SQRT2))


def _gelu_grad(x):
    return 0.5 * (1.0 + lax.erf(x * INV_SQRT2)) + x * (INV_SQRT_2PI * jnp.exp(-0.5 * x * x))


def _halo_specs(rows, cols, halo, n_rows_total, col_of):
    per = rows // halo
    last = n_rows_total // halo - 1
    cur = pl.BlockSpec((rows, cols), lambda j, i: (i, col_of(j)))
    prev = pl.BlockSpec((halo, cols), lambda j, i: (jnp.maximum(i * per - 1, 0), col_of(j)))
    nxt = pl.BlockSpec((halo, cols), lambda j, i: (jnp.minimum((i + 1) * per, last), col_of(j)))
    return [prev, cur, nxt]


def _shift_rows(x, k):
    n = x.shape[0]
    return pltpu.roll(x, k % n, axis=0)


def _conv3(ext, w):
    return w[0:1, :] * _shift_rows(ext, 1) + w[1:2, :] * ext + w[2:3, :] * _shift_rows(ext, -1)


def _ext_block(prev_ref, cur_ref, next_ref, i, n_i):
    prev = jnp.where(i > 0, prev_ref[...].astype(F32), 0.0)
    nxt = jnp.where(i < n_i - 1, next_ref[...].astype(F32), 0.0)
    return jnp.concatenate([prev, cur_ref[...].astype(F32), nxt], axis=0)


def _ffn_act(u, cw, name):
    s = u.shape[0]
    nc, ns = D_FF // FFN_COLS, s // FFN_ROWS

    def body(gp, gc, gn, vp, vc, vn, wg_ref, wv_ref, o_ref, cg_ref, cv_ref):
        i = pl.program_id(1)
        cg = _conv3(_ext_block(gp, gc, gn, i, ns), wg_ref[...])[HALO:HALO + FFN_ROWS]
        cv = _conv3(_ext_block(vp, vc, vn, i, ns), wv_ref[...])[HALO:HALO + FFN_ROWS]
        o_ref[...] = (_gelu(cg) * cv).astype(BF16)
        cg_ref[...] = cg.astype(BF16)
        cv_ref[...] = cv.astype(BF16)

    in_specs = (_halo_specs(FFN_ROWS, FFN_COLS, HALO, s, lambda j: j)
                + _halo_specs(FFN_ROWS, FFN_COLS, HALO, s, lambda j: j + nc)
                + [pl.BlockSpec((3, FFN_COLS), lambda j, i: (0, j)),
                   pl.BlockSpec((3, FFN_COLS), lambda j, i: (0, j + nc))])
    blk = pl.BlockSpec((FFN_ROWS, FFN_COLS), lambda j, i: (i, j))
    return pl.pallas_call(
        body, name=name, grid=(nc, ns), in_specs=in_specs, out_specs=(blk, blk, blk),
        out_shape=(jax.ShapeDtypeStruct((s, D_FF), BF16),) * 3, compiler_params=_params(2),
    )(u, u, u, u, u, u, cw, cw)


def _ffn_act_bwd(u, cg, cv, dact, cw, name):
    s = u.shape[0]
    nc, ns = D_FF // FFN_COLS, s // FFN_ROWS

    def body(ug_ref, uv_ref, gp, gc, gn, vp, vc, vn, dp, dc, dn, wg_ref, wv_ref, dug_ref, duv_ref, dwg_ref, dwv_ref):
        i = pl.program_id(1)

        @pl.when(i == 0)
        def _():
            dwg_ref[...] = jnp.zeros_like(dwg_ref)
            dwv_ref[...] = jnp.zeros_like(dwv_ref)

        cge = _ext_block(gp, gc, gn, i, ns)
        cve = _ext_block(vp, vc, vn, i, ns)
        da = _ext_block(dp, dc, dn, i, ns)
        dcg = da * cve * _gelu_grad(cge)
        dcv = da * _gelu(cge)
        inner = slice(HALO, HALO + FFN_ROWS)
        for d_c, u_ref, w_ref, du_ref, dw_ref in ((dcg, ug_ref, wg_ref, dug_ref, dwg_ref),
                                                  (dcv, uv_ref, wv_ref, duv_ref, dwv_ref)):
            w = w_ref[...]
            d_next, d_prev = _shift_rows(d_c, -1), _shift_rows(d_c, 1)
            du = w[0:1, :] * d_next + w[1:2, :] * d_c + w[2:3, :] * d_prev
            du_ref[...] = du[inner].astype(BF16)
            u_in = u_ref[...].astype(F32)
            for tap, d_tap in enumerate((d_next, d_c, d_prev)):
                dw_ref[tap:tap + 1, :] += jnp.sum(d_tap[inner] * u_in, axis=0, keepdims=True)

    blk = pl.BlockSpec((FFN_ROWS, FFN_COLS), lambda j, i: (i, j))
    in_specs = ([blk, pl.BlockSpec((FFN_ROWS, FFN_COLS), lambda j, i: (i, j + nc))]
                + _halo_specs(FFN_ROWS, FFN_COLS, HALO, s, lambda j: j) * 3
                + [pl.BlockSpec((3, FFN_COLS), lambda j, i: (0, j)),
                   pl.BlockSpec((3, FFN_COLS), lambda j, i: (0, j + nc))])
    acc = pl.BlockSpec((HALO, FFN_COLS), lambda j, i: (0, j))
    return pl.pallas_call(
        body, name=name, grid=(nc, ns), in_specs=in_specs, out_specs=(blk, blk, acc, acc),
        out_shape=(jax.ShapeDtypeStruct((s, D_FF), BF16), jax.ShapeDtypeStruct((s, D_FF), BF16),
                   jax.ShapeDtypeStruct((HALO, D_FF), F32), jax.ShapeDtypeStruct((HALO, D_FF), F32)),
        compiler_params=_params(2),
    )(u, u, cg, cg, cg, cv, cv, cv, dact, dact, dact, cw, cw)


CONV_ROWS = 512
CONV_HALO = 16
CONV_PAD = CONV_WIDTH // 2


def _conv_halo_specs(cols, s):
    per = CONV_ROWS // CONV_HALO
    last = s // CONV_HALO - 1
    return [pl.BlockSpec((CONV_HALO, cols), lambda i: (jnp.maximum(i * per - 1, 0), 0)),
            pl.BlockSpec((CONV_ROWS, cols), lambda i: (i, 0)),
            pl.BlockSpec((CONV_HALO, cols), lambda i: (jnp.minimum((i + 1) * per, last), 0))]


def _glu_ext(pp, pc, pn, i, n_i):
    ext = _ext_block(pp, pc, pn, i, n_i)
    return ext[:, :CONV_CH] * _sigmoid(ext[:, CONV_CH:])


def _conv_mixer(pa, cw, cb, lnw, lnb, name):
    s = pa.shape[0]
    ns = s // CONV_ROWS

    def body(pp, pc, pn, cw_ref, cb_ref, lnw_ref, lnb_ref, o_ref, c_ref):
        i = pl.program_id(0)
        a = _glu_ext(pp, pc, pn, i, ns)
        acc = jnp.zeros((CONV_ROWS, CONV_CH), F32)
        for tap in range(CONV_WIDTH):
            acc = acc + cw_ref[tap:tap + 1, :] * _shift_rows(a, -(tap + 1))[:CONV_ROWS]
        cv = acc + cb_ref[...]
        c_ref[...] = cv
        mu = jnp.mean(cv, axis=-1, keepdims=True)
        xc = cv - mu
        rstd = lax.rsqrt(jnp.mean(xc * xc, axis=-1, keepdims=True) + EPS)
        y = xc * rstd * lnw_ref[...] + lnb_ref[...]
        o_ref[...] = (y * _sigmoid(y)).astype(BF16)

    vec = pl.BlockSpec((1, CONV_CH), lambda i: (0, 0))
    blk = pl.BlockSpec((CONV_ROWS, CONV_CH), lambda i: (i, 0))
    return pl.pallas_call(
        body, name=name, grid=(ns,),
        in_specs=_conv_halo_specs(2 * CONV_CH, s) + [pl.BlockSpec((CONV_WIDTH, CONV_CH), lambda i: (0, 0)), vec, vec, vec],
        out_specs=(blk, blk),
        out_shape=(jax.ShapeDtypeStruct((s, CONV_CH), BF16), jax.ShapeDtypeStruct((s, CONV_CH), F32)),
        compiler_params=_params(1))(pa, pa, pa, cw, cb, lnw, lnb)


def _conv_mixer_bwd_ln(cv, dout, lnw, lnb, name):
    s = cv.shape[0]

    def body(c_ref, do_ref, lnw_ref, lnb_ref, dc_ref, dlnw_ref, dlnb_ref, dcb_ref):
        @pl.when(pl.program_id(0) == 0)
        def _():
            dlnw_ref[...] = jnp.zeros_like(dlnw_ref)
            dlnb_ref[...] = jnp.zeros_like(dlnb_ref)
            dcb_ref[...] = jnp.zeros_like(dcb_ref)

        c = c_ref[...]
        mu = jnp.mean(c, axis=-1, keepdims=True)
        xc = c - mu
        rstd = lax.rsqrt(jnp.mean(xc * xc, axis=-1, keepdims=True) + EPS)
        xh = xc * rstd
        w = lnw_ref[...]
        y = xh * w + lnb_ref[...]
        dy = do_ref[...] * _silu_grad(y)
        dlnw_ref[...] += jnp.sum(dy * xh, axis=0, keepdims=True)
        dlnb_ref[...] += jnp.sum(dy, axis=0, keepdims=True)
        dxh = dy * w
        dc = rstd * (dxh - jnp.mean(dxh, axis=-1, keepdims=True) - xh * jnp.mean(dxh * xh, axis=-1, keepdims=True))
        dc_ref[...] = dc
        dcb_ref[...] += jnp.sum(dc, axis=0, keepdims=True)

    vec = pl.BlockSpec((1, CONV_CH), lambda i: (0, 0))
    blk = pl.BlockSpec((CONV_ROWS, CONV_CH), lambda i: (i, 0))
    return pl.pallas_call(
        body, name=name, grid=(s // CONV_ROWS,), in_specs=[blk, blk, vec, vec], out_specs=(blk, vec, vec, vec),
        out_shape=(jax.ShapeDtypeStruct((s, CONV_CH), F32),) + (jax.ShapeDtypeStruct((1, CONV_CH), F32),) * 3,
        compiler_params=_params(1))(cv, dout, lnw, lnb)


def _conv_mixer_bwd_conv(pa, dc, cw, name):
    s = pa.shape[0]
    ns = s // CONV_ROWS

    def body(pc, dp, dcc, dn, cw_ref, dpa_ref, dcw_ref):
        i = pl.program_id(0)

        @pl.when(i == 0)
        def _():
            dcw_ref[...] = jnp.zeros_like(dcw_ref)

        cur = pc[...]
        val, sg = cur[:, :CONV_CH], _sigmoid(cur[:, CONV_CH:])
        a_cur = val * sg
        dce = _ext_block(dp, dcc, dn, i, ns)
        da = jnp.zeros((CONV_ROWS, CONV_CH), F32)
        for tap in range(CONV_WIDTH):
            shifted = _shift_rows(dce, -(CONV_WIDTH - tap))[:CONV_ROWS]
            da = da + cw_ref[tap:tap + 1, :] * shifted
            dcw_ref[tap:tap + 1, :] += jnp.sum(shifted * a_cur, axis=0, keepdims=True)
        dpa_ref[:, :CONV_CH] = (da * sg).astype(BF16)
        dpa_ref[:, CONV_CH:] = (da * val * sg * (1.0 - sg)).astype(BF16)

    return pl.pallas_call(
        body, name=name, grid=(ns,),
        in_specs=[pl.BlockSpec((CONV_ROWS, 2 * CONV_CH), lambda i: (i, 0))] + _conv_halo_specs(CONV_CH, s)
        + [pl.BlockSpec((CONV_WIDTH, CONV_CH), lambda i: (0, 0))],
        out_specs=(pl.BlockSpec((CONV_ROWS, 2 * CONV_CH), lambda i: (i, 0)),
                   pl.BlockSpec((32, CONV_CH), lambda i: (0, 0))),
        out_shape=(jax.ShapeDtypeStruct((s, 2 * CONV_CH), BF16), jax.ShapeDtypeStruct((32, CONV_CH), F32)),
        compiler_params=_params(1))(pa, dc, dc, dc, cw)


SLOPES = tuple(float(2.0 ** (-ALIBI_MAX_EXP * (h + 1) / N_HEADS)) for h in range(N_HEADS))
ATT_SCALE = HEAD_DIM ** -0.5


PAIR = 2 * HEAD_DIM
N_PAIRS = N_HEADS // 2
ATT_WIN = ATT_BLOCK + 2 * ATT_HALF


ATT_GROUPS = {1: 4, 4: 2, 16: 1}


def _window_specs(dil, n_steps, col_of):
    per = 2 * ATT_GROUPS[dil]
    rows, halo = ATT_BLOCK * dil * ATT_GROUPS[dil], ATT_HALF * dil
    return [pl.BlockSpec((halo, PAIR), lambda i, p: (jnp.maximum(per * i - 1, 0), col_of(p))),
            pl.BlockSpec((rows, PAIR), lambda i, p: (i, col_of(p))),
            pl.BlockSpec((halo, PAIR), lambda i, p: (jnp.minimum(per * (i + 1), per * n_steps - 1), col_of(p)))]


def _residue(ref, r, n, dil, start=0):
    return ref[pl.ds(start * dil + r, n, stride=dil), :] if dil > 1 else ref[pl.ds(start + r, n), :]


def _store_residue(ref, r, dil, start, val):
    if dil > 1:
        ref[pl.ds(start * dil + r, val.shape[0], stride=dil), :] = val
    else:
        ref[pl.ds(start + r, val.shape[0]), :] = val


def _residue_window(refs, r, dil, g=0):
    prev, cur, nxt = refs
    groups = ATT_GROUPS[dil]
    lo = max(g * ATT_BLOCK - ATT_HALF, 0)
    hi = min((g + 1) * ATT_BLOCK + ATT_HALF, groups * ATT_BLOCK)
    parts = [_residue(prev, r, ATT_HALF, dil)] if g == 0 else []
    parts.append(_residue(cur, r, hi - lo, dil, lo))
    if g == groups - 1:
        parts.append(_residue(nxt, r, ATT_HALF, dil))
    return jnp.concatenate(parts, axis=0)


def _band_masks(i, length, dil, transposed):
    shape = (ATT_WIN, ATT_BLOCK) if transposed else (ATT_BLOCK, ATT_WIN)
    row = lax.broadcasted_iota(jnp.int32, shape, 0)
    col = lax.broadcasted_iota(jnp.int32, shape, 1)
    wide = row if transposed else col
    dist = jnp.abs((row - col - ATT_HALF) if transposed else (row + ATT_HALF - col))
    wpos = i * ATT_BLOCK - ATT_HALF + wide
    valid = (dist <= ATT_HALF) & (wpos >= 0) & (wpos < length)
    return valid, dist.astype(F32) * float(dil)


def _attn_branch(qkv, dil, name):
    s = qkv.shape[0]
    groups = ATT_GROUPS[dil]
    rows = ATT_BLOCK * dil * groups
    n_steps = s // rows
    length = s // dil
    nt = (((1,), (1,)), ((), ()))

    def body(q_ref, kp, kc, kn, vp, vc, vn, o_ref, l_ref):
        i, pair = pl.program_id(0), pl.program_id(1)
        items = [(g, r) for g in range(groups) for r in range(dil)]
        q = jnp.stack([_residue(q_ref, r, ATT_BLOCK, dil, g * ATT_BLOCK) for g, r in items]).astype(BF16)
        k = jnp.stack([_residue_window((kp, kc, kn), r, dil, g) for g, r in items]).astype(BF16)
        v = jnp.stack([_residue_window((vp, vc, vn), r, dil, g) for g, r in items]).astype(BF16)
        per_group = [_band_masks(i * groups + g, length, dil, False) for g in range(groups)]
        valid = jnp.stack([per_group[g][0] for g, _ in items]) if groups > 1 else per_group[0][0][None]
        distf = jnp.stack([per_group[g][1] for g, _ in items]) if groups > 1 else per_group[0][1][None]
        outs, lses = [], []
        for hh in range(2):
            sl = slice(hh * HEAD_DIM, (hh + 1) * HEAD_DIM)
            slope = jnp.where(pair == 0, SLOPES[hh], jnp.where(pair == 1, SLOPES[2 + hh], SLOPES[4 + hh]))
            sc = jnp.einsum("bqd,bkd->bqk", q[:, :, sl], k[:, :, sl], preferred_element_type=F32) * ATT_SCALE
            sc = jnp.where(valid, sc - slope * distf, MASK_VALUE)
            m = jnp.max(sc, axis=-1, keepdims=True)
            p = jnp.exp(sc - m)
            den = jnp.sum(p, axis=-1, keepdims=True)
            outs.append(jnp.einsum("bqk,bkd->bqd", p.astype(BF16), v[:, :, sl], preferred_element_type=F32) / den)
            lses.append(jnp.broadcast_to(m + jnp.log(den), (len(items), ATT_BLOCK, HEAD_DIM)))
        o_all, l_all = jnp.concatenate(outs, axis=2), jnp.concatenate(lses, axis=2)
        for n, (g, r) in enumerate(items):
            _store_residue(o_ref, r, dil, g * ATT_BLOCK, o_all[n])
            _store_residue(l_ref, r, dil, g * ATT_BLOCK, l_all[n])

    out_blk = pl.BlockSpec((rows, PAIR), lambda i, p: (i, p))
    return pl.pallas_call(
        body, name=name, grid=(n_steps, N_PAIRS),
        in_specs=[pl.BlockSpec((rows, PAIR), lambda i, p: (i, p))]
        + _window_specs(dil, n_steps, lambda p: N_PAIRS + p) + _window_specs(dil, n_steps, lambda p: 2 * N_PAIRS + p),
        out_specs=(out_blk, out_blk),
        out_shape=(jax.ShapeDtypeStruct((s, ATT_WIDTH), F32),) * 2,
        compiler_params=_params(2))(qkv, qkv, qkv, qkv, qkv, qkv, qkv)


ATT_ROWS = 512


def _attn_combine(outs, lses, name):
    s = outs[0].shape[0]

    def body(o1, o2, o3, l1, l2, l3, att_ref, att32_ref, lse_ref):
        ls = [l1[...], l2[...], l3[...]]
        m = jnp.maximum(jnp.maximum(ls[0], ls[1]), ls[2])
        es = [jnp.exp(l - m) for l in ls]
        den = es[0] + es[1] + es[2]
        att = (es[0] * o1[...] + es[1] * o2[...] + es[2] * o3[...]) / den
        att_ref[...] = att.astype(BF16)
        att32_ref[...] = att
        lse_ref[...] = m + jnp.log(den)

    blk = pl.BlockSpec((ATT_ROWS, ATT_WIDTH), lambda i: (i, 0))
    return pl.pallas_call(
        body, name=name, grid=(s // ATT_ROWS,), in_specs=[blk] * 6, out_specs=(blk, blk, blk),
        out_shape=(jax.ShapeDtypeStruct((s, ATT_WIDTH), BF16), jax.ShapeDtypeStruct((s, ATT_WIDTH), F32),
                   jax.ShapeDtypeStruct((s, ATT_WIDTH), F32)),
        compiler_params=_params(1))(*outs, *lses)


def _attn_delta(datt, att, name):
    s = att.shape[0]

    def body(d_ref, a_ref, delta_ref):
        prod = d_ref[...] * a_ref[...]
        for h in range(N_HEADS):
            sl = slice(h * HEAD_DIM, (h + 1) * HEAD_DIM)
            delta_ref[:, sl] = jnp.broadcast_to(jnp.sum(prod[:, sl], axis=-1, keepdims=True), (ATT_ROWS, HEAD_DIM))

    blk = pl.BlockSpec((ATT_ROWS, ATT_WIDTH), lambda i: (i, 0))
    return pl.pallas_call(
        body, name=name, grid=(s // ATT_ROWS,), in_specs=[blk, blk], out_specs=blk,
        out_shape=jax.ShapeDtypeStruct((s, ATT_WIDTH), F32), compiler_params=_params(1))(datt, att)


def _attn_branch_bwd(qkv, do, lse, delta, prev, dil, out_dtype, name):
    s = qkv.shape[0]
    groups = ATT_GROUPS[dil]
    rows = ATT_BLOCK * dil * groups
    n_steps = s // rows
    length = s // dil
    has_prev = prev is not None
    tn = (((0,), (0,)), ((), ()))
    nt = (((1,), (1,)), ((), ()))

    def body(*refs):
        qs, ks, vs, dos, ls, des = (refs[3 * n:3 * n + 3] for n in range(6))
        rest = refs[18:]
        if has_prev:
            pq, pk, pv = rest[:3]
            rest = rest[3:]
        dq_ref, dk_ref, dv_ref = rest
        i, pair = pl.program_id(0), pl.program_id(1)
        items = [(g, r) for g in range(groups) for r in range(dil)]
        cur = lambda t: jnp.stack([_residue(t[1], r, ATT_BLOCK, dil, g * ATT_BLOCK) for g, r in items])
        win = lambda t: jnp.stack([_residue_window(t, r, dil, g) for g, r in items])
        q_cur, k_cur, v_cur, do_cur = (cur(t).astype(BF16) for t in (qs, ks, vs, dos))
        q_win, k_win, v_win, do_win = (win(t).astype(BF16) for t in (qs, ks, vs, dos))
        l_cur, de_cur, l_win, de_win = cur(ls), cur(des), win(ls), win(des)

        def masks(transposed):
            per_group = [_band_masks(i * groups + g, length, dil, transposed) for g in range(groups)]
            if groups == 1:
                return per_group[0][0][None], per_group[0][1][None]
            return jnp.stack([per_group[g][0] for g, _ in items]), jnp.stack([per_group[g][1] for g, _ in items])

        valid_q, distf_q = masks(False)
        valid_k, distf_k = masks(True)
        dot = lambda eq, a, b: jnp.einsum(eq, a, b, preferred_element_type=F32)
        dqs, dks, dvs = [], [], []
        for hh in range(2):
            sl = slice(hh * HEAD_DIM, (hh + 1) * HEAD_DIM)
            one = slice(hh * HEAD_DIM, hh * HEAD_DIM + 1)
            slope = jnp.where(pair == 0, SLOPES[hh], jnp.where(pair == 1, SLOPES[2 + hh], SLOPES[4 + hh]))
            sc = dot("bqd,bkd->bqk", q_cur[:, :, sl], k_win[:, :, sl]) * ATT_SCALE - slope * distf_q
            p = jnp.exp(jnp.where(valid_q, sc - l_cur[:, :, one], MASK_VALUE))
            dp = dot("bqd,bkd->bqk", do_cur[:, :, sl], v_win[:, :, sl])
            ds = (p * (dp - de_cur[:, :, one]) * ATT_SCALE).astype(BF16)
            dqs.append(dot("bqk,bkd->bqd", ds, k_win[:, :, sl]))

            sc2 = dot("bqd,bkd->bqk", q_win[:, :, sl], k_cur[:, :, sl]) * ATT_SCALE - slope * distf_k
            p2 = jnp.exp(jnp.where(valid_k, sc2 - l_win[:, :, one], MASK_VALUE))
            dvs.append(dot("bqk,bqd->bkd", p2.astype(BF16), do_win[:, :, sl]))
            dp2 = dot("bqd,bkd->bqk", do_win[:, :, sl], v_cur[:, :, sl])
            ds2 = (p2 * (dp2 - de_win[:, :, one]) * ATT_SCALE).astype(BF16)
            dks.append(dot("bqk,bqd->bkd", ds2, q_win[:, :, sl]))
        for parts, acc, out in ((dqs, pq if has_prev else None, dq_ref), (dks, pk if has_prev else None, dk_ref),
                                (dvs, pv if has_prev else None, dv_ref)):
            val = jnp.concatenate(parts, axis=2)
            for n, (g, r) in enumerate(items):
                piece = val[n]
                if has_prev:
                    piece = piece + _residue(acc, r, ATT_BLOCK, dil, g * ATT_BLOCK)
                _store_residue(out, r, dil, g * ATT_BLOCK, piece.astype(out_dtype))

    blk = pl.BlockSpec((rows, PAIR), lambda i, p: (i, p))
    in_specs = (_window_specs(dil, n_steps, lambda p: p) + _window_specs(dil, n_steps, lambda p: N_PAIRS + p)
                + _window_specs(dil, n_steps, lambda p: 2 * N_PAIRS + p) + _window_specs(dil, n_steps, lambda p: p) * 3)
    ins = [qkv] * 9 + [do] * 3 + [lse] * 3 + [delta] * 3
    if has_prev:
        in_specs += [blk] * 3
        ins += list(prev)
    return pl.pallas_call(
        body, name=name, grid=(n_steps, N_PAIRS), in_specs=in_specs, out_specs=(blk, blk, blk),
        out_shape=(jax.ShapeDtypeStruct((s, ATT_WIDTH), out_dtype),) * 3,
        compiler_params=_params(2))(*ins)


TB = 2 * REC_CHUNK
REC_ROWS = 5 * REC_WIDTH


REC_LEVELS = 6


def _scan_pos(p, rev):
    p = p & (REC_CHUNK - 1)
    return (REC_CHUNK - 1 - p) if rev else p


def _split3(x):
    hi = x.astype(BF16)
    rest = x - hi.astype(F32)
    mid = rest.astype(BF16)
    return hi, mid, (rest - mid.astype(F32)).astype(BF16)


def _chunk_sums(x, rev, with_levels):
    row = lax.broadcasted_iota(jnp.int32, (TB, TB), 0)
    col = lax.broadcasted_iota(jnp.int32, (TB, TB), 1)
    same = (row < REC_CHUNK) == (col < REC_CHUNK)
    s_row, s_col = _scan_pos(row, rev), _scan_pos(col, rev)
    mats = [same & (s_row <= s_col)]
    if with_levels:
        for level in range(1, REC_LEVELS + 1):
            shift = REC_LEVELS + 1 - level
            boundary = ((s_col >> shift) << shift) + (REC_CHUNK >> level) - 1
            mats.append(same & (s_row <= boundary))
        mats.append(same)
    cat = jnp.concatenate([m.astype(BF16) for m in mats], axis=1)
    total = sum(jnp.dot(term, cat, preferred_element_type=F32) for term in _split3(x))
    return [total[:, n * TB:(n + 1) * TB] for n in range(len(mats))]


def _hg_prep(qraw, z, lb, rev):
    lane = lax.broadcasted_iota(jnp.int32, (REC_WIDTH, TB), 1)
    in_a = lane < REC_CHUNK
    scan = _scan_pos(lane, rev)
    sig, sigm = _sigmoid(z), _sigmoid(-z)
    f = lb + (1.0 - lb) * sig
    kk = (1.0 - lb) * sigm
    sums = _chunk_sums(jnp.log(jnp.maximum(f, F_TINY)), rev, True)
    b, bend = sums[0], sums[-1]
    q = qraw * _sigmoid(qraw)
    eq, ek = [], []
    for level in range(1, REC_LEVELS + 1):
        r = sums[level]
        e = jnp.exp(jnp.minimum(b - r, r - b))
        second = ((scan >> (REC_LEVELS - level)) & 1) == 1
        eq.append(jnp.where(second, e, 0.0))
        ek.append(jnp.where(second, 0.0, e))
    lanes_end = (0, REC_CHUNK) if rev else (REC_CHUNK - 1, TB - 1)
    end_a, end_b = (b[:, n:n + 1] for n in lanes_end)
    return dict(in_a=in_a, sig=sig, sigm=sigm, f=f, kk=kk, b=b, end_a=end_a, end_b=end_b,
                q=q, qh=q * jnp.exp(b), kh=kk * jnp.exp(bend - b), ekb=jnp.exp(bend - b), eq=eq, ek=ek)


def _level_masks(rev):
    row = lax.broadcasted_iota(jnp.int32, (TB, TB), 0)
    col = lax.broadcasted_iota(jnp.int32, (TB, TB), 1)
    same = (row < REC_CHUNK) == (col < REC_CHUNK)
    s_row, s_col = _scan_pos(row, rev), _scan_pos(col, rev)
    masks = [same & ((s_row >> (REC_LEVELS + 1 - level)) == (s_col >> (REC_LEVELS + 1 - level)))
             for level in range(1, REC_LEVELS + 1)]
    return masks, row == col


def _head_rows(x, h):
    return x[h * HEAD_DIM:(h + 1) * HEAD_DIM, :]


def _block_diag_mask():
    r = lax.broadcasted_iota(jnp.int32, (REC_WIDTH, REC_WIDTH), 0) // HEAD_DIM
    c = lax.broadcasted_iota(jnp.int32, (REC_WIDTH, REC_WIDTH), 1) // HEAD_DIM
    return (r == c).astype(F32)


def _heads(x):
    return x.reshape(N_HEADS, HEAD_DIM, TB)


def _hgrn_scan(projt, lb, rev, name):
    s = projt.shape[1]
    nblk = s // TB
    zrow = 2 if rev else 1
    tmap = (lambda i: nblk - 1 - i) if rev else (lambda i: i)
    tn = (((0,), (0,)), ((), ()))
    nt = (((1,), (1,)), ((), ()))

    def body(q_ref, z_ref, v_ref, lb_ref, o_ref, hs_ref, at_ref, h_ref):
        @pl.when(pl.program_id(0) == 0)
        def _():
            h_ref[...] = jnp.zeros_like(h_ref)

        v = v_ref[...]
        vb = v.astype(BF16)
        pr = _hg_prep(q_ref[...], z_ref[...], lb_ref[...], rev)
        q, kk = pr["q"], pr["kk"]
        masks, diag = _level_masks(rev)
        own = jnp.sum(_heads(q * kk), axis=1, keepdims=True)
        sc = jnp.where(diag[None], own, 0.0)
        for level in range(REC_LEVELS):
            qt = _heads((q * pr["eq"][level]).astype(BF16))
            kt = _heads((kk * pr["ek"][level]).astype(BF16))
            sc = sc + jnp.where(masks[level][None],
                                jnp.einsum("hks,hkt->hst", kt, qt, preferred_element_type=F32), 0.0)
        a_bf = sc.astype(BF16)
        at_ref[...] = a_bf
        o = jnp.einsum("hvs,hst->hvt", _heads(vb), a_bf, preferred_element_type=F32).reshape(REC_WIDTH, TB)
        bd_mask = _block_diag_mask()
        order = ((1, ~pr["in_a"], pr["end_b"]), (0, pr["in_a"], pr["end_a"]))
        if not rev:
            order = order[::-1]
        for slot, msk, bend in order:
            h0 = h_ref[...]
            hs_ref[slot] = h0
            o = o + lax.dot_general(h0.astype(BF16), jnp.where(msk, pr["qh"], 0.0).astype(BF16), tn,
                                    preferred_element_type=F32)
            upd = lax.dot_general(jnp.where(msk, pr["kh"], 0.0).astype(BF16), vb, nt, preferred_element_type=F32)
            h_ref[...] = jnp.exp(bend) * h0 + upd * bd_mask
        o_ref[...] = o

    row_blk = lambda r: pl.BlockSpec((REC_WIDTH, TB), lambda i: (r, tmap(i)))
    return pl.pallas_call(
        body, name=name, grid=(nblk,),
        in_specs=[row_blk(0), row_blk(zrow), row_blk(3), pl.BlockSpec((REC_WIDTH, 1), lambda i: (0, 0))],
        out_specs=(pl.BlockSpec((REC_WIDTH, TB), lambda i: (0, tmap(i))),
                   pl.BlockSpec((2, REC_WIDTH, REC_WIDTH), lambda i: (tmap(i), 0, 0)),
                   pl.BlockSpec((None, N_HEADS, TB, TB), lambda i: (tmap(i), 0, 0, 0))),
        out_shape=(jax.ShapeDtypeStruct((REC_WIDTH, s), F32),
                   jax.ShapeDtypeStruct((s // REC_CHUNK, REC_WIDTH, REC_WIDTH), F32),
                   jax.ShapeDtypeStruct((nblk, N_HEADS, TB, TB), BF16)),
        scratch_shapes=[pltpu.VMEM((REC_WIDTH, REC_WIDTH), F32)],
        compiler_params=_params(1))(projt, projt, projt, lb)


def _hgrn_scan_bwd(projt, lb, dot, hs, at, prev, rev, name):
    s = projt.shape[1]
    nblk = s // TB
    zrow = 2 if rev else 1
    tmap = (lambda i: i) if rev else (lambda i: nblk - 1 - i)
    has_prev = prev is not None
    tn = (((0,), (0,)), ((), ()))
    nt = (((1,), (1,)), ((), ()))

    def body(*refs):
        q_ref, z_ref, v_ref, lb_ref, do_ref, hs_ref, at_ref = refs[:7]
        rest = refs[7:]
        if has_prev:
            pq_ref, pv_ref = rest[:2]
            rest = rest[2:]
        dq_ref, dz_ref, dv_ref, dlb_ref, dh_ref = rest

        @pl.when(pl.program_id(0) == 0)
        def _():
            dh_ref[...] = jnp.zeros_like(dh_ref)
            dlb_ref[...] = jnp.zeros_like(dlb_ref)

        qraw, v, do, lbv = q_ref[...], v_ref[...], do_ref[...], lb_ref[...]
        dob, vb = do.astype(BF16), v.astype(BF16)
        pr = _hg_prep(qraw, z_ref[...], lbv, rev)
        q, kk, b, in_a = pr["q"], pr["kk"], pr["b"], pr["in_a"]
        masks, diag = _level_masks(rev)
        dot = lambda eq, x, y: jnp.einsum(eq, x, y, preferred_element_type=F32)
        d_at = dot("hvs,hvt->hst", _heads(vb), _heads(dob))
        dv = dot("hvt,hst->hvs", _heads(dob), at_ref[...]).reshape(REC_WIDTH, TB)
        d_own = jnp.sum(jnp.where(diag[None], d_at, 0.0), axis=1, keepdims=True)
        dq_in = (d_own * _heads(kk)).reshape(REC_WIDTH, TB)
        dk_in = (d_own * _heads(q)).reshape(REC_WIDTH, TB)
        db_in = jnp.zeros((REC_WIDTH, TB), F32)
        for lv in range(REC_LEVELS):
            d_lv = jnp.where(masks[lv][None], d_at, 0.0).astype(BF16)
            q_lv, k_lv = (q * pr["eq"][lv]).astype(BF16), (kk * pr["ek"][lv]).astype(BF16)
            dqt = dot("hks,hst->hkt", _heads(k_lv), d_lv).reshape(REC_WIDTH, TB)
            dkt = dot("hkt,hst->hks", _heads(q_lv), d_lv).reshape(REC_WIDTH, TB)
            dq_in = dq_in + pr["eq"][lv] * dqt
            dk_in = dk_in + pr["ek"][lv] * dkt
            db_in = db_in + q_lv.astype(F32) * dqt - k_lv.astype(F32) * dkt
        dq = dk = jnp.zeros((REC_WIDTH, TB), F32)

        zero = jnp.zeros((REC_WIDTH, TB), F32)
        bd_mask = _block_diag_mask()
        eb = jnp.exp(b)
        const = zero
        order = ((0, in_a, pr["end_a"]), (1, ~in_a, pr["end_b"]))
        if not rev:
            order = order[::-1]
        for slot, msk, bend in order:
            h0 = hs_ref[slot]
            dh1 = dh_ref[...]
            dh1b = dh1.astype(BF16)
            dq = dq + eb * jnp.dot(h0.astype(BF16), jnp.where(msk, do, 0.0).astype(BF16), preferred_element_type=F32)
            dv = dv + lax.dot_general(dh1b, jnp.where(msk, pr["kh"], 0.0).astype(BF16), tn, preferred_element_type=F32)
            dk_int = pr["ekb"] * jnp.dot(dh1b, jnp.where(msk, v, 0.0).astype(BF16), preferred_element_type=F32)
            dk = dk + dk_int
            ebend = jnp.exp(bend)
            c = (jnp.sum(kk * dk_int, axis=1, keepdims=True)
                 + ebend * jnp.sum(h0 * dh1, axis=1, keepdims=True))
            const = const + jnp.where(msk, c, 0.0)
            upd = lax.dot_general(jnp.where(msk, pr["qh"], 0.0).astype(BF16), dob, nt, preferred_element_type=F32)
            dh_ref[...] = ebend * dh1 + upd * bd_mask

        dg = _chunk_sums(db_in + q * dq - kk * dk, not rev, False)[0] + const
        dq, dk = dq + dq_in, dk + dk_in
        sig, sigm, f = pr["sig"], pr["sigm"], pr["f"]
        live = f > F_TINY
        inv_f = 1.0 / jnp.maximum(f, F_TINY)
        one_lb = 1.0 - lbv
        dz = sig * sigm * one_lb * (jnp.where(live, dg * inv_f, 0.0) - dk)
        dlb_ref[...] += jnp.sum(sigm * (jnp.where(live, dg * inv_f, 0.0) - dk), axis=1, keepdims=True)
        dqr = dq * _silu_grad(qraw)
        if has_prev:
            dqr = dqr + pq_ref[...]
            dv = dv + pv_ref[...]
        dq_ref[...] = dqr
        dz_ref[...] = dz
        dv_ref[...] = dv

    row_blk = lambda r: pl.BlockSpec((REC_WIDTH, TB), lambda i: (r, tmap(i)))
    blk = pl.BlockSpec((REC_WIDTH, TB), lambda i: (0, tmap(i)))
    col = pl.BlockSpec((REC_WIDTH, 1), lambda i: (0, 0))
    in_specs = [row_blk(0), row_blk(zrow), row_blk(3), col, blk,
                pl.BlockSpec((2, REC_WIDTH, REC_WIDTH), lambda i: (tmap(i), 0, 0)),
                pl.BlockSpec((None, N_HEADS, TB, TB), lambda i: (tmap(i), 0, 0, 0))]
    ins = [projt, projt, projt, lb, dot, hs, at]
    if has_prev:
        in_specs += [blk, blk]
        ins += list(prev)
    t_shape = jax.ShapeDtypeStruct((REC_WIDTH, s), F32)
    return pl.pallas_call(
        body, name=name, grid=(nblk,), in_specs=in_specs, out_specs=(blk, blk, blk, col),
        out_shape=(t_shape, t_shape, t_shape, jax.ShapeDtypeStruct((REC_WIDTH, 1), F32)),
        scratch_shapes=[pltpu.VMEM((REC_WIDTH, REC_WIDTH), F32)],
        compiler_params=_params(1))(*ins)


REC_OUT_COLS = 512


def _head_rms(o):
    o3 = o.reshape(N_HEADS, HEAD_DIM, o.shape[1])
    rstd = lax.rsqrt(jnp.mean(o3 * o3, axis=1, keepdims=True) + EPS)
    return o3 * rstd, rstd


def _hgrn_out(of, ob, projt, wn, name):
    s = of.shape[1]

    def body(of_ref, ob_ref, g_ref, wn_ref, o_ref):
        on, _ = _head_rms(of_ref[...] + ob_ref[...])
        g = g_ref[...]
        y = on.reshape(REC_WIDTH, REC_OUT_COLS) * wn_ref[...] * (g * _sigmoid(g))
        o_ref[...] = y.T.astype(BF16)

    blk = pl.BlockSpec((REC_WIDTH, REC_OUT_COLS), lambda i: (0, i))
    return pl.pallas_call(
        body, name=name, grid=(s // REC_OUT_COLS,),
        in_specs=[blk, blk, pl.BlockSpec((REC_WIDTH, REC_OUT_COLS), lambda i: (4, i)),
                  pl.BlockSpec((REC_WIDTH, 1), lambda i: (0, 0))],
        out_specs=pl.BlockSpec((REC_OUT_COLS, REC_WIDTH), lambda i: (i, 0)),
        out_shape=jax.ShapeDtypeStruct((s, REC_WIDTH), BF16), compiler_params=_params(1))(of, ob, projt, wn)


def _hgrn_out_bwd(drec, of, ob, projt, wn, name):
    s = of.shape[1]

    def body(d_ref, of_ref, ob_ref, g_ref, wn_ref, do_ref, dg_ref, dwn_ref):
        @pl.when(pl.program_id(0) == 0)
        def _():
            dwn_ref[...] = jnp.zeros_like(dwn_ref)

        dy = d_ref[...].T
        on3, rstd = _head_rms(of_ref[...] + ob_ref[...])
        on = on3.reshape(REC_WIDTH, REC_OUT_COLS)
        g, wnv = g_ref[...], wn_ref[...]
        dg_ref[...] = dy * on * wnv * _silu_grad(g)
        d_onw = dy * (g * _sigmoid(g))
        dwn_ref[...] += jnp.sum(d_onw * on, axis=1, keepdims=True)
        d_on3 = (d_onw * wnv).reshape(N_HEADS, HEAD_DIM, REC_OUT_COLS)
        do3 = rstd * (d_on3 - on3 * jnp.mean(d_on3 * on3, axis=1, keepdims=True))
        do_ref[...] = do3.reshape(REC_WIDTH, REC_OUT_COLS)

    blk = pl.BlockSpec((REC_WIDTH, REC_OUT_COLS), lambda i: (0, i))
    col = pl.BlockSpec((REC_WIDTH, 1), lambda i: (0, 0))
    t_shape = jax.ShapeDtypeStruct((REC_WIDTH, s), F32)
    return pl.pallas_call(
        body, name=name, grid=(s // REC_OUT_COLS,),
        in_specs=[pl.BlockSpec((REC_OUT_COLS, REC_WIDTH), lambda i: (i, 0)), blk, blk,
                  pl.BlockSpec((REC_WIDTH, REC_OUT_COLS), lambda i: (4, i)), col],
        out_specs=(blk, blk, col),
        out_shape=(t_shape, t_shape, jax.ShapeDtypeStruct((REC_WIDTH, 1), F32)),
        compiler_params=_params(1))(drec, of, ob, projt, wn)


def _lower_bounds(gamma, name):
    def body(g_ref, lb_ref, p_ref):
        g0, g1 = g_ref[0:1, :], g_ref[1:2, :]
        m = jnp.maximum(g0, g1)
        e0, e1 = jnp.exp(g0 - m), jnp.exp(g1 - m)
        p0, p1 = e0 / (e0 + e1), e1 / (e0 + e1)
        lb_ref[...] = (p0 + p1) - p0
        p_ref[0:1, :] = p0
        p_ref[1:2, :] = p1

    n = gamma.shape[1]
    return pl.pallas_call(body, name=name,
                          out_shape=(jax.ShapeDtypeStruct((1, n), F32), jax.ShapeDtypeStruct((2, n), F32)))(gamma)


def _lower_bounds_bwd(dlb1, p, name):
    def body(d_ref, p_ref, o_ref):
        p0, p1, d = p_ref[0:1, :], p_ref[1:2, :], d_ref[...]
        inner = p1 * d
        o_ref[0:1, :] = p0 * (0.0 - inner)
        o_ref[1:2, :] = p1 * (d - inner)

    return pl.pallas_call(body, name=name, out_shape=jax.ShapeDtypeStruct(p.shape, F32))(dlb1, p)


def _split_w_in(w_in):
    return dict(conv=w_in[:, G_CONV[0]:G_CONV[1]], qkv=w_in[:, G_QKV[0]:G_QKV[1]],
                rec_t=w_in[:, G_REC[0]:].T, nat=w_in[:, :G_REC[0]])


def _split_w_rest(w_out, w_up, w_down):
    return dict(out=w_out, out_a=w_out[:CONV_CH], out_b=w_out[CONV_CH:CONV_CH + ATT_WIDTH],
                out_c=w_out[CONV_CH + ATT_WIDTH:], up=w_up, down=w_down)


def _col(v):
    return v.reshape(-1, 1)


def _sequence_step(x, tgt, mods, lbs, small, later_weights, final_w, on_layer_grads):
    saved = []
    xin = x
    h1 = _resid_norm_mod(x, None, None, small[0]["norm1_w"], mods[0][1:2], mods[0][0:1], "norm1_first")
    big = [_split_w_in(later_weights(0, h1)), None]
    for l in range(DEPTH):
        sm, w, md = small[l], big[l], mods[l]
        pa = _matmul(h1, w["conv"], "nn", F32, f"proj_conv")
        qkv = _matmul(h1, w["qkv"], "nn", F32, f"proj_qkv")
        projt = _matmul(w["rec_t"], h1, "nt", F32, f"proj_rec")
        a_out, cv = _conv_mixer(pa, sm["conv_a_w"], sm["conv_a_b"], sm["ln_a_w"], sm["ln_a_b"], f"conv_mixer")
        outs, lses = zip(*[_attn_branch(qkv, d, f"attn_d{d}") for d in DILATIONS])
        att, att32, lse = _attn_combine(outs, lses, f"attn_combine")
        lb_f, lb_b = _col(lbs[l][0]), _col(lbs[l][1])
        of, hsf, atf = _hgrn_scan(projt, lb_f, False, "hgrn_fwd")
        ob, hsb, atb = _hgrn_scan(projt, lb_b, True, "hgrn_rev")
        wn = _col(sm["rec_norm_w"])
        rec = _hgrn_out(of, ob, projt, wn, f"hgrn_out")
        mixed = jnp.concatenate([a_out, att, rec], axis=1)
        if l == 0:
            w_out_all = later_weights(1, rec)
            big[0].update(_split_w_rest(w_out_all[0], None, None))
        r1 = _matmul(mixed, w["out"], "nn", BF16, "out_proj")
        xmid, h2 = _resid_norm_mod(xin, r1, md[2:3], sm["norm2_w"], md[4:5], md[3:4], f"norm2")
        if l == 0:
            w_in1, w_up_all, w_down_all = later_weights(2, h2)
            big[0].update(up=w_up_all[0], down=w_down_all[0])
            big[1] = dict(_split_w_in(w_in1), **_split_w_rest(w_out_all[1], w_up_all[1], w_down_all[1]))
        u = _matmul(h2, w["up"], "nn", BF16, f"ffn_up")
        act, conv_g, conv_v = _ffn_act(u, sm["conv_f_w"], "ffn_act")
        r2 = _matmul(act, w["down"], "nn", BF16, "ffn_down")
        saved.append(dict(xin=xin, h1=h1, pa=pa, qkv=qkv, projt=projt, cv=cv, att32=att32, lse=lse, of=of, ob=ob,
                          hsf=hsf, hsb=hsb, atf=atf, atb=atb, lb_f=lb_f, lb_b=lb_b, wn=wn, mixed=mixed, r1=r1, xmid=xmid, h2=h2,
                          u=u, conv_g=conv_g, conv_v=conv_v, act=act, r2=r2))
        if l + 1 < DEPTH:
            nxt = small[l + 1]
            xin, h1 = _resid_norm_mod(xmid, r2, md[5:6], nxt["norm1_w"], mods[l + 1][1:2], mods[l + 1][0:1],
                                      "norm1")
    top = saved[-1]
    loss, dx, dr2, dg2, dfw = _final_loss(top["xmid"], top["r2"], mods[-1][5:6], final_w, tgt, "final_loss")

    grads = [None] * DEPTH
    order_after = None
    for l in reversed(range(DEPTH)):
        sm, w, md, sv = small[l], big[l], mods[l], saved[l]
        dact = _matmul(dr2, w["down"], "nt", BF16, f"d_act")
        g_down = _matmul(dr2, sv["act"], "tn", F32, "dw_down").T
        conv_f_w = sm["conv_f_w"] if order_after is None else sm["conv_f_w"] + order_after
        dug, duv, dwg, dwv = _ffn_act_bwd(sv["u"], sv["conv_g"], sv["conv_v"], dact, conv_f_w, "ffn_act_bwd")
        dh2 = _matmul_two_lhs(dug, duv, w["up"], BF16, "d_h2")
        g_up = _matmul_two_rhs(sv["h2"], dug, duv, F32, "dw_up")
        after_ffn = on_layer_grads(l, {2: g_up, 3: g_down}, False)
        norm2_w = sm["norm2_w"] if after_ffn is None else sm["norm2_w"] + after_ffn
        dxmid, dr1, dsh2, dsc2, dnw2, dg1 = _norm_bwd(sv["xmid"], [dh2], dx, norm2_w, md[4:5], md[2:3], sv["r1"],
                                                     f"norm2_bwd")
        dmix_a = _matmul(dr1, w["out_a"], "nt", F32, f"d_mix_a")
        dmix_b = _matmul(dr1, w["out_b"], "nt", F32, f"d_mix_b")
        dmix_c = _matmul(dr1, w["out_c"], "nt", F32, f"d_mix_c")
        g_out = _matmul(sv["mixed"], dr1, "tn", F32, f"dw_out")
        dc, dlnw, dlnb, dcb = _conv_mixer_bwd_ln(sv["cv"], dmix_a, sm["ln_a_w"], sm["ln_a_b"], f"conv_mixer_bwd_ln")
        dpa, dcw = _conv_mixer_bwd_conv(sv["pa"], dc, sm["conv_a_w"], f"conv_mixer_bwd_conv")
        delta = _attn_delta(dmix_b, sv["att32"], "attn_delta")
        dqkv = None
        for d in reversed(DILATIONS):
            dqkv = _attn_branch_bwd(sv["qkv"], dmix_b, sv["lse"], delta, dqkv, d, BF16 if d == 1 else F32,
                                    f"attn_bwd_d{d}")
        dot, dgt, dwn = _hgrn_out_bwd(dmix_c, sv["of"], sv["ob"], sv["projt"], sv["wn"], f"hgrn_out_bwd")
        dqf, dzf, dvf, dlbf = _hgrn_scan_bwd(sv["projt"], sv["lb_f"], dot, sv["hsf"], sv["atf"], None, False,
                                             "hgrn_fwd_bwd")
        dqt, dzb, dvt, dlbb = _hgrn_scan_bwd(sv["projt"], sv["lb_b"], dot, sv["hsb"], sv["atb"], (dqf, dvf), True,
                                             "hgrn_rev_bwd")
        dprojt = jnp.concatenate([dqt, dzf, dzb, dvt, dgt], axis=0).astype(BF16)
        dnat = jnp.concatenate([dpa, *dqkv], axis=1)
        dh1_a = _matmul(dnat, w["nat"], "nt", BF16, "d_h1_nat")
        dh1_b = _matmul(dprojt, w["rec_t"], "tn", BF16, "d_h1_rec")
        g_in_nat = _matmul(sv["h1"], dnat, "tn", F32, f"dw_in_nat")
        g_in_rec_t = _matmul(dprojt, sv["h1"], "nn", F32, f"dw_in_rec")
        g_in = jnp.concatenate([g_in_nat, g_in_rec_t.T], axis=1)
        if l > 0:
            below = saved[l - 1]
            dx, dr2, dsh1, dsc1, dnw1, dg2_below = _norm_bwd(sv["xin"], [dh1_a, dh1_b], dxmid, sm["norm1_w"], md[1:2],
                                                            mods[l - 1][5:6], below["r2"], f"norm1_bwd")
        else:
            dx, dsh1, dsc1, dnw1 = _norm_bwd(sv["xin"], [dh1_a, dh1_b], dxmid, sm["norm1_w"], md[1:2], None, None,
                                             f"norm1_bwd")
        grads[l] = dict(w_in=g_in, w_out=g_out, w_up=g_up, w_down=g_down,
                        mod=[dsh1, dsc1, dg1, dsh2, dsc2, dg2], norm1_w=dnw1, conv_a_w=dcw[:CONV_WIDTH], conv_a_b=dcb,
                        ln_a_w=dlnw, ln_a_b=dlnb, lb=jnp.concatenate([dlbf.reshape(1, -1), dlbb.reshape(1, -1)], axis=0),
                        rec_norm_w=dwn.reshape(1, -1), norm2_w=dnw2,
                        conv_f_w=jnp.concatenate([dwg[:3], dwv[:3]], axis=1))
        order_after = on_layer_grads(l, {0: g_in, 1: g_out}, True)
        if l > 0:
            dg2 = dg2_below
    return loss[0, 0], dx, grads, dfw


def _adamw_math(w, g, m, v):
    m = ADAM_B1 * m + (1.0 - ADAM_B1) * g
    v = ADAM_B2 * v + (1.0 - ADAM_B2) * (g * g)
    m_hat = m / (1.0 - ADAM_B1 ** ADAM_STEP)
    v_hat = v / (1.0 - ADAM_B2 ** ADAM_STEP)
    delta = -ADAM_LR * (m_hat / (jnp.sqrt(v_hat) + ADAM_EPS) + ADAM_WD * w)
    return delta, m, v


def _row_tile(rows, cols, max_elems=384 * 1024):
    best = None
    for t in range(8, rows + 1, 8):
        if rows % t == 0 and t * cols <= max_elems:
            best = t
    return best or rows


def _adamw(w, g, m, v, name):
    nl, r, c = w.shape
    tr = _row_tile(r, c)

    def body(w_ref, g_ref, m_ref, v_ref, d_ref, m2_ref, v2_ref):
        d_ref[...], m2_ref[...], v2_ref[...] = _adamw_math(w_ref[...], g_ref[...], m_ref[...], v_ref[...])

    blk = pl.BlockSpec((None, tr, c), lambda l, i: (l, i, 0))
    shape = jax.ShapeDtypeStruct((nl, r, c), F32)
    return pl.pallas_call(body, name=name, grid=(nl, r // tr), in_specs=[blk] * 4, out_specs=(blk, blk, blk),
                          out_shape=(shape, shape, shape), compiler_params=_params(2))(w, g, m, v)


ADA_SHARD = N_MOD * D_MODEL // 4
ADA_COLS = 512
ADA_ROWS = 256
HIGHEST = lax.Precision.HIGHEST


def _ada_mod(c_all, w_ada, b_sh, name):
    def body(c_ref, w_ref, b_ref, o_ref):
        cv = c_ref[...]
        o_ref[...] = jnp.dot(cv * _sigmoid(cv), w_ref[...], precision=HIGHEST, preferred_element_type=F32) + b_ref[...]

    return pl.pallas_call(
        body, name=name, grid=(DEPTH, ADA_SHARD // ADA_COLS),
        in_specs=[pl.BlockSpec((8, D_MODEL), lambda l, j: (0, 0)),
                  pl.BlockSpec((None, D_MODEL, ADA_COLS), lambda l, j: (l, 0, j)),
                  pl.BlockSpec((None, 1, ADA_COLS), lambda l, j: (l, 0, j))],
        out_specs=pl.BlockSpec((None, 8, ADA_COLS), lambda l, j: (l, 0, j)),
        out_shape=jax.ShapeDtypeStruct((DEPTH, 8, ADA_SHARD), F32), compiler_params=_params(2))(c_all, w_ada, b_sh)


def _ada_update(c_all, dmod_sh, w, m, v, name):
    def body(c_ref, d_ref, w_ref, m_ref, v_ref, g_ref, dl_ref, m2_ref, v2_ref):
        cv = c_ref[...]
        g = lax.dot_general(cv * _sigmoid(cv), d_ref[...], (((0,), (0,)), ((), ())), precision=HIGHEST,
                            preferred_element_type=F32)
        g_ref[...] = g
        dl_ref[...], m2_ref[...], v2_ref[...] = _adamw_math(w_ref[...], g, m_ref[...], v_ref[...])

    blk = pl.BlockSpec((None, ADA_ROWS, ADA_SHARD), lambda l, i: (l, i, 0))
    shape = jax.ShapeDtypeStruct((DEPTH, D_MODEL, ADA_SHARD), F32)
    return pl.pallas_call(
        body, name=name, grid=(DEPTH, D_MODEL // ADA_ROWS),
        in_specs=[pl.BlockSpec((8, ADA_ROWS), lambda l, i: (0, i)),
                  pl.BlockSpec((None, 8, ADA_SHARD), lambda l, i: (l, 0, 0)), blk, blk, blk],
        out_specs=(blk,) * 4, out_shape=(shape,) * 4, compiler_params=_params(2))(c_all, dmod_sh, w, m, v)


def _sum_devices(packs, name):
    def body(p_ref, o_ref):
        acc = p_ref[0]
        for dev in range(1, 8):
            acc = acc + p_ref[dev]
        o_ref[...] = acc

    return pl.pallas_call(body, name=name, out_shape=jax.ShapeDtypeStruct(packs.shape[1:], F32))(packs)


def _mesh_pos():
    return lax.axis_index("x"), lax.axis_index("y"), lax.axis_index("c")


def _flip(v, bit):
    return 1 - v if bit else v


def _allgather_devices(x, name):
    m_per, n = x.shape

    def body(x_ref, out_ref, send_sems, recv_sems, local_sem):
        ix, iy, ic = _mesh_pos()
        me, sibling = (ix, iy, ic), (ix, iy, 1 - ic)
        chips = [(1 - ix, iy), (ix, 1 - iy), (1 - ix, 1 - iy)]

        def rows(px, py, pc):
            return out_ref.at[pl.ds((4 * px + 2 * py + pc) * m_per, m_per), :]

        def copy(k, block, to, src=None):
            return pltpu.make_async_remote_copy(
                src_ref=rows(*block) if src is None else src, dst_ref=rows(*block),
                send_sem=send_sems.at[k], recv_sem=recv_sems.at[k], device_id=to, device_id_type=MESH)

        mine = pltpu.make_async_copy(x_ref, rows(*me), local_sem)
        mine.start()
        first = [copy(0, me, sibling, src=x_ref)]
        first += [copy(1 + j, me, (*chip, ic), src=x_ref) for j, chip in enumerate(chips)]
        for cp in first:
            cp.start()
        passed = [copy(4 + j, (*chip, ic), sibling) for j, chip in enumerate(chips)]
        for j, chip in enumerate(chips):
            copy(1 + j, (*chip, ic), me).wait_recv()
            passed[j].start()
        copy(0, sibling, me).wait_recv()
        for j, chip in enumerate(chips):
            copy(4 + j, (*chip, 1 - ic), me).wait_recv()
        for cp in first + passed:
            cp.wait_send()
        mine.wait()

    return pl.pallas_call(
        body, name=name, out_shape=jax.ShapeDtypeStruct((8 * m_per, n), x.dtype),
        in_specs=[pl.BlockSpec(memory_space=pltpu.VMEM)], out_specs=pl.BlockSpec(memory_space=pltpu.VMEM),
        scratch_shapes=[pltpu.SemaphoreType.DMA((7,)), pltpu.SemaphoreType.DMA((7,)), pltpu.SemaphoreType.DMA],
    )(x)


def _gather_chips(shards, name):
    n = len(shards)

    def body(*refs):
        ins, outs = refs[:n], refs[n:2 * n]
        send_sems, recv_sems, local_sems = refs[2 * n:]
        ix, iy, ic = _mesh_pos()
        me = 2 * ix + iy
        local = [pltpu.make_async_copy(ins[a], outs[a].at[me], local_sems.at[a]) for a in range(n)]
        for cp in local:
            cp.start()
        remote = []
        for a in range(n):
            for k in (1, 2, 3):
                px, py = _flip(ix, k & 2), _flip(iy, k & 1)
                sems = dict(send_sem=send_sems.at[3 * a + k - 1], recv_sem=recv_sems.at[3 * a + k - 1],
                            device_id=(px, py, ic), device_id_type=MESH)
                out_cp = pltpu.make_async_remote_copy(src_ref=ins[a], dst_ref=outs[a].at[me], **sems)
                in_cp = pltpu.make_async_remote_copy(src_ref=ins[a], dst_ref=outs[a].at[2 * px + py], **sems)
                out_cp.start()
                remote.append((out_cp, in_cp))
        for out_cp, in_cp in remote:
            out_cp.wait_send()
            in_cp.wait_recv()
        for cp in local:
            cp.wait()

    return pl.pallas_call(
        body, name=name, in_specs=[ANY] * n, out_specs=tuple([ANY] * n),
        out_shape=tuple(jax.ShapeDtypeStruct((4,) + t.shape, t.dtype) for t in shards),
        scratch_shapes=[pltpu.SemaphoreType.DMA((3 * n,)), pltpu.SemaphoreType.DMA((3 * n,)),
                        pltpu.SemaphoreType.DMA((n,))],
    )(*shards)


HBM = pl.BlockSpec(memory_space=pltpu.HBM)
SEM = pl.BlockSpec(memory_space=pltpu.SEMAPHORE)
DATAFLOW = pltpu.SideEffectType.DATAFLOW_SIDE_EFFECTING


def _peer_chip(ix, iy, k):
    return _flip(ix, k & 2), _flip(iy, k & 1)


def _gather_chips_start(shards, name):
    n = len(shards)

    def body(*refs):
        src, land = refs[:n], refs[n:2 * n]
        send_sems, recv_sems = refs[2 * n], refs[2 * n + 1]
        token = refs[-1]
        ix, iy, ic = _mesh_pos()
        me = 2 * ix + iy
        for a in range(n):
            for k in (1, 2, 3):
                px, py = _peer_chip(ix, iy, k)
                pltpu.make_async_remote_copy(
                    src_ref=src[a], dst_ref=land[a].at[me], send_sem=send_sems.at[3 * a + k - 1],
                    recv_sem=recv_sems.at[3 * a + k - 1], device_id=(px, py, ic), device_id_type=MESH).start()
        token[...] = jnp.zeros_like(token)

    hbm = lambda shape, dtype: pltpu.HBM(shape, dtype)
    operands = ([pltpu.with_memory_space_constraint(t, pltpu.HBM) for t in shards]
                + [pltpu.with_memory_space_constraint(lax.empty((4,) + t.shape, t.dtype), pltpu.HBM) for t in shards])
    return pl.pallas_call(
        body, name=name,
        out_shape=(pltpu.SemaphoreType.DMA((3 * n,)), pltpu.SemaphoreType.DMA((3 * n,)),
                   *[hbm(t.shape, t.dtype) for t in shards], *[hbm((4,) + t.shape, t.dtype) for t in shards],
                   jax.ShapeDtypeStruct((8, LANES), F32)),
        in_specs=(HBM,) * (2 * n),
        out_specs=(SEM, SEM) + (HBM,) * (2 * n) + (pl.BlockSpec(memory_space=pltpu.VMEM),),
        input_output_aliases={a: 2 + a for a in range(2 * n)},
        compiler_params=pltpu.CompilerParams(has_side_effects=DATAFLOW),
    )(*operands)


def _gather_chips_wait(started, which, after, name):
    send_sems, recv_sems = started[0], started[1]
    thru = started[2:-1]
    n = len(thru) // 2

    def body(*refs):
        src, land = refs[:n], refs[n:2 * n]
        send_sems, recv_sems = refs[2 * n], refs[2 * n + 1]
        ix, iy, ic = _mesh_pos()
        for a in which:
            for k in (1, 2, 3):
                px, py = _peer_chip(ix, iy, k)
                cp = pltpu.make_async_remote_copy(
                    src_ref=src[a], dst_ref=land[a].at[2 * px + py], send_sem=send_sems.at[3 * a + k - 1],
                    recv_sem=recv_sems.at[3 * a + k - 1], device_id=(px, py, ic), device_id_type=MESH)
                cp.wait_send()
                cp.wait_recv()

    outs = pl.pallas_call(
        body, name=name,
        out_shape=tuple(pltpu.HBM(t.shape, t.dtype) for t in thru),
        in_specs=(HBM,) * (2 * n) + (SEM, SEM, ANY), out_specs=(HBM,) * (2 * n),
        input_output_aliases={a: a for a in range(2 * n)},
        compiler_params=pltpu.CompilerParams(has_side_effects=DATAFLOW),
    )(*thru, send_sems, recv_sems, after)
    return outs[:n], outs[n:]


BIG_KINDS = (("w_in", "col", D_MODEL, IN_COLS), ("w_out", "row", D_MODEL, D_MODEL),
             ("w_up", "col", D_MODEL, 2 * D_FF), ("w_down", "row", D_FF, D_MODEL))


def _piece_shape(how, r, c):
    return (r // 2, c // 4) if how == "col" else (r // 8, c)


def _aligned(start, multiple):
    return start if isinstance(start, int) else pl.multiple_of(start, multiple)


def _piece(ref, how, r, c, chip, half):
    if how == "col":
        return ref.at[pl.ds(_aligned(half * (r // 2), 8), r // 2), pl.ds(_aligned(chip * (c // 4), LANES), c // 4)]
    n = r // 4
    return ref.at[pl.ds(_aligned(chip * n + half * (n // 2), 8), n // 2), :]


def _rs_pair_exchange(grads, kinds, name):
    nk = len(kinds)
    specs = [BIG_KINDS[ki] for ki in kinds]
    nl = len(grads[0])
    flat = [grads[ki][l] for ki in range(nk) for l in range(nl)]
    per = nl * 4

    def body(*refs):
        g, land = refs[:nk * nl], refs[nk * nl:nk * nl + nk]
        send_sems, recv_sems = refs[nk * nl + nk:]
        ix, iy, ic = _mesh_pos()
        sibling = (ix, iy, 1 - ic)
        copies = []
        for ki, (_, how, r, c) in enumerate(specs):
            for l in range(nl):
                for j in range(4):
                    sem = ki * per + l * 4 + j
                    rem = pltpu.make_async_remote_copy(
                        src_ref=_piece(g[ki * nl + l], how, r, c, j, 1 - ic), dst_ref=land[ki].at[l, j],
                        send_sem=send_sems.at[sem], recv_sem=recv_sems.at[sem], device_id=sibling, device_id_type=MESH)
                    rem.start()
                    copies.append(rem)
        for rem in copies:
            rem.wait_send()
            rem.wait_recv()

    shapes = [jax.ShapeDtypeStruct((nl, 4) + _piece_shape(how, r, c), F32) for _, how, r, c in specs]
    return pl.pallas_call(
        body, name=name, in_specs=[ANY] * len(flat), out_specs=tuple([ANY] * nk), out_shape=tuple(shapes),
        scratch_shapes=[pltpu.SemaphoreType.DMA((nk * per,))] * 2,
    )(*flat)


def _pair_sum(g, theirs, layer, how, core, name):
    r, c = g.shape
    pr, pc = _piece_shape(how, r, c)
    if how == "col":
        mine_spec = pl.BlockSpec((pr, pc), lambda j, core_ref: (core_ref[0], j))
    else:
        mine_spec = pl.BlockSpec((pr, pc), lambda j, core_ref: (2 * j + core_ref[0], 0))

    def body(core_ref, g_ref, t_ref, o_ref, ob_ref):
        total = g_ref[...] + t_ref[...]
        o_ref[...] = total
        ob_ref[...] = total.astype(BF16)

    out_blk = pl.BlockSpec((None, pr, pc), lambda j, core_ref: (j, 0, 0))
    return pl.pallas_call(
        body, name=name,
        grid_spec=pltpu.PrefetchScalarGridSpec(
            num_scalar_prefetch=1, grid=(4,),
            in_specs=[mine_spec, pl.BlockSpec((None, None, pr, pc), lambda j, core_ref: (layer, j, 0, 0))],
            out_specs=(out_blk, out_blk)),
        out_shape=(jax.ShapeDtypeStruct((4, pr, pc), F32), jax.ShapeDtypeStruct((4, pr, pc), BF16)),
        compiler_params=_params(1))(core, g, theirs)


def _rs_chip_exchange(pair_sums, name):
    nk = len(pair_sums)
    nl = len(pair_sums[0])
    flat = [pair_sums[ki][l] for ki in range(nk) for l in range(nl)]

    def body(*refs):
        src, dst = refs[:nk * nl], refs[nk * nl:nk * nl + nk]
        send_sems, recv_sems = refs[nk * nl + nk:]
        ix, iy, ic = _mesh_pos()
        copies = []
        for ki in range(nk):
            for l in range(nl):
                for k in (1, 2, 3):
                    px, py = _peer_chip(ix, iy, k)
                    sem = (ki * nl + l) * 3 + k - 1
                    rem = pltpu.make_async_remote_copy(
                        src_ref=src[ki * nl + l].at[2 * px + py], dst_ref=dst[ki].at[l, k - 1],
                        send_sem=send_sems.at[sem], recv_sem=recv_sems.at[sem], device_id=(px, py, ic), device_id_type=MESH)
                    rem.start()
                    copies.append(rem)
        for rem in copies:
            rem.wait_send()
            rem.wait_recv()

    return pl.pallas_call(
        body, name=name, in_specs=[ANY] * len(flat), out_specs=tuple([ANY] * nk),
        out_shape=tuple(jax.ShapeDtypeStruct((nl, 3) + pair_sums[ki][0].shape[1:], pair_sums[ki][0].dtype)
                        for ki in range(nk)),
        scratch_shapes=[pltpu.SemaphoreType.DMA((nk * nl * 3,))] * 2,
    )(*flat)


def _rs_chip_start(pieces, name):
    n = len(pieces)

    def body(*refs):
        src, land = refs[:n], refs[n:2 * n]
        send_sems, recv_sems = refs[2 * n], refs[2 * n + 1]
        token = refs[-1]
        ix, iy, ic = _mesh_pos()
        for a in range(n):
            for k in (1, 2, 3):
                px, py = _peer_chip(ix, iy, k)
                pltpu.make_async_remote_copy(
                    src_ref=src[a].at[2 * px + py], dst_ref=land[a].at[k - 1], send_sem=send_sems.at[3 * a + k - 1],
                    recv_sem=recv_sems.at[3 * a + k - 1], device_id=(px, py, ic), device_id_type=MESH).start()
        token[...] = jnp.zeros_like(token)

    land_shape = lambda t: (3,) + t.shape[1:]
    operands = ([pltpu.with_memory_space_constraint(t, pltpu.HBM) for t in pieces]
                + [pltpu.with_memory_space_constraint(lax.empty(land_shape(t), t.dtype), pltpu.HBM) for t in pieces])
    return pl.pallas_call(
        body, name=name,
        out_shape=(pltpu.SemaphoreType.DMA((3 * n,)), pltpu.SemaphoreType.DMA((3 * n,)),
                   *[pltpu.HBM(t.shape, t.dtype) for t in pieces], *[pltpu.HBM(land_shape(t), t.dtype) for t in pieces],
                   jax.ShapeDtypeStruct((8, LANES), F32)),
        in_specs=(HBM,) * (2 * n),
        out_specs=(SEM, SEM) + (HBM,) * (2 * n) + (pl.BlockSpec(memory_space=pltpu.VMEM),),
        input_output_aliases={a: 2 + a for a in range(2 * n)},
        compiler_params=pltpu.CompilerParams(has_side_effects=DATAFLOW),
    )(*operands)


def _rs_chip_wait(started, after, name):
    send_sems, recv_sems = started[0], started[1]
    thru = started[2:-1]
    n = len(thru) // 2

    def body(*refs):
        src, land = refs[:n], refs[n:2 * n]
        send_sems, recv_sems = refs[2 * n], refs[2 * n + 1]
        ix, iy, ic = _mesh_pos()
        for a in range(n):
            for k in (1, 2, 3):
                px, py = _peer_chip(ix, iy, k)
                cp = pltpu.make_async_remote_copy(
                    src_ref=src[a].at[2 * px + py], dst_ref=land[a].at[k - 1], send_sem=send_sems.at[3 * a + k - 1],
                    recv_sem=recv_sems.at[3 * a + k - 1], device_id=(px, py, ic), device_id_type=MESH)
                cp.wait_send()
                cp.wait_recv()

    outs = pl.pallas_call(
        body, name=name,
        out_shape=tuple(pltpu.HBM(t.shape, t.dtype) for t in thru),
        in_specs=(HBM,) * (2 * n) + (SEM, SEM, ANY), out_specs=(HBM,) * (2 * n),
        input_output_aliases={a: a for a in range(2 * n)},
        compiler_params=pltpu.CompilerParams(has_side_effects=DATAFLOW),
    )(*thru, send_sems, recv_sems, after)
    return outs[n:]


def _chip_sum(own, others, chip, name):
    _, pr, pc = own.shape

    def body(chip_ref, own_ref, s1, s2, s3, o_ref):
        o_ref[...] = ((own_ref[...] + s1[...].astype(F32)) + s2[...].astype(F32)) + s3[...].astype(F32)

    slot = lambda k: pl.BlockSpec((None, pr, pc), lambda i, chip_ref: (k, 0, 0))
    return pl.pallas_call(
        body, name=name,
        grid_spec=pltpu.PrefetchScalarGridSpec(
            num_scalar_prefetch=1, grid=(1,),
            in_specs=[pl.BlockSpec((None, pr, pc), lambda i, chip_ref: (chip_ref[0], 0, 0)), slot(0), slot(1), slot(2)],
            out_specs=pl.BlockSpec((pr, pc), lambda i, chip_ref: (0, 0))),
        out_shape=jax.ShapeDtypeStruct((pr, pc), F32), compiler_params=_params(1))(chip, own, others, others, others)


def _rs_pair_share(halves, name):
    nk = len(halves)
    flat = [halves[ki][l] for ki in range(nk) for l in range(DEPTH)]

    def body(*refs):
        src, dst = refs[:nk * DEPTH], refs[nk * DEPTH:nk * DEPTH + nk]
        send_sems, recv_sems = refs[nk * DEPTH + nk:]
        ix, iy, ic = _mesh_pos()
        copies = []
        for ki in range(nk):
            for l in range(DEPTH):
                sem = ki * DEPTH + l
                rem = pltpu.make_async_remote_copy(
                    src_ref=src[sem], dst_ref=dst[ki].at[l], send_sem=send_sems.at[sem], recv_sem=recv_sems.at[sem],
                    device_id=(ix, iy, 1 - ic), device_id_type=MESH)
                rem.start()
                copies.append(rem)
        for rem in copies:
            rem.wait_send()
            rem.wait_recv()

    return pl.pallas_call(
        body, name=name, in_specs=[ANY] * len(flat), out_specs=tuple([ANY] * nk),
        out_shape=tuple(jax.ShapeDtypeStruct((DEPTH,) + halves[ki][0].shape, F32) for ki in range(nk)),
        scratch_shapes=[pltpu.SemaphoreType.DMA((nk * DEPTH,))] * 2,
    )(*flat)


def _adamw_halves(w, mine, theirs, m, v, core, name):
    nl, pr, pc = theirs.shape
    shape = w.shape
    view = lambda t: t.reshape(nl, 2, pr, pc)
    tr = _row_tile(pr, pc, 256 * 1024)

    def body(core_ref, w_ref, a0_ref, a1_ref, t_ref, m_ref, v_ref, g_ref, d_ref, m2_ref, v2_ref):
        own = jnp.where(pl.program_id(0) == 0, a0_ref[...], a1_ref[...])
        g = jnp.where(pl.program_id(1) == core_ref[0], own, t_ref[...])
        g_ref[...] = g
        d_ref[...], m2_ref[...], v2_ref[...] = _adamw_math(w_ref[...], g, m_ref[...], v_ref[...])

    blk = pl.BlockSpec((None, None, tr, pc), lambda l, h, i, core_ref: (l, h, i, 0))
    own_blk = pl.BlockSpec((tr, pc), lambda l, h, i, core_ref: (i, 0))
    out = jax.ShapeDtypeStruct((nl, 2, pr, pc), F32)
    outs = pl.pallas_call(
        body, name=name,
        grid_spec=pltpu.PrefetchScalarGridSpec(
            num_scalar_prefetch=1, grid=(nl, 2, pr // tr),
            in_specs=[blk, own_blk, own_blk, pl.BlockSpec((None, tr, pc), lambda l, h, i, core_ref: (l, i, 0)), blk, blk],
            out_specs=(blk,) * 4),
        out_shape=(out,) * 4, compiler_params=_params(3),
    )(core, view(w), mine[0], mine[1], theirs, view(m), view(v))
    return tuple(t.reshape(shape) for t in outs)


def _rs_first_stages(layer_grads, kinds, core, tag, in_flight):
    theirs = _rs_pair_exchange([[g] for g in layer_grads], kinds, f"rs_pair_exchange_{tag}")
    sums = [_pair_sum(g, theirs[n], 0, BIG_KINDS[ki][1], core, f"rs_pair_sum_{BIG_KINDS[ki][0]}")
            for n, (ki, g) in enumerate(zip(kinds, layer_grads))]
    to_send = [both[1] for both in sums]
    if in_flight:
        return [both[0] for both in sums], _rs_chip_start(to_send, f"rs_chip_start_{tag}")
    slots = _rs_chip_exchange([[t] for t in to_send], f"rs_chip_exchange_{tag}")
    return [both[0] for both in sums], [t[0] for t in slots]


def _rs_last_stages(pair_sums, slots, chip):
    halves = [[_chip_sum(pair_sums[ki][l], slots[ki][l], chip, f"rs_chip_sum_{kind}") for l in range(DEPTH)]
              for ki, (kind, _, _, _) in enumerate(BIG_KINDS)]
    other = _rs_pair_share(halves, "rs_pair_share")
    return list(zip(halves, other))


WEIGHT_NAMES = ("w_ada", "b_ada", "norm1_w", "w_in", "conv_a_w", "conv_a_b", "ln_a_w", "ln_a_b", "lb_gamma",
                "rec_norm_w", "w_out", "norm2_w", "w_up", "conv_f_w", "w_down", "final_norm_w")
SMALL_PARAMS = (("b_ada", (DEPTH, N_MOD * D_MODEL), None), ("norm1_w", (DEPTH, D_MODEL), None),
                ("conv_a_w", (DEPTH, CONV_WIDTH, CONV_CH), 2), ("conv_a_b", (DEPTH, CONV_CH), None),
                ("ln_a_w", (DEPTH, CONV_CH), None), ("ln_a_b", (DEPTH, CONV_CH), None),
                ("lb_gamma", (DEPTH, 2, REC_WIDTH), 2), ("rec_norm_w", (DEPTH, REC_WIDTH), None),
                ("norm2_w", (DEPTH, D_MODEL), None), ("conv_f_w", (DEPTH, 3, 2 * D_FF), 2),
                ("final_norm_w", (D_MODEL,), None))


def _pack_rows(parts):
    flat = jnp.concatenate([p.reshape(-1) for p in parts])
    total = flat.shape[0]
    padded = -(-total // (8 * LANES)) * (8 * LANES)
    return jnp.pad(flat, (0, padded - total)).reshape(padded // LANES, LANES)


def _unpack(flat, shapes):
    out, off = [], 0
    for shp in shapes:
        size = int(np.prod(shp))
        out.append(flat[off:off + size].reshape(shp))
        off += size
    return out


def _unstack_chips(t, axis):
    return jnp.concatenate([t[j] for j in range(4)], axis=axis)


def kernel(x, c, w_ada, b_ada, norm1_w, w_in, conv_a_w, conv_a_b, ln_a_w, ln_a_b, lb_gamma, rec_norm_w, w_out, norm2_w, w_up, conv_f_w, w_down, final_norm_w, loss_target, m_w_ada, m_b_ada, m_norm1_w, m_w_in, m_conv_a_w, m_conv_a_b, m_ln_a_w, m_ln_a_b, m_lb_gamma, m_rec_norm_w, m_w_out, m_norm2_w, m_w_up, m_conv_f_w, m_w_down, m_final_norm_w, v_w_ada, v_b_ada, v_norm1_w, v_w_in, v_conv_a_w, v_conv_a_b, v_ln_a_w, v_ln_a_b, v_lb_gamma, v_rec_norm_w, v_w_out, v_norm2_w, v_w_up, v_conv_f_w, v_w_down, v_final_norm_w):
    params = dict(zip(WEIGHT_NAMES, (w_ada, b_ada, norm1_w, w_in, conv_a_w, conv_a_b, ln_a_w, ln_a_b, lb_gamma,
                                     rec_norm_w, w_out, norm2_w, w_up, conv_f_w, w_down, final_norm_w)))
    mom1 = dict(zip(WEIGHT_NAMES, (m_w_ada, m_b_ada, m_norm1_w, m_w_in, m_conv_a_w, m_conv_a_b, m_ln_a_w, m_ln_a_b,
                                   m_lb_gamma, m_rec_norm_w, m_w_out, m_norm2_w, m_w_up, m_conv_f_w, m_w_down,
                                   m_final_norm_w)))
    mom2 = dict(zip(WEIGHT_NAMES, (v_w_ada, v_b_ada, v_norm1_w, v_w_in, v_conv_a_w, v_conv_a_b, v_ln_a_w, v_ln_a_b,
                                   v_lb_gamma, v_rec_norm_w, v_w_out, v_norm2_w, v_w_up, v_conv_f_w, v_w_down,
                                   v_final_norm_w)))
    ix, iy, ic = _mesh_pos()
    chip = 2 * ix + iy
    dev = 2 * chip + ic

    c_all = _allgather_devices(c.reshape(8, LANES), "gather_cond").reshape(8, D_MODEL)
    b_sh = lax.dynamic_slice_in_dim(b_ada, chip * ADA_SHARD, ADA_SHARD, axis=1)
    mod_sh = _ada_mod(c_all, w_ada, b_sh.reshape(DEPTH, 1, ADA_SHARD), "ada_mod")
    w_in_b, w_out_b, w_up_b, w_down_b = (t.astype(BF16) for t in (w_in, w_out, w_up, w_down))
    first = _gather_chips([mod_sh, conv_a_w, conv_f_w, lb_gamma], "gather_first")
    later = [w_in_b[0], w_out_b, w_up_b, w_down_b, w_in_b[1]]
    started = _gather_chips_start(later, "gather_weights_start")
    mod_mine = lax.dynamic_index_in_dim(first[0], dev, axis=2, keepdims=False) + started[-1][0, 0]
    mods = [jnp.concatenate([mod_mine[j, l] for j in range(4)]).reshape(N_MOD, D_MODEL) for l in range(DEPTH)]
    conv_a_w_f, conv_f_w_f, gamma_f = (_unstack_chips(first[k], 2) for k in (1, 2, 3))

    flying = [started]

    def later_weights(stage, after):
        which = ([0], [1], [2, 3, 4])[stage]
        own, lands = _gather_chips_wait(flying[0], which, after, f"gather_weights_wait_{stage}")
        flying[0] = (started[0], started[1], *own, *lands, None)
        whole = lambda n, axis: jnp.concatenate([jnp.where(chip == j, own[n], lands[n][j]) for j in range(4)], axis=axis)
        if stage == 0:
            return whole(0, 1)
        if stage == 1:
            return whole(1, 1)
        return whole(4, 1), whole(2, 2), whole(3, 1)

    lb1, p_soft = _lower_bounds(gamma_f.reshape(DEPTH, 2 * REC_WIDTH), "lower_bounds")
    lbs = [jnp.zeros((2, REC_WIDTH), F32), lb1.reshape(2, REC_WIDTH)]
    small = []
    for l in range(DEPTH):
        small.append(dict(norm1_w=norm1_w[l][None], conv_a_w=conv_a_w_f[l], conv_a_b=conv_a_b[l][None],
                          ln_a_w=ln_a_w[l][None], ln_a_b=ln_a_b[l][None], rec_norm_w=rec_norm_w[l],
                          norm2_w=norm2_w[l][None], conv_f_w=conv_f_w_f[l]))

    core_id, chip_id = ic.astype(jnp.int32).reshape(1), chip.astype(jnp.int32).reshape(1)
    pending, groups = {}, []

    def on_layer_grads(l, by_kind, last):
        if l > 0:
            pending.update(by_kind)
            if not last:
                return None
            by_kind = dict(pending)
        kinds = sorted(by_kind)
        in_flight = not (l == 0 and last)
        tag = f"l{l}" if l > 0 else f"l{l}_{'mix' if last else 'ffn'}"
        sums, exchange = _rs_first_stages([by_kind[k] for k in kinds], kinds, core_id, tag, in_flight)
        groups.append((l, kinds, sums, exchange, in_flight, tag))
        return exchange[-1][0:1, 0:1] if in_flight else None

    loss, dx, grads, dfw = _sequence_step(x[0], loss_target[0], mods, lbs, small, later_weights,
                                          final_norm_w[None], on_layer_grads)
    loss = lax.psum(loss, ("x", "y", "c"))
    pair_sums = [[None] * DEPTH for _ in BIG_KINDS]
    slots = [[None] * DEPTH for _ in BIG_KINDS]
    for l, kinds, sums, exchange, in_flight, tag in groups:
        received = _rs_chip_wait(exchange, dx, f"rs_chip_wait_{tag}") if in_flight else exchange
        for n, ki in enumerate(kinds):
            pair_sums[ki][l], slots[ki][l] = sums[n], received[n]

    dgamma = _lower_bounds_bwd(grads[1]["lb"].reshape(1, 2 * REC_WIDTH), p_soft, "lower_bounds_bwd")
    dmod = [jnp.concatenate(grads[l]["mod"], axis=1) for l in range(DEPTH)]
    stack = lambda key: jnp.stack([grads[l][key] for l in range(DEPTH)])
    local_small = dict(b_ada=jnp.concatenate(dmod, axis=0), norm1_w=stack("norm1_w"), conv_a_w=stack("conv_a_w"),
                       conv_a_b=stack("conv_a_b"), ln_a_w=stack("ln_a_w"), ln_a_b=stack("ln_a_b"), lb_gamma=dgamma,
                       rec_norm_w=stack("rec_norm_w"), norm2_w=stack("norm2_w"), conv_f_w=stack("conv_f_w"),
                       final_norm_w=dfw)
    pack = _pack_rows([local_small[name] for name, _, _ in SMALL_PARAMS])
    rows = pack.shape[0]
    packs = _allgather_devices(pack, "gather_small_grads").reshape(8, rows, LANES)
    summed = _sum_devices(packs, "sum_small_grads").reshape(-1)
    small_grads = dict(zip([n for n, _, _ in SMALL_PARAMS], _unpack(summed, [shp for _, shp, _ in SMALL_PARAMS])))

    dmod_all = packs.reshape(8, rows * LANES)[:, :DEPTH * N_MOD * D_MODEL].reshape(8, DEPTH, N_MOD * D_MODEL)
    dmod_sh = lax.dynamic_slice_in_dim(dmod_all, chip * ADA_SHARD, ADA_SHARD, axis=2).transpose(1, 0, 2)
    g_ada, d_ada, m_ada, v_ada = _ada_update(c_all, dmod_sh, w_ada, m_w_ada, v_w_ada, "ada_update")

    for name, shp, axis in SMALL_PARAMS:
        if axis is not None:
            width = shp[axis] // 4
            small_grads[name] = lax.dynamic_slice_in_dim(small_grads[name], chip * width, width, axis=axis)
    names = [n for n, _, _ in SMALL_PARAMS]
    packed = [_pack_rows([src[n] for n in names])[None] for src in (params, small_grads, mom1, mom2)]
    small_out = _adamw(*packed, "adamw_small")
    shapes = [params[n].shape for n in names]
    small_delta, small_m, small_v = (dict(zip(names, _unpack(t.reshape(-1), shapes))) for t in small_out)

    summed_big = _rs_last_stages(pair_sums, slots, chip_id)
    grad, delta, new_m, new_v = dict(small_grads), small_delta, small_m, small_v
    grad["w_ada"], delta["w_ada"], new_m["w_ada"], new_v["w_ada"] = g_ada, d_ada, m_ada, v_ada
    for (name, _, _, _), (mine, theirs) in zip(BIG_KINDS, summed_big):
        grad[name], delta[name], new_m[name], new_v[name] = _adamw_halves(
            params[name], mine, theirs, mom1[name], mom2[name], core_id, f"adamw_{name}")

    return (loss, dx[None], *[grad[n] for n in WEIGHT_NAMES], *[delta[n] for n in WEIGHT_NAMES],
            *[new_m[n] for n in WEIGHT_NAMES], *[new_v[n] for n in WEIGHT_NAMES])
```

```python
import numpy as np
import jax
import jax.numpy as jnp
from jax import lax
from jax.experimental import pallas as pl
from jax.experimental.pallas import tpu as pltpu

F32 = jnp.float32
BF16 = jnp.bfloat16

D_MODEL = 1024
DEPTH = 2
HEAD_DIM = 64
CONV_CH = 256
CONV_WIDTH = 31
ATT_WIDTH = 384
N_HEADS = 6
DILATIONS = (1, 4, 16)
ATT_HALF = 64
ATT_BLOCK = 128
ALIBI_MAX_EXP = 8.0
MASK_VALUE = -1e30
REC_WIDTH = 384
REC_CHUNK = 64
F_TINY = 1e-30
D_FF = 2816
N_MOD = 6
EPS = 1e-6
G_CONV = (0, 512)
G_QKV = (512, 1664)
G_REC = (1664, 3584)
IN_COLS = 3584

ADAM_LR = 0.001
ADAM_B1 = 0.9
ADAM_B2 = 0.999
ADAM_EPS = 1e-08
ADAM_WD = 0.01
ADAM_STEP = 10

VMEM_LIMIT_BYTES = 56 * 1024 * 1024
LANES = 128
MESH = pl.DeviceIdType.MESH
ANY = pl.BlockSpec(memory_space=pl.ANY)


def _params(n_axes):
    return pltpu.CompilerParams(dimension_semantics=("arbitrary",) * n_axes,
                                vmem_limit_bytes=VMEM_LIMIT_BYTES)


def _tile(n, target):
    best = None
    for t in range(LANES, min(n, target) + 1, LANES):
        if n % t == 0:
            best = t
    return best or n


def _sigmoid(x):
    return jax.nn.sigmoid(x)


def _silu_grad(x):
    s = _sigmoid(x)
    return s * (1.0 + x * (1.0 - s))


MM_ACC_ELEMS = 1536 * 1024


def _matmul(a, b, mode, out_dtype, name, tm=1024, tn=1792, tk=1792):
    if mode == "nn":
        (m, k), (k2, n) = a.shape, b.shape
    elif mode == "nt":
        (m, k), (n, k2) = a.shape, b.shape
    else:
        (k, m), (k2, n) = a.shape, b.shape
    assert k == k2, (a.shape, b.shape, mode)
    tn, tk = _tile(n, tn), _tile(k, tk)
    tm = _tile(m, min(tm, MM_ACC_ELEMS // tn))
    nk = k // tk
    a_spec = (pl.BlockSpec((tk, tm), lambda i, j, kk: (kk, i)) if mode == "tn"
              else pl.BlockSpec((tm, tk), lambda i, j, kk: (i, kk)))
    b_spec = (pl.BlockSpec((tn, tk), lambda i, j, kk: (j, kk)) if mode == "nt"
              else pl.BlockSpec((tk, tn), lambda i, j, kk: (kk, j)))
    dims = {"nn": (((1,), (0,)), ((), ())), "nt": (((1,), (1,)), ((), ())),
            "tn": (((0,), (0,)), ((), ()))}[mode]

    def body(a_ref, b_ref, o_ref, *scratch):
        part = lax.dot_general(a_ref[...].astype(BF16), b_ref[...].astype(BF16), dims, preferred_element_type=F32)
        if nk == 1:
            o_ref[...] = part.astype(out_dtype)
            return
        acc_ref, = scratch
        kk = pl.program_id(2)

        @pl.when(kk == 0)
        def _():
            acc_ref[...] = part

        @pl.when(kk > 0)
        def _():
            acc_ref[...] += part

        @pl.when(kk == nk - 1)
        def _():
            o_ref[...] = acc_ref[...].astype(out_dtype)

    return pl.pallas_call(
        body, name=name, grid=(m // tm, n // tn, nk),
        in_specs=[a_spec, b_spec],
        out_specs=pl.BlockSpec((tm, tn), lambda i, j, kk: (i, j)),
        out_shape=jax.ShapeDtypeStruct((m, n), out_dtype),
        scratch_shapes=[pltpu.VMEM((tm, tn), F32)] if nk > 1 else [],
        compiler_params=pltpu.CompilerParams(dimension_semantics=("parallel", "parallel", "arbitrary"),
                                             vmem_limit_bytes=VMEM_LIMIT_BYTES),
    )(a, b)


def _matmul_two_lhs(a1, a2, b, out_dtype, name):
    (m, k1), n = a1.shape, b.shape[0]
    tn, tk = _tile(n, 1792), _tile(k1, 1792)
    tm = _tile(m, min(1024, MM_ACC_ELEMS // tn))
    nk1 = k1 // tk
    nk = 2 * nk1

    def body(a1_ref, a2_ref, b_ref, o_ref, acc_ref):
        kk = pl.program_id(2)
        lhs = jnp.where(kk < nk1, a1_ref[...], a2_ref[...])
        part = lax.dot_general(lhs, b_ref[...], (((1,), (1,)), ((), ())), preferred_element_type=F32)

        @pl.when(kk == 0)
        def _():
            acc_ref[...] = part

        @pl.when(kk > 0)
        def _():
            acc_ref[...] += part

        @pl.when(kk == nk - 1)
        def _():
            o_ref[...] = acc_ref[...].astype(out_dtype)

    return pl.pallas_call(
        body, name=name, grid=(m // tm, n // tn, nk),
        in_specs=[pl.BlockSpec((tm, tk), lambda i, j, kk: (i, jnp.minimum(kk, nk1 - 1))),
                  pl.BlockSpec((tm, tk), lambda i, j, kk: (i, jnp.maximum(kk - nk1, 0))),
                  pl.BlockSpec((tn, tk), lambda i, j, kk: (j, kk))],
        out_specs=pl.BlockSpec((tm, tn), lambda i, j, kk: (i, j)),
        out_shape=jax.ShapeDtypeStruct((m, n), out_dtype),
        scratch_shapes=[pltpu.VMEM((tm, tn), F32)],
        compiler_params=pltpu.CompilerParams(dimension_semantics=("parallel", "parallel", "arbitrary"),
                                             vmem_limit_bytes=VMEM_LIMIT_BYTES),
    )(a1, a2, b)


def _matmul_two_rhs(a, b1, b2, out_dtype, name):
    (k, m), n1 = a.shape, b1.shape[1]
    tn, tk = _tile(n1, 1792), _tile(k, 1792)
    tm = _tile(m, min(1024, MM_ACC_ELEMS // tn))
    nj1, nk = n1 // tn, k // tk

    def body(a_ref, b1_ref, b2_ref, o_ref, acc_ref):
        j, kk = pl.program_id(1), pl.program_id(2)
        rhs = jnp.where(j < nj1, b1_ref[...], b2_ref[...])
        part = lax.dot_general(a_ref[...], rhs, (((0,), (0,)), ((), ())), preferred_element_type=F32)

        @pl.when(kk == 0)
        def _():
            acc_ref[...] = part

        @pl.when(kk > 0)
        def _():
            acc_ref[...] += part

        @pl.when(kk == nk - 1)
        def _():
            o_ref[...] = acc_ref[...].astype(out_dtype)

    return pl.pallas_call(
        body, name=name, grid=(m // tm, 2 * nj1, nk),
        in_specs=[pl.BlockSpec((tk, tm), lambda i, j, kk: (kk, i)),
                  pl.BlockSpec((tk, tn), lambda i, j, kk: (jnp.where(j < nj1, kk, 0), jnp.minimum(j, nj1 - 1))),
                  pl.BlockSpec((tk, tn), lambda i, j, kk: (jnp.where(j < nj1, 0, kk), jnp.maximum(j - nj1, 0)))],
        out_specs=pl.BlockSpec((tm, tn), lambda i, j, kk: (i, j)),
        out_shape=jax.ShapeDtypeStruct((m, 2 * n1), out_dtype),
        scratch_shapes=[pltpu.VMEM((tm, tn), F32)],
        compiler_params=pltpu.CompilerParams(dimension_semantics=("parallel", "parallel", "arbitrary"),
                                             vmem_limit_bytes=VMEM_LIMIT_BYTES),
    )(a, b1, b2)


NORM_ROWS = 256


def _row_spec(width, rows=NORM_ROWS):
    return pl.BlockSpec((rows, width), lambda i: (i, 0))


def _vec_spec(width):
    return pl.BlockSpec((1, width), lambda i: (0, 0))


def _resid_norm_mod(x, r, g, nw, sc, sh, name):
    s, d = x.shape
    has_r = r is not None

    def body(*refs):
        if has_r:
            x_ref, r_ref, g_ref, nw_ref, sc_ref, sh_ref, xn_ref, h_ref = refs
            xn = x_ref[...] + g_ref[...] * r_ref[...].astype(F32)
            xn_ref[...] = xn
        else:
            x_ref, nw_ref, sc_ref, sh_ref, h_ref = refs
            xn = x_ref[...]
        rstd = lax.rsqrt(jnp.mean(xn * xn, axis=-1, keepdims=True) + EPS)
        y = xn * rstd * nw_ref[...]
        h_ref[...] = (y * (1.0 + sc_ref[...]) + sh_ref[...]).astype(BF16)

    if has_r:
        ins, in_specs = (x, r, g, nw, sc, sh), [_row_spec(d), _row_spec(d)] + [_vec_spec(d)] * 4
        out_shape = (jax.ShapeDtypeStruct((s, d), F32), jax.ShapeDtypeStruct((s, d), BF16))
        out_specs = (_row_spec(d), _row_spec(d))
    else:
        ins, in_specs = (x, nw, sc, sh), [_row_spec(d)] + [_vec_spec(d)] * 3
        out_shape = jax.ShapeDtypeStruct((s, d), BF16)
        out_specs = _row_spec(d)
    return pl.pallas_call(body, name=name, grid=(s // NORM_ROWS,), in_specs=in_specs, out_specs=out_specs,
                          out_shape=out_shape, compiler_params=_params(1))(*ins)


def _final_loss(x, r, g, fw, tgt, name):
    s, d = x.shape

    def body(x_ref, r_ref, g_ref, fw_ref, t_ref, loss_ref, dx_ref, dr_ref, dg_ref, dfw_ref):
        @pl.when(pl.program_id(0) == 0)
        def _():
            loss_ref[...] = jnp.zeros_like(loss_ref)
            dg_ref[...] = jnp.zeros_like(dg_ref)
            dfw_ref[...] = jnp.zeros_like(dfw_ref)

        rr = r_ref[...].astype(F32)
        gg = g_ref[...]
        xn = x_ref[...] + gg * rr
        rstd = lax.rsqrt(jnp.mean(xn * xn, axis=-1, keepdims=True) + EPS)
        xh = xn * rstd
        fwv = fw_ref[...]
        e = xh * fwv - t_ref[...]
        loss_ref[...] += 0.5 * jnp.sum(jnp.mean(e * e, axis=-1, keepdims=True), axis=0, keepdims=True)
        dy = e * (1.0 / d)
        dfw_ref[...] += jnp.sum(dy * xh, axis=0, keepdims=True)
        dxh = dy * fwv
        dx = rstd * (dxh - xh * jnp.mean(dxh * xh, axis=-1, keepdims=True))
        dx_ref[...] = dx
        dr_ref[...] = (gg * dx).astype(BF16)
        dg_ref[...] += jnp.sum(dx * rr, axis=0, keepdims=True)

    return pl.pallas_call(
        body, name=name, grid=(s // NORM_ROWS,),
        in_specs=[_row_spec(d), _row_spec(d), _vec_spec(d), _vec_spec(d), _row_spec(d)],
        out_specs=(_vec_spec(LANES), _row_spec(d), _row_spec(d), _vec_spec(d), _vec_spec(d)),
        out_shape=(jax.ShapeDtypeStruct((1, LANES), F32), jax.ShapeDtypeStruct((s, d), F32),
                   jax.ShapeDtypeStruct((s, d), BF16), jax.ShapeDtypeStruct((1, d), F32),
                   jax.ShapeDtypeStruct((1, d), F32)),
        compiler_params=_params(1))(x, r, g, fw, tgt)


def _norm_bwd(x, dhs, dxres, nw, sc, g, r, name):
    s, d = x.shape
    n_dh = len(dhs)
    has_g = g is not None

    def body(*refs):
        x_ref = refs[0]
        dh_refs = refs[1:1 + n_dh]
        dxres_ref, nw_ref, sc_ref = refs[1 + n_dh:4 + n_dh]
        pos = 4 + n_dh
        if has_g:
            g_ref, r_ref = refs[pos:pos + 2]
            pos += 2
            dx_ref, dr_ref, dsh_ref, dsc_ref, dnw_ref, dg_ref = refs[pos:]
            accs = (dsh_ref, dsc_ref, dnw_ref, dg_ref)
        else:
            dx_ref, dsh_ref, dsc_ref, dnw_ref = refs[pos:]
            accs = (dsh_ref, dsc_ref, dnw_ref)

        @pl.when(pl.program_id(0) == 0)
        def _():
            for acc in accs:
                acc[...] = jnp.zeros_like(acc)

        xv = x_ref[...]
        dh = dh_refs[0][...].astype(F32)
        for extra in dh_refs[1:]:
            dh = dh + extra[...].astype(F32)
        rstd = lax.rsqrt(jnp.mean(xv * xv, axis=-1, keepdims=True) + EPS)
        xh = xv * rstd
        nwv = nw_ref[...]
        dsh_ref[...] += jnp.sum(dh, axis=0, keepdims=True)
        dsc_ref[...] += jnp.sum(dh * (xh * nwv), axis=0, keepdims=True)
        dy = dh * (1.0 + sc_ref[...])
        dnw_ref[...] += jnp.sum(dy * xh, axis=0, keepdims=True)
        dxh = dy * nwv
        dx = dxres_ref[...] + rstd * (dxh - xh * jnp.mean(dxh * xh, axis=-1, keepdims=True))
        dx_ref[...] = dx
        if has_g:
            dr_ref[...] = (g_ref[...] * dx).astype(BF16)
            dg_ref[...] += jnp.sum(dx * r_ref[...].astype(F32), axis=0, keepdims=True)

    ins = [x, *dhs, dxres, nw, sc]
    in_specs = [_row_spec(d)] * (2 + n_dh) + [_vec_spec(d)] * 2
    out_shape = [jax.ShapeDtypeStruct((s, d), F32)]
    out_specs = [_row_spec(d)]
    if has_g:
        ins += [g, r]
        in_specs += [_vec_spec(d), _row_spec(d)]
        out_shape.append(jax.ShapeDtypeStruct((s, d), BF16))
        out_specs.append(_row_spec(d))
    n_vec = 4 if has_g else 3
    out_shape += [jax.ShapeDtypeStruct((1, d), F32)] * n_vec
    out_specs += [_vec_spec(d)] * n_vec
    return pl.pallas_call(body, name=name, grid=(s // NORM_ROWS,), in_specs=in_specs, out_specs=tuple(out_specs),
                          out_shape=tuple(out_shape), compiler_params=_params(1))(*ins)


FFN_ROWS = 256
FFN_COLS = 1408
HALO = 16
INV_SQRT2 = 0.7071067811865476
INV_SQRT_2PI = 0.3989422804014327


def _gelu(x):
    return 0.5 * x * (1.0 + lax.erf(x * INV_SQRT2))


def _gelu_grad(x):
    return 0.5 * (1.0 + lax.erf(x * INV_SQRT2)) + x * (INV_SQRT_2PI * jnp.exp(-0.5 * x * x))


def _halo_specs(rows, cols, halo, n_rows_total, col_of):
    per = rows // halo
    last = n_rows_total // halo - 1
    cur = pl.BlockSpec((rows, cols), lambda j, i: (i, col_of(j)))
    prev = pl.BlockSpec((halo, cols), lambda j, i: (jnp.maximum(i * per - 1, 0), col_of(j)))
    nxt = pl.BlockSpec((halo, cols), lambda j, i: (jnp.minimum((i + 1) * per, last), col_of(j)))
    return [prev, cur, nxt]


def _shift_rows(x, k):
    n = x.shape[0]
    return pltpu.roll(x, k % n, axis=0)


def _conv3(ext, w):
    return w[0:1, :] * _shift_rows(ext, 1) + w[1:2, :] * ext + w[2:3, :] * _shift_rows(ext, -1)


def _ext_block(prev_ref, cur_ref, next_ref, i, n_i):
    prev = jnp.where(i > 0, prev_ref[...].astype(F32), 0.0)
    nxt = jnp.where(i < n_i - 1, next_ref[...].astype(F32), 0.0)
    return jnp.concatenate([prev, cur_ref[...].astype(F32), nxt], axis=0)


def _ffn_act(u, cw, name):
    s = u.shape[0]
    nc, ns = D_FF // FFN_COLS, s // FFN_ROWS

    def body(gp, gc, gn, vp, vc, vn, wg_ref, wv_ref, o_ref, cg_ref, cv_ref):
        i = pl.program_id(1)
        cg = _conv3(_ext_block(gp, gc, gn, i, ns), wg_ref[...])[HALO:HALO + FFN_ROWS]
        cv = _conv3(_ext_block(vp, vc, vn, i, ns), wv_ref[...])[HALO:HALO + FFN_ROWS]
        o_ref[...] = (_gelu(cg) * cv).astype(BF16)
        cg_ref[...] = cg.astype(BF16)
        cv_ref[...] = cv.astype(BF16)

    in_specs = (_halo_specs(FFN_ROWS, FFN_COLS, HALO, s, lambda j: j)
                + _halo_specs(FFN_ROWS, FFN_COLS, HALO, s, lambda j: j + nc)
                + [pl.BlockSpec((3, FFN_COLS), lambda j, i: (0, j)),
                   pl.BlockSpec((3, FFN_COLS), lambda j, i: (0, j + nc))])
    blk = pl.BlockSpec((FFN_ROWS, FFN_COLS), lambda j, i: (i, j))
    return pl.pallas_call(
        body, name=name, grid=(nc, ns), in_specs=in_specs, out_specs=(blk, blk, blk),
        out_shape=(jax.ShapeDtypeStruct((s, D_FF), BF16),) * 3, compiler_params=_params(2),
    )(u, u, u, u, u, u, cw, cw)


def _ffn_act_bwd(u, cg, cv, dact, cw, name):
    s = u.shape[0]
    nc, ns = D_FF // FFN_COLS, s // FFN_ROWS

    def body(ug_ref, uv_ref, gp, gc, gn, vp, vc, vn, dp, dc, dn, wg_ref, wv_ref, dug_ref, duv_ref, dwg_ref, dwv_ref):
        i = pl.program_id(1)

        @pl.when(i == 0)
        def _():
            dwg_ref[...] = jnp.zeros_like(dwg_ref)
            dwv_ref[...] = jnp.zeros_like(dwv_ref)

        cge = _ext_block(gp, gc, gn, i, ns)
        cve = _ext_block(vp, vc, vn, i, ns)
        da = _ext_block(dp, dc, dn, i, ns)
        dcg = da * cve * _gelu_grad(cge)
        dcv = da * _gelu(cge)
        inner = slice(HALO, HALO + FFN_ROWS)
        for d_c, u_ref, w_ref, du_ref, dw_ref in ((dcg, ug_ref, wg_ref, dug_ref, dwg_ref),
                                                  (dcv, uv_ref, wv_ref, duv_ref, dwv_ref)):
            w = w_ref[...]
            d_next, d_prev = _shift_rows(d_c, -1), _shift_rows(d_c, 1)
            du = w[0:1, :] * d_next + w[1:2, :] * d_c + w[2:3, :] * d_prev
            du_ref[...] = du[inner].astype(BF16)
            u_in = u_ref[...].astype(F32)
            for tap, d_tap in enumerate((d_next, d_c, d_prev)):
                dw_ref[tap:tap + 1, :] += jnp.sum(d_tap[inner] * u_in, axis=0, keepdims=True)

    blk = pl.BlockSpec((FFN_ROWS, FFN_COLS), lambda j, i: (i, j))
    in_specs = ([blk, pl.BlockSpec((FFN_ROWS, FFN_COLS), lambda j, i: (i, j + nc))]
                + _halo_specs(FFN_ROWS, FFN_COLS, HALO, s, lambda j: j) * 3
                + [pl.BlockSpec((3, FFN_COLS), lambda j, i: (0, j)),
                   pl.BlockSpec((3, FFN_COLS), lambda j, i: (0, j + nc))])
    acc = pl.BlockSpec((HALO, FFN_COLS), lambda j, i: (0, j))
    return pl.pallas_call(
        body, name=name, grid=(nc, ns), in_specs=in_specs, out_specs=(blk, blk, acc, acc),
        out_shape=(jax.ShapeDtypeStruct((s, D_FF), BF16), jax.ShapeDtypeStruct((s, D_FF), BF16),
                   jax.ShapeDtypeStruct((HALO, D_FF), F32), jax.ShapeDtypeStruct((HALO, D_FF), F32)),
        compiler_params=_params(2),
    )(u, u, cg, cg, cg, cv, cv, cv, dact, dact, dact, cw, cw)


CONV_ROWS = 512
CONV_HALO = 16
CONV_PAD = CONV_WIDTH // 2


def _conv_halo_specs(cols, s):
    per = CONV_ROWS // CONV_HALO
    last = s // CONV_HALO - 1
    return [pl.BlockSpec((CONV_HALO, cols), lambda i: (jnp.maximum(i * per - 1, 0), 0)),
            pl.BlockSpec((CONV_ROWS, cols), lambda i: (i, 0)),
            pl.BlockSpec((CONV_HALO, cols), lambda i: (jnp.minimum((i + 1) * per, last), 0))]


def _glu_ext(pp, pc, pn, i, n_i):
    ext = _ext_block(pp, pc, pn, i, n_i)
    return ext[:, :CONV_CH] * _sigmoid(ext[:, CONV_CH:])


def _conv_mixer(pa, cw, cb, lnw, lnb, name):
    s = pa.shape[0]
    ns = s // CONV_ROWS

    def body(pp, pc, pn, cw_ref, cb_ref, lnw_ref, lnb_ref, o_ref, c_ref):
        i = pl.program_id(0)
        a = _glu_ext(pp, pc, pn, i, ns)
        acc = jnp.zeros((CONV_ROWS, CONV_CH), F32)
        for tap in range(CONV_WIDTH):
            acc = acc + cw_ref[tap:tap + 1, :] * _shift_rows(a, -(tap + 1))[:CONV_ROWS]
        cv = acc + cb_ref[...]
        c_ref[...] = cv
        mu = jnp.mean(cv, axis=-1, keepdims=True)
        xc = cv - mu
        rstd = lax.rsqrt(jnp.mean(xc * xc, axis=-1, keepdims=True) + EPS)
        y = xc * rstd * lnw_ref[...] + lnb_ref[...]
        o_ref[...] = (y * _sigmoid(y)).astype(BF16)

    vec = pl.BlockSpec((1, CONV_CH), lambda i: (0, 0))
    blk = pl.BlockSpec((CONV_ROWS, CONV_CH), lambda i: (i, 0))
    return pl.pallas_call(
        body, name=name, grid=(ns,),
        in_specs=_conv_halo_specs(2 * CONV_CH, s) + [pl.BlockSpec((CONV_WIDTH, CONV_CH), lambda i: (0, 0)), vec, vec, vec],
        out_specs=(blk, blk),
        out_shape=(jax.ShapeDtypeStruct((s, CONV_CH), BF16), jax.ShapeDtypeStruct((s, CONV_CH), F32)),
        compiler_params=_params(1))(pa, pa, pa, cw, cb, lnw, lnb)


def _conv_mixer_bwd_ln(cv, dout, lnw, lnb, name):
    s = cv.shape[0]

    def body(c_ref, do_ref, lnw_ref, lnb_ref, dc_ref, dlnw_ref, dlnb_ref, dcb_ref):
        @pl.when(pl.program_id(0) == 0)
        def _():
            dlnw_ref[...] = jnp.zeros_like(dlnw_ref)
            dlnb_ref[...] = jnp.zeros_like(dlnb_ref)
            dcb_ref[...] = jnp.zeros_like(dcb_ref)

        c = c_ref[...]
        mu = jnp.mean(c, axis=-1, keepdims=True)
        xc = c - mu
        rstd = lax.rsqrt(jnp.mean(xc * xc, axis=-1, keepdims=True) + EPS)
        xh = xc * rstd
        w = lnw_ref[...]
        y = xh * w + lnb_ref[...]
        dy = do_ref[...] * _silu_grad(y)
        dlnw_ref[...] += jnp.sum(dy * xh, axis=0, keepdims=True)
        dlnb_ref[...] += jnp.sum(dy, axis=0, keepdims=True)
        dxh = dy * w
        dc = rstd * (dxh - jnp.mean(dxh, axis=-1, keepdims=True) - xh * jnp.mean(dxh * xh, axis=-1, keepdims=True))
        dc_ref[...] = dc
        dcb_ref[...] += jnp.sum(dc, axis=0, keepdims=True)

    vec = pl.BlockSpec((1, CONV_CH), lambda i: (0, 0))
    blk = pl.BlockSpec((CONV_ROWS, CONV_CH), lambda i: (i, 0))
    return pl.pallas_call(
        body, name=name, grid=(s // CONV_ROWS,), in_specs=[blk, blk, vec, vec], out_specs=(blk, vec, vec, vec),
        out_shape=(jax.ShapeDtypeStruct((s, CONV_CH), F32),) + (jax.ShapeDtypeStruct((1, CONV_CH), F32),) * 3,
        compiler_params=_params(1))(cv, dout, lnw, lnb)


def _conv_mixer_bwd_conv(pa, dc, cw, name):
    s = pa.shape[0]
    ns = s // CONV_ROWS

    def body(pc, dp, dcc, dn, cw_ref, dpa_ref, dcw_ref):
        i = pl.program_id(0)

        @pl.when(i == 0)
        def _():
            dcw_ref[...] = jnp.zeros_like(dcw_ref)

        cur = pc[...]
        val, sg = cur[:, :CONV_CH], _sigmoid(cur[:, CONV_CH:])
        a_cur = val * sg
        dce = _ext_block(dp, dcc, dn, i, ns)
        da = jnp.zeros((CONV_ROWS, CONV_CH), F32)
        for tap in range(CONV_WIDTH):
            shifted = _shift_rows(dce, -(CONV_WIDTH - tap))[:CONV_ROWS]
            da = da + cw_ref[tap:tap + 1, :] * shifted
            dcw_ref[tap:tap + 1, :] += jnp.sum(shifted * a_cur, axis=0, keepdims=True)
        dpa_ref[:, :CONV_CH] = (da * sg).astype(BF16)
        dpa_ref[:, CONV_CH:] = (da * val * sg * (1.0 - sg)).astype(BF16)

    return pl.pallas_call(
        body, name=name, grid=(ns,),
        in_specs=[pl.BlockSpec((CONV_ROWS, 2 * CONV_CH), lambda i: (i, 0))] + _conv_halo_specs(CONV_CH, s)
        + [pl.BlockSpec((CONV_WIDTH, CONV_CH), lambda i: (0, 0))],
        out_specs=(pl.BlockSpec((CONV_ROWS, 2 * CONV_CH), lambda i: (i, 0)),
                   pl.BlockSpec((32, CONV_CH), lambda i: (0, 0))),
        out_shape=(jax.ShapeDtypeStruct((s, 2 * CONV_CH), BF16), jax.ShapeDtypeStruct((32, CONV_CH), F32)),
        compiler_params=_params(1))(pa, dc, dc, dc, cw)


SLOPES = tuple(float(2.0 ** (-ALIBI_MAX_EXP * (h + 1) / N_HEADS)) for h in range(N_HEADS))
ATT_SCALE = HEAD_DIM ** -0.5


PAIR = 2 * HEAD_DIM
N_PAIRS = N_HEADS // 2
ATT_WIN = ATT_BLOCK + 2 * ATT_HALF


ATT_GROUPS = {1: 8, 4: 4, 16: 1}


def _window_specs(dil, n_steps, col_of):
    per = 2 * ATT_GROUPS[dil]
    rows, halo = ATT_BLOCK * dil * ATT_GROUPS[dil], ATT_HALF * dil
    return [pl.BlockSpec((halo, PAIR), lambda i, p: (jnp.maximum(per * i - 1, 0), col_of(p))),
            pl.BlockSpec((rows, PAIR), lambda i, p: (i, col_of(p))),
            pl.BlockSpec((halo, PAIR), lambda i, p: (jnp.minimum(per * (i + 1), per * n_steps - 1), col_of(p)))]


def _residue(ref, r, n, dil, start=0):
    return ref[pl.ds(start * dil + r, n, stride=dil), :] if dil > 1 else ref[pl.ds(start + r, n), :]


def _store_residue(ref, r, dil, start, val):
    if dil > 1:
        ref[pl.ds(start * dil + r, val.shape[0], stride=dil), :] = val
    else:
        ref[pl.ds(start + r, val.shape[0]), :] = val


def _residue_window(refs, r, dil, g=0):
    prev, cur, nxt = refs
    groups = ATT_GROUPS[dil]
    lo = max(g * ATT_BLOCK - ATT_HALF, 0)
    hi = min((g + 1) * ATT_BLOCK + ATT_HALF, groups * ATT_BLOCK)
    parts = [_residue(prev, r, ATT_HALF, dil)] if g == 0 else []
    parts.append(_residue(cur, r, hi - lo, dil, lo))
    if g == groups - 1:
        parts.append(_residue(nxt, r, ATT_HALF, dil))
    return jnp.concatenate(parts, axis=0)


def _band_masks(i, length, dil, transposed):
    shape = (ATT_WIN, ATT_BLOCK) if transposed else (ATT_BLOCK, ATT_WIN)
    row = lax.broadcasted_iota(jnp.int32, shape, 0)
    col = lax.broadcasted_iota(jnp.int32, shape, 1)
    wide = row if transposed else col
    dist = jnp.abs((row - col - ATT_HALF) if transposed else (row + ATT_HALF - col))
    wpos = i * ATT_BLOCK - ATT_HALF + wide
    valid = (dist <= ATT_HALF) & (wpos >= 0) & (wpos < length)
    return valid, dist.astype(F32) * float(dil)


def _attn_branch(qkv, dil, name):
    s = qkv.shape[0]
    groups = ATT_GROUPS[dil]
    rows = ATT_BLOCK * dil * groups
    n_steps = s // rows
    length = s // dil
    nt = (((1,), (1,)), ((), ()))

    def body(q_ref, kp, kc, kn, vp, vc, vn, o_ref, l_ref):
        i, pair = pl.program_id(0), pl.program_id(1)
        items = [(g, r) for g in range(groups) for r in range(dil)]
        q = jnp.stack([_residue(q_ref, r, ATT_BLOCK, dil, g * ATT_BLOCK) for g, r in items]).astype(BF16)
        k = jnp.stack([_residue_window((kp, kc, kn), r, dil, g) for g, r in items]).astype(BF16)
        v = jnp.stack([_residue_window((vp, vc, vn), r, dil, g) for g, r in items]).astype(BF16)
        per_group = [_band_masks(i * groups + g, length, dil, False) for g in range(groups)]
        valid = jnp.stack([per_group[g][0] for g, _ in items]) if groups > 1 else per_group[0][0][None]
        distf = jnp.stack([per_group[g][1] for g, _ in items]) if groups > 1 else per_group[0][1][None]
        outs, lses = [], []
        for hh in range(2):
            sl = slice(hh * HEAD_DIM, (hh + 1) * HEAD_DIM)
            slope = jnp.where(pair == 0, SLOPES[hh], jnp.where(pair == 1, SLOPES[2 + hh], SLOPES[4 + hh]))
            sc = jnp.einsum("bqd,bkd->bqk", q[:, :, sl], k[:, :, sl], preferred_element_type=F32) * ATT_SCALE
            sc = jnp.where(valid, sc - slope * distf, MASK_VALUE)
            m = jnp.max(sc, axis=-1, keepdims=True)
            p = jnp.exp(sc - m)
            den = jnp.sum(p, axis=-1, keepdims=True)
            outs.append(jnp.einsum("bqk,bkd->bqd", p.astype(BF16), v[:, :, sl], preferred_element_type=F32) / den)
            lses.append(jnp.broadcast_to(m + jnp.log(den), (len(items), ATT_BLOCK, HEAD_DIM)))
        o_all, l_all = jnp.concatenate(outs, axis=2), jnp.concatenate(lses, axis=2)
        for n, (g, r) in enumerate(items):
            _store_residue(o_ref, r, dil, g * ATT_BLOCK, o_all[n])
            _store_residue(l_ref, r, dil, g * ATT_BLOCK, l_all[n])

    out_blk = pl.BlockSpec((rows, PAIR), lambda i, p: (i, p))
    return pl.pallas_call(
        body, name=name, grid=(n_steps, N_PAIRS),
        in_specs=[pl.BlockSpec((rows, PAIR), lambda i, p: (i, p))]
        + _window_specs(dil, n_steps, lambda p: N_PAIRS + p) + _window_specs(dil, n_steps, lambda p: 2 * N_PAIRS + p),
        out_specs=(out_blk, out_blk),
        out_shape=(jax.ShapeDtypeStruct((s, ATT_WIDTH), F32),) * 2,
        compiler_params=_params(2))(qkv, qkv, qkv, qkv, qkv, qkv, qkv)


ATT_ROWS = 512


def _attn_combine(outs, lses, name):
    s = outs[0].shape[0]

    def body(o1, o2, o3, l1, l2, l3, att_ref, att32_ref, lse_ref):
        ls = [l1[...], l2[...], l3[...]]
        m = jnp.maximum(jnp.maximum(ls[0], ls[1]), ls[2])
        es = [jnp.exp(l - m) for l in ls]
        den = es[0] + es[1] + es[2]
        att = (es[0] * o1[...] + es[1] * o2[...] + es[2] * o3[...]) / den
        att_ref[...] = att.astype(BF16)
        att32_ref[...] = att
        lse_ref[...] = m + jnp.log(den)

    blk = pl.BlockSpec((ATT_ROWS, ATT_WIDTH), lambda i: (i, 0))
    return pl.pallas_call(
        body, name=name, grid=(s // ATT_ROWS,), in_specs=[blk] * 6, out_specs=(blk, blk, blk),
        out_shape=(jax.ShapeDtypeStruct((s, ATT_WIDTH), BF16), jax.ShapeDtypeStruct((s, ATT_WIDTH), F32),
                   jax.ShapeDtypeStruct((s, ATT_WIDTH), F32)),
        compiler_params=_params(1))(*outs, *lses)


def _attn_delta(datt, att, name):
    s = att.shape[0]

    def body(d_ref, a_ref, delta_ref):
        prod = d_ref[...] * a_ref[...]
        for h in range(N_HEADS):
            sl = slice(h * HEAD_DIM, (h + 1) * HEAD_DIM)
            delta_ref[:, sl] = jnp.broadcast_to(jnp.sum(prod[:, sl], axis=-1, keepdims=True), (ATT_ROWS, HEAD_DIM))

    blk = pl.BlockSpec((ATT_ROWS, ATT_WIDTH), lambda i: (i, 0))
    return pl.pallas_call(
        body, name=name, grid=(s // ATT_ROWS,), in_specs=[blk, blk], out_specs=blk,
        out_shape=jax.ShapeDtypeStruct((s, ATT_WIDTH), F32), compiler_params=_params(1))(datt, att)


def _attn_branch_bwd(qkv, do, lse, delta, prev, dil, out_dtype, name):
    s = qkv.shape[0]
    groups = ATT_GROUPS[dil]
    rows = ATT_BLOCK * dil * groups
    n_steps = s // rows
    length = s // dil
    has_prev = prev is not None
    tn = (((0,), (0,)), ((), ()))
    nt = (((1,), (1,)), ((), ()))

    def body(*refs):
        qs, ks, vs, dos, ls, des = (refs[3 * n:3 * n + 3] for n in range(6))
        rest = refs[18:]
        if has_prev:
            pq, pk, pv = rest[:3]
            rest = rest[3:]
        dq_ref, dk_ref, dv_ref = rest
        i, pair = pl.program_id(0), pl.program_id(1)
        items = [(g, r) for g in range(groups) for r in range(dil)]
        cur = lambda t: jnp.stack([_residue(t[1], r, ATT_BLOCK, dil, g * ATT_BLOCK) for g, r in items])
        win = lambda t: jnp.stack([_residue_window(t, r, dil, g) for g, r in items])
        q_cur, k_cur, v_cur, do_cur = (cur(t).astype(BF16) for t in (qs, ks, vs, dos))
        q_win, k_win, v_win, do_win = (win(t).astype(BF16) for t in (qs, ks, vs, dos))
        l_cur, de_cur, l_win, de_win = cur(ls), cur(des), win(ls), win(des)

        def masks(transposed):
            per_group = [_band_masks(i * groups + g, length, dil, transposed) for g in range(groups)]
            if groups == 1:
                return per_group[0][0][None], per_group[0][1][None]
            return jnp.stack([per_group[g][0] for g, _ in items]), jnp.stack([per_group[g][1] for g, _ in items])

        valid_q, distf_q = masks(False)
        valid_k, distf_k = masks(True)
        dot = lambda eq, a, b: jnp.einsum(eq, a, b, preferred_element_type=F32)
        dqs, dks, dvs = [], [], []
        for hh in range(2):
            sl = slice(hh * HEAD_DIM, (hh + 1) * HEAD_DIM)
            one = slice(hh * HEAD_DIM, hh * HEAD_DIM + 1)
            slope = jnp.where(pair == 0, SLOPES[hh], jnp.where(pair == 1, SLOPES[2 + hh], SLOPES[4 + hh]))
            sc = dot("bqd,bkd->bqk", q_cur[:, :, sl], k_win[:, :, sl]) * ATT_SCALE - slope * distf_q
            p = jnp.exp(jnp.where(valid_q, sc - l_cur[:, :, one], MASK_VALUE))
            dp = dot("bqd,bkd->bqk", do_cur[:, :, sl], v_win[:, :, sl])
            ds = (p * (dp - de_cur[:, :, one]) * ATT_SCALE).astype(BF16)
            dqs.append(dot("bqk,bkd->bqd", ds, k_win[:, :, sl]))

            sc2 = dot("bqd,bkd->bqk", q_win[:, :, sl], k_cur[:, :, sl]) * ATT_SCALE - slope * distf_k
            p2 = jnp.exp(jnp.where(valid_k, sc2 - l_win[:, :, one], MASK_VALUE))
            dvs.append(dot("bqk,bqd->bkd", p2.astype(BF16), do_win[:, :, sl]))
            dp2 = dot("bqd,bkd->bqk", do_win[:, :, sl], v_cur[:, :, sl])
            ds2 = (p2 * (dp2 - de_win[:, :, one]) * ATT_SCALE).astype(BF16)
            dks.append(dot("bqk,bqd->bkd", ds2, q_win[:, :, sl]))
        for parts, acc, out in ((dqs, pq if has_prev else None, dq_ref), (dks, pk if has_prev else None, dk_ref),
                                (dvs, pv if has_prev else None, dv_ref)):
            val = jnp.concatenate(parts, axis=2)
            for n, (g, r) in enumerate(items):
                piece = val[n]
                if has_prev:
                    piece = piece + _residue(acc, r, ATT_BLOCK, dil, g * ATT_BLOCK)
                _store_residue(out, r, dil, g * ATT_BLOCK, piece.astype(out_dtype))

    blk = pl.BlockSpec((rows, PAIR), lambda i, p: (i, p))
    in_specs = (_window_specs(dil, n_steps, lambda p: p) + _window_specs(dil, n_steps, lambda p: N_PAIRS + p)
                + _window_specs(dil, n_steps, lambda p: 2 * N_PAIRS + p) + _window_specs(dil, n_steps, lambda p: p) * 3)
    ins = [qkv] * 9 + [do] * 3 + [lse] * 3 + [delta] * 3
    if has_prev:
        in_specs += [blk] * 3
        ins += list(prev)
    return pl.pallas_call(
        body, name=name, grid=(n_steps, N_PAIRS), in_specs=in_specs, out_specs=(blk, blk, blk),
        out_shape=(jax.ShapeDtypeStruct((s, ATT_WIDTH), out_dtype),) * 3,
        compiler_params=_params(2))(*ins)


TB = 2 * REC_CHUNK
REC_ROWS = 5 * REC_WIDTH


REC_LEVELS = 6


def _scan_pos(p, rev):
    p = p & (REC_CHUNK - 1)
    return (REC_CHUNK - 1 - p) if rev else p


def _split3(x):
    hi = x.astype(BF16)
    rest = x - hi.astype(F32)
    mid = rest.astype(BF16)
    return hi, mid, (rest - mid.astype(F32)).astype(BF16)


def _chunk_sums(x, rev, with_levels):
    row = lax.broadcasted_iota(jnp.int32, (TB, TB), 0)
    col = lax.broadcasted_iota(jnp.int32, (TB, TB), 1)
    same = (row < REC_CHUNK) == (col < REC_CHUNK)
    s_row, s_col = _scan_pos(row, rev), _scan_pos(col, rev)
    mats = [same & (s_row <= s_col)]
    if with_levels:
        for level in range(1, REC_LEVELS + 1):
            shift = REC_LEVELS + 1 - level
            boundary = ((s_col >> shift) << shift) + (REC_CHUNK >> level) - 1
            mats.append(same & (s_row <= boundary))
        mats.append(same)
    cat = jnp.concatenate([m.astype(BF16) for m in mats], axis=1)
    total = sum(jnp.dot(term, cat, preferred_element_type=F32) for term in _split3(x))
    return [total[:, n * TB:(n + 1) * TB] for n in range(len(mats))]


def _hg_prep(qraw, z, lb, rev):
    lane = lax.broadcasted_iota(jnp.int32, (REC_WIDTH, TB), 1)
    in_a = lane < REC_CHUNK
    scan = _scan_pos(lane, rev)
    sig, sigm = _sigmoid(z), _sigmoid(-z)
    f = lb + (1.0 - lb) * sig
    kk = (1.0 - lb) * sigm
    sums = _chunk_sums(jnp.log(jnp.maximum(f, F_TINY)), rev, True)
    b, bend = sums[0], sums[-1]
    q = qraw * _sigmoid(qraw)
    eq, ek = [], []
    for level in range(1, REC_LEVELS + 1):
        r = sums[level]
        e = jnp.exp(jnp.minimum(b - r, r - b))
        second = ((scan >> (REC_LEVELS - level)) & 1) == 1
        eq.append(jnp.where(second, e, 0.0))
        ek.append(jnp.where(second, 0.0, e))
    lanes_end = (0, REC_CHUNK) if rev else (REC_CHUNK - 1, TB - 1)
    end_a, end_b = (b[:, n:n + 1] for n in lanes_end)
    return dict(in_a=in_a, sig=sig, sigm=sigm, f=f, kk=kk, b=b, end_a=end_a, end_b=end_b,
                q=q, qh=q * jnp.exp(b), kh=kk * jnp.exp(bend - b), ekb=jnp.exp(bend - b), eq=eq, ek=ek)


def _level_masks(rev):
    row = lax.broadcasted_iota(jnp.int32, (TB, TB), 0)
    col = lax.broadcasted_iota(jnp.int32, (TB, TB), 1)
    same = (row < REC_CHUNK) == (col < REC_CHUNK)
    s_row, s_col = _scan_pos(row, rev), _scan_pos(col, rev)
    masks = [same & ((s_row >> (REC_LEVELS + 1 - level)) == (s_col >> (REC_LEVELS + 1 - level)))
             for level in range(1, REC_LEVELS + 1)]
    return masks, row == col


def _head_rows(x, h):
    return x[h * HEAD_DIM:(h + 1) * HEAD_DIM, :]


def _block_diag_mask():
    r = lax.broadcasted_iota(jnp.int32, (REC_WIDTH, REC_WIDTH), 0) // HEAD_DIM
    c = lax.broadcasted_iota(jnp.int32, (REC_WIDTH, REC_WIDTH), 1) // HEAD_DIM
    return (r == c).astype(F32)


def _heads(x):
    return x.reshape(N_HEADS, HEAD_DIM, TB)


def _hgrn_scan(projt, lb, rev, name):
    s = projt.shape[1]
    nblk = s // TB
    zrow = 2 if rev else 1
    tmap = (lambda i: nblk - 1 - i) if rev else (lambda i: i)
    tn = (((0,), (0,)), ((), ()))
    nt = (((1,), (1,)), ((), ()))

    def body(q_ref, z_ref, v_ref, lb_ref, o_ref, hs_ref, at_ref, h_ref):
        @pl.when(pl.program_id(0) == 0)
        def _():
            h_ref[...] = jnp.zeros_like(h_ref)

        v = v_ref[...]
        vb = v.astype(BF16)
        pr = _hg_prep(q_ref[...], z_ref[...], lb_ref[...], rev)
        q, kk = pr["q"], pr["kk"]
        masks, diag = _level_masks(rev)
        own = jnp.sum(_heads(q * kk), axis=1, keepdims=True)
        sc = jnp.where(diag[None], own, 0.0)
        for level in range(REC_LEVELS):
            qt = _heads((q * pr["eq"][level]).astype(BF16))
            kt = _heads((kk * pr["ek"][level]).astype(BF16))
            sc = sc + jnp.where(masks[level][None],
                                jnp.einsum("hks,hkt->hst", kt, qt, preferred_element_type=F32), 0.0)
        a_bf = sc.astype(BF16)
        at_ref[...] = a_bf
        o = jnp.einsum("hvs,hst->hvt", _heads(vb), a_bf, preferred_element_type=F32).reshape(REC_WIDTH, TB)
        bd_mask = _block_diag_mask()
        order = ((1, ~pr["in_a"], pr["end_b"]), (0, pr["in_a"], pr["end_a"]))
        if not rev:
            order = order[::-1]
        for slot, msk, bend in order:
            h0 = h_ref[...]
            hs_ref[slot] = h0
            o = o + lax.dot_general(h0.astype(BF16), jnp.where(msk, pr["qh"], 0.0).astype(BF16), tn,
                                    preferred_element_type=F32)
            upd = lax.dot_general(jnp.where(msk, pr["kh"], 0.0).astype(BF16), vb, nt, preferred_element_type=F32)
            h_ref[...] = jnp.exp(bend) * h0 + upd * bd_mask
        o_ref[...] = o

    row_blk = lambda r: pl.BlockSpec((REC_WIDTH, TB), lambda i: (r, tmap(i)))
    return pl.pallas_call(
        body, name=name, grid=(nblk,),
        in_specs=[row_blk(0), row_blk(zrow), row_blk(3), pl.BlockSpec((REC_WIDTH, 1), lambda i: (0, 0))],
        out_specs=(pl.BlockSpec((REC_WIDTH, TB), lambda i: (0, tmap(i))),
                   pl.BlockSpec((2, REC_WIDTH, REC_WIDTH), lambda i: (tmap(i), 0, 0)),
                   pl.BlockSpec((None, N_HEADS, TB, TB), lambda i: (tmap(i), 0, 0, 0))),
        out_shape=(jax.ShapeDtypeStruct((REC_WIDTH, s), F32),
                   jax.ShapeDtypeStruct((s // REC_CHUNK, REC_WIDTH, REC_WIDTH), F32),
                   jax.ShapeDtypeStruct((nblk, N_HEADS, TB, TB), BF16)),
        scratch_shapes=[pltpu.VMEM((REC_WIDTH, REC_WIDTH), F32)],
        compiler_params=_params(1))(projt, projt, projt, lb)


def _hgrn_scan_bwd(projt, lb, dot, hs, at, prev, rev, name):
    s = projt.shape[1]
    nblk = s // TB
    zrow = 2 if rev else 1
    tmap = (lambda i: i) if rev else (lambda i: nblk - 1 - i)
    has_prev = prev is not None
    tn = (((0,), (0,)), ((), ()))
    nt = (((1,), (1,)), ((), ()))

    def body(*refs):
        q_ref, z_ref, v_ref, lb_ref, do_ref, hs_ref, at_ref = refs[:7]
        rest = refs[7:]
        if has_prev:
            pq_ref, pv_ref = rest[:2]
            rest = rest[2:]
        dq_ref, dz_ref, dv_ref, dlb_ref, dh_ref = rest

        @pl.when(pl.program_id(0) == 0)
        def _():
            dh_ref[...] = jnp.zeros_like(dh_ref)
            dlb_ref[...] = jnp.zeros_like(dlb_ref)

        qraw, v, do, lbv = q_ref[...], v_ref[...], do_ref[...], lb_ref[...]
        dob, vb = do.astype(BF16), v.astype(BF16)
        pr = _hg_prep(qraw, z_ref[...], lbv, rev)
        q, kk, b, in_a = pr["q"], pr["kk"], pr["b"], pr["in_a"]
        masks, diag = _level_masks(rev)
        dot = lambda eq, x, y: jnp.einsum(eq, x, y, preferred_element_type=F32)
        d_at = dot("hvs,hvt->hst", _heads(vb), _heads(dob))
        dv = dot("hvt,hst->hvs", _heads(dob), at_ref[...]).reshape(REC_WIDTH, TB)
        d_own = jnp.sum(jnp.where(diag[None], d_at, 0.0), axis=1, keepdims=True)
        dq_in = (d_own * _heads(kk)).reshape(REC_WIDTH, TB)
        dk_in = (d_own * _heads(q)).reshape(REC_WIDTH, TB)
        db_in = jnp.zeros((REC_WIDTH, TB), F32)
        for lv in range(REC_LEVELS):
            d_lv = jnp.where(masks[lv][None], d_at, 0.0).astype(BF16)
            q_lv, k_lv = (q * pr["eq"][lv]).astype(BF16), (kk * pr["ek"][lv]).astype(BF16)
            dqt = dot("hks,hst->hkt", _heads(k_lv), d_lv).reshape(REC_WIDTH, TB)
            dkt = dot("hkt,hst->hks", _heads(q_lv), d_lv).reshape(REC_WIDTH, TB)
            dq_in = dq_in + pr["eq"][lv] * dqt
            dk_in = dk_in + pr["ek"][lv] * dkt
            db_in = db_in + q_lv.astype(F32) * dqt - k_lv.astype(F32) * dkt
        dq = dk = jnp.zeros((REC_WIDTH, TB), F32)

        zero = jnp.zeros((REC_WIDTH, TB), F32)
        bd_mask = _block_diag_mask()
        eb = jnp.exp(b)
        const = zero
        order = ((0, in_a, pr["end_a"]), (1, ~in_a, pr["end_b"]))
        if not rev:
            order = order[::-1]
        for slot, msk, bend in order:
            h0 = hs_ref[slot]
            dh1 = dh_ref[...]
            dh1b = dh1.astype(BF16)
            dq = dq + eb * jnp.dot(h0.astype(BF16), jnp.where(msk, do, 0.0).astype(BF16), preferred_element_type=F32)
            dv = dv + lax.dot_general(dh1b, jnp.where(msk, pr["kh"], 0.0).astype(BF16), tn, preferred_element_type=F32)
            dk_int = pr["ekb"] * jnp.dot(dh1b, jnp.where(msk, v, 0.0).astype(BF16), preferred_element_type=F32)
            dk = dk + dk_int
            ebend = jnp.exp(bend)
            c = (jnp.sum(kk * dk_int, axis=1, keepdims=True)
                 + ebend * jnp.sum(h0 * dh1, axis=1, keepdims=True))
            const = const + jnp.where(msk, c, 0.0)
            upd = lax.dot_general(jnp.where(msk, pr["qh"], 0.0).astype(BF16), dob, nt, preferred_element_type=F32)
            dh_ref[...] = ebend * dh1 + upd * bd_mask

        dg = _chunk_sums(db_in + q * dq - kk * dk, not rev, False)[0] + const
        dq, dk = dq + dq_in, dk + dk_in
        sig, sigm, f = pr["sig"], pr["sigm"], pr["f"]
        live = f > F_TINY
        inv_f = 1.0 / jnp.maximum(f, F_TINY)
        one_lb = 1.0 - lbv
        dz = sig * sigm * one_lb * (jnp.where(live, dg * inv_f, 0.0) - dk)
        dlb_ref[...] += jnp.sum(sigm * (jnp.where(live, dg * inv_f, 0.0) - dk), axis=1, keepdims=True)
        dqr = dq * _silu_grad(qraw)
        if has_prev:
            dqr = dqr + pq_ref[...]
            dv = dv + pv_ref[...]
        dq_ref[...] = dqr
        dz_ref[...] = dz
        dv_ref[...] = dv

    row_blk = lambda r: pl.BlockSpec((REC_WIDTH, TB), lambda i: (r, tmap(i)))
    blk = pl.BlockSpec((REC_WIDTH, TB), lambda i: (0, tmap(i)))
    col = pl.BlockSpec((REC_WIDTH, 1), lambda i: (0, 0))
    in_specs = [row_blk(0), row_blk(zrow), row_blk(3), col, blk,
                pl.BlockSpec((2, REC_WIDTH, REC_WIDTH), lambda i: (tmap(i), 0, 0)),
                pl.BlockSpec((None, N_HEADS, TB, TB), lambda i: (tmap(i), 0, 0, 0))]
    ins = [projt, projt, projt, lb, dot, hs, at]
    if has_prev:
        in_specs += [blk, blk]
        ins += list(prev)
    t_shape = jax.ShapeDtypeStruct((REC_WIDTH, s), F32)
    return pl.pallas_call(
        body, name=name, grid=(nblk,), in_specs=in_specs, out_specs=(blk, blk, blk, col),
        out_shape=(t_shape, t_shape, t_shape, jax.ShapeDtypeStruct((REC_WIDTH, 1), F32)),
        scratch_shapes=[pltpu.VMEM((REC_WIDTH, REC_WIDTH), F32)],
        compiler_params=_params(1))(*ins)


REC_OUT_COLS = 512


def _head_rms(o):
    o3 = o.reshape(N_HEADS, HEAD_DIM, o.shape[1])
    rstd = lax.rsqrt(jnp.mean(o3 * o3, axis=1, keepdims=True) + EPS)
    return o3 * rstd, rstd


def _hgrn_out(of, ob, projt, wn, name):
    s = of.shape[1]

    def body(of_ref, ob_ref, g_ref, wn_ref, o_ref):
        on, _ = _head_rms(of_ref[...] + ob_ref[...])
        g = g_ref[...]
        y = on.reshape(REC_WIDTH, REC_OUT_COLS) * wn_ref[...] * (g * _sigmoid(g))
        o_ref[...] = y.T.astype(BF16)

    blk = pl.BlockSpec((REC_WIDTH, REC_OUT_COLS), lambda i: (0, i))
    return pl.pallas_call(
        body, name=name, grid=(s // REC_OUT_COLS,),
        in_specs=[blk, blk, pl.BlockSpec((REC_WIDTH, REC_OUT_COLS), lambda i: (4, i)),
                  pl.BlockSpec((REC_WIDTH, 1), lambda i: (0, 0))],
        out_specs=pl.BlockSpec((REC_OUT_COLS, REC_WIDTH), lambda i: (i, 0)),
        out_shape=jax.ShapeDtypeStruct((s, REC_WIDTH), BF16), compiler_params=_params(1))(of, ob, projt, wn)


def _hgrn_out_bwd(drec, of, ob, projt, wn, name):
    s = of.shape[1]

    def body(d_ref, of_ref, ob_ref, g_ref, wn_ref, do_ref, dg_ref, dwn_ref):
        @pl.when(pl.program_id(0) == 0)
        def _():
            dwn_ref[...] = jnp.zeros_like(dwn_ref)

        dy = d_ref[...].T
        on3, rstd = _head_rms(of_ref[...] + ob_ref[...])
        on = on3.reshape(REC_WIDTH, REC_OUT_COLS)
        g, wnv = g_ref[...], wn_ref[...]
        dg_ref[...] = dy * on * wnv * _silu_grad(g)
        d_onw = dy * (g * _sigmoid(g))
        dwn_ref[...] += jnp.sum(d_onw * on, axis=1, keepdims=True)
        d_on3 = (d_onw * wnv).reshape(N_HEADS, HEAD_DIM, REC_OUT_COLS)
        do3 = rstd * (d_on3 - on3 * jnp.mean(d_on3 * on3, axis=1, keepdims=True))
        do_ref[...] = do3.reshape(REC_WIDTH, REC_OUT_COLS)

    blk = pl.BlockSpec((REC_WIDTH, REC_OUT_COLS), lambda i: (0, i))
    col = pl.BlockSpec((REC_WIDTH, 1), lambda i: (0, 0))
    t_shape = jax.ShapeDtypeStruct((REC_WIDTH, s), F32)
    return pl.pallas_call(
        body, name=name, grid=(s // REC_OUT_COLS,),
        in_specs=[pl.BlockSpec((REC_OUT_COLS, REC_WIDTH), lambda i: (i, 0)), blk, blk,
                  pl.BlockSpec((REC_WIDTH, REC_OUT_COLS), lambda i: (4, i)), col],
        out_specs=(blk, blk, col),
        out_shape=(t_shape, t_shape, jax.ShapeDtypeStruct((REC_WIDTH, 1), F32)),
        compiler_params=_params(1))(drec, of, ob, projt, wn)


def _lower_bounds(gamma, name):
    def body(g_ref, lb_ref, p_ref):
        g0, g1 = g_ref[0:1, :], g_ref[1:2, :]
        m = jnp.maximum(g0, g1)
        e0, e1 = jnp.exp(g0 - m), jnp.exp(g1 - m)
        p0, p1 = e0 / (e0 + e1), e1 / (e0 + e1)
        lb_ref[...] = (p0 + p1) - p0
        p_ref[0:1, :] = p0
        p_ref[1:2, :] = p1

    n = gamma.shape[1]
    return pl.pallas_call(body, name=name,
                          out_shape=(jax.ShapeDtypeStruct((1, n), F32), jax.ShapeDtypeStruct((2, n), F32)))(gamma)


def _lower_bounds_bwd(dlb1, p, name):
    def body(d_ref, p_ref, o_ref):
        p0, p1, d = p_ref[0:1, :], p_ref[1:2, :], d_ref[...]
        inner = p1 * d
        o_ref[0:1, :] = p0 * (0.0 - inner)
        o_ref[1:2, :] = p1 * (d - inner)

    return pl.pallas_call(body, name=name, out_shape=jax.ShapeDtypeStruct(p.shape, F32))(dlb1, p)


def _split_w_in(w_in):
    return dict(conv=w_in[:, G_CONV[0]:G_CONV[1]], qkv=w_in[:, G_QKV[0]:G_QKV[1]],
                rec_t=w_in[:, G_REC[0]:].T, nat=w_in[:, :G_REC[0]])


def _split_w_rest(w_out, w_up, w_down):
    return dict(out=w_out, out_a=w_out[:CONV_CH], out_b=w_out[CONV_CH:CONV_CH + ATT_WIDTH],
                out_c=w_out[CONV_CH + ATT_WIDTH:], up=w_up, down=w_down)


def _col(v):
    return v.reshape(-1, 1)


def _sequence_step(x, tgt, mods, lbs, small, later_weights, final_w, on_layer_grads):
    saved = []
    xin = x
    h1 = _resid_norm_mod(x, None, None, small[0]["norm1_w"], mods[0][1:2], mods[0][0:1], "norm1_first")
    big = [_split_w_in(later_weights(0, h1)), None]
    for l in range(DEPTH):
        sm, w, md = small[l], big[l], mods[l]
        pa = _matmul(h1, w["conv"], "nn", F32, f"proj_conv")
        qkv = _matmul(h1, w["qkv"], "nn", F32, f"proj_qkv")
        projt = _matmul(w["rec_t"], h1, "nt", F32, f"proj_rec")
        a_out, cv = _conv_mixer(pa, sm["conv_a_w"], sm["conv_a_b"], sm["ln_a_w"], sm["ln_a_b"], f"conv_mixer")
        outs, lses = zip(*[_attn_branch(qkv, d, f"attn_d{d}") for d in DILATIONS])
        att, att32, lse = _attn_combine(outs, lses, f"attn_combine")
        lb_f, lb_b = _col(lbs[l][0]), _col(lbs[l][1])
        of, hsf, atf = _hgrn_scan(projt, lb_f, False, "hgrn_fwd")
        ob, hsb, atb = _hgrn_scan(projt, lb_b, True, "hgrn_rev")
        wn = _col(sm["rec_norm_w"])
        rec = _hgrn_out(of, ob, projt, wn, f"hgrn_out")
        mixed = jnp.concatenate([a_out, att, rec], axis=1)
        if l == 0:
            w_out_all = later_weights(1, rec)
            big[0].update(_split_w_rest(w_out_all[0], None, None))
        r1 = _matmul(mixed, w["out"], "nn", BF16, "out_proj")
        xmid, h2 = _resid_norm_mod(xin, r1, md[2:3], sm["norm2_w"], md[4:5], md[3:4], f"norm2")
        if l == 0:
            w_in1, w_up_all, w_down_all = later_weights(2, h2)
            big[0].update(up=w_up_all[0], down=w_down_all[0])
            big[1] = dict(_split_w_in(w_in1), **_split_w_rest(w_out_all[1], w_up_all[1], w_down_all[1]))
        u = _matmul(h2, w["up"], "nn", BF16, f"ffn_up")
        act, conv_g, conv_v = _ffn_act(u, sm["conv_f_w"], "ffn_act")
        r2 = _matmul(act, w["down"], "nn", BF16, "ffn_down")
        saved.append(dict(xin=xin, h1=h1, pa=pa, qkv=qkv, projt=projt, cv=cv, att32=att32, lse=lse, of=of, ob=ob,
                          hsf=hsf, hsb=hsb, atf=atf, atb=atb, lb_f=lb_f, lb_b=lb_b, wn=wn, mixed=mixed, r1=r1, xmid=xmid, h2=h2,
                          u=u, conv_g=conv_g, conv_v=conv_v, act=act, r2=r2))
        if l + 1 < DEPTH:
            nxt = small[l + 1]
            xin, h1 = _resid_norm_mod(xmid, r2, md[5:6], nxt["norm1_w"], mods[l + 1][1:2], mods[l + 1][0:1],
                                      "norm1")
    top = saved[-1]
    loss, dx, dr2, dg2, dfw = _final_loss(top["xmid"], top["r2"], mods[-1][5:6], final_w, tgt, "final_loss")

    grads = [None] * DEPTH
    order_after = None
    for l in reversed(range(DEPTH)):
        sm, w, md, sv = small[l], big[l], mods[l], saved[l]
        dact = _matmul(dr2, w["down"], "nt", BF16, f"d_act")
        g_down = _matmul(dr2, sv["act"], "tn", F32, "dw_down").T
        conv_f_w = sm["conv_f_w"] if order_after is None else sm["conv_f_w"] + order_after
        dug, duv, dwg, dwv = _ffn_act_bwd(sv["u"], sv["conv_g"], sv["conv_v"], dact, conv_f_w, "ffn_act_bwd")
        dh2 = _matmul_two_lhs(dug, duv, w["up"], BF16, "d_h2")
        g_up = _matmul_two_rhs(sv["h2"], dug, duv, F32, "dw_up")
        after_ffn = on_layer_grads(l, {2: g_up, 3: g_down}, False)
        norm2_w = sm["norm2_w"] if after_ffn is None else sm["norm2_w"] + after_ffn
        dxmid, dr1, dsh2, dsc2, dnw2, dg1 = _norm_bwd(sv["xmid"], [dh2], dx, norm2_w, md[4:5], md[2:3], sv["r1"],
                                                     f"norm2_bwd")
        dmix_a = _matmul(dr1, w["out_a"], "nt", F32, f"d_mix_a")
        dmix_b = _matmul(dr1, w["out_b"], "nt", F32, f"d_mix_b")
        dmix_c = _matmul(dr1, w["out_c"], "nt", F32, f"d_mix_c")
        g_out = _matmul(sv["mixed"], dr1, "tn", F32, f"dw_out")
        dc, dlnw, dlnb, dcb = _conv_mixer_bwd_ln(sv["cv"], dmix_a, sm["ln_a_w"], sm["ln_a_b"], f"conv_mixer_bwd_ln")
        dpa, dcw = _conv_mixer_bwd_conv(sv["pa"], dc, sm["conv_a_w"], f"conv_mixer_bwd_conv")
        delta = _attn_delta(dmix_b, sv["att32"], "attn_delta")
        dqkv = None
        for d in reversed(DILATIONS):
            dqkv = _attn_branch_bwd(sv["qkv"], dmix_b, sv["lse"], delta, dqkv, d, BF16 if d == 1 else F32,
                                    f"attn_bwd_d{d}")
        dot, dgt, dwn = _hgrn_out_bwd(dmix_c, sv["of"], sv["ob"], sv["projt"], sv["wn"], f"hgrn_out_bwd")
        dqf, dzf, dvf, dlbf = _hgrn_scan_bwd(sv["projt"], sv["lb_f"], dot, sv["hsf"], sv["atf"], None, False,
                                             "hgrn_fwd_bwd")
        dqt, dzb, dvt, dlbb = _hgrn_scan_bwd(sv["projt"], sv["lb_b"], dot, sv["hsb"], sv["atb"], (dqf, dvf), True,
                                             "hgrn_rev_bwd")
        dprojt = jnp.concatenate([dqt, dzf, dzb, dvt, dgt], axis=0).astype(BF16)
        dnat = jnp.concatenate([dpa, *dqkv], axis=1)
        dh1_a = _matmul(dnat, w["nat"], "nt", BF16, "d_h1_nat")
        dh1_b = _matmul(dprojt, w["rec_t"], "tn", BF16, "d_h1_rec")
        g_in_nat = _matmul(sv["h1"], dnat, "tn", F32, f"dw_in_nat")
        g_in_rec_t = _matmul(dprojt, sv["h1"], "nn", F32, f"dw_in_rec")
        g_in = jnp.concatenate([g_in_nat, g_in_rec_t.T], axis=1)
        if l > 0:
            below = saved[l - 1]
            dx, dr2, dsh1, dsc1, dnw1, dg2_below = _norm_bwd(sv["xin"], [dh1_a, dh1_b], dxmid, sm["norm1_w"], md[1:2],
                                                            mods[l - 1][5:6], below["r2"], f"norm1_bwd")
        else:
            dx, dsh1, dsc1, dnw1 = _norm_bwd(sv["xin"], [dh1_a, dh1_b], dxmid, sm["norm1_w"], md[1:2], None, None,
                                             f"norm1_bwd")
        grads[l] = dict(w_in=g_in, w_out=g_out, w_up=g_up, w_down=g_down,
                        mod=[dsh1, dsc1, dg1, dsh2, dsc2, dg2], norm1_w=dnw1, conv_a_w=dcw[:CONV_WIDTH], conv_a_b=dcb,
                        ln_a_w=dlnw, ln_a_b=dlnb, lb=jnp.concatenate([dlbf.reshape(1, -1), dlbb.reshape(1, -1)], axis=0),
                        rec_norm_w=dwn.reshape(1, -1), norm2_w=dnw2,
                        conv_f_w=jnp.concatenate([dwg[:3], dwv[:3]], axis=1))
        order_after = on_layer_grads(l, {0: g_in, 1: g_out}, True)
        if l > 0:
            dg2 = dg2_below
    return loss[0, 0], dx, grads, dfw


def _adamw_math(w, g, m, v):
    m = ADAM_B1 * m + (1.0 - ADAM_B1) * g
    v = ADAM_B2 * v + (1.0 - ADAM_B2) * (g * g)
    m_hat = m / (1.0 - ADAM_B1 ** ADAM_STEP)
    v_hat = v / (1.0 - ADAM_B2 ** ADAM_STEP)
    delta = -ADAM_LR * (m_hat / (jnp.sqrt(v_hat) + ADAM_EPS) + ADAM_WD * w)
    return delta, m, v


def _row_tile(rows, cols, max_elems=384 * 1024):
    best = None
    for t in range(8, rows + 1, 8):
        if rows % t == 0 and t * cols <= max_elems:
            best = t
    return best or rows


def _adamw(w, g, m, v, name):
    nl, r, c = w.shape
    tr = _row_tile(r, c)

    def body(w_ref, g_ref, m_ref, v_ref, d_ref, m2_ref, v2_ref):
        d_ref[...], m2_ref[...], v2_ref[...] = _adamw_math(w_ref[...], g_ref[...], m_ref[...], v_ref[...])

    blk = pl.BlockSpec((None, tr, c), lambda l, i: (l, i, 0))
    shape = jax.ShapeDtypeStruct((nl, r, c), F32)
    return pl.pallas_call(body, name=name, grid=(nl, r // tr), in_specs=[blk] * 4, out_specs=(blk, blk, blk),
                          out_shape=(shape, shape, shape), compiler_params=_params(2))(w, g, m, v)


ADA_SHARD = N_MOD * D_MODEL // 4
ADA_COLS = 512
ADA_ROWS = 256
HIGHEST = lax.Precision.HIGHEST


def _ada_mod(c_all, w_ada, b_sh, name):
    def body(c_ref, w_ref, b_ref, o_ref):
        cv = c_ref[...]
        o_ref[...] = jnp.dot(cv * _sigmoid(cv), w_ref[...], precision=HIGHEST, preferred_element_type=F32) + b_ref[...]

    return pl.pallas_call(
        body, name=name, grid=(DEPTH, ADA_SHARD // ADA_COLS),
        in_specs=[pl.BlockSpec((8, D_MODEL), lambda l, j: (0, 0)),
                  pl.BlockSpec((None, D_MODEL, ADA_COLS), lambda l, j: (l, 0, j)),
                  pl.BlockSpec((None, 1, ADA_COLS), lambda l, j: (l, 0, j))],
        out_specs=pl.BlockSpec((None, 8, ADA_COLS), lambda l, j: (l, 0, j)),
        out_shape=jax.ShapeDtypeStruct((DEPTH, 8, ADA_SHARD), F32), compiler_params=_params(2))(c_all, w_ada, b_sh)


def _ada_update(c_all, dmod_sh, w, m, v, name):
    def body(c_ref, d_ref, w_ref, m_ref, v_ref, g_ref, dl_ref, m2_ref, v2_ref):
        cv = c_ref[...]
        g = lax.dot_general(cv * _sigmoid(cv), d_ref[...], (((0,), (0,)), ((), ())), precision=HIGHEST,
                            preferred_element_type=F32)
        g_ref[...] = g
        dl_ref[...], m2_ref[...], v2_ref[...] = _adamw_math(w_ref[...], g, m_ref[...], v_ref[...])

    blk = pl.BlockSpec((None, ADA_ROWS, ADA_SHARD), lambda l, i: (l, i, 0))
    shape = jax.ShapeDtypeStruct((DEPTH, D_MODEL, ADA_SHARD), F32)
    return pl.pallas_call(
        body, name=name, grid=(DEPTH, D_MODEL // ADA_ROWS),
        in_specs=[pl.BlockSpec((8, ADA_ROWS), lambda l, i: (0, i)),
                  pl.BlockSpec((None, 8, ADA_SHARD), lambda l, i: (l, 0, 0)), blk, blk, blk],
        out_specs=(blk,) * 4, out_shape=(shape,) * 4, compiler_params=_params(2))(c_all, dmod_sh, w, m, v)


def _sum_devices(packs, name):
    def body(p_ref, o_ref):
        acc = p_ref[0]
        for dev in range(1, 8):
            acc = acc + p_ref[dev]
        o_ref[...] = acc

    return pl.pallas_call(body, name=name, out_shape=jax.ShapeDtypeStruct(packs.shape[1:], F32))(packs)


def _mesh_pos():
    return lax.axis_index("x"), lax.axis_index("y"), lax.axis_index("c")


def _flip(v, bit):
    return 1 - v if bit else v


def _allgather_devices(x, name):
    m_per, n = x.shape

    def body(x_ref, out_ref, send_sems, recv_sems, local_sem):
        ix, iy, ic = _mesh_pos()
        me, sibling = (ix, iy, ic), (ix, iy, 1 - ic)
        chips = [(1 - ix, iy), (ix, 1 - iy), (1 - ix, 1 - iy)]

        def rows(px, py, pc):
            return out_ref.at[pl.ds((4 * px + 2 * py + pc) * m_per, m_per), :]

        def copy(k, block, to, src=None):
            return pltpu.make_async_remote_copy(
                src_ref=rows(*block) if src is None else src, dst_ref=rows(*block),
                send_sem=send_sems.at[k], recv_sem=recv_sems.at[k], device_id=to, device_id_type=MESH)

        mine = pltpu.make_async_copy(x_ref, rows(*me), local_sem)
        mine.start()
        first = [copy(0, me, sibling, src=x_ref)]
        first += [copy(1 + j, me, (*chip, ic), src=x_ref) for j, chip in enumerate(chips)]
        for cp in first:
            cp.start()
        passed = [copy(4 + j, (*chip, ic), sibling) for j, chip in enumerate(chips)]
        for j, chip in enumerate(chips):
            copy(1 + j, (*chip, ic), me).wait_recv()
            passed[j].start()
        copy(0, sibling, me).wait_recv()
        for j, chip in enumerate(chips):
            copy(4 + j, (*chip, 1 - ic), me).wait_recv()
        for cp in first + passed:
            cp.wait_send()
        mine.wait()

    return pl.pallas_call(
        body, name=name, out_shape=jax.ShapeDtypeStruct((8 * m_per, n), x.dtype),
        in_specs=[pl.BlockSpec(memory_space=pltpu.VMEM)], out_specs=pl.BlockSpec(memory_space=pltpu.VMEM),
        scratch_shapes=[pltpu.SemaphoreType.DMA((7,)), pltpu.SemaphoreType.DMA((7,)), pltpu.SemaphoreType.DMA],
    )(x)


def _gather_chips(shards, name):
    n = len(shards)

    def body(*refs):
        ins, outs = refs[:n], refs[n:2 * n]
        send_sems, recv_sems, local_sems = refs[2 * n:]
        ix, iy, ic = _mesh_pos()
        me = 2 * ix + iy
        local = [pltpu.make_async_copy(ins[a], outs[a].at[me], local_sems.at[a]) for a in range(n)]
        for cp in local:
            cp.start()
        remote = []
        for a in range(n):
            for k in (1, 2, 3):
                px, py = _flip(ix, k & 2), _flip(iy, k & 1)
                sems = dict(send_sem=send_sems.at[3 * a + k - 1], recv_sem=recv_sems.at[3 * a + k - 1],
                            device_id=(px, py, ic), device_id_type=MESH)
                out_cp = pltpu.make_async_remote_copy(src_ref=ins[a], dst_ref=outs[a].at[me], **sems)
                in_cp = pltpu.make_async_remote_copy(src_ref=ins[a], dst_ref=outs[a].at[2 * px + py], **sems)
                out_cp.start()
                remote.append((out_cp, in_cp))
        for out_cp, in_cp in remote:
            out_cp.wait_send()
            in_cp.wait_recv()
        for cp in local:
            cp.wait()

    return pl.pallas_call(
        body, name=name, in_specs=[ANY] * n, out_specs=tuple([ANY] * n),
        out_shape=tuple(jax.ShapeDtypeStruct((4,) + t.shape, t.dtype) for t in shards),
        scratch_shapes=[pltpu.SemaphoreType.DMA((3 * n,)), pltpu.SemaphoreType.DMA((3 * n,)),
                        pltpu.SemaphoreType.DMA((n,))],
    )(*shards)


HBM = pl.BlockSpec(memory_space=pltpu.HBM)
SEM = pl.BlockSpec(memory_space=pltpu.SEMAPHORE)
DATAFLOW = pltpu.SideEffectType.DATAFLOW_SIDE_EFFECTING


def _peer_chip(ix, iy, k):
    return _flip(ix, k & 2), _flip(iy, k & 1)


def _gather_chips_start(shards, name):
    n = len(shards)

    def body(*refs):
        src, land = refs[:n], refs[n:2 * n]
        send_sems, recv_sems = refs[2 * n], refs[2 * n + 1]
        token = refs[-1]
        ix, iy, ic = _mesh_pos()
        me = 2 * ix + iy
        for a in range(n):
            for k in (1, 2, 3):
                px, py = _peer_chip(ix, iy, k)
                pltpu.make_async_remote_copy(
                    src_ref=src[a], dst_ref=land[a].at[me], send_sem=send_sems.at[3 * a + k - 1],
                    recv_sem=recv_sems.at[3 * a + k - 1], device_id=(px, py, ic), device_id_type=MESH).start()
        token[...] = jnp.zeros_like(token)

    hbm = lambda shape, dtype: pltpu.HBM(shape, dtype)
    operands = ([pltpu.with_memory_space_constraint(t, pltpu.HBM) for t in shards]
                + [pltpu.with_memory_space_constraint(lax.empty((4,) + t.shape, t.dtype), pltpu.HBM) for t in shards])
    return pl.pallas_call(
        body, name=name,
        out_shape=(pltpu.SemaphoreType.DMA((3 * n,)), pltpu.SemaphoreType.DMA((3 * n,)),
                   *[hbm(t.shape, t.dtype) for t in shards], *[hbm((4,) + t.shape, t.dtype) for t in shards],
                   jax.ShapeDtypeStruct((8, LANES), F32)),
        in_specs=(HBM,) * (2 * n),
        out_specs=(SEM, SEM) + (HBM,) * (2 * n) + (pl.BlockSpec(memory_space=pltpu.VMEM),),
        input_output_aliases={a: 2 + a for a in range(2 * n)},
        compiler_params=pltpu.CompilerParams(has_side_effects=DATAFLOW),
    )(*operands)


def _gather_chips_wait(started, which, after, name):
    send_sems, recv_sems = started[0], started[1]
    thru = started[2:-1]
    n = len(thru) // 2

    def body(*refs):
        src, land = refs[:n], refs[n:2 * n]
        send_sems, recv_sems = refs[2 * n], refs[2 * n + 1]
        ix, iy, ic = _mesh_pos()
        for a in which:
            for k in (1, 2, 3):
                px, py = _peer_chip(ix, iy, k)
                cp = pltpu.make_async_remote_copy(
                    src_ref=src[a], dst_ref=land[a].at[2 * px + py], send_sem=send_sems.at[3 * a + k - 1],
                    recv_sem=recv_sems.at[3 * a + k - 1], device_id=(px, py, ic), device_id_type=MESH)
                cp.wait_send()
                cp.wait_recv()

    outs = pl.pallas_call(
        body, name=name,
        out_shape=tuple(pltpu.HBM(t.shape, t.dtype) for t in thru),
        in_specs=(HBM,) * (2 * n) + (SEM, SEM, ANY), out_specs=(HBM,) * (2 * n),
        input_output_aliases={a: a for a in range(2 * n)},
        compiler_params=pltpu.CompilerParams(has_side_effects=DATAFLOW),
    )(*thru, send_sems, recv_sems, after)
    return outs[:n], outs[n:]


BIG_KINDS = (("w_in", "col", D_MODEL, IN_COLS), ("w_out", "row", D_MODEL, D_MODEL),
             ("w_up", "col", D_MODEL, 2 * D_FF), ("w_down", "row", D_FF, D_MODEL))


def _piece_shape(how, r, c):
    return (r // 2, c // 4) if how == "col" else (r // 8, c)


def _aligned(start, multiple):
    return start if isinstance(start, int) else pl.multiple_of(start, multiple)


def _piece(ref, how, r, c, chip, half):
    if how == "col":
        return ref.at[pl.ds(_aligned(half * (r // 2), 8), r // 2), pl.ds(_aligned(chip * (c // 4), LANES), c // 4)]
    n = r // 4
    return ref.at[pl.ds(_aligned(chip * n + half * (n // 2), 8), n // 2), :]


def _rs_pair_exchange(grads, kinds, name):
    nk = len(kinds)
    specs = [BIG_KINDS[ki] for ki in kinds]
    nl = len(grads[0])
    flat = [grads[ki][l] for ki in range(nk) for l in range(nl)]
    per = nl * 4

    def body(*refs):
        g, land = refs[:nk * nl], refs[nk * nl:nk * nl + nk]
        send_sems, recv_sems = refs[nk * nl + nk:]
        ix, iy, ic = _mesh_pos()
        sibling = (ix, iy, 1 - ic)
        copies = []
        for ki, (_, how, r, c) in enumerate(specs):
            for l in range(nl):
                for j in range(4):
                    sem = ki * per + l * 4 + j
                    rem = pltpu.make_async_remote_copy(
                        src_ref=_piece(g[ki * nl + l], how, r, c, j, 1 - ic), dst_ref=land[ki].at[l, j],
                        send_sem=send_sems.at[sem], recv_sem=recv_sems.at[sem], device_id=sibling, device_id_type=MESH)
                    rem.start()
                    copies.append(rem)
        for rem in copies:
            rem.wait_send()
            rem.wait_recv()

    shapes = [jax.ShapeDtypeStruct((nl, 4) + _piece_shape(how, r, c), F32) for _, how, r, c in specs]
    return pl.pallas_call(
        body, name=name, in_specs=[ANY] * len(flat), out_specs=tuple([ANY] * nk), out_shape=tuple(shapes),
        scratch_shapes=[pltpu.SemaphoreType.DMA((nk * per,))] * 2,
    )(*flat)


def _pair_sum(g, theirs, layer, how, core, name):
    r, c = g.shape
    pr, pc = _piece_shape(how, r, c)
    if how == "col":
        mine_spec = pl.BlockSpec((pr, pc), lambda j, core_ref: (core_ref[0], j))
    else:
        mine_spec = pl.BlockSpec((pr, pc), lambda j, core_ref: (2 * j + core_ref[0], 0))

    def body(core_ref, g_ref, t_ref, o_ref, ob_ref):
        total = g_ref[...] + t_ref[...]
        o_ref[...] = total
        ob_ref[...] = total.astype(BF16)

    out_blk = pl.BlockSpec((None, pr, pc), lambda j, core_ref: (j, 0, 0))
    return pl.pallas_call(
        body, name=name,
        grid_spec=pltpu.PrefetchScalarGridSpec(
            num_scalar_prefetch=1, grid=(4,),
            in_specs=[mine_spec, pl.BlockSpec((None, None, pr, pc), lambda j, core_ref: (layer, j, 0, 0))],
            out_specs=(out_blk, out_blk)),
        out_shape=(jax.ShapeDtypeStruct((4, pr, pc), F32), jax.ShapeDtypeStruct((4, pr, pc), BF16)),
        compiler_params=_params(1))(core, g, theirs)


def _rs_chip_exchange(pair_sums, name):
    nk = len(pair_sums)
    nl = len(pair_sums[0])
    flat = [pair_sums[ki][l] for ki in range(nk) for l in range(nl)]

    def body(*refs):
        src, dst = refs[:nk * nl], refs[nk * nl:nk * nl + nk]
        send_sems, recv_sems = refs[nk * nl + nk:]
        ix, iy, ic = _mesh_pos()
        copies = []
        for ki in range(nk):
            for l in range(nl):
                for k in (1, 2, 3):
                    px, py = _peer_chip(ix, iy, k)
                    sem = (ki * nl + l) * 3 + k - 1
                    rem = pltpu.make_async_remote_copy(
                        src_ref=src[ki * nl + l].at[2 * px + py], dst_ref=dst[ki].at[l, k - 1],
                        send_sem=send_sems.at[sem], recv_sem=recv_sems.at[sem], device_id=(px, py, ic), device_id_type=MESH)
                    rem.start()
                    copies.append(rem)
        for rem in copies:
            rem.wait_send()
            rem.wait_recv()

    return pl.pallas_call(
        body, name=name, in_specs=[ANY] * len(flat), out_specs=tuple([ANY] * nk),
        out_shape=tuple(jax.ShapeDtypeStruct((nl, 3) + pair_sums[ki][0].shape[1:], pair_sums[ki][0].dtype)
                        for ki in range(nk)),
        scratch_shapes=[pltpu.SemaphoreType.DMA((nk * nl * 3,))] * 2,
    )(*flat)


def _rs_chip_start(pieces, name):
    n = len(pieces)

    def body(*refs):
        src, land = refs[:n], refs[n:2 * n]
        send_sems, recv_sems = refs[2 * n], refs[2 * n + 1]
        token = refs[-1]
        ix, iy, ic = _mesh_pos()
        for a in range(n):
            for k in (1, 2, 3):
                px, py = _peer_chip(ix, iy, k)
                pltpu.make_async_remote_copy(
                    src_ref=src[a].at[2 * px + py], dst_ref=land[a].at[k - 1], send_sem=send_sems.at[3 * a + k - 1],
                    recv_sem=recv_sems.at[3 * a + k - 1], device_id=(px, py, ic), device_id_type=MESH).start()
        token[...] = jnp.zeros_like(token)

    land_shape = lambda t: (3,) + t.shape[1:]
    operands = ([pltpu.with_memory_space_constraint(t, pltpu.HBM) for t in pieces]
                + [pltpu.with_memory_space_constraint(lax.empty(land_shape(t), t.dtype), pltpu.HBM) for t in pieces])
    return pl.pallas_call(
        body, name=name,
        out_shape=(pltpu.SemaphoreType.DMA((3 * n,)), pltpu.SemaphoreType.DMA((3 * n,)),
                   *[pltpu.HBM(t.shape, t.dtype) for t in pieces], *[pltpu.HBM(land_shape(t), t.dtype) for t in pieces],
                   jax.ShapeDtypeStruct((8, LANES), F32)),
        in_specs=(HBM,) * (2 * n),
        out_specs=(SEM, SEM) + (HBM,) * (2 * n) + (pl.BlockSpec(memory_space=pltpu.VMEM),),
        input_output_aliases={a: 2 + a for a in range(2 * n)},
        compiler_params=pltpu.CompilerParams(has_side_effects=DATAFLOW),
    )(*operands)


def _rs_chip_wait(started, after, name):
    send_sems, recv_sems = started[0], started[1]
    thru = started[2:-1]
    n = len(thru) // 2

    def body(*refs):
        src, land = refs[:n], refs[n:2 * n]
        send_sems, recv_sems = refs[2 * n], refs[2 * n + 1]
        ix, iy, ic = _mesh_pos()
        for a in range(n):
            for k in (1, 2, 3):
                px, py = _peer_chip(ix, iy, k)
                cp = pltpu.make_async_remote_copy(
                    src_ref=src[a].at[2 * px + py], dst_ref=land[a].at[k - 1], send_sem=send_sems.at[3 * a + k - 1],
                    recv_sem=recv_sems.at[3 * a + k - 1], device_id=(px, py, ic), device_id_type=MESH)
                cp.wait_send()
                cp.wait_recv()

    outs = pl.pallas_call(
        body, name=name,
        out_shape=tuple(pltpu.HBM(t.shape, t.dtype) for t in thru),
        in_specs=(HBM,) * (2 * n) + (SEM, SEM, ANY), out_specs=(HBM,) * (2 * n),
        input_output_aliases={a: a for a in range(2 * n)},
        compiler_params=pltpu.CompilerParams(has_side_effects=DATAFLOW),
    )(*thru, send_sems, recv_sems, after)
    return outs[n:]


def _chip_sum(own, others, chip, name):
    _, pr, pc = own.shape

    def body(chip_ref, own_ref, s1, s2, s3, o_ref):
        o_ref[...] = ((own_ref[...] + s1[...].astype(F32)) + s2[...].astype(F32)) + s3[...].astype(F32)

    slot = lambda k: pl.BlockSpec((None, pr, pc), lambda i, chip_ref: (k, 0, 0))
    return pl.pallas_call(
        body, name=name,
        grid_spec=pltpu.PrefetchScalarGridSpec(
            num_scalar_prefetch=1, grid=(1,),
            in_specs=[pl.BlockSpec((None, pr, pc), lambda i, chip_ref: (chip_ref[0], 0, 0)), slot(0), slot(1), slot(2)],
            out_specs=pl.BlockSpec((pr, pc), lambda i, chip_ref: (0, 0))),
        out_shape=jax.ShapeDtypeStruct((pr, pc), F32), compiler_params=_params(1))(chip, own, others, others, others)


def _rs_pair_share(halves, name):
    nk = len(halves)
    flat = [halves[ki][l] for ki in range(nk) for l in range(DEPTH)]

    def body(*refs):
        src, dst = refs[:nk * DEPTH], refs[nk * DEPTH:nk * DEPTH + nk]
        send_sems, recv_sems = refs[nk * DEPTH + nk:]
        ix, iy, ic = _mesh_pos()
        copies = []
        for ki in range(nk):
            for l in range(DEPTH):
                sem = ki * DEPTH + l
                rem = pltpu.make_async_remote_copy(
                    src_ref=src[sem], dst_ref=dst[ki].at[l], send_sem=send_sems.at[sem], recv_sem=recv_sems.at[sem],
                    device_id=(ix, iy, 1 - ic), device_id_type=MESH)
                rem.start()
                copies.append(rem)
        for rem in copies:
            rem.wait_send()
            rem.wait_recv()

    return pl.pallas_call(
        body, name=name, in_specs=[ANY] * len(flat), out_specs=tuple([ANY] * nk),
        out_shape=tuple(jax.ShapeDtypeStruct((DEPTH,) + halves[ki][0].shape, F32) for ki in range(nk)),
        scratch_shapes=[pltpu.SemaphoreType.DMA((nk * DEPTH,))] * 2,
    )(*flat)


def _adamw_halves(w, mine, theirs, m, v, core, name):
    nl, pr, pc = theirs.shape
    shape = w.shape
    view = lambda t: t.reshape(nl, 2, pr, pc)
    tr = _row_tile(pr, pc, 256 * 1024)

    def body(core_ref, w_ref, a0_ref, a1_ref, t_ref, m_ref, v_ref, g_ref, d_ref, m2_ref, v2_ref):
        own = jnp.where(pl.program_id(0) == 0, a0_ref[...], a1_ref[...])
        g = jnp.where(pl.program_id(1) == core_ref[0], own, t_ref[...])
        g_ref[...] = g
        d_ref[...], m2_ref[...], v2_ref[...] = _adamw_math(w_ref[...], g, m_ref[...], v_ref[...])

    blk = pl.BlockSpec((None, None, tr, pc), lambda l, h, i, core_ref: (l, h, i, 0))
    own_blk = pl.BlockSpec((tr, pc), lambda l, h, i, core_ref: (i, 0))
    out = jax.ShapeDtypeStruct((nl, 2, pr, pc), F32)
    outs = pl.pallas_call(
        body, name=name,
        grid_spec=pltpu.PrefetchScalarGridSpec(
            num_scalar_prefetch=1, grid=(nl, 2, pr // tr),
            in_specs=[blk, own_blk, own_blk, pl.BlockSpec((None, tr, pc), lambda l, h, i, core_ref: (l, i, 0)), blk, blk],
            out_specs=(blk,) * 4),
        out_shape=(out,) * 4, compiler_params=_params(3),
    )(core, view(w), mine[0], mine[1], theirs, view(m), view(v))
    return tuple(t.reshape(shape) for t in outs)


def _rs_first_stages(layer_grads, kinds, core, tag, in_flight):
    theirs = _rs_pair_exchange([[g] for g in layer_grads], kinds, f"rs_pair_exchange_{tag}")
    sums = [_pair_sum(g, theirs[n], 0, BIG_KINDS[ki][1], core, f"rs_pair_sum_{BIG_KINDS[ki][0]}")
            for n, (ki, g) in enumerate(zip(kinds, layer_grads))]
    to_send = [both[1] for both in sums]
    if in_flight:
        return [both[0] for both in sums], _rs_chip_start(to_send, f"rs_chip_start_{tag}")
    slots = _rs_chip_exchange([[t] for t in to_send], f"rs_chip_exchange_{tag}")
    return [both[0] for both in sums], [t[0] for t in slots]


def _rs_last_stages(pair_sums, slots, chip):
    halves = [[_chip_sum(pair_sums[ki][l], slots[ki][l], chip, f"rs_chip_sum_{kind}") for l in range(DEPTH)]
              for ki, (kind, _, _, _) in enumerate(BIG_KINDS)]
    other = _rs_pair_share(halves, "rs_pair_share")
    return list(zip(halves, other))


WEIGHT_NAMES = ("w_ada", "b_ada", "norm1_w", "w_in", "conv_a_w", "conv_a_b", "ln_a_w", "ln_a_b", "lb_gamma",
                "rec_norm_w", "w_out", "norm2_w", "w_up", "conv_f_w", "w_down", "final_norm_w")
SMALL_PARAMS = (("b_ada", (DEPTH, N_MOD * D_MODEL), None), ("norm1_w", (DEPTH, D_MODEL), None),
                ("conv_a_w", (DEPTH, CONV_WIDTH, CONV_CH), 2), ("conv_a_b", (DEPTH, CONV_CH), None),
                ("ln_a_w", (DEPTH, CONV_CH), None), ("ln_a_b", (DEPTH, CONV_CH), None),
                ("lb_gamma", (DEPTH, 2, REC_WIDTH), 2), ("rec_norm_w", (DEPTH, REC_WIDTH), None),
                ("norm2_w", (DEPTH, D_MODEL), None), ("conv_f_w", (DEPTH, 3, 2 * D_FF), 2),
                ("final_norm_w", (D_MODEL,), None))


def _pack_rows(parts):
    flat = jnp.concatenate([p.reshape(-1) for p in parts])
    total = flat.shape[0]
    padded = -(-total // (8 * LANES)) * (8 * LANES)
    return jnp.pad(flat, (0, padded - total)).reshape(padded // LANES, LANES)


def _unpack(flat, shapes):
    out, off = [], 0
    for shp in shapes:
        size = int(np.prod(shp))
        out.append(flat[off:off + size].reshape(shp))
        off += size
    return out


def _unstack_chips(t, axis):
    return jnp.concatenate([t[j] for j in range(4)], axis=axis)


def kernel(x, c, w_ada, b_ada, norm1_w, w_in, conv_a_w, conv_a_b, ln_a_w, ln_a_b, lb_gamma, rec_norm_w, w_out, norm2_w, w_up, conv_f_w, w_down, final_norm_w, loss_target, m_w_ada, m_b_ada, m_norm1_w, m_w_in, m_conv_a_w, m_conv_a_b, m_ln_a_w, m_ln_a_b, m_lb_gamma, m_rec_norm_w, m_w_out, m_norm2_w, m_w_up, m_conv_f_w, m_w_down, m_final_norm_w, v_w_ada, v_b_ada, v_norm1_w, v_w_in, v_conv_a_w, v_conv_a_b, v_ln_a_w, v_ln_a_b, v_lb_gamma, v_rec_norm_w, v_w_out, v_norm2_w, v_w_up, v_conv_f_w, v_w_down, v_final_norm_w):
    params = dict(zip(WEIGHT_NAMES, (w_ada, b_ada, norm1_w, w_in, conv_a_w, conv_a_b, ln_a_w, ln_a_b, lb_gamma,
                                     rec_norm_w, w_out, norm2_w, w_up, conv_f_w, w_down, final_norm_w)))
    mom1 = dict(zip(WEIGHT_NAMES, (m_w_ada, m_b_ada, m_norm1_w, m_w_in, m_conv_a_w, m_conv_a_b, m_ln_a_w, m_ln_a_b,
                                   m_lb_gamma, m_rec_norm_w, m_w_out, m_norm2_w, m_w_up, m_conv_f_w, m_w_down,
                                   m_final_norm_w)))
    mom2 = dict(zip(WEIGHT_NAMES, (v_w_ada, v_b_ada, v_norm1_w, v_w_in, v_conv_a_w, v_conv_a_b, v_ln_a_w, v_ln_a_b,
                                   v_lb_gamma, v_rec_norm_w, v_w_out, v_norm2_w, v_w_up, v_conv_f_w, v_w_down,
                                   v_final_norm_w)))
    ix, iy, ic = _mesh_pos()
    chip = 2 * ix + iy
    dev = 2 * chip + ic

    c_all = _allgather_devices(c.reshape(8, LANES), "gather_cond").reshape(8, D_MODEL)
    b_sh = lax.dynamic_slice_in_dim(b_ada, chip * ADA_SHARD, ADA_SHARD, axis=1)
    mod_sh = _ada_mod(c_all, w_ada, b_sh.reshape(DEPTH, 1, ADA_SHARD), "ada_mod")
    w_in_b, w_out_b, w_up_b, w_down_b = (t.astype(BF16) for t in (w_in, w_out, w_up, w_down))
    first = _gather_chips([mod_sh, conv_a_w, conv_f_w, lb_gamma], "gather_first")
    later = [w_in_b[0], w_out_b, w_up_b, w_down_b, w_in_b[1]]
    started = _gather_chips_start(later, "gather_weights_start")
    mod_mine = lax.dynamic_index_in_dim(first[0], dev, axis=2, keepdims=False) + started[-1][0, 0]
    mods = [jnp.concatenate([mod_mine[j, l] for j in range(4)]).reshape(N_MOD, D_MODEL) for l in range(DEPTH)]
    conv_a_w_f, conv_f_w_f, gamma_f = (_unstack_chips(first[k], 2) for k in (1, 2, 3))

    flying = [started]

    def later_weights(stage, after):
        which = ([0], [1], [2, 3, 4])[stage]
        own, lands = _gather_chips_wait(flying[0], which, after, f"gather_weights_wait_{stage}")
        flying[0] = (started[0], started[1], *own, *lands, None)
        whole = lambda n, axis: jnp.concatenate([jnp.where(chip == j, own[n], lands[n][j]) for j in range(4)], axis=axis)
        if stage == 0:
            return whole(0, 1)
        if stage == 1:
            return whole(1, 1)
        return whole(4, 1), whole(2, 2), whole(3, 1)

    lb1, p_soft = _lower_bounds(gamma_f.reshape(DEPTH, 2 * REC_WIDTH), "lower_bounds")
    lbs = [jnp.zeros((2, REC_WIDTH), F32), lb1.reshape(2, REC_WIDTH)]
    small = []
    for l in range(DEPTH):
        small.append(dict(norm1_w=norm1_w[l][None], conv_a_w=conv_a_w_f[l], conv_a_b=conv_a_b[l][None],
                          ln_a_w=ln_a_w[l][None], ln_a_b=ln_a_b[l][None], rec_norm_w=rec_norm_w[l],
                          norm2_w=norm2_w[l][None], conv_f_w=conv_f_w_f[l]))

    core_id, chip_id = ic.astype(jnp.int32).reshape(1), chip.astype(jnp.int32).reshape(1)
    pending, groups = {}, []

    def on_layer_grads(l, by_kind, last):
        if l > 0:
            pending.update(by_kind)
            if not last:
                return None
            by_kind = dict(pending)
        kinds = sorted(by_kind)
        in_flight = not (l == 0 and last)
        tag = f"l{l}" if l > 0 else f"l{l}_{'mix' if last else 'ffn'}"
        sums, exchange = _rs_first_stages([by_kind[k] for k in kinds], kinds, core_id, tag, in_flight)
        groups.append((l, kinds, sums, exchange, in_flight, tag))
        return exchange[-1][0:1, 0:1] if in_flight else None

    loss, dx, grads, dfw = _sequence_step(x[0], loss_target[0], mods, lbs, small, later_weights,
                                          final_norm_w[None], on_layer_grads)
    loss = lax.psum(loss, ("x", "y", "c"))
    pair_sums = [[None] * DEPTH for _ in BIG_KINDS]
    slots = [[None] * DEPTH for _ in BIG_KINDS]
    for l, kinds, sums, exchange, in_flight, tag in groups:
        received = _rs_chip_wait(exchange, dx, f"rs_chip_wait_{tag}") if in_flight else exchange
        for n, ki in enumerate(kinds):
            pair_sums[ki][l], slots[ki][l] = sums[n], received[n]

    dgamma = _lower_bounds_bwd(grads[1]["lb"].reshape(1, 2 * REC_WIDTH), p_soft, "lower_bounds_bwd")
    dmod = [jnp.concatenate(grads[l]["mod"], axis=1) for l in range(DEPTH)]
    stack = lambda key: jnp.stack([grads[l][key] for l in range(DEPTH)])
    local_small = dict(b_ada=jnp.concatenate(dmod, axis=0), norm1_w=stack("norm1_w"), conv_a_w=stack("conv_a_w"),
                       conv_a_b=stack("conv_a_b"), ln_a_w=stack("ln_a_w"), ln_a_b=stack("ln_a_b"), lb_gamma=dgamma,
                       rec_norm_w=stack("rec_norm_w"), norm2_w=stack("norm2_w"), conv_f_w=stack("conv_f_w"),
                       final_norm_w=dfw)
    pack = _pack_rows([local_small[name] for name, _, _ in SMALL_PARAMS])
    rows = pack.shape[0]
    packs = _allgather_devices(pack, "gather_small_grads").reshape(8, rows, LANES)
    summed = _sum_devices(packs, "sum_small_grads").reshape(-1)
    small_grads = dict(zip([n for n, _, _ in SMALL_PARAMS], _unpack(summed, [shp for _, shp, _ in SMALL_PARAMS])))

    dmod_all = packs.reshape(8, rows * LANES)[:, :DEPTH * N_MOD * D_MODEL].reshape(8, DEPTH, N_MOD * D_MODEL)
    dmod_sh = lax.dynamic_slice_in_dim(dmod_all, chip * ADA_SHARD, ADA_SHARD, axis=2).transpose(1, 0, 2)
    g_ada, d_ada, m_ada, v_ada = _ada_update(c_all, dmod_sh, w_ada, m_w_ada, v_w_ada, "ada_update")

    for name, shp, axis in SMALL_PARAMS:
        if axis is not None:
            width = shp[axis] // 4
            small_grads[name] = lax.dynamic_slice_in_dim(small_grads[name], chip * width, width, axis=axis)
    names = [n for n, _, _ in SMALL_PARAMS]
    packed = [_pack_rows([src[n] for n in names])[None] for src in (params, small_grads, mom1, mom2)]
    small_out = _adamw(*packed, "adamw_small")
    shapes = [params[n].shape for n in names]
    small_delta, small_m, small_v = (dict(zip(names, _unpack(t.reshape(-1), shapes))) for t in small_out)

    summed_big = _rs_last_stages(pair_sums, slots, chip_id)
    grad, delta, new_m, new_v = dict(small_grads), small_delta, small_m, small_v
    grad["w_ada"], delta["w_ada"], new_m["w_ada"], new_v["w_ada"] = g_ada, d_ada, m_ada, v_ada
    for (name, _, _, _), (mine, theirs) in zip(BIG_KINDS, summed_big):
        grad[name], delta[name], new_m[name], new_v[name] = _adamw_halves(
            params[name], mine, theirs, mom1[name], mom2[name], core_id, f"adamw_{name}")

    return (loss, dx[None], *[grad[n] for n in WEIGHT_NAMES], *[delta[n] for n in WEIGHT_NAMES],
            *[new_m[n] for n in WEIGHT_NAMES], *[new_v[n] for n in WEIGHT_NAMES])
```
